```python
import math
import jax, jax.numpy as jnp
from jax import lax
import numpy as np

D_MODEL = 1024
BATCH = 8
SEQ = 8192
DEPTH = 1

N_Q_HEADS = 8
N_KV_HEADS = 2
GROUP = N_Q_HEADS // N_KV_HEADS
HEAD_DIM = 64
WINDOW = 128
BLOCK = 128
Q_W = N_Q_HEADS * HEAD_DIM
KV_W = N_KV_HEADS * HEAD_DIM
N_BUCKETS = 32
MAX_DISTANCE = 128
GMLP_GROUPS = 8
GMLP_GROUP_DIM = 64
GMLP_WIDTH = GMLP_GROUPS * GMLP_GROUP_DIM
CHUNK = 128
N_BRANCHES = 2
IN_W = Q_W + 2 * KV_W + 2 * GMLP_WIDTH + N_BRANCHES * D_MODEL
D_FF = 2816
FFN_RES = 0.5
N_SUB = 3
N_ADA = 3 * N_SUB
EPS = 1e-6
NEG = -1e30

kernel_name = "hybrid_swa_sink_gmlp_macaron_block"


def rmsnorm(x, g):
    xf = x.astype(jnp.float32)
    r = lax.rsqrt(jnp.mean(xf * xf, axis=-1, keepdims=True) + EPS)
    return (xf * r * g.astype(jnp.float32)).astype(x.dtype)


def layernorm(x, g, b):
    xf = x.astype(jnp.float32)
    mu = jnp.mean(xf, axis=-1, keepdims=True)
    var = jnp.mean(jnp.square(xf - mu), axis=-1, keepdims=True)
    return ((xf - mu) * lax.rsqrt(var + EPS) * g + b).astype(x.dtype)


def modulate(x, shift, scale):
    return x * (1.0 + scale[:, None, :]) + shift[:, None, :]


def swiglu(x, w_in, w_out):
    gu = x @ w_in
    gate, up = jnp.split(gu, 2, axis=-1)
    return (jax.nn.silu(gate) * up) @ w_out


def t5_causal_bucket(dist):
    max_exact = N_BUCKETS // 2
    d_f = jnp.maximum(dist, max_exact).astype(jnp.float32)
    large = max_exact + (jnp.log(d_f / max_exact) / math.log(MAX_DISTANCE / max_exact)
                         * (N_BUCKETS - max_exact)).astype(jnp.int32)
    large = jnp.minimum(large, N_BUCKETS - 1)
    return jnp.where(dist < max_exact, dist, large)


def banded_sink_attention(q, k, v, sinks, rel_table):
    B, S, _ = q.shape
    nb = S // BLOCK
    q = q.reshape(B, nb, BLOCK, N_KV_HEADS, GROUP, HEAD_DIM)
    k = k.reshape(B, nb, BLOCK, N_KV_HEADS, HEAD_DIM)
    v = v.reshape(B, nb, BLOCK, N_KV_HEADS, HEAD_DIM)
    pad = ((0, 0), (1, 0), (0, 0), (0, 0), (0, 0))
    kk = jnp.concatenate([jnp.pad(k, pad)[:, :-1], k], axis=2)
    vv = jnp.concatenate([jnp.pad(v, pad)[:, :-1], v], axis=2)

    qi = jnp.arange(BLOCK, dtype=jnp.int32)[:, None]
    kj = jnp.arange(2 * BLOCK, dtype=jnp.int32)[None, :]
    dist = qi + BLOCK - kj
    in_window = (dist >= 0) & (dist < WINDOW)
    blk = jnp.arange(nb, dtype=jnp.int32)[:, None, None]
    key_pos = (blk - 1) * BLOCK + kj[None]
    mask = in_window[None] & (key_pos >= 0)
    bias = rel_table.astype(jnp.float32)[t5_causal_bucket(jnp.maximum(dist, 0))]
    bias = jnp.transpose(bias, (2, 0, 1)).reshape(N_KV_HEADS, GROUP, BLOCK, 2 * BLOCK)

    s = jnp.einsum('bnqkgd,bnskd->bnkgqs', q, kk,
                   preferred_element_type=jnp.float32) * (HEAD_DIM ** -0.5)
    s = s + bias[None, None]
    s = jnp.where(mask[None, :, None, None], s, NEG)
    sink = sinks.astype(jnp.float32).reshape(1, 1, N_KV_HEADS, GROUP, 1, 1)
    m = jnp.maximum(jnp.max(s, axis=-1, keepdims=True), sink)
    p = jnp.exp(s - m)
    denom = jnp.sum(p, axis=-1, keepdims=True) + jnp.exp(sink - m)
    o = jnp.einsum('bnkgqs,bnskd->bnqkgd', (p / denom).astype(vv.dtype), vv,
                   preferred_element_type=jnp.float32)
    return o.astype(q.dtype).reshape(B, S, Q_W)


def chunked_spatial_gating(z, ln_g, ln_b, w_s, b_s):
    z = jax.nn.gelu(z)
    u, vg = jnp.split(z, 2, axis=-1)
    vg = layernorm(vg, ln_g, ln_b)
    B, S, _ = vg.shape
    nc = S // CHUNK
    vg = vg.reshape(B, nc, CHUNK, GMLP_GROUPS, GMLP_GROUP_DIM)
    causal = jnp.tril(jnp.ones((CHUNK, CHUNK), dtype=w_s.dtype))
    w = w_s * causal[None]
    sp = jnp.einsum('gts,bnsgc->bntgc', w, vg)
    sp = sp + jnp.transpose(b_s)[None, None, :, :, None]
    return u * sp.reshape(B, S, GMLP_WIDTH)


def _fwd_setup_inputs(seed: int = 0) -> dict:
    key = jax.random.key(seed)
    ks = jax.random.split(key, 24)
    f32 = jnp.float32
    nrm = lambda k, shape, scale: jax.random.normal(k, shape, f32) * scale
    L = DEPTH
    return {
        "x": nrm(ks[0], (BATCH, SEQ, D_MODEL), 1.0),
        "c": nrm(ks[1], (BATCH, D_MODEL), 1.0),
        "rel_bias": nrm(ks[2], (N_BUCKETS, N_Q_HEADS), 0.5),
        "w_ada": nrm(ks[3], (L, D_MODEL, N_ADA * D_MODEL), 0.5 * D_MODEL ** -0.5),
        "b_ada": nrm(ks[4], (L, N_ADA * D_MODEL), 0.02),
        "pre_norm_g": 1.0 + nrm(ks[5], (L, N_SUB, D_MODEL), 0.05),
        "post_norm_g": 1.0 + nrm(ks[6], (L, N_SUB, D_MODEL), 0.05),
        "w_ffn1_in": nrm(ks[7], (L, D_MODEL, 2 * D_FF), D_MODEL ** -0.5),
        "w_ffn1_out": nrm(ks[8], (L, D_FF, D_MODEL), D_FF ** -0.5),
        "w_in": nrm(ks[9], (L, D_MODEL, IN_W), D_MODEL ** -0.5),
        "sinks": nrm(ks[10], (L, N_Q_HEADS), 1.0),
        "gmlp_ln_g": 1.0 + nrm(ks[11], (L, GMLP_WIDTH), 0.05),
        "gmlp_ln_b": nrm(ks[12], (L, GMLP_WIDTH), 0.02),
        "gmlp_w_s": nrm(ks[13], (L, GMLP_GROUPS, CHUNK, CHUNK), CHUNK ** -0.5),
        "gmlp_b_s": 1.0 + nrm(ks[14], (L, GMLP_GROUPS, CHUNK), 0.02),
        "w_br_attn": nrm(ks[15], (L, Q_W, D_MODEL), Q_W ** -0.5),
        "w_br_gmlp": nrm(ks[16], (L, GMLP_WIDTH, D_MODEL), GMLP_WIDTH ** -0.5),
        "w_out": nrm(ks[17], (L, D_MODEL, D_MODEL), D_MODEL ** -0.5),
        "w_ffn2_in": nrm(ks[18], (L, D_MODEL, 2 * D_FF), D_MODEL ** -0.5),
        "w_ffn2_out": nrm(ks[19], (L, D_FF, D_MODEL), D_FF ** -0.5),
    }


def _fwd_reference(x, c, rel_bias, w_ada, b_ada, pre_norm_g, post_norm_g, w_ffn1_in, w_ffn1_out,
              w_in, sinks, gmlp_ln_g, gmlp_ln_b, gmlp_w_s, gmlp_b_s, w_br_attn, w_br_gmlp,
              w_out, w_ffn2_in, w_ffn2_out):
    h = x
    splits = np.cumsum([Q_W, KV_W, KV_W, 2 * GMLP_WIDTH]).tolist()
    for l in range(DEPTH):
        ada = jax.nn.silu(c) @ w_ada[l] + b_ada[l]
        sh1, sc1, g1, sh2, sc2, g2, sh3, sc3, g3 = jnp.split(ada, N_ADA, axis=-1)

        n = modulate(rmsnorm(h, pre_norm_g[l, 0]), sh1, sc1)
        y = swiglu(n, w_ffn1_in[l], w_ffn1_out[l])
        h = h + FFN_RES * g1[:, None, :] * rmsnorm(y, post_norm_g[l, 0])

        n = modulate(rmsnorm(h, pre_norm_g[l, 1]), sh2, sc2)
        z = n @ w_in[l]
        q, k, v, zg, zgate = jnp.split(z, splits, axis=-1)
        ya = banded_sink_attention(q, k, v, sinks[l], rel_bias) @ w_br_attn[l]
        yg = chunked_spatial_gating(zg, gmlp_ln_g[l], gmlp_ln_b[l],
                                    gmlp_w_s[l], gmlp_b_s[l]) @ w_br_gmlp[l]
        ga, gg = jnp.split(jax.nn.sigmoid(zgate), N_BRANCHES, axis=-1)
        y = (ga * ya + gg * yg) @ w_out[l]
        h = h + g2[:, None, :] * rmsnorm(y, post_norm_g[l, 1])

        n = modulate(rmsnorm(h, pre_norm_g[l, 2]), sh3, sc3)
        y = swiglu(n, w_ffn2_in[l], w_ffn2_out[l])
        h = h + FFN_RES * g3[:, None, :] * rmsnorm(y, post_norm_g[l, 2])
    return h


import jax as _jax
import jax.numpy as _jnp

TWIN_FORMAT = 'train_step'
FWD_PARAMS = ['x', 'c', 'rel_bias', 'w_ada', 'b_ada', 'pre_norm_g', 'post_norm_g', 'w_ffn1_in', 'w_ffn1_out', 'w_in', 'sinks', 'gmlp_ln_g', 'gmlp_ln_b', 'gmlp_w_s', 'gmlp_b_s', 'w_br_attn', 'w_br_gmlp', 'w_out', 'w_ffn2_in', 'w_ffn2_out']
TWIN_WEIGHTS = ['rel_bias', 'w_ada', 'b_ada', 'pre_norm_g', 'post_norm_g', 'w_ffn1_in', 'w_ffn1_out', 'w_in', 'sinks', 'gmlp_ln_g', 'gmlp_ln_b', 'gmlp_w_s', 'gmlp_b_s', 'w_br_attn', 'w_br_gmlp', 'w_out', 'w_ffn2_in', 'w_ffn2_out']
TWIN_DIFF_INPUT = 'x'
TWIN_INPUTS = ['x', 'c', 'rel_bias', 'w_ada', 'b_ada', 'pre_norm_g', 'post_norm_g', 'w_ffn1_in', 'w_ffn1_out', 'w_in', 'sinks', 'gmlp_ln_g', 'gmlp_ln_b', 'gmlp_w_s', 'gmlp_b_s', 'w_br_attn', 'w_br_gmlp', 'w_out', 'w_ffn2_in', 'w_ffn2_out', 'loss_target', 'm_rel_bias', 'm_w_ada', 'm_b_ada', 'm_pre_norm_g', 'm_post_norm_g', 'm_w_ffn1_in', 'm_w_ffn1_out', 'm_w_in', 'm_sinks', 'm_gmlp_ln_g', 'm_gmlp_ln_b', 'm_gmlp_w_s', 'm_gmlp_b_s', 'm_w_br_attn', 'm_w_br_gmlp', 'm_w_out', 'm_w_ffn2_in', 'm_w_ffn2_out', 'v_rel_bias', 'v_w_ada', 'v_b_ada', 'v_pre_norm_g', 'v_post_norm_g', 'v_w_ffn1_in', 'v_w_ffn1_out', 'v_w_in', 'v_sinks', 'v_gmlp_ln_g', 'v_gmlp_ln_b', 'v_gmlp_w_s', 'v_gmlp_b_s', 'v_w_br_attn', 'v_w_br_gmlp', 'v_w_out', 'v_w_ffn2_in', 'v_w_ffn2_out']
TWIN_OUTPUTS = ['loss', 'grad_x', 'grad_rel_bias', 'grad_w_ada', 'grad_b_ada', 'grad_pre_norm_g', 'grad_post_norm_g', 'grad_w_ffn1_in', 'grad_w_ffn1_out', 'grad_w_in', 'grad_sinks', 'grad_gmlp_ln_g', 'grad_gmlp_ln_b', 'grad_gmlp_w_s', 'grad_gmlp_b_s', 'grad_w_br_attn', 'grad_w_br_gmlp', 'grad_w_out', 'grad_w_ffn2_in', 'grad_w_ffn2_out', 'delta_rel_bias', 'delta_w_ada', 'delta_b_ada', 'delta_pre_norm_g', 'delta_post_norm_g', 'delta_w_ffn1_in', 'delta_w_ffn1_out', 'delta_w_in', 'delta_sinks', 'delta_gmlp_ln_g', 'delta_gmlp_ln_b', 'delta_gmlp_w_s', 'delta_gmlp_b_s', 'delta_w_br_attn', 'delta_w_br_gmlp', 'delta_w_out', 'delta_w_ffn2_in', 'delta_w_ffn2_out', 'new_m_rel_bias', 'new_m_w_ada', 'new_m_b_ada', 'new_m_pre_norm_g', 'new_m_post_norm_g', 'new_m_w_ffn1_in', 'new_m_w_ffn1_out', 'new_m_w_in', 'new_m_sinks', 'new_m_gmlp_ln_g', 'new_m_gmlp_ln_b', 'new_m_gmlp_w_s', 'new_m_gmlp_b_s', 'new_m_w_br_attn', 'new_m_w_br_gmlp', 'new_m_w_out', 'new_m_w_ffn2_in', 'new_m_w_ffn2_out', 'new_v_rel_bias', 'new_v_w_ada', 'new_v_b_ada', 'new_v_pre_norm_g', 'new_v_post_norm_g', 'new_v_w_ffn1_in', 'new_v_w_ffn1_out', 'new_v_w_in', 'new_v_sinks', 'new_v_gmlp_ln_g', 'new_v_gmlp_ln_b', 'new_v_gmlp_w_s', 'new_v_gmlp_b_s', 'new_v_w_br_attn', 'new_v_w_br_gmlp', 'new_v_w_out', 'new_v_w_ffn2_in', 'new_v_w_ffn2_out']
TWIN_LEAF_KINDS = {'loss': 'loss', 'grad_x': 'grad_x', 'grad_rel_bias': 'grad_w', 'grad_w_ada': 'grad_w', 'grad_b_ada': 'grad_w', 'grad_pre_norm_g': 'grad_w', 'grad_post_norm_g': 'grad_w', 'grad_w_ffn1_in': 'grad_w', 'grad_w_ffn1_out': 'grad_w', 'grad_w_in': 'grad_w', 'grad_sinks': 'grad_w', 'grad_gmlp_ln_g': 'grad_w', 'grad_gmlp_ln_b': 'grad_w', 'grad_gmlp_w_s': 'grad_w', 'grad_gmlp_b_s': 'grad_w', 'grad_w_br_attn': 'grad_w', 'grad_w_br_gmlp': 'grad_w', 'grad_w_out': 'grad_w', 'grad_w_ffn2_in': 'grad_w', 'grad_w_ffn2_out': 'grad_w', 'delta_rel_bias': 'delta_w', 'delta_w_ada': 'delta_w', 'delta_b_ada': 'delta_w', 'delta_pre_norm_g': 'delta_w', 'delta_post_norm_g': 'delta_w', 'delta_w_ffn1_in': 'delta_w', 'delta_w_ffn1_out': 'delta_w', 'delta_w_in': 'delta_w', 'delta_sinks': 'delta_w', 'delta_gmlp_ln_g': 'delta_w', 'delta_gmlp_ln_b': 'delta_w', 'delta_gmlp_w_s': 'delta_w', 'delta_gmlp_b_s': 'delta_w', 'delta_w_br_attn': 'delta_w', 'delta_w_br_gmlp': 'delta_w', 'delta_w_out': 'delta_w', 'delta_w_ffn2_in': 'delta_w', 'delta_w_ffn2_out': 'delta_w', 'new_m_rel_bias': 'new_m', 'new_m_w_ada': 'new_m', 'new_m_b_ada': 'new_m', 'new_m_pre_norm_g': 'new_m', 'new_m_post_norm_g': 'new_m', 'new_m_w_ffn1_in': 'new_m', 'new_m_w_ffn1_out': 'new_m', 'new_m_w_in': 'new_m', 'new_m_sinks': 'new_m', 'new_m_gmlp_ln_g': 'new_m', 'new_m_gmlp_ln_b': 'new_m', 'new_m_gmlp_w_s': 'new_m', 'new_m_gmlp_b_s': 'new_m', 'new_m_w_br_attn': 'new_m', 'new_m_w_br_gmlp': 'new_m', 'new_m_w_out': 'new_m', 'new_m_w_ffn2_in': 'new_m', 'new_m_w_ffn2_out': 'new_m', 'new_v_rel_bias': 'new_v', 'new_v_w_ada': 'new_v', 'new_v_b_ada': 'new_v', 'new_v_pre_norm_g': 'new_v', 'new_v_post_norm_g': 'new_v', 'new_v_w_ffn1_in': 'new_v', 'new_v_w_ffn1_out': 'new_v', 'new_v_w_in': 'new_v', 'new_v_sinks': 'new_v', 'new_v_gmlp_ln_g': 'new_v', 'new_v_gmlp_ln_b': 'new_v', 'new_v_gmlp_w_s': 'new_v', 'new_v_gmlp_b_s': 'new_v', 'new_v_w_br_attn': 'new_v', 'new_v_w_br_gmlp': 'new_v', 'new_v_w_out': 'new_v', 'new_v_w_ffn2_in': 'new_v', 'new_v_w_ffn2_out': 'new_v'}


def _forward(args):
    return _fwd_reference(*[args[k] for k in FWD_PARAMS])


def _output_shape():
    def fwd():
        inp = _fwd_setup_inputs(0)
        return _fwd_reference(*[inp[k] for k in FWD_PARAMS])
    out = _jax.eval_shape(fwd)
    return out.shape, out.dtype

N_MICROBATCH = 1
ADAM_LR = 0.001
ADAM_B1 = 0.9
ADAM_B2 = 0.999
ADAM_EPS = 1e-08
ADAM_WD = 0.01
ADAM_STEP = 10
PER_EXAMPLE_BATCH_AXIS = {'x': 0, 'c': 0, 'loss_target': 0}
SHARED_INPUTS = []
_WEIGHT_DTYPES = {'rel_bias': _jnp.float32, 'w_ada': _jnp.float32, 'b_ada': _jnp.float32, 'pre_norm_g': _jnp.float32, 'post_norm_g': _jnp.float32, 'w_ffn1_in': _jnp.float32, 'w_ffn1_out': _jnp.float32, 'w_in': _jnp.float32, 'sinks': _jnp.float32, 'gmlp_ln_g': _jnp.float32, 'gmlp_ln_b': _jnp.float32, 'gmlp_w_s': _jnp.float32, 'gmlp_b_s': _jnp.float32, 'w_br_attn': _jnp.float32, 'w_br_gmlp': _jnp.float32, 'w_out': _jnp.float32, 'w_ffn2_in': _jnp.float32, 'w_ffn2_out': _jnp.float32}
MOMENT_SCALE = {'rel_bias': 4.758584e-02, 'w_ada': 1.539635e+00, 'b_ada': 3.372972e+00, 'pre_norm_g': 1.234918e-01, 'post_norm_g': 4.066817e+00, 'w_ffn1_in': 4.424024e-02, 'w_ffn1_out': 7.816615e-02, 'w_in': 1.664657e-01, 'sinks': 3.563177e-02, 'gmlp_ln_g': 9.391178e-02, 'gmlp_ln_b': 8.276094e-02, 'gmlp_w_s': 6.104710e-02, 'gmlp_b_s': 8.617450e-02, 'w_br_attn': 3.007019e-01, 'w_br_gmlp': 3.648479e-01, 'w_out': 4.686523e-01, 'w_ffn2_in': 4.218631e-02, 'w_ffn2_out': 7.529162e-02}


def _to_microbatches(a, axis):
    t = _jnp.moveaxis(a, axis, 0)
    t = t.reshape((N_MICROBATCH, t.shape[0] // N_MICROBATCH) + t.shape[1:])
    return _jnp.moveaxis(t, 1, axis + 1)


def setup_inputs(seed: int = 0) -> dict:
    inp = _fwd_setup_inputs(seed)
    key = _jax.random.fold_in(_jax.random.key(seed), 7919)
    shape, _ = _output_shape()
    out = dict(inp)
    out["loss_target"] = _jax.random.normal(_jax.random.fold_in(key, 0), shape, _jnp.float32)
    for i, name in enumerate(TWIN_WEIGHTS):
        w = inp[name].astype(_jnp.float32)
        if MOMENT_SCALE is None:
            s = _jnp.sqrt(_jnp.mean(_jnp.square(w)) + 1e-30)
        else:
            s = MOMENT_SCALE[name]
        km, kv = _jax.random.split(_jax.random.fold_in(key, i + 1))
        out[name] = w
        out["m_" + name] = s * _jax.random.normal(km, w.shape, _jnp.float32)
        out["v_" + name] = (s * s) * _jax.random.uniform(kv, w.shape, _jnp.float32, 0.5, 1.5)
    if N_MICROBATCH > 1:
        for name, axis in PER_EXAMPLE_BATCH_AXIS.items():
            out[name] = _to_microbatches(out[name], axis)
    return {'x': out['x'], 'c': out['c'], 'rel_bias': out['rel_bias'], 'w_ada': out['w_ada'], 'b_ada': out['b_ada'], 'pre_norm_g': out['pre_norm_g'], 'post_norm_g': out['post_norm_g'], 'w_ffn1_in': out['w_ffn1_in'], 'w_ffn1_out': out['w_ffn1_out'], 'w_in': out['w_in'], 'sinks': out['sinks'], 'gmlp_ln_g': out['gmlp_ln_g'], 'gmlp_ln_b': out['gmlp_ln_b'], 'gmlp_w_s': out['gmlp_w_s'], 'gmlp_b_s': out['gmlp_b_s'], 'w_br_attn': out['w_br_attn'], 'w_br_gmlp': out['w_br_gmlp'], 'w_out': out['w_out'], 'w_ffn2_in': out['w_ffn2_in'], 'w_ffn2_out': out['w_ffn2_out'], 'loss_target': out['loss_target'], 'm_rel_bias': out['m_rel_bias'], 'm_w_ada': out['m_w_ada'], 'm_b_ada': out['m_b_ada'], 'm_pre_norm_g': out['m_pre_norm_g'], 'm_post_norm_g': out['m_post_norm_g'], 'm_w_ffn1_in': out['m_w_ffn1_in'], 'm_w_ffn1_out': out['m_w_ffn1_out'], 'm_w_in': out['m_w_in'], 'm_sinks': out['m_sinks'], 'm_gmlp_ln_g': out['m_gmlp_ln_g'], 'm_gmlp_ln_b': out['m_gmlp_ln_b'], 'm_gmlp_w_s': out['m_gmlp_w_s'], 'm_gmlp_b_s': out['m_gmlp_b_s'], 'm_w_br_attn': out['m_w_br_attn'], 'm_w_br_gmlp': out['m_w_br_gmlp'], 'm_w_out': out['m_w_out'], 'm_w_ffn2_in': out['m_w_ffn2_in'], 'm_w_ffn2_out': out['m_w_ffn2_out'], 'v_rel_bias': out['v_rel_bias'], 'v_w_ada': out['v_w_ada'], 'v_b_ada': out['v_b_ada'], 'v_pre_norm_g': out['v_pre_norm_g'], 'v_post_norm_g': out['v_post_norm_g'], 'v_w_ffn1_in': out['v_w_ffn1_in'], 'v_w_ffn1_out': out['v_w_ffn1_out'], 'v_w_in': out['v_w_in'], 'v_sinks': out['v_sinks'], 'v_gmlp_ln_g': out['v_gmlp_ln_g'], 'v_gmlp_ln_b': out['v_gmlp_ln_b'], 'v_gmlp_w_s': out['v_gmlp_w_s'], 'v_gmlp_b_s': out['v_gmlp_b_s'], 'v_w_br_attn': out['v_w_br_attn'], 'v_w_br_gmlp': out['v_w_br_gmlp'], 'v_w_out': out['v_w_out'], 'v_w_ffn2_in': out['v_w_ffn2_in'], 'v_w_ffn2_out': out['v_w_ffn2_out']}


def _loss(weights, diff, rest, loss_target):
    with _jax.named_scope("forward"):
        args = {**rest, TWIN_DIFF_INPUT: diff, **{k: w.astype(_WEIGHT_DTYPES[k]) for k, w in weights.items()}}
        y = _forward(args)
    with _jax.named_scope("loss_head"):
        err = _jnp.square(y.astype(_jnp.float32) - loss_target)
        return 0.5 * _jnp.sum(_jnp.mean(err, axis=-1)) if err.ndim else 0.5 * err


def _adamw(w, g, m, v):
    m = ADAM_B1 * m + (1.0 - ADAM_B1) * g
    v = ADAM_B2 * v + (1.0 - ADAM_B2) * _jnp.square(g)
    m_hat = m / (1.0 - ADAM_B1 ** ADAM_STEP)
    v_hat = v / (1.0 - ADAM_B2 ** ADAM_STEP)
    delta = -ADAM_LR * (m_hat / (_jnp.sqrt(v_hat) + ADAM_EPS) + ADAM_WD * w)
    return delta, m, v


def reference(x, c, rel_bias, w_ada, b_ada, pre_norm_g, post_norm_g, w_ffn1_in, w_ffn1_out, w_in, sinks, gmlp_ln_g, gmlp_ln_b, gmlp_w_s, gmlp_b_s, w_br_attn, w_br_gmlp, w_out, w_ffn2_in, w_ffn2_out, loss_target, m_rel_bias, m_w_ada, m_b_ada, m_pre_norm_g, m_post_norm_g, m_w_ffn1_in, m_w_ffn1_out, m_w_in, m_sinks, m_gmlp_ln_g, m_gmlp_ln_b, m_gmlp_w_s, m_gmlp_b_s, m_w_br_attn, m_w_br_gmlp, m_w_out, m_w_ffn2_in, m_w_ffn2_out, v_rel_bias, v_w_ada, v_b_ada, v_pre_norm_g, v_post_norm_g, v_w_ffn1_in, v_w_ffn1_out, v_w_in, v_sinks, v_gmlp_ln_g, v_gmlp_ln_b, v_gmlp_w_s, v_gmlp_b_s, v_w_br_attn, v_w_br_gmlp, v_w_out, v_w_ffn2_in, v_w_ffn2_out):
    given = dict(x=x, c=c, rel_bias=rel_bias, w_ada=w_ada, b_ada=b_ada, pre_norm_g=pre_norm_g, post_norm_g=post_norm_g, w_ffn1_in=w_ffn1_in, w_ffn1_out=w_ffn1_out, w_in=w_in, sinks=sinks, gmlp_ln_g=gmlp_ln_g, gmlp_ln_b=gmlp_ln_b, gmlp_w_s=gmlp_w_s, gmlp_b_s=gmlp_b_s, w_br_attn=w_br_attn, w_br_gmlp=w_br_gmlp, w_out=w_out, w_ffn2_in=w_ffn2_in, w_ffn2_out=w_ffn2_out, loss_target=loss_target, m_rel_bias=m_rel_bias, m_w_ada=m_w_ada, m_b_ada=m_b_ada, m_pre_norm_g=m_pre_norm_g, m_post_norm_g=m_post_norm_g, m_w_ffn1_in=m_w_ffn1_in, m_w_ffn1_out=m_w_ffn1_out, m_w_in=m_w_in, m_sinks=m_sinks, m_gmlp_ln_g=m_gmlp_ln_g, m_gmlp_ln_b=m_gmlp_ln_b, m_gmlp_w_s=m_gmlp_w_s, m_gmlp_b_s=m_gmlp_b_s, m_w_br_attn=m_w_br_attn, m_w_br_gmlp=m_w_br_gmlp, m_w_out=m_w_out, m_w_ffn2_in=m_w_ffn2_in, m_w_ffn2_out=m_w_ffn2_out, v_rel_bias=v_rel_bias, v_w_ada=v_w_ada, v_b_ada=v_b_ada, v_pre_norm_g=v_pre_norm_g, v_post_norm_g=v_post_norm_g, v_w_ffn1_in=v_w_ffn1_in, v_w_ffn1_out=v_w_ffn1_out, v_w_in=v_w_in, v_sinks=v_sinks, v_gmlp_ln_g=v_gmlp_ln_g, v_gmlp_ln_b=v_gmlp_ln_b, v_gmlp_w_s=v_gmlp_w_s, v_gmlp_b_s=v_gmlp_b_s, v_w_br_attn=v_w_br_attn, v_w_br_gmlp=v_w_br_gmlp, v_w_out=v_w_out, v_w_ffn2_in=v_w_ffn2_in, v_w_ffn2_out=v_w_ffn2_out)
    weights = {n: given[n] for n in TWIN_WEIGHTS}
    shared = {n: given[n] for n in SHARED_INPUTS}
    per_example = {n: given[n] for n in ['x', 'c']}
    grad_fn = _jax.value_and_grad(_loss, argnums=(0, 1))

    def one_microbatch(ex, loss_target):
        ex = dict(ex)
        diff = ex.pop(TWIN_DIFF_INPUT)
        return grad_fn(weights, diff, {**shared, **ex}, loss_target)

    if N_MICROBATCH == 1:
        loss, (grad_w, grad_x) = one_microbatch(per_example, given["loss_target"])
    else:
        def body(carry, xs):
            loss_sum, grad_sum = carry
            l_k, (gw_k, gx_k) = one_microbatch(xs[0], xs[1])
            with _jax.named_scope("update"):
                return (loss_sum + l_k, _jax.tree.map(_jnp.add, grad_sum, gw_k)), gx_k

        init = (_jnp.zeros((), _jnp.float32), _jax.tree.map(_jnp.zeros_like, weights))
        (loss, grad_w), grad_x = _jax.lax.scan(body, init, (per_example, given["loss_target"]))
    with _jax.named_scope("update"):
        delta_w, new_m, new_v = {}, {}, {}
        for n in TWIN_WEIGHTS:
            delta_w[n], new_m[n], new_v[n] = _adamw(weights[n], grad_w[n], given["m_" + n], given["v_" + n])
    return (loss, grad_x, *[grad_w[n] for n in TWIN_WEIGHTS], *[delta_w[n] for n in TWIN_WEIGHTS],
            *[new_m[n] for n in TWIN_WEIGHTS], *[new_v[n] for n in TWIN_WEIGHTS])
```

```python
import functools
import math

import jax
import jax.numpy as jnp
from jax import lax
from jax.experimental import pallas as pl
from jax.experimental.pallas import tpu as pltpu

F32 = jnp.float32
BF = jnp.bfloat16

N_DEV = 8
D = 1024
D_FF = 2816
FS = D_FF // 4
N_HEADS = 8
N_KV = 2
GROUP = 4
HD = 64
BLK = 128
Q_W = 512
KV_W = 128
G_W = 512
QKV_W = Q_W + 2 * KV_W
ZG_OFF = QKV_W
GATE_OFF = ZG_OFF + 2 * G_W
IN_W = GATE_OFF + 2 * D
N_BUCKETS = 32
MAX_DISTANCE = 128
EPS = 1e-6
NEG = -1e30
SCALE = HD ** -0.5
ADA_W = 9 * D // N_DEV

ADAM_LR = 0.001
ADAM_B1 = 0.9
ADAM_B2 = 0.999
ADAM_EPS = 1e-08
ADAM_WD = 0.01
ADAM_STEP = 10

MIB = 1024 * 1024
MESH = pl.DeviceIdType.MESH
HIGH = lax.Precision.HIGHEST


def _cp(n_grid, vmem_mib):
    return pltpu.CompilerParams(dimension_semantics=("arbitrary",) * n_grid,
                                vmem_limit_bytes=vmem_mib * MIB)


def _const(shape):
    return pl.BlockSpec(shape, lambda *_: (0,) * len(shape))


def _sds(shape, dtype):
    return jax.ShapeDtypeStruct(shape, dtype)


def _dot(a, b):
    return jnp.dot(a, b, preferred_element_type=F32)


def _dot_nt(a, b):
    return lax.dot_general(a, b, (((1,), (1,)), ((), ())), preferred_element_type=F32)


def _dot_tn(a, b):
    return lax.dot_general(a, b, (((0,), (0,)), ((), ())), preferred_element_type=F32)


def _rms_r(x):
    return lax.rsqrt(jnp.mean(x * x, axis=-1, keepdims=True) + EPS)


def _colsum(x):
    return jnp.sum(x, axis=0, keepdims=True)


def _prenorm(x, gp, sc, sh):
    return (x * _rms_r(x) * gp) * (1.0 + sc) + sh


def _prenorm_bwd(dn, x, gp, sc):
    r = _rms_r(x)
    xh = x * r
    t = dn * (1.0 + sc) * gp
    dx = r * (t - xh * jnp.mean(t * xh, axis=-1, keepdims=True))
    return dx, _colsum(dn), _colsum(dn * xh * gp), _colsum(dn * (1.0 + sc) * xh)


def _postnorm_bwd(dh, y, gate, gp, res):
    r = _rms_r(y)
    yh = y * r
    dyn = (res * gate) * dh
    t = dyn * gp
    dy = r * (t - yh * jnp.mean(t * yh, axis=-1, keepdims=True))
    return dy, _colsum(res * dh * yh * gp), _colsum(dyn * yh)


def _gelu(x):
    k = math.sqrt(2.0 / math.pi)
    return 0.5 * x * (1.0 + jnp.tanh(k * (x + 0.044715 * x * x * x)))


def _gelu_grad(x):
    k = math.sqrt(2.0 / math.pi)
    t = jnp.tanh(k * (x + 0.044715 * x * x * x))
    return 0.5 * (1.0 + t) + 0.5 * x * (1.0 - t * t) * (k * (1.0 + 3.0 * 0.044715 * x * x))


def _my_place():
    x, y, c = lax.axis_index("x"), lax.axis_index("y"), lax.axis_index("c")
    return x, y, c, 4 * x + 2 * y + c


def _peer(x, y, c, k):
    px = 1 - x if k & 4 else x
    py = 1 - y if k & 2 else y
    pc = 1 - c if k & 1 else c
    return (px, py, pc), 4 * px + 2 * py + pc


def _all_gather_hbm(shards, name):
    n = len(shards)

    def body(*refs):
        ins, outs = refs[:n], refs[n:2 * n]
        send, recv, loc = refs[2 * n:]
        x, y, c, me = _my_place()
        local = [pltpu.make_async_copy(ins[t], outs[t].at[me], loc.at[t]) for t in range(n)]
        for cp in local:
            cp.start()
        remote = []
        for t in range(n):
            for k in range(1, N_DEV):
                peer, _ = _peer(x, y, c, k)
                cp = pltpu.make_async_remote_copy(
                    src_ref=ins[t], dst_ref=outs[t].at[me], send_sem=send.at[t * 7 + k - 1],
                    recv_sem=recv.at[t * 7 + k - 1], device_id=peer, device_id_type=MESH)
                cp.start()
                remote.append(cp)
        for cp in remote:
            cp.wait()
        for cp in local:
            cp.wait()

    return pl.pallas_call(
        body, name=name,
        out_shape=[_sds((N_DEV,) + s.shape, s.dtype) for s in shards],
        in_specs=[pl.BlockSpec(memory_space=pl.ANY)] * n,
        out_specs=[pl.BlockSpec(memory_space=pl.ANY)] * n,
        scratch_shapes=[pltpu.SemaphoreType.DMA((7 * n,)), pltpu.SemaphoreType.DMA((7 * n,)),
                        pltpu.SemaphoreType.DMA((n,))],
    )(*shards)


def _exchange_slabs_hbm(grads, name):
    n = len(grads)

    def body(*refs):
        ins, outs = refs[:n], refs[n:2 * n]
        send, recv, loc = refs[2 * n:]
        x, y, c, me = _my_place()
        local = [pltpu.make_async_copy(ins[t].at[me], outs[t].at[me], loc.at[t]) for t in range(n)]
        for cp in local:
            cp.start()
        remote = []
        for t in range(n):
            for k in range(1, N_DEV):
                peer, peer_lin = _peer(x, y, c, k)
                cp = pltpu.make_async_remote_copy(
                    src_ref=ins[t].at[peer_lin], dst_ref=outs[t].at[me], send_sem=send.at[t * 7 + k - 1],
                    recv_sem=recv.at[t * 7 + k - 1], device_id=peer, device_id_type=MESH)
                cp.start()
                remote.append(cp)
        for cp in remote:
            cp.wait()
        for cp in local:
            cp.wait()

    return pl.pallas_call(
        body, name=name,
        out_shape=[_sds(g.shape, g.dtype) for g in grads],
        in_specs=[pl.BlockSpec(memory_space=pl.ANY)] * n,
        out_specs=[pl.BlockSpec(memory_space=pl.ANY)] * n,
        scratch_shapes=[pltpu.SemaphoreType.DMA((7 * n,)), pltpu.SemaphoreType.DMA((7 * n,)),
                        pltpu.SemaphoreType.DMA((n,))],
    )(*grads)


def _ada_forward(small8, w_ada, b_ada64):
    sw = small8.shape[1]

    def body(sm_ref, w_ref, b_ref, gath_ref, ada_ref, part_ref, send1, recv1, send2, recv2):
        x, y, c, me = _my_place()
        row_me = pl.multiple_of(me * 8, 8)
        gath_ref[pl.ds(row_me, 8), :] = sm_ref[...]
        first = []
        for k in range(1, N_DEV):
            peer, _ = _peer(x, y, c, k)
            cp = pltpu.make_async_remote_copy(
                src_ref=sm_ref, dst_ref=gath_ref.at[pl.ds(row_me, 8), :], send_sem=send1.at[k - 1],
                recv_sem=recv1.at[k - 1], device_id=peer, device_id_type=MESH)
            cp.start()
            first.append(cp)
        for cp in first:
            cp.wait()
        cs = gath_ref[:, 0:D]
        cs = cs * jax.nn.sigmoid(cs)
        part_ref[...] = jnp.dot(cs, w_ref[...], preferred_element_type=F32, precision=HIGH)
        ada_ref[pl.ds(row_me, 8), :] = part_ref[pl.ds(row_me, 8), :]
        second = []
        for k in range(1, N_DEV):
            peer, peer_lin = _peer(x, y, c, k)
            cp = pltpu.make_async_remote_copy(
                src_ref=part_ref.at[pl.ds(pl.multiple_of(peer_lin * 8, 8), 8), :],
                dst_ref=ada_ref.at[pl.ds(row_me, 8), :], send_sem=send2.at[k - 1],
                recv_sem=recv2.at[k - 1], device_id=peer, device_id_type=MESH)
            cp.start()
            second.append(cp)
        for cp in second:
            cp.wait()
        ada_ref[...] = ada_ref[...] + b_ref[...]

    vm = pl.BlockSpec(memory_space=pltpu.VMEM)
    return pl.pallas_call(
        body, name="ada_forward",
        out_shape=[_sds((8 * N_DEV, sw), F32), _sds((8 * N_DEV, ADA_W), F32)],
        in_specs=[vm, vm, vm], out_specs=[vm, vm],
        scratch_shapes=[pltpu.VMEM((8 * N_DEV, ADA_W), F32)] + [pltpu.SemaphoreType.DMA((7,))] * 4,
        compiler_params=pltpu.CompilerParams(vmem_limit_bytes=32 * MIB),
    )(small8, w_ada, b_ada64)


def _small_allreduce(pack):
    rows = pack.shape[0]

    def body(p_ref, sum_ref, gath_ref, send, recv):
        x, y, c, me = _my_place()
        gath_ref[me] = p_ref[...]
        cps = []
        for k in range(1, N_DEV):
            peer, _ = _peer(x, y, c, k)
            cp = pltpu.make_async_remote_copy(
                src_ref=p_ref, dst_ref=gath_ref.at[me], send_sem=send.at[k - 1],
                recv_sem=recv.at[k - 1], device_id=peer, device_id_type=MESH)
            cp.start()
            cps.append(cp)
        for cp in cps:
            cp.wait()
        acc = gath_ref[0]
        for j in range(1, N_DEV):
            acc = acc + gath_ref[j]
        sum_ref[...] = acc

    vm = pl.BlockSpec(memory_space=pltpu.VMEM)
    return pl.pallas_call(
        body, name="small_allreduce",
        out_shape=[_sds((rows, 128), F32), _sds((N_DEV, rows, 128), F32)],
        in_specs=[vm], out_specs=[vm, vm],
        scratch_shapes=[pltpu.SemaphoreType.DMA((7,)), pltpu.SemaphoreType.DMA((7,))],
        compiler_params=pltpu.CompilerParams(vmem_limit_bytes=40 * MIB),
    )(pack)


def _ffn_in(h, sh, sc, gp, w8, name):
    S = h.shape[0]
    R = min(1024, S)

    def body(h_ref, sh_ref, sc_ref, gp_ref, wg_ref, wu_ref, n_ref, g_ref, u_ref, a_ref, n_scr):
        @pl.when(pl.program_id(1) == 0)
        def _():
            nb = _prenorm(h_ref[...], gp_ref[...], sc_ref[...], sh_ref[...]).astype(BF)
            n_scr[...] = nb
            n_ref[...] = nb
        n = n_scr[...]
        g = _dot(n, wg_ref[...])
        u = _dot(n, wu_ref[...])
        g_ref[...] = g.astype(BF)
        u_ref[...] = u.astype(BF)
        a_ref[...] = (g * jax.nn.sigmoid(g) * u).astype(BF)

    vec = _const((1, D))
    blk = pl.BlockSpec((None, R, FS), lambda i, s: (s, i, 0))
    return pl.pallas_call(
        body, name=name, grid=(S // R, 4),
        out_shape=[_sds((S, D), BF)] + [_sds((4, S, FS), BF)] * 3,
        in_specs=[pl.BlockSpec((R, D), lambda i, s: (i, 0)), vec, vec, vec,
                  pl.BlockSpec((None, D, FS), lambda i, s: (s, 0, 0)),
                  pl.BlockSpec((None, D, FS), lambda i, s: (s + 4, 0, 0))],
        out_specs=[pl.BlockSpec((R, D), lambda i, s: (i, 0)), blk, blk, blk],
        scratch_shapes=[pltpu.VMEM((R, D), BF)],
        compiler_params=_cp(2, 48),
    )(h, sh, sc, gp, w8, w8)


def _ffn_out(a, w4, h, gate, gp, name):
    S = h.shape[0]
    R = min(512, S)

    def body(a_ref, w_ref, h_ref, gate_ref, gp_ref, hn_ref, y_ref):
        y = _dot(a_ref[0], w_ref[0])
        for s in range(1, 4):
            y = y + _dot(a_ref[s], w_ref[s])
        hn_ref[...] = h_ref[...] + (0.5 * gate_ref[...]) * (y * _rms_r(y) * gp_ref[...])
        y_ref[...] = y

    vec = _const((1, D))
    row = pl.BlockSpec((R, D), lambda i: (i, 0))
    return pl.pallas_call(
        body, name=name, grid=(S // R,),
        out_shape=[_sds((S, D), F32), _sds((S, D), F32)],
        in_specs=[pl.BlockSpec((4, R, FS), lambda i: (0, i, 0)), _const((4, FS, D)), row, vec, vec],
        out_specs=[row, row],
        compiler_params=_cp(1, 48),
    )(a, w4, h, gate, gp)


def _ffn_out_bwd(dh, y, g, u, w4, gate, gp, name):
    S = dh.shape[0]
    R = min(256, S)

    def body(dh_ref, y_ref, g_ref, u_ref, w_ref, gate_ref, gp_ref, dy_ref, dgu_ref, dgate_ref, dgp_ref):
        @pl.when(pl.program_id(0) == 0)
        def _():
            dgate_ref[...] = jnp.zeros_like(dgate_ref)
            dgp_ref[...] = jnp.zeros_like(dgp_ref)
        dy, dgate, dgp = _postnorm_bwd(dh_ref[...], y_ref[...], gate_ref[...], gp_ref[...], 0.5)
        dgate_ref[...] += dgate
        dgp_ref[...] += dgp
        dyb = dy.astype(BF)
        dy_ref[...] = dyb
        for s in range(4):
            da = _dot_nt(dyb, w_ref[s])
            gg = g_ref[s].astype(F32)
            uu = u_ref[s].astype(F32)
            sg = jax.nn.sigmoid(gg)
            dgu_ref[s] = (da * uu * (sg * (1.0 + gg * (1.0 - sg)))).astype(BF)
            dgu_ref[s + 4] = (da * (gg * sg)).astype(BF)

    vec = _const((1, D))
    row = pl.BlockSpec((R, D), lambda i: (i, 0))
    blk4 = pl.BlockSpec((4, R, FS), lambda i: (0, i, 0))
    return pl.pallas_call(
        body, name=name, grid=(S // R,),
        out_shape=[_sds((S, D), BF), _sds((8, S, FS), BF), _sds((1, D), F32), _sds((1, D), F32)],
        in_specs=[row, row, blk4, blk4, _const((4, FS, D)), vec, vec],
        out_specs=[row, pl.BlockSpec((8, R, FS), lambda i: (0, i, 0)), vec, vec],
        compiler_params=_cp(1, 56),
    )(dh, y, g, u, w4, gate, gp)


def _ffn_dn(dgu, w8, h, dh, sc, gp, name):
    S = h.shape[0]
    R = min(1024, S)

    def body(dgu_ref, w_ref, h_ref, dh_ref, sc_ref, gp_ref, out_ref, dsh_ref, dsc_ref, dgp_ref, acc):
        i, j = pl.program_id(0), pl.program_id(1)

        @pl.when((i == 0) & (j == 0))
        def _():
            dsh_ref[...] = jnp.zeros_like(dsh_ref)
            dsc_ref[...] = jnp.zeros_like(dsc_ref)
            dgp_ref[...] = jnp.zeros_like(dgp_ref)

        part = _dot_nt(dgu_ref[...], w_ref[...])

        @pl.when(j == 0)
        def _():
            acc[...] = part

        @pl.when(j > 0)
        def _():
            acc[...] += part

        @pl.when(j == N_DEV - 1)
        def _():
            dx, dsh, dsc, dgp = _prenorm_bwd(acc[...], h_ref[...], gp_ref[...], sc_ref[...])
            out_ref[...] = dh_ref[...] + dx
            dsh_ref[...] += dsh
            dsc_ref[...] += dsc
            dgp_ref[...] += dgp

    vec = _const((1, D))
    row = pl.BlockSpec((R, D), lambda i, j: (i, 0))
    return pl.pallas_call(
        body, name=name, grid=(S // R, N_DEV),
        out_shape=[_sds((S, D), F32)] + [_sds((1, D), F32)] * 3,
        in_specs=[pl.BlockSpec((None, R, FS), lambda i, j: (j, i, 0)),
                  pl.BlockSpec((None, D, FS), lambda i, j: (j, 0, 0)), row, row, vec, vec],
        out_specs=[row, vec, vec, vec],
        scratch_shapes=[pltpu.VMEM((R, D), F32)],
        compiler_params=_cp(2, 48),
    )(dgu, w8, h, dh, sc, gp)


def _tn_matmul(a, b, name):
    a3 = a if a.ndim == 3 else a[None]
    b3 = b if b.ndim == 3 else b[None]
    GA, S, M = a3.shape
    GB, _, N = b3.shape
    ts = min(512, S)
    nk = S // ts

    def body(a_ref, b_ref, o_ref, acc):
        k = pl.program_id(2)
        part = _dot_tn(a_ref[...], b_ref[...])

        @pl.when(k == 0)
        def _():
            acc[...] = part

        @pl.when(k > 0)
        def _():
            acc[...] += part

        @pl.when(k == nk - 1)
        def _():
            o_ref[...] = acc[...].astype(BF)

    return pl.pallas_call(
        body, name=name, grid=(GA, GB, nk),
        out_shape=_sds((GA, GB, M, N), BF),
        in_specs=[pl.BlockSpec((None, ts, M), lambda ga, gb, k: (ga, k, 0)),
                  pl.BlockSpec((None, ts, N), lambda ga, gb, k: (gb, k, 0))],
        out_specs=pl.BlockSpec((None, None, M, N), lambda ga, gb, k: (ga, gb, 0, 0)),
        scratch_shapes=[pltpu.VMEM((M, N), F32)],
        compiler_params=_cp(3, 48),
    )(a3, b3)


def _mix_in(h, sh, sc, gp, w):
    S = h.shape[0]
    R = min(512, S)

    def body(h_ref, sh_ref, sc_ref, gp_ref, w_ref, n_ref, qkv_ref, zg_ref, gates_ref):
        nb = _prenorm(h_ref[...], gp_ref[...], sc_ref[...], sh_ref[...]).astype(BF)
        n_ref[...] = nb
        qkv_ref[...] = _dot(nb, w_ref[:, 0:ZG_OFF]).astype(BF)
        zg_ref[...] = _dot(nb, w_ref[:, ZG_OFF:GATE_OFF]).astype(BF)
        gates_ref[...] = jax.nn.sigmoid(_dot(nb, w_ref[:, GATE_OFF:IN_W])).astype(BF)

    vec = _const((1, D))
    rows = lambda w_: pl.BlockSpec((R, w_), lambda i: (i, 0))
    return pl.pallas_call(
        body, name="mix_in", grid=(S // R,),
        out_shape=[_sds((S, D), BF), _sds((S, QKV_W), BF), _sds((S, 2 * G_W), BF), _sds((S, 2 * D), BF)],
        in_specs=[rows(D), vec, vec, vec, _const((D, IN_W))],
        out_specs=[rows(D), rows(QKV_W), rows(2 * G_W), rows(2 * D)],
        compiler_params=_cp(1, 48),
    )(h, sh, sc, gp, w)


def _bias_table(rel_bias, bucket):
    def body(rel_ref, bk_ref, out_ref):
        bk = bk_ref[...]
        qi = lax.broadcasted_iota(jnp.int32, (BLK, 2 * BLK), 0)
        kj = lax.broadcasted_iota(jnp.int32, (BLK, 2 * BLK), 1)
        dist = qi + BLK - kj
        window = (dist >= 0) & (dist < BLK)
        for h in range(N_HEADS):
            acc = jnp.zeros((BLK, 2 * BLK), F32)
            for b in range(N_BUCKETS):
                acc = jnp.where(bk == b, rel_ref[b, h], acc)
            out_ref[h // GROUP, pl.ds((h % GROUP) * BLK, BLK), :] = jnp.where(window, acc, NEG)

    return pl.pallas_call(
        body, name="bias_table",
        out_shape=_sds((N_KV, GROUP * BLK, 2 * BLK), F32),
        in_specs=[pl.BlockSpec(memory_space=pltpu.SMEM), pl.BlockSpec(memory_space=pltpu.VMEM)],
        out_specs=pl.BlockSpec(memory_space=pltpu.VMEM),
    )(rel_bias, bucket)


def _attn_scores(q, kvc, kvp, bias_ref, sink_ref, blk, kh):
    k2 = jnp.concatenate([kvp[:, kh * HD:(kh + 1) * HD], kvc[:, kh * HD:(kh + 1) * HD]], axis=0)
    v2 = jnp.concatenate([kvp[:, KV_W + kh * HD:KV_W + (kh + 1) * HD],
                          kvc[:, KV_W + kh * HD:KV_W + (kh + 1) * HD]], axis=0)
    q4 = jnp.concatenate([q[:, (kh * GROUP + g) * HD:(kh * GROUP + g + 1) * HD] for g in range(GROUP)], axis=0)
    s = _dot_nt(q4, k2) * SCALE + bias_ref[kh]
    col = lax.broadcasted_iota(jnp.int32, (GROUP * BLK, 2 * BLK), 1)
    s = jnp.where((col >= BLK) | (blk > 0), s, NEG)
    rowg = lax.broadcasted_iota(jnp.int32, (GROUP * BLK, 1), 0) // BLK
    sink = jnp.zeros((GROUP * BLK, 1), F32)
    for g in range(GROUP):
        sink = jnp.where(rowg == g, sink_ref[kh * GROUP + g], sink)
    return q4, k2, v2, s, sink


def _attn_fwd(qkv, bias, sinks):
    S = qkv.shape[0]
    nb = S // BLK

    def body(sink_ref, q_ref, kvc_ref, kvp_ref, bias_ref, o_ref):
        blk = pl.program_id(0)
        q, kvc, kvp = q_ref[...], kvc_ref[...], kvp_ref[...]
        outs = []
        for kh in range(N_KV):
            q4, k2, v2, s, sink = _attn_scores(q, kvc, kvp, bias_ref, sink_ref, blk, kh)
            m = jnp.maximum(jnp.max(s, axis=1, keepdims=True), sink)
            p = jnp.exp(s - m)
            denom = jnp.sum(p, axis=1, keepdims=True) + jnp.exp(sink - m)
            o4 = _dot((p / denom).astype(BF), v2)
            outs += [o4[g * BLK:(g + 1) * BLK] for g in range(GROUP)]
        o_ref[...] = jnp.concatenate(outs, axis=1).astype(BF)

    return pl.pallas_call(
        body, name="attn_fwd", grid=(nb,),
        out_shape=_sds((S, Q_W), BF),
        in_specs=[pl.BlockSpec(memory_space=pltpu.SMEM),
                  pl.BlockSpec((BLK, Q_W), lambda i: (i, 0)),
                  pl.BlockSpec((BLK, 2 * KV_W), lambda i: (i, 2)),
                  pl.BlockSpec((BLK, 2 * KV_W), lambda i: (jnp.maximum(i - 1, 0), 2)),
                  _const((N_KV, GROUP * BLK, 2 * BLK))],
        out_specs=pl.BlockSpec((BLK, Q_W), lambda i: (i, 0)),
        compiler_params=_cp(1, 32),
    )(sinks, qkv, qkv, qkv, bias)


def _attn_bwd(qkv, bias, sinks, do):
    S = qkv.shape[0]
    nb = S // BLK

    def body(sink_ref, q_ref, kvc_ref, kvp_ref, bias_ref, do_ref, dq_ref, dkv_ref, dbias_ref, dsink_ref, carry):
        i = pl.program_id(0)
        blk = nb - 1 - i

        @pl.when(i == 0)
        def _():
            carry[...] = jnp.zeros_like(carry)
            dbias_ref[...] = jnp.zeros_like(dbias_ref)
            dsink_ref[...] = jnp.zeros_like(dsink_ref)

        q, kvc, kvp, do_ = q_ref[...], kvc_ref[...], kvp_ref[...], do_ref[...]
        dqs, dk_cur, dv_cur, dk_prev, dv_prev = [], [], [], [], []
        for kh in range(N_KV):
            q4, k2, v2, s, sink = _attn_scores(q, kvc, kvp, bias_ref, sink_ref, blk, kh)
            m = jnp.maximum(jnp.max(s, axis=1, keepdims=True), sink)
            p = jnp.exp(s - m)
            denom = jnp.sum(p, axis=1, keepdims=True) + jnp.exp(sink - m)
            prob = p / denom
            p_sink = jnp.exp(sink - m) / denom
            pb = prob.astype(BF)
            do4 = jnp.concatenate(
                [do_[:, (kh * GROUP + g) * HD:(kh * GROUP + g + 1) * HD] for g in range(GROUP)], axis=0)
            dp = _dot_nt(do4, v2)
            o4 = _dot(pb, v2)
            delta = jnp.sum(do4.astype(F32) * o4, axis=1, keepdims=True)
            ds = prob * (dp - delta)
            dbias_ref[kh] += ds
            sink_term = p_sink * delta
            for g in range(GROUP):
                h = kh * GROUP + g
                val = -jnp.sum(sink_term[g * BLK:(g + 1) * BLK], axis=0, keepdims=True)
                dsink_ref[pl.ds(h, 1), :] += jnp.broadcast_to(val, (1, 128))
            dsb = ds.astype(BF)
            dq4 = _dot(dsb, k2) * SCALE
            dk2 = _dot_tn(dsb, q4) * SCALE
            dv2 = _dot_tn(pb, do4)
            dqs += [dq4[g * BLK:(g + 1) * BLK] for g in range(GROUP)]
            dk_prev.append(dk2[0:BLK])
            dk_cur.append(dk2[BLK:2 * BLK])
            dv_prev.append(dv2[0:BLK])
            dv_cur.append(dv2[BLK:2 * BLK])
        dq_ref[...] = jnp.concatenate(dqs, axis=1).astype(BF)
        dkv_ref[...] = (jnp.concatenate(dk_cur + dv_cur, axis=1) + carry[...]).astype(BF)
        carry[...] = jnp.concatenate(dk_prev + dv_prev, axis=1)

    return pl.pallas_call(
        body, name="attn_bwd", grid=(nb,),
        out_shape=[_sds((S, Q_W), BF), _sds((S, 2 * KV_W), BF),
                   _sds((N_KV, GROUP * BLK, 2 * BLK), F32), _sds((N_HEADS, 128), F32)],
        in_specs=[pl.BlockSpec(memory_space=pltpu.SMEM),
                  pl.BlockSpec((BLK, Q_W), lambda i: (nb - 1 - i, 0)),
                  pl.BlockSpec((BLK, 2 * KV_W), lambda i: (nb - 1 - i, 2)),
                  pl.BlockSpec((BLK, 2 * KV_W), lambda i: (jnp.maximum(nb - 2 - i, 0), 2)),
                  _const((N_KV, GROUP * BLK, 2 * BLK)),
                  pl.BlockSpec((BLK, Q_W), lambda i: (nb - 1 - i, 0))],
        out_specs=[pl.BlockSpec((BLK, Q_W), lambda i: (nb - 1 - i, 0)),
                   pl.BlockSpec((BLK, 2 * KV_W), lambda i: (nb - 1 - i, 0)),
                   _const((N_KV, GROUP * BLK, 2 * BLK)), _const((N_HEADS, 128))],
        scratch_shapes=[pltpu.VMEM((BLK, 2 * KV_W), F32)],
        compiler_params=_cp(1, 32),
    )(sinks, qkv, qkv, qkv, bias, do)


def _rel_bias_grad(dbias, bucket):
    def body(db_ref, bk_ref, out_ref):
        bk = bk_ref[...]
        lane = lax.broadcasted_iota(jnp.int32, (1, 128), 1)
        for h in range(N_HEADS):
            d = db_ref[h // GROUP, pl.ds((h % GROUP) * BLK, BLK), :]
            row = jnp.zeros((1, 128), F32)
            for b in range(N_BUCKETS):
                tot = jnp.sum(jnp.sum(jnp.where(bk == b, d, 0.0), axis=1, keepdims=True), axis=0, keepdims=True)
                row = jnp.where(lane == b, tot, row)
            out_ref[pl.ds(h, 1), :] = row

    vm = pl.BlockSpec(memory_space=pltpu.VMEM)
    return pl.pallas_call(body, name="rel_bias_grad", out_shape=_sds((N_HEADS, 128), F32),
                          in_specs=[vm, vm], out_specs=vm)(dbias, bucket)


def _gmlp_parts(zg_ref, lg_ref, lb_ref):
    z = zg_ref[...].astype(F32)
    ge = _gelu(z)
    u, vg = ge[:, 0:G_W], ge[:, G_W:2 * G_W]
    mu = jnp.mean(vg, axis=-1, keepdims=True)
    xc = vg - mu
    rstd = lax.rsqrt(jnp.mean(xc * xc, axis=-1, keepdims=True) + EPS)
    xh = xc * rstd
    return z, u, xh, rstd, xh * lg_ref[...] + lb_ref[...]


def _causal_weights(ws_ref, wc):
    t = lax.broadcasted_iota(jnp.int32, (BLK, BLK), 0)
    s = lax.broadcasted_iota(jnp.int32, (BLK, BLK), 1)
    for g in range(N_HEADS):
        wc[g] = jnp.where(s <= t, ws_ref[g], 0.0).astype(BF)


def _spatial(vb, wc, bst_ref, p, low):
    xp = vb[:, p * 128:(p + 1) * 128]
    s0 = _dot(wc[2 * p], xp) + bst_ref[:, 2 * p:2 * p + 1]
    s1 = _dot(wc[2 * p + 1], xp) + bst_ref[:, 2 * p + 1:2 * p + 2]
    return xp, jnp.where(low, s0, s1)


def _gmlp_fwd(zg, lg, lb, ws, bst):
    S = zg.shape[0]

    def body(zg_ref, lg_ref, lb_ref, ws_ref, bst_ref, o_ref, wc):
        @pl.when(pl.program_id(0) == 0)
        def _():
            _causal_weights(ws_ref, wc)
        _, u, _, _, vln = _gmlp_parts(zg_ref, lg_ref, lb_ref)
        vb = vln.astype(BF)
        low = lax.broadcasted_iota(jnp.int32, (BLK, 128), 1) < HD
        for p in range(4):
            _, sp = _spatial(vb, wc, bst_ref, p, low)
            o_ref[:, p * 128:(p + 1) * 128] = (u[:, p * 128:(p + 1) * 128] * sp).astype(BF)

    return pl.pallas_call(
        body, name="gmlp_fwd", grid=(S // BLK,),
        out_shape=_sds((S, G_W), BF),
        in_specs=[pl.BlockSpec((BLK, 2 * G_W), lambda i: (i, 0)), _const((1, G_W)), _const((1, G_W)),
                  _const((N_HEADS, BLK, BLK)), _const((BLK, N_HEADS))],
        out_specs=pl.BlockSpec((BLK, G_W), lambda i: (i, 0)),
        scratch_shapes=[pltpu.VMEM((N_HEADS, BLK, BLK), BF)],
        compiler_params=_cp(1, 32),
    )(zg, lg, lb, ws, bst)


def _gmlp_bwd(zg, d_out, lg, lb, ws, bst):
    S = zg.shape[0]
    nb = S // BLK

    def body(zg_ref, d_ref, lg_ref, lb_ref, ws_ref, bst_ref, dzg_ref, dws_ref, dbs_ref, dlg_ref, dlb_ref, wc, dbacc):
        i = pl.program_id(0)

        @pl.when(i == 0)
        def _():
            _causal_weights(ws_ref, wc)
            dws_ref[...] = jnp.zeros_like(dws_ref)
            dlg_ref[...] = jnp.zeros_like(dlg_ref)
            dlb_ref[...] = jnp.zeros_like(dlb_ref)
            dbacc[...] = jnp.zeros_like(dbacc)

        z, u, xh, rstd, vln = _gmlp_parts(zg_ref, lg_ref, lb_ref)
        vb = vln.astype(BF)
        d = d_ref[...].astype(F32)
        low = lax.broadcasted_iota(jnp.int32, (BLK, 128), 1) < HD
        du_parts, dvln_parts = [], []
        for p in range(4):
            xp, sp = _spatial(vb, wc, bst_ref, p, low)
            dp = d[:, p * 128:(p + 1) * 128]
            du_parts.append(dp * sp)
            dsp = dp * u[:, p * 128:(p + 1) * 128]
            dbacc[:, p * 128:(p + 1) * 128] += dsp
            d0 = jnp.where(low, dsp, 0.0).astype(BF)
            d1 = jnp.where(low, 0.0, dsp).astype(BF)
            dws_ref[2 * p] += _dot_nt(d0, xp)
            dws_ref[2 * p + 1] += _dot_nt(d1, xp)
            dvln_parts.append(_dot_tn(wc[2 * p], d0) + _dot_tn(wc[2 * p + 1], d1))
        dvln = jnp.concatenate(dvln_parts, axis=1)
        dlg_ref[...] += _colsum(dvln * xh)
        dlb_ref[...] += _colsum(dvln)
        dxh = dvln * lg_ref[...]
        dvg = rstd * (dxh - jnp.mean(dxh, axis=-1, keepdims=True)
                      - xh * jnp.mean(dxh * xh, axis=-1, keepdims=True))
        dge = jnp.concatenate(du_parts + [dvg], axis=1)
        dzg_ref[...] = (dge * _gelu_grad(z)).astype(BF)

        @pl.when(i == nb - 1)
        def _():
            t = lax.broadcasted_iota(jnp.int32, (BLK, BLK), 0)
            s = lax.broadcasted_iota(jnp.int32, (BLK, BLK), 1)
            for g in range(N_HEADS):
                dws_ref[g] = jnp.where(s <= t, dws_ref[g], 0.0)
            grp = lax.broadcasted_iota(jnp.int32, (N_HEADS, G_W), 0)
            lane = lax.broadcasted_iota(jnp.int32, (N_HEADS, G_W), 1) // HD
            pick = jnp.where(grp == lane, 1.0, 0.0).astype(F32)
            dbs_ref[...] = lax.dot_general(pick, dbacc[...], (((1,), (1,)), ((), ())),
                                           preferred_element_type=F32, precision=HIGH)

    return pl.pallas_call(
        body, name="gmlp_bwd", grid=(nb,),
        out_shape=[_sds((S, 2 * G_W), BF), _sds((N_HEADS, BLK, BLK), F32), _sds((N_HEADS, BLK), F32),
                   _sds((1, G_W), F32), _sds((1, G_W), F32)],
        in_specs=[pl.BlockSpec((BLK, 2 * G_W), lambda i: (i, 0)), pl.BlockSpec((BLK, G_W), lambda i: (i, 0)),
                  _const((1, G_W)), _const((1, G_W)), _const((N_HEADS, BLK, BLK)), _const((BLK, N_HEADS))],
        out_specs=[pl.BlockSpec((BLK, 2 * G_W), lambda i: (i, 0)), _const((N_HEADS, BLK, BLK)),
                   _const((N_HEADS, BLK)), _const((1, G_W)), _const((1, G_W))],
        scratch_shapes=[pltpu.VMEM((N_HEADS, BLK, BLK), BF), pltpu.VMEM((BLK, G_W), F32)],
        compiler_params=_cp(1, 32),
    )(zg, d_out, lg, lb, ws, bst)


def _mix_out(o, gm, gates, h, wa, wg, wo, gate, gp):
    S = h.shape[0]
    R = min(512, S)

    def body(o_ref, gm_ref, gates_ref, h_ref, wa_ref, wg_ref, wo_ref, gate_ref, gp_ref,
             ya_ref, yg_ref, ym_ref, y_ref, hn_ref):
        ya = _dot(o_ref[...], wa_ref[...])
        yg = _dot(gm_ref[...], wg_ref[...])
        ya_ref[...] = ya.astype(BF)
        yg_ref[...] = yg.astype(BF)
        ym = (gates_ref[:, 0:D].astype(F32) * ya + gates_ref[:, D:2 * D].astype(F32) * yg).astype(BF)
        ym_ref[...] = ym
        y = _dot(ym, wo_ref[...])
        y_ref[...] = y
        hn_ref[...] = h_ref[...] + gate_ref[...] * (y * _rms_r(y) * gp_ref[...])

    vec = _const((1, D))
    rows = lambda w_: pl.BlockSpec((R, w_), lambda i: (i, 0))
    return pl.pallas_call(
        body, name="mix_out", grid=(S // R,),
        out_shape=[_sds((S, D), BF)] * 3 + [_sds((S, D), F32)] * 2,
        in_specs=[rows(Q_W), rows(G_W), rows(2 * D), rows(D), _const((Q_W, D)), _const((G_W, D)),
                  _const((D, D)), vec, vec],
        out_specs=[rows(D)] * 5,
        compiler_params=_cp(1, 48),
    )(o, gm, gates, h, wa, wg, wo, gate, gp)


def _mix_out_bwd(dh, y, ya, yg, gates, wa, wg, wo, gate, gp):
    S = dh.shape[0]
    R = min(256, S)

    def body(dh_ref, y_ref, ya_ref, yg_ref, gates_ref, wa_ref, wg_ref, wo_ref, gate_ref, gp_ref,
             dy_ref, dya_ref, dyg_ref, dz_ref, do_ref, dgm_ref, dgate_ref, dgp_ref):
        @pl.when(pl.program_id(0) == 0)
        def _():
            dgate_ref[...] = jnp.zeros_like(dgate_ref)
            dgp_ref[...] = jnp.zeros_like(dgp_ref)
        dy, dgate, dgp = _postnorm_bwd(dh_ref[...], y_ref[...], gate_ref[...], gp_ref[...], 1.0)
        dgate_ref[...] += dgate
        dgp_ref[...] += dgp
        dyb = dy.astype(BF)
        dy_ref[...] = dyb
        dym = _dot_nt(dyb, wo_ref[...])
        ga = gates_ref[:, 0:D].astype(F32)
        gg = gates_ref[:, D:2 * D].astype(F32)
        dya = (dym * ga).astype(BF)
        dyg = (dym * gg).astype(BF)
        dya_ref[...] = dya
        dyg_ref[...] = dyg
        dz_ref[:, 0:D] = (dym * ya_ref[...].astype(F32) * (ga * (1.0 - ga))).astype(BF)
        dz_ref[:, D:2 * D] = (dym * yg_ref[...].astype(F32) * (gg * (1.0 - gg))).astype(BF)
        do_ref[...] = _dot_nt(dya, wa_ref[...]).astype(BF)
        dgm_ref[...] = _dot_nt(dyg, wg_ref[...]).astype(BF)

    vec = _const((1, D))
    rows = lambda w_: pl.BlockSpec((R, w_), lambda i: (i, 0))
    return pl.pallas_call(
        body, name="mix_out_bwd", grid=(S // R,),
        out_shape=[_sds((S, D), BF)] * 3 + [_sds((S, 2 * D), BF), _sds((S, Q_W), BF), _sds((S, G_W), BF),
                                             _sds((1, D), F32), _sds((1, D), F32)],
        in_specs=[rows(D), rows(D), rows(D), rows(D), rows(2 * D), _const((Q_W, D)), _const((G_W, D)),
                  _const((D, D)), vec, vec],
        out_specs=[rows(D)] * 3 + [rows(2 * D), rows(Q_W), rows(G_W), vec, vec],
        compiler_params=_cp(1, 48),
    )(dh, y, ya, yg, gates, wa, wg, wo, gate, gp)


def _mix_dn(dq, dkv, dzg, dzgate, w, h, dh, sc, gp):
    S = h.shape[0]
    R = min(512, S)

    def body(dq_ref, dkv_ref, dzg_ref, dzt_ref, w_ref, h_ref, dh_ref, sc_ref, gp_ref,
             out_ref, dsh_ref, dsc_ref, dgp_ref):
        @pl.when(pl.program_id(0) == 0)
        def _():
            dsh_ref[...] = jnp.zeros_like(dsh_ref)
            dsc_ref[...] = jnp.zeros_like(dsc_ref)
            dgp_ref[...] = jnp.zeros_like(dgp_ref)
        dn = _dot_nt(dq_ref[...], w_ref[:, 0:Q_W])
        dn = dn + _dot_nt(dkv_ref[...], w_ref[:, Q_W:QKV_W])
        dn = dn + _dot_nt(dzg_ref[...], w_ref[:, ZG_OFF:GATE_OFF])
        dn = dn + _dot_nt(dzt_ref[...], w_ref[:, GATE_OFF:IN_W])
        dx, dsh, dsc, dgp = _prenorm_bwd(dn, h_ref[...], gp_ref[...], sc_ref[...])
        out_ref[...] = dh_ref[...] + dx
        dsh_ref[...] += dsh
        dsc_ref[...] += dsc
        dgp_ref[...] += dgp

    vec = _const((1, D))
    rows = lambda w_: pl.BlockSpec((R, w_), lambda i: (i, 0))
    return pl.pallas_call(
        body, name="mix_dn", grid=(S // R,),
        out_shape=[_sds((S, D), F32)] + [_sds((1, D), F32)] * 3,
        in_specs=[rows(Q_W), rows(2 * KV_W), rows(2 * G_W), rows(2 * D), _const((D, IN_W)),
                  rows(D), rows(D), vec, vec],
        out_specs=[rows(D), vec, vec, vec],
        compiler_params=_cp(1, 48),
    )(dq, dkv, dzg, dzgate, w, h, dh, sc, gp)


def _loss_head(h, target):
    S = h.shape[0]
    R = min(1024, S)

    def body(h_ref, t_ref, tot_ref, dh_ref):
        @pl.when(pl.program_id(0) == 0)
        def _():
            tot_ref[...] = jnp.zeros_like(tot_ref)
        e = h_ref[...] - t_ref[...]
        dh_ref[...] = e * (1.0 / D)
        tot_ref[...] += jnp.sum(jnp.sum(e * e, axis=1, keepdims=True), axis=0, keepdims=True)

    row = pl.BlockSpec((R, D), lambda i: (i, 0))
    return pl.pallas_call(
        body, name="loss_head", grid=(S // R,),
        out_shape=[_sds((1, 1), F32), _sds((S, D), F32)],
        in_specs=[row, row], out_specs=[_const((1, 1)), row],
        compiler_params=_cp(1, 40),
    )(h, target)


def _adamw_math(w, g, m, v):
    m2 = ADAM_B1 * m + (1.0 - ADAM_B1) * g
    v2 = ADAM_B2 * v + (1.0 - ADAM_B2) * (g * g)
    m_hat = m2 / (1.0 - ADAM_B1 ** ADAM_STEP)
    v_hat = v2 / (1.0 - ADAM_B2 ** ADAM_STEP)
    delta = -ADAM_LR * (m_hat / (jnp.sqrt(v_hat) + ADAM_EPS) + ADAM_WD * w)
    return delta, m2, v2


def _row_tile(rows, cols):
    best = None
    for t in range(16, rows + 1, 16):
        if rows % t == 0 and t * cols <= 256 * 1024:
            best = t
    return best if best is not None else rows


def _adamw_sharded(landing, w, m, v, name):
    r, c = w.shape
    tr = _row_tile(r, c)

    def body(l_ref, w_ref, m_ref, v_ref, g_ref, d_ref, m2_ref, v2_ref):
        g = l_ref[0].astype(F32)
        for j in range(1, N_DEV):
            g = g + l_ref[j].astype(F32)
        delta, m2, v2 = _adamw_math(w_ref[...], g, m_ref[...], v_ref[...])
        g_ref[...] = g
        d_ref[...] = delta
        m2_ref[...] = m2
        v2_ref[...] = v2

    row = pl.BlockSpec((tr, c), lambda i: (i, 0))
    return pl.pallas_call(
        body, name=name, grid=(r // tr,),
        out_shape=[_sds((r, c), F32)] * 4,
        in_specs=[pl.BlockSpec((N_DEV, tr, c), lambda i: (0, i, 0)), row, row, row],
        out_specs=[row] * 4,
        compiler_params=_cp(1, 48),
    )(landing, w, m, v)


def _adamw_small(w, g, m, v, name):
    def body(w_ref, g_ref, m_ref, v_ref, d_ref, m2_ref, v2_ref):
        delta, m2, v2 = _adamw_math(w_ref[...], g_ref[...], m_ref[...], v_ref[...])
        d_ref[...] = delta
        m2_ref[...] = m2
        v2_ref[...] = v2

    vm = pl.BlockSpec(memory_space=pltpu.VMEM)
    return pl.pallas_call(body, name=name, out_shape=[_sds(w.shape, F32)] * 3,
                          in_specs=[vm] * 4, out_specs=[vm] * 3)(w, g, m, v)


def _w_ada_update(c64, d_ada, w, m, v):
    tr = 256

    def body(c_ref, d_ref, w_ref, m_ref, v_ref, g_ref, dl_ref, m2_ref, v2_ref):
        i = pl.program_id(0)
        g = jnp.zeros((tr, ADA_W), F32)
        for j in range(N_DEV):
            cj = c_ref[pl.ds(8 * j, 8), :]
            sj = cj * jax.nn.sigmoid(cj)
            dj = jnp.broadcast_to(d_ref[pl.ds(j, 1), :], (8, ADA_W))
            g = g + lax.dot_general(sj, dj, (((0,), (0,)), ((), ())), preferred_element_type=F32,
                                    precision=HIGH) * 0.125
        delta, m2, v2 = _adamw_math(w_ref[...], g, m_ref[...], v_ref[...])
        g_ref[...] = g
        dl_ref[...] = delta
        m2_ref[...] = m2
        v2_ref[...] = v2

    row = pl.BlockSpec((tr, ADA_W), lambda i: (i, 0))
    return pl.pallas_call(
        body, name="w_ada_update", grid=(D // tr,),
        out_shape=[_sds((D, ADA_W), F32)] * 4,
        in_specs=[pl.BlockSpec((8 * N_DEV, tr), lambda i: (0, i)), _const((N_DEV, ADA_W)), row, row, row],
        out_specs=[row] * 4,
        compiler_params=_cp(1, 40),
    )(c64, d_ada, w, m, v)


def _t5_bucket():
    qi = jnp.arange(BLK, dtype=jnp.int32)[:, None]
    kj = jnp.arange(2 * BLK, dtype=jnp.int32)[None, :]
    dist = jnp.maximum(qi + BLK - kj, 0)
    max_exact = N_BUCKETS // 2
    d_f = jnp.maximum(dist, max_exact).astype(F32)
    large = max_exact + (jnp.log(d_f / max_exact) / math.log(MAX_DISTANCE / max_exact)
                         * (N_BUCKETS - max_exact)).astype(jnp.int32)
    large = jnp.minimum(large, N_BUCKETS - 1)
    return jnp.where(dist < max_exact, dist, large)


def _slabs_of_columns(w):
    r, c8 = w.shape
    return jnp.transpose(w.reshape(r, N_DEV, c8 // N_DEV), (1, 0, 2))


def _columns_of_slabs(w8):
    _, r, c = w8.shape
    return jnp.transpose(w8, (1, 0, 2)).reshape(r, N_DEV * c)


def kernel(x, c, rel_bias, w_ada, b_ada, pre_norm_g, post_norm_g, w_ffn1_in, w_ffn1_out, w_in, sinks, gmlp_ln_g, gmlp_ln_b, gmlp_w_s, gmlp_b_s, w_br_attn, w_br_gmlp, w_out, w_ffn2_in, w_ffn2_out, loss_target, m_rel_bias, m_w_ada, m_b_ada, m_pre_norm_g, m_post_norm_g, m_w_ffn1_in, m_w_ffn1_out, m_w_in, m_sinks, m_gmlp_ln_g, m_gmlp_ln_b, m_gmlp_w_s, m_gmlp_b_s, m_w_br_attn, m_w_br_gmlp, m_w_out, m_w_ffn2_in, m_w_ffn2_out, v_rel_bias, v_w_ada, v_b_ada, v_pre_norm_g, v_post_norm_g, v_w_ffn1_in, v_w_ffn1_out, v_w_in, v_sinks, v_gmlp_ln_g, v_gmlp_ln_b, v_gmlp_w_s, v_gmlp_b_s, v_w_br_attn, v_w_br_gmlp, v_w_out, v_w_ffn2_in, v_w_ffn2_out):
    me = 4 * lax.axis_index("x") + 2 * lax.axis_index("y") + lax.axis_index("c")
    x0 = x[0]
    target = loss_target[0]

    small = jnp.concatenate([c[0], pre_norm_g[0].reshape(-1), post_norm_g[0].reshape(-1)])
    small8 = jnp.broadcast_to(small[None, :], (8, small.shape[0]))
    b_ada64 = jnp.repeat(b_ada.reshape(N_DEV, ADA_W), 8, axis=0)
    gath, ada64 = _ada_forward(small8, w_ada[0], b_ada64)
    gath8 = gath[::8]
    ada = ada64[::8].reshape(9, D)
    sh1, sc1, g1, sh2, sc2, g2, sh3, sc3, g3 = [ada[k:k + 1] for k in range(9)]
    gains = gath8[:, D:].reshape(N_DEV, 2, 3, 128)
    pre_g = jnp.transpose(gains[:, 0], (1, 0, 2)).reshape(3, D)
    post_g = jnp.transpose(gains[:, 1], (1, 0, 2)).reshape(3, D)
    pre = [pre_g[k:k + 1] for k in range(3)]
    post = [post_g[k:k + 1] for k in range(3)]

    shards = [w_ffn1_in[0], w_ffn1_out[0], w_in[0], w_br_attn[0], w_br_gmlp[0], w_out[0],
              w_ffn2_in[0], w_ffn2_out[0]]
    full = _all_gather_hbm([s.astype(BF) for s in shards], "weights_all_gather")
    wf1_in = full[0]
    wf1_out = full[1].reshape(4, FS, D)
    w_in_full = _columns_of_slabs(full[2])
    w_bra = _columns_of_slabs(full[3])
    w_brg = _columns_of_slabs(full[4])
    w_out_full = full[5].reshape(D, D)
    wf2_in = full[6]
    wf2_out = full[7].reshape(4, FS, D)

    bucket = _t5_bucket()
    bias = _bias_table(rel_bias, bucket)
    sinks8 = sinks[0]
    lg, lb = gmlp_ln_g, gmlp_ln_b
    ws = gmlp_w_s[0]
    bst = jnp.transpose(gmlp_b_s[0])

    n1, fg1, fu1, fa1 = _ffn_in(x0, sh1, sc1, pre[0], wf1_in, "ffn1_in")
    h1, y1 = _ffn_out(fa1, wf1_out, x0, g1, post[0], "ffn1_out")
    n2, qkv, zg, gates = _mix_in(h1, sh2, sc2, pre[1], w_in_full)
    att = _attn_fwd(qkv, bias, sinks8)
    gm = _gmlp_fwd(zg, lg, lb, ws, bst)
    ya, yg, ymix, y2, h2 = _mix_out(att, gm, gates, h1, w_bra, w_brg, w_out_full, g2, post[1])
    n3, fg3, fu3, fa3 = _ffn_in(h2, sh3, sc3, pre[2], wf2_in, "ffn2_in")
    h3, y3 = _ffn_out(fa3, wf2_out, h2, g3, post[2], "ffn2_out")
    sq, dh3 = _loss_head(h3, target)
    loss = lax.psum(0.5 * sq[0, 0] / D, ("x", "y", "c"))

    dy3, dgu3, d_g3, d_post2 = _ffn_out_bwd(dh3, y3, fg3, fu3, wf2_out, g3, post[2], "ffn2_out_bwd")
    gw_f2_out = _tn_matmul(fa3, dy3, "ffn2_out_wgrad").reshape(N_DEV, D_FF // N_DEV, D)
    dh2, d_sh3, d_sc3, d_pre2 = _ffn_dn(dgu3, wf2_in, h2, dh3, sc3, pre[2], "ffn2_dn")
    gw_f2_in = _tn_matmul(n3, dgu3, "ffn2_in_wgrad").reshape(N_DEV, D, FS)

    dy2, dya, dyg, dzgate, d_att, d_gm, d_g2, d_post1 = _mix_out_bwd(
        dh2, y2, ya, yg, gates, w_bra, w_brg, w_out_full, g2, post[1])
    gw_out = _tn_matmul(ymix, dy2, "w_out_wgrad").reshape(N_DEV, D // N_DEV, D)
    gw_bra = _slabs_of_columns(_tn_matmul(att, dya, "w_br_attn_wgrad").reshape(Q_W, D))
    gw_brg = _slabs_of_columns(_tn_matmul(gm, dyg, "w_br_gmlp_wgrad").reshape(G_W, D))
    dq, dkv, dbias, dsink = _attn_bwd(qkv, bias, sinks8, d_att)
    dzg, d_ws, d_bs, d_lg, d_lb = _gmlp_bwd(zg, d_gm, lg, lb, ws, bst)
    dh1, d_sh2, d_sc2, d_pre1 = _mix_dn(dq, dkv, dzg, dzgate, w_in_full, h1, dh2, sc2, pre[1])
    gw_in = _slabs_of_columns(jnp.concatenate(
        [_tn_matmul(n2, dq, "w_in_q_wgrad").reshape(D, Q_W),
         _tn_matmul(n2, dkv, "w_in_kv_wgrad").reshape(D, 2 * KV_W),
         _tn_matmul(n2, dzg, "w_in_zg_wgrad").reshape(D, 2 * G_W),
         _tn_matmul(n2, dzgate, "w_in_gate_wgrad").reshape(D, 2 * D)], axis=1))

    dy1, dgu1, d_g1, d_post0 = _ffn_out_bwd(dh1, y1, fg1, fu1, wf1_out, g1, post[0], "ffn1_out_bwd")
    gw_f1_out = _tn_matmul(fa1, dy1, "ffn1_out_wgrad").reshape(N_DEV, D_FF // N_DEV, D)
    grad_x, d_sh1, d_sc1, d_pre0 = _ffn_dn(dgu1, wf1_in, x0, dh1, sc1, pre[0], "ffn1_dn")
    gw_f1_in = _tn_matmul(n1, dgu1, "ffn1_in_wgrad").reshape(N_DEV, D, FS)

    landing = _exchange_slabs_hbm(
        [gw_f1_in, gw_f1_out, gw_in, gw_bra, gw_brg, gw_out, gw_f2_in, gw_f2_out], "wgrad_exchange")
    moments = [(m_w_ffn1_in, v_w_ffn1_in), (m_w_ffn1_out, v_w_ffn1_out), (m_w_in, v_w_in),
               (m_w_br_attn, v_w_br_attn), (m_w_br_gmlp, v_w_br_gmlp), (m_w_out, v_w_out),
               (m_w_ffn2_in, v_w_ffn2_in), (m_w_ffn2_out, v_w_ffn2_out)]
    names = ["w_ffn1_in", "w_ffn1_out", "w_in", "w_br_attn", "w_br_gmlp", "w_out", "w_ffn2_in", "w_ffn2_out"]
    big = {}
    for nm, land, w_, (m_, v_) in zip(names, landing, shards, moments):
        big[nm] = [a[None] for a in _adamw_sharded(land, w_, m_[0], v_[0], "adamw_" + nm)]

    d_rel = _rel_bias_grad(dbias, bucket)
    d_ada = jnp.concatenate([v_.reshape(8, 128) for v_ in
                             (d_sh1, d_sc1, d_g1, d_sh2, d_sc2, d_g2, d_sh3, d_sc3, d_g3)], axis=0)
    d_pre = jnp.concatenate([d_pre0, d_pre1, d_pre2], axis=0)
    d_post = jnp.concatenate([d_post0, d_post1, d_post2], axis=0)
    pack = jnp.concatenate([
        d_ada,
        _slabs_of_columns(d_pre).reshape(24, 128),
        _slabs_of_columns(d_post).reshape(24, 128),
        jnp.concatenate([d_lg.reshape(4, 128), d_lb.reshape(4, 128)], axis=0),
        d_bs, d_rel, dsink,
        d_ws.reshape(N_HEADS * BLK, BLK)], axis=0)
    tot, every = _small_allreduce(pack)

    g_b_ada = tot[0:72].reshape(1, 9 * D)
    g_pre = lax.dynamic_slice_in_dim(tot[72:96], 3 * me, 3, axis=0)[None]
    g_post = lax.dynamic_slice_in_dim(tot[96:120], 3 * me, 3, axis=0)[None]
    g_lg = tot[120:124].reshape(1, G_W)
    g_lb = tot[124:128].reshape(1, G_W)
    g_bs = tot[128:136][None]
    g_rel = jnp.transpose(tot[136:144, 0:N_BUCKETS])
    g_sinks = tot[144:152, 0][None]
    g_ws = tot[152:1176].reshape(1, N_HEADS, BLK, BLK)

    d_ada_mine = lax.dynamic_slice_in_dim(every[:, 0:72].reshape(N_DEV, N_DEV, ADA_W), me, 1, axis=1)[:, 0]
    c64 = jnp.repeat(gath8[:, 0:D], 8, axis=0)
    ada_out = [a[None] for a in _w_ada_update(c64, d_ada_mine, w_ada[0], m_w_ada[0], v_w_ada[0])]

    def small_step(w_, g_, m_, v_, nm):
        shp = w_.shape
        two_d = (int(math.prod(shp[:-1])), shp[-1])
        d_, m2_, v2_ = _adamw_small(w_.reshape(two_d), g_.reshape(two_d), m_.reshape(two_d), v_.reshape(two_d),
                                    "adamw_" + nm)
        return [g_, d_.reshape(shp), m2_.reshape(shp), v2_.reshape(shp)]

    res = {
        "rel_bias": small_step(rel_bias, g_rel, m_rel_bias, v_rel_bias, "rel_bias"),
        "w_ada": ada_out,
        "b_ada": small_step(b_ada, g_b_ada, m_b_ada, v_b_ada, "b_ada"),
        "pre_norm_g": small_step(pre_norm_g, g_pre, m_pre_norm_g, v_pre_norm_g, "pre_norm_g"),
        "post_norm_g": small_step(post_norm_g, g_post, m_post_norm_g, v_post_norm_g, "post_norm_g"),
        "sinks": small_step(sinks, g_sinks, m_sinks, v_sinks, "sinks"),
        "gmlp_ln_g": small_step(gmlp_ln_g, g_lg, m_gmlp_ln_g, v_gmlp_ln_g, "gmlp_ln_g"),
        "gmlp_ln_b": small_step(gmlp_ln_b, g_lb, m_gmlp_ln_b, v_gmlp_ln_b, "gmlp_ln_b"),
        "gmlp_w_s": small_step(gmlp_w_s, g_ws, m_gmlp_w_s, v_gmlp_w_s, "gmlp_w_s"),
        "gmlp_b_s": small_step(gmlp_b_s, g_bs, m_gmlp_b_s, v_gmlp_b_s, "gmlp_b_s"),
    }
    res.update(big)
    order = ["rel_bias", "w_ada", "b_ada", "pre_norm_g", "post_norm_g", "w_ffn1_in", "w_ffn1_out", "w_in", "sinks",
             "gmlp_ln_g", "gmlp_ln_b", "gmlp_w_s", "gmlp_b_s", "w_br_attn", "w_br_gmlp", "w_out", "w_ffn2_in",
             "w_ffn2_out"]
    outs = [loss, grad_x[None]]
    for k in range(4):
        outs += [res[nm][k] for nm in order]
    return tuple(outs)
```

```python
import functools
import math

import jax
import jax.numpy as jnp
from jax import lax
from jax.experimental import pallas as pl
from jax.experimental.pallas import tpu as pltpu

F32 = jnp.float32
BF = jnp.bfloat16

N_DEV = 8
D = 1024
D_FF = 2816
FS = D_FF // 4
N_HEADS = 8
N_KV = 2
GROUP = 4
HD = 64
BLK = 128
Q_W = 512
KV_W = 128
G_W = 512
QKV_W = Q_W + 2 * KV_W
ZG_OFF = QKV_W
GATE_OFF = ZG_OFF + 2 * G_W
IN_W = GATE_OFF + 2 * D
N_BUCKETS = 32
MAX_DISTANCE = 128
EPS = 1e-6
NEG = -1e30
SCALE = HD ** -0.5
ADA_W = 9 * D // N_DEV

ADAM_LR = 0.001
ADAM_B1 = 0.9
ADAM_B2 = 0.999
ADAM_EPS = 1e-08
ADAM_WD = 0.01
ADAM_STEP = 10

MIB = 1024 * 1024
MESH = pl.DeviceIdType.MESH
HIGH = lax.Precision.HIGHEST


def _cp(n_grid, vmem_mib):
    return pltpu.CompilerParams(dimension_semantics=("arbitrary",) * n_grid,
                                vmem_limit_bytes=vmem_mib * MIB)


def _const(shape):
    return pl.BlockSpec(shape, lambda *_: (0,) * len(shape))


def _sds(shape, dtype):
    return jax.ShapeDtypeStruct(shape, dtype)


def _dot(a, b):
    return jnp.dot(a, b, preferred_element_type=F32)


def _dot_nt(a, b):
    return lax.dot_general(a, b, (((1,), (1,)), ((), ())), preferred_element_type=F32)


def _dot_tn(a, b):
    return lax.dot_general(a, b, (((0,), (0,)), ((), ())), preferred_element_type=F32)


def _rms_r(x):
    return lax.rsqrt(jnp.mean(x * x, axis=-1, keepdims=True) + EPS)


def _colsum(x):
    return jnp.sum(x, axis=0, keepdims=True)


def _prenorm(x, gp, sc, sh):
    return (x * _rms_r(x) * gp) * (1.0 + sc) + sh


def _prenorm_bwd(dn, x, gp, sc):
    r = _rms_r(x)
    xh = x * r
    t = dn * (1.0 + sc) * gp
    dx = r * (t - xh * jnp.mean(t * xh, axis=-1, keepdims=True))
    return dx, _colsum(dn), _colsum(dn * xh * gp), _colsum(dn * (1.0 + sc) * xh)


def _postnorm_bwd(dh, y, gate, gp, res):
    r = _rms_r(y)
    yh = y * r
    dyn = (res * gate) * dh
    t = dyn * gp
    dy = r * (t - yh * jnp.mean(t * yh, axis=-1, keepdims=True))
    return dy, _colsum(res * dh * yh * gp), _colsum(dyn * yh)


def _gelu(x):
    k = math.sqrt(2.0 / math.pi)
    return 0.5 * x * (1.0 + jnp.tanh(k * (x + 0.044715 * x * x * x)))


def _gelu_grad(x):
    k = math.sqrt(2.0 / math.pi)
    t = jnp.tanh(k * (x + 0.044715 * x * x * x))
    return 0.5 * (1.0 + t) + 0.5 * x * (1.0 - t * t) * (k * (1.0 + 3.0 * 0.044715 * x * x))


def _my_place():
    x, y, c = lax.axis_index("x"), lax.axis_index("y"), lax.axis_index("c")
    return x, y, c, 4 * x + 2 * y + c


def _peer(x, y, c, k):
    px = 1 - x if k & 4 else x
    py = 1 - y if k & 2 else y
    pc = 1 - c if k & 1 else c
    return (px, py, pc), 4 * px + 2 * py + pc


HBM_SPEC = pl.BlockSpec(memory_space=pltpu.HBM)
SEM_SPEC = pl.BlockSpec(memory_space=pltpu.SEMAPHORE)
EFFECT = pltpu.SideEffectType.DATAFLOW_SIDE_EFFECTING


def _gather_copies(srcs, lands, send, recv):
    x, y, c, me = _my_place()
    cps = []
    for t in range(len(srcs)):
        for k in range(1, N_DEV):
            peer, _ = _peer(x, y, c, k)
            cps.append(pltpu.make_async_remote_copy(
                src_ref=srcs[t], dst_ref=lands[t].at[me], send_sem=send.at[t * 7 + k - 1],
                recv_sem=recv.at[t * 7 + k - 1], device_id=peer, device_id_type=MESH))
    return cps


def _gather_start(shards, name):
    n = len(shards)

    def body(*refs):
        srcs, lands = refs[:n], refs[n:2 * n]
        send, recv = refs[2 * n], refs[2 * n + 1]
        token = refs[-1]
        for cp in _gather_copies(srcs, lands, send, recv):
            cp.start()
        token[...] = jnp.zeros_like(token)

    return pl.pallas_call(
        body, name=name,
        out_shape=(pltpu.SemaphoreType.DMA((7 * n,)), pltpu.SemaphoreType.DMA((7 * n,)),
                   *[pltpu.HBM(s.shape, s.dtype) for s in shards],
                   *[pltpu.HBM((N_DEV,) + s.shape, s.dtype) for s in shards],
                   _sds((1, D), F32)),
        in_specs=[HBM_SPEC] * (2 * n),
        out_specs=(SEM_SPEC, SEM_SPEC, *[HBM_SPEC] * (2 * n), pl.BlockSpec(memory_space=pltpu.VMEM)),
        input_output_aliases={t: 2 + t for t in range(2 * n)},
        compiler_params=pltpu.CompilerParams(has_side_effects=EFFECT),
    )(*[pltpu.with_memory_space_constraint(s, pltpu.HBM) for s in shards],
      *[pltpu.with_memory_space_constraint(lax.empty((N_DEV,) + s.shape, s.dtype), pltpu.HBM) for s in shards])


def _gather_wait(started, after, name):
    n = (len(started) - 3) // 2
    send, recv = started[0], started[1]
    thru = started[2:2 + 2 * n]

    def body(*refs):
        srcs, lands = refs[:n], refs[n:2 * n]
        for cp in _gather_copies(srcs, lands, refs[2 * n], refs[2 * n + 1]):
            cp.wait_send()
            cp.wait_recv()

    res = pl.pallas_call(
        body, name=name,
        out_shape=tuple(pltpu.HBM(a.shape, a.dtype) for a in thru),
        in_specs=[HBM_SPEC] * (2 * n) + [SEM_SPEC, SEM_SPEC, pl.BlockSpec(memory_space=pl.ANY)],
        out_specs=tuple([HBM_SPEC] * (2 * n)),
        input_output_aliases={t: t for t in range(2 * n)},
        compiler_params=pltpu.CompilerParams(has_side_effects=EFFECT),
    )(*thru, send, recv, after)
    return list(res[n:2 * n])


def _place_own(shards, lands, name):
    n = len(shards)

    def body(*refs):
        srcs, outs, sem = refs[:n], refs[2 * n:3 * n], refs[3 * n]
        _, _, _, me = _my_place()
        cps = [pltpu.make_async_copy(srcs[t], outs[t].at[me], sem.at[t]) for t in range(n)]
        for cp in cps:
            cp.start()
        for cp in cps:
            cp.wait()

    return pl.pallas_call(
        body, name=name,
        out_shape=[_sds(a.shape, a.dtype) for a in lands],
        in_specs=[pl.BlockSpec(memory_space=pl.ANY)] * (2 * n),
        out_specs=[pl.BlockSpec(memory_space=pl.ANY)] * n,
        input_output_aliases={n + t: t for t in range(n)},
        scratch_shapes=[pltpu.SemaphoreType.DMA((n,))],
    )(*shards, *lands)


def _exchange_slabs_hbm(grads, name):
    n = len(grads)

    def body(*refs):
        ins, outs = refs[:n], refs[n:2 * n]
        send, recv, loc = refs[2 * n:]
        x, y, c, me = _my_place()
        local = [pltpu.make_async_copy(ins[t].at[me], outs[t].at[me], loc.at[t]) for t in range(n)]
        for cp in local:
            cp.start()
        remote = []
        for t in range(n):
            for k in range(1, N_DEV):
                peer, peer_lin = _peer(x, y, c, k)
                cp = pltpu.make_async_remote_copy(
                    src_ref=ins[t].at[peer_lin], dst_ref=outs[t].at[me], send_sem=send.at[t * 7 + k - 1],
                    recv_sem=recv.at[t * 7 + k - 1], device_id=peer, device_id_type=MESH)
                cp.start()
                remote.append(cp)
        for cp in remote:
            cp.wait()
        for cp in local:
            cp.wait()

    return pl.pallas_call(
        body, name=name,
        out_shape=[_sds(g.shape, g.dtype) for g in grads],
        in_specs=[pl.BlockSpec(memory_space=pl.ANY)] * n,
        out_specs=[pl.BlockSpec(memory_space=pl.ANY)] * n,
        scratch_shapes=[pltpu.SemaphoreType.DMA((7 * n,)), pltpu.SemaphoreType.DMA((7 * n,)),
                        pltpu.SemaphoreType.DMA((n,))],
    )(*grads)


def _ada_forward(small8, w_ada, b_ada64):
    sw = small8.shape[1]

    def body(sm_ref, w_ref, b_ref, gath_ref, ada_ref, part_ref, send1, recv1, send2, recv2):
        x, y, c, me = _my_place()
        row_me = pl.multiple_of(me * 8, 8)
        gath_ref[pl.ds(row_me, 8), :] = sm_ref[...]
        first = []
        for k in range(1, N_DEV):
            peer, _ = _peer(x, y, c, k)
            cp = pltpu.make_async_remote_copy(
                src_ref=sm_ref, dst_ref=gath_ref.at[pl.ds(row_me, 8), :], send_sem=send1.at[k - 1],
                recv_sem=recv1.at[k - 1], device_id=peer, device_id_type=MESH)
            cp.start()
            first.append(cp)
        for cp in first:
            cp.wait()
        cs = gath_ref[:, 0:D]
        cs = cs * jax.nn.sigmoid(cs)
        part_ref[...] = jnp.dot(cs, w_ref[...], preferred_element_type=F32, precision=HIGH)
        ada_ref[pl.ds(row_me, 8), :] = part_ref[pl.ds(row_me, 8), :]
        second = []
        for k in range(1, N_DEV):
            peer, peer_lin = _peer(x, y, c, k)
            cp = pltpu.make_async_remote_copy(
                src_ref=part_ref.at[pl.ds(pl.multiple_of(peer_lin * 8, 8), 8), :],
                dst_ref=ada_ref.at[pl.ds(row_me, 8), :], send_sem=send2.at[k - 1],
                recv_sem=recv2.at[k - 1], device_id=peer, device_id_type=MESH)
            cp.start()
            second.append(cp)
        for cp in second:
            cp.wait()
        ada_ref[...] = ada_ref[...] + b_ref[...]

    vm = pl.BlockSpec(memory_space=pltpu.VMEM)
    return pl.pallas_call(
        body, name="ada_forward",
        out_shape=[_sds((8 * N_DEV, sw), F32), _sds((8 * N_DEV, ADA_W), F32)],
        in_specs=[vm, vm, vm], out_specs=[vm, vm],
        scratch_shapes=[pltpu.VMEM((8 * N_DEV, ADA_W), F32)] + [pltpu.SemaphoreType.DMA((7,))] * 4,
        compiler_params=pltpu.CompilerParams(vmem_limit_bytes=32 * MIB),
    )(small8, w_ada, b_ada64)


def _small_allreduce(pack):
    rows = pack.shape[0]

    def body(p_ref, sum_ref, gath_ref, send, recv):
        x, y, c, me = _my_place()
        gath_ref[me] = p_ref[...]
        cps = []
        for k in range(1, N_DEV):
            peer, _ = _peer(x, y, c, k)
            cp = pltpu.make_async_remote_copy(
                src_ref=p_ref, dst_ref=gath_ref.at[me], send_sem=send.at[k - 1],
                recv_sem=recv.at[k - 1], device_id=peer, device_id_type=MESH)
            cp.start()
            cps.append(cp)
        for cp in cps:
            cp.wait()
        acc = gath_ref[0]
        for j in range(1, N_DEV):
            acc = acc + gath_ref[j]
        sum_ref[...] = acc

    vm = pl.BlockSpec(memory_space=pltpu.VMEM)
    return pl.pallas_call(
        body, name="small_allreduce",
        out_shape=[_sds((rows, 128), F32), _sds((N_DEV, rows, 128), F32)],
        in_specs=[vm], out_specs=[vm, vm],
        scratch_shapes=[pltpu.SemaphoreType.DMA((7,)), pltpu.SemaphoreType.DMA((7,))],
        compiler_params=pltpu.CompilerParams(vmem_limit_bytes=40 * MIB),
    )(pack)


def _ffn_in(h, sh, sc, gp, w8, name):
    S = h.shape[0]
    R = min(1024, S)

    def body(h_ref, sh_ref, sc_ref, gp_ref, wg_ref, wu_ref, n_ref, g_ref, u_ref, a_ref, n_scr):
        @pl.when(pl.program_id(1) == 0)
        def _():
            nb = _prenorm(h_ref[...], gp_ref[...], sc_ref[...], sh_ref[...]).astype(BF)
            n_scr[...] = nb
            n_ref[...] = nb
        n = n_scr[...]
        g = _dot(n, wg_ref[...])
        u = _dot(n, wu_ref[...])
        g_ref[...] = g.astype(BF)
        u_ref[...] = u.astype(BF)
        a_ref[...] = (g * jax.nn.sigmoid(g) * u).astype(BF)

    vec = _const((1, D))
    blk = pl.BlockSpec((None, R, FS), lambda i, s: (s, i, 0))
    return pl.pallas_call(
        body, name=name, grid=(S // R, 4),
        out_shape=[_sds((S, D), BF)] + [_sds((4, S, FS), BF)] * 3,
        in_specs=[pl.BlockSpec((R, D), lambda i, s: (i, 0)), vec, vec, vec,
                  pl.BlockSpec((None, D, FS), lambda i, s: (s, 0, 0)),
                  pl.BlockSpec((None, D, FS), lambda i, s: (s + 4, 0, 0))],
        out_specs=[pl.BlockSpec((R, D), lambda i, s: (i, 0)), blk, blk, blk],
        scratch_shapes=[pltpu.VMEM((R, D), BF)],
        compiler_params=_cp(2, 48),
    )(h, sh, sc, gp, w8, w8)


def _ffn_out(a, w4, h, gate, gp, name):
    S = h.shape[0]
    R = min(512, S)

    def body(a_ref, w_ref, h_ref, gate_ref, gp_ref, hn_ref, y_ref):
        y = _dot(a_ref[0], w_ref[0])
        for s in range(1, 4):
            y = y + _dot(a_ref[s], w_ref[s])
        hn_ref[...] = h_ref[...] + (0.5 * gate_ref[...]) * (y * _rms_r(y) * gp_ref[...])
        y_ref[...] = y

    vec = _const((1, D))
    row = pl.BlockSpec((R, D), lambda i: (i, 0))
    return pl.pallas_call(
        body, name=name, grid=(S // R,),
        out_shape=[_sds((S, D), F32), _sds((S, D), F32)],
        in_specs=[pl.BlockSpec((4, R, FS), lambda i: (0, i, 0)), _const((4, FS, D)), row, vec, vec],
        out_specs=[row, row],
        compiler_params=_cp(1, 48),
    )(a, w4, h, gate, gp)


def _ffn_out_bwd(dh, y, g, u, w4, gate, gp, name):
    S = dh.shape[0]
    R = min(256, S)

    def body(dh_ref, y_ref, g_ref, u_ref, w_ref, gate_ref, gp_ref, dy_ref, dgu_ref, dgate_ref, dgp_ref):
        @pl.when(pl.program_id(0) == 0)
        def _():
            dgate_ref[...] = jnp.zeros_like(dgate_ref)
            dgp_ref[...] = jnp.zeros_like(dgp_ref)
        dy, dgate, dgp = _postnorm_bwd(dh_ref[...], y_ref[...], gate_ref[...], gp_ref[...], 0.5)
        dgate_ref[...] += dgate
        dgp_ref[...] += dgp
        dyb = dy.astype(BF)
        dy_ref[...] = dyb
        for s in range(4):
            da = _dot_nt(dyb, w_ref[s])
            gg = g_ref[s].astype(F32)
            uu = u_ref[s].astype(F32)
            sg = jax.nn.sigmoid(gg)
            dgu_ref[s] = (da * uu * (sg * (1.0 + gg * (1.0 - sg)))).astype(BF)
            dgu_ref[s + 4] = (da * (gg * sg)).astype(BF)

    vec = _const((1, D))
    row = pl.BlockSpec((R, D), lambda i: (i, 0))
    blk4 = pl.BlockSpec((4, R, FS), lambda i: (0, i, 0))
    return pl.pallas_call(
        body, name=name, grid=(S // R,),
        out_shape=[_sds((S, D), BF), _sds((8, S, FS), BF), _sds((1, D), F32), _sds((1, D), F32)],
        in_specs=[row, row, blk4, blk4, _const((4, FS, D)), vec, vec],
        out_specs=[row, pl.BlockSpec((8, R, FS), lambda i: (0, i, 0)), vec, vec],
        compiler_params=_cp(1, 56),
    )(dh, y, g, u, w4, gate, gp)


def _ffn_dn(dgu, w8, h, dh, sc, gp, name):
    S = h.shape[0]
    R = min(1024, S)

    def body(dgu_ref, w_ref, h_ref, dh_ref, sc_ref, gp_ref, out_ref, dsh_ref, dsc_ref, dgp_ref, acc):
        i, j = pl.program_id(0), pl.program_id(1)

        @pl.when((i == 0) & (j == 0))
        def _():
            dsh_ref[...] = jnp.zeros_like(dsh_ref)
            dsc_ref[...] = jnp.zeros_like(dsc_ref)
            dgp_ref[...] = jnp.zeros_like(dgp_ref)

        part = _dot_nt(dgu_ref[...], w_ref[...])

        @pl.when(j == 0)
        def _():
            acc[...] = part

        @pl.when(j > 0)
        def _():
            acc[...] += part

        @pl.when(j == N_DEV - 1)
        def _():
            dx, dsh, dsc, dgp = _prenorm_bwd(acc[...], h_ref[...], gp_ref[...], sc_ref[...])
            out_ref[...] = dh_ref[...] + dx
            dsh_ref[...] += dsh
            dsc_ref[...] += dsc
            dgp_ref[...] += dgp

    vec = _const((1, D))
    row = pl.BlockSpec((R, D), lambda i, j: (i, 0))
    return pl.pallas_call(
        body, name=name, grid=(S // R, N_DEV),
        out_shape=[_sds((S, D), F32)] + [_sds((1, D), F32)] * 3,
        in_specs=[pl.BlockSpec((None, R, FS), lambda i, j: (j, i, 0)),
                  pl.BlockSpec((None, D, FS), lambda i, j: (j, 0, 0)), row, row, vec, vec],
        out_specs=[row, vec, vec, vec],
        scratch_shapes=[pltpu.VMEM((R, D), F32)],
        compiler_params=_cp(2, 48),
    )(dgu, w8, h, dh, sc, gp)


def _tn_matmul(a, b, name):
    a3 = a if a.ndim == 3 else a[None]
    b3 = b if b.ndim == 3 else b[None]
    GA, S, M = a3.shape
    GB, _, N = b3.shape
    ts = min(512, S)
    nk = S // ts

    def body(a_ref, b_ref, o_ref, acc):
        k = pl.program_id(2)
        part = _dot_tn(a_ref[...], b_ref[...])

        @pl.when(k == 0)
        def _():
            acc[...] = part

        @pl.when(k > 0)
        def _():
            acc[...] += part

        @pl.when(k == nk - 1)
        def _():
            o_ref[...] = acc[...].astype(BF)

    return pl.pallas_call(
        body, name=name, grid=(GA, GB, nk),
        out_shape=_sds((GA, GB, M, N), BF),
        in_specs=[pl.BlockSpec((None, ts, M), lambda ga, gb, k: (ga, k, 0)),
                  pl.BlockSpec((None, ts, N), lambda ga, gb, k: (gb, k, 0))],
        out_specs=pl.BlockSpec((None, None, M, N), lambda ga, gb, k: (ga, gb, 0, 0)),
        scratch_shapes=[pltpu.VMEM((M, N), F32)],
        compiler_params=_cp(3, 48),
    )(a3, b3)


def _mix_in(h, sh, sc, gp, w):
    S = h.shape[0]
    R = min(512, S)

    def body(h_ref, sh_ref, sc_ref, gp_ref, w_ref, n_ref, qkv_ref, zg_ref, gates_ref):
        nb = _prenorm(h_ref[...], gp_ref[...], sc_ref[...], sh_ref[...]).astype(BF)
        n_ref[...] = nb
        qkv_ref[...] = _dot(nb, w_ref[:, 0:ZG_OFF]).astype(BF)
        zg_ref[...] = _dot(nb, w_ref[:, ZG_OFF:GATE_OFF]).astype(BF)
        gates_ref[...] = jax.nn.sigmoid(_dot(nb, w_ref[:, GATE_OFF:IN_W])).astype(BF)

    vec = _const((1, D))
    rows = lambda w_: pl.BlockSpec((R, w_), lambda i: (i, 0))
    return pl.pallas_call(
        body, name="mix_in", grid=(S // R,),
        out_shape=[_sds((S, D), BF), _sds((S, QKV_W), BF), _sds((S, 2 * G_W), BF), _sds((S, 2 * D), BF)],
        in_specs=[rows(D), vec, vec, vec, _const((D, IN_W))],
        out_specs=[rows(D), rows(QKV_W), rows(2 * G_W), rows(2 * D)],
        compiler_params=_cp(1, 48),
    )(h, sh, sc, gp, w)


def _bias_table(rel_bias, bucket):
    def body(rel_ref, bk_ref, out_ref):
        bk = bk_ref[...]
        qi = lax.broadcasted_iota(jnp.int32, (BLK, 2 * BLK), 0)
        kj = lax.broadcasted_iota(jnp.int32, (BLK, 2 * BLK), 1)
        dist = qi + BLK - kj
        window = (dist >= 0) & (dist < BLK)
        for h in range(N_HEADS):
            acc = jnp.zeros((BLK, 2 * BLK), F32)
            for b in range(N_BUCKETS):
                acc = jnp.where(bk == b, rel_ref[b, h], acc)
            out_ref[h // GROUP, pl.ds((h % GROUP) * BLK, BLK), :] = jnp.where(window, acc, NEG)

    return pl.pallas_call(
        body, name="bias_table",
        out_shape=_sds((N_KV, GROUP * BLK, 2 * BLK), F32),
        in_specs=[pl.BlockSpec(memory_space=pltpu.SMEM), pl.BlockSpec(memory_space=pltpu.VMEM)],
        out_specs=pl.BlockSpec(memory_space=pltpu.VMEM),
    )(rel_bias, bucket)


def _attn_scores(q, kvc, kvp, bias_ref, sink_ref, blk, kh):
    k2 = jnp.concatenate([kvp[:, kh * HD:(kh + 1) * HD], kvc[:, kh * HD:(kh + 1) * HD]], axis=0)
    v2 = jnp.concatenate([kvp[:, KV_W + kh * HD:KV_W + (kh + 1) * HD],
                          kvc[:, KV_W + kh * HD:KV_W + (kh + 1) * HD]], axis=0)
    q4 = jnp.concatenate([q[:, (kh * GROUP + g) * HD:(kh * GROUP + g + 1) * HD] for g in range(GROUP)], axis=0)
    s = _dot_nt(q4, k2) * SCALE + bias_ref[kh]
    col = lax.broadcasted_iota(jnp.int32, (GROUP * BLK, 2 * BLK), 1)
    s = jnp.where((col >= BLK) | (blk > 0), s, NEG)
    rowg = lax.broadcasted_iota(jnp.int32, (GROUP * BLK, 1), 0) // BLK
    sink = jnp.zeros((GROUP * BLK, 1), F32)
    for g in range(GROUP):
        sink = jnp.where(rowg == g, sink_ref[kh * GROUP + g], sink)
    return q4, k2, v2, s, sink


def _attn_fwd(qkv, bias, sinks):
    S = qkv.shape[0]
    nb = S // BLK

    def body(sink_ref, q_ref, kvc_ref, kvp_ref, bias_ref, o_ref):
        blk = pl.program_id(0)
        q, kvc, kvp = q_ref[...], kvc_ref[...], kvp_ref[...]
        outs = []
        for kh in range(N_KV):
            q4, k2, v2, s, sink = _attn_scores(q, kvc, kvp, bias_ref, sink_ref, blk, kh)
            m = jnp.maximum(jnp.max(s, axis=1, keepdims=True), sink)
            p = jnp.exp(s - m)
            denom = jnp.sum(p, axis=1, keepdims=True) + jnp.exp(sink - m)
            o4 = _dot((p / denom).astype(BF), v2)
            outs += [o4[g * BLK:(g + 1) * BLK] for g in range(GROUP)]
        o_ref[...] = jnp.concatenate(outs, axis=1).astype(BF)

    return pl.pallas_call(
        body, name="attn_fwd", grid=(nb,),
        out_shape=_sds((S, Q_W), BF),
        in_specs=[pl.BlockSpec(memory_space=pltpu.SMEM),
                  pl.BlockSpec((BLK, Q_W), lambda i: (i, 0)),
                  pl.BlockSpec((BLK, 2 * KV_W), lambda i: (i, 2)),
                  pl.BlockSpec((BLK, 2 * KV_W), lambda i: (jnp.maximum(i - 1, 0), 2)),
                  _const((N_KV, GROUP * BLK, 2 * BLK))],
        out_specs=pl.BlockSpec((BLK, Q_W), lambda i: (i, 0)),
        compiler_params=_cp(1, 32),
    )(sinks, qkv, qkv, qkv, bias)


def _attn_bwd(qkv, bias, sinks, do):
    S = qkv.shape[0]
    nb = S // BLK

    def body(sink_ref, q_ref, kvc_ref, kvp_ref, bias_ref, do_ref, dq_ref, dkv_ref, dbias_ref, dsink_ref, carry):
        i = pl.program_id(0)
        blk = nb - 1 - i

        @pl.when(i == 0)
        def _():
            carry[...] = jnp.zeros_like(carry)
            dbias_ref[...] = jnp.zeros_like(dbias_ref)
            dsink_ref[...] = jnp.zeros_like(dsink_ref)

        q, kvc, kvp, do_ = q_ref[...], kvc_ref[...], kvp_ref[...], do_ref[...]
        dqs, dk_cur, dv_cur, dk_prev, dv_prev = [], [], [], [], []
        for kh in range(N_KV):
            q4, k2, v2, s, sink = _attn_scores(q, kvc, kvp, bias_ref, sink_ref, blk, kh)
            m = jnp.maximum(jnp.max(s, axis=1, keepdims=True), sink)
            p = jnp.exp(s - m)
            denom = jnp.sum(p, axis=1, keepdims=True) + jnp.exp(sink - m)
            prob = p / denom
            p_sink = jnp.exp(sink - m) / denom
            pb = prob.astype(BF)
            do4 = jnp.concatenate(
                [do_[:, (kh * GROUP + g) * HD:(kh * GROUP + g + 1) * HD] for g in range(GROUP)], axis=0)
            dp = _dot_nt(do4, v2)
            o4 = _dot(pb, v2)
            delta = jnp.sum(do4.astype(F32) * o4, axis=1, keepdims=True)
            ds = prob * (dp - delta)
            dbias_ref[kh] += ds
            sink_term = p_sink * delta
            for g in range(GROUP):
                h = kh * GROUP + g
                val = -jnp.sum(sink_term[g * BLK:(g + 1) * BLK], axis=0, keepdims=True)
                dsink_ref[pl.ds(h, 1), :] += jnp.broadcast_to(val, (1, 128))
            dsb = ds.astype(BF)
            dq4 = _dot(dsb, k2) * SCALE
            dk2 = _dot_tn(dsb, q4) * SCALE
            dv2 = _dot_tn(pb, do4)
            dqs += [dq4[g * BLK:(g + 1) * BLK] for g in range(GROUP)]
            dk_prev.append(dk2[0:BLK])
            dk_cur.append(dk2[BLK:2 * BLK])
            dv_prev.append(dv2[0:BLK])
            dv_cur.append(dv2[BLK:2 * BLK])
        dq_ref[...] = jnp.concatenate(dqs, axis=1).astype(BF)
        dkv_ref[...] = (jnp.concatenate(dk_cur + dv_cur, axis=1) + carry[...]).astype(BF)
        carry[...] = jnp.concatenate(dk_prev + dv_prev, axis=1)

    return pl.pallas_call(
        body, name="attn_bwd", grid=(nb,),
        out_shape=[_sds((S, Q_W), BF), _sds((S, 2 * KV_W), BF),
                   _sds((N_KV, GROUP * BLK, 2 * BLK), F32), _sds((N_HEADS, 128), F32)],
        in_specs=[pl.BlockSpec(memory_space=pltpu.SMEM),
                  pl.BlockSpec((BLK, Q_W), lambda i: (nb - 1 - i, 0)),
                  pl.BlockSpec((BLK, 2 * KV_W), lambda i: (nb - 1 - i, 2)),
                  pl.BlockSpec((BLK, 2 * KV_W), lambda i: (jnp.maximum(nb - 2 - i, 0), 2)),
                  _const((N_KV, GROUP * BLK, 2 * BLK)),
                  pl.BlockSpec((BLK, Q_W), lambda i: (nb - 1 - i, 0))],
        out_specs=[pl.BlockSpec((BLK, Q_W), lambda i: (nb - 1 - i, 0)),
                   pl.BlockSpec((BLK, 2 * KV_W), lambda i: (nb - 1 - i, 0)),
                   _const((N_KV, GROUP * BLK, 2 * BLK)), _const((N_HEADS, 128))],
        scratch_shapes=[pltpu.VMEM((BLK, 2 * KV_W), F32)],
        compiler_params=_cp(1, 32),
    )(sinks, qkv, qkv, qkv, bias, do)


def _rel_bias_grad(dbias, bucket):
    def body(db_ref, bk_ref, out_ref):
        bk = bk_ref[...]
        lane = lax.broadcasted_iota(jnp.int32, (1, 128), 1)
        for h in range(N_HEADS):
            d = db_ref[h // GROUP, pl.ds((h % GROUP) * BLK, BLK), :]
            row = jnp.zeros((1, 128), F32)
            for b in range(N_BUCKETS):
                tot = jnp.sum(jnp.sum(jnp.where(bk == b, d, 0.0), axis=1, keepdims=True), axis=0, keepdims=True)
                row = jnp.where(lane == b, tot, row)
            out_ref[pl.ds(h, 1), :] = row

    vm = pl.BlockSpec(memory_space=pltpu.VMEM)
    return pl.pallas_call(body, name="rel_bias_grad", out_shape=_sds((N_HEADS, 128), F32),
                          in_specs=[vm, vm], out_specs=vm)(dbias, bucket)


def _gmlp_parts(zg_ref, lg_ref, lb_ref):
    z = zg_ref[...].astype(F32)
    ge = _gelu(z)
    u, vg = ge[:, 0:G_W], ge[:, G_W:2 * G_W]
    mu = jnp.mean(vg, axis=-1, keepdims=True)
    xc = vg - mu
    rstd = lax.rsqrt(jnp.mean(xc * xc, axis=-1, keepdims=True) + EPS)
    xh = xc * rstd
    return z, u, xh, rstd, xh * lg_ref[...] + lb_ref[...]


def _causal_weights(ws_ref, wc):
    t = lax.broadcasted_iota(jnp.int32, (BLK, BLK), 0)
    s = lax.broadcasted_iota(jnp.int32, (BLK, BLK), 1)
    for g in range(N_HEADS):
        wc[g] = jnp.where(s <= t, ws_ref[g], 0.0).astype(BF)


def _spatial(vb, wc, bst_ref, p, low):
    xp = vb[:, p * 128:(p + 1) * 128]
    s0 = _dot(wc[2 * p], xp) + bst_ref[:, 2 * p:2 * p + 1]
    s1 = _dot(wc[2 * p + 1], xp) + bst_ref[:, 2 * p + 1:2 * p + 2]
    return xp, jnp.where(low, s0, s1)


def _gmlp_fwd(zg, lg, lb, ws, bst):
    S = zg.shape[0]

    def body(zg_ref, lg_ref, lb_ref, ws_ref, bst_ref, o_ref, wc):
        @pl.when(pl.program_id(0) == 0)
        def _():
            _causal_weights(ws_ref, wc)
        _, u, _, _, vln = _gmlp_parts(zg_ref, lg_ref, lb_ref)
        vb = vln.astype(BF)
        low = lax.broadcasted_iota(jnp.int32, (BLK, 128), 1) < HD
        for p in range(4):
            _, sp = _spatial(vb, wc, bst_ref, p, low)
            o_ref[:, p * 128:(p + 1) * 128] = (u[:, p * 128:(p + 1) * 128] * sp).astype(BF)

    return pl.pallas_call(
        body, name="gmlp_fwd", grid=(S // BLK,),
        out_shape=_sds((S, G_W), BF),
        in_specs=[pl.BlockSpec((BLK, 2 * G_W), lambda i: (i, 0)), _const((1, G_W)), _const((1, G_W)),
                  _const((N_HEADS, BLK, BLK)), _const((BLK, N_HEADS))],
        out_specs=pl.BlockSpec((BLK, G_W), lambda i: (i, 0)),
        scratch_shapes=[pltpu.VMEM((N_HEADS, BLK, BLK), BF)],
        compiler_params=_cp(1, 32),
    )(zg, lg, lb, ws, bst)


def _gmlp_bwd(zg, d_out, lg, lb, ws, bst):
    S = zg.shape[0]
    nb = S // BLK

    def body(zg_ref, d_ref, lg_ref, lb_ref, ws_ref, bst_ref, dzg_ref, dws_ref, dbs_ref, dlg_ref, dlb_ref, wc, dbacc):
        i = pl.program_id(0)

        @pl.when(i == 0)
        def _():
            _causal_weights(ws_ref, wc)
            dws_ref[...] = jnp.zeros_like(dws_ref)
            dlg_ref[...] = jnp.zeros_like(dlg_ref)
            dlb_ref[...] = jnp.zeros_like(dlb_ref)
            dbacc[...] = jnp.zeros_like(dbacc)

        z, u, xh, rstd, vln = _gmlp_parts(zg_ref, lg_ref, lb_ref)
        vb = vln.astype(BF)
        d = d_ref[...].astype(F32)
        low = lax.broadcasted_iota(jnp.int32, (BLK, 128), 1) < HD
        du_parts, dvln_parts = [], []
        for p in range(4):
            xp, sp = _spatial(vb, wc, bst_ref, p, low)
            dp = d[:, p * 128:(p + 1) * 128]
            du_parts.append(dp * sp)
            dsp = dp * u[:, p * 128:(p + 1) * 128]
            dbacc[:, p * 128:(p + 1) * 128] += dsp
            d0 = jnp.where(low, dsp, 0.0).astype(BF)
            d1 = jnp.where(low, 0.0, dsp).astype(BF)
            dws_ref[2 * p] += _dot_nt(d0, xp)
            dws_ref[2 * p + 1] += _dot_nt(d1, xp)
            dvln_parts.append(_dot_tn(wc[2 * p], d0) + _dot_tn(wc[2 * p + 1], d1))
        dvln = jnp.concatenate(dvln_parts, axis=1)
        dlg_ref[...] += _colsum(dvln * xh)
        dlb_ref[...] += _colsum(dvln)
        dxh = dvln * lg_ref[...]
        dvg = rstd * (dxh - jnp.mean(dxh, axis=-1, keepdims=True)
                      - xh * jnp.mean(dxh * xh, axis=-1, keepdims=True))
        dge = jnp.concatenate(du_parts + [dvg], axis=1)
        dzg_ref[...] = (dge * _gelu_grad(z)).astype(BF)

        @pl.when(i == nb - 1)
        def _():
            t = lax.broadcasted_iota(jnp.int32, (BLK, BLK), 0)
            s = lax.broadcasted_iota(jnp.int32, (BLK, BLK), 1)
            for g in range(N_HEADS):
                dws_ref[g] = jnp.where(s <= t, dws_ref[g], 0.0)
            grp = lax.broadcasted_iota(jnp.int32, (N_HEADS, G_W), 0)
            lane = lax.broadcasted_iota(jnp.int32, (N_HEADS, G_W), 1) // HD
            pick = jnp.where(grp == lane, 1.0, 0.0).astype(F32)
            dbs_ref[...] = lax.dot_general(pick, dbacc[...], (((1,), (1,)), ((), ())),
                                           preferred_element_type=F32, precision=HIGH)

    return pl.pallas_call(
        body, name="gmlp_bwd", grid=(nb,),
        out_shape=[_sds((S, 2 * G_W), BF), _sds((N_HEADS, BLK, BLK), F32), _sds((N_HEADS, BLK), F32),
                   _sds((1, G_W), F32), _sds((1, G_W), F32)],
        in_specs=[pl.BlockSpec((BLK, 2 * G_W), lambda i: (i, 0)), pl.BlockSpec((BLK, G_W), lambda i: (i, 0)),
                  _const((1, G_W)), _const((1, G_W)), _const((N_HEADS, BLK, BLK)), _const((BLK, N_HEADS))],
        out_specs=[pl.BlockSpec((BLK, 2 * G_W), lambda i: (i, 0)), _const((N_HEADS, BLK, BLK)),
                   _const((N_HEADS, BLK)), _const((1, G_W)), _const((1, G_W))],
        scratch_shapes=[pltpu.VMEM((N_HEADS, BLK, BLK), BF), pltpu.VMEM((BLK, G_W), F32)],
        compiler_params=_cp(1, 32),
    )(zg, d_out, lg, lb, ws, bst)


def _mix_out(o, gm, gates, h, wa, wg, wo, gate, gp):
    S = h.shape[0]
    R = min(512, S)

    def body(o_ref, gm_ref, gates_ref, h_ref, wa_ref, wg_ref, wo_ref, gate_ref, gp_ref,
             ya_ref, yg_ref, ym_ref, y_ref, hn_ref):
        ya = _dot(o_ref[...], wa_ref[...])
        yg = _dot(gm_ref[...], wg_ref[...])
        ya_ref[...] = ya.astype(BF)
        yg_ref[...] = yg.astype(BF)
        ym = (gates_ref[:, 0:D].astype(F32) * ya + gates_ref[:, D:2 * D].astype(F32) * yg).astype(BF)
        ym_ref[...] = ym
        y = _dot(ym, wo_ref[...])
        y_ref[...] = y
        hn_ref[...] = h_ref[...] + gate_ref[...] * (y * _rms_r(y) * gp_ref[...])

    vec = _const((1, D))
    rows = lambda w_: pl.BlockSpec((R, w_), lambda i: (i, 0))
    return pl.pallas_call(
        body, name="mix_out", grid=(S // R,),
        out_shape=[_sds((S, D), BF)] * 3 + [_sds((S, D), F32)] * 2,
        in_specs=[rows(Q_W), rows(G_W), rows(2 * D), rows(D), _const((Q_W, D)), _const((G_W, D)),
                  _const((D, D)), vec, vec],
        out_specs=[rows(D)] * 5,
        compiler_params=_cp(1, 48),
    )(o, gm, gates, h, wa, wg, wo, gate, gp)


def _mix_out_bwd(dh, y, ya, yg, gates, wa, wg, wo, gate, gp):
    S = dh.shape[0]
    R = min(256, S)

    def body(dh_ref, y_ref, ya_ref, yg_ref, gates_ref, wa_ref, wg_ref, wo_ref, gate_ref, gp_ref,
             dy_ref, dya_ref, dyg_ref, dz_ref, do_ref, dgm_ref, dgate_ref, dgp_ref):
        @pl.when(pl.program_id(0) == 0)
        def _():
            dgate_ref[...] = jnp.zeros_like(dgate_ref)
            dgp_ref[...] = jnp.zeros_like(dgp_ref)
        dy, dgate, dgp = _postnorm_bwd(dh_ref[...], y_ref[...], gate_ref[...], gp_ref[...], 1.0)
        dgate_ref[...] += dgate
        dgp_ref[...] += dgp
        dyb = dy.astype(BF)
        dy_ref[...] = dyb
        dym = _dot_nt(dyb, wo_ref[...])
        ga = gates_ref[:, 0:D].astype(F32)
        gg = gates_ref[:, D:2 * D].astype(F32)
        dya = (dym * ga).astype(BF)
        dyg = (dym * gg).astype(BF)
        dya_ref[...] = dya
        dyg_ref[...] = dyg
        dz_ref[:, 0:D] = (dym * ya_ref[...].astype(F32) * (ga * (1.0 - ga))).astype(BF)
        dz_ref[:, D:2 * D] = (dym * yg_ref[...].astype(F32) * (gg * (1.0 - gg))).astype(BF)
        do_ref[...] = _dot_nt(dya, wa_ref[...]).astype(BF)
        dgm_ref[...] = _dot_nt(dyg, wg_ref[...]).astype(BF)

    vec = _const((1, D))
    rows = lambda w_: pl.BlockSpec((R, w_), lambda i: (i, 0))
    return pl.pallas_call(
        body, name="mix_out_bwd", grid=(S // R,),
        out_shape=[_sds((S, D), BF)] * 3 + [_sds((S, 2 * D), BF), _sds((S, Q_W), BF), _sds((S, G_W), BF),
                                             _sds((1, D), F32), _sds((1, D), F32)],
        in_specs=[rows(D), rows(D), rows(D), rows(D), rows(2 * D), _const((Q_W, D)), _const((G_W, D)),
                  _const((D, D)), vec, vec],
        out_specs=[rows(D)] * 3 + [rows(2 * D), rows(Q_W), rows(G_W), vec, vec],
        compiler_params=_cp(1, 48),
    )(dh, y, ya, yg, gates, wa, wg, wo, gate, gp)


def _mix_dn(dq, dkv, dzg, dzgate, w, h, dh, sc, gp):
    S = h.shape[0]
    R = min(512, S)

    def body(dq_ref, dkv_ref, dzg_ref, dzt_ref, w_ref, h_ref, dh_ref, sc_ref, gp_ref,
             out_ref, dsh_ref, dsc_ref, dgp_ref):
        @pl.when(pl.program_id(0) == 0)
        def _():
            dsh_ref[...] = jnp.zeros_like(dsh_ref)
            dsc_ref[...] = jnp.zeros_like(dsc_ref)
            dgp_ref[...] = jnp.zeros_like(dgp_ref)
        dn = _dot_nt(dq_ref[...], w_ref[:, 0:Q_W])
        dn = dn + _dot_nt(dkv_ref[...], w_ref[:, Q_W:QKV_W])
        dn = dn + _dot_nt(dzg_ref[...], w_ref[:, ZG_OFF:GATE_OFF])
        dn = dn + _dot_nt(dzt_ref[...], w_ref[:, GATE_OFF:IN_W])
        dx, dsh, dsc, dgp = _prenorm_bwd(dn, h_ref[...], gp_ref[...], sc_ref[...])
        out_ref[...] = dh_ref[...] + dx
        dsh_ref[...] += dsh
        dsc_ref[...] += dsc
        dgp_ref[...] += dgp

    vec = _const((1, D))
    rows = lambda w_: pl.BlockSpec((R, w_), lambda i: (i, 0))
    return pl.pallas_call(
        body, name="mix_dn", grid=(S // R,),
        out_shape=[_sds((S, D), F32)] + [_sds((1, D), F32)] * 3,
        in_specs=[rows(Q_W), rows(2 * KV_W), rows(2 * G_W), rows(2 * D), _const((D, IN_W)),
                  rows(D), rows(D), vec, vec],
        out_specs=[rows(D), vec, vec, vec],
        compiler_params=_cp(1, 48),
    )(dq, dkv, dzg, dzgate, w, h, dh, sc, gp)


def _loss_head(h, target):
    S = h.shape[0]
    R = min(1024, S)

    def body(h_ref, t_ref, tot_ref, dh_ref):
        @pl.when(pl.program_id(0) == 0)
        def _():
            tot_ref[...] = jnp.zeros_like(tot_ref)
        e = h_ref[...] - t_ref[...]
        dh_ref[...] = e * (1.0 / D)
        tot_ref[...] += jnp.sum(jnp.sum(e * e, axis=1, keepdims=True), axis=0, keepdims=True)

    row = pl.BlockSpec((R, D), lambda i: (i, 0))
    return pl.pallas_call(
        body, name="loss_head", grid=(S // R,),
        out_shape=[_sds((1, 1), F32), _sds((S, D), F32)],
        in_specs=[row, row], out_specs=[_const((1, 1)), row],
        compiler_params=_cp(1, 40),
    )(h, target)


def _adamw_math(w, g, m, v):
    m2 = ADAM_B1 * m + (1.0 - ADAM_B1) * g
    v2 = ADAM_B2 * v + (1.0 - ADAM_B2) * (g * g)
    m_hat = m2 / (1.0 - ADAM_B1 ** ADAM_STEP)
    v_hat = v2 / (1.0 - ADAM_B2 ** ADAM_STEP)
    delta = -ADAM_LR * (m_hat / (jnp.sqrt(v_hat) + ADAM_EPS) + ADAM_WD * w)
    return delta, m2, v2


def _row_tile(rows, cols):
    best = None
    for t in range(16, rows + 1, 16):
        if rows % t == 0 and t * cols <= 256 * 1024:
            best = t
    return best if best is not None else rows


def _adamw_sharded(landing, w, m, v, name):
    r, c = w.shape
    tr = _row_tile(r, c)

    def body(l_ref, w_ref, m_ref, v_ref, g_ref, d_ref, m2_ref, v2_ref):
        g = l_ref[0].astype(F32)
        for j in range(1, N_DEV):
            g = g + l_ref[j].astype(F32)
        delta, m2, v2 = _adamw_math(w_ref[...], g, m_ref[...], v_ref[...])
        g_ref[...] = g
        d_ref[...] = delta
        m2_ref[...] = m2
        v2_ref[...] = v2

    row = pl.BlockSpec((tr, c), lambda i: (i, 0))
    return pl.pallas_call(
        body, name=name, grid=(r // tr,),
        out_shape=[_sds((r, c), F32)] * 4,
        in_specs=[pl.BlockSpec((N_DEV, tr, c), lambda i: (0, i, 0)), row, row, row],
        out_specs=[row] * 4,
        compiler_params=_cp(1, 48),
    )(landing, w, m, v)


def _adamw_small(w, g, m, v, name):
    def body(w_ref, g_ref, m_ref, v_ref, d_ref, m2_ref, v2_ref):
        delta, m2, v2 = _adamw_math(w_ref[...], g_ref[...], m_ref[...], v_ref[...])
        d_ref[...] = delta
        m2_ref[...] = m2
        v2_ref[...] = v2

    vm = pl.BlockSpec(memory_space=pltpu.VMEM)
    return pl.pallas_call(body, name=name, out_shape=[_sds(w.shape, F32)] * 3,
                          in_specs=[vm] * 4, out_specs=[vm] * 3)(w, g, m, v)


def _w_ada_update(c64, d_ada, w, m, v):
    tr = 256

    def body(c_ref, d_ref, w_ref, m_ref, v_ref, g_ref, dl_ref, m2_ref, v2_ref):
        i = pl.program_id(0)
        g = jnp.zeros((tr, ADA_W), F32)
        for j in range(N_DEV):
            cj = c_ref[pl.ds(8 * j, 8), :]
            sj = cj * jax.nn.sigmoid(cj)
            dj = jnp.broadcast_to(d_ref[pl.ds(j, 1), :], (8, ADA_W))
            g = g + lax.dot_general(sj, dj, (((0,), (0,)), ((), ())), preferred_element_type=F32,
                                    precision=HIGH) * 0.125
        delta, m2, v2 = _adamw_math(w_ref[...], g, m_ref[...], v_ref[...])
        g_ref[...] = g
        dl_ref[...] = delta
        m2_ref[...] = m2
        v2_ref[...] = v2

    row = pl.BlockSpec((tr, ADA_W), lambda i: (i, 0))
    return pl.pallas_call(
        body, name="w_ada_update", grid=(D // tr,),
        out_shape=[_sds((D, ADA_W), F32)] * 4,
        in_specs=[pl.BlockSpec((8 * N_DEV, tr), lambda i: (0, i)), _const((N_DEV, ADA_W)), row, row, row],
        out_specs=[row] * 4,
        compiler_params=_cp(1, 40),
    )(c64, d_ada, w, m, v)


def _t5_bucket():
    qi = jnp.arange(BLK, dtype=jnp.int32)[:, None]
    kj = jnp.arange(2 * BLK, dtype=jnp.int32)[None, :]
    dist = jnp.maximum(qi + BLK - kj, 0)
    max_exact = N_BUCKETS // 2
    d_f = jnp.maximum(dist, max_exact).astype(F32)
    large = max_exact + (jnp.log(d_f / max_exact) / math.log(MAX_DISTANCE / max_exact)
                         * (N_BUCKETS - max_exact)).astype(jnp.int32)
    large = jnp.minimum(large, N_BUCKETS - 1)
    return jnp.where(dist < max_exact, dist, large)


def _slabs_of_columns(w):
    r, c8 = w.shape
    return jnp.transpose(w.reshape(r, N_DEV, c8 // N_DEV), (1, 0, 2))


def _columns_of_slabs(w8):
    _, r, c = w8.shape
    return jnp.transpose(w8, (1, 0, 2)).reshape(r, N_DEV * c)


def kernel(x, c, rel_bias, w_ada, b_ada, pre_norm_g, post_norm_g, w_ffn1_in, w_ffn1_out, w_in, sinks, gmlp_ln_g, gmlp_ln_b, gmlp_w_s, gmlp_b_s, w_br_attn, w_br_gmlp, w_out, w_ffn2_in, w_ffn2_out, loss_target, m_rel_bias, m_w_ada, m_b_ada, m_pre_norm_g, m_post_norm_g, m_w_ffn1_in, m_w_ffn1_out, m_w_in, m_sinks, m_gmlp_ln_g, m_gmlp_ln_b, m_gmlp_w_s, m_gmlp_b_s, m_w_br_attn, m_w_br_gmlp, m_w_out, m_w_ffn2_in, m_w_ffn2_out, v_rel_bias, v_w_ada, v_b_ada, v_pre_norm_g, v_post_norm_g, v_w_ffn1_in, v_w_ffn1_out, v_w_in, v_sinks, v_gmlp_ln_g, v_gmlp_ln_b, v_gmlp_w_s, v_gmlp_b_s, v_w_br_attn, v_w_br_gmlp, v_w_out, v_w_ffn2_in, v_w_ffn2_out):
    me = 4 * lax.axis_index("x") + 2 * lax.axis_index("y") + lax.axis_index("c")
    x0 = x[0]
    target = loss_target[0]

    shards = [w_ffn1_in[0], w_ffn1_out[0], w_in[0], w_br_attn[0], w_br_gmlp[0], w_out[0],
              w_ffn2_in[0], w_ffn2_out[0]]
    shards_bf = [s.astype(BF) for s in shards]
    groups = [shards_bf[0:1], shards_bf[1:2], shards_bf[2:6], shards_bf[6:7], shards_bf[7:8]]
    started = [_gather_start(g, "gather_start_%d" % i) for i, g in enumerate(groups)]
    start_token = started[0][-1]
    for st in started[1:]:
        start_token = start_token + st[-1]

    def gathered(i, after):
        lands = _gather_wait(started[i], after, "gather_wait_%d" % i)
        return _place_own(groups[i], lands, "gather_own_%d" % i)

    small = jnp.concatenate([c[0], pre_norm_g[0].reshape(-1), post_norm_g[0].reshape(-1)])
    small8 = jnp.broadcast_to(small[None, :], (8, small.shape[0]))
    b_ada64 = jnp.repeat(b_ada.reshape(N_DEV, ADA_W), 8, axis=0)
    gath, ada64 = _ada_forward(small8, w_ada[0], b_ada64)
    gath8 = gath[::8]
    ada = ada64[::8].reshape(9, D)
    sh1, sc1, g1, sh2, sc2, g2, sh3, sc3, g3 = [ada[k:k + 1] for k in range(9)]
    sh1 = sh1 + start_token
    gains = gath8[:, D:].reshape(N_DEV, 2, 3, 128)
    pre_g = jnp.transpose(gains[:, 0], (1, 0, 2)).reshape(3, D)
    post_g = jnp.transpose(gains[:, 1], (1, 0, 2)).reshape(3, D)
    pre = [pre_g[k:k + 1] for k in range(3)]
    post = [post_g[k:k + 1] for k in range(3)]

    bucket = _t5_bucket()
    bias = _bias_table(rel_bias, bucket)
    sinks8 = sinks[0]
    lg, lb = gmlp_ln_g, gmlp_ln_b
    ws = gmlp_w_s[0]
    bst = jnp.transpose(gmlp_b_s[0])

    (wf1_in,) = gathered(0, sh1)
    n1, fg1, fu1, fa1 = _ffn_in(x0, sh1, sc1, pre[0], wf1_in, "ffn1_in")
    wf1_out = gathered(1, n1)[0].reshape(4, FS, D)
    h1, y1 = _ffn_out(fa1, wf1_out, x0, g1, post[0], "ffn1_out")
    mix_w = gathered(2, h1)
    w_in_full = _columns_of_slabs(mix_w[0])
    w_bra = _columns_of_slabs(mix_w[1])
    w_brg = _columns_of_slabs(mix_w[2])
    w_out_full = mix_w[3].reshape(D, D)
    n2, qkv, zg, gates = _mix_in(h1, sh2, sc2, pre[1], w_in_full)
    att = _attn_fwd(qkv, bias, sinks8)
    gm = _gmlp_fwd(zg, lg, lb, ws, bst)
    ya, yg, ymix, y2, h2 = _mix_out(att, gm, gates, h1, w_bra, w_brg, w_out_full, g2, post[1])
    (wf2_in,) = gathered(3, h2)
    n3, fg3, fu3, fa3 = _ffn_in(h2, sh3, sc3, pre[2], wf2_in, "ffn2_in")
    wf2_out = gathered(4, n3)[0].reshape(4, FS, D)
    h3, y3 = _ffn_out(fa3, wf2_out, h2, g3, post[2], "ffn2_out")
    sq, dh3 = _loss_head(h3, target)
    loss = lax.psum(0.5 * sq[0, 0] / D, ("x", "y", "c"))

    dy3, dgu3, d_g3, d_post2 = _ffn_out_bwd(dh3, y3, fg3, fu3, wf2_out, g3, post[2], "ffn2_out_bwd")
    gw_f2_out = _tn_matmul(fa3, dy3, "ffn2_out_wgrad").reshape(N_DEV, D_FF // N_DEV, D)
    dh2, d_sh3, d_sc3, d_pre2 = _ffn_dn(dgu3, wf2_in, h2, dh3, sc3, pre[2], "ffn2_dn")
    gw_f2_in = _tn_matmul(n3, dgu3, "ffn2_in_wgrad").reshape(N_DEV, D, FS)

    dy2, dya, dyg, dzgate, d_att, d_gm, d_g2, d_post1 = _mix_out_bwd(
        dh2, y2, ya, yg, gates, w_bra, w_brg, w_out_full, g2, post[1])
    gw_out = _tn_matmul(ymix, dy2, "w_out_wgrad").reshape(N_DEV, D // N_DEV, D)
    gw_bra = _slabs_of_columns(_tn_matmul(att, dya, "w_br_attn_wgrad").reshape(Q_W, D))
    gw_brg = _slabs_of_columns(_tn_matmul(gm, dyg, "w_br_gmlp_wgrad").reshape(G_W, D))
    dq, dkv, dbias, dsink = _attn_bwd(qkv, bias, sinks8, d_att)
    dzg, d_ws, d_bs, d_lg, d_lb = _gmlp_bwd(zg, d_gm, lg, lb, ws, bst)
    dh1, d_sh2, d_sc2, d_pre1 = _mix_dn(dq, dkv, dzg, dzgate, w_in_full, h1, dh2, sc2, pre[1])
    gw_in = _slabs_of_columns(jnp.concatenate(
        [_tn_matmul(n2, dq, "w_in_q_wgrad").reshape(D, Q_W),
         _tn_matmul(n2, dkv, "w_in_kv_wgrad").reshape(D, 2 * KV_W),
         _tn_matmul(n2, dzg, "w_in_zg_wgrad").reshape(D, 2 * G_W),
         _tn_matmul(n2, dzgate, "w_in_gate_wgrad").reshape(D, 2 * D)], axis=1))

    dy1, dgu1, d_g1, d_post0 = _ffn_out_bwd(dh1, y1, fg1, fu1, wf1_out, g1, post[0], "ffn1_out_bwd")
    gw_f1_out = _tn_matmul(fa1, dy1, "ffn1_out_wgrad").reshape(N_DEV, D_FF // N_DEV, D)
    grad_x, d_sh1, d_sc1, d_pre0 = _ffn_dn(dgu1, wf1_in, x0, dh1, sc1, pre[0], "ffn1_dn")
    gw_f1_in = _tn_matmul(n1, dgu1, "ffn1_in_wgrad").reshape(N_DEV, D, FS)

    landing = _exchange_slabs_hbm(
        [gw_f1_in, gw_f1_out, gw_in, gw_bra, gw_brg, gw_out, gw_f2_in, gw_f2_out], "wgrad_exchange")
    moments = [(m_w_ffn1_in, v_w_ffn1_in), (m_w_ffn1_out, v_w_ffn1_out), (m_w_in, v_w_in),
               (m_w_br_attn, v_w_br_attn), (m_w_br_gmlp, v_w_br_gmlp), (m_w_out, v_w_out),
               (m_w_ffn2_in, v_w_ffn2_in), (m_w_ffn2_out, v_w_ffn2_out)]
    names = ["w_ffn1_in", "w_ffn1_out", "w_in", "w_br_attn", "w_br_gmlp", "w_out", "w_ffn2_in", "w_ffn2_out"]
    big = {}
    for nm, land, w_, (m_, v_) in zip(names, landing, shards, moments):
        big[nm] = [a[None] for a in _adamw_sharded(land, w_, m_[0], v_[0], "adamw_" + nm)]

    d_rel = _rel_bias_grad(dbias, bucket)
    d_ada = jnp.concatenate([v_.reshape(8, 128) for v_ in
                             (d_sh1, d_sc1, d_g1, d_sh2, d_sc2, d_g2, d_sh3, d_sc3, d_g3)], axis=0)
    d_pre = jnp.concatenate([d_pre0, d_pre1, d_pre2], axis=0)
    d_post = jnp.concatenate([d_post0, d_post1, d_post2], axis=0)
    pack = jnp.concatenate([
        d_ada,
        _slabs_of_columns(d_pre).reshape(24, 128),
        _slabs_of_columns(d_post).reshape(24, 128),
        jnp.concatenate([d_lg.reshape(4, 128), d_lb.reshape(4, 128)], axis=0),
        d_bs, d_rel, dsink,
        d_ws.reshape(N_HEADS * BLK, BLK)], axis=0)
    tot, every = _small_allreduce(pack)

    g_b_ada = tot[0:72].reshape(1, 9 * D)
    g_pre = lax.dynamic_slice_in_dim(tot[72:96], 3 * me, 3, axis=0)[None]
    g_post = lax.dynamic_slice_in_dim(tot[96:120], 3 * me, 3, axis=0)[None]
    g_lg = tot[120:124].reshape(1, G_W)
    g_lb = tot[124:128].reshape(1, G_W)
    g_bs = tot[128:136][None]
    g_rel = jnp.transpose(tot[136:144, 0:N_BUCKETS])
    g_sinks = tot[144:152, 0][None]
    g_ws = tot[152:1176].reshape(1, N_HEADS, BLK, BLK)

    d_ada_mine = lax.dynamic_slice_in_dim(every[:, 0:72].reshape(N_DEV, N_DEV, ADA_W), me, 1, axis=1)[:, 0]
    c64 = jnp.repeat(gath8[:, 0:D], 8, axis=0)
    ada_out = [a[None] for a in _w_ada_update(c64, d_ada_mine, w_ada[0], m_w_ada[0], v_w_ada[0])]

    def small_step(w_, g_, m_, v_, nm):
        shp = w_.shape
        two_d = (int(math.prod(shp[:-1])), shp[-1])
        d_, m2_, v2_ = _adamw_small(w_.reshape(two_d), g_.reshape(two_d), m_.reshape(two_d), v_.reshape(two_d),
                                    "adamw_" + nm)
        return [g_, d_.reshape(shp), m2_.reshape(shp), v2_.reshape(shp)]

    res = {
        "rel_bias": small_step(rel_bias, g_rel, m_rel_bias, v_rel_bias, "rel_bias"),
        "w_ada": ada_out,
        "b_ada": small_step(b_ada, g_b_ada, m_b_ada, v_b_ada, "b_ada"),
        "pre_norm_g": small_step(pre_norm_g, g_pre, m_pre_norm_g, v_pre_norm_g, "pre_norm_g"),
        "post_norm_g": small_step(post_norm_g, g_post, m_post_norm_g, v_post_norm_g, "post_norm_g"),
        "sinks": small_step(sinks, g_sinks, m_sinks, v_sinks, "sinks"),
        "gmlp_ln_g": small_step(gmlp_ln_g, g_lg, m_gmlp_ln_g, v_gmlp_ln_g, "gmlp_ln_g"),
        "gmlp_ln_b": small_step(gmlp_ln_b, g_lb, m_gmlp_ln_b, v_gmlp_ln_b, "gmlp_ln_b"),
        "gmlp_w_s": small_step(gmlp_w_s, g_ws, m_gmlp_w_s, v_gmlp_w_s, "gmlp_w_s"),
        "gmlp_b_s": small_step(gmlp_b_s, g_bs, m_gmlp_b_s, v_gmlp_b_s, "gmlp_b_s"),
    }
    res.update(big)
    order = ["rel_bias", "w_ada", "b_ada", "pre_norm_g", "post_norm_g", "w_ffn1_in", "w_ffn1_out", "w_in", "sinks",
             "gmlp_ln_g", "gmlp_ln_b", "gmlp_w_s", "gmlp_b_s", "w_br_attn", "w_br_gmlp", "w_out", "w_ffn2_in",
             "w_ffn2_out"]
    outs = [loss, grad_x[None]]
    for k in range(4):
        outs += [res[nm][k] for nm in order]
    return tuple(outs)
```

```python
import functools
import math

import jax
import jax.numpy as jnp
from jax import lax
from jax.experimental import pallas as pl
from jax.experimental.pallas import tpu as pltpu

F32 = jnp.float32
BF = jnp.bfloat16

N_DEV = 8
D = 1024
D_FF = 2816
FS = D_FF // 4
N_HEADS = 8
N_KV = 2
GROUP = 4
HD = 64
BLK = 128
Q_W = 512
KV_W = 128
G_W = 512
QKV_W = Q_W + 2 * KV_W
ZG_OFF = QKV_W
GATE_OFF = ZG_OFF + 2 * G_W
IN_W = GATE_OFF + 2 * D
N_BUCKETS = 32
MAX_DISTANCE = 128
EPS = 1e-6
NEG = -1e30
SCALE = HD ** -0.5
ADA_W = 9 * D // N_DEV

ADAM_LR = 0.001
ADAM_B1 = 0.9
ADAM_B2 = 0.999
ADAM_EPS = 1e-08
ADAM_WD = 0.01
ADAM_STEP = 10

MIB = 1024 * 1024
MESH = pl.DeviceIdType.MESH
HIGH = lax.Precision.HIGHEST


def _cp(n_grid, vmem_mib):
    return pltpu.CompilerParams(dimension_semantics=("arbitrary",) * n_grid,
                                vmem_limit_bytes=vmem_mib * MIB)


def _const(shape):
    return pl.BlockSpec(shape, lambda *_: (0,) * len(shape))


def _sds(shape, dtype):
    return jax.ShapeDtypeStruct(shape, dtype)


def _dot(a, b):
    return jnp.dot(a, b, preferred_element_type=F32)


def _dot_nt(a, b):
    return lax.dot_general(a, b, (((1,), (1,)), ((), ())), preferred_element_type=F32)


def _dot_tn(a, b):
    return lax.dot_general(a, b, (((0,), (0,)), ((), ())), preferred_element_type=F32)


def _rms_r(x):
    return lax.rsqrt(jnp.mean(x * x, axis=-1, keepdims=True) + EPS)


def _colsum(x):
    return jnp.sum(x, axis=0, keepdims=True)


def _prenorm(x, gp, sc, sh):
    return (x * _rms_r(x) * gp) * (1.0 + sc) + sh


def _prenorm_bwd(dn, x, gp, sc):
    r = _rms_r(x)
    xh = x * r
    t = dn * (1.0 + sc) * gp
    dx = r * (t - xh * jnp.mean(t * xh, axis=-1, keepdims=True))
    return dx, _colsum(dn), _colsum(dn * xh * gp), _colsum(dn * (1.0 + sc) * xh)


def _postnorm_bwd(dh, y, gate, gp, res):
    r = _rms_r(y)
    yh = y * r
    dyn = (res * gate) * dh
    t = dyn * gp
    dy = r * (t - yh * jnp.mean(t * yh, axis=-1, keepdims=True))
    return dy, _colsum(res * dh * yh * gp), _colsum(dyn * yh)


def _gelu(x):
    k = math.sqrt(2.0 / math.pi)
    return 0.5 * x * (1.0 + jnp.tanh(k * (x + 0.044715 * x * x * x)))


def _gelu_grad(x):
    k = math.sqrt(2.0 / math.pi)
    t = jnp.tanh(k * (x + 0.044715 * x * x * x))
    return 0.5 * (1.0 + t) + 0.5 * x * (1.0 - t * t) * (k * (1.0 + 3.0 * 0.044715 * x * x))


def _my_place():
    x, y, c = lax.axis_index("x"), lax.axis_index("y"), lax.axis_index("c")
    return x, y, c, 4 * x + 2 * y + c


def _peer(x, y, c, k):
    px = 1 - x if k & 4 else x
    py = 1 - y if k & 2 else y
    pc = 1 - c if k & 1 else c
    return (px, py, pc), 4 * px + 2 * py + pc


HBM_SPEC = pl.BlockSpec(memory_space=pltpu.HBM)
SEM_SPEC = pl.BlockSpec(memory_space=pltpu.SEMAPHORE)
EFFECT = pltpu.SideEffectType.DATAFLOW_SIDE_EFFECTING


def _slab_copies(exchange, srcs, lands, send, recv, loc):
    x, y, c, me = _my_place()
    remote, local = [], []
    for t in range(len(srcs)):
        for k in range(1, N_DEV):
            peer, peer_lin = _peer(x, y, c, k)
            remote.append(pltpu.make_async_remote_copy(
                src_ref=srcs[t].at[peer_lin] if exchange else srcs[t], dst_ref=lands[t].at[me],
                send_sem=send.at[t * 7 + k - 1], recv_sem=recv.at[t * 7 + k - 1],
                device_id=peer, device_id_type=MESH))
        local.append(pltpu.make_async_copy(srcs[t].at[me] if exchange else srcs[t], lands[t].at[me], loc.at[t]))
    return remote, local


def _slabs_start(exchange, arrays, after, name):
    n = len(arrays)
    land_shapes = [a.shape if exchange else (N_DEV,) + a.shape for a in arrays]

    def body(*refs):
        srcs, lands = refs[:n], refs[n:2 * n]
        send, recv, loc = refs[2 * n + 1:2 * n + 4]
        remote, local = _slab_copies(exchange, srcs, lands, send, recv, loc)
        for cp in remote + local:
            cp.start()
        refs[-1][...] = jnp.zeros_like(refs[-1])

    return pl.pallas_call(
        body, name=name,
        out_shape=(pltpu.SemaphoreType.DMA((7 * n,)), pltpu.SemaphoreType.DMA((7 * n,)),
                   pltpu.SemaphoreType.DMA((n,)),
                   *[pltpu.HBM(a.shape, a.dtype) for a in arrays],
                   *[pltpu.HBM(s, a.dtype) for s, a in zip(land_shapes, arrays)],
                   _sds((1, D), F32)),
        in_specs=[HBM_SPEC] * (2 * n) + [pl.BlockSpec(memory_space=pl.ANY)],
        out_specs=(SEM_SPEC, SEM_SPEC, SEM_SPEC, *[HBM_SPEC] * (2 * n), pl.BlockSpec(memory_space=pltpu.VMEM)),
        input_output_aliases={t: 3 + t for t in range(2 * n)},
        compiler_params=pltpu.CompilerParams(has_side_effects=EFFECT),
    )(*[pltpu.with_memory_space_constraint(a, pltpu.HBM) for a in arrays],
      *[pltpu.with_memory_space_constraint(lax.empty(s, a.dtype), pltpu.HBM) for s, a in zip(land_shapes, arrays)],
      after)


def _slabs_wait(exchange, started, after, name):
    n = (len(started) - 4) // 2
    sems = started[0:3]
    thru = started[3:3 + 2 * n]

    def body(*refs):
        srcs, lands = refs[:n], refs[n:2 * n]
        remote, local = _slab_copies(exchange, srcs, lands, *refs[2 * n:2 * n + 3])
        for cp in remote:
            cp.wait_send()
            cp.wait_recv()
        for cp in local:
            cp.wait()

    res = pl.pallas_call(
        body, name=name,
        out_shape=tuple(pltpu.HBM(a.shape, a.dtype) for a in thru),
        in_specs=[HBM_SPEC] * (2 * n) + [SEM_SPEC] * 3 + [pl.BlockSpec(memory_space=pl.ANY)],
        out_specs=tuple([HBM_SPEC] * (2 * n)),
        input_output_aliases={t: t for t in range(2 * n)},
        compiler_params=pltpu.CompilerParams(has_side_effects=EFFECT),
    )(*thru, *sems, after)
    return list(res[n:2 * n])


def _ada_forward(small8, w_ada, b_ada64):
    sw = small8.shape[1]

    def body(sm_ref, w_ref, b_ref, gath_ref, ada_ref, part_ref, send1, recv1, send2, recv2):
        x, y, c, me = _my_place()
        row_me = pl.multiple_of(me * 8, 8)
        gath_ref[pl.ds(row_me, 8), :] = sm_ref[...]
        first = []
        for k in range(1, N_DEV):
            peer, _ = _peer(x, y, c, k)
            cp = pltpu.make_async_remote_copy(
                src_ref=sm_ref, dst_ref=gath_ref.at[pl.ds(row_me, 8), :], send_sem=send1.at[k - 1],
                recv_sem=recv1.at[k - 1], device_id=peer, device_id_type=MESH)
            cp.start()
            first.append(cp)
        for cp in first:
            cp.wait()
        cs = gath_ref[:, 0:D]
        cs = cs * jax.nn.sigmoid(cs)
        part_ref[...] = jnp.dot(cs, w_ref[...], preferred_element_type=F32, precision=HIGH)
        ada_ref[pl.ds(row_me, 8), :] = part_ref[pl.ds(row_me, 8), :]
        second = []
        for k in range(1, N_DEV):
            peer, peer_lin = _peer(x, y, c, k)
            cp = pltpu.make_async_remote_copy(
                src_ref=part_ref.at[pl.ds(pl.multiple_of(peer_lin * 8, 8), 8), :],
                dst_ref=ada_ref.at[pl.ds(row_me, 8), :], send_sem=send2.at[k - 1],
                recv_sem=recv2.at[k - 1], device_id=peer, device_id_type=MESH)
            cp.start()
            second.append(cp)
        for cp in second:
            cp.wait()
        ada_ref[...] = ada_ref[...] + b_ref[...]

    vm = pl.BlockSpec(memory_space=pltpu.VMEM)
    return pl.pallas_call(
        body, name="ada_forward",
        out_shape=[_sds((8 * N_DEV, sw), F32), _sds((8 * N_DEV, ADA_W), F32)],
        in_specs=[vm, vm, vm], out_specs=[vm, vm],
        scratch_shapes=[pltpu.VMEM((8 * N_DEV, ADA_W), F32)] + [pltpu.SemaphoreType.DMA((7,))] * 4,
        compiler_params=pltpu.CompilerParams(vmem_limit_bytes=32 * MIB),
    )(small8, w_ada, b_ada64)


def _small_allreduce(pack):
    rows = pack.shape[0]

    def body(p_ref, sum_ref, gath_ref, send, recv):
        x, y, c, me = _my_place()
        gath_ref[me] = p_ref[...]
        cps = []
        for k in range(1, N_DEV):
            peer, _ = _peer(x, y, c, k)
            cp = pltpu.make_async_remote_copy(
                src_ref=p_ref, dst_ref=gath_ref.at[me], send_sem=send.at[k - 1],
                recv_sem=recv.at[k - 1], device_id=peer, device_id_type=MESH)
            cp.start()
            cps.append(cp)
        for cp in cps:
            cp.wait()
        acc = gath_ref[0]
        for j in range(1, N_DEV):
            acc = acc + gath_ref[j]
        sum_ref[...] = acc

    vm = pl.BlockSpec(memory_space=pltpu.VMEM)
    return pl.pallas_call(
        body, name="small_allreduce",
        out_shape=[_sds((rows, 128), F32), _sds((N_DEV, rows, 128), F32)],
        in_specs=[vm], out_specs=[vm, vm],
        scratch_shapes=[pltpu.SemaphoreType.DMA((7,)), pltpu.SemaphoreType.DMA((7,))],
        compiler_params=pltpu.CompilerParams(vmem_limit_bytes=40 * MIB),
    )(pack)


def _ffn_in(h, sh, sc, gp, w8, name):
    S = h.shape[0]
    R = min(1024, S)

    def body(h_ref, sh_ref, sc_ref, gp_ref, wg_ref, wu_ref, n_ref, g_ref, u_ref, a_ref, n_scr):
        @pl.when(pl.program_id(1) == 0)
        def _():
            nb = _prenorm(h_ref[...], gp_ref[...], sc_ref[...], sh_ref[...]).astype(BF)
            n_scr[...] = nb
            n_ref[...] = nb
        n = n_scr[...]
        g = _dot(n, wg_ref[...])
        u = _dot(n, wu_ref[...])
        g_ref[...] = g.astype(BF)
        u_ref[...] = u.astype(BF)
        a_ref[...] = (g * jax.nn.sigmoid(g) * u).astype(BF)

    vec = _const((1, D))
    blk = pl.BlockSpec((None, R, FS), lambda i, s: (s, i, 0))
    return pl.pallas_call(
        body, name=name, grid=(S // R, 4),
        out_shape=[_sds((S, D), BF)] + [_sds((4, S, FS), BF)] * 3,
        in_specs=[pl.BlockSpec((R, D), lambda i, s: (i, 0)), vec, vec, vec,
                  pl.BlockSpec((None, D, FS), lambda i, s: (s, 0, 0)),
                  pl.BlockSpec((None, D, FS), lambda i, s: (s + 4, 0, 0))],
        out_specs=[pl.BlockSpec((R, D), lambda i, s: (i, 0)), blk, blk, blk],
        scratch_shapes=[pltpu.VMEM((R, D), BF)],
        compiler_params=_cp(2, 48),
    )(h, sh, sc, gp, w8, w8)


def _ffn_out(a, w4, h, gate, gp, name):
    S = h.shape[0]
    R = min(512, S)

    def body(a_ref, w_ref, h_ref, gate_ref, gp_ref, hn_ref, y_ref):
        y = _dot(a_ref[0], w_ref[0])
        for s in range(1, 4):
            y = y + _dot(a_ref[s], w_ref[s])
        hn_ref[...] = h_ref[...] + (0.5 * gate_ref[...]) * (y * _rms_r(y) * gp_ref[...])
        y_ref[...] = y

    vec = _const((1, D))
    row = pl.BlockSpec((R, D), lambda i: (i, 0))
    return pl.pallas_call(
        body, name=name, grid=(S // R,),
        out_shape=[_sds((S, D), F32), _sds((S, D), F32)],
        in_specs=[pl.BlockSpec((4, R, FS), lambda i: (0, i, 0)), _const((4, FS, D)), row, vec, vec],
        out_specs=[row, row],
        compiler_params=_cp(1, 48),
    )(a, w4, h, gate, gp)


def _ffn_out_bwd(dh, y, g, u, w4, gate, gp, name):
    S = dh.shape[0]
    R = min(256, S)

    def body(dh_ref, y_ref, g_ref, u_ref, w_ref, gate_ref, gp_ref, dy_ref, dgu_ref, dgate_ref, dgp_ref):
        @pl.when(pl.program_id(0) == 0)
        def _():
            dgate_ref[...] = jnp.zeros_like(dgate_ref)
            dgp_ref[...] = jnp.zeros_like(dgp_ref)
        dy, dgate, dgp = _postnorm_bwd(dh_ref[...], y_ref[...], gate_ref[...], gp_ref[...], 0.5)
        dgate_ref[...] += dgate
        dgp_ref[...] += dgp
        dyb = dy.astype(BF)
        dy_ref[...] = dyb
        for s in range(4):
            da = _dot_nt(dyb, w_ref[s])
            gg = g_ref[s].astype(F32)
            uu = u_ref[s].astype(F32)
            sg = jax.nn.sigmoid(gg)
            dgu_ref[s] = (da * uu * (sg * (1.0 + gg * (1.0 - sg)))).astype(BF)
            dgu_ref[s + 4] = (da * (gg * sg)).astype(BF)

    vec = _const((1, D))
    row = pl.BlockSpec((R, D), lambda i: (i, 0))
    blk4 = pl.BlockSpec((4, R, FS), lambda i: (0, i, 0))
    return pl.pallas_call(
        body, name=name, grid=(S // R,),
        out_shape=[_sds((S, D), BF), _sds((8, S, FS), BF), _sds((1, D), F32), _sds((1, D), F32)],
        in_specs=[row, row, blk4, blk4, _const((4, FS, D)), vec, vec],
        out_specs=[row, pl.BlockSpec((8, R, FS), lambda i: (0, i, 0)), vec, vec],
        compiler_params=_cp(1, 56),
    )(dh, y, g, u, w4, gate, gp)


def _ffn_dn(dgu, w8, h, dh, sc, gp, name):
    S = h.shape[0]
    R = min(1024, S)

    def body(dgu_ref, w_ref, h_ref, dh_ref, sc_ref, gp_ref, out_ref, dsh_ref, dsc_ref, dgp_ref, acc):
        i, j = pl.program_id(0), pl.program_id(1)

        @pl.when((i == 0) & (j == 0))
        def _():
            dsh_ref[...] = jnp.zeros_like(dsh_ref)
            dsc_ref[...] = jnp.zeros_like(dsc_ref)
            dgp_ref[...] = jnp.zeros_like(dgp_ref)

        part = _dot_nt(dgu_ref[...], w_ref[...])

        @pl.when(j == 0)
        def _():
            acc[...] = part

        @pl.when(j > 0)
        def _():
            acc[...] += part

        @pl.when(j == N_DEV - 1)
        def _():
            dx, dsh, dsc, dgp = _prenorm_bwd(acc[...], h_ref[...], gp_ref[...], sc_ref[...])
            out_ref[...] = dh_ref[...] + dx
            dsh_ref[...] += dsh
            dsc_ref[...] += dsc
            dgp_ref[...] += dgp

    vec = _const((1, D))
    row = pl.BlockSpec((R, D), lambda i, j: (i, 0))
    return pl.pallas_call(
        body, name=name, grid=(S // R, N_DEV),
        out_shape=[_sds((S, D), F32)] + [_sds((1, D), F32)] * 3,
        in_specs=[pl.BlockSpec((None, R, FS), lambda i, j: (j, i, 0)),
                  pl.BlockSpec((None, D, FS), lambda i, j: (j, 0, 0)), row, row, vec, vec],
        out_specs=[row, vec, vec, vec],
        scratch_shapes=[pltpu.VMEM((R, D), F32)],
        compiler_params=_cp(2, 48),
    )(dgu, w8, h, dh, sc, gp)


def _tn_matmul(a, b, name):
    a3 = a if a.ndim == 3 else a[None]
    b3 = b if b.ndim == 3 else b[None]
    GA, S, M = a3.shape
    GB, _, N = b3.shape
    ts = min(512, S)
    nk = S // ts

    def body(a_ref, b_ref, o_ref, acc):
        k = pl.program_id(2)
        part = _dot_tn(a_ref[...], b_ref[...])

        @pl.when(k == 0)
        def _():
            acc[...] = part

        @pl.when(k > 0)
        def _():
            acc[...] += part

        @pl.when(k == nk - 1)
        def _():
            o_ref[...] = acc[...].astype(BF)

    return pl.pallas_call(
        body, name=name, grid=(GA, GB, nk),
        out_shape=_sds((GA, GB, M, N), BF),
        in_specs=[pl.BlockSpec((None, ts, M), lambda ga, gb, k: (ga, k, 0)),
                  pl.BlockSpec((None, ts, N), lambda ga, gb, k: (gb, k, 0))],
        out_specs=pl.BlockSpec((None, None, M, N), lambda ga, gb, k: (ga, gb, 0, 0)),
        scratch_shapes=[pltpu.VMEM((M, N), F32)],
        compiler_params=_cp(3, 48),
    )(a3, b3)


def _mix_in(h, sh, sc, gp, w):
    S = h.shape[0]
    R = min(512, S)

    def body(h_ref, sh_ref, sc_ref, gp_ref, w_ref, n_ref, qkv_ref, zg_ref, gates_ref):
        nb = _prenorm(h_ref[...], gp_ref[...], sc_ref[...], sh_ref[...]).astype(BF)
        n_ref[...] = nb
        qkv_ref[...] = _dot(nb, w_ref[:, 0:ZG_OFF]).astype(BF)
        zg_ref[...] = _dot(nb, w_ref[:, ZG_OFF:GATE_OFF]).astype(BF)
        gates_ref[...] = jax.nn.sigmoid(_dot(nb, w_ref[:, GATE_OFF:IN_W])).astype(BF)

    vec = _const((1, D))
    rows = lambda w_: pl.BlockSpec((R, w_), lambda i: (i, 0))
    return pl.pallas_call(
        body, name="mix_in", grid=(S // R,),
        out_shape=[_sds((S, D), BF), _sds((S, QKV_W), BF), _sds((S, 2 * G_W), BF), _sds((S, 2 * D), BF)],
        in_specs=[rows(D), vec, vec, vec, _const((D, IN_W))],
        out_specs=[rows(D), rows(QKV_W), rows(2 * G_W), rows(2 * D)],
        compiler_params=_cp(1, 48),
    )(h, sh, sc, gp, w)


def _bias_table(rel_bias, bucket):
    def body(rel_ref, bk_ref, out_ref):
        bk = bk_ref[...]
        qi = lax.broadcasted_iota(jnp.int32, (BLK, 2 * BLK), 0)
        kj = lax.broadcasted_iota(jnp.int32, (BLK, 2 * BLK), 1)
        dist = qi + BLK - kj
        window = (dist >= 0) & (dist < BLK)
        for h in range(N_HEADS):
            acc = jnp.zeros((BLK, 2 * BLK), F32)
            for b in range(N_BUCKETS):
                acc = jnp.where(bk == b, rel_ref[b, h], acc)
            out_ref[h // GROUP, pl.ds((h % GROUP) * BLK, BLK), :] = jnp.where(window, acc, NEG)

    return pl.pallas_call(
        body, name="bias_table",
        out_shape=_sds((N_KV, GROUP * BLK, 2 * BLK), F32),
        in_specs=[pl.BlockSpec(memory_space=pltpu.SMEM), pl.BlockSpec(memory_space=pltpu.VMEM)],
        out_specs=pl.BlockSpec(memory_space=pltpu.VMEM),
    )(rel_bias, bucket)


def _attn_scores(q, kvc, kvp, bias_ref, sink_ref, blk, kh):
    k2 = jnp.concatenate([kvp[:, kh * HD:(kh + 1) * HD], kvc[:, kh * HD:(kh + 1) * HD]], axis=0)
    v2 = jnp.concatenate([kvp[:, KV_W + kh * HD:KV_W + (kh + 1) * HD],
                          kvc[:, KV_W + kh * HD:KV_W + (kh + 1) * HD]], axis=0)
    q4 = jnp.concatenate([q[:, (kh * GROUP + g) * HD:(kh * GROUP + g + 1) * HD] for g in range(GROUP)], axis=0)
    s = _dot_nt(q4, k2) * SCALE + bias_ref[kh]
    col = lax.broadcasted_iota(jnp.int32, (GROUP * BLK, 2 * BLK), 1)
    s = jnp.where((col >= BLK) | (blk > 0), s, NEG)
    rowg = lax.broadcasted_iota(jnp.int32, (GROUP * BLK, 1), 0) // BLK
    sink = jnp.zeros((GROUP * BLK, 1), F32)
    for g in range(GROUP):
        sink = jnp.where(rowg == g, sink_ref[kh * GROUP + g], sink)
    return q4, k2, v2, s, sink


def _attn_fwd(qkv, bias, sinks):
    S = qkv.shape[0]
    nb = S // BLK

    def body(sink_ref, q_ref, kvc_ref, kvp_ref, bias_ref, o_ref):
        blk = pl.program_id(0)
        q, kvc, kvp = q_ref[...], kvc_ref[...], kvp_ref[...]
        outs = []
        for kh in range(N_KV):
            q4, k2, v2, s, sink = _attn_scores(q, kvc, kvp, bias_ref, sink_ref, blk, kh)
            m = jnp.maximum(jnp.max(s, axis=1, keepdims=True), sink)
            p = jnp.exp(s - m)
            denom = jnp.sum(p, axis=1, keepdims=True) + jnp.exp(sink - m)
            o4 = _dot((p / denom).astype(BF), v2)
            outs += [o4[g * BLK:(g + 1) * BLK] for g in range(GROUP)]
        o_ref[...] = jnp.concatenate(outs, axis=1).astype(BF)

    return pl.pallas_call(
        body, name="attn_fwd", grid=(nb,),
        out_shape=_sds((S, Q_W), BF),
        in_specs=[pl.BlockSpec(memory_space=pltpu.SMEM),
                  pl.BlockSpec((BLK, Q_W), lambda i: (i, 0)),
                  pl.BlockSpec((BLK, 2 * KV_W), lambda i: (i, 2)),
                  pl.BlockSpec((BLK, 2 * KV_W), lambda i: (jnp.maximum(i - 1, 0), 2)),
                  _const((N_KV, GROUP * BLK, 2 * BLK))],
        out_specs=pl.BlockSpec((BLK, Q_W), lambda i: (i, 0)),
        compiler_params=_cp(1, 32),
    )(sinks, qkv, qkv, qkv, bias)


def _attn_bwd(qkv, bias, sinks, do):
    S = qkv.shape[0]
    nb = S // BLK

    def body(sink_ref, q_ref, kvc_ref, kvp_ref, bias_ref, do_ref, dq_ref, dkv_ref, dbias_ref, dsink_ref, carry):
        i = pl.program_id(0)
        blk = nb - 1 - i

        @pl.when(i == 0)
        def _():
            carry[...] = jnp.zeros_like(carry)
            dbias_ref[...] = jnp.zeros_like(dbias_ref)
            dsink_ref[...] = jnp.zeros_like(dsink_ref)

        q, kvc, kvp, do_ = q_ref[...], kvc_ref[...], kvp_ref[...], do_ref[...]
        dqs, dk_cur, dv_cur, dk_prev, dv_prev = [], [], [], [], []
        for kh in range(N_KV):
            q4, k2, v2, s, sink = _attn_scores(q, kvc, kvp, bias_ref, sink_ref, blk, kh)
            m = jnp.maximum(jnp.max(s, axis=1, keepdims=True), sink)
            p = jnp.exp(s - m)
            denom = jnp.sum(p, axis=1, keepdims=True) + jnp.exp(sink - m)
            prob = p / denom
            p_sink = jnp.exp(sink - m) / denom
            pb = prob.astype(BF)
            do4 = jnp.concatenate(
                [do_[:, (kh * GROUP + g) * HD:(kh * GROUP + g + 1) * HD] for g in range(GROUP)], axis=0)
            dp = _dot_nt(do4, v2)
            o4 = _dot(pb, v2)
            delta = jnp.sum(do4.astype(F32) * o4, axis=1, keepdims=True)
            ds = prob * (dp - delta)
            dbias_ref[kh] += ds
            sink_term = p_sink * delta
            for g in range(GROUP):
                h = kh * GROUP + g
                val = -jnp.sum(sink_term[g * BLK:(g + 1) * BLK], axis=0, keepdims=True)
                dsink_ref[pl.ds(h, 1), :] += jnp.broadcast_to(val, (1, 128))
            dsb = ds.astype(BF)
            dq4 = _dot(dsb, k2) * SCALE
            dk2 = _dot_tn(dsb, q4) * SCALE
            dv2 = _dot_tn(pb, do4)
            dqs += [dq4[g * BLK:(g + 1) * BLK] for g in range(GROUP)]
            dk_prev.append(dk2[0:BLK])
            dk_cur.append(dk2[BLK:2 * BLK])
            dv_prev.append(dv2[0:BLK])
            dv_cur.append(dv2[BLK:2 * BLK])
        dq_ref[...] = jnp.concatenate(dqs, axis=1).astype(BF)
        dkv_ref[...] = (jnp.concatenate(dk_cur + dv_cur, axis=1) + carry[...]).astype(BF)
        carry[...] = jnp.concatenate(dk_prev + dv_prev, axis=1)

    return pl.pallas_call(
        body, name="attn_bwd", grid=(nb,),
        out_shape=[_sds((S, Q_W), BF), _sds((S, 2 * KV_W), BF),
                   _sds((N_KV, GROUP * BLK, 2 * BLK), F32), _sds((N_HEADS, 128), F32)],
        in_specs=[pl.BlockSpec(memory_space=pltpu.SMEM),
                  pl.BlockSpec((BLK, Q_W), lambda i: (nb - 1 - i, 0)),
                  pl.BlockSpec((BLK, 2 * KV_W), lambda i: (nb - 1 - i, 2)),
                  pl.BlockSpec((BLK, 2 * KV_W), lambda i: (jnp.maximum(nb - 2 - i, 0), 2)),
                  _const((N_KV, GROUP * BLK, 2 * BLK)),
                  pl.BlockSpec((BLK, Q_W), lambda i: (nb - 1 - i, 0))],
        out_specs=[pl.BlockSpec((BLK, Q_W), lambda i: (nb - 1 - i, 0)),
                   pl.BlockSpec((BLK, 2 * KV_W), lambda i: (nb - 1 - i, 0)),
                   _const((N_KV, GROUP * BLK, 2 * BLK)), _const((N_HEADS, 128))],
        scratch_shapes=[pltpu.VMEM((BLK, 2 * KV_W), F32)],
        compiler_params=_cp(1, 32),
    )(sinks, qkv, qkv, qkv, bias, do)


def _rel_bias_grad(dbias, bucket):
    def body(db_ref, bk_ref, out_ref):
        bk = bk_ref[...]
        lane = lax.broadcasted_iota(jnp.int32, (1, 128), 1)
        for h in range(N_HEADS):
            d = db_ref[h // GROUP, pl.ds((h % GROUP) * BLK, BLK), :]
            row = jnp.zeros((1, 128), F32)
            for b in range(N_BUCKETS):
                tot = jnp.sum(jnp.sum(jnp.where(bk == b, d, 0.0), axis=1, keepdims=True), axis=0, keepdims=True)
                row = jnp.where(lane == b, tot, row)
            out_ref[pl.ds(h, 1), :] = row

    vm = pl.BlockSpec(memory_space=pltpu.VMEM)
    return pl.pallas_call(body, name="rel_bias_grad", out_shape=_sds((N_HEADS, 128), F32),
                          in_specs=[vm, vm], out_specs=vm)(dbias, bucket)


def _gmlp_parts(zg_ref, lg_ref, lb_ref):
    z = zg_ref[...].astype(F32)
    ge = _gelu(z)
    u, vg = ge[:, 0:G_W], ge[:, G_W:2 * G_W]
    mu = jnp.mean(vg, axis=-1, keepdims=True)
    xc = vg - mu
    rstd = lax.rsqrt(jnp.mean(xc * xc, axis=-1, keepdims=True) + EPS)
    xh = xc * rstd
    return z, u, xh, rstd, xh * lg_ref[...] + lb_ref[...]


def _causal_weights(ws_ref, wc):
    t = lax.broadcasted_iota(jnp.int32, (BLK, BLK), 0)
    s = lax.broadcasted_iota(jnp.int32, (BLK, BLK), 1)
    for g in range(N_HEADS):
        wc[g] = jnp.where(s <= t, ws_ref[g], 0.0).astype(BF)


def _spatial(vb, wc, bst_ref, p, low):
    xp = vb[:, p * 128:(p + 1) * 128]
    s0 = _dot(wc[2 * p], xp) + bst_ref[:, 2 * p:2 * p + 1]
    s1 = _dot(wc[2 * p + 1], xp) + bst_ref[:, 2 * p + 1:2 * p + 2]
    return xp, jnp.where(low, s0, s1)


def _gmlp_fwd(zg, lg, lb, ws, bst):
    S = zg.shape[0]

    def body(zg_ref, lg_ref, lb_ref, ws_ref, bst_ref, o_ref, wc):
        @pl.when(pl.program_id(0) == 0)
        def _():
            _causal_weights(ws_ref, wc)
        _, u, _, _, vln = _gmlp_parts(zg_ref, lg_ref, lb_ref)
        vb = vln.astype(BF)
        low = lax.broadcasted_iota(jnp.int32, (BLK, 128), 1) < HD
        for p in range(4):
            _, sp = _spatial(vb, wc, bst_ref, p, low)
            o_ref[:, p * 128:(p + 1) * 128] = (u[:, p * 128:(p + 1) * 128] * sp).astype(BF)

    return pl.pallas_call(
        body, name="gmlp_fwd", grid=(S // BLK,),
        out_shape=_sds((S, G_W), BF),
        in_specs=[pl.BlockSpec((BLK, 2 * G_W), lambda i: (i, 0)), _const((1, G_W)), _const((1, G_W)),
                  _const((N_HEADS, BLK, BLK)), _const((BLK, N_HEADS))],
        out_specs=pl.BlockSpec((BLK, G_W), lambda i: (i, 0)),
        scratch_shapes=[pltpu.VMEM((N_HEADS, BLK, BLK), BF)],
        compiler_params=_cp(1, 32),
    )(zg, lg, lb, ws, bst)


def _gmlp_bwd(zg, d_out, lg, lb, ws, bst):
    S = zg.shape[0]
    nb = S // BLK

    def body(zg_ref, d_ref, lg_ref, lb_ref, ws_ref, bst_ref, dzg_ref, dws_ref, dbs_ref, dlg_ref, dlb_ref, wc, dbacc):
        i = pl.program_id(0)

        @pl.when(i == 0)
        def _():
            _causal_weights(ws_ref, wc)
            dws_ref[...] = jnp.zeros_like(dws_ref)
            dlg_ref[...] = jnp.zeros_like(dlg_ref)
            dlb_ref[...] = jnp.zeros_like(dlb_ref)
            dbacc[...] = jnp.zeros_like(dbacc)

        z, u, xh, rstd, vln = _gmlp_parts(zg_ref, lg_ref, lb_ref)
        vb = vln.astype(BF)
        d = d_ref[...].astype(F32)
        low = lax.broadcasted_iota(jnp.int32, (BLK, 128), 1) < HD
        du_parts, dvln_parts = [], []
        for p in range(4):
            xp, sp = _spatial(vb, wc, bst_ref, p, low)
            dp = d[:, p * 128:(p + 1) * 128]
            du_parts.append(dp * sp)
            dsp = dp * u[:, p * 128:(p + 1) * 128]
            dbacc[:, p * 128:(p + 1) * 128] += dsp
            d0 = jnp.where(low, dsp, 0.0).astype(BF)
            d1 = jnp.where(low, 0.0, dsp).astype(BF)
            dws_ref[2 * p] += _dot_nt(d0, xp)
            dws_ref[2 * p + 1] += _dot_nt(d1, xp)
            dvln_parts.append(_dot_tn(wc[2 * p], d0) + _dot_tn(wc[2 * p + 1], d1))
        dvln = jnp.concatenate(dvln_parts, axis=1)
        dlg_ref[...] += _colsum(dvln * xh)
        dlb_ref[...] += _colsum(dvln)
        dxh = dvln * lg_ref[...]
        dvg = rstd * (dxh - jnp.mean(dxh, axis=-1, keepdims=True)
                      - xh * jnp.mean(dxh * xh, axis=-1, keepdims=True))
        dge = jnp.concatenate(du_parts + [dvg], axis=1)
        dzg_ref[...] = (dge * _gelu_grad(z)).astype(BF)

        @pl.when(i == nb - 1)
        def _():
            t = lax.broadcasted_iota(jnp.int32, (BLK, BLK), 0)
            s = lax.broadcasted_iota(jnp.int32, (BLK, BLK), 1)
            for g in range(N_HEADS):
                dws_ref[g] = jnp.where(s <= t, dws_ref[g], 0.0)
            grp = lax.broadcasted_iota(jnp.int32, (N_HEADS, G_W), 0)
            lane = lax.broadcasted_iota(jnp.int32, (N_HEADS, G_W), 1) // HD
            pick = jnp.where(grp == lane, 1.0, 0.0).astype(F32)
            dbs_ref[...] = lax.dot_general(pick, dbacc[...], (((1,), (1,)), ((), ())),
                                           preferred_element_type=F32, precision=HIGH)

    return pl.pallas_call(
        body, name="gmlp_bwd", grid=(nb,),
        out_shape=[_sds((S, 2 * G_W), BF), _sds((N_HEADS, BLK, BLK), F32), _sds((N_HEADS, BLK), F32),
                   _sds((1, G_W), F32), _sds((1, G_W), F32)],
        in_specs=[pl.BlockSpec((BLK, 2 * G_W), lambda i: (i, 0)), pl.BlockSpec((BLK, G_W), lambda i: (i, 0)),
                  _const((1, G_W)), _const((1, G_W)), _const((N_HEADS, BLK, BLK)), _const((BLK, N_HEADS))],
        out_specs=[pl.BlockSpec((BLK, 2 * G_W), lambda i: (i, 0)), _const((N_HEADS, BLK, BLK)),
                   _const((N_HEADS, BLK)), _const((1, G_W)), _const((1, G_W))],
        scratch_shapes=[pltpu.VMEM((N_HEADS, BLK, BLK), BF), pltpu.VMEM((BLK, G_W), F32)],
        compiler_params=_cp(1, 32),
    )(zg, d_out, lg, lb, ws, bst)


def _mix_out(o, gm, gates, h, wa, wg, wo, gate, gp):
    S = h.shape[0]
    R = min(512, S)

    def body(o_ref, gm_ref, gates_ref, h_ref, wa_ref, wg_ref, wo_ref, gate_ref, gp_ref,
             ya_ref, yg_ref, ym_ref, y_ref, hn_ref):
        ya = _dot(o_ref[...], wa_ref[...])
        yg = _dot(gm_ref[...], wg_ref[...])
        ya_ref[...] = ya.astype(BF)
        yg_ref[...] = yg.astype(BF)
        ym = (gates_ref[:, 0:D].astype(F32) * ya + gates_ref[:, D:2 * D].astype(F32) * yg).astype(BF)
        ym_ref[...] = ym
        y = _dot(ym, wo_ref[...])
        y_ref[...] = y
        hn_ref[...] = h_ref[...] + gate_ref[...] * (y * _rms_r(y) * gp_ref[...])

    vec = _const((1, D))
    rows = lambda w_: pl.BlockSpec((R, w_), lambda i: (i, 0))
    return pl.pallas_call(
        body, name="mix_out", grid=(S // R,),
        out_shape=[_sds((S, D), BF)] * 3 + [_sds((S, D), F32)] * 2,
        in_specs=[rows(Q_W), rows(G_W), rows(2 * D), rows(D), _const((Q_W, D)), _const((G_W, D)),
                  _const((D, D)), vec, vec],
        out_specs=[rows(D)] * 5,
        compiler_params=_cp(1, 48),
    )(o, gm, gates, h, wa, wg, wo, gate, gp)


def _mix_out_bwd(dh, y, ya, yg, gates, wa, wg, wo, gate, gp):
    S = dh.shape[0]
    R = min(256, S)

    def body(dh_ref, y_ref, ya_ref, yg_ref, gates_ref, wa_ref, wg_ref, wo_ref, gate_ref, gp_ref,
             dy_ref, dya_ref, dyg_ref, dz_ref, do_ref, dgm_ref, dgate_ref, dgp_ref):
        @pl.when(pl.program_id(0) == 0)
        def _():
            dgate_ref[...] = jnp.zeros_like(dgate_ref)
            dgp_ref[...] = jnp.zeros_like(dgp_ref)
        dy, dgate, dgp = _postnorm_bwd(dh_ref[...], y_ref[...], gate_ref[...], gp_ref[...], 1.0)
        dgate_ref[...] += dgate
        dgp_ref[...] += dgp
        dyb = dy.astype(BF)
        dy_ref[...] = dyb
        dym = _dot_nt(dyb, wo_ref[...])
        ga = gates_ref[:, 0:D].astype(F32)
        gg = gates_ref[:, D:2 * D].astype(F32)
        dya = (dym * ga).astype(BF)
        dyg = (dym * gg).astype(BF)
        dya_ref[...] = dya
        dyg_ref[...] = dyg
        dz_ref[:, 0:D] = (dym * ya_ref[...].astype(F32) * (ga * (1.0 - ga))).astype(BF)
        dz_ref[:, D:2 * D] = (dym * yg_ref[...].astype(F32) * (gg * (1.0 - gg))).astype(BF)
        do_ref[...] = _dot_nt(dya, wa_ref[...]).astype(BF)
        dgm_ref[...] = _dot_nt(dyg, wg_ref[...]).astype(BF)

    vec = _const((1, D))
    rows = lambda w_: pl.BlockSpec((R, w_), lambda i: (i, 0))
    return pl.pallas_call(
        body, name="mix_out_bwd", grid=(S // R,),
        out_shape=[_sds((S, D), BF)] * 3 + [_sds((S, 2 * D), BF), _sds((S, Q_W), BF), _sds((S, G_W), BF),
                                             _sds((1, D), F32), _sds((1, D), F32)],
        in_specs=[rows(D), rows(D), rows(D), rows(D), rows(2 * D), _const((Q_W, D)), _const((G_W, D)),
                  _const((D, D)), vec, vec],
        out_specs=[rows(D)] * 3 + [rows(2 * D), rows(Q_W), rows(G_W), vec, vec],
        compiler_params=_cp(1, 48),
    )(dh, y, ya, yg, gates, wa, wg, wo, gate, gp)


def _mix_dn(dq, dkv, dzg, dzgate, w, h, dh, sc, gp):
    S = h.shape[0]
    R = min(512, S)

    def body(dq_ref, dkv_ref, dzg_ref, dzt_ref, w_ref, h_ref, dh_ref, sc_ref, gp_ref,
             out_ref, dsh_ref, dsc_ref, dgp_ref):
        @pl.when(pl.program_id(0) == 0)
        def _():
            dsh_ref[...] = jnp.zeros_like(dsh_ref)
            dsc_ref[...] = jnp.zeros_like(dsc_ref)
            dgp_ref[...] = jnp.zeros_like(dgp_ref)
        dn = _dot_nt(dq_ref[...], w_ref[:, 0:Q_W])
        dn = dn + _dot_nt(dkv_ref[...], w_ref[:, Q_W:QKV_W])
        dn = dn + _dot_nt(dzg_ref[...], w_ref[:, ZG_OFF:GATE_OFF])
        dn = dn + _dot_nt(dzt_ref[...], w_ref[:, GATE_OFF:IN_W])
        dx, dsh, dsc, dgp = _prenorm_bwd(dn, h_ref[...], gp_ref[...], sc_ref[...])
        out_ref[...] = dh_ref[...] + dx
        dsh_ref[...] += dsh
        dsc_ref[...] += dsc
        dgp_ref[...] += dgp

    vec = _const((1, D))
    rows = lambda w_: pl.BlockSpec((R, w_), lambda i: (i, 0))
    return pl.pallas_call(
        body, name="mix_dn", grid=(S // R,),
        out_shape=[_sds((S, D), F32)] + [_sds((1, D), F32)] * 3,
        in_specs=[rows(Q_W), rows(2 * KV_W), rows(2 * G_W), rows(2 * D), _const((D, IN_W)),
                  rows(D), rows(D), vec, vec],
        out_specs=[rows(D), vec, vec, vec],
        compiler_params=_cp(1, 48),
    )(dq, dkv, dzg, dzgate, w, h, dh, sc, gp)


def _loss_head(h, target):
    S = h.shape[0]
    R = min(1024, S)

    def body(h_ref, t_ref, tot_ref, dh_ref):
        @pl.when(pl.program_id(0) == 0)
        def _():
            tot_ref[...] = jnp.zeros_like(tot_ref)
        e = h_ref[...] - t_ref[...]
        dh_ref[...] = e * (1.0 / D)
        tot_ref[...] += jnp.sum(jnp.sum(e * e, axis=1, keepdims=True), axis=0, keepdims=True)

    row = pl.BlockSpec((R, D), lambda i: (i, 0))
    return pl.pallas_call(
        body, name="loss_head", grid=(S // R,),
        out_shape=[_sds((1, 1), F32), _sds((S, D), F32)],
        in_specs=[row, row], out_specs=[_const((1, 1)), row],
        compiler_params=_cp(1, 40),
    )(h, target)


def _adamw_math(w, g, m, v):
    m2 = ADAM_B1 * m + (1.0 - ADAM_B1) * g
    v2 = ADAM_B2 * v + (1.0 - ADAM_B2) * (g * g)
    m_hat = m2 / (1.0 - ADAM_B1 ** ADAM_STEP)
    v_hat = v2 / (1.0 - ADAM_B2 ** ADAM_STEP)
    delta = -ADAM_LR * (m_hat / (jnp.sqrt(v_hat) + ADAM_EPS) + ADAM_WD * w)
    return delta, m2, v2


def _row_tile(rows, cols):
    best = None
    for t in range(16, rows + 1, 16):
        if rows % t == 0 and t * cols <= 256 * 1024:
            best = t
    return best if best is not None else rows


def _adamw_sharded(landing, w, m, v, name):
    r, c = w.shape
    tr = _row_tile(r, c)

    def body(l_ref, w_ref, m_ref, v_ref, g_ref, d_ref, m2_ref, v2_ref):
        g = l_ref[0].astype(F32)
        for j in range(1, N_DEV):
            g = g + l_ref[j].astype(F32)
        delta, m2, v2 = _adamw_math(w_ref[...], g, m_ref[...], v_ref[...])
        g_ref[...] = g
        d_ref[...] = delta
        m2_ref[...] = m2
        v2_ref[...] = v2

    row = pl.BlockSpec((tr, c), lambda i: (i, 0))
    return pl.pallas_call(
        body, name=name, grid=(r // tr,),
        out_shape=[_sds((r, c), F32)] * 4,
        in_specs=[pl.BlockSpec((N_DEV, tr, c), lambda i: (0, i, 0)), row, row, row],
        out_specs=[row] * 4,
        compiler_params=_cp(1, 48),
    )(landing, w, m, v)


def _adamw_small(w, g, m, v, name):
    def body(w_ref, g_ref, m_ref, v_ref, d_ref, m2_ref, v2_ref):
        delta, m2, v2 = _adamw_math(w_ref[...], g_ref[...], m_ref[...], v_ref[...])
        d_ref[...] = delta
        m2_ref[...] = m2
        v2_ref[...] = v2

    vm = pl.BlockSpec(memory_space=pltpu.VMEM)
    return pl.pallas_call(body, name=name, out_shape=[_sds(w.shape, F32)] * 3,
                          in_specs=[vm] * 4, out_specs=[vm] * 3)(w, g, m, v)


def _w_ada_update(c64, d_ada, w, m, v):
    tr = 256

    def body(c_ref, d_ref, w_ref, m_ref, v_ref, g_ref, dl_ref, m2_ref, v2_ref):
        i = pl.program_id(0)
        g = jnp.zeros((tr, ADA_W), F32)
        for j in range(N_DEV):
            cj = c_ref[pl.ds(8 * j, 8), :]
            sj = cj * jax.nn.sigmoid(cj)
            dj = jnp.broadcast_to(d_ref[pl.ds(j, 1), :], (8, ADA_W))
            g = g + lax.dot_general(sj, dj, (((0,), (0,)), ((), ())), preferred_element_type=F32,
                                    precision=HIGH) * 0.125
        delta, m2, v2 = _adamw_math(w_ref[...], g, m_ref[...], v_ref[...])
        g_ref[...] = g
        dl_ref[...] = delta
        m2_ref[...] = m2
        v2_ref[...] = v2

    row = pl.BlockSpec((tr, ADA_W), lambda i: (i, 0))
    return pl.pallas_call(
        body, name="w_ada_update", grid=(D // tr,),
        out_shape=[_sds((D, ADA_W), F32)] * 4,
        in_specs=[pl.BlockSpec((8 * N_DEV, tr), lambda i: (0, i)), _const((N_DEV, ADA_W)), row, row, row],
        out_specs=[row] * 4,
        compiler_params=_cp(1, 40),
    )(c64, d_ada, w, m, v)


def _t5_bucket():
    qi = jnp.arange(BLK, dtype=jnp.int32)[:, None]
    kj = jnp.arange(2 * BLK, dtype=jnp.int32)[None, :]
    dist = jnp.maximum(qi + BLK - kj, 0)
    max_exact = N_BUCKETS // 2
    d_f = jnp.maximum(dist, max_exact).astype(F32)
    large = max_exact + (jnp.log(d_f / max_exact) / math.log(MAX_DISTANCE / max_exact)
                         * (N_BUCKETS - max_exact)).astype(jnp.int32)
    large = jnp.minimum(large, N_BUCKETS - 1)
    return jnp.where(dist < max_exact, dist, large)


def _slabs_of_columns(w):
    r, c8 = w.shape
    return jnp.transpose(w.reshape(r, N_DEV, c8 // N_DEV), (1, 0, 2))


def _columns_of_slabs(w8):
    _, r, c = w8.shape
    return jnp.transpose(w8, (1, 0, 2)).reshape(r, N_DEV * c)


def kernel(x, c, rel_bias, w_ada, b_ada, pre_norm_g, post_norm_g, w_ffn1_in, w_ffn1_out, w_in, sinks, gmlp_ln_g, gmlp_ln_b, gmlp_w_s, gmlp_b_s, w_br_attn, w_br_gmlp, w_out, w_ffn2_in, w_ffn2_out, loss_target, m_rel_bias, m_w_ada, m_b_ada, m_pre_norm_g, m_post_norm_g, m_w_ffn1_in, m_w_ffn1_out, m_w_in, m_sinks, m_gmlp_ln_g, m_gmlp_ln_b, m_gmlp_w_s, m_gmlp_b_s, m_w_br_attn, m_w_br_gmlp, m_w_out, m_w_ffn2_in, m_w_ffn2_out, v_rel_bias, v_w_ada, v_b_ada, v_pre_norm_g, v_post_norm_g, v_w_ffn1_in, v_w_ffn1_out, v_w_in, v_sinks, v_gmlp_ln_g, v_gmlp_ln_b, v_gmlp_w_s, v_gmlp_b_s, v_w_br_attn, v_w_br_gmlp, v_w_out, v_w_ffn2_in, v_w_ffn2_out):
    me = 4 * lax.axis_index("x") + 2 * lax.axis_index("y") + lax.axis_index("c")
    x0 = x[0]
    target = loss_target[0]

    shards = [w_ffn1_in[0], w_ffn1_out[0], w_in[0], w_br_attn[0], w_br_gmlp[0], w_out[0],
              w_ffn2_in[0], w_ffn2_out[0]]
    shards_bf = [s.astype(BF) for s in shards]
    groups = [shards_bf[0:1], shards_bf[1:2], shards_bf[2:6], shards_bf[6:7], shards_bf[7:8]]

    def gather_start(i, after):
        return _slabs_start(False, groups[i], after, "gather_start_%d" % i)

    def gathered(st, i, after):
        return _slabs_wait(False, st, after, "gather_wait_%d" % i)

    gs0 = gather_start(0, c)

    small = jnp.concatenate([c[0], pre_norm_g[0].reshape(-1), post_norm_g[0].reshape(-1)])
    small8 = jnp.broadcast_to(small[None, :], (8, small.shape[0]))
    b_ada64 = jnp.repeat(b_ada.reshape(N_DEV, ADA_W), 8, axis=0)
    gath, ada64 = _ada_forward(small8, w_ada[0], b_ada64)
    gath8 = gath[::8]
    ada = ada64[::8].reshape(9, D)
    sh1, sc1, g1, sh2, sc2, g2, sh3, sc3, g3 = [ada[k:k + 1] for k in range(9)]
    gains = gath8[:, D:].reshape(N_DEV, 2, 3, 128)
    pre_g = jnp.transpose(gains[:, 0], (1, 0, 2)).reshape(3, D)
    post_g = jnp.transpose(gains[:, 1], (1, 0, 2)).reshape(3, D)
    pre = [pre_g[k:k + 1] for k in range(3)]
    post = [post_g[k:k + 1] for k in range(3)]

    bucket = _t5_bucket()
    bias = _bias_table(rel_bias, bucket)
    sinks8 = sinks[0]
    lg, lb = gmlp_ln_g, gmlp_ln_b
    ws = gmlp_w_s[0]
    bst = jnp.transpose(gmlp_b_s[0])

    (wf1_in,) = gathered(gs0, 0, sh1)
    gs1 = gather_start(1, wf1_in)
    gs2 = gather_start(2, wf1_in)
    n1, fg1, fu1, fa1 = _ffn_in(x0, sh1 + gs1[-1] + gs2[-1], sc1, pre[0], wf1_in, "ffn1_in")
    wf1_out = gathered(gs1, 1, n1)[0]
    gs3 = gather_start(3, wf1_out)
    wf1_out = wf1_out.reshape(4, FS, D)
    h1, y1 = _ffn_out(fa1, wf1_out, x0, g1 + gs3[-1], post[0], "ffn1_out")
    mix_w = gathered(gs2, 2, h1)
    gs4 = gather_start(4, mix_w[0])
    w_in_full = _columns_of_slabs(mix_w[0])
    w_bra = _columns_of_slabs(mix_w[1])
    w_brg = _columns_of_slabs(mix_w[2])
    w_out_full = mix_w[3].reshape(D, D)
    n2, qkv, zg, gates = _mix_in(h1, sh2 + gs4[-1], sc2, pre[1], w_in_full)
    att = _attn_fwd(qkv, bias, sinks8)
    gm = _gmlp_fwd(zg, lg, lb, ws, bst)
    ya, yg, ymix, y2, h2 = _mix_out(att, gm, gates, h1, w_bra, w_brg, w_out_full, g2, post[1])
    (wf2_in,) = gathered(gs3, 3, h2)
    n3, fg3, fu3, fa3 = _ffn_in(h2, sh3, sc3, pre[2], wf2_in, "ffn2_in")
    wf2_out = gathered(gs4, 4, n3)[0].reshape(4, FS, D)
    h3, y3 = _ffn_out(fa3, wf2_out, h2, g3, post[2], "ffn2_out")
    sq, dh3 = _loss_head(h3, target)
    loss = lax.psum(0.5 * sq[0, 0] / D, ("x", "y", "c"))

    def exchange_start(i, arrays):
        return _slabs_start(True, arrays, sq, "exchange_start_%d" % i)

    dy3, dgu3, d_g3, d_post2 = _ffn_out_bwd(dh3, y3, fg3, fu3, wf2_out, g3, post[2], "ffn2_out_bwd")
    gw_f2_out = _tn_matmul(fa3, dy3, "ffn2_out_wgrad").reshape(N_DEV, D_FF // N_DEV, D)
    ex0 = exchange_start(0, [gw_f2_out])
    dh2, d_sh3, d_sc3, d_pre2 = _ffn_dn(dgu3, wf2_in, h2, dh3, sc3 + ex0[-1], pre[2], "ffn2_dn")
    gw_f2_in = _tn_matmul(n3, dgu3, "ffn2_in_wgrad").reshape(N_DEV, D, FS)
    ex1 = exchange_start(1, [gw_f2_in])

    dy2, dya, dyg, dzgate, d_att, d_gm, d_g2, d_post1 = _mix_out_bwd(
        dh2, y2, ya, yg, gates, w_bra, w_brg, w_out_full, g2 + ex1[-1], post[1])
    gw_out = _tn_matmul(ymix, dy2, "w_out_wgrad").reshape(N_DEV, D // N_DEV, D)
    gw_bra = _slabs_of_columns(_tn_matmul(att, dya, "w_br_attn_wgrad").reshape(Q_W, D))
    gw_brg = _slabs_of_columns(_tn_matmul(gm, dyg, "w_br_gmlp_wgrad").reshape(G_W, D))
    ex2 = exchange_start(2, [gw_bra, gw_brg, gw_out])
    dq, dkv, dbias, dsink = _attn_bwd(qkv, bias, sinks8, d_att)
    dzg, d_ws, d_bs, d_lg, d_lb = _gmlp_bwd(zg, d_gm, lg, lb, ws, bst)
    dh1, d_sh2, d_sc2, d_pre1 = _mix_dn(dq, dkv, dzg, dzgate, w_in_full, h1, dh2, sc2 + ex2[-1], pre[1])
    gw_in = _slabs_of_columns(jnp.concatenate(
        [_tn_matmul(n2, dq, "w_in_q_wgrad").reshape(D, Q_W),
         _tn_matmul(n2, dkv, "w_in_kv_wgrad").reshape(D, 2 * KV_W),
         _tn_matmul(n2, dzg, "w_in_zg_wgrad").reshape(D, 2 * G_W),
         _tn_matmul(n2, dzgate, "w_in_gate_wgrad").reshape(D, 2 * D)], axis=1))
    ex3 = exchange_start(3, [gw_in])

    dy1, dgu1, d_g1, d_post0 = _ffn_out_bwd(dh1, y1, fg1, fu1, wf1_out, g1 + ex3[-1], post[0], "ffn1_out_bwd")
    gw_f1_out = _tn_matmul(fa1, dy1, "ffn1_out_wgrad").reshape(N_DEV, D_FF // N_DEV, D)
    gw_f1_in = _tn_matmul(n1, dgu1, "ffn1_in_wgrad").reshape(N_DEV, D, FS)
    ex4 = exchange_start(4, [gw_f1_out, gw_f1_in])
    grad_x, d_sh1, d_sc1, d_pre0 = _ffn_dn(dgu1, wf1_in, x0, dh1, sc1 + ex4[-1], pre[0], "ffn1_dn")

    landed = {}
    for i, (ex, nms) in enumerate([(ex0, ["w_ffn2_out"]), (ex1, ["w_ffn2_in"]),
                                   (ex2, ["w_br_attn", "w_br_gmlp", "w_out"]), (ex3, ["w_in"]),
                                   (ex4, ["w_ffn1_out", "w_ffn1_in"])]):
        for nm, land in zip(nms, _slabs_wait(True, ex, grad_x, "exchange_wait_%d" % i)):
            landed[nm] = land
    moments = [(m_w_ffn1_in, v_w_ffn1_in), (m_w_ffn1_out, v_w_ffn1_out), (m_w_in, v_w_in),
               (m_w_br_attn, v_w_br_attn), (m_w_br_gmlp, v_w_br_gmlp), (m_w_out, v_w_out),
               (m_w_ffn2_in, v_w_ffn2_in), (m_w_ffn2_out, v_w_ffn2_out)]
    names = ["w_ffn1_in", "w_ffn1_out", "w_in", "w_br_attn", "w_br_gmlp", "w_out", "w_ffn2_in", "w_ffn2_out"]
    big = {}
    for nm, w_, (m_, v_) in zip(names, shards, moments):
        big[nm] = [a[None] for a in _adamw_sharded(landed[nm], w_, m_[0], v_[0], "adamw_" + nm)]

    d_rel = _rel_bias_grad(dbias, bucket)
    d_ada = jnp.concatenate([v_.reshape(8, 128) for v_ in
                             (d_sh1, d_sc1, d_g1, d_sh2, d_sc2, d_g2, d_sh3, d_sc3, d_g3)], axis=0)
    d_pre = jnp.concatenate([d_pre0, d_pre1, d_pre2], axis=0)
    d_post = jnp.concatenate([d_post0, d_post1, d_post2], axis=0)
    pack = jnp.concatenate([
        d_ada,
        _slabs_of_columns(d_pre).reshape(24, 128),
        _slabs_of_columns(d_post).reshape(24, 128),
        jnp.concatenate([d_lg.reshape(4, 128), d_lb.reshape(4, 128)], axis=0),
        d_bs, d_rel, dsink,
        d_ws.reshape(N_HEADS * BLK, BLK)], axis=0)
    tot, every = _small_allreduce(pack)

    g_b_ada = tot[0:72].reshape(1, 9 * D)
    g_pre = lax.dynamic_slice_in_dim(tot[72:96], 3 * me, 3, axis=0)[None]
    g_post = lax.dynamic_slice_in_dim(tot[96:120], 3 * me, 3, axis=0)[None]
    g_lg = tot[120:124].reshape(1, G_W)
    g_lb = tot[124:128].reshape(1, G_W)
    g_bs = tot[128:136][None]
    g_rel = jnp.transpose(tot[136:144, 0:N_BUCKETS])
    g_sinks = tot[144:152, 0][None]
    g_ws = tot[152:1176].reshape(1, N_HEADS, BLK, BLK)

    d_ada_mine = lax.dynamic_slice_in_dim(every[:, 0:72].reshape(N_DEV, N_DEV, ADA_W), me, 1, axis=1)[:, 0]
    c64 = jnp.repeat(gath8[:, 0:D], 8, axis=0)
    ada_out = [a[None] for a in _w_ada_update(c64, d_ada_mine, w_ada[0], m_w_ada[0], v_w_ada[0])]

    def small_step(w_, g_, m_, v_, nm):
        shp = w_.shape
        two_d = (int(math.prod(shp[:-1])), shp[-1])
        d_, m2_, v2_ = _adamw_small(w_.reshape(two_d), g_.reshape(two_d), m_.reshape(two_d), v_.reshape(two_d),
                                    "adamw_" + nm)
        return [g_, d_.reshape(shp), m2_.reshape(shp), v2_.reshape(shp)]

    res = {
        "rel_bias": small_step(rel_bias, g_rel, m_rel_bias, v_rel_bias, "rel_bias"),
        "w_ada": ada_out,
        "b_ada": small_step(b_ada, g_b_ada, m_b_ada, v_b_ada, "b_ada"),
        "pre_norm_g": small_step(pre_norm_g, g_pre, m_pre_norm_g, v_pre_norm_g, "pre_norm_g"),
        "post_norm_g": small_step(post_norm_g, g_post, m_post_norm_g, v_post_norm_g, "post_norm_g"),
        "sinks": small_step(sinks, g_sinks, m_sinks, v_sinks, "sinks"),
        "gmlp_ln_g": small_step(gmlp_ln_g, g_lg, m_gmlp_ln_g, v_gmlp_ln_g, "gmlp_ln_g"),
        "gmlp_ln_b": small_step(gmlp_ln_b, g_lb, m_gmlp_ln_b, v_gmlp_ln_b, "gmlp_ln_b"),
        "gmlp_w_s": small_step(gmlp_w_s, g_ws, m_gmlp_w_s, v_gmlp_w_s, "gmlp_w_s"),
        "gmlp_b_s": small_step(gmlp_b_s, g_bs, m_gmlp_b_s, v_gmlp_b_s, "gmlp_b_s"),
    }
    res.update(big)
    order = ["rel_bias", "w_ada", "b_ada", "pre_norm_g", "post_norm_g", "w_ffn1_in", "w_ffn1_out", "w_in", "sinks",
             "gmlp_ln_g", "gmlp_ln_b", "gmlp_w_s", "gmlp_b_s", "w_br_attn", "w_br_gmlp", "w_out", "w_ffn2_in",
             "w_ffn2_out"]
    outs = [loss, grad_x[None]]
    for k in range(4):
        outs += [res[nm][k] for nm in order]
    return tuple(outs)
```

```python
import functools
import math

import jax
import jax.numpy as jnp
from jax import lax
from jax.experimental import pallas as pl
from jax.experimental.pallas import tpu as pltpu

F32 = jnp.float32
BF = jnp.bfloat16

N_DEV = 8
D = 1024
D_FF = 2816
FS = D_FF // 4
N_HEADS = 8
N_KV = 2
GROUP = 4
HD = 64
BLK = 128
Q_W = 512
KV_W = 128
G_W = 512
QKV_W = Q_W + 2 * KV_W
ZG_OFF = QKV_W
GATE_OFF = ZG_OFF + 2 * G_W
IN_W = GATE_OFF + 2 * D
N_BUCKETS = 32
MAX_DISTANCE = 128
EPS = 1e-6
NEG = -1e30
SCALE = HD ** -0.5
ADA_W = 9 * D // N_DEV

ADAM_LR = 0.001
ADAM_B1 = 0.9
ADAM_B2 = 0.999
ADAM_EPS = 1e-08
ADAM_WD = 0.01
ADAM_STEP = 10

CHUNK = 256
MIB = 1024 * 1024
MESH = pl.DeviceIdType.MESH
HIGH = lax.Precision.HIGHEST


def _cp(n_grid, vmem_mib):
    return pltpu.CompilerParams(dimension_semantics=("arbitrary",) * n_grid,
                                vmem_limit_bytes=vmem_mib * MIB)


def _const(shape):
    return pl.BlockSpec(shape, lambda *_: (0,) * len(shape))


def _sds(shape, dtype):
    return jax.ShapeDtypeStruct(shape, dtype)


def _dot(a, b):
    return jnp.dot(a, b, preferred_element_type=F32)


def _dot_nt(a, b):
    return lax.dot_general(a, b, (((1,), (1,)), ((), ())), preferred_element_type=F32)


def _dot_tn(a, b):
    return lax.dot_general(a, b, (((0,), (0,)), ((), ())), preferred_element_type=F32)


def _rms_r(x):
    return lax.rsqrt(jnp.mean(x * x, axis=-1, keepdims=True) + EPS)


def _colsum(x):
    return jnp.sum(x, axis=0, keepdims=True)


def _prenorm(x, gp, sc, sh):
    return (x * _rms_r(x) * gp) * (1.0 + sc) + sh


def _prenorm_bwd(dn, x, gp, sc):
    r = _rms_r(x)
    xh = x * r
    t = dn * (1.0 + sc) * gp
    dx = r * (t - xh * jnp.mean(t * xh, axis=-1, keepdims=True))
    return dx, _colsum(dn), _colsum(dn * xh * gp), _colsum(dn * (1.0 + sc) * xh)


def _postnorm_bwd(dh, y, gate, gp, res):
    r = _rms_r(y)
    yh = y * r
    dyn = (res * gate) * dh
    t = dyn * gp
    dy = r * (t - yh * jnp.mean(t * yh, axis=-1, keepdims=True))
    return dy, _colsum(res * dh * yh * gp), _colsum(dyn * yh)


def _gelu(x):
    k = math.sqrt(2.0 / math.pi)
    return 0.5 * x * (1.0 + jnp.tanh(k * (x + 0.044715 * x * x * x)))


def _gelu_grad(x):
    k = math.sqrt(2.0 / math.pi)
    t = jnp.tanh(k * (x + 0.044715 * x * x * x))
    return 0.5 * (1.0 + t) + 0.5 * x * (1.0 - t * t) * (k * (1.0 + 3.0 * 0.044715 * x * x))


def _my_place():
    x, y, c = lax.axis_index("x"), lax.axis_index("y"), lax.axis_index("c")
    return x, y, c, 4 * x + 2 * y + c


def _peer(x, y, c, k):
    px = 1 - x if k & 4 else x
    py = 1 - y if k & 2 else y
    pc = 1 - c if k & 1 else c
    return (px, py, pc), 4 * px + 2 * py + pc


HBM_SPEC = pl.BlockSpec(memory_space=pltpu.HBM)
SEM_SPEC = pl.BlockSpec(memory_space=pltpu.SEMAPHORE)
EFFECT = pltpu.SideEffectType.DATAFLOW_SIDE_EFFECTING


def _slab_copies(exchange, srcs, lands, send, recv, loc):
    x, y, c, me = _my_place()
    remote, local = [], []
    for t in range(len(srcs)):
        for k in range(1, N_DEV):
            peer, peer_lin = _peer(x, y, c, k)
            remote.append(pltpu.make_async_remote_copy(
                src_ref=srcs[t].at[peer_lin] if exchange else srcs[t], dst_ref=lands[t].at[me],
                send_sem=send.at[t * 7 + k - 1], recv_sem=recv.at[t * 7 + k - 1],
                device_id=peer, device_id_type=MESH))
        local.append(pltpu.make_async_copy(srcs[t].at[me] if exchange else srcs[t], lands[t].at[me], loc.at[t]))
    return remote, local


def _slabs_start(exchange, arrays, after, name):
    n = len(arrays)
    land_shapes = [a.shape if exchange else (N_DEV,) + a.shape for a in arrays]

    def body(*refs):
        srcs, lands = refs[:n], refs[n:2 * n]
        send, recv, loc = refs[2 * n + 1:2 * n + 4]
        remote, local = _slab_copies(exchange, srcs, lands, send, recv, loc)
        for cp in remote + local:
            cp.start()
        refs[-1][...] = jnp.zeros_like(refs[-1])

    return pl.pallas_call(
        body, name=name,
        out_shape=(pltpu.SemaphoreType.DMA((7 * n,)), pltpu.SemaphoreType.DMA((7 * n,)),
                   pltpu.SemaphoreType.DMA((n,)),
                   *[pltpu.HBM(a.shape, a.dtype) for a in arrays],
                   *[pltpu.HBM(s, a.dtype) for s, a in zip(land_shapes, arrays)],
                   _sds((1, D), F32)),
        in_specs=[HBM_SPEC] * (2 * n) + [pl.BlockSpec(memory_space=pl.ANY)],
        out_specs=(SEM_SPEC, SEM_SPEC, SEM_SPEC, *[HBM_SPEC] * (2 * n), pl.BlockSpec(memory_space=pltpu.VMEM)),
        input_output_aliases={t: 3 + t for t in range(2 * n)},
        compiler_params=pltpu.CompilerParams(has_side_effects=EFFECT),
    )(*[pltpu.with_memory_space_constraint(a, pltpu.HBM) for a in arrays],
      *[pltpu.with_memory_space_constraint(lax.empty(s, a.dtype), pltpu.HBM) for s, a in zip(land_shapes, arrays)],
      after)


def _slabs_wait(exchange, started, after, name):
    n = (len(started) - 4) // 2
    sems = started[0:3]
    thru = started[3:3 + 2 * n]

    def body(*refs):
        srcs, lands = refs[:n], refs[n:2 * n]
        remote, local = _slab_copies(exchange, srcs, lands, *refs[2 * n:2 * n + 3])
        for cp in remote:
            cp.wait_send()
            cp.wait_recv()
        for cp in local:
            cp.wait()

    res = pl.pallas_call(
        body, name=name,
        out_shape=tuple(pltpu.HBM(a.shape, a.dtype) for a in thru),
        in_specs=[HBM_SPEC] * (2 * n) + [SEM_SPEC] * 3 + [pl.BlockSpec(memory_space=pl.ANY)],
        out_specs=tuple([HBM_SPEC] * (2 * n)),
        input_output_aliases={t: t for t in range(2 * n)},
        compiler_params=pltpu.CompilerParams(has_side_effects=EFFECT),
    )(*thru, *sems, after)
    return list(res[n:2 * n])


def _ada_forward(small8, w_ada, b_ada64):
    sw = small8.shape[1]

    def body(sm_ref, w_ref, b_ref, gath_ref, ada_ref, part_ref, send1, recv1, send2, recv2):
        x, y, c, me = _my_place()
        row_me = pl.multiple_of(me * 8, 8)
        gath_ref[pl.ds(row_me, 8), :] = sm_ref[...]
        first = []
        for k in range(1, N_DEV):
            peer, _ = _peer(x, y, c, k)
            cp = pltpu.make_async_remote_copy(
                src_ref=sm_ref, dst_ref=gath_ref.at[pl.ds(row_me, 8), :], send_sem=send1.at[k - 1],
                recv_sem=recv1.at[k - 1], device_id=peer, device_id_type=MESH)
            cp.start()
            first.append(cp)
        for cp in first:
            cp.wait()
        cs = gath_ref[:, 0:D]
        cs = cs * jax.nn.sigmoid(cs)
        part_ref[...] = jnp.dot(cs, w_ref[...], preferred_element_type=F32, precision=HIGH)
        ada_ref[pl.ds(row_me, 8), :] = part_ref[pl.ds(row_me, 8), :]
        second = []
        for k in range(1, N_DEV):
            peer, peer_lin = _peer(x, y, c, k)
            cp = pltpu.make_async_remote_copy(
                src_ref=part_ref.at[pl.ds(pl.multiple_of(peer_lin * 8, 8), 8), :],
                dst_ref=ada_ref.at[pl.ds(row_me, 8), :], send_sem=send2.at[k - 1],
                recv_sem=recv2.at[k - 1], device_id=peer, device_id_type=MESH)
            cp.start()
            second.append(cp)
        for cp in second:
            cp.wait()
        ada_ref[...] = ada_ref[...] + b_ref[...]

    vm = pl.BlockSpec(memory_space=pltpu.VMEM)
    return pl.pallas_call(
        body, name="ada_forward",
        out_shape=[_sds((8 * N_DEV, sw), F32), _sds((8 * N_DEV, ADA_W), F32)],
        in_specs=[vm, vm, vm], out_specs=[vm, vm],
        scratch_shapes=[pltpu.VMEM((8 * N_DEV, ADA_W), F32)] + [pltpu.SemaphoreType.DMA((7,))] * 4,
        compiler_params=pltpu.CompilerParams(vmem_limit_bytes=32 * MIB),
    )(small8, w_ada, b_ada64)


def _small_allreduce(pack):
    rows = pack.shape[0]

    def body(p_ref, sum_ref, gath_ref, send, recv):
        x, y, c, me = _my_place()
        gath_ref[me] = p_ref[...]
        cps = []
        for k in range(1, N_DEV):
            peer, _ = _peer(x, y, c, k)
            cp = pltpu.make_async_remote_copy(
                src_ref=p_ref, dst_ref=gath_ref.at[me], send_sem=send.at[k - 1],
                recv_sem=recv.at[k - 1], device_id=peer, device_id_type=MESH)
            cp.start()
            cps.append(cp)
        for cp in cps:
            cp.wait()
        acc = gath_ref[0]
        for j in range(1, N_DEV):
            acc = acc + gath_ref[j]
        sum_ref[...] = acc

    vm = pl.BlockSpec(memory_space=pltpu.VMEM)
    return pl.pallas_call(
        body, name="small_allreduce",
        out_shape=[_sds((rows, 128), F32), _sds((N_DEV, rows, 128), F32)],
        in_specs=[vm], out_specs=[vm, vm],
        scratch_shapes=[pltpu.SemaphoreType.DMA((7,)), pltpu.SemaphoreType.DMA((7,))],
        compiler_params=pltpu.CompilerParams(vmem_limit_bytes=40 * MIB),
    )(pack)


def _ffn_in(h, sh, sc, gp, w8, name):
    S = h.shape[0]
    R = min(1024, S)

    def body(h_ref, sh_ref, sc_ref, gp_ref, wg_ref, wu_ref, n_ref, dg_ref, sl_ref, a_ref):
        @pl.when(pl.program_id(1) == 0)
        def _():
            for r0 in range(0, R, CHUNK):
                rows = slice(r0, r0 + CHUNK)
                n_ref[rows, :] = _prenorm(h_ref[rows, :], gp_ref[...], sc_ref[...], sh_ref[...]).astype(BF)

        for r0 in range(0, R, CHUNK):
            rows = slice(r0, r0 + CHUNK)
            n = n_ref[rows, :]
            g = _dot(n, wg_ref[...])
            u = _dot(n, wu_ref[...])
            sg = jax.nn.sigmoid(g)
            silu = g * sg
            dg_ref[rows, :] = (u * (sg * (1.0 + g * (1.0 - sg)))).astype(BF)
            sl_ref[rows, :] = silu.astype(BF)
            a_ref[rows, :] = (silu * u).astype(BF)

    vec = _const((1, D))
    blk = pl.BlockSpec((None, R, FS), lambda i, s: (s, i, 0))
    return pl.pallas_call(
        body, name=name, grid=(S // R, 4),
        out_shape=[_sds((S, D), BF)] + [_sds((4, S, FS), BF)] * 3,
        in_specs=[pl.BlockSpec((R, D), lambda i, s: (i, 0)), vec, vec, vec,
                  pl.BlockSpec((None, D, FS), lambda i, s: (s, 0, 0)),
                  pl.BlockSpec((None, D, FS), lambda i, s: (s + 4, 0, 0))],
        out_specs=[pl.BlockSpec((R, D), lambda i, s: (i, 0)), blk, blk, blk],
        compiler_params=_cp(2, 48),
    )(h, sh, sc, gp, w8, w8)


def _ffn_out(a, w4, h, gate, gp, name):
    S = h.shape[0]
    R = min(512, S)

    def body(a_ref, w_ref, h_ref, gate_ref, gp_ref, hn_ref, y_ref):
        y = _dot(a_ref[0], w_ref[0])
        for s in range(1, 4):
            y = y + _dot(a_ref[s], w_ref[s])
        hn_ref[...] = h_ref[...] + (0.5 * gate_ref[...]) * (y * _rms_r(y) * gp_ref[...])
        y_ref[...] = y

    vec = _const((1, D))
    row = pl.BlockSpec((R, D), lambda i: (i, 0))
    return pl.pallas_call(
        body, name=name, grid=(S // R,),
        out_shape=[_sds((S, D), F32), _sds((S, D), F32)],
        in_specs=[pl.BlockSpec((4, R, FS), lambda i: (0, i, 0)), _const((4, FS, D)), row, vec, vec],
        out_specs=[row, row],
        compiler_params=_cp(1, 48),
    )(a, w4, h, gate, gp)


def _ffn_out_bwd(dh, y, dsilu_u, silu, w4, gate, gp, name):
    S = dh.shape[0]
    R = min(256, S)

    def body(dh_ref, y_ref, g_ref, u_ref, w_ref, gate_ref, gp_ref, dy_ref, dgu_ref, dgate_ref, dgp_ref):
        @pl.when(pl.program_id(0) == 0)
        def _():
            dgate_ref[...] = jnp.zeros_like(dgate_ref)
            dgp_ref[...] = jnp.zeros_like(dgp_ref)
        dy, dgate, dgp = _postnorm_bwd(dh_ref[...], y_ref[...], gate_ref[...], gp_ref[...], 0.5)
        dgate_ref[...] += dgate
        dgp_ref[...] += dgp
        dyb = dy.astype(BF)
        dy_ref[...] = dyb
        for s in range(4):
            da = _dot_nt(dyb, w_ref[s])
            dgu_ref[s] = (da * g_ref[s].astype(F32)).astype(BF)
            dgu_ref[s + 4] = (da * u_ref[s].astype(F32)).astype(BF)

    vec = _const((1, D))
    row = pl.BlockSpec((R, D), lambda i: (i, 0))
    blk4 = pl.BlockSpec((4, R, FS), lambda i: (0, i, 0))
    return pl.pallas_call(
        body, name=name, grid=(S // R,),
        out_shape=[_sds((S, D), BF), _sds((8, S, FS), BF), _sds((1, D), F32), _sds((1, D), F32)],
        in_specs=[row, row, blk4, blk4, _const((4, FS, D)), vec, vec],
        out_specs=[row, pl.BlockSpec((8, R, FS), lambda i: (0, i, 0)), vec, vec],
        compiler_params=_cp(1, 56),
    )(dh, y, dsilu_u, silu, w4, gate, gp)


def _ffn_dn(dgu, w8, h, dh, sc, gp, name):
    S = h.shape[0]
    R = min(1024, S)

    def body(dgu_ref, w_ref, h_ref, dh_ref, sc_ref, gp_ref, out_ref, dsh_ref, dsc_ref, dgp_ref, acc):
        i, j = pl.program_id(0), pl.program_id(1)

        @pl.when((i == 0) & (j == 0))
        def _():
            dsh_ref[...] = jnp.zeros_like(dsh_ref)
            dsc_ref[...] = jnp.zeros_like(dsc_ref)
            dgp_ref[...] = jnp.zeros_like(dgp_ref)

        @pl.when(j == 0)
        def _():
            acc[...] = jnp.zeros_like(acc)

        for r0 in range(0, R, CHUNK):
            acc[r0:r0 + CHUNK, :] += _dot_nt(dgu_ref[r0:r0 + CHUNK, :], w_ref[...])

        @pl.when(j == N_DEV - 1)
        def _():
            for r0 in range(0, R, CHUNK):
                rows = slice(r0, r0 + CHUNK)
                dx, dsh, dsc, dgp = _prenorm_bwd(acc[rows, :], h_ref[rows, :], gp_ref[...], sc_ref[...])
                out_ref[rows, :] = dh_ref[rows, :] + dx
                dsh_ref[...] += dsh
                dsc_ref[...] += dsc
                dgp_ref[...] += dgp

    vec = _const((1, D))
    row = pl.BlockSpec((R, D), lambda i, j: (i, 0))
    return pl.pallas_call(
        body, name=name, grid=(S // R, N_DEV),
        out_shape=[_sds((S, D), F32)] + [_sds((1, D), F32)] * 3,
        in_specs=[pl.BlockSpec((None, R, FS), lambda i, j: (j, i, 0)),
                  pl.BlockSpec((None, D, FS), lambda i, j: (j, 0, 0)), row, row, vec, vec],
        out_specs=[row, vec, vec, vec],
        scratch_shapes=[pltpu.VMEM((R, D), F32)],
        compiler_params=_cp(2, 48),
    )(dgu, w8, h, dh, sc, gp)


def _tn_matmul(a, b, name):
    a3 = a if a.ndim == 3 else a[None]
    b3 = b if b.ndim == 3 else b[None]
    GA, S, M = a3.shape
    GB, _, N = b3.shape
    ts = min(1024, S)
    nk = S // ts
    chunks = [(m0, min(CHUNK, M - m0)) for m0 in range(0, M, CHUNK)]

    def body(a_ref, b_ref, o_ref, acc):
        k = pl.program_id(2)

        @pl.when(k == 0)
        def _():
            acc[...] = jnp.zeros_like(acc)

        for m0, mc in chunks:
            acc[m0:m0 + mc, :] += _dot_tn(a_ref[:, m0:m0 + mc], b_ref[...])

        @pl.when(k == nk - 1)
        def _():
            for m0, mc in chunks:
                o_ref[m0:m0 + mc, :] = acc[m0:m0 + mc, :].astype(BF)

    return pl.pallas_call(
        body, name=name, grid=(GA, GB, nk),
        out_shape=_sds((GA, GB, M, N), BF),
        in_specs=[pl.BlockSpec((None, ts, M), lambda ga, gb, k: (ga, k, 0)),
                  pl.BlockSpec((None, ts, N), lambda ga, gb, k: (gb, k, 0))],
        out_specs=pl.BlockSpec((None, None, M, N), lambda ga, gb, k: (ga, gb, 0, 0)),
        scratch_shapes=[pltpu.VMEM((M, N), F32)],
        compiler_params=_cp(3, 48),
    )(a3, b3)


def _mix_in(h, sh, sc, gp, w):
    S = h.shape[0]
    R = min(512, S)

    def body(h_ref, sh_ref, sc_ref, gp_ref, w_ref, n_ref, qkv_ref, zg_ref, gates_ref):
        nb = _prenorm(h_ref[...], gp_ref[...], sc_ref[...], sh_ref[...]).astype(BF)
        n_ref[...] = nb
        qkv_ref[...] = _dot(nb, w_ref[:, 0:ZG_OFF]).astype(BF)
        zg_ref[...] = _dot(nb, w_ref[:, ZG_OFF:GATE_OFF]).astype(BF)
        gates_ref[...] = jax.nn.sigmoid(_dot(nb, w_ref[:, GATE_OFF:IN_W])).astype(BF)

    vec = _const((1, D))
    rows = lambda w_: pl.BlockSpec((R, w_), lambda i: (i, 0))
    return pl.pallas_call(
        body, name="mix_in", grid=(S // R,),
        out_shape=[_sds((S, D), BF), _sds((S, QKV_W), BF), _sds((S, 2 * G_W), BF), _sds((S, 2 * D), BF)],
        in_specs=[rows(D), vec, vec, vec, _const((D, IN_W))],
        out_specs=[rows(D), rows(QKV_W), rows(2 * G_W), rows(2 * D)],
        compiler_params=_cp(1, 48),
    )(h, sh, sc, gp, w)


def _bias_table(rel_bias, bucket):
    def body(rel_ref, bk_ref, out_ref):
        bk = bk_ref[...]
        qi = lax.broadcasted_iota(jnp.int32, (BLK, 2 * BLK), 0)
        kj = lax.broadcasted_iota(jnp.int32, (BLK, 2 * BLK), 1)
        dist = qi + BLK - kj
        window = (dist >= 0) & (dist < BLK)
        for h in range(N_HEADS):
            acc = jnp.zeros((BLK, 2 * BLK), F32)
            for b in range(N_BUCKETS):
                acc = jnp.where(bk == b, rel_ref[b, h], acc)
            out_ref[h // GROUP, pl.ds((h % GROUP) * BLK, BLK), :] = jnp.where(window, acc, NEG)

    return pl.pallas_call(
        body, name="bias_table",
        out_shape=_sds((N_KV, GROUP * BLK, 2 * BLK), F32),
        in_specs=[pl.BlockSpec(memory_space=pltpu.SMEM), pl.BlockSpec(memory_space=pltpu.VMEM)],
        out_specs=pl.BlockSpec(memory_space=pltpu.VMEM),
    )(rel_bias, bucket)


def _attn_scores(q, kvc, kvp, bias_ref, sink_ref, blk, kh):
    k2 = jnp.concatenate([kvp[:, kh * HD:(kh + 1) * HD], kvc[:, kh * HD:(kh + 1) * HD]], axis=0)
    v2 = jnp.concatenate([kvp[:, KV_W + kh * HD:KV_W + (kh + 1) * HD],
                          kvc[:, KV_W + kh * HD:KV_W + (kh + 1) * HD]], axis=0)
    q4 = jnp.concatenate([q[:, (kh * GROUP + g) * HD:(kh * GROUP + g + 1) * HD] for g in range(GROUP)], axis=0)
    s = _dot_nt(q4, k2) * SCALE + bias_ref[kh]
    col = lax.broadcasted_iota(jnp.int32, (GROUP * BLK, 2 * BLK), 1)
    s = jnp.where((col >= BLK) | (blk > 0), s, NEG)
    rowg = lax.broadcasted_iota(jnp.int32, (GROUP * BLK, 1), 0) // BLK
    sink = jnp.zeros((GROUP * BLK, 1), F32)
    for g in range(GROUP):
        sink = jnp.where(rowg == g, sink_ref[kh * GROUP + g], sink)
    return q4, k2, v2, s, sink


def _attn_fwd(qkv, bias, sinks):
    S = qkv.shape[0]
    nb = S // BLK

    def body(sink_ref, q_ref, kvc_ref, kvp_ref, bias_ref, o_ref):
        blk = pl.program_id(0)
        q, kvc, kvp = q_ref[...], kvc_ref[...], kvp_ref[...]
        outs = []
        for kh in range(N_KV):
            q4, k2, v2, s, sink = _attn_scores(q, kvc, kvp, bias_ref, sink_ref, blk, kh)
            m = jnp.maximum(jnp.max(s, axis=1, keepdims=True), sink)
            p = jnp.exp(s - m)
            denom = jnp.sum(p, axis=1, keepdims=True) + jnp.exp(sink - m)
            o4 = _dot((p / denom).astype(BF), v2)
            outs += [o4[g * BLK:(g + 1) * BLK] for g in range(GROUP)]
        o_ref[...] = jnp.concatenate(outs, axis=1).astype(BF)

    return pl.pallas_call(
        body, name="attn_fwd", grid=(nb,),
        out_shape=_sds((S, Q_W), BF),
        in_specs=[pl.BlockSpec(memory_space=pltpu.SMEM),
                  pl.BlockSpec((BLK, Q_W), lambda i: (i, 0)),
                  pl.BlockSpec((BLK, 2 * KV_W), lambda i: (i, 2)),
                  pl.BlockSpec((BLK, 2 * KV_W), lambda i: (jnp.maximum(i - 1, 0), 2)),
                  _const((N_KV, GROUP * BLK, 2 * BLK))],
        out_specs=pl.BlockSpec((BLK, Q_W), lambda i: (i, 0)),
        compiler_params=_cp(1, 32),
    )(sinks, qkv, qkv, qkv, bias)


def _attn_bwd(qkv, bias, sinks, do):
    S = qkv.shape[0]
    nb = S // BLK

    def body(sink_ref, q_ref, kvc_ref, kvp_ref, bias_ref, do_ref, dq_ref, dkv_ref, dbias_ref, dsink_ref, carry):
        i = pl.program_id(0)
        blk = nb - 1 - i

        @pl.when(i == 0)
        def _():
            carry[...] = jnp.zeros_like(carry)
            dbias_ref[...] = jnp.zeros_like(dbias_ref)
            dsink_ref[...] = jnp.zeros_like(dsink_ref)

        q, kvc, kvp, do_ = q_ref[...], kvc_ref[...], kvp_ref[...], do_ref[...]
        dqs, dk_cur, dv_cur, dk_prev, dv_prev = [], [], [], [], []
        for kh in range(N_KV):
            q4, k2, v2, s, sink = _attn_scores(q, kvc, kvp, bias_ref, sink_ref, blk, kh)
            m = jnp.maximum(jnp.max(s, axis=1, keepdims=True), sink)
            p = jnp.exp(s - m)
            denom = jnp.sum(p, axis=1, keepdims=True) + jnp.exp(sink - m)
            prob = p / denom
            p_sink = jnp.exp(sink - m) / denom
            pb = prob.astype(BF)
            do4 = jnp.concatenate(
                [do_[:, (kh * GROUP + g) * HD:(kh * GROUP + g + 1) * HD] for g in range(GROUP)], axis=0)
            dp = _dot_nt(do4, v2)
            o4 = _dot(pb, v2)
            delta = jnp.sum(do4.astype(F32) * o4, axis=1, keepdims=True)
            ds = prob * (dp - delta)
            dbias_ref[kh] += ds
            sink_term = p_sink * delta
            for g in range(GROUP):
                h = kh * GROUP + g
                val = -jnp.sum(sink_term[g * BLK:(g + 1) * BLK], axis=0, keepdims=True)
                dsink_ref[pl.ds(h, 1), :] += jnp.broadcast_to(val, (1, 128))
            dsb = ds.astype(BF)
            dq4 = _dot(dsb, k2) * SCALE
            dk2 = _dot_tn(dsb, q4) * SCALE
            dv2 = _dot_tn(pb, do4)
            dqs += [dq4[g * BLK:(g + 1) * BLK] for g in range(GROUP)]
            dk_prev.append(dk2[0:BLK])
            dk_cur.append(dk2[BLK:2 * BLK])
            dv_prev.append(dv2[0:BLK])
            dv_cur.append(dv2[BLK:2 * BLK])
        dq_ref[...] = jnp.concatenate(dqs, axis=1).astype(BF)
        dkv_ref[...] = (jnp.concatenate(dk_cur + dv_cur, axis=1) + carry[...]).astype(BF)
        carry[...] = jnp.concatenate(dk_prev + dv_prev, axis=1)

    return pl.pallas_call(
        body, name="attn_bwd", grid=(nb,),
        out_shape=[_sds((S, Q_W), BF), _sds((S, 2 * KV_W), BF),
                   _sds((N_KV, GROUP * BLK, 2 * BLK), F32), _sds((N_HEADS, 128), F32)],
        in_specs=[pl.BlockSpec(memory_space=pltpu.SMEM),
                  pl.BlockSpec((BLK, Q_W), lambda i: (nb - 1 - i, 0)),
                  pl.BlockSpec((BLK, 2 * KV_W), lambda i: (nb - 1 - i, 2)),
                  pl.BlockSpec((BLK, 2 * KV_W), lambda i: (jnp.maximum(nb - 2 - i, 0), 2)),
                  _const((N_KV, GROUP * BLK, 2 * BLK)),
                  pl.BlockSpec((BLK, Q_W), lambda i: (nb - 1 - i, 0))],
        out_specs=[pl.BlockSpec((BLK, Q_W), lambda i: (nb - 1 - i, 0)),
                   pl.BlockSpec((BLK, 2 * KV_W), lambda i: (nb - 1 - i, 0)),
                   _const((N_KV, GROUP * BLK, 2 * BLK)), _const((N_HEADS, 128))],
        scratch_shapes=[pltpu.VMEM((BLK, 2 * KV_W), F32)],
        compiler_params=_cp(1, 32),
    )(sinks, qkv, qkv, qkv, bias, do)


def _rel_bias_grad(dbias, bucket):
    def body(db_ref, bk_ref, out_ref):
        bk = bk_ref[...]
        lane = lax.broadcasted_iota(jnp.int32, (1, 128), 1)
        for h in range(N_HEADS):
            d = db_ref[h // GROUP, pl.ds((h % GROUP) * BLK, BLK), :]
            row = jnp.zeros((1, 128), F32)
            for b in range(N_BUCKETS):
                tot = jnp.sum(jnp.sum(jnp.where(bk == b, d, 0.0), axis=1, keepdims=True), axis=0, keepdims=True)
                row = jnp.where(lane == b, tot, row)
            out_ref[pl.ds(h, 1), :] = row

    vm = pl.BlockSpec(memory_space=pltpu.VMEM)
    return pl.pallas_call(body, name="rel_bias_grad", out_shape=_sds((N_HEADS, 128), F32),
                          in_specs=[vm, vm], out_specs=vm)(dbias, bucket)


def _gmlp_parts(zg_ref, lg_ref, lb_ref):
    z = zg_ref[...].astype(F32)
    ge = _gelu(z)
    u, vg = ge[:, 0:G_W], ge[:, G_W:2 * G_W]
    mu = jnp.mean(vg, axis=-1, keepdims=True)
    xc = vg - mu
    rstd = lax.rsqrt(jnp.mean(xc * xc, axis=-1, keepdims=True) + EPS)
    xh = xc * rstd
    return z, u, xh, rstd, xh * lg_ref[...] + lb_ref[...]


def _causal_weights(ws_ref, wc):
    t = lax.broadcasted_iota(jnp.int32, (BLK, BLK), 0)
    s = lax.broadcasted_iota(jnp.int32, (BLK, BLK), 1)
    for g in range(N_HEADS):
        wc[g] = jnp.where(s <= t, ws_ref[g], 0.0).astype(BF)


def _spatial(vb, wc, bst_ref, p, low):
    xp = vb[:, p * 128:(p + 1) * 128]
    s0 = _dot(wc[2 * p], xp) + bst_ref[:, 2 * p:2 * p + 1]
    s1 = _dot(wc[2 * p + 1], xp) + bst_ref[:, 2 * p + 1:2 * p + 2]
    return xp, jnp.where(low, s0, s1)


def _gmlp_fwd(zg, lg, lb, ws, bst):
    S = zg.shape[0]

    def body(zg_ref, lg_ref, lb_ref, ws_ref, bst_ref, o_ref, wc):
        @pl.when(pl.program_id(0) == 0)
        def _():
            _causal_weights(ws_ref, wc)
        _, u, _, _, vln = _gmlp_parts(zg_ref, lg_ref, lb_ref)
        vb = vln.astype(BF)
        low = lax.broadcasted_iota(jnp.int32, (BLK, 128), 1) < HD
        for p in range(4):
            _, sp = _spatial(vb, wc, bst_ref, p, low)
            o_ref[:, p * 128:(p + 1) * 128] = (u[:, p * 128:(p + 1) * 128] * sp).astype(BF)

    return pl.pallas_call(
        body, name="gmlp_fwd", grid=(S // BLK,),
        out_shape=_sds((S, G_W), BF),
        in_specs=[pl.BlockSpec((BLK, 2 * G_W), lambda i: (i, 0)), _const((1, G_W)), _const((1, G_W)),
                  _const((N_HEADS, BLK, BLK)), _const((BLK, N_HEADS))],
        out_specs=pl.BlockSpec((BLK, G_W), lambda i: (i, 0)),
        scratch_shapes=[pltpu.VMEM((N_HEADS, BLK, BLK), BF)],
        compiler_params=_cp(1, 32),
    )(zg, lg, lb, ws, bst)


def _gmlp_bwd(zg, d_out, lg, lb, ws, bst):
    S = zg.shape[0]
    nb = S // BLK

    def body(zg_ref, d_ref, lg_ref, lb_ref, ws_ref, bst_ref, dzg_ref, dws_ref, dbs_ref, dlg_ref, dlb_ref, wc, dbacc):
        i = pl.program_id(0)

        @pl.when(i == 0)
        def _():
            _causal_weights(ws_ref, wc)
            dws_ref[...] = jnp.zeros_like(dws_ref)
            dlg_ref[...] = jnp.zeros_like(dlg_ref)
            dlb_ref[...] = jnp.zeros_like(dlb_ref)
            dbacc[...] = jnp.zeros_like(dbacc)

        z, u, xh, rstd, vln = _gmlp_parts(zg_ref, lg_ref, lb_ref)
        vb = vln.astype(BF)
        d = d_ref[...].astype(F32)
        low = lax.broadcasted_iota(jnp.int32, (BLK, 128), 1) < HD
        du_parts, dvln_parts = [], []
        for p in range(4):
            xp, sp = _spatial(vb, wc, bst_ref, p, low)
            dp = d[:, p * 128:(p + 1) * 128]
            du_parts.append(dp * sp)
            dsp = dp * u[:, p * 128:(p + 1) * 128]
            dbacc[:, p * 128:(p + 1) * 128] += dsp
            d0 = jnp.where(low, dsp, 0.0).astype(BF)
            d1 = jnp.where(low, 0.0, dsp).astype(BF)
            dws_ref[2 * p] += _dot_nt(d0, xp)
            dws_ref[2 * p + 1] += _dot_nt(d1, xp)
            dvln_parts.append(_dot_tn(wc[2 * p], d0) + _dot_tn(wc[2 * p + 1], d1))
        dvln = jnp.concatenate(dvln_parts, axis=1)
        dlg_ref[...] += _colsum(dvln * xh)
        dlb_ref[...] += _colsum(dvln)
        dxh = dvln * lg_ref[...]
        dvg = rstd * (dxh - jnp.mean(dxh, axis=-1, keepdims=True)
                      - xh * jnp.mean(dxh * xh, axis=-1, keepdims=True))
        dge = jnp.concatenate(du_parts + [dvg], axis=1)
        dzg_ref[...] = (dge * _gelu_grad(z)).astype(BF)

        @pl.when(i == nb - 1)
        def _():
            t = lax.broadcasted_iota(jnp.int32, (BLK, BLK), 0)
            s = lax.broadcasted_iota(jnp.int32, (BLK, BLK), 1)
            for g in range(N_HEADS):
                dws_ref[g] = jnp.where(s <= t, dws_ref[g], 0.0)
            grp = lax.broadcasted_iota(jnp.int32, (N_HEADS, G_W), 0)
            lane = lax.broadcasted_iota(jnp.int32, (N_HEADS, G_W), 1) // HD
            pick = jnp.where(grp == lane, 1.0, 0.0).astype(F32)
            dbs_ref[...] = lax.dot_general(pick, dbacc[...], (((1,), (1,)), ((), ())),
                                           preferred_element_type=F32, precision=HIGH)

    return pl.pallas_call(
        body, name="gmlp_bwd", grid=(nb,),
        out_shape=[_sds((S, 2 * G_W), BF), _sds((N_HEADS, BLK, BLK), F32), _sds((N_HEADS, BLK), F32),
                   _sds((1, G_W), F32), _sds((1, G_W), F32)],
        in_specs=[pl.BlockSpec((BLK, 2 * G_W), lambda i: (i, 0)), pl.BlockSpec((BLK, G_W), lambda i: (i, 0)),
                  _const((1, G_W)), _const((1, G_W)), _const((N_HEADS, BLK, BLK)), _const((BLK, N_HEADS))],
        out_specs=[pl.BlockSpec((BLK, 2 * G_W), lambda i: (i, 0)), _const((N_HEADS, BLK, BLK)),
                   _const((N_HEADS, BLK)), _const((1, G_W)), _const((1, G_W))],
        scratch_shapes=[pltpu.VMEM((N_HEADS, BLK, BLK), BF), pltpu.VMEM((BLK, G_W), F32)],
        compiler_params=_cp(1, 32),
    )(zg, d_out, lg, lb, ws, bst)


def _mix_out(o, gm, gates, h, wa, wg, wo, gate, gp):
    S = h.shape[0]
    R = min(512, S)

    def body(o_ref, gm_ref, gates_ref, h_ref, wa_ref, wg_ref, wo_ref, gate_ref, gp_ref,
             ya_ref, yg_ref, ym_ref, y_ref, hn_ref):
        for r0 in range(0, R, CHUNK):
            rows = slice(r0, r0 + CHUNK)
            ya = _dot(o_ref[rows, :], wa_ref[...])
            yg = _dot(gm_ref[rows, :], wg_ref[...])
            ya_ref[rows, :] = ya.astype(BF)
            yg_ref[rows, :] = yg.astype(BF)
            ym = (gates_ref[rows, 0:D].astype(F32) * ya + gates_ref[rows, D:2 * D].astype(F32) * yg).astype(BF)
            ym_ref[rows, :] = ym
            y = _dot(ym, wo_ref[...])
            y_ref[rows, :] = y
            hn_ref[rows, :] = h_ref[rows, :] + gate_ref[...] * (y * _rms_r(y) * gp_ref[...])

    vec = _const((1, D))
    rows = lambda w_: pl.BlockSpec((R, w_), lambda i: (i, 0))
    return pl.pallas_call(
        body, name="mix_out", grid=(S // R,),
        out_shape=[_sds((S, D), BF)] * 3 + [_sds((S, D), F32)] * 2,
        in_specs=[rows(Q_W), rows(G_W), rows(2 * D), rows(D), _const((Q_W, D)), _const((G_W, D)),
                  _const((D, D)), vec, vec],
        out_specs=[rows(D)] * 5,
        compiler_params=_cp(1, 48),
    )(o, gm, gates, h, wa, wg, wo, gate, gp)


def _mix_out_bwd(dh, y, ya, yg, gates, wa, wg, wo, gate, gp):
    S = dh.shape[0]
    R = min(256, S)

    def body(dh_ref, y_ref, ya_ref, yg_ref, gates_ref, wa_ref, wg_ref, wo_ref, gate_ref, gp_ref,
             dy_ref, dya_ref, dyg_ref, dz_ref, do_ref, dgm_ref, dgate_ref, dgp_ref):
        @pl.when(pl.program_id(0) == 0)
        def _():
            dgate_ref[...] = jnp.zeros_like(dgate_ref)
            dgp_ref[...] = jnp.zeros_like(dgp_ref)
        dy, dgate, dgp = _postnorm_bwd(dh_ref[...], y_ref[...], gate_ref[...], gp_ref[...], 1.0)
        dgate_ref[...] += dgate
        dgp_ref[...] += dgp
        dyb = dy.astype(BF)
        dy_ref[...] = dyb
        dym = _dot_nt(dyb, wo_ref[...])
        ga = gates_ref[:, 0:D].astype(F32)
        gg = gates_ref[:, D:2 * D].astype(F32)
        dya = (dym * ga).astype(BF)
        dyg = (dym * gg).astype(BF)
        dya_ref[...] = dya
        dyg_ref[...] = dyg
        dz_ref[:, 0:D] = (dym * ya_ref[...].astype(F32) * (ga * (1.0 - ga))).astype(BF)
        dz_ref[:, D:2 * D] = (dym * yg_ref[...].astype(F32) * (gg * (1.0 - gg))).astype(BF)
        do_ref[...] = _dot_nt(dya, wa_ref[...]).astype(BF)
        dgm_ref[...] = _dot_nt(dyg, wg_ref[...]).astype(BF)

    vec = _const((1, D))
    rows = lambda w_: pl.BlockSpec((R, w_), lambda i: (i, 0))
    return pl.pallas_call(
        body, name="mix_out_bwd", grid=(S // R,),
        out_shape=[_sds((S, D), BF)] * 3 + [_sds((S, 2 * D), BF), _sds((S, Q_W), BF), _sds((S, G_W), BF),
                                             _sds((1, D), F32), _sds((1, D), F32)],
        in_specs=[rows(D), rows(D), rows(D), rows(D), rows(2 * D), _const((Q_W, D)), _const((G_W, D)),
                  _const((D, D)), vec, vec],
        out_specs=[rows(D)] * 3 + [rows(2 * D), rows(Q_W), rows(G_W), vec, vec],
        compiler_params=_cp(1, 48),
    )(dh, y, ya, yg, gates, wa, wg, wo, gate, gp)


def _mix_dn(dq, dkv, dzg, dzgate, w, h, dh, sc, gp):
    S = h.shape[0]
    R = min(512, S)

    def body(dq_ref, dkv_ref, dzg_ref, dzt_ref, w_ref, h_ref, dh_ref, sc_ref, gp_ref,
             out_ref, dsh_ref, dsc_ref, dgp_ref):
        @pl.when(pl.program_id(0) == 0)
        def _():
            dsh_ref[...] = jnp.zeros_like(dsh_ref)
            dsc_ref[...] = jnp.zeros_like(dsc_ref)
            dgp_ref[...] = jnp.zeros_like(dgp_ref)
        for r0 in range(0, R, CHUNK):
            rows = slice(r0, r0 + CHUNK)
            dn = _dot_nt(dq_ref[rows, :], w_ref[:, 0:Q_W])
            dn = dn + _dot_nt(dkv_ref[rows, :], w_ref[:, Q_W:QKV_W])
            dn = dn + _dot_nt(dzg_ref[rows, :], w_ref[:, ZG_OFF:GATE_OFF])
            dn = dn + _dot_nt(dzt_ref[rows, :], w_ref[:, GATE_OFF:IN_W])
            dx, dsh, dsc, dgp = _prenorm_bwd(dn, h_ref[rows, :], gp_ref[...], sc_ref[...])
            out_ref[rows, :] = dh_ref[rows, :] + dx
            dsh_ref[...] += dsh
            dsc_ref[...] += dsc
            dgp_ref[...] += dgp

    vec = _const((1, D))
    rows = lambda w_: pl.BlockSpec((R, w_), lambda i: (i, 0))
    return pl.pallas_call(
        body, name="mix_dn", grid=(S // R,),
        out_shape=[_sds((S, D), F32)] + [_sds((1, D), F32)] * 3,
        in_specs=[rows(Q_W), rows(2 * KV_W), rows(2 * G_W), rows(2 * D), _const((D, IN_W)),
                  rows(D), rows(D), vec, vec],
        out_specs=[rows(D), vec, vec, vec],
        compiler_params=_cp(1, 48),
    )(dq, dkv, dzg, dzgate, w, h, dh, sc, gp)


def _loss_head(h, target):
    S = h.shape[0]
    R = min(1024, S)

    def body(h_ref, t_ref, tot_ref, dh_ref):
        @pl.when(pl.program_id(0) == 0)
        def _():
            tot_ref[...] = jnp.zeros_like(tot_ref)
        e = h_ref[...] - t_ref[...]
        dh_ref[...] = e * (1.0 / D)
        tot_ref[...] += jnp.sum(jnp.sum(e * e, axis=1, keepdims=True), axis=0, keepdims=True)

    row = pl.BlockSpec((R, D), lambda i: (i, 0))
    return pl.pallas_call(
        body, name="loss_head", grid=(S // R,),
        out_shape=[_sds((1, 1), F32), _sds((S, D), F32)],
        in_specs=[row, row], out_specs=[_const((1, 1)), row],
        compiler_params=_cp(1, 40),
    )(h, target)


def _adamw_math(w, g, m, v):
    m2 = ADAM_B1 * m + (1.0 - ADAM_B1) * g
    v2 = ADAM_B2 * v + (1.0 - ADAM_B2) * (g * g)
    m_hat = m2 / (1.0 - ADAM_B1 ** ADAM_STEP)
    v_hat = v2 / (1.0 - ADAM_B2 ** ADAM_STEP)
    delta = -ADAM_LR * (m_hat / (jnp.sqrt(v_hat) + ADAM_EPS) + ADAM_WD * w)
    return delta, m2, v2


def _row_tile(rows, cols):
    best = None
    for t in range(16, rows + 1, 16):
        if rows % t == 0 and t * cols <= 256 * 1024:
            best = t
    return best if best is not None else rows


def _adamw_sharded(landing, w, m, v, name):
    r, c = w.shape
    tr = _row_tile(r, c)

    def body(l_ref, w_ref, m_ref, v_ref, g_ref, d_ref, m2_ref, v2_ref):
        g = l_ref[0].astype(F32)
        for j in range(1, N_DEV):
            g = g + l_ref[j].astype(F32)
        delta, m2, v2 = _adamw_math(w_ref[...], g, m_ref[...], v_ref[...])
        g_ref[...] = g
        d_ref[...] = delta
        m2_ref[...] = m2
        v2_ref[...] = v2

    row = pl.BlockSpec((tr, c), lambda i: (i, 0))
    return pl.pallas_call(
        body, name=name, grid=(r // tr,),
        out_shape=[_sds((r, c), F32)] * 4,
        in_specs=[pl.BlockSpec((N_DEV, tr, c), lambda i: (0, i, 0)), row, row, row],
        out_specs=[row] * 4,
        compiler_params=_cp(1, 48),
    )(landing, w, m, v)


def _adamw_small(w, g, m, v, name):
    def body(w_ref, g_ref, m_ref, v_ref, d_ref, m2_ref, v2_ref):
        delta, m2, v2 = _adamw_math(w_ref[...], g_ref[...], m_ref[...], v_ref[...])
        d_ref[...] = delta
        m2_ref[...] = m2
        v2_ref[...] = v2

    vm = pl.BlockSpec(memory_space=pltpu.VMEM)
    return pl.pallas_call(body, name=name, out_shape=[_sds(w.shape, F32)] * 3,
                          in_specs=[vm] * 4, out_specs=[vm] * 3)(w, g, m, v)


def _w_ada_update(c64, d_ada, w, m, v):
    tr = 256

    def body(c_ref, d_ref, w_ref, m_ref, v_ref, g_ref, dl_ref, m2_ref, v2_ref):
        i = pl.program_id(0)
        g = jnp.zeros((tr, ADA_W), F32)
        for j in range(N_DEV):
            cj = c_ref[pl.ds(8 * j, 8), :]
            sj = cj * jax.nn.sigmoid(cj)
            dj = jnp.broadcast_to(d_ref[pl.ds(j, 1), :], (8, ADA_W))
            g = g + lax.dot_general(sj, dj, (((0,), (0,)), ((), ())), preferred_element_type=F32,
                                    precision=HIGH) * 0.125
        delta, m2, v2 = _adamw_math(w_ref[...], g, m_ref[...], v_ref[...])
        g_ref[...] = g
        dl_ref[...] = delta
        m2_ref[...] = m2
        v2_ref[...] = v2

    row = pl.BlockSpec((tr, ADA_W), lambda i: (i, 0))
    return pl.pallas_call(
        body, name="w_ada_update", grid=(D // tr,),
        out_shape=[_sds((D, ADA_W), F32)] * 4,
        in_specs=[pl.BlockSpec((8 * N_DEV, tr), lambda i: (0, i)), _const((N_DEV, ADA_W)), row, row, row],
        out_specs=[row] * 4,
        compiler_params=_cp(1, 40),
    )(c64, d_ada, w, m, v)


def _t5_bucket():
    qi = jnp.arange(BLK, dtype=jnp.int32)[:, None]
    kj = jnp.arange(2 * BLK, dtype=jnp.int32)[None, :]
    dist = jnp.maximum(qi + BLK - kj, 0)
    max_exact = N_BUCKETS // 2
    d_f = jnp.maximum(dist, max_exact).astype(F32)
    large = max_exact + (jnp.log(d_f / max_exact) / math.log(MAX_DISTANCE / max_exact)
                         * (N_BUCKETS - max_exact)).astype(jnp.int32)
    large = jnp.minimum(large, N_BUCKETS - 1)
    return jnp.where(dist < max_exact, dist, large)


def _slabs_of_columns(w):
    r, c8 = w.shape
    return jnp.transpose(w.reshape(r, N_DEV, c8 // N_DEV), (1, 0, 2))


def _columns_of_slabs(w8):
    _, r, c = w8.shape
    return jnp.transpose(w8, (1, 0, 2)).reshape(r, N_DEV * c)


def kernel(x, c, rel_bias, w_ada, b_ada, pre_norm_g, post_norm_g, w_ffn1_in, w_ffn1_out, w_in, sinks, gmlp_ln_g, gmlp_ln_b, gmlp_w_s, gmlp_b_s, w_br_attn, w_br_gmlp, w_out, w_ffn2_in, w_ffn2_out, loss_target, m_rel_bias, m_w_ada, m_b_ada, m_pre_norm_g, m_post_norm_g, m_w_ffn1_in, m_w_ffn1_out, m_w_in, m_sinks, m_gmlp_ln_g, m_gmlp_ln_b, m_gmlp_w_s, m_gmlp_b_s, m_w_br_attn, m_w_br_gmlp, m_w_out, m_w_ffn2_in, m_w_ffn2_out, v_rel_bias, v_w_ada, v_b_ada, v_pre_norm_g, v_post_norm_g, v_w_ffn1_in, v_w_ffn1_out, v_w_in, v_sinks, v_gmlp_ln_g, v_gmlp_ln_b, v_gmlp_w_s, v_gmlp_b_s, v_w_br_attn, v_w_br_gmlp, v_w_out, v_w_ffn2_in, v_w_ffn2_out):
    me = 4 * lax.axis_index("x") + 2 * lax.axis_index("y") + lax.axis_index("c")
    x0 = x[0]
    target = loss_target[0]

    shards = [w_ffn1_in[0], w_ffn1_out[0], w_in[0], w_br_attn[0], w_br_gmlp[0], w_out[0],
              w_ffn2_in[0], w_ffn2_out[0]]
    shards_bf = [s.astype(BF) for s in shards]
    groups = [shards_bf[0:1], shards_bf[1:2], shards_bf[2:6], shards_bf[6:7], shards_bf[7:8]]

    def gather_start(i, after):
        return _slabs_start(False, groups[i], after, "gather_start_%d" % i)

    def gathered(st, i, after):
        return _slabs_wait(False, st, after, "gather_wait_%d" % i)

    gs0 = gather_start(0, c)

    small = jnp.concatenate([c[0], pre_norm_g[0].reshape(-1), post_norm_g[0].reshape(-1)])
    small8 = jnp.broadcast_to(small[None, :], (8, small.shape[0]))
    b_ada64 = jnp.repeat(b_ada.reshape(N_DEV, ADA_W), 8, axis=0)
    gath, ada64 = _ada_forward(small8, w_ada[0], b_ada64)
    gath8 = gath[::8]
    ada = ada64[::8].reshape(9, D)
    sh1, sc1, g1, sh2, sc2, g2, sh3, sc3, g3 = [ada[k:k + 1] for k in range(9)]
    gains = gath8[:, D:].reshape(N_DEV, 2, 3, 128)
    pre_g = jnp.transpose(gains[:, 0], (1, 0, 2)).reshape(3, D)
    post_g = jnp.transpose(gains[:, 1], (1, 0, 2)).reshape(3, D)
    pre = [pre_g[k:k + 1] for k in range(3)]
    post = [post_g[k:k + 1] for k in range(3)]

    bucket = _t5_bucket()
    bias = _bias_table(rel_bias, bucket)
    sinks8 = sinks[0]
    lg, lb = gmlp_ln_g, gmlp_ln_b
    ws = gmlp_w_s[0]
    bst = jnp.transpose(gmlp_b_s[0])

    (wf1_in,) = gathered(gs0, 0, sh1)
    gs1 = gather_start(1, wf1_in)
    gs2 = gather_start(2, wf1_in)
    n1, fg1, fu1, fa1 = _ffn_in(x0, sh1 + gs1[-1] + gs2[-1], sc1, pre[0], wf1_in, "ffn1_in")
    wf1_out = gathered(gs1, 1, n1)[0]
    gs3 = gather_start(3, wf1_out)
    wf1_out = wf1_out.reshape(4, FS, D)
    h1, y1 = _ffn_out(fa1, wf1_out, x0, g1 + gs3[-1], post[0], "ffn1_out")
    mix_w = gathered(gs2, 2, h1)
    gs4 = gather_start(4, mix_w[0])
    w_in_full = _columns_of_slabs(mix_w[0])
    w_bra = _columns_of_slabs(mix_w[1])
    w_brg = _columns_of_slabs(mix_w[2])
    w_out_full = mix_w[3].reshape(D, D)
    n2, qkv, zg, gates = _mix_in(h1, sh2 + gs4[-1], sc2, pre[1], w_in_full)
    att = _attn_fwd(qkv, bias, sinks8)
    gm = _gmlp_fwd(zg, lg, lb, ws, bst)
    ya, yg, ymix, y2, h2 = _mix_out(att, gm, gates, h1, w_bra, w_brg, w_out_full, g2, post[1])
    (wf2_in,) = gathered(gs3, 3, h2)
    n3, fg3, fu3, fa3 = _ffn_in(h2, sh3, sc3, pre[2], wf2_in, "ffn2_in")
    wf2_out = gathered(gs4, 4, n3)[0].reshape(4, FS, D)
    h3, y3 = _ffn_out(fa3, wf2_out, h2, g3, post[2], "ffn2_out")
    sq, dh3 = _loss_head(h3, target)
    loss = lax.psum(0.5 * sq[0, 0] / D, ("x", "y", "c"))

    def exchange_start(i, arrays):
        return _slabs_start(True, arrays, sq, "exchange_start_%d" % i)

    dy3, dgu3, d_g3, d_post2 = _ffn_out_bwd(dh3, y3, fg3, fu3, wf2_out, g3, post[2], "ffn2_out_bwd")
    gw_f2_out = _tn_matmul(fa3, dy3, "ffn2_out_wgrad").reshape(N_DEV, D_FF // N_DEV, D)
    ex0 = exchange_start(0, [gw_f2_out])
    dh2, d_sh3, d_sc3, d_pre2 = _ffn_dn(dgu3, wf2_in, h2, dh3, sc3 + ex0[-1], pre[2], "ffn2_dn")
    gw_f2_in = _tn_matmul(n3, dgu3, "ffn2_in_wgrad").reshape(N_DEV, D, FS)
    ex1 = exchange_start(1, [gw_f2_in])

    dy2, dya, dyg, dzgate, d_att, d_gm, d_g2, d_post1 = _mix_out_bwd(
        dh2, y2, ya, yg, gates, w_bra, w_brg, w_out_full, g2 + ex1[-1], post[1])
    gw_out = _tn_matmul(ymix, dy2, "w_out_wgrad").reshape(N_DEV, D // N_DEV, D)
    gw_bra = _slabs_of_columns(_tn_matmul(att, dya, "w_br_attn_wgrad").reshape(Q_W, D))
    gw_brg = _slabs_of_columns(_tn_matmul(gm, dyg, "w_br_gmlp_wgrad").reshape(G_W, D))
    ex2 = exchange_start(2, [gw_bra, gw_brg, gw_out])
    dq, dkv, dbias, dsink = _attn_bwd(qkv, bias, sinks8, d_att)
    dzg, d_ws, d_bs, d_lg, d_lb = _gmlp_bwd(zg, d_gm, lg, lb, ws, bst)
    dh1, d_sh2, d_sc2, d_pre1 = _mix_dn(dq, dkv, dzg, dzgate, w_in_full, h1, dh2, sc2 + ex2[-1], pre[1])
    gw_in = _slabs_of_columns(jnp.concatenate(
        [_tn_matmul(n2, dq, "w_in_q_wgrad").reshape(D, Q_W),
         _tn_matmul(n2, dkv, "w_in_kv_wgrad").reshape(D, 2 * KV_W),
         _tn_matmul(n2, dzg, "w_in_zg_wgrad").reshape(D, 2 * G_W),
         _tn_matmul(n2, dzgate, "w_in_gate_wgrad").reshape(D, 2 * D)], axis=1))
    ex3 = exchange_start(3, [gw_in])

    dy1, dgu1, d_g1, d_post0 = _ffn_out_bwd(dh1, y1, fg1, fu1, wf1_out, g1 + ex3[-1], post[0], "ffn1_out_bwd")
    gw_f1_out = _tn_matmul(fa1, dy1, "ffn1_out_wgrad").reshape(N_DEV, D_FF // N_DEV, D)
    gw_f1_in = _tn_matmul(n1, dgu1, "ffn1_in_wgrad").reshape(N_DEV, D, FS)
    ex4 = exchange_start(4, [gw_f1_out, gw_f1_in])
    grad_x, d_sh1, d_sc1, d_pre0 = _ffn_dn(dgu1, wf1_in, x0, dh1, sc1 + ex4[-1], pre[0], "ffn1_dn")

    landed = {}
    for i, (ex, nms) in enumerate([(ex0, ["w_ffn2_out"]), (ex1, ["w_ffn2_in"]),
                                   (ex2, ["w_br_attn", "w_br_gmlp", "w_out"]), (ex3, ["w_in"]),
                                   (ex4, ["w_ffn1_out", "w_ffn1_in"])]):
        for nm, land in zip(nms, _slabs_wait(True, ex, grad_x, "exchange_wait_%d" % i)):
            landed[nm] = land
    moments = [(m_w_ffn1_in, v_w_ffn1_in), (m_w_ffn1_out, v_w_ffn1_out), (m_w_in, v_w_in),
               (m_w_br_attn, v_w_br_attn), (m_w_br_gmlp, v_w_br_gmlp), (m_w_out, v_w_out),
               (m_w_ffn2_in, v_w_ffn2_in), (m_w_ffn2_out, v_w_ffn2_out)]
    names = ["w_ffn1_in", "w_ffn1_out", "w_in", "w_br_attn", "w_br_gmlp", "w_out", "w_ffn2_in", "w_ffn2_out"]
    big = {}
    for nm, w_, (m_, v_) in zip(names, shards, moments):
        big[nm] = [a[None] for a in _adamw_sharded(landed[nm], w_, m_[0], v_[0], "adamw_" + nm)]

    d_rel = _rel_bias_grad(dbias, bucket)
    d_ada = jnp.concatenate([v_.reshape(8, 128) for v_ in
                             (d_sh1, d_sc1, d_g1, d_sh2, d_sc2, d_g2, d_sh3, d_sc3, d_g3)], axis=0)
    d_pre = jnp.concatenate([d_pre0, d_pre1, d_pre2], axis=0)
    d_post = jnp.concatenate([d_post0, d_post1, d_post2], axis=0)
    pack = jnp.concatenate([
        d_ada,
        _slabs_of_columns(d_pre).reshape(24, 128),
        _slabs_of_columns(d_post).reshape(24, 128),
        jnp.concatenate([d_lg.reshape(4, 128), d_lb.reshape(4, 128)], axis=0),
        d_bs, d_rel, dsink,
        d_ws.reshape(N_HEADS * BLK, BLK)], axis=0)
    tot, every = _small_allreduce(pack)

    g_b_ada = tot[0:72].reshape(1, 9 * D)
    g_pre = lax.dynamic_slice_in_dim(tot[72:96], 3 * me, 3, axis=0)[None]
    g_post = lax.dynamic_slice_in_dim(tot[96:120], 3 * me, 3, axis=0)[None]
    g_lg = tot[120:124].reshape(1, G_W)
    g_lb = tot[124:128].reshape(1, G_W)
    g_bs = tot[128:136][None]
    g_rel = jnp.transpose(tot[136:144, 0:N_BUCKETS])
    g_sinks = tot[144:152, 0][None]
    g_ws = tot[152:1176].reshape(1, N_HEADS, BLK, BLK)

    d_ada_mine = lax.dynamic_slice_in_dim(every[:, 0:72].reshape(N_DEV, N_DEV, ADA_W), me, 1, axis=1)[:, 0]
    c64 = jnp.repeat(gath8[:, 0:D], 8, axis=0)
    ada_out = [a[None] for a in _w_ada_update(c64, d_ada_mine, w_ada[0], m_w_ada[0], v_w_ada[0])]

    def small_step(w_, g_, m_, v_, nm):
        shp = w_.shape
        two_d = (int(math.prod(shp[:-1])), shp[-1])
        d_, m2_, v2_ = _adamw_small(w_.reshape(two_d), g_.reshape(two_d), m_.reshape(two_d), v_.reshape(two_d),
                                    "adamw_" + nm)
        return [g_, d_.reshape(shp), m2_.reshape(shp), v2_.reshape(shp)]

    res = {
        "rel_bias": small_step(rel_bias, g_rel, m_rel_bias, v_rel_bias, "rel_bias"),
        "w_ada": ada_out,
        "b_ada": small_step(b_ada, g_b_ada, m_b_ada, v_b_ada, "b_ada"),
        "pre_norm_g": small_step(pre_norm_g, g_pre, m_pre_norm_g, v_pre_norm_g, "pre_norm_g"),
        "post_norm_g": small_step(post_norm_g, g_post, m_post_norm_g, v_post_norm_g, "post_norm_g"),
        "sinks": small_step(sinks, g_sinks, m_sinks, v_sinks, "sinks"),
        "gmlp_ln_g": small_step(gmlp_ln_g, g_lg, m_gmlp_ln_g, v_gmlp_ln_g, "gmlp_ln_g"),
        "gmlp_ln_b": small_step(gmlp_ln_b, g_lb, m_gmlp_ln_b, v_gmlp_ln_b, "gmlp_ln_b"),
        "gmlp_w_s": small_step(gmlp_w_s, g_ws, m_gmlp_w_s, v_gmlp_w_s, "gmlp_w_s"),
        "gmlp_b_s": small_step(gmlp_b_s, g_bs, m_gmlp_b_s, v_gmlp_b_s, "gmlp_b_s"),
    }
    res.update(big)
    order = ["rel_bias", "w_ada", "b_ada", "pre_norm_g", "post_norm_g", "w_ffn1_in", "w_ffn1_out", "w_in", "sinks",
             "gmlp_ln_g", "gmlp_ln_b", "gmlp_w_s", "gmlp_b_s", "w_br_attn", "w_br_gmlp", "w_out", "w_ffn2_in",
             "w_ffn2_out"]
    outs = [loss, grad_x[None]]
    for k in range(4):
        outs += [res[nm][k] for nm in order]
    return tuple(outs)
```

```python
import functools
import math

import jax
import jax.numpy as jnp
from jax import lax
from jax.experimental import pallas as pl
from jax.experimental.pallas import tpu as pltpu

F32 = jnp.float32
BF = jnp.bfloat16

N_DEV = 8
D = 1024
D_FF = 2816
FS = D_FF // 4
N_HEADS = 8
N_KV = 2
GROUP = 4
HD = 64
BLK = 128
Q_W = 512
KV_W = 128
G_W = 512
QKV_W = Q_W + 2 * KV_W
ZG_OFF = QKV_W
GATE_OFF = ZG_OFF + 2 * G_W
IN_W = GATE_OFF + 2 * D
N_BUCKETS = 32
MAX_DISTANCE = 128
EPS = 1e-6
NEG = -1e30
SCALE = HD ** -0.5
ADA_W = 9 * D // N_DEV

ADAM_LR = 0.001
ADAM_B1 = 0.9
ADAM_B2 = 0.999
ADAM_EPS = 1e-08
ADAM_WD = 0.01
ADAM_STEP = 10

CHUNK = 256
MIB = 1024 * 1024
MESH = pl.DeviceIdType.MESH
HIGH = lax.Precision.HIGHEST


def _cp(n_grid, vmem_mib):
    return pltpu.CompilerParams(dimension_semantics=("arbitrary",) * n_grid,
                                vmem_limit_bytes=vmem_mib * MIB)


def _const(shape):
    return pl.BlockSpec(shape, lambda *_: (0,) * len(shape))


def _resident(shape):
    return pl.BlockSpec(shape, lambda *_: (0,) * len(shape), pipeline_mode=pl.Buffered(1))


def _sds(shape, dtype):
    return jax.ShapeDtypeStruct(shape, dtype)


def _dot(a, b):
    return jnp.dot(a, b, preferred_element_type=F32)


def _dot_nt(a, b):
    return lax.dot_general(a, b, (((1,), (1,)), ((), ())), preferred_element_type=F32)


def _dot_tn(a, b):
    return lax.dot_general(a, b, (((0,), (0,)), ((), ())), preferred_element_type=F32)


def _rms_r(x):
    return lax.rsqrt(jnp.mean(x * x, axis=-1, keepdims=True) + EPS)


def _colsum(x):
    return jnp.sum(x, axis=0, keepdims=True)


def _prenorm(x, gp, sc, sh):
    return (x * _rms_r(x) * gp) * (1.0 + sc) + sh


def _prenorm_bwd(dn, x, gp, sc):
    r = _rms_r(x)
    xh = x * r
    t = dn * (1.0 + sc) * gp
    dx = r * (t - xh * jnp.mean(t * xh, axis=-1, keepdims=True))
    return dx, _colsum(dn), _colsum(dn * xh * gp), _colsum(dn * (1.0 + sc) * xh)


def _postnorm_bwd(dh, y, gate, gp, res):
    r = _rms_r(y)
    yh = y * r
    dyn = (res * gate) * dh
    t = dyn * gp
    dy = r * (t - yh * jnp.mean(t * yh, axis=-1, keepdims=True))
    return dy, _colsum(res * dh * yh * gp), _colsum(dyn * yh)


def _gelu(x):
    k = math.sqrt(2.0 / math.pi)
    return 0.5 * x * (1.0 + jnp.tanh(k * (x + 0.044715 * x * x * x)))


def _gelu_grad(x):
    k = math.sqrt(2.0 / math.pi)
    t = jnp.tanh(k * (x + 0.044715 * x * x * x))
    return 0.5 * (1.0 + t) + 0.5 * x * (1.0 - t * t) * (k * (1.0 + 3.0 * 0.044715 * x * x))


def _my_place():
    x, y, c = lax.axis_index("x"), lax.axis_index("y"), lax.axis_index("c")
    return x, y, c, 4 * x + 2 * y + c


def _peer(x, y, c, k):
    px = 1 - x if k & 4 else x
    py = 1 - y if k & 2 else y
    pc = 1 - c if k & 1 else c
    return (px, py, pc), 4 * px + 2 * py + pc


HBM_SPEC = pl.BlockSpec(memory_space=pltpu.HBM)
SEM_SPEC = pl.BlockSpec(memory_space=pltpu.SEMAPHORE)
EFFECT = pltpu.SideEffectType.DATAFLOW_SIDE_EFFECTING


def _slab_copies(exchange, srcs, lands, send, recv, loc):
    x, y, c, me = _my_place()
    remote, local = [], []
    for t in range(len(srcs)):
        for k in range(1, N_DEV):
            peer, peer_lin = _peer(x, y, c, k)
            remote.append(pltpu.make_async_remote_copy(
                src_ref=srcs[t].at[peer_lin] if exchange else srcs[t], dst_ref=lands[t].at[me],
                send_sem=send.at[t * 7 + k - 1], recv_sem=recv.at[t * 7 + k - 1],
                device_id=peer, device_id_type=MESH))
        local.append(pltpu.make_async_copy(srcs[t].at[me] if exchange else srcs[t], lands[t].at[me], loc.at[t]))
    return remote, local


def _slabs_start(exchange, arrays, after, name):
    n = len(arrays)
    land_shapes = [a.shape if exchange else (N_DEV,) + a.shape for a in arrays]

    def body(*refs):
        srcs, lands = refs[:n], refs[n:2 * n]
        send, recv, loc = refs[2 * n + 1:2 * n + 4]
        remote, local = _slab_copies(exchange, srcs, lands, send, recv, loc)
        for cp in remote + local:
            cp.start()
        refs[-1][...] = jnp.zeros_like(refs[-1])

    return pl.pallas_call(
        body, name=name,
        out_shape=(pltpu.SemaphoreType.DMA((7 * n,)), pltpu.SemaphoreType.DMA((7 * n,)),
                   pltpu.SemaphoreType.DMA((n,)),
                   *[pltpu.HBM(a.shape, a.dtype) for a in arrays],
                   *[pltpu.HBM(s, a.dtype) for s, a in zip(land_shapes, arrays)],
                   _sds((1, D), F32)),
        in_specs=[HBM_SPEC] * (2 * n) + [pl.BlockSpec(memory_space=pl.ANY)],
        out_specs=(SEM_SPEC, SEM_SPEC, SEM_SPEC, *[HBM_SPEC] * (2 * n), pl.BlockSpec(memory_space=pltpu.VMEM)),
        input_output_aliases={t: 3 + t for t in range(2 * n)},
        compiler_params=pltpu.CompilerParams(has_side_effects=EFFECT),
    )(*[pltpu.with_memory_space_constraint(a, pltpu.HBM) for a in arrays],
      *[pltpu.with_memory_space_constraint(lax.empty(s, a.dtype), pltpu.HBM) for s, a in zip(land_shapes, arrays)],
      after)


def _slabs_wait(exchange, started, after, name):
    n = (len(started) - 4) // 2
    sems = started[0:3]
    thru = started[3:3 + 2 * n]

    def body(*refs):
        srcs, lands = refs[:n], refs[n:2 * n]
        remote, local = _slab_copies(exchange, srcs, lands, *refs[2 * n:2 * n + 3])
        for cp in remote:
            cp.wait_send()
            cp.wait_recv()
        for cp in local:
            cp.wait()

    res = pl.pallas_call(
        body, name=name,
        out_shape=tuple(pltpu.HBM(a.shape, a.dtype) for a in thru),
        in_specs=[HBM_SPEC] * (2 * n) + [SEM_SPEC] * 3 + [pl.BlockSpec(memory_space=pl.ANY)],
        out_specs=tuple([HBM_SPEC] * (2 * n)),
        input_output_aliases={t: t for t in range(2 * n)},
        compiler_params=pltpu.CompilerParams(has_side_effects=EFFECT),
    )(*thru, *sems, after)
    return list(res[n:2 * n])


def _ada_forward(small8, w_ada, b_ada64):
    sw = small8.shape[1]

    def body(sm_ref, w_ref, b_ref, gath_ref, ada_ref, part_ref, send1, recv1, send2, recv2):
        x, y, c, me = _my_place()
        row_me = pl.multiple_of(me * 8, 8)
        gath_ref[pl.ds(row_me, 8), :] = sm_ref[...]
        first = []
        for k in range(1, N_DEV):
            peer, _ = _peer(x, y, c, k)
            cp = pltpu.make_async_remote_copy(
                src_ref=sm_ref, dst_ref=gath_ref.at[pl.ds(row_me, 8), :], send_sem=send1.at[k - 1],
                recv_sem=recv1.at[k - 1], device_id=peer, device_id_type=MESH)
            cp.start()
            first.append(cp)
        for cp in first:
            cp.wait()
        cs = gath_ref[:, 0:D]
        cs = cs * jax.nn.sigmoid(cs)
        part_ref[...] = jnp.dot(cs, w_ref[...], preferred_element_type=F32, precision=HIGH)
        ada_ref[pl.ds(row_me, 8), :] = part_ref[pl.ds(row_me, 8), :]
        second = []
        for k in range(1, N_DEV):
            peer, peer_lin = _peer(x, y, c, k)
            cp = pltpu.make_async_remote_copy(
                src_ref=part_ref.at[pl.ds(pl.multiple_of(peer_lin * 8, 8), 8), :],
                dst_ref=ada_ref.at[pl.ds(row_me, 8), :], send_sem=send2.at[k - 1],
                recv_sem=recv2.at[k - 1], device_id=peer, device_id_type=MESH)
            cp.start()
            second.append(cp)
        for cp in second:
            cp.wait()
        ada_ref[...] = ada_ref[...] + b_ref[...]

    vm = pl.BlockSpec(memory_space=pltpu.VMEM)
    return pl.pallas_call(
        body, name="ada_forward",
        out_shape=[_sds((8 * N_DEV, sw), F32), _sds((8 * N_DEV, ADA_W), F32)],
        in_specs=[vm, vm, vm], out_specs=[vm, vm],
        scratch_shapes=[pltpu.VMEM((8 * N_DEV, ADA_W), F32)] + [pltpu.SemaphoreType.DMA((7,))] * 4,
        compiler_params=pltpu.CompilerParams(vmem_limit_bytes=32 * MIB),
    )(small8, w_ada, b_ada64)


def _small_allreduce(pack):
    rows = pack.shape[0]

    def body(p_ref, sum_ref, gath_ref, send, recv):
        x, y, c, me = _my_place()
        gath_ref[me] = p_ref[...]
        cps = []
        for k in range(1, N_DEV):
            peer, _ = _peer(x, y, c, k)
            cp = pltpu.make_async_remote_copy(
                src_ref=p_ref, dst_ref=gath_ref.at[me], send_sem=send.at[k - 1],
                recv_sem=recv.at[k - 1], device_id=peer, device_id_type=MESH)
            cp.start()
            cps.append(cp)
        for cp in cps:
            cp.wait()
        acc = gath_ref[0]
        for j in range(1, N_DEV):
            acc = acc + gath_ref[j]
        sum_ref[...] = acc

    vm = pl.BlockSpec(memory_space=pltpu.VMEM)
    return pl.pallas_call(
        body, name="small_allreduce",
        out_shape=[_sds((rows, 128), F32), _sds((N_DEV, rows, 128), F32)],
        in_specs=[vm], out_specs=[vm, vm],
        scratch_shapes=[pltpu.SemaphoreType.DMA((7,)), pltpu.SemaphoreType.DMA((7,))],
        compiler_params=pltpu.CompilerParams(vmem_limit_bytes=40 * MIB),
    )(pack)


def _ffn_in(h, sh, sc, gp, wt8, name):
    S = h.shape[0]
    R = min(512, S)

    def body(h_ref, sh_ref, sc_ref, gp_ref, w_ref, n_ref, dg_ref, sl_ref, a_ref):
        for r0 in range(0, R, CHUNK):
            rows = slice(r0, r0 + CHUNK)
            n = _prenorm(h_ref[rows, :], gp_ref[...], sc_ref[...], sh_ref[...]).astype(BF)
            n_ref[rows, :] = n
            for s in range(4):
                g = _dot_nt(n, w_ref[s])
                u = _dot_nt(n, w_ref[s + 4])
                sg = jax.nn.sigmoid(g)
                silu = g * sg
                dg_ref[s, rows, :] = (u * (sg * (1.0 + g * (1.0 - sg)))).astype(BF)
                sl_ref[s, rows, :] = silu.astype(BF)
                a_ref[s, rows, :] = (silu * u).astype(BF)

    vec = _const((1, D))
    row = pl.BlockSpec((R, D), lambda i: (i, 0))
    blk = pl.BlockSpec((4, R, FS), lambda i: (0, i, 0))
    return pl.pallas_call(
        body, name=name, grid=(S // R,),
        out_shape=[_sds((S, D), BF)] + [_sds((4, S, FS), BF)] * 3,
        in_specs=[row, vec, vec, vec, _resident((N_DEV, FS, D))],
        out_specs=[row, blk, blk, blk],
        compiler_params=_cp(1, 56),
    )(h, sh, sc, gp, wt8)


def _ffn_out(a, w4, h, gate, gp, name, target=None):
    S = h.shape[0]
    R = min(512, S)
    with_loss = target is not None

    def body(a_ref, w_ref, h_ref, gate_ref, gp_ref, *rest):
        if with_loss:
            t_ref, out_ref, y_ref, tot_ref = rest

            @pl.when(pl.program_id(0) == 0)
            def _():
                tot_ref[...] = jnp.zeros_like(tot_ref)
        else:
            out_ref, y_ref = rest
        for r0 in range(0, R, CHUNK):
            rows = slice(r0, r0 + CHUNK)
            y = _dot(a_ref[0, rows, :], w_ref[0])
            for s in range(1, 4):
                y = y + _dot(a_ref[s, rows, :], w_ref[s])
            y_ref[rows, :] = y
            hn = h_ref[rows, :] + (0.5 * gate_ref[...]) * (y * _rms_r(y) * gp_ref[...])
            if with_loss:
                e = hn - t_ref[rows, :]
                out_ref[rows, :] = e * (1.0 / D)
                tot_ref[...] += jnp.sum(jnp.sum(e * e, axis=1, keepdims=True), axis=0, keepdims=True)
            else:
                out_ref[rows, :] = hn

    vec = _const((1, D))
    row = pl.BlockSpec((R, D), lambda i: (i, 0))
    return pl.pallas_call(
        body, name=name, grid=(S // R,),
        out_shape=[_sds((S, D), F32), _sds((S, D), F32)] + ([_sds((1, 1), F32)] if with_loss else []),
        in_specs=[pl.BlockSpec((4, R, FS), lambda i: (0, i, 0)), _resident((4, FS, D)), row, vec, vec]
        + ([row] if with_loss else []),
        out_specs=[row, row] + ([_const((1, 1))] if with_loss else []),
        compiler_params=_cp(1, 48),
    )(*((a, w4, h, gate, gp) + ((target,) if with_loss else ())))


def _ffn_out_bwd(dh, y, dsilu_u, silu, w4, gate, gp, name):
    S = dh.shape[0]
    R = min(512, S)

    def body(dh_ref, y_ref, g_ref, u_ref, w_ref, gate_ref, gp_ref, dy_ref, dgu_ref, dgate_ref, dgp_ref):
        @pl.when(pl.program_id(0) == 0)
        def _():
            dgate_ref[...] = jnp.zeros_like(dgate_ref)
            dgp_ref[...] = jnp.zeros_like(dgp_ref)
        for r0 in range(0, R, CHUNK):
            rows = slice(r0, r0 + CHUNK)
            dy, dgate, dgp = _postnorm_bwd(dh_ref[rows, :], y_ref[rows, :], gate_ref[...], gp_ref[...], 0.5)
            dgate_ref[...] += dgate
            dgp_ref[...] += dgp
            dyb = dy.astype(BF)
            dy_ref[rows, :] = dyb
            for s in range(4):
                da = _dot_nt(dyb, w_ref[s])
                dgu_ref[s, rows, :] = (da * g_ref[s, rows, :].astype(F32)).astype(BF)
                dgu_ref[s + 4, rows, :] = (da * u_ref[s, rows, :].astype(F32)).astype(BF)

    vec = _const((1, D))
    row = pl.BlockSpec((R, D), lambda i: (i, 0))
    blk4 = pl.BlockSpec((4, R, FS), lambda i: (0, i, 0))
    return pl.pallas_call(
        body, name=name, grid=(S // R,),
        out_shape=[_sds((S, D), BF), _sds((8, S, FS), BF), _sds((1, D), F32), _sds((1, D), F32)],
        in_specs=[row, row, blk4, blk4, _resident((4, FS, D)), vec, vec],
        out_specs=[row, pl.BlockSpec((8, R, FS), lambda i: (0, i, 0)), vec, vec],
        compiler_params=_cp(1, 56),
    )(dh, y, dsilu_u, silu, w4, gate, gp)


def _ffn_dn(dgu, wt8, h, dh, sc, gp, name):
    S = h.shape[0]
    R = min(512, S)

    def body(dgu_ref, w_ref, h_ref, dh_ref, sc_ref, gp_ref, out_ref, dsh_ref, dsc_ref, dgp_ref):
        @pl.when(pl.program_id(0) == 0)
        def _():
            dsh_ref[...] = jnp.zeros_like(dsh_ref)
            dsc_ref[...] = jnp.zeros_like(dsc_ref)
            dgp_ref[...] = jnp.zeros_like(dgp_ref)

        for r0 in range(0, R, CHUNK):
            rows = slice(r0, r0 + CHUNK)
            dn = _dot(dgu_ref[0, rows, :], w_ref[0])
            for j in range(1, N_DEV):
                dn = dn + _dot(dgu_ref[j, rows, :], w_ref[j])
            dx, dsh, dsc, dgp = _prenorm_bwd(dn, h_ref[rows, :], gp_ref[...], sc_ref[...])
            out_ref[rows, :] = dh_ref[rows, :] + dx
            dsh_ref[...] += dsh
            dsc_ref[...] += dsc
            dgp_ref[...] += dgp

    vec = _const((1, D))
    row = pl.BlockSpec((R, D), lambda i: (i, 0))
    return pl.pallas_call(
        body, name=name, grid=(S // R,),
        out_shape=[_sds((S, D), F32)] + [_sds((1, D), F32)] * 3,
        in_specs=[pl.BlockSpec((N_DEV, R, FS), lambda i: (0, i, 0)), _resident((N_DEV, FS, D)),
                  row, row, vec, vec],
        out_specs=[row, vec, vec, vec],
        compiler_params=_cp(1, 56),
    )(dgu, wt8, h, dh, sc, gp)


def _tn_matmul(a, b, name):
    a3 = a if a.ndim == 3 else a[None]
    b3 = b if b.ndim == 3 else b[None]
    GA, S, M = a3.shape
    GB, _, N = b3.shape
    ts = min(1024, S)
    nk = S // ts
    chunks = [(m0, min(CHUNK, M - m0)) for m0 in range(0, M, CHUNK)]

    def body(a_ref, b_ref, o_ref, acc):
        k = pl.program_id(2)

        @pl.when(k == 0)
        def _():
            acc[...] = jnp.zeros_like(acc)

        for m0, mc in chunks:
            acc[m0:m0 + mc, :] += _dot_tn(a_ref[:, m0:m0 + mc], b_ref[...])

        @pl.when(k == nk - 1)
        def _():
            for m0, mc in chunks:
                o_ref[m0:m0 + mc, :] = acc[m0:m0 + mc, :].astype(BF)

    return pl.pallas_call(
        body, name=name, grid=(GA, GB, nk),
        out_shape=_sds((GA, GB, M, N), BF),
        in_specs=[pl.BlockSpec((None, ts, M), lambda ga, gb, k: (ga, k, 0)),
                  pl.BlockSpec((None, ts, N), lambda ga, gb, k: (gb, k, 0))],
        out_specs=pl.BlockSpec((None, None, M, N), lambda ga, gb, k: (ga, gb, 0, 0)),
        scratch_shapes=[pltpu.VMEM((M, N), F32)],
        compiler_params=_cp(3, 48),
    )(a3, b3)


def _mix_in(h, sh, sc, gp, w):
    S = h.shape[0]
    R = min(512, S)

    def body(h_ref, sh_ref, sc_ref, gp_ref, w_ref, n_ref, qkv_ref, zg_ref, gates_ref):
        for r0 in range(0, R, CHUNK):
            rows = slice(r0, r0 + CHUNK)
            nb = _prenorm(h_ref[rows, :], gp_ref[...], sc_ref[...], sh_ref[...]).astype(BF)
            n_ref[rows, :] = nb
            qkv_ref[rows, :] = _dot_nt(nb, w_ref[0:ZG_OFF, :]).astype(BF)
            zg_ref[rows, :] = _dot_nt(nb, w_ref[ZG_OFF:GATE_OFF, :]).astype(BF)
            gates_ref[rows, :] = jax.nn.sigmoid(_dot_nt(nb, w_ref[GATE_OFF:IN_W, :])).astype(BF)

    vec = _const((1, D))
    rows = lambda w_: pl.BlockSpec((R, w_), lambda i: (i, 0))
    return pl.pallas_call(
        body, name="mix_in", grid=(S // R,),
        out_shape=[_sds((S, D), BF), _sds((S, QKV_W), BF), _sds((S, 2 * G_W), BF), _sds((S, 2 * D), BF)],
        in_specs=[rows(D), vec, vec, vec, _resident((IN_W, D))],
        out_specs=[rows(D), rows(QKV_W), rows(2 * G_W), rows(2 * D)],
        compiler_params=_cp(1, 48),
    )(h, sh, sc, gp, w)


def _bias_table(rel_bias, bucket):
    def body(rel_ref, bk_ref, out_ref):
        bk = bk_ref[...]
        qi = lax.broadcasted_iota(jnp.int32, (BLK, 2 * BLK), 0)
        kj = lax.broadcasted_iota(jnp.int32, (BLK, 2 * BLK), 1)
        dist = qi + BLK - kj
        window = (dist >= 0) & (dist < BLK)
        for h in range(N_HEADS):
            acc = jnp.zeros((BLK, 2 * BLK), F32)
            for b in range(N_BUCKETS):
                acc = jnp.where(bk == b, rel_ref[b, h], acc)
            out_ref[h // GROUP, pl.ds((h % GROUP) * BLK, BLK), :] = jnp.where(window, acc, NEG)

    return pl.pallas_call(
        body, name="bias_table",
        out_shape=_sds((N_KV, GROUP * BLK, 2 * BLK), F32),
        in_specs=[pl.BlockSpec(memory_space=pltpu.SMEM), pl.BlockSpec(memory_space=pltpu.VMEM)],
        out_specs=pl.BlockSpec(memory_space=pltpu.VMEM),
    )(rel_bias, bucket)


def _attn_scores(q, kvc, kvp, bias_ref, sink_ref, blk, kh):
    k2 = jnp.concatenate([kvp[:, kh * HD:(kh + 1) * HD], kvc[:, kh * HD:(kh + 1) * HD]], axis=0)
    v2 = jnp.concatenate([kvp[:, KV_W + kh * HD:KV_W + (kh + 1) * HD],
                          kvc[:, KV_W + kh * HD:KV_W + (kh + 1) * HD]], axis=0)
    q4 = jnp.concatenate([q[:, (kh * GROUP + g) * HD:(kh * GROUP + g + 1) * HD] for g in range(GROUP)], axis=0)
    s = _dot_nt(q4, k2) * SCALE + bias_ref[kh]
    col = lax.broadcasted_iota(jnp.int32, (GROUP * BLK, 2 * BLK), 1)
    s = jnp.where((col >= BLK) | (blk > 0), s, NEG)
    rowg = lax.broadcasted_iota(jnp.int32, (GROUP * BLK, 1), 0) // BLK
    sink = jnp.zeros((GROUP * BLK, 1), F32)
    for g in range(GROUP):
        sink = jnp.where(rowg == g, sink_ref[kh * GROUP + g], sink)
    return q4, k2, v2, s, sink


def _attn_fwd(qkv, bias, sinks):
    S = qkv.shape[0]
    nb = S // BLK

    def body(sink_ref, q_ref, kvc_ref, kvp_ref, bias_ref, o_ref):
        blk = pl.program_id(0)
        q, kvc, kvp = q_ref[...], kvc_ref[...], kvp_ref[...]
        outs = []
        for kh in range(N_KV):
            q4, k2, v2, s, sink = _attn_scores(q, kvc, kvp, bias_ref, sink_ref, blk, kh)
            m = jnp.maximum(jnp.max(s, axis=1, keepdims=True), sink)
            p = jnp.exp(s - m)
            denom = jnp.sum(p, axis=1, keepdims=True) + jnp.exp(sink - m)
            o4 = _dot((p / denom).astype(BF), v2)
            outs += [o4[g * BLK:(g + 1) * BLK] for g in range(GROUP)]
        o_ref[...] = jnp.concatenate(outs, axis=1).astype(BF)

    return pl.pallas_call(
        body, name="attn_fwd", grid=(nb,),
        out_shape=_sds((S, Q_W), BF),
        in_specs=[pl.BlockSpec(memory_space=pltpu.SMEM),
                  pl.BlockSpec((BLK, Q_W), lambda i: (i, 0)),
                  pl.BlockSpec((BLK, 2 * KV_W), lambda i: (i, 2)),
                  pl.BlockSpec((BLK, 2 * KV_W), lambda i: (jnp.maximum(i - 1, 0), 2)),
                  _const((N_KV, GROUP * BLK, 2 * BLK))],
        out_specs=pl.BlockSpec((BLK, Q_W), lambda i: (i, 0)),
        compiler_params=_cp(1, 32),
    )(sinks, qkv, qkv, qkv, bias)


def _attn_bwd(qkv, bias, sinks, do):
    S = qkv.shape[0]
    nb = S // BLK

    def body(sink_ref, q_ref, kvc_ref, kvp_ref, bias_ref, do_ref, dq_ref, dkv_ref, dbias_ref, dsink_ref, carry):
        i = pl.program_id(0)
        blk = nb - 1 - i

        @pl.when(i == 0)
        def _():
            carry[...] = jnp.zeros_like(carry)
            dbias_ref[...] = jnp.zeros_like(dbias_ref)
            dsink_ref[...] = jnp.zeros_like(dsink_ref)

        q, kvc, kvp, do_ = q_ref[...], kvc_ref[...], kvp_ref[...], do_ref[...]
        dqs, dk_cur, dv_cur, dk_prev, dv_prev = [], [], [], [], []
        for kh in range(N_KV):
            q4, k2, v2, s, sink = _attn_scores(q, kvc, kvp, bias_ref, sink_ref, blk, kh)
            m = jnp.maximum(jnp.max(s, axis=1, keepdims=True), sink)
            p = jnp.exp(s - m)
            denom = jnp.sum(p, axis=1, keepdims=True) + jnp.exp(sink - m)
            prob = p / denom
            p_sink = jnp.exp(sink - m) / denom
            pb = prob.astype(BF)
            do4 = jnp.concatenate(
                [do_[:, (kh * GROUP + g) * HD:(kh * GROUP + g + 1) * HD] for g in range(GROUP)], axis=0)
            dp = _dot_nt(do4, v2)
            o4 = _dot(pb, v2)
            delta = jnp.sum(do4.astype(F32) * o4, axis=1, keepdims=True)
            ds = prob * (dp - delta)
            dbias_ref[kh] += ds
            sink_term = p_sink * delta
            for g in range(GROUP):
                h = kh * GROUP + g
                val = -jnp.sum(sink_term[g * BLK:(g + 1) * BLK], axis=0, keepdims=True)
                dsink_ref[pl.ds(h, 1), :] += jnp.broadcast_to(val, (1, 128))
            dsb = ds.astype(BF)
            dq4 = _dot(dsb, k2) * SCALE
            dk2 = _dot_tn(dsb, q4) * SCALE
            dv2 = _dot_tn(pb, do4)
            dqs += [dq4[g * BLK:(g + 1) * BLK] for g in range(GROUP)]
            dk_prev.append(dk2[0:BLK])
            dk_cur.append(dk2[BLK:2 * BLK])
            dv_prev.append(dv2[0:BLK])
            dv_cur.append(dv2[BLK:2 * BLK])
        dq_ref[...] = jnp.concatenate(dqs, axis=1).astype(BF)
        dkv_ref[...] = (jnp.concatenate(dk_cur + dv_cur, axis=1) + carry[...]).astype(BF)
        carry[...] = jnp.concatenate(dk_prev + dv_prev, axis=1)

    return pl.pallas_call(
        body, name="attn_bwd", grid=(nb,),
        out_shape=[_sds((S, Q_W), BF), _sds((S, 2 * KV_W), BF),
                   _sds((N_KV, GROUP * BLK, 2 * BLK), F32), _sds((N_HEADS, 128), F32)],
        in_specs=[pl.BlockSpec(memory_space=pltpu.SMEM),
                  pl.BlockSpec((BLK, Q_W), lambda i: (nb - 1 - i, 0)),
                  pl.BlockSpec((BLK, 2 * KV_W), lambda i: (nb - 1 - i, 2)),
                  pl.BlockSpec((BLK, 2 * KV_W), lambda i: (jnp.maximum(nb - 2 - i, 0), 2)),
                  _const((N_KV, GROUP * BLK, 2 * BLK)),
                  pl.BlockSpec((BLK, Q_W), lambda i: (nb - 1 - i, 0))],
        out_specs=[pl.BlockSpec((BLK, Q_W), lambda i: (nb - 1 - i, 0)),
                   pl.BlockSpec((BLK, 2 * KV_W), lambda i: (nb - 1 - i, 0)),
                   _const((N_KV, GROUP * BLK, 2 * BLK)), _const((N_HEADS, 128))],
        scratch_shapes=[pltpu.VMEM((BLK, 2 * KV_W), F32)],
        compiler_params=_cp(1, 32),
    )(sinks, qkv, qkv, qkv, bias, do)


def _rel_bias_grad(dbias, bucket):
    def body(db_ref, bk_ref, out_ref):
        bk = bk_ref[...]
        lane = lax.broadcasted_iota(jnp.int32, (1, 128), 1)
        for h in range(N_HEADS):
            d = db_ref[h // GROUP, pl.ds((h % GROUP) * BLK, BLK), :]
            row = jnp.zeros((1, 128), F32)
            for b in range(N_BUCKETS):
                tot = jnp.sum(jnp.sum(jnp.where(bk == b, d, 0.0), axis=1, keepdims=True), axis=0, keepdims=True)
                row = jnp.where(lane == b, tot, row)
            out_ref[pl.ds(h, 1), :] = row

    vm = pl.BlockSpec(memory_space=pltpu.VMEM)
    return pl.pallas_call(body, name="rel_bias_grad", out_shape=_sds((N_HEADS, 128), F32),
                          in_specs=[vm, vm], out_specs=vm)(dbias, bucket)


def _gmlp_parts(zg_ref, lg_ref, lb_ref):
    z = zg_ref[...].astype(F32)
    ge = _gelu(z)
    u, vg = ge[:, 0:G_W], ge[:, G_W:2 * G_W]
    mu = jnp.mean(vg, axis=-1, keepdims=True)
    xc = vg - mu
    rstd = lax.rsqrt(jnp.mean(xc * xc, axis=-1, keepdims=True) + EPS)
    xh = xc * rstd
    return z, u, xh, rstd, xh * lg_ref[...] + lb_ref[...]


def _causal_weights(ws_ref, wc):
    t = lax.broadcasted_iota(jnp.int32, (BLK, BLK), 0)
    s = lax.broadcasted_iota(jnp.int32, (BLK, BLK), 1)
    for g in range(N_HEADS):
        wc[g] = jnp.where(s <= t, ws_ref[g], 0.0).astype(BF)


def _spatial(vb, wc, bst_ref, p, low):
    xp = vb[:, p * 128:(p + 1) * 128]
    s0 = _dot(wc[2 * p], xp) + bst_ref[:, 2 * p:2 * p + 1]
    s1 = _dot(wc[2 * p + 1], xp) + bst_ref[:, 2 * p + 1:2 * p + 2]
    return xp, jnp.where(low, s0, s1)


def _gmlp_fwd(zg, lg, lb, ws, bst):
    S = zg.shape[0]

    def body(zg_ref, lg_ref, lb_ref, ws_ref, bst_ref, o_ref, wc):
        @pl.when(pl.program_id(0) == 0)
        def _():
            _causal_weights(ws_ref, wc)
        _, u, _, _, vln = _gmlp_parts(zg_ref, lg_ref, lb_ref)
        vb = vln.astype(BF)
        low = lax.broadcasted_iota(jnp.int32, (BLK, 128), 1) < HD
        for p in range(4):
            _, sp = _spatial(vb, wc, bst_ref, p, low)
            o_ref[:, p * 128:(p + 1) * 128] = (u[:, p * 128:(p + 1) * 128] * sp).astype(BF)

    return pl.pallas_call(
        body, name="gmlp_fwd", grid=(S // BLK,),
        out_shape=_sds((S, G_W), BF),
        in_specs=[pl.BlockSpec((BLK, 2 * G_W), lambda i: (i, 0)), _const((1, G_W)), _const((1, G_W)),
                  _const((N_HEADS, BLK, BLK)), _const((BLK, N_HEADS))],
        out_specs=pl.BlockSpec((BLK, G_W), lambda i: (i, 0)),
        scratch_shapes=[pltpu.VMEM((N_HEADS, BLK, BLK), BF)],
        compiler_params=_cp(1, 32),
    )(zg, lg, lb, ws, bst)


def _gmlp_bwd(zg, d_out, lg, lb, ws, bst):
    S = zg.shape[0]
    nb = S // BLK

    def body(zg_ref, d_ref, lg_ref, lb_ref, ws_ref, bst_ref, dzg_ref, dws_ref, dbs_ref, dlg_ref, dlb_ref, wc, dbacc):
        i = pl.program_id(0)

        @pl.when(i == 0)
        def _():
            _causal_weights(ws_ref, wc)
            dws_ref[...] = jnp.zeros_like(dws_ref)
            dlg_ref[...] = jnp.zeros_like(dlg_ref)
            dlb_ref[...] = jnp.zeros_like(dlb_ref)
            dbacc[...] = jnp.zeros_like(dbacc)

        z, u, xh, rstd, vln = _gmlp_parts(zg_ref, lg_ref, lb_ref)
        vb = vln.astype(BF)
        d = d_ref[...].astype(F32)
        low = lax.broadcasted_iota(jnp.int32, (BLK, 128), 1) < HD
        du_parts, dvln_parts = [], []
        for p in range(4):
            xp, sp = _spatial(vb, wc, bst_ref, p, low)
            dp = d[:, p * 128:(p + 1) * 128]
            du_parts.append(dp * sp)
            dsp = dp * u[:, p * 128:(p + 1) * 128]
            dbacc[:, p * 128:(p + 1) * 128] += dsp
            d0 = jnp.where(low, dsp, 0.0).astype(BF)
            d1 = jnp.where(low, 0.0, dsp).astype(BF)
            dws_ref[2 * p] += _dot_nt(d0, xp)
            dws_ref[2 * p + 1] += _dot_nt(d1, xp)
            dvln_parts.append(_dot_tn(wc[2 * p], d0) + _dot_tn(wc[2 * p + 1], d1))
        dvln = jnp.concatenate(dvln_parts, axis=1)
        dlg_ref[...] += _colsum(dvln * xh)
        dlb_ref[...] += _colsum(dvln)
        dxh = dvln * lg_ref[...]
        dvg = rstd * (dxh - jnp.mean(dxh, axis=-1, keepdims=True)
                      - xh * jnp.mean(dxh * xh, axis=-1, keepdims=True))
        dge = jnp.concatenate(du_parts + [dvg], axis=1)
        dzg_ref[...] = (dge * _gelu_grad(z)).astype(BF)

        @pl.when(i == nb - 1)
        def _():
            t = lax.broadcasted_iota(jnp.int32, (BLK, BLK), 0)
            s = lax.broadcasted_iota(jnp.int32, (BLK, BLK), 1)
            for g in range(N_HEADS):
                dws_ref[g] = jnp.where(s <= t, dws_ref[g], 0.0)
            grp = lax.broadcasted_iota(jnp.int32, (N_HEADS, G_W), 0)
            lane = lax.broadcasted_iota(jnp.int32, (N_HEADS, G_W), 1) // HD
            pick = jnp.where(grp == lane, 1.0, 0.0).astype(F32)
            dbs_ref[...] = lax.dot_general(pick, dbacc[...], (((1,), (1,)), ((), ())),
                                           preferred_element_type=F32, precision=HIGH)

    return pl.pallas_call(
        body, name="gmlp_bwd", grid=(nb,),
        out_shape=[_sds((S, 2 * G_W), BF), _sds((N_HEADS, BLK, BLK), F32), _sds((N_HEADS, BLK), F32),
                   _sds((1, G_W), F32), _sds((1, G_W), F32)],
        in_specs=[pl.BlockSpec((BLK, 2 * G_W), lambda i: (i, 0)), pl.BlockSpec((BLK, G_W), lambda i: (i, 0)),
                  _const((1, G_W)), _const((1, G_W)), _const((N_HEADS, BLK, BLK)), _const((BLK, N_HEADS))],
        out_specs=[pl.BlockSpec((BLK, 2 * G_W), lambda i: (i, 0)), _const((N_HEADS, BLK, BLK)),
                   _const((N_HEADS, BLK)), _const((1, G_W)), _const((1, G_W))],
        scratch_shapes=[pltpu.VMEM((N_HEADS, BLK, BLK), BF), pltpu.VMEM((BLK, G_W), F32)],
        compiler_params=_cp(1, 32),
    )(zg, d_out, lg, lb, ws, bst)


def _mix_out(o, gm, gates, h, wa, wg, wo, gate, gp):
    S = h.shape[0]
    R = min(512, S)

    def body(o_ref, gm_ref, gates_ref, h_ref, wa_ref, wg_ref, wo_ref, gate_ref, gp_ref,
             ya_ref, yg_ref, ym_ref, y_ref, hn_ref):
        for r0 in range(0, R, CHUNK):
            rows = slice(r0, r0 + CHUNK)
            ya = _dot(o_ref[rows, :], wa_ref[...])
            yg = _dot(gm_ref[rows, :], wg_ref[...])
            ya_ref[rows, :] = ya.astype(BF)
            yg_ref[rows, :] = yg.astype(BF)
            ym = (gates_ref[rows, 0:D].astype(F32) * ya + gates_ref[rows, D:2 * D].astype(F32) * yg).astype(BF)
            ym_ref[rows, :] = ym
            y = _dot(ym, wo_ref[...])
            y_ref[rows, :] = y
            hn_ref[rows, :] = h_ref[rows, :] + gate_ref[...] * (y * _rms_r(y) * gp_ref[...])

    vec = _const((1, D))
    rows = lambda w_: pl.BlockSpec((R, w_), lambda i: (i, 0))
    return pl.pallas_call(
        body, name="mix_out", grid=(S // R,),
        out_shape=[_sds((S, D), BF)] * 3 + [_sds((S, D), F32)] * 2,
        in_specs=[rows(Q_W), rows(G_W), rows(2 * D), rows(D), _resident((Q_W, D)), _resident((G_W, D)),
                  _resident((D, D)), vec, vec],
        out_specs=[rows(D)] * 5,
        compiler_params=_cp(1, 48),
    )(o, gm, gates, h, wa, wg, wo, gate, gp)


def _mix_out_bwd(dh, y, ya, yg, gates, wa, wg, wo, gate, gp):
    S = dh.shape[0]
    R = min(256, S)

    def body(dh_ref, y_ref, ya_ref, yg_ref, gates_ref, wa_ref, wg_ref, wo_ref, gate_ref, gp_ref,
             dy_ref, dya_ref, dyg_ref, dz_ref, do_ref, dgm_ref, dgate_ref, dgp_ref):
        @pl.when(pl.program_id(0) == 0)
        def _():
            dgate_ref[...] = jnp.zeros_like(dgate_ref)
            dgp_ref[...] = jnp.zeros_like(dgp_ref)
        dy, dgate, dgp = _postnorm_bwd(dh_ref[...], y_ref[...], gate_ref[...], gp_ref[...], 1.0)
        dgate_ref[...] += dgate
        dgp_ref[...] += dgp
        dyb = dy.astype(BF)
        dy_ref[...] = dyb
        dym = _dot_nt(dyb, wo_ref[...])
        ga = gates_ref[:, 0:D].astype(F32)
        gg = gates_ref[:, D:2 * D].astype(F32)
        dya = (dym * ga).astype(BF)
        dyg = (dym * gg).astype(BF)
        dya_ref[...] = dya
        dyg_ref[...] = dyg
        dz_ref[:, 0:D] = (dym * ya_ref[...].astype(F32) * (ga * (1.0 - ga))).astype(BF)
        dz_ref[:, D:2 * D] = (dym * yg_ref[...].astype(F32) * (gg * (1.0 - gg))).astype(BF)
        do_ref[...] = _dot_nt(dya, wa_ref[...]).astype(BF)
        dgm_ref[...] = _dot_nt(dyg, wg_ref[...]).astype(BF)

    vec = _const((1, D))
    rows = lambda w_: pl.BlockSpec((R, w_), lambda i: (i, 0))
    return pl.pallas_call(
        body, name="mix_out_bwd", grid=(S // R,),
        out_shape=[_sds((S, D), BF)] * 3 + [_sds((S, 2 * D), BF), _sds((S, Q_W), BF), _sds((S, G_W), BF),
                                             _sds((1, D), F32), _sds((1, D), F32)],
        in_specs=[rows(D), rows(D), rows(D), rows(D), rows(2 * D), _resident((Q_W, D)), _resident((G_W, D)),
                  _resident((D, D)), vec, vec],
        out_specs=[rows(D)] * 3 + [rows(2 * D), rows(Q_W), rows(G_W), vec, vec],
        compiler_params=_cp(1, 48),
    )(dh, y, ya, yg, gates, wa, wg, wo, gate, gp)


def _mix_dn(dq, dkv, dzg, dzgate, w, h, dh, sc, gp):
    S = h.shape[0]
    R = min(512, S)

    def body(dq_ref, dkv_ref, dzg_ref, dzt_ref, w_ref, h_ref, dh_ref, sc_ref, gp_ref,
             out_ref, dsh_ref, dsc_ref, dgp_ref):
        @pl.when(pl.program_id(0) == 0)
        def _():
            dsh_ref[...] = jnp.zeros_like(dsh_ref)
            dsc_ref[...] = jnp.zeros_like(dsc_ref)
            dgp_ref[...] = jnp.zeros_like(dgp_ref)
        for r0 in range(0, R, CHUNK):
            rows = slice(r0, r0 + CHUNK)
            dn = _dot(dq_ref[rows, :], w_ref[0:Q_W, :])
            dn = dn + _dot(dkv_ref[rows, :], w_ref[Q_W:QKV_W, :])
            dn = dn + _dot(dzg_ref[rows, :], w_ref[ZG_OFF:GATE_OFF, :])
            dn = dn + _dot(dzt_ref[rows, :], w_ref[GATE_OFF:IN_W, :])
            dx, dsh, dsc, dgp = _prenorm_bwd(dn, h_ref[rows, :], gp_ref[...], sc_ref[...])
            out_ref[rows, :] = dh_ref[rows, :] + dx
            dsh_ref[...] += dsh
            dsc_ref[...] += dsc
            dgp_ref[...] += dgp

    vec = _const((1, D))
    rows = lambda w_: pl.BlockSpec((R, w_), lambda i: (i, 0))
    return pl.pallas_call(
        body, name="mix_dn", grid=(S // R,),
        out_shape=[_sds((S, D), F32)] + [_sds((1, D), F32)] * 3,
        in_specs=[rows(Q_W), rows(2 * KV_W), rows(2 * G_W), rows(2 * D), _resident((IN_W, D)),
                  rows(D), rows(D), vec, vec],
        out_specs=[rows(D), vec, vec, vec],
        compiler_params=_cp(1, 48),
    )(dq, dkv, dzg, dzgate, w, h, dh, sc, gp)


def _adamw_math(w, g, m, v):
    m2 = ADAM_B1 * m + (1.0 - ADAM_B1) * g
    v2 = ADAM_B2 * v + (1.0 - ADAM_B2) * (g * g)
    m_hat = m2 / (1.0 - ADAM_B1 ** ADAM_STEP)
    v_hat = v2 / (1.0 - ADAM_B2 ** ADAM_STEP)
    delta = -ADAM_LR * (m_hat / (jnp.sqrt(v_hat) + ADAM_EPS) + ADAM_WD * w)
    return delta, m2, v2


def _row_tile(rows, cols):
    best = None
    for t in range(16, rows + 1, 16):
        if rows % t == 0 and t * cols <= 256 * 1024:
            best = t
    return best if best is not None else rows


def _adamw_sharded(landing, w, m, v, name):
    r, c = w.shape
    tr = _row_tile(r, c)

    def body(l_ref, w_ref, m_ref, v_ref, g_ref, d_ref, m2_ref, v2_ref):
        g = l_ref[0].astype(F32)
        for j in range(1, N_DEV):
            g = g + l_ref[j].astype(F32)
        delta, m2, v2 = _adamw_math(w_ref[...], g, m_ref[...], v_ref[...])
        g_ref[...] = g
        d_ref[...] = delta
        m2_ref[...] = m2
        v2_ref[...] = v2

    row = pl.BlockSpec((tr, c), lambda i: (i, 0))
    return pl.pallas_call(
        body, name=name, grid=(r // tr,),
        out_shape=[_sds((r, c), F32)] * 4,
        in_specs=[pl.BlockSpec((N_DEV, tr, c), lambda i: (0, i, 0)), row, row, row],
        out_specs=[row] * 4,
        compiler_params=_cp(1, 48),
    )(landing, w, m, v)


def _adamw_small(w, g, m, v, name):
    def body(w_ref, g_ref, m_ref, v_ref, d_ref, m2_ref, v2_ref):
        delta, m2, v2 = _adamw_math(w_ref[...], g_ref[...], m_ref[...], v_ref[...])
        d_ref[...] = delta
        m2_ref[...] = m2
        v2_ref[...] = v2

    vm = pl.BlockSpec(memory_space=pltpu.VMEM)
    return pl.pallas_call(body, name=name, out_shape=[_sds(w.shape, F32)] * 3,
                          in_specs=[vm] * 4, out_specs=[vm] * 3)(w, g, m, v)


def _w_ada_update(c64, d_ada, w, m, v):
    tr = 256

    def body(c_ref, d_ref, w_ref, m_ref, v_ref, g_ref, dl_ref, m2_ref, v2_ref):
        i = pl.program_id(0)
        g = jnp.zeros((tr, ADA_W), F32)
        for j in range(N_DEV):
            cj = c_ref[pl.ds(8 * j, 8), :]
            sj = cj * jax.nn.sigmoid(cj)
            dj = jnp.broadcast_to(d_ref[pl.ds(j, 1), :], (8, ADA_W))
            g = g + lax.dot_general(sj, dj, (((0,), (0,)), ((), ())), preferred_element_type=F32,
                                    precision=HIGH) * 0.125
        delta, m2, v2 = _adamw_math(w_ref[...], g, m_ref[...], v_ref[...])
        g_ref[...] = g
        dl_ref[...] = delta
        m2_ref[...] = m2
        v2_ref[...] = v2

    row = pl.BlockSpec((tr, ADA_W), lambda i: (i, 0))
    return pl.pallas_call(
        body, name="w_ada_update", grid=(D // tr,),
        out_shape=[_sds((D, ADA_W), F32)] * 4,
        in_specs=[pl.BlockSpec((8 * N_DEV, tr), lambda i: (0, i)), _const((N_DEV, ADA_W)), row, row, row],
        out_specs=[row] * 4,
        compiler_params=_cp(1, 40),
    )(c64, d_ada, w, m, v)


def _t5_bucket():
    qi = jnp.arange(BLK, dtype=jnp.int32)[:, None]
    kj = jnp.arange(2 * BLK, dtype=jnp.int32)[None, :]
    dist = jnp.maximum(qi + BLK - kj, 0)
    max_exact = N_BUCKETS // 2
    d_f = jnp.maximum(dist, max_exact).astype(F32)
    large = max_exact + (jnp.log(d_f / max_exact) / math.log(MAX_DISTANCE / max_exact)
                         * (N_BUCKETS - max_exact)).astype(jnp.int32)
    large = jnp.minimum(large, N_BUCKETS - 1)
    return jnp.where(dist < max_exact, dist, large)


def _slabs_of_columns(w):
    r, c8 = w.shape
    return jnp.transpose(w.reshape(r, N_DEV, c8 // N_DEV), (1, 0, 2))


def _columns_of_slabs(w8):
    _, r, c = w8.shape
    return jnp.transpose(w8, (1, 0, 2)).reshape(r, N_DEV * c)


def kernel(x, c, rel_bias, w_ada, b_ada, pre_norm_g, post_norm_g, w_ffn1_in, w_ffn1_out, w_in, sinks, gmlp_ln_g, gmlp_ln_b, gmlp_w_s, gmlp_b_s, w_br_attn, w_br_gmlp, w_out, w_ffn2_in, w_ffn2_out, loss_target, m_rel_bias, m_w_ada, m_b_ada, m_pre_norm_g, m_post_norm_g, m_w_ffn1_in, m_w_ffn1_out, m_w_in, m_sinks, m_gmlp_ln_g, m_gmlp_ln_b, m_gmlp_w_s, m_gmlp_b_s, m_w_br_attn, m_w_br_gmlp, m_w_out, m_w_ffn2_in, m_w_ffn2_out, v_rel_bias, v_w_ada, v_b_ada, v_pre_norm_g, v_post_norm_g, v_w_ffn1_in, v_w_ffn1_out, v_w_in, v_sinks, v_gmlp_ln_g, v_gmlp_ln_b, v_gmlp_w_s, v_gmlp_b_s, v_w_br_attn, v_w_br_gmlp, v_w_out, v_w_ffn2_in, v_w_ffn2_out):
    me = 4 * lax.axis_index("x") + 2 * lax.axis_index("y") + lax.axis_index("c")
    x0 = x[0]
    target = loss_target[0]

    transposed = ("w_ffn1_in", "w_in", "w_ffn2_in")
    shards = [w_ffn1_in[0].T, w_ffn1_out[0], w_in[0].T, w_br_attn[0], w_br_gmlp[0], w_out[0],
              w_ffn2_in[0].T, w_ffn2_out[0]]
    shards_bf = [s.astype(BF) for s in shards]
    groups = [shards_bf[0:1], shards_bf[1:2], shards_bf[2:6], shards_bf[6:7], shards_bf[7:8]]

    def gather_start(i, after):
        return _slabs_start(False, groups[i], after, "gather_start_%d" % i)

    def gathered(st, i, after):
        return _slabs_wait(False, st, after, "gather_wait_%d" % i)

    gs0 = gather_start(0, c)

    small = jnp.concatenate([c[0], pre_norm_g[0].reshape(-1), post_norm_g[0].reshape(-1)])
    small8 = jnp.broadcast_to(small[None, :], (8, small.shape[0]))
    b_ada64 = jnp.repeat(b_ada.reshape(N_DEV, ADA_W), 8, axis=0)
    gath, ada64 = _ada_forward(small8, w_ada[0], b_ada64)
    gath8 = gath[::8]
    ada = ada64[::8].reshape(9, D)
    sh1, sc1, g1, sh2, sc2, g2, sh3, sc3, g3 = [ada[k:k + 1] for k in range(9)]
    gains = gath8[:, D:].reshape(N_DEV, 2, 3, 128)
    pre_g = jnp.transpose(gains[:, 0], (1, 0, 2)).reshape(3, D)
    post_g = jnp.transpose(gains[:, 1], (1, 0, 2)).reshape(3, D)
    pre = [pre_g[k:k + 1] for k in range(3)]
    post = [post_g[k:k + 1] for k in range(3)]

    bucket = _t5_bucket()
    bias = _bias_table(rel_bias, bucket)
    sinks8 = sinks[0]
    lg, lb = gmlp_ln_g, gmlp_ln_b
    ws = gmlp_w_s[0]
    bst = jnp.transpose(gmlp_b_s[0])

    (wf1_in,) = gathered(gs0, 0, sh1)
    gs1 = gather_start(1, wf1_in)
    gs2 = gather_start(2, wf1_in)
    n1, fg1, fu1, fa1 = _ffn_in(x0, sh1 + gs1[-1] + gs2[-1], sc1, pre[0], wf1_in, "ffn1_in")
    wf1_out = gathered(gs1, 1, n1)[0]
    gs3 = gather_start(3, wf1_out)
    wf1_out = wf1_out.reshape(4, FS, D)
    h1, y1 = _ffn_out(fa1, wf1_out, x0, g1 + gs3[-1], post[0], "ffn1_out")
    mix_w = gathered(gs2, 2, h1)
    gs4 = gather_start(4, mix_w[0])
    w_in_full = mix_w[0].reshape(IN_W, D)
    w_bra = _columns_of_slabs(mix_w[1])
    w_brg = _columns_of_slabs(mix_w[2])
    w_out_full = mix_w[3].reshape(D, D)
    n2, qkv, zg, gates = _mix_in(h1, sh2 + gs4[-1], sc2, pre[1], w_in_full)
    att = _attn_fwd(qkv, bias, sinks8)
    gm = _gmlp_fwd(zg, lg, lb, ws, bst)
    ya, yg, ymix, y2, h2 = _mix_out(att, gm, gates, h1, w_bra, w_brg, w_out_full, g2, post[1])
    (wf2_in,) = gathered(gs3, 3, h2)
    n3, fg3, fu3, fa3 = _ffn_in(h2, sh3, sc3, pre[2], wf2_in, "ffn2_in")
    wf2_out = gathered(gs4, 4, n3)[0].reshape(4, FS, D)
    dh3, y3, sq = _ffn_out(fa3, wf2_out, h2, g3, post[2], "ffn2_out", target=target)
    loss = lax.psum(0.5 * sq[0, 0] / D, ("x", "y", "c"))

    def exchange_start(i, arrays):
        return _slabs_start(True, arrays, sq, "exchange_start_%d" % i)

    dy3, dgu3, d_g3, d_post2 = _ffn_out_bwd(dh3, y3, fg3, fu3, wf2_out, g3, post[2], "ffn2_out_bwd")
    gw_f2_out = _tn_matmul(fa3, dy3, "ffn2_out_wgrad").reshape(N_DEV, D_FF // N_DEV, D)
    ex0 = exchange_start(0, [gw_f2_out])
    dh2, d_sh3, d_sc3, d_pre2 = _ffn_dn(dgu3, wf2_in, h2, dh3, sc3 + ex0[-1], pre[2], "ffn2_dn")
    gw_f2_in = _tn_matmul(dgu3, n3, "ffn2_in_wgrad").reshape(N_DEV, FS, D)
    ex1 = exchange_start(1, [gw_f2_in])

    dy2, dya, dyg, dzgate, d_att, d_gm, d_g2, d_post1 = _mix_out_bwd(
        dh2, y2, ya, yg, gates, w_bra, w_brg, w_out_full, g2 + ex1[-1], post[1])
    gw_out = _tn_matmul(ymix, dy2, "w_out_wgrad").reshape(N_DEV, D // N_DEV, D)
    gw_bra = _slabs_of_columns(_tn_matmul(att, dya, "w_br_attn_wgrad").reshape(Q_W, D))
    gw_brg = _slabs_of_columns(_tn_matmul(gm, dyg, "w_br_gmlp_wgrad").reshape(G_W, D))
    ex2 = exchange_start(2, [gw_bra, gw_brg, gw_out])
    dq, dkv, dbias, dsink = _attn_bwd(qkv, bias, sinks8, d_att)
    dzg, d_ws, d_bs, d_lg, d_lb = _gmlp_bwd(zg, d_gm, lg, lb, ws, bst)
    dh1, d_sh2, d_sc2, d_pre1 = _mix_dn(dq, dkv, dzg, dzgate, w_in_full, h1, dh2, sc2 + ex2[-1], pre[1])
    gw_in = jnp.concatenate(
        [_tn_matmul(dq, n2, "w_in_q_wgrad").reshape(Q_W, D),
         _tn_matmul(dkv, n2, "w_in_kv_wgrad").reshape(2 * KV_W, D),
         _tn_matmul(dzg, n2, "w_in_zg_wgrad").reshape(2 * G_W, D),
         _tn_matmul(dzgate, n2, "w_in_gate_wgrad").reshape(2 * D, D)], axis=0).reshape(N_DEV, IN_W // N_DEV, D)
    ex3 = exchange_start(3, [gw_in])

    dy1, dgu1, d_g1, d_post0 = _ffn_out_bwd(dh1, y1, fg1, fu1, wf1_out, g1 + ex3[-1], post[0], "ffn1_out_bwd")
    gw_f1_out = _tn_matmul(fa1, dy1, "ffn1_out_wgrad").reshape(N_DEV, D_FF // N_DEV, D)
    gw_f1_in = _tn_matmul(dgu1, n1, "ffn1_in_wgrad").reshape(N_DEV, FS, D)
    ex4 = exchange_start(4, [gw_f1_out, gw_f1_in])
    grad_x, d_sh1, d_sc1, d_pre0 = _ffn_dn(dgu1, wf1_in, x0, dh1, sc1 + ex4[-1], pre[0], "ffn1_dn")

    landed = {}
    for i, (ex, nms) in enumerate([(ex0, ["w_ffn2_out"]), (ex1, ["w_ffn2_in"]),
                                   (ex2, ["w_br_attn", "w_br_gmlp", "w_out"]), (ex3, ["w_in"]),
                                   (ex4, ["w_ffn1_out", "w_ffn1_in"])]):
        for nm, land in zip(nms, _slabs_wait(True, ex, grad_x, "exchange_wait_%d" % i)):
            landed[nm] = land
    moments = [(m_w_ffn1_in, v_w_ffn1_in), (m_w_ffn1_out, v_w_ffn1_out), (m_w_in, v_w_in),
               (m_w_br_attn, v_w_br_attn), (m_w_br_gmlp, v_w_br_gmlp), (m_w_out, v_w_out),
               (m_w_ffn2_in, v_w_ffn2_in), (m_w_ffn2_out, v_w_ffn2_out)]
    names = ["w_ffn1_in", "w_ffn1_out", "w_in", "w_br_attn", "w_br_gmlp", "w_out", "w_ffn2_in", "w_ffn2_out"]
    big = {}
    for nm, w_, (m_, v_) in zip(names, shards, moments):
        if nm in transposed:
            res4 = _adamw_sharded(landed[nm], w_, m_[0].T, v_[0].T, "adamw_" + nm)
            big[nm] = [a.T[None] for a in res4]
        else:
            big[nm] = [a[None] for a in _adamw_sharded(landed[nm], w_, m_[0], v_[0], "adamw_" + nm)]

    d_rel = _rel_bias_grad(dbias, bucket)
    d_ada = jnp.concatenate([v_.reshape(8, 128) for v_ in
                             (d_sh1, d_sc1, d_g1, d_sh2, d_sc2, d_g2, d_sh3, d_sc3, d_g3)], axis=0)
    d_pre = jnp.concatenate([d_pre0, d_pre1, d_pre2], axis=0)
    d_post = jnp.concatenate([d_post0, d_post1, d_post2], axis=0)
    pack = jnp.concatenate([
        d_ada,
        _slabs_of_columns(d_pre).reshape(24, 128),
        _slabs_of_columns(d_post).reshape(24, 128),
        jnp.concatenate([d_lg.reshape(4, 128), d_lb.reshape(4, 128)], axis=0),
        d_bs, d_rel, dsink,
        d_ws.reshape(N_HEADS * BLK, BLK)], axis=0)
    tot, every = _small_allreduce(pack)

    g_b_ada = tot[0:72].reshape(1, 9 * D)
    g_pre = lax.dynamic_slice_in_dim(tot[72:96], 3 * me, 3, axis=0)[None]
    g_post = lax.dynamic_slice_in_dim(tot[96:120], 3 * me, 3, axis=0)[None]
    g_lg = tot[120:124].reshape(1, G_W)
    g_lb = tot[124:128].reshape(1, G_W)
    g_bs = tot[128:136][None]
    g_rel = jnp.transpose(tot[136:144, 0:N_BUCKETS])
    g_sinks = tot[144:152, 0][None]
    g_ws = tot[152:1176].reshape(1, N_HEADS, BLK, BLK)

    d_ada_mine = lax.dynamic_slice_in_dim(every[:, 0:72].reshape(N_DEV, N_DEV, ADA_W), me, 1, axis=1)[:, 0]
    c64 = jnp.repeat(gath8[:, 0:D], 8, axis=0)
    ada_out = [a[None] for a in _w_ada_update(c64, d_ada_mine, w_ada[0], m_w_ada[0], v_w_ada[0])]

    def small_step(w_, g_, m_, v_, nm):
        shp = w_.shape
        two_d = (int(math.prod(shp[:-1])), shp[-1])
        d_, m2_, v2_ = _adamw_small(w_.reshape(two_d), g_.reshape(two_d), m_.reshape(two_d), v_.reshape(two_d),
                                    "adamw_" + nm)
        return [g_, d_.reshape(shp), m2_.reshape(shp), v2_.reshape(shp)]

    res = {
        "rel_bias": small_step(rel_bias, g_rel, m_rel_bias, v_rel_bias, "rel_bias"),
        "w_ada": ada_out,
        "b_ada": small_step(b_ada, g_b_ada, m_b_ada, v_b_ada, "b_ada"),
        "pre_norm_g": small_step(pre_norm_g, g_pre, m_pre_norm_g, v_pre_norm_g, "pre_norm_g"),
        "post_norm_g": small_step(post_norm_g, g_post, m_post_norm_g, v_post_norm_g, "post_norm_g"),
        "sinks": small_step(sinks, g_sinks, m_sinks, v_sinks, "sinks"),
        "gmlp_ln_g": small_step(gmlp_ln_g, g_lg, m_gmlp_ln_g, v_gmlp_ln_g, "gmlp_ln_g"),
        "gmlp_ln_b": small_step(gmlp_ln_b, g_lb, m_gmlp_ln_b, v_gmlp_ln_b, "gmlp_ln_b"),
        "gmlp_w_s": small_step(gmlp_w_s, g_ws, m_gmlp_w_s, v_gmlp_w_s, "gmlp_w_s"),
        "gmlp_b_s": small_step(gmlp_b_s, g_bs, m_gmlp_b_s, v_gmlp_b_s, "gmlp_b_s"),
    }
    res.update(big)
    order = ["rel_bias", "w_ada", "b_ada", "pre_norm_g", "post_norm_g", "w_ffn1_in", "w_ffn1_out", "w_in", "sinks",
             "gmlp_ln_g", "gmlp_ln_b", "gmlp_w_s", "gmlp_b_s", "w_br_attn", "w_br_gmlp", "w_out", "w_ffn2_in",
             "w_ffn2_out"]
    outs = [loss, grad_x[None]]
    for k in range(4):
        outs += [res[nm][k] for nm in order]
    return tuple(outs)
```

```python
import functools
import math

import jax
import jax.numpy as jnp
from jax import lax
from jax.experimental import pallas as pl
from jax.experimental.pallas import tpu as pltpu

F32 = jnp.float32
BF = jnp.bfloat16

N_DEV = 8
D = 1024
D_FF = 2816
FS = D_FF // 4
N_HEADS = 8
N_KV = 2
GROUP = 4
HD = 64
BLK = 128
Q_W = 512
KV_W = 128
G_W = 512
QKV_W = Q_W + 2 * KV_W
ZG_OFF = QKV_W
GATE_OFF = ZG_OFF + 2 * G_W
IN_W = GATE_OFF + 2 * D
N_BUCKETS = 32
MAX_DISTANCE = 128
EPS = 1e-6
NEG = -1e30
SCALE = HD ** -0.5
ADA_W = 9 * D // N_DEV

ADAM_LR = 0.001
ADAM_B1 = 0.9
ADAM_B2 = 0.999
ADAM_EPS = 1e-08
ADAM_WD = 0.01
ADAM_STEP = 10

CHUNK = 256
MIB = 1024 * 1024
MESH = pl.DeviceIdType.MESH
HIGH = lax.Precision.HIGHEST


def _cp(n_grid, vmem_mib):
    return pltpu.CompilerParams(dimension_semantics=("arbitrary",) * n_grid,
                                vmem_limit_bytes=vmem_mib * MIB)


def _const(shape):
    return pl.BlockSpec(shape, lambda *_: (0,) * len(shape))


def _resident(shape):
    return pl.BlockSpec(shape, lambda *_: (0,) * len(shape), pipeline_mode=pl.Buffered(1))


def _sds(shape, dtype):
    return jax.ShapeDtypeStruct(shape, dtype)


def _dot(a, b):
    return jnp.dot(a, b, preferred_element_type=F32)


def _dot_nt(a, b):
    return lax.dot_general(a, b, (((1,), (1,)), ((), ())), preferred_element_type=F32)


def _dot_tn(a, b):
    return lax.dot_general(a, b, (((0,), (0,)), ((), ())), preferred_element_type=F32)


def _rms_r(x):
    return lax.rsqrt(jnp.mean(x * x, axis=-1, keepdims=True) + EPS)


def _colsum(x):
    return jnp.sum(x, axis=0, keepdims=True)


def _prenorm(x, gp, sc, sh):
    return (x * _rms_r(x) * gp) * (1.0 + sc) + sh


def _prenorm_bwd(dn, x, gp, sc):
    r = _rms_r(x)
    xh = x * r
    t = dn * (1.0 + sc) * gp
    dx = r * (t - xh * jnp.mean(t * xh, axis=-1, keepdims=True))
    return dx, _colsum(dn), _colsum(dn * xh * gp), _colsum(dn * (1.0 + sc) * xh)


def _postnorm_bwd(dh, y, gate, gp, res):
    r = _rms_r(y)
    yh = y * r
    dyn = (res * gate) * dh
    t = dyn * gp
    dy = r * (t - yh * jnp.mean(t * yh, axis=-1, keepdims=True))
    return dy, _colsum(res * dh * yh * gp), _colsum(dyn * yh)


def _gelu(x):
    k = math.sqrt(2.0 / math.pi)
    return 0.5 * x * (1.0 + jnp.tanh(k * (x + 0.044715 * x * x * x)))


def _gelu_grad(x):
    k = math.sqrt(2.0 / math.pi)
    t = jnp.tanh(k * (x + 0.044715 * x * x * x))
    return 0.5 * (1.0 + t) + 0.5 * x * (1.0 - t * t) * (k * (1.0 + 3.0 * 0.044715 * x * x))


def _my_place():
    x, y, c = lax.axis_index("x"), lax.axis_index("y"), lax.axis_index("c")
    return x, y, c, 4 * x + 2 * y + c


def _peer(x, y, c, k):
    px = 1 - x if k & 4 else x
    py = 1 - y if k & 2 else y
    pc = 1 - c if k & 1 else c
    return (px, py, pc), 4 * px + 2 * py + pc


HBM_SPEC = pl.BlockSpec(memory_space=pltpu.HBM)
SEM_SPEC = pl.BlockSpec(memory_space=pltpu.SEMAPHORE)
EFFECT = pltpu.SideEffectType.DATAFLOW_SIDE_EFFECTING


RELATIONS = {"exchange": (1, 2, 3, 4, 5, 6, 7), "gather": (1, 2, 4, 6), "forward": (2, 4, 6),
             "gather_all": (1, 2, 3, 4, 5, 6, 7)}


def _slab_copies(mode, srcs, lands, send, recv, loc):
    x, y, c, me = _my_place()
    rel = RELATIONS[mode]
    remote, local = [], []
    for t in range(len(lands)):
        for i, k in enumerate(rel):
            peer, peer_lin = _peer(x, y, c, k)
            if mode == "exchange":
                src, dst, to = srcs[t].at[peer_lin], lands[t].at[me], peer
            elif mode in ("gather", "gather_all"):
                src, dst, to = srcs[t], lands[t].at[me], peer
            else:
                src, dst, to = lands[t].at[peer_lin], lands[t].at[peer_lin], _peer(x, y, c, 1)[0]
            remote.append(pltpu.make_async_remote_copy(
                src_ref=src, dst_ref=dst, send_sem=send.at[t * len(rel) + i], recv_sem=recv.at[t * len(rel) + i],
                device_id=to, device_id_type=MESH))
        if mode == "exchange":
            local.append(pltpu.make_async_copy(srcs[t].at[me], lands[t].at[me], loc.at[t]))
        elif mode in ("gather", "gather_all"):
            local.append(pltpu.make_async_copy(srcs[t], lands[t].at[me], loc.at[t]))
    return remote, local


def _slabs_start(mode, arrays, after, name):
    n = len(arrays)
    if mode == "forward":
        thru = list(arrays)
    else:
        shapes = [a.shape if mode == "exchange" else (N_DEV,) + a.shape for a in arrays]
        thru = list(arrays) + [lax.empty(s, a.dtype) for s, a in zip(shapes, arrays)]
    m = len(thru)
    n_sem = n * len(RELATIONS[mode])

    def body(*refs):
        srcs, lands = refs[:n], refs[m - n:m]
        send, recv, loc = refs[m + 1:m + 4]
        remote, local = _slab_copies(mode, srcs, lands, send, recv, loc)
        for cp in remote + local:
            cp.start()
        refs[-1][...] = jnp.zeros_like(refs[-1])

    return pl.pallas_call(
        body, name=name,
        out_shape=(pltpu.SemaphoreType.DMA((n_sem,)), pltpu.SemaphoreType.DMA((n_sem,)),
                   pltpu.SemaphoreType.DMA((n,)),
                   *[pltpu.HBM(a.shape, a.dtype) for a in thru],
                   _sds((1, D), F32)),
        in_specs=[HBM_SPEC] * m + [pl.BlockSpec(memory_space=pl.ANY)],
        out_specs=(SEM_SPEC, SEM_SPEC, SEM_SPEC, *[HBM_SPEC] * m, pl.BlockSpec(memory_space=pltpu.VMEM)),
        input_output_aliases={t: 3 + t for t in range(m)},
        compiler_params=pltpu.CompilerParams(has_side_effects=EFFECT),
    )(*[pltpu.with_memory_space_constraint(a, pltpu.HBM) for a in thru], after)


def _slabs_wait(mode, n, started, after, name):
    sems = started[0:3]
    thru = started[3:-1]
    m = len(thru)

    def body(*refs):
        srcs, lands = refs[:n], refs[m - n:m]
        remote, local = _slab_copies(mode, srcs, lands, *refs[m:m + 3])
        for cp in remote:
            cp.wait_send()
            cp.wait_recv()
        for cp in local:
            cp.wait()

    res = pl.pallas_call(
        body, name=name,
        out_shape=tuple(pltpu.HBM(a.shape, a.dtype) for a in thru),
        in_specs=[HBM_SPEC] * m + [SEM_SPEC] * 3 + [pl.BlockSpec(memory_space=pl.ANY)],
        out_specs=tuple([HBM_SPEC] * m),
        input_output_aliases={t: t for t in range(m)},
        compiler_params=pltpu.CompilerParams(has_side_effects=EFFECT),
    )(*thru, *sems, after)
    return list(res[m - n:m])


def _ada_forward(small8, w_ada, b_ada64):
    sw = small8.shape[1]

    def body(sm_ref, w_ref, b_ref, gath_ref, ada_ref, part_ref, send1, recv1, send2, recv2):
        x, y, c, me = _my_place()
        row_me = pl.multiple_of(me * 8, 8)
        gath_ref[pl.ds(row_me, 8), :] = sm_ref[...]
        first = []
        for k in range(1, N_DEV):
            peer, _ = _peer(x, y, c, k)
            cp = pltpu.make_async_remote_copy(
                src_ref=sm_ref, dst_ref=gath_ref.at[pl.ds(row_me, 8), :], send_sem=send1.at[k - 1],
                recv_sem=recv1.at[k - 1], device_id=peer, device_id_type=MESH)
            cp.start()
            first.append(cp)
        for cp in first:
            cp.wait()
        cs = gath_ref[:, 0:D]
        cs = cs * jax.nn.sigmoid(cs)
        part_ref[...] = jnp.dot(cs, w_ref[...], preferred_element_type=F32, precision=HIGH)
        ada_ref[pl.ds(row_me, 8), :] = part_ref[pl.ds(row_me, 8), :]
        second = []
        for k in range(1, N_DEV):
            peer, peer_lin = _peer(x, y, c, k)
            cp = pltpu.make_async_remote_copy(
                src_ref=part_ref.at[pl.ds(pl.multiple_of(peer_lin * 8, 8), 8), :],
                dst_ref=ada_ref.at[pl.ds(row_me, 8), :], send_sem=send2.at[k - 1],
                recv_sem=recv2.at[k - 1], device_id=peer, device_id_type=MESH)
            cp.start()
            second.append(cp)
        for cp in second:
            cp.wait()
        ada_ref[...] = ada_ref[...] + b_ref[...]

    vm = pl.BlockSpec(memory_space=pltpu.VMEM)
    return pl.pallas_call(
        body, name="ada_forward",
        out_shape=[_sds((8 * N_DEV, sw), F32), _sds((8 * N_DEV, ADA_W), F32)],
        in_specs=[vm, vm, vm], out_specs=[vm, vm],
        scratch_shapes=[pltpu.VMEM((8 * N_DEV, ADA_W), F32)] + [pltpu.SemaphoreType.DMA((7,))] * 4,
        compiler_params=pltpu.CompilerParams(vmem_limit_bytes=32 * MIB),
    )(small8, w_ada, b_ada64)


def _sum_slabs(land):
    def body(l_ref, o_ref):
        acc = l_ref[0]
        for j in range(1, N_DEV):
            acc = acc + l_ref[j]
        o_ref[...] = acc

    vm = pl.BlockSpec(memory_space=pltpu.VMEM)
    return pl.pallas_call(body, name="sum_slabs", out_shape=_sds(land.shape[1:], F32), in_specs=[vm], out_specs=vm,
                          compiler_params=pltpu.CompilerParams(vmem_limit_bytes=32 * MIB))(land)


def _small_allreduce(pack):
    rows = pack.shape[0]

    def body(p_ref, sum_ref, gath_ref, send, recv):
        x, y, c, me = _my_place()
        gath_ref[me] = p_ref[...]
        cps = []
        for k in range(1, N_DEV):
            peer, _ = _peer(x, y, c, k)
            cp = pltpu.make_async_remote_copy(
                src_ref=p_ref, dst_ref=gath_ref.at[me], send_sem=send.at[k - 1],
                recv_sem=recv.at[k - 1], device_id=peer, device_id_type=MESH)
            cp.start()
            cps.append(cp)
        for cp in cps:
            cp.wait()
        acc = gath_ref[0]
        for j in range(1, N_DEV):
            acc = acc + gath_ref[j]
        sum_ref[...] = acc

    vm = pl.BlockSpec(memory_space=pltpu.VMEM)
    return pl.pallas_call(
        body, name="small_allreduce",
        out_shape=[_sds((rows, 128), F32), _sds((N_DEV, rows, 128), F32)],
        in_specs=[vm], out_specs=[vm, vm],
        scratch_shapes=[pltpu.SemaphoreType.DMA((7,)), pltpu.SemaphoreType.DMA((7,))],
        compiler_params=pltpu.CompilerParams(vmem_limit_bytes=40 * MIB),
    )(pack)


def _ffn_in(h, sh, sc, gp, wt8, name):
    S = h.shape[0]
    R = min(512, S)

    def body(h_ref, sh_ref, sc_ref, gp_ref, w_ref, n_ref, dg_ref, sl_ref, a_ref):
        for r0 in range(0, R, CHUNK):
            rows = slice(r0, r0 + CHUNK)
            n = _prenorm(h_ref[rows, :], gp_ref[...], sc_ref[...], sh_ref[...]).astype(BF)
            n_ref[rows, :] = n
            for s in range(4):
                g = _dot_nt(n, w_ref[s])
                u = _dot_nt(n, w_ref[s + 4])
                sg = jax.nn.sigmoid(g)
                silu = g * sg
                dg_ref[s, rows, :] = (u * (sg * (1.0 + g * (1.0 - sg)))).astype(BF)
                sl_ref[s, rows, :] = silu.astype(BF)
                a_ref[s, rows, :] = (silu * u).astype(BF)

    vec = _const((1, D))
    row = pl.BlockSpec((R, D), lambda i: (i, 0))
    blk = pl.BlockSpec((4, R, FS), lambda i: (0, i, 0))
    return pl.pallas_call(
        body, name=name, grid=(S // R,),
        out_shape=[_sds((S, D), BF)] + [_sds((4, S, FS), BF)] * 3,
        in_specs=[row, vec, vec, vec, _resident((N_DEV, FS, D))],
        out_specs=[row, blk, blk, blk],
        compiler_params=_cp(1, 56),
    )(h, sh, sc, gp, wt8)


def _ffn_out(a, w4, h, gate, gp, name, target=None):
    S = h.shape[0]
    R = min(512, S)
    with_loss = target is not None

    def body(a_ref, w_ref, h_ref, gate_ref, gp_ref, *rest):
        if with_loss:
            t_ref, out_ref, y_ref, tot_ref = rest

            @pl.when(pl.program_id(0) == 0)
            def _():
                tot_ref[...] = jnp.zeros_like(tot_ref)
        else:
            out_ref, y_ref = rest
        for r0 in range(0, R, CHUNK):
            rows = slice(r0, r0 + CHUNK)
            y = _dot(a_ref[0, rows, :], w_ref[0])
            for s in range(1, 4):
                y = y + _dot(a_ref[s, rows, :], w_ref[s])
            y_ref[rows, :] = y
            hn = h_ref[rows, :] + (0.5 * gate_ref[...]) * (y * _rms_r(y) * gp_ref[...])
            if with_loss:
                e = hn - t_ref[rows, :]
                out_ref[rows, :] = e * (1.0 / D)
                tot_ref[...] += jnp.sum(jnp.sum(e * e, axis=1, keepdims=True), axis=0, keepdims=True)
            else:
                out_ref[rows, :] = hn

    vec = _const((1, D))
    row = pl.BlockSpec((R, D), lambda i: (i, 0))
    return pl.pallas_call(
        body, name=name, grid=(S // R,),
        out_shape=[_sds((S, D), F32), _sds((S, D), F32)] + ([_sds((1, 1), F32)] if with_loss else []),
        in_specs=[pl.BlockSpec((4, R, FS), lambda i: (0, i, 0)), _resident((4, FS, D)), row, vec, vec]
        + ([row] if with_loss else []),
        out_specs=[row, row] + ([_const((1, 1))] if with_loss else []),
        compiler_params=_cp(1, 48),
    )(*((a, w4, h, gate, gp) + ((target,) if with_loss else ())))


def _ffn_out_bwd(dh, y, dsilu_u, silu, w4, gate, gp, name):
    S = dh.shape[0]
    R = min(512, S)

    def body(dh_ref, y_ref, g_ref, u_ref, w_ref, gate_ref, gp_ref, dy_ref, dgu_ref, dgate_ref, dgp_ref):
        @pl.when(pl.program_id(0) == 0)
        def _():
            dgate_ref[...] = jnp.zeros_like(dgate_ref)
            dgp_ref[...] = jnp.zeros_like(dgp_ref)
        for r0 in range(0, R, CHUNK):
            rows = slice(r0, r0 + CHUNK)
            dy, dgate, dgp = _postnorm_bwd(dh_ref[rows, :], y_ref[rows, :], gate_ref[...], gp_ref[...], 0.5)
            dgate_ref[...] += dgate
            dgp_ref[...] += dgp
            dyb = dy.astype(BF)
            dy_ref[rows, :] = dyb
            for s in range(4):
                da = _dot_nt(dyb, w_ref[s])
                dgu_ref[s, rows, :] = (da * g_ref[s, rows, :].astype(F32)).astype(BF)
                dgu_ref[s + 4, rows, :] = (da * u_ref[s, rows, :].astype(F32)).astype(BF)

    vec = _const((1, D))
    row = pl.BlockSpec((R, D), lambda i: (i, 0))
    blk4 = pl.BlockSpec((4, R, FS), lambda i: (0, i, 0))
    return pl.pallas_call(
        body, name=name, grid=(S // R,),
        out_shape=[_sds((S, D), BF), _sds((8, S, FS), BF), _sds((1, D), F32), _sds((1, D), F32)],
        in_specs=[row, row, blk4, blk4, _resident((4, FS, D)), vec, vec],
        out_specs=[row, pl.BlockSpec((8, R, FS), lambda i: (0, i, 0)), vec, vec],
        compiler_params=_cp(1, 56),
    )(dh, y, dsilu_u, silu, w4, gate, gp)


def _ffn_dn(dgu, wt8, h, dh, sc, gp, name):
    S = h.shape[0]
    R = min(512, S)

    def body(dgu_ref, w_ref, h_ref, dh_ref, sc_ref, gp_ref, out_ref, dsh_ref, dsc_ref, dgp_ref):
        @pl.when(pl.program_id(0) == 0)
        def _():
            dsh_ref[...] = jnp.zeros_like(dsh_ref)
            dsc_ref[...] = jnp.zeros_like(dsc_ref)
            dgp_ref[...] = jnp.zeros_like(dgp_ref)

        for r0 in range(0, R, CHUNK):
            rows = slice(r0, r0 + CHUNK)
            dn = _dot(dgu_ref[0, rows, :], w_ref[0])
            for j in range(1, N_DEV):
                dn = dn + _dot(dgu_ref[j, rows, :], w_ref[j])
            dx, dsh, dsc, dgp = _prenorm_bwd(dn, h_ref[rows, :], gp_ref[...], sc_ref[...])
            out_ref[rows, :] = dh_ref[rows, :] + dx
            dsh_ref[...] += dsh
            dsc_ref[...] += dsc
            dgp_ref[...] += dgp

    vec = _const((1, D))
    row = pl.BlockSpec((R, D), lambda i: (i, 0))
    return pl.pallas_call(
        body, name=name, grid=(S // R,),
        out_shape=[_sds((S, D), F32)] + [_sds((1, D), F32)] * 3,
        in_specs=[pl.BlockSpec((N_DEV, R, FS), lambda i: (0, i, 0)), _resident((N_DEV, FS, D)),
                  row, row, vec, vec],
        out_specs=[row, vec, vec, vec],
        compiler_params=_cp(1, 56),
    )(dgu, wt8, h, dh, sc, gp)


def _tn_matmul(a, b, name):
    a3 = a if a.ndim == 3 else a[None]
    b3 = b if b.ndim == 3 else b[None]
    GA, S, M = a3.shape
    GB, _, N = b3.shape
    ts = min(2048, S)
    nk = S // ts
    chunks = [(m0, min(CHUNK, M - m0)) for m0 in range(0, M, CHUNK)]

    def body(a_ref, b_ref, o_ref, acc):
        k = pl.program_id(2)

        @pl.when(k == 0)
        def _():
            acc[...] = jnp.zeros_like(acc)

        for m0, mc in chunks:
            acc[m0:m0 + mc, :] += _dot_tn(a_ref[:, m0:m0 + mc], b_ref[...])

        @pl.when(k == nk - 1)
        def _():
            for m0, mc in chunks:
                o_ref[m0:m0 + mc, :] = acc[m0:m0 + mc, :].astype(BF)

    return pl.pallas_call(
        body, name=name, grid=(GA, GB, nk),
        out_shape=_sds((GA, GB, M, N), BF),
        in_specs=[pl.BlockSpec((None, ts, M), lambda ga, gb, k: (ga, k, 0)),
                  pl.BlockSpec((None, ts, N), lambda ga, gb, k: (gb, k, 0))],
        out_specs=pl.BlockSpec((None, None, M, N), lambda ga, gb, k: (ga, gb, 0, 0)),
        scratch_shapes=[pltpu.VMEM((M, N), F32)],
        compiler_params=_cp(3, 56),
    )(a3, b3)


def _mix_in(h, sh, sc, gp, w):
    S = h.shape[0]
    R = min(512, S)

    def body(h_ref, sh_ref, sc_ref, gp_ref, w_ref, n_ref, qkv_ref, zg_ref, gates_ref):
        for r0 in range(0, R, CHUNK):
            rows = slice(r0, r0 + CHUNK)
            nb = _prenorm(h_ref[rows, :], gp_ref[...], sc_ref[...], sh_ref[...]).astype(BF)
            n_ref[rows, :] = nb
            qkv_ref[rows, :] = _dot_nt(nb, w_ref[0:ZG_OFF, :]).astype(BF)
            zg_ref[rows, :] = _dot_nt(nb, w_ref[ZG_OFF:GATE_OFF, :]).astype(BF)
            gates_ref[rows, :] = jax.nn.sigmoid(_dot_nt(nb, w_ref[GATE_OFF:IN_W, :])).astype(BF)

    vec = _const((1, D))
    rows = lambda w_: pl.BlockSpec((R, w_), lambda i: (i, 0))
    return pl.pallas_call(
        body, name="mix_in", grid=(S // R,),
        out_shape=[_sds((S, D), BF), _sds((S, QKV_W), BF), _sds((S, 2 * G_W), BF), _sds((S, 2 * D), BF)],
        in_specs=[rows(D), vec, vec, vec, _resident((IN_W, D))],
        out_specs=[rows(D), rows(QKV_W), rows(2 * G_W), rows(2 * D)],
        compiler_params=_cp(1, 48),
    )(h, sh, sc, gp, w)


def _bias_table(rel_bias, bucket):
    def body(rel_ref, bk_ref, out_ref):
        bk = bk_ref[...]
        qi = lax.broadcasted_iota(jnp.int32, (BLK, 2 * BLK), 0)
        kj = lax.broadcasted_iota(jnp.int32, (BLK, 2 * BLK), 1)
        dist = qi + BLK - kj
        window = (dist >= 0) & (dist < BLK)
        for h in range(N_HEADS):
            acc = jnp.zeros((BLK, 2 * BLK), F32)
            for b in range(N_BUCKETS):
                acc = jnp.where(bk == b, rel_ref[b, h], acc)
            out_ref[h // GROUP, pl.ds((h % GROUP) * BLK, BLK), :] = jnp.where(window, acc, NEG)

    return pl.pallas_call(
        body, name="bias_table",
        out_shape=_sds((N_KV, GROUP * BLK, 2 * BLK), F32),
        in_specs=[pl.BlockSpec(memory_space=pltpu.SMEM), pl.BlockSpec(memory_space=pltpu.VMEM)],
        out_specs=pl.BlockSpec(memory_space=pltpu.VMEM),
    )(rel_bias, bucket)


def _attn_scores(q, kvc, kvp, bias_ref, sink_ref, blk, kh):
    k2 = jnp.concatenate([kvp[:, kh * HD:(kh + 1) * HD], kvc[:, kh * HD:(kh + 1) * HD]], axis=0)
    v2 = jnp.concatenate([kvp[:, KV_W + kh * HD:KV_W + (kh + 1) * HD],
                          kvc[:, KV_W + kh * HD:KV_W + (kh + 1) * HD]], axis=0)
    q4 = jnp.concatenate([q[:, (kh * GROUP + g) * HD:(kh * GROUP + g + 1) * HD] for g in range(GROUP)], axis=0)
    s = _dot_nt(q4, k2) * SCALE + bias_ref[kh]
    col = lax.broadcasted_iota(jnp.int32, (GROUP * BLK, 2 * BLK), 1)
    s = jnp.where((col >= BLK) | (blk > 0), s, NEG)
    rowg = lax.broadcasted_iota(jnp.int32, (GROUP * BLK, 1), 0) // BLK
    sink = jnp.zeros((GROUP * BLK, 1), F32)
    for g in range(GROUP):
        sink = jnp.where(rowg == g, sink_ref[kh * GROUP + g], sink)
    return q4, k2, v2, s, sink


def _attn_fwd(qkv, bias, sinks):
    S = qkv.shape[0]
    nb = S // BLK

    def body(sink_ref, q_ref, kvc_ref, kvp_ref, bias_ref, o_ref):
        blk = pl.program_id(0)
        q, kvc, kvp = q_ref[...], kvc_ref[...], kvp_ref[...]
        outs = []
        for kh in range(N_KV):
            q4, k2, v2, s, sink = _attn_scores(q, kvc, kvp, bias_ref, sink_ref, blk, kh)
            m = jnp.maximum(jnp.max(s, axis=1, keepdims=True), sink)
            p = jnp.exp(s - m)
            denom = jnp.sum(p, axis=1, keepdims=True) + jnp.exp(sink - m)
            o4 = _dot((p / denom).astype(BF), v2)
            outs += [o4[g * BLK:(g + 1) * BLK] for g in range(GROUP)]
        o_ref[...] = jnp.concatenate(outs, axis=1).astype(BF)

    return pl.pallas_call(
        body, name="attn_fwd", grid=(nb,),
        out_shape=_sds((S, Q_W), BF),
        in_specs=[pl.BlockSpec(memory_space=pltpu.SMEM),
                  pl.BlockSpec((BLK, Q_W), lambda i: (i, 0)),
                  pl.BlockSpec((BLK, 2 * KV_W), lambda i: (i, 2)),
                  pl.BlockSpec((BLK, 2 * KV_W), lambda i: (jnp.maximum(i - 1, 0), 2)),
                  _const((N_KV, GROUP * BLK, 2 * BLK))],
        out_specs=pl.BlockSpec((BLK, Q_W), lambda i: (i, 0)),
        compiler_params=_cp(1, 32),
    )(sinks, qkv, qkv, qkv, bias)


def _attn_bwd(qkv, bias, sinks, do):
    S = qkv.shape[0]
    nb = S // BLK

    def body(sink_ref, q_ref, kvc_ref, kvp_ref, bias_ref, do_ref, dq_ref, dkv_ref, dbias_ref, dsink_ref, carry):
        i = pl.program_id(0)
        blk = nb - 1 - i

        @pl.when(i == 0)
        def _():
            carry[...] = jnp.zeros_like(carry)
            dbias_ref[...] = jnp.zeros_like(dbias_ref)
            dsink_ref[...] = jnp.zeros_like(dsink_ref)

        q, kvc, kvp, do_ = q_ref[...], kvc_ref[...], kvp_ref[...], do_ref[...]
        dqs, dk_cur, dv_cur, dk_prev, dv_prev = [], [], [], [], []
        for kh in range(N_KV):
            q4, k2, v2, s, sink = _attn_scores(q, kvc, kvp, bias_ref, sink_ref, blk, kh)
            m = jnp.maximum(jnp.max(s, axis=1, keepdims=True), sink)
            p = jnp.exp(s - m)
            denom = jnp.sum(p, axis=1, keepdims=True) + jnp.exp(sink - m)
            prob = p / denom
            p_sink = jnp.exp(sink - m) / denom
            pb = prob.astype(BF)
            do4 = jnp.concatenate(
                [do_[:, (kh * GROUP + g) * HD:(kh * GROUP + g + 1) * HD] for g in range(GROUP)], axis=0)
            dp = _dot_nt(do4, v2)
            o4 = _dot(pb, v2)
            delta = jnp.sum(do4.astype(F32) * o4, axis=1, keepdims=True)
            ds = prob * (dp - delta)
            dbias_ref[kh] += ds
            sink_term = p_sink * delta
            for g in range(GROUP):
                h = kh * GROUP + g
                val = -jnp.sum(sink_term[g * BLK:(g + 1) * BLK], axis=0, keepdims=True)
                dsink_ref[pl.ds(h, 1), :] += jnp.broadcast_to(val, (1, 128))
            dsb = ds.astype(BF)
            dq4 = _dot(dsb, k2) * SCALE
            dk2 = _dot_tn(dsb, q4) * SCALE
            dv2 = _dot_tn(pb, do4)
            dqs += [dq4[g * BLK:(g + 1) * BLK] for g in range(GROUP)]
            dk_prev.append(dk2[0:BLK])
            dk_cur.append(dk2[BLK:2 * BLK])
            dv_prev.append(dv2[0:BLK])
            dv_cur.append(dv2[BLK:2 * BLK])
        dq_ref[...] = jnp.concatenate(dqs, axis=1).astype(BF)
        dkv_ref[...] = (jnp.concatenate(dk_cur + dv_cur, axis=1) + carry[...]).astype(BF)
        carry[...] = jnp.concatenate(dk_prev + dv_prev, axis=1)

    return pl.pallas_call(
        body, name="attn_bwd", grid=(nb,),
        out_shape=[_sds((S, Q_W), BF), _sds((S, 2 * KV_W), BF),
                   _sds((N_KV, GROUP * BLK, 2 * BLK), F32), _sds((N_HEADS, 128), F32)],
        in_specs=[pl.BlockSpec(memory_space=pltpu.SMEM),
                  pl.BlockSpec((BLK, Q_W), lambda i: (nb - 1 - i, 0)),
                  pl.BlockSpec((BLK, 2 * KV_W), lambda i: (nb - 1 - i, 2)),
                  pl.BlockSpec((BLK, 2 * KV_W), lambda i: (jnp.maximum(nb - 2 - i, 0), 2)),
                  _const((N_KV, GROUP * BLK, 2 * BLK)),
                  pl.BlockSpec((BLK, Q_W), lambda i: (nb - 1 - i, 0))],
        out_specs=[pl.BlockSpec((BLK, Q_W), lambda i: (nb - 1 - i, 0)),
                   pl.BlockSpec((BLK, 2 * KV_W), lambda i: (nb - 1 - i, 0)),
                   _const((N_KV, GROUP * BLK, 2 * BLK)), _const((N_HEADS, 128))],
        scratch_shapes=[pltpu.VMEM((BLK, 2 * KV_W), F32)],
        compiler_params=_cp(1, 32),
    )(sinks, qkv, qkv, qkv, bias, do)


def _rel_bias_grad(dbias, bucket):
    def body(db_ref, bk_ref, out_ref):
        bk = bk_ref[...]
        lane = lax.broadcasted_iota(jnp.int32, (1, 128), 1)
        for h in range(N_HEADS):
            d = db_ref[h // GROUP, pl.ds((h % GROUP) * BLK, BLK), :]
            row = jnp.zeros((1, 128), F32)
            for b in range(N_BUCKETS):
                tot = jnp.sum(jnp.sum(jnp.where(bk == b, d, 0.0), axis=1, keepdims=True), axis=0, keepdims=True)
                row = jnp.where(lane == b, tot, row)
            out_ref[pl.ds(h, 1), :] = row

    vm = pl.BlockSpec(memory_space=pltpu.VMEM)
    return pl.pallas_call(body, name="rel_bias_grad", out_shape=_sds((N_HEADS, 128), F32),
                          in_specs=[vm, vm], out_specs=vm)(dbias, bucket)


def _gmlp_parts(zg_ref, lg_ref, lb_ref):
    z = zg_ref[...].astype(F32)
    ge = _gelu(z)
    u, vg = ge[:, 0:G_W], ge[:, G_W:2 * G_W]
    mu = jnp.mean(vg, axis=-1, keepdims=True)
    xc = vg - mu
    rstd = lax.rsqrt(jnp.mean(xc * xc, axis=-1, keepdims=True) + EPS)
    xh = xc * rstd
    return z, u, xh, rstd, xh * lg_ref[...] + lb_ref[...]


def _causal_weights(ws_ref, wc):
    t = lax.broadcasted_iota(jnp.int32, (BLK, BLK), 0)
    s = lax.broadcasted_iota(jnp.int32, (BLK, BLK), 1)
    for g in range(N_HEADS):
        wc[g] = jnp.where(s <= t, ws_ref[g], 0.0).astype(BF)


def _spatial(vb, wc, bst_ref, p, low):
    xp = vb[:, p * 128:(p + 1) * 128]
    s0 = _dot(wc[2 * p], xp) + bst_ref[:, 2 * p:2 * p + 1]
    s1 = _dot(wc[2 * p + 1], xp) + bst_ref[:, 2 * p + 1:2 * p + 2]
    return xp, jnp.where(low, s0, s1)


def _gmlp_fwd(zg, lg, lb, ws, bst):
    S = zg.shape[0]

    def body(zg_ref, lg_ref, lb_ref, ws_ref, bst_ref, o_ref, wc):
        @pl.when(pl.program_id(0) == 0)
        def _():
            _causal_weights(ws_ref, wc)
        _, u, _, _, vln = _gmlp_parts(zg_ref, lg_ref, lb_ref)
        vb = vln.astype(BF)
        low = lax.broadcasted_iota(jnp.int32, (BLK, 128), 1) < HD
        for p in range(4):
            _, sp = _spatial(vb, wc, bst_ref, p, low)
            o_ref[:, p * 128:(p + 1) * 128] = (u[:, p * 128:(p + 1) * 128] * sp).astype(BF)

    return pl.pallas_call(
        body, name="gmlp_fwd", grid=(S // BLK,),
        out_shape=_sds((S, G_W), BF),
        in_specs=[pl.BlockSpec((BLK, 2 * G_W), lambda i: (i, 0)), _const((1, G_W)), _const((1, G_W)),
                  _const((N_HEADS, BLK, BLK)), _const((BLK, N_HEADS))],
        out_specs=pl.BlockSpec((BLK, G_W), lambda i: (i, 0)),
        scratch_shapes=[pltpu.VMEM((N_HEADS, BLK, BLK), BF)],
        compiler_params=_cp(1, 32),
    )(zg, lg, lb, ws, bst)


def _gmlp_bwd(zg, d_out, lg, lb, ws, bst):
    S = zg.shape[0]
    nb = S // BLK

    def body(zg_ref, d_ref, lg_ref, lb_ref, ws_ref, bst_ref, dzg_ref, dws_ref, dbs_ref, dlg_ref, dlb_ref, wc, dbacc):
        i = pl.program_id(0)

        @pl.when(i == 0)
        def _():
            _causal_weights(ws_ref, wc)
            dws_ref[...] = jnp.zeros_like(dws_ref)
            dlg_ref[...] = jnp.zeros_like(dlg_ref)
            dlb_ref[...] = jnp.zeros_like(dlb_ref)
            dbacc[...] = jnp.zeros_like(dbacc)

        z, u, xh, rstd, vln = _gmlp_parts(zg_ref, lg_ref, lb_ref)
        vb = vln.astype(BF)
        d = d_ref[...].astype(F32)
        low = lax.broadcasted_iota(jnp.int32, (BLK, 128), 1) < HD
        du_parts, dvln_parts = [], []
        for p in range(4):
            xp, sp = _spatial(vb, wc, bst_ref, p, low)
            dp = d[:, p * 128:(p + 1) * 128]
            du_parts.append(dp * sp)
            dsp = dp * u[:, p * 128:(p + 1) * 128]
            dbacc[:, p * 128:(p + 1) * 128] += dsp
            d0 = jnp.where(low, dsp, 0.0).astype(BF)
            d1 = jnp.where(low, 0.0, dsp).astype(BF)
            dws_ref[2 * p] += _dot_nt(d0, xp)
            dws_ref[2 * p + 1] += _dot_nt(d1, xp)
            dvln_parts.append(_dot_tn(wc[2 * p], d0) + _dot_tn(wc[2 * p + 1], d1))
        dvln = jnp.concatenate(dvln_parts, axis=1)
        dlg_ref[...] += _colsum(dvln * xh)
        dlb_ref[...] += _colsum(dvln)
        dxh = dvln * lg_ref[...]
        dvg = rstd * (dxh - jnp.mean(dxh, axis=-1, keepdims=True)
                      - xh * jnp.mean(dxh * xh, axis=-1, keepdims=True))
        dge = jnp.concatenate(du_parts + [dvg], axis=1)
        dzg_ref[...] = (dge * _gelu_grad(z)).astype(BF)

        @pl.when(i == nb - 1)
        def _():
            t = lax.broadcasted_iota(jnp.int32, (BLK, BLK), 0)
            s = lax.broadcasted_iota(jnp.int32, (BLK, BLK), 1)
            for g in range(N_HEADS):
                dws_ref[g] = jnp.where(s <= t, dws_ref[g], 0.0)
            grp = lax.broadcasted_iota(jnp.int32, (N_HEADS, G_W), 0)
            lane = lax.broadcasted_iota(jnp.int32, (N_HEADS, G_W), 1) // HD
            pick = jnp.where(grp == lane, 1.0, 0.0).astype(F32)
            dbs_ref[...] = lax.dot_general(pick, dbacc[...], (((1,), (1,)), ((), ())),
                                           preferred_element_type=F32, precision=HIGH)

    return pl.pallas_call(
        body, name="gmlp_bwd", grid=(nb,),
        out_shape=[_sds((S, 2 * G_W), BF), _sds((N_HEADS, BLK, BLK), F32), _sds((N_HEADS, BLK), F32),
                   _sds((1, G_W), F32), _sds((1, G_W), F32)],
        in_specs=[pl.BlockSpec((BLK, 2 * G_W), lambda i: (i, 0)), pl.BlockSpec((BLK, G_W), lambda i: (i, 0)),
                  _const((1, G_W)), _const((1, G_W)), _const((N_HEADS, BLK, BLK)), _const((BLK, N_HEADS))],
        out_specs=[pl.BlockSpec((BLK, 2 * G_W), lambda i: (i, 0)), _const((N_HEADS, BLK, BLK)),
                   _const((N_HEADS, BLK)), _const((1, G_W)), _const((1, G_W))],
        scratch_shapes=[pltpu.VMEM((N_HEADS, BLK, BLK), BF), pltpu.VMEM((BLK, G_W), F32)],
        compiler_params=_cp(1, 32),
    )(zg, d_out, lg, lb, ws, bst)


def _mix_out(o, gm, gates, h, wa, wg, wo, gate, gp):
    S = h.shape[0]
    R = min(512, S)

    def body(o_ref, gm_ref, gates_ref, h_ref, wa_ref, wg_ref, wo_ref, gate_ref, gp_ref,
             ya_ref, yg_ref, ym_ref, y_ref, hn_ref):
        for r0 in range(0, R, CHUNK):
            rows = slice(r0, r0 + CHUNK)
            ya = _dot(o_ref[rows, :], wa_ref[...])
            yg = _dot(gm_ref[rows, :], wg_ref[...])
            ya_ref[rows, :] = ya.astype(BF)
            yg_ref[rows, :] = yg.astype(BF)
            ym = (gates_ref[rows, 0:D].astype(F32) * ya + gates_ref[rows, D:2 * D].astype(F32) * yg).astype(BF)
            ym_ref[rows, :] = ym
            y = _dot(ym, wo_ref[...])
            y_ref[rows, :] = y
            hn_ref[rows, :] = h_ref[rows, :] + gate_ref[...] * (y * _rms_r(y) * gp_ref[...])

    vec = _const((1, D))
    rows = lambda w_: pl.BlockSpec((R, w_), lambda i: (i, 0))
    return pl.pallas_call(
        body, name="mix_out", grid=(S // R,),
        out_shape=[_sds((S, D), BF)] * 3 + [_sds((S, D), F32)] * 2,
        in_specs=[rows(Q_W), rows(G_W), rows(2 * D), rows(D), _resident((Q_W, D)), _resident((G_W, D)),
                  _resident((D, D)), vec, vec],
        out_specs=[rows(D)] * 5,
        compiler_params=_cp(1, 48),
    )(o, gm, gates, h, wa, wg, wo, gate, gp)


def _mix_out_bwd(dh, y, ya, yg, gates, wa, wg, wo, gate, gp):
    S = dh.shape[0]
    R = min(256, S)

    def body(dh_ref, y_ref, ya_ref, yg_ref, gates_ref, wa_ref, wg_ref, wo_ref, gate_ref, gp_ref,
             dy_ref, dya_ref, dyg_ref, dz_ref, do_ref, dgm_ref, dgate_ref, dgp_ref):
        @pl.when(pl.program_id(0) == 0)
        def _():
            dgate_ref[...] = jnp.zeros_like(dgate_ref)
            dgp_ref[...] = jnp.zeros_like(dgp_ref)
        dy, dgate, dgp = _postnorm_bwd(dh_ref[...], y_ref[...], gate_ref[...], gp_ref[...], 1.0)
        dgate_ref[...] += dgate
        dgp_ref[...] += dgp
        dyb = dy.astype(BF)
        dy_ref[...] = dyb
        dym = _dot_nt(dyb, wo_ref[...])
        ga = gates_ref[:, 0:D].astype(F32)
        gg = gates_ref[:, D:2 * D].astype(F32)
        dya = (dym * ga).astype(BF)
        dyg = (dym * gg).astype(BF)
        dya_ref[...] = dya
        dyg_ref[...] = dyg
        dz_ref[:, 0:D] = (dym * ya_ref[...].astype(F32) * (ga * (1.0 - ga))).astype(BF)
        dz_ref[:, D:2 * D] = (dym * yg_ref[...].astype(F32) * (gg * (1.0 - gg))).astype(BF)
        do_ref[...] = _dot_nt(dya, wa_ref[...]).astype(BF)
        dgm_ref[...] = _dot_nt(dyg, wg_ref[...]).astype(BF)

    vec = _const((1, D))
    rows = lambda w_: pl.BlockSpec((R, w_), lambda i: (i, 0))
    return pl.pallas_call(
        body, name="mix_out_bwd", grid=(S // R,),
        out_shape=[_sds((S, D), BF)] * 3 + [_sds((S, 2 * D), BF), _sds((S, Q_W), BF), _sds((S, G_W), BF),
                                             _sds((1, D), F32), _sds((1, D), F32)],
        in_specs=[rows(D), rows(D), rows(D), rows(D), rows(2 * D), _resident((Q_W, D)), _resident((G_W, D)),
                  _resident((D, D)), vec, vec],
        out_specs=[rows(D)] * 3 + [rows(2 * D), rows(Q_W), rows(G_W), vec, vec],
        compiler_params=_cp(1, 48),
    )(dh, y, ya, yg, gates, wa, wg, wo, gate, gp)


def _mix_dn(dq, dkv, dzg, dzgate, w, h, dh, sc, gp):
    S = h.shape[0]
    R = min(512, S)

    def body(dq_ref, dkv_ref, dzg_ref, dzt_ref, w_ref, h_ref, dh_ref, sc_ref, gp_ref,
             out_ref, dsh_ref, dsc_ref, dgp_ref):
        @pl.when(pl.program_id(0) == 0)
        def _():
            dsh_ref[...] = jnp.zeros_like(dsh_ref)
            dsc_ref[...] = jnp.zeros_like(dsc_ref)
            dgp_ref[...] = jnp.zeros_like(dgp_ref)
        for r0 in range(0, R, CHUNK):
            rows = slice(r0, r0 + CHUNK)
            dn = _dot(dq_ref[rows, :], w_ref[0:Q_W, :])
            dn = dn + _dot(dkv_ref[rows, :], w_ref[Q_W:QKV_W, :])
            dn = dn + _dot(dzg_ref[rows, :], w_ref[ZG_OFF:GATE_OFF, :])
            dn = dn + _dot(dzt_ref[rows, :], w_ref[GATE_OFF:IN_W, :])
            dx, dsh, dsc, dgp = _prenorm_bwd(dn, h_ref[rows, :], gp_ref[...], sc_ref[...])
            out_ref[rows, :] = dh_ref[rows, :] + dx
            dsh_ref[...] += dsh
            dsc_ref[...] += dsc
            dgp_ref[...] += dgp

    vec = _const((1, D))
    rows = lambda w_: pl.BlockSpec((R, w_), lambda i: (i, 0))
    return pl.pallas_call(
        body, name="mix_dn", grid=(S // R,),
        out_shape=[_sds((S, D), F32)] + [_sds((1, D), F32)] * 3,
        in_specs=[rows(Q_W), rows(2 * KV_W), rows(2 * G_W), rows(2 * D), _resident((IN_W, D)),
                  rows(D), rows(D), vec, vec],
        out_specs=[rows(D), vec, vec, vec],
        compiler_params=_cp(1, 48),
    )(dq, dkv, dzg, dzgate, w, h, dh, sc, gp)


def _adamw_math(w, g, m, v):
    m2 = ADAM_B1 * m + (1.0 - ADAM_B1) * g
    v2 = ADAM_B2 * v + (1.0 - ADAM_B2) * (g * g)
    m_hat = m2 / (1.0 - ADAM_B1 ** ADAM_STEP)
    v_hat = v2 / (1.0 - ADAM_B2 ** ADAM_STEP)
    delta = -ADAM_LR * (m_hat / (jnp.sqrt(v_hat) + ADAM_EPS) + ADAM_WD * w)
    return delta, m2, v2


def _row_tile(rows, cols):
    best = None
    for t in range(16, rows + 1, 16):
        if rows % t == 0 and t * cols <= 256 * 1024:
            best = t
    return best if best is not None else rows


def _adamw_sharded(landing, w, m, v, name):
    r, c = w.shape
    tr = _row_tile(r, c)

    def body(l_ref, w_ref, m_ref, v_ref, g_ref, d_ref, m2_ref, v2_ref):
        g = l_ref[0].astype(F32)
        for j in range(1, N_DEV):
            g = g + l_ref[j].astype(F32)
        delta, m2, v2 = _adamw_math(w_ref[...], g, m_ref[...], v_ref[...])
        g_ref[...] = g
        d_ref[...] = delta
        m2_ref[...] = m2
        v2_ref[...] = v2

    row = pl.BlockSpec((tr, c), lambda i: (i, 0))
    return pl.pallas_call(
        body, name=name, grid=(r // tr,),
        out_shape=[_sds((r, c), F32)] * 4,
        in_specs=[pl.BlockSpec((N_DEV, tr, c), lambda i: (0, i, 0)), row, row, row],
        out_specs=[row] * 4,
        compiler_params=_cp(1, 48),
    )(landing, w, m, v)


def _adamw_small(w, g, m, v, name):
    def body(w_ref, g_ref, m_ref, v_ref, d_ref, m2_ref, v2_ref):
        delta, m2, v2 = _adamw_math(w_ref[...], g_ref[...], m_ref[...], v_ref[...])
        d_ref[...] = delta
        m2_ref[...] = m2
        v2_ref[...] = v2

    vm = pl.BlockSpec(memory_space=pltpu.VMEM)
    return pl.pallas_call(body, name=name, out_shape=[_sds(w.shape, F32)] * 3,
                          in_specs=[vm] * 4, out_specs=[vm] * 3)(w, g, m, v)


def _w_ada_update(c8, d_ada, w, m, v):
    tr = 256

    def body(c_ref, d_ref, w_ref, m_ref, v_ref, g_ref, dl_ref, m2_ref, v2_ref):
        cs = c_ref[...]
        cs = cs * jax.nn.sigmoid(cs)
        g = lax.dot_general(cs, d_ref[...], (((0,), (0,)), ((), ())), preferred_element_type=F32, precision=HIGH)
        delta, m2, v2 = _adamw_math(w_ref[...], g, m_ref[...], v_ref[...])
        g_ref[...] = g
        dl_ref[...] = delta
        m2_ref[...] = m2
        v2_ref[...] = v2

    row = pl.BlockSpec((tr, ADA_W), lambda i: (i, 0))
    return pl.pallas_call(
        body, name="w_ada_update", grid=(D // tr,),
        out_shape=[_sds((D, ADA_W), F32)] * 4,
        in_specs=[pl.BlockSpec((N_DEV, tr), lambda i: (0, i)), _const((N_DEV, ADA_W)), row, row, row],
        out_specs=[row] * 4,
        compiler_params=_cp(1, 40),
    )(c8, d_ada, w, m, v)


def _t5_bucket():
    qi = jnp.arange(BLK, dtype=jnp.int32)[:, None]
    kj = jnp.arange(2 * BLK, dtype=jnp.int32)[None, :]
    dist = jnp.maximum(qi + BLK - kj, 0)
    max_exact = N_BUCKETS // 2
    d_f = jnp.maximum(dist, max_exact).astype(F32)
    large = max_exact + (jnp.log(d_f / max_exact) / math.log(MAX_DISTANCE / max_exact)
                         * (N_BUCKETS - max_exact)).astype(jnp.int32)
    large = jnp.minimum(large, N_BUCKETS - 1)
    return jnp.where(dist < max_exact, dist, large)


def _slabs_of_columns(w):
    r, c8 = w.shape
    return jnp.transpose(w.reshape(r, N_DEV, c8 // N_DEV), (1, 0, 2))


def _columns_of_slabs(w8):
    _, r, c = w8.shape
    return jnp.transpose(w8, (1, 0, 2)).reshape(r, N_DEV * c)


def kernel(x, c, rel_bias, w_ada, b_ada, pre_norm_g, post_norm_g, w_ffn1_in, w_ffn1_out, w_in, sinks, gmlp_ln_g, gmlp_ln_b, gmlp_w_s, gmlp_b_s, w_br_attn, w_br_gmlp, w_out, w_ffn2_in, w_ffn2_out, loss_target, m_rel_bias, m_w_ada, m_b_ada, m_pre_norm_g, m_post_norm_g, m_w_ffn1_in, m_w_ffn1_out, m_w_in, m_sinks, m_gmlp_ln_g, m_gmlp_ln_b, m_gmlp_w_s, m_gmlp_b_s, m_w_br_attn, m_w_br_gmlp, m_w_out, m_w_ffn2_in, m_w_ffn2_out, v_rel_bias, v_w_ada, v_b_ada, v_pre_norm_g, v_post_norm_g, v_w_ffn1_in, v_w_ffn1_out, v_w_in, v_sinks, v_gmlp_ln_g, v_gmlp_ln_b, v_gmlp_w_s, v_gmlp_b_s, v_w_br_attn, v_w_br_gmlp, v_w_out, v_w_ffn2_in, v_w_ffn2_out):
    me = 4 * lax.axis_index("x") + 2 * lax.axis_index("y") + lax.axis_index("c")
    x0 = x[0]
    target = loss_target[0]

    transposed = ("w_ffn1_in", "w_in", "w_ffn2_in")
    shards = [w_ffn1_in[0].T, w_ffn1_out[0], w_in[0].T, w_br_attn[0], w_br_gmlp[0], w_out[0],
              w_ffn2_in[0].T, w_ffn2_out[0]]
    shards_bf = [s.astype(BF) for s in shards]
    groups = [shards_bf[0:1], shards_bf[1:2], shards_bf[2:6], shards_bf[6:7], shards_bf[7:8]]

    def gather_start(i, after):
        return _slabs_start("gather", groups[i], after, "gather_start_%d" % i)

    def forward_start(st, i, after):
        lands = _slabs_wait("gather", len(groups[i]), st, after, "gather_wait_%d" % i)
        return _slabs_start("forward", lands, c, "forward_start_%d" % i)

    def gathered(st, i, after):
        return _slabs_wait("forward", len(groups[i]), st, after, "forward_wait_%d" % i)

    gs0 = gather_start(0, c)

    small = jnp.concatenate([c[0], pre_norm_g[0].reshape(-1), post_norm_g[0].reshape(-1)])
    small8 = jnp.broadcast_to(small[None, :], (8, small.shape[0]))
    b_ada64 = jnp.repeat(b_ada.reshape(N_DEV, ADA_W), 8, axis=0)
    gath, ada64 = _ada_forward(small8, w_ada[0], b_ada64)
    gath8 = gath[::8]
    ada = ada64[::8].reshape(9, D)
    sh1, sc1, g1, sh2, sc2, g2, sh3, sc3, g3 = [ada[k:k + 1] for k in range(9)]
    gains = gath8[:, D:].reshape(N_DEV, 2, 3, 128)
    pre_g = jnp.transpose(gains[:, 0], (1, 0, 2)).reshape(3, D)
    post_g = jnp.transpose(gains[:, 1], (1, 0, 2)).reshape(3, D)
    pre = [pre_g[k:k + 1] for k in range(3)]
    post = [post_g[k:k + 1] for k in range(3)]

    bucket = _t5_bucket()
    bias = _bias_table(rel_bias, bucket)
    sinks8 = sinks[0]
    lg, lb = gmlp_ln_g, gmlp_ln_b
    ws = gmlp_w_s[0]
    bst = jnp.transpose(gmlp_b_s[0])

    fs0 = forward_start(gs0, 0, sh1)
    gs1 = gather_start(1, fs0[-1])
    gs2 = gather_start(2, gs1[-1])
    (wf1_in,) = gathered(fs0, 0, gs2[-1])
    n1, fg1, fu1, fa1 = _ffn_in(x0, sh1, sc1, pre[0], wf1_in, "ffn1_in")
    fs1 = forward_start(gs1, 1, n1)
    fs2 = forward_start(gs2, 2, fs1[-1])
    gs3 = gather_start(3, fs2[-1])
    gs4 = gather_start(4, gs3[-1])
    wf1_out = gathered(fs1, 1, gs4[-1])[0].reshape(4, FS, D)
    h1, y1 = _ffn_out(fa1, wf1_out, x0, g1, post[0], "ffn1_out")
    mix_w = gathered(fs2, 2, h1)
    w_in_full = mix_w[0].reshape(IN_W, D)
    w_bra = _columns_of_slabs(mix_w[1])
    w_brg = _columns_of_slabs(mix_w[2])
    w_out_full = mix_w[3].reshape(D, D)
    n2, qkv, zg, gates = _mix_in(h1, sh2, sc2, pre[1], w_in_full)
    att = _attn_fwd(qkv, bias, sinks8)
    gm = _gmlp_fwd(zg, lg, lb, ws, bst)
    fs3 = forward_start(gs3, 3, gm)
    fs4 = forward_start(gs4, 4, fs3[-1])
    ya, yg, ymix, y2, h2 = _mix_out(att, gm, gates, h1, w_bra, w_brg, w_out_full, g2 + fs4[-1], post[1])
    (wf2_in,) = gathered(fs3, 3, h2)
    n3, fg3, fu3, fa3 = _ffn_in(h2, sh3, sc3, pre[2], wf2_in, "ffn2_in")
    wf2_out = gathered(fs4, 4, n3)[0].reshape(4, FS, D)
    dh3, y3, sq = _ffn_out(fa3, wf2_out, h2, g3, post[2], "ffn2_out", target=target)
    loss = lax.psum(0.5 * sq[0, 0] / D, ("x", "y", "c"))

    def exchange_start(i, arrays):
        return _slabs_start("exchange", arrays, sq, "exchange_start_%d" % i)

    dy3, dgu3, d_g3, d_post2 = _ffn_out_bwd(dh3, y3, fg3, fu3, wf2_out, g3, post[2], "ffn2_out_bwd")
    gw_f2_out = _tn_matmul(fa3, dy3, "ffn2_out_wgrad").reshape(N_DEV, D_FF // N_DEV, D)
    ex0 = exchange_start(0, [gw_f2_out])
    dh2, d_sh3, d_sc3, d_pre2 = _ffn_dn(dgu3, wf2_in, h2, dh3, sc3 + ex0[-1], pre[2], "ffn2_dn")
    gw_f2_in = _tn_matmul(dgu3, n3, "ffn2_in_wgrad").reshape(N_DEV, FS, D)
    ex1 = exchange_start(1, [gw_f2_in])

    dy2, dya, dyg, dzgate, d_att, d_gm, d_g2, d_post1 = _mix_out_bwd(
        dh2, y2, ya, yg, gates, w_bra, w_brg, w_out_full, g2 + ex1[-1], post[1])
    gw_out = _tn_matmul(ymix, dy2, "w_out_wgrad").reshape(N_DEV, D // N_DEV, D)
    gw_bra = _slabs_of_columns(_tn_matmul(att, dya, "w_br_attn_wgrad").reshape(Q_W, D))
    gw_brg = _slabs_of_columns(_tn_matmul(gm, dyg, "w_br_gmlp_wgrad").reshape(G_W, D))
    ex2 = exchange_start(2, [gw_bra, gw_brg, gw_out])
    dq, dkv, dbias, dsink = _attn_bwd(qkv, bias, sinks8, d_att)
    dzg, d_ws, d_bs, d_lg, d_lb = _gmlp_bwd(zg, d_gm, lg, lb, ws, bst)
    d_rel = _rel_bias_grad(dbias, bucket)
    early = jnp.concatenate([
        jnp.concatenate([d_lg.reshape(4, 128), d_lb.reshape(4, 128)], axis=0),
        d_bs, d_rel, dsink, d_ws.reshape(N_HEADS * BLK, BLK)], axis=0)
    sm0 = _slabs_start("gather_all", [early], sq, "small_gather_start")
    dh1, d_sh2, d_sc2, d_pre1 = _mix_dn(dq, dkv, dzg, dzgate, w_in_full, h1, dh2, sc2 + ex2[-1] + sm0[-1], pre[1])
    gw_in = jnp.concatenate(
        [_tn_matmul(dq, n2, "w_in_q_wgrad").reshape(Q_W, D),
         _tn_matmul(dkv, n2, "w_in_kv_wgrad").reshape(2 * KV_W, D),
         _tn_matmul(dzg, n2, "w_in_zg_wgrad").reshape(2 * G_W, D),
         _tn_matmul(dzgate, n2, "w_in_gate_wgrad").reshape(2 * D, D)], axis=0).reshape(N_DEV, IN_W // N_DEV, D)
    ex3 = exchange_start(3, [gw_in])

    dy1, dgu1, d_g1, d_post0 = _ffn_out_bwd(dh1, y1, fg1, fu1, wf1_out, g1 + ex3[-1], post[0], "ffn1_out_bwd")
    gw_f1_out = _tn_matmul(fa1, dy1, "ffn1_out_wgrad").reshape(N_DEV, D_FF // N_DEV, D)
    gw_f1_in = _tn_matmul(dgu1, n1, "ffn1_in_wgrad").reshape(N_DEV, FS, D)
    ex4 = exchange_start(4, [gw_f1_out, gw_f1_in])
    grad_x, d_sh1, d_sc1, d_pre0 = _ffn_dn(dgu1, wf1_in, x0, dh1, sc1 + ex4[-1], pre[0], "ffn1_dn")

    landed = {}
    for i, (ex, nms) in enumerate([(ex0, ["w_ffn2_out"]), (ex1, ["w_ffn2_in"]),
                                   (ex2, ["w_br_attn", "w_br_gmlp", "w_out"]), (ex3, ["w_in"]),
                                   (ex4, ["w_ffn1_out", "w_ffn1_in"])]):
        for nm, land in zip(nms, _slabs_wait("exchange", len(nms), ex, grad_x, "exchange_wait_%d" % i)):
            landed[nm] = land
    moments = [(m_w_ffn1_in, v_w_ffn1_in), (m_w_ffn1_out, v_w_ffn1_out), (m_w_in, v_w_in),
               (m_w_br_attn, v_w_br_attn), (m_w_br_gmlp, v_w_br_gmlp), (m_w_out, v_w_out),
               (m_w_ffn2_in, v_w_ffn2_in), (m_w_ffn2_out, v_w_ffn2_out)]
    names = ["w_ffn1_in", "w_ffn1_out", "w_in", "w_br_attn", "w_br_gmlp", "w_out", "w_ffn2_in", "w_ffn2_out"]
    big = {}
    for nm, w_, (m_, v_) in zip(names, shards, moments):
        if nm in transposed:
            res4 = _adamw_sharded(landed[nm], w_, m_[0].T, v_[0].T, "adamw_" + nm)
            big[nm] = [a.T[None] for a in res4]
        else:
            big[nm] = [a[None] for a in _adamw_sharded(landed[nm], w_, m_[0], v_[0], "adamw_" + nm)]

    d_ada = jnp.concatenate([v_.reshape(8, 128) for v_ in
                             (d_sh1, d_sc1, d_g1, d_sh2, d_sc2, d_g2, d_sh3, d_sc3, d_g3)], axis=0)
    d_pre = jnp.concatenate([d_pre0, d_pre1, d_pre2], axis=0)
    d_post = jnp.concatenate([d_post0, d_post1, d_post2], axis=0)
    late = jnp.concatenate([d_ada, _slabs_of_columns(d_pre).reshape(24, 128),
                            _slabs_of_columns(d_post).reshape(24, 128)], axis=0)
    tot, every = _small_allreduce(late)
    (early_land,) = _slabs_wait("gather_all", 1, sm0, grad_x, "small_gather_wait")
    tot_early = _sum_slabs(early_land)

    g_b_ada = tot[0:72].reshape(1, 9 * D)
    g_pre = lax.dynamic_slice_in_dim(tot[72:96], 3 * me, 3, axis=0)[None]
    g_post = lax.dynamic_slice_in_dim(tot[96:120], 3 * me, 3, axis=0)[None]
    g_lg = tot_early[0:4].reshape(1, G_W)
    g_lb = tot_early[4:8].reshape(1, G_W)
    g_bs = tot_early[8:16][None]
    g_rel = jnp.transpose(tot_early[16:24, 0:N_BUCKETS])
    g_sinks = tot_early[24:32, 0][None]
    g_ws = tot_early[32:1056].reshape(1, N_HEADS, BLK, BLK)

    d_ada_mine = lax.dynamic_slice_in_dim(every[:, 0:72].reshape(N_DEV, N_DEV, ADA_W), me, 1, axis=1)[:, 0]
    ada_out = [a[None] for a in _w_ada_update(gath8[:, 0:D], d_ada_mine, w_ada[0], m_w_ada[0], v_w_ada[0])]

    def small_step(w_, g_, m_, v_, nm):
        shp = w_.shape
        two_d = (int(math.prod(shp[:-1])), shp[-1])
        d_, m2_, v2_ = _adamw_small(w_.reshape(two_d), g_.reshape(two_d), m_.reshape(two_d), v_.reshape(two_d),
                                    "adamw_" + nm)
        return [g_, d_.reshape(shp), m2_.reshape(shp), v2_.reshape(shp)]

    res = {
        "rel_bias": small_step(rel_bias, g_rel, m_rel_bias, v_rel_bias, "rel_bias"),
        "w_ada": ada_out,
        "b_ada": small_step(b_ada, g_b_ada, m_b_ada, v_b_ada, "b_ada"),
        "pre_norm_g": small_step(pre_norm_g, g_pre, m_pre_norm_g, v_pre_norm_g, "pre_norm_g"),
        "post_norm_g": small_step(post_norm_g, g_post, m_post_norm_g, v_post_norm_g, "post_norm_g"),
        "sinks": small_step(sinks, g_sinks, m_sinks, v_sinks, "sinks"),
        "gmlp_ln_g": small_step(gmlp_ln_g, g_lg, m_gmlp_ln_g, v_gmlp_ln_g, "gmlp_ln_g"),
        "gmlp_ln_b": small_step(gmlp_ln_b, g_lb, m_gmlp_ln_b, v_gmlp_ln_b, "gmlp_ln_b"),
        "gmlp_w_s": small_step(gmlp_w_s, g_ws, m_gmlp_w_s, v_gmlp_w_s, "gmlp_w_s"),
        "gmlp_b_s": small_step(gmlp_b_s, g_bs, m_gmlp_b_s, v_gmlp_b_s, "gmlp_b_s"),
    }
    res.update(big)
    order = ["rel_bias", "w_ada", "b_ada", "pre_norm_g", "post_norm_g", "w_ffn1_in", "w_ffn1_out", "w_in", "sinks",
             "gmlp_ln_g", "gmlp_ln_b", "gmlp_w_s", "gmlp_b_s", "w_br_attn", "w_br_gmlp", "w_out", "w_ffn2_in",
             "w_ffn2_out"]
    outs = [loss, grad_x[None]]
    for k in range(4):
        outs += [res[nm][k] for nm in order]
    return tuple(outs)
```

```python
import functools
import math

import jax
import jax.numpy as jnp
from jax import lax
from jax.experimental import pallas as pl
from jax.experimental.pallas import tpu as pltpu

F32 = jnp.float32
BF = jnp.bfloat16

N_DEV = 8
D = 1024
D_FF = 2816
FS = D_FF // 4
N_HEADS = 8
N_KV = 2
GROUP = 4
HD = 64
BLK = 128
Q_W = 512
KV_W = 128
G_W = 512
QKV_W = Q_W + 2 * KV_W
ZG_OFF = QKV_W
GATE_OFF = ZG_OFF + 2 * G_W
IN_W = GATE_OFF + 2 * D
N_BUCKETS = 32
MAX_DISTANCE = 128
EPS = 1e-6
NEG = -1e30
SCALE = HD ** -0.5
ADA_W = 9 * D // N_DEV

ADAM_LR = 0.001
ADAM_B1 = 0.9
ADAM_B2 = 0.999
ADAM_EPS = 1e-08
ADAM_WD = 0.01
ADAM_STEP = 10

CHUNK = 256
MIB = 1024 * 1024
MESH = pl.DeviceIdType.MESH
HIGH = lax.Precision.HIGHEST


def _cp(n_grid, vmem_mib):
    return pltpu.CompilerParams(dimension_semantics=("arbitrary",) * n_grid,
                                vmem_limit_bytes=vmem_mib * MIB)


def _const(shape):
    return pl.BlockSpec(shape, lambda *_: (0,) * len(shape))


def _resident(shape):
    return pl.BlockSpec(shape, lambda *_: (0,) * len(shape), pipeline_mode=pl.Buffered(1))


def _sds(shape, dtype):
    return jax.ShapeDtypeStruct(shape, dtype)


def _dot(a, b):
    return jnp.dot(a, b, preferred_element_type=F32)


def _dot_nt(a, b):
    return lax.dot_general(a, b, (((1,), (1,)), ((), ())), preferred_element_type=F32)


def _dot_tn(a, b):
    return lax.dot_general(a, b, (((0,), (0,)), ((), ())), preferred_element_type=F32)


def _rms_r(x):
    return lax.rsqrt(jnp.mean(x * x, axis=-1, keepdims=True) + EPS)


def _colsum(x):
    return jnp.sum(x, axis=0, keepdims=True)


def _prenorm(x, gp, sc, sh):
    return (x * _rms_r(x) * gp) * (1.0 + sc) + sh


def _prenorm_bwd(dn, x, gp, sc):
    r = _rms_r(x)
    xh = x * r
    t = dn * (1.0 + sc) * gp
    dx = r * (t - xh * jnp.mean(t * xh, axis=-1, keepdims=True))
    return dx, _colsum(dn), _colsum(dn * xh * gp), _colsum(dn * (1.0 + sc) * xh)


def _postnorm_bwd(dh, y, gate, gp, res):
    r = _rms_r(y)
    yh = y * r
    dyn = (res * gate) * dh
    t = dyn * gp
    dy = r * (t - yh * jnp.mean(t * yh, axis=-1, keepdims=True))
    return dy, _colsum(res * dh * yh * gp), _colsum(dyn * yh)


def _gelu(x):
    k = math.sqrt(2.0 / math.pi)
    return 0.5 * x * (1.0 + jnp.tanh(k * (x + 0.044715 * x * x * x)))


def _gelu_grad(x):
    k = math.sqrt(2.0 / math.pi)
    t = jnp.tanh(k * (x + 0.044715 * x * x * x))
    return 0.5 * (1.0 + t) + 0.5 * x * (1.0 - t * t) * (k * (1.0 + 3.0 * 0.044715 * x * x))


def _my_place():
    x, y, c = lax.axis_index("x"), lax.axis_index("y"), lax.axis_index("c")
    return x, y, c, 4 * x + 2 * y + c


def _peer(x, y, c, k):
    px = 1 - x if k & 4 else x
    py = 1 - y if k & 2 else y
    pc = 1 - c if k & 1 else c
    return (px, py, pc), 4 * px + 2 * py + pc


HBM_SPEC = pl.BlockSpec(memory_space=pltpu.HBM)
SEM_SPEC = pl.BlockSpec(memory_space=pltpu.SEMAPHORE)
EFFECT = pltpu.SideEffectType.DATAFLOW_SIDE_EFFECTING


RELATIONS = {"exchange": (1, 2, 3, 4, 5, 6, 7), "gather": (1, 2, 4, 6), "forward": (2, 4, 6),
             "gather_all": (1, 2, 3, 4, 5, 6, 7)}


def _slab_copies(mode, srcs, lands, send, recv, loc):
    x, y, c, me = _my_place()
    rel = RELATIONS[mode]
    remote, local = [], []
    for t in range(len(lands)):
        for i, k in enumerate(rel):
            peer, peer_lin = _peer(x, y, c, k)
            if mode == "exchange":
                src, dst, to = srcs[t].at[peer_lin], lands[t].at[me], peer
            elif mode in ("gather", "gather_all"):
                src, dst, to = srcs[t], lands[t].at[me], peer
            else:
                src, dst, to = lands[t].at[peer_lin], lands[t].at[peer_lin], _peer(x, y, c, 1)[0]
            remote.append(pltpu.make_async_remote_copy(
                src_ref=src, dst_ref=dst, send_sem=send.at[t * len(rel) + i], recv_sem=recv.at[t * len(rel) + i],
                device_id=to, device_id_type=MESH))
        if mode == "exchange":
            local.append(pltpu.make_async_copy(srcs[t].at[me], lands[t].at[me], loc.at[t]))
        elif mode in ("gather", "gather_all"):
            local.append(pltpu.make_async_copy(srcs[t], lands[t].at[me], loc.at[t]))
    return remote, local


def _slabs_start(mode, arrays, after, name):
    n = len(arrays)
    if mode == "forward":
        thru = list(arrays)
    else:
        shapes = [a.shape if mode == "exchange" else (N_DEV,) + a.shape for a in arrays]
        thru = list(arrays) + [lax.empty(s, a.dtype) for s, a in zip(shapes, arrays)]
    m = len(thru)
    n_sem = n * len(RELATIONS[mode])

    def body(*refs):
        srcs, lands = refs[:n], refs[m - n:m]
        send, recv, loc = refs[m + 1:m + 4]
        remote, local = _slab_copies(mode, srcs, lands, send, recv, loc)
        for cp in remote + local:
            cp.start()
        refs[-1][...] = jnp.zeros_like(refs[-1])

    return pl.pallas_call(
        body, name=name,
        out_shape=(pltpu.SemaphoreType.DMA((n_sem,)), pltpu.SemaphoreType.DMA((n_sem,)),
                   pltpu.SemaphoreType.DMA((n,)),
                   *[pltpu.HBM(a.shape, a.dtype) for a in thru],
                   _sds((1, D), F32)),
        in_specs=[HBM_SPEC] * m + [pl.BlockSpec(memory_space=pl.ANY)],
        out_specs=(SEM_SPEC, SEM_SPEC, SEM_SPEC, *[HBM_SPEC] * m, pl.BlockSpec(memory_space=pltpu.VMEM)),
        input_output_aliases={t: 3 + t for t in range(m)},
        compiler_params=pltpu.CompilerParams(has_side_effects=EFFECT),
    )(*[pltpu.with_memory_space_constraint(a, pltpu.HBM) for a in thru], after)


def _slabs_wait(mode, n, started, after, name):
    sems = started[0:3]
    thru = started[3:-1]
    m = len(thru)

    def body(*refs):
        srcs, lands = refs[:n], refs[m - n:m]
        remote, local = _slab_copies(mode, srcs, lands, *refs[m:m + 3])
        for cp in remote:
            cp.wait_send()
            cp.wait_recv()
        for cp in local:
            cp.wait()

    res = pl.pallas_call(
        body, name=name,
        out_shape=tuple(pltpu.HBM(a.shape, a.dtype) for a in thru),
        in_specs=[HBM_SPEC] * m + [SEM_SPEC] * 3 + [pl.BlockSpec(memory_space=pl.ANY)],
        out_specs=tuple([HBM_SPEC] * m),
        input_output_aliases={t: t for t in range(m)},
        compiler_params=pltpu.CompilerParams(has_side_effects=EFFECT),
    )(*thru, *sems, after)
    return list(res[m - n:m])


def _ada_forward(small8, w_ada, b_ada64):
    sw = small8.shape[1]

    def body(sm_ref, w_ref, b_ref, gath_ref, ada_ref, part_ref, send1, recv1, send2, recv2):
        x, y, c, me = _my_place()
        row_me = pl.multiple_of(me * 8, 8)
        gath_ref[pl.ds(row_me, 8), :] = sm_ref[...]
        first = []
        for k in range(1, N_DEV):
            peer, _ = _peer(x, y, c, k)
            cp = pltpu.make_async_remote_copy(
                src_ref=sm_ref, dst_ref=gath_ref.at[pl.ds(row_me, 8), :], send_sem=send1.at[k - 1],
                recv_sem=recv1.at[k - 1], device_id=peer, device_id_type=MESH)
            cp.start()
            first.append(cp)
        for cp in first:
            cp.wait()
        cs = gath_ref[:, 0:D]
        cs = cs * jax.nn.sigmoid(cs)
        part_ref[...] = jnp.dot(cs, w_ref[...], preferred_element_type=F32, precision=HIGH)
        ada_ref[pl.ds(row_me, 8), :] = part_ref[pl.ds(row_me, 8), :]
        second = []
        for k in range(1, N_DEV):
            peer, peer_lin = _peer(x, y, c, k)
            cp = pltpu.make_async_remote_copy(
                src_ref=part_ref.at[pl.ds(pl.multiple_of(peer_lin * 8, 8), 8), :],
                dst_ref=ada_ref.at[pl.ds(row_me, 8), :], send_sem=send2.at[k - 1],
                recv_sem=recv2.at[k - 1], device_id=peer, device_id_type=MESH)
            cp.start()
            second.append(cp)
        for cp in second:
            cp.wait()
        ada_ref[...] = ada_ref[...] + b_ref[...]

    vm = pl.BlockSpec(memory_space=pltpu.VMEM)
    return pl.pallas_call(
        body, name="ada_forward",
        out_shape=[_sds((8 * N_DEV, sw), F32), _sds((8 * N_DEV, ADA_W), F32)],
        in_specs=[vm, vm, vm], out_specs=[vm, vm],
        scratch_shapes=[pltpu.VMEM((8 * N_DEV, ADA_W), F32)] + [pltpu.SemaphoreType.DMA((7,))] * 4,
        compiler_params=pltpu.CompilerParams(vmem_limit_bytes=32 * MIB),
    )(small8, w_ada, b_ada64)


def _sum_slabs(land):
    def body(l_ref, o_ref):
        acc = l_ref[0]
        for j in range(1, N_DEV):
            acc = acc + l_ref[j]
        o_ref[...] = acc

    vm = pl.BlockSpec(memory_space=pltpu.VMEM)
    return pl.pallas_call(body, name="sum_slabs", out_shape=_sds(land.shape[1:], F32), in_specs=[vm], out_specs=vm,
                          compiler_params=pltpu.CompilerParams(vmem_limit_bytes=32 * MIB))(land)


def _small_allreduce(pack):
    rows = pack.shape[0]

    def body(p_ref, sum_ref, gath_ref, send, recv):
        x, y, c, me = _my_place()
        gath_ref[me] = p_ref[...]
        cps = []
        for k in range(1, N_DEV):
            peer, _ = _peer(x, y, c, k)
            cp = pltpu.make_async_remote_copy(
                src_ref=p_ref, dst_ref=gath_ref.at[me], send_sem=send.at[k - 1],
                recv_sem=recv.at[k - 1], device_id=peer, device_id_type=MESH)
            cp.start()
            cps.append(cp)
        for cp in cps:
            cp.wait()
        acc = gath_ref[0]
        for j in range(1, N_DEV):
            acc = acc + gath_ref[j]
        sum_ref[...] = acc

    vm = pl.BlockSpec(memory_space=pltpu.VMEM)
    return pl.pallas_call(
        body, name="small_allreduce",
        out_shape=[_sds((rows, 128), F32), _sds((N_DEV, rows, 128), F32)],
        in_specs=[vm], out_specs=[vm, vm],
        scratch_shapes=[pltpu.SemaphoreType.DMA((7,)), pltpu.SemaphoreType.DMA((7,))],
        compiler_params=pltpu.CompilerParams(vmem_limit_bytes=40 * MIB),
    )(pack)


def _ffn_in(h, sh, sc, gp, wt8, name):
    S = h.shape[0]
    R = min(512, S)

    def body(h_ref, sh_ref, sc_ref, gp_ref, w_ref, n_ref, dg_ref, sl_ref, a_ref):
        for r0 in range(0, R, CHUNK):
            rows = slice(r0, r0 + CHUNK)
            n = _prenorm(h_ref[rows, :], gp_ref[...], sc_ref[...], sh_ref[...]).astype(BF)
            n_ref[rows, :] = n
            for s in range(4):
                g = _dot_nt(n, w_ref[s])
                u = _dot_nt(n, w_ref[s + 4])
                sg = jax.nn.sigmoid(g)
                silu = g * sg
                dg_ref[s, rows, :] = (u * (sg * (1.0 + g * (1.0 - sg)))).astype(BF)
                sl_ref[s, rows, :] = silu.astype(BF)
                a_ref[s, rows, :] = (silu * u).astype(BF)

    vec = _const((1, D))
    row = pl.BlockSpec((R, D), lambda i: (i, 0))
    blk = pl.BlockSpec((4, R, FS), lambda i: (0, i, 0))
    return pl.pallas_call(
        body, name=name, grid=(S // R,),
        out_shape=[_sds((S, D), BF)] + [_sds((4, S, FS), BF)] * 3,
        in_specs=[row, vec, vec, vec, _resident((N_DEV, FS, D))],
        out_specs=[row, blk, blk, blk],
        compiler_params=_cp(1, 56),
    )(h, sh, sc, gp, wt8)


def _ffn_out(a, w4, h, gate, gp, name):
    S = h.shape[0]
    R = min(512, S)

    def body(a_ref, w_ref, h_ref, gate_ref, gp_ref, hn_ref, y_ref):
        for r0 in range(0, R, CHUNK):
            rows = slice(r0, r0 + CHUNK)
            y = _dot(a_ref[0, rows, :], w_ref[0])
            for s in range(1, 4):
                y = y + _dot(a_ref[s, rows, :], w_ref[s])
            y_ref[rows, :] = y
            hn_ref[rows, :] = h_ref[rows, :] + (0.5 * gate_ref[...]) * (y * _rms_r(y) * gp_ref[...])

    vec = _const((1, D))
    row = pl.BlockSpec((R, D), lambda i: (i, 0))
    return pl.pallas_call(
        body, name=name, grid=(S // R,),
        out_shape=[_sds((S, D), F32), _sds((S, D), F32)],
        in_specs=[pl.BlockSpec((4, R, FS), lambda i: (0, i, 0)), _resident((4, FS, D)), row, vec, vec],
        out_specs=[row, row],
        compiler_params=_cp(1, 48),
    )(a, w4, h, gate, gp)


def _ffn_fwd_loss(h, sh, sc, gpre, wt8, w4, gate, gpost, target, name):
    S = h.shape[0]
    R = min(256, S)

    def body(h_ref, sh_ref, sc_ref, gpre_ref, w_ref, w4_ref, gate_ref, gpost_ref, t_ref,
             n_ref, dg_ref, sl_ref, a_ref, y_ref, dh_ref, tot_ref):
        @pl.when(pl.program_id(0) == 0)
        def _():
            tot_ref[...] = jnp.zeros_like(tot_ref)
        hh = h_ref[...]
        n = _prenorm(hh, gpre_ref[...], sc_ref[...], sh_ref[...]).astype(BF)
        n_ref[...] = n
        y = None
        for s in range(4):
            g = _dot_nt(n, w_ref[s])
            u = _dot_nt(n, w_ref[s + 4])
            sg = jax.nn.sigmoid(g)
            silu = g * sg
            dg_ref[s] = (u * (sg * (1.0 + g * (1.0 - sg)))).astype(BF)
            sl_ref[s] = silu.astype(BF)
            a = (silu * u).astype(BF)
            a_ref[s] = a
            part = _dot(a, w4_ref[s])
            y = part if y is None else y + part
        y_ref[...] = y
        e = hh + (0.5 * gate_ref[...]) * (y * _rms_r(y) * gpost_ref[...]) - t_ref[...]
        dh_ref[...] = e * (1.0 / D)
        tot_ref[...] += jnp.sum(jnp.sum(e * e, axis=1, keepdims=True), axis=0, keepdims=True)

    vec = _const((1, D))
    row = pl.BlockSpec((R, D), lambda i: (i, 0))
    blk = pl.BlockSpec((4, R, FS), lambda i: (0, i, 0))
    return pl.pallas_call(
        body, name=name, grid=(S // R,),
        out_shape=[_sds((S, D), BF)] + [_sds((4, S, FS), BF)] * 3 + [_sds((S, D), F32)] * 2 + [_sds((1, 1), F32)],
        in_specs=[row, vec, vec, vec, _resident((N_DEV, FS, D)), _resident((4, FS, D)), vec, vec, row],
        out_specs=[row, blk, blk, blk, row, row, _const((1, 1))],
        compiler_params=_cp(1, 56),
    )(h, sh, sc, gpre, wt8, w4, gate, gpost, target)


def _ffn_out_bwd(dh, y, dsilu_u, silu, w4, gate, gp, name):
    S = dh.shape[0]
    R = min(512, S)

    def body(dh_ref, y_ref, g_ref, u_ref, w_ref, gate_ref, gp_ref, dy_ref, dgu_ref, dgate_ref, dgp_ref):
        @pl.when(pl.program_id(0) == 0)
        def _():
            dgate_ref[...] = jnp.zeros_like(dgate_ref)
            dgp_ref[...] = jnp.zeros_like(dgp_ref)
        for r0 in range(0, R, CHUNK):
            rows = slice(r0, r0 + CHUNK)
            dy, dgate, dgp = _postnorm_bwd(dh_ref[rows, :], y_ref[rows, :], gate_ref[...], gp_ref[...], 0.5)
            dgate_ref[...] += dgate
            dgp_ref[...] += dgp
            dyb = dy.astype(BF)
            dy_ref[rows, :] = dyb
            for s in range(4):
                da = _dot_nt(dyb, w_ref[s])
                dgu_ref[s, rows, :] = (da * g_ref[s, rows, :].astype(F32)).astype(BF)
                dgu_ref[s + 4, rows, :] = (da * u_ref[s, rows, :].astype(F32)).astype(BF)

    vec = _const((1, D))
    row = pl.BlockSpec((R, D), lambda i: (i, 0))
    blk4 = pl.BlockSpec((4, R, FS), lambda i: (0, i, 0))
    return pl.pallas_call(
        body, name=name, grid=(S // R,),
        out_shape=[_sds((S, D), BF), _sds((8, S, FS), BF), _sds((1, D), F32), _sds((1, D), F32)],
        in_specs=[row, row, blk4, blk4, _resident((4, FS, D)), vec, vec],
        out_specs=[row, pl.BlockSpec((8, R, FS), lambda i: (0, i, 0)), vec, vec],
        compiler_params=_cp(1, 56),
    )(dh, y, dsilu_u, silu, w4, gate, gp)


def _ffn_dn(dgu, wt8, h, dh, sc, gp, name):
    S = h.shape[0]
    R = min(512, S)

    def body(dgu_ref, w_ref, h_ref, dh_ref, sc_ref, gp_ref, out_ref, dsh_ref, dsc_ref, dgp_ref):
        @pl.when(pl.program_id(0) == 0)
        def _():
            dsh_ref[...] = jnp.zeros_like(dsh_ref)
            dsc_ref[...] = jnp.zeros_like(dsc_ref)
            dgp_ref[...] = jnp.zeros_like(dgp_ref)

        for r0 in range(0, R, CHUNK):
            rows = slice(r0, r0 + CHUNK)
            dn = _dot(dgu_ref[0, rows, :], w_ref[0])
            for j in range(1, N_DEV):
                dn = dn + _dot(dgu_ref[j, rows, :], w_ref[j])
            dx, dsh, dsc, dgp = _prenorm_bwd(dn, h_ref[rows, :], gp_ref[...], sc_ref[...])
            out_ref[rows, :] = dh_ref[rows, :] + dx
            dsh_ref[...] += dsh
            dsc_ref[...] += dsc
            dgp_ref[...] += dgp

    vec = _const((1, D))
    row = pl.BlockSpec((R, D), lambda i: (i, 0))
    return pl.pallas_call(
        body, name=name, grid=(S // R,),
        out_shape=[_sds((S, D), F32)] + [_sds((1, D), F32)] * 3,
        in_specs=[pl.BlockSpec((N_DEV, R, FS), lambda i: (0, i, 0)), _resident((N_DEV, FS, D)),
                  row, row, vec, vec],
        out_specs=[row, vec, vec, vec],
        compiler_params=_cp(1, 56),
    )(dgu, wt8, h, dh, sc, gp)


def _ffn_bwd(dh, y, dsilu_u, silu, w4, wt8, h, gate, gpost, sc, gpre, name):
    S = dh.shape[0]
    R = min(256, S)

    def body(dh_ref, y_ref, g_ref, u_ref, w4_ref, w_ref, h_ref, gate_ref, gpost_ref, sc_ref, gpre_ref,
             dy_ref, dgu_ref, out_ref, dgate_ref, dgpost_ref, dsh_ref, dsc_ref, dgpre_ref):
        @pl.when(pl.program_id(0) == 0)
        def _():
            for r in (dgate_ref, dgpost_ref, dsh_ref, dsc_ref, dgpre_ref):
                r[...] = jnp.zeros_like(r)
        dhh = dh_ref[...]
        dy, dgate, dgpost = _postnorm_bwd(dhh, y_ref[...], gate_ref[...], gpost_ref[...], 0.5)
        dgate_ref[...] += dgate
        dgpost_ref[...] += dgpost
        dyb = dy.astype(BF)
        dy_ref[...] = dyb
        dn = None
        for s in range(4):
            da = _dot_nt(dyb, w4_ref[s])
            dg = (da * g_ref[s].astype(F32)).astype(BF)
            du = (da * u_ref[s].astype(F32)).astype(BF)
            dgu_ref[s] = dg
            dgu_ref[s + 4] = du
            part = _dot(dg, w_ref[s]) + _dot(du, w_ref[s + 4])
            dn = part if dn is None else dn + part
        dx, dsh, dsc, dgpre = _prenorm_bwd(dn, h_ref[...], gpre_ref[...], sc_ref[...])
        out_ref[...] = dhh + dx
        dsh_ref[...] += dsh
        dsc_ref[...] += dsc
        dgpre_ref[...] += dgpre

    vec = _const((1, D))
    row = pl.BlockSpec((R, D), lambda i: (i, 0))
    blk4 = pl.BlockSpec((4, R, FS), lambda i: (0, i, 0))
    return pl.pallas_call(
        body, name=name, grid=(S // R,),
        out_shape=[_sds((S, D), BF), _sds((8, S, FS), BF), _sds((S, D), F32)] + [_sds((1, D), F32)] * 5,
        in_specs=[row, row, blk4, blk4, _resident((4, FS, D)), _resident((N_DEV, FS, D)), row, vec, vec, vec, vec],
        out_specs=[row, pl.BlockSpec((8, R, FS), lambda i: (0, i, 0)), row] + [vec] * 5,
        compiler_params=_cp(1, 56),
    )(dh, y, dsilu_u, silu, w4, wt8, h, gate, gpost, sc, gpre)


def _tn_matmul(a, b, name):
    a3 = a if a.ndim == 3 else a[None]
    b3 = b if b.ndim == 3 else b[None]
    GA, S, M = a3.shape
    GB, _, N = b3.shape
    ts = min(2048, S)
    nk = S // ts
    chunks = [(m0, min(CHUNK, M - m0)) for m0 in range(0, M, CHUNK)]

    def body(a_ref, b_ref, o_ref, acc):
        k = pl.program_id(2)

        @pl.when(k == 0)
        def _():
            acc[...] = jnp.zeros_like(acc)

        for m0, mc in chunks:
            acc[m0:m0 + mc, :] += _dot_tn(a_ref[:, m0:m0 + mc], b_ref[...])

        @pl.when(k == nk - 1)
        def _():
            for m0, mc in chunks:
                o_ref[m0:m0 + mc, :] = acc[m0:m0 + mc, :].astype(BF)

    return pl.pallas_call(
        body, name=name, grid=(GA, GB, nk),
        out_shape=_sds((GA, GB, M, N), BF),
        in_specs=[pl.BlockSpec((None, ts, M), lambda ga, gb, k: (ga, k, 0)),
                  pl.BlockSpec((None, ts, N), lambda ga, gb, k: (gb, k, 0))],
        out_specs=pl.BlockSpec((None, None, M, N), lambda ga, gb, k: (ga, gb, 0, 0)),
        scratch_shapes=[pltpu.VMEM((M, N), F32)],
        compiler_params=_cp(3, 56),
    )(a3, b3)


def _mix_in(h, sh, sc, gp, w):
    S = h.shape[0]
    R = min(512, S)

    def body(h_ref, sh_ref, sc_ref, gp_ref, w_ref, n_ref, qkv_ref, zg_ref, gates_ref):
        for r0 in range(0, R, CHUNK):
            rows = slice(r0, r0 + CHUNK)
            nb = _prenorm(h_ref[rows, :], gp_ref[...], sc_ref[...], sh_ref[...]).astype(BF)
            n_ref[rows, :] = nb
            qkv_ref[rows, :] = _dot_nt(nb, w_ref[0:ZG_OFF, :]).astype(BF)
            zg_ref[rows, :] = _dot_nt(nb, w_ref[ZG_OFF:GATE_OFF, :]).astype(BF)
            gates_ref[rows, :] = jax.nn.sigmoid(_dot_nt(nb, w_ref[GATE_OFF:IN_W, :])).astype(BF)

    vec = _const((1, D))
    rows = lambda w_: pl.BlockSpec((R, w_), lambda i: (i, 0))
    return pl.pallas_call(
        body, name="mix_in", grid=(S // R,),
        out_shape=[_sds((S, D), BF), _sds((S, QKV_W), BF), _sds((S, 2 * G_W), BF), _sds((S, 2 * D), BF)],
        in_specs=[rows(D), vec, vec, vec, _resident((IN_W, D))],
        out_specs=[rows(D), rows(QKV_W), rows(2 * G_W), rows(2 * D)],
        compiler_params=_cp(1, 48),
    )(h, sh, sc, gp, w)


def _bias_table(rel_bias, bucket):
    def body(rel_ref, bk_ref, out_ref):
        bk = bk_ref[...]
        qi = lax.broadcasted_iota(jnp.int32, (BLK, 2 * BLK), 0)
        kj = lax.broadcasted_iota(jnp.int32, (BLK, 2 * BLK), 1)
        dist = qi + BLK - kj
        window = (dist >= 0) & (dist < BLK)
        for h in range(N_HEADS):
            acc = jnp.zeros((BLK, 2 * BLK), F32)
            for b in range(N_BUCKETS):
                acc = jnp.where(bk == b, rel_ref[b, h], acc)
            out_ref[h // GROUP, pl.ds((h % GROUP) * BLK, BLK), :] = jnp.where(window, acc, NEG)

    return pl.pallas_call(
        body, name="bias_table",
        out_shape=_sds((N_KV, GROUP * BLK, 2 * BLK), F32),
        in_specs=[pl.BlockSpec(memory_space=pltpu.SMEM), pl.BlockSpec(memory_space=pltpu.VMEM)],
        out_specs=pl.BlockSpec(memory_space=pltpu.VMEM),
    )(rel_bias, bucket)


def _attn_scores(q, kvc, kvp, bias_ref, sink_ref, blk, kh):
    k2 = jnp.concatenate([kvp[:, kh * HD:(kh + 1) * HD], kvc[:, kh * HD:(kh + 1) * HD]], axis=0)
    v2 = jnp.concatenate([kvp[:, KV_W + kh * HD:KV_W + (kh + 1) * HD],
                          kvc[:, KV_W + kh * HD:KV_W + (kh + 1) * HD]], axis=0)
    q4 = jnp.concatenate([q[:, (kh * GROUP + g) * HD:(kh * GROUP + g + 1) * HD] for g in range(GROUP)], axis=0)
    s = _dot_nt(q4, k2) * SCALE + bias_ref[kh]
    col = lax.broadcasted_iota(jnp.int32, (GROUP * BLK, 2 * BLK), 1)
    s = jnp.where((col >= BLK) | (blk > 0), s, NEG)
    rowg = lax.broadcasted_iota(jnp.int32, (GROUP * BLK, 1), 0) // BLK
    sink = jnp.zeros((GROUP * BLK, 1), F32)
    for g in range(GROUP):
        sink = jnp.where(rowg == g, sink_ref[kh * GROUP + g], sink)
    return q4, k2, v2, s, sink


def _attn_fwd(qkv, bias, sinks):
    S = qkv.shape[0]
    nb = S // BLK

    def body(sink_ref, q_ref, kvc_ref, kvp_ref, bias_ref, o_ref):
        blk = pl.program_id(0)
        q, kvc, kvp = q_ref[...], kvc_ref[...], kvp_ref[...]
        outs = []
        for kh in range(N_KV):
            q4, k2, v2, s, sink = _attn_scores(q, kvc, kvp, bias_ref, sink_ref, blk, kh)
            m = jnp.maximum(jnp.max(s, axis=1, keepdims=True), sink)
            p = jnp.exp(s - m)
            denom = jnp.sum(p, axis=1, keepdims=True) + jnp.exp(sink - m)
            o4 = _dot((p / denom).astype(BF), v2)
            outs += [o4[g * BLK:(g + 1) * BLK] for g in range(GROUP)]
        o_ref[...] = jnp.concatenate(outs, axis=1).astype(BF)

    return pl.pallas_call(
        body, name="attn_fwd", grid=(nb,),
        out_shape=_sds((S, Q_W), BF),
        in_specs=[pl.BlockSpec(memory_space=pltpu.SMEM),
                  pl.BlockSpec((BLK, Q_W), lambda i: (i, 0)),
                  pl.BlockSpec((BLK, 2 * KV_W), lambda i: (i, 2)),
                  pl.BlockSpec((BLK, 2 * KV_W), lambda i: (jnp.maximum(i - 1, 0), 2)),
                  _const((N_KV, GROUP * BLK, 2 * BLK))],
        out_specs=pl.BlockSpec((BLK, Q_W), lambda i: (i, 0)),
        compiler_params=_cp(1, 32),
    )(sinks, qkv, qkv, qkv, bias)


def _attn_bwd(qkv, bias, sinks, do):
    S = qkv.shape[0]
    nb = S // BLK

    def body(sink_ref, q_ref, kvc_ref, kvp_ref, bias_ref, do_ref, dq_ref, dkv_ref, dbias_ref, dsink_ref, carry):
        i = pl.program_id(0)
        blk = nb - 1 - i

        @pl.when(i == 0)
        def _():
            carry[...] = jnp.zeros_like(carry)
            dbias_ref[...] = jnp.zeros_like(dbias_ref)
            dsink_ref[...] = jnp.zeros_like(dsink_ref)

        q, kvc, kvp, do_ = q_ref[...], kvc_ref[...], kvp_ref[...], do_ref[...]
        dqs, dk_cur, dv_cur, dk_prev, dv_prev = [], [], [], [], []
        for kh in range(N_KV):
            q4, k2, v2, s, sink = _attn_scores(q, kvc, kvp, bias_ref, sink_ref, blk, kh)
            m = jnp.maximum(jnp.max(s, axis=1, keepdims=True), sink)
            p = jnp.exp(s - m)
            denom = jnp.sum(p, axis=1, keepdims=True) + jnp.exp(sink - m)
            prob = p / denom
            p_sink = jnp.exp(sink - m) / denom
            pb = prob.astype(BF)
            do4 = jnp.concatenate(
                [do_[:, (kh * GROUP + g) * HD:(kh * GROUP + g + 1) * HD] for g in range(GROUP)], axis=0)
            dp = _dot_nt(do4, v2)
            o4 = _dot(pb, v2)
            delta = jnp.sum(do4.astype(F32) * o4, axis=1, keepdims=True)
            ds = prob * (dp - delta)
            dbias_ref[kh] += ds
            sink_term = p_sink * delta
            for g in range(GROUP):
                h = kh * GROUP + g
                val = -jnp.sum(sink_term[g * BLK:(g + 1) * BLK], axis=0, keepdims=True)
                dsink_ref[pl.ds(h, 1), :] += jnp.broadcast_to(val, (1, 128))
            dsb = ds.astype(BF)
            dq4 = _dot(dsb, k2) * SCALE
            dk2 = _dot_tn(dsb, q4) * SCALE
            dv2 = _dot_tn(pb, do4)
            dqs += [dq4[g * BLK:(g + 1) * BLK] for g in range(GROUP)]
            dk_prev.append(dk2[0:BLK])
            dk_cur.append(dk2[BLK:2 * BLK])
            dv_prev.append(dv2[0:BLK])
            dv_cur.append(dv2[BLK:2 * BLK])
        dq_ref[...] = jnp.concatenate(dqs, axis=1).astype(BF)
        dkv_ref[...] = (jnp.concatenate(dk_cur + dv_cur, axis=1) + carry[...]).astype(BF)
        carry[...] = jnp.concatenate(dk_prev + dv_prev, axis=1)

    return pl.pallas_call(
        body, name="attn_bwd", grid=(nb,),
        out_shape=[_sds((S, Q_W), BF), _sds((S, 2 * KV_W), BF),
                   _sds((N_KV, GROUP * BLK, 2 * BLK), F32), _sds((N_HEADS, 128), F32)],
        in_specs=[pl.BlockSpec(memory_space=pltpu.SMEM),
                  pl.BlockSpec((BLK, Q_W), lambda i: (nb - 1 - i, 0)),
                  pl.BlockSpec((BLK, 2 * KV_W), lambda i: (nb - 1 - i, 2)),
                  pl.BlockSpec((BLK, 2 * KV_W), lambda i: (jnp.maximum(nb - 2 - i, 0), 2)),
                  _const((N_KV, GROUP * BLK, 2 * BLK)),
                  pl.BlockSpec((BLK, Q_W), lambda i: (nb - 1 - i, 0))],
        out_specs=[pl.BlockSpec((BLK, Q_W), lambda i: (nb - 1 - i, 0)),
                   pl.BlockSpec((BLK, 2 * KV_W), lambda i: (nb - 1 - i, 0)),
                   _const((N_KV, GROUP * BLK, 2 * BLK)), _const((N_HEADS, 128))],
        scratch_shapes=[pltpu.VMEM((BLK, 2 * KV_W), F32)],
        compiler_params=_cp(1, 32),
    )(sinks, qkv, qkv, qkv, bias, do)


def _rel_bias_grad(dbias, bucket):
    def body(db_ref, bk_ref, out_ref):
        bk = bk_ref[...]
        lane = lax.broadcasted_iota(jnp.int32, (1, 128), 1)
        for h in range(N_HEADS):
            d = db_ref[h // GROUP, pl.ds((h % GROUP) * BLK, BLK), :]
            row = jnp.zeros((1, 128), F32)
            for b in range(N_BUCKETS):
                tot = jnp.sum(jnp.sum(jnp.where(bk == b, d, 0.0), axis=1, keepdims=True), axis=0, keepdims=True)
                row = jnp.where(lane == b, tot, row)
            out_ref[pl.ds(h, 1), :] = row

    vm = pl.BlockSpec(memory_space=pltpu.VMEM)
    return pl.pallas_call(body, name="rel_bias_grad", out_shape=_sds((N_HEADS, 128), F32),
                          in_specs=[vm, vm], out_specs=vm)(dbias, bucket)


def _gmlp_parts(zg_ref, lg_ref, lb_ref):
    z = zg_ref[...].astype(F32)
    ge = _gelu(z)
    u, vg = ge[:, 0:G_W], ge[:, G_W:2 * G_W]
    mu = jnp.mean(vg, axis=-1, keepdims=True)
    xc = vg - mu
    rstd = lax.rsqrt(jnp.mean(xc * xc, axis=-1, keepdims=True) + EPS)
    xh = xc * rstd
    return z, u, xh, rstd, xh * lg_ref[...] + lb_ref[...]


def _causal_weights(ws_ref, wc):
    t = lax.broadcasted_iota(jnp.int32, (BLK, BLK), 0)
    s = lax.broadcasted_iota(jnp.int32, (BLK, BLK), 1)
    for g in range(N_HEADS):
        wc[g] = jnp.where(s <= t, ws_ref[g], 0.0).astype(BF)


def _spatial(vb, wc, bst_ref, p, low):
    xp = vb[:, p * 128:(p + 1) * 128]
    s0 = _dot(wc[2 * p], xp) + bst_ref[:, 2 * p:2 * p + 1]
    s1 = _dot(wc[2 * p + 1], xp) + bst_ref[:, 2 * p + 1:2 * p + 2]
    return xp, jnp.where(low, s0, s1)


def _gmlp_fwd(zg, lg, lb, ws, bst):
    S = zg.shape[0]

    def body(zg_ref, lg_ref, lb_ref, ws_ref, bst_ref, o_ref, wc):
        @pl.when(pl.program_id(0) == 0)
        def _():
            _causal_weights(ws_ref, wc)
        _, u, _, _, vln = _gmlp_parts(zg_ref, lg_ref, lb_ref)
        vb = vln.astype(BF)
        low = lax.broadcasted_iota(jnp.int32, (BLK, 128), 1) < HD
        for p in range(4):
            _, sp = _spatial(vb, wc, bst_ref, p, low)
            o_ref[:, p * 128:(p + 1) * 128] = (u[:, p * 128:(p + 1) * 128] * sp).astype(BF)

    return pl.pallas_call(
        body, name="gmlp_fwd", grid=(S // BLK,),
        out_shape=_sds((S, G_W), BF),
        in_specs=[pl.BlockSpec((BLK, 2 * G_W), lambda i: (i, 0)), _const((1, G_W)), _const((1, G_W)),
                  _const((N_HEADS, BLK, BLK)), _const((BLK, N_HEADS))],
        out_specs=pl.BlockSpec((BLK, G_W), lambda i: (i, 0)),
        scratch_shapes=[pltpu.VMEM((N_HEADS, BLK, BLK), BF)],
        compiler_params=_cp(1, 32),
    )(zg, lg, lb, ws, bst)


def _gmlp_bwd(zg, d_out, lg, lb, ws, bst):
    S = zg.shape[0]
    nb = S // BLK

    def body(zg_ref, d_ref, lg_ref, lb_ref, ws_ref, bst_ref, dzg_ref, dws_ref, dbs_ref, dlg_ref, dlb_ref, wc, dbacc):
        i = pl.program_id(0)

        @pl.when(i == 0)
        def _():
            _causal_weights(ws_ref, wc)
            dws_ref[...] = jnp.zeros_like(dws_ref)
            dlg_ref[...] = jnp.zeros_like(dlg_ref)
            dlb_ref[...] = jnp.zeros_like(dlb_ref)
            dbacc[...] = jnp.zeros_like(dbacc)

        z, u, xh, rstd, vln = _gmlp_parts(zg_ref, lg_ref, lb_ref)
        vb = vln.astype(BF)
        d = d_ref[...].astype(F32)
        low = lax.broadcasted_iota(jnp.int32, (BLK, 128), 1) < HD
        du_parts, dvln_parts = [], []
        for p in range(4):
            xp, sp = _spatial(vb, wc, bst_ref, p, low)
            dp = d[:, p * 128:(p + 1) * 128]
            du_parts.append(dp * sp)
            dsp = dp * u[:, p * 128:(p + 1) * 128]
            dbacc[:, p * 128:(p + 1) * 128] += dsp
            d0 = jnp.where(low, dsp, 0.0).astype(BF)
            d1 = jnp.where(low, 0.0, dsp).astype(BF)
            dws_ref[2 * p] += _dot_nt(d0, xp)
            dws_ref[2 * p + 1] += _dot_nt(d1, xp)
            dvln_parts.append(_dot_tn(wc[2 * p], d0) + _dot_tn(wc[2 * p + 1], d1))
        dvln = jnp.concatenate(dvln_parts, axis=1)
        dlg_ref[...] += _colsum(dvln * xh)
        dlb_ref[...] += _colsum(dvln)
        dxh = dvln * lg_ref[...]
        dvg = rstd * (dxh - jnp.mean(dxh, axis=-1, keepdims=True)
                      - xh * jnp.mean(dxh * xh, axis=-1, keepdims=True))
        dge = jnp.concatenate(du_parts + [dvg], axis=1)
        dzg_ref[...] = (dge * _gelu_grad(z)).astype(BF)

        @pl.when(i == nb - 1)
        def _():
            t = lax.broadcasted_iota(jnp.int32, (BLK, BLK), 0)
            s = lax.broadcasted_iota(jnp.int32, (BLK, BLK), 1)
            for g in range(N_HEADS):
                dws_ref[g] = jnp.where(s <= t, dws_ref[g], 0.0)
            grp = lax.broadcasted_iota(jnp.int32, (N_HEADS, G_W), 0)
            lane = lax.broadcasted_iota(jnp.int32, (N_HEADS, G_W), 1) // HD
            pick = jnp.where(grp == lane, 1.0, 0.0).astype(F32)
            dbs_ref[...] = lax.dot_general(pick, dbacc[...], (((1,), (1,)), ((), ())),
                                           preferred_element_type=F32, precision=HIGH)

    return pl.pallas_call(
        body, name="gmlp_bwd", grid=(nb,),
        out_shape=[_sds((S, 2 * G_W), BF), _sds((N_HEADS, BLK, BLK), F32), _sds((N_HEADS, BLK), F32),
                   _sds((1, G_W), F32), _sds((1, G_W), F32)],
        in_specs=[pl.BlockSpec((BLK, 2 * G_W), lambda i: (i, 0)), pl.BlockSpec((BLK, G_W), lambda i: (i, 0)),
                  _const((1, G_W)), _const((1, G_W)), _const((N_HEADS, BLK, BLK)), _const((BLK, N_HEADS))],
        out_specs=[pl.BlockSpec((BLK, 2 * G_W), lambda i: (i, 0)), _const((N_HEADS, BLK, BLK)),
                   _const((N_HEADS, BLK)), _const((1, G_W)), _const((1, G_W))],
        scratch_shapes=[pltpu.VMEM((N_HEADS, BLK, BLK), BF), pltpu.VMEM((BLK, G_W), F32)],
        compiler_params=_cp(1, 32),
    )(zg, d_out, lg, lb, ws, bst)


def _mix_out(o, gm, gates, h, wa, wg, wo, gate, gp):
    S = h.shape[0]
    R = min(512, S)

    def body(o_ref, gm_ref, gates_ref, h_ref, wa_ref, wg_ref, wo_ref, gate_ref, gp_ref,
             ya_ref, yg_ref, ym_ref, y_ref, hn_ref):
        for r0 in range(0, R, CHUNK):
            rows = slice(r0, r0 + CHUNK)
            ya = _dot(o_ref[rows, :], wa_ref[...])
            yg = _dot(gm_ref[rows, :], wg_ref[...])
            ya_ref[rows, :] = ya.astype(BF)
            yg_ref[rows, :] = yg.astype(BF)
            ym = (gates_ref[rows, 0:D].astype(F32) * ya + gates_ref[rows, D:2 * D].astype(F32) * yg).astype(BF)
            ym_ref[rows, :] = ym
            y = _dot(ym, wo_ref[...])
            y_ref[rows, :] = y
            hn_ref[rows, :] = h_ref[rows, :] + gate_ref[...] * (y * _rms_r(y) * gp_ref[...])

    vec = _const((1, D))
    rows = lambda w_: pl.BlockSpec((R, w_), lambda i: (i, 0))
    return pl.pallas_call(
        body, name="mix_out", grid=(S // R,),
        out_shape=[_sds((S, D), BF)] * 3 + [_sds((S, D), F32)] * 2,
        in_specs=[rows(Q_W), rows(G_W), rows(2 * D), rows(D), _resident((Q_W, D)), _resident((G_W, D)),
                  _resident((D, D)), vec, vec],
        out_specs=[rows(D)] * 5,
        compiler_params=_cp(1, 48),
    )(o, gm, gates, h, wa, wg, wo, gate, gp)


def _mix_out_bwd(dh, y, ya, yg, gates, wa, wg, wo, gate, gp):
    S = dh.shape[0]
    R = min(256, S)

    def body(dh_ref, y_ref, ya_ref, yg_ref, gates_ref, wa_ref, wg_ref, wo_ref, gate_ref, gp_ref,
             dy_ref, dya_ref, dyg_ref, dz_ref, do_ref, dgm_ref, dgate_ref, dgp_ref):
        @pl.when(pl.program_id(0) == 0)
        def _():
            dgate_ref[...] = jnp.zeros_like(dgate_ref)
            dgp_ref[...] = jnp.zeros_like(dgp_ref)
        dy, dgate, dgp = _postnorm_bwd(dh_ref[...], y_ref[...], gate_ref[...], gp_ref[...], 1.0)
        dgate_ref[...] += dgate
        dgp_ref[...] += dgp
        dyb = dy.astype(BF)
        dy_ref[...] = dyb
        dym = _dot_nt(dyb, wo_ref[...])
        ga = gates_ref[:, 0:D].astype(F32)
        gg = gates_ref[:, D:2 * D].astype(F32)
        dya = (dym * ga).astype(BF)
        dyg = (dym * gg).astype(BF)
        dya_ref[...] = dya
        dyg_ref[...] = dyg
        dz_ref[:, 0:D] = (dym * ya_ref[...].astype(F32) * (ga * (1.0 - ga))).astype(BF)
        dz_ref[:, D:2 * D] = (dym * yg_ref[...].astype(F32) * (gg * (1.0 - gg))).astype(BF)
        do_ref[...] = _dot_nt(dya, wa_ref[...]).astype(BF)
        dgm_ref[...] = _dot_nt(dyg, wg_ref[...]).astype(BF)

    vec = _const((1, D))
    rows = lambda w_: pl.BlockSpec((R, w_), lambda i: (i, 0))
    return pl.pallas_call(
        body, name="mix_out_bwd", grid=(S // R,),
        out_shape=[_sds((S, D), BF)] * 3 + [_sds((S, 2 * D), BF), _sds((S, Q_W), BF), _sds((S, G_W), BF),
                                             _sds((1, D), F32), _sds((1, D), F32)],
        in_specs=[rows(D), rows(D), rows(D), rows(D), rows(2 * D), _resident((Q_W, D)), _resident((G_W, D)),
                  _resident((D, D)), vec, vec],
        out_specs=[rows(D)] * 3 + [rows(2 * D), rows(Q_W), rows(G_W), vec, vec],
        compiler_params=_cp(1, 48),
    )(dh, y, ya, yg, gates, wa, wg, wo, gate, gp)


def _mix_dn(dq, dkv, dzg, dzgate, w, h, dh, sc, gp):
    S = h.shape[0]
    R = min(512, S)

    def body(dq_ref, dkv_ref, dzg_ref, dzt_ref, w_ref, h_ref, dh_ref, sc_ref, gp_ref,
             out_ref, dsh_ref, dsc_ref, dgp_ref):
        @pl.when(pl.program_id(0) == 0)
        def _():
            dsh_ref[...] = jnp.zeros_like(dsh_ref)
            dsc_ref[...] = jnp.zeros_like(dsc_ref)
            dgp_ref[...] = jnp.zeros_like(dgp_ref)
        for r0 in range(0, R, CHUNK):
            rows = slice(r0, r0 + CHUNK)
            dn = _dot(dq_ref[rows, :], w_ref[0:Q_W, :])
            dn = dn + _dot(dkv_ref[rows, :], w_ref[Q_W:QKV_W, :])
            dn = dn + _dot(dzg_ref[rows, :], w_ref[ZG_OFF:GATE_OFF, :])
            dn = dn + _dot(dzt_ref[rows, :], w_ref[GATE_OFF:IN_W, :])
            dx, dsh, dsc, dgp = _prenorm_bwd(dn, h_ref[rows, :], gp_ref[...], sc_ref[...])
            out_ref[rows, :] = dh_ref[rows, :] + dx
            dsh_ref[...] += dsh
            dsc_ref[...] += dsc
            dgp_ref[...] += dgp

    vec = _const((1, D))
    rows = lambda w_: pl.BlockSpec((R, w_), lambda i: (i, 0))
    return pl.pallas_call(
        body, name="mix_dn", grid=(S // R,),
        out_shape=[_sds((S, D), F32)] + [_sds((1, D), F32)] * 3,
        in_specs=[rows(Q_W), rows(2 * KV_W), rows(2 * G_W), rows(2 * D), _resident((IN_W, D)),
                  rows(D), rows(D), vec, vec],
        out_specs=[rows(D), vec, vec, vec],
        compiler_params=_cp(1, 48),
    )(dq, dkv, dzg, dzgate, w, h, dh, sc, gp)


def _adamw_math(w, g, m, v):
    m2 = ADAM_B1 * m + (1.0 - ADAM_B1) * g
    v2 = ADAM_B2 * v + (1.0 - ADAM_B2) * (g * g)
    m_hat = m2 / (1.0 - ADAM_B1 ** ADAM_STEP)
    v_hat = v2 / (1.0 - ADAM_B2 ** ADAM_STEP)
    delta = -ADAM_LR * (m_hat / (jnp.sqrt(v_hat) + ADAM_EPS) + ADAM_WD * w)
    return delta, m2, v2


def _row_tile(rows, cols):
    best = None
    for t in range(16, rows + 1, 16):
        if rows % t == 0 and t * cols <= 256 * 1024:
            best = t
    return best if best is not None else rows


def _adamw_sharded(landing, w, m, v, name):
    r, c = w.shape
    tr = _row_tile(r, c)

    def body(l_ref, w_ref, m_ref, v_ref, g_ref, d_ref, m2_ref, v2_ref):
        g = l_ref[0].astype(F32)
        for j in range(1, N_DEV):
            g = g + l_ref[j].astype(F32)
        delta, m2, v2 = _adamw_math(w_ref[...], g, m_ref[...], v_ref[...])
        g_ref[...] = g
        d_ref[...] = delta
        m2_ref[...] = m2
        v2_ref[...] = v2

    row = pl.BlockSpec((tr, c), lambda i: (i, 0))
    return pl.pallas_call(
        body, name=name, grid=(r // tr,),
        out_shape=[_sds((r, c), F32)] * 4,
        in_specs=[pl.BlockSpec((N_DEV, tr, c), lambda i: (0, i, 0)), row, row, row],
        out_specs=[row] * 4,
        compiler_params=_cp(1, 48),
    )(landing, w, m, v)


def _adamw_small(w, g, m, v, name):
    def body(w_ref, g_ref, m_ref, v_ref, d_ref, m2_ref, v2_ref):
        delta, m2, v2 = _adamw_math(w_ref[...], g_ref[...], m_ref[...], v_ref[...])
        d_ref[...] = delta
        m2_ref[...] = m2
        v2_ref[...] = v2

    vm = pl.BlockSpec(memory_space=pltpu.VMEM)
    return pl.pallas_call(body, name=name, out_shape=[_sds(w.shape, F32)] * 3,
                          in_specs=[vm] * 4, out_specs=[vm] * 3)(w, g, m, v)


def _w_ada_update(c8, d_ada, w, m, v):
    tr = 256

    def body(c_ref, d_ref, w_ref, m_ref, v_ref, g_ref, dl_ref, m2_ref, v2_ref):
        cs = c_ref[...]
        cs = cs * jax.nn.sigmoid(cs)
        g = lax.dot_general(cs, d_ref[...], (((0,), (0,)), ((), ())), preferred_element_type=F32, precision=HIGH)
        delta, m2, v2 = _adamw_math(w_ref[...], g, m_ref[...], v_ref[...])
        g_ref[...] = g
        dl_ref[...] = delta
        m2_ref[...] = m2
        v2_ref[...] = v2

    row = pl.BlockSpec((tr, ADA_W), lambda i: (i, 0))
    return pl.pallas_call(
        body, name="w_ada_update", grid=(D // tr,),
        out_shape=[_sds((D, ADA_W), F32)] * 4,
        in_specs=[pl.BlockSpec((N_DEV, tr), lambda i: (0, i)), _const((N_DEV, ADA_W)), row, row, row],
        out_specs=[row] * 4,
        compiler_params=_cp(1, 40),
    )(c8, d_ada, w, m, v)


def _t5_bucket():
    qi = jnp.arange(BLK, dtype=jnp.int32)[:, None]
    kj = jnp.arange(2 * BLK, dtype=jnp.int32)[None, :]
    dist = jnp.maximum(qi + BLK - kj, 0)
    max_exact = N_BUCKETS // 2
    d_f = jnp.maximum(dist, max_exact).astype(F32)
    large = max_exact + (jnp.log(d_f / max_exact) / math.log(MAX_DISTANCE / max_exact)
                         * (N_BUCKETS - max_exact)).astype(jnp.int32)
    large = jnp.minimum(large, N_BUCKETS - 1)
    return jnp.where(dist < max_exact, dist, large)


def _slabs_of_columns(w):
    r, c8 = w.shape
    return jnp.transpose(w.reshape(r, N_DEV, c8 // N_DEV), (1, 0, 2))


def _columns_of_slabs(w8):
    _, r, c = w8.shape
    return jnp.transpose(w8, (1, 0, 2)).reshape(r, N_DEV * c)


def kernel(x, c, rel_bias, w_ada, b_ada, pre_norm_g, post_norm_g, w_ffn1_in, w_ffn1_out, w_in, sinks, gmlp_ln_g, gmlp_ln_b, gmlp_w_s, gmlp_b_s, w_br_attn, w_br_gmlp, w_out, w_ffn2_in, w_ffn2_out, loss_target, m_rel_bias, m_w_ada, m_b_ada, m_pre_norm_g, m_post_norm_g, m_w_ffn1_in, m_w_ffn1_out, m_w_in, m_sinks, m_gmlp_ln_g, m_gmlp_ln_b, m_gmlp_w_s, m_gmlp_b_s, m_w_br_attn, m_w_br_gmlp, m_w_out, m_w_ffn2_in, m_w_ffn2_out, v_rel_bias, v_w_ada, v_b_ada, v_pre_norm_g, v_post_norm_g, v_w_ffn1_in, v_w_ffn1_out, v_w_in, v_sinks, v_gmlp_ln_g, v_gmlp_ln_b, v_gmlp_w_s, v_gmlp_b_s, v_w_br_attn, v_w_br_gmlp, v_w_out, v_w_ffn2_in, v_w_ffn2_out):
    me = 4 * lax.axis_index("x") + 2 * lax.axis_index("y") + lax.axis_index("c")
    x0 = x[0]
    target = loss_target[0]

    transposed = ("w_ffn1_in", "w_in", "w_ffn2_in")
    shards = [w_ffn1_in[0].T, w_ffn1_out[0], w_in[0].T, w_br_attn[0], w_br_gmlp[0], w_out[0],
              w_ffn2_in[0].T, w_ffn2_out[0]]
    shards_bf = [s.astype(BF) for s in shards]
    groups = [shards_bf[0:1], shards_bf[1:2], shards_bf[2:6], shards_bf[6:7], shards_bf[7:8]]

    def gather_start(i, after):
        return _slabs_start("gather", groups[i], after, "gather_start_%d" % i)

    def forward_start(st, i, after):
        lands = _slabs_wait("gather", len(groups[i]), st, after, "gather_wait_%d" % i)
        return _slabs_start("forward", lands, c, "forward_start_%d" % i)

    def gathered(st, i, after):
        return _slabs_wait("forward", len(groups[i]), st, after, "forward_wait_%d" % i)

    gs0 = gather_start(0, c)

    small = jnp.concatenate([c[0], pre_norm_g[0].reshape(-1), post_norm_g[0].reshape(-1)])
    small8 = jnp.broadcast_to(small[None, :], (8, small.shape[0]))
    b_ada64 = jnp.repeat(b_ada.reshape(N_DEV, ADA_W), 8, axis=0)
    gath, ada64 = _ada_forward(small8, w_ada[0], b_ada64)
    gath8 = gath[::8]
    ada = ada64[::8].reshape(9, D)
    sh1, sc1, g1, sh2, sc2, g2, sh3, sc3, g3 = [ada[k:k + 1] for k in range(9)]
    gains = gath8[:, D:].reshape(N_DEV, 2, 3, 128)
    pre_g = jnp.transpose(gains[:, 0], (1, 0, 2)).reshape(3, D)
    post_g = jnp.transpose(gains[:, 1], (1, 0, 2)).reshape(3, D)
    pre = [pre_g[k:k + 1] for k in range(3)]
    post = [post_g[k:k + 1] for k in range(3)]

    bucket = _t5_bucket()
    bias = _bias_table(rel_bias, bucket)
    sinks8 = sinks[0]
    lg, lb = gmlp_ln_g, gmlp_ln_b
    ws = gmlp_w_s[0]
    bst = jnp.transpose(gmlp_b_s[0])

    fs0 = forward_start(gs0, 0, sh1)
    gs1 = gather_start(1, fs0[-1])
    gs2 = gather_start(2, gs1[-1])
    (wf1_in,) = gathered(fs0, 0, gs2[-1])
    n1, fg1, fu1, fa1 = _ffn_in(x0, sh1, sc1, pre[0], wf1_in, "ffn1_in")
    fs1 = forward_start(gs1, 1, n1)
    fs2 = forward_start(gs2, 2, fs1[-1])
    gs3 = gather_start(3, fs2[-1])
    gs4 = gather_start(4, gs3[-1])
    wf1_out = gathered(fs1, 1, gs4[-1])[0].reshape(4, FS, D)
    h1, y1 = _ffn_out(fa1, wf1_out, x0, g1, post[0], "ffn1_out")
    mix_w = gathered(fs2, 2, h1)
    w_in_full = mix_w[0].reshape(IN_W, D)
    w_bra = _columns_of_slabs(mix_w[1])
    w_brg = _columns_of_slabs(mix_w[2])
    w_out_full = mix_w[3].reshape(D, D)
    n2, qkv, zg, gates = _mix_in(h1, sh2, sc2, pre[1], w_in_full)
    att = _attn_fwd(qkv, bias, sinks8)
    gm = _gmlp_fwd(zg, lg, lb, ws, bst)
    fs3 = forward_start(gs3, 3, gm)
    fs4 = forward_start(gs4, 4, fs3[-1])
    ya, yg, ymix, y2, h2 = _mix_out(att, gm, gates, h1, w_bra, w_brg, w_out_full, g2 + fs4[-1], post[1])
    (wf2_in,) = gathered(fs3, 3, h2)
    wf2_out = gathered(fs4, 4, wf2_in)[0].reshape(4, FS, D)
    n3, fg3, fu3, fa3, y3, dh3, sq = _ffn_fwd_loss(h2, sh3, sc3, pre[2], wf2_in, wf2_out, g3, post[2], target,
                                                   "ffn2_fwd_loss")
    loss = lax.psum(0.5 * sq[0, 0] / D, ("x", "y", "c"))

    def exchange_start(i, arrays):
        return _slabs_start("exchange", arrays, sq, "exchange_start_%d" % i)

    dy3, dgu3, dh2, d_g3, d_post2, d_sh3, d_sc3, d_pre2 = _ffn_bwd(
        dh3, y3, fg3, fu3, wf2_out, wf2_in, h2, g3, post[2], sc3, pre[2], "ffn2_bwd")
    gw_f2_out = _tn_matmul(fa3, dy3, "ffn2_out_wgrad").reshape(N_DEV, D_FF // N_DEV, D)
    ex0 = exchange_start(0, [gw_f2_out])
    gw_f2_in = _tn_matmul(dgu3, n3, "ffn2_in_wgrad").reshape(N_DEV, FS, D)
    ex1 = exchange_start(1, [gw_f2_in])

    dy2, dya, dyg, dzgate, d_att, d_gm, d_g2, d_post1 = _mix_out_bwd(
        dh2, y2, ya, yg, gates, w_bra, w_brg, w_out_full, g2 + ex0[-1] + ex1[-1], post[1])
    gw_out = _tn_matmul(ymix, dy2, "w_out_wgrad").reshape(N_DEV, D // N_DEV, D)
    gw_bra = _slabs_of_columns(_tn_matmul(att, dya, "w_br_attn_wgrad").reshape(Q_W, D))
    gw_brg = _slabs_of_columns(_tn_matmul(gm, dyg, "w_br_gmlp_wgrad").reshape(G_W, D))
    ex2 = exchange_start(2, [gw_bra, gw_brg, gw_out])
    dq, dkv, dbias, dsink = _attn_bwd(qkv, bias, sinks8, d_att)
    dzg, d_ws, d_bs, d_lg, d_lb = _gmlp_bwd(zg, d_gm, lg, lb, ws, bst)
    d_rel = _rel_bias_grad(dbias, bucket)
    early = jnp.concatenate([
        jnp.concatenate([d_lg.reshape(4, 128), d_lb.reshape(4, 128)], axis=0),
        d_bs, d_rel, dsink, d_ws.reshape(N_HEADS * BLK, BLK)], axis=0)
    sm0 = _slabs_start("gather_all", [early], sq, "small_gather_start")
    dh1, d_sh2, d_sc2, d_pre1 = _mix_dn(dq, dkv, dzg, dzgate, w_in_full, h1, dh2, sc2 + ex2[-1] + sm0[-1], pre[1])
    gw_in = jnp.concatenate(
        [_tn_matmul(dq, n2, "w_in_q_wgrad").reshape(Q_W, D),
         _tn_matmul(dkv, n2, "w_in_kv_wgrad").reshape(2 * KV_W, D),
         _tn_matmul(dzg, n2, "w_in_zg_wgrad").reshape(2 * G_W, D),
         _tn_matmul(dzgate, n2, "w_in_gate_wgrad").reshape(2 * D, D)], axis=0).reshape(N_DEV, IN_W // N_DEV, D)
    ex3 = exchange_start(3, [gw_in])

    dy1, dgu1, d_g1, d_post0 = _ffn_out_bwd(dh1, y1, fg1, fu1, wf1_out, g1 + ex3[-1], post[0], "ffn1_out_bwd")
    gw_f1_out = _tn_matmul(fa1, dy1, "ffn1_out_wgrad").reshape(N_DEV, D_FF // N_DEV, D)
    gw_f1_in = _tn_matmul(dgu1, n1, "ffn1_in_wgrad").reshape(N_DEV, FS, D)
    ex4 = exchange_start(4, [gw_f1_out, gw_f1_in])
    grad_x, d_sh1, d_sc1, d_pre0 = _ffn_dn(dgu1, wf1_in, x0, dh1, sc1 + ex4[-1], pre[0], "ffn1_dn")

    landed = {}
    for i, (ex, nms) in enumerate([(ex0, ["w_ffn2_out"]), (ex1, ["w_ffn2_in"]),
                                   (ex2, ["w_br_attn", "w_br_gmlp", "w_out"]), (ex3, ["w_in"]),
                                   (ex4, ["w_ffn1_out", "w_ffn1_in"])]):
        for nm, land in zip(nms, _slabs_wait("exchange", len(nms), ex, grad_x, "exchange_wait_%d" % i)):
            landed[nm] = land
    moments = [(m_w_ffn1_in, v_w_ffn1_in), (m_w_ffn1_out, v_w_ffn1_out), (m_w_in, v_w_in),
               (m_w_br_attn, v_w_br_attn), (m_w_br_gmlp, v_w_br_gmlp), (m_w_out, v_w_out),
               (m_w_ffn2_in, v_w_ffn2_in), (m_w_ffn2_out, v_w_ffn2_out)]
    names = ["w_ffn1_in", "w_ffn1_out", "w_in", "w_br_attn", "w_br_gmlp", "w_out", "w_ffn2_in", "w_ffn2_out"]
    big = {}
    for nm, w_, (m_, v_) in zip(names, shards, moments):
        if nm in transposed:
            res4 = _adamw_sharded(landed[nm], w_, m_[0].T, v_[0].T, "adamw_" + nm)
            big[nm] = [a.T[None] for a in res4]
        else:
            big[nm] = [a[None] for a in _adamw_sharded(landed[nm], w_, m_[0], v_[0], "adamw_" + nm)]

    d_ada = jnp.concatenate([v_.reshape(8, 128) for v_ in
                             (d_sh1, d_sc1, d_g1, d_sh2, d_sc2, d_g2, d_sh3, d_sc3, d_g3)], axis=0)
    d_pre = jnp.concatenate([d_pre0, d_pre1, d_pre2], axis=0)
    d_post = jnp.concatenate([d_post0, d_post1, d_post2], axis=0)
    late = jnp.concatenate([d_ada, _slabs_of_columns(d_pre).reshape(24, 128),
                            _slabs_of_columns(d_post).reshape(24, 128)], axis=0)
    tot, every = _small_allreduce(late)
    (early_land,) = _slabs_wait("gather_all", 1, sm0, grad_x, "small_gather_wait")
    tot_early = _sum_slabs(early_land)

    g_b_ada = tot[0:72].reshape(1, 9 * D)
    g_pre = lax.dynamic_slice_in_dim(tot[72:96], 3 * me, 3, axis=0)[None]
    g_post = lax.dynamic_slice_in_dim(tot[96:120], 3 * me, 3, axis=0)[None]
    g_lg = tot_early[0:4].reshape(1, G_W)
    g_lb = tot_early[4:8].reshape(1, G_W)
    g_bs = tot_early[8:16][None]
    g_rel = jnp.transpose(tot_early[16:24, 0:N_BUCKETS])
    g_sinks = tot_early[24:32, 0][None]
    g_ws = tot_early[32:1056].reshape(1, N_HEADS, BLK, BLK)

    d_ada_mine = lax.dynamic_slice_in_dim(every[:, 0:72].reshape(N_DEV, N_DEV, ADA_W), me, 1, axis=1)[:, 0]
    ada_out = [a[None] for a in _w_ada_update(gath8[:, 0:D], d_ada_mine, w_ada[0], m_w_ada[0], v_w_ada[0])]

    def small_step(w_, g_, m_, v_, nm):
        shp = w_.shape
        two_d = (int(math.prod(shp[:-1])), shp[-1])
        d_, m2_, v2_ = _adamw_small(w_.reshape(two_d), g_.reshape(two_d), m_.reshape(two_d), v_.reshape(two_d),
                                    "adamw_" + nm)
        return [g_, d_.reshape(shp), m2_.reshape(shp), v2_.reshape(shp)]

    res = {
        "rel_bias": small_step(rel_bias, g_rel, m_rel_bias, v_rel_bias, "rel_bias"),
        "w_ada": ada_out,
        "b_ada": small_step(b_ada, g_b_ada, m_b_ada, v_b_ada, "b_ada"),
        "pre_norm_g": small_step(pre_norm_g, g_pre, m_pre_norm_g, v_pre_norm_g, "pre_norm_g"),
        "post_norm_g": small_step(post_norm_g, g_post, m_post_norm_g, v_post_norm_g, "post_norm_g"),
        "sinks": small_step(sinks, g_sinks, m_sinks, v_sinks, "sinks"),
        "gmlp_ln_g": small_step(gmlp_ln_g, g_lg, m_gmlp_ln_g, v_gmlp_ln_g, "gmlp_ln_g"),
        "gmlp_ln_b": small_step(gmlp_ln_b, g_lb, m_gmlp_ln_b, v_gmlp_ln_b, "gmlp_ln_b"),
        "gmlp_w_s": small_step(gmlp_w_s, g_ws, m_gmlp_w_s, v_gmlp_w_s, "gmlp_w_s"),
        "gmlp_b_s": small_step(gmlp_b_s, g_bs, m_gmlp_b_s, v_gmlp_b_s, "gmlp_b_s"),
    }
    res.update(big)
    order = ["rel_bias", "w_ada", "b_ada", "pre_norm_g", "post_norm_g", "w_ffn1_in", "w_ffn1_out", "w_in", "sinks",
             "gmlp_ln_g", "gmlp_ln_b", "gmlp_w_s", "gmlp_b_s", "w_br_attn", "w_br_gmlp", "w_out", "w_ffn2_in",
             "w_ffn2_out"]
    outs = [loss, grad_x[None]]
    for k in range(4):
        outs += [res[nm][k] for nm in order]
    return tuple(outs)
```

```python
import functools
import math

import jax
import jax.numpy as jnp
from jax import lax
from jax.experimental import pallas as pl
from jax.experimental.pallas import tpu as pltpu

F32 = jnp.float32
BF = jnp.bfloat16

N_DEV = 8
D = 1024
D_FF = 2816
FS = D_FF // 4
N_HEADS = 8
N_KV = 2
GROUP = 4
HD = 64
BLK = 128
Q_W = 512
KV_W = 128
G_W = 512
QKV_W = Q_W + 2 * KV_W
ZG_OFF = QKV_W
GATE_OFF = ZG_OFF + 2 * G_W
IN_W = GATE_OFF + 2 * D
N_BUCKETS = 32
MAX_DISTANCE = 128
EPS = 1e-6
NEG = -1e30
SCALE = HD ** -0.5
ADA_W = 9 * D // N_DEV

ADAM_LR = 0.001
ADAM_B1 = 0.9
ADAM_B2 = 0.999
ADAM_EPS = 1e-08
ADAM_WD = 0.01
ADAM_STEP = 10

CHUNK = 256
MIB = 1024 * 1024
MESH = pl.DeviceIdType.MESH
HIGH = lax.Precision.HIGHEST


def _cp(n_grid, vmem_mib):
    return pltpu.CompilerParams(dimension_semantics=("arbitrary",) * n_grid,
                                vmem_limit_bytes=vmem_mib * MIB)


def _const(shape):
    return pl.BlockSpec(shape, lambda *_: (0,) * len(shape))


def _resident(shape):
    return pl.BlockSpec(shape, lambda *_: (0,) * len(shape), pipeline_mode=pl.Buffered(1))


def _sds(shape, dtype):
    return jax.ShapeDtypeStruct(shape, dtype)


def _dot(a, b):
    return jnp.dot(a, b, preferred_element_type=F32)


def _dot_nt(a, b):
    return lax.dot_general(a, b, (((1,), (1,)), ((), ())), preferred_element_type=F32)


def _dot_tn(a, b):
    return lax.dot_general(a, b, (((0,), (0,)), ((), ())), preferred_element_type=F32)


def _rms_r(x):
    return lax.rsqrt(jnp.mean(x * x, axis=-1, keepdims=True) + EPS)


def _colsum(x):
    return jnp.sum(x, axis=0, keepdims=True)


def _prenorm(x, gp, sc, sh):
    return (x * _rms_r(x) * gp) * (1.0 + sc) + sh


def _prenorm_bwd(dn, x, gp, sc):
    r = _rms_r(x)
    xh = x * r
    t = dn * (1.0 + sc) * gp
    dx = r * (t - xh * jnp.mean(t * xh, axis=-1, keepdims=True))
    return dx, _colsum(dn), _colsum(dn * xh * gp), _colsum(dn * (1.0 + sc) * xh)


def _postnorm_bwd(dh, y, gate, gp, res):
    r = _rms_r(y)
    yh = y * r
    dyn = (res * gate) * dh
    t = dyn * gp
    dy = r * (t - yh * jnp.mean(t * yh, axis=-1, keepdims=True))
    return dy, _colsum(res * dh * yh * gp), _colsum(dyn * yh)


def _gelu(x):
    k = math.sqrt(2.0 / math.pi)
    return 0.5 * x * (1.0 + jnp.tanh(k * (x + 0.044715 * x * x * x)))


def _gelu_grad(x):
    k = math.sqrt(2.0 / math.pi)
    t = jnp.tanh(k * (x + 0.044715 * x * x * x))
    return 0.5 * (1.0 + t) + 0.5 * x * (1.0 - t * t) * (k * (1.0 + 3.0 * 0.044715 * x * x))


def _my_place():
    x, y, c = lax.axis_index("x"), lax.axis_index("y"), lax.axis_index("c")
    return x, y, c, 4 * x + 2 * y + c


def _peer(x, y, c, k):
    px = 1 - x if k & 4 else x
    py = 1 - y if k & 2 else y
    pc = 1 - c if k & 1 else c
    return (px, py, pc), 4 * px + 2 * py + pc


HBM_SPEC = pl.BlockSpec(memory_space=pltpu.HBM)
SEM_SPEC = pl.BlockSpec(memory_space=pltpu.SEMAPHORE)
EFFECT = pltpu.SideEffectType.DATAFLOW_SIDE_EFFECTING


RELATIONS = {"exchange": (1, 2, 3, 4, 5, 6, 7), "gather": (1, 2, 4, 6), "forward": (2, 4, 6),
             "gather_all": (1, 2, 3, 4, 5, 6, 7)}


def _slab_copies(mode, srcs, lands, send, recv, loc):
    x, y, c, me = _my_place()
    rel = RELATIONS[mode]
    remote, local = [], []
    for t in range(len(lands)):
        for i, k in enumerate(rel):
            peer, peer_lin = _peer(x, y, c, k)
            if mode == "exchange":
                src, dst, to = srcs[t].at[peer_lin], lands[t].at[me], peer
            elif mode in ("gather", "gather_all"):
                src, dst, to = srcs[t], lands[t].at[me], peer
            else:
                src, dst, to = lands[t].at[peer_lin], lands[t].at[peer_lin], _peer(x, y, c, 1)[0]
            remote.append(pltpu.make_async_remote_copy(
                src_ref=src, dst_ref=dst, send_sem=send.at[t * len(rel) + i], recv_sem=recv.at[t * len(rel) + i],
                device_id=to, device_id_type=MESH))
        if mode == "exchange":
            local.append(pltpu.make_async_copy(srcs[t].at[me], lands[t].at[me], loc.at[t]))
        elif mode in ("gather", "gather_all"):
            local.append(pltpu.make_async_copy(srcs[t], lands[t].at[me], loc.at[t]))
    return remote, local


def _slabs_start(mode, arrays, after, name):
    n = len(arrays)
    if mode == "forward":
        thru = list(arrays)
    else:
        shapes = [a.shape if mode == "exchange" else (N_DEV,) + a.shape for a in arrays]
        thru = list(arrays) + [lax.empty(s, a.dtype) for s, a in zip(shapes, arrays)]
    m = len(thru)
    n_sem = n * len(RELATIONS[mode])

    def body(*refs):
        srcs, lands = refs[:n], refs[m - n:m]
        send, recv, loc = refs[m + 1:m + 4]
        remote, local = _slab_copies(mode, srcs, lands, send, recv, loc)
        for cp in remote + local:
            cp.start()
        refs[-1][...] = jnp.zeros_like(refs[-1])

    return pl.pallas_call(
        body, name=name,
        out_shape=(pltpu.SemaphoreType.DMA((n_sem,)), pltpu.SemaphoreType.DMA((n_sem,)),
                   pltpu.SemaphoreType.DMA((n,)),
                   *[pltpu.HBM(a.shape, a.dtype) for a in thru],
                   _sds((1, D), F32)),
        in_specs=[HBM_SPEC] * m + [pl.BlockSpec(memory_space=pl.ANY)],
        out_specs=(SEM_SPEC, SEM_SPEC, SEM_SPEC, *[HBM_SPEC] * m, pl.BlockSpec(memory_space=pltpu.VMEM)),
        input_output_aliases={t: 3 + t for t in range(m)},
        compiler_params=pltpu.CompilerParams(has_side_effects=EFFECT),
    )(*[pltpu.with_memory_space_constraint(a, pltpu.HBM) for a in thru], after)


def _slabs_wait(mode, n, started, after, name):
    sems = started[0:3]
    thru = started[3:-1]
    m = len(thru)

    def body(*refs):
        srcs, lands = refs[:n], refs[m - n:m]
        remote, local = _slab_copies(mode, srcs, lands, *refs[m:m + 3])
        for cp in remote:
            cp.wait_send()
            cp.wait_recv()
        for cp in local:
            cp.wait()

    res = pl.pallas_call(
        body, name=name,
        out_shape=tuple(pltpu.HBM(a.shape, a.dtype) for a in thru),
        in_specs=[HBM_SPEC] * m + [SEM_SPEC] * 3 + [pl.BlockSpec(memory_space=pl.ANY)],
        out_specs=tuple([HBM_SPEC] * m),
        input_output_aliases={t: t for t in range(m)},
        compiler_params=pltpu.CompilerParams(has_side_effects=EFFECT),
    )(*thru, *sems, after)
    return list(res[m - n:m])


def _ada_forward(small8, w_ada, b_ada64):
    sw = small8.shape[1]

    def body(sm_ref, w_ref, b_ref, gath_ref, ada_ref, part_ref, send1, recv1, send2, recv2):
        x, y, c, me = _my_place()
        row_me = pl.multiple_of(me * 8, 8)
        gath_ref[pl.ds(row_me, 8), :] = sm_ref[...]
        first = []
        for k in range(1, N_DEV):
            peer, _ = _peer(x, y, c, k)
            cp = pltpu.make_async_remote_copy(
                src_ref=sm_ref, dst_ref=gath_ref.at[pl.ds(row_me, 8), :], send_sem=send1.at[k - 1],
                recv_sem=recv1.at[k - 1], device_id=peer, device_id_type=MESH)
            cp.start()
            first.append(cp)
        for cp in first:
            cp.wait()
        cs = gath_ref[:, 0:D]
        cs = cs * jax.nn.sigmoid(cs)
        part_ref[...] = jnp.dot(cs, w_ref[...], preferred_element_type=F32, precision=HIGH)
        ada_ref[pl.ds(row_me, 8), :] = part_ref[pl.ds(row_me, 8), :]
        second = []
        for k in range(1, N_DEV):
            peer, peer_lin = _peer(x, y, c, k)
            cp = pltpu.make_async_remote_copy(
                src_ref=part_ref.at[pl.ds(pl.multiple_of(peer_lin * 8, 8), 8), :],
                dst_ref=ada_ref.at[pl.ds(row_me, 8), :], send_sem=send2.at[k - 1],
                recv_sem=recv2.at[k - 1], device_id=peer, device_id_type=MESH)
            cp.start()
            second.append(cp)
        for cp in second:
            cp.wait()
        ada_ref[...] = ada_ref[...] + b_ref[...]

    vm = pl.BlockSpec(memory_space=pltpu.VMEM)
    return pl.pallas_call(
        body, name="ada_forward",
        out_shape=[_sds((8 * N_DEV, sw), F32), _sds((8 * N_DEV, ADA_W), F32)],
        in_specs=[vm, vm, vm], out_specs=[vm, vm],
        scratch_shapes=[pltpu.VMEM((8 * N_DEV, ADA_W), F32)] + [pltpu.SemaphoreType.DMA((7,))] * 4,
        compiler_params=pltpu.CompilerParams(vmem_limit_bytes=32 * MIB),
    )(small8, w_ada, b_ada64)


def _sum_slabs(land):
    def body(l_ref, o_ref):
        acc = l_ref[0]
        for j in range(1, N_DEV):
            acc = acc + l_ref[j]
        o_ref[...] = acc

    vm = pl.BlockSpec(memory_space=pltpu.VMEM)
    return pl.pallas_call(body, name="sum_slabs", out_shape=_sds(land.shape[1:], F32), in_specs=[vm], out_specs=vm,
                          compiler_params=pltpu.CompilerParams(vmem_limit_bytes=32 * MIB))(land)


def _small_allreduce(pack):
    rows = pack.shape[0]

    def body(p_ref, sum_ref, gath_ref, send, recv):
        x, y, c, me = _my_place()
        gath_ref[me] = p_ref[...]
        cps = []
        for k in range(1, N_DEV):
            peer, _ = _peer(x, y, c, k)
            cp = pltpu.make_async_remote_copy(
                src_ref=p_ref, dst_ref=gath_ref.at[me], send_sem=send.at[k - 1],
                recv_sem=recv.at[k - 1], device_id=peer, device_id_type=MESH)
            cp.start()
            cps.append(cp)
        for cp in cps:
            cp.wait()
        acc = gath_ref[0]
        for j in range(1, N_DEV):
            acc = acc + gath_ref[j]
        sum_ref[...] = acc

    vm = pl.BlockSpec(memory_space=pltpu.VMEM)
    return pl.pallas_call(
        body, name="small_allreduce",
        out_shape=[_sds((rows, 128), F32), _sds((N_DEV, rows, 128), F32)],
        in_specs=[vm], out_specs=[vm, vm],
        scratch_shapes=[pltpu.SemaphoreType.DMA((7,)), pltpu.SemaphoreType.DMA((7,))],
        compiler_params=pltpu.CompilerParams(vmem_limit_bytes=40 * MIB),
    )(pack)


def _ffn_in(h, sh, sc, gp, wt8, name):
    S = h.shape[0]
    R = min(512, S)

    def body(h_ref, sh_ref, sc_ref, gp_ref, w_ref, n_ref, dg_ref, sl_ref, a_ref):
        for r0 in range(0, R, CHUNK):
            rows = slice(r0, r0 + CHUNK)
            n = _prenorm(h_ref[rows, :], gp_ref[...], sc_ref[...], sh_ref[...]).astype(BF)
            n_ref[rows, :] = n
            for s in range(4):
                g = _dot_nt(n, w_ref[s])
                u = _dot_nt(n, w_ref[s + 4])
                sg = jax.nn.sigmoid(g)
                silu = g * sg
                dg_ref[s, rows, :] = (u * (sg * (1.0 + g * (1.0 - sg)))).astype(BF)
                sl_ref[s, rows, :] = silu.astype(BF)
                a_ref[s, rows, :] = (silu * u).astype(BF)

    vec = _const((1, D))
    row = pl.BlockSpec((R, D), lambda i: (i, 0))
    blk = pl.BlockSpec((4, R, FS), lambda i: (0, i, 0))
    return pl.pallas_call(
        body, name=name, grid=(S // R,),
        out_shape=[_sds((S, D), BF)] + [_sds((4, S, FS), BF)] * 3,
        in_specs=[row, vec, vec, vec, _resident((N_DEV, FS, D))],
        out_specs=[row, blk, blk, blk],
        compiler_params=_cp(1, 56),
    )(h, sh, sc, gp, wt8)


def _ffn_out(a, w4, h, gate, gp, name):
    S = h.shape[0]
    R = min(512, S)

    def body(a_ref, w_ref, h_ref, gate_ref, gp_ref, hn_ref, y_ref):
        for r0 in range(0, R, CHUNK):
            rows = slice(r0, r0 + CHUNK)
            y = _dot(a_ref[0, rows, :], w_ref[0])
            for s in range(1, 4):
                y = y + _dot(a_ref[s, rows, :], w_ref[s])
            y_ref[rows, :] = y
            hn_ref[rows, :] = h_ref[rows, :] + (0.5 * gate_ref[...]) * (y * _rms_r(y) * gp_ref[...])

    vec = _const((1, D))
    row = pl.BlockSpec((R, D), lambda i: (i, 0))
    return pl.pallas_call(
        body, name=name, grid=(S // R,),
        out_shape=[_sds((S, D), F32), _sds((S, D), F32)],
        in_specs=[pl.BlockSpec((4, R, FS), lambda i: (0, i, 0)), _resident((4, FS, D)), row, vec, vec],
        out_specs=[row, row],
        compiler_params=_cp(1, 48),
    )(a, w4, h, gate, gp)


def _ffn_fwd_loss(h, sh, sc, gpre, wt8, w4, gate, gpost, target, name):
    S = h.shape[0]
    R = min(256, S)

    def body(h_ref, sh_ref, sc_ref, gpre_ref, w_ref, w4_ref, gate_ref, gpost_ref, t_ref,
             n_ref, dg_ref, sl_ref, a_ref, y_ref, dh_ref, tot_ref):
        @pl.when(pl.program_id(0) == 0)
        def _():
            tot_ref[...] = jnp.zeros_like(tot_ref)
        hh = h_ref[...]
        n = _prenorm(hh, gpre_ref[...], sc_ref[...], sh_ref[...]).astype(BF)
        n_ref[...] = n
        y = None
        for s in range(4):
            g = _dot_nt(n, w_ref[s])
            u = _dot_nt(n, w_ref[s + 4])
            sg = jax.nn.sigmoid(g)
            silu = g * sg
            dg_ref[s] = (u * (sg * (1.0 + g * (1.0 - sg)))).astype(BF)
            sl_ref[s] = silu.astype(BF)
            a = (silu * u).astype(BF)
            a_ref[s] = a
            part = _dot(a, w4_ref[s])
            y = part if y is None else y + part
        y_ref[...] = y
        e = hh + (0.5 * gate_ref[...]) * (y * _rms_r(y) * gpost_ref[...]) - t_ref[...]
        dh_ref[...] = e * (1.0 / D)
        tot_ref[...] += jnp.sum(jnp.sum(e * e, axis=1, keepdims=True), axis=0, keepdims=True)

    vec = _const((1, D))
    row = pl.BlockSpec((R, D), lambda i: (i, 0))
    blk = pl.BlockSpec((4, R, FS), lambda i: (0, i, 0))
    return pl.pallas_call(
        body, name=name, grid=(S // R,),
        out_shape=[_sds((S, D), BF)] + [_sds((4, S, FS), BF)] * 3 + [_sds((S, D), F32)] * 2 + [_sds((1, 1), F32)],
        in_specs=[row, vec, vec, vec, _resident((N_DEV, FS, D)), _resident((4, FS, D)), vec, vec, row],
        out_specs=[row, blk, blk, blk, row, row, _const((1, 1))],
        compiler_params=_cp(1, 56),
    )(h, sh, sc, gpre, wt8, w4, gate, gpost, target)


def _ffn_out_bwd(dh, y, dsilu_u, silu, w4, gate, gp, name):
    S = dh.shape[0]
    R = min(512, S)

    def body(dh_ref, y_ref, g_ref, u_ref, w_ref, gate_ref, gp_ref, dy_ref, dgu_ref, dgate_ref, dgp_ref):
        @pl.when(pl.program_id(0) == 0)
        def _():
            dgate_ref[...] = jnp.zeros_like(dgate_ref)
            dgp_ref[...] = jnp.zeros_like(dgp_ref)
        for r0 in range(0, R, CHUNK):
            rows = slice(r0, r0 + CHUNK)
            dy, dgate, dgp = _postnorm_bwd(dh_ref[rows, :], y_ref[rows, :], gate_ref[...], gp_ref[...], 0.5)
            dgate_ref[...] += dgate
            dgp_ref[...] += dgp
            dyb = dy.astype(BF)
            dy_ref[rows, :] = dyb
            for s in range(4):
                da = _dot_nt(dyb, w_ref[s])
                dgu_ref[s, rows, :] = (da * g_ref[s, rows, :].astype(F32)).astype(BF)
                dgu_ref[s + 4, rows, :] = (da * u_ref[s, rows, :].astype(F32)).astype(BF)

    vec = _const((1, D))
    row = pl.BlockSpec((R, D), lambda i: (i, 0))
    blk4 = pl.BlockSpec((4, R, FS), lambda i: (0, i, 0))
    return pl.pallas_call(
        body, name=name, grid=(S // R,),
        out_shape=[_sds((S, D), BF), _sds((8, S, FS), BF), _sds((1, D), F32), _sds((1, D), F32)],
        in_specs=[row, row, blk4, blk4, _resident((4, FS, D)), vec, vec],
        out_specs=[row, pl.BlockSpec((8, R, FS), lambda i: (0, i, 0)), vec, vec],
        compiler_params=_cp(1, 56),
    )(dh, y, dsilu_u, silu, w4, gate, gp)


def _ffn_dn(dgu, wt8, h, dh, sc, gp, name):
    S = h.shape[0]
    R = min(512, S)

    def body(dgu_ref, w_ref, h_ref, dh_ref, sc_ref, gp_ref, out_ref, dsh_ref, dsc_ref, dgp_ref):
        @pl.when(pl.program_id(0) == 0)
        def _():
            dsh_ref[...] = jnp.zeros_like(dsh_ref)
            dsc_ref[...] = jnp.zeros_like(dsc_ref)
            dgp_ref[...] = jnp.zeros_like(dgp_ref)

        for r0 in range(0, R, CHUNK):
            rows = slice(r0, r0 + CHUNK)
            dn = _dot(dgu_ref[0, rows, :], w_ref[0])
            for j in range(1, N_DEV):
                dn = dn + _dot(dgu_ref[j, rows, :], w_ref[j])
            dx, dsh, dsc, dgp = _prenorm_bwd(dn, h_ref[rows, :], gp_ref[...], sc_ref[...])
            out_ref[rows, :] = dh_ref[rows, :] + dx
            dsh_ref[...] += dsh
            dsc_ref[...] += dsc
            dgp_ref[...] += dgp

    vec = _const((1, D))
    row = pl.BlockSpec((R, D), lambda i: (i, 0))
    return pl.pallas_call(
        body, name=name, grid=(S // R,),
        out_shape=[_sds((S, D), F32)] + [_sds((1, D), F32)] * 3,
        in_specs=[pl.BlockSpec((N_DEV, R, FS), lambda i: (0, i, 0)), _resident((N_DEV, FS, D)),
                  row, row, vec, vec],
        out_specs=[row, vec, vec, vec],
        compiler_params=_cp(1, 56),
    )(dgu, wt8, h, dh, sc, gp)


def _ffn_bwd(dh, y, dsilu_u, silu, w4, wt8, h, gate, gpost, sc, gpre, name):
    S = dh.shape[0]
    R = min(256, S)

    def body(dh_ref, y_ref, g_ref, u_ref, w4_ref, w_ref, h_ref, gate_ref, gpost_ref, sc_ref, gpre_ref,
             dy_ref, dgu_ref, out_ref, dgate_ref, dgpost_ref, dsh_ref, dsc_ref, dgpre_ref):
        @pl.when(pl.program_id(0) == 0)
        def _():
            for r in (dgate_ref, dgpost_ref, dsh_ref, dsc_ref, dgpre_ref):
                r[...] = jnp.zeros_like(r)
        dhh = dh_ref[...]
        dy, dgate, dgpost = _postnorm_bwd(dhh, y_ref[...], gate_ref[...], gpost_ref[...], 0.5)
        dgate_ref[...] += dgate
        dgpost_ref[...] += dgpost
        dyb = dy.astype(BF)
        dy_ref[...] = dyb
        dn = None
        for s in range(4):
            da = _dot_nt(dyb, w4_ref[s])
            dg = (da * g_ref[s].astype(F32)).astype(BF)
            du = (da * u_ref[s].astype(F32)).astype(BF)
            dgu_ref[s] = dg
            dgu_ref[s + 4] = du
            part = _dot(dg, w_ref[s]) + _dot(du, w_ref[s + 4])
            dn = part if dn is None else dn + part
        dx, dsh, dsc, dgpre = _prenorm_bwd(dn, h_ref[...], gpre_ref[...], sc_ref[...])
        out_ref[...] = dhh + dx
        dsh_ref[...] += dsh
        dsc_ref[...] += dsc
        dgpre_ref[...] += dgpre

    vec = _const((1, D))
    row = pl.BlockSpec((R, D), lambda i: (i, 0))
    blk4 = pl.BlockSpec((4, R, FS), lambda i: (0, i, 0))
    return pl.pallas_call(
        body, name=name, grid=(S // R,),
        out_shape=[_sds((S, D), BF), _sds((8, S, FS), BF), _sds((S, D), F32)] + [_sds((1, D), F32)] * 5,
        in_specs=[row, row, blk4, blk4, _resident((4, FS, D)), _resident((N_DEV, FS, D)), row, vec, vec, vec, vec],
        out_specs=[row, pl.BlockSpec((8, R, FS), lambda i: (0, i, 0)), row] + [vec] * 5,
        compiler_params=_cp(1, 56),
    )(dh, y, dsilu_u, silu, w4, wt8, h, gate, gpost, sc, gpre)


def _tn_matmul(a, b, name):
    a3 = a if a.ndim == 3 else a[None]
    b3 = b if b.ndim == 3 else b[None]
    GA, S, M = a3.shape
    GB, _, N = b3.shape
    ts = min(2048, S)
    nk = S // ts
    chunks = [(m0, min(CHUNK, M - m0)) for m0 in range(0, M, CHUNK)]

    def body(a_ref, b_ref, o_ref, acc):
        k = pl.program_id(2)

        @pl.when(k == 0)
        def _():
            acc[...] = jnp.zeros_like(acc)

        for m0, mc in chunks:
            acc[m0:m0 + mc, :] += _dot_tn(a_ref[:, m0:m0 + mc], b_ref[...])

        @pl.when(k == nk - 1)
        def _():
            for m0, mc in chunks:
                o_ref[m0:m0 + mc, :] = acc[m0:m0 + mc, :].astype(BF)

    return pl.pallas_call(
        body, name=name, grid=(GA, GB, nk),
        out_shape=_sds((GA, GB, M, N), BF),
        in_specs=[pl.BlockSpec((None, ts, M), lambda ga, gb, k: (ga, k, 0)),
                  pl.BlockSpec((None, ts, N), lambda ga, gb, k: (gb, k, 0))],
        out_specs=pl.BlockSpec((None, None, M, N), lambda ga, gb, k: (ga, gb, 0, 0)),
        scratch_shapes=[pltpu.VMEM((M, N), F32)],
        compiler_params=_cp(3, 56),
    )(a3, b3)


def _mix_in(h, sh, sc, gp, w):
    S = h.shape[0]
    R = min(512, S)

    def body(h_ref, sh_ref, sc_ref, gp_ref, w_ref, n_ref, qkv_ref, zg_ref, gates_ref):
        for r0 in range(0, R, CHUNK):
            rows = slice(r0, r0 + CHUNK)
            nb = _prenorm(h_ref[rows, :], gp_ref[...], sc_ref[...], sh_ref[...]).astype(BF)
            n_ref[rows, :] = nb
            qkv_ref[rows, :] = _dot_nt(nb, w_ref[0:ZG_OFF, :]).astype(BF)
            zg_ref[rows, :] = _dot_nt(nb, w_ref[ZG_OFF:GATE_OFF, :]).astype(BF)
            gates_ref[rows, :] = jax.nn.sigmoid(_dot_nt(nb, w_ref[GATE_OFF:IN_W, :])).astype(BF)

    vec = _const((1, D))
    rows = lambda w_: pl.BlockSpec((R, w_), lambda i: (i, 0))
    return pl.pallas_call(
        body, name="mix_in", grid=(S // R,),
        out_shape=[_sds((S, D), BF), _sds((S, QKV_W), BF), _sds((S, 2 * G_W), BF), _sds((S, 2 * D), BF)],
        in_specs=[rows(D), vec, vec, vec, _resident((IN_W, D))],
        out_specs=[rows(D), rows(QKV_W), rows(2 * G_W), rows(2 * D)],
        compiler_params=_cp(1, 48),
    )(h, sh, sc, gp, w)


def _bias_table(rel_bias, bucket):
    def body(rel_ref, bk_ref, out_ref):
        bk = bk_ref[...]
        qi = lax.broadcasted_iota(jnp.int32, (BLK, 2 * BLK), 0)
        kj = lax.broadcasted_iota(jnp.int32, (BLK, 2 * BLK), 1)
        dist = qi + BLK - kj
        window = (dist >= 0) & (dist < BLK)
        for h in range(N_HEADS):
            acc = jnp.zeros((BLK, 2 * BLK), F32)
            for b in range(N_BUCKETS):
                acc = jnp.where(bk == b, rel_ref[b, h], acc)
            out_ref[h // GROUP, pl.ds((h % GROUP) * BLK, BLK), :] = jnp.where(window, acc, NEG)

    return pl.pallas_call(
        body, name="bias_table",
        out_shape=_sds((N_KV, GROUP * BLK, 2 * BLK), F32),
        in_specs=[pl.BlockSpec(memory_space=pltpu.SMEM), pl.BlockSpec(memory_space=pltpu.VMEM)],
        out_specs=pl.BlockSpec(memory_space=pltpu.VMEM),
    )(rel_bias, bucket)


ATT_TB = 4


def _attn_scores(q, kvc, kvp, bias_ref, sink_ref, has_prev, kh):
    k2 = jnp.concatenate([kvp[:, kh * HD:(kh + 1) * HD], kvc[:, kh * HD:(kh + 1) * HD]], axis=0)
    v2 = jnp.concatenate([kvp[:, KV_W + kh * HD:KV_W + (kh + 1) * HD],
                          kvc[:, KV_W + kh * HD:KV_W + (kh + 1) * HD]], axis=0)
    q4 = jnp.concatenate([q[:, (kh * GROUP + g) * HD:(kh * GROUP + g + 1) * HD] for g in range(GROUP)], axis=0)
    s = _dot_nt(q4, k2) * SCALE + bias_ref[kh]
    if has_prev is not None:
        col = lax.broadcasted_iota(jnp.int32, (GROUP * BLK, 2 * BLK), 1)
        s = jnp.where((col >= BLK) | has_prev, s, NEG)
    rowg = lax.broadcasted_iota(jnp.int32, (GROUP * BLK, 1), 0) // BLK
    sink = jnp.zeros((GROUP * BLK, 1), F32)
    for g in range(GROUP):
        sink = jnp.where(rowg == g, sink_ref[kh * GROUP + g], sink)
    return q4, k2, v2, s, sink


def _attn_fwd(qkv, bias, sinks):
    S = qkv.shape[0]
    tb = min(ATT_TB, S // BLK)
    T = tb * BLK

    def body(sink_ref, q_ref, kv_ref, kvp_ref, bias_ref, o_ref):
        step = pl.program_id(0)
        for j in range(tb):
            rows = slice(j * BLK, (j + 1) * BLK)
            q, kvc = q_ref[rows, :], kv_ref[rows, :]
            kvp = kvp_ref[...] if j == 0 else kv_ref[(j - 1) * BLK:j * BLK, :]
            has_prev = (step > 0) if j == 0 else None
            outs = []
            for kh in range(N_KV):
                q4, k2, v2, s, sink = _attn_scores(q, kvc, kvp, bias_ref, sink_ref, has_prev, kh)
                m = jnp.maximum(jnp.max(s, axis=1, keepdims=True), sink)
                p = jnp.exp(s - m)
                denom = jnp.sum(p, axis=1, keepdims=True) + jnp.exp(sink - m)
                o4 = _dot((p / denom).astype(BF), v2)
                outs += [o4[g * BLK:(g + 1) * BLK] for g in range(GROUP)]
            o_ref[rows, :] = jnp.concatenate(outs, axis=1).astype(BF)

    return pl.pallas_call(
        body, name="attn_fwd", grid=(S // T,),
        out_shape=_sds((S, Q_W), BF),
        in_specs=[pl.BlockSpec(memory_space=pltpu.SMEM),
                  pl.BlockSpec((T, Q_W), lambda i: (i, 0)),
                  pl.BlockSpec((T, 2 * KV_W), lambda i: (i, 2)),
                  pl.BlockSpec((BLK, 2 * KV_W), lambda i: (jnp.maximum(i * tb - 1, 0), 2)),
                  _const((N_KV, GROUP * BLK, 2 * BLK))],
        out_specs=pl.BlockSpec((T, Q_W), lambda i: (i, 0)),
        compiler_params=_cp(1, 32),
    )(sinks, qkv, qkv, qkv, bias)


def _attn_bwd(qkv, bias, sinks, do):
    S = qkv.shape[0]
    tb = 1
    T = tb * BLK
    nt = S // T

    def body(sink_ref, q_ref, kv_ref, kvp_ref, bias_ref, do_ref, dq_ref, dkv_ref, dbias_ref, dsink_ref, carry):
        i = pl.program_id(0)

        @pl.when(i == 0)
        def _():
            carry[...] = jnp.zeros_like(carry)
            dbias_ref[...] = jnp.zeros_like(dbias_ref)
            dsink_ref[...] = jnp.zeros_like(dsink_ref)

        from_next = carry[...]
        for j in reversed(range(tb)):
            rows = slice(j * BLK, (j + 1) * BLK)
            q, kvc, do_ = q_ref[rows, :], kv_ref[rows, :], do_ref[rows, :]
            kvp = kvp_ref[...] if j == 0 else kv_ref[(j - 1) * BLK:j * BLK, :]
            has_prev = (i < nt - 1) if j == 0 else None
            dqs, dk_cur, dv_cur, dk_prev, dv_prev = [], [], [], [], []
            for kh in range(N_KV):
                q4, k2, v2, s, sink = _attn_scores(q, kvc, kvp, bias_ref, sink_ref, has_prev, kh)
                m = jnp.maximum(jnp.max(s, axis=1, keepdims=True), sink)
                p = jnp.exp(s - m)
                denom = jnp.sum(p, axis=1, keepdims=True) + jnp.exp(sink - m)
                prob = p / denom
                p_sink = jnp.exp(sink - m) / denom
                pb = prob.astype(BF)
                do4 = jnp.concatenate(
                    [do_[:, (kh * GROUP + g) * HD:(kh * GROUP + g + 1) * HD] for g in range(GROUP)], axis=0)
                dp = _dot_nt(do4, v2)
                o4 = _dot(pb, v2)
                delta = jnp.sum(do4.astype(F32) * o4, axis=1, keepdims=True)
                ds = prob * (dp - delta)
                dbias_ref[kh] += ds
                sink_term = p_sink * delta
                for g in range(GROUP):
                    h = kh * GROUP + g
                    val = -jnp.sum(sink_term[g * BLK:(g + 1) * BLK], axis=0, keepdims=True)
                    dsink_ref[pl.ds(h, 1), :] += jnp.broadcast_to(val, (1, 128))
                dsb = ds.astype(BF)
                dq4 = _dot(dsb, k2) * SCALE
                dk2 = jnp.transpose(_dot_tn(q4, dsb)) * SCALE
                dv2 = jnp.transpose(_dot_tn(do4, pb))
                dqs += [dq4[g * BLK:(g + 1) * BLK] for g in range(GROUP)]
                dk_prev.append(dk2[0:BLK])
                dk_cur.append(dk2[BLK:2 * BLK])
                dv_prev.append(dv2[0:BLK])
                dv_cur.append(dv2[BLK:2 * BLK])
            dq_ref[rows, :] = jnp.concatenate(dqs, axis=1).astype(BF)
            dkv_ref[rows, :] = (jnp.concatenate(dk_cur + dv_cur, axis=1) + from_next).astype(BF)
            from_next = jnp.concatenate(dk_prev + dv_prev, axis=1)
        carry[...] = from_next

    return pl.pallas_call(
        body, name="attn_bwd", grid=(nt,),
        out_shape=[_sds((S, Q_W), BF), _sds((S, 2 * KV_W), BF),
                   _sds((N_KV, GROUP * BLK, 2 * BLK), F32), _sds((N_HEADS, 128), F32)],
        in_specs=[pl.BlockSpec(memory_space=pltpu.SMEM),
                  pl.BlockSpec((T, Q_W), lambda i: (nt - 1 - i, 0)),
                  pl.BlockSpec((T, 2 * KV_W), lambda i: (nt - 1 - i, 2)),
                  pl.BlockSpec((BLK, 2 * KV_W), lambda i: (jnp.maximum((nt - 1 - i) * tb - 1, 0), 2)),
                  _const((N_KV, GROUP * BLK, 2 * BLK)),
                  pl.BlockSpec((T, Q_W), lambda i: (nt - 1 - i, 0))],
        out_specs=[pl.BlockSpec((T, Q_W), lambda i: (nt - 1 - i, 0)),
                   pl.BlockSpec((T, 2 * KV_W), lambda i: (nt - 1 - i, 0)),
                   _const((N_KV, GROUP * BLK, 2 * BLK)), _const((N_HEADS, 128))],
        scratch_shapes=[pltpu.VMEM((BLK, 2 * KV_W), F32)],
        compiler_params=_cp(1, 32),
    )(sinks, qkv, qkv, qkv, bias, do)


def _rel_bias_grad(dbias, bucket):
    def body(db_ref, bk_ref, out_ref):
        bk = bk_ref[...]
        lane = lax.broadcasted_iota(jnp.int32, (1, 128), 1)
        for h in range(N_HEADS):
            d = db_ref[h // GROUP, pl.ds((h % GROUP) * BLK, BLK), :]
            row = jnp.zeros((1, 128), F32)
            for b in range(N_BUCKETS):
                tot = jnp.sum(jnp.sum(jnp.where(bk == b, d, 0.0), axis=1, keepdims=True), axis=0, keepdims=True)
                row = jnp.where(lane == b, tot, row)
            out_ref[pl.ds(h, 1), :] = row

    vm = pl.BlockSpec(memory_space=pltpu.VMEM)
    return pl.pallas_call(body, name="rel_bias_grad", out_shape=_sds((N_HEADS, 128), F32),
                          in_specs=[vm, vm], out_specs=vm)(dbias, bucket)


def _gmlp_parts(zg, lg_ref, lb_ref):
    z = zg.astype(F32)
    ge = _gelu(z)
    u, vg = ge[:, 0:G_W], ge[:, G_W:2 * G_W]
    mu = jnp.mean(vg, axis=-1, keepdims=True)
    xc = vg - mu
    rstd = lax.rsqrt(jnp.mean(xc * xc, axis=-1, keepdims=True) + EPS)
    xh = xc * rstd
    return z, u, xh, rstd, xh * lg_ref[...] + lb_ref[...]


def _causal_weights(ws_ref, wc):
    t = lax.broadcasted_iota(jnp.int32, (BLK, BLK), 0)
    s = lax.broadcasted_iota(jnp.int32, (BLK, BLK), 1)
    for g in range(N_HEADS):
        wc[g] = jnp.where(s <= t, ws_ref[g], 0.0).astype(BF)


def _spatial(vb, wc, bst_ref, p, low):
    xp = vb[:, p * 128:(p + 1) * 128]
    s0 = _dot(wc[2 * p], xp) + bst_ref[:, 2 * p:2 * p + 1]
    s1 = _dot(wc[2 * p + 1], xp) + bst_ref[:, 2 * p + 1:2 * p + 2]
    return xp, jnp.where(low, s0, s1)


def _gmlp_fwd(zg, lg, lb, ws, bst):
    S = zg.shape[0]
    tb = min(ATT_TB, S // BLK)
    T = tb * BLK

    def body(zg_ref, lg_ref, lb_ref, ws_ref, bst_ref, o_ref, wc):
        @pl.when(pl.program_id(0) == 0)
        def _():
            _causal_weights(ws_ref, wc)
        low = lax.broadcasted_iota(jnp.int32, (BLK, 128), 1) < HD
        for j in range(tb):
            rows = slice(j * BLK, (j + 1) * BLK)
            _, u, _, _, vln = _gmlp_parts(zg_ref[rows, :], lg_ref, lb_ref)
            vb = vln.astype(BF)
            for p in range(4):
                _, sp = _spatial(vb, wc, bst_ref, p, low)
                o_ref[rows, p * 128:(p + 1) * 128] = (u[:, p * 128:(p + 1) * 128] * sp).astype(BF)

    return pl.pallas_call(
        body, name="gmlp_fwd", grid=(S // T,),
        out_shape=_sds((S, G_W), BF),
        in_specs=[pl.BlockSpec((T, 2 * G_W), lambda i: (i, 0)), _const((1, G_W)), _const((1, G_W)),
                  _const((N_HEADS, BLK, BLK)), _const((BLK, N_HEADS))],
        out_specs=pl.BlockSpec((T, G_W), lambda i: (i, 0)),
        scratch_shapes=[pltpu.VMEM((N_HEADS, BLK, BLK), BF)],
        compiler_params=_cp(1, 32),
    )(zg, lg, lb, ws, bst)


def _gmlp_bwd(zg, d_out, lg, lb, ws, bst):
    S = zg.shape[0]
    tb = min(ATT_TB, S // BLK)
    T = tb * BLK
    nb = S // T

    def body(zg_ref, d_ref, lg_ref, lb_ref, ws_ref, bst_ref, dzg_ref, dws_ref, dbs_ref, dlg_ref, dlb_ref, wc, dbacc):
        i = pl.program_id(0)

        @pl.when(i == 0)
        def _():
            _causal_weights(ws_ref, wc)
            dws_ref[...] = jnp.zeros_like(dws_ref)
            dlg_ref[...] = jnp.zeros_like(dlg_ref)
            dlb_ref[...] = jnp.zeros_like(dlb_ref)
            dbacc[...] = jnp.zeros_like(dbacc)

        low = lax.broadcasted_iota(jnp.int32, (BLK, 128), 1) < HD
        for j in range(tb):
            rows = slice(j * BLK, (j + 1) * BLK)
            z, u, xh, rstd, vln = _gmlp_parts(zg_ref[rows, :], lg_ref, lb_ref)
            vb = vln.astype(BF)
            d = d_ref[rows, :].astype(F32)
            du_parts, dvln_parts = [], []
            for p in range(4):
                xp, sp = _spatial(vb, wc, bst_ref, p, low)
                dp = d[:, p * 128:(p + 1) * 128]
                du_parts.append(dp * sp)
                dsp = dp * u[:, p * 128:(p + 1) * 128]
                dbacc[:, p * 128:(p + 1) * 128] += dsp
                d0 = jnp.where(low, dsp, 0.0).astype(BF)
                d1 = jnp.where(low, 0.0, dsp).astype(BF)
                dws_ref[2 * p] += _dot_nt(d0, xp)
                dws_ref[2 * p + 1] += _dot_nt(d1, xp)
                dvln_parts.append(_dot_tn(wc[2 * p], d0) + _dot_tn(wc[2 * p + 1], d1))
            dvln = jnp.concatenate(dvln_parts, axis=1)
            dlg_ref[...] += _colsum(dvln * xh)
            dlb_ref[...] += _colsum(dvln)
            dxh = dvln * lg_ref[...]
            dvg = rstd * (dxh - jnp.mean(dxh, axis=-1, keepdims=True)
                          - xh * jnp.mean(dxh * xh, axis=-1, keepdims=True))
            dge = jnp.concatenate(du_parts + [dvg], axis=1)
            dzg_ref[rows, :] = (dge * _gelu_grad(z)).astype(BF)

        @pl.when(i == nb - 1)
        def _():
            t = lax.broadcasted_iota(jnp.int32, (BLK, BLK), 0)
            s = lax.broadcasted_iota(jnp.int32, (BLK, BLK), 1)
            for g in range(N_HEADS):
                dws_ref[g] = jnp.where(s <= t, dws_ref[g], 0.0)
            grp = lax.broadcasted_iota(jnp.int32, (N_HEADS, G_W), 0)
            lane = lax.broadcasted_iota(jnp.int32, (N_HEADS, G_W), 1) // HD
            pick = jnp.where(grp == lane, 1.0, 0.0).astype(F32)
            dbs_ref[...] = lax.dot_general(pick, dbacc[...], (((1,), (1,)), ((), ())),
                                           preferred_element_type=F32, precision=HIGH)

    return pl.pallas_call(
        body, name="gmlp_bwd", grid=(nb,),
        out_shape=[_sds((S, 2 * G_W), BF), _sds((N_HEADS, BLK, BLK), F32), _sds((N_HEADS, BLK), F32),
                   _sds((1, G_W), F32), _sds((1, G_W), F32)],
        in_specs=[pl.BlockSpec((T, 2 * G_W), lambda i: (i, 0)), pl.BlockSpec((T, G_W), lambda i: (i, 0)),
                  _const((1, G_W)), _const((1, G_W)), _const((N_HEADS, BLK, BLK)), _const((BLK, N_HEADS))],
        out_specs=[pl.BlockSpec((T, 2 * G_W), lambda i: (i, 0)), _const((N_HEADS, BLK, BLK)),
                   _const((N_HEADS, BLK)), _const((1, G_W)), _const((1, G_W))],
        scratch_shapes=[pltpu.VMEM((N_HEADS, BLK, BLK), BF), pltpu.VMEM((BLK, G_W), F32)],
        compiler_params=_cp(1, 32),
    )(zg, d_out, lg, lb, ws, bst)


def _mix_out(o, gm, gates, h, wa, wg, wo, gate, gp):
    S = h.shape[0]
    R = min(512, S)

    def body(o_ref, gm_ref, gates_ref, h_ref, wa_ref, wg_ref, wo_ref, gate_ref, gp_ref,
             ya_ref, yg_ref, ym_ref, y_ref, hn_ref):
        for r0 in range(0, R, CHUNK):
            rows = slice(r0, r0 + CHUNK)
            ya = _dot(o_ref[rows, :], wa_ref[...])
            yg = _dot(gm_ref[rows, :], wg_ref[...])
            ya_ref[rows, :] = ya.astype(BF)
            yg_ref[rows, :] = yg.astype(BF)
            ym = (gates_ref[rows, 0:D].astype(F32) * ya + gates_ref[rows, D:2 * D].astype(F32) * yg).astype(BF)
            ym_ref[rows, :] = ym
            y = _dot(ym, wo_ref[...])
            y_ref[rows, :] = y
            hn_ref[rows, :] = h_ref[rows, :] + gate_ref[...] * (y * _rms_r(y) * gp_ref[...])

    vec = _const((1, D))
    rows = lambda w_: pl.BlockSpec((R, w_), lambda i: (i, 0))
    return pl.pallas_call(
        body, name="mix_out", grid=(S // R,),
        out_shape=[_sds((S, D), BF)] * 3 + [_sds((S, D), F32)] * 2,
        in_specs=[rows(Q_W), rows(G_W), rows(2 * D), rows(D), _resident((Q_W, D)), _resident((G_W, D)),
                  _resident((D, D)), vec, vec],
        out_specs=[rows(D)] * 5,
        compiler_params=_cp(1, 48),
    )(o, gm, gates, h, wa, wg, wo, gate, gp)


def _mix_out_bwd(dh, y, ya, yg, gates, wa, wg, wo, gate, gp):
    S = dh.shape[0]
    R = min(256, S)

    def body(dh_ref, y_ref, ya_ref, yg_ref, gates_ref, wa_ref, wg_ref, wo_ref, gate_ref, gp_ref,
             dy_ref, dya_ref, dyg_ref, dz_ref, do_ref, dgm_ref, dgate_ref, dgp_ref):
        @pl.when(pl.program_id(0) == 0)
        def _():
            dgate_ref[...] = jnp.zeros_like(dgate_ref)
            dgp_ref[...] = jnp.zeros_like(dgp_ref)
        dy, dgate, dgp = _postnorm_bwd(dh_ref[...], y_ref[...], gate_ref[...], gp_ref[...], 1.0)
        dgate_ref[...] += dgate
        dgp_ref[...] += dgp
        dyb = dy.astype(BF)
        dy_ref[...] = dyb
        dym = _dot_nt(dyb, wo_ref[...])
        ga = gates_ref[:, 0:D].astype(F32)
        gg = gates_ref[:, D:2 * D].astype(F32)
        dya = (dym * ga).astype(BF)
        dyg = (dym * gg).astype(BF)
        dya_ref[...] = dya
        dyg_ref[...] = dyg
        dz_ref[:, 0:D] = (dym * ya_ref[...].astype(F32) * (ga * (1.0 - ga))).astype(BF)
        dz_ref[:, D:2 * D] = (dym * yg_ref[...].astype(F32) * (gg * (1.0 - gg))).astype(BF)
        do_ref[...] = _dot_nt(dya, wa_ref[...]).astype(BF)
        dgm_ref[...] = _dot_nt(dyg, wg_ref[...]).astype(BF)

    vec = _const((1, D))
    rows = lambda w_: pl.BlockSpec((R, w_), lambda i: (i, 0))
    return pl.pallas_call(
        body, name="mix_out_bwd", grid=(S // R,),
        out_shape=[_sds((S, D), BF)] * 3 + [_sds((S, 2 * D), BF), _sds((S, Q_W), BF), _sds((S, G_W), BF),
                                             _sds((1, D), F32), _sds((1, D), F32)],
        in_specs=[rows(D), rows(D), rows(D), rows(D), rows(2 * D), _resident((Q_W, D)), _resident((G_W, D)),
                  _resident((D, D)), vec, vec],
        out_specs=[rows(D)] * 3 + [rows(2 * D), rows(Q_W), rows(G_W), vec, vec],
        compiler_params=_cp(1, 48),
    )(dh, y, ya, yg, gates, wa, wg, wo, gate, gp)


def _mix_dn(dq, dkv, dzg, dzgate, w, h, dh, sc, gp):
    S = h.shape[0]
    R = min(512, S)

    def body(dq_ref, dkv_ref, dzg_ref, dzt_ref, w_ref, h_ref, dh_ref, sc_ref, gp_ref,
             out_ref, dsh_ref, dsc_ref, dgp_ref):
        @pl.when(pl.program_id(0) == 0)
        def _():
            dsh_ref[...] = jnp.zeros_like(dsh_ref)
            dsc_ref[...] = jnp.zeros_like(dsc_ref)
            dgp_ref[...] = jnp.zeros_like(dgp_ref)
        for r0 in range(0, R, CHUNK):
            rows = slice(r0, r0 + CHUNK)
            dn = _dot(dq_ref[rows, :], w_ref[0:Q_W, :])
            dn = dn + _dot(dkv_ref[rows, :], w_ref[Q_W:QKV_W, :])
            dn = dn + _dot(dzg_ref[rows, :], w_ref[ZG_OFF:GATE_OFF, :])
            dn = dn + _dot(dzt_ref[rows, :], w_ref[GATE_OFF:IN_W, :])
            dx, dsh, dsc, dgp = _prenorm_bwd(dn, h_ref[rows, :], gp_ref[...], sc_ref[...])
            out_ref[rows, :] = dh_ref[rows, :] + dx
            dsh_ref[...] += dsh
            dsc_ref[...] += dsc
            dgp_ref[...] += dgp

    vec = _const((1, D))
    rows = lambda w_: pl.BlockSpec((R, w_), lambda i: (i, 0))
    return pl.pallas_call(
        body, name="mix_dn", grid=(S // R,),
        out_shape=[_sds((S, D), F32)] + [_sds((1, D), F32)] * 3,
        in_specs=[rows(Q_W), rows(2 * KV_W), rows(2 * G_W), rows(2 * D), _resident((IN_W, D)),
                  rows(D), rows(D), vec, vec],
        out_specs=[rows(D), vec, vec, vec],
        compiler_params=_cp(1, 48),
    )(dq, dkv, dzg, dzgate, w, h, dh, sc, gp)


def _adamw_math(w, g, m, v):
    m2 = ADAM_B1 * m + (1.0 - ADAM_B1) * g
    v2 = ADAM_B2 * v + (1.0 - ADAM_B2) * (g * g)
    m_hat = m2 / (1.0 - ADAM_B1 ** ADAM_STEP)
    v_hat = v2 / (1.0 - ADAM_B2 ** ADAM_STEP)
    delta = -ADAM_LR * (m_hat / (jnp.sqrt(v_hat) + ADAM_EPS) + ADAM_WD * w)
    return delta, m2, v2


def _row_tile(rows, cols):
    best = None
    for t in range(16, rows + 1, 16):
        if rows % t == 0 and t * cols <= 256 * 1024:
            best = t
    return best if best is not None else rows


def _adamw_sharded(landing, w, m, v, name):
    r, c = w.shape
    tr = _row_tile(r, c)

    def body(l_ref, w_ref, m_ref, v_ref, g_ref, d_ref, m2_ref, v2_ref):
        g = l_ref[0].astype(F32)
        for j in range(1, N_DEV):
            g = g + l_ref[j].astype(F32)
        delta, m2, v2 = _adamw_math(w_ref[...], g, m_ref[...], v_ref[...])
        g_ref[...] = g
        d_ref[...] = delta
        m2_ref[...] = m2
        v2_ref[...] = v2

    row = pl.BlockSpec((tr, c), lambda i: (i, 0))
    return pl.pallas_call(
        body, name=name, grid=(r // tr,),
        out_shape=[_sds((r, c), F32)] * 4,
        in_specs=[pl.BlockSpec((N_DEV, tr, c), lambda i: (0, i, 0)), row, row, row],
        out_specs=[row] * 4,
        compiler_params=_cp(1, 48),
    )(landing, w, m, v)


def _adamw_small(w, g, m, v, name):
    def body(w_ref, g_ref, m_ref, v_ref, d_ref, m2_ref, v2_ref):
        delta, m2, v2 = _adamw_math(w_ref[...], g_ref[...], m_ref[...], v_ref[...])
        d_ref[...] = delta
        m2_ref[...] = m2
        v2_ref[...] = v2

    vm = pl.BlockSpec(memory_space=pltpu.VMEM)
    return pl.pallas_call(body, name=name, out_shape=[_sds(w.shape, F32)] * 3,
                          in_specs=[vm] * 4, out_specs=[vm] * 3)(w, g, m, v)


def _w_ada_update(c8, d_ada, w, m, v):
    tr = 256

    def body(c_ref, d_ref, w_ref, m_ref, v_ref, g_ref, dl_ref, m2_ref, v2_ref):
        cs = c_ref[...]
        cs = cs * jax.nn.sigmoid(cs)
        g = lax.dot_general(cs, d_ref[...], (((0,), (0,)), ((), ())), preferred_element_type=F32, precision=HIGH)
        delta, m2, v2 = _adamw_math(w_ref[...], g, m_ref[...], v_ref[...])
        g_ref[...] = g
        dl_ref[...] = delta
        m2_ref[...] = m2
        v2_ref[...] = v2

    row = pl.BlockSpec((tr, ADA_W), lambda i: (i, 0))
    return pl.pallas_call(
        body, name="w_ada_update", grid=(D // tr,),
        out_shape=[_sds((D, ADA_W), F32)] * 4,
        in_specs=[pl.BlockSpec((N_DEV, tr), lambda i: (0, i)), _const((N_DEV, ADA_W)), row, row, row],
        out_specs=[row] * 4,
        compiler_params=_cp(1, 40),
    )(c8, d_ada, w, m, v)


def _t5_bucket():
    qi = jnp.arange(BLK, dtype=jnp.int32)[:, None]
    kj = jnp.arange(2 * BLK, dtype=jnp.int32)[None, :]
    dist = jnp.maximum(qi + BLK - kj, 0)
    max_exact = N_BUCKETS // 2
    d_f = jnp.maximum(dist, max_exact).astype(F32)
    large = max_exact + (jnp.log(d_f / max_exact) / math.log(MAX_DISTANCE / max_exact)
                         * (N_BUCKETS - max_exact)).astype(jnp.int32)
    large = jnp.minimum(large, N_BUCKETS - 1)
    return jnp.where(dist < max_exact, dist, large)


def _slabs_of_columns(w):
    r, c8 = w.shape
    return jnp.transpose(w.reshape(r, N_DEV, c8 // N_DEV), (1, 0, 2))


def _columns_of_slabs(w8):
    _, r, c = w8.shape
    return jnp.transpose(w8, (1, 0, 2)).reshape(r, N_DEV * c)


def kernel(x, c, rel_bias, w_ada, b_ada, pre_norm_g, post_norm_g, w_ffn1_in, w_ffn1_out, w_in, sinks, gmlp_ln_g, gmlp_ln_b, gmlp_w_s, gmlp_b_s, w_br_attn, w_br_gmlp, w_out, w_ffn2_in, w_ffn2_out, loss_target, m_rel_bias, m_w_ada, m_b_ada, m_pre_norm_g, m_post_norm_g, m_w_ffn1_in, m_w_ffn1_out, m_w_in, m_sinks, m_gmlp_ln_g, m_gmlp_ln_b, m_gmlp_w_s, m_gmlp_b_s, m_w_br_attn, m_w_br_gmlp, m_w_out, m_w_ffn2_in, m_w_ffn2_out, v_rel_bias, v_w_ada, v_b_ada, v_pre_norm_g, v_post_norm_g, v_w_ffn1_in, v_w_ffn1_out, v_w_in, v_sinks, v_gmlp_ln_g, v_gmlp_ln_b, v_gmlp_w_s, v_gmlp_b_s, v_w_br_attn, v_w_br_gmlp, v_w_out, v_w_ffn2_in, v_w_ffn2_out):
    me = 4 * lax.axis_index("x") + 2 * lax.axis_index("y") + lax.axis_index("c")
    x0 = x[0]
    target = loss_target[0]

    transposed = ("w_ffn1_in", "w_in", "w_ffn2_in")
    shards = [w_ffn1_in[0].T, w_ffn1_out[0], w_in[0].T, w_br_attn[0], w_br_gmlp[0], w_out[0],
              w_ffn2_in[0].T, w_ffn2_out[0]]
    shards_bf = [s.astype(BF) for s in shards]
    groups = [shards_bf[0:1], shards_bf[1:2], shards_bf[2:6], shards_bf[6:7], shards_bf[7:8]]

    def gather_start(i, after):
        return _slabs_start("gather", groups[i], after, "gather_start_%d" % i)

    def forward_start(st, i, after):
        lands = _slabs_wait("gather", len(groups[i]), st, after, "gather_wait_%d" % i)
        return _slabs_start("forward", lands, c, "forward_start_%d" % i)

    def gathered(st, i, after):
        return _slabs_wait("forward", len(groups[i]), st, after, "forward_wait_%d" % i)

    gs0 = gather_start(0, c)

    small = jnp.concatenate([c[0], pre_norm_g[0].reshape(-1), post_norm_g[0].reshape(-1)])
    small8 = jnp.broadcast_to(small[None, :], (8, small.shape[0]))
    b_ada64 = jnp.repeat(b_ada.reshape(N_DEV, ADA_W), 8, axis=0)
    gath, ada64 = _ada_forward(small8, w_ada[0], b_ada64)
    gath8 = gath[::8]
    ada = ada64[::8].reshape(9, D)
    sh1, sc1, g1, sh2, sc2, g2, sh3, sc3, g3 = [ada[k:k + 1] for k in range(9)]
    gains = gath8[:, D:].reshape(N_DEV, 2, 3, 128)
    pre_g = jnp.transpose(gains[:, 0], (1, 0, 2)).reshape(3, D)
    post_g = jnp.transpose(gains[:, 1], (1, 0, 2)).reshape(3, D)
    pre = [pre_g[k:k + 1] for k in range(3)]
    post = [post_g[k:k + 1] for k in range(3)]

    bucket = _t5_bucket()
    bias = _bias_table(rel_bias, bucket)
    sinks8 = sinks[0]
    lg, lb = gmlp_ln_g, gmlp_ln_b
    ws = gmlp_w_s[0]
    bst = jnp.transpose(gmlp_b_s[0])

    fs0 = forward_start(gs0, 0, sh1)
    gs1 = gather_start(1, fs0[-1])
    gs2 = gather_start(2, gs1[-1])
    (wf1_in,) = gathered(fs0, 0, gs2[-1])
    n1, fg1, fu1, fa1 = _ffn_in(x0, sh1, sc1, pre[0], wf1_in, "ffn1_in")
    fs1 = forward_start(gs1, 1, n1)
    fs2 = forward_start(gs2, 2, fs1[-1])
    gs3 = gather_start(3, fs2[-1])
    gs4 = gather_start(4, gs3[-1])
    wf1_out = gathered(fs1, 1, gs4[-1])[0].reshape(4, FS, D)
    h1, y1 = _ffn_out(fa1, wf1_out, x0, g1, post[0], "ffn1_out")
    mix_w = gathered(fs2, 2, h1)
    w_in_full = mix_w[0].reshape(IN_W, D)
    w_bra = _columns_of_slabs(mix_w[1])
    w_brg = _columns_of_slabs(mix_w[2])
    w_out_full = mix_w[3].reshape(D, D)
    n2, qkv, zg, gates = _mix_in(h1, sh2, sc2, pre[1], w_in_full)
    att = _attn_fwd(qkv, bias, sinks8)
    gm = _gmlp_fwd(zg, lg, lb, ws, bst)
    fs3 = forward_start(gs3, 3, gm)
    fs4 = forward_start(gs4, 4, fs3[-1])
    ya, yg, ymix, y2, h2 = _mix_out(att, gm, gates, h1, w_bra, w_brg, w_out_full, g2 + fs4[-1], post[1])
    (wf2_in,) = gathered(fs3, 3, h2)
    wf2_out = gathered(fs4, 4, wf2_in)[0].reshape(4, FS, D)
    n3, fg3, fu3, fa3, y3, dh3, sq = _ffn_fwd_loss(h2, sh3, sc3, pre[2], wf2_in, wf2_out, g3, post[2], target,
                                                   "ffn2_fwd_loss")
    loss = lax.psum(0.5 * sq[0, 0] / D, ("x", "y", "c"))

    def exchange_start(i, arrays):
        return _slabs_start("exchange", arrays, sq, "exchange_start_%d" % i)

    dy3, dgu3, dh2, d_g3, d_post2, d_sh3, d_sc3, d_pre2 = _ffn_bwd(
        dh3, y3, fg3, fu3, wf2_out, wf2_in, h2, g3, post[2], sc3, pre[2], "ffn2_bwd")
    gw_f2_out = _tn_matmul(fa3, dy3, "ffn2_out_wgrad").reshape(N_DEV, D_FF // N_DEV, D)
    ex0 = exchange_start(0, [gw_f2_out])
    gw_f2_in = _tn_matmul(dgu3, n3, "ffn2_in_wgrad").reshape(N_DEV, FS, D)
    ex1 = exchange_start(1, [gw_f2_in])

    dy2, dya, dyg, dzgate, d_att, d_gm, d_g2, d_post1 = _mix_out_bwd(
        dh2, y2, ya, yg, gates, w_bra, w_brg, w_out_full, g2 + ex0[-1] + ex1[-1], post[1])
    gw_out = _tn_matmul(ymix, dy2, "w_out_wgrad").reshape(N_DEV, D // N_DEV, D)
    gw_bra = _slabs_of_columns(_tn_matmul(att, dya, "w_br_attn_wgrad").reshape(Q_W, D))
    gw_brg = _slabs_of_columns(_tn_matmul(gm, dyg, "w_br_gmlp_wgrad").reshape(G_W, D))
    ex2 = exchange_start(2, [gw_bra, gw_brg, gw_out])
    dq, dkv, dbias, dsink = _attn_bwd(qkv, bias, sinks8, d_att)
    dzg, d_ws, d_bs, d_lg, d_lb = _gmlp_bwd(zg, d_gm, lg, lb, ws, bst)
    d_rel = _rel_bias_grad(dbias, bucket)
    early = jnp.concatenate([
        jnp.concatenate([d_lg.reshape(4, 128), d_lb.reshape(4, 128)], axis=0),
        d_bs, d_rel, dsink, d_ws.reshape(N_HEADS * BLK, BLK)], axis=0)
    sm0 = _slabs_start("gather_all", [early], sq, "small_gather_start")
    dh1, d_sh2, d_sc2, d_pre1 = _mix_dn(dq, dkv, dzg, dzgate, w_in_full, h1, dh2, sc2 + ex2[-1] + sm0[-1], pre[1])
    gw_in = jnp.concatenate(
        [_tn_matmul(dq, n2, "w_in_q_wgrad").reshape(Q_W, D),
         _tn_matmul(dkv, n2, "w_in_kv_wgrad").reshape(2 * KV_W, D),
         _tn_matmul(dzg, n2, "w_in_zg_wgrad").reshape(2 * G_W, D),
         _tn_matmul(dzgate, n2, "w_in_gate_wgrad").reshape(2 * D, D)], axis=0).reshape(N_DEV, IN_W // N_DEV, D)
    ex3 = exchange_start(3, [gw_in])

    dy1, dgu1, d_g1, d_post0 = _ffn_out_bwd(dh1, y1, fg1, fu1, wf1_out, g1 + ex3[-1], post[0], "ffn1_out_bwd")
    gw_f1_out = _tn_matmul(fa1, dy1, "ffn1_out_wgrad").reshape(N_DEV, D_FF // N_DEV, D)
    gw_f1_in = _tn_matmul(dgu1, n1, "ffn1_in_wgrad").reshape(N_DEV, FS, D)
    ex4 = exchange_start(4, [gw_f1_out, gw_f1_in])
    grad_x, d_sh1, d_sc1, d_pre0 = _ffn_dn(dgu1, wf1_in, x0, dh1, sc1 + ex4[-1], pre[0], "ffn1_dn")

    landed = {}
    for i, (ex, nms) in enumerate([(ex0, ["w_ffn2_out"]), (ex1, ["w_ffn2_in"]),
                                   (ex2, ["w_br_attn", "w_br_gmlp", "w_out"]), (ex3, ["w_in"]),
                                   (ex4, ["w_ffn1_out", "w_ffn1_in"])]):
        for nm, land in zip(nms, _slabs_wait("exchange", len(nms), ex, grad_x, "exchange_wait_%d" % i)):
            landed[nm] = land
    moments = [(m_w_ffn1_in, v_w_ffn1_in), (m_w_ffn1_out, v_w_ffn1_out), (m_w_in, v_w_in),
               (m_w_br_attn, v_w_br_attn), (m_w_br_gmlp, v_w_br_gmlp), (m_w_out, v_w_out),
               (m_w_ffn2_in, v_w_ffn2_in), (m_w_ffn2_out, v_w_ffn2_out)]
    names = ["w_ffn1_in", "w_ffn1_out", "w_in", "w_br_attn", "w_br_gmlp", "w_out", "w_ffn2_in", "w_ffn2_out"]
    big = {}
    for nm, w_, (m_, v_) in zip(names, shards, moments):
        if nm in transposed:
            res4 = _adamw_sharded(landed[nm], w_, m_[0].T, v_[0].T, "adamw_" + nm)
            big[nm] = [a.T[None] for a in res4]
        else:
            big[nm] = [a[None] for a in _adamw_sharded(landed[nm], w_, m_[0], v_[0], "adamw_" + nm)]

    d_ada = jnp.concatenate([v_.reshape(8, 128) for v_ in
                             (d_sh1, d_sc1, d_g1, d_sh2, d_sc2, d_g2, d_sh3, d_sc3, d_g3)], axis=0)
    d_pre = jnp.concatenate([d_pre0, d_pre1, d_pre2], axis=0)
    d_post = jnp.concatenate([d_post0, d_post1, d_post2], axis=0)
    late = jnp.concatenate([d_ada, _slabs_of_columns(d_pre).reshape(24, 128),
                            _slabs_of_columns(d_post).reshape(24, 128)], axis=0)
    tot, every = _small_allreduce(late)
    (early_land,) = _slabs_wait("gather_all", 1, sm0, grad_x, "small_gather_wait")
    tot_early = _sum_slabs(early_land)

    g_b_ada = tot[0:72].reshape(1, 9 * D)
    g_pre = lax.dynamic_slice_in_dim(tot[72:96], 3 * me, 3, axis=0)[None]
    g_post = lax.dynamic_slice_in_dim(tot[96:120], 3 * me, 3, axis=0)[None]
    g_lg = tot_early[0:4].reshape(1, G_W)
    g_lb = tot_early[4:8].reshape(1, G_W)
    g_bs = tot_early[8:16][None]
    g_rel = jnp.transpose(tot_early[16:24, 0:N_BUCKETS])
    g_sinks = tot_early[24:32, 0][None]
    g_ws = tot_early[32:1056].reshape(1, N_HEADS, BLK, BLK)

    d_ada_mine = lax.dynamic_slice_in_dim(every[:, 0:72].reshape(N_DEV, N_DEV, ADA_W), me, 1, axis=1)[:, 0]
    ada_out = [a[None] for a in _w_ada_update(gath8[:, 0:D], d_ada_mine, w_ada[0], m_w_ada[0], v_w_ada[0])]

    def small_step(w_, g_, m_, v_, nm):
        shp = w_.shape
        two_d = (int(math.prod(shp[:-1])), shp[-1])
        d_, m2_, v2_ = _adamw_small(w_.reshape(two_d), g_.reshape(two_d), m_.reshape(two_d), v_.reshape(two_d),
                                    "adamw_" + nm)
        return [g_, d_.reshape(shp), m2_.reshape(shp), v2_.reshape(shp)]

    res = {
        "rel_bias": small_step(rel_bias, g_rel, m_rel_bias, v_rel_bias, "rel_bias"),
        "w_ada": ada_out,
        "b_ada": small_step(b_ada, g_b_ada, m_b_ada, v_b_ada, "b_ada"),
        "pre_norm_g": small_step(pre_norm_g, g_pre, m_pre_norm_g, v_pre_norm_g, "pre_norm_g"),
        "post_norm_g": small_step(post_norm_g, g_post, m_post_norm_g, v_post_norm_g, "post_norm_g"),
        "sinks": small_step(sinks, g_sinks, m_sinks, v_sinks, "sinks"),
        "gmlp_ln_g": small_step(gmlp_ln_g, g_lg, m_gmlp_ln_g, v_gmlp_ln_g, "gmlp_ln_g"),
        "gmlp_ln_b": small_step(gmlp_ln_b, g_lb, m_gmlp_ln_b, v_gmlp_ln_b, "gmlp_ln_b"),
        "gmlp_w_s": small_step(gmlp_w_s, g_ws, m_gmlp_w_s, v_gmlp_w_s, "gmlp_w_s"),
        "gmlp_b_s": small_step(gmlp_b_s, g_bs, m_gmlp_b_s, v_gmlp_b_s, "gmlp_b_s"),
    }
    res.update(big)
    order = ["rel_bias", "w_ada", "b_ada", "pre_norm_g", "post_norm_g", "w_ffn1_in", "w_ffn1_out", "w_in", "sinks",
             "gmlp_ln_g", "gmlp_ln_b", "gmlp_w_s", "gmlp_b_s", "w_br_attn", "w_br_gmlp", "w_out", "w_ffn2_in",
             "w_ffn2_out"]
    outs = [loss, grad_x[None]]
    for k in range(4):
        outs += [res[nm][k] for nm in order]
    return tuple(outs)
```

```python
import functools
import math

import jax
import jax.numpy as jnp
from jax import lax
from jax.experimental import pallas as pl
from jax.experimental.pallas import tpu as pltpu

F32 = jnp.float32
BF = jnp.bfloat16

N_DEV = 8
D = 1024
D_FF = 2816
FS = D_FF // 4
N_HEADS = 8
N_KV = 2
GROUP = 4
HD = 64
BLK = 128
Q_W = 512
KV_W = 128
G_W = 512
QKV_W = Q_W + 2 * KV_W
ZG_OFF = QKV_W
GATE_OFF = ZG_OFF + 2 * G_W
IN_W = GATE_OFF + 2 * D
N_BUCKETS = 32
MAX_DISTANCE = 128
EPS = 1e-6
NEG = -1e30
SCALE = HD ** -0.5
ADA_W = 9 * D // N_DEV

ADAM_LR = 0.001
ADAM_B1 = 0.9
ADAM_B2 = 0.999
ADAM_EPS = 1e-08
ADAM_WD = 0.01
ADAM_STEP = 10

CHUNK = 256
MIB = 1024 * 1024
MESH = pl.DeviceIdType.MESH
HIGH = lax.Precision.HIGHEST


def _cp(n_grid, vmem_mib):
    return pltpu.CompilerParams(dimension_semantics=("arbitrary",) * n_grid,
                                vmem_limit_bytes=vmem_mib * MIB)


def _const(shape):
    return pl.BlockSpec(shape, lambda *_: (0,) * len(shape))


def _resident(shape):
    return pl.BlockSpec(shape, lambda *_: (0,) * len(shape), pipeline_mode=pl.Buffered(1))


def _sds(shape, dtype):
    return jax.ShapeDtypeStruct(shape, dtype)


def _dot(a, b):
    return jnp.dot(a, b, preferred_element_type=F32)


def _dot_nt(a, b):
    return lax.dot_general(a, b, (((1,), (1,)), ((), ())), preferred_element_type=F32)


def _dot_tn(a, b):
    return lax.dot_general(a, b, (((0,), (0,)), ((), ())), preferred_element_type=F32)


def _rms_r(x):
    return lax.rsqrt(jnp.mean(x * x, axis=-1, keepdims=True) + EPS)


def _colsum(x):
    return jnp.sum(x, axis=0, keepdims=True)


def _prenorm(x, gp, sc, sh):
    return (x * _rms_r(x) * gp) * (1.0 + sc) + sh


def _prenorm_bwd(dn, x, gp, sc):
    r = _rms_r(x)
    xh = x * r
    t = dn * (1.0 + sc) * gp
    dx = r * (t - xh * jnp.mean(t * xh, axis=-1, keepdims=True))
    return dx, _colsum(dn), _colsum(dn * xh * gp), _colsum(dn * (1.0 + sc) * xh)


def _postnorm_bwd(dh, y, gate, gp, res):
    r = _rms_r(y)
    yh = y * r
    dyn = (res * gate) * dh
    t = dyn * gp
    dy = r * (t - yh * jnp.mean(t * yh, axis=-1, keepdims=True))
    return dy, _colsum(res * dh * yh * gp), _colsum(dyn * yh)


def _gelu(x):
    k = math.sqrt(2.0 / math.pi)
    return 0.5 * x * (1.0 + jnp.tanh(k * (x + 0.044715 * x * x * x)))


def _gelu_grad(x):
    k = math.sqrt(2.0 / math.pi)
    t = jnp.tanh(k * (x + 0.044715 * x * x * x))
    return 0.5 * (1.0 + t) + 0.5 * x * (1.0 - t * t) * (k * (1.0 + 3.0 * 0.044715 * x * x))


def _my_place():
    x, y, c = lax.axis_index("x"), lax.axis_index("y"), lax.axis_index("c")
    return x, y, c, 4 * x + 2 * y + c


def _peer(x, y, c, k):
    px = 1 - x if k & 4 else x
    py = 1 - y if k & 2 else y
    pc = 1 - c if k & 1 else c
    return (px, py, pc), 4 * px + 2 * py + pc


HBM_SPEC = pl.BlockSpec(memory_space=pltpu.HBM)
SEM_SPEC = pl.BlockSpec(memory_space=pltpu.SEMAPHORE)
EFFECT = pltpu.SideEffectType.DATAFLOW_SIDE_EFFECTING


RELATIONS = {"exchange": (1, 2, 3, 4, 5, 6, 7), "gather": (1, 2, 4, 6), "forward": (2, 4, 6),
             "gather_all": (1, 2, 3, 4, 5, 6, 7)}


def _slab_copies(mode, srcs, lands, send, recv, loc):
    x, y, c, me = _my_place()
    rel = RELATIONS[mode]
    remote, local = [], []
    for t in range(len(lands)):
        for i, k in enumerate(rel):
            peer, peer_lin = _peer(x, y, c, k)
            if mode == "exchange":
                src, dst, to = srcs[t].at[peer_lin], lands[t].at[me], peer
            elif mode in ("gather", "gather_all"):
                src, dst, to = srcs[t], lands[t].at[me], peer
            else:
                src, dst, to = lands[t].at[peer_lin], lands[t].at[peer_lin], _peer(x, y, c, 1)[0]
            remote.append(pltpu.make_async_remote_copy(
                src_ref=src, dst_ref=dst, send_sem=send.at[t * len(rel) + i], recv_sem=recv.at[t * len(rel) + i],
                device_id=to, device_id_type=MESH))
        if mode == "exchange":
            local.append(pltpu.make_async_copy(srcs[t].at[me], lands[t].at[me], loc.at[t]))
        elif mode in ("gather", "gather_all"):
            local.append(pltpu.make_async_copy(srcs[t], lands[t].at[me], loc.at[t]))
    return remote, local


def _slabs_start(mode, arrays, after, name):
    n = len(arrays)
    if mode == "forward":
        thru = list(arrays)
    else:
        shapes = [a.shape if mode == "exchange" else (N_DEV,) + a.shape for a in arrays]
        thru = list(arrays) + [lax.empty(s, a.dtype) for s, a in zip(shapes, arrays)]
    m = len(thru)
    n_sem = n * len(RELATIONS[mode])

    def body(*refs):
        srcs, lands = refs[:n], refs[m - n:m]
        send, recv, loc = refs[m + 1:m + 4]
        remote, local = _slab_copies(mode, srcs, lands, send, recv, loc)
        for cp in remote + local:
            cp.start()
        refs[-1][...] = jnp.zeros_like(refs[-1])

    return pl.pallas_call(
        body, name=name,
        out_shape=(pltpu.SemaphoreType.DMA((n_sem,)), pltpu.SemaphoreType.DMA((n_sem,)),
                   pltpu.SemaphoreType.DMA((n,)),
                   *[pltpu.HBM(a.shape, a.dtype) for a in thru],
                   _sds((1, D), F32)),
        in_specs=[HBM_SPEC] * m + [pl.BlockSpec(memory_space=pl.ANY)],
        out_specs=(SEM_SPEC, SEM_SPEC, SEM_SPEC, *[HBM_SPEC] * m, pl.BlockSpec(memory_space=pltpu.VMEM)),
        input_output_aliases={t: 3 + t for t in range(m)},
        compiler_params=pltpu.CompilerParams(has_side_effects=EFFECT),
    )(*[pltpu.with_memory_space_constraint(a, pltpu.HBM) for a in thru], after)


def _slabs_wait(mode, n, started, after, name):
    sems = started[0:3]
    thru = started[3:-1]
    m = len(thru)

    def body(*refs):
        srcs, lands = refs[:n], refs[m - n:m]
        remote, local = _slab_copies(mode, srcs, lands, *refs[m:m + 3])
        for cp in remote:
            cp.wait_send()
            cp.wait_recv()
        for cp in local:
            cp.wait()

    res = pl.pallas_call(
        body, name=name,
        out_shape=tuple(pltpu.HBM(a.shape, a.dtype) for a in thru),
        in_specs=[HBM_SPEC] * m + [SEM_SPEC] * 3 + [pl.BlockSpec(memory_space=pl.ANY)],
        out_specs=tuple([HBM_SPEC] * m),
        input_output_aliases={t: t for t in range(m)},
        compiler_params=pltpu.CompilerParams(has_side_effects=EFFECT),
    )(*thru, *sems, after)
    return list(res[m - n:m])


def _ada_forward(small8, w_ada, b_ada64):
    sw = small8.shape[1]

    def body(sm_ref, w_ref, b_ref, gath_ref, ada_ref, part_ref, send1, recv1, send2, recv2):
        x, y, c, me = _my_place()
        row_me = pl.multiple_of(me * 8, 8)
        gath_ref[pl.ds(row_me, 8), :] = sm_ref[...]
        first = []
        for k in range(1, N_DEV):
            peer, _ = _peer(x, y, c, k)
            cp = pltpu.make_async_remote_copy(
                src_ref=sm_ref, dst_ref=gath_ref.at[pl.ds(row_me, 8), :], send_sem=send1.at[k - 1],
                recv_sem=recv1.at[k - 1], device_id=peer, device_id_type=MESH)
            cp.start()
            first.append(cp)
        for cp in first:
            cp.wait()
        cs = gath_ref[:, 0:D]
        cs = cs * jax.nn.sigmoid(cs)
        part_ref[...] = jnp.dot(cs, w_ref[...], preferred_element_type=F32, precision=HIGH)
        ada_ref[pl.ds(row_me, 8), :] = part_ref[pl.ds(row_me, 8), :]
        second = []
        for k in range(1, N_DEV):
            peer, peer_lin = _peer(x, y, c, k)
            cp = pltpu.make_async_remote_copy(
                src_ref=part_ref.at[pl.ds(pl.multiple_of(peer_lin * 8, 8), 8), :],
                dst_ref=ada_ref.at[pl.ds(row_me, 8), :], send_sem=send2.at[k - 1],
                recv_sem=recv2.at[k - 1], device_id=peer, device_id_type=MESH)
            cp.start()
            second.append(cp)
        for cp in second:
            cp.wait()
        ada_ref[...] = ada_ref[...] + b_ref[...]

    vm = pl.BlockSpec(memory_space=pltpu.VMEM)
    return pl.pallas_call(
        body, name="ada_forward",
        out_shape=[_sds((8 * N_DEV, sw), F32), _sds((8 * N_DEV, ADA_W), F32)],
        in_specs=[vm, vm, vm], out_specs=[vm, vm],
        scratch_shapes=[pltpu.VMEM((8 * N_DEV, ADA_W), F32)] + [pltpu.SemaphoreType.DMA((7,))] * 4,
        compiler_params=pltpu.CompilerParams(vmem_limit_bytes=32 * MIB),
    )(small8, w_ada, b_ada64)


def _sum_slabs(land):
    def body(l_ref, o_ref):
        acc = l_ref[0]
        for j in range(1, N_DEV):
            acc = acc + l_ref[j]
        o_ref[...] = acc

    vm = pl.BlockSpec(memory_space=pltpu.VMEM)
    return pl.pallas_call(body, name="sum_slabs", out_shape=_sds(land.shape[1:], F32), in_specs=[vm], out_specs=vm,
                          compiler_params=pltpu.CompilerParams(vmem_limit_bytes=32 * MIB))(land)


def _small_allreduce(pack):
    rows = pack.shape[0]

    def body(p_ref, sum_ref, gath_ref, send, recv):
        x, y, c, me = _my_place()
        gath_ref[me] = p_ref[...]
        cps = []
        for k in range(1, N_DEV):
            peer, _ = _peer(x, y, c, k)
            cp = pltpu.make_async_remote_copy(
                src_ref=p_ref, dst_ref=gath_ref.at[me], send_sem=send.at[k - 1],
                recv_sem=recv.at[k - 1], device_id=peer, device_id_type=MESH)
            cp.start()
            cps.append(cp)
        for cp in cps:
            cp.wait()
        acc = gath_ref[0]
        for j in range(1, N_DEV):
            acc = acc + gath_ref[j]
        sum_ref[...] = acc

    vm = pl.BlockSpec(memory_space=pltpu.VMEM)
    return pl.pallas_call(
        body, name="small_allreduce",
        out_shape=[_sds((rows, 128), F32), _sds((N_DEV, rows, 128), F32)],
        in_specs=[vm], out_specs=[vm, vm],
        scratch_shapes=[pltpu.SemaphoreType.DMA((7,)), pltpu.SemaphoreType.DMA((7,))],
        compiler_params=pltpu.CompilerParams(vmem_limit_bytes=40 * MIB),
    )(pack)


def _ffn_in(h, sh, sc, gp, wt8, name):
    S = h.shape[0]
    R = min(512, S)

    def body(h_ref, sh_ref, sc_ref, gp_ref, w_ref, n_ref, dg_ref, sl_ref, a_ref):
        for r0 in range(0, R, CHUNK):
            rows = slice(r0, r0 + CHUNK)
            n = _prenorm(h_ref[rows, :], gp_ref[...], sc_ref[...], sh_ref[...]).astype(BF)
            n_ref[rows, :] = n
            for s in range(4):
                g = _dot_nt(n, w_ref[s])
                u = _dot_nt(n, w_ref[s + 4])
                sg = jax.nn.sigmoid(g)
                silu = g * sg
                dg_ref[s, rows, :] = (u * (sg * (1.0 + g * (1.0 - sg)))).astype(BF)
                sl_ref[s, rows, :] = silu.astype(BF)
                a_ref[s, rows, :] = (silu * u).astype(BF)

    vec = _const((1, D))
    row = pl.BlockSpec((R, D), lambda i: (i, 0))
    blk = pl.BlockSpec((4, R, FS), lambda i: (0, i, 0))
    return pl.pallas_call(
        body, name=name, grid=(S // R,),
        out_shape=[_sds((S, D), BF)] + [_sds((4, S, FS), BF)] * 3,
        in_specs=[row, vec, vec, vec, _resident((N_DEV, FS, D))],
        out_specs=[row, blk, blk, blk],
        compiler_params=_cp(1, 56),
    )(h, sh, sc, gp, wt8)


def _ffn_out(a, w4, h, gate, gp, name):
    S = h.shape[0]
    R = min(512, S)

    def body(a_ref, w_ref, h_ref, gate_ref, gp_ref, hn_ref, y_ref):
        for r0 in range(0, R, CHUNK):
            rows = slice(r0, r0 + CHUNK)
            y = _dot(a_ref[0, rows, :], w_ref[0])
            for s in range(1, 4):
                y = y + _dot(a_ref[s, rows, :], w_ref[s])
            y_ref[rows, :] = y
            hn_ref[rows, :] = h_ref[rows, :] + (0.5 * gate_ref[...]) * (y * _rms_r(y) * gp_ref[...])

    vec = _const((1, D))
    row = pl.BlockSpec((R, D), lambda i: (i, 0))
    return pl.pallas_call(
        body, name=name, grid=(S // R,),
        out_shape=[_sds((S, D), F32), _sds((S, D), F32)],
        in_specs=[pl.BlockSpec((4, R, FS), lambda i: (0, i, 0)), _resident((4, FS, D)), row, vec, vec],
        out_specs=[row, row],
        compiler_params=_cp(1, 48),
    )(a, w4, h, gate, gp)


def _ffn_fwd_loss(h, sh, sc, gpre, wt8, w4, gate, gpost, target, name):
    S = h.shape[0]
    R = min(256, S)

    def body(h_ref, sh_ref, sc_ref, gpre_ref, w_ref, w4_ref, gate_ref, gpost_ref, t_ref,
             n_ref, dg_ref, sl_ref, a_ref, y_ref, dh_ref, tot_ref):
        @pl.when(pl.program_id(0) == 0)
        def _():
            tot_ref[...] = jnp.zeros_like(tot_ref)
        hh = h_ref[...]
        n = _prenorm(hh, gpre_ref[...], sc_ref[...], sh_ref[...]).astype(BF)
        n_ref[...] = n
        y = None
        for s in range(4):
            g = _dot_nt(n, w_ref[s])
            u = _dot_nt(n, w_ref[s + 4])
            sg = jax.nn.sigmoid(g)
            silu = g * sg
            dg_ref[s] = (u * (sg * (1.0 + g * (1.0 - sg)))).astype(BF)
            sl_ref[s] = silu.astype(BF)
            a = (silu * u).astype(BF)
            a_ref[s] = a
            part = _dot(a, w4_ref[s])
            y = part if y is None else y + part
        y_ref[...] = y
        e = hh + (0.5 * gate_ref[...]) * (y * _rms_r(y) * gpost_ref[...]) - t_ref[...]
        dh_ref[...] = e * (1.0 / D)
        tot_ref[...] += jnp.sum(jnp.sum(e * e, axis=1, keepdims=True), axis=0, keepdims=True)

    vec = _const((1, D))
    row = pl.BlockSpec((R, D), lambda i: (i, 0))
    blk = pl.BlockSpec((4, R, FS), lambda i: (0, i, 0))
    return pl.pallas_call(
        body, name=name, grid=(S // R,),
        out_shape=[_sds((S, D), BF)] + [_sds((4, S, FS), BF)] * 3 + [_sds((S, D), F32)] * 2 + [_sds((1, 1), F32)],
        in_specs=[row, vec, vec, vec, _resident((N_DEV, FS, D)), _resident((4, FS, D)), vec, vec, row],
        out_specs=[row, blk, blk, blk, row, row, _const((1, 1))],
        compiler_params=_cp(1, 56),
    )(h, sh, sc, gpre, wt8, w4, gate, gpost, target)


def _ffn_out_bwd(dh, y, dsilu_u, silu, w4, gate, gp, name):
    S = dh.shape[0]
    R = min(512, S)

    def body(dh_ref, y_ref, g_ref, u_ref, w_ref, gate_ref, gp_ref, dy_ref, dgu_ref, dgate_ref, dgp_ref):
        @pl.when(pl.program_id(0) == 0)
        def _():
            dgate_ref[...] = jnp.zeros_like(dgate_ref)
            dgp_ref[...] = jnp.zeros_like(dgp_ref)
        for r0 in range(0, R, CHUNK):
            rows = slice(r0, r0 + CHUNK)
            dy, dgate, dgp = _postnorm_bwd(dh_ref[rows, :], y_ref[rows, :], gate_ref[...], gp_ref[...], 0.5)
            dgate_ref[...] += dgate
            dgp_ref[...] += dgp
            dyb = dy.astype(BF)
            dy_ref[rows, :] = dyb
            for s in range(4):
                da = _dot_nt(dyb, w_ref[s])
                dgu_ref[s, rows, :] = (da * g_ref[s, rows, :].astype(F32)).astype(BF)
                dgu_ref[s + 4, rows, :] = (da * u_ref[s, rows, :].astype(F32)).astype(BF)

    vec = _const((1, D))
    row = pl.BlockSpec((R, D), lambda i: (i, 0))
    blk4 = pl.BlockSpec((4, R, FS), lambda i: (0, i, 0))
    return pl.pallas_call(
        body, name=name, grid=(S // R,),
        out_shape=[_sds((S, D), BF), _sds((8, S, FS), BF), _sds((1, D), F32), _sds((1, D), F32)],
        in_specs=[row, row, blk4, blk4, _resident((4, FS, D)), vec, vec],
        out_specs=[row, pl.BlockSpec((8, R, FS), lambda i: (0, i, 0)), vec, vec],
        compiler_params=_cp(1, 56),
    )(dh, y, dsilu_u, silu, w4, gate, gp)


def _ffn_dn(dgu, wt8, h, dh, sc, gp, name):
    S = h.shape[0]
    R = min(512, S)

    def body(dgu_ref, w_ref, h_ref, dh_ref, sc_ref, gp_ref, out_ref, dsh_ref, dsc_ref, dgp_ref):
        @pl.when(pl.program_id(0) == 0)
        def _():
            dsh_ref[...] = jnp.zeros_like(dsh_ref)
            dsc_ref[...] = jnp.zeros_like(dsc_ref)
            dgp_ref[...] = jnp.zeros_like(dgp_ref)

        for r0 in range(0, R, CHUNK):
            rows = slice(r0, r0 + CHUNK)
            dn = _dot(dgu_ref[0, rows, :], w_ref[0])
            for j in range(1, N_DEV):
                dn = dn + _dot(dgu_ref[j, rows, :], w_ref[j])
            dx, dsh, dsc, dgp = _prenorm_bwd(dn, h_ref[rows, :], gp_ref[...], sc_ref[...])
            out_ref[rows, :] = dh_ref[rows, :] + dx
            dsh_ref[...] += dsh
            dsc_ref[...] += dsc
            dgp_ref[...] += dgp

    vec = _const((1, D))
    row = pl.BlockSpec((R, D), lambda i: (i, 0))
    return pl.pallas_call(
        body, name=name, grid=(S // R,),
        out_shape=[_sds((S, D), F32)] + [_sds((1, D), F32)] * 3,
        in_specs=[pl.BlockSpec((N_DEV, R, FS), lambda i: (0, i, 0)), _resident((N_DEV, FS, D)),
                  row, row, vec, vec],
        out_specs=[row, vec, vec, vec],
        compiler_params=_cp(1, 56),
    )(dgu, wt8, h, dh, sc, gp)


def _ffn_bwd(dh, y, dsilu_u, silu, w4, wt8, h, gate, gpost, sc, gpre, name):
    S = dh.shape[0]
    R = min(256, S)

    def body(dh_ref, y_ref, g_ref, u_ref, w4_ref, w_ref, h_ref, gate_ref, gpost_ref, sc_ref, gpre_ref,
             dy_ref, dgu_ref, out_ref, dgate_ref, dgpost_ref, dsh_ref, dsc_ref, dgpre_ref):
        @pl.when(pl.program_id(0) == 0)
        def _():
            for r in (dgate_ref, dgpost_ref, dsh_ref, dsc_ref, dgpre_ref):
                r[...] = jnp.zeros_like(r)
        dhh = dh_ref[...]
        dy, dgate, dgpost = _postnorm_bwd(dhh, y_ref[...], gate_ref[...], gpost_ref[...], 0.5)
        dgate_ref[...] += dgate
        dgpost_ref[...] += dgpost
        dyb = dy.astype(BF)
        dy_ref[...] = dyb
        dn = None
        for s in range(4):
            da = _dot_nt(dyb, w4_ref[s])
            dg = (da * g_ref[s].astype(F32)).astype(BF)
            du = (da * u_ref[s].astype(F32)).astype(BF)
            dgu_ref[s] = dg
            dgu_ref[s + 4] = du
            part = _dot(dg, w_ref[s]) + _dot(du, w_ref[s + 4])
            dn = part if dn is None else dn + part
        dx, dsh, dsc, dgpre = _prenorm_bwd(dn, h_ref[...], gpre_ref[...], sc_ref[...])
        out_ref[...] = dhh + dx
        dsh_ref[...] += dsh
        dsc_ref[...] += dsc
        dgpre_ref[...] += dgpre

    vec = _const((1, D))
    row = pl.BlockSpec((R, D), lambda i: (i, 0))
    blk4 = pl.BlockSpec((4, R, FS), lambda i: (0, i, 0))
    return pl.pallas_call(
        body, name=name, grid=(S // R,),
        out_shape=[_sds((S, D), BF), _sds((8, S, FS), BF), _sds((S, D), F32)] + [_sds((1, D), F32)] * 5,
        in_specs=[row, row, blk4, blk4, _resident((4, FS, D)), _resident((N_DEV, FS, D)), row, vec, vec, vec, vec],
        out_specs=[row, pl.BlockSpec((8, R, FS), lambda i: (0, i, 0)), row] + [vec] * 5,
        compiler_params=_cp(1, 56),
    )(dh, y, dsilu_u, silu, w4, wt8, h, gate, gpost, sc, gpre)


def _tn_matmul(a, b, name):
    a3 = a if a.ndim == 3 else a[None]
    b3 = b if b.ndim == 3 else b[None]
    GA, S, M = a3.shape
    GB, _, N = b3.shape
    ts = min(2048, S)
    nk = S // ts
    chunks = [(m0, min(CHUNK, M - m0)) for m0 in range(0, M, CHUNK)]

    def body(a_ref, b_ref, o_ref, acc):
        k = pl.program_id(2)

        @pl.when(k == 0)
        def _():
            acc[...] = jnp.zeros_like(acc)

        for m0, mc in chunks:
            acc[m0:m0 + mc, :] += _dot_tn(a_ref[:, m0:m0 + mc], b_ref[...])

        @pl.when(k == nk - 1)
        def _():
            for m0, mc in chunks:
                o_ref[m0:m0 + mc, :] = acc[m0:m0 + mc, :].astype(BF)

    return pl.pallas_call(
        body, name=name, grid=(GA, GB, nk),
        out_shape=_sds((GA, GB, M, N), BF),
        in_specs=[pl.BlockSpec((None, ts, M), lambda ga, gb, k: (ga, k, 0)),
                  pl.BlockSpec((None, ts, N), lambda ga, gb, k: (gb, k, 0))],
        out_specs=pl.BlockSpec((None, None, M, N), lambda ga, gb, k: (ga, gb, 0, 0)),
        scratch_shapes=[pltpu.VMEM((M, N), F32)],
        compiler_params=_cp(3, 56),
    )(a3, b3)


def _mix_in(h, sh, sc, gp, w):
    S = h.shape[0]
    R = min(512, S)

    def body(h_ref, sh_ref, sc_ref, gp_ref, w_ref, n_ref, qkv_ref, zg_ref, gates_ref):
        for r0 in range(0, R, CHUNK):
            rows = slice(r0, r0 + CHUNK)
            nb = _prenorm(h_ref[rows, :], gp_ref[...], sc_ref[...], sh_ref[...]).astype(BF)
            n_ref[rows, :] = nb
            qkv_ref[rows, :] = _dot_nt(nb, w_ref[0:ZG_OFF, :]).astype(BF)
            zg_ref[rows, :] = _dot_nt(nb, w_ref[ZG_OFF:GATE_OFF, :]).astype(BF)
            gates_ref[rows, :] = jax.nn.sigmoid(_dot_nt(nb, w_ref[GATE_OFF:IN_W, :])).astype(BF)

    vec = _const((1, D))
    rows = lambda w_: pl.BlockSpec((R, w_), lambda i: (i, 0))
    return pl.pallas_call(
        body, name="mix_in", grid=(S // R,),
        out_shape=[_sds((S, D), BF), _sds((S, QKV_W), BF), _sds((S, 2 * G_W), BF), _sds((S, 2 * D), BF)],
        in_specs=[rows(D), vec, vec, vec, _resident((IN_W, D))],
        out_specs=[rows(D), rows(QKV_W), rows(2 * G_W), rows(2 * D)],
        compiler_params=_cp(1, 48),
    )(h, sh, sc, gp, w)


def _bias_table(rel_bias, bucket):
    def body(rel_ref, bk_ref, out_ref):
        bk = bk_ref[...]
        qi = lax.broadcasted_iota(jnp.int32, (BLK, 2 * BLK), 0)
        kj = lax.broadcasted_iota(jnp.int32, (BLK, 2 * BLK), 1)
        dist = qi + BLK - kj
        window = (dist >= 0) & (dist < BLK)
        for h in range(N_HEADS):
            acc = jnp.zeros((BLK, 2 * BLK), F32)
            for b in range(N_BUCKETS):
                acc = jnp.where(bk == b, rel_ref[b, h], acc)
            out_ref[h // GROUP, pl.ds((h % GROUP) * BLK, BLK), :] = jnp.where(window, acc, NEG)

    return pl.pallas_call(
        body, name="bias_table",
        out_shape=_sds((N_KV, GROUP * BLK, 2 * BLK), F32),
        in_specs=[pl.BlockSpec(memory_space=pltpu.SMEM), pl.BlockSpec(memory_space=pltpu.VMEM)],
        out_specs=pl.BlockSpec(memory_space=pltpu.VMEM),
    )(rel_bias, bucket)


ATT_TB = 4


def _attn_scores(q, kvc, kvp, bias_ref, sink_ref, has_prev, kh):
    k2 = jnp.concatenate([kvp[:, kh * HD:(kh + 1) * HD], kvc[:, kh * HD:(kh + 1) * HD]], axis=0)
    v2 = jnp.concatenate([kvp[:, KV_W + kh * HD:KV_W + (kh + 1) * HD],
                          kvc[:, KV_W + kh * HD:KV_W + (kh + 1) * HD]], axis=0)
    q4 = jnp.concatenate([q[:, (kh * GROUP + g) * HD:(kh * GROUP + g + 1) * HD] for g in range(GROUP)], axis=0)
    s = _dot_nt(q4, k2) * SCALE + bias_ref[kh]
    if has_prev is not None:
        col = lax.broadcasted_iota(jnp.int32, (GROUP * BLK, 2 * BLK), 1)
        s = jnp.where((col >= BLK) | has_prev, s, NEG)
    rowg = lax.broadcasted_iota(jnp.int32, (GROUP * BLK, 1), 0) // BLK
    sink = jnp.zeros((GROUP * BLK, 1), F32)
    for g in range(GROUP):
        sink = jnp.where(rowg == g, sink_ref[kh * GROUP + g], sink)
    return q4, k2, v2, s, sink


def _attn_fwd(qkv, bias, sinks):
    S = qkv.shape[0]
    tb = min(ATT_TB, S // BLK)
    T = tb * BLK

    def body(sink_ref, q_ref, kv_ref, kvp_ref, bias_ref, o_ref):
        step = pl.program_id(0)
        for j in range(tb):
            rows = slice(j * BLK, (j + 1) * BLK)
            q, kvc = q_ref[rows, :], kv_ref[rows, :]
            kvp = kvp_ref[...] if j == 0 else kv_ref[(j - 1) * BLK:j * BLK, :]
            has_prev = (step > 0) if j == 0 else None
            outs = []
            for kh in range(N_KV):
                q4, k2, v2, s, sink = _attn_scores(q, kvc, kvp, bias_ref, sink_ref, has_prev, kh)
                m = jnp.maximum(jnp.max(s, axis=1, keepdims=True), sink)
                p = jnp.exp(s - m)
                denom = jnp.sum(p, axis=1, keepdims=True) + jnp.exp(sink - m)
                o4 = _dot((p / denom).astype(BF), v2)
                outs += [o4[g * BLK:(g + 1) * BLK] for g in range(GROUP)]
            o_ref[rows, :] = jnp.concatenate(outs, axis=1).astype(BF)

    return pl.pallas_call(
        body, name="attn_fwd", grid=(S // T,),
        out_shape=_sds((S, Q_W), BF),
        in_specs=[pl.BlockSpec(memory_space=pltpu.SMEM),
                  pl.BlockSpec((T, Q_W), lambda i: (i, 0)),
                  pl.BlockSpec((T, 2 * KV_W), lambda i: (i, 2)),
                  pl.BlockSpec((BLK, 2 * KV_W), lambda i: (jnp.maximum(i * tb - 1, 0), 2)),
                  _const((N_KV, GROUP * BLK, 2 * BLK))],
        out_specs=pl.BlockSpec((T, Q_W), lambda i: (i, 0)),
        compiler_params=_cp(1, 32),
    )(sinks, qkv, qkv, qkv, bias)


def _attn_bwd(qkv, bias, sinks, do):
    S = qkv.shape[0]
    tb = 1
    T = tb * BLK
    nt = S // T

    def body(sink_ref, q_ref, kv_ref, kvp_ref, bias_ref, do_ref, dq_ref, dkv_ref, dbias_ref, dsink_ref, carry):
        i = pl.program_id(0)

        @pl.when(i == 0)
        def _():
            carry[...] = jnp.zeros_like(carry)
            dbias_ref[...] = jnp.zeros_like(dbias_ref)
            dsink_ref[...] = jnp.zeros_like(dsink_ref)

        from_next = carry[...]
        for j in reversed(range(tb)):
            rows = slice(j * BLK, (j + 1) * BLK)
            q, kvc, do_ = q_ref[rows, :], kv_ref[rows, :], do_ref[rows, :]
            kvp = kvp_ref[...] if j == 0 else kv_ref[(j - 1) * BLK:j * BLK, :]
            has_prev = (i < nt - 1) if j == 0 else None
            dqs, dk_cur, dv_cur, dk_prev, dv_prev = [], [], [], [], []
            for kh in range(N_KV):
                q4, k2, v2, s, sink = _attn_scores(q, kvc, kvp, bias_ref, sink_ref, has_prev, kh)
                m = jnp.maximum(jnp.max(s, axis=1, keepdims=True), sink)
                p = jnp.exp(s - m)
                denom = jnp.sum(p, axis=1, keepdims=True) + jnp.exp(sink - m)
                prob = p / denom
                p_sink = jnp.exp(sink - m) / denom
                pb = prob.astype(BF)
                do4 = jnp.concatenate(
                    [do_[:, (kh * GROUP + g) * HD:(kh * GROUP + g + 1) * HD] for g in range(GROUP)], axis=0)
                dp = _dot_nt(do4, v2)
                o4 = _dot(pb, v2)
                delta = jnp.sum(do4.astype(F32) * o4, axis=1, keepdims=True)
                ds = prob * (dp - delta)
                dbias_ref[kh] += ds
                sink_term = p_sink * delta
                for g in range(GROUP):
                    h = kh * GROUP + g
                    val = -jnp.sum(sink_term[g * BLK:(g + 1) * BLK], axis=0, keepdims=True)
                    dsink_ref[pl.ds(h, 1), :] += jnp.broadcast_to(val, (1, 128))
                dsb = ds.astype(BF)
                dq4 = _dot(dsb, k2) * SCALE
                dk2 = jnp.transpose(_dot_tn(q4, dsb)) * SCALE
                dv2 = jnp.transpose(_dot_tn(do4, pb))
                dqs += [dq4[g * BLK:(g + 1) * BLK] for g in range(GROUP)]
                dk_prev.append(dk2[0:BLK])
                dk_cur.append(dk2[BLK:2 * BLK])
                dv_prev.append(dv2[0:BLK])
                dv_cur.append(dv2[BLK:2 * BLK])
            dq_ref[rows, :] = jnp.concatenate(dqs, axis=1).astype(BF)
            dkv_ref[rows, :] = (jnp.concatenate(dk_cur + dv_cur, axis=1) + from_next).astype(BF)
            from_next = jnp.concatenate(dk_prev + dv_prev, axis=1)
        carry[...] = from_next

    return pl.pallas_call(
        body, name="attn_bwd", grid=(nt,),
        out_shape=[_sds((S, Q_W), BF), _sds((S, 2 * KV_W), BF),
                   _sds((N_KV, GROUP * BLK, 2 * BLK), F32), _sds((N_HEADS, 128), F32)],
        in_specs=[pl.BlockSpec(memory_space=pltpu.SMEM),
                  pl.BlockSpec((T, Q_W), lambda i: (nt - 1 - i, 0)),
                  pl.BlockSpec((T, 2 * KV_W), lambda i: (nt - 1 - i, 2)),
                  pl.BlockSpec((BLK, 2 * KV_W), lambda i: (jnp.maximum((nt - 1 - i) * tb - 1, 0), 2)),
                  _const((N_KV, GROUP * BLK, 2 * BLK)),
                  pl.BlockSpec((T, Q_W), lambda i: (nt - 1 - i, 0))],
        out_specs=[pl.BlockSpec((T, Q_W), lambda i: (nt - 1 - i, 0)),
                   pl.BlockSpec((T, 2 * KV_W), lambda i: (nt - 1 - i, 0)),
                   _const((N_KV, GROUP * BLK, 2 * BLK)), _const((N_HEADS, 128))],
        scratch_shapes=[pltpu.VMEM((BLK, 2 * KV_W), F32)],
        compiler_params=_cp(1, 32),
    )(sinks, qkv, qkv, qkv, bias, do)


def _rel_bias_grad(dbias, bucket):
    def body(db_ref, bk_ref, out_ref):
        bk = bk_ref[...]
        lane = lax.broadcasted_iota(jnp.int32, (1, 128), 1)
        for h in range(N_HEADS):
            d = db_ref[h // GROUP, pl.ds((h % GROUP) * BLK, BLK), :]
            row = jnp.zeros((1, 128), F32)
            for b in range(N_BUCKETS):
                tot = jnp.sum(jnp.sum(jnp.where(bk == b, d, 0.0), axis=1, keepdims=True), axis=0, keepdims=True)
                row = jnp.where(lane == b, tot, row)
            out_ref[pl.ds(h, 1), :] = row

    vm = pl.BlockSpec(memory_space=pltpu.VMEM)
    return pl.pallas_call(body, name="rel_bias_grad", out_shape=_sds((N_HEADS, 128), F32),
                          in_specs=[vm, vm], out_specs=vm)(dbias, bucket)


def _gmlp_parts(zg, lg_ref, lb_ref):
    z = zg.astype(F32)
    ge = _gelu(z)
    u, vg = ge[:, 0:G_W], ge[:, G_W:2 * G_W]
    mu = jnp.mean(vg, axis=-1, keepdims=True)
    xc = vg - mu
    rstd = lax.rsqrt(jnp.mean(xc * xc, axis=-1, keepdims=True) + EPS)
    xh = xc * rstd
    return z, u, xh, rstd, xh * lg_ref[...] + lb_ref[...]


def _causal_weights(ws_ref, wc):
    t = lax.broadcasted_iota(jnp.int32, (BLK, BLK), 0)
    s = lax.broadcasted_iota(jnp.int32, (BLK, BLK), 1)
    for g in range(N_HEADS):
        wc[g] = jnp.where(s <= t, ws_ref[g], 0.0).astype(BF)


def _spatial(vb, wc, bst_ref, p, low):
    xp = vb[:, p * 128:(p + 1) * 128]
    s0 = _dot(wc[2 * p], xp) + bst_ref[:, 2 * p:2 * p + 1]
    s1 = _dot(wc[2 * p + 1], xp) + bst_ref[:, 2 * p + 1:2 * p + 2]
    return xp, jnp.where(low, s0, s1)


def _gmlp_fwd(zg, lg, lb, ws, bst):
    S = zg.shape[0]
    tb = min(ATT_TB, S // BLK)
    T = tb * BLK

    def body(zg_ref, lg_ref, lb_ref, ws_ref, bst_ref, o_ref, wc):
        @pl.when(pl.program_id(0) == 0)
        def _():
            _causal_weights(ws_ref, wc)
        low = lax.broadcasted_iota(jnp.int32, (BLK, 128), 1) < HD
        for j in range(tb):
            rows = slice(j * BLK, (j + 1) * BLK)
            _, u, _, _, vln = _gmlp_parts(zg_ref[rows, :], lg_ref, lb_ref)
            vb = vln.astype(BF)
            for p in range(4):
                _, sp = _spatial(vb, wc, bst_ref, p, low)
                o_ref[rows, p * 128:(p + 1) * 128] = (u[:, p * 128:(p + 1) * 128] * sp).astype(BF)

    return pl.pallas_call(
        body, name="gmlp_fwd", grid=(S // T,),
        out_shape=_sds((S, G_W), BF),
        in_specs=[pl.BlockSpec((T, 2 * G_W), lambda i: (i, 0)), _const((1, G_W)), _const((1, G_W)),
                  _const((N_HEADS, BLK, BLK)), _const((BLK, N_HEADS))],
        out_specs=pl.BlockSpec((T, G_W), lambda i: (i, 0)),
        scratch_shapes=[pltpu.VMEM((N_HEADS, BLK, BLK), BF)],
        compiler_params=_cp(1, 32),
    )(zg, lg, lb, ws, bst)


def _gmlp_bwd(zg, d_out, lg, lb, ws, bst):
    S = zg.shape[0]
    tb = min(ATT_TB, S // BLK)
    T = tb * BLK
    nb = S // T

    def body(zg_ref, d_ref, lg_ref, lb_ref, ws_ref, bst_ref, dzg_ref, dws_ref, dbs_ref, dlg_ref, dlb_ref, wc, dbacc):
        i = pl.program_id(0)

        @pl.when(i == 0)
        def _():
            _causal_weights(ws_ref, wc)
            dws_ref[...] = jnp.zeros_like(dws_ref)
            dlg_ref[...] = jnp.zeros_like(dlg_ref)
            dlb_ref[...] = jnp.zeros_like(dlb_ref)
            dbacc[...] = jnp.zeros_like(dbacc)

        low = lax.broadcasted_iota(jnp.int32, (BLK, 128), 1) < HD
        for j in range(tb):
            rows = slice(j * BLK, (j + 1) * BLK)
            z, u, xh, rstd, vln = _gmlp_parts(zg_ref[rows, :], lg_ref, lb_ref)
            vb = vln.astype(BF)
            d = d_ref[rows, :].astype(F32)
            du_parts, dvln_parts = [], []
            for p in range(4):
                xp, sp = _spatial(vb, wc, bst_ref, p, low)
                dp = d[:, p * 128:(p + 1) * 128]
                du_parts.append(dp * sp)
                dsp = dp * u[:, p * 128:(p + 1) * 128]
                dbacc[:, p * 128:(p + 1) * 128] += dsp
                d0 = jnp.where(low, dsp, 0.0).astype(BF)
                d1 = jnp.where(low, 0.0, dsp).astype(BF)
                dws_ref[2 * p] += _dot_nt(d0, xp)
                dws_ref[2 * p + 1] += _dot_nt(d1, xp)
                dvln_parts.append(_dot_tn(wc[2 * p], d0) + _dot_tn(wc[2 * p + 1], d1))
            dvln = jnp.concatenate(dvln_parts, axis=1)
            dlg_ref[...] += _colsum(dvln * xh)
            dlb_ref[...] += _colsum(dvln)
            dxh = dvln * lg_ref[...]
            dvg = rstd * (dxh - jnp.mean(dxh, axis=-1, keepdims=True)
                          - xh * jnp.mean(dxh * xh, axis=-1, keepdims=True))
            dge = jnp.concatenate(du_parts + [dvg], axis=1)
            dzg_ref[rows, :] = (dge * _gelu_grad(z)).astype(BF)

        @pl.when(i == nb - 1)
        def _():
            t = lax.broadcasted_iota(jnp.int32, (BLK, BLK), 0)
            s = lax.broadcasted_iota(jnp.int32, (BLK, BLK), 1)
            for g in range(N_HEADS):
                dws_ref[g] = jnp.where(s <= t, dws_ref[g], 0.0)
            grp = lax.broadcasted_iota(jnp.int32, (N_HEADS, G_W), 0)
            lane = lax.broadcasted_iota(jnp.int32, (N_HEADS, G_W), 1) // HD
            pick = jnp.where(grp == lane, 1.0, 0.0).astype(F32)
            dbs_ref[...] = lax.dot_general(pick, dbacc[...], (((1,), (1,)), ((), ())),
                                           preferred_element_type=F32, precision=HIGH)

    return pl.pallas_call(
        body, name="gmlp_bwd", grid=(nb,),
        out_shape=[_sds((S, 2 * G_W), BF), _sds((N_HEADS, BLK, BLK), F32), _sds((N_HEADS, BLK), F32),
                   _sds((1, G_W), F32), _sds((1, G_W), F32)],
        in_specs=[pl.BlockSpec((T, 2 * G_W), lambda i: (i, 0)), pl.BlockSpec((T, G_W), lambda i: (i, 0)),
                  _const((1, G_W)), _const((1, G_W)), _const((N_HEADS, BLK, BLK)), _const((BLK, N_HEADS))],
        out_specs=[pl.BlockSpec((T, 2 * G_W), lambda i: (i, 0)), _const((N_HEADS, BLK, BLK)),
                   _const((N_HEADS, BLK)), _const((1, G_W)), _const((1, G_W))],
        scratch_shapes=[pltpu.VMEM((N_HEADS, BLK, BLK), BF), pltpu.VMEM((BLK, G_W), F32)],
        compiler_params=_cp(1, 32),
    )(zg, d_out, lg, lb, ws, bst)


def _mix_out(o, gm, gates, h, wa, wg, wo, gate, gp):
    S = h.shape[0]
    R = min(512, S)

    def body(o_ref, gm_ref, gates_ref, h_ref, wa_ref, wg_ref, wo_ref, gate_ref, gp_ref,
             ya_ref, yg_ref, ym_ref, y_ref, hn_ref):
        for r0 in range(0, R, CHUNK):
            rows = slice(r0, r0 + CHUNK)
            ya = _dot(o_ref[rows, :], wa_ref[...])
            yg = _dot(gm_ref[rows, :], wg_ref[...])
            ya_ref[rows, :] = ya.astype(BF)
            yg_ref[rows, :] = yg.astype(BF)
            ym = (gates_ref[rows, 0:D].astype(F32) * ya + gates_ref[rows, D:2 * D].astype(F32) * yg).astype(BF)
            ym_ref[rows, :] = ym
            y = _dot(ym, wo_ref[...])
            y_ref[rows, :] = y
            hn_ref[rows, :] = h_ref[rows, :] + gate_ref[...] * (y * _rms_r(y) * gp_ref[...])

    vec = _const((1, D))
    rows = lambda w_: pl.BlockSpec((R, w_), lambda i: (i, 0))
    return pl.pallas_call(
        body, name="mix_out", grid=(S // R,),
        out_shape=[_sds((S, D), BF)] * 3 + [_sds((S, D), F32)] * 2,
        in_specs=[rows(Q_W), rows(G_W), rows(2 * D), rows(D), _resident((Q_W, D)), _resident((G_W, D)),
                  _resident((D, D)), vec, vec],
        out_specs=[rows(D)] * 5,
        compiler_params=_cp(1, 48),
    )(o, gm, gates, h, wa, wg, wo, gate, gp)


def _mix_out_bwd(dh, y, ya, yg, gates, att, gm, ymix, wa, wg, wo, gate, gp):
    S = dh.shape[0]
    R = min(512, S)
    nb = S // R

    def body(dh_ref, y_ref, ya_ref, yg_ref, gates_ref, att_ref, gm_ref, ym_ref, wa_ref, wg_ref, wo_ref,
             gate_ref, gp_ref, dz_ref, do_ref, dgm_ref, dgate_ref, dgp_ref, gwo_ref, gwa_ref, gwg_ref,
             acc_o, acc_a, acc_g, dy_scr, dya_scr, dyg_scr):
        i = pl.program_id(0)

        @pl.when(i == 0)
        def _():
            for r in (dgate_ref, dgp_ref, acc_o, acc_a, acc_g):
                r[...] = jnp.zeros_like(r)
        for r0 in range(0, R, CHUNK):
            rows = slice(r0, r0 + CHUNK)
            dy, dgate, dgp = _postnorm_bwd(dh_ref[rows, :], y_ref[rows, :], gate_ref[...], gp_ref[...], 1.0)
            dgate_ref[...] += dgate
            dgp_ref[...] += dgp
            dyb = dy.astype(BF)
            dy_scr[rows, :] = dyb
            dym = _dot_nt(dyb, wo_ref[...])
            ga = gates_ref[rows, 0:D].astype(F32)
            gg = gates_ref[rows, D:2 * D].astype(F32)
            dya = (dym * ga).astype(BF)
            dyg = (dym * gg).astype(BF)
            dya_scr[rows, :] = dya
            dyg_scr[rows, :] = dyg
            dz_ref[rows, 0:D] = (dym * ya_ref[rows, :].astype(F32) * (ga * (1.0 - ga))).astype(BF)
            dz_ref[rows, D:2 * D] = (dym * yg_ref[rows, :].astype(F32) * (gg * (1.0 - gg))).astype(BF)
            do_ref[rows, :] = _dot_nt(dya, wa_ref[...]).astype(BF)
            dgm_ref[rows, :] = _dot_nt(dyg, wg_ref[...]).astype(BF)
        for m0 in range(0, D, CHUNK):
            acc_o[m0:m0 + CHUNK, :] += _dot_tn(ym_ref[:, m0:m0 + CHUNK], dy_scr[...])
        for m0 in range(0, Q_W, CHUNK):
            acc_a[m0:m0 + CHUNK, :] += _dot_tn(att_ref[:, m0:m0 + CHUNK], dya_scr[...])
            acc_g[m0:m0 + CHUNK, :] += _dot_tn(gm_ref[:, m0:m0 + CHUNK], dyg_scr[...])

        @pl.when(i == nb - 1)
        def _():
            for m0 in range(0, D, CHUNK):
                gwo_ref[m0:m0 + CHUNK, :] = acc_o[m0:m0 + CHUNK, :].astype(BF)
            for m0 in range(0, Q_W, CHUNK):
                gwa_ref[m0:m0 + CHUNK, :] = acc_a[m0:m0 + CHUNK, :].astype(BF)
                gwg_ref[m0:m0 + CHUNK, :] = acc_g[m0:m0 + CHUNK, :].astype(BF)

    vec = _const((1, D))
    rows = lambda w_: pl.BlockSpec((R, w_), lambda i: (i, 0))
    return pl.pallas_call(
        body, name="mix_out_bwd", grid=(nb,),
        out_shape=[_sds((S, 2 * D), BF), _sds((S, Q_W), BF), _sds((S, G_W), BF), _sds((1, D), F32),
                   _sds((1, D), F32), _sds((D, D), BF), _sds((Q_W, D), BF), _sds((G_W, D), BF)],
        in_specs=[rows(D), rows(D), rows(D), rows(D), rows(2 * D), rows(Q_W), rows(G_W), rows(D),
                  _resident((Q_W, D)), _resident((G_W, D)), _resident((D, D)), vec, vec],
        out_specs=[rows(2 * D), rows(Q_W), rows(G_W), vec, vec, _const((D, D)), _const((Q_W, D)),
                   _const((G_W, D))],
        scratch_shapes=[pltpu.VMEM((D, D), F32), pltpu.VMEM((Q_W, D), F32), pltpu.VMEM((G_W, D), F32)]
        + [pltpu.VMEM((R, D), BF)] * 3,
        compiler_params=_cp(1, 60),
    )(dh, y, ya, yg, gates, att, gm, ymix, wa, wg, wo, gate, gp)


def _mix_dn(dq, dkv, dzg, dzgate, w, h, dh, sc, gp):
    S = h.shape[0]
    R = min(512, S)

    def body(dq_ref, dkv_ref, dzg_ref, dzt_ref, w_ref, h_ref, dh_ref, sc_ref, gp_ref,
             out_ref, dsh_ref, dsc_ref, dgp_ref):
        @pl.when(pl.program_id(0) == 0)
        def _():
            dsh_ref[...] = jnp.zeros_like(dsh_ref)
            dsc_ref[...] = jnp.zeros_like(dsc_ref)
            dgp_ref[...] = jnp.zeros_like(dgp_ref)
        for r0 in range(0, R, CHUNK):
            rows = slice(r0, r0 + CHUNK)
            dn = _dot(dq_ref[rows, :], w_ref[0:Q_W, :])
            dn = dn + _dot(dkv_ref[rows, :], w_ref[Q_W:QKV_W, :])
            dn = dn + _dot(dzg_ref[rows, :], w_ref[ZG_OFF:GATE_OFF, :])
            dn = dn + _dot(dzt_ref[rows, :], w_ref[GATE_OFF:IN_W, :])
            dx, dsh, dsc, dgp = _prenorm_bwd(dn, h_ref[rows, :], gp_ref[...], sc_ref[...])
            out_ref[rows, :] = dh_ref[rows, :] + dx
            dsh_ref[...] += dsh
            dsc_ref[...] += dsc
            dgp_ref[...] += dgp

    vec = _const((1, D))
    rows = lambda w_: pl.BlockSpec((R, w_), lambda i: (i, 0))
    return pl.pallas_call(
        body, name="mix_dn", grid=(S // R,),
        out_shape=[_sds((S, D), F32)] + [_sds((1, D), F32)] * 3,
        in_specs=[rows(Q_W), rows(2 * KV_W), rows(2 * G_W), rows(2 * D), _resident((IN_W, D)),
                  rows(D), rows(D), vec, vec],
        out_specs=[rows(D), vec, vec, vec],
        compiler_params=_cp(1, 48),
    )(dq, dkv, dzg, dzgate, w, h, dh, sc, gp)


def _adamw_math(w, g, m, v):
    m2 = ADAM_B1 * m + (1.0 - ADAM_B1) * g
    v2 = ADAM_B2 * v + (1.0 - ADAM_B2) * (g * g)
    m_hat = m2 / (1.0 - ADAM_B1 ** ADAM_STEP)
    v_hat = v2 / (1.0 - ADAM_B2 ** ADAM_STEP)
    delta = -ADAM_LR * (m_hat / (jnp.sqrt(v_hat) + ADAM_EPS) + ADAM_WD * w)
    return delta, m2, v2


def _row_tile(rows, cols):
    best = None
    for t in range(16, rows + 1, 16):
        if rows % t == 0 and t * cols <= 256 * 1024:
            best = t
    return best if best is not None else rows


def _adamw_sharded(landing, w, m, v, name):
    r, c = w.shape
    tr = _row_tile(r, c)

    def body(l_ref, w_ref, m_ref, v_ref, g_ref, d_ref, m2_ref, v2_ref):
        g = l_ref[0].astype(F32)
        for j in range(1, N_DEV):
            g = g + l_ref[j].astype(F32)
        delta, m2, v2 = _adamw_math(w_ref[...], g, m_ref[...], v_ref[...])
        g_ref[...] = g
        d_ref[...] = delta
        m2_ref[...] = m2
        v2_ref[...] = v2

    row = pl.BlockSpec((tr, c), lambda i: (i, 0))
    return pl.pallas_call(
        body, name=name, grid=(r // tr,),
        out_shape=[_sds((r, c), F32)] * 4,
        in_specs=[pl.BlockSpec((N_DEV, tr, c), lambda i: (0, i, 0)), row, row, row],
        out_specs=[row] * 4,
        compiler_params=_cp(1, 48),
    )(landing, w, m, v)


def _adamw_small(w, g, m, v, name):
    def body(w_ref, g_ref, m_ref, v_ref, d_ref, m2_ref, v2_ref):
        delta, m2, v2 = _adamw_math(w_ref[...], g_ref[...], m_ref[...], v_ref[...])
        d_ref[...] = delta
        m2_ref[...] = m2
        v2_ref[...] = v2

    vm = pl.BlockSpec(memory_space=pltpu.VMEM)
    return pl.pallas_call(body, name=name, out_shape=[_sds(w.shape, F32)] * 3,
                          in_specs=[vm] * 4, out_specs=[vm] * 3)(w, g, m, v)


def _w_ada_update(c8, d_ada, w, m, v):
    tr = 256

    def body(c_ref, d_ref, w_ref, m_ref, v_ref, g_ref, dl_ref, m2_ref, v2_ref):
        cs = c_ref[...]
        cs = cs * jax.nn.sigmoid(cs)
        g = lax.dot_general(cs, d_ref[...], (((0,), (0,)), ((), ())), preferred_element_type=F32, precision=HIGH)
        delta, m2, v2 = _adamw_math(w_ref[...], g, m_ref[...], v_ref[...])
        g_ref[...] = g
        dl_ref[...] = delta
        m2_ref[...] = m2
        v2_ref[...] = v2

    row = pl.BlockSpec((tr, ADA_W), lambda i: (i, 0))
    return pl.pallas_call(
        body, name="w_ada_update", grid=(D // tr,),
        out_shape=[_sds((D, ADA_W), F32)] * 4,
        in_specs=[pl.BlockSpec((N_DEV, tr), lambda i: (0, i)), _const((N_DEV, ADA_W)), row, row, row],
        out_specs=[row] * 4,
        compiler_params=_cp(1, 40),
    )(c8, d_ada, w, m, v)


def _t5_bucket():
    qi = jnp.arange(BLK, dtype=jnp.int32)[:, None]
    kj = jnp.arange(2 * BLK, dtype=jnp.int32)[None, :]
    dist = jnp.maximum(qi + BLK - kj, 0)
    max_exact = N_BUCKETS // 2
    d_f = jnp.maximum(dist, max_exact).astype(F32)
    large = max_exact + (jnp.log(d_f / max_exact) / math.log(MAX_DISTANCE / max_exact)
                         * (N_BUCKETS - max_exact)).astype(jnp.int32)
    large = jnp.minimum(large, N_BUCKETS - 1)
    return jnp.where(dist < max_exact, dist, large)


def _slabs_of_columns(w):
    r, c8 = w.shape
    return jnp.transpose(w.reshape(r, N_DEV, c8 // N_DEV), (1, 0, 2))


def _columns_of_slabs(w8):
    _, r, c = w8.shape
    return jnp.transpose(w8, (1, 0, 2)).reshape(r, N_DEV * c)


def kernel(x, c, rel_bias, w_ada, b_ada, pre_norm_g, post_norm_g, w_ffn1_in, w_ffn1_out, w_in, sinks, gmlp_ln_g, gmlp_ln_b, gmlp_w_s, gmlp_b_s, w_br_attn, w_br_gmlp, w_out, w_ffn2_in, w_ffn2_out, loss_target, m_rel_bias, m_w_ada, m_b_ada, m_pre_norm_g, m_post_norm_g, m_w_ffn1_in, m_w_ffn1_out, m_w_in, m_sinks, m_gmlp_ln_g, m_gmlp_ln_b, m_gmlp_w_s, m_gmlp_b_s, m_w_br_attn, m_w_br_gmlp, m_w_out, m_w_ffn2_in, m_w_ffn2_out, v_rel_bias, v_w_ada, v_b_ada, v_pre_norm_g, v_post_norm_g, v_w_ffn1_in, v_w_ffn1_out, v_w_in, v_sinks, v_gmlp_ln_g, v_gmlp_ln_b, v_gmlp_w_s, v_gmlp_b_s, v_w_br_attn, v_w_br_gmlp, v_w_out, v_w_ffn2_in, v_w_ffn2_out):
    me = 4 * lax.axis_index("x") + 2 * lax.axis_index("y") + lax.axis_index("c")
    x0 = x[0]
    target = loss_target[0]

    transposed = ("w_ffn1_in", "w_in", "w_ffn2_in")
    shards = [w_ffn1_in[0].T, w_ffn1_out[0], w_in[0].T, w_br_attn[0], w_br_gmlp[0], w_out[0],
              w_ffn2_in[0].T, w_ffn2_out[0]]
    shards_bf = [s.astype(BF) for s in shards]
    groups = [shards_bf[0:1], shards_bf[1:6], shards_bf[6:8]]

    def gather_start(i, after):
        return _slabs_start("gather", groups[i], after, "gather_start_%d" % i)

    def forward_start(st, i, after):
        lands = _slabs_wait("gather", len(groups[i]), st, after, "gather_wait_%d" % i)
        return _slabs_start("forward", lands, c, "forward_start_%d" % i)

    def gathered(st, i, after):
        return _slabs_wait("forward", len(groups[i]), st, after, "forward_wait_%d" % i)

    gs0 = gather_start(0, c)

    small = jnp.concatenate([c[0], pre_norm_g[0].reshape(-1), post_norm_g[0].reshape(-1)])
    small8 = jnp.broadcast_to(small[None, :], (8, small.shape[0]))
    b_ada64 = jnp.repeat(b_ada.reshape(N_DEV, ADA_W), 8, axis=0)
    gath, ada64 = _ada_forward(small8, w_ada[0], b_ada64)
    gath8 = gath[::8]
    ada = ada64[::8].reshape(9, D)
    sh1, sc1, g1, sh2, sc2, g2, sh3, sc3, g3 = [ada[k:k + 1] for k in range(9)]
    gains = gath8[:, D:].reshape(N_DEV, 2, 3, 128)
    pre_g = jnp.transpose(gains[:, 0], (1, 0, 2)).reshape(3, D)
    post_g = jnp.transpose(gains[:, 1], (1, 0, 2)).reshape(3, D)
    pre = [pre_g[k:k + 1] for k in range(3)]
    post = [post_g[k:k + 1] for k in range(3)]

    bucket = _t5_bucket()
    bias = _bias_table(rel_bias, bucket)
    sinks8 = sinks[0]
    lg, lb = gmlp_ln_g, gmlp_ln_b
    ws = gmlp_w_s[0]
    bst = jnp.transpose(gmlp_b_s[0])

    fs0 = forward_start(gs0, 0, sh1)
    gs1 = gather_start(1, fs0[-1])
    (wf1_in,) = gathered(fs0, 0, gs1[-1])
    n1, fg1, fu1, fa1 = _ffn_in(x0, sh1, sc1, pre[0], wf1_in, "ffn1_in")
    fs1 = forward_start(gs1, 1, n1)
    gs2 = gather_start(2, fs1[-1])
    mix_w = gathered(fs1, 1, gs2[-1])
    wf1_out = mix_w[0].reshape(4, FS, D)
    w_in_full = mix_w[1].reshape(IN_W, D)
    w_bra = _columns_of_slabs(mix_w[2])
    w_brg = _columns_of_slabs(mix_w[3])
    w_out_full = mix_w[4].reshape(D, D)
    h1, y1 = _ffn_out(fa1, wf1_out, x0, g1, post[0], "ffn1_out")
    n2, qkv, zg, gates = _mix_in(h1, sh2, sc2, pre[1], w_in_full)
    att = _attn_fwd(qkv, bias, sinks8)
    gm = _gmlp_fwd(zg, lg, lb, ws, bst)
    fs2 = forward_start(gs2, 2, gm)
    ya, yg, ymix, y2, h2 = _mix_out(att, gm, gates, h1, w_bra, w_brg, w_out_full, g2 + fs2[-1], post[1])
    wf2_in, wf2_out = gathered(fs2, 2, h2)
    wf2_out = wf2_out.reshape(4, FS, D)
    n3, fg3, fu3, fa3, y3, dh3, sq = _ffn_fwd_loss(h2, sh3, sc3, pre[2], wf2_in, wf2_out, g3, post[2], target,
                                                   "ffn2_fwd_loss")
    loss = lax.psum(0.5 * sq[0, 0] / D, ("x", "y", "c"))

    def exchange_start(i, arrays):
        return _slabs_start("exchange", arrays, sq, "exchange_start_%d" % i)

    dy3, dgu3, dh2, d_g3, d_post2, d_sh3, d_sc3, d_pre2 = _ffn_bwd(
        dh3, y3, fg3, fu3, wf2_out, wf2_in, h2, g3, post[2], sc3, pre[2], "ffn2_bwd")
    gw_f2_out = _tn_matmul(fa3, dy3, "ffn2_out_wgrad").reshape(N_DEV, D_FF // N_DEV, D)
    ex0 = exchange_start(0, [gw_f2_out])
    gw_f2_in = _tn_matmul(dgu3, n3, "ffn2_in_wgrad").reshape(N_DEV, FS, D)
    ex1 = exchange_start(1, [gw_f2_in])

    dzgate, d_att, d_gm, d_g2, d_post1, gw_out, gw_bra, gw_brg = _mix_out_bwd(
        dh2, y2, ya, yg, gates, att, gm, ymix, w_bra, w_brg, w_out_full, g2 + ex0[-1] + ex1[-1], post[1])
    ex2 = exchange_start(2, [_slabs_of_columns(gw_bra), _slabs_of_columns(gw_brg),
                             gw_out.reshape(N_DEV, D // N_DEV, D)])
    dq, dkv, dbias, dsink = _attn_bwd(qkv, bias, sinks8, d_att)
    dzg, d_ws, d_bs, d_lg, d_lb = _gmlp_bwd(zg, d_gm, lg, lb, ws, bst)
    d_rel = _rel_bias_grad(dbias, bucket)
    early = jnp.concatenate([
        jnp.concatenate([d_lg.reshape(4, 128), d_lb.reshape(4, 128)], axis=0),
        d_bs, d_rel, dsink, d_ws.reshape(N_HEADS * BLK, BLK)], axis=0)
    sm0 = _slabs_start("gather_all", [early], sq, "small_gather_start")
    dh1, d_sh2, d_sc2, d_pre1 = _mix_dn(dq, dkv, dzg, dzgate, w_in_full, h1, dh2, sc2 + ex2[-1] + sm0[-1], pre[1])
    gw_in = jnp.concatenate(
        [_tn_matmul(dq, n2, "w_in_q_wgrad").reshape(Q_W, D),
         _tn_matmul(dkv, n2, "w_in_kv_wgrad").reshape(2 * KV_W, D),
         _tn_matmul(dzg, n2, "w_in_zg_wgrad").reshape(2 * G_W, D),
         _tn_matmul(dzgate, n2, "w_in_gate_wgrad").reshape(2 * D, D)], axis=0).reshape(N_DEV, IN_W // N_DEV, D)
    ex3 = exchange_start(3, [gw_in])

    dy1, dgu1, d_g1, d_post0 = _ffn_out_bwd(dh1, y1, fg1, fu1, wf1_out, g1 + ex3[-1], post[0], "ffn1_out_bwd")
    gw_f1_out = _tn_matmul(fa1, dy1, "ffn1_out_wgrad").reshape(N_DEV, D_FF // N_DEV, D)
    gw_f1_in = _tn_matmul(dgu1, n1, "ffn1_in_wgrad").reshape(N_DEV, FS, D)
    ex4 = exchange_start(4, [gw_f1_out, gw_f1_in])
    grad_x, d_sh1, d_sc1, d_pre0 = _ffn_dn(dgu1, wf1_in, x0, dh1, sc1 + ex4[-1], pre[0], "ffn1_dn")

    landed = {}
    for i, (ex, nms) in enumerate([(ex0, ["w_ffn2_out"]), (ex1, ["w_ffn2_in"]),
                                   (ex2, ["w_br_attn", "w_br_gmlp", "w_out"]), (ex3, ["w_in"]),
                                   (ex4, ["w_ffn1_out", "w_ffn1_in"])]):
        for nm, land in zip(nms, _slabs_wait("exchange", len(nms), ex, grad_x, "exchange_wait_%d" % i)):
            landed[nm] = land
    moments = [(m_w_ffn1_in, v_w_ffn1_in), (m_w_ffn1_out, v_w_ffn1_out), (m_w_in, v_w_in),
               (m_w_br_attn, v_w_br_attn), (m_w_br_gmlp, v_w_br_gmlp), (m_w_out, v_w_out),
               (m_w_ffn2_in, v_w_ffn2_in), (m_w_ffn2_out, v_w_ffn2_out)]
    names = ["w_ffn1_in", "w_ffn1_out", "w_in", "w_br_attn", "w_br_gmlp", "w_out", "w_ffn2_in", "w_ffn2_out"]
    big = {}
    for nm, w_, (m_, v_) in zip(names, shards, moments):
        if nm in transposed:
            res4 = _adamw_sharded(landed[nm], w_, m_[0].T, v_[0].T, "adamw_" + nm)
            big[nm] = [a.T[None] for a in res4]
        else:
            big[nm] = [a[None] for a in _adamw_sharded(landed[nm], w_, m_[0], v_[0], "adamw_" + nm)]

    d_ada = jnp.concatenate([v_.reshape(8, 128) for v_ in
                             (d_sh1, d_sc1, d_g1, d_sh2, d_sc2, d_g2, d_sh3, d_sc3, d_g3)], axis=0)
    d_pre = jnp.concatenate([d_pre0, d_pre1, d_pre2], axis=0)
    d_post = jnp.concatenate([d_post0, d_post1, d_post2], axis=0)
    late = jnp.concatenate([d_ada, _slabs_of_columns(d_pre).reshape(24, 128),
                            _slabs_of_columns(d_post).reshape(24, 128)], axis=0)
    tot, every = _small_allreduce(late)
    (early_land,) = _slabs_wait("gather_all", 1, sm0, grad_x, "small_gather_wait")
    tot_early = _sum_slabs(early_land)

    g_b_ada = tot[0:72].reshape(1, 9 * D)
    g_pre = lax.dynamic_slice_in_dim(tot[72:96], 3 * me, 3, axis=0)[None]
    g_post = lax.dynamic_slice_in_dim(tot[96:120], 3 * me, 3, axis=0)[None]
    g_lg = tot_early[0:4].reshape(1, G_W)
    g_lb = tot_early[4:8].reshape(1, G_W)
    g_bs = tot_early[8:16][None]
    g_rel = jnp.transpose(tot_early[16:24, 0:N_BUCKETS])
    g_sinks = tot_early[24:32, 0][None]
    g_ws = tot_early[32:1056].reshape(1, N_HEADS, BLK, BLK)

    d_ada_mine = lax.dynamic_slice_in_dim(every[:, 0:72].reshape(N_DEV, N_DEV, ADA_W), me, 1, axis=1)[:, 0]
    ada_out = [a[None] for a in _w_ada_update(gath8[:, 0:D], d_ada_mine, w_ada[0], m_w_ada[0], v_w_ada[0])]

    def small_step(w_, g_, m_, v_, nm):
        shp = w_.shape
        two_d = (int(math.prod(shp[:-1])), shp[-1])
        d_, m2_, v2_ = _adamw_small(w_.reshape(two_d), g_.reshape(two_d), m_.reshape(two_d), v_.reshape(two_d),
                                    "adamw_" + nm)
        return [g_, d_.reshape(shp), m2_.reshape(shp), v2_.reshape(shp)]

    res = {
        "rel_bias": small_step(rel_bias, g_rel, m_rel_bias, v_rel_bias, "rel_bias"),
        "w_ada": ada_out,
        "b_ada": small_step(b_ada, g_b_ada, m_b_ada, v_b_ada, "b_ada"),
        "pre_norm_g": small_step(pre_norm_g, g_pre, m_pre_norm_g, v_pre_norm_g, "pre_norm_g"),
        "post_norm_g": small_step(post_norm_g, g_post, m_post_norm_g, v_post_norm_g, "post_norm_g"),
        "sinks": small_step(sinks, g_sinks, m_sinks, v_sinks, "sinks"),
        "gmlp_ln_g": small_step(gmlp_ln_g, g_lg, m_gmlp_ln_g, v_gmlp_ln_g, "gmlp_ln_g"),
        "gmlp_ln_b": small_step(gmlp_ln_b, g_lb, m_gmlp_ln_b, v_gmlp_ln_b, "gmlp_ln_b"),
        "gmlp_w_s": small_step(gmlp_w_s, g_ws, m_gmlp_w_s, v_gmlp_w_s, "gmlp_w_s"),
        "gmlp_b_s": small_step(gmlp_b_s, g_bs, m_gmlp_b_s, v_gmlp_b_s, "gmlp_b_s"),
    }
    res.update(big)
    order = ["rel_bias", "w_ada", "b_ada", "pre_norm_g", "post_norm_g", "w_ffn1_in", "w_ffn1_out", "w_in", "sinks",
             "gmlp_ln_g", "gmlp_ln_b", "gmlp_w_s", "gmlp_b_s", "w_br_attn", "w_br_gmlp", "w_out", "w_ffn2_in",
             "w_ffn2_out"]
    outs = [loss, grad_x[None]]
    for k in range(4):
        outs += [res[nm][k] for nm in order]
    return tuple(outs)
```

```python
import functools
import math

import jax
import jax.numpy as jnp
from jax import lax
from jax.experimental import pallas as pl
from jax.experimental.pallas import tpu as pltpu

F32 = jnp.float32
BF = jnp.bfloat16

N_DEV = 8
D = 1024
D_FF = 2816
FS = D_FF // 4
N_HEADS = 8
N_KV = 2
GROUP = 4
HD = 64
BLK = 128
Q_W = 512
KV_W = 128
G_W = 512
QKV_W = Q_W + 2 * KV_W
ZG_OFF = QKV_W
GATE_OFF = ZG_OFF + 2 * G_W
IN_W = GATE_OFF + 2 * D
N_BUCKETS = 32
MAX_DISTANCE = 128
EPS = 1e-6
NEG = -1e30
SCALE = HD ** -0.5
ADA_W = 9 * D // N_DEV

ADAM_LR = 0.001
ADAM_B1 = 0.9
ADAM_B2 = 0.999
ADAM_EPS = 1e-08
ADAM_WD = 0.01
ADAM_STEP = 10

CHUNK = 256
MIB = 1024 * 1024
MESH = pl.DeviceIdType.MESH
HIGH = lax.Precision.HIGHEST


def _cp(n_grid, vmem_mib):
    return pltpu.CompilerParams(dimension_semantics=("arbitrary",) * n_grid,
                                vmem_limit_bytes=vmem_mib * MIB)


def _const(shape):
    return pl.BlockSpec(shape, lambda *_: (0,) * len(shape))


def _resident(shape):
    return pl.BlockSpec(shape, lambda *_: (0,) * len(shape), pipeline_mode=pl.Buffered(1))


def _sds(shape, dtype):
    return jax.ShapeDtypeStruct(shape, dtype)


def _dot(a, b):
    return jnp.dot(a, b, preferred_element_type=F32)


def _dot_nt(a, b):
    return lax.dot_general(a, b, (((1,), (1,)), ((), ())), preferred_element_type=F32)


def _dot_tn(a, b):
    return lax.dot_general(a, b, (((0,), (0,)), ((), ())), preferred_element_type=F32)


def _rms_r(x):
    return lax.rsqrt(jnp.mean(x * x, axis=-1, keepdims=True) + EPS)


def _colsum(x):
    return jnp.sum(x, axis=0, keepdims=True)


def _prenorm(x, gp, sc, sh):
    return (x * _rms_r(x) * gp) * (1.0 + sc) + sh


def _prenorm_bwd(dn, x, gp, sc):
    r = _rms_r(x)
    xh = x * r
    t = dn * (1.0 + sc) * gp
    dx = r * (t - xh * jnp.mean(t * xh, axis=-1, keepdims=True))
    return dx, _colsum(dn), _colsum(dn * xh * gp), _colsum(dn * (1.0 + sc) * xh)


def _postnorm_bwd(dh, y, gate, gp, res):
    r = _rms_r(y)
    yh = y * r
    dyn = (res * gate) * dh
    t = dyn * gp
    dy = r * (t - yh * jnp.mean(t * yh, axis=-1, keepdims=True))
    return dy, _colsum(res * dh * yh * gp), _colsum(dyn * yh)


def _gelu(x):
    k = math.sqrt(2.0 / math.pi)
    return 0.5 * x * (1.0 + jnp.tanh(k * (x + 0.044715 * x * x * x)))


def _gelu_grad(x):
    k = math.sqrt(2.0 / math.pi)
    t = jnp.tanh(k * (x + 0.044715 * x * x * x))
    return 0.5 * (1.0 + t) + 0.5 * x * (1.0 - t * t) * (k * (1.0 + 3.0 * 0.044715 * x * x))


def _my_place():
    x, y, c = lax.axis_index("x"), lax.axis_index("y"), lax.axis_index("c")
    return x, y, c, 4 * x + 2 * y + c


def _peer(x, y, c, k):
    px = 1 - x if k & 4 else x
    py = 1 - y if k & 2 else y
    pc = 1 - c if k & 1 else c
    return (px, py, pc), 4 * px + 2 * py + pc


HBM_SPEC = pl.BlockSpec(memory_space=pltpu.HBM)
SEM_SPEC = pl.BlockSpec(memory_space=pltpu.SEMAPHORE)
EFFECT = pltpu.SideEffectType.DATAFLOW_SIDE_EFFECTING


RELATIONS = {"exchange": (1, 2, 3, 4, 5, 6, 7), "gather": (1, 2, 4, 6), "forward": (2, 4, 6),
             "gather_all": (1, 2, 3, 4, 5, 6, 7)}


def _slab_copies(mode, srcs, lands, send, recv, loc):
    x, y, c, me = _my_place()
    rel = RELATIONS[mode]
    remote, local = [], []
    for t in range(len(lands)):
        for i, k in enumerate(rel):
            peer, peer_lin = _peer(x, y, c, k)
            if mode == "exchange":
                src, dst, to = srcs[t].at[peer_lin], lands[t].at[me], peer
            elif mode in ("gather", "gather_all"):
                src, dst, to = srcs[t], lands[t].at[me], peer
            else:
                src, dst, to = lands[t].at[peer_lin], lands[t].at[peer_lin], _peer(x, y, c, 1)[0]
            remote.append(pltpu.make_async_remote_copy(
                src_ref=src, dst_ref=dst, send_sem=send.at[t * len(rel) + i], recv_sem=recv.at[t * len(rel) + i],
                device_id=to, device_id_type=MESH))
        if mode == "exchange":
            local.append(pltpu.make_async_copy(srcs[t].at[me], lands[t].at[me], loc.at[t]))
        elif mode in ("gather", "gather_all"):
            local.append(pltpu.make_async_copy(srcs[t], lands[t].at[me], loc.at[t]))
    return remote, local


def _slabs_start(mode, arrays, after, name):
    n = len(arrays)
    if mode == "forward":
        thru = list(arrays)
    else:
        shapes = [a.shape if mode == "exchange" else (N_DEV,) + a.shape for a in arrays]
        thru = list(arrays) + [lax.empty(s, a.dtype) for s, a in zip(shapes, arrays)]
    m = len(thru)
    n_sem = n * len(RELATIONS[mode])

    def body(*refs):
        srcs, lands = refs[:n], refs[m - n:m]
        send, recv, loc = refs[m + 1:m + 4]
        remote, local = _slab_copies(mode, srcs, lands, send, recv, loc)
        for cp in remote + local:
            cp.start()
        refs[-1][...] = jnp.zeros_like(refs[-1])

    return pl.pallas_call(
        body, name=name,
        out_shape=(pltpu.SemaphoreType.DMA((n_sem,)), pltpu.SemaphoreType.DMA((n_sem,)),
                   pltpu.SemaphoreType.DMA((n,)),
                   *[pltpu.HBM(a.shape, a.dtype) for a in thru],
                   _sds((1, D), F32)),
        in_specs=[HBM_SPEC] * m + [pl.BlockSpec(memory_space=pl.ANY)],
        out_specs=(SEM_SPEC, SEM_SPEC, SEM_SPEC, *[HBM_SPEC] * m, pl.BlockSpec(memory_space=pltpu.VMEM)),
        input_output_aliases={t: 3 + t for t in range(m)},
        compiler_params=pltpu.CompilerParams(has_side_effects=EFFECT),
    )(*[pltpu.with_memory_space_constraint(a, pltpu.HBM) for a in thru], after)


def _slabs_wait(mode, n, started, after, name):
    sems = started[0:3]
    thru = started[3:-1]
    m = len(thru)

    def body(*refs):
        srcs, lands = refs[:n], refs[m - n:m]
        remote, local = _slab_copies(mode, srcs, lands, *refs[m:m + 3])
        for cp in remote:
            cp.wait_send()
            cp.wait_recv()
        for cp in local:
            cp.wait()

    res = pl.pallas_call(
        body, name=name,
        out_shape=tuple(pltpu.HBM(a.shape, a.dtype) for a in thru),
        in_specs=[HBM_SPEC] * m + [SEM_SPEC] * 3 + [pl.BlockSpec(memory_space=pl.ANY)],
        out_specs=tuple([HBM_SPEC] * m),
        input_output_aliases={t: t for t in range(m)},
        compiler_params=pltpu.CompilerParams(has_side_effects=EFFECT),
    )(*thru, *sems, after)
    return list(res[m - n:m])


def _ada_forward(small8, w_ada, b_ada64):
    sw = small8.shape[1]

    def body(sm_ref, w_ref, b_ref, gath_ref, ada_ref, part_ref, send1, recv1, send2, recv2):
        x, y, c, me = _my_place()
        row_me = pl.multiple_of(me * 8, 8)
        gath_ref[pl.ds(row_me, 8), :] = sm_ref[...]
        first = []
        for k in range(1, N_DEV):
            peer, _ = _peer(x, y, c, k)
            cp = pltpu.make_async_remote_copy(
                src_ref=sm_ref, dst_ref=gath_ref.at[pl.ds(row_me, 8), :], send_sem=send1.at[k - 1],
                recv_sem=recv1.at[k - 1], device_id=peer, device_id_type=MESH)
            cp.start()
            first.append(cp)
        for cp in first:
            cp.wait()
        cs = gath_ref[:, 0:D]
        cs = cs * jax.nn.sigmoid(cs)
        part_ref[...] = jnp.dot(cs, w_ref[...], preferred_element_type=F32, precision=HIGH)
        ada_ref[pl.ds(row_me, 8), :] = part_ref[pl.ds(row_me, 8), :]
        second = []
        for k in range(1, N_DEV):
            peer, peer_lin = _peer(x, y, c, k)
            cp = pltpu.make_async_remote_copy(
                src_ref=part_ref.at[pl.ds(pl.multiple_of(peer_lin * 8, 8), 8), :],
                dst_ref=ada_ref.at[pl.ds(row_me, 8), :], send_sem=send2.at[k - 1],
                recv_sem=recv2.at[k - 1], device_id=peer, device_id_type=MESH)
            cp.start()
            second.append(cp)
        for cp in second:
            cp.wait()
        ada_ref[...] = ada_ref[...] + b_ref[...]

    vm = pl.BlockSpec(memory_space=pltpu.VMEM)
    return pl.pallas_call(
        body, name="ada_forward",
        out_shape=[_sds((8 * N_DEV, sw), F32), _sds((8 * N_DEV, ADA_W), F32)],
        in_specs=[vm, vm, vm], out_specs=[vm, vm],
        scratch_shapes=[pltpu.VMEM((8 * N_DEV, ADA_W), F32)] + [pltpu.SemaphoreType.DMA((7,))] * 4,
        compiler_params=pltpu.CompilerParams(vmem_limit_bytes=32 * MIB),
    )(small8, w_ada, b_ada64)


def _sum_slabs(land):
    def body(l_ref, o_ref):
        acc = l_ref[0]
        for j in range(1, N_DEV):
            acc = acc + l_ref[j]
        o_ref[...] = acc

    vm = pl.BlockSpec(memory_space=pltpu.VMEM)
    return pl.pallas_call(body, name="sum_slabs", out_shape=_sds(land.shape[1:], F32), in_specs=[vm], out_specs=vm,
                          compiler_params=pltpu.CompilerParams(vmem_limit_bytes=32 * MIB))(land)


def _small_allreduce(pack):
    rows = pack.shape[0]

    def body(p_ref, sum_ref, gath_ref, send, recv):
        x, y, c, me = _my_place()
        gath_ref[me] = p_ref[...]
        cps = []
        for k in range(1, N_DEV):
            peer, _ = _peer(x, y, c, k)
            cp = pltpu.make_async_remote_copy(
                src_ref=p_ref, dst_ref=gath_ref.at[me], send_sem=send.at[k - 1],
                recv_sem=recv.at[k - 1], device_id=peer, device_id_type=MESH)
            cp.start()
            cps.append(cp)
        for cp in cps:
            cp.wait()
        acc = gath_ref[0]
        for j in range(1, N_DEV):
            acc = acc + gath_ref[j]
        sum_ref[...] = acc

    vm = pl.BlockSpec(memory_space=pltpu.VMEM)
    return pl.pallas_call(
        body, name="small_allreduce",
        out_shape=[_sds((rows, 128), F32), _sds((N_DEV, rows, 128), F32)],
        in_specs=[vm], out_specs=[vm, vm],
        scratch_shapes=[pltpu.SemaphoreType.DMA((7,)), pltpu.SemaphoreType.DMA((7,))],
        compiler_params=pltpu.CompilerParams(vmem_limit_bytes=40 * MIB),
    )(pack)


F_TILES = tuple((f0, min(512, D_FF - f0)) for f0 in range(0, D_FF, 512))


def _swiglu_tile(n, wt_ref, f0, tf):
    g = _dot_nt(n, wt_ref[f0:f0 + tf, :])
    u = _dot_nt(n, wt_ref[D_FF + f0:D_FF + f0 + tf, :])
    sg = jax.nn.sigmoid(g)
    silu = g * sg
    return (u * (sg * (1.0 + g * (1.0 - sg)))).astype(BF), silu.astype(BF), (silu * u).astype(BF)


def _ffn_in(h, sh, sc, gp, wt, name):
    S = h.shape[0]
    R = min(512, S)

    def body(h_ref, sh_ref, sc_ref, gp_ref, w_ref, n_ref, dg_ref, sl_ref, a_ref):
        for r0 in range(0, R, CHUNK):
            rows = slice(r0, r0 + CHUNK)
            n = _prenorm(h_ref[rows, :], gp_ref[...], sc_ref[...], sh_ref[...]).astype(BF)
            n_ref[rows, :] = n
            for f0, tf in F_TILES:
                dg_ref[rows, f0:f0 + tf], sl_ref[rows, f0:f0 + tf], a_ref[rows, f0:f0 + tf] = _swiglu_tile(
                    n, w_ref, f0, tf)

    vec = _const((1, D))
    rows_ = lambda w_: pl.BlockSpec((R, w_), lambda i: (i, 0))
    return pl.pallas_call(
        body, name=name, grid=(S // R,),
        out_shape=[_sds((S, D), BF)] + [_sds((S, D_FF), BF)] * 3,
        in_specs=[rows_(D), vec, vec, vec, _resident((2 * D_FF, D))],
        out_specs=[rows_(D), rows_(D_FF), rows_(D_FF), rows_(D_FF)],
        compiler_params=_cp(1, 56),
    )(h, sh, sc, gp, wt)


def _ffn_out(a, w, h, gate, gp, name):
    S = h.shape[0]
    R = min(512, S)

    def body(a_ref, w_ref, h_ref, gate_ref, gp_ref, hn_ref, y_ref):
        for r0 in range(0, R, CHUNK):
            rows = slice(r0, r0 + CHUNK)
            y = _dot(a_ref[rows, :], w_ref[...])
            y_ref[rows, :] = y
            hn_ref[rows, :] = h_ref[rows, :] + (0.5 * gate_ref[...]) * (y * _rms_r(y) * gp_ref[...])

    vec = _const((1, D))
    rows_ = lambda w_: pl.BlockSpec((R, w_), lambda i: (i, 0))
    return pl.pallas_call(
        body, name=name, grid=(S // R,),
        out_shape=[_sds((S, D), F32), _sds((S, D), F32)],
        in_specs=[rows_(D_FF), _resident((D_FF, D)), rows_(D), vec, vec],
        out_specs=[rows_(D), rows_(D)],
        compiler_params=_cp(1, 48),
    )(a, w, h, gate, gp)


def _ffn_fwd_loss(h, sh, sc, gpre, wt, w, gate, gpost, target, name):
    S = h.shape[0]
    R = min(256, S)

    def body(h_ref, sh_ref, sc_ref, gpre_ref, wt_ref, w_ref, gate_ref, gpost_ref, t_ref,
             n_ref, dg_ref, sl_ref, a_ref, y_ref, dh_ref, tot_ref):
        @pl.when(pl.program_id(0) == 0)
        def _():
            tot_ref[...] = jnp.zeros_like(tot_ref)
        hh = h_ref[...]
        n = _prenorm(hh, gpre_ref[...], sc_ref[...], sh_ref[...]).astype(BF)
        n_ref[...] = n
        y = None
        for f0, tf in F_TILES:
            dg_ref[:, f0:f0 + tf], sl_ref[:, f0:f0 + tf], a = _swiglu_tile(n, wt_ref, f0, tf)
            a_ref[:, f0:f0 + tf] = a
            part = _dot(a, w_ref[f0:f0 + tf, :])
            y = part if y is None else y + part
        y_ref[...] = y
        e = hh + (0.5 * gate_ref[...]) * (y * _rms_r(y) * gpost_ref[...]) - t_ref[...]
        dh_ref[...] = e * (1.0 / D)
        tot_ref[...] += jnp.sum(jnp.sum(e * e, axis=1, keepdims=True), axis=0, keepdims=True)

    vec = _const((1, D))
    rows_ = lambda w_: pl.BlockSpec((R, w_), lambda i: (i, 0))
    return pl.pallas_call(
        body, name=name, grid=(S // R,),
        out_shape=[_sds((S, D), BF)] + [_sds((S, D_FF), BF)] * 3 + [_sds((S, D), F32)] * 2 + [_sds((1, 1), F32)],
        in_specs=[rows_(D), vec, vec, vec, _resident((2 * D_FF, D)), _resident((D_FF, D)), vec, vec, rows_(D)],
        out_specs=[rows_(D), rows_(D_FF), rows_(D_FF), rows_(D_FF), rows_(D), rows_(D), _const((1, 1))],
        compiler_params=_cp(1, 56),
    )(h, sh, sc, gpre, wt, w, gate, gpost, target)


def _ffn_out_bwd(dh, y, dsilu_u, silu, w, gate, gp, name):
    S = dh.shape[0]
    R = min(512, S)

    def body(dh_ref, y_ref, g_ref, u_ref, w_ref, gate_ref, gp_ref, dy_ref, dgu_ref, dgate_ref, dgp_ref):
        @pl.when(pl.program_id(0) == 0)
        def _():
            dgate_ref[...] = jnp.zeros_like(dgate_ref)
            dgp_ref[...] = jnp.zeros_like(dgp_ref)
        for r0 in range(0, R, CHUNK):
            rows = slice(r0, r0 + CHUNK)
            dy, dgate, dgp = _postnorm_bwd(dh_ref[rows, :], y_ref[rows, :], gate_ref[...], gp_ref[...], 0.5)
            dgate_ref[...] += dgate
            dgp_ref[...] += dgp
            dyb = dy.astype(BF)
            dy_ref[rows, :] = dyb
            for f0, tf in F_TILES:
                da = _dot_nt(dyb, w_ref[f0:f0 + tf, :])
                dgu_ref[rows, f0:f0 + tf] = (da * g_ref[rows, f0:f0 + tf].astype(F32)).astype(BF)
                dgu_ref[rows, D_FF + f0:D_FF + f0 + tf] = (da * u_ref[rows, f0:f0 + tf].astype(F32)).astype(BF)

    vec = _const((1, D))
    rows_ = lambda w_: pl.BlockSpec((R, w_), lambda i: (i, 0))
    return pl.pallas_call(
        body, name=name, grid=(S // R,),
        out_shape=[_sds((S, D), BF), _sds((S, 2 * D_FF), BF), _sds((1, D), F32), _sds((1, D), F32)],
        in_specs=[rows_(D), rows_(D), rows_(D_FF), rows_(D_FF), _resident((D_FF, D)), vec, vec],
        out_specs=[rows_(D), rows_(2 * D_FF), vec, vec],
        compiler_params=_cp(1, 56),
    )(dh, y, dsilu_u, silu, w, gate, gp)


def _ffn_dn(dgu, wt, h, dh, sc, gp, name):
    S = h.shape[0]
    R = min(512, S)

    def body(dgu_ref, w_ref, h_ref, dh_ref, sc_ref, gp_ref, out_ref, dsh_ref, dsc_ref, dgp_ref):
        @pl.when(pl.program_id(0) == 0)
        def _():
            dsh_ref[...] = jnp.zeros_like(dsh_ref)
            dsc_ref[...] = jnp.zeros_like(dsc_ref)
            dgp_ref[...] = jnp.zeros_like(dgp_ref)

        for r0 in range(0, R, CHUNK):
            rows = slice(r0, r0 + CHUNK)
            dn = _dot(dgu_ref[rows, :], w_ref[...])
            dx, dsh, dsc, dgp = _prenorm_bwd(dn, h_ref[rows, :], gp_ref[...], sc_ref[...])
            out_ref[rows, :] = dh_ref[rows, :] + dx
            dsh_ref[...] += dsh
            dsc_ref[...] += dsc
            dgp_ref[...] += dgp

    vec = _const((1, D))
    rows_ = lambda w_: pl.BlockSpec((R, w_), lambda i: (i, 0))
    return pl.pallas_call(
        body, name=name, grid=(S // R,),
        out_shape=[_sds((S, D), F32)] + [_sds((1, D), F32)] * 3,
        in_specs=[rows_(2 * D_FF), _resident((2 * D_FF, D)), rows_(D), rows_(D), vec, vec],
        out_specs=[rows_(D), vec, vec, vec],
        compiler_params=_cp(1, 56),
    )(dgu, wt, h, dh, sc, gp)


def _ffn_bwd(dh, y, dsilu_u, silu, w, wt, h, gate, gpost, sc, gpre, name):
    S = dh.shape[0]
    R = min(256, S)

    def body(dh_ref, y_ref, g_ref, u_ref, w_ref, wt_ref, h_ref, gate_ref, gpost_ref, sc_ref, gpre_ref,
             dy_ref, dgu_ref, out_ref, dgate_ref, dgpost_ref, dsh_ref, dsc_ref, dgpre_ref):
        @pl.when(pl.program_id(0) == 0)
        def _():
            for r in (dgate_ref, dgpost_ref, dsh_ref, dsc_ref, dgpre_ref):
                r[...] = jnp.zeros_like(r)
        dhh = dh_ref[...]
        dy, dgate, dgpost = _postnorm_bwd(dhh, y_ref[...], gate_ref[...], gpost_ref[...], 0.5)
        dgate_ref[...] += dgate
        dgpost_ref[...] += dgpost
        dyb = dy.astype(BF)
        dy_ref[...] = dyb
        dn = None
        for f0, tf in F_TILES:
            da = _dot_nt(dyb, w_ref[f0:f0 + tf, :])
            dg = (da * g_ref[:, f0:f0 + tf].astype(F32)).astype(BF)
            du = (da * u_ref[:, f0:f0 + tf].astype(F32)).astype(BF)
            dgu_ref[:, f0:f0 + tf] = dg
            dgu_ref[:, D_FF + f0:D_FF + f0 + tf] = du
            part = _dot(dg, wt_ref[f0:f0 + tf, :]) + _dot(du, wt_ref[D_FF + f0:D_FF + f0 + tf, :])
            dn = part if dn is None else dn + part
        dx, dsh, dsc, dgpre = _prenorm_bwd(dn, h_ref[...], gpre_ref[...], sc_ref[...])
        out_ref[...] = dhh + dx
        dsh_ref[...] += dsh
        dsc_ref[...] += dsc
        dgpre_ref[...] += dgpre

    vec = _const((1, D))
    rows_ = lambda w_: pl.BlockSpec((R, w_), lambda i: (i, 0))
    return pl.pallas_call(
        body, name=name, grid=(S // R,),
        out_shape=[_sds((S, D), BF), _sds((S, 2 * D_FF), BF), _sds((S, D), F32)] + [_sds((1, D), F32)] * 5,
        in_specs=[rows_(D), rows_(D), rows_(D_FF), rows_(D_FF), _resident((D_FF, D)), _resident((2 * D_FF, D)),
                  rows_(D), vec, vec, vec, vec],
        out_specs=[rows_(D), rows_(2 * D_FF), rows_(D)] + [vec] * 5,
        compiler_params=_cp(1, 56),
    )(dh, y, dsilu_u, silu, w, wt, h, gate, gpost, sc, gpre)


def _tn_matmul(a, b, name, tm=None):
    S, M_all = a.shape
    N = b.shape[1]
    M = M_all if tm is None else tm
    GA = M_all // M
    ts = min(2048 if M * N <= 2 * D * D else 1024, S)
    nk = S // ts
    chunks = [(m0, min(CHUNK, M - m0)) for m0 in range(0, M, CHUNK)]

    def body(a_ref, b_ref, o_ref, acc):
        k = pl.program_id(1)

        @pl.when(k == 0)
        def _():
            acc[...] = jnp.zeros_like(acc)

        for m0, mc in chunks:
            acc[m0:m0 + mc, :] += _dot_tn(a_ref[:, m0:m0 + mc], b_ref[...])

        @pl.when(k == nk - 1)
        def _():
            for m0, mc in chunks:
                o_ref[m0:m0 + mc, :] = acc[m0:m0 + mc, :].astype(BF)

    return pl.pallas_call(
        body, name=name, grid=(GA, nk),
        out_shape=_sds((M_all, N), BF),
        in_specs=[pl.BlockSpec((ts, M), lambda ga, k: (k, ga)), pl.BlockSpec((ts, N), lambda ga, k: (k, 0))],
        out_specs=pl.BlockSpec((M, N), lambda ga, k: (ga, 0)),
        scratch_shapes=[pltpu.VMEM((M, N), F32)],
        compiler_params=_cp(2, 56),
    )(a, b)


def _mix_in(h, sh, sc, gp, w):
    S = h.shape[0]
    R = min(512, S)

    def body(h_ref, sh_ref, sc_ref, gp_ref, w_ref, n_ref, qkv_ref, zg_ref, gates_ref):
        for r0 in range(0, R, CHUNK):
            rows = slice(r0, r0 + CHUNK)
            nb = _prenorm(h_ref[rows, :], gp_ref[...], sc_ref[...], sh_ref[...]).astype(BF)
            n_ref[rows, :] = nb
            qkv_ref[rows, :] = _dot_nt(nb, w_ref[0:ZG_OFF, :]).astype(BF)
            zg_ref[rows, :] = _dot_nt(nb, w_ref[ZG_OFF:GATE_OFF, :]).astype(BF)
            gates_ref[rows, :] = jax.nn.sigmoid(_dot_nt(nb, w_ref[GATE_OFF:IN_W, :])).astype(BF)

    vec = _const((1, D))
    rows = lambda w_: pl.BlockSpec((R, w_), lambda i: (i, 0))
    return pl.pallas_call(
        body, name="mix_in", grid=(S // R,),
        out_shape=[_sds((S, D), BF), _sds((S, QKV_W), BF), _sds((S, 2 * G_W), BF), _sds((S, 2 * D), BF)],
        in_specs=[rows(D), vec, vec, vec, _resident((IN_W, D))],
        out_specs=[rows(D), rows(QKV_W), rows(2 * G_W), rows(2 * D)],
        compiler_params=_cp(1, 48),
    )(h, sh, sc, gp, w)


def _bias_table(rel_bias, bucket):
    def body(rel_ref, bk_ref, out_ref):
        bk = bk_ref[...]
        qi = lax.broadcasted_iota(jnp.int32, (BLK, 2 * BLK), 0)
        kj = lax.broadcasted_iota(jnp.int32, (BLK, 2 * BLK), 1)
        dist = qi + BLK - kj
        window = (dist >= 0) & (dist < BLK)
        for h in range(N_HEADS):
            acc = jnp.zeros((BLK, 2 * BLK), F32)
            for b in range(N_BUCKETS):
                acc = jnp.where(bk == b, rel_ref[b, h], acc)
            out_ref[h // GROUP, pl.ds((h % GROUP) * BLK, BLK), :] = jnp.where(window, acc, NEG)

    return pl.pallas_call(
        body, name="bias_table",
        out_shape=_sds((N_KV, GROUP * BLK, 2 * BLK), F32),
        in_specs=[pl.BlockSpec(memory_space=pltpu.SMEM), pl.BlockSpec(memory_space=pltpu.VMEM)],
        out_specs=pl.BlockSpec(memory_space=pltpu.VMEM),
    )(rel_bias, bucket)


ATT_TB = 4


def _attn_scores(q, kvc, kvp, bias_ref, sink_ref, has_prev, kh):
    k2 = jnp.concatenate([kvp[:, kh * HD:(kh + 1) * HD], kvc[:, kh * HD:(kh + 1) * HD]], axis=0)
    v2 = jnp.concatenate([kvp[:, KV_W + kh * HD:KV_W + (kh + 1) * HD],
                          kvc[:, KV_W + kh * HD:KV_W + (kh + 1) * HD]], axis=0)
    q4 = jnp.concatenate([q[:, (kh * GROUP + g) * HD:(kh * GROUP + g + 1) * HD] for g in range(GROUP)], axis=0)
    s = _dot_nt(q4, k2) * SCALE + bias_ref[kh]
    if has_prev is not None:
        col = lax.broadcasted_iota(jnp.int32, (GROUP * BLK, 2 * BLK), 1)
        s = jnp.where((col >= BLK) | has_prev, s, NEG)
    rowg = lax.broadcasted_iota(jnp.int32, (GROUP * BLK, 1), 0) // BLK
    sink = jnp.zeros((GROUP * BLK, 1), F32)
    for g in range(GROUP):
        sink = jnp.where(rowg == g, sink_ref[kh * GROUP + g], sink)
    return q4, k2, v2, s, sink


def _attn_fwd(qkv, bias, sinks):
    S = qkv.shape[0]
    tb = min(ATT_TB, S // BLK)
    T = tb * BLK

    def body(sink_ref, q_ref, kv_ref, kvp_ref, bias_ref, o_ref):
        step = pl.program_id(0)
        for j in range(tb):
            rows = slice(j * BLK, (j + 1) * BLK)
            q, kvc = q_ref[rows, :], kv_ref[rows, :]
            kvp = kvp_ref[...] if j == 0 else kv_ref[(j - 1) * BLK:j * BLK, :]
            has_prev = (step > 0) if j == 0 else None
            outs = []
            for kh in range(N_KV):
                q4, k2, v2, s, sink = _attn_scores(q, kvc, kvp, bias_ref, sink_ref, has_prev, kh)
                m = jnp.maximum(jnp.max(s, axis=1, keepdims=True), sink)
                p = jnp.exp(s - m)
                denom = jnp.sum(p, axis=1, keepdims=True) + jnp.exp(sink - m)
                o4 = _dot((p / denom).astype(BF), v2)
                outs += [o4[g * BLK:(g + 1) * BLK] for g in range(GROUP)]
            o_ref[rows, :] = jnp.concatenate(outs, axis=1).astype(BF)

    return pl.pallas_call(
        body, name="attn_fwd", grid=(S // T,),
        out_shape=_sds((S, Q_W), BF),
        in_specs=[pl.BlockSpec(memory_space=pltpu.SMEM),
                  pl.BlockSpec((T, Q_W), lambda i: (i, 0)),
                  pl.BlockSpec((T, 2 * KV_W), lambda i: (i, 2)),
                  pl.BlockSpec((BLK, 2 * KV_W), lambda i: (jnp.maximum(i * tb - 1, 0), 2)),
                  _const((N_KV, GROUP * BLK, 2 * BLK))],
        out_specs=pl.BlockSpec((T, Q_W), lambda i: (i, 0)),
        compiler_params=_cp(1, 32),
    )(sinks, qkv, qkv, qkv, bias)


def _attn_bwd(qkv, bias, sinks, do):
    S = qkv.shape[0]
    tb = 1
    T = tb * BLK
    nt = S // T

    def body(sink_ref, q_ref, kv_ref, kvp_ref, bias_ref, do_ref, dq_ref, dkv_ref, dbias_ref, dsink_ref, carry):
        i = pl.program_id(0)

        @pl.when(i == 0)
        def _():
            carry[...] = jnp.zeros_like(carry)
            dbias_ref[...] = jnp.zeros_like(dbias_ref)
            dsink_ref[...] = jnp.zeros_like(dsink_ref)

        from_next = carry[...]
        for j in reversed(range(tb)):
            rows = slice(j * BLK, (j + 1) * BLK)
            q, kvc, do_ = q_ref[rows, :], kv_ref[rows, :], do_ref[rows, :]
            kvp = kvp_ref[...] if j == 0 else kv_ref[(j - 1) * BLK:j * BLK, :]
            has_prev = (i < nt - 1) if j == 0 else None
            dqs, dk_cur, dv_cur, dk_prev, dv_prev = [], [], [], [], []
            for kh in range(N_KV):
                q4, k2, v2, s, sink = _attn_scores(q, kvc, kvp, bias_ref, sink_ref, has_prev, kh)
                m = jnp.maximum(jnp.max(s, axis=1, keepdims=True), sink)
                p = jnp.exp(s - m)
                denom = jnp.sum(p, axis=1, keepdims=True) + jnp.exp(sink - m)
                prob = p / denom
                p_sink = jnp.exp(sink - m) / denom
                pb = prob.astype(BF)
                do4 = jnp.concatenate(
                    [do_[:, (kh * GROUP + g) * HD:(kh * GROUP + g + 1) * HD] for g in range(GROUP)], axis=0)
                dp = _dot_nt(do4, v2)
                o4 = _dot(pb, v2)
                delta = jnp.sum(do4.astype(F32) * o4, axis=1, keepdims=True)
                ds = prob * (dp - delta)
                dbias_ref[kh] += ds
                sink_term = p_sink * delta
                for g in range(GROUP):
                    h = kh * GROUP + g
                    val = -jnp.sum(sink_term[g * BLK:(g + 1) * BLK], axis=0, keepdims=True)
                    dsink_ref[pl.ds(h, 1), :] += jnp.broadcast_to(val, (1, 128))
                dsb = ds.astype(BF)
                dq4 = _dot(dsb, k2) * SCALE
                dk2 = jnp.transpose(_dot_tn(q4, dsb)) * SCALE
                dv2 = jnp.transpose(_dot_tn(do4, pb))
                dqs += [dq4[g * BLK:(g + 1) * BLK] for g in range(GROUP)]
                dk_prev.append(dk2[0:BLK])
                dk_cur.append(dk2[BLK:2 * BLK])
                dv_prev.append(dv2[0:BLK])
                dv_cur.append(dv2[BLK:2 * BLK])
            dq_ref[rows, :] = jnp.concatenate(dqs, axis=1).astype(BF)
            dkv_ref[rows, :] = (jnp.concatenate(dk_cur + dv_cur, axis=1) + from_next).astype(BF)
            from_next = jnp.concatenate(dk_prev + dv_prev, axis=1)
        carry[...] = from_next

    return pl.pallas_call(
        body, name="attn_bwd", grid=(nt,),
        out_shape=[_sds((S, Q_W), BF), _sds((S, 2 * KV_W), BF),
                   _sds((N_KV, GROUP * BLK, 2 * BLK), F32), _sds((N_HEADS, 128), F32)],
        in_specs=[pl.BlockSpec(memory_space=pltpu.SMEM),
                  pl.BlockSpec((T, Q_W), lambda i: (nt - 1 - i, 0)),
                  pl.BlockSpec((T, 2 * KV_W), lambda i: (nt - 1 - i, 2)),
                  pl.BlockSpec((BLK, 2 * KV_W), lambda i: (jnp.maximum((nt - 1 - i) * tb - 1, 0), 2)),
                  _const((N_KV, GROUP * BLK, 2 * BLK)),
                  pl.BlockSpec((T, Q_W), lambda i: (nt - 1 - i, 0))],
        out_specs=[pl.BlockSpec((T, Q_W), lambda i: (nt - 1 - i, 0)),
                   pl.BlockSpec((T, 2 * KV_W), lambda i: (nt - 1 - i, 0)),
                   _const((N_KV, GROUP * BLK, 2 * BLK)), _const((N_HEADS, 128))],
        scratch_shapes=[pltpu.VMEM((BLK, 2 * KV_W), F32)],
        compiler_params=_cp(1, 32),
    )(sinks, qkv, qkv, qkv, bias, do)


def _rel_bias_grad(dbias, bucket):
    def body(db_ref, bk_ref, out_ref):
        bk = bk_ref[...]
        lane = lax.broadcasted_iota(jnp.int32, (1, 128), 1)
        for h in range(N_HEADS):
            d = db_ref[h // GROUP, pl.ds((h % GROUP) * BLK, BLK), :]
            row = jnp.zeros((1, 128), F32)
            for b in range(N_BUCKETS):
                tot = jnp.sum(jnp.sum(jnp.where(bk == b, d, 0.0), axis=1, keepdims=True), axis=0, keepdims=True)
                row = jnp.where(lane == b, tot, row)
            out_ref[pl.ds(h, 1), :] = row

    vm = pl.BlockSpec(memory_space=pltpu.VMEM)
    return pl.pallas_call(body, name="rel_bias_grad", out_shape=_sds((N_HEADS, 128), F32),
                          in_specs=[vm, vm], out_specs=vm)(dbias, bucket)


def _gmlp_parts(zg, lg_ref, lb_ref):
    z = zg.astype(F32)
    ge = _gelu(z)
    u, vg = ge[:, 0:G_W], ge[:, G_W:2 * G_W]
    mu = jnp.mean(vg, axis=-1, keepdims=True)
    xc = vg - mu
    rstd = lax.rsqrt(jnp.mean(xc * xc, axis=-1, keepdims=True) + EPS)
    xh = xc * rstd
    return z, u, xh, rstd, xh * lg_ref[...] + lb_ref[...]


def _causal_weights(ws_ref, wc):
    t = lax.broadcasted_iota(jnp.int32, (BLK, BLK), 0)
    s = lax.broadcasted_iota(jnp.int32, (BLK, BLK), 1)
    for g in range(N_HEADS):
        wc[g] = jnp.where(s <= t, ws_ref[g], 0.0).astype(BF)


def _spatial(vb, wc, bst_ref, p, low):
    xp = vb[:, p * 128:(p + 1) * 128]
    s0 = _dot(wc[2 * p], xp) + bst_ref[:, 2 * p:2 * p + 1]
    s1 = _dot(wc[2 * p + 1], xp) + bst_ref[:, 2 * p + 1:2 * p + 2]
    return xp, jnp.where(low, s0, s1)


def _gmlp_fwd(zg, lg, lb, ws, bst):
    S = zg.shape[0]
    tb = min(ATT_TB, S // BLK)
    T = tb * BLK

    def body(zg_ref, lg_ref, lb_ref, ws_ref, bst_ref, o_ref, wc):
        @pl.when(pl.program_id(0) == 0)
        def _():
            _causal_weights(ws_ref, wc)
        low = lax.broadcasted_iota(jnp.int32, (BLK, 128), 1) < HD
        for j in range(tb):
            rows = slice(j * BLK, (j + 1) * BLK)
            _, u, _, _, vln = _gmlp_parts(zg_ref[rows, :], lg_ref, lb_ref)
            vb = vln.astype(BF)
            for p in range(4):
                _, sp = _spatial(vb, wc, bst_ref, p, low)
                o_ref[rows, p * 128:(p + 1) * 128] = (u[:, p * 128:(p + 1) * 128] * sp).astype(BF)

    return pl.pallas_call(
        body, name="gmlp_fwd", grid=(S // T,),
        out_shape=_sds((S, G_W), BF),
        in_specs=[pl.BlockSpec((T, 2 * G_W), lambda i: (i, 0)), _const((1, G_W)), _const((1, G_W)),
                  _const((N_HEADS, BLK, BLK)), _const((BLK, N_HEADS))],
        out_specs=pl.BlockSpec((T, G_W), lambda i: (i, 0)),
        scratch_shapes=[pltpu.VMEM((N_HEADS, BLK, BLK), BF)],
        compiler_params=_cp(1, 32),
    )(zg, lg, lb, ws, bst)


def _gmlp_bwd(zg, d_out, lg, lb, ws, bst):
    S = zg.shape[0]
    tb = min(ATT_TB, S // BLK)
    T = tb * BLK
    nb = S // T

    def body(zg_ref, d_ref, lg_ref, lb_ref, ws_ref, bst_ref, dzg_ref, dws_ref, dbs_ref, dlg_ref, dlb_ref, wc, dbacc):
        i = pl.program_id(0)

        @pl.when(i == 0)
        def _():
            _causal_weights(ws_ref, wc)
            dws_ref[...] = jnp.zeros_like(dws_ref)
            dlg_ref[...] = jnp.zeros_like(dlg_ref)
            dlb_ref[...] = jnp.zeros_like(dlb_ref)
            dbacc[...] = jnp.zeros_like(dbacc)

        low = lax.broadcasted_iota(jnp.int32, (BLK, 128), 1) < HD
        for j in range(tb):
            rows = slice(j * BLK, (j + 1) * BLK)
            z, u, xh, rstd, vln = _gmlp_parts(zg_ref[rows, :], lg_ref, lb_ref)
            vb = vln.astype(BF)
            d = d_ref[rows, :].astype(F32)
            du_parts, dvln_parts = [], []
            for p in range(4):
                xp, sp = _spatial(vb, wc, bst_ref, p, low)
                dp = d[:, p * 128:(p + 1) * 128]
                du_parts.append(dp * sp)
                dsp = dp * u[:, p * 128:(p + 1) * 128]
                dbacc[:, p * 128:(p + 1) * 128] += dsp
                d0 = jnp.where(low, dsp, 0.0).astype(BF)
                d1 = jnp.where(low, 0.0, dsp).astype(BF)
                dws_ref[2 * p] += _dot_nt(d0, xp)
                dws_ref[2 * p + 1] += _dot_nt(d1, xp)
                dvln_parts.append(_dot_tn(wc[2 * p], d0) + _dot_tn(wc[2 * p + 1], d1))
            dvln = jnp.concatenate(dvln_parts, axis=1)
            dlg_ref[...] += _colsum(dvln * xh)
            dlb_ref[...] += _colsum(dvln)
            dxh = dvln * lg_ref[...]
            dvg = rstd * (dxh - jnp.mean(dxh, axis=-1, keepdims=True)
                          - xh * jnp.mean(dxh * xh, axis=-1, keepdims=True))
            dge = jnp.concatenate(du_parts + [dvg], axis=1)
            dzg_ref[rows, :] = (dge * _gelu_grad(z)).astype(BF)

        @pl.when(i == nb - 1)
        def _():
            t = lax.broadcasted_iota(jnp.int32, (BLK, BLK), 0)
            s = lax.broadcasted_iota(jnp.int32, (BLK, BLK), 1)
            for g in range(N_HEADS):
                dws_ref[g] = jnp.where(s <= t, dws_ref[g], 0.0)
            grp = lax.broadcasted_iota(jnp.int32, (N_HEADS, G_W), 0)
            lane = lax.broadcasted_iota(jnp.int32, (N_HEADS, G_W), 1) // HD
            pick = jnp.where(grp == lane, 1.0, 0.0).astype(F32)
            dbs_ref[...] = lax.dot_general(pick, dbacc[...], (((1,), (1,)), ((), ())),
                                           preferred_element_type=F32, precision=HIGH)

    return pl.pallas_call(
        body, name="gmlp_bwd", grid=(nb,),
        out_shape=[_sds((S, 2 * G_W), BF), _sds((N_HEADS, BLK, BLK), F32), _sds((N_HEADS, BLK), F32),
                   _sds((1, G_W), F32), _sds((1, G_W), F32)],
        in_specs=[pl.BlockSpec((T, 2 * G_W), lambda i: (i, 0)), pl.BlockSpec((T, G_W), lambda i: (i, 0)),
                  _const((1, G_W)), _const((1, G_W)), _const((N_HEADS, BLK, BLK)), _const((BLK, N_HEADS))],
        out_specs=[pl.BlockSpec((T, 2 * G_W), lambda i: (i, 0)), _const((N_HEADS, BLK, BLK)),
                   _const((N_HEADS, BLK)), _const((1, G_W)), _const((1, G_W))],
        scratch_shapes=[pltpu.VMEM((N_HEADS, BLK, BLK), BF), pltpu.VMEM((BLK, G_W), F32)],
        compiler_params=_cp(1, 32),
    )(zg, d_out, lg, lb, ws, bst)


def _mix_out(o, gm, gates, h, wa, wg, wo, gate, gp):
    S = h.shape[0]
    R = min(512, S)

    def body(o_ref, gm_ref, gates_ref, h_ref, wa_ref, wg_ref, wo_ref, gate_ref, gp_ref,
             ya_ref, yg_ref, ym_ref, y_ref, hn_ref):
        for r0 in range(0, R, CHUNK):
            rows = slice(r0, r0 + CHUNK)
            ya = _dot(o_ref[rows, :], wa_ref[...])
            yg = _dot(gm_ref[rows, :], wg_ref[...])
            ya_ref[rows, :] = ya.astype(BF)
            yg_ref[rows, :] = yg.astype(BF)
            ym = (gates_ref[rows, 0:D].astype(F32) * ya + gates_ref[rows, D:2 * D].astype(F32) * yg).astype(BF)
            ym_ref[rows, :] = ym
            y = _dot(ym, wo_ref[...])
            y_ref[rows, :] = y
            hn_ref[rows, :] = h_ref[rows, :] + gate_ref[...] * (y * _rms_r(y) * gp_ref[...])

    vec = _const((1, D))
    rows = lambda w_: pl.BlockSpec((R, w_), lambda i: (i, 0))
    return pl.pallas_call(
        body, name="mix_out", grid=(S // R,),
        out_shape=[_sds((S, D), BF)] * 3 + [_sds((S, D), F32)] * 2,
        in_specs=[rows(Q_W), rows(G_W), rows(2 * D), rows(D), _resident((Q_W, D)), _resident((G_W, D)),
                  _resident((D, D)), vec, vec],
        out_specs=[rows(D)] * 5,
        compiler_params=_cp(1, 48),
    )(o, gm, gates, h, wa, wg, wo, gate, gp)


def _mix_out_bwd(dh, y, ya, yg, gates, att, gm, ymix, wa, wg, wo, gate, gp):
    S = dh.shape[0]
    R = min(512, S)
    nb = S // R

    def body(dh_ref, y_ref, ya_ref, yg_ref, gates_ref, att_ref, gm_ref, ym_ref, wa_ref, wg_ref, wo_ref,
             gate_ref, gp_ref, dz_ref, do_ref, dgm_ref, dgate_ref, dgp_ref, gwo_ref, gwa_ref, gwg_ref,
             acc_o, acc_a, acc_g, dy_scr, dya_scr, dyg_scr):
        i = pl.program_id(0)

        @pl.when(i == 0)
        def _():
            for r in (dgate_ref, dgp_ref, acc_o, acc_a, acc_g):
                r[...] = jnp.zeros_like(r)
        for r0 in range(0, R, CHUNK):
            rows = slice(r0, r0 + CHUNK)
            dy, dgate, dgp = _postnorm_bwd(dh_ref[rows, :], y_ref[rows, :], gate_ref[...], gp_ref[...], 1.0)
            dgate_ref[...] += dgate
            dgp_ref[...] += dgp
            dyb = dy.astype(BF)
            dy_scr[rows, :] = dyb
            dym = _dot_nt(dyb, wo_ref[...])
            ga = gates_ref[rows, 0:D].astype(F32)
            gg = gates_ref[rows, D:2 * D].astype(F32)
            dya = (dym * ga).astype(BF)
            dyg = (dym * gg).astype(BF)
            dya_scr[rows, :] = dya
            dyg_scr[rows, :] = dyg
            dz_ref[rows, 0:D] = (dym * ya_ref[rows, :].astype(F32) * (ga * (1.0 - ga))).astype(BF)
            dz_ref[rows, D:2 * D] = (dym * yg_ref[rows, :].astype(F32) * (gg * (1.0 - gg))).astype(BF)
            do_ref[rows, :] = _dot_nt(dya, wa_ref[...]).astype(BF)
            dgm_ref[rows, :] = _dot_nt(dyg, wg_ref[...]).astype(BF)
        for m0 in range(0, D, CHUNK):
            acc_o[m0:m0 + CHUNK, :] += _dot_tn(ym_ref[:, m0:m0 + CHUNK], dy_scr[...])
        for m0 in range(0, Q_W, CHUNK):
            acc_a[m0:m0 + CHUNK, :] += _dot_tn(att_ref[:, m0:m0 + CHUNK], dya_scr[...])
            acc_g[m0:m0 + CHUNK, :] += _dot_tn(gm_ref[:, m0:m0 + CHUNK], dyg_scr[...])

        @pl.when(i == nb - 1)
        def _():
            for m0 in range(0, D, CHUNK):
                gwo_ref[m0:m0 + CHUNK, :] = acc_o[m0:m0 + CHUNK, :].astype(BF)
            for m0 in range(0, Q_W, CHUNK):
                gwa_ref[m0:m0 + CHUNK, :] = acc_a[m0:m0 + CHUNK, :].astype(BF)
                gwg_ref[m0:m0 + CHUNK, :] = acc_g[m0:m0 + CHUNK, :].astype(BF)

    vec = _const((1, D))
    rows = lambda w_: pl.BlockSpec((R, w_), lambda i: (i, 0))
    return pl.pallas_call(
        body, name="mix_out_bwd", grid=(nb,),
        out_shape=[_sds((S, 2 * D), BF), _sds((S, Q_W), BF), _sds((S, G_W), BF), _sds((1, D), F32),
                   _sds((1, D), F32), _sds((D, D), BF), _sds((Q_W, D), BF), _sds((G_W, D), BF)],
        in_specs=[rows(D), rows(D), rows(D), rows(D), rows(2 * D), rows(Q_W), rows(G_W), rows(D),
                  _resident((Q_W, D)), _resident((G_W, D)), _resident((D, D)), vec, vec],
        out_specs=[rows(2 * D), rows(Q_W), rows(G_W), vec, vec, _const((D, D)), _const((Q_W, D)),
                   _const((G_W, D))],
        scratch_shapes=[pltpu.VMEM((D, D), F32), pltpu.VMEM((Q_W, D), F32), pltpu.VMEM((G_W, D), F32)]
        + [pltpu.VMEM((R, D), BF)] * 3,
        compiler_params=_cp(1, 60),
    )(dh, y, ya, yg, gates, att, gm, ymix, wa, wg, wo, gate, gp)


def _mix_dn(dq, dkv, dzg, dzgate, w, h, dh, sc, gp):
    S = h.shape[0]
    R = min(512, S)

    def body(dq_ref, dkv_ref, dzg_ref, dzt_ref, w_ref, h_ref, dh_ref, sc_ref, gp_ref,
             out_ref, dsh_ref, dsc_ref, dgp_ref):
        @pl.when(pl.program_id(0) == 0)
        def _():
            dsh_ref[...] = jnp.zeros_like(dsh_ref)
            dsc_ref[...] = jnp.zeros_like(dsc_ref)
            dgp_ref[...] = jnp.zeros_like(dgp_ref)
        for r0 in range(0, R, CHUNK):
            rows = slice(r0, r0 + CHUNK)
            dn = _dot(dq_ref[rows, :], w_ref[0:Q_W, :])
            dn = dn + _dot(dkv_ref[rows, :], w_ref[Q_W:QKV_W, :])
            dn = dn + _dot(dzg_ref[rows, :], w_ref[ZG_OFF:GATE_OFF, :])
            dn = dn + _dot(dzt_ref[rows, :], w_ref[GATE_OFF:IN_W, :])
            dx, dsh, dsc, dgp = _prenorm_bwd(dn, h_ref[rows, :], gp_ref[...], sc_ref[...])
            out_ref[rows, :] = dh_ref[rows, :] + dx
            dsh_ref[...] += dsh
            dsc_ref[...] += dsc
            dgp_ref[...] += dgp

    vec = _const((1, D))
    rows = lambda w_: pl.BlockSpec((R, w_), lambda i: (i, 0))
    return pl.pallas_call(
        body, name="mix_dn", grid=(S // R,),
        out_shape=[_sds((S, D), F32)] + [_sds((1, D), F32)] * 3,
        in_specs=[rows(Q_W), rows(2 * KV_W), rows(2 * G_W), rows(2 * D), _resident((IN_W, D)),
                  rows(D), rows(D), vec, vec],
        out_specs=[rows(D), vec, vec, vec],
        compiler_params=_cp(1, 48),
    )(dq, dkv, dzg, dzgate, w, h, dh, sc, gp)


def _adamw_math(w, g, m, v):
    m2 = ADAM_B1 * m + (1.0 - ADAM_B1) * g
    v2 = ADAM_B2 * v + (1.0 - ADAM_B2) * (g * g)
    m_hat = m2 / (1.0 - ADAM_B1 ** ADAM_STEP)
    v_hat = v2 / (1.0 - ADAM_B2 ** ADAM_STEP)
    delta = -ADAM_LR * (m_hat / (jnp.sqrt(v_hat) + ADAM_EPS) + ADAM_WD * w)
    return delta, m2, v2


def _row_tile(rows, cols):
    best = None
    for t in range(16, rows + 1, 16):
        if rows % t == 0 and t * cols <= 256 * 1024:
            best = t
    return best if best is not None else rows


def _adamw_sharded(landing, w, m, v, name):
    r, c = w.shape
    tr = _row_tile(r, c)

    def body(l_ref, w_ref, m_ref, v_ref, g_ref, d_ref, m2_ref, v2_ref):
        g = l_ref[0].astype(F32)
        for j in range(1, N_DEV):
            g = g + l_ref[j].astype(F32)
        delta, m2, v2 = _adamw_math(w_ref[...], g, m_ref[...], v_ref[...])
        g_ref[...] = g
        d_ref[...] = delta
        m2_ref[...] = m2
        v2_ref[...] = v2

    row = pl.BlockSpec((tr, c), lambda i: (i, 0))
    return pl.pallas_call(
        body, name=name, grid=(r // tr,),
        out_shape=[_sds((r, c), F32)] * 4,
        in_specs=[pl.BlockSpec((N_DEV, tr, c), lambda i: (0, i, 0)), row, row, row],
        out_specs=[row] * 4,
        compiler_params=_cp(1, 48),
    )(landing, w, m, v)


def _adamw_small(w, g, m, v, name):
    def body(w_ref, g_ref, m_ref, v_ref, d_ref, m2_ref, v2_ref):
        delta, m2, v2 = _adamw_math(w_ref[...], g_ref[...], m_ref[...], v_ref[...])
        d_ref[...] = delta
        m2_ref[...] = m2
        v2_ref[...] = v2

    vm = pl.BlockSpec(memory_space=pltpu.VMEM)
    return pl.pallas_call(body, name=name, out_shape=[_sds(w.shape, F32)] * 3,
                          in_specs=[vm] * 4, out_specs=[vm] * 3)(w, g, m, v)


def _w_ada_update(c8, d_ada, w, m, v):
    tr = 256

    def body(c_ref, d_ref, w_ref, m_ref, v_ref, g_ref, dl_ref, m2_ref, v2_ref):
        cs = c_ref[...]
        cs = cs * jax.nn.sigmoid(cs)
        g = lax.dot_general(cs, d_ref[...], (((0,), (0,)), ((), ())), preferred_element_type=F32, precision=HIGH)
        delta, m2, v2 = _adamw_math(w_ref[...], g, m_ref[...], v_ref[...])
        g_ref[...] = g
        dl_ref[...] = delta
        m2_ref[...] = m2
        v2_ref[...] = v2

    row = pl.BlockSpec((tr, ADA_W), lambda i: (i, 0))
    return pl.pallas_call(
        body, name="w_ada_update", grid=(D // tr,),
        out_shape=[_sds((D, ADA_W), F32)] * 4,
        in_specs=[pl.BlockSpec((N_DEV, tr), lambda i: (0, i)), _const((N_DEV, ADA_W)), row, row, row],
        out_specs=[row] * 4,
        compiler_params=_cp(1, 40),
    )(c8, d_ada, w, m, v)


def _t5_bucket():
    qi = jnp.arange(BLK, dtype=jnp.int32)[:, None]
    kj = jnp.arange(2 * BLK, dtype=jnp.int32)[None, :]
    dist = jnp.maximum(qi + BLK - kj, 0)
    max_exact = N_BUCKETS // 2
    d_f = jnp.maximum(dist, max_exact).astype(F32)
    large = max_exact + (jnp.log(d_f / max_exact) / math.log(MAX_DISTANCE / max_exact)
                         * (N_BUCKETS - max_exact)).astype(jnp.int32)
    large = jnp.minimum(large, N_BUCKETS - 1)
    return jnp.where(dist < max_exact, dist, large)


def _slabs_of_columns(w):
    r, c8 = w.shape
    return jnp.transpose(w.reshape(r, N_DEV, c8 // N_DEV), (1, 0, 2))


def _columns_of_slabs(w8):
    _, r, c = w8.shape
    return jnp.transpose(w8, (1, 0, 2)).reshape(r, N_DEV * c)


def kernel(x, c, rel_bias, w_ada, b_ada, pre_norm_g, post_norm_g, w_ffn1_in, w_ffn1_out, w_in, sinks, gmlp_ln_g, gmlp_ln_b, gmlp_w_s, gmlp_b_s, w_br_attn, w_br_gmlp, w_out, w_ffn2_in, w_ffn2_out, loss_target, m_rel_bias, m_w_ada, m_b_ada, m_pre_norm_g, m_post_norm_g, m_w_ffn1_in, m_w_ffn1_out, m_w_in, m_sinks, m_gmlp_ln_g, m_gmlp_ln_b, m_gmlp_w_s, m_gmlp_b_s, m_w_br_attn, m_w_br_gmlp, m_w_out, m_w_ffn2_in, m_w_ffn2_out, v_rel_bias, v_w_ada, v_b_ada, v_pre_norm_g, v_post_norm_g, v_w_ffn1_in, v_w_ffn1_out, v_w_in, v_sinks, v_gmlp_ln_g, v_gmlp_ln_b, v_gmlp_w_s, v_gmlp_b_s, v_w_br_attn, v_w_br_gmlp, v_w_out, v_w_ffn2_in, v_w_ffn2_out):
    me = 4 * lax.axis_index("x") + 2 * lax.axis_index("y") + lax.axis_index("c")
    x0 = x[0]
    target = loss_target[0]

    transposed = ("w_ffn1_in", "w_in", "w_ffn2_in")
    shards = [w_ffn1_in[0].T, w_ffn1_out[0], w_in[0].T, w_br_attn[0], w_br_gmlp[0], w_out[0],
              w_ffn2_in[0].T, w_ffn2_out[0]]
    shards_bf = [s.astype(BF) for s in shards]
    groups = [shards_bf[0:1], shards_bf[1:6], shards_bf[6:8]]

    def gather_start(i, after):
        return _slabs_start("gather", groups[i], after, "gather_start_%d" % i)

    def forward_start(st, i, after):
        lands = _slabs_wait("gather", len(groups[i]), st, after, "gather_wait_%d" % i)
        return _slabs_start("forward", lands, c, "forward_start_%d" % i)

    def gathered(st, i, after):
        return _slabs_wait("forward", len(groups[i]), st, after, "forward_wait_%d" % i)

    gs0 = gather_start(0, c)

    small = jnp.concatenate([c[0], pre_norm_g[0].reshape(-1), post_norm_g[0].reshape(-1)])
    small8 = jnp.broadcast_to(small[None, :], (8, small.shape[0]))
    b_ada64 = jnp.repeat(b_ada.reshape(N_DEV, ADA_W), 8, axis=0)
    gath, ada64 = _ada_forward(small8, w_ada[0], b_ada64)
    gath8 = gath[::8]
    ada = ada64[::8].reshape(9, D)
    sh1, sc1, g1, sh2, sc2, g2, sh3, sc3, g3 = [ada[k:k + 1] for k in range(9)]
    gains = gath8[:, D:].reshape(N_DEV, 2, 3, 128)
    pre_g = jnp.transpose(gains[:, 0], (1, 0, 2)).reshape(3, D)
    post_g = jnp.transpose(gains[:, 1], (1, 0, 2)).reshape(3, D)
    pre = [pre_g[k:k + 1] for k in range(3)]
    post = [post_g[k:k + 1] for k in range(3)]

    bucket = _t5_bucket()
    bias = _bias_table(rel_bias, bucket)
    sinks8 = sinks[0]
    lg, lb = gmlp_ln_g, gmlp_ln_b
    ws = gmlp_w_s[0]
    bst = jnp.transpose(gmlp_b_s[0])

    fs0 = forward_start(gs0, 0, sh1)
    gs1 = gather_start(1, fs0[-1])
    wf1_in = gathered(fs0, 0, gs1[-1])[0].reshape(2 * D_FF, D)
    n1, fg1, fu1, fa1 = _ffn_in(x0, sh1, sc1, pre[0], wf1_in, "ffn1_in")
    fs1 = forward_start(gs1, 1, n1)
    gs2 = gather_start(2, fs1[-1])
    mix_w = gathered(fs1, 1, gs2[-1])
    wf1_out = mix_w[0].reshape(D_FF, D)
    w_in_full = mix_w[1].reshape(IN_W, D)
    w_bra = _columns_of_slabs(mix_w[2])
    w_brg = _columns_of_slabs(mix_w[3])
    w_out_full = mix_w[4].reshape(D, D)
    h1, y1 = _ffn_out(fa1, wf1_out, x0, g1, post[0], "ffn1_out")
    n2, qkv, zg, gates = _mix_in(h1, sh2, sc2, pre[1], w_in_full)
    att = _attn_fwd(qkv, bias, sinks8)
    gm = _gmlp_fwd(zg, lg, lb, ws, bst)
    fs2 = forward_start(gs2, 2, gm)
    ya, yg, ymix, y2, h2 = _mix_out(att, gm, gates, h1, w_bra, w_brg, w_out_full, g2 + fs2[-1], post[1])
    wf2_in, wf2_out = gathered(fs2, 2, h2)
    wf2_in = wf2_in.reshape(2 * D_FF, D)
    wf2_out = wf2_out.reshape(D_FF, D)
    n3, fg3, fu3, fa3, y3, dh3, sq = _ffn_fwd_loss(h2, sh3, sc3, pre[2], wf2_in, wf2_out, g3, post[2], target,
                                                   "ffn2_fwd_loss")
    loss = lax.psum(0.5 * sq[0, 0] / D, ("x", "y", "c"))

    def exchange_start(i, arrays):
        return _slabs_start("exchange", arrays, sq, "exchange_start_%d" % i)

    dy3, dgu3, dh2, d_g3, d_post2, d_sh3, d_sc3, d_pre2 = _ffn_bwd(
        dh3, y3, fg3, fu3, wf2_out, wf2_in, h2, g3, post[2], sc3, pre[2], "ffn2_bwd")
    gw_f2_out = _tn_matmul(fa3, dy3, "ffn2_out_wgrad", tm=D_FF // 2).reshape(N_DEV, D_FF // N_DEV, D)
    ex0 = exchange_start(0, [gw_f2_out])
    gw_f2_in = _tn_matmul(dgu3, n3, "ffn2_in_wgrad", tm=D_FF // 2).reshape(N_DEV, FS, D)
    ex1 = exchange_start(1, [gw_f2_in])

    dzgate, d_att, d_gm, d_g2, d_post1, gw_out, gw_bra, gw_brg = _mix_out_bwd(
        dh2, y2, ya, yg, gates, att, gm, ymix, w_bra, w_brg, w_out_full, g2 + ex0[-1] + ex1[-1], post[1])
    ex2 = exchange_start(2, [_slabs_of_columns(gw_bra), _slabs_of_columns(gw_brg),
                             gw_out.reshape(N_DEV, D // N_DEV, D)])
    dq, dkv, dbias, dsink = _attn_bwd(qkv, bias, sinks8, d_att)
    dzg, d_ws, d_bs, d_lg, d_lb = _gmlp_bwd(zg, d_gm, lg, lb, ws, bst)
    d_rel = _rel_bias_grad(dbias, bucket)
    early = jnp.concatenate([
        jnp.concatenate([d_lg.reshape(4, 128), d_lb.reshape(4, 128)], axis=0),
        d_bs, d_rel, dsink, d_ws.reshape(N_HEADS * BLK, BLK)], axis=0)
    sm0 = _slabs_start("gather_all", [early], sq, "small_gather_start")
    dh1, d_sh2, d_sc2, d_pre1 = _mix_dn(dq, dkv, dzg, dzgate, w_in_full, h1, dh2, sc2 + ex2[-1] + sm0[-1], pre[1])
    gw_in = jnp.concatenate(
        [_tn_matmul(dq, n2, "w_in_q_wgrad"), _tn_matmul(dkv, n2, "w_in_kv_wgrad"),
         _tn_matmul(dzg, n2, "w_in_zg_wgrad"), _tn_matmul(dzgate, n2, "w_in_gate_wgrad")],
        axis=0).reshape(N_DEV, IN_W // N_DEV, D)
    ex3 = exchange_start(3, [gw_in])

    dy1, dgu1, d_g1, d_post0 = _ffn_out_bwd(dh1, y1, fg1, fu1, wf1_out, g1 + ex3[-1], post[0], "ffn1_out_bwd")
    gw_f1_out = _tn_matmul(fa1, dy1, "ffn1_out_wgrad", tm=D_FF // 2).reshape(N_DEV, D_FF // N_DEV, D)
    gw_f1_in = _tn_matmul(dgu1, n1, "ffn1_in_wgrad", tm=D_FF // 2).reshape(N_DEV, FS, D)
    ex4 = exchange_start(4, [gw_f1_out, gw_f1_in])
    grad_x, d_sh1, d_sc1, d_pre0 = _ffn_dn(dgu1, wf1_in, x0, dh1, sc1 + ex4[-1], pre[0], "ffn1_dn")

    landed = {}
    for i, (ex, nms) in enumerate([(ex0, ["w_ffn2_out"]), (ex1, ["w_ffn2_in"]),
                                   (ex2, ["w_br_attn", "w_br_gmlp", "w_out"]), (ex3, ["w_in"]),
                                   (ex4, ["w_ffn1_out", "w_ffn1_in"])]):
        for nm, land in zip(nms, _slabs_wait("exchange", len(nms), ex, grad_x, "exchange_wait_%d" % i)):
            landed[nm] = land
    moments = [(m_w_ffn1_in, v_w_ffn1_in), (m_w_ffn1_out, v_w_ffn1_out), (m_w_in, v_w_in),
               (m_w_br_attn, v_w_br_attn), (m_w_br_gmlp, v_w_br_gmlp), (m_w_out, v_w_out),
               (m_w_ffn2_in, v_w_ffn2_in), (m_w_ffn2_out, v_w_ffn2_out)]
    names = ["w_ffn1_in", "w_ffn1_out", "w_in", "w_br_attn", "w_br_gmlp", "w_out", "w_ffn2_in", "w_ffn2_out"]
    big = {}
    for nm, w_, (m_, v_) in zip(names, shards, moments):
        if nm in transposed:
            res4 = _adamw_sharded(landed[nm], w_, m_[0].T, v_[0].T, "adamw_" + nm)
            big[nm] = [a.T[None] for a in res4]
        else:
            big[nm] = [a[None] for a in _adamw_sharded(landed[nm], w_, m_[0], v_[0], "adamw_" + nm)]

    d_ada = jnp.concatenate([v_.reshape(8, 128) for v_ in
                             (d_sh1, d_sc1, d_g1, d_sh2, d_sc2, d_g2, d_sh3, d_sc3, d_g3)], axis=0)
    d_pre = jnp.concatenate([d_pre0, d_pre1, d_pre2], axis=0)
    d_post = jnp.concatenate([d_post0, d_post1, d_post2], axis=0)
    late = jnp.concatenate([d_ada, _slabs_of_columns(d_pre).reshape(24, 128),
                            _slabs_of_columns(d_post).reshape(24, 128)], axis=0)
    tot, every = _small_allreduce(late)
    (early_land,) = _slabs_wait("gather_all", 1, sm0, grad_x, "small_gather_wait")
    tot_early = _sum_slabs(early_land)

    g_b_ada = tot[0:72].reshape(1, 9 * D)
    g_pre = lax.dynamic_slice_in_dim(tot[72:96], 3 * me, 3, axis=0)[None]
    g_post = lax.dynamic_slice_in_dim(tot[96:120], 3 * me, 3, axis=0)[None]
    g_lg = tot_early[0:4].reshape(1, G_W)
    g_lb = tot_early[4:8].reshape(1, G_W)
    g_bs = tot_early[8:16][None]
    g_rel = jnp.transpose(tot_early[16:24, 0:N_BUCKETS])
    g_sinks = tot_early[24:32, 0][None]
    g_ws = tot_early[32:1056].reshape(1, N_HEADS, BLK, BLK)

    d_ada_mine = lax.dynamic_slice_in_dim(every[:, 0:72].reshape(N_DEV, N_DEV, ADA_W), me, 1, axis=1)[:, 0]
    ada_out = [a[None] for a in _w_ada_update(gath8[:, 0:D], d_ada_mine, w_ada[0], m_w_ada[0], v_w_ada[0])]

    def small_step(w_, g_, m_, v_, nm):
        shp = w_.shape
        two_d = (int(math.prod(shp[:-1])), shp[-1])
        d_, m2_, v2_ = _adamw_small(w_.reshape(two_d), g_.reshape(two_d), m_.reshape(two_d), v_.reshape(two_d),
                                    "adamw_" + nm)
        return [g_, d_.reshape(shp), m2_.reshape(shp), v2_.reshape(shp)]

    res = {
        "rel_bias": small_step(rel_bias, g_rel, m_rel_bias, v_rel_bias, "rel_bias"),
        "w_ada": ada_out,
        "b_ada": small_step(b_ada, g_b_ada, m_b_ada, v_b_ada, "b_ada"),
        "pre_norm_g": small_step(pre_norm_g, g_pre, m_pre_norm_g, v_pre_norm_g, "pre_norm_g"),
        "post_norm_g": small_step(post_norm_g, g_post, m_post_norm_g, v_post_norm_g, "post_norm_g"),
        "sinks": small_step(sinks, g_sinks, m_sinks, v_sinks, "sinks"),
        "gmlp_ln_g": small_step(gmlp_ln_g, g_lg, m_gmlp_ln_g, v_gmlp_ln_g, "gmlp_ln_g"),
        "gmlp_ln_b": small_step(gmlp_ln_b, g_lb, m_gmlp_ln_b, v_gmlp_ln_b, "gmlp_ln_b"),
        "gmlp_w_s": small_step(gmlp_w_s, g_ws, m_gmlp_w_s, v_gmlp_w_s, "gmlp_w_s"),
        "gmlp_b_s": small_step(gmlp_b_s, g_bs, m_gmlp_b_s, v_gmlp_b_s, "gmlp_b_s"),
    }
    res.update(big)
    order = ["rel_bias", "w_ada", "b_ada", "pre_norm_g", "post_norm_g", "w_ffn1_in", "w_ffn1_out", "w_in", "sinks",
             "gmlp_ln_g", "gmlp_ln_b", "gmlp_w_s", "gmlp_b_s", "w_br_attn", "w_br_gmlp", "w_out", "w_ffn2_in",
             "w_ffn2_out"]
    outs = [loss, grad_x[None]]
    for k in range(4):
        outs += [res[nm][k] for nm in order]
    return tuple(outs)
```

```python
import functools
import math

import jax
import jax.numpy as jnp
from jax import lax
from jax.experimental import pallas as pl
from jax.experimental.pallas import tpu as pltpu

F32 = jnp.float32
BF = jnp.bfloat16

N_DEV = 8
D = 1024
D_FF = 2816
FS = D_FF // 4
N_HEADS = 8
N_KV = 2
GROUP = 4
HD = 64
BLK = 128
Q_W = 512
KV_W = 128
G_W = 512
QKV_W = Q_W + 2 * KV_W
ZG_OFF = QKV_W
GATE_OFF = ZG_OFF + 2 * G_W
IN_W = GATE_OFF + 2 * D
N_BUCKETS = 32
MAX_DISTANCE = 128
EPS = 1e-6
NEG = -1e30
SCALE = HD ** -0.5
ADA_W = 9 * D // N_DEV

ADAM_LR = 0.001
ADAM_B1 = 0.9
ADAM_B2 = 0.999
ADAM_EPS = 1e-08
ADAM_WD = 0.01
ADAM_STEP = 10

CHUNK = 256
MIB = 1024 * 1024
MESH = pl.DeviceIdType.MESH
HIGH = lax.Precision.HIGHEST


def _cp(n_grid, vmem_mib):
    return pltpu.CompilerParams(dimension_semantics=("arbitrary",) * n_grid,
                                vmem_limit_bytes=vmem_mib * MIB)


def _const(shape):
    return pl.BlockSpec(shape, lambda *_: (0,) * len(shape))


def _resident(shape):
    return pl.BlockSpec(shape, lambda *_: (0,) * len(shape), pipeline_mode=pl.Buffered(1))


def _sds(shape, dtype):
    return jax.ShapeDtypeStruct(shape, dtype)


def _dot(a, b):
    return jnp.dot(a, b, preferred_element_type=F32)


def _dot_nt(a, b):
    return lax.dot_general(a, b, (((1,), (1,)), ((), ())), preferred_element_type=F32)


def _dot_tn(a, b):
    return lax.dot_general(a, b, (((0,), (0,)), ((), ())), preferred_element_type=F32)


def _rms_r(x):
    return lax.rsqrt(jnp.mean(x * x, axis=-1, keepdims=True) + EPS)


def _colsum(x):
    return jnp.sum(x, axis=0, keepdims=True)


def _prenorm(x, gp, sc, sh):
    return (x * _rms_r(x) * gp) * (1.0 + sc) + sh


def _prenorm_bwd(dn, x, gp, sc):
    r = _rms_r(x)
    xh = x * r
    t = dn * (1.0 + sc) * gp
    dx = r * (t - xh * jnp.mean(t * xh, axis=-1, keepdims=True))
    return dx, _colsum(dn), _colsum(dn * xh * gp), _colsum(dn * (1.0 + sc) * xh)


def _postnorm_bwd(dh, y, gate, gp, res):
    y = y.astype(F32)
    r = _rms_r(y)
    yh = y * r
    dyn = (res * gate) * dh
    t = dyn * gp
    dy = r * (t - yh * jnp.mean(t * yh, axis=-1, keepdims=True))
    return dy, _colsum(res * dh * yh * gp), _colsum(dyn * yh)


def _gelu(x):
    k = math.sqrt(2.0 / math.pi)
    return 0.5 * x * (1.0 + jnp.tanh(k * (x + 0.044715 * x * x * x)))


def _gelu_grad(x):
    k = math.sqrt(2.0 / math.pi)
    t = jnp.tanh(k * (x + 0.044715 * x * x * x))
    return 0.5 * (1.0 + t) + 0.5 * x * (1.0 - t * t) * (k * (1.0 + 3.0 * 0.044715 * x * x))


def _my_place():
    x, y, c = lax.axis_index("x"), lax.axis_index("y"), lax.axis_index("c")
    return x, y, c, 4 * x + 2 * y + c


def _peer(x, y, c, k):
    px = 1 - x if k & 4 else x
    py = 1 - y if k & 2 else y
    pc = 1 - c if k & 1 else c
    return (px, py, pc), 4 * px + 2 * py + pc


HBM_SPEC = pl.BlockSpec(memory_space=pltpu.HBM)
SEM_SPEC = pl.BlockSpec(memory_space=pltpu.SEMAPHORE)
EFFECT = pltpu.SideEffectType.DATAFLOW_SIDE_EFFECTING


RELATIONS = {"exchange": (1, 2, 3, 4, 5, 6, 7), "gather": (1, 2, 4, 6), "forward": (2, 4, 6),
             "gather_all": (1, 2, 3, 4, 5, 6, 7)}


def _slab_copies(mode, srcs, lands, send, recv, loc):
    x, y, c, me = _my_place()
    rel = RELATIONS[mode]
    remote, local = [], []
    for t in range(len(lands)):
        for i, k in enumerate(rel):
            peer, peer_lin = _peer(x, y, c, k)
            if mode == "exchange":
                src, dst, to = srcs[t].at[peer_lin], lands[t].at[me], peer
            elif mode in ("gather", "gather_all"):
                src, dst, to = srcs[t], lands[t].at[me], peer
            else:
                src, dst, to = lands[t].at[peer_lin], lands[t].at[peer_lin], _peer(x, y, c, 1)[0]
            remote.append(pltpu.make_async_remote_copy(
                src_ref=src, dst_ref=dst, send_sem=send.at[t * len(rel) + i], recv_sem=recv.at[t * len(rel) + i],
                device_id=to, device_id_type=MESH))
        if mode == "exchange":
            local.append(pltpu.make_async_copy(srcs[t].at[me], lands[t].at[me], loc.at[t]))
        elif mode in ("gather", "gather_all"):
            local.append(pltpu.make_async_copy(srcs[t], lands[t].at[me], loc.at[t]))
    return remote, local


def _slabs_start(mode, arrays, after, name):
    n = len(arrays)
    if mode == "forward":
        thru = list(arrays)
    else:
        shapes = [a.shape if mode == "exchange" else (N_DEV,) + a.shape for a in arrays]
        thru = list(arrays) + [lax.empty(s, a.dtype) for s, a in zip(shapes, arrays)]
    m = len(thru)
    n_sem = n * len(RELATIONS[mode])

    def body(*refs):
        srcs, lands = refs[:n], refs[m - n:m]
        send, recv, loc = refs[m + 1:m + 4]
        remote, local = _slab_copies(mode, srcs, lands, send, recv, loc)
        for cp in remote + local:
            cp.start()
        refs[-1][...] = jnp.zeros_like(refs[-1])

    return pl.pallas_call(
        body, name=name,
        out_shape=(pltpu.SemaphoreType.DMA((n_sem,)), pltpu.SemaphoreType.DMA((n_sem,)),
                   pltpu.SemaphoreType.DMA((n,)),
                   *[pltpu.HBM(a.shape, a.dtype) for a in thru],
                   _sds((1, D), F32)),
        in_specs=[HBM_SPEC] * m + [pl.BlockSpec(memory_space=pl.ANY)],
        out_specs=(SEM_SPEC, SEM_SPEC, SEM_SPEC, *[HBM_SPEC] * m, pl.BlockSpec(memory_space=pltpu.VMEM)),
        input_output_aliases={t: 3 + t for t in range(m)},
        compiler_params=pltpu.CompilerParams(has_side_effects=EFFECT),
    )(*[pltpu.with_memory_space_constraint(a, pltpu.HBM) for a in thru], after)


def _slabs_wait(mode, n, started, after, name):
    sems = started[0:3]
    thru = started[3:-1]
    m = len(thru)

    def body(*refs):
        srcs, lands = refs[:n], refs[m - n:m]
        remote, local = _slab_copies(mode, srcs, lands, *refs[m:m + 3])
        for cp in remote:
            cp.wait_send()
            cp.wait_recv()
        for cp in local:
            cp.wait()

    res = pl.pallas_call(
        body, name=name,
        out_shape=tuple(pltpu.HBM(a.shape, a.dtype) for a in thru),
        in_specs=[HBM_SPEC] * m + [SEM_SPEC] * 3 + [pl.BlockSpec(memory_space=pl.ANY)],
        out_specs=tuple([HBM_SPEC] * m),
        input_output_aliases={t: t for t in range(m)},
        compiler_params=pltpu.CompilerParams(has_side_effects=EFFECT),
    )(*thru, *sems, after)
    return list(res[m - n:m])


def _ada_forward(small8, w_ada, b_ada64):
    sw = small8.shape[1]

    def body(sm_ref, w_ref, b_ref, gath_ref, ada_ref, part_ref, send1, recv1, send2, recv2):
        x, y, c, me = _my_place()
        row_me = pl.multiple_of(me * 8, 8)
        gath_ref[pl.ds(row_me, 8), :] = sm_ref[...]
        first = []
        for k in range(1, N_DEV):
            peer, _ = _peer(x, y, c, k)
            cp = pltpu.make_async_remote_copy(
                src_ref=sm_ref, dst_ref=gath_ref.at[pl.ds(row_me, 8), :], send_sem=send1.at[k - 1],
                recv_sem=recv1.at[k - 1], device_id=peer, device_id_type=MESH)
            cp.start()
            first.append(cp)
        for cp in first:
            cp.wait()
        cs = gath_ref[:, 0:D]
        cs = cs * jax.nn.sigmoid(cs)
        part_ref[...] = jnp.dot(cs, w_ref[...], preferred_element_type=F32, precision=HIGH)
        ada_ref[pl.ds(row_me, 8), :] = part_ref[pl.ds(row_me, 8), :]
        second = []
        for k in range(1, N_DEV):
            peer, peer_lin = _peer(x, y, c, k)
            cp = pltpu.make_async_remote_copy(
                src_ref=part_ref.at[pl.ds(pl.multiple_of(peer_lin * 8, 8), 8), :],
                dst_ref=ada_ref.at[pl.ds(row_me, 8), :], send_sem=send2.at[k - 1],
                recv_sem=recv2.at[k - 1], device_id=peer, device_id_type=MESH)
            cp.start()
            second.append(cp)
        for cp in second:
            cp.wait()
        ada_ref[...] = ada_ref[...] + b_ref[...]

    vm = pl.BlockSpec(memory_space=pltpu.VMEM)
    return pl.pallas_call(
        body, name="ada_forward",
        out_shape=[_sds((8 * N_DEV, sw), F32), _sds((8 * N_DEV, ADA_W), F32)],
        in_specs=[vm, vm, vm], out_specs=[vm, vm],
        scratch_shapes=[pltpu.VMEM((8 * N_DEV, ADA_W), F32)] + [pltpu.SemaphoreType.DMA((7,))] * 4,
        compiler_params=pltpu.CompilerParams(vmem_limit_bytes=32 * MIB),
    )(small8, w_ada, b_ada64)


def _sum_slabs(land):
    def body(l_ref, o_ref):
        acc = l_ref[0]
        for j in range(1, N_DEV):
            acc = acc + l_ref[j]
        o_ref[...] = acc

    vm = pl.BlockSpec(memory_space=pltpu.VMEM)
    return pl.pallas_call(body, name="sum_slabs", out_shape=_sds(land.shape[1:], F32), in_specs=[vm], out_specs=vm,
                          compiler_params=pltpu.CompilerParams(vmem_limit_bytes=32 * MIB))(land)


def _small_allreduce(pack):
    rows = pack.shape[0]

    def body(p_ref, sum_ref, gath_ref, send, recv):
        x, y, c, me = _my_place()
        gath_ref[me] = p_ref[...]
        cps = []
        for k in range(1, N_DEV):
            peer, _ = _peer(x, y, c, k)
            cp = pltpu.make_async_remote_copy(
                src_ref=p_ref, dst_ref=gath_ref.at[me], send_sem=send.at[k - 1],
                recv_sem=recv.at[k - 1], device_id=peer, device_id_type=MESH)
            cp.start()
            cps.append(cp)
        for cp in cps:
            cp.wait()
        acc = gath_ref[0]
        for j in range(1, N_DEV):
            acc = acc + gath_ref[j]
        sum_ref[...] = acc

    vm = pl.BlockSpec(memory_space=pltpu.VMEM)
    return pl.pallas_call(
        body, name="small_allreduce",
        out_shape=[_sds((rows, 128), F32), _sds((N_DEV, rows, 128), F32)],
        in_specs=[vm], out_specs=[vm, vm],
        scratch_shapes=[pltpu.SemaphoreType.DMA((7,)), pltpu.SemaphoreType.DMA((7,))],
        compiler_params=pltpu.CompilerParams(vmem_limit_bytes=40 * MIB),
    )(pack)


F_TILES = tuple((f0, min(512, D_FF - f0)) for f0 in range(0, D_FF, 512))


def _swiglu_tile(n, wt_ref, f0, tf):
    g = _dot_nt(n, wt_ref[f0:f0 + tf, :])
    u = _dot_nt(n, wt_ref[D_FF + f0:D_FF + f0 + tf, :])
    sg = jax.nn.sigmoid(g)
    silu = g * sg
    return (u * (sg * (1.0 + g * (1.0 - sg)))).astype(BF), silu.astype(BF), (silu * u).astype(BF)


def _ffn_in(h, sh, sc, gp, wt, name):
    S = h.shape[0]
    R = min(512, S)

    def body(h_ref, sh_ref, sc_ref, gp_ref, w_ref, n_ref, dg_ref, sl_ref, a_ref):
        for r0 in range(0, R, CHUNK):
            rows = slice(r0, r0 + CHUNK)
            n = _prenorm(h_ref[rows, :], gp_ref[...], sc_ref[...], sh_ref[...]).astype(BF)
            n_ref[rows, :] = n
            for f0, tf in F_TILES:
                dg_ref[rows, f0:f0 + tf], sl_ref[rows, f0:f0 + tf], a_ref[rows, f0:f0 + tf] = _swiglu_tile(
                    n, w_ref, f0, tf)

    vec = _const((1, D))
    rows_ = lambda w_: pl.BlockSpec((R, w_), lambda i: (i, 0))
    return pl.pallas_call(
        body, name=name, grid=(S // R,),
        out_shape=[_sds((S, D), BF)] + [_sds((S, D_FF), BF)] * 3,
        in_specs=[rows_(D), vec, vec, vec, _resident((2 * D_FF, D))],
        out_specs=[rows_(D), rows_(D_FF), rows_(D_FF), rows_(D_FF)],
        compiler_params=_cp(1, 56),
    )(h, sh, sc, gp, wt)


def _ffn_out(a, w, h, gate, gp, name, target=None):
    S = h.shape[0]
    R = min(512, S)
    with_loss = target is not None

    def body(a_ref, w_ref, h_ref, gate_ref, gp_ref, *rest):
        if with_loss:
            t_ref, out_ref, y_ref, tot_ref = rest

            @pl.when(pl.program_id(0) == 0)
            def _():
                tot_ref[...] = jnp.zeros_like(tot_ref)
        else:
            out_ref, y_ref = rest
        for r0 in range(0, R, CHUNK):
            rows = slice(r0, r0 + CHUNK)
            y = _dot(a_ref[rows, :], w_ref[...])
            y_ref[rows, :] = y.astype(BF)
            hn = h_ref[rows, :] + (0.5 * gate_ref[...]) * (y * _rms_r(y) * gp_ref[...])
            if with_loss:
                e = hn - t_ref[rows, :]
                out_ref[rows, :] = e * (1.0 / D)
                tot_ref[...] += jnp.sum(jnp.sum(e * e, axis=1, keepdims=True), axis=0, keepdims=True)
            else:
                out_ref[rows, :] = hn

    vec = _const((1, D))
    rows_ = lambda w_: pl.BlockSpec((R, w_), lambda i: (i, 0))
    return pl.pallas_call(
        body, name=name, grid=(S // R,),
        out_shape=[_sds((S, D), F32), _sds((S, D), BF)] + ([_sds((1, 1), F32)] if with_loss else []),
        in_specs=[rows_(D_FF), _resident((D_FF, D)), rows_(D), vec, vec] + ([rows_(D)] if with_loss else []),
        out_specs=[rows_(D), rows_(D)] + ([_const((1, 1))] if with_loss else []),
        compiler_params=_cp(1, 48),
    )(*((a, w, h, gate, gp) + ((target,) if with_loss else ())))


def _ffn_out_bwd(dh, y, dsilu_u, silu, w, gate, gp, name):
    S = dh.shape[0]
    R = min(512, S)

    def body(dh_ref, y_ref, g_ref, u_ref, w_ref, gate_ref, gp_ref, dy_ref, dgu_ref, dgate_ref, dgp_ref):
        @pl.when(pl.program_id(0) == 0)
        def _():
            dgate_ref[...] = jnp.zeros_like(dgate_ref)
            dgp_ref[...] = jnp.zeros_like(dgp_ref)
        for r0 in range(0, R, CHUNK):
            rows = slice(r0, r0 + CHUNK)
            dy, dgate, dgp = _postnorm_bwd(dh_ref[rows, :], y_ref[rows, :], gate_ref[...], gp_ref[...], 0.5)
            dgate_ref[...] += dgate
            dgp_ref[...] += dgp
            dyb = dy.astype(BF)
            dy_ref[rows, :] = dyb
            for f0, tf in F_TILES:
                da = _dot_nt(dyb, w_ref[f0:f0 + tf, :])
                dgu_ref[rows, f0:f0 + tf] = (da * g_ref[rows, f0:f0 + tf].astype(F32)).astype(BF)
                dgu_ref[rows, D_FF + f0:D_FF + f0 + tf] = (da * u_ref[rows, f0:f0 + tf].astype(F32)).astype(BF)

    vec = _const((1, D))
    rows_ = lambda w_: pl.BlockSpec((R, w_), lambda i: (i, 0))
    return pl.pallas_call(
        body, name=name, grid=(S // R,),
        out_shape=[_sds((S, D), BF), _sds((S, 2 * D_FF), BF), _sds((1, D), F32), _sds((1, D), F32)],
        in_specs=[rows_(D), rows_(D), rows_(D_FF), rows_(D_FF), _resident((D_FF, D)), vec, vec],
        out_specs=[rows_(D), rows_(2 * D_FF), vec, vec],
        compiler_params=_cp(1, 56),
    )(dh, y, dsilu_u, silu, w, gate, gp)


def _ffn_dn(dgu, wt, h, dh, sc, gp, name):
    S = h.shape[0]
    R = min(512, S)

    def body(dgu_ref, w_ref, h_ref, dh_ref, sc_ref, gp_ref, out_ref, dsh_ref, dsc_ref, dgp_ref):
        @pl.when(pl.program_id(0) == 0)
        def _():
            dsh_ref[...] = jnp.zeros_like(dsh_ref)
            dsc_ref[...] = jnp.zeros_like(dsc_ref)
            dgp_ref[...] = jnp.zeros_like(dgp_ref)

        for r0 in range(0, R, CHUNK):
            rows = slice(r0, r0 + CHUNK)
            dn = _dot(dgu_ref[rows, :], w_ref[...])
            dx, dsh, dsc, dgp = _prenorm_bwd(dn, h_ref[rows, :], gp_ref[...], sc_ref[...])
            out_ref[rows, :] = dh_ref[rows, :] + dx
            dsh_ref[...] += dsh
            dsc_ref[...] += dsc
            dgp_ref[...] += dgp

    vec = _const((1, D))
    rows_ = lambda w_: pl.BlockSpec((R, w_), lambda i: (i, 0))
    return pl.pallas_call(
        body, name=name, grid=(S // R,),
        out_shape=[_sds((S, D), F32)] + [_sds((1, D), F32)] * 3,
        in_specs=[rows_(2 * D_FF), _resident((2 * D_FF, D)), rows_(D), rows_(D), vec, vec],
        out_specs=[rows_(D), vec, vec, vec],
        compiler_params=_cp(1, 56),
    )(dgu, wt, h, dh, sc, gp)


def _ffn_bwd(dh, y, dsilu_u, silu, w, wt, h, gate, gpost, sc, gpre, name):
    S = dh.shape[0]
    R = min(256, S)

    def body(dh_ref, y_ref, g_ref, u_ref, w_ref, wt_ref, h_ref, gate_ref, gpost_ref, sc_ref, gpre_ref,
             dy_ref, dgu_ref, out_ref, dgate_ref, dgpost_ref, dsh_ref, dsc_ref, dgpre_ref):
        @pl.when(pl.program_id(0) == 0)
        def _():
            for r in (dgate_ref, dgpost_ref, dsh_ref, dsc_ref, dgpre_ref):
                r[...] = jnp.zeros_like(r)
        dhh = dh_ref[...]
        dy, dgate, dgpost = _postnorm_bwd(dhh, y_ref[...], gate_ref[...], gpost_ref[...], 0.5)
        dgate_ref[...] += dgate
        dgpost_ref[...] += dgpost
        dyb = dy.astype(BF)
        dy_ref[...] = dyb
        dn = None
        for f0, tf in F_TILES:
            da = _dot_nt(dyb, w_ref[f0:f0 + tf, :])
            dg = (da * g_ref[:, f0:f0 + tf].astype(F32)).astype(BF)
            du = (da * u_ref[:, f0:f0 + tf].astype(F32)).astype(BF)
            dgu_ref[:, f0:f0 + tf] = dg
            dgu_ref[:, D_FF + f0:D_FF + f0 + tf] = du
            part = _dot(dg, wt_ref[f0:f0 + tf, :]) + _dot(du, wt_ref[D_FF + f0:D_FF + f0 + tf, :])
            dn = part if dn is None else dn + part
        dx, dsh, dsc, dgpre = _prenorm_bwd(dn, h_ref[...], gpre_ref[...], sc_ref[...])
        out_ref[...] = dhh + dx
        dsh_ref[...] += dsh
        dsc_ref[...] += dsc
        dgpre_ref[...] += dgpre

    vec = _const((1, D))
    rows_ = lambda w_: pl.BlockSpec((R, w_), lambda i: (i, 0))
    return pl.pallas_call(
        body, name=name, grid=(S // R,),
        out_shape=[_sds((S, D), BF), _sds((S, 2 * D_FF), BF), _sds((S, D), F32)] + [_sds((1, D), F32)] * 5,
        in_specs=[rows_(D), rows_(D), rows_(D_FF), rows_(D_FF), _resident((D_FF, D)), _resident((2 * D_FF, D)),
                  rows_(D), vec, vec, vec, vec],
        out_specs=[rows_(D), rows_(2 * D_FF), rows_(D)] + [vec] * 5,
        compiler_params=_cp(1, 56),
    )(dh, y, dsilu_u, silu, w, wt, h, gate, gpost, sc, gpre)


def _tn_matmul(a, b, name, tm=None):
    S, M_all = a.shape
    N = b.shape[1]
    M = M_all if tm is None else tm
    GA = M_all // M
    ts = min(2048 if M * N <= 2 * D * D else 1024, S)
    nk = S // ts
    chunks = [(m0, min(CHUNK, M - m0)) for m0 in range(0, M, CHUNK)]

    def body(a_ref, b_ref, o_ref, acc):
        k = pl.program_id(1)

        @pl.when(k == 0)
        def _():
            acc[...] = jnp.zeros_like(acc)

        for m0, mc in chunks:
            acc[m0:m0 + mc, :] += _dot_tn(a_ref[:, m0:m0 + mc], b_ref[...])

        @pl.when(k == nk - 1)
        def _():
            for m0, mc in chunks:
                o_ref[m0:m0 + mc, :] = acc[m0:m0 + mc, :].astype(BF)

    return pl.pallas_call(
        body, name=name, grid=(GA, nk),
        out_shape=_sds((M_all, N), BF),
        in_specs=[pl.BlockSpec((ts, M), lambda ga, k: (k, ga)), pl.BlockSpec((ts, N), lambda ga, k: (k, 0))],
        out_specs=pl.BlockSpec((M, N), lambda ga, k: (ga, 0)),
        scratch_shapes=[pltpu.VMEM((M, N), F32)],
        compiler_params=_cp(2, 56),
    )(a, b)


def _mix_in(h, sh, sc, gp, w):
    S = h.shape[0]
    R = min(512, S)

    def body(h_ref, sh_ref, sc_ref, gp_ref, w_ref, n_ref, qkv_ref, zg_ref, gates_ref):
        for r0 in range(0, R, CHUNK):
            rows = slice(r0, r0 + CHUNK)
            nb = _prenorm(h_ref[rows, :], gp_ref[...], sc_ref[...], sh_ref[...]).astype(BF)
            n_ref[rows, :] = nb
            qkv_ref[rows, :] = _dot_nt(nb, w_ref[0:ZG_OFF, :]).astype(BF)
            zg_ref[rows, :] = _dot_nt(nb, w_ref[ZG_OFF:GATE_OFF, :]).astype(BF)
            gates_ref[rows, :] = jax.nn.sigmoid(_dot_nt(nb, w_ref[GATE_OFF:IN_W, :])).astype(BF)

    vec = _const((1, D))
    rows = lambda w_: pl.BlockSpec((R, w_), lambda i: (i, 0))
    return pl.pallas_call(
        body, name="mix_in", grid=(S // R,),
        out_shape=[_sds((S, D), BF), _sds((S, QKV_W), BF), _sds((S, 2 * G_W), BF), _sds((S, 2 * D), BF)],
        in_specs=[rows(D), vec, vec, vec, _resident((IN_W, D))],
        out_specs=[rows(D), rows(QKV_W), rows(2 * G_W), rows(2 * D)],
        compiler_params=_cp(1, 48),
    )(h, sh, sc, gp, w)


def _bias_table(rel_bias, bucket):
    def body(rel_ref, bk_ref, out_ref):
        bk = bk_ref[...]
        qi = lax.broadcasted_iota(jnp.int32, (BLK, 2 * BLK), 0)
        kj = lax.broadcasted_iota(jnp.int32, (BLK, 2 * BLK), 1)
        dist = qi + BLK - kj
        window = (dist >= 0) & (dist < BLK)
        for h in range(N_HEADS):
            acc = jnp.zeros((BLK, 2 * BLK), F32)
            for b in range(N_BUCKETS):
                acc = jnp.where(bk == b, rel_ref[b, h], acc)
            out_ref[h // GROUP, pl.ds((h % GROUP) * BLK, BLK), :] = jnp.where(window, acc, NEG)

    return pl.pallas_call(
        body, name="bias_table",
        out_shape=_sds((N_KV, GROUP * BLK, 2 * BLK), F32),
        in_specs=[pl.BlockSpec(memory_space=pltpu.SMEM), pl.BlockSpec(memory_space=pltpu.VMEM)],
        out_specs=pl.BlockSpec(memory_space=pltpu.VMEM),
    )(rel_bias, bucket)


ATT_TB = 4


def _attn_scores(q, kvc, kvp, bias_ref, sink_ref, has_prev, kh, g0=0, ng=GROUP):
    k2 = jnp.concatenate([kvp[:, kh * HD:(kh + 1) * HD], kvc[:, kh * HD:(kh + 1) * HD]], axis=0)
    v2 = jnp.concatenate([kvp[:, KV_W + kh * HD:KV_W + (kh + 1) * HD],
                          kvc[:, KV_W + kh * HD:KV_W + (kh + 1) * HD]], axis=0)
    q4 = jnp.concatenate([q[:, (kh * GROUP + g) * HD:(kh * GROUP + g + 1) * HD] for g in range(g0, g0 + ng)], axis=0)
    s = _dot_nt(q4, k2) * SCALE + bias_ref[kh, g0 * BLK:(g0 + ng) * BLK, :]
    if has_prev is not None:
        col = lax.broadcasted_iota(jnp.int32, (ng * BLK, 2 * BLK), 1)
        s = jnp.where((col >= BLK) | has_prev, s, NEG)
    rowg = lax.broadcasted_iota(jnp.int32, (ng * BLK, 1), 0) // BLK
    sink = jnp.zeros((ng * BLK, 1), F32)
    for g in range(ng):
        sink = jnp.where(rowg == g, sink_ref[kh * GROUP + g0 + g], sink)
    return q4, k2, v2, s, sink


def _attn_fwd(qkv, bias, sinks):
    S = qkv.shape[0]
    tb = min(ATT_TB, S // BLK)
    T = tb * BLK

    def body(sink_ref, q_ref, kv_ref, kvp_ref, bias_ref, o_ref):
        step = pl.program_id(0)
        for j in range(tb):
            rows = slice(j * BLK, (j + 1) * BLK)
            q, kvc = q_ref[rows, :], kv_ref[rows, :]
            kvp = kvp_ref[...] if j == 0 else kv_ref[(j - 1) * BLK:j * BLK, :]
            has_prev = (step > 0) if j == 0 else None
            outs = []
            for kh in range(N_KV):
                q4, k2, v2, s, sink = _attn_scores(q, kvc, kvp, bias_ref, sink_ref, has_prev, kh)
                m = jnp.maximum(jnp.max(s, axis=1, keepdims=True), sink)
                p = jnp.exp(s - m)
                denom = jnp.sum(p, axis=1, keepdims=True) + jnp.exp(sink - m)
                o4 = _dot((p / denom).astype(BF), v2)
                outs += [o4[g * BLK:(g + 1) * BLK] for g in range(GROUP)]
            o_ref[rows, :] = jnp.concatenate(outs, axis=1).astype(BF)

    return pl.pallas_call(
        body, name="attn_fwd", grid=(S // T,),
        out_shape=_sds((S, Q_W), BF),
        in_specs=[pl.BlockSpec(memory_space=pltpu.SMEM),
                  pl.BlockSpec((T, Q_W), lambda i: (i, 0)),
                  pl.BlockSpec((T, 2 * KV_W), lambda i: (i, 2)),
                  pl.BlockSpec((BLK, 2 * KV_W), lambda i: (jnp.maximum(i * tb - 1, 0), 2)),
                  _const((N_KV, GROUP * BLK, 2 * BLK))],
        out_specs=pl.BlockSpec((T, Q_W), lambda i: (i, 0)),
        compiler_params=_cp(1, 32),
    )(sinks, qkv, qkv, qkv, bias)


def _attn_bwd(qkv, bias, sinks, do):
    S = qkv.shape[0]
    tb = 1
    ng = GROUP
    T = tb * BLK
    nt = S // T

    def body(sink_ref, q_ref, kv_ref, kvp_ref, bias_ref, do_ref, dq_ref, dkv_ref, dbias_ref, dsink_ref, carry):
        i = pl.program_id(0)

        @pl.when(i == 0)
        def _():
            carry[...] = jnp.zeros_like(carry)
            dbias_ref[...] = jnp.zeros_like(dbias_ref)
            dsink_ref[...] = jnp.zeros_like(dsink_ref)

        from_next = carry[...]
        for j in reversed(range(tb)):
            rows = slice(j * BLK, (j + 1) * BLK)
            q, kvc, do_ = q_ref[rows, :], kv_ref[rows, :], do_ref[rows, :]
            kvp = kvp_ref[...] if j == 0 else kv_ref[(j - 1) * BLK:j * BLK, :]
            has_prev = (i < nt - 1) if j == 0 else None
            dqs, dk_cur, dv_cur, dk_prev, dv_prev = [], [], [], [], []
            head_row = lax.broadcasted_iota(jnp.int32, (N_HEADS, 128), 0)
            dsink_rows = jnp.zeros((N_HEADS, 128), F32)
            for kh in range(N_KV):
                dk2, dv2 = None, None
                for g0 in range(0, GROUP, ng):
                    q4, k2, v2, s, sink = _attn_scores(q, kvc, kvp, bias_ref, sink_ref, has_prev, kh, g0, ng)
                    m = jnp.maximum(jnp.max(s, axis=1, keepdims=True), sink)
                    p = jnp.exp(s - m)
                    denom = jnp.sum(p, axis=1, keepdims=True) + jnp.exp(sink - m)
                    prob = p / denom
                    p_sink = jnp.exp(sink - m) / denom
                    pb = prob.astype(BF)
                    do4 = jnp.concatenate([do_[:, (kh * GROUP + g) * HD:(kh * GROUP + g + 1) * HD]
                                           for g in range(g0, g0 + ng)], axis=0)
                    dp = _dot_nt(do4, v2)
                    o4 = _dot(pb, v2)
                    delta = jnp.sum(do4.astype(F32) * o4, axis=1, keepdims=True)
                    ds = prob * (dp - delta)
                    dbias_ref[kh, g0 * BLK:(g0 + ng) * BLK, :] += ds
                    sink_term = p_sink * delta
                    for g in range(ng):
                        val = -jnp.sum(sink_term[g * BLK:(g + 1) * BLK], axis=0, keepdims=True)
                        dsink_rows = jnp.where(head_row == kh * GROUP + g0 + g, val, dsink_rows)
                    dsb = ds.astype(BF)
                    dq4 = _dot(dsb, k2) * SCALE
                    dk_part = jnp.transpose(_dot_tn(q4, dsb)) * SCALE
                    dv_part = jnp.transpose(_dot_tn(do4, pb))
                    dk2 = dk_part if dk2 is None else dk2 + dk_part
                    dv2 = dv_part if dv2 is None else dv2 + dv_part
                    dqs += [dq4[g * BLK:(g + 1) * BLK] for g in range(ng)]
                dk_prev.append(dk2[0:BLK])
                dk_cur.append(dk2[BLK:2 * BLK])
                dv_prev.append(dv2[0:BLK])
                dv_cur.append(dv2[BLK:2 * BLK])
            dsink_ref[...] += dsink_rows
            dq_ref[rows, :] = jnp.concatenate(dqs, axis=1).astype(BF)
            dkv_ref[rows, :] = (jnp.concatenate(dk_cur + dv_cur, axis=1) + from_next).astype(BF)
            from_next = jnp.concatenate(dk_prev + dv_prev, axis=1)
        carry[...] = from_next

    return pl.pallas_call(
        body, name="attn_bwd", grid=(nt,),
        out_shape=[_sds((S, Q_W), BF), _sds((S, 2 * KV_W), BF),
                   _sds((N_KV, GROUP * BLK, 2 * BLK), F32), _sds((N_HEADS, 128), F32)],
        in_specs=[pl.BlockSpec(memory_space=pltpu.SMEM),
                  pl.BlockSpec((T, Q_W), lambda i: (nt - 1 - i, 0)),
                  pl.BlockSpec((T, 2 * KV_W), lambda i: (nt - 1 - i, 2)),
                  pl.BlockSpec((BLK, 2 * KV_W), lambda i: (jnp.maximum((nt - 1 - i) * tb - 1, 0), 2)),
                  _const((N_KV, GROUP * BLK, 2 * BLK)),
                  pl.BlockSpec((T, Q_W), lambda i: (nt - 1 - i, 0))],
        out_specs=[pl.BlockSpec((T, Q_W), lambda i: (nt - 1 - i, 0)),
                   pl.BlockSpec((T, 2 * KV_W), lambda i: (nt - 1 - i, 0)),
                   _const((N_KV, GROUP * BLK, 2 * BLK)), _const((N_HEADS, 128))],
        scratch_shapes=[pltpu.VMEM((BLK, 2 * KV_W), F32)],
        compiler_params=_cp(1, 32),
    )(sinks, qkv, qkv, qkv, bias, do)


def _rel_bias_grad(dbias, bucket):
    def body(db_ref, bk_ref, out_ref):
        bk = bk_ref[...]
        lane = lax.broadcasted_iota(jnp.int32, (1, 128), 1)
        for h in range(N_HEADS):
            d = db_ref[h // GROUP, pl.ds((h % GROUP) * BLK, BLK), :]
            row = jnp.zeros((1, 128), F32)
            for b in range(N_BUCKETS):
                tot = jnp.sum(jnp.sum(jnp.where(bk == b, d, 0.0), axis=1, keepdims=True), axis=0, keepdims=True)
                row = jnp.where(lane == b, tot, row)
            out_ref[pl.ds(h, 1), :] = row

    vm = pl.BlockSpec(memory_space=pltpu.VMEM)
    return pl.pallas_call(body, name="rel_bias_grad", out_shape=_sds((N_HEADS, 128), F32),
                          in_specs=[vm, vm], out_specs=vm)(dbias, bucket)


def _gmlp_parts(zg, lg_ref, lb_ref):
    z = zg.astype(F32)
    ge = _gelu(z)
    u, vg = ge[:, 0:G_W], ge[:, G_W:2 * G_W]
    mu = jnp.mean(vg, axis=-1, keepdims=True)
    xc = vg - mu
    rstd = lax.rsqrt(jnp.mean(xc * xc, axis=-1, keepdims=True) + EPS)
    xh = xc * rstd
    return z, u, xh, rstd, xh * lg_ref[...] + lb_ref[...]


def _causal_weights(ws_ref, wc):
    t = lax.broadcasted_iota(jnp.int32, (BLK, BLK), 0)
    s = lax.broadcasted_iota(jnp.int32, (BLK, BLK), 1)
    for g in range(N_HEADS):
        wc[g] = jnp.where(s <= t, ws_ref[g], 0.0).astype(BF)


def _spatial(vb, wc, bst_ref, p, low):
    xp = vb[:, p * 128:(p + 1) * 128]
    s0 = _dot(wc[2 * p], xp) + bst_ref[:, 2 * p:2 * p + 1]
    s1 = _dot(wc[2 * p + 1], xp) + bst_ref[:, 2 * p + 1:2 * p + 2]
    return xp, jnp.where(low, s0, s1)


def _gmlp_fwd(zg, lg, lb, ws, bst):
    S = zg.shape[0]
    tb = min(ATT_TB, S // BLK)
    T = tb * BLK

    def body(zg_ref, lg_ref, lb_ref, ws_ref, bst_ref, o_ref, wc):
        @pl.when(pl.program_id(0) == 0)
        def _():
            _causal_weights(ws_ref, wc)
        low = lax.broadcasted_iota(jnp.int32, (BLK, 128), 1) < HD
        for j in range(tb):
            rows = slice(j * BLK, (j + 1) * BLK)
            _, u, _, _, vln = _gmlp_parts(zg_ref[rows, :], lg_ref, lb_ref)
            vb = vln.astype(BF)
            for p in range(4):
                _, sp = _spatial(vb, wc, bst_ref, p, low)
                o_ref[rows, p * 128:(p + 1) * 128] = (u[:, p * 128:(p + 1) * 128] * sp).astype(BF)

    return pl.pallas_call(
        body, name="gmlp_fwd", grid=(S // T,),
        out_shape=_sds((S, G_W), BF),
        in_specs=[pl.BlockSpec((T, 2 * G_W), lambda i: (i, 0)), _const((1, G_W)), _const((1, G_W)),
                  _const((N_HEADS, BLK, BLK)), _const((BLK, N_HEADS))],
        out_specs=pl.BlockSpec((T, G_W), lambda i: (i, 0)),
        scratch_shapes=[pltpu.VMEM((N_HEADS, BLK, BLK), BF)],
        compiler_params=_cp(1, 32),
    )(zg, lg, lb, ws, bst)


def _gmlp_bwd(zg, d_out, lg, lb, ws, bst):
    S = zg.shape[0]
    tb = min(ATT_TB, S // BLK)
    T = tb * BLK
    nb = S // T

    def body(zg_ref, d_ref, lg_ref, lb_ref, ws_ref, bst_ref, dzg_ref, dws_ref, dbs_ref, dlg_ref, dlb_ref, wc, dbacc):
        i = pl.program_id(0)

        @pl.when(i == 0)
        def _():
            _causal_weights(ws_ref, wc)
            dws_ref[...] = jnp.zeros_like(dws_ref)
            dlg_ref[...] = jnp.zeros_like(dlg_ref)
            dlb_ref[...] = jnp.zeros_like(dlb_ref)
            dbacc[...] = jnp.zeros_like(dbacc)

        low = lax.broadcasted_iota(jnp.int32, (BLK, 128), 1) < HD
        for j in range(tb):
            rows = slice(j * BLK, (j + 1) * BLK)
            z, u, xh, rstd, vln = _gmlp_parts(zg_ref[rows, :], lg_ref, lb_ref)
            vb = vln.astype(BF)
            d = d_ref[rows, :].astype(F32)
            du_parts, dvln_parts = [], []
            for p in range(4):
                xp, sp = _spatial(vb, wc, bst_ref, p, low)
                dp = d[:, p * 128:(p + 1) * 128]
                du_parts.append(dp * sp)
                dsp = dp * u[:, p * 128:(p + 1) * 128]
                dbacc[:, p * 128:(p + 1) * 128] += dsp
                d0 = jnp.where(low, dsp, 0.0).astype(BF)
                d1 = jnp.where(low, 0.0, dsp).astype(BF)
                dws_ref[2 * p] += _dot_nt(d0, xp)
                dws_ref[2 * p + 1] += _dot_nt(d1, xp)
                dvln_parts.append(_dot_tn(wc[2 * p], d0) + _dot_tn(wc[2 * p + 1], d1))
            dvln = jnp.concatenate(dvln_parts, axis=1)
            dlg_ref[...] += _colsum(dvln * xh)
            dlb_ref[...] += _colsum(dvln)
            dxh = dvln * lg_ref[...]
            dvg = rstd * (dxh - jnp.mean(dxh, axis=-1, keepdims=True)
                          - xh * jnp.mean(dxh * xh, axis=-1, keepdims=True))
            dge = jnp.concatenate(du_parts + [dvg], axis=1)
            dzg_ref[rows, :] = (dge * _gelu_grad(z)).astype(BF)

        @pl.when(i == nb - 1)
        def _():
            t = lax.broadcasted_iota(jnp.int32, (BLK, BLK), 0)
            s = lax.broadcasted_iota(jnp.int32, (BLK, BLK), 1)
            for g in range(N_HEADS):
                dws_ref[g] = jnp.where(s <= t, dws_ref[g], 0.0)
            grp = lax.broadcasted_iota(jnp.int32, (N_HEADS, G_W), 0)
            lane = lax.broadcasted_iota(jnp.int32, (N_HEADS, G_W), 1) // HD
            pick = jnp.where(grp == lane, 1.0, 0.0).astype(F32)
            dbs_ref[...] = lax.dot_general(pick, dbacc[...], (((1,), (1,)), ((), ())),
                                           preferred_element_type=F32, precision=HIGH)

    return pl.pallas_call(
        body, name="gmlp_bwd", grid=(nb,),
        out_shape=[_sds((S, 2 * G_W), BF), _sds((N_HEADS, BLK, BLK), F32), _sds((N_HEADS, BLK), F32),
                   _sds((1, G_W), F32), _sds((1, G_W), F32)],
        in_specs=[pl.BlockSpec((T, 2 * G_W), lambda i: (i, 0)), pl.BlockSpec((T, G_W), lambda i: (i, 0)),
                  _const((1, G_W)), _const((1, G_W)), _const((N_HEADS, BLK, BLK)), _const((BLK, N_HEADS))],
        out_specs=[pl.BlockSpec((T, 2 * G_W), lambda i: (i, 0)), _const((N_HEADS, BLK, BLK)),
                   _const((N_HEADS, BLK)), _const((1, G_W)), _const((1, G_W))],
        scratch_shapes=[pltpu.VMEM((N_HEADS, BLK, BLK), BF), pltpu.VMEM((BLK, G_W), F32)],
        compiler_params=_cp(1, 32),
    )(zg, d_out, lg, lb, ws, bst)


def _mix_out(o, gm, gates, h, wa, wg, wo, gate, gp):
    S = h.shape[0]
    R = min(512, S)

    def body(o_ref, gm_ref, gates_ref, h_ref, wa_ref, wg_ref, wo_ref, gate_ref, gp_ref,
             ya_ref, yg_ref, ym_ref, y_ref, hn_ref):
        for r0 in range(0, R, CHUNK):
            rows = slice(r0, r0 + CHUNK)
            ya = _dot(o_ref[rows, :], wa_ref[...])
            yg = _dot(gm_ref[rows, :], wg_ref[...])
            ya_ref[rows, :] = ya.astype(BF)
            yg_ref[rows, :] = yg.astype(BF)
            ym = (gates_ref[rows, 0:D].astype(F32) * ya + gates_ref[rows, D:2 * D].astype(F32) * yg).astype(BF)
            ym_ref[rows, :] = ym
            y = _dot(ym, wo_ref[...])
            y_ref[rows, :] = y.astype(BF)
            hn_ref[rows, :] = h_ref[rows, :] + gate_ref[...] * (y * _rms_r(y) * gp_ref[...])

    vec = _const((1, D))
    rows = lambda w_: pl.BlockSpec((R, w_), lambda i: (i, 0))
    return pl.pallas_call(
        body, name="mix_out", grid=(S // R,),
        out_shape=[_sds((S, D), BF)] * 4 + [_sds((S, D), F32)],
        in_specs=[rows(Q_W), rows(G_W), rows(2 * D), rows(D), _resident((Q_W, D)), _resident((G_W, D)),
                  _resident((D, D)), vec, vec],
        out_specs=[rows(D)] * 5,
        compiler_params=_cp(1, 48),
    )(o, gm, gates, h, wa, wg, wo, gate, gp)


def _mix_out_bwd(dh, y, ya, yg, gates, att, gm, ymix, wa, wg, wo, gate, gp):
    S = dh.shape[0]
    R = min(512, S)
    nb = S // R

    def body(dh_ref, y_ref, ya_ref, yg_ref, gates_ref, att_ref, gm_ref, ym_ref, wa_ref, wg_ref, wo_ref,
             gate_ref, gp_ref, dz_ref, do_ref, dgm_ref, dgate_ref, dgp_ref, gwo_ref, gwa_ref, gwg_ref,
             acc_o, acc_a, acc_g, dy_scr, dya_scr, dyg_scr):
        i = pl.program_id(0)

        @pl.when(i == 0)
        def _():
            for r in (dgate_ref, dgp_ref, acc_o, acc_a, acc_g):
                r[...] = jnp.zeros_like(r)
        for r0 in range(0, R, CHUNK):
            rows = slice(r0, r0 + CHUNK)
            dy, dgate, dgp = _postnorm_bwd(dh_ref[rows, :], y_ref[rows, :], gate_ref[...], gp_ref[...], 1.0)
            dgate_ref[...] += dgate
            dgp_ref[...] += dgp
            dyb = dy.astype(BF)
            dy_scr[rows, :] = dyb
            dym = _dot_nt(dyb, wo_ref[...])
            ga = gates_ref[rows, 0:D].astype(F32)
            gg = gates_ref[rows, D:2 * D].astype(F32)
            dya = (dym * ga).astype(BF)
            dyg = (dym * gg).astype(BF)
            dya_scr[rows, :] = dya
            dyg_scr[rows, :] = dyg
            dz_ref[rows, 0:D] = (dym * ya_ref[rows, :].astype(F32) * (ga * (1.0 - ga))).astype(BF)
            dz_ref[rows, D:2 * D] = (dym * yg_ref[rows, :].astype(F32) * (gg * (1.0 - gg))).astype(BF)
            do_ref[rows, :] = _dot_nt(dya, wa_ref[...]).astype(BF)
            dgm_ref[rows, :] = _dot_nt(dyg, wg_ref[...]).astype(BF)
        for m0 in range(0, D, CHUNK):
            acc_o[m0:m0 + CHUNK, :] += _dot_tn(ym_ref[:, m0:m0 + CHUNK], dy_scr[...])
        for m0 in range(0, Q_W, CHUNK):
            acc_a[m0:m0 + CHUNK, :] += _dot_tn(att_ref[:, m0:m0 + CHUNK], dya_scr[...])
            acc_g[m0:m0 + CHUNK, :] += _dot_tn(gm_ref[:, m0:m0 + CHUNK], dyg_scr[...])

        @pl.when(i == nb - 1)
        def _():
            for m0 in range(0, D, CHUNK):
                gwo_ref[m0:m0 + CHUNK, :] = acc_o[m0:m0 + CHUNK, :].astype(BF)
            for m0 in range(0, Q_W, CHUNK):
                gwa_ref[m0:m0 + CHUNK, :] = acc_a[m0:m0 + CHUNK, :].astype(BF)
                gwg_ref[m0:m0 + CHUNK, :] = acc_g[m0:m0 + CHUNK, :].astype(BF)

    vec = _const((1, D))
    rows = lambda w_: pl.BlockSpec((R, w_), lambda i: (i, 0))
    return pl.pallas_call(
        body, name="mix_out_bwd", grid=(nb,),
        out_shape=[_sds((S, 2 * D), BF), _sds((S, Q_W), BF), _sds((S, G_W), BF), _sds((1, D), F32),
                   _sds((1, D), F32), _sds((D, D), BF), _sds((Q_W, D), BF), _sds((G_W, D), BF)],
        in_specs=[rows(D), rows(D), rows(D), rows(D), rows(2 * D), rows(Q_W), rows(G_W), rows(D),
                  _resident((Q_W, D)), _resident((G_W, D)), _resident((D, D)), vec, vec],
        out_specs=[rows(2 * D), rows(Q_W), rows(G_W), vec, vec, _const((D, D)), _const((Q_W, D)),
                   _const((G_W, D))],
        scratch_shapes=[pltpu.VMEM((D, D), F32), pltpu.VMEM((Q_W, D), F32), pltpu.VMEM((G_W, D), F32)]
        + [pltpu.VMEM((R, D), BF)] * 3,
        compiler_params=_cp(1, 60),
    )(dh, y, ya, yg, gates, att, gm, ymix, wa, wg, wo, gate, gp)


def _mix_dn(dq, dkv, dzg, dzgate, w, h, dh, sc, gp):
    S = h.shape[0]
    R = min(512, S)

    def body(dq_ref, dkv_ref, dzg_ref, dzt_ref, w_ref, h_ref, dh_ref, sc_ref, gp_ref,
             out_ref, dsh_ref, dsc_ref, dgp_ref):
        @pl.when(pl.program_id(0) == 0)
        def _():
            dsh_ref[...] = jnp.zeros_like(dsh_ref)
            dsc_ref[...] = jnp.zeros_like(dsc_ref)
            dgp_ref[...] = jnp.zeros_like(dgp_ref)
        for r0 in range(0, R, CHUNK):
            rows = slice(r0, r0 + CHUNK)
            dn = _dot(dq_ref[rows, :], w_ref[0:Q_W, :])
            dn = dn + _dot(dkv_ref[rows, :], w_ref[Q_W:QKV_W, :])
            dn = dn + _dot(dzg_ref[rows, :], w_ref[ZG_OFF:GATE_OFF, :])
            dn = dn + _dot(dzt_ref[rows, :], w_ref[GATE_OFF:IN_W, :])
            dx, dsh, dsc, dgp = _prenorm_bwd(dn, h_ref[rows, :], gp_ref[...], sc_ref[...])
            out_ref[rows, :] = dh_ref[rows, :] + dx
            dsh_ref[...] += dsh
            dsc_ref[...] += dsc
            dgp_ref[...] += dgp

    vec = _const((1, D))
    rows = lambda w_: pl.BlockSpec((R, w_), lambda i: (i, 0))
    return pl.pallas_call(
        body, name="mix_dn", grid=(S // R,),
        out_shape=[_sds((S, D), F32)] + [_sds((1, D), F32)] * 3,
        in_specs=[rows(Q_W), rows(2 * KV_W), rows(2 * G_W), rows(2 * D), _resident((IN_W, D)),
                  rows(D), rows(D), vec, vec],
        out_specs=[rows(D), vec, vec, vec],
        compiler_params=_cp(1, 48),
    )(dq, dkv, dzg, dzgate, w, h, dh, sc, gp)


def _adamw_math(w, g, m, v):
    m2 = ADAM_B1 * m + (1.0 - ADAM_B1) * g
    v2 = ADAM_B2 * v + (1.0 - ADAM_B2) * (g * g)
    m_hat = m2 / (1.0 - ADAM_B1 ** ADAM_STEP)
    v_hat = v2 / (1.0 - ADAM_B2 ** ADAM_STEP)
    delta = -ADAM_LR * (m_hat / (jnp.sqrt(v_hat) + ADAM_EPS) + ADAM_WD * w)
    return delta, m2, v2


def _row_tile(rows, cols):
    best = None
    for t in range(16, rows + 1, 16):
        if rows % t == 0 and t * cols <= 256 * 1024:
            best = t
    return best if best is not None else rows


def _adamw_sharded(landing, w, m, v, name):
    r, c = w.shape
    tr = _row_tile(r, c)

    def body(l_ref, w_ref, m_ref, v_ref, g_ref, d_ref, m2_ref, v2_ref):
        g = l_ref[0].astype(F32)
        for j in range(1, N_DEV):
            g = g + l_ref[j].astype(F32)
        delta, m2, v2 = _adamw_math(w_ref[...], g, m_ref[...], v_ref[...])
        g_ref[...] = g
        d_ref[...] = delta
        m2_ref[...] = m2
        v2_ref[...] = v2

    row = pl.BlockSpec((tr, c), lambda i: (i, 0))
    return pl.pallas_call(
        body, name=name, grid=(r // tr,),
        out_shape=[_sds((r, c), F32)] * 4,
        in_specs=[pl.BlockSpec((N_DEV, tr, c), lambda i: (0, i, 0)), row, row, row],
        out_specs=[row] * 4,
        compiler_params=_cp(1, 48),
    )(landing, w, m, v)


def _adamw_small(w, g, m, v, name):
    def body(w_ref, g_ref, m_ref, v_ref, d_ref, m2_ref, v2_ref):
        delta, m2, v2 = _adamw_math(w_ref[...], g_ref[...], m_ref[...], v_ref[...])
        d_ref[...] = delta
        m2_ref[...] = m2
        v2_ref[...] = v2

    vm = pl.BlockSpec(memory_space=pltpu.VMEM)
    return pl.pallas_call(body, name=name, out_shape=[_sds(w.shape, F32)] * 3,
                          in_specs=[vm] * 4, out_specs=[vm] * 3)(w, g, m, v)


def _w_ada_update(c8, d_ada, w, m, v):
    tr = 256

    def body(c_ref, d_ref, w_ref, m_ref, v_ref, g_ref, dl_ref, m2_ref, v2_ref):
        cs = c_ref[...]
        cs = cs * jax.nn.sigmoid(cs)
        g = lax.dot_general(cs, d_ref[...], (((0,), (0,)), ((), ())), preferred_element_type=F32, precision=HIGH)
        delta, m2, v2 = _adamw_math(w_ref[...], g, m_ref[...], v_ref[...])
        g_ref[...] = g
        dl_ref[...] = delta
        m2_ref[...] = m2
        v2_ref[...] = v2

    row = pl.BlockSpec((tr, ADA_W), lambda i: (i, 0))
    return pl.pallas_call(
        body, name="w_ada_update", grid=(D // tr,),
        out_shape=[_sds((D, ADA_W), F32)] * 4,
        in_specs=[pl.BlockSpec((N_DEV, tr), lambda i: (0, i)), _const((N_DEV, ADA_W)), row, row, row],
        out_specs=[row] * 4,
        compiler_params=_cp(1, 40),
    )(c8, d_ada, w, m, v)


def _t5_bucket():
    qi = jnp.arange(BLK, dtype=jnp.int32)[:, None]
    kj = jnp.arange(2 * BLK, dtype=jnp.int32)[None, :]
    dist = jnp.maximum(qi + BLK - kj, 0)
    max_exact = N_BUCKETS // 2
    d_f = jnp.maximum(dist, max_exact).astype(F32)
    large = max_exact + (jnp.log(d_f / max_exact) / math.log(MAX_DISTANCE / max_exact)
                         * (N_BUCKETS - max_exact)).astype(jnp.int32)
    large = jnp.minimum(large, N_BUCKETS - 1)
    return jnp.where(dist < max_exact, dist, large)


def _slabs_of_columns(w):
    r, c8 = w.shape
    return jnp.transpose(w.reshape(r, N_DEV, c8 // N_DEV), (1, 0, 2))


def _columns_of_slabs(w8):
    _, r, c = w8.shape
    return jnp.transpose(w8, (1, 0, 2)).reshape(r, N_DEV * c)


def kernel(x, c, rel_bias, w_ada, b_ada, pre_norm_g, post_norm_g, w_ffn1_in, w_ffn1_out, w_in, sinks, gmlp_ln_g, gmlp_ln_b, gmlp_w_s, gmlp_b_s, w_br_attn, w_br_gmlp, w_out, w_ffn2_in, w_ffn2_out, loss_target, m_rel_bias, m_w_ada, m_b_ada, m_pre_norm_g, m_post_norm_g, m_w_ffn1_in, m_w_ffn1_out, m_w_in, m_sinks, m_gmlp_ln_g, m_gmlp_ln_b, m_gmlp_w_s, m_gmlp_b_s, m_w_br_attn, m_w_br_gmlp, m_w_out, m_w_ffn2_in, m_w_ffn2_out, v_rel_bias, v_w_ada, v_b_ada, v_pre_norm_g, v_post_norm_g, v_w_ffn1_in, v_w_ffn1_out, v_w_in, v_sinks, v_gmlp_ln_g, v_gmlp_ln_b, v_gmlp_w_s, v_gmlp_b_s, v_w_br_attn, v_w_br_gmlp, v_w_out, v_w_ffn2_in, v_w_ffn2_out):
    me = 4 * lax.axis_index("x") + 2 * lax.axis_index("y") + lax.axis_index("c")
    x0 = x[0]
    target = loss_target[0]

    transposed = ("w_ffn1_in", "w_in", "w_ffn2_in")
    shards = [w_ffn1_in[0].T, w_ffn1_out[0], w_in[0].T, w_br_attn[0], w_br_gmlp[0], w_out[0],
              w_ffn2_in[0].T, w_ffn2_out[0]]
    shards_bf = [s.astype(BF) for s in shards]
    groups = [shards_bf[0:1], shards_bf[1:6], shards_bf[6:8]]

    def gather_start(i, after):
        return _slabs_start("gather", groups[i], after, "gather_start_%d" % i)

    def forward_start(st, i, after):
        lands = _slabs_wait("gather", len(groups[i]), st, after, "gather_wait_%d" % i)
        return _slabs_start("forward", lands, c, "forward_start_%d" % i)

    def gathered(st, i, after):
        return _slabs_wait("forward", len(groups[i]), st, after, "forward_wait_%d" % i)

    gs0 = gather_start(0, c)

    small = jnp.concatenate([c[0], pre_norm_g[0].reshape(-1), post_norm_g[0].reshape(-1)])
    small8 = jnp.broadcast_to(small[None, :], (8, small.shape[0]))
    b_ada64 = jnp.repeat(b_ada.reshape(N_DEV, ADA_W), 8, axis=0)
    gath, ada64 = _ada_forward(small8, w_ada[0], b_ada64)
    gath8 = gath[::8]
    ada = ada64[::8].reshape(9, D)
    sh1, sc1, g1, sh2, sc2, g2, sh3, sc3, g3 = [ada[k:k + 1] for k in range(9)]
    gains = gath8[:, D:].reshape(N_DEV, 2, 3, 128)
    pre_g = jnp.transpose(gains[:, 0], (1, 0, 2)).reshape(3, D)
    post_g = jnp.transpose(gains[:, 1], (1, 0, 2)).reshape(3, D)
    pre = [pre_g[k:k + 1] for k in range(3)]
    post = [post_g[k:k + 1] for k in range(3)]

    bucket = _t5_bucket()
    bias = _bias_table(rel_bias, bucket)
    sinks8 = sinks[0]
    lg, lb = gmlp_ln_g, gmlp_ln_b
    ws = gmlp_w_s[0]
    bst = jnp.transpose(gmlp_b_s[0])

    fs0 = forward_start(gs0, 0, sh1)
    gs1 = gather_start(1, fs0[-1])
    wf1_in = gathered(fs0, 0, gs1[-1])[0].reshape(2 * D_FF, D)
    n1, fg1, fu1, fa1 = _ffn_in(x0, sh1, sc1, pre[0], wf1_in, "ffn1_in")
    fs1 = forward_start(gs1, 1, n1)
    gs2 = gather_start(2, fs1[-1])
    mix_w = gathered(fs1, 1, gs2[-1])
    wf1_out = mix_w[0].reshape(D_FF, D)
    w_in_full = mix_w[1].reshape(IN_W, D)
    w_bra = _columns_of_slabs(mix_w[2])
    w_brg = _columns_of_slabs(mix_w[3])
    w_out_full = mix_w[4].reshape(D, D)
    h1, y1 = _ffn_out(fa1, wf1_out, x0, g1, post[0], "ffn1_out")
    n2, qkv, zg, gates = _mix_in(h1, sh2, sc2, pre[1], w_in_full)
    att = _attn_fwd(qkv, bias, sinks8)
    gm = _gmlp_fwd(zg, lg, lb, ws, bst)
    fs2 = forward_start(gs2, 2, gm)
    ya, yg, ymix, y2, h2 = _mix_out(att, gm, gates, h1, w_bra, w_brg, w_out_full, g2 + fs2[-1], post[1])
    wf2_in, wf2_out = gathered(fs2, 2, h2)
    wf2_in = wf2_in.reshape(2 * D_FF, D)
    wf2_out = wf2_out.reshape(D_FF, D)
    n3, fg3, fu3, fa3 = _ffn_in(h2, sh3, sc3, pre[2], wf2_in, "ffn2_in")
    dh3, y3, sq = _ffn_out(fa3, wf2_out, h2, g3, post[2], "ffn2_out", target=target)
    loss = lax.psum(0.5 * sq[0, 0] / D, ("x", "y", "c"))

    def exchange_start(i, arrays):
        return _slabs_start("exchange", arrays, sq, "exchange_start_%d" % i)

    dy3, dgu3, dh2, d_g3, d_post2, d_sh3, d_sc3, d_pre2 = _ffn_bwd(
        dh3, y3, fg3, fu3, wf2_out, wf2_in, h2, g3, post[2], sc3, pre[2], "ffn2_bwd")
    gw_f2_out = _tn_matmul(fa3, dy3, "ffn2_out_wgrad", tm=D_FF // 2).reshape(N_DEV, D_FF // N_DEV, D)
    ex0 = exchange_start(0, [gw_f2_out])
    gw_f2_in = _tn_matmul(dgu3, n3, "ffn2_in_wgrad", tm=D_FF // 2).reshape(N_DEV, FS, D)
    ex1 = exchange_start(1, [gw_f2_in])

    dzgate, d_att, d_gm, d_g2, d_post1, gw_out, gw_bra, gw_brg = _mix_out_bwd(
        dh2, y2, ya, yg, gates, att, gm, ymix, w_bra, w_brg, w_out_full, g2 + ex0[-1] + ex1[-1], post[1])
    ex2 = exchange_start(2, [_slabs_of_columns(gw_bra), _slabs_of_columns(gw_brg),
                             gw_out.reshape(N_DEV, D // N_DEV, D)])
    dq, dkv, dbias, dsink = _attn_bwd(qkv, bias, sinks8, d_att)
    dzg, d_ws, d_bs, d_lg, d_lb = _gmlp_bwd(zg, d_gm, lg, lb, ws, bst)
    d_rel = _rel_bias_grad(dbias, bucket)
    early = jnp.concatenate([
        jnp.concatenate([d_lg.reshape(4, 128), d_lb.reshape(4, 128)], axis=0),
        d_bs, d_rel, dsink, d_ws.reshape(N_HEADS * BLK, BLK)], axis=0)
    sm0 = _slabs_start("gather_all", [early], sq, "small_gather_start")
    dh1, d_sh2, d_sc2, d_pre1 = _mix_dn(dq, dkv, dzg, dzgate, w_in_full, h1, dh2, sc2 + ex2[-1] + sm0[-1], pre[1])
    gw_in = jnp.concatenate(
        [_tn_matmul(dq, n2, "w_in_q_wgrad"), _tn_matmul(dkv, n2, "w_in_kv_wgrad"),
         _tn_matmul(dzg, n2, "w_in_zg_wgrad"), _tn_matmul(dzgate, n2, "w_in_gate_wgrad")],
        axis=0).reshape(N_DEV, IN_W // N_DEV, D)
    ex3 = exchange_start(3, [gw_in])

    dy1, dgu1, d_g1, d_post0 = _ffn_out_bwd(dh1, y1, fg1, fu1, wf1_out, g1 + ex3[-1], post[0], "ffn1_out_bwd")
    gw_f1_out = _tn_matmul(fa1, dy1, "ffn1_out_wgrad", tm=D_FF // 2).reshape(N_DEV, D_FF // N_DEV, D)
    gw_f1_in = _tn_matmul(dgu1, n1, "ffn1_in_wgrad", tm=D_FF // 2).reshape(N_DEV, FS, D)
    ex4 = exchange_start(4, [gw_f1_out, gw_f1_in])
    grad_x, d_sh1, d_sc1, d_pre0 = _ffn_dn(dgu1, wf1_in, x0, dh1, sc1 + ex4[-1], pre[0], "ffn1_dn")

    landed = {}
    for i, (ex, nms) in enumerate([(ex0, ["w_ffn2_out"]), (ex1, ["w_ffn2_in"]),
                                   (ex2, ["w_br_attn", "w_br_gmlp", "w_out"]), (ex3, ["w_in"]),
                                   (ex4, ["w_ffn1_out", "w_ffn1_in"])]):
        for nm, land in zip(nms, _slabs_wait("exchange", len(nms), ex, grad_x, "exchange_wait_%d" % i)):
            landed[nm] = land
    moments = [(m_w_ffn1_in, v_w_ffn1_in), (m_w_ffn1_out, v_w_ffn1_out), (m_w_in, v_w_in),
               (m_w_br_attn, v_w_br_attn), (m_w_br_gmlp, v_w_br_gmlp), (m_w_out, v_w_out),
               (m_w_ffn2_in, v_w_ffn2_in), (m_w_ffn2_out, v_w_ffn2_out)]
    names = ["w_ffn1_in", "w_ffn1_out", "w_in", "w_br_attn", "w_br_gmlp", "w_out", "w_ffn2_in", "w_ffn2_out"]
    big = {}
    for nm, w_, (m_, v_) in zip(names, shards, moments):
        if nm in transposed:
            res4 = _adamw_sharded(landed[nm], w_, m_[0].T, v_[0].T, "adamw_" + nm)
            big[nm] = [a.T[None] for a in res4]
        else:
            big[nm] = [a[None] for a in _adamw_sharded(landed[nm], w_, m_[0], v_[0], "adamw_" + nm)]

    d_ada = jnp.concatenate([v_.reshape(8, 128) for v_ in
                             (d_sh1, d_sc1, d_g1, d_sh2, d_sc2, d_g2, d_sh3, d_sc3, d_g3)], axis=0)
    d_pre = jnp.concatenate([d_pre0, d_pre1, d_pre2], axis=0)
    d_post = jnp.concatenate([d_post0, d_post1, d_post2], axis=0)
    late = jnp.concatenate([d_ada, _slabs_of_columns(d_pre).reshape(24, 128),
                            _slabs_of_columns(d_post).reshape(24, 128)], axis=0)
    tot, every = _small_allreduce(late)
    (early_land,) = _slabs_wait("gather_all", 1, sm0, grad_x, "small_gather_wait")
    tot_early = _sum_slabs(early_land)

    g_b_ada = tot[0:72].reshape(1, 9 * D)
    g_pre = lax.dynamic_slice_in_dim(tot[72:96], 3 * me, 3, axis=0)[None]
    g_post = lax.dynamic_slice_in_dim(tot[96:120], 3 * me, 3, axis=0)[None]
    g_lg = tot_early[0:4].reshape(1, G_W)
    g_lb = tot_early[4:8].reshape(1, G_W)
    g_bs = tot_early[8:16][None]
    g_rel = jnp.transpose(tot_early[16:24, 0:N_BUCKETS])
    g_sinks = tot_early[24:32, 0][None]
    g_ws = tot_early[32:1056].reshape(1, N_HEADS, BLK, BLK)

    d_ada_mine = lax.dynamic_slice_in_dim(every[:, 0:72].reshape(N_DEV, N_DEV, ADA_W), me, 1, axis=1)[:, 0]
    ada_out = [a[None] for a in _w_ada_update(gath8[:, 0:D], d_ada_mine, w_ada[0], m_w_ada[0], v_w_ada[0])]

    def small_step(w_, g_, m_, v_, nm):
        shp = w_.shape
        two_d = (int(math.prod(shp[:-1])), shp[-1])
        d_, m2_, v2_ = _adamw_small(w_.reshape(two_d), g_.reshape(two_d), m_.reshape(two_d), v_.reshape(two_d),
                                    "adamw_" + nm)
        return [g_, d_.reshape(shp), m2_.reshape(shp), v2_.reshape(shp)]

    res = {
        "rel_bias": small_step(rel_bias, g_rel, m_rel_bias, v_rel_bias, "rel_bias"),
        "w_ada": ada_out,
        "b_ada": small_step(b_ada, g_b_ada, m_b_ada, v_b_ada, "b_ada"),
        "pre_norm_g": small_step(pre_norm_g, g_pre, m_pre_norm_g, v_pre_norm_g, "pre_norm_g"),
        "post_norm_g": small_step(post_norm_g, g_post, m_post_norm_g, v_post_norm_g, "post_norm_g"),
        "sinks": small_step(sinks, g_sinks, m_sinks, v_sinks, "sinks"),
        "gmlp_ln_g": small_step(gmlp_ln_g, g_lg, m_gmlp_ln_g, v_gmlp_ln_g, "gmlp_ln_g"),
        "gmlp_ln_b": small_step(gmlp_ln_b, g_lb, m_gmlp_ln_b, v_gmlp_ln_b, "gmlp_ln_b"),
        "gmlp_w_s": small_step(gmlp_w_s, g_ws, m_gmlp_w_s, v_gmlp_w_s, "gmlp_w_s"),
        "gmlp_b_s": small_step(gmlp_b_s, g_bs, m_gmlp_b_s, v_gmlp_b_s, "gmlp_b_s"),
    }
    res.update(big)
    order = ["rel_bias", "w_ada", "b_ada", "pre_norm_g", "post_norm_g", "w_ffn1_in", "w_ffn1_out", "w_in", "sinks",
             "gmlp_ln_g", "gmlp_ln_b", "gmlp_w_s", "gmlp_b_s", "w_br_attn", "w_br_gmlp", "w_out", "w_ffn2_in",
             "w_ffn2_out"]
    outs = [loss, grad_x[None]]
    for k in range(4):
        outs += [res[nm][k] for nm in order]
    return tuple(outs)
```

```python
import functools
import math

import jax
import jax.numpy as jnp
from jax import lax
from jax.experimental import pallas as pl
from jax.experimental.pallas import tpu as pltpu

F32 = jnp.float32
BF = jnp.bfloat16

N_DEV = 8
D = 1024
D_FF = 2816
FS = D_FF // 4
N_HEADS = 8
N_KV = 2
GROUP = 4
HD = 64
BLK = 128
Q_W = 512
KV_W = 128
G_W = 512
QKV_W = Q_W + 2 * KV_W
ZG_OFF = QKV_W
GATE_OFF = ZG_OFF + 2 * G_W
IN_W = GATE_OFF + 2 * D
N_BUCKETS = 32
MAX_DISTANCE = 128
EPS = 1e-6
NEG = -1e30
SCALE = HD ** -0.5
ADA_W = 9 * D // N_DEV

ADAM_LR = 0.001
ADAM_B1 = 0.9
ADAM_B2 = 0.999
ADAM_EPS = 1e-08
ADAM_WD = 0.01
ADAM_STEP = 10

CHUNK = 256
MIB = 1024 * 1024
MESH = pl.DeviceIdType.MESH
HIGH = lax.Precision.HIGHEST


def _cp(n_grid, vmem_mib):
    return pltpu.CompilerParams(dimension_semantics=("arbitrary",) * n_grid,
                                vmem_limit_bytes=vmem_mib * MIB)


def _const(shape):
    return pl.BlockSpec(shape, lambda *_: (0,) * len(shape))


def _resident(shape):
    return pl.BlockSpec(shape, lambda *_: (0,) * len(shape), pipeline_mode=pl.Buffered(1))


def _sds(shape, dtype):
    return jax.ShapeDtypeStruct(shape, dtype)


def _dot(a, b):
    return jnp.dot(a, b, preferred_element_type=F32)


def _dot_nt(a, b):
    return lax.dot_general(a, b, (((1,), (1,)), ((), ())), preferred_element_type=F32)


def _dot_tn(a, b):
    return lax.dot_general(a, b, (((0,), (0,)), ((), ())), preferred_element_type=F32)


def _rms_r(x):
    return lax.rsqrt(jnp.mean(x * x, axis=-1, keepdims=True) + EPS)


def _colsum(x):
    return jnp.sum(x, axis=0, keepdims=True)


def _prenorm(x, gp, sc, sh):
    return (x * _rms_r(x) * gp) * (1.0 + sc) + sh


def _prenorm_bwd(dn, x, gp, sc):
    r = _rms_r(x)
    xh = x * r
    t = dn * (1.0 + sc) * gp
    dx = r * (t - xh * jnp.mean(t * xh, axis=-1, keepdims=True))
    return dx, _colsum(dn), _colsum(dn * xh * gp), _colsum(dn * (1.0 + sc) * xh)


def _postnorm_bwd(dh, y, gate, gp, res):
    y = y.astype(F32)
    r = _rms_r(y)
    yh = y * r
    dyn = (res * gate) * dh
    t = dyn * gp
    dy = r * (t - yh * jnp.mean(t * yh, axis=-1, keepdims=True))
    return dy, _colsum(res * dh * yh * gp), _colsum(dyn * yh)


def _gelu(x):
    k = math.sqrt(2.0 / math.pi)
    return 0.5 * x * (1.0 + jnp.tanh(k * (x + 0.044715 * x * x * x)))


def _gelu_grad(x):
    k = math.sqrt(2.0 / math.pi)
    t = jnp.tanh(k * (x + 0.044715 * x * x * x))
    return 0.5 * (1.0 + t) + 0.5 * x * (1.0 - t * t) * (k * (1.0 + 3.0 * 0.044715 * x * x))


def _my_place():
    x, y, c = lax.axis_index("x"), lax.axis_index("y"), lax.axis_index("c")
    return x, y, c, 4 * x + 2 * y + c


def _peer(x, y, c, k):
    px = 1 - x if k & 4 else x
    py = 1 - y if k & 2 else y
    pc = 1 - c if k & 1 else c
    return (px, py, pc), 4 * px + 2 * py + pc


HBM_SPEC = pl.BlockSpec(memory_space=pltpu.HBM)
SEM_SPEC = pl.BlockSpec(memory_space=pltpu.SEMAPHORE)
EFFECT = pltpu.SideEffectType.DATAFLOW_SIDE_EFFECTING


RELATIONS = {"exchange": (1, 2, 3, 4, 5, 6, 7), "gather": (1, 2, 4, 6), "forward": (2, 4, 6),
             "gather_all": (1, 2, 3, 4, 5, 6, 7)}


def _slab_copies(mode, srcs, lands, send, recv, loc):
    x, y, c, me = _my_place()
    rel = RELATIONS[mode]
    remote, local = [], []
    for t in range(len(lands)):
        for i, k in enumerate(rel):
            peer, peer_lin = _peer(x, y, c, k)
            if mode == "exchange":
                src, dst, to = srcs[t].at[peer_lin], lands[t].at[me], peer
            elif mode in ("gather", "gather_all"):
                src, dst, to = srcs[t], lands[t].at[me], peer
            else:
                src, dst, to = lands[t].at[peer_lin], lands[t].at[peer_lin], _peer(x, y, c, 1)[0]
            remote.append(pltpu.make_async_remote_copy(
                src_ref=src, dst_ref=dst, send_sem=send.at[t * len(rel) + i], recv_sem=recv.at[t * len(rel) + i],
                device_id=to, device_id_type=MESH))
        if mode == "exchange":
            local.append(pltpu.make_async_copy(srcs[t].at[me], lands[t].at[me], loc.at[t]))
        elif mode in ("gather", "gather_all"):
            local.append(pltpu.make_async_copy(srcs[t], lands[t].at[me], loc.at[t]))
    return remote, local


def _slabs_start(mode, arrays, after, name):
    n = len(arrays)
    if mode == "forward":
        thru = list(arrays)
    else:
        shapes = [a.shape if mode == "exchange" else (N_DEV,) + a.shape for a in arrays]
        thru = list(arrays) + [lax.empty(s, a.dtype) for s, a in zip(shapes, arrays)]
    m = len(thru)
    n_sem = n * len(RELATIONS[mode])

    def body(*refs):
        srcs, lands = refs[:n], refs[m - n:m]
        send, recv, loc = refs[m + 1:m + 4]
        remote, local = _slab_copies(mode, srcs, lands, send, recv, loc)
        for cp in remote + local:
            cp.start()
        refs[-1][...] = jnp.zeros_like(refs[-1])

    return pl.pallas_call(
        body, name=name,
        out_shape=(pltpu.SemaphoreType.DMA((n_sem,)), pltpu.SemaphoreType.DMA((n_sem,)),
                   pltpu.SemaphoreType.DMA((n,)),
                   *[pltpu.HBM(a.shape, a.dtype) for a in thru],
                   _sds((1, D), F32)),
        in_specs=[HBM_SPEC] * m + [pl.BlockSpec(memory_space=pl.ANY)],
        out_specs=(SEM_SPEC, SEM_SPEC, SEM_SPEC, *[HBM_SPEC] * m, pl.BlockSpec(memory_space=pltpu.VMEM)),
        input_output_aliases={t: 3 + t for t in range(m)},
        compiler_params=pltpu.CompilerParams(has_side_effects=EFFECT),
    )(*[pltpu.with_memory_space_constraint(a, pltpu.HBM) for a in thru], after)


def _slabs_wait(mode, n, started, after, name):
    sems = started[0:3]
    thru = started[3:-1]
    m = len(thru)

    def body(*refs):
        srcs, lands = refs[:n], refs[m - n:m]
        remote, local = _slab_copies(mode, srcs, lands, *refs[m:m + 3])
        for cp in remote:
            cp.wait_send()
            cp.wait_recv()
        for cp in local:
            cp.wait()

    res = pl.pallas_call(
        body, name=name,
        out_shape=tuple(pltpu.HBM(a.shape, a.dtype) for a in thru),
        in_specs=[HBM_SPEC] * m + [SEM_SPEC] * 3 + [pl.BlockSpec(memory_space=pl.ANY)],
        out_specs=tuple([HBM_SPEC] * m),
        input_output_aliases={t: t for t in range(m)},
        compiler_params=pltpu.CompilerParams(has_side_effects=EFFECT),
    )(*thru, *sems, after)
    return list(res[m - n:m])


def _ada_forward(small8, w_ada, b_ada64):
    sw = small8.shape[1]

    def body(sm_ref, w_ref, b_ref, gath_ref, ada_ref, part_ref, send1, recv1, send2, recv2):
        x, y, c, me = _my_place()
        row_me = pl.multiple_of(me * 8, 8)
        gath_ref[pl.ds(row_me, 8), :] = sm_ref[...]
        first = []
        for k in range(1, N_DEV):
            peer, _ = _peer(x, y, c, k)
            cp = pltpu.make_async_remote_copy(
                src_ref=sm_ref, dst_ref=gath_ref.at[pl.ds(row_me, 8), :], send_sem=send1.at[k - 1],
                recv_sem=recv1.at[k - 1], device_id=peer, device_id_type=MESH)
            cp.start()
            first.append(cp)
        for cp in first:
            cp.wait()
        cs = gath_ref[:, 0:D]
        cs = cs * jax.nn.sigmoid(cs)
        part_ref[...] = jnp.dot(cs, w_ref[...], preferred_element_type=F32, precision=HIGH)
        ada_ref[pl.ds(row_me, 8), :] = part_ref[pl.ds(row_me, 8), :]
        second = []
        for k in range(1, N_DEV):
            peer, peer_lin = _peer(x, y, c, k)
            cp = pltpu.make_async_remote_copy(
                src_ref=part_ref.at[pl.ds(pl.multiple_of(peer_lin * 8, 8), 8), :],
                dst_ref=ada_ref.at[pl.ds(row_me, 8), :], send_sem=send2.at[k - 1],
                recv_sem=recv2.at[k - 1], device_id=peer, device_id_type=MESH)
            cp.start()
            second.append(cp)
        for cp in second:
            cp.wait()
        ada_ref[...] = ada_ref[...] + b_ref[...]

    vm = pl.BlockSpec(memory_space=pltpu.VMEM)
    return pl.pallas_call(
        body, name="ada_forward",
        out_shape=[_sds((8 * N_DEV, sw), F32), _sds((8 * N_DEV, ADA_W), F32)],
        in_specs=[vm, vm, vm], out_specs=[vm, vm],
        scratch_shapes=[pltpu.VMEM((8 * N_DEV, ADA_W), F32)] + [pltpu.SemaphoreType.DMA((7,))] * 4,
        compiler_params=pltpu.CompilerParams(vmem_limit_bytes=32 * MIB),
    )(small8, w_ada, b_ada64)


def _sum_slabs(land):
    def body(l_ref, o_ref):
        acc = l_ref[0]
        for j in range(1, N_DEV):
            acc = acc + l_ref[j]
        o_ref[...] = acc

    vm = pl.BlockSpec(memory_space=pltpu.VMEM)
    return pl.pallas_call(body, name="sum_slabs", out_shape=_sds(land.shape[1:], F32), in_specs=[vm], out_specs=vm,
                          compiler_params=pltpu.CompilerParams(vmem_limit_bytes=32 * MIB))(land)


def _small_allreduce(pack):
    rows = pack.shape[0]

    def body(p_ref, sum_ref, gath_ref, send, recv):
        x, y, c, me = _my_place()
        gath_ref[me] = p_ref[...]
        cps = []
        for k in range(1, N_DEV):
            peer, _ = _peer(x, y, c, k)
            cp = pltpu.make_async_remote_copy(
                src_ref=p_ref, dst_ref=gath_ref.at[me], send_sem=send.at[k - 1],
                recv_sem=recv.at[k - 1], device_id=peer, device_id_type=MESH)
            cp.start()
            cps.append(cp)
        for cp in cps:
            cp.wait()
        acc = gath_ref[0]
        for j in range(1, N_DEV):
            acc = acc + gath_ref[j]
        sum_ref[...] = acc

    vm = pl.BlockSpec(memory_space=pltpu.VMEM)
    return pl.pallas_call(
        body, name="small_allreduce",
        out_shape=[_sds((rows, 128), F32), _sds((N_DEV, rows, 128), F32)],
        in_specs=[vm], out_specs=[vm, vm],
        scratch_shapes=[pltpu.SemaphoreType.DMA((7,)), pltpu.SemaphoreType.DMA((7,))],
        compiler_params=pltpu.CompilerParams(vmem_limit_bytes=40 * MIB),
    )(pack)


F_TILES = tuple((f0, min(512, D_FF - f0)) for f0 in range(0, D_FF, 512))
F_TILES_NARROW = tuple((f0, 256) for f0 in range(0, D_FF, 256))


def _swiglu_tile(n, wt_ref, f0, tf):
    g = _dot_nt(n, wt_ref[f0:f0 + tf, :])
    u = _dot_nt(n, wt_ref[D_FF + f0:D_FF + f0 + tf, :])
    sg = jax.nn.sigmoid(g)
    silu = g * sg
    return (u * (sg * (1.0 + g * (1.0 - sg)))).astype(BF), silu.astype(BF), (silu * u).astype(BF)


def _ffn_in(h, sh, sc, gp, wt, name):
    S = h.shape[0]
    R = min(512, S)

    def body(h_ref, sh_ref, sc_ref, gp_ref, w_ref, n_ref, dg_ref, sl_ref, a_ref):
        for r0 in range(0, R, CHUNK):
            rows = slice(r0, r0 + CHUNK)
            n = _prenorm(h_ref[rows, :], gp_ref[...], sc_ref[...], sh_ref[...]).astype(BF)
            n_ref[rows, :] = n
            for f0, tf in F_TILES_NARROW:
                dg_ref[rows, f0:f0 + tf], sl_ref[rows, f0:f0 + tf], a_ref[rows, f0:f0 + tf] = _swiglu_tile(
                    n, w_ref, f0, tf)

    vec = _const((1, D))
    rows_ = lambda w_: pl.BlockSpec((R, w_), lambda i: (i, 0))
    return pl.pallas_call(
        body, name=name, grid=(S // R,),
        out_shape=[_sds((S, D), BF)] + [_sds((S, D_FF), BF)] * 3,
        in_specs=[rows_(D), vec, vec, vec, _resident((2 * D_FF, D))],
        out_specs=[rows_(D), rows_(D_FF), rows_(D_FF), rows_(D_FF)],
        compiler_params=_cp(1, 56),
    )(h, sh, sc, gp, wt)


def _ffn_out(a, w, h, gate, gp, name, target=None):
    S = h.shape[0]
    R = min(512, S)
    with_loss = target is not None

    def body(a_ref, w_ref, h_ref, gate_ref, gp_ref, *rest):
        if with_loss:
            t_ref, out_ref, y_ref, tot_ref = rest

            @pl.when(pl.program_id(0) == 0)
            def _():
                tot_ref[...] = jnp.zeros_like(tot_ref)
        else:
            out_ref, y_ref = rest
        for r0 in range(0, R, CHUNK):
            rows = slice(r0, r0 + CHUNK)
            y = _dot(a_ref[rows, :], w_ref[...])
            y_ref[rows, :] = y.astype(BF)
            hn = h_ref[rows, :] + (0.5 * gate_ref[...]) * (y * _rms_r(y) * gp_ref[...])
            if with_loss:
                e = hn - t_ref[rows, :]
                out_ref[rows, :] = e * (1.0 / D)
                tot_ref[...] += jnp.sum(jnp.sum(e * e, axis=1, keepdims=True), axis=0, keepdims=True)
            else:
                out_ref[rows, :] = hn

    vec = _const((1, D))
    rows_ = lambda w_: pl.BlockSpec((R, w_), lambda i: (i, 0))
    return pl.pallas_call(
        body, name=name, grid=(S // R,),
        out_shape=[_sds((S, D), F32), _sds((S, D), BF)] + ([_sds((1, 1), F32)] if with_loss else []),
        in_specs=[rows_(D_FF), _resident((D_FF, D)), rows_(D), vec, vec] + ([rows_(D)] if with_loss else []),
        out_specs=[rows_(D), rows_(D)] + ([_const((1, 1))] if with_loss else []),
        compiler_params=_cp(1, 48),
    )(*((a, w, h, gate, gp) + ((target,) if with_loss else ())))


def _ffn_out_bwd(dh, y, dsilu_u, silu, w, gate, gp, name):
    S = dh.shape[0]
    R = min(512, S)

    def body(dh_ref, y_ref, g_ref, u_ref, w_ref, gate_ref, gp_ref, dy_ref, dgu_ref, dgate_ref, dgp_ref):
        @pl.when(pl.program_id(0) == 0)
        def _():
            dgate_ref[...] = jnp.zeros_like(dgate_ref)
            dgp_ref[...] = jnp.zeros_like(dgp_ref)
        for r0 in range(0, R, CHUNK):
            rows = slice(r0, r0 + CHUNK)
            dy, dgate, dgp = _postnorm_bwd(dh_ref[rows, :], y_ref[rows, :], gate_ref[...], gp_ref[...], 0.5)
            dgate_ref[...] += dgate
            dgp_ref[...] += dgp
            dyb = dy.astype(BF)
            dy_ref[rows, :] = dyb
            for f0, tf in F_TILES:
                da = _dot_nt(dyb, w_ref[f0:f0 + tf, :])
                dgu_ref[rows, f0:f0 + tf] = (da * g_ref[rows, f0:f0 + tf].astype(F32)).astype(BF)
                dgu_ref[rows, D_FF + f0:D_FF + f0 + tf] = (da * u_ref[rows, f0:f0 + tf].astype(F32)).astype(BF)

    vec = _const((1, D))
    rows_ = lambda w_: pl.BlockSpec((R, w_), lambda i: (i, 0))
    return pl.pallas_call(
        body, name=name, grid=(S // R,),
        out_shape=[_sds((S, D), BF), _sds((S, 2 * D_FF), BF), _sds((1, D), F32), _sds((1, D), F32)],
        in_specs=[rows_(D), rows_(D), rows_(D_FF), rows_(D_FF), _resident((D_FF, D)), vec, vec],
        out_specs=[rows_(D), rows_(2 * D_FF), vec, vec],
        compiler_params=_cp(1, 56),
    )(dh, y, dsilu_u, silu, w, gate, gp)


def _ffn_dn(dgu, wt, h, dh, sc, gp, name):
    S = h.shape[0]
    R = min(512, S)

    def body(dgu_ref, w_ref, h_ref, dh_ref, sc_ref, gp_ref, out_ref, dsh_ref, dsc_ref, dgp_ref):
        @pl.when(pl.program_id(0) == 0)
        def _():
            dsh_ref[...] = jnp.zeros_like(dsh_ref)
            dsc_ref[...] = jnp.zeros_like(dsc_ref)
            dgp_ref[...] = jnp.zeros_like(dgp_ref)

        for r0 in range(0, R, CHUNK):
            rows = slice(r0, r0 + CHUNK)
            dn = _dot(dgu_ref[rows, :], w_ref[...])
            dx, dsh, dsc, dgp = _prenorm_bwd(dn, h_ref[rows, :], gp_ref[...], sc_ref[...])
            out_ref[rows, :] = dh_ref[rows, :] + dx
            dsh_ref[...] += dsh
            dsc_ref[...] += dsc
            dgp_ref[...] += dgp

    vec = _const((1, D))
    rows_ = lambda w_: pl.BlockSpec((R, w_), lambda i: (i, 0))
    return pl.pallas_call(
        body, name=name, grid=(S // R,),
        out_shape=[_sds((S, D), F32)] + [_sds((1, D), F32)] * 3,
        in_specs=[rows_(2 * D_FF), _resident((2 * D_FF, D)), rows_(D), rows_(D), vec, vec],
        out_specs=[rows_(D), vec, vec, vec],
        compiler_params=_cp(1, 56),
    )(dgu, wt, h, dh, sc, gp)


def _ffn_bwd(dh, y, dsilu_u, silu, w, wt, h, gate, gpost, sc, gpre, name):
    S = dh.shape[0]
    R = min(256, S)

    def body(dh_ref, y_ref, g_ref, u_ref, w_ref, wt_ref, h_ref, gate_ref, gpost_ref, sc_ref, gpre_ref,
             dy_ref, dgu_ref, out_ref, dgate_ref, dgpost_ref, dsh_ref, dsc_ref, dgpre_ref):
        @pl.when(pl.program_id(0) == 0)
        def _():
            for r in (dgate_ref, dgpost_ref, dsh_ref, dsc_ref, dgpre_ref):
                r[...] = jnp.zeros_like(r)
        dhh = dh_ref[...]
        dy, dgate, dgpost = _postnorm_bwd(dhh, y_ref[...], gate_ref[...], gpost_ref[...], 0.5)
        dgate_ref[...] += dgate
        dgpost_ref[...] += dgpost
        dyb = dy.astype(BF)
        dy_ref[...] = dyb
        dn = None
        for f0, tf in F_TILES:
            da = _dot_nt(dyb, w_ref[f0:f0 + tf, :])
            dg = (da * g_ref[:, f0:f0 + tf].astype(F32)).astype(BF)
            du = (da * u_ref[:, f0:f0 + tf].astype(F32)).astype(BF)
            dgu_ref[:, f0:f0 + tf] = dg
            dgu_ref[:, D_FF + f0:D_FF + f0 + tf] = du
            part = _dot(dg, wt_ref[f0:f0 + tf, :]) + _dot(du, wt_ref[D_FF + f0:D_FF + f0 + tf, :])
            dn = part if dn is None else dn + part
        dx, dsh, dsc, dgpre = _prenorm_bwd(dn, h_ref[...], gpre_ref[...], sc_ref[...])
        out_ref[...] = dhh + dx
        dsh_ref[...] += dsh
        dsc_ref[...] += dsc
        dgpre_ref[...] += dgpre

    vec = _const((1, D))
    rows_ = lambda w_: pl.BlockSpec((R, w_), lambda i: (i, 0))
    return pl.pallas_call(
        body, name=name, grid=(S // R,),
        out_shape=[_sds((S, D), BF), _sds((S, 2 * D_FF), BF), _sds((S, D), F32)] + [_sds((1, D), F32)] * 5,
        in_specs=[rows_(D), rows_(D), rows_(D_FF), rows_(D_FF), _resident((D_FF, D)), _resident((2 * D_FF, D)),
                  rows_(D), vec, vec, vec, vec],
        out_specs=[rows_(D), rows_(2 * D_FF), rows_(D)] + [vec] * 5,
        compiler_params=_cp(1, 56),
    )(dh, y, dsilu_u, silu, w, wt, h, gate, gpost, sc, gpre)


def _tn_matmul(a, b, name, tm=None):
    S, M_all = a.shape
    N = b.shape[1]
    M = M_all if tm is None else tm
    GA = M_all // M
    ts = min(2048 if M * N <= 2 * D * D else 1024, S)
    nk = S // ts
    chunks = [(m0, min(CHUNK, M - m0)) for m0 in range(0, M, CHUNK)]

    def body(a_ref, b_ref, o_ref, acc):
        k = pl.program_id(1)

        @pl.when(k == 0)
        def _():
            acc[...] = jnp.zeros_like(acc)

        for m0, mc in chunks:
            acc[m0:m0 + mc, :] += _dot_tn(a_ref[:, m0:m0 + mc], b_ref[...])

        @pl.when(k == nk - 1)
        def _():
            for m0, mc in chunks:
                o_ref[m0:m0 + mc, :] = acc[m0:m0 + mc, :].astype(BF)

    return pl.pallas_call(
        body, name=name, grid=(GA, nk),
        out_shape=_sds((M_all, N), BF),
        in_specs=[pl.BlockSpec((ts, M), lambda ga, k: (k, ga)), pl.BlockSpec((ts, N), lambda ga, k: (k, 0))],
        out_specs=pl.BlockSpec((M, N), lambda ga, k: (ga, 0)),
        scratch_shapes=[pltpu.VMEM((M, N), F32)],
        compiler_params=_cp(2, 56),
    )(a, b)


def _mix_in(h, sh, sc, gp, w):
    S = h.shape[0]
    R = min(512, S)

    def body(h_ref, sh_ref, sc_ref, gp_ref, w_ref, n_ref, qkv_ref, zg_ref, gates_ref):
        for r0 in range(0, R, CHUNK):
            rows = slice(r0, r0 + CHUNK)
            nb = _prenorm(h_ref[rows, :], gp_ref[...], sc_ref[...], sh_ref[...]).astype(BF)
            n_ref[rows, :] = nb
            qkv_ref[rows, :] = _dot_nt(nb, w_ref[0:ZG_OFF, :]).astype(BF)
            zg_ref[rows, :] = _dot_nt(nb, w_ref[ZG_OFF:GATE_OFF, :]).astype(BF)
            gates_ref[rows, :] = jax.nn.sigmoid(_dot_nt(nb, w_ref[GATE_OFF:IN_W, :])).astype(BF)

    vec = _const((1, D))
    rows = lambda w_: pl.BlockSpec((R, w_), lambda i: (i, 0))
    return pl.pallas_call(
        body, name="mix_in", grid=(S // R,),
        out_shape=[_sds((S, D), BF), _sds((S, QKV_W), BF), _sds((S, 2 * G_W), BF), _sds((S, 2 * D), BF)],
        in_specs=[rows(D), vec, vec, vec, _resident((IN_W, D))],
        out_specs=[rows(D), rows(QKV_W), rows(2 * G_W), rows(2 * D)],
        compiler_params=_cp(1, 48),
    )(h, sh, sc, gp, w)


def _bias_table(rel_bias, bucket):
    def body(rel_ref, bk_ref, out_ref):
        bk = bk_ref[...]
        qi = lax.broadcasted_iota(jnp.int32, (BLK, 2 * BLK), 0)
        kj = lax.broadcasted_iota(jnp.int32, (BLK, 2 * BLK), 1)
        dist = qi + BLK - kj
        window = (dist >= 0) & (dist < BLK)
        for h in range(N_HEADS):
            acc = jnp.zeros((BLK, 2 * BLK), F32)
            for b in range(N_BUCKETS):
                acc = jnp.where(bk == b, rel_ref[b, h], acc)
            out_ref[h // GROUP, pl.ds((h % GROUP) * BLK, BLK), :] = jnp.where(window, acc, NEG)

    return pl.pallas_call(
        body, name="bias_table",
        out_shape=_sds((N_KV, GROUP * BLK, 2 * BLK), F32),
        in_specs=[pl.BlockSpec(memory_space=pltpu.SMEM), pl.BlockSpec(memory_space=pltpu.VMEM)],
        out_specs=pl.BlockSpec(memory_space=pltpu.VMEM),
    )(rel_bias, bucket)


ATT_TB = 4


def _attn_scores(q, kvc, kvp, bias_ref, sink_ref, has_prev, kh, g0=0, ng=GROUP):
    k2 = jnp.concatenate([kvp[:, kh * HD:(kh + 1) * HD], kvc[:, kh * HD:(kh + 1) * HD]], axis=0)
    v2 = jnp.concatenate([kvp[:, KV_W + kh * HD:KV_W + (kh + 1) * HD],
                          kvc[:, KV_W + kh * HD:KV_W + (kh + 1) * HD]], axis=0)
    q4 = jnp.concatenate([q[:, (kh * GROUP + g) * HD:(kh * GROUP + g + 1) * HD] for g in range(g0, g0 + ng)], axis=0)
    s = _dot_nt(q4, k2) * SCALE + bias_ref[kh, g0 * BLK:(g0 + ng) * BLK, :]
    if has_prev is not None:
        col = lax.broadcasted_iota(jnp.int32, (ng * BLK, 2 * BLK), 1)
        s = jnp.where((col >= BLK) | has_prev, s, NEG)
    rowg = lax.broadcasted_iota(jnp.int32, (ng * BLK, 1), 0) // BLK
    sink = jnp.zeros((ng * BLK, 1), F32)
    for g in range(ng):
        sink = jnp.where(rowg == g, sink_ref[kh * GROUP + g0 + g], sink)
    return q4, k2, v2, s, sink


def _attn_fwd(qkv, bias, sinks):
    S = qkv.shape[0]
    tb = min(ATT_TB, S // BLK)
    T = tb * BLK

    def body(sink_ref, q_ref, kv_ref, kvp_ref, bias_ref, o_ref):
        step = pl.program_id(0)
        for j in range(tb):
            rows = slice(j * BLK, (j + 1) * BLK)
            q, kvc = q_ref[rows, :], kv_ref[rows, :]
            kvp = kvp_ref[...] if j == 0 else kv_ref[(j - 1) * BLK:j * BLK, :]
            has_prev = (step > 0) if j == 0 else None
            outs = []
            for kh in range(N_KV):
                q4, k2, v2, s, sink = _attn_scores(q, kvc, kvp, bias_ref, sink_ref, has_prev, kh)
                m = jnp.maximum(jnp.max(s, axis=1, keepdims=True), sink)
                p = jnp.exp(s - m)
                denom = jnp.sum(p, axis=1, keepdims=True) + jnp.exp(sink - m)
                o4 = _dot((p / denom).astype(BF), v2)
                outs += [o4[g * BLK:(g + 1) * BLK] for g in range(GROUP)]
            o_ref[rows, :] = jnp.concatenate(outs, axis=1).astype(BF)

    return pl.pallas_call(
        body, name="attn_fwd", grid=(S // T,),
        out_shape=_sds((S, Q_W), BF),
        in_specs=[pl.BlockSpec(memory_space=pltpu.SMEM),
                  pl.BlockSpec((T, Q_W), lambda i: (i, 0)),
                  pl.BlockSpec((T, 2 * KV_W), lambda i: (i, 2)),
                  pl.BlockSpec((BLK, 2 * KV_W), lambda i: (jnp.maximum(i * tb - 1, 0), 2)),
                  _const((N_KV, GROUP * BLK, 2 * BLK))],
        out_specs=pl.BlockSpec((T, Q_W), lambda i: (i, 0)),
        compiler_params=_cp(1, 32),
    )(sinks, qkv, qkv, qkv, bias)


def _attn_bwd(qkv, bias, sinks, do):
    S = qkv.shape[0]
    tb = 1
    ng = GROUP
    T = tb * BLK
    nt = S // T

    def body(sink_ref, q_ref, kv_ref, kvp_ref, bias_ref, do_ref, dq_ref, dkv_ref, dbias_ref, dsink_ref, carry):
        i = pl.program_id(0)

        @pl.when(i == 0)
        def _():
            carry[...] = jnp.zeros_like(carry)
            dbias_ref[...] = jnp.zeros_like(dbias_ref)
            dsink_ref[...] = jnp.zeros_like(dsink_ref)

        from_next = carry[...]
        for j in reversed(range(tb)):
            rows = slice(j * BLK, (j + 1) * BLK)
            q, kvc, do_ = q_ref[rows, :], kv_ref[rows, :], do_ref[rows, :]
            kvp = kvp_ref[...] if j == 0 else kv_ref[(j - 1) * BLK:j * BLK, :]
            has_prev = (i < nt - 1) if j == 0 else None
            dqs, dk_cur, dv_cur, dk_prev, dv_prev = [], [], [], [], []
            head_row = lax.broadcasted_iota(jnp.int32, (N_HEADS, 128), 0)
            dsink_rows = jnp.zeros((N_HEADS, 128), F32)
            for kh in range(N_KV):
                dk2, dv2 = None, None
                for g0 in range(0, GROUP, ng):
                    q4, k2, v2, s, sink = _attn_scores(q, kvc, kvp, bias_ref, sink_ref, has_prev, kh, g0, ng)
                    m = jnp.maximum(jnp.max(s, axis=1, keepdims=True), sink)
                    p = jnp.exp(s - m)
                    denom = jnp.sum(p, axis=1, keepdims=True) + jnp.exp(sink - m)
                    prob = p / denom
                    p_sink = jnp.exp(sink - m) / denom
                    pb = prob.astype(BF)
                    do4 = jnp.concatenate([do_[:, (kh * GROUP + g) * HD:(kh * GROUP + g + 1) * HD]
                                           for g in range(g0, g0 + ng)], axis=0)
                    dp = _dot_nt(do4, v2)
                    o4 = _dot(pb, v2)
                    delta = jnp.sum(do4.astype(F32) * o4, axis=1, keepdims=True)
                    ds = prob * (dp - delta)
                    dbias_ref[kh, g0 * BLK:(g0 + ng) * BLK, :] += ds
                    sink_term = p_sink * delta
                    for g in range(ng):
                        val = -jnp.sum(sink_term[g * BLK:(g + 1) * BLK], axis=0, keepdims=True)
                        dsink_rows = jnp.where(head_row == kh * GROUP + g0 + g, val, dsink_rows)
                    dsb = ds.astype(BF)
                    dq4 = _dot(dsb, k2) * SCALE
                    dk_part = jnp.transpose(_dot_tn(q4, dsb)) * SCALE
                    dv_part = jnp.transpose(_dot_tn(do4, pb))
                    dk2 = dk_part if dk2 is None else dk2 + dk_part
                    dv2 = dv_part if dv2 is None else dv2 + dv_part
                    dqs += [dq4[g * BLK:(g + 1) * BLK] for g in range(ng)]
                dk_prev.append(dk2[0:BLK])
                dk_cur.append(dk2[BLK:2 * BLK])
                dv_prev.append(dv2[0:BLK])
                dv_cur.append(dv2[BLK:2 * BLK])
            dsink_ref[...] += dsink_rows
            dq_ref[rows, :] = jnp.concatenate(dqs, axis=1).astype(BF)
            dkv_ref[rows, :] = (jnp.concatenate(dk_cur + dv_cur, axis=1) + from_next).astype(BF)
            from_next = jnp.concatenate(dk_prev + dv_prev, axis=1)
        carry[...] = from_next

    return pl.pallas_call(
        body, name="attn_bwd", grid=(nt,),
        out_shape=[_sds((S, Q_W), BF), _sds((S, 2 * KV_W), BF),
                   _sds((N_KV, GROUP * BLK, 2 * BLK), F32), _sds((N_HEADS, 128), F32)],
        in_specs=[pl.BlockSpec(memory_space=pltpu.SMEM),
                  pl.BlockSpec((T, Q_W), lambda i: (nt - 1 - i, 0)),
                  pl.BlockSpec((T, 2 * KV_W), lambda i: (nt - 1 - i, 2)),
                  pl.BlockSpec((BLK, 2 * KV_W), lambda i: (jnp.maximum((nt - 1 - i) * tb - 1, 0), 2)),
                  _const((N_KV, GROUP * BLK, 2 * BLK)),
                  pl.BlockSpec((T, Q_W), lambda i: (nt - 1 - i, 0))],
        out_specs=[pl.BlockSpec((T, Q_W), lambda i: (nt - 1 - i, 0)),
                   pl.BlockSpec((T, 2 * KV_W), lambda i: (nt - 1 - i, 0)),
                   _const((N_KV, GROUP * BLK, 2 * BLK)), _const((N_HEADS, 128))],
        scratch_shapes=[pltpu.VMEM((BLK, 2 * KV_W), F32)],
        compiler_params=_cp(1, 32),
    )(sinks, qkv, qkv, qkv, bias, do)


def _rel_bias_grad(dbias, bucket):
    def body(db_ref, bk_ref, out_ref):
        bk = bk_ref[...]
        lane = lax.broadcasted_iota(jnp.int32, (1, 128), 1)
        for h in range(N_HEADS):
            d = db_ref[h // GROUP, pl.ds((h % GROUP) * BLK, BLK), :]
            row = jnp.zeros((1, 128), F32)
            for b in range(N_BUCKETS):
                tot = jnp.sum(jnp.sum(jnp.where(bk == b, d, 0.0), axis=1, keepdims=True), axis=0, keepdims=True)
                row = jnp.where(lane == b, tot, row)
            out_ref[pl.ds(h, 1), :] = row

    vm = pl.BlockSpec(memory_space=pltpu.VMEM)
    return pl.pallas_call(body, name="rel_bias_grad", out_shape=_sds((N_HEADS, 128), F32),
                          in_specs=[vm, vm], out_specs=vm)(dbias, bucket)


def _gmlp_parts(zg, lg_ref, lb_ref):
    z = zg.astype(F32)
    ge = _gelu(z)
    u, vg = ge[:, 0:G_W], ge[:, G_W:2 * G_W]
    mu = jnp.mean(vg, axis=-1, keepdims=True)
    xc = vg - mu
    rstd = lax.rsqrt(jnp.mean(xc * xc, axis=-1, keepdims=True) + EPS)
    xh = xc * rstd
    return z, u, xh, rstd, xh * lg_ref[...] + lb_ref[...]


def _causal_weights(ws_ref, wc):
    t = lax.broadcasted_iota(jnp.int32, (BLK, BLK), 0)
    s = lax.broadcasted_iota(jnp.int32, (BLK, BLK), 1)
    for g in range(N_HEADS):
        wc[g] = jnp.where(s <= t, ws_ref[g], 0.0).astype(BF)


def _spatial(vb, wc, bst_ref, p, low):
    xp = vb[:, p * 128:(p + 1) * 128]
    s0 = _dot(wc[2 * p], xp) + bst_ref[:, 2 * p:2 * p + 1]
    s1 = _dot(wc[2 * p + 1], xp) + bst_ref[:, 2 * p + 1:2 * p + 2]
    return xp, jnp.where(low, s0, s1)


def _gmlp_fwd(zg, lg, lb, ws, bst):
    S = zg.shape[0]
    tb = min(ATT_TB, S // BLK)
    T = tb * BLK

    def body(zg_ref, lg_ref, lb_ref, ws_ref, bst_ref, o_ref, wc):
        @pl.when(pl.program_id(0) == 0)
        def _():
            _causal_weights(ws_ref, wc)
        low = lax.broadcasted_iota(jnp.int32, (BLK, 128), 1) < HD
        for j in range(tb):
            rows = slice(j * BLK, (j + 1) * BLK)
            _, u, _, _, vln = _gmlp_parts(zg_ref[rows, :], lg_ref, lb_ref)
            vb = vln.astype(BF)
            for p in range(4):
                _, sp = _spatial(vb, wc, bst_ref, p, low)
                o_ref[rows, p * 128:(p + 1) * 128] = (u[:, p * 128:(p + 1) * 128] * sp).astype(BF)

    return pl.pallas_call(
        body, name="gmlp_fwd", grid=(S // T,),
        out_shape=_sds((S, G_W), BF),
        in_specs=[pl.BlockSpec((T, 2 * G_W), lambda i: (i, 0)), _const((1, G_W)), _const((1, G_W)),
                  _const((N_HEADS, BLK, BLK)), _const((BLK, N_HEADS))],
        out_specs=pl.BlockSpec((T, G_W), lambda i: (i, 0)),
        scratch_shapes=[pltpu.VMEM((N_HEADS, BLK, BLK), BF)],
        compiler_params=_cp(1, 32),
    )(zg, lg, lb, ws, bst)


def _gmlp_bwd(zg, d_out, lg, lb, ws, bst):
    S = zg.shape[0]
    tb = min(ATT_TB, S // BLK)
    T = tb * BLK
    nb = S // T

    def body(zg_ref, d_ref, lg_ref, lb_ref, ws_ref, bst_ref, dzg_ref, dws_ref, dbs_ref, dlg_ref, dlb_ref, wc, dbacc):
        i = pl.program_id(0)

        @pl.when(i == 0)
        def _():
            _causal_weights(ws_ref, wc)
            dws_ref[...] = jnp.zeros_like(dws_ref)
            dlg_ref[...] = jnp.zeros_like(dlg_ref)
            dlb_ref[...] = jnp.zeros_like(dlb_ref)
            dbacc[...] = jnp.zeros_like(dbacc)

        low = lax.broadcasted_iota(jnp.int32, (BLK, 128), 1) < HD
        for j in range(tb):
            rows = slice(j * BLK, (j + 1) * BLK)
            z, u, xh, rstd, vln = _gmlp_parts(zg_ref[rows, :], lg_ref, lb_ref)
            vb = vln.astype(BF)
            d = d_ref[rows, :].astype(F32)
            du_parts, dvln_parts = [], []
            for p in range(4):
                xp, sp = _spatial(vb, wc, bst_ref, p, low)
                dp = d[:, p * 128:(p + 1) * 128]
                du_parts.append(dp * sp)
                dsp = dp * u[:, p * 128:(p + 1) * 128]
                dbacc[:, p * 128:(p + 1) * 128] += dsp
                d0 = jnp.where(low, dsp, 0.0).astype(BF)
                d1 = jnp.where(low, 0.0, dsp).astype(BF)
                dws_ref[2 * p] += _dot_nt(d0, xp)
                dws_ref[2 * p + 1] += _dot_nt(d1, xp)
                dvln_parts.append(_dot_tn(wc[2 * p], d0) + _dot_tn(wc[2 * p + 1], d1))
            dvln = jnp.concatenate(dvln_parts, axis=1)
            dlg_ref[...] += _colsum(dvln * xh)
            dlb_ref[...] += _colsum(dvln)
            dxh = dvln * lg_ref[...]
            dvg = rstd * (dxh - jnp.mean(dxh, axis=-1, keepdims=True)
                          - xh * jnp.mean(dxh * xh, axis=-1, keepdims=True))
            dge = jnp.concatenate(du_parts + [dvg], axis=1)
            dzg_ref[rows, :] = (dge * _gelu_grad(z)).astype(BF)

        @pl.when(i == nb - 1)
        def _():
            t = lax.broadcasted_iota(jnp.int32, (BLK, BLK), 0)
            s = lax.broadcasted_iota(jnp.int32, (BLK, BLK), 1)
            for g in range(N_HEADS):
                dws_ref[g] = jnp.where(s <= t, dws_ref[g], 0.0)
            grp = lax.broadcasted_iota(jnp.int32, (N_HEADS, G_W), 0)
            lane = lax.broadcasted_iota(jnp.int32, (N_HEADS, G_W), 1) // HD
            pick = jnp.where(grp == lane, 1.0, 0.0).astype(F32)
            dbs_ref[...] = lax.dot_general(pick, dbacc[...], (((1,), (1,)), ((), ())),
                                           preferred_element_type=F32, precision=HIGH)

    return pl.pallas_call(
        body, name="gmlp_bwd", grid=(nb,),
        out_shape=[_sds((S, 2 * G_W), BF), _sds((N_HEADS, BLK, BLK), F32), _sds((N_HEADS, BLK), F32),
                   _sds((1, G_W), F32), _sds((1, G_W), F32)],
        in_specs=[pl.BlockSpec((T, 2 * G_W), lambda i: (i, 0)), pl.BlockSpec((T, G_W), lambda i: (i, 0)),
                  _const((1, G_W)), _const((1, G_W)), _const((N_HEADS, BLK, BLK)), _const((BLK, N_HEADS))],
        out_specs=[pl.BlockSpec((T, 2 * G_W), lambda i: (i, 0)), _const((N_HEADS, BLK, BLK)),
                   _const((N_HEADS, BLK)), _const((1, G_W)), _const((1, G_W))],
        scratch_shapes=[pltpu.VMEM((N_HEADS, BLK, BLK), BF), pltpu.VMEM((BLK, G_W), F32)],
        compiler_params=_cp(1, 32),
    )(zg, d_out, lg, lb, ws, bst)


def _mix_out(o, gm, gates, h, wa, wg, wo, gate, gp):
    S = h.shape[0]
    R = min(512, S)

    def body(o_ref, gm_ref, gates_ref, h_ref, wa_ref, wg_ref, wo_ref, gate_ref, gp_ref,
             ya_ref, yg_ref, ym_ref, y_ref, hn_ref):
        for r0 in range(0, R, CHUNK):
            rows = slice(r0, r0 + CHUNK)
            ya = _dot(o_ref[rows, :], wa_ref[...])
            yg = _dot(gm_ref[rows, :], wg_ref[...])
            ya_ref[rows, :] = ya.astype(BF)
            yg_ref[rows, :] = yg.astype(BF)
            ym = (gates_ref[rows, 0:D].astype(F32) * ya + gates_ref[rows, D:2 * D].astype(F32) * yg).astype(BF)
            ym_ref[rows, :] = ym
            y = _dot(ym, wo_ref[...])
            y_ref[rows, :] = y.astype(BF)
            hn_ref[rows, :] = h_ref[rows, :] + gate_ref[...] * (y * _rms_r(y) * gp_ref[...])

    vec = _const((1, D))
    rows = lambda w_: pl.BlockSpec((R, w_), lambda i: (i, 0))
    return pl.pallas_call(
        body, name="mix_out", grid=(S // R,),
        out_shape=[_sds((S, D), BF)] * 4 + [_sds((S, D), F32)],
        in_specs=[rows(Q_W), rows(G_W), rows(2 * D), rows(D), _resident((Q_W, D)), _resident((G_W, D)),
                  _resident((D, D)), vec, vec],
        out_specs=[rows(D)] * 5,
        compiler_params=_cp(1, 48),
    )(o, gm, gates, h, wa, wg, wo, gate, gp)


def _mix_out_bwd(dh, y, ya, yg, gates, att, gm, ymix, wa, wg, wo, gate, gp):
    S = dh.shape[0]
    R = min(512, S)
    nb = S // R

    def body(dh_ref, y_ref, ya_ref, yg_ref, gates_ref, att_ref, gm_ref, ym_ref, wa_ref, wg_ref, wo_ref,
             gate_ref, gp_ref, dz_ref, do_ref, dgm_ref, dgate_ref, dgp_ref, gwo_ref, gwa_ref, gwg_ref,
             acc_o, acc_a, acc_g, dy_scr, dya_scr, dyg_scr):
        i = pl.program_id(0)

        @pl.when(i == 0)
        def _():
            for r in (dgate_ref, dgp_ref, acc_o, acc_a, acc_g):
                r[...] = jnp.zeros_like(r)
        for r0 in range(0, R, CHUNK):
            rows = slice(r0, r0 + CHUNK)
            dy, dgate, dgp = _postnorm_bwd(dh_ref[rows, :], y_ref[rows, :], gate_ref[...], gp_ref[...], 1.0)
            dgate_ref[...] += dgate
            dgp_ref[...] += dgp
            dyb = dy.astype(BF)
            dy_scr[rows, :] = dyb
            dym = _dot_nt(dyb, wo_ref[...])
            ga = gates_ref[rows, 0:D].astype(F32)
            gg = gates_ref[rows, D:2 * D].astype(F32)
            dya = (dym * ga).astype(BF)
            dyg = (dym * gg).astype(BF)
            dya_scr[rows, :] = dya
            dyg_scr[rows, :] = dyg
            dz_ref[rows, 0:D] = (dym * ya_ref[rows, :].astype(F32) * (ga * (1.0 - ga))).astype(BF)
            dz_ref[rows, D:2 * D] = (dym * yg_ref[rows, :].astype(F32) * (gg * (1.0 - gg))).astype(BF)
            do_ref[rows, :] = _dot_nt(dya, wa_ref[...]).astype(BF)
            dgm_ref[rows, :] = _dot_nt(dyg, wg_ref[...]).astype(BF)
        for m0 in range(0, D, CHUNK):
            acc_o[m0:m0 + CHUNK, :] += _dot_tn(ym_ref[:, m0:m0 + CHUNK], dy_scr[...])
        for m0 in range(0, Q_W, CHUNK):
            acc_a[m0:m0 + CHUNK, :] += _dot_tn(att_ref[:, m0:m0 + CHUNK], dya_scr[...])
            acc_g[m0:m0 + CHUNK, :] += _dot_tn(gm_ref[:, m0:m0 + CHUNK], dyg_scr[...])

        @pl.when(i == nb - 1)
        def _():
            for m0 in range(0, D, CHUNK):
                gwo_ref[m0:m0 + CHUNK, :] = acc_o[m0:m0 + CHUNK, :].astype(BF)
            for m0 in range(0, Q_W, CHUNK):
                gwa_ref[m0:m0 + CHUNK, :] = acc_a[m0:m0 + CHUNK, :].astype(BF)
                gwg_ref[m0:m0 + CHUNK, :] = acc_g[m0:m0 + CHUNK, :].astype(BF)

    vec = _const((1, D))
    rows = lambda w_: pl.BlockSpec((R, w_), lambda i: (i, 0))
    return pl.pallas_call(
        body, name="mix_out_bwd", grid=(nb,),
        out_shape=[_sds((S, 2 * D), BF), _sds((S, Q_W), BF), _sds((S, G_W), BF), _sds((1, D), F32),
                   _sds((1, D), F32), _sds((D, D), BF), _sds((Q_W, D), BF), _sds((G_W, D), BF)],
        in_specs=[rows(D), rows(D), rows(D), rows(D), rows(2 * D), rows(Q_W), rows(G_W), rows(D),
                  _resident((Q_W, D)), _resident((G_W, D)), _resident((D, D)), vec, vec],
        out_specs=[rows(2 * D), rows(Q_W), rows(G_W), vec, vec, _const((D, D)), _const((Q_W, D)),
                   _const((G_W, D))],
        scratch_shapes=[pltpu.VMEM((D, D), F32), pltpu.VMEM((Q_W, D), F32), pltpu.VMEM((G_W, D), F32)]
        + [pltpu.VMEM((R, D), BF)] * 3,
        compiler_params=_cp(1, 60),
    )(dh, y, ya, yg, gates, att, gm, ymix, wa, wg, wo, gate, gp)


def _mix_dn(dq, dkv, dzg, dzgate, w, h, dh, sc, gp):
    S = h.shape[0]
    R = min(512, S)

    def body(dq_ref, dkv_ref, dzg_ref, dzt_ref, w_ref, h_ref, dh_ref, sc_ref, gp_ref,
             out_ref, dsh_ref, dsc_ref, dgp_ref):
        @pl.when(pl.program_id(0) == 0)
        def _():
            dsh_ref[...] = jnp.zeros_like(dsh_ref)
            dsc_ref[...] = jnp.zeros_like(dsc_ref)
            dgp_ref[...] = jnp.zeros_like(dgp_ref)
        for r0 in range(0, R, CHUNK):
            rows = slice(r0, r0 + CHUNK)
            dn = _dot(dq_ref[rows, :], w_ref[0:Q_W, :])
            dn = dn + _dot(dkv_ref[rows, :], w_ref[Q_W:QKV_W, :])
            dn = dn + _dot(dzg_ref[rows, :], w_ref[ZG_OFF:GATE_OFF, :])
            dn = dn + _dot(dzt_ref[rows, :], w_ref[GATE_OFF:IN_W, :])
            dx, dsh, dsc, dgp = _prenorm_bwd(dn, h_ref[rows, :], gp_ref[...], sc_ref[...])
            out_ref[rows, :] = dh_ref[rows, :] + dx
            dsh_ref[...] += dsh
            dsc_ref[...] += dsc
            dgp_ref[...] += dgp

    vec = _const((1, D))
    rows = lambda w_: pl.BlockSpec((R, w_), lambda i: (i, 0))
    return pl.pallas_call(
        body, name="mix_dn", grid=(S // R,),
        out_shape=[_sds((S, D), F32)] + [_sds((1, D), F32)] * 3,
        in_specs=[rows(Q_W), rows(2 * KV_W), rows(2 * G_W), rows(2 * D), _resident((IN_W, D)),
                  rows(D), rows(D), vec, vec],
        out_specs=[rows(D), vec, vec, vec],
        compiler_params=_cp(1, 48),
    )(dq, dkv, dzg, dzgate, w, h, dh, sc, gp)


def _adamw_math(w, g, m, v):
    m2 = ADAM_B1 * m + (1.0 - ADAM_B1) * g
    v2 = ADAM_B2 * v + (1.0 - ADAM_B2) * (g * g)
    m_hat = m2 / (1.0 - ADAM_B1 ** ADAM_STEP)
    v_hat = v2 / (1.0 - ADAM_B2 ** ADAM_STEP)
    delta = -ADAM_LR * (m_hat / (jnp.sqrt(v_hat) + ADAM_EPS) + ADAM_WD * w)
    return delta, m2, v2


def _row_tile(rows, cols):
    best = None
    for t in range(16, rows + 1, 16):
        if rows % t == 0 and t * cols <= 256 * 1024:
            best = t
    return best if best is not None else rows


def _adamw_sharded(landing, w, m, v, name):
    r, c = w.shape
    tr = _row_tile(r, c)

    def body(l_ref, w_ref, m_ref, v_ref, g_ref, d_ref, m2_ref, v2_ref):
        g = l_ref[0].astype(F32)
        for j in range(1, N_DEV):
            g = g + l_ref[j].astype(F32)
        delta, m2, v2 = _adamw_math(w_ref[...], g, m_ref[...], v_ref[...])
        g_ref[...] = g
        d_ref[...] = delta
        m2_ref[...] = m2
        v2_ref[...] = v2

    row = pl.BlockSpec((tr, c), lambda i: (i, 0))
    return pl.pallas_call(
        body, name=name, grid=(r // tr,),
        out_shape=[_sds((r, c), F32)] * 4,
        in_specs=[pl.BlockSpec((N_DEV, tr, c), lambda i: (0, i, 0)), row, row, row],
        out_specs=[row] * 4,
        compiler_params=_cp(1, 48),
    )(landing, w, m, v)


def _adamw_small(items):
    n = len(items)

    def body(*refs):
        for k in range(n):
            w_ref, g_ref, m_ref, v_ref = refs[4 * k:4 * k + 4]
            outs = refs[4 * n + 3 * k:4 * n + 3 * k + 3]
            for o_ref, val in zip(outs, _adamw_math(w_ref[...], g_ref[...], m_ref[...], v_ref[...])):
                o_ref[...] = val

    vm = pl.BlockSpec(memory_space=pltpu.VMEM)
    flat = pl.pallas_call(
        body, name="adamw_small",
        out_shape=[_sds(it[0].shape, F32) for it in items for _ in range(3)],
        in_specs=[vm] * (4 * n), out_specs=[vm] * (3 * n),
    )(*[a for it in items for a in it])
    return [tuple(flat[3 * k:3 * k + 3]) for k in range(n)]


def _w_ada_update(c8, d_ada, w, m, v):
    tr = 256

    def body(c_ref, d_ref, w_ref, m_ref, v_ref, g_ref, dl_ref, m2_ref, v2_ref):
        cs = c_ref[...]
        cs = cs * jax.nn.sigmoid(cs)
        g = lax.dot_general(cs, d_ref[...], (((0,), (0,)), ((), ())), preferred_element_type=F32, precision=HIGH)
        delta, m2, v2 = _adamw_math(w_ref[...], g, m_ref[...], v_ref[...])
        g_ref[...] = g
        dl_ref[...] = delta
        m2_ref[...] = m2
        v2_ref[...] = v2

    row = pl.BlockSpec((tr, ADA_W), lambda i: (i, 0))
    return pl.pallas_call(
        body, name="w_ada_update", grid=(D // tr,),
        out_shape=[_sds((D, ADA_W), F32)] * 4,
        in_specs=[pl.BlockSpec((N_DEV, tr), lambda i: (0, i)), _const((N_DEV, ADA_W)), row, row, row],
        out_specs=[row] * 4,
        compiler_params=_cp(1, 40),
    )(c8, d_ada, w, m, v)


def _t5_bucket():
    qi = jnp.arange(BLK, dtype=jnp.int32)[:, None]
    kj = jnp.arange(2 * BLK, dtype=jnp.int32)[None, :]
    dist = jnp.maximum(qi + BLK - kj, 0)
    max_exact = N_BUCKETS // 2
    d_f = jnp.maximum(dist, max_exact).astype(F32)
    large = max_exact + (jnp.log(d_f / max_exact) / math.log(MAX_DISTANCE / max_exact)
                         * (N_BUCKETS - max_exact)).astype(jnp.int32)
    large = jnp.minimum(large, N_BUCKETS - 1)
    return jnp.where(dist < max_exact, dist, large)


def _slabs_of_columns(w):
    r, c8 = w.shape
    return jnp.transpose(w.reshape(r, N_DEV, c8 // N_DEV), (1, 0, 2))


def _columns_of_slabs(w8):
    _, r, c = w8.shape
    return jnp.transpose(w8, (1, 0, 2)).reshape(r, N_DEV * c)


def kernel(x, c, rel_bias, w_ada, b_ada, pre_norm_g, post_norm_g, w_ffn1_in, w_ffn1_out, w_in, sinks, gmlp_ln_g, gmlp_ln_b, gmlp_w_s, gmlp_b_s, w_br_attn, w_br_gmlp, w_out, w_ffn2_in, w_ffn2_out, loss_target, m_rel_bias, m_w_ada, m_b_ada, m_pre_norm_g, m_post_norm_g, m_w_ffn1_in, m_w_ffn1_out, m_w_in, m_sinks, m_gmlp_ln_g, m_gmlp_ln_b, m_gmlp_w_s, m_gmlp_b_s, m_w_br_attn, m_w_br_gmlp, m_w_out, m_w_ffn2_in, m_w_ffn2_out, v_rel_bias, v_w_ada, v_b_ada, v_pre_norm_g, v_post_norm_g, v_w_ffn1_in, v_w_ffn1_out, v_w_in, v_sinks, v_gmlp_ln_g, v_gmlp_ln_b, v_gmlp_w_s, v_gmlp_b_s, v_w_br_attn, v_w_br_gmlp, v_w_out, v_w_ffn2_in, v_w_ffn2_out):
    me = 4 * lax.axis_index("x") + 2 * lax.axis_index("y") + lax.axis_index("c")
    x0 = x[0]
    target = loss_target[0]

    transposed = ("w_ffn1_in", "w_in", "w_ffn2_in")
    shards = [w_ffn1_in[0].T, w_ffn1_out[0], w_in[0].T, w_br_attn[0], w_br_gmlp[0], w_out[0],
              w_ffn2_in[0].T, w_ffn2_out[0]]
    shards_bf = [s.astype(BF) for s in shards]
    groups = [shards_bf[0:1], shards_bf[1:6], shards_bf[6:8]]

    def gather_start(i, after):
        return _slabs_start("gather", groups[i], after, "gather_start_%d" % i)

    def forward_start(st, i, after):
        lands = _slabs_wait("gather", len(groups[i]), st, after, "gather_wait_%d" % i)
        return _slabs_start("forward", lands, c, "forward_start_%d" % i)

    def gathered(st, i, after):
        return _slabs_wait("forward", len(groups[i]), st, after, "forward_wait_%d" % i)

    gs0 = gather_start(0, c)

    mine = jnp.concatenate([c[0], pre_norm_g[0].reshape(-1), post_norm_g[0].reshape(-1)])
    small8 = jnp.broadcast_to(mine[None, :], (8, mine.shape[0]))
    b_ada64 = jnp.repeat(b_ada.reshape(N_DEV, ADA_W), 8, axis=0)
    gath, ada64 = _ada_forward(small8, w_ada[0], b_ada64)
    gath8 = gath[::8]
    ada = ada64[::8].reshape(9, D)
    sh1, sc1, g1, sh2, sc2, g2, sh3, sc3, g3 = [ada[k:k + 1] for k in range(9)]
    gains = gath8[:, D:].reshape(N_DEV, 2, 3, 128)
    pre_g = jnp.transpose(gains[:, 0], (1, 0, 2)).reshape(3, D)
    post_g = jnp.transpose(gains[:, 1], (1, 0, 2)).reshape(3, D)
    pre = [pre_g[k:k + 1] for k in range(3)]
    post = [post_g[k:k + 1] for k in range(3)]

    bucket = _t5_bucket()
    bias = _bias_table(rel_bias, bucket)
    sinks8 = sinks[0]
    lg, lb = gmlp_ln_g, gmlp_ln_b
    ws = gmlp_w_s[0]
    bst = jnp.transpose(gmlp_b_s[0])

    fs0 = forward_start(gs0, 0, sh1)
    gs1 = gather_start(1, fs0[-1])
    wf1_in = gathered(fs0, 0, gs1[-1])[0].reshape(2 * D_FF, D)
    n1, fg1, fu1, fa1 = _ffn_in(x0, sh1, sc1, pre[0], wf1_in, "ffn1_in")
    fs1 = forward_start(gs1, 1, n1)
    gs2 = gather_start(2, fs1[-1])
    mix_w = gathered(fs1, 1, gs2[-1])
    wf1_out = mix_w[0].reshape(D_FF, D)
    w_in_full = mix_w[1].reshape(IN_W, D)
    w_bra = _columns_of_slabs(mix_w[2])
    w_brg = _columns_of_slabs(mix_w[3])
    w_out_full = mix_w[4].reshape(D, D)
    h1, y1 = _ffn_out(fa1, wf1_out, x0, g1, post[0], "ffn1_out")
    n2, qkv, zg, gates = _mix_in(h1, sh2, sc2, pre[1], w_in_full)
    att = _attn_fwd(qkv, bias, sinks8)
    gm = _gmlp_fwd(zg, lg, lb, ws, bst)
    fs2 = forward_start(gs2, 2, gm)
    ya, yg, ymix, y2, h2 = _mix_out(att, gm, gates, h1, w_bra, w_brg, w_out_full, g2 + fs2[-1], post[1])
    wf2_in, wf2_out = gathered(fs2, 2, h2)
    wf2_in = wf2_in.reshape(2 * D_FF, D)
    wf2_out = wf2_out.reshape(D_FF, D)
    n3, fg3, fu3, fa3 = _ffn_in(h2, sh3, sc3, pre[2], wf2_in, "ffn2_in")
    dh3, y3, sq = _ffn_out(fa3, wf2_out, h2, g3, post[2], "ffn2_out", target=target)
    loss = lax.psum(0.5 * sq[0, 0] / D, ("x", "y", "c"))

    def exchange_start(i, arrays):
        return _slabs_start("exchange", arrays, sq, "exchange_start_%d" % i)

    dy3, dgu3, dh2, d_g3, d_post2, d_sh3, d_sc3, d_pre2 = _ffn_bwd(
        dh3, y3, fg3, fu3, wf2_out, wf2_in, h2, g3, post[2], sc3, pre[2], "ffn2_bwd")
    gw_f2_out = _tn_matmul(fa3, dy3, "ffn2_out_wgrad", tm=D_FF // 2).reshape(N_DEV, D_FF // N_DEV, D)
    gw_f2_in = _tn_matmul(dgu3, n3, "ffn2_in_wgrad", tm=D_FF // 2).reshape(N_DEV, FS, D)
    ex1 = exchange_start(1, [gw_f2_out, gw_f2_in])

    dzgate, d_att, d_gm, d_g2, d_post1, gw_out, gw_bra, gw_brg = _mix_out_bwd(
        dh2, y2, ya, yg, gates, att, gm, ymix, w_bra, w_brg, w_out_full, g2 + ex1[-1], post[1])
    ex2 = exchange_start(2, [_slabs_of_columns(gw_bra), _slabs_of_columns(gw_brg),
                             gw_out.reshape(N_DEV, D // N_DEV, D)])
    dq, dkv, dbias, dsink = _attn_bwd(qkv, bias, sinks8, d_att)
    dzg, d_ws, d_bs, d_lg, d_lb = _gmlp_bwd(zg, d_gm, lg, lb, ws, bst)
    d_rel = _rel_bias_grad(dbias, bucket)
    early = jnp.concatenate([
        jnp.concatenate([d_lg.reshape(4, 128), d_lb.reshape(4, 128)], axis=0),
        d_bs, d_rel, dsink, d_ws.reshape(N_HEADS * BLK, BLK)], axis=0)
    sm0 = _slabs_start("gather_all", [early], sq, "small_gather_start")
    dh1, d_sh2, d_sc2, d_pre1 = _mix_dn(dq, dkv, dzg, dzgate, w_in_full, h1, dh2, sc2 + ex2[-1] + sm0[-1], pre[1])
    gw_in = jnp.concatenate(
        [_tn_matmul(dq, n2, "w_in_q_wgrad"), _tn_matmul(dkv, n2, "w_in_kv_wgrad"),
         _tn_matmul(dzg, n2, "w_in_zg_wgrad"), _tn_matmul(dzgate, n2, "w_in_gate_wgrad")],
        axis=0).reshape(N_DEV, IN_W // N_DEV, D)
    ex3 = exchange_start(3, [gw_in])

    dy1, dgu1, d_g1, d_post0 = _ffn_out_bwd(dh1, y1, fg1, fu1, wf1_out, g1 + ex3[-1], post[0], "ffn1_out_bwd")
    gw_f1_out = _tn_matmul(fa1, dy1, "ffn1_out_wgrad", tm=D_FF // 2).reshape(N_DEV, D_FF // N_DEV, D)
    gw_f1_in = _tn_matmul(dgu1, n1, "ffn1_in_wgrad", tm=D_FF // 2).reshape(N_DEV, FS, D)
    ex4 = exchange_start(4, [gw_f1_out, gw_f1_in])
    grad_x, d_sh1, d_sc1, d_pre0 = _ffn_dn(dgu1, wf1_in, x0, dh1, sc1 + ex4[-1], pre[0], "ffn1_dn")

    landed = {}
    for i, (ex, nms) in enumerate([(ex1, ["w_ffn2_out", "w_ffn2_in"]),
                                   (ex2, ["w_br_attn", "w_br_gmlp", "w_out"]), (ex3, ["w_in"]),
                                   (ex4, ["w_ffn1_out", "w_ffn1_in"])]):
        for nm, land in zip(nms, _slabs_wait("exchange", len(nms), ex, grad_x, "exchange_wait_%d" % i)):
            landed[nm] = land
    moments = [(m_w_ffn1_in, v_w_ffn1_in), (m_w_ffn1_out, v_w_ffn1_out), (m_w_in, v_w_in),
               (m_w_br_attn, v_w_br_attn), (m_w_br_gmlp, v_w_br_gmlp), (m_w_out, v_w_out),
               (m_w_ffn2_in, v_w_ffn2_in), (m_w_ffn2_out, v_w_ffn2_out)]
    names = ["w_ffn1_in", "w_ffn1_out", "w_in", "w_br_attn", "w_br_gmlp", "w_out", "w_ffn2_in", "w_ffn2_out"]
    big = {}
    for nm, w_, (m_, v_) in zip(names, shards, moments):
        if nm in transposed:
            res4 = _adamw_sharded(landed[nm], w_, m_[0].T, v_[0].T, "adamw_" + nm)
            big[nm] = [a.T[None] for a in res4]
        else:
            big[nm] = [a[None] for a in _adamw_sharded(landed[nm], w_, m_[0], v_[0], "adamw_" + nm)]

    d_ada = jnp.concatenate([v_.reshape(8, 128) for v_ in
                             (d_sh1, d_sc1, d_g1, d_sh2, d_sc2, d_g2, d_sh3, d_sc3, d_g3)], axis=0)
    d_pre = jnp.concatenate([d_pre0, d_pre1, d_pre2], axis=0)
    d_post = jnp.concatenate([d_post0, d_post1, d_post2], axis=0)
    late = jnp.concatenate([d_ada, _slabs_of_columns(d_pre).reshape(24, 128),
                            _slabs_of_columns(d_post).reshape(24, 128)], axis=0)
    tot, every = _small_allreduce(late)
    (early_land,) = _slabs_wait("gather_all", 1, sm0, grad_x, "small_gather_wait")
    tot_early = _sum_slabs(early_land)

    g_b_ada = tot[0:72].reshape(1, 9 * D)
    g_pre = lax.dynamic_slice_in_dim(tot[72:96], 3 * me, 3, axis=0)[None]
    g_post = lax.dynamic_slice_in_dim(tot[96:120], 3 * me, 3, axis=0)[None]
    g_lg = tot_early[0:4].reshape(1, G_W)
    g_lb = tot_early[4:8].reshape(1, G_W)
    g_bs = tot_early[8:16][None]
    g_rel = jnp.transpose(tot_early[16:24, 0:N_BUCKETS])
    g_sinks = tot_early[24:32, 0][None]
    g_ws = tot_early[32:1056].reshape(1, N_HEADS, BLK, BLK)

    d_ada_mine = lax.dynamic_slice_in_dim(every[:, 0:72].reshape(N_DEV, N_DEV, ADA_W), me, 1, axis=1)[:, 0]
    ada_out = [a[None] for a in _w_ada_update(gath8[:, 0:D], d_ada_mine, w_ada[0], m_w_ada[0], v_w_ada[0])]

    small = [("rel_bias", rel_bias, g_rel, m_rel_bias, v_rel_bias), ("b_ada", b_ada, g_b_ada, m_b_ada, v_b_ada),
             ("pre_norm_g", pre_norm_g, g_pre, m_pre_norm_g, v_pre_norm_g),
             ("post_norm_g", post_norm_g, g_post, m_post_norm_g, v_post_norm_g),
             ("sinks", sinks, g_sinks, m_sinks, v_sinks), ("gmlp_ln_g", gmlp_ln_g, g_lg, m_gmlp_ln_g, v_gmlp_ln_g),
             ("gmlp_ln_b", gmlp_ln_b, g_lb, m_gmlp_ln_b, v_gmlp_ln_b),
             ("gmlp_w_s", gmlp_w_s, g_ws, m_gmlp_w_s, v_gmlp_w_s), ("gmlp_b_s", gmlp_b_s, g_bs, m_gmlp_b_s, v_gmlp_b_s)]
    two_d = lambda a: a.reshape(int(math.prod(a.shape[:-1])), a.shape[-1])
    stepped = _adamw_small([tuple(two_d(a) for a in item[1:]) for item in small])
    res = {"w_ada": ada_out}
    for (nm, w_, g_, _, _), new in zip(small, stepped):
        res[nm] = [g_] + [a.reshape(w_.shape) for a in new]
    res.update(big)
    order = ["rel_bias", "w_ada", "b_ada", "pre_norm_g", "post_norm_g", "w_ffn1_in", "w_ffn1_out", "w_in", "sinks",
             "gmlp_ln_g", "gmlp_ln_b", "gmlp_w_s", "gmlp_b_s", "w_br_attn", "w_br_gmlp", "w_out", "w_ffn2_in",
             "w_ffn2_out"]
    outs = [loss, grad_x[None]]
    for k in range(4):
        outs += [res[nm][k] for nm in order]
    return tuple(outs)
```

```python
import functools
import math

import jax
import jax.numpy as jnp
from jax import lax
from jax.experimental import pallas as pl
from jax.experimental.pallas import tpu as pltpu

F32 = jnp.float32
BF = jnp.bfloat16

N_DEV = 8
D = 1024
D_FF = 2816
FS = D_FF // 4
N_HEADS = 8
N_KV = 2
GROUP = 4
HD = 64
BLK = 128
Q_W = 512
KV_W = 128
G_W = 512
QKV_W = Q_W + 2 * KV_W
ZG_OFF = QKV_W
GATE_OFF = ZG_OFF + 2 * G_W
IN_W = GATE_OFF + 2 * D
N_BUCKETS = 32
MAX_DISTANCE = 128
EPS = 1e-6
NEG = -1e30
SCALE = HD ** -0.5
ADA_W = 9 * D // N_DEV

ADAM_LR = 0.001
ADAM_B1 = 0.9
ADAM_B2 = 0.999
ADAM_EPS = 1e-08
ADAM_WD = 0.01
ADAM_STEP = 10

CHUNK = 256
MIB = 1024 * 1024
MESH = pl.DeviceIdType.MESH
HIGH = lax.Precision.HIGHEST


def _cp(n_grid, vmem_mib):
    return pltpu.CompilerParams(dimension_semantics=("arbitrary",) * n_grid,
                                vmem_limit_bytes=vmem_mib * MIB)


def _const(shape):
    return pl.BlockSpec(shape, lambda *_: (0,) * len(shape))


def _resident(shape):
    return pl.BlockSpec(shape, lambda *_: (0,) * len(shape), pipeline_mode=pl.Buffered(1))


def _sds(shape, dtype):
    return jax.ShapeDtypeStruct(shape, dtype)


def _dot(a, b):
    return jnp.dot(a, b, preferred_element_type=F32)


def _dot_nt(a, b):
    return lax.dot_general(a, b, (((1,), (1,)), ((), ())), preferred_element_type=F32)


def _dot_tn(a, b):
    return lax.dot_general(a, b, (((0,), (0,)), ((), ())), preferred_element_type=F32)


def _rms_r(x):
    return lax.rsqrt(jnp.mean(x * x, axis=-1, keepdims=True) + EPS)


def _colsum(x):
    return jnp.sum(x, axis=0, keepdims=True)


def _prenorm(x, gp, sc, sh):
    return (x * _rms_r(x) * gp) * (1.0 + sc) + sh


def _prenorm_bwd(dn, x, gp, sc):
    r = _rms_r(x)
    xh = x * r
    t = dn * (1.0 + sc) * gp
    dx = r * (t - xh * jnp.mean(t * xh, axis=-1, keepdims=True))
    return dx, _colsum(dn), _colsum(dn * xh * gp), _colsum(dn * (1.0 + sc) * xh)


def _postnorm_bwd(dh, y, gate, gp, res):
    y = y.astype(F32)
    r = _rms_r(y)
    yh = y * r
    dyn = (res * gate) * dh
    t = dyn * gp
    dy = r * (t - yh * jnp.mean(t * yh, axis=-1, keepdims=True))
    return dy, _colsum(res * dh * yh * gp), _colsum(dyn * yh)


def _gelu(x):
    k = math.sqrt(2.0 / math.pi)
    return 0.5 * x * (1.0 + jnp.tanh(k * (x + 0.044715 * x * x * x)))


def _gelu_grad(x):
    k = math.sqrt(2.0 / math.pi)
    t = jnp.tanh(k * (x + 0.044715 * x * x * x))
    return 0.5 * (1.0 + t) + 0.5 * x * (1.0 - t * t) * (k * (1.0 + 3.0 * 0.044715 * x * x))


def _my_place():
    x, y, c = lax.axis_index("x"), lax.axis_index("y"), lax.axis_index("c")
    return x, y, c, 4 * x + 2 * y + c


def _peer(x, y, c, k):
    px = 1 - x if k & 4 else x
    py = 1 - y if k & 2 else y
    pc = 1 - c if k & 1 else c
    return (px, py, pc), 4 * px + 2 * py + pc


HBM_SPEC = pl.BlockSpec(memory_space=pltpu.HBM)
SEM_SPEC = pl.BlockSpec(memory_space=pltpu.SEMAPHORE)
EFFECT = pltpu.SideEffectType.DATAFLOW_SIDE_EFFECTING


RELATIONS = {"exchange": (1, 2, 3, 4, 5, 6, 7), "gather": (1, 2, 4, 6), "forward": (2, 4, 6),
             "gather_all": (1, 2, 3, 4, 5, 6, 7)}


def _slab_copies(mode, srcs, lands, send, recv, loc):
    x, y, c, me = _my_place()
    rel = RELATIONS[mode]
    remote, local = [], []
    for t in range(len(lands)):
        for i, k in enumerate(rel):
            peer, peer_lin = _peer(x, y, c, k)
            if mode == "exchange":
                src, dst, to = srcs[t].at[peer_lin], lands[t].at[me], peer
            elif mode in ("gather", "gather_all"):
                src, dst, to = srcs[t], lands[t].at[me], peer
            else:
                src, dst, to = lands[t].at[peer_lin], lands[t].at[peer_lin], _peer(x, y, c, 1)[0]
            remote.append(pltpu.make_async_remote_copy(
                src_ref=src, dst_ref=dst, send_sem=send.at[t * len(rel) + i], recv_sem=recv.at[t * len(rel) + i],
                device_id=to, device_id_type=MESH))
        if mode == "exchange":
            local.append(pltpu.make_async_copy(srcs[t].at[me], lands[t].at[me], loc.at[t]))
        elif mode in ("gather", "gather_all"):
            local.append(pltpu.make_async_copy(srcs[t], lands[t].at[me], loc.at[t]))
    return remote, local


def _slabs_start(mode, arrays, after, name):
    n = len(arrays)
    if mode == "forward":
        thru = list(arrays)
    else:
        shapes = [a.shape if mode == "exchange" else (N_DEV,) + a.shape for a in arrays]
        thru = list(arrays) + [lax.empty(s, a.dtype) for s, a in zip(shapes, arrays)]
    m = len(thru)
    n_sem = n * len(RELATIONS[mode])

    def body(*refs):
        srcs, lands = refs[:n], refs[m - n:m]
        send, recv, loc = refs[m + 1:m + 4]
        remote, local = _slab_copies(mode, srcs, lands, send, recv, loc)
        for cp in remote + local:
            cp.start()
        refs[-1][...] = jnp.zeros_like(refs[-1])

    return pl.pallas_call(
        body, name=name,
        out_shape=(pltpu.SemaphoreType.DMA((n_sem,)), pltpu.SemaphoreType.DMA((n_sem,)),
                   pltpu.SemaphoreType.DMA((n,)),
                   *[pltpu.HBM(a.shape, a.dtype) for a in thru],
                   _sds((1, D), F32)),
        in_specs=[HBM_SPEC] * m + [pl.BlockSpec(memory_space=pl.ANY)],
        out_specs=(SEM_SPEC, SEM_SPEC, SEM_SPEC, *[HBM_SPEC] * m, pl.BlockSpec(memory_space=pltpu.VMEM)),
        input_output_aliases={t: 3 + t for t in range(m)},
        compiler_params=pltpu.CompilerParams(has_side_effects=EFFECT),
    )(*[pltpu.with_memory_space_constraint(a, pltpu.HBM) for a in thru], after)


def _slabs_wait(mode, n, started, after, name):
    sems = started[0:3]
    thru = started[3:-1]
    m = len(thru)

    def body(*refs):
        srcs, lands = refs[:n], refs[m - n:m]
        remote, local = _slab_copies(mode, srcs, lands, *refs[m:m + 3])
        for cp in remote:
            cp.wait_send()
            cp.wait_recv()
        for cp in local:
            cp.wait()

    res = pl.pallas_call(
        body, name=name,
        out_shape=tuple(pltpu.HBM(a.shape, a.dtype) for a in thru),
        in_specs=[HBM_SPEC] * m + [SEM_SPEC] * 3 + [pl.BlockSpec(memory_space=pl.ANY)],
        out_specs=tuple([HBM_SPEC] * m),
        input_output_aliases={t: t for t in range(m)},
        compiler_params=pltpu.CompilerParams(has_side_effects=EFFECT),
    )(*thru, *sems, after)
    return list(res[m - n:m])


def _ada_forward(small8, w_ada, b_ada64):
    sw = small8.shape[1]

    def body(sm_ref, w_ref, b_ref, gath_ref, ada_ref, part_ref, send1, recv1, send2, recv2):
        x, y, c, me = _my_place()
        row_me = pl.multiple_of(me * 8, 8)
        gath_ref[pl.ds(row_me, 8), :] = sm_ref[...]
        first = []
        for k in range(1, N_DEV):
            peer, _ = _peer(x, y, c, k)
            cp = pltpu.make_async_remote_copy(
                src_ref=sm_ref, dst_ref=gath_ref.at[pl.ds(row_me, 8), :], send_sem=send1.at[k - 1],
                recv_sem=recv1.at[k - 1], device_id=peer, device_id_type=MESH)
            cp.start()
            first.append(cp)
        for cp in first:
            cp.wait()
        cs = gath_ref[:, 0:D]
        cs = cs * jax.nn.sigmoid(cs)
        part_ref[...] = jnp.dot(cs, w_ref[...], preferred_element_type=F32, precision=HIGH)
        ada_ref[pl.ds(row_me, 8), :] = part_ref[pl.ds(row_me, 8), :]
        second = []
        for k in range(1, N_DEV):
            peer, peer_lin = _peer(x, y, c, k)
            cp = pltpu.make_async_remote_copy(
                src_ref=part_ref.at[pl.ds(pl.multiple_of(peer_lin * 8, 8), 8), :],
                dst_ref=ada_ref.at[pl.ds(row_me, 8), :], send_sem=send2.at[k - 1],
                recv_sem=recv2.at[k - 1], device_id=peer, device_id_type=MESH)
            cp.start()
            second.append(cp)
        for cp in second:
            cp.wait()
        ada_ref[...] = ada_ref[...] + b_ref[...]

    vm = pl.BlockSpec(memory_space=pltpu.VMEM)
    return pl.pallas_call(
        body, name="ada_forward",
        out_shape=[_sds((8 * N_DEV, sw), F32), _sds((8 * N_DEV, ADA_W), F32)],
        in_specs=[vm, vm, vm], out_specs=[vm, vm],
        scratch_shapes=[pltpu.VMEM((8 * N_DEV, ADA_W), F32)] + [pltpu.SemaphoreType.DMA((7,))] * 4,
        compiler_params=pltpu.CompilerParams(vmem_limit_bytes=32 * MIB),
    )(small8, w_ada, b_ada64)


def _sum_slabs(land):
    def body(l_ref, o_ref):
        acc = l_ref[0]
        for j in range(1, N_DEV):
            acc = acc + l_ref[j]
        o_ref[...] = acc

    vm = pl.BlockSpec(memory_space=pltpu.VMEM)
    return pl.pallas_call(body, name="sum_slabs", out_shape=_sds(land.shape[1:], F32), in_specs=[vm], out_specs=vm,
                          compiler_params=pltpu.CompilerParams(vmem_limit_bytes=32 * MIB))(land)


def _small_allreduce(pack):
    rows = pack.shape[0]

    def body(p_ref, sum_ref, gath_ref, send, recv):
        x, y, c, me = _my_place()
        gath_ref[me] = p_ref[...]
        cps = []
        for k in range(1, N_DEV):
            peer, _ = _peer(x, y, c, k)
            cp = pltpu.make_async_remote_copy(
                src_ref=p_ref, dst_ref=gath_ref.at[me], send_sem=send.at[k - 1],
                recv_sem=recv.at[k - 1], device_id=peer, device_id_type=MESH)
            cp.start()
            cps.append(cp)
        for cp in cps:
            cp.wait()
        acc = gath_ref[0]
        for j in range(1, N_DEV):
            acc = acc + gath_ref[j]
        sum_ref[...] = acc

    vm = pl.BlockSpec(memory_space=pltpu.VMEM)
    return pl.pallas_call(
        body, name="small_allreduce",
        out_shape=[_sds((rows, 128), F32), _sds((N_DEV, rows, 128), F32)],
        in_specs=[vm], out_specs=[vm, vm],
        scratch_shapes=[pltpu.SemaphoreType.DMA((7,)), pltpu.SemaphoreType.DMA((7,))],
        compiler_params=pltpu.CompilerParams(vmem_limit_bytes=40 * MIB),
    )(pack)


F_TILES = tuple((f0, min(512, D_FF - f0)) for f0 in range(0, D_FF, 512))
F_TILES_NARROW = tuple((f0, 256) for f0 in range(0, D_FF, 256))


def _swiglu_tile(n, wt_ref, f0, tf):
    g = _dot_nt(n, wt_ref[f0:f0 + tf, :])
    u = _dot_nt(n, wt_ref[D_FF + f0:D_FF + f0 + tf, :])
    sg = jax.nn.sigmoid(g)
    silu = g * sg
    return (u * (sg * (1.0 + g * (1.0 - sg)))).astype(BF), silu.astype(BF), (silu * u).astype(BF)


def _ffn_in(h, sh, sc, gp, wt, name):
    S = h.shape[0]
    R = min(512, S)

    def body(h_ref, sh_ref, sc_ref, gp_ref, w_ref, n_ref, dg_ref, sl_ref, a_ref):
        for r0 in range(0, R, CHUNK):
            rows = slice(r0, r0 + CHUNK)
            n = _prenorm(h_ref[rows, :], gp_ref[...], sc_ref[...], sh_ref[...]).astype(BF)
            n_ref[rows, :] = n
            for f0, tf in F_TILES_NARROW:
                dg_ref[rows, f0:f0 + tf], sl_ref[rows, f0:f0 + tf], a_ref[rows, f0:f0 + tf] = _swiglu_tile(
                    n, w_ref, f0, tf)

    vec = _const((1, D))
    rows_ = lambda w_: pl.BlockSpec((R, w_), lambda i: (i, 0))
    return pl.pallas_call(
        body, name=name, grid=(S // R,),
        out_shape=[_sds((S, D), BF)] + [_sds((S, D_FF), BF)] * 3,
        in_specs=[rows_(D), vec, vec, vec, _resident((2 * D_FF, D))],
        out_specs=[rows_(D), rows_(D_FF), rows_(D_FF), rows_(D_FF)],
        compiler_params=_cp(1, 56),
    )(h, sh, sc, gp, wt)


def _ffn_out(a, w, h, gate, gp, name, target=None):
    S = h.shape[0]
    R = min(512, S)
    with_loss = target is not None

    def body(a_ref, w_ref, h_ref, gate_ref, gp_ref, *rest):
        if with_loss:
            t_ref, out_ref, y_ref, tot_ref = rest

            @pl.when(pl.program_id(0) == 0)
            def _():
                tot_ref[...] = jnp.zeros_like(tot_ref)
        else:
            out_ref, y_ref = rest
        for r0 in range(0, R, CHUNK):
            rows = slice(r0, r0 + CHUNK)
            y = _dot(a_ref[rows, :], w_ref[...])
            y_ref[rows, :] = y.astype(BF)
            hn = h_ref[rows, :] + (0.5 * gate_ref[...]) * (y * _rms_r(y) * gp_ref[...])
            if with_loss:
                e = hn - t_ref[rows, :]
                out_ref[rows, :] = e * (1.0 / D)
                tot_ref[...] += jnp.sum(jnp.sum(e * e, axis=1, keepdims=True), axis=0, keepdims=True)
            else:
                out_ref[rows, :] = hn

    vec = _const((1, D))
    rows_ = lambda w_: pl.BlockSpec((R, w_), lambda i: (i, 0))
    return pl.pallas_call(
        body, name=name, grid=(S // R,),
        out_shape=[_sds((S, D), F32), _sds((S, D), BF)] + ([_sds((1, 1), F32)] if with_loss else []),
        in_specs=[rows_(D_FF), _resident((D_FF, D)), rows_(D), vec, vec] + ([rows_(D)] if with_loss else []),
        out_specs=[rows_(D), rows_(D)] + ([_const((1, 1))] if with_loss else []),
        compiler_params=_cp(1, 48),
    )(*((a, w, h, gate, gp) + ((target,) if with_loss else ())))


def _ffn_out_bwd(dh, y, dsilu_u, silu, w, gate, gp, name):
    S = dh.shape[0]
    R = min(512, S)

    def body(dh_ref, y_ref, g_ref, u_ref, w_ref, gate_ref, gp_ref, dy_ref, dgu_ref, dgate_ref, dgp_ref):
        @pl.when(pl.program_id(0) == 0)
        def _():
            dgate_ref[...] = jnp.zeros_like(dgate_ref)
            dgp_ref[...] = jnp.zeros_like(dgp_ref)
        for r0 in range(0, R, CHUNK):
            rows = slice(r0, r0 + CHUNK)
            dy, dgate, dgp = _postnorm_bwd(dh_ref[rows, :], y_ref[rows, :], gate_ref[...], gp_ref[...], 0.5)
            dgate_ref[...] += dgate
            dgp_ref[...] += dgp
            dyb = dy.astype(BF)
            dy_ref[rows, :] = dyb
            for f0, tf in F_TILES:
                da = _dot_nt(dyb, w_ref[f0:f0 + tf, :])
                dgu_ref[rows, f0:f0 + tf] = (da * g_ref[rows, f0:f0 + tf].astype(F32)).astype(BF)
                dgu_ref[rows, D_FF + f0:D_FF + f0 + tf] = (da * u_ref[rows, f0:f0 + tf].astype(F32)).astype(BF)

    vec = _const((1, D))
    rows_ = lambda w_: pl.BlockSpec((R, w_), lambda i: (i, 0))
    return pl.pallas_call(
        body, name=name, grid=(S // R,),
        out_shape=[_sds((S, D), BF), _sds((S, 2 * D_FF), BF), _sds((1, D), F32), _sds((1, D), F32)],
        in_specs=[rows_(D), rows_(D), rows_(D_FF), rows_(D_FF), _resident((D_FF, D)), vec, vec],
        out_specs=[rows_(D), rows_(2 * D_FF), vec, vec],
        compiler_params=_cp(1, 56),
    )(dh, y, dsilu_u, silu, w, gate, gp)


def _ffn_dn(dgu, wt, h, dh, sc, gp, name):
    S = h.shape[0]
    R = min(512, S)

    def body(dgu_ref, w_ref, h_ref, dh_ref, sc_ref, gp_ref, out_ref, dsh_ref, dsc_ref, dgp_ref):
        @pl.when(pl.program_id(0) == 0)
        def _():
            dsh_ref[...] = jnp.zeros_like(dsh_ref)
            dsc_ref[...] = jnp.zeros_like(dsc_ref)
            dgp_ref[...] = jnp.zeros_like(dgp_ref)

        for r0 in range(0, R, CHUNK):
            rows = slice(r0, r0 + CHUNK)
            dn = _dot(dgu_ref[rows, :], w_ref[...])
            dx, dsh, dsc, dgp = _prenorm_bwd(dn, h_ref[rows, :], gp_ref[...], sc_ref[...])
            out_ref[rows, :] = dh_ref[rows, :] + dx
            dsh_ref[...] += dsh
            dsc_ref[...] += dsc
            dgp_ref[...] += dgp

    vec = _const((1, D))
    rows_ = lambda w_: pl.BlockSpec((R, w_), lambda i: (i, 0))
    return pl.pallas_call(
        body, name=name, grid=(S // R,),
        out_shape=[_sds((S, D), F32)] + [_sds((1, D), F32)] * 3,
        in_specs=[rows_(2 * D_FF), _resident((2 * D_FF, D)), rows_(D), rows_(D), vec, vec],
        out_specs=[rows_(D), vec, vec, vec],
        compiler_params=_cp(1, 56),
    )(dgu, wt, h, dh, sc, gp)


def _ffn_bwd(dh, y, dsilu_u, silu, w, wt, h, gate, gpost, sc, gpre, name):
    S = dh.shape[0]
    R = min(256, S)

    def body(dh_ref, y_ref, g_ref, u_ref, w_ref, wt_ref, h_ref, gate_ref, gpost_ref, sc_ref, gpre_ref,
             dy_ref, dgu_ref, out_ref, dgate_ref, dgpost_ref, dsh_ref, dsc_ref, dgpre_ref):
        @pl.when(pl.program_id(0) == 0)
        def _():
            for r in (dgate_ref, dgpost_ref, dsh_ref, dsc_ref, dgpre_ref):
                r[...] = jnp.zeros_like(r)
        dhh = dh_ref[...]
        dy, dgate, dgpost = _postnorm_bwd(dhh, y_ref[...], gate_ref[...], gpost_ref[...], 0.5)
        dgate_ref[...] += dgate
        dgpost_ref[...] += dgpost
        dyb = dy.astype(BF)
        dy_ref[...] = dyb
        dn = None
        for f0, tf in F_TILES:
            da = _dot_nt(dyb, w_ref[f0:f0 + tf, :])
            dg = (da * g_ref[:, f0:f0 + tf].astype(F32)).astype(BF)
            du = (da * u_ref[:, f0:f0 + tf].astype(F32)).astype(BF)
            dgu_ref[:, f0:f0 + tf] = dg
            dgu_ref[:, D_FF + f0:D_FF + f0 + tf] = du
            part = _dot(dg, wt_ref[f0:f0 + tf, :]) + _dot(du, wt_ref[D_FF + f0:D_FF + f0 + tf, :])
            dn = part if dn is None else dn + part
        dx, dsh, dsc, dgpre = _prenorm_bwd(dn, h_ref[...], gpre_ref[...], sc_ref[...])
        out_ref[...] = dhh + dx
        dsh_ref[...] += dsh
        dsc_ref[...] += dsc
        dgpre_ref[...] += dgpre

    vec = _const((1, D))
    rows_ = lambda w_: pl.BlockSpec((R, w_), lambda i: (i, 0))
    return pl.pallas_call(
        body, name=name, grid=(S // R,),
        out_shape=[_sds((S, D), BF), _sds((S, 2 * D_FF), BF), _sds((S, D), F32)] + [_sds((1, D), F32)] * 5,
        in_specs=[rows_(D), rows_(D), rows_(D_FF), rows_(D_FF), _resident((D_FF, D)), _resident((2 * D_FF, D)),
                  rows_(D), vec, vec, vec, vec],
        out_specs=[rows_(D), rows_(2 * D_FF), rows_(D)] + [vec] * 5,
        compiler_params=_cp(1, 56),
    )(dh, y, dsilu_u, silu, w, wt, h, gate, gpost, sc, gpre)


def _tn_matmul(a, b, name, tm=None):
    S, M_all = a.shape
    N = b.shape[1]
    M = M_all if tm is None else tm
    GA = M_all // M
    ts = min(2048 if M * N <= 2 * D * D else 1024, S)
    nk = S // ts
    chunks = [(m0, min(CHUNK, M - m0)) for m0 in range(0, M, CHUNK)]

    def body(a_ref, b_ref, o_ref, acc):
        k = pl.program_id(1)

        @pl.when(k == 0)
        def _():
            acc[...] = jnp.zeros_like(acc)

        for m0, mc in chunks:
            acc[m0:m0 + mc, :] += _dot_tn(a_ref[:, m0:m0 + mc], b_ref[...])

        @pl.when(k == nk - 1)
        def _():
            for m0, mc in chunks:
                o_ref[m0:m0 + mc, :] = acc[m0:m0 + mc, :].astype(BF)

    return pl.pallas_call(
        body, name=name, grid=(GA, nk),
        out_shape=_sds((M_all, N), BF),
        in_specs=[pl.BlockSpec((ts, M), lambda ga, k: (k, ga)), pl.BlockSpec((ts, N), lambda ga, k: (k, 0))],
        out_specs=pl.BlockSpec((M, N), lambda ga, k: (ga, 0)),
        scratch_shapes=[pltpu.VMEM((M, N), F32)],
        compiler_params=_cp(2, 56),
    )(a, b)


def _mix_in(h, sh, sc, gp, w):
    S = h.shape[0]
    R = min(512, S)

    def body(h_ref, sh_ref, sc_ref, gp_ref, w_ref, n_ref, qkv_ref, zg_ref, gates_ref):
        for r0 in range(0, R, CHUNK):
            rows = slice(r0, r0 + CHUNK)
            nb = _prenorm(h_ref[rows, :], gp_ref[...], sc_ref[...], sh_ref[...]).astype(BF)
            n_ref[rows, :] = nb
            qkv_ref[rows, :] = _dot_nt(nb, w_ref[0:ZG_OFF, :]).astype(BF)
            zg_ref[rows, :] = _dot_nt(nb, w_ref[ZG_OFF:GATE_OFF, :]).astype(BF)
            gates_ref[rows, :] = jax.nn.sigmoid(_dot_nt(nb, w_ref[GATE_OFF:IN_W, :])).astype(BF)

    vec = _const((1, D))
    rows = lambda w_: pl.BlockSpec((R, w_), lambda i: (i, 0))
    return pl.pallas_call(
        body, name="mix_in", grid=(S // R,),
        out_shape=[_sds((S, D), BF), _sds((S, QKV_W), BF), _sds((S, 2 * G_W), BF), _sds((S, 2 * D), BF)],
        in_specs=[rows(D), vec, vec, vec, _resident((IN_W, D))],
        out_specs=[rows(D), rows(QKV_W), rows(2 * G_W), rows(2 * D)],
        compiler_params=_cp(1, 48),
    )(h, sh, sc, gp, w)


def _bias_table(rel_bias, bucket):
    def body(rel_ref, bk_ref, out_ref):
        bk = bk_ref[...]
        qi = lax.broadcasted_iota(jnp.int32, (BLK, 2 * BLK), 0)
        kj = lax.broadcasted_iota(jnp.int32, (BLK, 2 * BLK), 1)
        dist = qi + BLK - kj
        window = (dist >= 0) & (dist < BLK)
        for h in range(N_HEADS):
            acc = jnp.zeros((BLK, 2 * BLK), F32)
            for b in range(N_BUCKETS):
                acc = jnp.where(bk == b, rel_ref[b, h], acc)
            out_ref[h // GROUP, pl.ds((h % GROUP) * BLK, BLK), :] = jnp.where(window, acc, NEG)

    return pl.pallas_call(
        body, name="bias_table",
        out_shape=_sds((N_KV, GROUP * BLK, 2 * BLK), F32),
        in_specs=[pl.BlockSpec(memory_space=pltpu.SMEM), pl.BlockSpec(memory_space=pltpu.VMEM)],
        out_specs=pl.BlockSpec(memory_space=pltpu.VMEM),
    )(rel_bias, bucket)


ATT_TB = 4


def _attn_scores(q, kvc, kvp, bias_ref, sink_ref, has_prev, kh, g0=0, ng=GROUP):
    k2 = jnp.concatenate([kvp[:, kh * HD:(kh + 1) * HD], kvc[:, kh * HD:(kh + 1) * HD]], axis=0)
    v2 = jnp.concatenate([kvp[:, KV_W + kh * HD:KV_W + (kh + 1) * HD],
                          kvc[:, KV_W + kh * HD:KV_W + (kh + 1) * HD]], axis=0)
    q4 = jnp.concatenate([q[:, (kh * GROUP + g) * HD:(kh * GROUP + g + 1) * HD] for g in range(g0, g0 + ng)], axis=0)
    s = _dot_nt(q4, k2) * SCALE + bias_ref[kh, g0 * BLK:(g0 + ng) * BLK, :]
    if has_prev is not None:
        col = lax.broadcasted_iota(jnp.int32, (ng * BLK, 2 * BLK), 1)
        s = jnp.where((col >= BLK) | has_prev, s, NEG)
    rowg = lax.broadcasted_iota(jnp.int32, (ng * BLK, 1), 0) // BLK
    sink = jnp.zeros((ng * BLK, 1), F32)
    for g in range(ng):
        sink = jnp.where(rowg == g, sink_ref[kh * GROUP + g0 + g], sink)
    return q4, k2, v2, s, sink


def _attn_fwd(qkv, bias, sinks):
    S = qkv.shape[0]
    tb = min(ATT_TB, S // BLK)
    T = tb * BLK

    def body(sink_ref, q_ref, kv_ref, kvp_ref, bias_ref, o_ref):
        step = pl.program_id(0)
        for j in range(tb):
            rows = slice(j * BLK, (j + 1) * BLK)
            q, kvc = q_ref[rows, :], kv_ref[rows, :]
            kvp = kvp_ref[...] if j == 0 else kv_ref[(j - 1) * BLK:j * BLK, :]
            has_prev = (step > 0) if j == 0 else None
            outs = []
            for kh in range(N_KV):
                q4, k2, v2, s, sink = _attn_scores(q, kvc, kvp, bias_ref, sink_ref, has_prev, kh)
                m = jnp.maximum(jnp.max(s, axis=1, keepdims=True), sink)
                p = jnp.exp(s - m)
                denom = jnp.sum(p, axis=1, keepdims=True) + jnp.exp(sink - m)
                o4 = _dot((p / denom).astype(BF), v2)
                outs += [o4[g * BLK:(g + 1) * BLK] for g in range(GROUP)]
            o_ref[rows, :] = jnp.concatenate(outs, axis=1).astype(BF)

    return pl.pallas_call(
        body, name="attn_fwd", grid=(S // T,),
        out_shape=_sds((S, Q_W), BF),
        in_specs=[pl.BlockSpec(memory_space=pltpu.SMEM),
                  pl.BlockSpec((T, Q_W), lambda i: (i, 0)),
                  pl.BlockSpec((T, 2 * KV_W), lambda i: (i, 2)),
                  pl.BlockSpec((BLK, 2 * KV_W), lambda i: (jnp.maximum(i * tb - 1, 0), 2)),
                  _const((N_KV, GROUP * BLK, 2 * BLK))],
        out_specs=pl.BlockSpec((T, Q_W), lambda i: (i, 0)),
        compiler_params=_cp(1, 32),
    )(sinks, qkv, qkv, qkv, bias)


def _attn_bwd(qkv, bias, sinks, do):
    S = qkv.shape[0]
    tb = 1
    ng = GROUP
    T = tb * BLK
    nt = S // T

    def body(sink_ref, q_ref, kv_ref, kvp_ref, bias_ref, do_ref, dq_ref, dkv_ref, dbias_ref, dsink_ref, carry):
        i = pl.program_id(0)

        @pl.when(i == 0)
        def _():
            carry[...] = jnp.zeros_like(carry)
            dbias_ref[...] = jnp.zeros_like(dbias_ref)
            dsink_ref[...] = jnp.zeros_like(dsink_ref)

        from_next = carry[...]
        for j in reversed(range(tb)):
            rows = slice(j * BLK, (j + 1) * BLK)
            q, kvc, do_ = q_ref[rows, :], kv_ref[rows, :], do_ref[rows, :]
            kvp = kvp_ref[...] if j == 0 else kv_ref[(j - 1) * BLK:j * BLK, :]
            has_prev = (i < nt - 1) if j == 0 else None
            dqs, dk_cur, dv_cur, dk_prev, dv_prev = [], [], [], [], []
            head_row = lax.broadcasted_iota(jnp.int32, (N_HEADS, 128), 0)
            dsink_rows = jnp.zeros((N_HEADS, 128), F32)
            for kh in range(N_KV):
                dk2, dv2 = None, None
                for g0 in range(0, GROUP, ng):
                    q4, k2, v2, s, sink = _attn_scores(q, kvc, kvp, bias_ref, sink_ref, has_prev, kh, g0, ng)
                    m = jnp.maximum(jnp.max(s, axis=1, keepdims=True), sink)
                    p = jnp.exp(s - m)
                    denom = jnp.sum(p, axis=1, keepdims=True) + jnp.exp(sink - m)
                    prob = p / denom
                    p_sink = jnp.exp(sink - m) / denom
                    pb = prob.astype(BF)
                    do4 = jnp.concatenate([do_[:, (kh * GROUP + g) * HD:(kh * GROUP + g + 1) * HD]
                                           for g in range(g0, g0 + ng)], axis=0)
                    dp = _dot_nt(do4, v2)
                    o4 = _dot(pb, v2)
                    delta = jnp.sum(do4.astype(F32) * o4, axis=1, keepdims=True)
                    ds = prob * (dp - delta)
                    dbias_ref[kh, g0 * BLK:(g0 + ng) * BLK, :] += ds
                    sink_term = p_sink * delta
                    for g in range(ng):
                        val = -jnp.sum(sink_term[g * BLK:(g + 1) * BLK], axis=0, keepdims=True)
                        dsink_rows = jnp.where(head_row == kh * GROUP + g0 + g, val, dsink_rows)
                    dsb = ds.astype(BF)
                    dq4 = _dot(dsb, k2) * SCALE
                    dk_part = jnp.transpose(_dot_tn(q4, dsb)) * SCALE
                    dv_part = jnp.transpose(_dot_tn(do4, pb))
                    dk2 = dk_part if dk2 is None else dk2 + dk_part
                    dv2 = dv_part if dv2 is None else dv2 + dv_part
                    dqs += [dq4[g * BLK:(g + 1) * BLK] for g in range(ng)]
                dk_prev.append(dk2[0:BLK])
                dk_cur.append(dk2[BLK:2 * BLK])
                dv_prev.append(dv2[0:BLK])
                dv_cur.append(dv2[BLK:2 * BLK])
            dsink_ref[...] += dsink_rows
            dq_ref[rows, :] = jnp.concatenate(dqs, axis=1).astype(BF)
            dkv_ref[rows, :] = (jnp.concatenate(dk_cur + dv_cur, axis=1) + from_next).astype(BF)
            from_next = jnp.concatenate(dk_prev + dv_prev, axis=1)
        carry[...] = from_next

    return pl.pallas_call(
        body, name="attn_bwd", grid=(nt,),
        out_shape=[_sds((S, Q_W), BF), _sds((S, 2 * KV_W), BF),
                   _sds((N_KV, GROUP * BLK, 2 * BLK), F32), _sds((N_HEADS, 128), F32)],
        in_specs=[pl.BlockSpec(memory_space=pltpu.SMEM),
                  pl.BlockSpec((T, Q_W), lambda i: (nt - 1 - i, 0)),
                  pl.BlockSpec((T, 2 * KV_W), lambda i: (nt - 1 - i, 2)),
                  pl.BlockSpec((BLK, 2 * KV_W), lambda i: (jnp.maximum((nt - 1 - i) * tb - 1, 0), 2)),
                  _const((N_KV, GROUP * BLK, 2 * BLK)),
                  pl.BlockSpec((T, Q_W), lambda i: (nt - 1 - i, 0))],
        out_specs=[pl.BlockSpec((T, Q_W), lambda i: (nt - 1 - i, 0)),
                   pl.BlockSpec((T, 2 * KV_W), lambda i: (nt - 1 - i, 0)),
                   _const((N_KV, GROUP * BLK, 2 * BLK)), _const((N_HEADS, 128))],
        scratch_shapes=[pltpu.VMEM((BLK, 2 * KV_W), F32)],
        compiler_params=_cp(1, 32),
    )(sinks, qkv, qkv, qkv, bias, do)


def _rel_bias_grad(dbias, bucket):
    def body(db_ref, bk_ref, out_ref):
        bk = bk_ref[...]
        lane = lax.broadcasted_iota(jnp.int32, (1, 128), 1)
        for h in range(N_HEADS):
            d = db_ref[h // GROUP, pl.ds((h % GROUP) * BLK, BLK), :]
            row = jnp.zeros((1, 128), F32)
            for b in range(N_BUCKETS):
                tot = jnp.sum(jnp.sum(jnp.where(bk == b, d, 0.0), axis=1, keepdims=True), axis=0, keepdims=True)
                row = jnp.where(lane == b, tot, row)
            out_ref[pl.ds(h, 1), :] = row

    vm = pl.BlockSpec(memory_space=pltpu.VMEM)
    return pl.pallas_call(body, name="rel_bias_grad", out_shape=_sds((N_HEADS, 128), F32),
                          in_specs=[vm, vm], out_specs=vm)(dbias, bucket)


def _gmlp_parts(zg, lg_ref, lb_ref):
    z = zg.astype(F32)
    ge = _gelu(z)
    u, vg = ge[:, 0:G_W], ge[:, G_W:2 * G_W]
    mu = jnp.mean(vg, axis=-1, keepdims=True)
    xc = vg - mu
    rstd = lax.rsqrt(jnp.mean(xc * xc, axis=-1, keepdims=True) + EPS)
    xh = xc * rstd
    return z, u, xh, rstd, xh * lg_ref[...] + lb_ref[...]


def _causal_weights(ws_ref, wc):
    t = lax.broadcasted_iota(jnp.int32, (BLK, BLK), 0)
    s = lax.broadcasted_iota(jnp.int32, (BLK, BLK), 1)
    for g in range(N_HEADS):
        wc[g] = jnp.where(s <= t, ws_ref[g], 0.0).astype(BF)


def _spatial(vb, wc, bst_ref, p, low):
    xp = vb[:, p * 128:(p + 1) * 128]
    s0 = _dot(wc[2 * p], xp) + bst_ref[:, 2 * p:2 * p + 1]
    s1 = _dot(wc[2 * p + 1], xp) + bst_ref[:, 2 * p + 1:2 * p + 2]
    return xp, jnp.where(low, s0, s1)


def _gmlp_fwd(zg, lg, lb, ws, bst):
    S = zg.shape[0]
    tb = min(ATT_TB, S // BLK)
    T = tb * BLK

    def body(zg_ref, lg_ref, lb_ref, ws_ref, bst_ref, o_ref, wc):
        @pl.when(pl.program_id(0) == 0)
        def _():
            _causal_weights(ws_ref, wc)
        low = lax.broadcasted_iota(jnp.int32, (BLK, 128), 1) < HD
        for j in range(tb):
            rows = slice(j * BLK, (j + 1) * BLK)
            _, u, _, _, vln = _gmlp_parts(zg_ref[rows, :], lg_ref, lb_ref)
            vb = vln.astype(BF)
            for p in range(4):
                _, sp = _spatial(vb, wc, bst_ref, p, low)
                o_ref[rows, p * 128:(p + 1) * 128] = (u[:, p * 128:(p + 1) * 128] * sp).astype(BF)

    return pl.pallas_call(
        body, name="gmlp_fwd", grid=(S // T,),
        out_shape=_sds((S, G_W), BF),
        in_specs=[pl.BlockSpec((T, 2 * G_W), lambda i: (i, 0)), _const((1, G_W)), _const((1, G_W)),
                  _const((N_HEADS, BLK, BLK)), _const((BLK, N_HEADS))],
        out_specs=pl.BlockSpec((T, G_W), lambda i: (i, 0)),
        scratch_shapes=[pltpu.VMEM((N_HEADS, BLK, BLK), BF)],
        compiler_params=_cp(1, 32),
    )(zg, lg, lb, ws, bst)


def _gmlp_bwd(zg, d_out, lg, lb, ws, bst):
    S = zg.shape[0]
    tb = min(ATT_TB, S // BLK)
    T = tb * BLK
    nb = S // T

    def body(zg_ref, d_ref, lg_ref, lb_ref, ws_ref, bst_ref, dzg_ref, dws_ref, dbs_ref, dlg_ref, dlb_ref, wc, dbacc):
        i = pl.program_id(0)

        @pl.when(i == 0)
        def _():
            _causal_weights(ws_ref, wc)
            dws_ref[...] = jnp.zeros_like(dws_ref)
            dlg_ref[...] = jnp.zeros_like(dlg_ref)
            dlb_ref[...] = jnp.zeros_like(dlb_ref)
            dbacc[...] = jnp.zeros_like(dbacc)

        low = lax.broadcasted_iota(jnp.int32, (BLK, 128), 1) < HD
        for j in range(tb):
            rows = slice(j * BLK, (j + 1) * BLK)
            z, u, xh, rstd, vln = _gmlp_parts(zg_ref[rows, :], lg_ref, lb_ref)
            vb = vln.astype(BF)
            d = d_ref[rows, :].astype(F32)
            du_parts, dvln_parts = [], []
            for p in range(4):
                xp, sp = _spatial(vb, wc, bst_ref, p, low)
                dp = d[:, p * 128:(p + 1) * 128]
                du_parts.append(dp * sp)
                dsp = dp * u[:, p * 128:(p + 1) * 128]
                dbacc[:, p * 128:(p + 1) * 128] += dsp
                d0 = jnp.where(low, dsp, 0.0).astype(BF)
                d1 = jnp.where(low, 0.0, dsp).astype(BF)
                dws_ref[2 * p] += _dot_nt(d0, xp)
                dws_ref[2 * p + 1] += _dot_nt(d1, xp)
                dvln_parts.append(_dot_tn(wc[2 * p], d0) + _dot_tn(wc[2 * p + 1], d1))
            dvln = jnp.concatenate(dvln_parts, axis=1)
            dlg_ref[...] += _colsum(dvln * xh)
            dlb_ref[...] += _colsum(dvln)
            dxh = dvln * lg_ref[...]
            dvg = rstd * (dxh - jnp.mean(dxh, axis=-1, keepdims=True)
                          - xh * jnp.mean(dxh * xh, axis=-1, keepdims=True))
            dge = jnp.concatenate(du_parts + [dvg], axis=1)
            dzg_ref[rows, :] = (dge * _gelu_grad(z)).astype(BF)

        @pl.when(i == nb - 1)
        def _():
            t = lax.broadcasted_iota(jnp.int32, (BLK, BLK), 0)
            s = lax.broadcasted_iota(jnp.int32, (BLK, BLK), 1)
            for g in range(N_HEADS):
                dws_ref[g] = jnp.where(s <= t, dws_ref[g], 0.0)
            grp = lax.broadcasted_iota(jnp.int32, (N_HEADS, G_W), 0)
            lane = lax.broadcasted_iota(jnp.int32, (N_HEADS, G_W), 1) // HD
            pick = jnp.where(grp == lane, 1.0, 0.0).astype(F32)
            dbs_ref[...] = lax.dot_general(pick, dbacc[...], (((1,), (1,)), ((), ())),
                                           preferred_element_type=F32, precision=HIGH)

    return pl.pallas_call(
        body, name="gmlp_bwd", grid=(nb,),
        out_shape=[_sds((S, 2 * G_W), BF), _sds((N_HEADS, BLK, BLK), F32), _sds((N_HEADS, BLK), F32),
                   _sds((1, G_W), F32), _sds((1, G_W), F32)],
        in_specs=[pl.BlockSpec((T, 2 * G_W), lambda i: (i, 0)), pl.BlockSpec((T, G_W), lambda i: (i, 0)),
                  _const((1, G_W)), _const((1, G_W)), _const((N_HEADS, BLK, BLK)), _const((BLK, N_HEADS))],
        out_specs=[pl.BlockSpec((T, 2 * G_W), lambda i: (i, 0)), _const((N_HEADS, BLK, BLK)),
                   _const((N_HEADS, BLK)), _const((1, G_W)), _const((1, G_W))],
        scratch_shapes=[pltpu.VMEM((N_HEADS, BLK, BLK), BF), pltpu.VMEM((BLK, G_W), F32)],
        compiler_params=_cp(1, 32),
    )(zg, d_out, lg, lb, ws, bst)


def _mix_out(o, gm, gates, h, wa, wg, wo, gate, gp):
    S = h.shape[0]
    R = min(512, S)

    def body(o_ref, gm_ref, gates_ref, h_ref, wa_ref, wg_ref, wo_ref, gate_ref, gp_ref,
             ya_ref, yg_ref, ym_ref, y_ref, hn_ref):
        for r0 in range(0, R, CHUNK):
            rows = slice(r0, r0 + CHUNK)
            ya = _dot(o_ref[rows, :], wa_ref[...])
            yg = _dot(gm_ref[rows, :], wg_ref[...])
            ya_ref[rows, :] = ya.astype(BF)
            yg_ref[rows, :] = yg.astype(BF)
            ym = (gates_ref[rows, 0:D].astype(F32) * ya + gates_ref[rows, D:2 * D].astype(F32) * yg).astype(BF)
            ym_ref[rows, :] = ym
            y = _dot(ym, wo_ref[...])
            y_ref[rows, :] = y.astype(BF)
            hn_ref[rows, :] = h_ref[rows, :] + gate_ref[...] * (y * _rms_r(y) * gp_ref[...])

    vec = _const((1, D))
    rows = lambda w_: pl.BlockSpec((R, w_), lambda i: (i, 0))
    return pl.pallas_call(
        body, name="mix_out", grid=(S // R,),
        out_shape=[_sds((S, D), BF)] * 4 + [_sds((S, D), F32)],
        in_specs=[rows(Q_W), rows(G_W), rows(2 * D), rows(D), _resident((Q_W, D)), _resident((G_W, D)),
                  _resident((D, D)), vec, vec],
        out_specs=[rows(D)] * 5,
        compiler_params=_cp(1, 48),
    )(o, gm, gates, h, wa, wg, wo, gate, gp)


def _mix_out_bwd(dh, y, ya, yg, gates, att, gm, ymix, wa, wg, wo, gate, gp):
    S = dh.shape[0]
    R = min(512, S)
    nb = S // R

    def body(dh_ref, y_ref, ya_ref, yg_ref, gates_ref, att_ref, gm_ref, ym_ref, wa_ref, wg_ref, wo_ref,
             gate_ref, gp_ref, dz_ref, do_ref, dgm_ref, dgate_ref, dgp_ref, gwo_ref, gwa_ref, gwg_ref,
             acc_o, acc_a, acc_g, dy_scr, dya_scr, dyg_scr):
        i = pl.program_id(0)

        @pl.when(i == 0)
        def _():
            for r in (dgate_ref, dgp_ref, acc_o, acc_a, acc_g):
                r[...] = jnp.zeros_like(r)
        for r0 in range(0, R, CHUNK):
            rows = slice(r0, r0 + CHUNK)
            dy, dgate, dgp = _postnorm_bwd(dh_ref[rows, :], y_ref[rows, :], gate_ref[...], gp_ref[...], 1.0)
            dgate_ref[...] += dgate
            dgp_ref[...] += dgp
            dyb = dy.astype(BF)
            dy_scr[rows, :] = dyb
            dym = _dot_nt(dyb, wo_ref[...])
            ga = gates_ref[rows, 0:D].astype(F32)
            gg = gates_ref[rows, D:2 * D].astype(F32)
            dya = (dym * ga).astype(BF)
            dyg = (dym * gg).astype(BF)
            dya_scr[rows, :] = dya
            dyg_scr[rows, :] = dyg
            dz_ref[rows, 0:D] = (dym * ya_ref[rows, :].astype(F32) * (ga * (1.0 - ga))).astype(BF)
            dz_ref[rows, D:2 * D] = (dym * yg_ref[rows, :].astype(F32) * (gg * (1.0 - gg))).astype(BF)
            do_ref[rows, :] = _dot_nt(dya, wa_ref[...]).astype(BF)
            dgm_ref[rows, :] = _dot_nt(dyg, wg_ref[...]).astype(BF)
        for m0 in range(0, D, CHUNK):
            acc_o[m0:m0 + CHUNK, :] += _dot_tn(ym_ref[:, m0:m0 + CHUNK], dy_scr[...])
        for m0 in range(0, Q_W, CHUNK):
            acc_a[m0:m0 + CHUNK, :] += _dot_tn(att_ref[:, m0:m0 + CHUNK], dya_scr[...])
            acc_g[m0:m0 + CHUNK, :] += _dot_tn(gm_ref[:, m0:m0 + CHUNK], dyg_scr[...])

        @pl.when(i == nb - 1)
        def _():
            for m0 in range(0, D, CHUNK):
                gwo_ref[m0:m0 + CHUNK, :] = acc_o[m0:m0 + CHUNK, :].astype(BF)
            for m0 in range(0, Q_W, CHUNK):
                gwa_ref[m0:m0 + CHUNK, :] = acc_a[m0:m0 + CHUNK, :].astype(BF)
                gwg_ref[m0:m0 + CHUNK, :] = acc_g[m0:m0 + CHUNK, :].astype(BF)

    vec = _const((1, D))
    rows = lambda w_: pl.BlockSpec((R, w_), lambda i: (i, 0))
    return pl.pallas_call(
        body, name="mix_out_bwd", grid=(nb,),
        out_shape=[_sds((S, 2 * D), BF), _sds((S, Q_W), BF), _sds((S, G_W), BF), _sds((1, D), F32),
                   _sds((1, D), F32), _sds((D, D), BF), _sds((Q_W, D), BF), _sds((G_W, D), BF)],
        in_specs=[rows(D), rows(D), rows(D), rows(D), rows(2 * D), rows(Q_W), rows(G_W), rows(D),
                  _resident((Q_W, D)), _resident((G_W, D)), _resident((D, D)), vec, vec],
        out_specs=[rows(2 * D), rows(Q_W), rows(G_W), vec, vec, _const((D, D)), _const((Q_W, D)),
                   _const((G_W, D))],
        scratch_shapes=[pltpu.VMEM((D, D), F32), pltpu.VMEM((Q_W, D), F32), pltpu.VMEM((G_W, D), F32)]
        + [pltpu.VMEM((R, D), BF)] * 3,
        compiler_params=_cp(1, 60),
    )(dh, y, ya, yg, gates, att, gm, ymix, wa, wg, wo, gate, gp)


def _mix_dn(dq, dkv, dzg, dzgate, w, h, dh, sc, gp):
    S = h.shape[0]
    R = min(512, S)

    def body(dq_ref, dkv_ref, dzg_ref, dzt_ref, w_ref, h_ref, dh_ref, sc_ref, gp_ref,
             out_ref, dsh_ref, dsc_ref, dgp_ref):
        @pl.when(pl.program_id(0) == 0)
        def _():
            dsh_ref[...] = jnp.zeros_like(dsh_ref)
            dsc_ref[...] = jnp.zeros_like(dsc_ref)
            dgp_ref[...] = jnp.zeros_like(dgp_ref)
        for r0 in range(0, R, CHUNK):
            rows = slice(r0, r0 + CHUNK)
            dn = _dot(dq_ref[rows, :], w_ref[0:Q_W, :])
            dn = dn + _dot(dkv_ref[rows, :], w_ref[Q_W:QKV_W, :])
            dn = dn + _dot(dzg_ref[rows, :], w_ref[ZG_OFF:GATE_OFF, :])
            dn = dn + _dot(dzt_ref[rows, :], w_ref[GATE_OFF:IN_W, :])
            dx, dsh, dsc, dgp = _prenorm_bwd(dn, h_ref[rows, :], gp_ref[...], sc_ref[...])
            out_ref[rows, :] = dh_ref[rows, :] + dx
            dsh_ref[...] += dsh
            dsc_ref[...] += dsc
            dgp_ref[...] += dgp

    vec = _const((1, D))
    rows = lambda w_: pl.BlockSpec((R, w_), lambda i: (i, 0))
    return pl.pallas_call(
        body, name="mix_dn", grid=(S // R,),
        out_shape=[_sds((S, D), F32)] + [_sds((1, D), F32)] * 3,
        in_specs=[rows(Q_W), rows(2 * KV_W), rows(2 * G_W), rows(2 * D), _resident((IN_W, D)),
                  rows(D), rows(D), vec, vec],
        out_specs=[rows(D), vec, vec, vec],
        compiler_params=_cp(1, 48),
    )(dq, dkv, dzg, dzgate, w, h, dh, sc, gp)


def _adamw_math(w, g, m, v):
    m2 = ADAM_B1 * m + (1.0 - ADAM_B1) * g
    v2 = ADAM_B2 * v + (1.0 - ADAM_B2) * (g * g)
    m_hat = m2 / (1.0 - ADAM_B1 ** ADAM_STEP)
    v_hat = v2 / (1.0 - ADAM_B2 ** ADAM_STEP)
    delta = -ADAM_LR * (m_hat / (jnp.sqrt(v_hat) + ADAM_EPS) + ADAM_WD * w)
    return delta, m2, v2


def _row_tile(rows, cols):
    best = None
    for t in range(16, rows + 1, 16):
        if rows % t == 0 and t * cols <= 256 * 1024:
            best = t
    return best if best is not None else rows


def _adamw_sharded(landing, w, m, v, name):
    r, c = w.shape
    tr = _row_tile(r, c)

    def body(l_ref, w_ref, m_ref, v_ref, g_ref, d_ref, m2_ref, v2_ref):
        g = l_ref[0].astype(F32)
        for j in range(1, N_DEV):
            g = g + l_ref[j].astype(F32)
        delta, m2, v2 = _adamw_math(w_ref[...], g, m_ref[...], v_ref[...])
        g_ref[...] = g
        d_ref[...] = delta
        m2_ref[...] = m2
        v2_ref[...] = v2

    row = pl.BlockSpec((tr, c), lambda i: (i, 0))
    return pl.pallas_call(
        body, name=name, grid=(r // tr,),
        out_shape=[_sds((r, c), F32)] * 4,
        in_specs=[pl.BlockSpec((N_DEV, tr, c), lambda i: (0, i, 0)), row, row, row],
        out_specs=[row] * 4,
        compiler_params=_cp(1, 48),
    )(landing, w, m, v)


def _adamw_small(items):
    n = len(items)

    def body(*refs):
        for k in range(n):
            w_ref, g_ref, m_ref, v_ref = refs[4 * k:4 * k + 4]
            outs = refs[4 * n + 3 * k:4 * n + 3 * k + 3]
            for o_ref, val in zip(outs, _adamw_math(w_ref[...], g_ref[...], m_ref[...], v_ref[...])):
                o_ref[...] = val

    vm = pl.BlockSpec(memory_space=pltpu.VMEM)
    flat = pl.pallas_call(
        body, name="adamw_small",
        out_shape=[_sds(it[0].shape, F32) for it in items for _ in range(3)],
        in_specs=[vm] * (4 * n), out_specs=[vm] * (3 * n),
    )(*[a for it in items for a in it])
    return [tuple(flat[3 * k:3 * k + 3]) for k in range(n)]


def _w_ada_update(c8, d_ada, w, m, v):
    tr = 256

    def body(c_ref, d_ref, w_ref, m_ref, v_ref, g_ref, dl_ref, m2_ref, v2_ref):
        cs = c_ref[...]
        cs = cs * jax.nn.sigmoid(cs)
        g = lax.dot_general(cs, d_ref[...], (((0,), (0,)), ((), ())), preferred_element_type=F32, precision=HIGH)
        delta, m2, v2 = _adamw_math(w_ref[...], g, m_ref[...], v_ref[...])
        g_ref[...] = g
        dl_ref[...] = delta
        m2_ref[...] = m2
        v2_ref[...] = v2

    row = pl.BlockSpec((tr, ADA_W), lambda i: (i, 0))
    return pl.pallas_call(
        body, name="w_ada_update", grid=(D // tr,),
        out_shape=[_sds((D, ADA_W), F32)] * 4,
        in_specs=[pl.BlockSpec((N_DEV, tr), lambda i: (0, i)), _const((N_DEV, ADA_W)), row, row, row],
        out_specs=[row] * 4,
        compiler_params=_cp(1, 40),
    )(c8, d_ada, w, m, v)


def _t5_bucket():
    qi = jnp.arange(BLK, dtype=jnp.int32)[:, None]
    kj = jnp.arange(2 * BLK, dtype=jnp.int32)[None, :]
    dist = jnp.maximum(qi + BLK - kj, 0)
    max_exact = N_BUCKETS // 2
    d_f = jnp.maximum(dist, max_exact).astype(F32)
    large = max_exact + (jnp.log(d_f / max_exact) / math.log(MAX_DISTANCE / max_exact)
                         * (N_BUCKETS - max_exact)).astype(jnp.int32)
    large = jnp.minimum(large, N_BUCKETS - 1)
    return jnp.where(dist < max_exact, dist, large)


def _slabs_of_columns(w):
    r, c8 = w.shape
    return jnp.transpose(w.reshape(r, N_DEV, c8 // N_DEV), (1, 0, 2))


def _columns_of_slabs(w8):
    _, r, c = w8.shape
    return jnp.transpose(w8, (1, 0, 2)).reshape(r, N_DEV * c)


def kernel(x, c, rel_bias, w_ada, b_ada, pre_norm_g, post_norm_g, w_ffn1_in, w_ffn1_out, w_in, sinks, gmlp_ln_g, gmlp_ln_b, gmlp_w_s, gmlp_b_s, w_br_attn, w_br_gmlp, w_out, w_ffn2_in, w_ffn2_out, loss_target, m_rel_bias, m_w_ada, m_b_ada, m_pre_norm_g, m_post_norm_g, m_w_ffn1_in, m_w_ffn1_out, m_w_in, m_sinks, m_gmlp_ln_g, m_gmlp_ln_b, m_gmlp_w_s, m_gmlp_b_s, m_w_br_attn, m_w_br_gmlp, m_w_out, m_w_ffn2_in, m_w_ffn2_out, v_rel_bias, v_w_ada, v_b_ada, v_pre_norm_g, v_post_norm_g, v_w_ffn1_in, v_w_ffn1_out, v_w_in, v_sinks, v_gmlp_ln_g, v_gmlp_ln_b, v_gmlp_w_s, v_gmlp_b_s, v_w_br_attn, v_w_br_gmlp, v_w_out, v_w_ffn2_in, v_w_ffn2_out):
    me = 4 * lax.axis_index("x") + 2 * lax.axis_index("y") + lax.axis_index("c")
    x0 = x[0]
    target = loss_target[0]

    transposed = ("w_ffn1_in", "w_in", "w_ffn2_in")
    shards = [w_ffn1_in[0].T, w_ffn1_out[0], w_in[0].T, w_br_attn[0], w_br_gmlp[0], w_out[0],
              w_ffn2_in[0].T, w_ffn2_out[0]]
    shards_bf = [s.astype(BF) for s in shards]
    groups = [shards_bf[0:1], shards_bf[1:6], shards_bf[6:8]]

    def gather_start(i, after):
        return _slabs_start("gather", groups[i], after, "gather_start_%d" % i)

    def forward_start(st, i, after):
        lands = _slabs_wait("gather", len(groups[i]), st, after, "gather_wait_%d" % i)
        return _slabs_start("forward", lands, c, "forward_start_%d" % i)

    def gathered(st, i, after):
        return _slabs_wait("forward", len(groups[i]), st, after, "forward_wait_%d" % i)

    gs0 = gather_start(0, c)

    mine = jnp.concatenate([c[0], pre_norm_g[0].reshape(-1), post_norm_g[0].reshape(-1)])
    small8 = jnp.broadcast_to(mine[None, :], (8, mine.shape[0]))
    b_ada64 = jnp.repeat(b_ada.reshape(N_DEV, ADA_W), 8, axis=0)
    gath, ada64 = _ada_forward(small8, w_ada[0], b_ada64)
    gath8 = gath[::8]
    ada = ada64[::8].reshape(9, D)
    sh1, sc1, g1, sh2, sc2, g2, sh3, sc3, g3 = [ada[k:k + 1] for k in range(9)]
    gains = gath8[:, D:].reshape(N_DEV, 2, 3, 128)
    pre_g = jnp.transpose(gains[:, 0], (1, 0, 2)).reshape(3, D)
    post_g = jnp.transpose(gains[:, 1], (1, 0, 2)).reshape(3, D)
    pre = [pre_g[k:k + 1] for k in range(3)]
    post = [post_g[k:k + 1] for k in range(3)]

    bucket = _t5_bucket()
    bias = _bias_table(rel_bias, bucket)
    sinks8 = sinks[0]
    lg, lb = gmlp_ln_g, gmlp_ln_b
    ws = gmlp_w_s[0]
    bst = jnp.transpose(gmlp_b_s[0])

    fs0 = forward_start(gs0, 0, sh1)
    gs1 = gather_start(1, fs0[-1])
    wf1_in = gathered(fs0, 0, gs1[-1])[0].reshape(2 * D_FF, D)
    n1, fg1, fu1, fa1 = _ffn_in(x0, sh1, sc1, pre[0], wf1_in, "ffn1_in")
    fs1 = forward_start(gs1, 1, n1)
    gs2 = gather_start(2, fs1[-1])
    mix_w = gathered(fs1, 1, gs2[-1])
    wf1_out = mix_w[0].reshape(D_FF, D)
    w_in_full = mix_w[1].reshape(IN_W, D)
    w_bra = _columns_of_slabs(mix_w[2])
    w_brg = _columns_of_slabs(mix_w[3])
    w_out_full = mix_w[4].reshape(D, D)
    h1, y1 = _ffn_out(fa1, wf1_out, x0, g1, post[0], "ffn1_out")
    n2, qkv, zg, gates = _mix_in(h1, sh2, sc2, pre[1], w_in_full)
    att = _attn_fwd(qkv, bias, sinks8)
    gm = _gmlp_fwd(zg, lg, lb, ws, bst)
    fs2 = forward_start(gs2, 2, gm)
    ya, yg, ymix, y2, h2 = _mix_out(att, gm, gates, h1, w_bra, w_brg, w_out_full, g2 + fs2[-1], post[1])
    wf2_in, wf2_out = gathered(fs2, 2, h2)
    wf2_in = wf2_in.reshape(2 * D_FF, D)
    wf2_out = wf2_out.reshape(D_FF, D)
    n3, fg3, fu3, fa3 = _ffn_in(h2, sh3, sc3, pre[2], wf2_in, "ffn2_in")
    dh3, y3, sq = _ffn_out(fa3, wf2_out, h2, g3, post[2], "ffn2_out", target=target)
    loss = lax.psum(0.5 * sq[0, 0] / D, ("x", "y", "c"))

    def exchange_start(i, arrays):
        return _slabs_start("exchange", arrays, sq, "exchange_start_%d" % i)

    dy3, dgu3, dh2, d_g3, d_post2, d_sh3, d_sc3, d_pre2 = _ffn_bwd(
        dh3, y3, fg3, fu3, wf2_out, wf2_in, h2, g3, post[2], sc3, pre[2], "ffn2_bwd")
    gw_f2_out = _tn_matmul(fa3, dy3, "ffn2_out_wgrad", tm=D_FF // 2).reshape(N_DEV, D_FF // N_DEV, D)
    gw_f2_in = _tn_matmul(dgu3, n3, "ffn2_in_wgrad", tm=D_FF // 2).reshape(N_DEV, FS, D)
    ex1 = exchange_start(1, [gw_f2_out, gw_f2_in])

    dzgate, d_att, d_gm, d_g2, d_post1, gw_out, gw_bra, gw_brg = _mix_out_bwd(
        dh2, y2, ya, yg, gates, att, gm, ymix, w_bra, w_brg, w_out_full, g2 + ex1[-1], post[1])
    ex2 = exchange_start(2, [_slabs_of_columns(gw_bra), _slabs_of_columns(gw_brg),
                             gw_out.reshape(N_DEV, D // N_DEV, D)])
    dq, dkv, dbias, dsink = _attn_bwd(qkv, bias, sinks8, d_att)
    dzg, d_ws, d_bs, d_lg, d_lb = _gmlp_bwd(zg, d_gm, lg, lb, ws, bst)
    d_rel = _rel_bias_grad(dbias, bucket)
    early = jnp.concatenate([
        jnp.concatenate([d_lg.reshape(4, 128), d_lb.reshape(4, 128)], axis=0),
        d_bs, d_rel, dsink, d_ws.reshape(N_HEADS * BLK, BLK)], axis=0)
    sm0 = _slabs_start("gather_all", [early], sq, "small_gather_start")
    dh1, d_sh2, d_sc2, d_pre1 = _mix_dn(dq, dkv, dzg, dzgate, w_in_full, h1, dh2, sc2 + ex2[-1] + sm0[-1], pre[1])
    gw_in = jnp.concatenate(
        [_tn_matmul(dq, n2, "w_in_q_wgrad"), _tn_matmul(dkv, n2, "w_in_kv_wgrad"),
         _tn_matmul(dzg, n2, "w_in_zg_wgrad"), _tn_matmul(dzgate, n2, "w_in_gate_wgrad")],
        axis=0).reshape(N_DEV, IN_W // N_DEV, D)
    ex3 = exchange_start(3, [gw_in])

    dy1, dgu1, d_g1, d_post0 = _ffn_out_bwd(dh1, y1, fg1, fu1, wf1_out, g1 + ex3[-1], post[0], "ffn1_out_bwd")
    gw_f1_out = _tn_matmul(fa1, dy1, "ffn1_out_wgrad", tm=D_FF // 2).reshape(N_DEV, D_FF // N_DEV, D)
    gw_f1_in = _tn_matmul(dgu1, n1, "ffn1_in_wgrad", tm=D_FF // 2).reshape(N_DEV, FS, D)
    ex4 = exchange_start(4, [gw_f1_out, gw_f1_in])
    grad_x, d_sh1, d_sc1, d_pre0 = _ffn_dn(dgu1, wf1_in, x0, dh1, sc1 + ex4[-1], pre[0], "ffn1_dn")

    landed = {}
    for i, (ex, nms) in enumerate([(ex1, ["w_ffn2_out", "w_ffn2_in"]),
                                   (ex2, ["w_br_attn", "w_br_gmlp", "w_out"]), (ex3, ["w_in"]),
                                   (ex4, ["w_ffn1_out", "w_ffn1_in"])]):
        for nm, land in zip(nms, _slabs_wait("exchange", len(nms), ex, grad_x, "exchange_wait_%d" % i)):
            landed[nm] = land
    moments = [(m_w_ffn1_in, v_w_ffn1_in), (m_w_ffn1_out, v_w_ffn1_out), (m_w_in, v_w_in),
               (m_w_br_attn, v_w_br_attn), (m_w_br_gmlp, v_w_br_gmlp), (m_w_out, v_w_out),
               (m_w_ffn2_in, v_w_ffn2_in), (m_w_ffn2_out, v_w_ffn2_out)]
    names = ["w_ffn1_in", "w_ffn1_out", "w_in", "w_br_attn", "w_br_gmlp", "w_out", "w_ffn2_in", "w_ffn2_out"]
    big = {}
    for nm, w_, (m_, v_) in zip(names, shards, moments):
        if nm in transposed:
            res4 = _adamw_sharded(landed[nm], w_, m_[0].T, v_[0].T, "adamw_" + nm)
            big[nm] = [a.T[None] for a in res4]
        else:
            big[nm] = [a[None] for a in _adamw_sharded(landed[nm], w_, m_[0], v_[0], "adamw_" + nm)]

    d_ada = jnp.concatenate([v_.reshape(8, 128) for v_ in
                             (d_sh1, d_sc1, d_g1, d_sh2, d_sc2, d_g2, d_sh3, d_sc3, d_g3)], axis=0)
    d_pre = jnp.concatenate([d_pre0, d_pre1, d_pre2], axis=0)
    d_post = jnp.concatenate([d_post0, d_post1, d_post2], axis=0)
    late = jnp.concatenate([d_ada, _slabs_of_columns(d_pre).reshape(24, 128),
                            _slabs_of_columns(d_post).reshape(24, 128)], axis=0)
    late, _ = lax.optimization_barrier((late, landed["w_ffn1_in"]))
    tot, every = _small_allreduce(late)
    (early_land,) = _slabs_wait("gather_all", 1, sm0, grad_x, "small_gather_wait")
    tot_early = _sum_slabs(early_land)

    g_b_ada = tot[0:72].reshape(1, 9 * D)
    g_pre = lax.dynamic_slice_in_dim(tot[72:96], 3 * me, 3, axis=0)[None]
    g_post = lax.dynamic_slice_in_dim(tot[96:120], 3 * me, 3, axis=0)[None]
    g_lg = tot_early[0:4].reshape(1, G_W)
    g_lb = tot_early[4:8].reshape(1, G_W)
    g_bs = tot_early[8:16][None]
    g_rel = jnp.transpose(tot_early[16:24, 0:N_BUCKETS])
    g_sinks = tot_early[24:32, 0][None]
    g_ws = tot_early[32:1056].reshape(1, N_HEADS, BLK, BLK)

    d_ada_mine = lax.dynamic_slice_in_dim(every[:, 0:72].reshape(N_DEV, N_DEV, ADA_W), me, 1, axis=1)[:, 0]
    ada_out = [a[None] for a in _w_ada_update(gath8[:, 0:D], d_ada_mine, w_ada[0], m_w_ada[0], v_w_ada[0])]

    small = [("rel_bias", rel_bias, g_rel, m_rel_bias, v_rel_bias), ("b_ada", b_ada, g_b_ada, m_b_ada, v_b_ada),
             ("pre_norm_g", pre_norm_g, g_pre, m_pre_norm_g, v_pre_norm_g),
             ("post_norm_g", post_norm_g, g_post, m_post_norm_g, v_post_norm_g),
             ("sinks", sinks, g_sinks, m_sinks, v_sinks), ("gmlp_ln_g", gmlp_ln_g, g_lg, m_gmlp_ln_g, v_gmlp_ln_g),
             ("gmlp_ln_b", gmlp_ln_b, g_lb, m_gmlp_ln_b, v_gmlp_ln_b),
             ("gmlp_w_s", gmlp_w_s, g_ws, m_gmlp_w_s, v_gmlp_w_s), ("gmlp_b_s", gmlp_b_s, g_bs, m_gmlp_b_s, v_gmlp_b_s)]
    two_d = lambda a: a.reshape(int(math.prod(a.shape[:-1])), a.shape[-1])
    stepped = _adamw_small([tuple(two_d(a) for a in item[1:]) for item in small])
    res = {"w_ada": ada_out}
    for (nm, w_, g_, _, _), new in zip(small, stepped):
        res[nm] = [g_] + [a.reshape(w_.shape) for a in new]
    res.update(big)
    order = ["rel_bias", "w_ada", "b_ada", "pre_norm_g", "post_norm_g", "w_ffn1_in", "w_ffn1_out", "w_in", "sinks",
             "gmlp_ln_g", "gmlp_ln_b", "gmlp_w_s", "gmlp_b_s", "w_br_attn", "w_br_gmlp", "w_out", "w_ffn2_in",
             "w_ffn2_out"]
    outs = [loss, grad_x[None]]
    for k in range(4):
        outs += [res[nm][k] for nm in order]
    return tuple(outs)
```

```python
import functools
import math

import jax
import jax.numpy as jnp
from jax import lax
from jax.experimental import pallas as pl
from jax.experimental.pallas import tpu as pltpu

F32 = jnp.float32
BF = jnp.bfloat16

N_DEV = 8
D = 1024
D_FF = 2816
FS = D_FF // 4
N_HEADS = 8
N_KV = 2
GROUP = 4
HD = 64
BLK = 128
Q_W = 512
KV_W = 128
G_W = 512
QKV_W = Q_W + 2 * KV_W
ZG_OFF = QKV_W
GATE_OFF = ZG_OFF + 2 * G_W
IN_W = GATE_OFF + 2 * D
N_BUCKETS = 32
MAX_DISTANCE = 128
EPS = 1e-6
NEG = -1e30
SCALE = HD ** -0.5
ADA_W = 9 * D // N_DEV

ADAM_LR = 0.001
ADAM_B1 = 0.9
ADAM_B2 = 0.999
ADAM_EPS = 1e-08
ADAM_WD = 0.01
ADAM_STEP = 10

CHUNK = 256
MIB = 1024 * 1024
MESH = pl.DeviceIdType.MESH
HIGH = lax.Precision.HIGHEST


def _cp(n_grid, vmem_mib):
    return pltpu.CompilerParams(dimension_semantics=("arbitrary",) * n_grid,
                                vmem_limit_bytes=vmem_mib * MIB)


def _const(shape):
    return pl.BlockSpec(shape, lambda *_: (0,) * len(shape))


def _resident(shape):
    return pl.BlockSpec(shape, lambda *_: (0,) * len(shape), pipeline_mode=pl.Buffered(1))


def _sds(shape, dtype):
    return jax.ShapeDtypeStruct(shape, dtype)


def _dot(a, b):
    return jnp.dot(a, b, preferred_element_type=F32)


def _dot_nt(a, b):
    return lax.dot_general(a, b, (((1,), (1,)), ((), ())), preferred_element_type=F32)


def _dot_tn(a, b):
    return lax.dot_general(a, b, (((0,), (0,)), ((), ())), preferred_element_type=F32)


def _rms_r(x):
    return lax.rsqrt(jnp.mean(x * x, axis=-1, keepdims=True) + EPS)


def _colsum(x):
    return jnp.sum(x, axis=0, keepdims=True)


def _prenorm(x, gp, sc, sh):
    return (x * _rms_r(x) * gp) * (1.0 + sc) + sh


def _prenorm_bwd(dn, x, gp, sc):
    r = _rms_r(x)
    xh = x * r
    t = dn * (1.0 + sc) * gp
    dx = r * (t - xh * jnp.mean(t * xh, axis=-1, keepdims=True))
    return dx, _colsum(dn), _colsum(dn * xh * gp), _colsum(dn * (1.0 + sc) * xh)


def _postnorm_bwd(dh, y, gate, gp, res):
    y = y.astype(F32)
    r = _rms_r(y)
    yh = y * r
    dyn = (res * gate) * dh
    t = dyn * gp
    dy = r * (t - yh * jnp.mean(t * yh, axis=-1, keepdims=True))
    return dy, _colsum(res * dh * yh * gp), _colsum(dyn * yh)


def _gelu(x):
    k = math.sqrt(2.0 / math.pi)
    return 0.5 * x * (1.0 + jnp.tanh(k * (x + 0.044715 * x * x * x)))


def _gelu_grad(x):
    k = math.sqrt(2.0 / math.pi)
    t = jnp.tanh(k * (x + 0.044715 * x * x * x))
    return 0.5 * (1.0 + t) + 0.5 * x * (1.0 - t * t) * (k * (1.0 + 3.0 * 0.044715 * x * x))


def _my_place():
    x, y, c = lax.axis_index("x"), lax.axis_index("y"), lax.axis_index("c")
    return x, y, c, 4 * x + 2 * y + c


def _peer(x, y, c, k):
    px = 1 - x if k & 4 else x
    py = 1 - y if k & 2 else y
    pc = 1 - c if k & 1 else c
    return (px, py, pc), 4 * px + 2 * py + pc


HBM_SPEC = pl.BlockSpec(memory_space=pltpu.HBM)
SEM_SPEC = pl.BlockSpec(memory_space=pltpu.SEMAPHORE)
EFFECT = pltpu.SideEffectType.DATAFLOW_SIDE_EFFECTING


RELATIONS = {"exchange": (1, 2, 3, 4, 5, 6, 7), "gather": (1, 2, 4, 6), "forward": (2, 4, 6),
             "gather_all": (1, 2, 3, 4, 5, 6, 7)}


def _slab_copies(mode, srcs, lands, send, recv, loc):
    x, y, c, me = _my_place()
    rel = RELATIONS[mode]
    remote, local = [], []
    for t in range(len(lands)):
        for i, k in enumerate(rel):
            peer, peer_lin = _peer(x, y, c, k)
            if mode == "exchange":
                src, dst, to = srcs[t].at[peer_lin], lands[t].at[me], peer
            elif mode in ("gather", "gather_all"):
                src, dst, to = srcs[t], lands[t].at[me], peer
            else:
                src, dst, to = lands[t].at[peer_lin], lands[t].at[peer_lin], _peer(x, y, c, 1)[0]
            remote.append(pltpu.make_async_remote_copy(
                src_ref=src, dst_ref=dst, send_sem=send.at[t * len(rel) + i], recv_sem=recv.at[t * len(rel) + i],
                device_id=to, device_id_type=MESH))
        if mode == "exchange":
            local.append(pltpu.make_async_copy(srcs[t].at[me], lands[t].at[me], loc.at[t]))
        elif mode in ("gather", "gather_all"):
            local.append(pltpu.make_async_copy(srcs[t], lands[t].at[me], loc.at[t]))
    return remote, local


def _slabs_start(mode, arrays, after, name):
    n = len(arrays)
    if mode == "forward":
        thru = list(arrays)
    else:
        shapes = [a.shape if mode == "exchange" else (N_DEV,) + a.shape for a in arrays]
        thru = list(arrays) + [lax.empty(s, a.dtype) for s, a in zip(shapes, arrays)]
    m = len(thru)
    n_sem = n * len(RELATIONS[mode])

    def body(*refs):
        srcs, lands = refs[:n], refs[m - n:m]
        send, recv, loc = refs[m + 1:m + 4]
        remote, local = _slab_copies(mode, srcs, lands, send, recv, loc)
        for cp in remote + local:
            cp.start()
        refs[-1][...] = jnp.zeros_like(refs[-1])

    return pl.pallas_call(
        body, name=name,
        out_shape=(pltpu.SemaphoreType.DMA((n_sem,)), pltpu.SemaphoreType.DMA((n_sem,)),
                   pltpu.SemaphoreType.DMA((n,)),
                   *[pltpu.HBM(a.shape, a.dtype) for a in thru],
                   _sds((1, D), F32)),
        in_specs=[HBM_SPEC] * m + [pl.BlockSpec(memory_space=pl.ANY)],
        out_specs=(SEM_SPEC, SEM_SPEC, SEM_SPEC, *[HBM_SPEC] * m, pl.BlockSpec(memory_space=pltpu.VMEM)),
        input_output_aliases={t: 3 + t for t in range(m)},
        compiler_params=pltpu.CompilerParams(has_side_effects=EFFECT),
    )(*[pltpu.with_memory_space_constraint(a, pltpu.HBM) for a in thru], after)


def _slabs_wait(mode, n, started, after, name):
    sems = started[0:3]
    thru = started[3:-1]
    m = len(thru)

    def body(*refs):
        srcs, lands = refs[:n], refs[m - n:m]
        remote, local = _slab_copies(mode, srcs, lands, *refs[m:m + 3])
        for cp in remote:
            cp.wait_send()
            cp.wait_recv()
        for cp in local:
            cp.wait()

    res = pl.pallas_call(
        body, name=name,
        out_shape=tuple(pltpu.HBM(a.shape, a.dtype) for a in thru),
        in_specs=[HBM_SPEC] * m + [SEM_SPEC] * 3 + [pl.BlockSpec(memory_space=pl.ANY)],
        out_specs=tuple([HBM_SPEC] * m),
        input_output_aliases={t: t for t in range(m)},
        compiler_params=pltpu.CompilerParams(has_side_effects=EFFECT),
    )(*thru, *sems, after)
    return list(res[m - n:m])


def _ada_forward(small8, w_ada, b_ada64):
    sw = small8.shape[1]

    def body(sm_ref, w_ref, b_ref, gath_ref, ada_ref, part_ref, send1, recv1, send2, recv2):
        x, y, c, me = _my_place()
        row_me = pl.multiple_of(me * 8, 8)
        gath_ref[pl.ds(row_me, 8), :] = sm_ref[...]
        first = []
        for k in range(1, N_DEV):
            peer, _ = _peer(x, y, c, k)
            cp = pltpu.make_async_remote_copy(
                src_ref=sm_ref, dst_ref=gath_ref.at[pl.ds(row_me, 8), :], send_sem=send1.at[k - 1],
                recv_sem=recv1.at[k - 1], device_id=peer, device_id_type=MESH)
            cp.start()
            first.append(cp)
        for cp in first:
            cp.wait()
        cs = gath_ref[:, 0:D]
        cs = cs * jax.nn.sigmoid(cs)
        part_ref[...] = jnp.dot(cs, w_ref[...], preferred_element_type=F32, precision=HIGH)
        ada_ref[pl.ds(row_me, 8), :] = part_ref[pl.ds(row_me, 8), :]
        second = []
        for k in range(1, N_DEV):
            peer, peer_lin = _peer(x, y, c, k)
            cp = pltpu.make_async_remote_copy(
                src_ref=part_ref.at[pl.ds(pl.multiple_of(peer_lin * 8, 8), 8), :],
                dst_ref=ada_ref.at[pl.ds(row_me, 8), :], send_sem=send2.at[k - 1],
                recv_sem=recv2.at[k - 1], device_id=peer, device_id_type=MESH)
            cp.start()
            second.append(cp)
        for cp in second:
            cp.wait()
        ada_ref[...] = ada_ref[...] + b_ref[...]

    vm = pl.BlockSpec(memory_space=pltpu.VMEM)
    return pl.pallas_call(
        body, name="ada_forward",
        out_shape=[_sds((8 * N_DEV, sw), F32), _sds((8 * N_DEV, ADA_W), F32)],
        in_specs=[vm, vm, vm], out_specs=[vm, vm],
        scratch_shapes=[pltpu.VMEM((8 * N_DEV, ADA_W), F32)] + [pltpu.SemaphoreType.DMA((7,))] * 4,
        compiler_params=pltpu.CompilerParams(vmem_limit_bytes=32 * MIB),
    )(small8, w_ada, b_ada64)


def _sum_slabs(land):
    def body(l_ref, o_ref):
        acc = l_ref[0]
        for j in range(1, N_DEV):
            acc = acc + l_ref[j]
        o_ref[...] = acc

    vm = pl.BlockSpec(memory_space=pltpu.VMEM)
    return pl.pallas_call(body, name="sum_slabs", out_shape=_sds(land.shape[1:], F32), in_specs=[vm], out_specs=vm,
                          compiler_params=pltpu.CompilerParams(vmem_limit_bytes=32 * MIB))(land)


def _small_allreduce(pack):
    rows = pack.shape[0]

    def body(p_ref, sum_ref, gath_ref, send, recv):
        x, y, c, me = _my_place()
        gath_ref[me] = p_ref[...]
        cps = []
        for k in range(1, N_DEV):
            peer, _ = _peer(x, y, c, k)
            cp = pltpu.make_async_remote_copy(
                src_ref=p_ref, dst_ref=gath_ref.at[me], send_sem=send.at[k - 1],
                recv_sem=recv.at[k - 1], device_id=peer, device_id_type=MESH)
            cp.start()
            cps.append(cp)
        for cp in cps:
            cp.wait()
        acc = gath_ref[0]
        for j in range(1, N_DEV):
            acc = acc + gath_ref[j]
        sum_ref[...] = acc

    vm = pl.BlockSpec(memory_space=pltpu.VMEM)
    return pl.pallas_call(
        body, name="small_allreduce",
        out_shape=[_sds((rows, 128), F32), _sds((N_DEV, rows, 128), F32)],
        in_specs=[vm], out_specs=[vm, vm],
        scratch_shapes=[pltpu.SemaphoreType.DMA((7,)), pltpu.SemaphoreType.DMA((7,))],
        compiler_params=pltpu.CompilerParams(vmem_limit_bytes=40 * MIB),
    )(pack)


F_TILES = tuple((f0, min(512, D_FF - f0)) for f0 in range(0, D_FF, 512))
F_TILES_NARROW = tuple((f0, 256) for f0 in range(0, D_FF, 256))


def _swiglu_tile(n, wt_ref, f0, tf):
    g = _dot_nt(n, wt_ref[f0:f0 + tf, :])
    u = _dot_nt(n, wt_ref[D_FF + f0:D_FF + f0 + tf, :])
    sg = jax.nn.sigmoid(g)
    silu = g * sg
    return (u * (sg * (1.0 + g * (1.0 - sg)))).astype(BF), silu.astype(BF), (silu * u).astype(BF)


def _ffn_in(h, sh, sc, gp, wt, name):
    S = h.shape[0]
    R = min(512, S)

    def body(h_ref, sh_ref, sc_ref, gp_ref, w_ref, n_ref, dg_ref, sl_ref, a_ref):
        for r0 in range(0, R, CHUNK):
            rows = slice(r0, r0 + CHUNK)
            n = _prenorm(h_ref[rows, :], gp_ref[...], sc_ref[...], sh_ref[...]).astype(BF)
            n_ref[rows, :] = n
            for f0, tf in F_TILES_NARROW:
                dg_ref[rows, f0:f0 + tf], sl_ref[rows, f0:f0 + tf], a_ref[rows, f0:f0 + tf] = _swiglu_tile(
                    n, w_ref, f0, tf)

    vec = _const((1, D))
    rows_ = lambda w_: pl.BlockSpec((R, w_), lambda i: (i, 0))
    return pl.pallas_call(
        body, name=name, grid=(S // R,),
        out_shape=[_sds((S, D), BF)] + [_sds((S, D_FF), BF)] * 3,
        in_specs=[rows_(D), vec, vec, vec, _resident((2 * D_FF, D))],
        out_specs=[rows_(D), rows_(D_FF), rows_(D_FF), rows_(D_FF)],
        compiler_params=_cp(1, 56),
    )(h, sh, sc, gp, wt)


def _ffn_out(a, w, h, gate, gp, name, target=None):
    S = h.shape[0]
    R = min(512, S)
    with_loss = target is not None

    def body(a_ref, w_ref, h_ref, gate_ref, gp_ref, *rest):
        if with_loss:
            t_ref, out_ref, y_ref, tot_ref = rest

            @pl.when(pl.program_id(0) == 0)
            def _():
                tot_ref[...] = jnp.zeros_like(tot_ref)
        else:
            out_ref, y_ref = rest
        for r0 in range(0, R, CHUNK):
            rows = slice(r0, r0 + CHUNK)
            y = _dot(a_ref[rows, :], w_ref[...])
            y_ref[rows, :] = y.astype(BF)
            hn = h_ref[rows, :] + (0.5 * gate_ref[...]) * (y * _rms_r(y) * gp_ref[...])
            if with_loss:
                e = hn - t_ref[rows, :]
                out_ref[rows, :] = e * (1.0 / D)
                tot_ref[...] += jnp.sum(jnp.sum(e * e, axis=1, keepdims=True), axis=0, keepdims=True)
            else:
                out_ref[rows, :] = hn

    vec = _const((1, D))
    rows_ = lambda w_: pl.BlockSpec((R, w_), lambda i: (i, 0))
    return pl.pallas_call(
        body, name=name, grid=(S // R,),
        out_shape=[_sds((S, D), F32), _sds((S, D), BF)] + ([_sds((1, 1), F32)] if with_loss else []),
        in_specs=[rows_(D_FF), _resident((D_FF, D)), rows_(D), vec, vec] + ([rows_(D)] if with_loss else []),
        out_specs=[rows_(D), rows_(D)] + ([_const((1, 1))] if with_loss else []),
        compiler_params=_cp(1, 48),
    )(*((a, w, h, gate, gp) + ((target,) if with_loss else ())))


def _ffn_out_bwd(dh, y, dsilu_u, silu, w, gate, gp, name):
    S = dh.shape[0]
    R = min(512, S)

    def body(dh_ref, y_ref, g_ref, u_ref, w_ref, gate_ref, gp_ref, dy_ref, dgu_ref, dgate_ref, dgp_ref):
        @pl.when(pl.program_id(0) == 0)
        def _():
            dgate_ref[...] = jnp.zeros_like(dgate_ref)
            dgp_ref[...] = jnp.zeros_like(dgp_ref)
        for r0 in range(0, R, CHUNK):
            rows = slice(r0, r0 + CHUNK)
            dy, dgate, dgp = _postnorm_bwd(dh_ref[rows, :], y_ref[rows, :], gate_ref[...], gp_ref[...], 0.5)
            dgate_ref[...] += dgate
            dgp_ref[...] += dgp
            dyb = dy.astype(BF)
            dy_ref[rows, :] = dyb
            for f0, tf in F_TILES:
                da = _dot_nt(dyb, w_ref[f0:f0 + tf, :])
                dgu_ref[rows, f0:f0 + tf] = (da * g_ref[rows, f0:f0 + tf].astype(F32)).astype(BF)
                dgu_ref[rows, D_FF + f0:D_FF + f0 + tf] = (da * u_ref[rows, f0:f0 + tf].astype(F32)).astype(BF)

    vec = _const((1, D))
    rows_ = lambda w_: pl.BlockSpec((R, w_), lambda i: (i, 0))
    return pl.pallas_call(
        body, name=name, grid=(S // R,),
        out_shape=[_sds((S, D), BF), _sds((S, 2 * D_FF), BF), _sds((1, D), F32), _sds((1, D), F32)],
        in_specs=[rows_(D), rows_(D), rows_(D_FF), rows_(D_FF), _resident((D_FF, D)), vec, vec],
        out_specs=[rows_(D), rows_(2 * D_FF), vec, vec],
        compiler_params=_cp(1, 56),
    )(dh, y, dsilu_u, silu, w, gate, gp)


def _ffn_dn(dgu, wt, h, dh, sc, gp, name):
    S = h.shape[0]
    R = min(512, S)

    def body(dgu_ref, w_ref, h_ref, dh_ref, sc_ref, gp_ref, out_ref, dsh_ref, dsc_ref, dgp_ref):
        @pl.when(pl.program_id(0) == 0)
        def _():
            dsh_ref[...] = jnp.zeros_like(dsh_ref)
            dsc_ref[...] = jnp.zeros_like(dsc_ref)
            dgp_ref[...] = jnp.zeros_like(dgp_ref)

        for r0 in range(0, R, CHUNK):
            rows = slice(r0, r0 + CHUNK)
            dn = _dot(dgu_ref[rows, :], w_ref[...])
            dx, dsh, dsc, dgp = _prenorm_bwd(dn, h_ref[rows, :], gp_ref[...], sc_ref[...])
            out_ref[rows, :] = dh_ref[rows, :] + dx
            dsh_ref[...] += dsh
            dsc_ref[...] += dsc
            dgp_ref[...] += dgp

    vec = _const((1, D))
    rows_ = lambda w_: pl.BlockSpec((R, w_), lambda i: (i, 0))
    return pl.pallas_call(
        body, name=name, grid=(S // R,),
        out_shape=[_sds((S, D), F32)] + [_sds((1, D), F32)] * 3,
        in_specs=[rows_(2 * D_FF), _resident((2 * D_FF, D)), rows_(D), rows_(D), vec, vec],
        out_specs=[rows_(D), vec, vec, vec],
        compiler_params=_cp(1, 56),
    )(dgu, wt, h, dh, sc, gp)


def _ffn_bwd(dh, y, dsilu_u, silu, w, wt, h, gate, gpost, sc, gpre, name):
    S = dh.shape[0]
    R = min(256, S)

    def body(dh_ref, y_ref, g_ref, u_ref, w_ref, wt_ref, h_ref, gate_ref, gpost_ref, sc_ref, gpre_ref,
             dy_ref, dgu_ref, out_ref, dgate_ref, dgpost_ref, dsh_ref, dsc_ref, dgpre_ref):
        @pl.when(pl.program_id(0) == 0)
        def _():
            for r in (dgate_ref, dgpost_ref, dsh_ref, dsc_ref, dgpre_ref):
                r[...] = jnp.zeros_like(r)
        dhh = dh_ref[...]
        dy, dgate, dgpost = _postnorm_bwd(dhh, y_ref[...], gate_ref[...], gpost_ref[...], 0.5)
        dgate_ref[...] += dgate
        dgpost_ref[...] += dgpost
        dyb = dy.astype(BF)
        dy_ref[...] = dyb
        dn = None
        for f0, tf in F_TILES:
            da = _dot_nt(dyb, w_ref[f0:f0 + tf, :])
            dg = (da * g_ref[:, f0:f0 + tf].astype(F32)).astype(BF)
            du = (da * u_ref[:, f0:f0 + tf].astype(F32)).astype(BF)
            dgu_ref[:, f0:f0 + tf] = dg
            dgu_ref[:, D_FF + f0:D_FF + f0 + tf] = du
            part = _dot(dg, wt_ref[f0:f0 + tf, :]) + _dot(du, wt_ref[D_FF + f0:D_FF + f0 + tf, :])
            dn = part if dn is None else dn + part
        dx, dsh, dsc, dgpre = _prenorm_bwd(dn, h_ref[...], gpre_ref[...], sc_ref[...])
        out_ref[...] = dhh + dx
        dsh_ref[...] += dsh
        dsc_ref[...] += dsc
        dgpre_ref[...] += dgpre

    vec = _const((1, D))
    rows_ = lambda w_: pl.BlockSpec((R, w_), lambda i: (i, 0))
    return pl.pallas_call(
        body, name=name, grid=(S // R,),
        out_shape=[_sds((S, D), BF), _sds((S, 2 * D_FF), BF), _sds((S, D), F32)] + [_sds((1, D), F32)] * 5,
        in_specs=[rows_(D), rows_(D), rows_(D_FF), rows_(D_FF), _resident((D_FF, D)), _resident((2 * D_FF, D)),
                  rows_(D), vec, vec, vec, vec],
        out_specs=[rows_(D), rows_(2 * D_FF), rows_(D)] + [vec] * 5,
        compiler_params=_cp(1, 56),
    )(dh, y, dsilu_u, silu, w, wt, h, gate, gpost, sc, gpre)


def _tn_matmul(a, b, name, tm=None):
    S, M_all = a.shape
    N = b.shape[1]
    M = M_all if tm is None else tm
    GA = M_all // M
    ts = min(2048 if M * N <= 2 * D * D else 1024, S)
    nk = S // ts
    chunks = [(m0, min(CHUNK, M - m0)) for m0 in range(0, M, CHUNK)]

    def body(a_ref, b_ref, o_ref, acc):
        k = pl.program_id(1)

        @pl.when(k == 0)
        def _():
            acc[...] = jnp.zeros_like(acc)

        for m0, mc in chunks:
            acc[m0:m0 + mc, :] += _dot_tn(a_ref[:, m0:m0 + mc], b_ref[...])

        @pl.when(k == nk - 1)
        def _():
            for m0, mc in chunks:
                o_ref[m0:m0 + mc, :] = acc[m0:m0 + mc, :].astype(BF)

    return pl.pallas_call(
        body, name=name, grid=(GA, nk),
        out_shape=_sds((M_all, N), BF),
        in_specs=[pl.BlockSpec((ts, M), lambda ga, k: (k, ga)), pl.BlockSpec((ts, N), lambda ga, k: (k, 0))],
        out_specs=pl.BlockSpec((M, N), lambda ga, k: (ga, 0)),
        scratch_shapes=[pltpu.VMEM((M, N), F32)],
        compiler_params=_cp(2, 56),
    )(a, b)


def _mix_in(h, sh, sc, gp, w):
    S = h.shape[0]
    R = min(512, S)

    def body(h_ref, sh_ref, sc_ref, gp_ref, w_ref, n_ref, qkv_ref, zg_ref, gates_ref):
        for r0 in range(0, R, CHUNK):
            rows = slice(r0, r0 + CHUNK)
            nb = _prenorm(h_ref[rows, :], gp_ref[...], sc_ref[...], sh_ref[...]).astype(BF)
            n_ref[rows, :] = nb
            qkv_ref[rows, :] = _dot_nt(nb, w_ref[0:ZG_OFF, :]).astype(BF)
            zg_ref[rows, :] = _dot_nt(nb, w_ref[ZG_OFF:GATE_OFF, :]).astype(BF)
            gates_ref[rows, :] = jax.nn.sigmoid(_dot_nt(nb, w_ref[GATE_OFF:IN_W, :])).astype(BF)

    vec = _const((1, D))
    rows = lambda w_: pl.BlockSpec((R, w_), lambda i: (i, 0))
    return pl.pallas_call(
        body, name="mix_in", grid=(S // R,),
        out_shape=[_sds((S, D), BF), _sds((S, QKV_W), BF), _sds((S, 2 * G_W), BF), _sds((S, 2 * D), BF)],
        in_specs=[rows(D), vec, vec, vec, _resident((IN_W, D))],
        out_specs=[rows(D), rows(QKV_W), rows(2 * G_W), rows(2 * D)],
        compiler_params=_cp(1, 48),
    )(h, sh, sc, gp, w)


def _bias_table(rel_bias, bucket):
    def body(rel_ref, bk_ref, out_ref):
        bk = bk_ref[...]
        qi = lax.broadcasted_iota(jnp.int32, (BLK, 2 * BLK), 0)
        kj = lax.broadcasted_iota(jnp.int32, (BLK, 2 * BLK), 1)
        dist = qi + BLK - kj
        window = (dist >= 0) & (dist < BLK)
        for h in range(N_HEADS):
            acc = jnp.zeros((BLK, 2 * BLK), F32)
            for b in range(N_BUCKETS):
                acc = jnp.where(bk == b, rel_ref[b, h], acc)
            out_ref[h // GROUP, pl.ds((h % GROUP) * BLK, BLK), :] = jnp.where(window, acc, NEG)

    return pl.pallas_call(
        body, name="bias_table",
        out_shape=_sds((N_KV, GROUP * BLK, 2 * BLK), F32),
        in_specs=[pl.BlockSpec(memory_space=pltpu.SMEM), pl.BlockSpec(memory_space=pltpu.VMEM)],
        out_specs=pl.BlockSpec(memory_space=pltpu.VMEM),
    )(rel_bias, bucket)


ATT_TB = 4


def _attn_scores(q, kvc, kvp, bias_ref, sink_ref, has_prev, kh, g0=0, ng=GROUP):
    k2 = jnp.concatenate([kvp[:, kh * HD:(kh + 1) * HD], kvc[:, kh * HD:(kh + 1) * HD]], axis=0)
    v2 = jnp.concatenate([kvp[:, KV_W + kh * HD:KV_W + (kh + 1) * HD],
                          kvc[:, KV_W + kh * HD:KV_W + (kh + 1) * HD]], axis=0)
    q4 = jnp.concatenate([q[:, (kh * GROUP + g) * HD:(kh * GROUP + g + 1) * HD] for g in range(g0, g0 + ng)], axis=0)
    s = _dot_nt(q4, k2) * SCALE + bias_ref[kh, g0 * BLK:(g0 + ng) * BLK, :]
    if has_prev is not None:
        col = lax.broadcasted_iota(jnp.int32, (ng * BLK, 2 * BLK), 1)
        s = jnp.where((col >= BLK) | has_prev, s, NEG)
    rowg = lax.broadcasted_iota(jnp.int32, (ng * BLK, 1), 0) // BLK
    sink = jnp.zeros((ng * BLK, 1), F32)
    for g in range(ng):
        sink = jnp.where(rowg == g, sink_ref[kh * GROUP + g0 + g], sink)
    return q4, k2, v2, s, sink


def _attn_fwd(qkv, bias, sinks):
    S = qkv.shape[0]
    tb = min(ATT_TB, S // BLK)
    T = tb * BLK

    def body(sink_ref, q_ref, kv_ref, kvp_ref, bias_ref, o_ref):
        step = pl.program_id(0)
        for j in range(tb):
            rows = slice(j * BLK, (j + 1) * BLK)
            q, kvc = q_ref[rows, :], kv_ref[rows, :]
            kvp = kvp_ref[...] if j == 0 else kv_ref[(j - 1) * BLK:j * BLK, :]
            has_prev = (step > 0) if j == 0 else None
            outs = []
            for kh in range(N_KV):
                q4, k2, v2, s, sink = _attn_scores(q, kvc, kvp, bias_ref, sink_ref, has_prev, kh)
                m = jnp.maximum(jnp.max(s, axis=1, keepdims=True), sink)
                p = jnp.exp(s - m)
                denom = jnp.sum(p, axis=1, keepdims=True) + jnp.exp(sink - m)
                o4 = _dot((p / denom).astype(BF), v2)
                outs += [o4[g * BLK:(g + 1) * BLK] for g in range(GROUP)]
            o_ref[rows, :] = jnp.concatenate(outs, axis=1).astype(BF)

    return pl.pallas_call(
        body, name="attn_fwd", grid=(S // T,),
        out_shape=_sds((S, Q_W), BF),
        in_specs=[pl.BlockSpec(memory_space=pltpu.SMEM),
                  pl.BlockSpec((T, Q_W), lambda i: (i, 0)),
                  pl.BlockSpec((T, 2 * KV_W), lambda i: (i, 2)),
                  pl.BlockSpec((BLK, 2 * KV_W), lambda i: (jnp.maximum(i * tb - 1, 0), 2)),
                  _const((N_KV, GROUP * BLK, 2 * BLK))],
        out_specs=pl.BlockSpec((T, Q_W), lambda i: (i, 0)),
        compiler_params=_cp(1, 32),
    )(sinks, qkv, qkv, qkv, bias)


def _attn_bwd(qkv, bias, sinks, do):
    S = qkv.shape[0]
    tb = 1
    ng = GROUP
    T = tb * BLK
    nt = S // T

    def body(sink_ref, q_ref, kv_ref, kvp_ref, bias_ref, do_ref, dq_ref, dkv_ref, dbias_ref, dsink_ref, carry):
        i = pl.program_id(0)

        @pl.when(i == 0)
        def _():
            carry[...] = jnp.zeros_like(carry)
            dbias_ref[...] = jnp.zeros_like(dbias_ref)
            dsink_ref[...] = jnp.zeros_like(dsink_ref)

        from_next = carry[...]
        for j in reversed(range(tb)):
            rows = slice(j * BLK, (j + 1) * BLK)
            q, kvc, do_ = q_ref[rows, :], kv_ref[rows, :], do_ref[rows, :]
            kvp = kvp_ref[...] if j == 0 else kv_ref[(j - 1) * BLK:j * BLK, :]
            has_prev = (i < nt - 1) if j == 0 else None
            dqs, dk_cur, dv_cur, dk_prev, dv_prev = [], [], [], [], []
            head_row = lax.broadcasted_iota(jnp.int32, (N_HEADS, 128), 0)
            dsink_rows = jnp.zeros((N_HEADS, 128), F32)
            for kh in range(N_KV):
                dk2, dv2 = None, None
                for g0 in range(0, GROUP, ng):
                    q4, k2, v2, s, sink = _attn_scores(q, kvc, kvp, bias_ref, sink_ref, has_prev, kh, g0, ng)
                    m = jnp.maximum(jnp.max(s, axis=1, keepdims=True), sink)
                    p = jnp.exp(s - m)
                    denom = jnp.sum(p, axis=1, keepdims=True) + jnp.exp(sink - m)
                    prob = p / denom
                    p_sink = jnp.exp(sink - m) / denom
                    pb = prob.astype(BF)
                    do4 = jnp.concatenate([do_[:, (kh * GROUP + g) * HD:(kh * GROUP + g + 1) * HD]
                                           for g in range(g0, g0 + ng)], axis=0)
                    dp = _dot_nt(do4, v2)
                    o4 = _dot(pb, v2)
                    delta = jnp.sum(do4.astype(F32) * o4, axis=1, keepdims=True)
                    ds = prob * (dp - delta)
                    dbias_ref[kh, g0 * BLK:(g0 + ng) * BLK, :] += ds
                    sink_term = p_sink * delta
                    for g in range(ng):
                        val = -jnp.sum(sink_term[g * BLK:(g + 1) * BLK], axis=0, keepdims=True)
                        dsink_rows = jnp.where(head_row == kh * GROUP + g0 + g, val, dsink_rows)
                    dsb = ds.astype(BF)
                    dq4 = _dot(dsb, k2) * SCALE
                    dk_part = jnp.transpose(_dot_tn(q4, dsb)) * SCALE
                    dv_part = jnp.transpose(_dot_tn(do4, pb))
                    dk2 = dk_part if dk2 is None else dk2 + dk_part
                    dv2 = dv_part if dv2 is None else dv2 + dv_part
                    dqs += [dq4[g * BLK:(g + 1) * BLK] for g in range(ng)]
                dk_prev.append(dk2[0:BLK])
                dk_cur.append(dk2[BLK:2 * BLK])
                dv_prev.append(dv2[0:BLK])
                dv_cur.append(dv2[BLK:2 * BLK])
            dsink_ref[...] += dsink_rows
            dq_ref[rows, :] = jnp.concatenate(dqs, axis=1).astype(BF)
            dkv_ref[rows, :] = (jnp.concatenate(dk_cur + dv_cur, axis=1) + from_next).astype(BF)
            from_next = jnp.concatenate(dk_prev + dv_prev, axis=1)
        carry[...] = from_next

    return pl.pallas_call(
        body, name="attn_bwd", grid=(nt,),
        out_shape=[_sds((S, Q_W), BF), _sds((S, 2 * KV_W), BF),
                   _sds((N_KV, GROUP * BLK, 2 * BLK), F32), _sds((N_HEADS, 128), F32)],
        in_specs=[pl.BlockSpec(memory_space=pltpu.SMEM),
                  pl.BlockSpec((T, Q_W), lambda i: (nt - 1 - i, 0)),
                  pl.BlockSpec((T, 2 * KV_W), lambda i: (nt - 1 - i, 2)),
                  pl.BlockSpec((BLK, 2 * KV_W), lambda i: (jnp.maximum((nt - 1 - i) * tb - 1, 0), 2)),
                  _const((N_KV, GROUP * BLK, 2 * BLK)),
                  pl.BlockSpec((T, Q_W), lambda i: (nt - 1 - i, 0))],
        out_specs=[pl.BlockSpec((T, Q_W), lambda i: (nt - 1 - i, 0)),
                   pl.BlockSpec((T, 2 * KV_W), lambda i: (nt - 1 - i, 0)),
                   _const((N_KV, GROUP * BLK, 2 * BLK)), _const((N_HEADS, 128))],
        scratch_shapes=[pltpu.VMEM((BLK, 2 * KV_W), F32)],
        compiler_params=_cp(1, 32),
    )(sinks, qkv, qkv, qkv, bias, do)


def _rel_bias_grad(dbias, bucket):
    def body(db_ref, bk_ref, out_ref):
        bk = bk_ref[...]
        lane = lax.broadcasted_iota(jnp.int32, (1, 128), 1)
        for h in range(N_HEADS):
            d = db_ref[h // GROUP, pl.ds((h % GROUP) * BLK, BLK), :]
            row = jnp.zeros((1, 128), F32)
            for b in range(N_BUCKETS):
                tot = jnp.sum(jnp.sum(jnp.where(bk == b, d, 0.0), axis=1, keepdims=True), axis=0, keepdims=True)
                row = jnp.where(lane == b, tot, row)
            out_ref[pl.ds(h, 1), :] = row

    vm = pl.BlockSpec(memory_space=pltpu.VMEM)
    return pl.pallas_call(body, name="rel_bias_grad", out_shape=_sds((N_HEADS, 128), F32),
                          in_specs=[vm, vm], out_specs=vm)(dbias, bucket)


def _gmlp_parts(zg, lg_ref, lb_ref):
    z = zg.astype(F32)
    ge = _gelu(z)
    u, vg = ge[:, 0:G_W], ge[:, G_W:2 * G_W]
    mu = jnp.mean(vg, axis=-1, keepdims=True)
    xc = vg - mu
    rstd = lax.rsqrt(jnp.mean(xc * xc, axis=-1, keepdims=True) + EPS)
    xh = xc * rstd
    return z, u, xh, rstd, xh * lg_ref[...] + lb_ref[...]


def _causal_weights(ws_ref, wc):
    t = lax.broadcasted_iota(jnp.int32, (BLK, BLK), 0)
    s = lax.broadcasted_iota(jnp.int32, (BLK, BLK), 1)
    for g in range(N_HEADS):
        wc[g] = jnp.where(s <= t, ws_ref[g], 0.0).astype(BF)


def _spatial(vb, wc, bst_ref, p, low):
    xp = vb[:, p * 128:(p + 1) * 128]
    s0 = _dot(wc[2 * p], xp) + bst_ref[:, 2 * p:2 * p + 1]
    s1 = _dot(wc[2 * p + 1], xp) + bst_ref[:, 2 * p + 1:2 * p + 2]
    return xp, jnp.where(low, s0, s1)


def _gmlp_fwd(zg, lg, lb, ws, bst):
    S = zg.shape[0]
    tb = min(ATT_TB, S // BLK)
    T = tb * BLK

    def body(zg_ref, lg_ref, lb_ref, ws_ref, bst_ref, o_ref, wc):
        @pl.when(pl.program_id(0) == 0)
        def _():
            _causal_weights(ws_ref, wc)
        low = lax.broadcasted_iota(jnp.int32, (BLK, 128), 1) < HD
        for j in range(tb):
            rows = slice(j * BLK, (j + 1) * BLK)
            _, u, _, _, vln = _gmlp_parts(zg_ref[rows, :], lg_ref, lb_ref)
            vb = vln.astype(BF)
            for p in range(4):
                _, sp = _spatial(vb, wc, bst_ref, p, low)
                o_ref[rows, p * 128:(p + 1) * 128] = (u[:, p * 128:(p + 1) * 128] * sp).astype(BF)

    return pl.pallas_call(
        body, name="gmlp_fwd", grid=(S // T,),
        out_shape=_sds((S, G_W), BF),
        in_specs=[pl.BlockSpec((T, 2 * G_W), lambda i: (i, 0)), _const((1, G_W)), _const((1, G_W)),
                  _const((N_HEADS, BLK, BLK)), _const((BLK, N_HEADS))],
        out_specs=pl.BlockSpec((T, G_W), lambda i: (i, 0)),
        scratch_shapes=[pltpu.VMEM((N_HEADS, BLK, BLK), BF)],
        compiler_params=_cp(1, 32),
    )(zg, lg, lb, ws, bst)


def _gmlp_bwd(zg, d_out, lg, lb, ws, bst):
    S = zg.shape[0]
    tb = min(ATT_TB, S // BLK)
    T = tb * BLK
    nb = S // T

    def body(zg_ref, d_ref, lg_ref, lb_ref, ws_ref, bst_ref, dzg_ref, dws_ref, dbs_ref, dlg_ref, dlb_ref, wc, dbacc):
        i = pl.program_id(0)

        @pl.when(i == 0)
        def _():
            _causal_weights(ws_ref, wc)
            dws_ref[...] = jnp.zeros_like(dws_ref)
            dlg_ref[...] = jnp.zeros_like(dlg_ref)
            dlb_ref[...] = jnp.zeros_like(dlb_ref)
            dbacc[...] = jnp.zeros_like(dbacc)

        low = lax.broadcasted_iota(jnp.int32, (BLK, 128), 1) < HD
        for j in range(tb):
            rows = slice(j * BLK, (j + 1) * BLK)
            z, u, xh, rstd, vln = _gmlp_parts(zg_ref[rows, :], lg_ref, lb_ref)
            vb = vln.astype(BF)
            d = d_ref[rows, :].astype(F32)
            du_parts, dvln_parts = [], []
            for p in range(4):
                xp, sp = _spatial(vb, wc, bst_ref, p, low)
                dp = d[:, p * 128:(p + 1) * 128]
                du_parts.append(dp * sp)
                dsp = dp * u[:, p * 128:(p + 1) * 128]
                dbacc[:, p * 128:(p + 1) * 128] += dsp
                d0 = jnp.where(low, dsp, 0.0).astype(BF)
                d1 = jnp.where(low, 0.0, dsp).astype(BF)
                dws_ref[2 * p] += _dot_nt(d0, xp)
                dws_ref[2 * p + 1] += _dot_nt(d1, xp)
                dvln_parts.append(_dot_tn(wc[2 * p], d0) + _dot_tn(wc[2 * p + 1], d1))
            dvln = jnp.concatenate(dvln_parts, axis=1)
            dlg_ref[...] += _colsum(dvln * xh)
            dlb_ref[...] += _colsum(dvln)
            dxh = dvln * lg_ref[...]
            dvg = rstd * (dxh - jnp.mean(dxh, axis=-1, keepdims=True)
                          - xh * jnp.mean(dxh * xh, axis=-1, keepdims=True))
            dge = jnp.concatenate(du_parts + [dvg], axis=1)
            dzg_ref[rows, :] = (dge * _gelu_grad(z)).astype(BF)

        @pl.when(i == nb - 1)
        def _():
            t = lax.broadcasted_iota(jnp.int32, (BLK, BLK), 0)
            s = lax.broadcasted_iota(jnp.int32, (BLK, BLK), 1)
            for g in range(N_HEADS):
                dws_ref[g] = jnp.where(s <= t, dws_ref[g], 0.0)
            grp = lax.broadcasted_iota(jnp.int32, (N_HEADS, G_W), 0)
            lane = lax.broadcasted_iota(jnp.int32, (N_HEADS, G_W), 1) // HD
            pick = jnp.where(grp == lane, 1.0, 0.0).astype(F32)
            dbs_ref[...] = lax.dot_general(pick, dbacc[...], (((1,), (1,)), ((), ())),
                                           preferred_element_type=F32, precision=HIGH)

    return pl.pallas_call(
        body, name="gmlp_bwd", grid=(nb,),
        out_shape=[_sds((S, 2 * G_W), BF), _sds((N_HEADS, BLK, BLK), F32), _sds((N_HEADS, BLK), F32),
                   _sds((1, G_W), F32), _sds((1, G_W), F32)],
        in_specs=[pl.BlockSpec((T, 2 * G_W), lambda i: (i, 0)), pl.BlockSpec((T, G_W), lambda i: (i, 0)),
                  _const((1, G_W)), _const((1, G_W)), _const((N_HEADS, BLK, BLK)), _const((BLK, N_HEADS))],
        out_specs=[pl.BlockSpec((T, 2 * G_W), lambda i: (i, 0)), _const((N_HEADS, BLK, BLK)),
                   _const((N_HEADS, BLK)), _const((1, G_W)), _const((1, G_W))],
        scratch_shapes=[pltpu.VMEM((N_HEADS, BLK, BLK), BF), pltpu.VMEM((BLK, G_W), F32)],
        compiler_params=_cp(1, 32),
    )(zg, d_out, lg, lb, ws, bst)


def _mix_out(o, gm, gates, h, wa, wg, wo, gate, gp):
    S = h.shape[0]
    R = min(512, S)

    def body(o_ref, gm_ref, gates_ref, h_ref, wa_ref, wg_ref, wo_ref, gate_ref, gp_ref,
             ya_ref, yg_ref, ym_ref, y_ref, hn_ref):
        for r0 in range(0, R, CHUNK):
            rows = slice(r0, r0 + CHUNK)
            ya = _dot(o_ref[rows, :], wa_ref[...])
            yg = _dot(gm_ref[rows, :], wg_ref[...])
            ya_ref[rows, :] = ya.astype(BF)
            yg_ref[rows, :] = yg.astype(BF)
            ym = (gates_ref[rows, 0:D].astype(F32) * ya + gates_ref[rows, D:2 * D].astype(F32) * yg).astype(BF)
            ym_ref[rows, :] = ym
            y = _dot(ym, wo_ref[...])
            y_ref[rows, :] = y.astype(BF)
            hn_ref[rows, :] = h_ref[rows, :] + gate_ref[...] * (y * _rms_r(y) * gp_ref[...])

    vec = _const((1, D))
    rows = lambda w_: pl.BlockSpec((R, w_), lambda i: (i, 0))
    return pl.pallas_call(
        body, name="mix_out", grid=(S // R,),
        out_shape=[_sds((S, D), BF)] * 4 + [_sds((S, D), F32)],
        in_specs=[rows(Q_W), rows(G_W), rows(2 * D), rows(D), _resident((Q_W, D)), _resident((G_W, D)),
                  _resident((D, D)), vec, vec],
        out_specs=[rows(D)] * 5,
        compiler_params=_cp(1, 48),
    )(o, gm, gates, h, wa, wg, wo, gate, gp)


def _mix_out_bwd(dh, y, ya, yg, gates, att, gm, ymix, wa, wg, wo, gate, gp):
    S = dh.shape[0]
    R = min(512, S)
    nb = S // R

    def body(dh_ref, y_ref, ya_ref, yg_ref, gates_ref, att_ref, gm_ref, ym_ref, wa_ref, wg_ref, wo_ref,
             gate_ref, gp_ref, dz_ref, do_ref, dgm_ref, dgate_ref, dgp_ref, gwo_ref, gwa_ref, gwg_ref,
             acc_o, acc_a, acc_g, dy_scr, dya_scr, dyg_scr):
        i = pl.program_id(0)

        @pl.when(i == 0)
        def _():
            for r in (dgate_ref, dgp_ref, acc_o, acc_a, acc_g):
                r[...] = jnp.zeros_like(r)
        for r0 in range(0, R, CHUNK):
            rows = slice(r0, r0 + CHUNK)
            dy, dgate, dgp = _postnorm_bwd(dh_ref[rows, :], y_ref[rows, :], gate_ref[...], gp_ref[...], 1.0)
            dgate_ref[...] += dgate
            dgp_ref[...] += dgp
            dyb = dy.astype(BF)
            dy_scr[rows, :] = dyb
            dym = _dot_nt(dyb, wo_ref[...])
            ga = gates_ref[rows, 0:D].astype(F32)
            gg = gates_ref[rows, D:2 * D].astype(F32)
            dya = (dym * ga).astype(BF)
            dyg = (dym * gg).astype(BF)
            dya_scr[rows, :] = dya
            dyg_scr[rows, :] = dyg
            dz_ref[rows, 0:D] = (dym * ya_ref[rows, :].astype(F32) * (ga * (1.0 - ga))).astype(BF)
            dz_ref[rows, D:2 * D] = (dym * yg_ref[rows, :].astype(F32) * (gg * (1.0 - gg))).astype(BF)
            do_ref[rows, :] = _dot_nt(dya, wa_ref[...]).astype(BF)
            dgm_ref[rows, :] = _dot_nt(dyg, wg_ref[...]).astype(BF)
        for m0 in range(0, D, CHUNK):
            acc_o[m0:m0 + CHUNK, :] += _dot_tn(ym_ref[:, m0:m0 + CHUNK], dy_scr[...])
        for m0 in range(0, Q_W, CHUNK):
            acc_a[m0:m0 + CHUNK, :] += _dot_tn(att_ref[:, m0:m0 + CHUNK], dya_scr[...])
            acc_g[m0:m0 + CHUNK, :] += _dot_tn(gm_ref[:, m0:m0 + CHUNK], dyg_scr[...])

        @pl.when(i == nb - 1)
        def _():
            for m0 in range(0, D, CHUNK):
                gwo_ref[m0:m0 + CHUNK, :] = acc_o[m0:m0 + CHUNK, :].astype(BF)
            for m0 in range(0, Q_W, CHUNK):
                gwa_ref[m0:m0 + CHUNK, :] = acc_a[m0:m0 + CHUNK, :].astype(BF)
                gwg_ref[m0:m0 + CHUNK, :] = acc_g[m0:m0 + CHUNK, :].astype(BF)

    vec = _const((1, D))
    rows = lambda w_: pl.BlockSpec((R, w_), lambda i: (i, 0))
    return pl.pallas_call(
        body, name="mix_out_bwd", grid=(nb,),
        out_shape=[_sds((S, 2 * D), BF), _sds((S, Q_W), BF), _sds((S, G_W), BF), _sds((1, D), F32),
                   _sds((1, D), F32), _sds((D, D), BF), _sds((Q_W, D), BF), _sds((G_W, D), BF)],
        in_specs=[rows(D), rows(D), rows(D), rows(D), rows(2 * D), rows(Q_W), rows(G_W), rows(D),
                  _resident((Q_W, D)), _resident((G_W, D)), _resident((D, D)), vec, vec],
        out_specs=[rows(2 * D), rows(Q_W), rows(G_W), vec, vec, _const((D, D)), _const((Q_W, D)),
                   _const((G_W, D))],
        scratch_shapes=[pltpu.VMEM((D, D), F32), pltpu.VMEM((Q_W, D), F32), pltpu.VMEM((G_W, D), F32)]
        + [pltpu.VMEM((R, D), BF)] * 3,
        compiler_params=_cp(1, 60),
    )(dh, y, ya, yg, gates, att, gm, ymix, wa, wg, wo, gate, gp)


def _mix_dn(dq, dkv, dzg, dzgate, w, h, dh, sc, gp):
    S = h.shape[0]
    R = min(512, S)

    def body(dq_ref, dkv_ref, dzg_ref, dzt_ref, w_ref, h_ref, dh_ref, sc_ref, gp_ref,
             out_ref, dsh_ref, dsc_ref, dgp_ref):
        @pl.when(pl.program_id(0) == 0)
        def _():
            dsh_ref[...] = jnp.zeros_like(dsh_ref)
            dsc_ref[...] = jnp.zeros_like(dsc_ref)
            dgp_ref[...] = jnp.zeros_like(dgp_ref)
        for r0 in range(0, R, CHUNK):
            rows = slice(r0, r0 + CHUNK)
            dn = _dot(dq_ref[rows, :], w_ref[0:Q_W, :])
            dn = dn + _dot(dkv_ref[rows, :], w_ref[Q_W:QKV_W, :])
            dn = dn + _dot(dzg_ref[rows, :], w_ref[ZG_OFF:GATE_OFF, :])
            dn = dn + _dot(dzt_ref[rows, :], w_ref[GATE_OFF:IN_W, :])
            dx, dsh, dsc, dgp = _prenorm_bwd(dn, h_ref[rows, :], gp_ref[...], sc_ref[...])
            out_ref[rows, :] = dh_ref[rows, :] + dx
            dsh_ref[...] += dsh
            dsc_ref[...] += dsc
            dgp_ref[...] += dgp

    vec = _const((1, D))
    rows = lambda w_: pl.BlockSpec((R, w_), lambda i: (i, 0))
    return pl.pallas_call(
        body, name="mix_dn", grid=(S // R,),
        out_shape=[_sds((S, D), F32)] + [_sds((1, D), F32)] * 3,
        in_specs=[rows(Q_W), rows(2 * KV_W), rows(2 * G_W), rows(2 * D), _resident((IN_W, D)),
                  rows(D), rows(D), vec, vec],
        out_specs=[rows(D), vec, vec, vec],
        compiler_params=_cp(1, 48),
    )(dq, dkv, dzg, dzgate, w, h, dh, sc, gp)


def _adamw_math(w, g, m, v):
    m2 = ADAM_B1 * m + (1.0 - ADAM_B1) * g
    v2 = ADAM_B2 * v + (1.0 - ADAM_B2) * (g * g)
    m_hat = m2 / (1.0 - ADAM_B1 ** ADAM_STEP)
    v_hat = v2 / (1.0 - ADAM_B2 ** ADAM_STEP)
    delta = -ADAM_LR * (m_hat / (jnp.sqrt(v_hat) + ADAM_EPS) + ADAM_WD * w)
    return delta, m2, v2


def _row_tile(rows, cols):
    best = None
    for t in range(16, rows + 1, 16):
        if rows % t == 0 and t * cols <= 256 * 1024:
            best = t
    return best if best is not None else rows


def _adamw_sharded(landing, w, m, v, name):
    r, c = w.shape
    tr = _row_tile(r, c)

    def body(l_ref, w_ref, m_ref, v_ref, g_ref, d_ref, m2_ref, v2_ref):
        g = l_ref[0].astype(F32)
        for j in range(1, N_DEV):
            g = g + l_ref[j].astype(F32)
        delta, m2, v2 = _adamw_math(w_ref[...], g, m_ref[...], v_ref[...])
        g_ref[...] = g
        d_ref[...] = delta
        m2_ref[...] = m2
        v2_ref[...] = v2

    row = pl.BlockSpec((tr, c), lambda i: (i, 0))
    return pl.pallas_call(
        body, name=name, grid=(r // tr,),
        out_shape=[_sds((r, c), F32)] * 4,
        in_specs=[pl.BlockSpec((N_DEV, tr, c), lambda i: (0, i, 0)), row, row, row],
        out_specs=[row] * 4,
        compiler_params=_cp(1, 48),
    )(landing, w, m, v)


def _adamw_small(items):
    n = len(items)

    def body(*refs):
        for k in range(n):
            w_ref, g_ref, m_ref, v_ref = refs[4 * k:4 * k + 4]
            outs = refs[4 * n + 3 * k:4 * n + 3 * k + 3]
            for o_ref, val in zip(outs, _adamw_math(w_ref[...], g_ref[...], m_ref[...], v_ref[...])):
                o_ref[...] = val

    vm = pl.BlockSpec(memory_space=pltpu.VMEM)
    flat = pl.pallas_call(
        body, name="adamw_small",
        out_shape=[_sds(it[0].shape, F32) for it in items for _ in range(3)],
        in_specs=[vm] * (4 * n), out_specs=[vm] * (3 * n),
    )(*[a for it in items for a in it])
    return [tuple(flat[3 * k:3 * k + 3]) for k in range(n)]


def _w_ada_update(c8, d_ada, w, m, v):
    tr = 256

    def body(c_ref, d_ref, w_ref, m_ref, v_ref, g_ref, dl_ref, m2_ref, v2_ref):
        cs = c_ref[...]
        cs = cs * jax.nn.sigmoid(cs)
        g = lax.dot_general(cs, d_ref[...], (((0,), (0,)), ((), ())), preferred_element_type=F32, precision=HIGH)
        delta, m2, v2 = _adamw_math(w_ref[...], g, m_ref[...], v_ref[...])
        g_ref[...] = g
        dl_ref[...] = delta
        m2_ref[...] = m2
        v2_ref[...] = v2

    row = pl.BlockSpec((tr, ADA_W), lambda i: (i, 0))
    return pl.pallas_call(
        body, name="w_ada_update", grid=(D // tr,),
        out_shape=[_sds((D, ADA_W), F32)] * 4,
        in_specs=[pl.BlockSpec((N_DEV, tr), lambda i: (0, i)), _const((N_DEV, ADA_W)), row, row, row],
        out_specs=[row] * 4,
        compiler_params=_cp(1, 40),
    )(c8, d_ada, w, m, v)


def _t5_bucket():
    qi = jnp.arange(BLK, dtype=jnp.int32)[:, None]
    kj = jnp.arange(2 * BLK, dtype=jnp.int32)[None, :]
    dist = jnp.maximum(qi + BLK - kj, 0)
    max_exact = N_BUCKETS // 2
    d_f = jnp.maximum(dist, max_exact).astype(F32)
    large = max_exact + (jnp.log(d_f / max_exact) / math.log(MAX_DISTANCE / max_exact)
                         * (N_BUCKETS - max_exact)).astype(jnp.int32)
    large = jnp.minimum(large, N_BUCKETS - 1)
    return jnp.where(dist < max_exact, dist, large)


def _slabs_of_columns(w):
    r, c8 = w.shape
    return jnp.transpose(w.reshape(r, N_DEV, c8 // N_DEV), (1, 0, 2))


def _columns_of_slabs(w8):
    _, r, c = w8.shape
    return jnp.transpose(w8, (1, 0, 2)).reshape(r, N_DEV * c)


def kernel(x, c, rel_bias, w_ada, b_ada, pre_norm_g, post_norm_g, w_ffn1_in, w_ffn1_out, w_in, sinks, gmlp_ln_g, gmlp_ln_b, gmlp_w_s, gmlp_b_s, w_br_attn, w_br_gmlp, w_out, w_ffn2_in, w_ffn2_out, loss_target, m_rel_bias, m_w_ada, m_b_ada, m_pre_norm_g, m_post_norm_g, m_w_ffn1_in, m_w_ffn1_out, m_w_in, m_sinks, m_gmlp_ln_g, m_gmlp_ln_b, m_gmlp_w_s, m_gmlp_b_s, m_w_br_attn, m_w_br_gmlp, m_w_out, m_w_ffn2_in, m_w_ffn2_out, v_rel_bias, v_w_ada, v_b_ada, v_pre_norm_g, v_post_norm_g, v_w_ffn1_in, v_w_ffn1_out, v_w_in, v_sinks, v_gmlp_ln_g, v_gmlp_ln_b, v_gmlp_w_s, v_gmlp_b_s, v_w_br_attn, v_w_br_gmlp, v_w_out, v_w_ffn2_in, v_w_ffn2_out):
    me = 4 * lax.axis_index("x") + 2 * lax.axis_index("y") + lax.axis_index("c")
    x0 = x[0]
    target = loss_target[0]

    transposed = ("w_ffn1_in", "w_in", "w_ffn2_in")
    shards = [w_ffn1_in[0].T, w_ffn1_out[0], w_in[0].T, w_br_attn[0], w_br_gmlp[0], w_out[0],
              w_ffn2_in[0].T, w_ffn2_out[0]]
    shards_bf = [s.astype(BF) for s in shards]
    groups = [shards_bf[0:1], shards_bf[1:6], shards_bf[6:8]]

    def gather_start(i, after):
        return _slabs_start("gather", groups[i], after, "gather_start_%d" % i)

    def forward_start(st, i, after):
        lands = _slabs_wait("gather", len(groups[i]), st, after, "gather_wait_%d" % i)
        return _slabs_start("forward", lands, c, "forward_start_%d" % i)

    def gathered(st, i, after):
        return _slabs_wait("forward", len(groups[i]), st, after, "forward_wait_%d" % i)

    gs0 = gather_start(0, c)

    mine = jnp.concatenate([c[0], pre_norm_g[0].reshape(-1), post_norm_g[0].reshape(-1)])
    small8 = jnp.broadcast_to(mine[None, :], (8, mine.shape[0]))
    b_ada64 = jnp.repeat(b_ada.reshape(N_DEV, ADA_W), 8, axis=0)
    gath, ada64 = _ada_forward(small8, w_ada[0], b_ada64)
    gath8 = gath[::8]
    ada = ada64[::8].reshape(9, D)
    sh1, sc1, g1, sh2, sc2, g2, sh3, sc3, g3 = [ada[k:k + 1] for k in range(9)]
    gains = gath8[:, D:].reshape(N_DEV, 2, 3, 128)
    pre_g = jnp.transpose(gains[:, 0], (1, 0, 2)).reshape(3, D)
    post_g = jnp.transpose(gains[:, 1], (1, 0, 2)).reshape(3, D)
    pre = [pre_g[k:k + 1] for k in range(3)]
    post = [post_g[k:k + 1] for k in range(3)]

    bucket = _t5_bucket()
    bias = _bias_table(rel_bias, bucket)
    sinks8 = sinks[0]
    lg, lb = gmlp_ln_g, gmlp_ln_b
    ws = gmlp_w_s[0]
    bst = jnp.transpose(gmlp_b_s[0])

    fs0 = forward_start(gs0, 0, sh1)
    gs1 = gather_start(1, fs0[-1])
    wf1_in = gathered(fs0, 0, gs1[-1])[0].reshape(2 * D_FF, D)
    n1, fg1, fu1, fa1 = _ffn_in(x0, sh1, sc1, pre[0], wf1_in, "ffn1_in")
    fs1 = forward_start(gs1, 1, n1)
    gs2 = gather_start(2, fs1[-1])
    mix_w = gathered(fs1, 1, gs2[-1])
    wf1_out = mix_w[0].reshape(D_FF, D)
    w_in_full = mix_w[1].reshape(IN_W, D)
    w_bra = _columns_of_slabs(mix_w[2])
    w_brg = _columns_of_slabs(mix_w[3])
    w_out_full = mix_w[4].reshape(D, D)
    h1, y1 = _ffn_out(fa1, wf1_out, x0, g1, post[0], "ffn1_out")
    n2, qkv, zg, gates = _mix_in(h1, sh2, sc2, pre[1], w_in_full)
    att = _attn_fwd(qkv, bias, sinks8)
    gm = _gmlp_fwd(zg, lg, lb, ws, bst)
    fs2 = forward_start(gs2, 2, gm)
    ya, yg, ymix, y2, h2 = _mix_out(att, gm, gates, h1, w_bra, w_brg, w_out_full, g2 + fs2[-1], post[1])
    wf2_in, wf2_out = gathered(fs2, 2, h2)
    wf2_in = wf2_in.reshape(2 * D_FF, D)
    wf2_out = wf2_out.reshape(D_FF, D)
    n3, fg3, fu3, fa3 = _ffn_in(h2, sh3, sc3, pre[2], wf2_in, "ffn2_in")
    dh3, y3, sq = _ffn_out(fa3, wf2_out, h2, g3, post[2], "ffn2_out", target=target)
    loss = lax.psum(0.5 * sq[0, 0] / D, ("x", "y", "c"))

    def exchange_start(i, arrays):
        return _slabs_start("exchange", arrays, sq, "exchange_start_%d" % i)

    dy3, dgu3, dh2, d_g3, d_post2, d_sh3, d_sc3, d_pre2 = _ffn_bwd(
        dh3, y3, fg3, fu3, wf2_out, wf2_in, h2, g3, post[2], sc3, pre[2], "ffn2_bwd")
    gw_f2_out = _tn_matmul(fa3, dy3, "ffn2_out_wgrad", tm=D_FF // 2).reshape(N_DEV, D_FF // N_DEV, D)
    gw_f2_in = _tn_matmul(dgu3, n3, "ffn2_in_wgrad", tm=D_FF // 2).reshape(N_DEV, FS, D)
    ex1 = exchange_start(1, [gw_f2_out, gw_f2_in])

    dzgate, d_att, d_gm, d_g2, d_post1, gw_out, gw_bra, gw_brg = _mix_out_bwd(
        dh2, y2, ya, yg, gates, att, gm, ymix, w_bra, w_brg, w_out_full, g2 + ex1[-1], post[1])
    ex2 = exchange_start(2, [_slabs_of_columns(gw_bra), _slabs_of_columns(gw_brg),
                             gw_out.reshape(N_DEV, D // N_DEV, D)])
    dq, dkv, dbias, dsink = _attn_bwd(qkv, bias, sinks8, d_att)
    dzg, d_ws, d_bs, d_lg, d_lb = _gmlp_bwd(zg, d_gm, lg, lb, ws, bst)
    d_rel = _rel_bias_grad(dbias, bucket)
    early = jnp.concatenate([
        jnp.concatenate([d_lg.reshape(4, 128), d_lb.reshape(4, 128)], axis=0),
        d_bs, d_rel, dsink, d_ws.reshape(N_HEADS * BLK, BLK)], axis=0)
    sm0 = _slabs_start("gather_all", [early], sq, "small_gather_start")
    dh1, d_sh2, d_sc2, d_pre1 = _mix_dn(dq, dkv, dzg, dzgate, w_in_full, h1, dh2, sc2 + ex2[-1] + sm0[-1], pre[1])
    gw_in = jnp.concatenate(
        [_tn_matmul(dq, n2, "w_in_q_wgrad"), _tn_matmul(dkv, n2, "w_in_kv_wgrad"),
         _tn_matmul(dzg, n2, "w_in_zg_wgrad"), _tn_matmul(dzgate, n2, "w_in_gate_wgrad")],
        axis=0).reshape(N_DEV, IN_W // N_DEV, D)
    ex3 = exchange_start(3, [gw_in])

    dy1, dgu1, d_g1, d_post0 = _ffn_out_bwd(dh1, y1, fg1, fu1, wf1_out, g1 + ex3[-1], post[0], "ffn1_out_bwd")
    gw_f1_out = _tn_matmul(fa1, dy1, "ffn1_out_wgrad", tm=D_FF // 2).reshape(N_DEV, D_FF // N_DEV, D)
    ex4 = exchange_start(4, [gw_f1_out])
    gw_f1_in = _tn_matmul(dgu1, n1, "ffn1_in_wgrad", tm=D_FF // 2).reshape(N_DEV, FS, D)
    ex5 = exchange_start(5, [gw_f1_in])
    grad_x, d_sh1, d_sc1, d_pre0 = _ffn_dn(dgu1, wf1_in, x0, dh1, sc1 + ex4[-1] + ex5[-1], pre[0], "ffn1_dn")

    landed = {}
    for i, (ex, nms) in enumerate([(ex1, ["w_ffn2_out", "w_ffn2_in"]),
                                   (ex2, ["w_br_attn", "w_br_gmlp", "w_out"]), (ex3, ["w_in"]),
                                   (ex4, ["w_ffn1_out"]), (ex5, ["w_ffn1_in"])]):
        for nm, land in zip(nms, _slabs_wait("exchange", len(nms), ex, grad_x, "exchange_wait_%d" % i)):
            landed[nm] = land
    moments = [(m_w_ffn1_in, v_w_ffn1_in), (m_w_ffn1_out, v_w_ffn1_out), (m_w_in, v_w_in),
               (m_w_br_attn, v_w_br_attn), (m_w_br_gmlp, v_w_br_gmlp), (m_w_out, v_w_out),
               (m_w_ffn2_in, v_w_ffn2_in), (m_w_ffn2_out, v_w_ffn2_out)]
    names = ["w_ffn1_in", "w_ffn1_out", "w_in", "w_br_attn", "w_br_gmlp", "w_out", "w_ffn2_in", "w_ffn2_out"]
    big = {}
    for nm, w_, (m_, v_) in zip(names, shards, moments):
        if nm in transposed:
            res4 = _adamw_sharded(landed[nm], w_, m_[0].T, v_[0].T, "adamw_" + nm)
            big[nm] = [a.T[None] for a in res4]
        else:
            big[nm] = [a[None] for a in _adamw_sharded(landed[nm], w_, m_[0], v_[0], "adamw_" + nm)]

    d_ada = jnp.concatenate([v_.reshape(8, 128) for v_ in
                             (d_sh1, d_sc1, d_g1, d_sh2, d_sc2, d_g2, d_sh3, d_sc3, d_g3)], axis=0)
    d_pre = jnp.concatenate([d_pre0, d_pre1, d_pre2], axis=0)
    d_post = jnp.concatenate([d_post0, d_post1, d_post2], axis=0)
    late = jnp.concatenate([d_ada, _slabs_of_columns(d_pre).reshape(24, 128),
                            _slabs_of_columns(d_post).reshape(24, 128)], axis=0)
    late, _ = lax.optimization_barrier((late, landed["w_ffn1_in"]))
    tot, every = _small_allreduce(late)
    (early_land,) = _slabs_wait("gather_all", 1, sm0, grad_x, "small_gather_wait")
    tot_early = _sum_slabs(early_land)

    g_b_ada = tot[0:72].reshape(1, 9 * D)
    g_pre = lax.dynamic_slice_in_dim(tot[72:96], 3 * me, 3, axis=0)[None]
    g_post = lax.dynamic_slice_in_dim(tot[96:120], 3 * me, 3, axis=0)[None]
    g_lg = tot_early[0:4].reshape(1, G_W)
    g_lb = tot_early[4:8].reshape(1, G_W)
    g_bs = tot_early[8:16][None]
    g_rel = jnp.transpose(tot_early[16:24, 0:N_BUCKETS])
    g_sinks = tot_early[24:32, 0][None]
    g_ws = tot_early[32:1056].reshape(1, N_HEADS, BLK, BLK)

    d_ada_mine = lax.dynamic_slice_in_dim(every[:, 0:72].reshape(N_DEV, N_DEV, ADA_W), me, 1, axis=1)[:, 0]
    ada_out = [a[None] for a in _w_ada_update(gath8[:, 0:D], d_ada_mine, w_ada[0], m_w_ada[0], v_w_ada[0])]

    small = [("rel_bias", rel_bias, g_rel, m_rel_bias, v_rel_bias), ("b_ada", b_ada, g_b_ada, m_b_ada, v_b_ada),
             ("pre_norm_g", pre_norm_g, g_pre, m_pre_norm_g, v_pre_norm_g),
             ("post_norm_g", post_norm_g, g_post, m_post_norm_g, v_post_norm_g),
             ("sinks", sinks, g_sinks, m_sinks, v_sinks), ("gmlp_ln_g", gmlp_ln_g, g_lg, m_gmlp_ln_g, v_gmlp_ln_g),
             ("gmlp_ln_b", gmlp_ln_b, g_lb, m_gmlp_ln_b, v_gmlp_ln_b),
             ("gmlp_w_s", gmlp_w_s, g_ws, m_gmlp_w_s, v_gmlp_w_s), ("gmlp_b_s", gmlp_b_s, g_bs, m_gmlp_b_s, v_gmlp_b_s)]
    two_d = lambda a: a.reshape(int(math.prod(a.shape[:-1])), a.shape[-1])
    stepped = _adamw_small([tuple(two_d(a) for a in item[1:]) for item in small])
    res = {"w_ada": ada_out}
    for (nm, w_, g_, _, _), new in zip(small, stepped):
        res[nm] = [g_] + [a.reshape(w_.shape) for a in new]
    res.update(big)
    order = ["rel_bias", "w_ada", "b_ada", "pre_norm_g", "post_norm_g", "w_ffn1_in", "w_ffn1_out", "w_in", "sinks",
             "gmlp_ln_g", "gmlp_ln_b", "gmlp_w_s", "gmlp_b_s", "w_br_attn", "w_br_gmlp", "w_out", "w_ffn2_in",
             "w_ffn2_out"]
    outs = [loss, grad_x[None]]
    for k in range(4):
        outs += [res[nm][k] for nm in order]
    return tuple(outs)
```

```python
import functools
import math

import jax
import jax.numpy as jnp
import numpy as np
from jax import lax
from jax.experimental import pallas as pl
from jax.experimental.pallas import tpu as pltpu

F32 = jnp.float32
BF = jnp.bfloat16

N_DEV = 8
D = 1024
D_FF = 2816
FS = D_FF // 4
N_HEADS = 8
N_KV = 2
GROUP = 4
HD = 64
BLK = 128
Q_W = 512
KV_W = 128
G_W = 512
QKV_W = Q_W + 2 * KV_W
ZG_OFF = QKV_W
GATE_OFF = ZG_OFF + 2 * G_W
IN_W = GATE_OFF + 2 * D
N_BUCKETS = 32
MAX_DISTANCE = 128
EPS = 1e-6
NEG = -1e30
SCALE = HD ** -0.5
ADA_W = 9 * D // N_DEV

ADAM_LR = 0.001
ADAM_B1 = 0.9
ADAM_B2 = 0.999
ADAM_EPS = 1e-08
ADAM_WD = 0.01
ADAM_STEP = 10

CHUNK = 256
MIB = 1024 * 1024
MESH = pl.DeviceIdType.MESH
HIGH = lax.Precision.HIGHEST


def _cp(n_grid, vmem_mib):
    return pltpu.CompilerParams(dimension_semantics=("arbitrary",) * n_grid,
                                vmem_limit_bytes=vmem_mib * MIB)


def _const(shape):
    return pl.BlockSpec(shape, lambda *_: (0,) * len(shape))


def _resident(shape):
    return pl.BlockSpec(shape, lambda *_: (0,) * len(shape), pipeline_mode=pl.Buffered(1))


def _sds(shape, dtype):
    return jax.ShapeDtypeStruct(shape, dtype)


def _dot(a, b):
    return jnp.dot(a, b, preferred_element_type=F32)


def _dot_nt(a, b):
    return lax.dot_general(a, b, (((1,), (1,)), ((), ())), preferred_element_type=F32)


def _dot_tn(a, b):
    return lax.dot_general(a, b, (((0,), (0,)), ((), ())), preferred_element_type=F32)


def _rms_r(x):
    return lax.rsqrt(jnp.mean(x * x, axis=-1, keepdims=True) + EPS)


def _colsum(x):
    return jnp.sum(x, axis=0, keepdims=True)


def _prenorm(x, gp, sc, sh):
    return (x * _rms_r(x) * gp) * (1.0 + sc) + sh


def _prenorm_bwd(dn, x, gp, sc):
    r = _rms_r(x)
    xh = x * r
    t = dn * (1.0 + sc) * gp
    dx = r * (t - xh * jnp.mean(t * xh, axis=-1, keepdims=True))
    return dx, _colsum(dn), _colsum(dn * xh * gp), _colsum(dn * (1.0 + sc) * xh)


def _postnorm_bwd(dh, y, gate, gp, res):
    y = y.astype(F32)
    r = _rms_r(y)
    yh = y * r
    dyn = (res * gate) * dh
    t = dyn * gp
    dy = r * (t - yh * jnp.mean(t * yh, axis=-1, keepdims=True))
    return dy, _colsum(res * dh * yh * gp), _colsum(dyn * yh)


def _gelu(x):
    k = math.sqrt(2.0 / math.pi)
    return 0.5 * x * (1.0 + jnp.tanh(k * (x + 0.044715 * x * x * x)))


def _gelu_grad(x):
    k = math.sqrt(2.0 / math.pi)
    t = jnp.tanh(k * (x + 0.044715 * x * x * x))
    return 0.5 * (1.0 + t) + 0.5 * x * (1.0 - t * t) * (k * (1.0 + 3.0 * 0.044715 * x * x))


def _my_place():
    x, y, c = lax.axis_index("x"), lax.axis_index("y"), lax.axis_index("c")
    return x, y, c, 4 * x + 2 * y + c


def _peer(x, y, c, k):
    px = 1 - x if k & 4 else x
    py = 1 - y if k & 2 else y
    pc = 1 - c if k & 1 else c
    return (px, py, pc), 4 * px + 2 * py + pc


HBM_SPEC = pl.BlockSpec(memory_space=pltpu.HBM)
SEM_SPEC = pl.BlockSpec(memory_space=pltpu.SEMAPHORE)
EFFECT = pltpu.SideEffectType.DATAFLOW_SIDE_EFFECTING


RELATIONS = {"exchange": (1, 2, 3, 4, 5, 6, 7), "gather": (1, 2, 4, 6), "forward": (2, 4, 6),
             "gather_all": (1, 2, 3, 4, 5, 6, 7)}


def _slab_copies(mode, srcs, lands, send, recv, loc):
    x, y, c, me = _my_place()
    rel = RELATIONS[mode]
    remote, local = [], []
    for t in range(len(lands)):
        for i, k in enumerate(rel):
            peer, peer_lin = _peer(x, y, c, k)
            if mode == "exchange":
                src, dst, to = srcs[t].at[peer_lin], lands[t].at[me], peer
            elif mode in ("gather", "gather_all"):
                src, dst, to = srcs[t], lands[t].at[me], peer
            else:
                src, dst, to = lands[t].at[peer_lin], lands[t].at[peer_lin], _peer(x, y, c, 1)[0]
            remote.append(pltpu.make_async_remote_copy(
                src_ref=src, dst_ref=dst, send_sem=send.at[t * len(rel) + i], recv_sem=recv.at[t * len(rel) + i],
                device_id=to, device_id_type=MESH))
        if mode == "exchange":
            local.append(pltpu.make_async_copy(srcs[t].at[me], lands[t].at[me], loc.at[t]))
        elif mode in ("gather", "gather_all"):
            local.append(pltpu.make_async_copy(srcs[t], lands[t].at[me], loc.at[t]))
    return remote, local


def _slabs_start(mode, arrays, after, name):
    n = len(arrays)
    if mode == "forward":
        thru = list(arrays)
    else:
        shapes = [a.shape if mode == "exchange" else (N_DEV,) + a.shape for a in arrays]
        thru = list(arrays) + [lax.empty(s, a.dtype) for s, a in zip(shapes, arrays)]
    m = len(thru)
    n_sem = n * len(RELATIONS[mode])

    def body(*refs):
        srcs, lands = refs[:n], refs[m - n:m]
        send, recv, loc = refs[m + 1:m + 4]
        remote, local = _slab_copies(mode, srcs, lands, send, recv, loc)
        for cp in remote + local:
            cp.start()
        refs[-1][...] = jnp.zeros_like(refs[-1])

    return pl.pallas_call(
        body, name=name,
        out_shape=(pltpu.SemaphoreType.DMA((n_sem,)), pltpu.SemaphoreType.DMA((n_sem,)),
                   pltpu.SemaphoreType.DMA((n,)),
                   *[pltpu.HBM(a.shape, a.dtype) for a in thru],
                   _sds((1, D), F32)),
        in_specs=[HBM_SPEC] * m + [pl.BlockSpec(memory_space=pl.ANY)],
        out_specs=(SEM_SPEC, SEM_SPEC, SEM_SPEC, *[HBM_SPEC] * m, pl.BlockSpec(memory_space=pltpu.VMEM)),
        input_output_aliases={t: 3 + t for t in range(m)},
        compiler_params=pltpu.CompilerParams(has_side_effects=EFFECT),
    )(*[pltpu.with_memory_space_constraint(a, pltpu.HBM) for a in thru], after)


def _slabs_wait(mode, n, started, after, name):
    sems = started[0:3]
    thru = started[3:-1]
    m = len(thru)

    def body(*refs):
        srcs, lands = refs[:n], refs[m - n:m]
        remote, local = _slab_copies(mode, srcs, lands, *refs[m:m + 3])
        for cp in remote:
            cp.wait_send()
            cp.wait_recv()
        for cp in local:
            cp.wait()

    res = pl.pallas_call(
        body, name=name,
        out_shape=tuple(pltpu.HBM(a.shape, a.dtype) for a in thru),
        in_specs=[HBM_SPEC] * m + [SEM_SPEC] * 3 + [pl.BlockSpec(memory_space=pl.ANY)],
        out_specs=tuple([HBM_SPEC] * m),
        input_output_aliases={t: t for t in range(m)},
        compiler_params=pltpu.CompilerParams(has_side_effects=EFFECT),
    )(*thru, *sems, after)
    return list(res[m - n:m])


def _ada_forward(small8, w_ada, b_ada64):
    sw = small8.shape[1]

    def body(sm_ref, w_ref, b_ref, gath_ref, ada_ref, part_ref, send1, recv1, send2, recv2):
        x, y, c, me = _my_place()
        row_me = pl.multiple_of(me * 8, 8)
        gath_ref[pl.ds(row_me, 8), :] = sm_ref[...]
        first = []
        for k in range(1, N_DEV):
            peer, _ = _peer(x, y, c, k)
            cp = pltpu.make_async_remote_copy(
                src_ref=sm_ref, dst_ref=gath_ref.at[pl.ds(row_me, 8), :], send_sem=send1.at[k - 1],
                recv_sem=recv1.at[k - 1], device_id=peer, device_id_type=MESH)
            cp.start()
            first.append(cp)
        for cp in first:
            cp.wait()
        cs = gath_ref[:, 0:D]
        cs = cs * jax.nn.sigmoid(cs)
        part_ref[...] = jnp.dot(cs, w_ref[...], preferred_element_type=F32, precision=HIGH)
        ada_ref[pl.ds(row_me, 8), :] = part_ref[pl.ds(row_me, 8), :]
        second = []
        for k in range(1, N_DEV):
            peer, peer_lin = _peer(x, y, c, k)
            cp = pltpu.make_async_remote_copy(
                src_ref=part_ref.at[pl.ds(pl.multiple_of(peer_lin * 8, 8), 8), :],
                dst_ref=ada_ref.at[pl.ds(row_me, 8), :], send_sem=send2.at[k - 1],
                recv_sem=recv2.at[k - 1], device_id=peer, device_id_type=MESH)
            cp.start()
            second.append(cp)
        for cp in second:
            cp.wait()
        ada_ref[...] = ada_ref[...] + b_ref[...]

    vm = pl.BlockSpec(memory_space=pltpu.VMEM)
    return pl.pallas_call(
        body, name="ada_forward",
        out_shape=[_sds((8 * N_DEV, sw), F32), _sds((8 * N_DEV, ADA_W), F32)],
        in_specs=[vm, vm, vm], out_specs=[vm, vm],
        scratch_shapes=[pltpu.VMEM((8 * N_DEV, ADA_W), F32)] + [pltpu.SemaphoreType.DMA((7,))] * 4,
        compiler_params=pltpu.CompilerParams(vmem_limit_bytes=32 * MIB),
    )(small8, w_ada, b_ada64)


def _sum_slabs(land):
    def body(l_ref, o_ref):
        acc = l_ref[0]
        for j in range(1, N_DEV):
            acc = acc + l_ref[j]
        o_ref[...] = acc

    vm = pl.BlockSpec(memory_space=pltpu.VMEM)
    return pl.pallas_call(body, name="sum_slabs", out_shape=_sds(land.shape[1:], F32), in_specs=[vm], out_specs=vm,
                          compiler_params=pltpu.CompilerParams(vmem_limit_bytes=32 * MIB))(land)


def _small_allreduce(pack):
    rows = pack.shape[0]

    def body(p_ref, sum_ref, gath_ref, send, recv):
        x, y, c, me = _my_place()
        gath_ref[me] = p_ref[...]
        cps = []
        for k in range(1, N_DEV):
            peer, _ = _peer(x, y, c, k)
            cp = pltpu.make_async_remote_copy(
                src_ref=p_ref, dst_ref=gath_ref.at[me], send_sem=send.at[k - 1],
                recv_sem=recv.at[k - 1], device_id=peer, device_id_type=MESH)
            cp.start()
            cps.append(cp)
        for cp in cps:
            cp.wait()
        acc = gath_ref[0]
        for j in range(1, N_DEV):
            acc = acc + gath_ref[j]
        sum_ref[...] = acc

    vm = pl.BlockSpec(memory_space=pltpu.VMEM)
    return pl.pallas_call(
        body, name="small_allreduce",
        out_shape=[_sds((rows, 128), F32), _sds((N_DEV, rows, 128), F32)],
        in_specs=[vm], out_specs=[vm, vm],
        scratch_shapes=[pltpu.SemaphoreType.DMA((7,)), pltpu.SemaphoreType.DMA((7,))],
        compiler_params=pltpu.CompilerParams(vmem_limit_bytes=40 * MIB),
    )(pack)


F_TILES = tuple((f0, min(512, D_FF - f0)) for f0 in range(0, D_FF, 512))
F_TILES_NARROW = tuple((f0, 256) for f0 in range(0, D_FF, 256))


def _swiglu_tile(n, wt_ref, f0, tf):
    g = _dot_nt(n, wt_ref[f0:f0 + tf, :])
    u = _dot_nt(n, wt_ref[D_FF + f0:D_FF + f0 + tf, :])
    sg = jax.nn.sigmoid(g)
    silu = g * sg
    return (u * (sg * (1.0 + g * (1.0 - sg)))).astype(BF), silu.astype(BF), (silu * u).astype(BF)


def _ffn_in(h, sh, sc, gp, wt, name):
    S = h.shape[0]
    R = min(512, S)

    def body(h_ref, sh_ref, sc_ref, gp_ref, w_ref, n_ref, dg_ref, sl_ref, a_ref):
        for r0 in range(0, R, CHUNK):
            rows = slice(r0, r0 + CHUNK)
            n = _prenorm(h_ref[rows, :], gp_ref[...], sc_ref[...], sh_ref[...]).astype(BF)
            n_ref[rows, :] = n
            for f0, tf in F_TILES_NARROW:
                dg_ref[rows, f0:f0 + tf], sl_ref[rows, f0:f0 + tf], a_ref[rows, f0:f0 + tf] = _swiglu_tile(
                    n, w_ref, f0, tf)

    vec = _const((1, D))
    rows_ = lambda w_: pl.BlockSpec((R, w_), lambda i: (i, 0))
    return pl.pallas_call(
        body, name=name, grid=(S // R,),
        out_shape=[_sds((S, D), BF)] + [_sds((S, D_FF), BF)] * 3,
        in_specs=[rows_(D), vec, vec, vec, _resident((2 * D_FF, D))],
        out_specs=[rows_(D), rows_(D_FF), rows_(D_FF), rows_(D_FF)],
        compiler_params=_cp(1, 56),
    )(h, sh, sc, gp, wt)


def _ffn_out(a, w, h, gate, gp, name, target=None):
    S = h.shape[0]
    R = min(512, S)
    with_loss = target is not None

    def body(a_ref, w_ref, h_ref, gate_ref, gp_ref, *rest):
        if with_loss:
            t_ref, out_ref, y_ref, tot_ref = rest

            @pl.when(pl.program_id(0) == 0)
            def _():
                tot_ref[...] = jnp.zeros_like(tot_ref)
        else:
            out_ref, y_ref = rest
        for r0 in range(0, R, CHUNK):
            rows = slice(r0, r0 + CHUNK)
            y = _dot(a_ref[rows, :], w_ref[...])
            y_ref[rows, :] = y.astype(BF)
            hn = h_ref[rows, :] + (0.5 * gate_ref[...]) * (y * _rms_r(y) * gp_ref[...])
            if with_loss:
                e = hn - t_ref[rows, :]
                out_ref[rows, :] = e * (1.0 / D)
                tot_ref[...] += jnp.sum(jnp.sum(e * e, axis=1, keepdims=True), axis=0, keepdims=True)
            else:
                out_ref[rows, :] = hn

    vec = _const((1, D))
    rows_ = lambda w_: pl.BlockSpec((R, w_), lambda i: (i, 0))
    return pl.pallas_call(
        body, name=name, grid=(S // R,),
        out_shape=[_sds((S, D), F32), _sds((S, D), BF)] + ([_sds((1, 1), F32)] if with_loss else []),
        in_specs=[rows_(D_FF), _resident((D_FF, D)), rows_(D), vec, vec] + ([rows_(D)] if with_loss else []),
        out_specs=[rows_(D), rows_(D)] + ([_const((1, 1))] if with_loss else []),
        compiler_params=_cp(1, 48),
    )(*((a, w, h, gate, gp) + ((target,) if with_loss else ())))


def _ffn_out_bwd(dh, y, dsilu_u, silu, w, gate, gp, name):
    S = dh.shape[0]
    R = min(512, S)

    def body(dh_ref, y_ref, g_ref, u_ref, w_ref, gate_ref, gp_ref, dy_ref, dgu_ref, dgate_ref, dgp_ref):
        @pl.when(pl.program_id(0) == 0)
        def _():
            dgate_ref[...] = jnp.zeros_like(dgate_ref)
            dgp_ref[...] = jnp.zeros_like(dgp_ref)
        for r0 in range(0, R, CHUNK):
            rows = slice(r0, r0 + CHUNK)
            dy, dgate, dgp = _postnorm_bwd(dh_ref[rows, :], y_ref[rows, :], gate_ref[...], gp_ref[...], 0.5)
            dgate_ref[...] += dgate
            dgp_ref[...] += dgp
            dyb = dy.astype(BF)
            dy_ref[rows, :] = dyb
            for f0, tf in F_TILES:
                da = _dot_nt(dyb, w_ref[f0:f0 + tf, :])
                dgu_ref[rows, f0:f0 + tf] = (da * g_ref[rows, f0:f0 + tf].astype(F32)).astype(BF)
                dgu_ref[rows, D_FF + f0:D_FF + f0 + tf] = (da * u_ref[rows, f0:f0 + tf].astype(F32)).astype(BF)

    vec = _const((1, D))
    rows_ = lambda w_: pl.BlockSpec((R, w_), lambda i: (i, 0))
    return pl.pallas_call(
        body, name=name, grid=(S // R,),
        out_shape=[_sds((S, D), BF), _sds((S, 2 * D_FF), BF), _sds((1, D), F32), _sds((1, D), F32)],
        in_specs=[rows_(D), rows_(D), rows_(D_FF), rows_(D_FF), _resident((D_FF, D)), vec, vec],
        out_specs=[rows_(D), rows_(2 * D_FF), vec, vec],
        compiler_params=_cp(1, 56),
    )(dh, y, dsilu_u, silu, w, gate, gp)


def _ffn_dn(dgu, wt, h, dh, sc, gp, name):
    S = h.shape[0]
    R = min(512, S)

    def body(dgu_ref, w_ref, h_ref, dh_ref, sc_ref, gp_ref, out_ref, dsh_ref, dsc_ref, dgp_ref):
        @pl.when(pl.program_id(0) == 0)
        def _():
            dsh_ref[...] = jnp.zeros_like(dsh_ref)
            dsc_ref[...] = jnp.zeros_like(dsc_ref)
            dgp_ref[...] = jnp.zeros_like(dgp_ref)

        for r0 in range(0, R, CHUNK):
            rows = slice(r0, r0 + CHUNK)
            dn = _dot(dgu_ref[rows, :], w_ref[...])
            dx, dsh, dsc, dgp = _prenorm_bwd(dn, h_ref[rows, :], gp_ref[...], sc_ref[...])
            out_ref[rows, :] = dh_ref[rows, :] + dx
            dsh_ref[...] += dsh
            dsc_ref[...] += dsc
            dgp_ref[...] += dgp

    vec = _const((1, D))
    rows_ = lambda w_: pl.BlockSpec((R, w_), lambda i: (i, 0))
    return pl.pallas_call(
        body, name=name, grid=(S // R,),
        out_shape=[_sds((S, D), F32)] + [_sds((1, D), F32)] * 3,
        in_specs=[rows_(2 * D_FF), _resident((2 * D_FF, D)), rows_(D), rows_(D), vec, vec],
        out_specs=[rows_(D), vec, vec, vec],
        compiler_params=_cp(1, 56),
    )(dgu, wt, h, dh, sc, gp)


def _ffn_bwd(dh, y, dsilu_u, silu, w, wt, h, gate, gpost, sc, gpre, name):
    S = dh.shape[0]
    R = min(256, S)

    def body(dh_ref, y_ref, g_ref, u_ref, w_ref, wt_ref, h_ref, gate_ref, gpost_ref, sc_ref, gpre_ref,
             dy_ref, dgu_ref, out_ref, dgate_ref, dgpost_ref, dsh_ref, dsc_ref, dgpre_ref):
        @pl.when(pl.program_id(0) == 0)
        def _():
            for r in (dgate_ref, dgpost_ref, dsh_ref, dsc_ref, dgpre_ref):
                r[...] = jnp.zeros_like(r)
        dhh = dh_ref[...]
        dy, dgate, dgpost = _postnorm_bwd(dhh, y_ref[...], gate_ref[...], gpost_ref[...], 0.5)
        dgate_ref[...] += dgate
        dgpost_ref[...] += dgpost
        dyb = dy.astype(BF)
        dy_ref[...] = dyb
        dn = None
        for f0, tf in F_TILES:
            da = _dot_nt(dyb, w_ref[f0:f0 + tf, :])
            dg = (da * g_ref[:, f0:f0 + tf].astype(F32)).astype(BF)
            du = (da * u_ref[:, f0:f0 + tf].astype(F32)).astype(BF)
            dgu_ref[:, f0:f0 + tf] = dg
            dgu_ref[:, D_FF + f0:D_FF + f0 + tf] = du
            part = _dot(dg, wt_ref[f0:f0 + tf, :]) + _dot(du, wt_ref[D_FF + f0:D_FF + f0 + tf, :])
            dn = part if dn is None else dn + part
        dx, dsh, dsc, dgpre = _prenorm_bwd(dn, h_ref[...], gpre_ref[...], sc_ref[...])
        out_ref[...] = dhh + dx
        dsh_ref[...] += dsh
        dsc_ref[...] += dsc
        dgpre_ref[...] += dgpre

    vec = _const((1, D))
    rows_ = lambda w_: pl.BlockSpec((R, w_), lambda i: (i, 0))
    return pl.pallas_call(
        body, name=name, grid=(S // R,),
        out_shape=[_sds((S, D), BF), _sds((S, 2 * D_FF), BF), _sds((S, D), F32)] + [_sds((1, D), F32)] * 5,
        in_specs=[rows_(D), rows_(D), rows_(D_FF), rows_(D_FF), _resident((D_FF, D)), _resident((2 * D_FF, D)),
                  rows_(D), vec, vec, vec, vec],
        out_specs=[rows_(D), rows_(2 * D_FF), rows_(D)] + [vec] * 5,
        compiler_params=_cp(1, 56),
    )(dh, y, dsilu_u, silu, w, wt, h, gate, gpost, sc, gpre)


def _tn_matmul(a, b, name, tm=None):
    S, M_all = a.shape
    N = b.shape[1]
    M = M_all if tm is None else tm
    GA = M_all // M
    ts = min(2048 if M * N <= 2 * D * D else 1024, S)
    nk = S // ts
    chunks = [(m0, min(CHUNK, M - m0)) for m0 in range(0, M, CHUNK)]

    def body(a_ref, b_ref, o_ref, acc):
        k = pl.program_id(1)

        @pl.when(k == 0)
        def _():
            acc[...] = jnp.zeros_like(acc)

        for m0, mc in chunks:
            acc[m0:m0 + mc, :] += _dot_tn(a_ref[:, m0:m0 + mc], b_ref[...])

        @pl.when(k == nk - 1)
        def _():
            for m0, mc in chunks:
                o_ref[m0:m0 + mc, :] = acc[m0:m0 + mc, :].astype(BF)

    return pl.pallas_call(
        body, name=name, grid=(GA, nk),
        out_shape=_sds((M_all, N), BF),
        in_specs=[pl.BlockSpec((ts, M), lambda ga, k: (k, ga)), pl.BlockSpec((ts, N), lambda ga, k: (k, 0))],
        out_specs=pl.BlockSpec((M, N), lambda ga, k: (ga, 0)),
        scratch_shapes=[pltpu.VMEM((M, N), F32)],
        compiler_params=_cp(2, 56),
    )(a, b)


def _mix_in(h, sh, sc, gp, w):
    S = h.shape[0]
    R = min(512, S)

    def body(h_ref, sh_ref, sc_ref, gp_ref, w_ref, n_ref, qkv_ref, zg_ref, gates_ref):
        for r0 in range(0, R, CHUNK):
            rows = slice(r0, r0 + CHUNK)
            nb = _prenorm(h_ref[rows, :], gp_ref[...], sc_ref[...], sh_ref[...]).astype(BF)
            n_ref[rows, :] = nb
            qkv_ref[rows, :] = _dot_nt(nb, w_ref[0:ZG_OFF, :]).astype(BF)
            zg_ref[rows, :] = _dot_nt(nb, w_ref[ZG_OFF:GATE_OFF, :]).astype(BF)
            gates_ref[rows, :] = jax.nn.sigmoid(_dot_nt(nb, w_ref[GATE_OFF:IN_W, :])).astype(BF)

    vec = _const((1, D))
    rows = lambda w_: pl.BlockSpec((R, w_), lambda i: (i, 0))
    return pl.pallas_call(
        body, name="mix_in", grid=(S // R,),
        out_shape=[_sds((S, D), BF), _sds((S, QKV_W), BF), _sds((S, 2 * G_W), BF), _sds((S, 2 * D), BF)],
        in_specs=[rows(D), vec, vec, vec, _resident((IN_W, D))],
        out_specs=[rows(D), rows(QKV_W), rows(2 * G_W), rows(2 * D)],
        compiler_params=_cp(1, 48),
    )(h, sh, sc, gp, w)


def _bias_table(rel_bias, bucket):
    def body(rel_ref, bk_ref, out_ref):
        bk = bk_ref[...]
        qi = lax.broadcasted_iota(jnp.int32, (BLK, 2 * BLK), 0)
        kj = lax.broadcasted_iota(jnp.int32, (BLK, 2 * BLK), 1)
        dist = qi + BLK - kj
        window = (dist >= 0) & (dist < BLK)
        for h in range(N_HEADS):
            acc = jnp.zeros((BLK, 2 * BLK), F32)
            for b in range(N_BUCKETS):
                acc = jnp.where(bk == b, rel_ref[b, h], acc)
            out_ref[h // GROUP, pl.ds((h % GROUP) * BLK, BLK), :] = jnp.where(window, acc, NEG)

    return pl.pallas_call(
        body, name="bias_table",
        out_shape=_sds((N_KV, GROUP * BLK, 2 * BLK), F32),
        in_specs=[pl.BlockSpec(memory_space=pltpu.SMEM), pl.BlockSpec(memory_space=pltpu.VMEM)],
        out_specs=pl.BlockSpec(memory_space=pltpu.VMEM),
    )(rel_bias, bucket)


ATT_TB = 4


def _attn_scores(q, kvc, kvp, bias_ref, sink_ref, has_prev, kh, g0=0, ng=GROUP):
    k2 = jnp.concatenate([kvp[:, kh * HD:(kh + 1) * HD], kvc[:, kh * HD:(kh + 1) * HD]], axis=0)
    v2 = jnp.concatenate([kvp[:, KV_W + kh * HD:KV_W + (kh + 1) * HD],
                          kvc[:, KV_W + kh * HD:KV_W + (kh + 1) * HD]], axis=0)
    q4 = jnp.concatenate([q[:, (kh * GROUP + g) * HD:(kh * GROUP + g + 1) * HD] for g in range(g0, g0 + ng)], axis=0)
    s = _dot_nt(q4, k2) * SCALE + bias_ref[kh, g0 * BLK:(g0 + ng) * BLK, :]
    if has_prev is not None:
        col = lax.broadcasted_iota(jnp.int32, (ng * BLK, 2 * BLK), 1)
        s = jnp.where((col >= BLK) | has_prev, s, NEG)
    rowg = lax.broadcasted_iota(jnp.int32, (ng * BLK, 1), 0) // BLK
    sink = jnp.zeros((ng * BLK, 1), F32)
    for g in range(ng):
        sink = jnp.where(rowg == g, sink_ref[kh * GROUP + g0 + g], sink)
    return q4, k2, v2, s, sink


def _attn_fwd(qkv, bias, sinks):
    S = qkv.shape[0]
    tb = min(ATT_TB, S // BLK)
    T = tb * BLK

    def body(sink_ref, q_ref, kv_ref, kvp_ref, bias_ref, o_ref):
        step = pl.program_id(0)
        for j in range(tb):
            rows = slice(j * BLK, (j + 1) * BLK)
            q, kvc = q_ref[rows, :], kv_ref[rows, :]
            kvp = kvp_ref[...] if j == 0 else kv_ref[(j - 1) * BLK:j * BLK, :]
            has_prev = (step > 0) if j == 0 else None
            outs = []
            for kh in range(N_KV):
                q4, k2, v2, s, sink = _attn_scores(q, kvc, kvp, bias_ref, sink_ref, has_prev, kh)
                m = jnp.maximum(jnp.max(s, axis=1, keepdims=True), sink)
                p = jnp.exp(s - m)
                denom = jnp.sum(p, axis=1, keepdims=True) + jnp.exp(sink - m)
                o4 = _dot((p / denom).astype(BF), v2)
                outs += [o4[g * BLK:(g + 1) * BLK] for g in range(GROUP)]
            o_ref[rows, :] = jnp.concatenate(outs, axis=1).astype(BF)

    return pl.pallas_call(
        body, name="attn_fwd", grid=(S // T,),
        out_shape=_sds((S, Q_W), BF),
        in_specs=[pl.BlockSpec(memory_space=pltpu.SMEM),
                  pl.BlockSpec((T, Q_W), lambda i: (i, 0)),
                  pl.BlockSpec((T, 2 * KV_W), lambda i: (i, 2)),
                  pl.BlockSpec((BLK, 2 * KV_W), lambda i: (jnp.maximum(i * tb - 1, 0), 2)),
                  _const((N_KV, GROUP * BLK, 2 * BLK))],
        out_specs=pl.BlockSpec((T, Q_W), lambda i: (i, 0)),
        compiler_params=_cp(1, 32),
    )(sinks, qkv, qkv, qkv, bias)


def _attn_bwd(qkv, bias, sinks, do):
    S = qkv.shape[0]
    tb = 1
    ng = GROUP
    T = tb * BLK
    nt = S // T

    def body(sink_ref, q_ref, kv_ref, kvp_ref, bias_ref, do_ref, dq_ref, dkv_ref, dbias_ref, dsink_ref, carry):
        i = pl.program_id(0)

        @pl.when(i == 0)
        def _():
            carry[...] = jnp.zeros_like(carry)
            dbias_ref[...] = jnp.zeros_like(dbias_ref)
            dsink_ref[...] = jnp.zeros_like(dsink_ref)

        from_next = carry[...]
        for j in reversed(range(tb)):
            rows = slice(j * BLK, (j + 1) * BLK)
            q, kvc, do_ = q_ref[rows, :], kv_ref[rows, :], do_ref[rows, :]
            kvp = kvp_ref[...] if j == 0 else kv_ref[(j - 1) * BLK:j * BLK, :]
            has_prev = (i < nt - 1) if j == 0 else None
            dqs, dk_cur, dv_cur, dk_prev, dv_prev = [], [], [], [], []
            head_row = lax.broadcasted_iota(jnp.int32, (N_HEADS, 128), 0)
            dsink_rows = jnp.zeros((N_HEADS, 128), F32)
            for kh in range(N_KV):
                dk2, dv2 = None, None
                for g0 in range(0, GROUP, ng):
                    q4, k2, v2, s, sink = _attn_scores(q, kvc, kvp, bias_ref, sink_ref, has_prev, kh, g0, ng)
                    m = jnp.maximum(jnp.max(s, axis=1, keepdims=True), sink)
                    p = jnp.exp(s - m)
                    denom = jnp.sum(p, axis=1, keepdims=True) + jnp.exp(sink - m)
                    prob = p / denom
                    p_sink = jnp.exp(sink - m) / denom
                    pb = prob.astype(BF)
                    do4 = jnp.concatenate([do_[:, (kh * GROUP + g) * HD:(kh * GROUP + g + 1) * HD]
                                           for g in range(g0, g0 + ng)], axis=0)
                    dp = _dot_nt(do4, v2)
                    o4 = _dot(pb, v2)
                    delta = jnp.sum(do4.astype(F32) * o4, axis=1, keepdims=True)
                    ds = prob * (dp - delta)
                    dbias_ref[kh, g0 * BLK:(g0 + ng) * BLK, :] += ds
                    sink_term = p_sink * delta
                    for g in range(ng):
                        val = -jnp.sum(sink_term[g * BLK:(g + 1) * BLK], axis=0, keepdims=True)
                        dsink_rows = jnp.where(head_row == kh * GROUP + g0 + g, val, dsink_rows)
                    dsb = ds.astype(BF)
                    dq4 = _dot(dsb, k2) * SCALE
                    dk_part = jnp.transpose(_dot_tn(q4, dsb)) * SCALE
                    dv_part = jnp.transpose(_dot_tn(do4, pb))
                    dk2 = dk_part if dk2 is None else dk2 + dk_part
                    dv2 = dv_part if dv2 is None else dv2 + dv_part
                    dqs += [dq4[g * BLK:(g + 1) * BLK] for g in range(ng)]
                dk_prev.append(dk2[0:BLK])
                dk_cur.append(dk2[BLK:2 * BLK])
                dv_prev.append(dv2[0:BLK])
                dv_cur.append(dv2[BLK:2 * BLK])
            dsink_ref[...] += dsink_rows
            dq_ref[rows, :] = jnp.concatenate(dqs, axis=1).astype(BF)
            dkv_ref[rows, :] = (jnp.concatenate(dk_cur + dv_cur, axis=1) + from_next).astype(BF)
            from_next = jnp.concatenate(dk_prev + dv_prev, axis=1)
        carry[...] = from_next

    return pl.pallas_call(
        body, name="attn_bwd", grid=(nt,),
        out_shape=[_sds((S, Q_W), BF), _sds((S, 2 * KV_W), BF),
                   _sds((N_KV, GROUP * BLK, 2 * BLK), F32), _sds((N_HEADS, 128), F32)],
        in_specs=[pl.BlockSpec(memory_space=pltpu.SMEM),
                  pl.BlockSpec((T, Q_W), lambda i: (nt - 1 - i, 0)),
                  pl.BlockSpec((T, 2 * KV_W), lambda i: (nt - 1 - i, 2)),
                  pl.BlockSpec((BLK, 2 * KV_W), lambda i: (jnp.maximum((nt - 1 - i) * tb - 1, 0), 2)),
                  _const((N_KV, GROUP * BLK, 2 * BLK)),
                  pl.BlockSpec((T, Q_W), lambda i: (nt - 1 - i, 0))],
        out_specs=[pl.BlockSpec((T, Q_W), lambda i: (nt - 1 - i, 0)),
                   pl.BlockSpec((T, 2 * KV_W), lambda i: (nt - 1 - i, 0)),
                   _const((N_KV, GROUP * BLK, 2 * BLK)), _const((N_HEADS, 128))],
        scratch_shapes=[pltpu.VMEM((BLK, 2 * KV_W), F32)],
        compiler_params=_cp(1, 32),
    )(sinks, qkv, qkv, qkv, bias, do)


def _rel_bias_grad(dbias, bucket):
    def body(db_ref, bk_ref, out_ref):
        bk = bk_ref[...]
        lane = lax.broadcasted_iota(jnp.int32, (1, 128), 1)
        for h in range(N_HEADS):
            d = db_ref[h // GROUP, pl.ds((h % GROUP) * BLK, BLK), :]
            row = jnp.zeros((1, 128), F32)
            for b in range(N_BUCKETS):
                tot = jnp.sum(jnp.sum(jnp.where(bk == b, d, 0.0), axis=1, keepdims=True), axis=0, keepdims=True)
                row = jnp.where(lane == b, tot, row)
            out_ref[pl.ds(h, 1), :] = row

    vm = pl.BlockSpec(memory_space=pltpu.VMEM)
    return pl.pallas_call(body, name="rel_bias_grad", out_shape=_sds((N_HEADS, 128), F32),
                          in_specs=[vm, vm], out_specs=vm)(dbias, bucket)


def _gmlp_parts(zg, lg_ref, lb_ref):
    z = zg.astype(F32)
    ge = _gelu(z)
    u, vg = ge[:, 0:G_W], ge[:, G_W:2 * G_W]
    mu = jnp.mean(vg, axis=-1, keepdims=True)
    xc = vg - mu
    rstd = lax.rsqrt(jnp.mean(xc * xc, axis=-1, keepdims=True) + EPS)
    xh = xc * rstd
    return z, u, xh, rstd, xh * lg_ref[...] + lb_ref[...]


def _causal_weights(ws_ref, wc):
    t = lax.broadcasted_iota(jnp.int32, (BLK, BLK), 0)
    s = lax.broadcasted_iota(jnp.int32, (BLK, BLK), 1)
    for g in range(N_HEADS):
        wc[g] = jnp.where(s <= t, ws_ref[g], 0.0).astype(BF)


def _spatial(vb, wc, bst_ref, p, low):
    xp = vb[:, p * 128:(p + 1) * 128]
    s0 = _dot(wc[2 * p], xp) + bst_ref[:, 2 * p:2 * p + 1]
    s1 = _dot(wc[2 * p + 1], xp) + bst_ref[:, 2 * p + 1:2 * p + 2]
    return xp, jnp.where(low, s0, s1)


def _gmlp_fwd(zg, lg, lb, ws, bst):
    S = zg.shape[0]
    tb = min(ATT_TB, S // BLK)
    T = tb * BLK

    def body(zg_ref, lg_ref, lb_ref, ws_ref, bst_ref, o_ref, wc):
        @pl.when(pl.program_id(0) == 0)
        def _():
            _causal_weights(ws_ref, wc)
        low = lax.broadcasted_iota(jnp.int32, (BLK, 128), 1) < HD
        for j in range(tb):
            rows = slice(j * BLK, (j + 1) * BLK)
            _, u, _, _, vln = _gmlp_parts(zg_ref[rows, :], lg_ref, lb_ref)
            vb = vln.astype(BF)
            for p in range(4):
                _, sp = _spatial(vb, wc, bst_ref, p, low)
                o_ref[rows, p * 128:(p + 1) * 128] = (u[:, p * 128:(p + 1) * 128] * sp).astype(BF)

    return pl.pallas_call(
        body, name="gmlp_fwd", grid=(S // T,),
        out_shape=_sds((S, G_W), BF),
        in_specs=[pl.BlockSpec((T, 2 * G_W), lambda i: (i, 0)), _const((1, G_W)), _const((1, G_W)),
                  _const((N_HEADS, BLK, BLK)), _const((BLK, N_HEADS))],
        out_specs=pl.BlockSpec((T, G_W), lambda i: (i, 0)),
        scratch_shapes=[pltpu.VMEM((N_HEADS, BLK, BLK), BF)],
        compiler_params=_cp(1, 32),
    )(zg, lg, lb, ws, bst)


def _gmlp_bwd(zg, d_out, lg, lb, ws, bst):
    S = zg.shape[0]
    tb = min(ATT_TB, S // BLK)
    T = tb * BLK
    nb = S // T

    def body(zg_ref, d_ref, lg_ref, lb_ref, ws_ref, bst_ref, dzg_ref, dws_ref, dbs_ref, dlg_ref, dlb_ref, wc, dbacc):
        i = pl.program_id(0)

        @pl.when(i == 0)
        def _():
            _causal_weights(ws_ref, wc)
            dws_ref[...] = jnp.zeros_like(dws_ref)
            dlg_ref[...] = jnp.zeros_like(dlg_ref)
            dlb_ref[...] = jnp.zeros_like(dlb_ref)
            dbacc[...] = jnp.zeros_like(dbacc)

        low = lax.broadcasted_iota(jnp.int32, (BLK, 128), 1) < HD
        for j in range(tb):
            rows = slice(j * BLK, (j + 1) * BLK)
            z, u, xh, rstd, vln = _gmlp_parts(zg_ref[rows, :], lg_ref, lb_ref)
            vb = vln.astype(BF)
            d = d_ref[rows, :].astype(F32)
            du_parts, dvln_parts = [], []
            for p in range(4):
                xp, sp = _spatial(vb, wc, bst_ref, p, low)
                dp = d[:, p * 128:(p + 1) * 128]
                du_parts.append(dp * sp)
                dsp = dp * u[:, p * 128:(p + 1) * 128]
                dbacc[:, p * 128:(p + 1) * 128] += dsp
                d0 = jnp.where(low, dsp, 0.0).astype(BF)
                d1 = jnp.where(low, 0.0, dsp).astype(BF)
                dws_ref[2 * p] += _dot_nt(d0, xp)
                dws_ref[2 * p + 1] += _dot_nt(d1, xp)
                dvln_parts.append(_dot_tn(wc[2 * p], d0) + _dot_tn(wc[2 * p + 1], d1))
            dvln = jnp.concatenate(dvln_parts, axis=1)
            dlg_ref[...] += _colsum(dvln * xh)
            dlb_ref[...] += _colsum(dvln)
            dxh = dvln * lg_ref[...]
            dvg = rstd * (dxh - jnp.mean(dxh, axis=-1, keepdims=True)
                          - xh * jnp.mean(dxh * xh, axis=-1, keepdims=True))
            dge = jnp.concatenate(du_parts + [dvg], axis=1)
            dzg_ref[rows, :] = (dge * _gelu_grad(z)).astype(BF)

        @pl.when(i == nb - 1)
        def _():
            t = lax.broadcasted_iota(jnp.int32, (BLK, BLK), 0)
            s = lax.broadcasted_iota(jnp.int32, (BLK, BLK), 1)
            for g in range(N_HEADS):
                dws_ref[g] = jnp.where(s <= t, dws_ref[g], 0.0)
            grp = lax.broadcasted_iota(jnp.int32, (N_HEADS, G_W), 0)
            lane = lax.broadcasted_iota(jnp.int32, (N_HEADS, G_W), 1) // HD
            pick = jnp.where(grp == lane, 1.0, 0.0).astype(F32)
            dbs_ref[...] = lax.dot_general(pick, dbacc[...], (((1,), (1,)), ((), ())),
                                           preferred_element_type=F32, precision=HIGH)

    return pl.pallas_call(
        body, name="gmlp_bwd", grid=(nb,),
        out_shape=[_sds((S, 2 * G_W), BF), _sds((N_HEADS, BLK, BLK), F32), _sds((N_HEADS, BLK), F32),
                   _sds((1, G_W), F32), _sds((1, G_W), F32)],
        in_specs=[pl.BlockSpec((T, 2 * G_W), lambda i: (i, 0)), pl.BlockSpec((T, G_W), lambda i: (i, 0)),
                  _const((1, G_W)), _const((1, G_W)), _const((N_HEADS, BLK, BLK)), _const((BLK, N_HEADS))],
        out_specs=[pl.BlockSpec((T, 2 * G_W), lambda i: (i, 0)), _const((N_HEADS, BLK, BLK)),
                   _const((N_HEADS, BLK)), _const((1, G_W)), _const((1, G_W))],
        scratch_shapes=[pltpu.VMEM((N_HEADS, BLK, BLK), BF), pltpu.VMEM((BLK, G_W), F32)],
        compiler_params=_cp(1, 32),
    )(zg, d_out, lg, lb, ws, bst)


def _mix_out(o, gm, gates, h, wa, wg, wo, gate, gp):
    S = h.shape[0]
    R = min(512, S)

    def body(o_ref, gm_ref, gates_ref, h_ref, wa_ref, wg_ref, wo_ref, gate_ref, gp_ref,
             ya_ref, yg_ref, ym_ref, y_ref, hn_ref):
        for r0 in range(0, R, CHUNK):
            rows = slice(r0, r0 + CHUNK)
            ya = _dot(o_ref[rows, :], wa_ref[...])
            yg = _dot(gm_ref[rows, :], wg_ref[...])
            ya_ref[rows, :] = ya.astype(BF)
            yg_ref[rows, :] = yg.astype(BF)
            ym = (gates_ref[rows, 0:D].astype(F32) * ya + gates_ref[rows, D:2 * D].astype(F32) * yg).astype(BF)
            ym_ref[rows, :] = ym
            y = _dot(ym, wo_ref[...])
            y_ref[rows, :] = y.astype(BF)
            hn_ref[rows, :] = h_ref[rows, :] + gate_ref[...] * (y * _rms_r(y) * gp_ref[...])

    vec = _const((1, D))
    rows = lambda w_: pl.BlockSpec((R, w_), lambda i: (i, 0))
    return pl.pallas_call(
        body, name="mix_out", grid=(S // R,),
        out_shape=[_sds((S, D), BF)] * 4 + [_sds((S, D), F32)],
        in_specs=[rows(Q_W), rows(G_W), rows(2 * D), rows(D), _resident((Q_W, D)), _resident((G_W, D)),
                  _resident((D, D)), vec, vec],
        out_specs=[rows(D)] * 5,
        compiler_params=_cp(1, 48),
    )(o, gm, gates, h, wa, wg, wo, gate, gp)


def _mix_out_bwd(dh, y, ya, yg, gates, att, gm, ymix, wa, wg, wo, gate, gp):
    S = dh.shape[0]
    R = min(512, S)
    nb = S // R

    def body(dh_ref, y_ref, ya_ref, yg_ref, gates_ref, att_ref, gm_ref, ym_ref, wa_ref, wg_ref, wo_ref,
             gate_ref, gp_ref, dz_ref, do_ref, dgm_ref, dgate_ref, dgp_ref, gwo_ref, gwa_ref, gwg_ref,
             acc_o, acc_a, acc_g, dy_scr, dya_scr, dyg_scr):
        i = pl.program_id(0)

        @pl.when(i == 0)
        def _():
            for r in (dgate_ref, dgp_ref, acc_o, acc_a, acc_g):
                r[...] = jnp.zeros_like(r)
        for r0 in range(0, R, CHUNK):
            rows = slice(r0, r0 + CHUNK)
            dy, dgate, dgp = _postnorm_bwd(dh_ref[rows, :], y_ref[rows, :], gate_ref[...], gp_ref[...], 1.0)
            dgate_ref[...] += dgate
            dgp_ref[...] += dgp
            dyb = dy.astype(BF)
            dy_scr[rows, :] = dyb
            dym = _dot_nt(dyb, wo_ref[...])
            ga = gates_ref[rows, 0:D].astype(F32)
            gg = gates_ref[rows, D:2 * D].astype(F32)
            dya = (dym * ga).astype(BF)
            dyg = (dym * gg).astype(BF)
            dya_scr[rows, :] = dya
            dyg_scr[rows, :] = dyg
            dz_ref[rows, 0:D] = (dym * ya_ref[rows, :].astype(F32) * (ga * (1.0 - ga))).astype(BF)
            dz_ref[rows, D:2 * D] = (dym * yg_ref[rows, :].astype(F32) * (gg * (1.0 - gg))).astype(BF)
            do_ref[rows, :] = _dot_nt(dya, wa_ref[...]).astype(BF)
            dgm_ref[rows, :] = _dot_nt(dyg, wg_ref[...]).astype(BF)
        for m0 in range(0, D, CHUNK):
            acc_o[m0:m0 + CHUNK, :] += _dot_tn(ym_ref[:, m0:m0 + CHUNK], dy_scr[...])
        for m0 in range(0, Q_W, CHUNK):
            acc_a[m0:m0 + CHUNK, :] += _dot_tn(att_ref[:, m0:m0 + CHUNK], dya_scr[...])
            acc_g[m0:m0 + CHUNK, :] += _dot_tn(gm_ref[:, m0:m0 + CHUNK], dyg_scr[...])

        @pl.when(i == nb - 1)
        def _():
            for m0 in range(0, D, CHUNK):
                gwo_ref[m0:m0 + CHUNK, :] = acc_o[m0:m0 + CHUNK, :].astype(BF)
            for m0 in range(0, Q_W, CHUNK):
                gwa_ref[m0:m0 + CHUNK, :] = acc_a[m0:m0 + CHUNK, :].astype(BF)
                gwg_ref[m0:m0 + CHUNK, :] = acc_g[m0:m0 + CHUNK, :].astype(BF)

    vec = _const((1, D))
    rows = lambda w_: pl.BlockSpec((R, w_), lambda i: (i, 0))
    return pl.pallas_call(
        body, name="mix_out_bwd", grid=(nb,),
        out_shape=[_sds((S, 2 * D), BF), _sds((S, Q_W), BF), _sds((S, G_W), BF), _sds((1, D), F32),
                   _sds((1, D), F32), _sds((D, D), BF), _sds((Q_W, D), BF), _sds((G_W, D), BF)],
        in_specs=[rows(D), rows(D), rows(D), rows(D), rows(2 * D), rows(Q_W), rows(G_W), rows(D),
                  _resident((Q_W, D)), _resident((G_W, D)), _resident((D, D)), vec, vec],
        out_specs=[rows(2 * D), rows(Q_W), rows(G_W), vec, vec, _const((D, D)), _const((Q_W, D)),
                   _const((G_W, D))],
        scratch_shapes=[pltpu.VMEM((D, D), F32), pltpu.VMEM((Q_W, D), F32), pltpu.VMEM((G_W, D), F32)]
        + [pltpu.VMEM((R, D), BF)] * 3,
        compiler_params=_cp(1, 60),
    )(dh, y, ya, yg, gates, att, gm, ymix, wa, wg, wo, gate, gp)


def _mix_dn(dq, dkv, dzg, dzgate, w, h, dh, sc, gp):
    S = h.shape[0]
    R = min(512, S)

    def body(dq_ref, dkv_ref, dzg_ref, dzt_ref, w_ref, h_ref, dh_ref, sc_ref, gp_ref,
             out_ref, dsh_ref, dsc_ref, dgp_ref):
        @pl.when(pl.program_id(0) == 0)
        def _():
            dsh_ref[...] = jnp.zeros_like(dsh_ref)
            dsc_ref[...] = jnp.zeros_like(dsc_ref)
            dgp_ref[...] = jnp.zeros_like(dgp_ref)
        for r0 in range(0, R, CHUNK):
            rows = slice(r0, r0 + CHUNK)
            dn = _dot(dq_ref[rows, :], w_ref[0:Q_W, :])
            dn = dn + _dot(dkv_ref[rows, :], w_ref[Q_W:QKV_W, :])
            dn = dn + _dot(dzg_ref[rows, :], w_ref[ZG_OFF:GATE_OFF, :])
            dn = dn + _dot(dzt_ref[rows, :], w_ref[GATE_OFF:IN_W, :])
            dx, dsh, dsc, dgp = _prenorm_bwd(dn, h_ref[rows, :], gp_ref[...], sc_ref[...])
            out_ref[rows, :] = dh_ref[rows, :] + dx
            dsh_ref[...] += dsh
            dsc_ref[...] += dsc
            dgp_ref[...] += dgp

    vec = _const((1, D))
    rows = lambda w_: pl.BlockSpec((R, w_), lambda i: (i, 0))
    return pl.pallas_call(
        body, name="mix_dn", grid=(S // R,),
        out_shape=[_sds((S, D), F32)] + [_sds((1, D), F32)] * 3,
        in_specs=[rows(Q_W), rows(2 * KV_W), rows(2 * G_W), rows(2 * D), _resident((IN_W, D)),
                  rows(D), rows(D), vec, vec],
        out_specs=[rows(D), vec, vec, vec],
        compiler_params=_cp(1, 48),
    )(dq, dkv, dzg, dzgate, w, h, dh, sc, gp)


def _adamw_math(w, g, m, v):
    m2 = ADAM_B1 * m + (1.0 - ADAM_B1) * g
    v2 = ADAM_B2 * v + (1.0 - ADAM_B2) * (g * g)
    m_hat = m2 / (1.0 - ADAM_B1 ** ADAM_STEP)
    v_hat = v2 / (1.0 - ADAM_B2 ** ADAM_STEP)
    delta = -ADAM_LR * (m_hat / (jnp.sqrt(v_hat) + ADAM_EPS) + ADAM_WD * w)
    return delta, m2, v2


def _row_tile(rows, cols):
    best = None
    for t in range(16, rows + 1, 16):
        if rows % t == 0 and t * cols <= 256 * 1024:
            best = t
    return best if best is not None else rows


def _adamw_sharded(landing, w, m, v, name):
    r, c = w.shape
    tr = _row_tile(r, c)

    def body(l_ref, w_ref, m_ref, v_ref, g_ref, d_ref, m2_ref, v2_ref):
        g = l_ref[0].astype(F32)
        for j in range(1, N_DEV):
            g = g + l_ref[j].astype(F32)
        delta, m2, v2 = _adamw_math(w_ref[...], g, m_ref[...], v_ref[...])
        g_ref[...] = g
        d_ref[...] = delta
        m2_ref[...] = m2
        v2_ref[...] = v2

    row = pl.BlockSpec((tr, c), lambda i: (i, 0))
    return pl.pallas_call(
        body, name=name, grid=(r // tr,),
        out_shape=[_sds((r, c), F32)] * 4,
        in_specs=[pl.BlockSpec((N_DEV, tr, c), lambda i: (0, i, 0)), row, row, row],
        out_specs=[row] * 4,
        compiler_params=_cp(1, 48),
    )(landing, w, m, v)


def _adamw_small(items):
    n = len(items)

    def body(*refs):
        for k in range(n):
            w_ref, g_ref, m_ref, v_ref = refs[4 * k:4 * k + 4]
            outs = refs[4 * n + 3 * k:4 * n + 3 * k + 3]
            for o_ref, val in zip(outs, _adamw_math(w_ref[...], g_ref[...], m_ref[...], v_ref[...])):
                o_ref[...] = val

    vm = pl.BlockSpec(memory_space=pltpu.VMEM)
    flat = pl.pallas_call(
        body, name="adamw_small",
        out_shape=[_sds(it[0].shape, F32) for it in items for _ in range(3)],
        in_specs=[vm] * (4 * n), out_specs=[vm] * (3 * n),
    )(*[a for it in items for a in it])
    return [tuple(flat[3 * k:3 * k + 3]) for k in range(n)]


def _w_ada_update(c8, d_ada, w, m, v):
    tr = 256

    def body(c_ref, d_ref, w_ref, m_ref, v_ref, g_ref, dl_ref, m2_ref, v2_ref):
        cs = c_ref[...]
        cs = cs * jax.nn.sigmoid(cs)
        g = lax.dot_general(cs, d_ref[...], (((0,), (0,)), ((), ())), preferred_element_type=F32, precision=HIGH)
        delta, m2, v2 = _adamw_math(w_ref[...], g, m_ref[...], v_ref[...])
        g_ref[...] = g
        dl_ref[...] = delta
        m2_ref[...] = m2
        v2_ref[...] = v2

    row = pl.BlockSpec((tr, ADA_W), lambda i: (i, 0))
    return pl.pallas_call(
        body, name="w_ada_update", grid=(D // tr,),
        out_shape=[_sds((D, ADA_W), F32)] * 4,
        in_specs=[pl.BlockSpec((N_DEV, tr), lambda i: (0, i)), _const((N_DEV, ADA_W)), row, row, row],
        out_specs=[row] * 4,
        compiler_params=_cp(1, 40),
    )(c8, d_ada, w, m, v)


def _t5_bucket():
    qi = np.arange(BLK, dtype=np.int32)[:, None]
    kj = np.arange(2 * BLK, dtype=np.int32)[None, :]
    dist = np.maximum(qi + BLK - kj, 0)
    max_exact = N_BUCKETS // 2
    d_f = np.maximum(dist, max_exact).astype(np.float32)
    large = max_exact + (np.log(d_f / np.float32(max_exact)) / np.float32(math.log(MAX_DISTANCE / max_exact))
                         * np.float32(N_BUCKETS - max_exact)).astype(np.int32)
    large = np.minimum(large, N_BUCKETS - 1)
    return jnp.asarray(np.where(dist < max_exact, dist, large).astype(np.int32))


def _slabs_of_columns(w):
    r, c8 = w.shape
    return jnp.transpose(w.reshape(r, N_DEV, c8 // N_DEV), (1, 0, 2))


def _columns_of_slabs(w8):
    _, r, c = w8.shape
    return jnp.transpose(w8, (1, 0, 2)).reshape(r, N_DEV * c)


def kernel(x, c, rel_bias, w_ada, b_ada, pre_norm_g, post_norm_g, w_ffn1_in, w_ffn1_out, w_in, sinks, gmlp_ln_g, gmlp_ln_b, gmlp_w_s, gmlp_b_s, w_br_attn, w_br_gmlp, w_out, w_ffn2_in, w_ffn2_out, loss_target, m_rel_bias, m_w_ada, m_b_ada, m_pre_norm_g, m_post_norm_g, m_w_ffn1_in, m_w_ffn1_out, m_w_in, m_sinks, m_gmlp_ln_g, m_gmlp_ln_b, m_gmlp_w_s, m_gmlp_b_s, m_w_br_attn, m_w_br_gmlp, m_w_out, m_w_ffn2_in, m_w_ffn2_out, v_rel_bias, v_w_ada, v_b_ada, v_pre_norm_g, v_post_norm_g, v_w_ffn1_in, v_w_ffn1_out, v_w_in, v_sinks, v_gmlp_ln_g, v_gmlp_ln_b, v_gmlp_w_s, v_gmlp_b_s, v_w_br_attn, v_w_br_gmlp, v_w_out, v_w_ffn2_in, v_w_ffn2_out):
    me = 4 * lax.axis_index("x") + 2 * lax.axis_index("y") + lax.axis_index("c")
    x0 = x[0]
    target = loss_target[0]

    transposed = ("w_ffn1_in", "w_in", "w_ffn2_in")
    shards = [w_ffn1_in[0].T, w_ffn1_out[0], w_in[0].T, w_br_attn[0], w_br_gmlp[0], w_out[0],
              w_ffn2_in[0].T, w_ffn2_out[0]]
    shards_bf = [s.astype(BF) for s in shards]
    groups = [shards_bf[0:1], shards_bf[1:6], shards_bf[6:8]]

    def gather_start(i, after):
        return _slabs_start("gather", groups[i], after, "gather_start_%d" % i)

    def forward_start(st, i, after):
        lands = _slabs_wait("gather", len(groups[i]), st, after, "gather_wait_%d" % i)
        return _slabs_start("forward", lands, c, "forward_start_%d" % i)

    def gathered(st, i, after):
        return _slabs_wait("forward", len(groups[i]), st, after, "forward_wait_%d" % i)

    gs0 = gather_start(0, c)

    mine = jnp.concatenate([c[0], pre_norm_g[0].reshape(-1), post_norm_g[0].reshape(-1)])
    small8 = jnp.broadcast_to(mine[None, :], (8, mine.shape[0]))
    b_ada64 = jnp.repeat(b_ada.reshape(N_DEV, ADA_W), 8, axis=0)
    gath, ada64 = _ada_forward(small8, w_ada[0], b_ada64)
    gath8 = gath[::8]
    ada = ada64[::8].reshape(9, D)
    sh1, sc1, g1, sh2, sc2, g2, sh3, sc3, g3 = [ada[k:k + 1] for k in range(9)]
    gains = gath8[:, D:].reshape(N_DEV, 2, 3, 128)
    pre_g = jnp.transpose(gains[:, 0], (1, 0, 2)).reshape(3, D)
    post_g = jnp.transpose(gains[:, 1], (1, 0, 2)).reshape(3, D)
    pre = [pre_g[k:k + 1] for k in range(3)]
    post = [post_g[k:k + 1] for k in range(3)]

    bucket = _t5_bucket()
    bias = _bias_table(rel_bias, bucket)
    sinks8 = sinks[0]
    lg, lb = gmlp_ln_g, gmlp_ln_b
    ws = gmlp_w_s[0]
    bst = jnp.transpose(gmlp_b_s[0])

    fs0 = forward_start(gs0, 0, sh1)
    gs1 = gather_start(1, fs0[-1])
    wf1_in = gathered(fs0, 0, gs1[-1])[0].reshape(2 * D_FF, D)
    n1, fg1, fu1, fa1 = _ffn_in(x0, sh1, sc1, pre[0], wf1_in, "ffn1_in")
    fs1 = forward_start(gs1, 1, n1)
    gs2 = gather_start(2, fs1[-1])
    mix_w = gathered(fs1, 1, gs2[-1])
    wf1_out = mix_w[0].reshape(D_FF, D)
    w_in_full = mix_w[1].reshape(IN_W, D)
    w_bra = _columns_of_slabs(mix_w[2])
    w_brg = _columns_of_slabs(mix_w[3])
    w_out_full = mix_w[4].reshape(D, D)
    h1, y1 = _ffn_out(fa1, wf1_out, x0, g1, post[0], "ffn1_out")
    n2, qkv, zg, gates = _mix_in(h1, sh2, sc2, pre[1], w_in_full)
    att = _attn_fwd(qkv, bias, sinks8)
    gm = _gmlp_fwd(zg, lg, lb, ws, bst)
    fs2 = forward_start(gs2, 2, gm)
    ya, yg, ymix, y2, h2 = _mix_out(att, gm, gates, h1, w_bra, w_brg, w_out_full, g2 + fs2[-1], post[1])
    wf2_in, wf2_out = gathered(fs2, 2, h2)
    wf2_in = wf2_in.reshape(2 * D_FF, D)
    wf2_out = wf2_out.reshape(D_FF, D)
    n3, fg3, fu3, fa3 = _ffn_in(h2, sh3, sc3, pre[2], wf2_in, "ffn2_in")
    dh3, y3, sq = _ffn_out(fa3, wf2_out, h2, g3, post[2], "ffn2_out", target=target)

    def exchange_start(i, arrays):
        return _slabs_start("exchange", arrays, sq, "exchange_start_%d" % i)

    dy3, dgu3, dh2, d_g3, d_post2, d_sh3, d_sc3, d_pre2 = _ffn_bwd(
        dh3, y3, fg3, fu3, wf2_out, wf2_in, h2, g3, post[2], sc3, pre[2], "ffn2_bwd")
    gw_f2_out = _tn_matmul(fa3, dy3, "ffn2_out_wgrad", tm=D_FF // 2).reshape(N_DEV, D_FF // N_DEV, D)
    gw_f2_in = _tn_matmul(dgu3, n3, "ffn2_in_wgrad", tm=D_FF // 2).reshape(N_DEV, FS, D)
    ex1 = exchange_start(1, [gw_f2_out, gw_f2_in])

    dzgate, d_att, d_gm, d_g2, d_post1, gw_out, gw_bra, gw_brg = _mix_out_bwd(
        dh2, y2, ya, yg, gates, att, gm, ymix, w_bra, w_brg, w_out_full, g2 + ex1[-1], post[1])
    ex2 = exchange_start(2, [_slabs_of_columns(gw_bra), _slabs_of_columns(gw_brg),
                             gw_out.reshape(N_DEV, D // N_DEV, D)])
    dq, dkv, dbias, dsink = _attn_bwd(qkv, bias, sinks8, d_att)
    dzg, d_ws, d_bs, d_lg, d_lb = _gmlp_bwd(zg, d_gm, lg, lb, ws, bst)
    d_rel = _rel_bias_grad(dbias, bucket)
    early = jnp.concatenate([
        jnp.concatenate([d_lg.reshape(4, 128), d_lb.reshape(4, 128)], axis=0),
        d_bs, d_rel, dsink, d_ws.reshape(N_HEADS * BLK, BLK)], axis=0)
    sm0 = _slabs_start("gather_all", [early], sq, "small_gather_start")
    dh1, d_sh2, d_sc2, d_pre1 = _mix_dn(dq, dkv, dzg, dzgate, w_in_full, h1, dh2, sc2 + ex2[-1] + sm0[-1], pre[1])
    gw_in = jnp.concatenate(
        [_tn_matmul(dq, n2, "w_in_q_wgrad"), _tn_matmul(dkv, n2, "w_in_kv_wgrad"),
         _tn_matmul(dzg, n2, "w_in_zg_wgrad"), _tn_matmul(dzgate, n2, "w_in_gate_wgrad")],
        axis=0).reshape(N_DEV, IN_W // N_DEV, D)
    ex3 = exchange_start(3, [gw_in])

    dy1, dgu1, d_g1, d_post0 = _ffn_out_bwd(dh1, y1, fg1, fu1, wf1_out, g1 + ex3[-1], post[0], "ffn1_out_bwd")
    gw_f1_out = _tn_matmul(fa1, dy1, "ffn1_out_wgrad", tm=D_FF // 2).reshape(N_DEV, D_FF // N_DEV, D)
    ex4 = exchange_start(4, [gw_f1_out])
    gw_f1_in = _tn_matmul(dgu1, n1, "ffn1_in_wgrad", tm=D_FF // 2).reshape(N_DEV, FS, D)
    ex5 = exchange_start(5, [gw_f1_in])
    grad_x, d_sh1, d_sc1, d_pre0 = _ffn_dn(dgu1, wf1_in, x0, dh1, sc1 + ex4[-1] + ex5[-1], pre[0], "ffn1_dn")

    landed = {}
    for i, (ex, nms) in enumerate([(ex1, ["w_ffn2_out", "w_ffn2_in"]),
                                   (ex2, ["w_br_attn", "w_br_gmlp", "w_out"]), (ex3, ["w_in"]),
                                   (ex4, ["w_ffn1_out"]), (ex5, ["w_ffn1_in"])]):
        for nm, land in zip(nms, _slabs_wait("exchange", len(nms), ex, grad_x, "exchange_wait_%d" % i)):
            landed[nm] = land
    moments = [(m_w_ffn1_in, v_w_ffn1_in), (m_w_ffn1_out, v_w_ffn1_out), (m_w_in, v_w_in),
               (m_w_br_attn, v_w_br_attn), (m_w_br_gmlp, v_w_br_gmlp), (m_w_out, v_w_out),
               (m_w_ffn2_in, v_w_ffn2_in), (m_w_ffn2_out, v_w_ffn2_out)]
    names = ["w_ffn1_in", "w_ffn1_out", "w_in", "w_br_attn", "w_br_gmlp", "w_out", "w_ffn2_in", "w_ffn2_out"]
    big = {}
    for nm, w_, (m_, v_) in zip(names, shards, moments):
        if nm in transposed:
            res4 = _adamw_sharded(landed[nm], w_, m_[0].T, v_[0].T, "adamw_" + nm)
            big[nm] = [a.T[None] for a in res4]
        else:
            big[nm] = [a[None] for a in _adamw_sharded(landed[nm], w_, m_[0], v_[0], "adamw_" + nm)]

    d_ada = jnp.concatenate([v_.reshape(8, 128) for v_ in
                             (d_sh1, d_sc1, d_g1, d_sh2, d_sc2, d_g2, d_sh3, d_sc3, d_g3)], axis=0)
    d_pre = jnp.concatenate([d_pre0, d_pre1, d_pre2], axis=0)
    d_post = jnp.concatenate([d_post0, d_post1, d_post2], axis=0)
    late = jnp.concatenate([d_ada, _slabs_of_columns(d_pre).reshape(24, 128),
                            _slabs_of_columns(d_post).reshape(24, 128),
                            jnp.broadcast_to(sq * (0.5 / D), (8, 128))], axis=0)
    late, _ = lax.optimization_barrier((late, landed["w_ffn1_in"]))
    tot, every = _small_allreduce(late)
    (early_land,) = _slabs_wait("gather_all", 1, sm0, grad_x, "small_gather_wait")
    tot_early = _sum_slabs(early_land)

    loss = tot[120, 0]
    g_b_ada = tot[0:72].reshape(1, 9 * D)
    g_pre = lax.dynamic_slice_in_dim(tot[72:96], 3 * me, 3, axis=0)[None]
    g_post = lax.dynamic_slice_in_dim(tot[96:120], 3 * me, 3, axis=0)[None]
    g_lg = tot_early[0:4].reshape(1, G_W)
    g_lb = tot_early[4:8].reshape(1, G_W)
    g_bs = tot_early[8:16][None]
    g_rel = jnp.transpose(tot_early[16:24, 0:N_BUCKETS])
    g_sinks = tot_early[24:32, 0][None]
    g_ws = tot_early[32:1056].reshape(1, N_HEADS, BLK, BLK)

    d_ada_mine = lax.dynamic_slice_in_dim(every[:, 0:72].reshape(N_DEV, N_DEV, ADA_W), me, 1, axis=1)[:, 0]
    ada_out = [a[None] for a in _w_ada_update(gath8[:, 0:D], d_ada_mine, w_ada[0], m_w_ada[0], v_w_ada[0])]

    small = [("rel_bias", rel_bias, g_rel, m_rel_bias, v_rel_bias), ("b_ada", b_ada, g_b_ada, m_b_ada, v_b_ada),
             ("pre_norm_g", pre_norm_g, g_pre, m_pre_norm_g, v_pre_norm_g),
             ("post_norm_g", post_norm_g, g_post, m_post_norm_g, v_post_norm_g),
             ("sinks", sinks, g_sinks, m_sinks, v_sinks), ("gmlp_ln_g", gmlp_ln_g, g_lg, m_gmlp_ln_g, v_gmlp_ln_g),
             ("gmlp_ln_b", gmlp_ln_b, g_lb, m_gmlp_ln_b, v_gmlp_ln_b),
             ("gmlp_w_s", gmlp_w_s, g_ws, m_gmlp_w_s, v_gmlp_w_s), ("gmlp_b_s", gmlp_b_s, g_bs, m_gmlp_b_s, v_gmlp_b_s)]
    two_d = lambda a: a.reshape(int(math.prod(a.shape[:-1])), a.shape[-1])
    stepped = _adamw_small([tuple(two_d(a) for a in item[1:]) for item in small])
    res = {"w_ada": ada_out}
    for (nm, w_, g_, _, _), new in zip(small, stepped):
        res[nm] = [g_] + [a.reshape(w_.shape) for a in new]
    res.update(big)
    order = ["rel_bias", "w_ada", "b_ada", "pre_norm_g", "post_norm_g", "w_ffn1_in", "w_ffn1_out", "w_in", "sinks",
             "gmlp_ln_g", "gmlp_ln_b", "gmlp_w_s", "gmlp_b_s", "w_br_attn", "w_br_gmlp", "w_out", "w_ffn2_in",
             "w_ffn2_out"]
    outs = [loss, grad_x[None]]
    for k in range(4):
        outs += [res[nm][k] for nm in order]
    return tuple(outs)
```

```python
import functools
import math

import jax
import jax.numpy as jnp
import numpy as np
from jax import lax
from jax.experimental import pallas as pl
from jax.experimental.pallas import tpu as pltpu

F32 = jnp.float32
BF = jnp.bfloat16

N_DEV = 8
D = 1024
D_FF = 2816
FS = D_FF // 4
N_HEADS = 8
N_KV = 2
GROUP = 4
HD = 64
BLK = 128
Q_W = 512
KV_W = 128
G_W = 512
QKV_W = Q_W + 2 * KV_W
ZG_OFF = QKV_W
GATE_OFF = ZG_OFF + 2 * G_W
IN_W = GATE_OFF + 2 * D
N_BUCKETS = 32
MAX_DISTANCE = 128
EPS = 1e-6
NEG = -1e30
SCALE = HD ** -0.5
ADA_W = 9 * D // N_DEV

ADAM_LR = 0.001
ADAM_B1 = 0.9
ADAM_B2 = 0.999
ADAM_EPS = 1e-08
ADAM_WD = 0.01
ADAM_STEP = 10

CHUNK = 256
MIB = 1024 * 1024
MESH = pl.DeviceIdType.MESH
HIGH = lax.Precision.HIGHEST


def _cp(n_grid, vmem_mib):
    return pltpu.CompilerParams(dimension_semantics=("arbitrary",) * n_grid,
                                vmem_limit_bytes=vmem_mib * MIB)


def _const(shape):
    return pl.BlockSpec(shape, lambda *_: (0,) * len(shape))


def _resident(shape):
    return pl.BlockSpec(shape, lambda *_: (0,) * len(shape), pipeline_mode=pl.Buffered(1))


def _sds(shape, dtype):
    return jax.ShapeDtypeStruct(shape, dtype)


def _dot(a, b):
    return jnp.dot(a, b, preferred_element_type=F32)


def _dot_nt(a, b):
    return lax.dot_general(a, b, (((1,), (1,)), ((), ())), preferred_element_type=F32)


def _dot_tn(a, b):
    return lax.dot_general(a, b, (((0,), (0,)), ((), ())), preferred_element_type=F32)


def _rms_r(x):
    return lax.rsqrt(jnp.mean(x * x, axis=-1, keepdims=True) + EPS)


def _colsum(x):
    return jnp.sum(x, axis=0, keepdims=True)


def _prenorm(x, gp, sc, sh):
    return (x * _rms_r(x) * gp) * (1.0 + sc) + sh


def _prenorm_bwd(dn, x, gp, sc):
    r = _rms_r(x)
    xh = x * r
    t = dn * (1.0 + sc) * gp
    dx = r * (t - xh * jnp.mean(t * xh, axis=-1, keepdims=True))
    return dx, _colsum(dn), _colsum(dn * xh * gp), _colsum(dn * (1.0 + sc) * xh)


def _postnorm_bwd(dh, y, gate, gp, res):
    y = y.astype(F32)
    r = _rms_r(y)
    yh = y * r
    dyn = (res * gate) * dh
    t = dyn * gp
    dy = r * (t - yh * jnp.mean(t * yh, axis=-1, keepdims=True))
    return dy, _colsum(res * dh * yh * gp), _colsum(dyn * yh)


def _gelu(x):
    k = math.sqrt(2.0 / math.pi)
    return 0.5 * x * (1.0 + jnp.tanh(k * (x + 0.044715 * x * x * x)))


def _gelu_grad(x):
    k = math.sqrt(2.0 / math.pi)
    t = jnp.tanh(k * (x + 0.044715 * x * x * x))
    return 0.5 * (1.0 + t) + 0.5 * x * (1.0 - t * t) * (k * (1.0 + 3.0 * 0.044715 * x * x))


def _my_place():
    x, y, c = lax.axis_index("x"), lax.axis_index("y"), lax.axis_index("c")
    return x, y, c, 4 * x + 2 * y + c


def _peer(x, y, c, k):
    px = 1 - x if k & 4 else x
    py = 1 - y if k & 2 else y
    pc = 1 - c if k & 1 else c
    return (px, py, pc), 4 * px + 2 * py + pc


HBM_SPEC = pl.BlockSpec(memory_space=pltpu.HBM)
SEM_SPEC = pl.BlockSpec(memory_space=pltpu.SEMAPHORE)
EFFECT = pltpu.SideEffectType.DATAFLOW_SIDE_EFFECTING


RELATIONS = {"exchange": (1, 2, 3, 4, 5, 6, 7), "gather": (1, 2, 4, 6), "forward": (2, 4, 6),
             "gather_all": (1, 2, 3, 4, 5, 6, 7)}


def _slab_copies(mode, srcs, lands, send, recv, loc):
    x, y, c, me = _my_place()
    rel = RELATIONS[mode]
    remote, local = [], []
    for t in range(len(lands)):
        for i, k in enumerate(rel):
            peer, peer_lin = _peer(x, y, c, k)
            if mode == "exchange":
                src, dst, to = srcs[t].at[peer_lin], lands[t].at[me], peer
            elif mode in ("gather", "gather_all"):
                src, dst, to = srcs[t], lands[t].at[me], peer
            else:
                src, dst, to = lands[t].at[peer_lin], lands[t].at[peer_lin], _peer(x, y, c, 1)[0]
            remote.append(pltpu.make_async_remote_copy(
                src_ref=src, dst_ref=dst, send_sem=send.at[t * len(rel) + i], recv_sem=recv.at[t * len(rel) + i],
                device_id=to, device_id_type=MESH))
        if mode == "exchange":
            local.append(pltpu.make_async_copy(srcs[t].at[me], lands[t].at[me], loc.at[t]))
        elif mode in ("gather", "gather_all"):
            local.append(pltpu.make_async_copy(srcs[t], lands[t].at[me], loc.at[t]))
    return remote, local


def _slabs_start(mode, arrays, after, name):
    n = len(arrays)
    if mode == "forward":
        thru = list(arrays)
    else:
        shapes = [a.shape if mode == "exchange" else (N_DEV,) + a.shape for a in arrays]
        thru = list(arrays) + [lax.empty(s, a.dtype) for s, a in zip(shapes, arrays)]
    m = len(thru)
    n_sem = n * len(RELATIONS[mode])

    def body(*refs):
        srcs, lands = refs[:n], refs[m - n:m]
        send, recv, loc = refs[m + 1:m + 4]
        remote, local = _slab_copies(mode, srcs, lands, send, recv, loc)
        for cp in remote + local:
            cp.start()
        refs[-1][...] = jnp.zeros_like(refs[-1])

    return pl.pallas_call(
        body, name=name,
        out_shape=(pltpu.SemaphoreType.DMA((n_sem,)), pltpu.SemaphoreType.DMA((n_sem,)),
                   pltpu.SemaphoreType.DMA((n,)),
                   *[pltpu.HBM(a.shape, a.dtype) for a in thru],
                   _sds((1, D), F32)),
        in_specs=[HBM_SPEC] * m + [pl.BlockSpec(memory_space=pl.ANY)],
        out_specs=(SEM_SPEC, SEM_SPEC, SEM_SPEC, *[HBM_SPEC] * m, pl.BlockSpec(memory_space=pltpu.VMEM)),
        input_output_aliases={t: 3 + t for t in range(m)},
        compiler_params=pltpu.CompilerParams(has_side_effects=EFFECT),
    )(*[pltpu.with_memory_space_constraint(a, pltpu.HBM) for a in thru], after)


def _slabs_wait(mode, n, started, after, name):
    sems = started[0:3]
    thru = started[3:-1]
    m = len(thru)

    def body(*refs):
        srcs, lands = refs[:n], refs[m - n:m]
        remote, local = _slab_copies(mode, srcs, lands, *refs[m:m + 3])
        for cp in remote:
            cp.wait_send()
            cp.wait_recv()
        for cp in local:
            cp.wait()

    res = pl.pallas_call(
        body, name=name,
        out_shape=tuple(pltpu.HBM(a.shape, a.dtype) for a in thru),
        in_specs=[HBM_SPEC] * m + [SEM_SPEC] * 3 + [pl.BlockSpec(memory_space=pl.ANY)],
        out_specs=tuple([HBM_SPEC] * m),
        input_output_aliases={t: t for t in range(m)},
        compiler_params=pltpu.CompilerParams(has_side_effects=EFFECT),
    )(*thru, *sems, after)
    return list(res[m - n:m])


def _ada_forward(small8, w_ada, b_ada64):
    sw = small8.shape[1]

    def body(sm_ref, w_ref, b_ref, gath_ref, ada_ref, part_ref, send1, recv1, send2, recv2):
        x, y, c, me = _my_place()
        row_me = pl.multiple_of(me * 8, 8)
        gath_ref[pl.ds(row_me, 8), :] = sm_ref[...]
        first = []
        for k in range(1, N_DEV):
            peer, _ = _peer(x, y, c, k)
            cp = pltpu.make_async_remote_copy(
                src_ref=sm_ref, dst_ref=gath_ref.at[pl.ds(row_me, 8), :], send_sem=send1.at[k - 1],
                recv_sem=recv1.at[k - 1], device_id=peer, device_id_type=MESH)
            cp.start()
            first.append(cp)
        for cp in first:
            cp.wait()
        cs = gath_ref[:, 0:D]
        cs = cs * jax.nn.sigmoid(cs)
        part_ref[...] = jnp.dot(cs, w_ref[...], preferred_element_type=F32, precision=HIGH)
        ada_ref[pl.ds(row_me, 8), :] = part_ref[pl.ds(row_me, 8), :]
        second = []
        for k in range(1, N_DEV):
            peer, peer_lin = _peer(x, y, c, k)
            cp = pltpu.make_async_remote_copy(
                src_ref=part_ref.at[pl.ds(pl.multiple_of(peer_lin * 8, 8), 8), :],
                dst_ref=ada_ref.at[pl.ds(row_me, 8), :], send_sem=send2.at[k - 1],
                recv_sem=recv2.at[k - 1], device_id=peer, device_id_type=MESH)
            cp.start()
            second.append(cp)
        for cp in second:
            cp.wait()
        ada_ref[...] = ada_ref[...] + b_ref[...]

    vm = pl.BlockSpec(memory_space=pltpu.VMEM)
    return pl.pallas_call(
        body, name="ada_forward",
        out_shape=[_sds((8 * N_DEV, sw), F32), _sds((8 * N_DEV, ADA_W), F32)],
        in_specs=[vm, vm, vm], out_specs=[vm, vm],
        scratch_shapes=[pltpu.VMEM((8 * N_DEV, ADA_W), F32)] + [pltpu.SemaphoreType.DMA((7,))] * 4,
        compiler_params=pltpu.CompilerParams(vmem_limit_bytes=32 * MIB),
    )(small8, w_ada, b_ada64)


def _sum_slabs(land):
    def body(l_ref, o_ref):
        acc = l_ref[0]
        for j in range(1, N_DEV):
            acc = acc + l_ref[j]
        o_ref[...] = acc

    vm = pl.BlockSpec(memory_space=pltpu.VMEM)
    return pl.pallas_call(body, name="sum_slabs", out_shape=_sds(land.shape[1:], F32), in_specs=[vm], out_specs=vm,
                          compiler_params=pltpu.CompilerParams(vmem_limit_bytes=32 * MIB))(land)


def _small_allreduce(pack):
    rows = pack.shape[0]

    def body(p_ref, sum_ref, gath_ref, send, recv):
        x, y, c, me = _my_place()
        gath_ref[me] = p_ref[...]
        cps = []
        for k in range(1, N_DEV):
            peer, _ = _peer(x, y, c, k)
            cp = pltpu.make_async_remote_copy(
                src_ref=p_ref, dst_ref=gath_ref.at[me], send_sem=send.at[k - 1],
                recv_sem=recv.at[k - 1], device_id=peer, device_id_type=MESH)
            cp.start()
            cps.append(cp)
        for cp in cps:
            cp.wait()
        acc = gath_ref[0]
        for j in range(1, N_DEV):
            acc = acc + gath_ref[j]
        sum_ref[...] = acc

    vm = pl.BlockSpec(memory_space=pltpu.VMEM)
    return pl.pallas_call(
        body, name="small_allreduce",
        out_shape=[_sds((rows, 128), F32), _sds((N_DEV, rows, 128), F32)],
        in_specs=[vm], out_specs=[vm, vm],
        scratch_shapes=[pltpu.SemaphoreType.DMA((7,)), pltpu.SemaphoreType.DMA((7,))],
        compiler_params=pltpu.CompilerParams(vmem_limit_bytes=40 * MIB),
    )(pack)


F_TILES = tuple((f0, min(512, D_FF - f0)) for f0 in range(0, D_FF, 512))
F_TILES_NARROW = tuple((f0, 256) for f0 in range(0, D_FF, 256))


def _swiglu_tile(n, wt_ref, f0, tf):
    g = _dot_nt(n, wt_ref[f0:f0 + tf, :])
    u = _dot_nt(n, wt_ref[D_FF + f0:D_FF + f0 + tf, :])
    sg = jax.nn.sigmoid(g)
    silu = g * sg
    return (u * (sg * (1.0 + g * (1.0 - sg)))).astype(BF), silu.astype(BF), (silu * u).astype(BF)


def _ffn_in(h, sh, sc, gp, wt, name):
    S = h.shape[0]
    R = min(512, S)

    def body(h_ref, sh_ref, sc_ref, gp_ref, w_ref, n_ref, dg_ref, sl_ref, a_ref):
        for r0 in range(0, R, CHUNK):
            rows = slice(r0, r0 + CHUNK)
            n = _prenorm(h_ref[rows, :], gp_ref[...], sc_ref[...], sh_ref[...]).astype(BF)
            n_ref[rows, :] = n
            for f0, tf in F_TILES_NARROW:
                dg_ref[rows, f0:f0 + tf], sl_ref[rows, f0:f0 + tf], a_ref[rows, f0:f0 + tf] = _swiglu_tile(
                    n, w_ref, f0, tf)

    vec = _const((1, D))
    rows_ = lambda w_: pl.BlockSpec((R, w_), lambda i: (i, 0))
    return pl.pallas_call(
        body, name=name, grid=(S // R,),
        out_shape=[_sds((S, D), BF)] + [_sds((S, D_FF), BF)] * 3,
        in_specs=[rows_(D), vec, vec, vec, _resident((2 * D_FF, D))],
        out_specs=[rows_(D), rows_(D_FF), rows_(D_FF), rows_(D_FF)],
        compiler_params=_cp(1, 56),
    )(h, sh, sc, gp, wt)


def _ffn_out(a, w, h, gate, gp, name, target=None):
    S = h.shape[0]
    R = min(512, S)
    with_loss = target is not None

    def body(a_ref, w_ref, h_ref, gate_ref, gp_ref, *rest):
        if with_loss:
            t_ref, out_ref, y_ref, tot_ref = rest

            @pl.when(pl.program_id(0) == 0)
            def _():
                tot_ref[...] = jnp.zeros_like(tot_ref)
        else:
            out_ref, y_ref = rest
        for r0 in range(0, R, CHUNK):
            rows = slice(r0, r0 + CHUNK)
            y = _dot(a_ref[rows, :], w_ref[...])
            y_ref[rows, :] = y.astype(BF)
            hn = h_ref[rows, :] + (0.5 * gate_ref[...]) * (y * _rms_r(y) * gp_ref[...])
            if with_loss:
                e = hn - t_ref[rows, :]
                out_ref[rows, :] = e * (1.0 / D)
                tot_ref[...] += jnp.sum(jnp.sum(e * e, axis=1, keepdims=True), axis=0, keepdims=True)
            else:
                out_ref[rows, :] = hn

    vec = _const((1, D))
    rows_ = lambda w_: pl.BlockSpec((R, w_), lambda i: (i, 0))
    return pl.pallas_call(
        body, name=name, grid=(S // R,),
        out_shape=[_sds((S, D), F32), _sds((S, D), BF)] + ([_sds((1, 1), F32)] if with_loss else []),
        in_specs=[rows_(D_FF), _resident((D_FF, D)), rows_(D), vec, vec] + ([rows_(D)] if with_loss else []),
        out_specs=[rows_(D), rows_(D)] + ([_const((1, 1))] if with_loss else []),
        compiler_params=_cp(1, 48),
    )(*((a, w, h, gate, gp) + ((target,) if with_loss else ())))


def _ffn_out_bwd(dh, y, dsilu_u, silu, w, gate, gp, name):
    S = dh.shape[0]
    R = min(512, S)

    def body(dh_ref, y_ref, g_ref, u_ref, w_ref, gate_ref, gp_ref, dy_ref, dgu_ref, dgate_ref, dgp_ref):
        @pl.when(pl.program_id(0) == 0)
        def _():
            dgate_ref[...] = jnp.zeros_like(dgate_ref)
            dgp_ref[...] = jnp.zeros_like(dgp_ref)
        for r0 in range(0, R, CHUNK):
            rows = slice(r0, r0 + CHUNK)
            dy, dgate, dgp = _postnorm_bwd(dh_ref[rows, :], y_ref[rows, :], gate_ref[...], gp_ref[...], 0.5)
            dgate_ref[...] += dgate
            dgp_ref[...] += dgp
            dyb = dy.astype(BF)
            dy_ref[rows, :] = dyb
            for f0, tf in F_TILES:
                da = _dot_nt(dyb, w_ref[f0:f0 + tf, :])
                dgu_ref[rows, f0:f0 + tf] = (da * g_ref[rows, f0:f0 + tf].astype(F32)).astype(BF)
                dgu_ref[rows, D_FF + f0:D_FF + f0 + tf] = (da * u_ref[rows, f0:f0 + tf].astype(F32)).astype(BF)

    vec = _const((1, D))
    rows_ = lambda w_: pl.BlockSpec((R, w_), lambda i: (i, 0))
    return pl.pallas_call(
        body, name=name, grid=(S // R,),
        out_shape=[_sds((S, D), BF), _sds((S, 2 * D_FF), BF), _sds((1, D), F32), _sds((1, D), F32)],
        in_specs=[rows_(D), rows_(D), rows_(D_FF), rows_(D_FF), _resident((D_FF, D)), vec, vec],
        out_specs=[rows_(D), rows_(2 * D_FF), vec, vec],
        compiler_params=_cp(1, 56),
    )(dh, y, dsilu_u, silu, w, gate, gp)


def _ffn_dn(dgu, wt, h, dh, sc, gp, name):
    S = h.shape[0]
    R = min(512, S)

    def body(dgu_ref, w_ref, h_ref, dh_ref, sc_ref, gp_ref, out_ref, dsh_ref, dsc_ref, dgp_ref):
        @pl.when(pl.program_id(0) == 0)
        def _():
            dsh_ref[...] = jnp.zeros_like(dsh_ref)
            dsc_ref[...] = jnp.zeros_like(dsc_ref)
            dgp_ref[...] = jnp.zeros_like(dgp_ref)

        for r0 in range(0, R, CHUNK):
            rows = slice(r0, r0 + CHUNK)
            dn = _dot(dgu_ref[rows, :], w_ref[...])
            dx, dsh, dsc, dgp = _prenorm_bwd(dn, h_ref[rows, :], gp_ref[...], sc_ref[...])
            out_ref[rows, :] = dh_ref[rows, :] + dx
            dsh_ref[...] += dsh
            dsc_ref[...] += dsc
            dgp_ref[...] += dgp

    vec = _const((1, D))
    rows_ = lambda w_: pl.BlockSpec((R, w_), lambda i: (i, 0))
    return pl.pallas_call(
        body, name=name, grid=(S // R,),
        out_shape=[_sds((S, D), F32)] + [_sds((1, D), F32)] * 3,
        in_specs=[rows_(2 * D_FF), _resident((2 * D_FF, D)), rows_(D), rows_(D), vec, vec],
        out_specs=[rows_(D), vec, vec, vec],
        compiler_params=_cp(1, 56),
    )(dgu, wt, h, dh, sc, gp)


def _ffn_bwd(dh, y, dsilu_u, silu, w, wt, h, gate, gpost, sc, gpre, name):
    S = dh.shape[0]
    R = min(256, S)

    def body(dh_ref, y_ref, g_ref, u_ref, w_ref, wt_ref, h_ref, gate_ref, gpost_ref, sc_ref, gpre_ref,
             dy_ref, dgu_ref, out_ref, dgate_ref, dgpost_ref, dsh_ref, dsc_ref, dgpre_ref):
        @pl.when(pl.program_id(0) == 0)
        def _():
            for r in (dgate_ref, dgpost_ref, dsh_ref, dsc_ref, dgpre_ref):
                r[...] = jnp.zeros_like(r)
        dhh = dh_ref[...]
        dy, dgate, dgpost = _postnorm_bwd(dhh, y_ref[...], gate_ref[...], gpost_ref[...], 0.5)
        dgate_ref[...] += dgate
        dgpost_ref[...] += dgpost
        dyb = dy.astype(BF)
        dy_ref[...] = dyb
        dn = None
        for f0, tf in F_TILES:
            da = _dot_nt(dyb, w_ref[f0:f0 + tf, :])
            dg = (da * g_ref[:, f0:f0 + tf].astype(F32)).astype(BF)
            du = (da * u_ref[:, f0:f0 + tf].astype(F32)).astype(BF)
            dgu_ref[:, f0:f0 + tf] = dg
            dgu_ref[:, D_FF + f0:D_FF + f0 + tf] = du
            part = _dot(dg, wt_ref[f0:f0 + tf, :]) + _dot(du, wt_ref[D_FF + f0:D_FF + f0 + tf, :])
            dn = part if dn is None else dn + part
        dx, dsh, dsc, dgpre = _prenorm_bwd(dn, h_ref[...], gpre_ref[...], sc_ref[...])
        out_ref[...] = dhh + dx
        dsh_ref[...] += dsh
        dsc_ref[...] += dsc
        dgpre_ref[...] += dgpre

    vec = _const((1, D))
    rows_ = lambda w_: pl.BlockSpec((R, w_), lambda i: (i, 0))
    return pl.pallas_call(
        body, name=name, grid=(S // R,),
        out_shape=[_sds((S, D), BF), _sds((S, 2 * D_FF), BF), _sds((S, D), F32)] + [_sds((1, D), F32)] * 5,
        in_specs=[rows_(D), rows_(D), rows_(D_FF), rows_(D_FF), _resident((D_FF, D)), _resident((2 * D_FF, D)),
                  rows_(D), vec, vec, vec, vec],
        out_specs=[rows_(D), rows_(2 * D_FF), rows_(D)] + [vec] * 5,
        compiler_params=_cp(1, 56),
    )(dh, y, dsilu_u, silu, w, wt, h, gate, gpost, sc, gpre)


def _tn_matmul(a, b, name, tm=None):
    S, M_all = a.shape
    N = b.shape[1]
    M = M_all if tm is None else tm
    GA = M_all // M
    ts = min(2048 if M * N <= 2 * D * D else 1024, S)
    nk = S // ts
    chunks = [(m0, min(CHUNK, M - m0)) for m0 in range(0, M, CHUNK)]

    def body(a_ref, b_ref, o_ref, acc):
        k = pl.program_id(1)

        @pl.when(k == 0)
        def _():
            acc[...] = jnp.zeros_like(acc)

        for m0, mc in chunks:
            acc[m0:m0 + mc, :] += _dot_tn(a_ref[:, m0:m0 + mc], b_ref[...])

        @pl.when(k == nk - 1)
        def _():
            for m0, mc in chunks:
                o_ref[m0:m0 + mc, :] = acc[m0:m0 + mc, :].astype(BF)

    return pl.pallas_call(
        body, name=name, grid=(GA, nk),
        out_shape=_sds((M_all, N), BF),
        in_specs=[pl.BlockSpec((ts, M), lambda ga, k: (k, ga)), pl.BlockSpec((ts, N), lambda ga, k: (k, 0))],
        out_specs=pl.BlockSpec((M, N), lambda ga, k: (ga, 0)),
        scratch_shapes=[pltpu.VMEM((M, N), F32)],
        compiler_params=_cp(2, 56),
    )(a, b)


def _mix_in(h, sh, sc, gp, w):
    S = h.shape[0]
    R = min(512, S)

    def body(h_ref, sh_ref, sc_ref, gp_ref, w_ref, n_ref, qkv_ref, zg_ref, gates_ref):
        for r0 in range(0, R, CHUNK):
            rows = slice(r0, r0 + CHUNK)
            nb = _prenorm(h_ref[rows, :], gp_ref[...], sc_ref[...], sh_ref[...]).astype(BF)
            n_ref[rows, :] = nb
            qkv_ref[rows, :] = _dot_nt(nb, w_ref[0:ZG_OFF, :]).astype(BF)
            zg_ref[rows, :] = _dot_nt(nb, w_ref[ZG_OFF:GATE_OFF, :]).astype(BF)
            gates_ref[rows, :] = jax.nn.sigmoid(_dot_nt(nb, w_ref[GATE_OFF:IN_W, :])).astype(BF)

    vec = _const((1, D))
    rows = lambda w_: pl.BlockSpec((R, w_), lambda i: (i, 0))
    return pl.pallas_call(
        body, name="mix_in", grid=(S // R,),
        out_shape=[_sds((S, D), BF), _sds((S, QKV_W), BF), _sds((S, 2 * G_W), BF), _sds((S, 2 * D), BF)],
        in_specs=[rows(D), vec, vec, vec, _resident((IN_W, D))],
        out_specs=[rows(D), rows(QKV_W), rows(2 * G_W), rows(2 * D)],
        compiler_params=_cp(1, 48),
    )(h, sh, sc, gp, w)


def _bias_table(rel_bias, bucket):
    def body(rel_ref, bk_ref, out_ref):
        bk = bk_ref[...]
        qi = lax.broadcasted_iota(jnp.int32, (BLK, 2 * BLK), 0)
        kj = lax.broadcasted_iota(jnp.int32, (BLK, 2 * BLK), 1)
        dist = qi + BLK - kj
        window = (dist >= 0) & (dist < BLK)
        for h in range(N_HEADS):
            acc = jnp.zeros((BLK, 2 * BLK), F32)
            for b in range(N_BUCKETS):
                acc = jnp.where(bk == b, rel_ref[b, h], acc)
            out_ref[h // GROUP, pl.ds((h % GROUP) * BLK, BLK), :] = jnp.where(window, acc, NEG)

    return pl.pallas_call(
        body, name="bias_table",
        out_shape=_sds((N_KV, GROUP * BLK, 2 * BLK), F32),
        in_specs=[pl.BlockSpec(memory_space=pltpu.SMEM), pl.BlockSpec(memory_space=pltpu.VMEM)],
        out_specs=pl.BlockSpec(memory_space=pltpu.VMEM),
    )(rel_bias, bucket)


ATT_TB = 4


HEAD_ROWS = N_HEADS * BLK


PAIR_HEADS = tuple(h for g in range(GROUP) for h in (g, GROUP + g))
UNPAIR_HEADS = tuple(PAIR_HEADS.index(h) for h in range(N_HEADS))


def _reorder_heads(w, order):
    return jnp.concatenate([w[h * HD:(h + 1) * HD] for h in order], axis=0)


def _halves(x):
    low = lax.broadcasted_iota(jnp.int32, x.shape, 1) < HD
    xf = x.astype(F32)
    return jnp.where(low, xf, 0.0).astype(BF), jnp.where(low, 0.0, xf).astype(BF)


def _stack_heads(x):
    halves = [_halves(x[:, g * 128:(g + 1) * 128]) for g in range(GROUP)]
    return jnp.concatenate([lo for lo, _ in halves] + [hi for _, hi in halves], axis=0)


def _attn_probs(q, kvc, kvp, bias_ref, sink_ref, has_prev):
    kv2 = jnp.concatenate([kvp, kvc], axis=0)
    kboth, vboth = kv2[:, 0:KV_W], kv2[:, KV_W:2 * KV_W]
    qpad = _stack_heads(q)
    s = _dot_nt(qpad, kboth) * SCALE + bias_ref[...]
    if has_prev is not None:
        col = lax.broadcasted_iota(jnp.int32, (HEAD_ROWS, 2 * BLK), 1)
        s = jnp.where((col >= BLK) | has_prev, s, NEG)
    row_head = lax.broadcasted_iota(jnp.int32, (HEAD_ROWS, 1), 0) // BLK
    sink = jnp.zeros((HEAD_ROWS, 1), F32)
    for h in range(N_HEADS):
        sink = jnp.where(row_head == h, sink_ref[h], sink)
    m = jnp.maximum(jnp.max(s, axis=1, keepdims=True), sink)
    p = jnp.exp(s - m)
    e_sink = jnp.exp(sink - m)
    inv = 1.0 / (jnp.sum(p, axis=1, keepdims=True) + e_sink)
    return qpad, kboth, vboth, p * inv, e_sink * inv


def _attn_fwd(qkv, bias, sinks):
    S = qkv.shape[0]
    tb = min(ATT_TB, S // BLK)
    T = tb * BLK

    def body(sink_ref, q_ref, kv_ref, kvp_ref, bias_ref, o_ref):
        step = pl.program_id(0)
        for j in range(tb):
            rows = slice(j * BLK, (j + 1) * BLK)
            kvp = kvp_ref[...] if j == 0 else kv_ref[(j - 1) * BLK:j * BLK, :]
            has_prev = (step > 0) if j == 0 else None
            _, _, vboth, prob, _ = _attn_probs(q_ref[rows, :], kv_ref[rows, :], kvp, bias_ref, sink_ref, has_prev)
            pb = prob.astype(BF)
            v_low, v_high = _halves(vboth)
            half = HEAD_ROWS // 2
            o = _dot(pb[0:half], v_low) + _dot(pb[half:HEAD_ROWS], v_high)
            for g in range(GROUP):
                o_ref[rows, g * 128:(g + 1) * 128] = o[g * BLK:(g + 1) * BLK].astype(BF)

    return pl.pallas_call(
        body, name="attn_fwd", grid=(S // T,),
        out_shape=_sds((S, Q_W), BF),
        in_specs=[pl.BlockSpec(memory_space=pltpu.SMEM),
                  pl.BlockSpec((T, Q_W), lambda i: (i, 0)),
                  pl.BlockSpec((T, 2 * KV_W), lambda i: (i, 2)),
                  pl.BlockSpec((BLK, 2 * KV_W), lambda i: (jnp.maximum(i * tb - 1, 0), 2)),
                  _const((HEAD_ROWS, 2 * BLK))],
        out_specs=pl.BlockSpec((T, Q_W), lambda i: (i, 0)),
        compiler_params=_cp(1, 32),
    )(sinks, qkv, qkv, qkv, bias)


def _attn_bwd(qkv, bias, sinks, do):
    S = qkv.shape[0]
    tb = 1
    T = tb * BLK
    nt = S // T
    half = HEAD_ROWS // 2

    def body(sink_ref, q_ref, kv_ref, kvp_ref, bias_ref, do_ref, dq_ref, dkv_ref, dbias_ref, dsink_ref, carry):
        i = pl.program_id(0)

        @pl.when(i == 0)
        def _():
            carry[...] = jnp.zeros_like(carry)
            dbias_ref[...] = jnp.zeros_like(dbias_ref)
            dsink_ref[...] = jnp.zeros_like(dsink_ref)

        from_next = carry[...]
        head_row = lax.broadcasted_iota(jnp.int32, (N_HEADS, 128), 0)
        low = lax.broadcasted_iota(jnp.int32, (BLK, 128), 1) < HD
        for j in reversed(range(tb)):
            rows = slice(j * BLK, (j + 1) * BLK)
            kvp = kvp_ref[...] if j == 0 else kv_ref[(j - 1) * BLK:j * BLK, :]
            has_prev = (i < nt - 1) if j == 0 else None
            qpad, kboth, vboth, prob, p_sink = _attn_probs(q_ref[rows, :], kv_ref[rows, :], kvp, bias_ref, sink_ref,
                                                           has_prev)
            pb = prob.astype(BF)
            dopad = _stack_heads(do_ref[rows, :])
            dp = _dot_nt(dopad, vboth)
            delta = jnp.sum(prob * dp, axis=1, keepdims=True)
            ds = prob * (dp - delta)
            dbias_ref[...] += ds
            sink_term = p_sink * delta
            dsink_rows = jnp.zeros((N_HEADS, 128), F32)
            for h in range(N_HEADS):
                val = -jnp.sum(sink_term[h * BLK:(h + 1) * BLK], axis=0, keepdims=True)
                dsink_rows = jnp.where(head_row == h, val, dsink_rows)
            dsink_ref[...] += dsink_rows
            dsb = ds.astype(BF)
            dqpad = _dot(dsb, kboth) * SCALE
            for g in range(GROUP):
                dq_ref[rows, g * 128:(g + 1) * 128] = jnp.where(
                    low, dqpad[g * BLK:(g + 1) * BLK], dqpad[half + g * BLK:half + (g + 1) * BLK]).astype(BF)
            dkv2 = jnp.concatenate([jnp.transpose(_dot_tn(qpad, dsb)) * SCALE,
                                    jnp.transpose(_dot_tn(dopad, pb))], axis=1)
            dkv_ref[rows, :] = (dkv2[BLK:2 * BLK] + from_next).astype(BF)
            from_next = dkv2[0:BLK]
        carry[...] = from_next

    return pl.pallas_call(
        body, name="attn_bwd", grid=(nt,),
        out_shape=[_sds((S, Q_W), BF), _sds((S, 2 * KV_W), BF),
                   _sds((HEAD_ROWS, 2 * BLK), F32), _sds((N_HEADS, 128), F32)],
        in_specs=[pl.BlockSpec(memory_space=pltpu.SMEM),
                  pl.BlockSpec((T, Q_W), lambda i: (nt - 1 - i, 0)),
                  pl.BlockSpec((T, 2 * KV_W), lambda i: (nt - 1 - i, 2)),
                  pl.BlockSpec((BLK, 2 * KV_W), lambda i: (jnp.maximum((nt - 1 - i) * tb - 1, 0), 2)),
                  _const((HEAD_ROWS, 2 * BLK)),
                  pl.BlockSpec((T, Q_W), lambda i: (nt - 1 - i, 0))],
        out_specs=[pl.BlockSpec((T, Q_W), lambda i: (nt - 1 - i, 0)),
                   pl.BlockSpec((T, 2 * KV_W), lambda i: (nt - 1 - i, 0)),
                   _const((HEAD_ROWS, 2 * BLK)), _const((N_HEADS, 128))],
        scratch_shapes=[pltpu.VMEM((BLK, 2 * KV_W), F32)],
        compiler_params=_cp(1, 32),
    )(sinks, qkv, qkv, qkv, bias, do)


def _rel_bias_grad(dbias, bucket):
    def body(db_ref, bk_ref, out_ref):
        bk = bk_ref[...]
        lane = lax.broadcasted_iota(jnp.int32, (1, 128), 1)
        for h in range(N_HEADS):
            d = db_ref[h // GROUP, pl.ds((h % GROUP) * BLK, BLK), :]
            row = jnp.zeros((1, 128), F32)
            for b in range(N_BUCKETS):
                tot = jnp.sum(jnp.sum(jnp.where(bk == b, d, 0.0), axis=1, keepdims=True), axis=0, keepdims=True)
                row = jnp.where(lane == b, tot, row)
            out_ref[pl.ds(h, 1), :] = row

    vm = pl.BlockSpec(memory_space=pltpu.VMEM)
    return pl.pallas_call(body, name="rel_bias_grad", out_shape=_sds((N_HEADS, 128), F32),
                          in_specs=[vm, vm], out_specs=vm)(dbias, bucket)


def _gmlp_parts(zg, lg_ref, lb_ref):
    z = zg.astype(F32)
    ge = _gelu(z)
    u, vg = ge[:, 0:G_W], ge[:, G_W:2 * G_W]
    mu = jnp.mean(vg, axis=-1, keepdims=True)
    xc = vg - mu
    rstd = lax.rsqrt(jnp.mean(xc * xc, axis=-1, keepdims=True) + EPS)
    xh = xc * rstd
    return z, u, xh, rstd, xh * lg_ref[...] + lb_ref[...]


def _causal_weights(ws_ref, wc):
    t = lax.broadcasted_iota(jnp.int32, (BLK, BLK), 0)
    s = lax.broadcasted_iota(jnp.int32, (BLK, BLK), 1)
    for g in range(N_HEADS):
        wc[g] = jnp.where(s <= t, ws_ref[g], 0.0).astype(BF)


def _spatial(vb, wc, bst_ref, p, low):
    xp = vb[:, p * 128:(p + 1) * 128]
    s0 = _dot(wc[2 * p], xp) + bst_ref[:, 2 * p:2 * p + 1]
    s1 = _dot(wc[2 * p + 1], xp) + bst_ref[:, 2 * p + 1:2 * p + 2]
    return xp, jnp.where(low, s0, s1)


def _gmlp_fwd(zg, lg, lb, ws, bst):
    S = zg.shape[0]
    tb = min(ATT_TB, S // BLK)
    T = tb * BLK

    def body(zg_ref, lg_ref, lb_ref, ws_ref, bst_ref, o_ref, wc):
        @pl.when(pl.program_id(0) == 0)
        def _():
            _causal_weights(ws_ref, wc)
        low = lax.broadcasted_iota(jnp.int32, (BLK, 128), 1) < HD
        for j in range(tb):
            rows = slice(j * BLK, (j + 1) * BLK)
            _, u, _, _, vln = _gmlp_parts(zg_ref[rows, :], lg_ref, lb_ref)
            vb = vln.astype(BF)
            for p in range(4):
                _, sp = _spatial(vb, wc, bst_ref, p, low)
                o_ref[rows, p * 128:(p + 1) * 128] = (u[:, p * 128:(p + 1) * 128] * sp).astype(BF)

    return pl.pallas_call(
        body, name="gmlp_fwd", grid=(S // T,),
        out_shape=_sds((S, G_W), BF),
        in_specs=[pl.BlockSpec((T, 2 * G_W), lambda i: (i, 0)), _const((1, G_W)), _const((1, G_W)),
                  _const((N_HEADS, BLK, BLK)), _const((BLK, N_HEADS))],
        out_specs=pl.BlockSpec((T, G_W), lambda i: (i, 0)),
        scratch_shapes=[pltpu.VMEM((N_HEADS, BLK, BLK), BF)],
        compiler_params=_cp(1, 32),
    )(zg, lg, lb, ws, bst)


def _gmlp_bwd(zg, d_out, lg, lb, ws, bst):
    S = zg.shape[0]
    tb = min(ATT_TB, S // BLK)
    T = tb * BLK
    nb = S // T

    def body(zg_ref, d_ref, lg_ref, lb_ref, ws_ref, bst_ref, dzg_ref, dws_ref, dbs_ref, dlg_ref, dlb_ref, wc, dbacc):
        i = pl.program_id(0)

        @pl.when(i == 0)
        def _():
            _causal_weights(ws_ref, wc)
            dws_ref[...] = jnp.zeros_like(dws_ref)
            dlg_ref[...] = jnp.zeros_like(dlg_ref)
            dlb_ref[...] = jnp.zeros_like(dlb_ref)
            dbacc[...] = jnp.zeros_like(dbacc)

        low = lax.broadcasted_iota(jnp.int32, (BLK, 128), 1) < HD
        for j in range(tb):
            rows = slice(j * BLK, (j + 1) * BLK)
            z, u, xh, rstd, vln = _gmlp_parts(zg_ref[rows, :], lg_ref, lb_ref)
            vb = vln.astype(BF)
            d = d_ref[rows, :].astype(F32)
            du_parts, dvln_parts = [], []
            for p in range(4):
                xp, sp = _spatial(vb, wc, bst_ref, p, low)
                dp = d[:, p * 128:(p + 1) * 128]
                du_parts.append(dp * sp)
                dsp = dp * u[:, p * 128:(p + 1) * 128]
                dbacc[:, p * 128:(p + 1) * 128] += dsp
                d0 = jnp.where(low, dsp, 0.0).astype(BF)
                d1 = jnp.where(low, 0.0, dsp).astype(BF)
                dws_ref[2 * p] += _dot_nt(d0, xp)
                dws_ref[2 * p + 1] += _dot_nt(d1, xp)
                dvln_parts.append(_dot_tn(wc[2 * p], d0) + _dot_tn(wc[2 * p + 1], d1))
            dvln = jnp.concatenate(dvln_parts, axis=1)
            dlg_ref[...] += _colsum(dvln * xh)
            dlb_ref[...] += _colsum(dvln)
            dxh = dvln * lg_ref[...]
            dvg = rstd * (dxh - jnp.mean(dxh, axis=-1, keepdims=True)
                          - xh * jnp.mean(dxh * xh, axis=-1, keepdims=True))
            dge = jnp.concatenate(du_parts + [dvg], axis=1)
            dzg_ref[rows, :] = (dge * _gelu_grad(z)).astype(BF)

        @pl.when(i == nb - 1)
        def _():
            t = lax.broadcasted_iota(jnp.int32, (BLK, BLK), 0)
            s = lax.broadcasted_iota(jnp.int32, (BLK, BLK), 1)
            for g in range(N_HEADS):
                dws_ref[g] = jnp.where(s <= t, dws_ref[g], 0.0)
            grp = lax.broadcasted_iota(jnp.int32, (N_HEADS, G_W), 0)
            lane = lax.broadcasted_iota(jnp.int32, (N_HEADS, G_W), 1) // HD
            pick = jnp.where(grp == lane, 1.0, 0.0).astype(F32)
            dbs_ref[...] = lax.dot_general(pick, dbacc[...], (((1,), (1,)), ((), ())),
                                           preferred_element_type=F32, precision=HIGH)

    return pl.pallas_call(
        body, name="gmlp_bwd", grid=(nb,),
        out_shape=[_sds((S, 2 * G_W), BF), _sds((N_HEADS, BLK, BLK), F32), _sds((N_HEADS, BLK), F32),
                   _sds((1, G_W), F32), _sds((1, G_W), F32)],
        in_specs=[pl.BlockSpec((T, 2 * G_W), lambda i: (i, 0)), pl.BlockSpec((T, G_W), lambda i: (i, 0)),
                  _const((1, G_W)), _const((1, G_W)), _const((N_HEADS, BLK, BLK)), _const((BLK, N_HEADS))],
        out_specs=[pl.BlockSpec((T, 2 * G_W), lambda i: (i, 0)), _const((N_HEADS, BLK, BLK)),
                   _const((N_HEADS, BLK)), _const((1, G_W)), _const((1, G_W))],
        scratch_shapes=[pltpu.VMEM((N_HEADS, BLK, BLK), BF), pltpu.VMEM((BLK, G_W), F32)],
        compiler_params=_cp(1, 32),
    )(zg, d_out, lg, lb, ws, bst)


def _mix_out(o, gm, gates, h, wa, wg, wo, gate, gp):
    S = h.shape[0]
    R = min(512, S)

    def body(o_ref, gm_ref, gates_ref, h_ref, wa_ref, wg_ref, wo_ref, gate_ref, gp_ref,
             ya_ref, yg_ref, ym_ref, y_ref, hn_ref):
        for r0 in range(0, R, CHUNK):
            rows = slice(r0, r0 + CHUNK)
            ya = _dot(o_ref[rows, :], wa_ref[...])
            yg = _dot(gm_ref[rows, :], wg_ref[...])
            ya_ref[rows, :] = ya.astype(BF)
            yg_ref[rows, :] = yg.astype(BF)
            ym = (gates_ref[rows, 0:D].astype(F32) * ya + gates_ref[rows, D:2 * D].astype(F32) * yg).astype(BF)
            ym_ref[rows, :] = ym
            y = _dot(ym, wo_ref[...])
            y_ref[rows, :] = y.astype(BF)
            hn_ref[rows, :] = h_ref[rows, :] + gate_ref[...] * (y * _rms_r(y) * gp_ref[...])

    vec = _const((1, D))
    rows = lambda w_: pl.BlockSpec((R, w_), lambda i: (i, 0))
    return pl.pallas_call(
        body, name="mix_out", grid=(S // R,),
        out_shape=[_sds((S, D), BF)] * 4 + [_sds((S, D), F32)],
        in_specs=[rows(Q_W), rows(G_W), rows(2 * D), rows(D), _resident((Q_W, D)), _resident((G_W, D)),
                  _resident((D, D)), vec, vec],
        out_specs=[rows(D)] * 5,
        compiler_params=_cp(1, 48),
    )(o, gm, gates, h, wa, wg, wo, gate, gp)


def _mix_out_bwd(dh, y, ya, yg, gates, att, gm, ymix, wa, wg, wo, gate, gp):
    S = dh.shape[0]
    R = min(512, S)
    nb = S // R

    def body(dh_ref, y_ref, ya_ref, yg_ref, gates_ref, att_ref, gm_ref, ym_ref, wa_ref, wg_ref, wo_ref,
             gate_ref, gp_ref, dz_ref, do_ref, dgm_ref, dgate_ref, dgp_ref, gwo_ref, gwa_ref, gwg_ref,
             acc_o, acc_a, acc_g, dy_scr, dya_scr, dyg_scr):
        i = pl.program_id(0)

        @pl.when(i == 0)
        def _():
            for r in (dgate_ref, dgp_ref, acc_o, acc_a, acc_g):
                r[...] = jnp.zeros_like(r)
        for r0 in range(0, R, CHUNK):
            rows = slice(r0, r0 + CHUNK)
            dy, dgate, dgp = _postnorm_bwd(dh_ref[rows, :], y_ref[rows, :], gate_ref[...], gp_ref[...], 1.0)
            dgate_ref[...] += dgate
            dgp_ref[...] += dgp
            dyb = dy.astype(BF)
            dy_scr[rows, :] = dyb
            dym = _dot_nt(dyb, wo_ref[...])
            ga = gates_ref[rows, 0:D].astype(F32)
            gg = gates_ref[rows, D:2 * D].astype(F32)
            dya = (dym * ga).astype(BF)
            dyg = (dym * gg).astype(BF)
            dya_scr[rows, :] = dya
            dyg_scr[rows, :] = dyg
            dz_ref[rows, 0:D] = (dym * ya_ref[rows, :].astype(F32) * (ga * (1.0 - ga))).astype(BF)
            dz_ref[rows, D:2 * D] = (dym * yg_ref[rows, :].astype(F32) * (gg * (1.0 - gg))).astype(BF)
            do_ref[rows, :] = _dot_nt(dya, wa_ref[...]).astype(BF)
            dgm_ref[rows, :] = _dot_nt(dyg, wg_ref[...]).astype(BF)
        for m0 in range(0, D, CHUNK):
            acc_o[m0:m0 + CHUNK, :] += _dot_tn(ym_ref[:, m0:m0 + CHUNK], dy_scr[...])
        for m0 in range(0, Q_W, CHUNK):
            acc_a[m0:m0 + CHUNK, :] += _dot_tn(att_ref[:, m0:m0 + CHUNK], dya_scr[...])
            acc_g[m0:m0 + CHUNK, :] += _dot_tn(gm_ref[:, m0:m0 + CHUNK], dyg_scr[...])

        @pl.when(i == nb - 1)
        def _():
            for m0 in range(0, D, CHUNK):
                gwo_ref[m0:m0 + CHUNK, :] = acc_o[m0:m0 + CHUNK, :].astype(BF)
            for m0 in range(0, Q_W, CHUNK):
                gwa_ref[m0:m0 + CHUNK, :] = acc_a[m0:m0 + CHUNK, :].astype(BF)
                gwg_ref[m0:m0 + CHUNK, :] = acc_g[m0:m0 + CHUNK, :].astype(BF)

    vec = _const((1, D))
    rows = lambda w_: pl.BlockSpec((R, w_), lambda i: (i, 0))
    return pl.pallas_call(
        body, name="mix_out_bwd", grid=(nb,),
        out_shape=[_sds((S, 2 * D), BF), _sds((S, Q_W), BF), _sds((S, G_W), BF), _sds((1, D), F32),
                   _sds((1, D), F32), _sds((D, D), BF), _sds((Q_W, D), BF), _sds((G_W, D), BF)],
        in_specs=[rows(D), rows(D), rows(D), rows(D), rows(2 * D), rows(Q_W), rows(G_W), rows(D),
                  _resident((Q_W, D)), _resident((G_W, D)), _resident((D, D)), vec, vec],
        out_specs=[rows(2 * D), rows(Q_W), rows(G_W), vec, vec, _const((D, D)), _const((Q_W, D)),
                   _const((G_W, D))],
        scratch_shapes=[pltpu.VMEM((D, D), F32), pltpu.VMEM((Q_W, D), F32), pltpu.VMEM((G_W, D), F32)]
        + [pltpu.VMEM((R, D), BF)] * 3,
        compiler_params=_cp(1, 60),
    )(dh, y, ya, yg, gates, att, gm, ymix, wa, wg, wo, gate, gp)


def _mix_dn(dq, dkv, dzg, dzgate, w, h, dh, sc, gp):
    S = h.shape[0]
    R = min(512, S)

    def body(dq_ref, dkv_ref, dzg_ref, dzt_ref, w_ref, h_ref, dh_ref, sc_ref, gp_ref,
             out_ref, dsh_ref, dsc_ref, dgp_ref):
        @pl.when(pl.program_id(0) == 0)
        def _():
            dsh_ref[...] = jnp.zeros_like(dsh_ref)
            dsc_ref[...] = jnp.zeros_like(dsc_ref)
            dgp_ref[...] = jnp.zeros_like(dgp_ref)
        for r0 in range(0, R, CHUNK):
            rows = slice(r0, r0 + CHUNK)
            dn = _dot(dq_ref[rows, :], w_ref[0:Q_W, :])
            dn = dn + _dot(dkv_ref[rows, :], w_ref[Q_W:QKV_W, :])
            dn = dn + _dot(dzg_ref[rows, :], w_ref[ZG_OFF:GATE_OFF, :])
            dn = dn + _dot(dzt_ref[rows, :], w_ref[GATE_OFF:IN_W, :])
            dx, dsh, dsc, dgp = _prenorm_bwd(dn, h_ref[rows, :], gp_ref[...], sc_ref[...])
            out_ref[rows, :] = dh_ref[rows, :] + dx
            dsh_ref[...] += dsh
            dsc_ref[...] += dsc
            dgp_ref[...] += dgp

    vec = _const((1, D))
    rows = lambda w_: pl.BlockSpec((R, w_), lambda i: (i, 0))
    return pl.pallas_call(
        body, name="mix_dn", grid=(S // R,),
        out_shape=[_sds((S, D), F32)] + [_sds((1, D), F32)] * 3,
        in_specs=[rows(Q_W), rows(2 * KV_W), rows(2 * G_W), rows(2 * D), _resident((IN_W, D)),
                  rows(D), rows(D), vec, vec],
        out_specs=[rows(D), vec, vec, vec],
        compiler_params=_cp(1, 48),
    )(dq, dkv, dzg, dzgate, w, h, dh, sc, gp)


def _adamw_math(w, g, m, v):
    m2 = ADAM_B1 * m + (1.0 - ADAM_B1) * g
    v2 = ADAM_B2 * v + (1.0 - ADAM_B2) * (g * g)
    m_hat = m2 / (1.0 - ADAM_B1 ** ADAM_STEP)
    v_hat = v2 / (1.0 - ADAM_B2 ** ADAM_STEP)
    delta = -ADAM_LR * (m_hat / (jnp.sqrt(v_hat) + ADAM_EPS) + ADAM_WD * w)
    return delta, m2, v2


def _row_tile(rows, cols):
    best = None
    for t in range(16, rows + 1, 16):
        if rows % t == 0 and t * cols <= 256 * 1024:
            best = t
    return best if best is not None else rows


def _adamw_sharded(landing, w, m, v, name):
    r, c = w.shape
    tr = _row_tile(r, c)

    def body(l_ref, w_ref, m_ref, v_ref, g_ref, d_ref, m2_ref, v2_ref):
        g = l_ref[0].astype(F32)
        for j in range(1, N_DEV):
            g = g + l_ref[j].astype(F32)
        delta, m2, v2 = _adamw_math(w_ref[...], g, m_ref[...], v_ref[...])
        g_ref[...] = g
        d_ref[...] = delta
        m2_ref[...] = m2
        v2_ref[...] = v2

    row = pl.BlockSpec((tr, c), lambda i: (i, 0))
    return pl.pallas_call(
        body, name=name, grid=(r // tr,),
        out_shape=[_sds((r, c), F32)] * 4,
        in_specs=[pl.BlockSpec((N_DEV, tr, c), lambda i: (0, i, 0)), row, row, row],
        out_specs=[row] * 4,
        compiler_params=_cp(1, 48),
    )(landing, w, m, v)


def _adamw_small(items):
    n = len(items)

    def body(*refs):
        for k in range(n):
            w_ref, g_ref, m_ref, v_ref = refs[4 * k:4 * k + 4]
            outs = refs[4 * n + 3 * k:4 * n + 3 * k + 3]
            for o_ref, val in zip(outs, _adamw_math(w_ref[...], g_ref[...], m_ref[...], v_ref[...])):
                o_ref[...] = val

    vm = pl.BlockSpec(memory_space=pltpu.VMEM)
    flat = pl.pallas_call(
        body, name="adamw_small",
        out_shape=[_sds(it[0].shape, F32) for it in items for _ in range(3)],
        in_specs=[vm] * (4 * n), out_specs=[vm] * (3 * n),
    )(*[a for it in items for a in it])
    return [tuple(flat[3 * k:3 * k + 3]) for k in range(n)]


def _w_ada_update(c8, d_ada, w, m, v):
    tr = 256

    def body(c_ref, d_ref, w_ref, m_ref, v_ref, g_ref, dl_ref, m2_ref, v2_ref):
        cs = c_ref[...]
        cs = cs * jax.nn.sigmoid(cs)
        g = lax.dot_general(cs, d_ref[...], (((0,), (0,)), ((), ())), preferred_element_type=F32, precision=HIGH)
        delta, m2, v2 = _adamw_math(w_ref[...], g, m_ref[...], v_ref[...])
        g_ref[...] = g
        dl_ref[...] = delta
        m2_ref[...] = m2
        v2_ref[...] = v2

    row = pl.BlockSpec((tr, ADA_W), lambda i: (i, 0))
    return pl.pallas_call(
        body, name="w_ada_update", grid=(D // tr,),
        out_shape=[_sds((D, ADA_W), F32)] * 4,
        in_specs=[pl.BlockSpec((N_DEV, tr), lambda i: (0, i)), _const((N_DEV, ADA_W)), row, row, row],
        out_specs=[row] * 4,
        compiler_params=_cp(1, 40),
    )(c8, d_ada, w, m, v)


def _t5_bucket():
    qi = np.arange(BLK, dtype=np.int32)[:, None]
    kj = np.arange(2 * BLK, dtype=np.int32)[None, :]
    dist = np.maximum(qi + BLK - kj, 0)
    max_exact = N_BUCKETS // 2
    d_f = np.maximum(dist, max_exact).astype(np.float32)
    large = max_exact + (np.log(d_f / np.float32(max_exact)) / np.float32(math.log(MAX_DISTANCE / max_exact))
                         * np.float32(N_BUCKETS - max_exact)).astype(np.int32)
    large = np.minimum(large, N_BUCKETS - 1)
    return jnp.asarray(np.where(dist < max_exact, dist, large).astype(np.int32))


def _slabs_of_columns(w):
    r, c8 = w.shape
    return jnp.transpose(w.reshape(r, N_DEV, c8 // N_DEV), (1, 0, 2))


def _columns_of_slabs(w8):
    _, r, c = w8.shape
    return jnp.transpose(w8, (1, 0, 2)).reshape(r, N_DEV * c)


def kernel(x, c, rel_bias, w_ada, b_ada, pre_norm_g, post_norm_g, w_ffn1_in, w_ffn1_out, w_in, sinks, gmlp_ln_g, gmlp_ln_b, gmlp_w_s, gmlp_b_s, w_br_attn, w_br_gmlp, w_out, w_ffn2_in, w_ffn2_out, loss_target, m_rel_bias, m_w_ada, m_b_ada, m_pre_norm_g, m_post_norm_g, m_w_ffn1_in, m_w_ffn1_out, m_w_in, m_sinks, m_gmlp_ln_g, m_gmlp_ln_b, m_gmlp_w_s, m_gmlp_b_s, m_w_br_attn, m_w_br_gmlp, m_w_out, m_w_ffn2_in, m_w_ffn2_out, v_rel_bias, v_w_ada, v_b_ada, v_pre_norm_g, v_post_norm_g, v_w_ffn1_in, v_w_ffn1_out, v_w_in, v_sinks, v_gmlp_ln_g, v_gmlp_ln_b, v_gmlp_w_s, v_gmlp_b_s, v_w_br_attn, v_w_br_gmlp, v_w_out, v_w_ffn2_in, v_w_ffn2_out):
    me = 4 * lax.axis_index("x") + 2 * lax.axis_index("y") + lax.axis_index("c")
    x0 = x[0]
    target = loss_target[0]

    transposed = ("w_ffn1_in", "w_in", "w_ffn2_in")
    shards = [w_ffn1_in[0].T, w_ffn1_out[0], w_in[0].T, w_br_attn[0], w_br_gmlp[0], w_out[0],
              w_ffn2_in[0].T, w_ffn2_out[0]]
    shards_bf = [s.astype(BF) for s in shards]
    groups = [shards_bf[0:1], shards_bf[1:6], shards_bf[6:8]]

    def gather_start(i, after):
        return _slabs_start("gather", groups[i], after, "gather_start_%d" % i)

    def forward_start(st, i, after):
        lands = _slabs_wait("gather", len(groups[i]), st, after, "gather_wait_%d" % i)
        return _slabs_start("forward", lands, c, "forward_start_%d" % i)

    def gathered(st, i, after):
        return _slabs_wait("forward", len(groups[i]), st, after, "forward_wait_%d" % i)

    gs0 = gather_start(0, c)

    mine = jnp.concatenate([c[0], pre_norm_g[0].reshape(-1), post_norm_g[0].reshape(-1)])
    small8 = jnp.broadcast_to(mine[None, :], (8, mine.shape[0]))
    b_ada64 = jnp.repeat(b_ada.reshape(N_DEV, ADA_W), 8, axis=0)
    gath, ada64 = _ada_forward(small8, w_ada[0], b_ada64)
    gath8 = gath[::8]
    ada = ada64[::8].reshape(9, D)
    sh1, sc1, g1, sh2, sc2, g2, sh3, sc3, g3 = [ada[k:k + 1] for k in range(9)]
    gains = gath8[:, D:].reshape(N_DEV, 2, 3, 128)
    pre_g = jnp.transpose(gains[:, 0], (1, 0, 2)).reshape(3, D)
    post_g = jnp.transpose(gains[:, 1], (1, 0, 2)).reshape(3, D)
    pre = [pre_g[k:k + 1] for k in range(3)]
    post = [post_g[k:k + 1] for k in range(3)]

    bucket = _t5_bucket()
    bias = _bias_table(rel_bias, bucket).reshape(HEAD_ROWS, 2 * BLK)
    sinks8 = sinks[0]
    lg, lb = gmlp_ln_g, gmlp_ln_b
    ws = gmlp_w_s[0]
    bst = jnp.transpose(gmlp_b_s[0])

    fs0 = forward_start(gs0, 0, sh1)
    gs1 = gather_start(1, fs0[-1])
    wf1_in = gathered(fs0, 0, gs1[-1])[0].reshape(2 * D_FF, D)
    n1, fg1, fu1, fa1 = _ffn_in(x0, sh1, sc1, pre[0], wf1_in, "ffn1_in")
    fs1 = forward_start(gs1, 1, n1)
    gs2 = gather_start(2, fs1[-1])
    mix_w = gathered(fs1, 1, gs2[-1])
    wf1_out = mix_w[0].reshape(D_FF, D)
    w_in_full = mix_w[1].reshape(IN_W, D)
    w_in_full = jnp.concatenate([_reorder_heads(w_in_full[0:Q_W], PAIR_HEADS), w_in_full[Q_W:]], axis=0)
    w_bra = _reorder_heads(_columns_of_slabs(mix_w[2]), PAIR_HEADS)
    w_brg = _columns_of_slabs(mix_w[3])
    w_out_full = mix_w[4].reshape(D, D)
    h1, y1 = _ffn_out(fa1, wf1_out, x0, g1, post[0], "ffn1_out")
    n2, qkv, zg, gates = _mix_in(h1, sh2, sc2, pre[1], w_in_full)
    att = _attn_fwd(qkv, bias, sinks8)
    gm = _gmlp_fwd(zg, lg, lb, ws, bst)
    fs2 = forward_start(gs2, 2, gm)
    ya, yg, ymix, y2, h2 = _mix_out(att, gm, gates, h1, w_bra, w_brg, w_out_full, g2 + fs2[-1], post[1])
    wf2_in, wf2_out = gathered(fs2, 2, h2)
    wf2_in = wf2_in.reshape(2 * D_FF, D)
    wf2_out = wf2_out.reshape(D_FF, D)
    n3, fg3, fu3, fa3 = _ffn_in(h2, sh3, sc3, pre[2], wf2_in, "ffn2_in")
    dh3, y3, sq = _ffn_out(fa3, wf2_out, h2, g3, post[2], "ffn2_out", target=target)

    def exchange_start(i, arrays):
        return _slabs_start("exchange", arrays, sq, "exchange_start_%d" % i)

    dy3, dgu3, dh2, d_g3, d_post2, d_sh3, d_sc3, d_pre2 = _ffn_bwd(
        dh3, y3, fg3, fu3, wf2_out, wf2_in, h2, g3, post[2], sc3, pre[2], "ffn2_bwd")
    gw_f2_out = _tn_matmul(fa3, dy3, "ffn2_out_wgrad", tm=D_FF // 2).reshape(N_DEV, D_FF // N_DEV, D)
    gw_f2_in = _tn_matmul(dgu3, n3, "ffn2_in_wgrad", tm=D_FF // 2).reshape(N_DEV, FS, D)
    ex1 = exchange_start(1, [gw_f2_out, gw_f2_in])

    dzgate, d_att, d_gm, d_g2, d_post1, gw_out, gw_bra, gw_brg = _mix_out_bwd(
        dh2, y2, ya, yg, gates, att, gm, ymix, w_bra, w_brg, w_out_full, g2 + ex1[-1], post[1])
    ex2 = exchange_start(2, [_slabs_of_columns(_reorder_heads(gw_bra, UNPAIR_HEADS)), _slabs_of_columns(gw_brg),
                             gw_out.reshape(N_DEV, D // N_DEV, D)])
    dq, dkv, dbias, dsink = _attn_bwd(qkv, bias, sinks8, d_att)
    dzg, d_ws, d_bs, d_lg, d_lb = _gmlp_bwd(zg, d_gm, lg, lb, ws, bst)
    d_rel = _rel_bias_grad(dbias.reshape(N_KV, GROUP * BLK, 2 * BLK), bucket)
    early = jnp.concatenate([
        jnp.concatenate([d_lg.reshape(4, 128), d_lb.reshape(4, 128)], axis=0),
        d_bs, d_rel, dsink, d_ws.reshape(N_HEADS * BLK, BLK)], axis=0)
    sm0 = _slabs_start("gather_all", [early], sq, "small_gather_start")
    dh1, d_sh2, d_sc2, d_pre1 = _mix_dn(dq, dkv, dzg, dzgate, w_in_full, h1, dh2, sc2 + ex2[-1] + sm0[-1], pre[1])
    gw_in = jnp.concatenate(
        [_reorder_heads(_tn_matmul(dq, n2, "w_in_q_wgrad"), UNPAIR_HEADS), _tn_matmul(dkv, n2, "w_in_kv_wgrad"),
         _tn_matmul(dzg, n2, "w_in_zg_wgrad"), _tn_matmul(dzgate, n2, "w_in_gate_wgrad")],
        axis=0).reshape(N_DEV, IN_W // N_DEV, D)
    ex3 = exchange_start(3, [gw_in])

    dy1, dgu1, d_g1, d_post0 = _ffn_out_bwd(dh1, y1, fg1, fu1, wf1_out, g1 + ex3[-1], post[0], "ffn1_out_bwd")
    gw_f1_out = _tn_matmul(fa1, dy1, "ffn1_out_wgrad", tm=D_FF // 2).reshape(N_DEV, D_FF // N_DEV, D)
    ex4 = exchange_start(4, [gw_f1_out])
    gw_f1_in = _tn_matmul(dgu1, n1, "ffn1_in_wgrad", tm=D_FF // 2).reshape(N_DEV, FS, D)
    ex5 = exchange_start(5, [gw_f1_in])
    grad_x, d_sh1, d_sc1, d_pre0 = _ffn_dn(dgu1, wf1_in, x0, dh1, sc1 + ex4[-1] + ex5[-1], pre[0], "ffn1_dn")

    landed = {}
    for i, (ex, nms) in enumerate([(ex1, ["w_ffn2_out", "w_ffn2_in"]),
                                   (ex2, ["w_br_attn", "w_br_gmlp", "w_out"]), (ex3, ["w_in"]),
                                   (ex4, ["w_ffn1_out"]), (ex5, ["w_ffn1_in"])]):
        for nm, land in zip(nms, _slabs_wait("exchange", len(nms), ex, grad_x, "exchange_wait_%d" % i)):
            landed[nm] = land
    moments = [(m_w_ffn1_in, v_w_ffn1_in), (m_w_ffn1_out, v_w_ffn1_out), (m_w_in, v_w_in),
               (m_w_br_attn, v_w_br_attn), (m_w_br_gmlp, v_w_br_gmlp), (m_w_out, v_w_out),
               (m_w_ffn2_in, v_w_ffn2_in), (m_w_ffn2_out, v_w_ffn2_out)]
    names = ["w_ffn1_in", "w_ffn1_out", "w_in", "w_br_attn", "w_br_gmlp", "w_out", "w_ffn2_in", "w_ffn2_out"]
    big = {}
    for nm, w_, (m_, v_) in zip(names, shards, moments):
        if nm in transposed:
            res4 = _adamw_sharded(landed[nm], w_, m_[0].T, v_[0].T, "adamw_" + nm)
            big[nm] = [a.T[None] for a in res4]
        else:
            big[nm] = [a[None] for a in _adamw_sharded(landed[nm], w_, m_[0], v_[0], "adamw_" + nm)]

    d_ada = jnp.concatenate([v_.reshape(8, 128) for v_ in
                             (d_sh1, d_sc1, d_g1, d_sh2, d_sc2, d_g2, d_sh3, d_sc3, d_g3)], axis=0)
    d_pre = jnp.concatenate([d_pre0, d_pre1, d_pre2], axis=0)
    d_post = jnp.concatenate([d_post0, d_post1, d_post2], axis=0)
    late = jnp.concatenate([d_ada, _slabs_of_columns(d_pre).reshape(24, 128),
                            _slabs_of_columns(d_post).reshape(24, 128),
                            jnp.broadcast_to(sq * (0.5 / D), (8, 128))], axis=0)
    late, _ = lax.optimization_barrier((late, landed["w_ffn1_in"]))
    tot, every = _small_allreduce(late)
    (early_land,) = _slabs_wait("gather_all", 1, sm0, grad_x, "small_gather_wait")
    tot_early = _sum_slabs(early_land)

    loss = tot[120, 0]
    g_b_ada = tot[0:72].reshape(1, 9 * D)
    g_pre = lax.dynamic_slice_in_dim(tot[72:96], 3 * me, 3, axis=0)[None]
    g_post = lax.dynamic_slice_in_dim(tot[96:120], 3 * me, 3, axis=0)[None]
    g_lg = tot_early[0:4].reshape(1, G_W)
    g_lb = tot_early[4:8].reshape(1, G_W)
    g_bs = tot_early[8:16][None]
    g_rel = jnp.transpose(tot_early[16:24, 0:N_BUCKETS])
    g_sinks = tot_early[24:32, 0][None]
    g_ws = tot_early[32:1056].reshape(1, N_HEADS, BLK, BLK)

    d_ada_mine = lax.dynamic_slice_in_dim(every[:, 0:72].reshape(N_DEV, N_DEV, ADA_W), me, 1, axis=1)[:, 0]
    ada_out = [a[None] for a in _w_ada_update(gath8[:, 0:D], d_ada_mine, w_ada[0], m_w_ada[0], v_w_ada[0])]

    small = [("rel_bias", rel_bias, g_rel, m_rel_bias, v_rel_bias), ("b_ada", b_ada, g_b_ada, m_b_ada, v_b_ada),
             ("pre_norm_g", pre_norm_g, g_pre, m_pre_norm_g, v_pre_norm_g),
             ("post_norm_g", post_norm_g, g_post, m_post_norm_g, v_post_norm_g),
             ("sinks", sinks, g_sinks, m_sinks, v_sinks), ("gmlp_ln_g", gmlp_ln_g, g_lg, m_gmlp_ln_g, v_gmlp_ln_g),
             ("gmlp_ln_b", gmlp_ln_b, g_lb, m_gmlp_ln_b, v_gmlp_ln_b),
             ("gmlp_w_s", gmlp_w_s, g_ws, m_gmlp_w_s, v_gmlp_w_s), ("gmlp_b_s", gmlp_b_s, g_bs, m_gmlp_b_s, v_gmlp_b_s)]
    two_d = lambda a: a.reshape(int(math.prod(a.shape[:-1])), a.shape[-1])
    stepped = _adamw_small([tuple(two_d(a) for a in item[1:]) for item in small])
    res = {"w_ada": ada_out}
    for (nm, w_, g_, _, _), new in zip(small, stepped):
        res[nm] = [g_] + [a.reshape(w_.shape) for a in new]
    res.update(big)
    order = ["rel_bias", "w_ada", "b_ada", "pre_norm_g", "post_norm_g", "w_ffn1_in", "w_ffn1_out", "w_in", "sinks",
             "gmlp_ln_g", "gmlp_ln_b", "gmlp_w_s", "gmlp_b_s", "w_br_attn", "w_br_gmlp", "w_out", "w_ffn2_in",
             "w_ffn2_out"]
    outs = [loss, grad_x[None]]
    for k in range(4):
        outs += [res[nm][k] for nm in order]
    return tuple(outs)
```

```python
import functools
import math

import jax
import jax.numpy as jnp
import numpy as np
from jax import lax
from jax.experimental import pallas as pl
from jax.experimental.pallas import tpu as pltpu

F32 = jnp.float32
BF = jnp.bfloat16

N_DEV = 8
D = 1024
D_FF = 2816
FS = D_FF // 4
N_HEADS = 8
N_KV = 2
GROUP = 4
HD = 64
BLK = 128
Q_W = 512
KV_W = 128
G_W = 512
QKV_W = Q_W + 2 * KV_W
ZG_OFF = QKV_W
GATE_OFF = ZG_OFF + 2 * G_W
IN_W = GATE_OFF + 2 * D
N_BUCKETS = 32
MAX_DISTANCE = 128
EPS = 1e-6
NEG = -1e30
SCALE = HD ** -0.5
ADA_W = 9 * D // N_DEV

ADAM_LR = 0.001
ADAM_B1 = 0.9
ADAM_B2 = 0.999
ADAM_EPS = 1e-08
ADAM_WD = 0.01
ADAM_STEP = 10

CHUNK = 256
MIB = 1024 * 1024
MESH = pl.DeviceIdType.MESH
HIGH = lax.Precision.HIGHEST


def _cp(n_grid, vmem_mib):
    return pltpu.CompilerParams(dimension_semantics=("arbitrary",) * n_grid,
                                vmem_limit_bytes=vmem_mib * MIB)


def _const(shape):
    return pl.BlockSpec(shape, lambda *_: (0,) * len(shape))


def _resident(shape):
    return pl.BlockSpec(shape, lambda *_: (0,) * len(shape), pipeline_mode=pl.Buffered(1))


def _sds(shape, dtype):
    return jax.ShapeDtypeStruct(shape, dtype)


def _dot(a, b):
    return jnp.dot(a, b, preferred_element_type=F32)


def _dot_nt(a, b):
    return lax.dot_general(a, b, (((1,), (1,)), ((), ())), preferred_element_type=F32)


def _dot_tn(a, b):
    return lax.dot_general(a, b, (((0,), (0,)), ((), ())), preferred_element_type=F32)


def _rms_r(x):
    return lax.rsqrt(jnp.mean(x * x, axis=-1, keepdims=True) + EPS)


def _colsum(x):
    return jnp.sum(x, axis=0, keepdims=True)


def _prenorm(x, gp, sc, sh):
    return (x * _rms_r(x) * gp) * (1.0 + sc) + sh


def _prenorm_bwd(dn, x, gp, sc):
    r = _rms_r(x)
    xh = x * r
    t = dn * (1.0 + sc) * gp
    dx = r * (t - xh * jnp.mean(t * xh, axis=-1, keepdims=True))
    return dx, _colsum(dn), _colsum(dn * xh * gp), _colsum(dn * (1.0 + sc) * xh)


def _postnorm_bwd(dh, y, gate, gp, res):
    y = y.astype(F32)
    r = _rms_r(y)
    yh = y * r
    dyn = (res * gate) * dh
    t = dyn * gp
    dy = r * (t - yh * jnp.mean(t * yh, axis=-1, keepdims=True))
    return dy, _colsum(res * dh * yh * gp), _colsum(dyn * yh)


def _gelu(x):
    k = math.sqrt(2.0 / math.pi)
    return 0.5 * x * (1.0 + jnp.tanh(k * (x + 0.044715 * x * x * x)))


def _gelu_grad(x):
    k = math.sqrt(2.0 / math.pi)
    t = jnp.tanh(k * (x + 0.044715 * x * x * x))
    return 0.5 * (1.0 + t) + 0.5 * x * (1.0 - t * t) * (k * (1.0 + 3.0 * 0.044715 * x * x))


def _my_place():
    x, y, c = lax.axis_index("x"), lax.axis_index("y"), lax.axis_index("c")
    return x, y, c, 4 * x + 2 * y + c


def _peer(x, y, c, k):
    px = 1 - x if k & 4 else x
    py = 1 - y if k & 2 else y
    pc = 1 - c if k & 1 else c
    return (px, py, pc), 4 * px + 2 * py + pc


HBM_SPEC = pl.BlockSpec(memory_space=pltpu.HBM)
SEM_SPEC = pl.BlockSpec(memory_space=pltpu.SEMAPHORE)
EFFECT = pltpu.SideEffectType.DATAFLOW_SIDE_EFFECTING


RELATIONS = {"exchange": (1, 2, 3, 4, 5, 6, 7), "gather": (1, 2, 4, 6), "forward": (2, 4, 6),
             "gather_all": (1, 2, 3, 4, 5, 6, 7)}


def _slab_copies(mode, srcs, lands, send, recv, loc):
    x, y, c, me = _my_place()
    rel = RELATIONS[mode]
    remote, local = [], []
    for t in range(len(lands)):
        for i, k in enumerate(rel):
            peer, peer_lin = _peer(x, y, c, k)
            if mode == "exchange":
                src, dst, to = srcs[t].at[peer_lin], lands[t].at[me], peer
            elif mode in ("gather", "gather_all"):
                src, dst, to = srcs[t], lands[t].at[me], peer
            else:
                src, dst, to = lands[t].at[peer_lin], lands[t].at[peer_lin], _peer(x, y, c, 1)[0]
            remote.append(pltpu.make_async_remote_copy(
                src_ref=src, dst_ref=dst, send_sem=send.at[t * len(rel) + i], recv_sem=recv.at[t * len(rel) + i],
                device_id=to, device_id_type=MESH))
        if mode == "exchange":
            local.append(pltpu.make_async_copy(srcs[t].at[me], lands[t].at[me], loc.at[t]))
        elif mode in ("gather", "gather_all"):
            local.append(pltpu.make_async_copy(srcs[t], lands[t].at[me], loc.at[t]))
    return remote, local


def _slabs_start(mode, arrays, after, name):
    n = len(arrays)
    if mode == "forward":
        thru = list(arrays)
    else:
        shapes = [a.shape if mode == "exchange" else (N_DEV,) + a.shape for a in arrays]
        thru = list(arrays) + [lax.empty(s, a.dtype) for s, a in zip(shapes, arrays)]
    m = len(thru)
    n_sem = n * len(RELATIONS[mode])

    def body(*refs):
        srcs, lands = refs[:n], refs[m - n:m]
        send, recv, loc = refs[m + 1:m + 4]
        remote, local = _slab_copies(mode, srcs, lands, send, recv, loc)
        for cp in remote + local:
            cp.start()
        refs[-1][...] = jnp.zeros_like(refs[-1])

    return pl.pallas_call(
        body, name=name,
        out_shape=(pltpu.SemaphoreType.DMA((n_sem,)), pltpu.SemaphoreType.DMA((n_sem,)),
                   pltpu.SemaphoreType.DMA((n,)),
                   *[pltpu.HBM(a.shape, a.dtype) for a in thru],
                   _sds((1, D), F32)),
        in_specs=[HBM_SPEC] * m + [pl.BlockSpec(memory_space=pl.ANY)],
        out_specs=(SEM_SPEC, SEM_SPEC, SEM_SPEC, *[HBM_SPEC] * m, pl.BlockSpec(memory_space=pltpu.VMEM)),
        input_output_aliases={t: 3 + t for t in range(m)},
        compiler_params=pltpu.CompilerParams(has_side_effects=EFFECT),
    )(*[pltpu.with_memory_space_constraint(a, pltpu.HBM) for a in thru], after)


def _slabs_wait(mode, n, started, after, name):
    sems = started[0:3]
    thru = started[3:-1]
    m = len(thru)

    def body(*refs):
        srcs, lands = refs[:n], refs[m - n:m]
        remote, local = _slab_copies(mode, srcs, lands, *refs[m:m + 3])
        for cp in remote:
            cp.wait_send()
            cp.wait_recv()
        for cp in local:
            cp.wait()

    res = pl.pallas_call(
        body, name=name,
        out_shape=tuple(pltpu.HBM(a.shape, a.dtype) for a in thru),
        in_specs=[HBM_SPEC] * m + [SEM_SPEC] * 3 + [pl.BlockSpec(memory_space=pl.ANY)],
        out_specs=tuple([HBM_SPEC] * m),
        input_output_aliases={t: t for t in range(m)},
        compiler_params=pltpu.CompilerParams(has_side_effects=EFFECT),
    )(*thru, *sems, after)
    return list(res[m - n:m])


def _ada_forward(small8, w_ada, b_ada64):
    sw = small8.shape[1]

    def body(sm_ref, w_ref, b_ref, gath_ref, ada_ref, part_ref, send1, recv1, send2, recv2):
        x, y, c, me = _my_place()
        row_me = pl.multiple_of(me * 8, 8)
        gath_ref[pl.ds(row_me, 8), :] = sm_ref[...]
        first = []
        for k in range(1, N_DEV):
            peer, _ = _peer(x, y, c, k)
            cp = pltpu.make_async_remote_copy(
                src_ref=sm_ref, dst_ref=gath_ref.at[pl.ds(row_me, 8), :], send_sem=send1.at[k - 1],
                recv_sem=recv1.at[k - 1], device_id=peer, device_id_type=MESH)
            cp.start()
            first.append(cp)
        for cp in first:
            cp.wait()
        cs = gath_ref[:, 0:D]
        cs = cs * jax.nn.sigmoid(cs)
        part_ref[...] = jnp.dot(cs, w_ref[...], preferred_element_type=F32, precision=HIGH)
        ada_ref[pl.ds(row_me, 8), :] = part_ref[pl.ds(row_me, 8), :]
        second = []
        for k in range(1, N_DEV):
            peer, peer_lin = _peer(x, y, c, k)
            cp = pltpu.make_async_remote_copy(
                src_ref=part_ref.at[pl.ds(pl.multiple_of(peer_lin * 8, 8), 8), :],
                dst_ref=ada_ref.at[pl.ds(row_me, 8), :], send_sem=send2.at[k - 1],
                recv_sem=recv2.at[k - 1], device_id=peer, device_id_type=MESH)
            cp.start()
            second.append(cp)
        for cp in second:
            cp.wait()
        ada_ref[...] = ada_ref[...] + b_ref[...]

    vm = pl.BlockSpec(memory_space=pltpu.VMEM)
    return pl.pallas_call(
        body, name="ada_forward",
        out_shape=[_sds((8 * N_DEV, sw), F32), _sds((8 * N_DEV, ADA_W), F32)],
        in_specs=[vm, vm, vm], out_specs=[vm, vm],
        scratch_shapes=[pltpu.VMEM((8 * N_DEV, ADA_W), F32)] + [pltpu.SemaphoreType.DMA((7,))] * 4,
        compiler_params=pltpu.CompilerParams(vmem_limit_bytes=32 * MIB),
    )(small8, w_ada, b_ada64)


def _sum_slabs(land):
    def body(l_ref, o_ref):
        acc = l_ref[0]
        for j in range(1, N_DEV):
            acc = acc + l_ref[j]
        o_ref[...] = acc

    vm = pl.BlockSpec(memory_space=pltpu.VMEM)
    return pl.pallas_call(body, name="sum_slabs", out_shape=_sds(land.shape[1:], F32), in_specs=[vm], out_specs=vm,
                          compiler_params=pltpu.CompilerParams(vmem_limit_bytes=32 * MIB))(land)


def _small_allreduce(vectors):
    n = len(vectors)

    def body(*refs):
        v_refs, (sum_ref, gath_ref, pack, send, recv) = refs[:n], refs[n:]
        x, y, c, me = _my_place()
        for k in range(n):
            pack[k:k + 1, :] = v_refs[k][...]
        gath_ref[me] = pack[...]
        cps = []
        for k in range(1, N_DEV):
            peer, _ = _peer(x, y, c, k)
            cp = pltpu.make_async_remote_copy(
                src_ref=pack, dst_ref=gath_ref.at[me], send_sem=send.at[k - 1],
                recv_sem=recv.at[k - 1], device_id=peer, device_id_type=MESH)
            cp.start()
            cps.append(cp)
        for cp in cps:
            cp.wait()
        acc = gath_ref[0]
        for j in range(1, N_DEV):
            acc = acc + gath_ref[j]
        sum_ref[...] = acc

    vm = pl.BlockSpec(memory_space=pltpu.VMEM)
    return pl.pallas_call(
        body, name="small_allreduce",
        out_shape=[_sds((n, D), F32), _sds((N_DEV, n, D), F32)],
        in_specs=[vm] * n, out_specs=[vm, vm],
        scratch_shapes=[pltpu.VMEM((n, D), F32), pltpu.SemaphoreType.DMA((7,)), pltpu.SemaphoreType.DMA((7,))],
    )(*vectors)


F_TILES = tuple((f0, min(512, D_FF - f0)) for f0 in range(0, D_FF, 512))
F_TILES_NARROW = tuple((f0, 256) for f0 in range(0, D_FF, 256))


def _swiglu_tile(n, wt_ref, f0, tf):
    g = _dot_nt(n, wt_ref[f0:f0 + tf, :])
    u = _dot_nt(n, wt_ref[D_FF + f0:D_FF + f0 + tf, :])
    sg = jax.nn.sigmoid(g)
    silu = g * sg
    return (u * (sg * (1.0 + g * (1.0 - sg)))).astype(BF), silu.astype(BF), (silu * u).astype(BF)


def _ffn_in(h, sh, sc, gp, wt, name):
    S = h.shape[0]
    R = min(512, S)

    def body(h_ref, sh_ref, sc_ref, gp_ref, w_ref, n_ref, dg_ref, sl_ref, a_ref):
        for r0 in range(0, R, CHUNK):
            rows = slice(r0, r0 + CHUNK)
            n = _prenorm(h_ref[rows, :], gp_ref[...], sc_ref[...], sh_ref[...]).astype(BF)
            n_ref[rows, :] = n
            for f0, tf in F_TILES_NARROW:
                dg_ref[rows, f0:f0 + tf], sl_ref[rows, f0:f0 + tf], a_ref[rows, f0:f0 + tf] = _swiglu_tile(
                    n, w_ref, f0, tf)

    vec = _const((1, D))
    rows_ = lambda w_: pl.BlockSpec((R, w_), lambda i: (i, 0))
    return pl.pallas_call(
        body, name=name, grid=(S // R,),
        out_shape=[_sds((S, D), BF)] + [_sds((S, D_FF), BF)] * 3,
        in_specs=[rows_(D), vec, vec, vec, _resident((2 * D_FF, D))],
        out_specs=[rows_(D), rows_(D_FF), rows_(D_FF), rows_(D_FF)],
        compiler_params=_cp(1, 56),
    )(h, sh, sc, gp, wt)


def _ffn_out(a, w, h, gate, gp, name, target=None):
    S = h.shape[0]
    R = min(512, S)
    with_loss = target is not None

    def body(a_ref, w_ref, h_ref, gate_ref, gp_ref, *rest):
        if with_loss:
            t_ref, out_ref, y_ref, tot_ref = rest

            @pl.when(pl.program_id(0) == 0)
            def _():
                tot_ref[...] = jnp.zeros_like(tot_ref)
        else:
            out_ref, y_ref = rest
        for r0 in range(0, R, CHUNK):
            rows = slice(r0, r0 + CHUNK)
            y = _dot(a_ref[rows, :], w_ref[...])
            y_ref[rows, :] = y.astype(BF)
            hn = h_ref[rows, :] + (0.5 * gate_ref[...]) * (y * _rms_r(y) * gp_ref[...])
            if with_loss:
                e = hn - t_ref[rows, :]
                out_ref[rows, :] = e * (1.0 / D)
                tot_ref[...] += jnp.sum(jnp.sum(e * e, axis=1, keepdims=True), axis=0, keepdims=True)
            else:
                out_ref[rows, :] = hn

    vec = _const((1, D))
    rows_ = lambda w_: pl.BlockSpec((R, w_), lambda i: (i, 0))
    return pl.pallas_call(
        body, name=name, grid=(S // R,),
        out_shape=[_sds((S, D), F32), _sds((S, D), BF)] + ([_sds((1, 1), F32)] if with_loss else []),
        in_specs=[rows_(D_FF), _resident((D_FF, D)), rows_(D), vec, vec] + ([rows_(D)] if with_loss else []),
        out_specs=[rows_(D), rows_(D)] + ([_const((1, 1))] if with_loss else []),
        compiler_params=_cp(1, 48),
    )(*((a, w, h, gate, gp) + ((target,) if with_loss else ())))


def _ffn_out_bwd(dh, y, dsilu_u, silu, w, gate, gp, name):
    S = dh.shape[0]
    R = min(512, S)

    def body(dh_ref, y_ref, g_ref, u_ref, w_ref, gate_ref, gp_ref, dy_ref, dgu_ref, dgate_ref, dgp_ref):
        @pl.when(pl.program_id(0) == 0)
        def _():
            dgate_ref[...] = jnp.zeros_like(dgate_ref)
            dgp_ref[...] = jnp.zeros_like(dgp_ref)
        for r0 in range(0, R, CHUNK):
            rows = slice(r0, r0 + CHUNK)
            dy, dgate, dgp = _postnorm_bwd(dh_ref[rows, :], y_ref[rows, :], gate_ref[...], gp_ref[...], 0.5)
            dgate_ref[...] += dgate
            dgp_ref[...] += dgp
            dyb = dy.astype(BF)
            dy_ref[rows, :] = dyb
            for f0, tf in F_TILES:
                da = _dot_nt(dyb, w_ref[f0:f0 + tf, :])
                dgu_ref[rows, f0:f0 + tf] = (da * g_ref[rows, f0:f0 + tf].astype(F32)).astype(BF)
                dgu_ref[rows, D_FF + f0:D_FF + f0 + tf] = (da * u_ref[rows, f0:f0 + tf].astype(F32)).astype(BF)

    vec = _const((1, D))
    rows_ = lambda w_: pl.BlockSpec((R, w_), lambda i: (i, 0))
    return pl.pallas_call(
        body, name=name, grid=(S // R,),
        out_shape=[_sds((S, D), BF), _sds((S, 2 * D_FF), BF), _sds((1, D), F32), _sds((1, D), F32)],
        in_specs=[rows_(D), rows_(D), rows_(D_FF), rows_(D_FF), _resident((D_FF, D)), vec, vec],
        out_specs=[rows_(D), rows_(2 * D_FF), vec, vec],
        compiler_params=_cp(1, 56),
    )(dh, y, dsilu_u, silu, w, gate, gp)


def _ffn_dn(dgu, wt, h, dh, sc, gp, name):
    S = h.shape[0]
    R = min(512, S)

    def body(dgu_ref, w_ref, h_ref, dh_ref, sc_ref, gp_ref, out_ref, dsh_ref, dsc_ref, dgp_ref):
        @pl.when(pl.program_id(0) == 0)
        def _():
            dsh_ref[...] = jnp.zeros_like(dsh_ref)
            dsc_ref[...] = jnp.zeros_like(dsc_ref)
            dgp_ref[...] = jnp.zeros_like(dgp_ref)

        for r0 in range(0, R, CHUNK):
            rows = slice(r0, r0 + CHUNK)
            dn = _dot(dgu_ref[rows, :], w_ref[...])
            dx, dsh, dsc, dgp = _prenorm_bwd(dn, h_ref[rows, :], gp_ref[...], sc_ref[...])
            out_ref[rows, :] = dh_ref[rows, :] + dx
            dsh_ref[...] += dsh
            dsc_ref[...] += dsc
            dgp_ref[...] += dgp

    vec = _const((1, D))
    rows_ = lambda w_: pl.BlockSpec((R, w_), lambda i: (i, 0))
    return pl.pallas_call(
        body, name=name, grid=(S // R,),
        out_shape=[_sds((S, D), F32)] + [_sds((1, D), F32)] * 3,
        in_specs=[rows_(2 * D_FF), _resident((2 * D_FF, D)), rows_(D), rows_(D), vec, vec],
        out_specs=[rows_(D), vec, vec, vec],
        compiler_params=_cp(1, 56),
    )(dgu, wt, h, dh, sc, gp)


def _ffn_bwd(dh, y, dsilu_u, silu, w, wt, h, gate, gpost, sc, gpre, name):
    S = dh.shape[0]
    R = min(256, S)

    def body(dh_ref, y_ref, g_ref, u_ref, w_ref, wt_ref, h_ref, gate_ref, gpost_ref, sc_ref, gpre_ref,
             dy_ref, dgu_ref, out_ref, dgate_ref, dgpost_ref, dsh_ref, dsc_ref, dgpre_ref):
        @pl.when(pl.program_id(0) == 0)
        def _():
            for r in (dgate_ref, dgpost_ref, dsh_ref, dsc_ref, dgpre_ref):
                r[...] = jnp.zeros_like(r)
        dhh = dh_ref[...]
        dy, dgate, dgpost = _postnorm_bwd(dhh, y_ref[...], gate_ref[...], gpost_ref[...], 0.5)
        dgate_ref[...] += dgate
        dgpost_ref[...] += dgpost
        dyb = dy.astype(BF)
        dy_ref[...] = dyb
        dn = None
        for f0, tf in F_TILES:
            da = _dot_nt(dyb, w_ref[f0:f0 + tf, :])
            dg = (da * g_ref[:, f0:f0 + tf].astype(F32)).astype(BF)
            du = (da * u_ref[:, f0:f0 + tf].astype(F32)).astype(BF)
            dgu_ref[:, f0:f0 + tf] = dg
            dgu_ref[:, D_FF + f0:D_FF + f0 + tf] = du
            part = _dot(dg, wt_ref[f0:f0 + tf, :]) + _dot(du, wt_ref[D_FF + f0:D_FF + f0 + tf, :])
            dn = part if dn is None else dn + part
        dx, dsh, dsc, dgpre = _prenorm_bwd(dn, h_ref[...], gpre_ref[...], sc_ref[...])
        out_ref[...] = dhh + dx
        dsh_ref[...] += dsh
        dsc_ref[...] += dsc
        dgpre_ref[...] += dgpre

    vec = _const((1, D))
    rows_ = lambda w_: pl.BlockSpec((R, w_), lambda i: (i, 0))
    return pl.pallas_call(
        body, name=name, grid=(S // R,),
        out_shape=[_sds((S, D), BF), _sds((S, 2 * D_FF), BF), _sds((S, D), F32)] + [_sds((1, D), F32)] * 5,
        in_specs=[rows_(D), rows_(D), rows_(D_FF), rows_(D_FF), _resident((D_FF, D)), _resident((2 * D_FF, D)),
                  rows_(D), vec, vec, vec, vec],
        out_specs=[rows_(D), rows_(2 * D_FF), rows_(D)] + [vec] * 5,
        compiler_params=_cp(1, 56),
    )(dh, y, dsilu_u, silu, w, wt, h, gate, gpost, sc, gpre)


def _tn_matmul(a, b, name, tm=None):
    S, M_all = a.shape
    N = b.shape[1]
    M = M_all if tm is None else tm
    GA = M_all // M
    ts = min(2048 if M * N <= 2 * D * D else 1024, S)
    nk = S // ts
    chunks = [(m0, min(CHUNK, M - m0)) for m0 in range(0, M, CHUNK)]

    def body(a_ref, b_ref, o_ref, acc):
        k = pl.program_id(1)

        @pl.when(k == 0)
        def _():
            acc[...] = jnp.zeros_like(acc)

        for m0, mc in chunks:
            acc[m0:m0 + mc, :] += _dot_tn(a_ref[:, m0:m0 + mc], b_ref[...])

        @pl.when(k == nk - 1)
        def _():
            for m0, mc in chunks:
                o_ref[m0:m0 + mc, :] = acc[m0:m0 + mc, :].astype(BF)

    return pl.pallas_call(
        body, name=name, grid=(GA, nk),
        out_shape=_sds((M_all, N), BF),
        in_specs=[pl.BlockSpec((ts, M), lambda ga, k: (k, ga)), pl.BlockSpec((ts, N), lambda ga, k: (k, 0))],
        out_specs=pl.BlockSpec((M, N), lambda ga, k: (ga, 0)),
        scratch_shapes=[pltpu.VMEM((M, N), F32)],
        compiler_params=_cp(2, 56),
    )(a, b)


def _mix_in(h, sh, sc, gp, w, wq):
    S = h.shape[0]
    R = min(512, S)

    def body(h_ref, sh_ref, sc_ref, gp_ref, w_ref, wq_ref, n_ref, qkv_ref, zg_ref, gates_ref):
        for r0 in range(0, R, CHUNK):
            rows = slice(r0, r0 + CHUNK)
            nb = _prenorm(h_ref[rows, :], gp_ref[...], sc_ref[...], sh_ref[...]).astype(BF)
            n_ref[rows, :] = nb
            qkv_ref[rows, 0:Q_W] = _dot_nt(nb, wq_ref[...]).astype(BF)
            qkv_ref[rows, Q_W:QKV_W] = _dot_nt(nb, w_ref[Q_W:QKV_W, :]).astype(BF)
            zg_ref[rows, :] = _dot_nt(nb, w_ref[ZG_OFF:GATE_OFF, :]).astype(BF)
            gates_ref[rows, :] = jax.nn.sigmoid(_dot_nt(nb, w_ref[GATE_OFF:IN_W, :])).astype(BF)

    vec = _const((1, D))
    rows = lambda w_: pl.BlockSpec((R, w_), lambda i: (i, 0))
    return pl.pallas_call(
        body, name="mix_in", grid=(S // R,),
        out_shape=[_sds((S, D), BF), _sds((S, QKV_W), BF), _sds((S, 2 * G_W), BF), _sds((S, 2 * D), BF)],
        in_specs=[rows(D), vec, vec, vec, _resident((IN_W, D)), _resident((Q_W, D))],
        out_specs=[rows(D), rows(QKV_W), rows(2 * G_W), rows(2 * D)],
        compiler_params=_cp(1, 48),
    )(h, sh, sc, gp, w, wq)


def _bias_table(rel_bias, bucket):
    def body(rel_ref, bk_ref, out_ref):
        bk = bk_ref[...]
        qi = lax.broadcasted_iota(jnp.int32, (BLK, 2 * BLK), 0)
        kj = lax.broadcasted_iota(jnp.int32, (BLK, 2 * BLK), 1)
        dist = qi + BLK - kj
        window = (dist >= 0) & (dist < BLK)
        for h in range(N_HEADS):
            acc = jnp.zeros((BLK, 2 * BLK), F32)
            for b in range(N_BUCKETS):
                acc = jnp.where(bk == b, rel_ref[b, h], acc)
            out_ref[h // GROUP, pl.ds((h % GROUP) * BLK, BLK), :] = jnp.where(window, acc, NEG)

    return pl.pallas_call(
        body, name="bias_table",
        out_shape=_sds((N_KV, GROUP * BLK, 2 * BLK), F32),
        in_specs=[pl.BlockSpec(memory_space=pltpu.SMEM), pl.BlockSpec(memory_space=pltpu.VMEM)],
        out_specs=pl.BlockSpec(memory_space=pltpu.VMEM),
    )(rel_bias, bucket)


ATT_TB = 4


HEAD_ROWS = N_HEADS * BLK


def _pair_heads(w):
    return jnp.transpose(w.reshape(N_KV, GROUP, HD, w.shape[1]), (1, 0, 2, 3)).reshape(w.shape)


def _unpair_heads(w):
    return jnp.transpose(w.reshape(GROUP, N_KV, HD, w.shape[1]), (1, 0, 2, 3)).reshape(w.shape)


def _halves(x):
    low = lax.broadcasted_iota(jnp.int32, x.shape, 1) < HD
    xf = x.astype(F32)
    return jnp.where(low, xf, 0.0).astype(BF), jnp.where(low, 0.0, xf).astype(BF)


def _stack_heads(x):
    halves = [_halves(x[:, g * 128:(g + 1) * 128]) for g in range(GROUP)]
    return jnp.concatenate([lo for lo, _ in halves] + [hi for _, hi in halves], axis=0)


def _attn_probs(q, kvc, kvp, bias_ref, sink_ref, has_prev):
    kv2 = jnp.concatenate([kvp, kvc], axis=0)
    kboth, vboth = kv2[:, 0:KV_W], kv2[:, KV_W:2 * KV_W]
    qpad = _stack_heads(q)
    s = _dot_nt(qpad, kboth) * SCALE + bias_ref[...]
    if has_prev is not None:
        col = lax.broadcasted_iota(jnp.int32, (HEAD_ROWS, 2 * BLK), 1)
        s = jnp.where((col >= BLK) | has_prev, s, NEG)
    row_head = lax.broadcasted_iota(jnp.int32, (HEAD_ROWS, 1), 0) // BLK
    sink = jnp.zeros((HEAD_ROWS, 1), F32)
    for h in range(N_HEADS):
        sink = jnp.where(row_head == h, sink_ref[h], sink)
    m = jnp.maximum(jnp.max(s, axis=1, keepdims=True), sink)
    p = jnp.exp(s - m)
    e_sink = jnp.exp(sink - m)
    inv = 1.0 / (jnp.sum(p, axis=1, keepdims=True) + e_sink)
    return qpad, kboth, vboth, p * inv, e_sink * inv


def _attn_fwd(qkv, bias, sinks):
    S = qkv.shape[0]
    tb = min(ATT_TB, S // BLK)
    T = tb * BLK

    def body(sink_ref, q_ref, kv_ref, kvp_ref, bias_ref, o_ref):
        step = pl.program_id(0)
        for j in range(tb):
            rows = slice(j * BLK, (j + 1) * BLK)
            kvp = kvp_ref[...] if j == 0 else kv_ref[(j - 1) * BLK:j * BLK, :]
            has_prev = (step > 0) if j == 0 else None
            _, _, vboth, prob, _ = _attn_probs(q_ref[rows, :], kv_ref[rows, :], kvp, bias_ref, sink_ref, has_prev)
            pb = prob.astype(BF)
            v_low, v_high = _halves(vboth)
            half = HEAD_ROWS // 2
            o = _dot(pb[0:half], v_low) + _dot(pb[half:HEAD_ROWS], v_high)
            for g in range(GROUP):
                o_ref[rows, g * 128:(g + 1) * 128] = o[g * BLK:(g + 1) * BLK].astype(BF)

    return pl.pallas_call(
        body, name="attn_fwd", grid=(S // T,),
        out_shape=_sds((S, Q_W), BF),
        in_specs=[pl.BlockSpec(memory_space=pltpu.SMEM),
                  pl.BlockSpec((T, Q_W), lambda i: (i, 0)),
                  pl.BlockSpec((T, 2 * KV_W), lambda i: (i, 2)),
                  pl.BlockSpec((BLK, 2 * KV_W), lambda i: (jnp.maximum(i * tb - 1, 0), 2)),
                  _const((HEAD_ROWS, 2 * BLK))],
        out_specs=pl.BlockSpec((T, Q_W), lambda i: (i, 0)),
        compiler_params=_cp(1, 32),
    )(sinks, qkv, qkv, qkv, bias)


def _attn_bwd(qkv, bias, sinks, do):
    S = qkv.shape[0]
    tb = min(2, S // BLK)
    T = tb * BLK
    nt = S // T
    half = HEAD_ROWS // 2

    def body(sink_ref, q_ref, kv_ref, kvp_ref, bias_ref, do_ref, dq_ref, dkv_ref, dbias_ref, dsink_ref, carry):
        i = pl.program_id(0)

        @pl.when(i == 0)
        def _():
            carry[...] = jnp.zeros_like(carry)
            dbias_ref[...] = jnp.zeros_like(dbias_ref)
            dsink_ref[...] = jnp.zeros_like(dsink_ref)

        from_next = carry[...]
        head_row = lax.broadcasted_iota(jnp.int32, (N_HEADS, 128), 0)
        low = lax.broadcasted_iota(jnp.int32, (BLK, 128), 1) < HD
        for j in reversed(range(tb)):
            rows = slice(j * BLK, (j + 1) * BLK)
            kvp = kvp_ref[...] if j == 0 else kv_ref[(j - 1) * BLK:j * BLK, :]
            has_prev = (i < nt - 1) if j == 0 else None
            qpad, kboth, vboth, prob, p_sink = _attn_probs(q_ref[rows, :], kv_ref[rows, :], kvp, bias_ref, sink_ref,
                                                           has_prev)
            pb = prob.astype(BF)
            dopad = _stack_heads(do_ref[rows, :])
            dp = _dot_nt(dopad, vboth)
            delta = jnp.sum(prob * dp, axis=1, keepdims=True)
            ds = prob * (dp - delta)
            dbias_ref[...] += ds
            sink_term = p_sink * delta
            dsink_rows = jnp.zeros((N_HEADS, 128), F32)
            for h in range(N_HEADS):
                val = -jnp.sum(sink_term[h * BLK:(h + 1) * BLK], axis=0, keepdims=True)
                dsink_rows = jnp.where(head_row == h, val, dsink_rows)
            dsink_ref[...] += dsink_rows
            dsb = ds.astype(BF)
            dqpad = _dot(dsb, kboth) * SCALE
            for g in range(GROUP):
                dq_ref[rows, g * 128:(g + 1) * 128] = jnp.where(
                    low, dqpad[g * BLK:(g + 1) * BLK], dqpad[half + g * BLK:half + (g + 1) * BLK]).astype(BF)
            dkv2 = jnp.concatenate([jnp.transpose(_dot_tn(qpad, dsb)) * SCALE,
                                    jnp.transpose(_dot_tn(dopad, pb))], axis=1)
            dkv_ref[rows, :] = (dkv2[BLK:2 * BLK] + from_next).astype(BF)
            from_next = dkv2[0:BLK]
        carry[...] = from_next

    return pl.pallas_call(
        body, name="attn_bwd", grid=(nt,),
        out_shape=[_sds((S, Q_W), BF), _sds((S, 2 * KV_W), BF),
                   _sds((HEAD_ROWS, 2 * BLK), F32), _sds((N_HEADS, 128), F32)],
        in_specs=[pl.BlockSpec(memory_space=pltpu.SMEM),
                  pl.BlockSpec((T, Q_W), lambda i: (nt - 1 - i, 0)),
                  pl.BlockSpec((T, 2 * KV_W), lambda i: (nt - 1 - i, 2)),
                  pl.BlockSpec((BLK, 2 * KV_W), lambda i: (jnp.maximum((nt - 1 - i) * tb - 1, 0), 2)),
                  _const((HEAD_ROWS, 2 * BLK)),
                  pl.BlockSpec((T, Q_W), lambda i: (nt - 1 - i, 0))],
        out_specs=[pl.BlockSpec((T, Q_W), lambda i: (nt - 1 - i, 0)),
                   pl.BlockSpec((T, 2 * KV_W), lambda i: (nt - 1 - i, 0)),
                   _const((HEAD_ROWS, 2 * BLK)), _const((N_HEADS, 128))],
        scratch_shapes=[pltpu.VMEM((BLK, 2 * KV_W), F32)],
        compiler_params=_cp(1, 32),
    )(sinks, qkv, qkv, qkv, bias, do)


def _rel_bias_grad(dbias, bucket):
    def body(db_ref, bk_ref, out_ref):
        bk = bk_ref[...]
        lane = lax.broadcasted_iota(jnp.int32, (1, 128), 1)
        for h in range(N_HEADS):
            d = db_ref[h // GROUP, pl.ds((h % GROUP) * BLK, BLK), :]
            row = jnp.zeros((1, 128), F32)
            for b in range(N_BUCKETS):
                tot = jnp.sum(jnp.sum(jnp.where(bk == b, d, 0.0), axis=1, keepdims=True), axis=0, keepdims=True)
                row = jnp.where(lane == b, tot, row)
            out_ref[pl.ds(h, 1), :] = row

    vm = pl.BlockSpec(memory_space=pltpu.VMEM)
    return pl.pallas_call(body, name="rel_bias_grad", out_shape=_sds((N_HEADS, 128), F32),
                          in_specs=[vm, vm], out_specs=vm)(dbias, bucket)


def _gmlp_parts(zg, lg_ref, lb_ref):
    z = zg.astype(F32)
    ge = _gelu(z)
    u, vg = ge[:, 0:G_W], ge[:, G_W:2 * G_W]
    mu = jnp.mean(vg, axis=-1, keepdims=True)
    xc = vg - mu
    rstd = lax.rsqrt(jnp.mean(xc * xc, axis=-1, keepdims=True) + EPS)
    xh = xc * rstd
    return z, u, xh, rstd, xh * lg_ref[...] + lb_ref[...]


def _causal_weights(ws_ref, wc):
    t = lax.broadcasted_iota(jnp.int32, (BLK, BLK), 0)
    s = lax.broadcasted_iota(jnp.int32, (BLK, BLK), 1)
    for g in range(N_HEADS):
        wc[g] = jnp.where(s <= t, ws_ref[g], 0.0).astype(BF)


def _spatial(vb, wc, bst_ref, p, low):
    xp = vb[:, p * 128:(p + 1) * 128]
    s0 = _dot(wc[2 * p], xp) + bst_ref[:, 2 * p:2 * p + 1]
    s1 = _dot(wc[2 * p + 1], xp) + bst_ref[:, 2 * p + 1:2 * p + 2]
    return xp, jnp.where(low, s0, s1)


def _gmlp_fwd(zg, lg, lb, ws, bst):
    S = zg.shape[0]
    tb = min(ATT_TB, S // BLK)
    T = tb * BLK

    def body(zg_ref, lg_ref, lb_ref, ws_ref, bst_ref, o_ref, wc):
        @pl.when(pl.program_id(0) == 0)
        def _():
            _causal_weights(ws_ref, wc)
        low = lax.broadcasted_iota(jnp.int32, (BLK, 128), 1) < HD
        for j in range(tb):
            rows = slice(j * BLK, (j + 1) * BLK)
            _, u, _, _, vln = _gmlp_parts(zg_ref[rows, :], lg_ref, lb_ref)
            vb = vln.astype(BF)
            for p in range(4):
                _, sp = _spatial(vb, wc, bst_ref, p, low)
                o_ref[rows, p * 128:(p + 1) * 128] = (u[:, p * 128:(p + 1) * 128] * sp).astype(BF)

    return pl.pallas_call(
        body, name="gmlp_fwd", grid=(S // T,),
        out_shape=_sds((S, G_W), BF),
        in_specs=[pl.BlockSpec((T, 2 * G_W), lambda i: (i, 0)), _const((1, G_W)), _const((1, G_W)),
                  _const((N_HEADS, BLK, BLK)), _const((BLK, N_HEADS))],
        out_specs=pl.BlockSpec((T, G_W), lambda i: (i, 0)),
        scratch_shapes=[pltpu.VMEM((N_HEADS, BLK, BLK), BF)],
        compiler_params=_cp(1, 32),
    )(zg, lg, lb, ws, bst)


def _gmlp_bwd(zg, d_out, lg, lb, ws, bst):
    S = zg.shape[0]
    tb = min(ATT_TB, S // BLK)
    T = tb * BLK
    nb = S // T

    def body(zg_ref, d_ref, lg_ref, lb_ref, ws_ref, bst_ref, dzg_ref, dws_ref, dbs_ref, dlg_ref, dlb_ref, wc, dbacc):
        i = pl.program_id(0)

        @pl.when(i == 0)
        def _():
            _causal_weights(ws_ref, wc)
            dws_ref[...] = jnp.zeros_like(dws_ref)
            dlg_ref[...] = jnp.zeros_like(dlg_ref)
            dlb_ref[...] = jnp.zeros_like(dlb_ref)
            dbacc[...] = jnp.zeros_like(dbacc)

        low = lax.broadcasted_iota(jnp.int32, (BLK, 128), 1) < HD
        for j in range(tb):
            rows = slice(j * BLK, (j + 1) * BLK)
            z, u, xh, rstd, vln = _gmlp_parts(zg_ref[rows, :], lg_ref, lb_ref)
            vb = vln.astype(BF)
            d = d_ref[rows, :].astype(F32)
            du_parts, dvln_parts = [], []
            for p in range(4):
                xp, sp = _spatial(vb, wc, bst_ref, p, low)
                dp = d[:, p * 128:(p + 1) * 128]
                du_parts.append(dp * sp)
                dsp = dp * u[:, p * 128:(p + 1) * 128]
                dbacc[:, p * 128:(p + 1) * 128] += dsp
                d0 = jnp.where(low, dsp, 0.0).astype(BF)
                d1 = jnp.where(low, 0.0, dsp).astype(BF)
                dws_ref[2 * p] += _dot_nt(d0, xp)
                dws_ref[2 * p + 1] += _dot_nt(d1, xp)
                dvln_parts.append(_dot_tn(wc[2 * p], d0) + _dot_tn(wc[2 * p + 1], d1))
            dvln = jnp.concatenate(dvln_parts, axis=1)
            dlg_ref[...] += _colsum(dvln * xh)
            dlb_ref[...] += _colsum(dvln)
            dxh = dvln * lg_ref[...]
            dvg = rstd * (dxh - jnp.mean(dxh, axis=-1, keepdims=True)
                          - xh * jnp.mean(dxh * xh, axis=-1, keepdims=True))
            dge = jnp.concatenate(du_parts + [dvg], axis=1)
            dzg_ref[rows, :] = (dge * _gelu_grad(z)).astype(BF)

        @pl.when(i == nb - 1)
        def _():
            t = lax.broadcasted_iota(jnp.int32, (BLK, BLK), 0)
            s = lax.broadcasted_iota(jnp.int32, (BLK, BLK), 1)
            for g in range(N_HEADS):
                dws_ref[g] = jnp.where(s <= t, dws_ref[g], 0.0)
            grp = lax.broadcasted_iota(jnp.int32, (N_HEADS, G_W), 0)
            lane = lax.broadcasted_iota(jnp.int32, (N_HEADS, G_W), 1) // HD
            pick = jnp.where(grp == lane, 1.0, 0.0).astype(F32)
            dbs_ref[...] = lax.dot_general(pick, dbacc[...], (((1,), (1,)), ((), ())),
                                           preferred_element_type=F32, precision=HIGH)

    return pl.pallas_call(
        body, name="gmlp_bwd", grid=(nb,),
        out_shape=[_sds((S, 2 * G_W), BF), _sds((N_HEADS, BLK, BLK), F32), _sds((N_HEADS, BLK), F32),
                   _sds((1, G_W), F32), _sds((1, G_W), F32)],
        in_specs=[pl.BlockSpec((T, 2 * G_W), lambda i: (i, 0)), pl.BlockSpec((T, G_W), lambda i: (i, 0)),
                  _const((1, G_W)), _const((1, G_W)), _const((N_HEADS, BLK, BLK)), _const((BLK, N_HEADS))],
        out_specs=[pl.BlockSpec((T, 2 * G_W), lambda i: (i, 0)), _const((N_HEADS, BLK, BLK)),
                   _const((N_HEADS, BLK)), _const((1, G_W)), _const((1, G_W))],
        scratch_shapes=[pltpu.VMEM((N_HEADS, BLK, BLK), BF), pltpu.VMEM((BLK, G_W), F32)],
        compiler_params=_cp(1, 32),
    )(zg, d_out, lg, lb, ws, bst)


def _mix_out(o, gm, gates, h, wa, wg, wo, gate, gp):
    S = h.shape[0]
    R = min(512, S)

    def body(o_ref, gm_ref, gates_ref, h_ref, wa_ref, wg_ref, wo_ref, gate_ref, gp_ref,
             ya_ref, yg_ref, ym_ref, y_ref, hn_ref):
        for r0 in range(0, R, CHUNK):
            rows = slice(r0, r0 + CHUNK)
            ya = _dot(o_ref[rows, :], wa_ref[...])
            yg = _dot(gm_ref[rows, :], wg_ref[...])
            ya_ref[rows, :] = ya.astype(BF)
            yg_ref[rows, :] = yg.astype(BF)
            ym = (gates_ref[rows, 0:D].astype(F32) * ya + gates_ref[rows, D:2 * D].astype(F32) * yg).astype(BF)
            ym_ref[rows, :] = ym
            y = _dot(ym, wo_ref[...])
            y_ref[rows, :] = y.astype(BF)
            hn_ref[rows, :] = h_ref[rows, :] + gate_ref[...] * (y * _rms_r(y) * gp_ref[...])

    vec = _const((1, D))
    rows = lambda w_: pl.BlockSpec((R, w_), lambda i: (i, 0))
    return pl.pallas_call(
        body, name="mix_out", grid=(S // R,),
        out_shape=[_sds((S, D), BF)] * 4 + [_sds((S, D), F32)],
        in_specs=[rows(Q_W), rows(G_W), rows(2 * D), rows(D), _resident((Q_W, D)), _resident((G_W, D)),
                  _resident((D, D)), vec, vec],
        out_specs=[rows(D)] * 5,
        compiler_params=_cp(1, 48),
    )(o, gm, gates, h, wa, wg, wo, gate, gp)


def _mix_out_bwd(dh, y, ya, yg, gates, att, gm, ymix, wa, wg, wo, gate, gp):
    S = dh.shape[0]
    R = min(512, S)
    nb = S // R

    def body(dh_ref, y_ref, ya_ref, yg_ref, gates_ref, att_ref, gm_ref, ym_ref, wa_ref, wg_ref, wo_ref,
             gate_ref, gp_ref, dz_ref, do_ref, dgm_ref, dgate_ref, dgp_ref, gwo_ref, gwa_ref, gwg_ref,
             acc_o, acc_a, acc_g, dy_scr, dya_scr, dyg_scr):
        i = pl.program_id(0)

        @pl.when(i == 0)
        def _():
            for r in (dgate_ref, dgp_ref, acc_o, acc_a, acc_g):
                r[...] = jnp.zeros_like(r)
        for r0 in range(0, R, CHUNK):
            rows = slice(r0, r0 + CHUNK)
            dy, dgate, dgp = _postnorm_bwd(dh_ref[rows, :], y_ref[rows, :], gate_ref[...], gp_ref[...], 1.0)
            dgate_ref[...] += dgate
            dgp_ref[...] += dgp
            dyb = dy.astype(BF)
            dy_scr[rows, :] = dyb
            dym = _dot_nt(dyb, wo_ref[...])
            ga = gates_ref[rows, 0:D].astype(F32)
            gg = gates_ref[rows, D:2 * D].astype(F32)
            dya = (dym * ga).astype(BF)
            dyg = (dym * gg).astype(BF)
            dya_scr[rows, :] = dya
            dyg_scr[rows, :] = dyg
            dz_ref[rows, 0:D] = (dym * ya_ref[rows, :].astype(F32) * (ga * (1.0 - ga))).astype(BF)
            dz_ref[rows, D:2 * D] = (dym * yg_ref[rows, :].astype(F32) * (gg * (1.0 - gg))).astype(BF)
            do_ref[rows, :] = _dot_nt(dya, wa_ref[...]).astype(BF)
            dgm_ref[rows, :] = _dot_nt(dyg, wg_ref[...]).astype(BF)
        for m0 in range(0, D, CHUNK):
            acc_o[m0:m0 + CHUNK, :] += _dot_tn(ym_ref[:, m0:m0 + CHUNK], dy_scr[...])
        for m0 in range(0, Q_W, CHUNK):
            acc_a[m0:m0 + CHUNK, :] += _dot_tn(att_ref[:, m0:m0 + CHUNK], dya_scr[...])
            acc_g[m0:m0 + CHUNK, :] += _dot_tn(gm_ref[:, m0:m0 + CHUNK], dyg_scr[...])

        @pl.when(i == nb - 1)
        def _():
            for m0 in range(0, D, CHUNK):
                gwo_ref[m0:m0 + CHUNK, :] = acc_o[m0:m0 + CHUNK, :].astype(BF)
            for m0 in range(0, Q_W, CHUNK):
                gwa_ref[m0:m0 + CHUNK, :] = acc_a[m0:m0 + CHUNK, :].astype(BF)
                gwg_ref[m0:m0 + CHUNK, :] = acc_g[m0:m0 + CHUNK, :].astype(BF)

    vec = _const((1, D))
    rows = lambda w_: pl.BlockSpec((R, w_), lambda i: (i, 0))
    return pl.pallas_call(
        body, name="mix_out_bwd", grid=(nb,),
        out_shape=[_sds((S, 2 * D), BF), _sds((S, Q_W), BF), _sds((S, G_W), BF), _sds((1, D), F32),
                   _sds((1, D), F32), _sds((D, D), BF), _sds((Q_W, D), BF), _sds((G_W, D), BF)],
        in_specs=[rows(D), rows(D), rows(D), rows(D), rows(2 * D), rows(Q_W), rows(G_W), rows(D),
                  _resident((Q_W, D)), _resident((G_W, D)), _resident((D, D)), vec, vec],
        out_specs=[rows(2 * D), rows(Q_W), rows(G_W), vec, vec, _const((D, D)), _const((Q_W, D)),
                   _const((G_W, D))],
        scratch_shapes=[pltpu.VMEM((D, D), F32), pltpu.VMEM((Q_W, D), F32), pltpu.VMEM((G_W, D), F32)]
        + [pltpu.VMEM((R, D), BF)] * 3,
        compiler_params=_cp(1, 60),
    )(dh, y, ya, yg, gates, att, gm, ymix, wa, wg, wo, gate, gp)


def _mix_dn(dq, dkv, dzg, dzgate, w, wq, h, dh, sc, gp):
    S = h.shape[0]
    R = min(512, S)

    def body(dq_ref, dkv_ref, dzg_ref, dzt_ref, w_ref, wq_ref, h_ref, dh_ref, sc_ref, gp_ref,
             out_ref, dsh_ref, dsc_ref, dgp_ref):
        @pl.when(pl.program_id(0) == 0)
        def _():
            dsh_ref[...] = jnp.zeros_like(dsh_ref)
            dsc_ref[...] = jnp.zeros_like(dsc_ref)
            dgp_ref[...] = jnp.zeros_like(dgp_ref)
        for r0 in range(0, R, CHUNK):
            rows = slice(r0, r0 + CHUNK)
            dn = _dot(dq_ref[rows, :], wq_ref[...])
            dn = dn + _dot(dkv_ref[rows, :], w_ref[Q_W:QKV_W, :])
            dn = dn + _dot(dzg_ref[rows, :], w_ref[ZG_OFF:GATE_OFF, :])
            dn = dn + _dot(dzt_ref[rows, :], w_ref[GATE_OFF:IN_W, :])
            dx, dsh, dsc, dgp = _prenorm_bwd(dn, h_ref[rows, :], gp_ref[...], sc_ref[...])
            out_ref[rows, :] = dh_ref[rows, :] + dx
            dsh_ref[...] += dsh
            dsc_ref[...] += dsc
            dgp_ref[...] += dgp

    vec = _const((1, D))
    rows = lambda w_: pl.BlockSpec((R, w_), lambda i: (i, 0))
    return pl.pallas_call(
        body, name="mix_dn", grid=(S // R,),
        out_shape=[_sds((S, D), F32)] + [_sds((1, D), F32)] * 3,
        in_specs=[rows(Q_W), rows(2 * KV_W), rows(2 * G_W), rows(2 * D), _resident((IN_W, D)),
                  _resident((Q_W, D)), rows(D), rows(D), vec, vec],
        out_specs=[rows(D), vec, vec, vec],
        compiler_params=_cp(1, 48),
    )(dq, dkv, dzg, dzgate, w, wq, h, dh, sc, gp)


def _adamw_math(w, g, m, v):
    m2 = ADAM_B1 * m + (1.0 - ADAM_B1) * g
    v2 = ADAM_B2 * v + (1.0 - ADAM_B2) * (g * g)
    m_hat = m2 / (1.0 - ADAM_B1 ** ADAM_STEP)
    v_hat = v2 / (1.0 - ADAM_B2 ** ADAM_STEP)
    delta = -ADAM_LR * (m_hat / (jnp.sqrt(v_hat) + ADAM_EPS) + ADAM_WD * w)
    return delta, m2, v2


def _row_tile(rows, cols):
    best = None
    for t in range(16, rows + 1, 16):
        if rows % t == 0 and t * cols <= 256 * 1024:
            best = t
    return best if best is not None else rows


def _adamw_sharded(landing, w, m, v, name):
    r, c = w.shape
    tr = _row_tile(r, c)

    def body(l_ref, w_ref, m_ref, v_ref, g_ref, d_ref, m2_ref, v2_ref):
        g = l_ref[0].astype(F32)
        for j in range(1, N_DEV):
            g = g + l_ref[j].astype(F32)
        delta, m2, v2 = _adamw_math(w_ref[...], g, m_ref[...], v_ref[...])
        g_ref[...] = g
        d_ref[...] = delta
        m2_ref[...] = m2
        v2_ref[...] = v2

    row = pl.BlockSpec((tr, c), lambda i: (i, 0))
    return pl.pallas_call(
        body, name=name, grid=(r // tr,),
        out_shape=[_sds((r, c), F32)] * 4,
        in_specs=[pl.BlockSpec((N_DEV, tr, c), lambda i: (0, i, 0)), row, row, row],
        out_specs=[row] * 4,
        compiler_params=_cp(1, 48),
    )(landing, w, m, v)


def _adamw_small(items):
    n = len(items)

    def body(*refs):
        for k in range(n):
            w_ref, g_ref, m_ref, v_ref = refs[4 * k:4 * k + 4]
            outs = refs[4 * n + 3 * k:4 * n + 3 * k + 3]
            for o_ref, val in zip(outs, _adamw_math(w_ref[...], g_ref[...], m_ref[...], v_ref[...])):
                o_ref[...] = val

    vm = pl.BlockSpec(memory_space=pltpu.VMEM)
    flat = pl.pallas_call(
        body, name="adamw_small",
        out_shape=[_sds(it[0].shape, F32) for it in items for _ in range(3)],
        in_specs=[vm] * (4 * n), out_specs=[vm] * (3 * n),
    )(*[a for it in items for a in it])
    return [tuple(flat[3 * k:3 * k + 3]) for k in range(n)]


def _w_ada_update(c8, d_ada, w, m, v):
    tr = 256

    def body(c_ref, d_ref, w_ref, m_ref, v_ref, g_ref, dl_ref, m2_ref, v2_ref):
        cs = c_ref[...]
        cs = cs * jax.nn.sigmoid(cs)
        g = lax.dot_general(cs, d_ref[...], (((0,), (0,)), ((), ())), preferred_element_type=F32, precision=HIGH)
        delta, m2, v2 = _adamw_math(w_ref[...], g, m_ref[...], v_ref[...])
        g_ref[...] = g
        dl_ref[...] = delta
        m2_ref[...] = m2
        v2_ref[...] = v2

    row = pl.BlockSpec((tr, ADA_W), lambda i: (i, 0))
    return pl.pallas_call(
        body, name="w_ada_update", grid=(D // tr,),
        out_shape=[_sds((D, ADA_W), F32)] * 4,
        in_specs=[pl.BlockSpec((N_DEV, tr), lambda i: (0, i)), _const((N_DEV, ADA_W)), row, row, row],
        out_specs=[row] * 4,
        compiler_params=_cp(1, 40),
    )(c8, d_ada, w, m, v)


def _t5_bucket():
    qi = np.arange(BLK, dtype=np.int32)[:, None]
    kj = np.arange(2 * BLK, dtype=np.int32)[None, :]
    dist = np.maximum(qi + BLK - kj, 0)
    max_exact = N_BUCKETS // 2
    d_f = np.maximum(dist, max_exact).astype(np.float32)
    large = max_exact + (np.log(d_f / np.float32(max_exact)) / np.float32(math.log(MAX_DISTANCE / max_exact))
                         * np.float32(N_BUCKETS - max_exact)).astype(np.int32)
    large = np.minimum(large, N_BUCKETS - 1)
    return jnp.asarray(np.where(dist < max_exact, dist, large).astype(np.int32))


def _slabs_of_columns(w):
    r, c8 = w.shape
    return jnp.transpose(w.reshape(r, N_DEV, c8 // N_DEV), (1, 0, 2))


def _columns_of_slabs(w8):
    _, r, c = w8.shape
    return jnp.transpose(w8, (1, 0, 2)).reshape(r, N_DEV * c)


def kernel(x, c, rel_bias, w_ada, b_ada, pre_norm_g, post_norm_g, w_ffn1_in, w_ffn1_out, w_in, sinks, gmlp_ln_g, gmlp_ln_b, gmlp_w_s, gmlp_b_s, w_br_attn, w_br_gmlp, w_out, w_ffn2_in, w_ffn2_out, loss_target, m_rel_bias, m_w_ada, m_b_ada, m_pre_norm_g, m_post_norm_g, m_w_ffn1_in, m_w_ffn1_out, m_w_in, m_sinks, m_gmlp_ln_g, m_gmlp_ln_b, m_gmlp_w_s, m_gmlp_b_s, m_w_br_attn, m_w_br_gmlp, m_w_out, m_w_ffn2_in, m_w_ffn2_out, v_rel_bias, v_w_ada, v_b_ada, v_pre_norm_g, v_post_norm_g, v_w_ffn1_in, v_w_ffn1_out, v_w_in, v_sinks, v_gmlp_ln_g, v_gmlp_ln_b, v_gmlp_w_s, v_gmlp_b_s, v_w_br_attn, v_w_br_gmlp, v_w_out, v_w_ffn2_in, v_w_ffn2_out):
    me = 4 * lax.axis_index("x") + 2 * lax.axis_index("y") + lax.axis_index("c")
    x0 = x[0]
    target = loss_target[0]

    transposed = ("w_ffn1_in", "w_in", "w_ffn2_in")
    shards = [w_ffn1_in[0].T, w_ffn1_out[0], w_in[0].T, w_br_attn[0], w_br_gmlp[0], w_out[0],
              w_ffn2_in[0].T, w_ffn2_out[0]]
    shards_bf = [s.astype(BF) for s in shards]
    groups = [shards_bf[0:1], shards_bf[1:6], shards_bf[6:8]]

    def gather_start(i, after):
        return _slabs_start("gather", groups[i], after, "gather_start_%d" % i)

    def forward_start(st, i, after):
        lands = _slabs_wait("gather", len(groups[i]), st, after, "gather_wait_%d" % i)
        return _slabs_start("forward", lands, c, "forward_start_%d" % i)

    def gathered(st, i, after):
        return _slabs_wait("forward", len(groups[i]), st, after, "forward_wait_%d" % i)

    gs0 = gather_start(0, c)

    mine = jnp.concatenate([c[0], pre_norm_g[0].reshape(-1), post_norm_g[0].reshape(-1)])
    small8 = jnp.broadcast_to(mine[None, :], (8, mine.shape[0]))
    b_ada64 = jnp.repeat(b_ada.reshape(N_DEV, ADA_W), 8, axis=0)
    gath, ada64 = _ada_forward(small8, w_ada[0], b_ada64)
    gath8 = gath[::8]
    ada = ada64[::8].reshape(9, D)
    sh1, sc1, g1, sh2, sc2, g2, sh3, sc3, g3 = [ada[k:k + 1] for k in range(9)]
    gains = gath8[:, D:].reshape(N_DEV, 2, 3, 128)
    pre_g = jnp.transpose(gains[:, 0], (1, 0, 2)).reshape(3, D)
    post_g = jnp.transpose(gains[:, 1], (1, 0, 2)).reshape(3, D)
    pre = [pre_g[k:k + 1] for k in range(3)]
    post = [post_g[k:k + 1] for k in range(3)]

    bucket = _t5_bucket()
    bias = _bias_table(rel_bias, bucket).reshape(HEAD_ROWS, 2 * BLK)
    sinks8 = sinks[0]
    lg, lb = gmlp_ln_g, gmlp_ln_b
    ws = gmlp_w_s[0]
    bst = jnp.transpose(gmlp_b_s[0])

    fs0 = forward_start(gs0, 0, sh1)
    gs1 = gather_start(1, fs0[-1])
    wf1_in = gathered(fs0, 0, gs1[-1])[0].reshape(2 * D_FF, D)
    n1, fg1, fu1, fa1 = _ffn_in(x0, sh1, sc1, pre[0], wf1_in, "ffn1_in")
    fs1 = forward_start(gs1, 1, n1)
    gs2 = gather_start(2, fs1[-1])
    mix_w = gathered(fs1, 1, gs2[-1])
    wf1_out = mix_w[0].reshape(D_FF, D)
    w_in_full = mix_w[1].reshape(IN_W, D)
    w_q = _pair_heads(w_in_full[0:Q_W])
    w_bra = _pair_heads(_columns_of_slabs(mix_w[2]))
    w_brg = _columns_of_slabs(mix_w[3])
    w_out_full = mix_w[4].reshape(D, D)
    h1, y1 = _ffn_out(fa1, wf1_out, x0, g1, post[0], "ffn1_out")
    n2, qkv, zg, gates = _mix_in(h1, sh2, sc2, pre[1], w_in_full, w_q)
    att = _attn_fwd(qkv, bias, sinks8)
    gm = _gmlp_fwd(zg, lg, lb, ws, bst)
    fs2 = forward_start(gs2, 2, gm)
    ya, yg, ymix, y2, h2 = _mix_out(att, gm, gates, h1, w_bra, w_brg, w_out_full, g2 + fs2[-1], post[1])
    wf2_in, wf2_out = gathered(fs2, 2, h2)
    wf2_in = wf2_in.reshape(2 * D_FF, D)
    wf2_out = wf2_out.reshape(D_FF, D)
    n3, fg3, fu3, fa3 = _ffn_in(h2, sh3, sc3, pre[2], wf2_in, "ffn2_in")
    dh3, y3, sq = _ffn_out(fa3, wf2_out, h2, g3, post[2], "ffn2_out", target=target)

    def exchange_start(i, arrays):
        return _slabs_start("exchange", arrays, sq, "exchange_start_%d" % i)

    dy3, dgu3, dh2, d_g3, d_post2, d_sh3, d_sc3, d_pre2 = _ffn_bwd(
        dh3, y3, fg3, fu3, wf2_out, wf2_in, h2, g3, post[2], sc3, pre[2], "ffn2_bwd")
    gw_f2_out = _tn_matmul(fa3, dy3, "ffn2_out_wgrad", tm=D_FF // 2).reshape(N_DEV, D_FF // N_DEV, D)
    gw_f2_in = _tn_matmul(dgu3, n3, "ffn2_in_wgrad", tm=D_FF // 2).reshape(N_DEV, FS, D)
    ex1 = exchange_start(1, [gw_f2_out, gw_f2_in])

    dzgate, d_att, d_gm, d_g2, d_post1, gw_out, gw_bra, gw_brg = _mix_out_bwd(
        dh2, y2, ya, yg, gates, att, gm, ymix, w_bra, w_brg, w_out_full, g2 + ex1[-1], post[1])
    ex2 = exchange_start(2, [_slabs_of_columns(_unpair_heads(gw_bra)), _slabs_of_columns(gw_brg),
                             gw_out.reshape(N_DEV, D // N_DEV, D)])
    dq, dkv, dbias, dsink = _attn_bwd(qkv, bias, sinks8, d_att)
    dzg, d_ws, d_bs, d_lg, d_lb = _gmlp_bwd(zg, d_gm, lg, lb, ws, bst)
    d_rel = _rel_bias_grad(dbias.reshape(N_KV, GROUP * BLK, 2 * BLK), bucket)
    early = jnp.concatenate([
        jnp.concatenate([d_lg.reshape(4, 128), d_lb.reshape(4, 128)], axis=0),
        d_bs, d_rel, dsink, d_ws.reshape(N_HEADS * BLK, BLK)], axis=0)
    sm0 = _slabs_start("gather_all", [early], sq, "small_gather_start")
    dh1, d_sh2, d_sc2, d_pre1 = _mix_dn(dq, dkv, dzg, dzgate, w_in_full, w_q, h1, dh2, sc2 + ex2[-1] + sm0[-1],
                                        pre[1])
    gw_in = jnp.concatenate(
        [_unpair_heads(_tn_matmul(dq, n2, "w_in_q_wgrad")), _tn_matmul(dkv, n2, "w_in_kv_wgrad"),
         _tn_matmul(dzg, n2, "w_in_zg_wgrad"), _tn_matmul(dzgate, n2, "w_in_gate_wgrad")],
        axis=0).reshape(N_DEV, IN_W // N_DEV, D)
    ex3 = exchange_start(3, [gw_in])

    dy1, dgu1, d_g1, d_post0 = _ffn_out_bwd(dh1, y1, fg1, fu1, wf1_out, g1 + ex3[-1], post[0], "ffn1_out_bwd")
    gw_f1_out = _tn_matmul(fa1, dy1, "ffn1_out_wgrad", tm=D_FF // 2).reshape(N_DEV, D_FF // N_DEV, D)
    ex4 = exchange_start(4, [gw_f1_out])
    gw_f1_in = _tn_matmul(dgu1, n1, "ffn1_in_wgrad", tm=D_FF // 2).reshape(N_DEV, FS, D)
    ex5 = exchange_start(5, [gw_f1_in])
    grad_x, d_sh1, d_sc1, d_pre0 = _ffn_dn(dgu1, wf1_in, x0, dh1, sc1 + ex4[-1] + ex5[-1], pre[0], "ffn1_dn")

    landed = {}
    for i, (ex, nms) in enumerate([(ex1, ["w_ffn2_out", "w_ffn2_in"]),
                                   (ex2, ["w_br_attn", "w_br_gmlp", "w_out"]), (ex3, ["w_in"]),
                                   (ex4, ["w_ffn1_out"]), (ex5, ["w_ffn1_in"])]):
        for nm, land in zip(nms, _slabs_wait("exchange", len(nms), ex, grad_x, "exchange_wait_%d" % i)):
            landed[nm] = land
    moments = [(m_w_ffn1_in, v_w_ffn1_in), (m_w_ffn1_out, v_w_ffn1_out), (m_w_in, v_w_in),
               (m_w_br_attn, v_w_br_attn), (m_w_br_gmlp, v_w_br_gmlp), (m_w_out, v_w_out),
               (m_w_ffn2_in, v_w_ffn2_in), (m_w_ffn2_out, v_w_ffn2_out)]
    names = ["w_ffn1_in", "w_ffn1_out", "w_in", "w_br_attn", "w_br_gmlp", "w_out", "w_ffn2_in", "w_ffn2_out"]
    big = {}
    for nm, w_, (m_, v_) in zip(names, shards, moments):
        if nm in transposed:
            res4 = _adamw_sharded(landed[nm], w_, m_[0].T, v_[0].T, "adamw_" + nm)
            big[nm] = [a.T[None] for a in res4]
        else:
            big[nm] = [a[None] for a in _adamw_sharded(landed[nm], w_, m_[0], v_[0], "adamw_" + nm)]

    my_loss = jnp.broadcast_to(sq * (0.5 / D), (1, D))
    my_loss, _ = lax.optimization_barrier((my_loss, landed["w_ffn1_in"]))
    tot, every = _small_allreduce([d_sh1, d_sc1, d_g1, d_sh2, d_sc2, d_g2, d_sh3, d_sc3, d_g3,
                                   d_pre0, d_pre1, d_pre2, d_post0, d_post1, d_post2, my_loss])
    (early_land,) = _slabs_wait("gather_all", 1, sm0, grad_x, "small_gather_wait")
    tot_early = _sum_slabs(early_land)

    loss = tot[15, 0]
    g_b_ada = tot[0:9].reshape(1, 9 * D)
    g_pre = lax.dynamic_slice_in_dim(tot[9:12], 128 * me, 128, axis=1)[None]
    g_post = lax.dynamic_slice_in_dim(tot[12:15], 128 * me, 128, axis=1)[None]
    g_lg = tot_early[0:4].reshape(1, G_W)
    g_lb = tot_early[4:8].reshape(1, G_W)
    g_bs = tot_early[8:16][None]
    g_rel = jnp.transpose(tot_early[16:24, 0:N_BUCKETS])
    g_sinks = tot_early[24:32, 0][None]
    g_ws = tot_early[32:1056].reshape(1, N_HEADS, BLK, BLK)

    d_ada_mine = lax.dynamic_slice_in_dim(every[:, 0:9].reshape(N_DEV, 9 * D), ADA_W * me, ADA_W, axis=1)
    ada_out = [a[None] for a in _w_ada_update(gath8[:, 0:D], d_ada_mine, w_ada[0], m_w_ada[0], v_w_ada[0])]

    small = [("rel_bias", rel_bias, g_rel, m_rel_bias, v_rel_bias), ("b_ada", b_ada, g_b_ada, m_b_ada, v_b_ada),
             ("pre_norm_g", pre_norm_g, g_pre, m_pre_norm_g, v_pre_norm_g),
             ("post_norm_g", post_norm_g, g_post, m_post_norm_g, v_post_norm_g),
             ("sinks", sinks, g_sinks, m_sinks, v_sinks), ("gmlp_ln_g", gmlp_ln_g, g_lg, m_gmlp_ln_g, v_gmlp_ln_g),
             ("gmlp_ln_b", gmlp_ln_b, g_lb, m_gmlp_ln_b, v_gmlp_ln_b),
             ("gmlp_w_s", gmlp_w_s, g_ws, m_gmlp_w_s, v_gmlp_w_s), ("gmlp_b_s", gmlp_b_s, g_bs, m_gmlp_b_s, v_gmlp_b_s)]
    two_d = lambda a: a.reshape(int(math.prod(a.shape[:-1])), a.shape[-1])
    stepped = _adamw_small([tuple(two_d(a) for a in item[1:]) for item in small])
    res = {"w_ada": ada_out}
    for (nm, w_, g_, _, _), new in zip(small, stepped):
        res[nm] = [g_] + [a.reshape(w_.shape) for a in new]
    res.update(big)
    order = ["rel_bias", "w_ada", "b_ada", "pre_norm_g", "post_norm_g", "w_ffn1_in", "w_ffn1_out", "w_in", "sinks",
             "gmlp_ln_g", "gmlp_ln_b", "gmlp_w_s", "gmlp_b_s", "w_br_attn", "w_br_gmlp", "w_out", "w_ffn2_in",
             "w_ffn2_out"]
    outs = [loss, grad_x[None]]
    for k in range(4):
        outs += [res[nm][k] for nm in order]
    return tuple(outs)
```

```python
import functools
import math

import jax
import jax.numpy as jnp
import numpy as np
from jax import lax
from jax.experimental import pallas as pl
from jax.experimental.pallas import tpu as pltpu

F32 = jnp.float32
BF = jnp.bfloat16

N_DEV = 8
D = 1024
D_FF = 2816
FS = D_FF // 4
N_HEADS = 8
N_KV = 2
GROUP = 4
HD = 64
BLK = 128
Q_W = 512
KV_W = 128
G_W = 512
QKV_W = Q_W + 2 * KV_W
ZG_OFF = QKV_W
GATE_OFF = ZG_OFF + 2 * G_W
IN_W = GATE_OFF + 2 * D
N_BUCKETS = 32
MAX_DISTANCE = 128
EPS = 1e-6
NEG = -1e30
SCALE = HD ** -0.5
ADA_W = 9 * D // N_DEV

ADAM_LR = 0.001
ADAM_B1 = 0.9
ADAM_B2 = 0.999
ADAM_EPS = 1e-08
ADAM_WD = 0.01
ADAM_STEP = 10

CHUNK = 256
MIB = 1024 * 1024
MESH = pl.DeviceIdType.MESH
HIGH = lax.Precision.HIGHEST


def _cp(n_grid, vmem_mib):
    return pltpu.CompilerParams(dimension_semantics=("arbitrary",) * n_grid,
                                vmem_limit_bytes=vmem_mib * MIB)


def _const(shape):
    return pl.BlockSpec(shape, lambda *_: (0,) * len(shape))


def _resident(shape):
    return pl.BlockSpec(shape, lambda *_: (0,) * len(shape), pipeline_mode=pl.Buffered(1))


def _sds(shape, dtype):
    return jax.ShapeDtypeStruct(shape, dtype)


def _dot(a, b):
    return jnp.dot(a, b, preferred_element_type=F32)


def _dot_nt(a, b):
    return lax.dot_general(a, b, (((1,), (1,)), ((), ())), preferred_element_type=F32)


def _dot_tn(a, b):
    return lax.dot_general(a, b, (((0,), (0,)), ((), ())), preferred_element_type=F32)


def _rms_r(x):
    return lax.rsqrt(jnp.mean(x * x, axis=-1, keepdims=True) + EPS)


def _colsum(x):
    return jnp.sum(x, axis=0, keepdims=True)


def _prenorm(x, gp, sc, sh):
    return (x * _rms_r(x) * gp) * (1.0 + sc) + sh


def _prenorm_bwd(dn, x, gp, sc):
    r = _rms_r(x)
    xh = x * r
    t = dn * (1.0 + sc) * gp
    dx = r * (t - xh * jnp.mean(t * xh, axis=-1, keepdims=True))
    return dx, _colsum(dn), _colsum(dn * xh * gp), _colsum(dn * (1.0 + sc) * xh)


def _postnorm_bwd(dh, y, gate, gp, res):
    y = y.astype(F32)
    r = _rms_r(y)
    yh = y * r
    dyn = (res * gate) * dh
    t = dyn * gp
    dy = r * (t - yh * jnp.mean(t * yh, axis=-1, keepdims=True))
    return dy, _colsum(res * dh * yh * gp), _colsum(dyn * yh)


def _gelu(x):
    k = math.sqrt(2.0 / math.pi)
    return 0.5 * x * (1.0 + jnp.tanh(k * (x + 0.044715 * x * x * x)))


def _gelu_grad(x):
    k = math.sqrt(2.0 / math.pi)
    t = jnp.tanh(k * (x + 0.044715 * x * x * x))
    return 0.5 * (1.0 + t) + 0.5 * x * (1.0 - t * t) * (k * (1.0 + 3.0 * 0.044715 * x * x))


def _my_place():
    x, y, c = lax.axis_index("x"), lax.axis_index("y"), lax.axis_index("c")
    return x, y, c, 4 * x + 2 * y + c


def _peer(x, y, c, k):
    px = 1 - x if k & 4 else x
    py = 1 - y if k & 2 else y
    pc = 1 - c if k & 1 else c
    return (px, py, pc), 4 * px + 2 * py + pc


HBM_SPEC = pl.BlockSpec(memory_space=pltpu.HBM)
SEM_SPEC = pl.BlockSpec(memory_space=pltpu.SEMAPHORE)
EFFECT = pltpu.SideEffectType.DATAFLOW_SIDE_EFFECTING


RELATIONS = {"exchange": (1, 2, 3, 4, 5, 6, 7), "gather": (1, 2, 4, 6), "forward": (2, 4, 6),
             "gather_all": (1, 2, 3, 4, 5, 6, 7)}


def _slab_copies(mode, srcs, lands, send, recv, loc):
    x, y, c, me = _my_place()
    rel = RELATIONS[mode]
    remote, local = [], []
    for t in range(len(lands)):
        for i, k in enumerate(rel):
            peer, peer_lin = _peer(x, y, c, k)
            if mode == "exchange":
                src, dst, to = srcs[t].at[peer_lin], lands[t].at[me], peer
            elif mode in ("gather", "gather_all"):
                src, dst, to = srcs[t], lands[t].at[me], peer
            else:
                src, dst, to = lands[t].at[peer_lin], lands[t].at[peer_lin], _peer(x, y, c, 1)[0]
            remote.append(pltpu.make_async_remote_copy(
                src_ref=src, dst_ref=dst, send_sem=send.at[t * len(rel) + i], recv_sem=recv.at[t * len(rel) + i],
                device_id=to, device_id_type=MESH))
        if mode == "exchange":
            local.append(pltpu.make_async_copy(srcs[t].at[me], lands[t].at[me], loc.at[t]))
        elif mode in ("gather", "gather_all"):
            local.append(pltpu.make_async_copy(srcs[t], lands[t].at[me], loc.at[t]))
    return remote, local


def _slabs_start(mode, arrays, after, name):
    n = len(arrays)
    if mode == "forward":
        thru = list(arrays)
    else:
        shapes = [a.shape if mode == "exchange" else (N_DEV,) + a.shape for a in arrays]
        thru = list(arrays) + [lax.empty(s, a.dtype) for s, a in zip(shapes, arrays)]
    m = len(thru)
    n_sem = n * len(RELATIONS[mode])

    def body(*refs):
        srcs, lands = refs[:n], refs[m - n:m]
        send, recv, loc = refs[m + 1:m + 4]
        remote, local = _slab_copies(mode, srcs, lands, send, recv, loc)
        for cp in remote + local:
            cp.start()
        refs[-1][...] = jnp.zeros_like(refs[-1])

    return pl.pallas_call(
        body, name=name,
        out_shape=(pltpu.SemaphoreType.DMA((n_sem,)), pltpu.SemaphoreType.DMA((n_sem,)),
                   pltpu.SemaphoreType.DMA((n,)),
                   *[pltpu.HBM(a.shape, a.dtype) for a in thru],
                   _sds((1, D), F32)),
        in_specs=[HBM_SPEC] * m + [pl.BlockSpec(memory_space=pl.ANY)],
        out_specs=(SEM_SPEC, SEM_SPEC, SEM_SPEC, *[HBM_SPEC] * m, pl.BlockSpec(memory_space=pltpu.VMEM)),
        input_output_aliases={t: 3 + t for t in range(m)},
        compiler_params=pltpu.CompilerParams(has_side_effects=EFFECT),
    )(*[pltpu.with_memory_space_constraint(a, pltpu.HBM) for a in thru], after)


def _slabs_wait(mode, n, started, after, name):
    sems = started[0:3]
    thru = started[3:-1]
    m = len(thru)

    def body(*refs):
        srcs, lands = refs[:n], refs[m - n:m]
        remote, local = _slab_copies(mode, srcs, lands, *refs[m:m + 3])
        for cp in remote:
            cp.wait_send()
            cp.wait_recv()
        for cp in local:
            cp.wait()

    res = pl.pallas_call(
        body, name=name,
        out_shape=tuple(pltpu.HBM(a.shape, a.dtype) for a in thru),
        in_specs=[HBM_SPEC] * m + [SEM_SPEC] * 3 + [pl.BlockSpec(memory_space=pl.ANY)],
        out_specs=tuple([HBM_SPEC] * m),
        input_output_aliases={t: t for t in range(m)},
        compiler_params=pltpu.CompilerParams(has_side_effects=EFFECT),
    )(*thru, *sems, after)
    return list(res[m - n:m])


def _ada_forward(small8, w_ada, b_ada64):
    sw = small8.shape[1]

    def body(sm_ref, w_ref, b_ref, gath_ref, ada_ref, part_ref, send1, recv1, send2, recv2):
        x, y, c, me = _my_place()
        row_me = pl.multiple_of(me * 8, 8)
        gath_ref[pl.ds(row_me, 8), :] = sm_ref[...]
        first = []
        for k in range(1, N_DEV):
            peer, _ = _peer(x, y, c, k)
            cp = pltpu.make_async_remote_copy(
                src_ref=sm_ref, dst_ref=gath_ref.at[pl.ds(row_me, 8), :], send_sem=send1.at[k - 1],
                recv_sem=recv1.at[k - 1], device_id=peer, device_id_type=MESH)
            cp.start()
            first.append(cp)
        for cp in first:
            cp.wait()
        cs = gath_ref[:, 0:D]
        cs = cs * jax.nn.sigmoid(cs)
        part_ref[...] = jnp.dot(cs, w_ref[...], preferred_element_type=F32, precision=HIGH)
        ada_ref[pl.ds(row_me, 8), :] = part_ref[pl.ds(row_me, 8), :]
        second = []
        for k in range(1, N_DEV):
            peer, peer_lin = _peer(x, y, c, k)
            cp = pltpu.make_async_remote_copy(
                src_ref=part_ref.at[pl.ds(pl.multiple_of(peer_lin * 8, 8), 8), :],
                dst_ref=ada_ref.at[pl.ds(row_me, 8), :], send_sem=send2.at[k - 1],
                recv_sem=recv2.at[k - 1], device_id=peer, device_id_type=MESH)
            cp.start()
            second.append(cp)
        for cp in second:
            cp.wait()
        ada_ref[...] = ada_ref[...] + b_ref[...]

    vm = pl.BlockSpec(memory_space=pltpu.VMEM)
    return pl.pallas_call(
        body, name="ada_forward",
        out_shape=[_sds((8 * N_DEV, sw), F32), _sds((8 * N_DEV, ADA_W), F32)],
        in_specs=[vm, vm, vm], out_specs=[vm, vm],
        scratch_shapes=[pltpu.VMEM((8 * N_DEV, ADA_W), F32)] + [pltpu.SemaphoreType.DMA((7,))] * 4,
        compiler_params=pltpu.CompilerParams(vmem_limit_bytes=32 * MIB),
    )(small8, w_ada, b_ada64)


def _sum_slabs(land):
    def body(l_ref, o_ref):
        acc = l_ref[0]
        for j in range(1, N_DEV):
            acc = acc + l_ref[j]
        o_ref[...] = acc

    vm = pl.BlockSpec(memory_space=pltpu.VMEM)
    return pl.pallas_call(body, name="sum_slabs", out_shape=_sds(land.shape[1:], F32), in_specs=[vm], out_specs=vm,
                          compiler_params=pltpu.CompilerParams(vmem_limit_bytes=32 * MIB))(land)


def _small_allreduce(vectors):
    n = len(vectors)

    def body(*refs):
        v_refs, (sum_ref, gath_ref, pack, send, recv) = refs[:n], refs[n:]
        x, y, c, me = _my_place()
        for k in range(n):
            pack[k:k + 1, :] = v_refs[k][...]
        gath_ref[me] = pack[...]
        cps = []
        for k in range(1, N_DEV):
            peer, _ = _peer(x, y, c, k)
            cp = pltpu.make_async_remote_copy(
                src_ref=pack, dst_ref=gath_ref.at[me], send_sem=send.at[k - 1],
                recv_sem=recv.at[k - 1], device_id=peer, device_id_type=MESH)
            cp.start()
            cps.append(cp)
        for cp in cps:
            cp.wait()
        acc = gath_ref[0]
        for j in range(1, N_DEV):
            acc = acc + gath_ref[j]
        sum_ref[...] = acc

    vm = pl.BlockSpec(memory_space=pltpu.VMEM)
    return pl.pallas_call(
        body, name="small_allreduce",
        out_shape=[_sds((n, D), F32), _sds((N_DEV, n, D), F32)],
        in_specs=[vm] * n, out_specs=[vm, vm],
        scratch_shapes=[pltpu.VMEM((n, D), F32), pltpu.SemaphoreType.DMA((7,)), pltpu.SemaphoreType.DMA((7,))],
    )(*vectors)


F_TILES = tuple((f0, min(512, D_FF - f0)) for f0 in range(0, D_FF, 512))
F_TILES_NARROW = tuple((f0, 256) for f0 in range(0, D_FF, 256))


def _swiglu_tile(n, wt_ref, f0, tf):
    g = _dot_nt(n, wt_ref[f0:f0 + tf, :])
    u = _dot_nt(n, wt_ref[D_FF + f0:D_FF + f0 + tf, :])
    sg = jax.nn.sigmoid(g)
    silu = g * sg
    return (u * (sg * (1.0 + g * (1.0 - sg)))).astype(BF), silu.astype(BF), (silu * u).astype(BF)


def _ffn_in(h, sh, sc, gp, wt, name):
    S = h.shape[0]
    R = min(512, S)

    def body(h_ref, sh_ref, sc_ref, gp_ref, w_ref, n_ref, dg_ref, sl_ref, a_ref):
        for r0 in range(0, R, CHUNK):
            rows = slice(r0, r0 + CHUNK)
            n = _prenorm(h_ref[rows, :], gp_ref[...], sc_ref[...], sh_ref[...]).astype(BF)
            n_ref[rows, :] = n
            for f0, tf in F_TILES_NARROW:
                dg_ref[rows, f0:f0 + tf], sl_ref[rows, f0:f0 + tf], a_ref[rows, f0:f0 + tf] = _swiglu_tile(
                    n, w_ref, f0, tf)

    vec = _const((1, D))
    rows_ = lambda w_: pl.BlockSpec((R, w_), lambda i: (i, 0))
    return pl.pallas_call(
        body, name=name, grid=(S // R,),
        out_shape=[_sds((S, D), BF)] + [_sds((S, D_FF), BF)] * 3,
        in_specs=[rows_(D), vec, vec, vec, _resident((2 * D_FF, D))],
        out_specs=[rows_(D), rows_(D_FF), rows_(D_FF), rows_(D_FF)],
        compiler_params=_cp(1, 56),
    )(h, sh, sc, gp, wt)


def _ffn_out(a, w, h, gate, gp, name, target=None):
    S = h.shape[0]
    R = min(512, S)
    with_loss = target is not None

    def body(a_ref, w_ref, h_ref, gate_ref, gp_ref, *rest):
        if with_loss:
            t_ref, out_ref, y_ref, tot_ref = rest

            @pl.when(pl.program_id(0) == 0)
            def _():
                tot_ref[...] = jnp.zeros_like(tot_ref)
        else:
            out_ref, y_ref = rest
        for r0 in range(0, R, CHUNK):
            rows = slice(r0, r0 + CHUNK)
            y = _dot(a_ref[rows, :], w_ref[...])
            y_ref[rows, :] = y.astype(BF)
            hn = h_ref[rows, :] + (0.5 * gate_ref[...]) * (y * _rms_r(y) * gp_ref[...])
            if with_loss:
                e = hn - t_ref[rows, :]
                out_ref[rows, :] = e * (1.0 / D)
                tot_ref[...] += jnp.sum(jnp.sum(e * e, axis=1, keepdims=True), axis=0, keepdims=True)
            else:
                out_ref[rows, :] = hn

    vec = _const((1, D))
    rows_ = lambda w_: pl.BlockSpec((R, w_), lambda i: (i, 0))
    return pl.pallas_call(
        body, name=name, grid=(S // R,),
        out_shape=[_sds((S, D), F32), _sds((S, D), BF)] + ([_sds((1, 1), F32)] if with_loss else []),
        in_specs=[rows_(D_FF), _resident((D_FF, D)), rows_(D), vec, vec] + ([rows_(D)] if with_loss else []),
        out_specs=[rows_(D), rows_(D)] + ([_const((1, 1))] if with_loss else []),
        compiler_params=_cp(1, 48),
    )(*((a, w, h, gate, gp) + ((target,) if with_loss else ())))


def _ffn_out_bwd(dh, y, dsilu_u, silu, w, gate, gp, name):
    S = dh.shape[0]
    R = min(512, S)

    def body(dh_ref, y_ref, g_ref, u_ref, w_ref, gate_ref, gp_ref, dy_ref, dgu_ref, dgate_ref, dgp_ref):
        @pl.when(pl.program_id(0) == 0)
        def _():
            dgate_ref[...] = jnp.zeros_like(dgate_ref)
            dgp_ref[...] = jnp.zeros_like(dgp_ref)
        for r0 in range(0, R, CHUNK):
            rows = slice(r0, r0 + CHUNK)
            dy, dgate, dgp = _postnorm_bwd(dh_ref[rows, :], y_ref[rows, :], gate_ref[...], gp_ref[...], 0.5)
            dgate_ref[...] += dgate
            dgp_ref[...] += dgp
            dyb = dy.astype(BF)
            dy_ref[rows, :] = dyb
            for f0, tf in F_TILES:
                da = _dot_nt(dyb, w_ref[f0:f0 + tf, :])
                dgu_ref[rows, f0:f0 + tf] = (da * g_ref[rows, f0:f0 + tf].astype(F32)).astype(BF)
                dgu_ref[rows, D_FF + f0:D_FF + f0 + tf] = (da * u_ref[rows, f0:f0 + tf].astype(F32)).astype(BF)

    vec = _const((1, D))
    rows_ = lambda w_: pl.BlockSpec((R, w_), lambda i: (i, 0))
    return pl.pallas_call(
        body, name=name, grid=(S // R,),
        out_shape=[_sds((S, D), BF), _sds((S, 2 * D_FF), BF), _sds((1, D), F32), _sds((1, D), F32)],
        in_specs=[rows_(D), rows_(D), rows_(D_FF), rows_(D_FF), _resident((D_FF, D)), vec, vec],
        out_specs=[rows_(D), rows_(2 * D_FF), vec, vec],
        compiler_params=_cp(1, 56),
    )(dh, y, dsilu_u, silu, w, gate, gp)


def _ffn_dn(dgu, wt, h, dh, sc, gp, name):
    S = h.shape[0]
    R = min(512, S)

    def body(dgu_ref, w_ref, h_ref, dh_ref, sc_ref, gp_ref, out_ref, dsh_ref, dsc_ref, dgp_ref):
        @pl.when(pl.program_id(0) == 0)
        def _():
            dsh_ref[...] = jnp.zeros_like(dsh_ref)
            dsc_ref[...] = jnp.zeros_like(dsc_ref)
            dgp_ref[...] = jnp.zeros_like(dgp_ref)

        for r0 in range(0, R, CHUNK):
            rows = slice(r0, r0 + CHUNK)
            dn = _dot(dgu_ref[rows, :], w_ref[...])
            dx, dsh, dsc, dgp = _prenorm_bwd(dn, h_ref[rows, :], gp_ref[...], sc_ref[...])
            out_ref[rows, :] = dh_ref[rows, :] + dx
            dsh_ref[...] += dsh
            dsc_ref[...] += dsc
            dgp_ref[...] += dgp

    vec = _const((1, D))
    rows_ = lambda w_: pl.BlockSpec((R, w_), lambda i: (i, 0))
    return pl.pallas_call(
        body, name=name, grid=(S // R,),
        out_shape=[_sds((S, D), F32)] + [_sds((1, D), F32)] * 3,
        in_specs=[rows_(2 * D_FF), _resident((2 * D_FF, D)), rows_(D), rows_(D), vec, vec],
        out_specs=[rows_(D), vec, vec, vec],
        compiler_params=_cp(1, 56),
    )(dgu, wt, h, dh, sc, gp)


def _ffn_bwd(dh, y, dsilu_u, silu, w, wt, h, gate, gpost, sc, gpre, name):
    S = dh.shape[0]
    R = min(256, S)

    def body(dh_ref, y_ref, g_ref, u_ref, w_ref, wt_ref, h_ref, gate_ref, gpost_ref, sc_ref, gpre_ref,
             dy_ref, dgu_ref, out_ref, dgate_ref, dgpost_ref, dsh_ref, dsc_ref, dgpre_ref):
        @pl.when(pl.program_id(0) == 0)
        def _():
            for r in (dgate_ref, dgpost_ref, dsh_ref, dsc_ref, dgpre_ref):
                r[...] = jnp.zeros_like(r)
        dhh = dh_ref[...]
        dy, dgate, dgpost = _postnorm_bwd(dhh, y_ref[...], gate_ref[...], gpost_ref[...], 0.5)
        dgate_ref[...] += dgate
        dgpost_ref[...] += dgpost
        dyb = dy.astype(BF)
        dy_ref[...] = dyb
        dn = None
        for f0, tf in F_TILES:
            da = _dot_nt(dyb, w_ref[f0:f0 + tf, :])
            dg = (da * g_ref[:, f0:f0 + tf].astype(F32)).astype(BF)
            du = (da * u_ref[:, f0:f0 + tf].astype(F32)).astype(BF)
            dgu_ref[:, f0:f0 + tf] = dg
            dgu_ref[:, D_FF + f0:D_FF + f0 + tf] = du
            part = _dot(dg, wt_ref[f0:f0 + tf, :]) + _dot(du, wt_ref[D_FF + f0:D_FF + f0 + tf, :])
            dn = part if dn is None else dn + part
        dx, dsh, dsc, dgpre = _prenorm_bwd(dn, h_ref[...], gpre_ref[...], sc_ref[...])
        out_ref[...] = dhh + dx
        dsh_ref[...] += dsh
        dsc_ref[...] += dsc
        dgpre_ref[...] += dgpre

    vec = _const((1, D))
    rows_ = lambda w_: pl.BlockSpec((R, w_), lambda i: (i, 0))
    return pl.pallas_call(
        body, name=name, grid=(S // R,),
        out_shape=[_sds((S, D), BF), _sds((S, 2 * D_FF), BF), _sds((S, D), F32)] + [_sds((1, D), F32)] * 5,
        in_specs=[rows_(D), rows_(D), rows_(D_FF), rows_(D_FF), _resident((D_FF, D)), _resident((2 * D_FF, D)),
                  rows_(D), vec, vec, vec, vec],
        out_specs=[rows_(D), rows_(2 * D_FF), rows_(D)] + [vec] * 5,
        compiler_params=_cp(1, 56),
    )(dh, y, dsilu_u, silu, w, wt, h, gate, gpost, sc, gpre)


def _tn_matmul(a, b, name, tm=None):
    S, M_all = a.shape
    N = b.shape[1]
    M = M_all if tm is None else tm
    GA = M_all // M
    ts = min(2048 if M * N <= 2 * D * D else 1024, S)
    nk = S // ts
    chunks = [(m0, min(CHUNK, M - m0)) for m0 in range(0, M, CHUNK)]

    def body(a_ref, b_ref, o_ref, acc):
        k = pl.program_id(1)

        @pl.when(k == 0)
        def _():
            acc[...] = jnp.zeros_like(acc)

        for m0, mc in chunks:
            acc[m0:m0 + mc, :] += _dot_tn(a_ref[:, m0:m0 + mc], b_ref[...])

        @pl.when(k == nk - 1)
        def _():
            for m0, mc in chunks:
                o_ref[m0:m0 + mc, :] = acc[m0:m0 + mc, :].astype(BF)

    return pl.pallas_call(
        body, name=name, grid=(GA, nk),
        out_shape=_sds((M_all, N), BF),
        in_specs=[pl.BlockSpec((ts, M), lambda ga, k: (k, ga)), pl.BlockSpec((ts, N), lambda ga, k: (k, 0))],
        out_specs=pl.BlockSpec((M, N), lambda ga, k: (ga, 0)),
        scratch_shapes=[pltpu.VMEM((M, N), F32)],
        compiler_params=_cp(2, 56),
    )(a, b)


def _mix_in(h, sh, sc, gp, w, wq):
    S = h.shape[0]
    R = min(512, S)

    def body(h_ref, sh_ref, sc_ref, gp_ref, w_ref, wq_ref, n_ref, qkv_ref, zg_ref, gates_ref):
        for r0 in range(0, R, CHUNK):
            rows = slice(r0, r0 + CHUNK)
            nb = _prenorm(h_ref[rows, :], gp_ref[...], sc_ref[...], sh_ref[...]).astype(BF)
            n_ref[rows, :] = nb
            qkv_ref[rows, 0:Q_W] = _dot_nt(nb, wq_ref[...]).astype(BF)
            qkv_ref[rows, Q_W:QKV_W] = _dot_nt(nb, w_ref[Q_W:QKV_W, :]).astype(BF)
            zg_ref[rows, :] = _dot_nt(nb, w_ref[ZG_OFF:GATE_OFF, :]).astype(BF)
            gates_ref[rows, :] = jax.nn.sigmoid(_dot_nt(nb, w_ref[GATE_OFF:IN_W, :])).astype(BF)

    vec = _const((1, D))
    rows = lambda w_: pl.BlockSpec((R, w_), lambda i: (i, 0))
    return pl.pallas_call(
        body, name="mix_in", grid=(S // R,),
        out_shape=[_sds((S, D), BF), _sds((S, QKV_W), BF), _sds((S, 2 * G_W), BF), _sds((S, 2 * D), BF)],
        in_specs=[rows(D), vec, vec, vec, _resident((IN_W, D)), _resident((Q_W, D))],
        out_specs=[rows(D), rows(QKV_W), rows(2 * G_W), rows(2 * D)],
        compiler_params=_cp(1, 48),
    )(h, sh, sc, gp, w, wq)


def _bias_table(rel_bias, bucket):
    def body(rel_ref, bk_ref, out_ref):
        bk = bk_ref[...]
        qi = lax.broadcasted_iota(jnp.int32, (BLK, 2 * BLK), 0)
        kj = lax.broadcasted_iota(jnp.int32, (BLK, 2 * BLK), 1)
        dist = qi + BLK - kj
        window = (dist >= 0) & (dist < BLK)
        for h in range(N_HEADS):
            acc = jnp.zeros((BLK, 2 * BLK), F32)
            for b in range(N_BUCKETS):
                acc = jnp.where(bk == b, rel_ref[b, h], acc)
            out_ref[h // GROUP, pl.ds((h % GROUP) * BLK, BLK), :] = jnp.where(window, acc, NEG)

    return pl.pallas_call(
        body, name="bias_table",
        out_shape=_sds((N_KV, GROUP * BLK, 2 * BLK), F32),
        in_specs=[pl.BlockSpec(memory_space=pltpu.SMEM), pl.BlockSpec(memory_space=pltpu.VMEM)],
        out_specs=pl.BlockSpec(memory_space=pltpu.VMEM),
    )(rel_bias, bucket)


ATT_TB = 4


HEAD_ROWS = N_HEADS * BLK


def _pair_heads(w):
    return jnp.transpose(w.reshape(N_KV, GROUP, HD, w.shape[1]), (1, 0, 2, 3)).reshape(w.shape)


def _unpair_heads(w):
    return jnp.transpose(w.reshape(GROUP, N_KV, HD, w.shape[1]), (1, 0, 2, 3)).reshape(w.shape)


def _halves(x, scale=1.0):
    low = lax.broadcasted_iota(jnp.int32, x.shape, 1) < HD
    xf = x.astype(F32) * scale
    return jnp.where(low, xf, 0.0).astype(BF), jnp.where(low, 0.0, xf).astype(BF)


def _stack_heads(x, scale=1.0):
    halves = [_halves(x[:, g * 128:(g + 1) * 128], scale) for g in range(GROUP)]
    return jnp.concatenate([lo for lo, _ in halves] + [hi for _, hi in halves], axis=0)


def _attn_probs(q, kvc, kvp, bias_ref, sink_ref, has_prev):
    kv2 = jnp.concatenate([kvp, kvc], axis=0)
    kboth, vboth = kv2[:, 0:KV_W], kv2[:, KV_W:2 * KV_W]
    qpad = _stack_heads(q, SCALE)
    s = _dot_nt(qpad, kboth) + bias_ref[...]
    if has_prev is not None:
        col = lax.broadcasted_iota(jnp.int32, (HEAD_ROWS, 2 * BLK), 1)
        s = jnp.where((col >= BLK) | has_prev, s, NEG)
    row_head = lax.broadcasted_iota(jnp.int32, (HEAD_ROWS, 1), 0) // BLK
    sink = jnp.zeros((HEAD_ROWS, 1), F32)
    for h in range(N_HEADS):
        sink = jnp.where(row_head == h, sink_ref[h], sink)
    m = jnp.maximum(jnp.max(s, axis=1, keepdims=True), sink)
    p = jnp.exp(s - m)
    e_sink = jnp.exp(sink - m)
    inv = 1.0 / (jnp.sum(p, axis=1, keepdims=True) + e_sink)
    return qpad, kboth, vboth, p * inv, e_sink * inv


def _attn_fwd(qkv, bias, sinks):
    S = qkv.shape[0]
    tb = min(ATT_TB, S // BLK)
    T = tb * BLK

    def body(sink_ref, q_ref, kv_ref, kvp_ref, bias_ref, o_ref):
        step = pl.program_id(0)
        for j in range(tb):
            rows = slice(j * BLK, (j + 1) * BLK)
            kvp = kvp_ref[...] if j == 0 else kv_ref[(j - 1) * BLK:j * BLK, :]
            has_prev = (step > 0) if j == 0 else None
            _, _, vboth, prob, _ = _attn_probs(q_ref[rows, :], kv_ref[rows, :], kvp, bias_ref, sink_ref, has_prev)
            pb = prob.astype(BF)
            v_low, v_high = _halves(vboth)
            half = HEAD_ROWS // 2
            o = _dot(pb[0:half], v_low) + _dot(pb[half:HEAD_ROWS], v_high)
            for g in range(GROUP):
                o_ref[rows, g * 128:(g + 1) * 128] = o[g * BLK:(g + 1) * BLK].astype(BF)

    return pl.pallas_call(
        body, name="attn_fwd", grid=(S // T,),
        out_shape=_sds((S, Q_W), BF),
        in_specs=[pl.BlockSpec(memory_space=pltpu.SMEM),
                  pl.BlockSpec((T, Q_W), lambda i: (i, 0)),
                  pl.BlockSpec((T, 2 * KV_W), lambda i: (i, 2)),
                  pl.BlockSpec((BLK, 2 * KV_W), lambda i: (jnp.maximum(i * tb - 1, 0), 2)),
                  _const((HEAD_ROWS, 2 * BLK))],
        out_specs=pl.BlockSpec((T, Q_W), lambda i: (i, 0)),
        compiler_params=_cp(1, 32),
    )(sinks, qkv, qkv, qkv, bias)


def _attn_bwd(qkv, bias, sinks, do):
    S = qkv.shape[0]
    tb = min(ATT_TB, S // BLK)
    T = tb * BLK
    nt = S // T
    half = HEAD_ROWS // 2

    def body(sink_ref, q_ref, kv_ref, kvp_ref, bias_ref, do_ref, dq_ref, dkv_ref, dbias_ref, dsink_ref, carry):
        i = pl.program_id(0)

        @pl.when(i == 0)
        def _():
            carry[...] = jnp.zeros_like(carry)
            dbias_ref[...] = jnp.zeros_like(dbias_ref)
            dsink_ref[...] = jnp.zeros_like(dsink_ref)

        from_next = carry[...]
        head_row = lax.broadcasted_iota(jnp.int32, (N_HEADS, 128), 0)
        low = lax.broadcasted_iota(jnp.int32, (BLK, 128), 1) < HD
        for j in reversed(range(tb)):
            rows = slice(j * BLK, (j + 1) * BLK)
            kvp = kvp_ref[...] if j == 0 else kv_ref[(j - 1) * BLK:j * BLK, :]
            has_prev = (i < nt - 1) if j == 0 else None
            qpad, kboth, vboth, prob, p_sink = _attn_probs(q_ref[rows, :], kv_ref[rows, :], kvp, bias_ref, sink_ref,
                                                           has_prev)
            pb = prob.astype(BF)
            dopad = _stack_heads(do_ref[rows, :])
            dp = _dot_nt(dopad, vboth)
            delta = jnp.sum(prob * dp, axis=1, keepdims=True)
            ds = prob * (dp - delta)
            dbias_ref[...] += ds
            sink_term = p_sink * delta
            dsink_rows = jnp.zeros((N_HEADS, 128), F32)
            for h in range(N_HEADS):
                val = -jnp.sum(sink_term[h * BLK:(h + 1) * BLK], axis=0, keepdims=True)
                dsink_rows = jnp.where(head_row == h, val, dsink_rows)
            dsink_ref[...] += dsink_rows
            dsb = ds.astype(BF)
            dqpad = _dot(dsb, kboth) * SCALE
            for g in range(GROUP):
                dq_ref[rows, g * 128:(g + 1) * 128] = jnp.where(
                    low, dqpad[g * BLK:(g + 1) * BLK], dqpad[half + g * BLK:half + (g + 1) * BLK]).astype(BF)
            dkv2 = jnp.concatenate([jnp.transpose(_dot_tn(qpad, dsb)),
                                    jnp.transpose(_dot_tn(dopad, pb))], axis=1)
            dkv_ref[rows, :] = (dkv2[BLK:2 * BLK] + from_next).astype(BF)
            from_next = dkv2[0:BLK]
        carry[...] = from_next

    return pl.pallas_call(
        body, name="attn_bwd", grid=(nt,),
        out_shape=[_sds((S, Q_W), BF), _sds((S, 2 * KV_W), BF),
                   _sds((HEAD_ROWS, 2 * BLK), F32), _sds((N_HEADS, 128), F32)],
        in_specs=[pl.BlockSpec(memory_space=pltpu.SMEM),
                  pl.BlockSpec((T, Q_W), lambda i: (nt - 1 - i, 0)),
                  pl.BlockSpec((T, 2 * KV_W), lambda i: (nt - 1 - i, 2)),
                  pl.BlockSpec((BLK, 2 * KV_W), lambda i: (jnp.maximum((nt - 1 - i) * tb - 1, 0), 2)),
                  _const((HEAD_ROWS, 2 * BLK)),
                  pl.BlockSpec((T, Q_W), lambda i: (nt - 1 - i, 0))],
        out_specs=[pl.BlockSpec((T, Q_W), lambda i: (nt - 1 - i, 0)),
                   pl.BlockSpec((T, 2 * KV_W), lambda i: (nt - 1 - i, 0)),
                   _const((HEAD_ROWS, 2 * BLK)), _const((N_HEADS, 128))],
        scratch_shapes=[pltpu.VMEM((BLK, 2 * KV_W), F32)],
        compiler_params=_cp(1, 32),
    )(sinks, qkv, qkv, qkv, bias, do)


def _rel_bias_grad(dbias, bucket):
    def body(db_ref, bk_ref, out_ref):
        bk = bk_ref[...]
        lane = lax.broadcasted_iota(jnp.int32, (1, 128), 1)
        for h in range(N_HEADS):
            d = db_ref[h // GROUP, pl.ds((h % GROUP) * BLK, BLK), :]
            row = jnp.zeros((1, 128), F32)
            for b in range(N_BUCKETS):
                tot = jnp.sum(jnp.sum(jnp.where(bk == b, d, 0.0), axis=1, keepdims=True), axis=0, keepdims=True)
                row = jnp.where(lane == b, tot, row)
            out_ref[pl.ds(h, 1), :] = row

    vm = pl.BlockSpec(memory_space=pltpu.VMEM)
    return pl.pallas_call(body, name="rel_bias_grad", out_shape=_sds((N_HEADS, 128), F32),
                          in_specs=[vm, vm], out_specs=vm)(dbias, bucket)


def _gmlp_parts(zg, lg_ref, lb_ref):
    z = zg.astype(F32)
    ge = _gelu(z)
    u, vg = ge[:, 0:G_W], ge[:, G_W:2 * G_W]
    mu = jnp.mean(vg, axis=-1, keepdims=True)
    xc = vg - mu
    rstd = lax.rsqrt(jnp.mean(xc * xc, axis=-1, keepdims=True) + EPS)
    xh = xc * rstd
    return z, u, xh, rstd, xh * lg_ref[...] + lb_ref[...]


def _causal_weights(ws_ref, wc):
    t = lax.broadcasted_iota(jnp.int32, (BLK, BLK), 0)
    s = lax.broadcasted_iota(jnp.int32, (BLK, BLK), 1)
    for g in range(N_HEADS):
        wc[g] = jnp.where(s <= t, ws_ref[g], 0.0).astype(BF)


def _spatial(vb, wc, bst_ref, p, low):
    xp = vb[:, p * 128:(p + 1) * 128]
    s0 = _dot(wc[2 * p], xp) + bst_ref[:, 2 * p:2 * p + 1]
    s1 = _dot(wc[2 * p + 1], xp) + bst_ref[:, 2 * p + 1:2 * p + 2]
    return xp, jnp.where(low, s0, s1)


def _gmlp_fwd(zg, lg, lb, ws, bst):
    S = zg.shape[0]
    tb = min(ATT_TB, S // BLK)
    T = tb * BLK

    def body(zg_ref, lg_ref, lb_ref, ws_ref, bst_ref, o_ref, wc):
        @pl.when(pl.program_id(0) == 0)
        def _():
            _causal_weights(ws_ref, wc)
        low = lax.broadcasted_iota(jnp.int32, (BLK, 128), 1) < HD
        for j in range(tb):
            rows = slice(j * BLK, (j + 1) * BLK)
            _, u, _, _, vln = _gmlp_parts(zg_ref[rows, :], lg_ref, lb_ref)
            vb = vln.astype(BF)
            for p in range(4):
                _, sp = _spatial(vb, wc, bst_ref, p, low)
                o_ref[rows, p * 128:(p + 1) * 128] = (u[:, p * 128:(p + 1) * 128] * sp).astype(BF)

    return pl.pallas_call(
        body, name="gmlp_fwd", grid=(S // T,),
        out_shape=_sds((S, G_W), BF),
        in_specs=[pl.BlockSpec((T, 2 * G_W), lambda i: (i, 0)), _const((1, G_W)), _const((1, G_W)),
                  _const((N_HEADS, BLK, BLK)), _const((BLK, N_HEADS))],
        out_specs=pl.BlockSpec((T, G_W), lambda i: (i, 0)),
        scratch_shapes=[pltpu.VMEM((N_HEADS, BLK, BLK), BF)],
        compiler_params=_cp(1, 32),
    )(zg, lg, lb, ws, bst)


def _gmlp_bwd(zg, d_out, lg, lb, ws, bst):
    S = zg.shape[0]
    tb = min(ATT_TB, S // BLK)
    T = tb * BLK
    nb = S // T

    def body(zg_ref, d_ref, lg_ref, lb_ref, ws_ref, bst_ref, dzg_ref, dws_ref, dbs_ref, dlg_ref, dlb_ref, wc, dbacc):
        i = pl.program_id(0)

        @pl.when(i == 0)
        def _():
            _causal_weights(ws_ref, wc)
            dws_ref[...] = jnp.zeros_like(dws_ref)
            dlg_ref[...] = jnp.zeros_like(dlg_ref)
            dlb_ref[...] = jnp.zeros_like(dlb_ref)
            dbacc[...] = jnp.zeros_like(dbacc)

        low = lax.broadcasted_iota(jnp.int32, (BLK, 128), 1) < HD
        for j in range(tb):
            rows = slice(j * BLK, (j + 1) * BLK)
            z, u, xh, rstd, vln = _gmlp_parts(zg_ref[rows, :], lg_ref, lb_ref)
            vb = vln.astype(BF)
            d = d_ref[rows, :].astype(F32)
            du_parts, dvln_parts = [], []
            for p in range(4):
                xp, sp = _spatial(vb, wc, bst_ref, p, low)
                dp = d[:, p * 128:(p + 1) * 128]
                du_parts.append(dp * sp)
                dsp = dp * u[:, p * 128:(p + 1) * 128]
                dbacc[:, p * 128:(p + 1) * 128] += dsp
                d0 = jnp.where(low, dsp, 0.0).astype(BF)
                d1 = jnp.where(low, 0.0, dsp).astype(BF)
                dws_ref[2 * p] += _dot_nt(d0, xp)
                dws_ref[2 * p + 1] += _dot_nt(d1, xp)
                dvln_parts.append(_dot_tn(wc[2 * p], d0) + _dot_tn(wc[2 * p + 1], d1))
            dvln = jnp.concatenate(dvln_parts, axis=1)
            dlg_ref[...] += _colsum(dvln * xh)
            dlb_ref[...] += _colsum(dvln)
            dxh = dvln * lg_ref[...]
            dvg = rstd * (dxh - jnp.mean(dxh, axis=-1, keepdims=True)
                          - xh * jnp.mean(dxh * xh, axis=-1, keepdims=True))
            dge = jnp.concatenate(du_parts + [dvg], axis=1)
            dzg_ref[rows, :] = (dge * _gelu_grad(z)).astype(BF)

        @pl.when(i == nb - 1)
        def _():
            t = lax.broadcasted_iota(jnp.int32, (BLK, BLK), 0)
            s = lax.broadcasted_iota(jnp.int32, (BLK, BLK), 1)
            for g in range(N_HEADS):
                dws_ref[g] = jnp.where(s <= t, dws_ref[g], 0.0)
            grp = lax.broadcasted_iota(jnp.int32, (N_HEADS, G_W), 0)
            lane = lax.broadcasted_iota(jnp.int32, (N_HEADS, G_W), 1) // HD
            pick = jnp.where(grp == lane, 1.0, 0.0).astype(F32)
            dbs_ref[...] = lax.dot_general(pick, dbacc[...], (((1,), (1,)), ((), ())),
                                           preferred_element_type=F32, precision=HIGH)

    return pl.pallas_call(
        body, name="gmlp_bwd", grid=(nb,),
        out_shape=[_sds((S, 2 * G_W), BF), _sds((N_HEADS, BLK, BLK), F32), _sds((N_HEADS, BLK), F32),
                   _sds((1, G_W), F32), _sds((1, G_W), F32)],
        in_specs=[pl.BlockSpec((T, 2 * G_W), lambda i: (i, 0)), pl.BlockSpec((T, G_W), lambda i: (i, 0)),
                  _const((1, G_W)), _const((1, G_W)), _const((N_HEADS, BLK, BLK)), _const((BLK, N_HEADS))],
        out_specs=[pl.BlockSpec((T, 2 * G_W), lambda i: (i, 0)), _const((N_HEADS, BLK, BLK)),
                   _const((N_HEADS, BLK)), _const((1, G_W)), _const((1, G_W))],
        scratch_shapes=[pltpu.VMEM((N_HEADS, BLK, BLK), BF), pltpu.VMEM((BLK, G_W), F32)],
        compiler_params=_cp(1, 32),
    )(zg, d_out, lg, lb, ws, bst)


def _mix_out(o, gm, gates, h, wa, wg, wo, gate, gp):
    S = h.shape[0]
    R = min(512, S)

    def body(o_ref, gm_ref, gates_ref, h_ref, wa_ref, wg_ref, wo_ref, gate_ref, gp_ref,
             ya_ref, yg_ref, ym_ref, y_ref, hn_ref):
        for r0 in range(0, R, CHUNK):
            rows = slice(r0, r0 + CHUNK)
            ya = _dot(o_ref[rows, :], wa_ref[...])
            yg = _dot(gm_ref[rows, :], wg_ref[...])
            ya_ref[rows, :] = ya.astype(BF)
            yg_ref[rows, :] = yg.astype(BF)
            ym = (gates_ref[rows, 0:D].astype(F32) * ya + gates_ref[rows, D:2 * D].astype(F32) * yg).astype(BF)
            ym_ref[rows, :] = ym
            y = _dot(ym, wo_ref[...])
            y_ref[rows, :] = y.astype(BF)
            hn_ref[rows, :] = h_ref[rows, :] + gate_ref[...] * (y * _rms_r(y) * gp_ref[...])

    vec = _const((1, D))
    rows = lambda w_: pl.BlockSpec((R, w_), lambda i: (i, 0))
    return pl.pallas_call(
        body, name="mix_out", grid=(S // R,),
        out_shape=[_sds((S, D), BF)] * 4 + [_sds((S, D), F32)],
        in_specs=[rows(Q_W), rows(G_W), rows(2 * D), rows(D), _resident((Q_W, D)), _resident((G_W, D)),
                  _resident((D, D)), vec, vec],
        out_specs=[rows(D)] * 5,
        compiler_params=_cp(1, 48),
    )(o, gm, gates, h, wa, wg, wo, gate, gp)


def _mix_out_bwd(dh, y, ya, yg, gates, att, gm, ymix, wa, wg, wo, gate, gp):
    S = dh.shape[0]
    R = min(512, S)
    nb = S // R

    def body(dh_ref, y_ref, ya_ref, yg_ref, gates_ref, att_ref, gm_ref, ym_ref, wa_ref, wg_ref, wo_ref,
             gate_ref, gp_ref, dz_ref, do_ref, dgm_ref, dgate_ref, dgp_ref, gwo_ref, gwa_ref, gwg_ref,
             acc_o, acc_a, acc_g, dy_scr, dya_scr, dyg_scr):
        i = pl.program_id(0)

        @pl.when(i == 0)
        def _():
            for r in (dgate_ref, dgp_ref, acc_o, acc_a, acc_g):
                r[...] = jnp.zeros_like(r)
        for r0 in range(0, R, CHUNK):
            rows = slice(r0, r0 + CHUNK)
            dy, dgate, dgp = _postnorm_bwd(dh_ref[rows, :], y_ref[rows, :], gate_ref[...], gp_ref[...], 1.0)
            dgate_ref[...] += dgate
            dgp_ref[...] += dgp
            dyb = dy.astype(BF)
            dy_scr[rows, :] = dyb
            dym = _dot_nt(dyb, wo_ref[...])
            ga = gates_ref[rows, 0:D].astype(F32)
            gg = gates_ref[rows, D:2 * D].astype(F32)
            dya = (dym * ga).astype(BF)
            dyg = (dym * gg).astype(BF)
            dya_scr[rows, :] = dya
            dyg_scr[rows, :] = dyg
            dz_ref[rows, 0:D] = (dym * ya_ref[rows, :].astype(F32) * (ga * (1.0 - ga))).astype(BF)
            dz_ref[rows, D:2 * D] = (dym * yg_ref[rows, :].astype(F32) * (gg * (1.0 - gg))).astype(BF)
            do_ref[rows, :] = _dot_nt(dya, wa_ref[...]).astype(BF)
            dgm_ref[rows, :] = _dot_nt(dyg, wg_ref[...]).astype(BF)
        for m0 in range(0, D, CHUNK):
            acc_o[m0:m0 + CHUNK, :] += _dot_tn(ym_ref[:, m0:m0 + CHUNK], dy_scr[...])
        for m0 in range(0, Q_W, CHUNK):
            acc_a[m0:m0 + CHUNK, :] += _dot_tn(att_ref[:, m0:m0 + CHUNK], dya_scr[...])
            acc_g[m0:m0 + CHUNK, :] += _dot_tn(gm_ref[:, m0:m0 + CHUNK], dyg_scr[...])

        @pl.when(i == nb - 1)
        def _():
            for m0 in range(0, D, CHUNK):
                gwo_ref[m0:m0 + CHUNK, :] = acc_o[m0:m0 + CHUNK, :].astype(BF)
            for m0 in range(0, Q_W, CHUNK):
                gwa_ref[m0:m0 + CHUNK, :] = acc_a[m0:m0 + CHUNK, :].astype(BF)
                gwg_ref[m0:m0 + CHUNK, :] = acc_g[m0:m0 + CHUNK, :].astype(BF)

    vec = _const((1, D))
    rows = lambda w_: pl.BlockSpec((R, w_), lambda i: (i, 0))
    return pl.pallas_call(
        body, name="mix_out_bwd", grid=(nb,),
        out_shape=[_sds((S, 2 * D), BF), _sds((S, Q_W), BF), _sds((S, G_W), BF), _sds((1, D), F32),
                   _sds((1, D), F32), _sds((D, D), BF), _sds((Q_W, D), BF), _sds((G_W, D), BF)],
        in_specs=[rows(D), rows(D), rows(D), rows(D), rows(2 * D), rows(Q_W), rows(G_W), rows(D),
                  _resident((Q_W, D)), _resident((G_W, D)), _resident((D, D)), vec, vec],
        out_specs=[rows(2 * D), rows(Q_W), rows(G_W), vec, vec, _const((D, D)), _const((Q_W, D)),
                   _const((G_W, D))],
        scratch_shapes=[pltpu.VMEM((D, D), F32), pltpu.VMEM((Q_W, D), F32), pltpu.VMEM((G_W, D), F32)]
        + [pltpu.VMEM((R, D), BF)] * 3,
        compiler_params=_cp(1, 60),
    )(dh, y, ya, yg, gates, att, gm, ymix, wa, wg, wo, gate, gp)


def _mix_dn(dq, dkv, dzg, dzgate, w, wq, h, dh, sc, gp):
    S = h.shape[0]
    R = min(512, S)

    def body(dq_ref, dkv_ref, dzg_ref, dzt_ref, w_ref, wq_ref, h_ref, dh_ref, sc_ref, gp_ref,
             out_ref, dsh_ref, dsc_ref, dgp_ref):
        @pl.when(pl.program_id(0) == 0)
        def _():
            dsh_ref[...] = jnp.zeros_like(dsh_ref)
            dsc_ref[...] = jnp.zeros_like(dsc_ref)
            dgp_ref[...] = jnp.zeros_like(dgp_ref)
        for r0 in range(0, R, CHUNK):
            rows = slice(r0, r0 + CHUNK)
            dn = _dot(dq_ref[rows, :], wq_ref[...])
            dn = dn + _dot(dkv_ref[rows, :], w_ref[Q_W:QKV_W, :])
            dn = dn + _dot(dzg_ref[rows, :], w_ref[ZG_OFF:GATE_OFF, :])
            dn = dn + _dot(dzt_ref[rows, :], w_ref[GATE_OFF:IN_W, :])
            dx, dsh, dsc, dgp = _prenorm_bwd(dn, h_ref[rows, :], gp_ref[...], sc_ref[...])
            out_ref[rows, :] = dh_ref[rows, :] + dx
            dsh_ref[...] += dsh
            dsc_ref[...] += dsc
            dgp_ref[...] += dgp

    vec = _const((1, D))
    rows = lambda w_: pl.BlockSpec((R, w_), lambda i: (i, 0))
    return pl.pallas_call(
        body, name="mix_dn", grid=(S // R,),
        out_shape=[_sds((S, D), F32)] + [_sds((1, D), F32)] * 3,
        in_specs=[rows(Q_W), rows(2 * KV_W), rows(2 * G_W), rows(2 * D), _resident((IN_W, D)),
                  _resident((Q_W, D)), rows(D), rows(D), vec, vec],
        out_specs=[rows(D), vec, vec, vec],
        compiler_params=_cp(1, 48),
    )(dq, dkv, dzg, dzgate, w, wq, h, dh, sc, gp)


def _adamw_math(w, g, m, v):
    m2 = ADAM_B1 * m + (1.0 - ADAM_B1) * g
    v2 = ADAM_B2 * v + (1.0 - ADAM_B2) * (g * g)
    m_hat = m2 / (1.0 - ADAM_B1 ** ADAM_STEP)
    v_hat = v2 / (1.0 - ADAM_B2 ** ADAM_STEP)
    delta = -ADAM_LR * (m_hat / (jnp.sqrt(v_hat) + ADAM_EPS) + ADAM_WD * w)
    return delta, m2, v2


def _row_tile(rows, cols):
    best = None
    for t in range(16, rows + 1, 16):
        if rows % t == 0 and t * cols <= 256 * 1024:
            best = t
    return best if best is not None else rows


def _adamw_sharded(landing, w, m, v, name):
    r, c = w.shape
    tr = _row_tile(r, c)

    def body(l_ref, w_ref, m_ref, v_ref, g_ref, d_ref, m2_ref, v2_ref):
        g = l_ref[0].astype(F32)
        for j in range(1, N_DEV):
            g = g + l_ref[j].astype(F32)
        delta, m2, v2 = _adamw_math(w_ref[...], g, m_ref[...], v_ref[...])
        g_ref[...] = g
        d_ref[...] = delta
        m2_ref[...] = m2
        v2_ref[...] = v2

    row = pl.BlockSpec((tr, c), lambda i: (i, 0))
    return pl.pallas_call(
        body, name=name, grid=(r // tr,),
        out_shape=[_sds((r, c), F32)] * 4,
        in_specs=[pl.BlockSpec((N_DEV, tr, c), lambda i: (0, i, 0)), row, row, row],
        out_specs=[row] * 4,
        compiler_params=_cp(1, 48),
    )(landing, w, m, v)


def _adamw_small(items):
    n = len(items)

    def body(*refs):
        for k in range(n):
            w_ref, g_ref, m_ref, v_ref = refs[4 * k:4 * k + 4]
            outs = refs[4 * n + 3 * k:4 * n + 3 * k + 3]
            for o_ref, val in zip(outs, _adamw_math(w_ref[...], g_ref[...], m_ref[...], v_ref[...])):
                o_ref[...] = val

    vm = pl.BlockSpec(memory_space=pltpu.VMEM)
    flat = pl.pallas_call(
        body, name="adamw_small",
        out_shape=[_sds(it[0].shape, F32) for it in items for _ in range(3)],
        in_specs=[vm] * (4 * n), out_specs=[vm] * (3 * n),
    )(*[a for it in items for a in it])
    return [tuple(flat[3 * k:3 * k + 3]) for k in range(n)]


def _w_ada_update(c8, d_ada, w, m, v):
    tr = 256

    def body(c_ref, d_ref, w_ref, m_ref, v_ref, g_ref, dl_ref, m2_ref, v2_ref):
        cs = c_ref[...]
        cs = cs * jax.nn.sigmoid(cs)
        g = lax.dot_general(cs, d_ref[...], (((0,), (0,)), ((), ())), preferred_element_type=F32, precision=HIGH)
        delta, m2, v2 = _adamw_math(w_ref[...], g, m_ref[...], v_ref[...])
        g_ref[...] = g
        dl_ref[...] = delta
        m2_ref[...] = m2
        v2_ref[...] = v2

    row = pl.BlockSpec((tr, ADA_W), lambda i: (i, 0))
    return pl.pallas_call(
        body, name="w_ada_update", grid=(D // tr,),
        out_shape=[_sds((D, ADA_W), F32)] * 4,
        in_specs=[pl.BlockSpec((N_DEV, tr), lambda i: (0, i)), _const((N_DEV, ADA_W)), row, row, row],
        out_specs=[row] * 4,
        compiler_params=_cp(1, 40),
    )(c8, d_ada, w, m, v)


def _t5_bucket():
    qi = np.arange(BLK, dtype=np.int32)[:, None]
    kj = np.arange(2 * BLK, dtype=np.int32)[None, :]
    dist = np.maximum(qi + BLK - kj, 0)
    max_exact = N_BUCKETS // 2
    d_f = np.maximum(dist, max_exact).astype(np.float32)
    large = max_exact + (np.log(d_f / np.float32(max_exact)) / np.float32(math.log(MAX_DISTANCE / max_exact))
                         * np.float32(N_BUCKETS - max_exact)).astype(np.int32)
    large = np.minimum(large, N_BUCKETS - 1)
    return jnp.asarray(np.where(dist < max_exact, dist, large).astype(np.int32))


def _slabs_of_columns(w):
    r, c8 = w.shape
    return jnp.transpose(w.reshape(r, N_DEV, c8 // N_DEV), (1, 0, 2))


def _columns_of_slabs(w8):
    _, r, c = w8.shape
    return jnp.transpose(w8, (1, 0, 2)).reshape(r, N_DEV * c)


def kernel(x, c, rel_bias, w_ada, b_ada, pre_norm_g, post_norm_g, w_ffn1_in, w_ffn1_out, w_in, sinks, gmlp_ln_g, gmlp_ln_b, gmlp_w_s, gmlp_b_s, w_br_attn, w_br_gmlp, w_out, w_ffn2_in, w_ffn2_out, loss_target, m_rel_bias, m_w_ada, m_b_ada, m_pre_norm_g, m_post_norm_g, m_w_ffn1_in, m_w_ffn1_out, m_w_in, m_sinks, m_gmlp_ln_g, m_gmlp_ln_b, m_gmlp_w_s, m_gmlp_b_s, m_w_br_attn, m_w_br_gmlp, m_w_out, m_w_ffn2_in, m_w_ffn2_out, v_rel_bias, v_w_ada, v_b_ada, v_pre_norm_g, v_post_norm_g, v_w_ffn1_in, v_w_ffn1_out, v_w_in, v_sinks, v_gmlp_ln_g, v_gmlp_ln_b, v_gmlp_w_s, v_gmlp_b_s, v_w_br_attn, v_w_br_gmlp, v_w_out, v_w_ffn2_in, v_w_ffn2_out):
    me = 4 * lax.axis_index("x") + 2 * lax.axis_index("y") + lax.axis_index("c")
    x0 = x[0]
    target = loss_target[0]

    transposed = ("w_ffn1_in", "w_in", "w_ffn2_in")
    shards = [w_ffn1_in[0].T, w_ffn1_out[0], w_in[0].T, w_br_attn[0], w_br_gmlp[0], w_out[0],
              w_ffn2_in[0].T, w_ffn2_out[0]]
    shards_bf = [s.astype(BF) for s in shards]
    groups = [shards_bf[0:1], shards_bf[1:6], shards_bf[6:8]]

    def gather_start(i, after):
        return _slabs_start("gather", groups[i], after, "gather_start_%d" % i)

    def forward_start(st, i, after):
        lands = _slabs_wait("gather", len(groups[i]), st, after, "gather_wait_%d" % i)
        return _slabs_start("forward", lands, c, "forward_start_%d" % i)

    def gathered(st, i, after):
        return _slabs_wait("forward", len(groups[i]), st, after, "forward_wait_%d" % i)

    gs0 = gather_start(0, c)

    mine = jnp.concatenate([c[0], pre_norm_g[0].reshape(-1), post_norm_g[0].reshape(-1)])
    small8 = jnp.broadcast_to(mine[None, :], (8, mine.shape[0]))
    b_ada64 = jnp.repeat(b_ada.reshape(N_DEV, ADA_W), 8, axis=0)
    gath, ada64 = _ada_forward(small8, w_ada[0], b_ada64)
    gath8 = gath[::8]
    ada = ada64[::8].reshape(9, D)
    sh1, sc1, g1, sh2, sc2, g2, sh3, sc3, g3 = [ada[k:k + 1] for k in range(9)]
    gains = gath8[:, D:].reshape(N_DEV, 2, 3, 128)
    pre_g = jnp.transpose(gains[:, 0], (1, 0, 2)).reshape(3, D)
    post_g = jnp.transpose(gains[:, 1], (1, 0, 2)).reshape(3, D)
    pre = [pre_g[k:k + 1] for k in range(3)]
    post = [post_g[k:k + 1] for k in range(3)]

    bucket = _t5_bucket()
    bias = _bias_table(rel_bias, bucket).reshape(HEAD_ROWS, 2 * BLK)
    sinks8 = sinks[0]
    lg, lb = gmlp_ln_g, gmlp_ln_b
    ws = gmlp_w_s[0]
    bst = jnp.transpose(gmlp_b_s[0])

    fs0 = forward_start(gs0, 0, sh1)
    gs1 = gather_start(1, fs0[-1])
    wf1_in = gathered(fs0, 0, gs1[-1])[0].reshape(2 * D_FF, D)
    n1, fg1, fu1, fa1 = _ffn_in(x0, sh1, sc1, pre[0], wf1_in, "ffn1_in")
    fs1 = forward_start(gs1, 1, n1)
    gs2 = gather_start(2, fs1[-1])
    mix_w = gathered(fs1, 1, gs2[-1])
    wf1_out = mix_w[0].reshape(D_FF, D)
    w_in_full = mix_w[1].reshape(IN_W, D)
    w_q = _pair_heads(w_in_full[0:Q_W])
    w_bra = _pair_heads(_columns_of_slabs(mix_w[2]))
    w_brg = _columns_of_slabs(mix_w[3])
    w_out_full = mix_w[4].reshape(D, D)
    h1, y1 = _ffn_out(fa1, wf1_out, x0, g1, post[0], "ffn1_out")
    n2, qkv, zg, gates = _mix_in(h1, sh2, sc2, pre[1], w_in_full, w_q)
    att = _attn_fwd(qkv, bias, sinks8)
    gm = _gmlp_fwd(zg, lg, lb, ws, bst)
    fs2 = forward_start(gs2, 2, gm)
    ya, yg, ymix, y2, h2 = _mix_out(att, gm, gates, h1, w_bra, w_brg, w_out_full, g2 + fs2[-1], post[1])
    wf2_in, wf2_out = gathered(fs2, 2, h2)
    wf2_in = wf2_in.reshape(2 * D_FF, D)
    wf2_out = wf2_out.reshape(D_FF, D)
    n3, fg3, fu3, fa3 = _ffn_in(h2, sh3, sc3, pre[2], wf2_in, "ffn2_in")
    dh3, y3, sq = _ffn_out(fa3, wf2_out, h2, g3, post[2], "ffn2_out", target=target)

    def exchange_start(i, arrays):
        return _slabs_start("exchange", arrays, sq, "exchange_start_%d" % i)

    dy3, dgu3, dh2, d_g3, d_post2, d_sh3, d_sc3, d_pre2 = _ffn_bwd(
        dh3, y3, fg3, fu3, wf2_out, wf2_in, h2, g3, post[2], sc3, pre[2], "ffn2_bwd")
    gw_f2_out = _tn_matmul(fa3, dy3, "ffn2_out_wgrad", tm=D_FF // 2).reshape(N_DEV, D_FF // N_DEV, D)
    gw_f2_in = _tn_matmul(dgu3, n3, "ffn2_in_wgrad", tm=D_FF // 2).reshape(N_DEV, FS, D)
    ex1 = exchange_start(1, [gw_f2_out, gw_f2_in])

    dzgate, d_att, d_gm, d_g2, d_post1, gw_out, gw_bra, gw_brg = _mix_out_bwd(
        dh2, y2, ya, yg, gates, att, gm, ymix, w_bra, w_brg, w_out_full, g2 + ex1[-1], post[1])
    ex2 = exchange_start(2, [_slabs_of_columns(_unpair_heads(gw_bra)), _slabs_of_columns(gw_brg),
                             gw_out.reshape(N_DEV, D // N_DEV, D)])
    dq, dkv, dbias, dsink = _attn_bwd(qkv, bias, sinks8, d_att)
    dzg, d_ws, d_bs, d_lg, d_lb = _gmlp_bwd(zg, d_gm, lg, lb, ws, bst)
    d_rel = _rel_bias_grad(dbias.reshape(N_KV, GROUP * BLK, 2 * BLK), bucket)
    early = jnp.concatenate([
        jnp.concatenate([d_lg.reshape(4, 128), d_lb.reshape(4, 128)], axis=0),
        d_bs, d_rel, dsink, d_ws.reshape(N_HEADS * BLK, BLK)], axis=0)
    sm0 = _slabs_start("gather_all", [early], sq, "small_gather_start")
    dh1, d_sh2, d_sc2, d_pre1 = _mix_dn(dq, dkv, dzg, dzgate, w_in_full, w_q, h1, dh2, sc2 + ex2[-1] + sm0[-1],
                                        pre[1])
    gw_in = jnp.concatenate(
        [_unpair_heads(_tn_matmul(dq, n2, "w_in_q_wgrad")), _tn_matmul(dkv, n2, "w_in_kv_wgrad"),
         _tn_matmul(dzg, n2, "w_in_zg_wgrad"), _tn_matmul(dzgate, n2, "w_in_gate_wgrad")],
        axis=0).reshape(N_DEV, IN_W // N_DEV, D)
    ex3 = exchange_start(3, [gw_in])

    dy1, dgu1, d_g1, d_post0 = _ffn_out_bwd(dh1, y1, fg1, fu1, wf1_out, g1 + ex3[-1], post[0], "ffn1_out_bwd")
    gw_f1_out = _tn_matmul(fa1, dy1, "ffn1_out_wgrad", tm=D_FF // 2).reshape(N_DEV, D_FF // N_DEV, D)
    ex4 = exchange_start(4, [gw_f1_out])
    gw_f1_in = _tn_matmul(dgu1, n1, "ffn1_in_wgrad", tm=D_FF // 2).reshape(N_DEV, FS, D)
    ex5 = exchange_start(5, [gw_f1_in])
    grad_x, d_sh1, d_sc1, d_pre0 = _ffn_dn(dgu1, wf1_in, x0, dh1, sc1 + ex4[-1] + ex5[-1], pre[0], "ffn1_dn")

    landed = {}
    for i, (ex, nms) in enumerate([(ex1, ["w_ffn2_out", "w_ffn2_in"]),
                                   (ex2, ["w_br_attn", "w_br_gmlp", "w_out"]), (ex3, ["w_in"]),
                                   (ex4, ["w_ffn1_out"]), (ex5, ["w_ffn1_in"])]):
        for nm, land in zip(nms, _slabs_wait("exchange", len(nms), ex, grad_x, "exchange_wait_%d" % i)):
            landed[nm] = land
    moments = [(m_w_ffn1_in, v_w_ffn1_in), (m_w_ffn1_out, v_w_ffn1_out), (m_w_in, v_w_in),
               (m_w_br_attn, v_w_br_attn), (m_w_br_gmlp, v_w_br_gmlp), (m_w_out, v_w_out),
               (m_w_ffn2_in, v_w_ffn2_in), (m_w_ffn2_out, v_w_ffn2_out)]
    names = ["w_ffn1_in", "w_ffn1_out", "w_in", "w_br_attn", "w_br_gmlp", "w_out", "w_ffn2_in", "w_ffn2_out"]
    big = {}
    for nm, w_, (m_, v_) in zip(names, shards, moments):
        if nm in transposed:
            res4 = _adamw_sharded(landed[nm], w_, m_[0].T, v_[0].T, "adamw_" + nm)
            big[nm] = [a.T[None] for a in res4]
        else:
            big[nm] = [a[None] for a in _adamw_sharded(landed[nm], w_, m_[0], v_[0], "adamw_" + nm)]

    my_loss = jnp.broadcast_to(sq * (0.5 / D), (1, D))
    my_loss, _ = lax.optimization_barrier((my_loss, landed["w_ffn1_in"]))
    tot, every = _small_allreduce([d_sh1, d_sc1, d_g1, d_sh2, d_sc2, d_g2, d_sh3, d_sc3, d_g3,
                                   d_pre0, d_pre1, d_pre2, d_post0, d_post1, d_post2, my_loss])
    (early_land,) = _slabs_wait("gather_all", 1, sm0, grad_x, "small_gather_wait")
    tot_early = _sum_slabs(early_land)

    loss = tot[15, 0]
    g_b_ada = tot[0:9].reshape(1, 9 * D)
    g_pre = lax.dynamic_slice_in_dim(tot[9:12], 128 * me, 128, axis=1)[None]
    g_post = lax.dynamic_slice_in_dim(tot[12:15], 128 * me, 128, axis=1)[None]
    g_lg = tot_early[0:4].reshape(1, G_W)
    g_lb = tot_early[4:8].reshape(1, G_W)
    g_bs = tot_early[8:16][None]
    g_rel = jnp.transpose(tot_early[16:24, 0:N_BUCKETS])
    g_sinks = tot_early[24:32, 0][None]
    g_ws = tot_early[32:1056].reshape(1, N_HEADS, BLK, BLK)

    d_ada_mine = lax.dynamic_slice_in_dim(every[:, 0:9].reshape(N_DEV, 9 * D), ADA_W * me, ADA_W, axis=1)
    ada_out = [a[None] for a in _w_ada_update(gath8[:, 0:D], d_ada_mine, w_ada[0], m_w_ada[0], v_w_ada[0])]

    small = [("rel_bias", rel_bias, g_rel, m_rel_bias, v_rel_bias), ("b_ada", b_ada, g_b_ada, m_b_ada, v_b_ada),
             ("pre_norm_g", pre_norm_g, g_pre, m_pre_norm_g, v_pre_norm_g),
             ("post_norm_g", post_norm_g, g_post, m_post_norm_g, v_post_norm_g),
             ("sinks", sinks, g_sinks, m_sinks, v_sinks), ("gmlp_ln_g", gmlp_ln_g, g_lg, m_gmlp_ln_g, v_gmlp_ln_g),
             ("gmlp_ln_b", gmlp_ln_b, g_lb, m_gmlp_ln_b, v_gmlp_ln_b),
             ("gmlp_w_s", gmlp_w_s, g_ws, m_gmlp_w_s, v_gmlp_w_s), ("gmlp_b_s", gmlp_b_s, g_bs, m_gmlp_b_s, v_gmlp_b_s)]
    two_d = lambda a: a.reshape(int(math.prod(a.shape[:-1])), a.shape[-1])
    stepped = _adamw_small([tuple(two_d(a) for a in item[1:]) for item in small])
    res = {"w_ada": ada_out}
    for (nm, w_, g_, _, _), new in zip(small, stepped):
        res[nm] = [g_] + [a.reshape(w_.shape) for a in new]
    res.update(big)
    order = ["rel_bias", "w_ada", "b_ada", "pre_norm_g", "post_norm_g", "w_ffn1_in", "w_ffn1_out", "w_in", "sinks",
             "gmlp_ln_g", "gmlp_ln_b", "gmlp_w_s", "gmlp_b_s", "w_br_attn", "w_br_gmlp", "w_out", "w_ffn2_in",
             "w_ffn2_out"]
    outs = [loss, grad_x[None]]
    for k in range(4):
        outs += [res[nm][k] for nm in order]
    return tuple(outs)
```

```python
import math

import jax
import jax.numpy as jnp
import numpy as np
from jax import lax
from jax.experimental import pallas as pl
from jax.experimental.pallas import tpu as pltpu

F32 = jnp.float32
BF = jnp.bfloat16

N_DEV = 8
D = 1024
D_FF = 2816
FS = D_FF // 4
N_HEADS = 8
N_KV = 2
GROUP = 4
HD = 64
BLK = 128
Q_W = 512
KV_W = 128
G_W = 512
QKV_W = Q_W + 2 * KV_W
ZG_OFF = QKV_W
GATE_OFF = ZG_OFF + 2 * G_W
IN_W = GATE_OFF + 2 * D
N_BUCKETS = 32
MAX_DISTANCE = 128
EPS = 1e-6
NEG = -1e30
SCALE = HD ** -0.5
ADA_W = 9 * D // N_DEV

ADAM_LR = 0.001
ADAM_B1 = 0.9
ADAM_B2 = 0.999
ADAM_EPS = 1e-08
ADAM_WD = 0.01
ADAM_STEP = 10

CHUNK = 256
MIB = 1024 * 1024
MESH = pl.DeviceIdType.MESH
HIGH = lax.Precision.HIGHEST


def _cp(n_grid, vmem_mib):
    return pltpu.CompilerParams(dimension_semantics=("arbitrary",) * n_grid,
                                vmem_limit_bytes=vmem_mib * MIB)


def _const(shape):
    return pl.BlockSpec(shape, lambda *_: (0,) * len(shape))


def _resident(shape):
    return pl.BlockSpec(shape, lambda *_: (0,) * len(shape), pipeline_mode=pl.Buffered(1))


def _in_hbm(*arrays):
    return [pltpu.with_memory_space_constraint(a, pltpu.HBM) for a in arrays]


def _sds(shape, dtype):
    return jax.ShapeDtypeStruct(shape, dtype)


def _dot(a, b):
    return jnp.dot(a, b, preferred_element_type=F32)


def _dot_nt(a, b):
    return lax.dot_general(a, b, (((1,), (1,)), ((), ())), preferred_element_type=F32)


def _dot_tn(a, b):
    return lax.dot_general(a, b, (((0,), (0,)), ((), ())), preferred_element_type=F32)


def _rms_r(x):
    return lax.rsqrt(jnp.mean(x * x, axis=-1, keepdims=True) + EPS)


def _colsum(x):
    return jnp.sum(x, axis=0, keepdims=True)


def _prenorm(x, gp, sc, sh):
    return (x * _rms_r(x) * gp) * (1.0 + sc) + sh


def _prenorm_bwd(dn, x, gp, sc):
    r = _rms_r(x)
    xh = x * r
    t = dn * (1.0 + sc) * gp
    dx = r * (t - xh * jnp.mean(t * xh, axis=-1, keepdims=True))
    return dx, _colsum(dn), _colsum(dn * xh * gp), _colsum(dn * (1.0 + sc) * xh)


def _postnorm_bwd(dh, y, gate, gp, res):
    y = y.astype(F32)
    r = _rms_r(y)
    yh = y * r
    dyn = (res * gate) * dh
    t = dyn * gp
    dy = r * (t - yh * jnp.mean(t * yh, axis=-1, keepdims=True))
    return dy, _colsum(res * dh * yh * gp), _colsum(dyn * yh)


def _gelu(x):
    k = math.sqrt(2.0 / math.pi)
    return 0.5 * x * (1.0 + jnp.tanh(k * (x + 0.044715 * x * x * x)))


def _gelu_grad(x):
    k = math.sqrt(2.0 / math.pi)
    t = jnp.tanh(k * (x + 0.044715 * x * x * x))
    return 0.5 * (1.0 + t) + 0.5 * x * (1.0 - t * t) * (k * (1.0 + 3.0 * 0.044715 * x * x))


def _my_place():
    x, y, c = lax.axis_index("x"), lax.axis_index("y"), lax.axis_index("c")
    return x, y, c, 4 * x + 2 * y + c


def _peer(x, y, c, k):
    px = 1 - x if k & 4 else x
    py = 1 - y if k & 2 else y
    pc = 1 - c if k & 1 else c
    return (px, py, pc), 4 * px + 2 * py + pc


HBM_SPEC = pl.BlockSpec(memory_space=pltpu.HBM)
SEM_SPEC = pl.BlockSpec(memory_space=pltpu.SEMAPHORE)
EFFECT = pltpu.SideEffectType.DATAFLOW_SIDE_EFFECTING


RELATIONS = {"exchange": (1, 2, 3, 4, 5, 6, 7), "gather": (1, 2, 4, 6), "forward": (2, 4, 6),
             "gather_all": (1, 2, 3, 4, 5, 6, 7)}


def _slab_copies(mode, srcs, lands, send, recv, loc):
    x, y, c, me = _my_place()
    rel = RELATIONS[mode]
    remote, local = [], []
    for t in range(len(lands)):
        for i, k in enumerate(rel):
            peer, peer_lin = _peer(x, y, c, k)
            if mode == "exchange":
                src, dst, to = srcs[t].at[peer_lin], lands[t].at[me], peer
            elif mode in ("gather", "gather_all"):
                src, dst, to = srcs[t], lands[t].at[me], peer
            else:
                src, dst, to = lands[t].at[peer_lin], lands[t].at[peer_lin], _peer(x, y, c, 1)[0]
            remote.append(pltpu.make_async_remote_copy(
                src_ref=src, dst_ref=dst, send_sem=send.at[t * len(rel) + i], recv_sem=recv.at[t * len(rel) + i],
                device_id=to, device_id_type=MESH))
        if mode == "exchange":
            local.append(pltpu.make_async_copy(srcs[t].at[me], lands[t].at[me], loc.at[t]))
        elif mode in ("gather", "gather_all"):
            local.append(pltpu.make_async_copy(srcs[t], lands[t].at[me], loc.at[t]))
    return remote, local


def _slabs_start(mode, arrays, after, name):
    n = len(arrays)
    if mode == "forward":
        thru = list(arrays)
    else:
        shapes = [a.shape if mode == "exchange" else (N_DEV,) + a.shape for a in arrays]
        thru = list(arrays) + [lax.empty(s, a.dtype) for s, a in zip(shapes, arrays)]
    m = len(thru)
    n_sem = n * len(RELATIONS[mode])

    def body(*refs):
        srcs, lands = refs[:n], refs[m - n:m]
        send, recv, loc = refs[m + 1:m + 4]
        remote, local = _slab_copies(mode, srcs, lands, send, recv, loc)
        for cp in remote + local:
            cp.start()
        refs[-1][...] = jnp.zeros_like(refs[-1])

    return pl.pallas_call(
        body, name=name,
        out_shape=(pltpu.SemaphoreType.DMA((n_sem,)), pltpu.SemaphoreType.DMA((n_sem,)),
                   pltpu.SemaphoreType.DMA((n,)),
                   *[pltpu.HBM(a.shape, a.dtype) for a in thru],
                   _sds((1, D), F32)),
        in_specs=[HBM_SPEC] * m + [pl.BlockSpec(memory_space=pl.ANY)],
        out_specs=(SEM_SPEC, SEM_SPEC, SEM_SPEC, *[HBM_SPEC] * m, pl.BlockSpec(memory_space=pltpu.VMEM)),
        input_output_aliases={t: 3 + t for t in range(m)},
        compiler_params=pltpu.CompilerParams(has_side_effects=EFFECT),
    )(*[pltpu.with_memory_space_constraint(a, pltpu.HBM) for a in thru], after)


def _slabs_wait(mode, n, started, after, name):
    sems = started[0:3]
    thru = started[3:-1]
    m = len(thru)

    def body(*refs):
        srcs, lands = refs[:n], refs[m - n:m]
        remote, local = _slab_copies(mode, srcs, lands, *refs[m:m + 3])
        for cp in remote:
            cp.wait_send()
            cp.wait_recv()
        for cp in local:
            cp.wait()

    res = pl.pallas_call(
        body, name=name,
        out_shape=tuple(pltpu.HBM(a.shape, a.dtype) for a in thru),
        in_specs=[HBM_SPEC] * m + [SEM_SPEC] * 3 + [pl.BlockSpec(memory_space=pl.ANY)],
        out_specs=tuple([HBM_SPEC] * m),
        input_output_aliases={t: t for t in range(m)},
        compiler_params=pltpu.CompilerParams(has_side_effects=EFFECT),
    )(*thru, *sems, after)
    return list(res[m - n:m])


def _ada_forward(small8, w_ada, b_ada64):
    sw = small8.shape[1]

    def body(sm_ref, w_ref, b_ref, gath_ref, ada_ref, part_ref, send1, recv1, send2, recv2):
        x, y, c, me = _my_place()
        row_me = pl.multiple_of(me * 8, 8)
        gath_ref[pl.ds(row_me, 8), :] = sm_ref[...]
        first = []
        for k in range(1, N_DEV):
            peer, _ = _peer(x, y, c, k)
            cp = pltpu.make_async_remote_copy(
                src_ref=sm_ref, dst_ref=gath_ref.at[pl.ds(row_me, 8), :], send_sem=send1.at[k - 1],
                recv_sem=recv1.at[k - 1], device_id=peer, device_id_type=MESH)
            cp.start()
            first.append(cp)
        for cp in first:
            cp.wait()
        cs = gath_ref[:, 0:D]
        cs = cs * jax.nn.sigmoid(cs)
        part_ref[...] = jnp.dot(cs, w_ref[...], preferred_element_type=F32, precision=HIGH)
        ada_ref[pl.ds(row_me, 8), :] = part_ref[pl.ds(row_me, 8), :]
        second = []
        for k in range(1, N_DEV):
            peer, peer_lin = _peer(x, y, c, k)
            cp = pltpu.make_async_remote_copy(
                src_ref=part_ref.at[pl.ds(pl.multiple_of(peer_lin * 8, 8), 8), :],
                dst_ref=ada_ref.at[pl.ds(row_me, 8), :], send_sem=send2.at[k - 1],
                recv_sem=recv2.at[k - 1], device_id=peer, device_id_type=MESH)
            cp.start()
            second.append(cp)
        for cp in second:
            cp.wait()
        ada_ref[...] = ada_ref[...] + b_ref[...]

    vm = pl.BlockSpec(memory_space=pltpu.VMEM)
    return pl.pallas_call(
        body, name="ada_forward",
        out_shape=[_sds((8 * N_DEV, sw), F32), _sds((8 * N_DEV, ADA_W), F32)],
        in_specs=[vm, vm, vm], out_specs=[vm, vm],
        scratch_shapes=[pltpu.VMEM((8 * N_DEV, ADA_W), F32)] + [pltpu.SemaphoreType.DMA((7,))] * 4,
        compiler_params=pltpu.CompilerParams(vmem_limit_bytes=32 * MIB),
    )(small8, w_ada, b_ada64)


def _sum_slabs(land):
    def body(l_ref, o_ref):
        acc = l_ref[0]
        for j in range(1, N_DEV):
            acc = acc + l_ref[j]
        o_ref[...] = acc

    vm = pl.BlockSpec(memory_space=pltpu.VMEM)
    return pl.pallas_call(body, name="sum_slabs", out_shape=_sds(land.shape[1:], F32), in_specs=[vm], out_specs=vm,
                          compiler_params=pltpu.CompilerParams(vmem_limit_bytes=32 * MIB))(land)


def _small_allreduce(vectors):
    n = len(vectors)

    def body(*refs):
        v_refs, (sum_ref, gath_ref, pack, send, recv) = refs[:n], refs[n:]
        x, y, c, me = _my_place()
        for k in range(n):
            pack[k:k + 1, :] = v_refs[k][...]
        gath_ref[me] = pack[...]
        cps = []
        for k in range(1, N_DEV):
            peer, _ = _peer(x, y, c, k)
            cp = pltpu.make_async_remote_copy(
                src_ref=pack, dst_ref=gath_ref.at[me], send_sem=send.at[k - 1],
                recv_sem=recv.at[k - 1], device_id=peer, device_id_type=MESH)
            cp.start()
            cps.append(cp)
        for cp in cps:
            cp.wait()
        acc = gath_ref[0]
        for j in range(1, N_DEV):
            acc = acc + gath_ref[j]
        sum_ref[...] = acc

    vm = pl.BlockSpec(memory_space=pltpu.VMEM)
    return pl.pallas_call(
        body, name="small_allreduce",
        out_shape=[_sds((n, D), F32), _sds((N_DEV, n, D), F32)],
        in_specs=[vm] * n, out_specs=[vm, vm],
        scratch_shapes=[pltpu.VMEM((n, D), F32), pltpu.SemaphoreType.DMA((7,)), pltpu.SemaphoreType.DMA((7,))],
    )(*vectors)


F_TILES = tuple((f0, min(512, D_FF - f0)) for f0 in range(0, D_FF, 512))
F_TILES_NARROW = tuple((f0, 256) for f0 in range(0, D_FF, 256))


def _swiglu_tile(n, wt_ref, f0, tf):
    g = _dot_nt(n, wt_ref[f0:f0 + tf, :])
    u = _dot_nt(n, wt_ref[D_FF + f0:D_FF + f0 + tf, :])
    sg = jax.nn.sigmoid(g)
    silu = g * sg
    return (u * (sg * (1.0 + g * (1.0 - sg)))).astype(BF), silu.astype(BF), (silu * u).astype(BF)


def _ffn_in(h, sh, sc, gp, wt, name):
    S = h.shape[0]
    R = min(512, S)

    def body(h_ref, sh_ref, sc_ref, gp_ref, w_ref, n_ref, dg_ref, sl_ref, a_ref):
        for r0 in range(0, R, CHUNK):
            rows = slice(r0, r0 + CHUNK)
            n = _prenorm(h_ref[rows, :], gp_ref[...], sc_ref[...], sh_ref[...]).astype(BF)
            n_ref[rows, :] = n
            for f0, tf in F_TILES_NARROW:
                dg_ref[rows, f0:f0 + tf], sl_ref[rows, f0:f0 + tf], a_ref[rows, f0:f0 + tf] = _swiglu_tile(
                    n, w_ref, f0, tf)

    vec = _const((1, D))
    rows_ = lambda w_: pl.BlockSpec((R, w_), lambda i: (i, 0))
    return pl.pallas_call(
        body, name=name, grid=(S // R,),
        out_shape=[_sds((S, D), BF)] + [_sds((S, D_FF), BF)] * 3,
        in_specs=[rows_(D), vec, vec, vec, _resident((2 * D_FF, D))],
        out_specs=[rows_(D), rows_(D_FF), rows_(D_FF), rows_(D_FF)],
        compiler_params=_cp(1, 56),
    )(*_in_hbm(h), sh, sc, gp, *_in_hbm(wt))


def _ffn_out(a, w, h, gate, gp, name, target=None):
    S = h.shape[0]
    R = min(512, S)
    with_loss = target is not None

    def body(a_ref, w_ref, h_ref, gate_ref, gp_ref, *rest):
        if with_loss:
            t_ref, out_ref, y_ref, tot_ref = rest

            @pl.when(pl.program_id(0) == 0)
            def _():
                tot_ref[...] = jnp.zeros_like(tot_ref)
        else:
            out_ref, y_ref = rest
        for r0 in range(0, R, CHUNK):
            rows = slice(r0, r0 + CHUNK)
            y = _dot(a_ref[rows, :], w_ref[...])
            y_ref[rows, :] = y.astype(BF)
            hn = h_ref[rows, :] + (0.5 * gate_ref[...]) * (y * _rms_r(y) * gp_ref[...])
            if with_loss:
                e = hn - t_ref[rows, :]
                out_ref[rows, :] = e * (1.0 / D)
                tot_ref[...] += jnp.sum(jnp.sum(e * e, axis=1, keepdims=True), axis=0, keepdims=True)
            else:
                out_ref[rows, :] = hn

    vec = _const((1, D))
    rows_ = lambda w_: pl.BlockSpec((R, w_), lambda i: (i, 0))
    return pl.pallas_call(
        body, name=name, grid=(S // R,),
        out_shape=[_sds((S, D), F32), _sds((S, D), BF)] + ([_sds((1, 1), F32)] if with_loss else []),
        in_specs=[rows_(D_FF), _resident((D_FF, D)), rows_(D), vec, vec] + ([rows_(D)] if with_loss else []),
        out_specs=[rows_(D), rows_(D)] + ([_const((1, 1))] if with_loss else []),
        compiler_params=_cp(1, 48),
    )(*_in_hbm(a, w, h), gate, gp, *(_in_hbm(target) if with_loss else ()))


def _ffn_out_bwd(dh, y, dsilu_u, silu, w, gate, gp, name):
    S = dh.shape[0]
    R = min(512, S)

    def body(dh_ref, y_ref, g_ref, u_ref, w_ref, gate_ref, gp_ref, dy_ref, dgu_ref, dgate_ref, dgp_ref):
        @pl.when(pl.program_id(0) == 0)
        def _():
            dgate_ref[...] = jnp.zeros_like(dgate_ref)
            dgp_ref[...] = jnp.zeros_like(dgp_ref)
        for r0 in range(0, R, CHUNK):
            rows = slice(r0, r0 + CHUNK)
            dy, dgate, dgp = _postnorm_bwd(dh_ref[rows, :], y_ref[rows, :], gate_ref[...], gp_ref[...], 0.5)
            dgate_ref[...] += dgate
            dgp_ref[...] += dgp
            dyb = dy.astype(BF)
            dy_ref[rows, :] = dyb
            for f0, tf in F_TILES:
                da = _dot_nt(dyb, w_ref[f0:f0 + tf, :])
                dgu_ref[rows, f0:f0 + tf] = (da * g_ref[rows, f0:f0 + tf].astype(F32)).astype(BF)
                dgu_ref[rows, D_FF + f0:D_FF + f0 + tf] = (da * u_ref[rows, f0:f0 + tf].astype(F32)).astype(BF)

    vec = _const((1, D))
    rows_ = lambda w_: pl.BlockSpec((R, w_), lambda i: (i, 0))
    return pl.pallas_call(
        body, name=name, grid=(S // R,),
        out_shape=[_sds((S, D), BF), _sds((S, 2 * D_FF), BF), _sds((1, D), F32), _sds((1, D), F32)],
        in_specs=[rows_(D), rows_(D), rows_(D_FF), rows_(D_FF), _resident((D_FF, D)), vec, vec],
        out_specs=[rows_(D), rows_(2 * D_FF), vec, vec],
        compiler_params=_cp(1, 56),
    )(*_in_hbm(dh, y, dsilu_u, silu, w), gate, gp)


def _ffn_dn(dgu, wt, h, dh, sc, gp, name):
    S = h.shape[0]
    R = min(512, S)

    def body(dgu_ref, w_ref, h_ref, dh_ref, sc_ref, gp_ref, out_ref, dsh_ref, dsc_ref, dgp_ref):
        @pl.when(pl.program_id(0) == 0)
        def _():
            dsh_ref[...] = jnp.zeros_like(dsh_ref)
            dsc_ref[...] = jnp.zeros_like(dsc_ref)
            dgp_ref[...] = jnp.zeros_like(dgp_ref)

        for r0 in range(0, R, CHUNK):
            rows = slice(r0, r0 + CHUNK)
            dn = _dot(dgu_ref[rows, :], w_ref[...])
            dx, dsh, dsc, dgp = _prenorm_bwd(dn, h_ref[rows, :], gp_ref[...], sc_ref[...])
            out_ref[rows, :] = dh_ref[rows, :] + dx
            dsh_ref[...] += dsh
            dsc_ref[...] += dsc
            dgp_ref[...] += dgp

    vec = _const((1, D))
    rows_ = lambda w_: pl.BlockSpec((R, w_), lambda i: (i, 0))
    return pl.pallas_call(
        body, name=name, grid=(S // R,),
        out_shape=[_sds((S, D), F32)] + [_sds((1, D), F32)] * 3,
        in_specs=[rows_(2 * D_FF), _resident((2 * D_FF, D)), rows_(D), rows_(D), vec, vec],
        out_specs=[rows_(D), vec, vec, vec],
        compiler_params=_cp(1, 56),
    )(*_in_hbm(dgu, wt, h, dh), sc, gp)


def _ffn_bwd(dh, y, dsilu_u, silu, w, wt, h, gate, gpost, sc, gpre, name):
    S = dh.shape[0]
    R = min(256, S)

    def body(dh_ref, y_ref, g_ref, u_ref, w_ref, wt_ref, h_ref, gate_ref, gpost_ref, sc_ref, gpre_ref,
             dy_ref, dgu_ref, out_ref, dgate_ref, dgpost_ref, dsh_ref, dsc_ref, dgpre_ref):
        @pl.when(pl.program_id(0) == 0)
        def _():
            for r in (dgate_ref, dgpost_ref, dsh_ref, dsc_ref, dgpre_ref):
                r[...] = jnp.zeros_like(r)
        dhh = dh_ref[...]
        dy, dgate, dgpost = _postnorm_bwd(dhh, y_ref[...], gate_ref[...], gpost_ref[...], 0.5)
        dgate_ref[...] += dgate
        dgpost_ref[...] += dgpost
        dyb = dy.astype(BF)
        dy_ref[...] = dyb
        dn = None
        for f0, tf in F_TILES:
            da = _dot_nt(dyb, w_ref[f0:f0 + tf, :])
            dg = (da * g_ref[:, f0:f0 + tf].astype(F32)).astype(BF)
            du = (da * u_ref[:, f0:f0 + tf].astype(F32)).astype(BF)
            dgu_ref[:, f0:f0 + tf] = dg
            dgu_ref[:, D_FF + f0:D_FF + f0 + tf] = du
            part = _dot(dg, wt_ref[f0:f0 + tf, :]) + _dot(du, wt_ref[D_FF + f0:D_FF + f0 + tf, :])
            dn = part if dn is None else dn + part
        dx, dsh, dsc, dgpre = _prenorm_bwd(dn, h_ref[...], gpre_ref[...], sc_ref[...])
        out_ref[...] = dhh + dx
        dsh_ref[...] += dsh
        dsc_ref[...] += dsc
        dgpre_ref[...] += dgpre

    vec = _const((1, D))
    rows_ = lambda w_: pl.BlockSpec((R, w_), lambda i: (i, 0))
    return pl.pallas_call(
        body, name=name, grid=(S // R,),
        out_shape=[_sds((S, D), BF), _sds((S, 2 * D_FF), BF), _sds((S, D), F32)] + [_sds((1, D), F32)] * 5,
        in_specs=[rows_(D), rows_(D), rows_(D_FF), rows_(D_FF), _resident((D_FF, D)), _resident((2 * D_FF, D)),
                  rows_(D), vec, vec, vec, vec],
        out_specs=[rows_(D), rows_(2 * D_FF), rows_(D)] + [vec] * 5,
        compiler_params=_cp(1, 56),
    )(*_in_hbm(dh, y, dsilu_u, silu, w, wt, h), gate, gpost, sc, gpre)


def _tn_matmul(a, b, name, tm=None):
    S, M_all = a.shape
    N = b.shape[1]
    M = M_all if tm is None else tm
    GA = M_all // M
    ts = min(2048 if M * N <= 2 * D * D else 1024, S)
    nk = S // ts
    chunks = [(m0, min(CHUNK, M - m0)) for m0 in range(0, M, CHUNK)]

    def body(a_ref, b_ref, o_ref, acc):
        k = pl.program_id(1)

        @pl.when(k == 0)
        def _():
            acc[...] = jnp.zeros_like(acc)

        for m0, mc in chunks:
            acc[m0:m0 + mc, :] += _dot_tn(a_ref[:, m0:m0 + mc], b_ref[...])

        @pl.when(k == nk - 1)
        def _():
            for m0, mc in chunks:
                o_ref[m0:m0 + mc, :] = acc[m0:m0 + mc, :].astype(BF)

    return pl.pallas_call(
        body, name=name, grid=(GA, nk),
        out_shape=_sds((M_all, N), BF),
        in_specs=[pl.BlockSpec((ts, M), lambda ga, k: (k, ga)), pl.BlockSpec((ts, N), lambda ga, k: (k, 0))],
        out_specs=pl.BlockSpec((M, N), lambda ga, k: (ga, 0)),
        scratch_shapes=[pltpu.VMEM((M, N), F32)],
        compiler_params=_cp(2, 56),
    )(*_in_hbm(a, b))


def _mix_in(h, sh, sc, gp, w, wq):
    S = h.shape[0]
    R = min(512, S)

    def body(h_ref, sh_ref, sc_ref, gp_ref, w_ref, wq_ref, n_ref, qkv_ref, zg_ref, gates_ref):
        for r0 in range(0, R, CHUNK):
            rows = slice(r0, r0 + CHUNK)
            nb = _prenorm(h_ref[rows, :], gp_ref[...], sc_ref[...], sh_ref[...]).astype(BF)
            n_ref[rows, :] = nb
            qkv_ref[rows, 0:Q_W] = _dot_nt(nb, wq_ref[...]).astype(BF)
            qkv_ref[rows, Q_W:QKV_W] = _dot_nt(nb, w_ref[Q_W:QKV_W, :]).astype(BF)
            zg_ref[rows, :] = _dot_nt(nb, w_ref[ZG_OFF:GATE_OFF, :]).astype(BF)
            gates_ref[rows, :] = jax.nn.sigmoid(_dot_nt(nb, w_ref[GATE_OFF:IN_W, :])).astype(BF)

    vec = _const((1, D))
    rows = lambda w_: pl.BlockSpec((R, w_), lambda i: (i, 0))
    return pl.pallas_call(
        body, name="mix_in", grid=(S // R,),
        out_shape=[_sds((S, D), BF), _sds((S, QKV_W), BF), _sds((S, 2 * G_W), BF), _sds((S, 2 * D), BF)],
        in_specs=[rows(D), vec, vec, vec, _resident((IN_W, D)), _resident((Q_W, D))],
        out_specs=[rows(D), rows(QKV_W), rows(2 * G_W), rows(2 * D)],
        compiler_params=_cp(1, 48),
    )(*_in_hbm(h), sh, sc, gp, *_in_hbm(w, wq))


def _bias_table(rel_bias, bucket):
    def body(rel_ref, bk_ref, out_ref):
        bk = bk_ref[...]
        qi = lax.broadcasted_iota(jnp.int32, (BLK, 2 * BLK), 0)
        kj = lax.broadcasted_iota(jnp.int32, (BLK, 2 * BLK), 1)
        dist = qi + BLK - kj
        window = (dist >= 0) & (dist < BLK)
        for h in range(N_HEADS):
            acc = jnp.zeros((BLK, 2 * BLK), F32)
            for b in range(N_BUCKETS):
                acc = jnp.where(bk == b, rel_ref[b, h], acc)
            out_ref[h // GROUP, pl.ds((h % GROUP) * BLK, BLK), :] = jnp.where(window, acc, NEG)

    return pl.pallas_call(
        body, name="bias_table",
        out_shape=_sds((N_KV, GROUP * BLK, 2 * BLK), F32),
        in_specs=[pl.BlockSpec(memory_space=pltpu.SMEM), pl.BlockSpec(memory_space=pltpu.VMEM)],
        out_specs=pl.BlockSpec(memory_space=pltpu.VMEM),
    )(rel_bias, bucket)


ATT_TB = 4


HEAD_ROWS = N_HEADS * BLK


def _pair_heads(w):
    return jnp.transpose(w.reshape(N_KV, GROUP, HD, w.shape[1]), (1, 0, 2, 3)).reshape(w.shape)


def _unpair_heads(w):
    return jnp.transpose(w.reshape(GROUP, N_KV, HD, w.shape[1]), (1, 0, 2, 3)).reshape(w.shape)


def _halves(x, scale=1.0):
    low = lax.broadcasted_iota(jnp.int32, x.shape, 1) < HD
    xf = x.astype(F32) * scale
    return jnp.where(low, xf, 0.0).astype(BF), jnp.where(low, 0.0, xf).astype(BF)


def _stack_heads(x, scale=1.0):
    halves = [_halves(x[:, g * 128:(g + 1) * 128], scale) for g in range(GROUP)]
    return jnp.concatenate([lo for lo, _ in halves] + [hi for _, hi in halves], axis=0)


def _attn_probs(q, kvc, kvp, bias_ref, sink_ref, has_prev):
    kv2 = jnp.concatenate([kvp, kvc], axis=0)
    kboth, vboth = kv2[:, 0:KV_W], kv2[:, KV_W:2 * KV_W]
    qpad = _stack_heads(q, SCALE)
    s = _dot_nt(qpad, kboth) + bias_ref[...]
    if has_prev is not None:
        col = lax.broadcasted_iota(jnp.int32, (HEAD_ROWS, 2 * BLK), 1)
        s = jnp.where((col >= BLK) | has_prev, s, NEG)
    row_head = lax.broadcasted_iota(jnp.int32, (HEAD_ROWS, 1), 0) // BLK
    sink = jnp.zeros((HEAD_ROWS, 1), F32)
    for h in range(N_HEADS):
        sink = jnp.where(row_head == h, sink_ref[h], sink)
    m = jnp.maximum(jnp.max(s, axis=1, keepdims=True), sink)
    p = jnp.exp(s - m)
    e_sink = jnp.exp(sink - m)
    inv = 1.0 / (jnp.sum(p, axis=1, keepdims=True) + e_sink)
    return qpad, kboth, vboth, p * inv, e_sink * inv


def _attn_fwd(qkv, bias, sinks):
    S = qkv.shape[0]
    tb = min(ATT_TB, S // BLK)
    T = tb * BLK

    def body(sink_ref, q_ref, kv_ref, kvp_ref, bias_ref, o_ref):
        step = pl.program_id(0)
        for j in range(tb):
            rows = slice(j * BLK, (j + 1) * BLK)
            kvp = kvp_ref[...] if j == 0 else kv_ref[(j - 1) * BLK:j * BLK, :]
            has_prev = (step > 0) if j == 0 else None
            _, _, vboth, prob, _ = _attn_probs(q_ref[rows, :], kv_ref[rows, :], kvp, bias_ref, sink_ref, has_prev)
            pb = prob.astype(BF)
            v_low, v_high = _halves(vboth)
            half = HEAD_ROWS // 2
            o = _dot(pb[0:half], v_low) + _dot(pb[half:HEAD_ROWS], v_high)
            for g in range(GROUP):
                o_ref[rows, g * 128:(g + 1) * 128] = o[g * BLK:(g + 1) * BLK].astype(BF)

    return pl.pallas_call(
        body, name="attn_fwd", grid=(S // T,),
        out_shape=_sds((S, Q_W), BF),
        in_specs=[pl.BlockSpec(memory_space=pltpu.SMEM),
                  pl.BlockSpec((T, Q_W), lambda i: (i, 0)),
                  pl.BlockSpec((T, 2 * KV_W), lambda i: (i, 2)),
                  pl.BlockSpec((BLK, 2 * KV_W), lambda i: (jnp.maximum(i * tb - 1, 0), 2)),
                  _const((HEAD_ROWS, 2 * BLK))],
        out_specs=pl.BlockSpec((T, Q_W), lambda i: (i, 0)),
        compiler_params=_cp(1, 32),
    )(sinks, *_in_hbm(qkv, qkv, qkv, bias))


def _attn_bwd(qkv, bias, sinks, do):
    S = qkv.shape[0]
    tb = min(ATT_TB, S // BLK)
    T = tb * BLK
    nt = S // T
    half = HEAD_ROWS // 2

    def body(sink_ref, q_ref, kv_ref, kvp_ref, bias_ref, do_ref, dq_ref, dkv_ref, dbias_ref, dsink_ref, carry):
        i = pl.program_id(0)

        @pl.when(i == 0)
        def _():
            carry[...] = jnp.zeros_like(carry)
            dbias_ref[...] = jnp.zeros_like(dbias_ref)
            dsink_ref[...] = jnp.zeros_like(dsink_ref)

        from_next = carry[...]
        head_row = lax.broadcasted_iota(jnp.int32, (N_HEADS, 128), 0)
        low = lax.broadcasted_iota(jnp.int32, (BLK, 128), 1) < HD
        for j in reversed(range(tb)):
            rows = slice(j * BLK, (j + 1) * BLK)
            kvp = kvp_ref[...] if j == 0 else kv_ref[(j - 1) * BLK:j * BLK, :]
            has_prev = (i < nt - 1) if j == 0 else None
            qpad, kboth, vboth, prob, p_sink = _attn_probs(q_ref[rows, :], kv_ref[rows, :], kvp, bias_ref, sink_ref,
                                                           has_prev)
            pb = prob.astype(BF)
            dopad = _stack_heads(do_ref[rows, :])
            dp = _dot_nt(dopad, vboth)
            delta = jnp.sum(prob * dp, axis=1, keepdims=True)
            ds = prob * (dp - delta)
            dbias_ref[...] += ds
            sink_term = p_sink * delta
            dsink_rows = jnp.zeros((N_HEADS, 128), F32)
            for h in range(N_HEADS):
                val = -jnp.sum(sink_term[h * BLK:(h + 1) * BLK], axis=0, keepdims=True)
                dsink_rows = jnp.where(head_row == h, val, dsink_rows)
            dsink_ref[...] += dsink_rows
            dsb = ds.astype(BF)
            dqpad = _dot(dsb, kboth) * SCALE
            for g in range(GROUP):
                dq_ref[rows, g * 128:(g + 1) * 128] = jnp.where(
                    low, dqpad[g * BLK:(g + 1) * BLK], dqpad[half + g * BLK:half + (g + 1) * BLK]).astype(BF)
            dkv2 = jnp.concatenate([jnp.transpose(_dot_tn(qpad, dsb)),
                                    jnp.transpose(_dot_tn(dopad, pb))], axis=1)
            dkv_ref[rows, :] = (dkv2[BLK:2 * BLK] + from_next).astype(BF)
            from_next = dkv2[0:BLK]
        carry[...] = from_next

    return pl.pallas_call(
        body, name="attn_bwd", grid=(nt,),
        out_shape=[_sds((S, Q_W), BF), _sds((S, 2 * KV_W), BF),
                   _sds((HEAD_ROWS, 2 * BLK), F32), _sds((N_HEADS, 128), F32)],
        in_specs=[pl.BlockSpec(memory_space=pltpu.SMEM),
                  pl.BlockSpec((T, Q_W), lambda i: (nt - 1 - i, 0)),
                  pl.BlockSpec((T, 2 * KV_W), lambda i: (nt - 1 - i, 2)),
                  pl.BlockSpec((BLK, 2 * KV_W), lambda i: (jnp.maximum((nt - 1 - i) * tb - 1, 0), 2)),
                  _const((HEAD_ROWS, 2 * BLK)),
                  pl.BlockSpec((T, Q_W), lambda i: (nt - 1 - i, 0))],
        out_specs=[pl.BlockSpec((T, Q_W), lambda i: (nt - 1 - i, 0)),
                   pl.BlockSpec((T, 2 * KV_W), lambda i: (nt - 1 - i, 0)),
                   _const((HEAD_ROWS, 2 * BLK)), _const((N_HEADS, 128))],
        scratch_shapes=[pltpu.VMEM((BLK, 2 * KV_W), F32)],
        compiler_params=_cp(1, 32),
    )(sinks, *_in_hbm(qkv, qkv, qkv, bias, do))


def _rel_bias_grad(dbias, bucket):
    def body(db_ref, bk_ref, out_ref):
        bk = bk_ref[...]
        lane = lax.broadcasted_iota(jnp.int32, (1, 128), 1)
        for h in range(N_HEADS):
            d = db_ref[h // GROUP, pl.ds((h % GROUP) * BLK, BLK), :]
            row = jnp.zeros((1, 128), F32)
            for b in range(N_BUCKETS):
                tot = jnp.sum(jnp.sum(jnp.where(bk == b, d, 0.0), axis=1, keepdims=True), axis=0, keepdims=True)
                row = jnp.where(lane == b, tot, row)
            out_ref[pl.ds(h, 1), :] = row

    vm = pl.BlockSpec(memory_space=pltpu.VMEM)
    return pl.pallas_call(body, name="rel_bias_grad", out_shape=_sds((N_HEADS, 128), F32),
                          in_specs=[vm, vm], out_specs=vm)(dbias, bucket)


def _gmlp_parts(zg, lg_ref, lb_ref):
    z = zg.astype(F32)
    ge = _gelu(z)
    u, vg = ge[:, 0:G_W], ge[:, G_W:2 * G_W]
    mu = jnp.mean(vg, axis=-1, keepdims=True)
    xc = vg - mu
    rstd = lax.rsqrt(jnp.mean(xc * xc, axis=-1, keepdims=True) + EPS)
    xh = xc * rstd
    return z, u, xh, rstd, xh * lg_ref[...] + lb_ref[...]


def _causal_weights(ws_ref, wc):
    t = lax.broadcasted_iota(jnp.int32, (BLK, BLK), 0)
    s = lax.broadcasted_iota(jnp.int32, (BLK, BLK), 1)
    for g in range(N_HEADS):
        wc[g] = jnp.where(s <= t, ws_ref[g], 0.0).astype(BF)


def _spatial(vb, wc, bst_ref, p, low):
    xp = vb[:, p * 128:(p + 1) * 128]
    s0 = _dot(wc[2 * p], xp) + bst_ref[:, 2 * p:2 * p + 1]
    s1 = _dot(wc[2 * p + 1], xp) + bst_ref[:, 2 * p + 1:2 * p + 2]
    return xp, jnp.where(low, s0, s1)


def _gmlp_fwd(zg, lg, lb, ws, bst):
    S = zg.shape[0]
    tb = min(ATT_TB, S // BLK)
    T = tb * BLK

    def body(zg_ref, lg_ref, lb_ref, ws_ref, bst_ref, o_ref, wc):
        @pl.when(pl.program_id(0) == 0)
        def _():
            _causal_weights(ws_ref, wc)
        low = lax.broadcasted_iota(jnp.int32, (BLK, 128), 1) < HD
        for j in range(tb):
            rows = slice(j * BLK, (j + 1) * BLK)
            _, u, _, _, vln = _gmlp_parts(zg_ref[rows, :], lg_ref, lb_ref)
            vb = vln.astype(BF)
            for p in range(4):
                _, sp = _spatial(vb, wc, bst_ref, p, low)
                o_ref[rows, p * 128:(p + 1) * 128] = (u[:, p * 128:(p + 1) * 128] * sp).astype(BF)

    return pl.pallas_call(
        body, name="gmlp_fwd", grid=(S // T,),
        out_shape=_sds((S, G_W), BF),
        in_specs=[pl.BlockSpec((T, 2 * G_W), lambda i: (i, 0)), _const((1, G_W)), _const((1, G_W)),
                  _const((N_HEADS, BLK, BLK)), _const((BLK, N_HEADS))],
        out_specs=pl.BlockSpec((T, G_W), lambda i: (i, 0)),
        scratch_shapes=[pltpu.VMEM((N_HEADS, BLK, BLK), BF)],
        compiler_params=_cp(1, 32),
    )(*_in_hbm(zg), lg, lb, ws, bst)


def _gmlp_bwd(zg, d_out, lg, lb, ws, bst):
    S = zg.shape[0]
    tb = min(ATT_TB, S // BLK)
    T = tb * BLK
    nb = S // T

    def body(zg_ref, d_ref, lg_ref, lb_ref, ws_ref, bst_ref, dzg_ref, dws_ref, dbs_ref, dlg_ref, dlb_ref, wc, dbacc):
        i = pl.program_id(0)

        @pl.when(i == 0)
        def _():
            _causal_weights(ws_ref, wc)
            dws_ref[...] = jnp.zeros_like(dws_ref)
            dlg_ref[...] = jnp.zeros_like(dlg_ref)
            dlb_ref[...] = jnp.zeros_like(dlb_ref)
            dbacc[...] = jnp.zeros_like(dbacc)

        low = lax.broadcasted_iota(jnp.int32, (BLK, 128), 1) < HD
        for j in range(tb):
            rows = slice(j * BLK, (j + 1) * BLK)
            z, u, xh, rstd, vln = _gmlp_parts(zg_ref[rows, :], lg_ref, lb_ref)
            vb = vln.astype(BF)
            d = d_ref[rows, :].astype(F32)
            du_parts, dvln_parts = [], []
            for p in range(4):
                xp, sp = _spatial(vb, wc, bst_ref, p, low)
                dp = d[:, p * 128:(p + 1) * 128]
                du_parts.append(dp * sp)
                dsp = dp * u[:, p * 128:(p + 1) * 128]
                dbacc[:, p * 128:(p + 1) * 128] += dsp
                d0 = jnp.where(low, dsp, 0.0).astype(BF)
                d1 = jnp.where(low, 0.0, dsp).astype(BF)
                dws_ref[2 * p] += _dot_nt(d0, xp)
                dws_ref[2 * p + 1] += _dot_nt(d1, xp)
                dvln_parts.append(_dot_tn(wc[2 * p], d0) + _dot_tn(wc[2 * p + 1], d1))
            dvln = jnp.concatenate(dvln_parts, axis=1)
            dlg_ref[...] += _colsum(dvln * xh)
            dlb_ref[...] += _colsum(dvln)
            dxh = dvln * lg_ref[...]
            dvg = rstd * (dxh - jnp.mean(dxh, axis=-1, keepdims=True)
                          - xh * jnp.mean(dxh * xh, axis=-1, keepdims=True))
            dge = jnp.concatenate(du_parts + [dvg], axis=1)
            dzg_ref[rows, :] = (dge * _gelu_grad(z)).astype(BF)

        @pl.when(i == nb - 1)
        def _():
            t = lax.broadcasted_iota(jnp.int32, (BLK, BLK), 0)
            s = lax.broadcasted_iota(jnp.int32, (BLK, BLK), 1)
            for g in range(N_HEADS):
                dws_ref[g] = jnp.where(s <= t, dws_ref[g], 0.0)
            grp = lax.broadcasted_iota(jnp.int32, (N_HEADS, G_W), 0)
            lane = lax.broadcasted_iota(jnp.int32, (N_HEADS, G_W), 1) // HD
            pick = jnp.where(grp == lane, 1.0, 0.0).astype(F32)
            dbs_ref[...] = lax.dot_general(pick, dbacc[...], (((1,), (1,)), ((), ())),
                                           preferred_element_type=F32, precision=HIGH)

    return pl.pallas_call(
        body, name="gmlp_bwd", grid=(nb,),
        out_shape=[_sds((S, 2 * G_W), BF), _sds((N_HEADS, BLK, BLK), F32), _sds((N_HEADS, BLK), F32),
                   _sds((1, G_W), F32), _sds((1, G_W), F32)],
        in_specs=[pl.BlockSpec((T, 2 * G_W), lambda i: (i, 0)), pl.BlockSpec((T, G_W), lambda i: (i, 0)),
                  _const((1, G_W)), _const((1, G_W)), _const((N_HEADS, BLK, BLK)), _const((BLK, N_HEADS))],
        out_specs=[pl.BlockSpec((T, 2 * G_W), lambda i: (i, 0)), _const((N_HEADS, BLK, BLK)),
                   _const((N_HEADS, BLK)), _const((1, G_W)), _const((1, G_W))],
        scratch_shapes=[pltpu.VMEM((N_HEADS, BLK, BLK), BF), pltpu.VMEM((BLK, G_W), F32)],
        compiler_params=_cp(1, 32),
    )(*_in_hbm(zg, d_out), lg, lb, ws, bst)


def _mix_out(o, gm, gates, h, wa, wg, wo, gate, gp):
    S = h.shape[0]
    R = min(512, S)

    def body(o_ref, gm_ref, gates_ref, h_ref, wa_ref, wg_ref, wo_ref, gate_ref, gp_ref,
             ya_ref, yg_ref, ym_ref, y_ref, hn_ref):
        for r0 in range(0, R, CHUNK):
            rows = slice(r0, r0 + CHUNK)
            ya = _dot(o_ref[rows, :], wa_ref[...])
            yg = _dot(gm_ref[rows, :], wg_ref[...])
            ya_ref[rows, :] = ya.astype(BF)
            yg_ref[rows, :] = yg.astype(BF)
            ym = (gates_ref[rows, 0:D].astype(F32) * ya + gates_ref[rows, D:2 * D].astype(F32) * yg).astype(BF)
            ym_ref[rows, :] = ym
            y = _dot(ym, wo_ref[...])
            y_ref[rows, :] = y.astype(BF)
            hn_ref[rows, :] = h_ref[rows, :] + gate_ref[...] * (y * _rms_r(y) * gp_ref[...])

    vec = _const((1, D))
    rows = lambda w_: pl.BlockSpec((R, w_), lambda i: (i, 0))
    return pl.pallas_call(
        body, name="mix_out", grid=(S // R,),
        out_shape=[_sds((S, D), BF)] * 4 + [_sds((S, D), F32)],
        in_specs=[rows(Q_W), rows(G_W), rows(2 * D), rows(D), _resident((Q_W, D)), _resident((G_W, D)),
                  _resident((D, D)), vec, vec],
        out_specs=[rows(D)] * 5,
        compiler_params=_cp(1, 48),
    )(*_in_hbm(o, gm, gates, h, wa, wg, wo), gate, gp)


def _mix_out_bwd(dh, y, ya, yg, gates, att, gm, ymix, wa, wg, wo, gate, gp):
    S = dh.shape[0]
    R = min(512, S)
    nb = S // R

    def body(dh_ref, y_ref, ya_ref, yg_ref, gates_ref, att_ref, gm_ref, ym_ref, wa_ref, wg_ref, wo_ref,
             gate_ref, gp_ref, dz_ref, do_ref, dgm_ref, dgate_ref, dgp_ref, gwo_ref, gwa_ref, gwg_ref,
             acc_o, acc_a, acc_g, dy_scr, dya_scr, dyg_scr):
        i = pl.program_id(0)

        @pl.when(i == 0)
        def _():
            for r in (dgate_ref, dgp_ref, acc_o, acc_a, acc_g):
                r[...] = jnp.zeros_like(r)
        for r0 in range(0, R, CHUNK):
            rows = slice(r0, r0 + CHUNK)
            dy, dgate, dgp = _postnorm_bwd(dh_ref[rows, :], y_ref[rows, :], gate_ref[...], gp_ref[...], 1.0)
            dgate_ref[...] += dgate
            dgp_ref[...] += dgp
            dyb = dy.astype(BF)
            dy_scr[rows, :] = dyb
            dym = _dot_nt(dyb, wo_ref[...])
            ga = gates_ref[rows, 0:D].astype(F32)
            gg = gates_ref[rows, D:2 * D].astype(F32)
            dya = (dym * ga).astype(BF)
            dyg = (dym * gg).astype(BF)
            dya_scr[rows, :] = dya
            dyg_scr[rows, :] = dyg
            dz_ref[rows, 0:D] = (dym * ya_ref[rows, :].astype(F32) * (ga * (1.0 - ga))).astype(BF)
            dz_ref[rows, D:2 * D] = (dym * yg_ref[rows, :].astype(F32) * (gg * (1.0 - gg))).astype(BF)
            do_ref[rows, :] = _dot_nt(dya, wa_ref[...]).astype(BF)
            dgm_ref[rows, :] = _dot_nt(dyg, wg_ref[...]).astype(BF)
        for m0 in range(0, D, CHUNK):
            acc_o[m0:m0 + CHUNK, :] += _dot_tn(ym_ref[:, m0:m0 + CHUNK], dy_scr[...])
        for m0 in range(0, Q_W, CHUNK):
            acc_a[m0:m0 + CHUNK, :] += _dot_tn(att_ref[:, m0:m0 + CHUNK], dya_scr[...])
            acc_g[m0:m0 + CHUNK, :] += _dot_tn(gm_ref[:, m0:m0 + CHUNK], dyg_scr[...])

        @pl.when(i == nb - 1)
        def _():
            for m0 in range(0, D, CHUNK):
                gwo_ref[m0:m0 + CHUNK, :] = acc_o[m0:m0 + CHUNK, :].astype(BF)
            for m0 in range(0, Q_W, CHUNK):
                gwa_ref[m0:m0 + CHUNK, :] = acc_a[m0:m0 + CHUNK, :].astype(BF)
                gwg_ref[m0:m0 + CHUNK, :] = acc_g[m0:m0 + CHUNK, :].astype(BF)

    vec = _const((1, D))
    rows = lambda w_: pl.BlockSpec((R, w_), lambda i: (i, 0))
    return pl.pallas_call(
        body, name="mix_out_bwd", grid=(nb,),
        out_shape=[_sds((S, 2 * D), BF), _sds((S, Q_W), BF), _sds((S, G_W), BF), _sds((1, D), F32),
                   _sds((1, D), F32), _sds((D, D), BF), _sds((Q_W, D), BF), _sds((G_W, D), BF)],
        in_specs=[rows(D), rows(D), rows(D), rows(D), rows(2 * D), rows(Q_W), rows(G_W), rows(D),
                  _resident((Q_W, D)), _resident((G_W, D)), _resident((D, D)), vec, vec],
        out_specs=[rows(2 * D), rows(Q_W), rows(G_W), vec, vec, _const((D, D)), _const((Q_W, D)),
                   _const((G_W, D))],
        scratch_shapes=[pltpu.VMEM((D, D), F32), pltpu.VMEM((Q_W, D), F32), pltpu.VMEM((G_W, D), F32)]
        + [pltpu.VMEM((R, D), BF)] * 3,
        compiler_params=_cp(1, 60),
    )(*_in_hbm(dh, y, ya, yg, gates, att, gm, ymix, wa, wg, wo), gate, gp)


def _mix_dn(dq, dkv, dzg, dzgate, w, wq, h, dh, sc, gp):
    S = h.shape[0]
    R = min(512, S)

    def body(dq_ref, dkv_ref, dzg_ref, dzt_ref, w_ref, wq_ref, h_ref, dh_ref, sc_ref, gp_ref,
             out_ref, dsh_ref, dsc_ref, dgp_ref):
        @pl.when(pl.program_id(0) == 0)
        def _():
            dsh_ref[...] = jnp.zeros_like(dsh_ref)
            dsc_ref[...] = jnp.zeros_like(dsc_ref)
            dgp_ref[...] = jnp.zeros_like(dgp_ref)
        for r0 in range(0, R, CHUNK):
            rows = slice(r0, r0 + CHUNK)
            dn = _dot(dq_ref[rows, :], wq_ref[...])
            dn = dn + _dot(dkv_ref[rows, :], w_ref[Q_W:QKV_W, :])
            dn = dn + _dot(dzg_ref[rows, :], w_ref[ZG_OFF:GATE_OFF, :])
            dn = dn + _dot(dzt_ref[rows, :], w_ref[GATE_OFF:IN_W, :])
            dx, dsh, dsc, dgp = _prenorm_bwd(dn, h_ref[rows, :], gp_ref[...], sc_ref[...])
            out_ref[rows, :] = dh_ref[rows, :] + dx
            dsh_ref[...] += dsh
            dsc_ref[...] += dsc
            dgp_ref[...] += dgp

    vec = _const((1, D))
    rows = lambda w_: pl.BlockSpec((R, w_), lambda i: (i, 0))
    return pl.pallas_call(
        body, name="mix_dn", grid=(S // R,),
        out_shape=[_sds((S, D), F32)] + [_sds((1, D), F32)] * 3,
        in_specs=[rows(Q_W), rows(2 * KV_W), rows(2 * G_W), rows(2 * D), _resident((IN_W, D)),
                  _resident((Q_W, D)), rows(D), rows(D), vec, vec],
        out_specs=[rows(D), vec, vec, vec],
        compiler_params=_cp(1, 48),
    )(*_in_hbm(dq, dkv, dzg, dzgate, w, wq, h, dh), sc, gp)


def _adamw_math(w, g, m, v):
    m2 = ADAM_B1 * m + (1.0 - ADAM_B1) * g
    v2 = ADAM_B2 * v + (1.0 - ADAM_B2) * (g * g)
    m_hat = m2 / (1.0 - ADAM_B1 ** ADAM_STEP)
    v_hat = v2 / (1.0 - ADAM_B2 ** ADAM_STEP)
    delta = -ADAM_LR * (m_hat / (jnp.sqrt(v_hat) + ADAM_EPS) + ADAM_WD * w)
    return delta, m2, v2


def _row_tile(rows, cols):
    best = None
    for t in range(16, rows + 1, 16):
        if rows % t == 0 and t * cols <= 256 * 1024:
            best = t
    return best if best is not None else rows


def _adamw_sharded(landing, w, m, v, name):
    r, c = w.shape
    tr = _row_tile(r, c)

    def body(l_ref, w_ref, m_ref, v_ref, g_ref, d_ref, m2_ref, v2_ref):
        g = l_ref[0].astype(F32)
        for j in range(1, N_DEV):
            g = g + l_ref[j].astype(F32)
        delta, m2, v2 = _adamw_math(w_ref[...], g, m_ref[...], v_ref[...])
        g_ref[...] = g
        d_ref[...] = delta
        m2_ref[...] = m2
        v2_ref[...] = v2

    row = pl.BlockSpec((tr, c), lambda i: (i, 0))
    return pl.pallas_call(
        body, name=name, grid=(r // tr,),
        out_shape=[_sds((r, c), F32)] * 4,
        in_specs=[pl.BlockSpec((N_DEV, tr, c), lambda i: (0, i, 0)), row, row, row],
        out_specs=[row] * 4,
        compiler_params=_cp(1, 48),
    )(*_in_hbm(landing, w, m, v))


def _adamw_small(items):
    n = len(items)

    def body(*refs):
        for k in range(n):
            w_ref, g_ref, m_ref, v_ref = refs[4 * k:4 * k + 4]
            outs = refs[4 * n + 3 * k:4 * n + 3 * k + 3]
            for o_ref, val in zip(outs, _adamw_math(w_ref[...], g_ref[...], m_ref[...], v_ref[...])):
                o_ref[...] = val

    vm = pl.BlockSpec(memory_space=pltpu.VMEM)
    flat = pl.pallas_call(
        body, name="adamw_small",
        out_shape=[_sds(it[0].shape, F32) for it in items for _ in range(3)],
        in_specs=[vm] * (4 * n), out_specs=[vm] * (3 * n),
    )(*[a for it in items for a in it])
    return [tuple(flat[3 * k:3 * k + 3]) for k in range(n)]


def _w_ada_update(c8, d_ada, w, m, v):
    tr = 256

    def body(c_ref, d_ref, w_ref, m_ref, v_ref, g_ref, dl_ref, m2_ref, v2_ref):
        cs = c_ref[...]
        cs = cs * jax.nn.sigmoid(cs)
        g = lax.dot_general(cs, d_ref[...], (((0,), (0,)), ((), ())), preferred_element_type=F32, precision=HIGH)
        delta, m2, v2 = _adamw_math(w_ref[...], g, m_ref[...], v_ref[...])
        g_ref[...] = g
        dl_ref[...] = delta
        m2_ref[...] = m2
        v2_ref[...] = v2

    row = pl.BlockSpec((tr, ADA_W), lambda i: (i, 0))
    return pl.pallas_call(
        body, name="w_ada_update", grid=(D // tr,),
        out_shape=[_sds((D, ADA_W), F32)] * 4,
        in_specs=[pl.BlockSpec((N_DEV, tr), lambda i: (0, i)), _const((N_DEV, ADA_W)), row, row, row],
        out_specs=[row] * 4,
        compiler_params=_cp(1, 40),
    )(c8, d_ada, *_in_hbm(w, m, v))


def _t5_bucket():
    qi = np.arange(BLK, dtype=np.int32)[:, None]
    kj = np.arange(2 * BLK, dtype=np.int32)[None, :]
    dist = np.maximum(qi + BLK - kj, 0)
    max_exact = N_BUCKETS // 2
    d_f = np.maximum(dist, max_exact).astype(np.float32)
    large = max_exact + (np.log(d_f / np.float32(max_exact)) / np.float32(math.log(MAX_DISTANCE / max_exact))
                         * np.float32(N_BUCKETS - max_exact)).astype(np.int32)
    large = np.minimum(large, N_BUCKETS - 1)
    return jnp.asarray(np.where(dist < max_exact, dist, large).astype(np.int32))


def _slabs_of_columns(w):
    r, c8 = w.shape
    return jnp.transpose(w.reshape(r, N_DEV, c8 // N_DEV), (1, 0, 2))


def _columns_of_slabs(w8):
    _, r, c = w8.shape
    return jnp.transpose(w8, (1, 0, 2)).reshape(r, N_DEV * c)


def kernel(x, c, rel_bias, w_ada, b_ada, pre_norm_g, post_norm_g, w_ffn1_in, w_ffn1_out, w_in, sinks, gmlp_ln_g, gmlp_ln_b, gmlp_w_s, gmlp_b_s, w_br_attn, w_br_gmlp, w_out, w_ffn2_in, w_ffn2_out, loss_target, m_rel_bias, m_w_ada, m_b_ada, m_pre_norm_g, m_post_norm_g, m_w_ffn1_in, m_w_ffn1_out, m_w_in, m_sinks, m_gmlp_ln_g, m_gmlp_ln_b, m_gmlp_w_s, m_gmlp_b_s, m_w_br_attn, m_w_br_gmlp, m_w_out, m_w_ffn2_in, m_w_ffn2_out, v_rel_bias, v_w_ada, v_b_ada, v_pre_norm_g, v_post_norm_g, v_w_ffn1_in, v_w_ffn1_out, v_w_in, v_sinks, v_gmlp_ln_g, v_gmlp_ln_b, v_gmlp_w_s, v_gmlp_b_s, v_w_br_attn, v_w_br_gmlp, v_w_out, v_w_ffn2_in, v_w_ffn2_out):
    me = 4 * lax.axis_index("x") + 2 * lax.axis_index("y") + lax.axis_index("c")
    x0 = x[0]
    target = loss_target[0]

    transposed = ("w_ffn1_in", "w_in", "w_ffn2_in")
    shards = [w_ffn1_in[0].T, w_ffn1_out[0], w_in[0].T, w_br_attn[0], w_br_gmlp[0], w_out[0],
              w_ffn2_in[0].T, w_ffn2_out[0]]
    shards_bf = [s.astype(BF) for s in shards]
    groups = [shards_bf[0:1], shards_bf[1:6], shards_bf[6:8]]

    def gather_start(i, after):
        return _slabs_start("gather", groups[i], after, "gather_start_%d" % i)

    def forward_start(st, i, after):
        lands = _slabs_wait("gather", len(groups[i]), st, after, "gather_wait_%d" % i)
        return _slabs_start("forward", lands, c, "forward_start_%d" % i)

    def gathered(st, i, after):
        return _slabs_wait("forward", len(groups[i]), st, after, "forward_wait_%d" % i)

    gs0 = gather_start(0, c)

    mine = jnp.concatenate([c[0], pre_norm_g[0].reshape(-1), post_norm_g[0].reshape(-1)])
    small8 = jnp.broadcast_to(mine[None, :], (8, mine.shape[0]))
    b_ada64 = jnp.repeat(b_ada.reshape(N_DEV, ADA_W), 8, axis=0)
    gath, ada64 = _ada_forward(small8, w_ada[0], b_ada64)
    gath8 = gath[::8]
    ada = ada64[::8].reshape(9, D)
    sh1, sc1, g1, sh2, sc2, g2, sh3, sc3, g3 = [ada[k:k + 1] for k in range(9)]
    gains = gath8[:, D:].reshape(N_DEV, 2, 3, 128)
    pre_g = jnp.transpose(gains[:, 0], (1, 0, 2)).reshape(3, D)
    post_g = jnp.transpose(gains[:, 1], (1, 0, 2)).reshape(3, D)
    pre = [pre_g[k:k + 1] for k in range(3)]
    post = [post_g[k:k + 1] for k in range(3)]

    bucket = _t5_bucket()
    bias = _bias_table(rel_bias, bucket).reshape(HEAD_ROWS, 2 * BLK)
    sinks8 = sinks[0]
    lg, lb = gmlp_ln_g, gmlp_ln_b
    ws = gmlp_w_s[0]
    bst = jnp.transpose(gmlp_b_s[0])

    fs0 = forward_start(gs0, 0, sh1)
    gs1 = gather_start(1, fs0[-1])
    wf1_in = gathered(fs0, 0, gs1[-1])[0].reshape(2 * D_FF, D)
    n1, fg1, fu1, fa1 = _ffn_in(x0, sh1, sc1, pre[0], wf1_in, "ffn1_in")
    fs1 = forward_start(gs1, 1, n1)
    gs2 = gather_start(2, fs1[-1])
    mix_w = gathered(fs1, 1, gs2[-1])
    wf1_out = mix_w[0].reshape(D_FF, D)
    w_in_full = mix_w[1].reshape(IN_W, D)
    w_q = _pair_heads(w_in_full[0:Q_W])
    w_bra = _pair_heads(_columns_of_slabs(mix_w[2]))
    w_brg = _columns_of_slabs(mix_w[3])
    w_out_full = mix_w[4].reshape(D, D)
    h1, y1 = _ffn_out(fa1, wf1_out, x0, g1, post[0], "ffn1_out")
    n2, qkv, zg, gates = _mix_in(h1, sh2, sc2, pre[1], w_in_full, w_q)
    att = _attn_fwd(qkv, bias, sinks8)
    gm = _gmlp_fwd(zg, lg, lb, ws, bst)
    fs2 = forward_start(gs2, 2, gm)
    ya, yg, ymix, y2, h2 = _mix_out(att, gm, gates, h1, w_bra, w_brg, w_out_full, g2 + fs2[-1], post[1])
    wf2_in, wf2_out = gathered(fs2, 2, h2)
    wf2_in = wf2_in.reshape(2 * D_FF, D)
    wf2_out = wf2_out.reshape(D_FF, D)
    n3, fg3, fu3, fa3 = _ffn_in(h2, sh3, sc3, pre[2], wf2_in, "ffn2_in")
    dh3, y3, sq = _ffn_out(fa3, wf2_out, h2, g3, post[2], "ffn2_out", target=target)

    def exchange_start(i, arrays):
        return _slabs_start("exchange", arrays, sq, "exchange_start_%d" % i)

    dy3, dgu3, dh2, d_g3, d_post2, d_sh3, d_sc3, d_pre2 = _ffn_bwd(
        dh3, y3, fg3, fu3, wf2_out, wf2_in, h2, g3, post[2], sc3, pre[2], "ffn2_bwd")
    gw_f2_out = _tn_matmul(fa3, dy3, "ffn2_out_wgrad", tm=D_FF // 2).reshape(N_DEV, D_FF // N_DEV, D)
    gw_f2_in = _tn_matmul(dgu3, n3, "ffn2_in_wgrad", tm=D_FF // 2).reshape(N_DEV, FS, D)
    ex1 = exchange_start(1, [gw_f2_out, gw_f2_in])

    dzgate, d_att, d_gm, d_g2, d_post1, gw_out, gw_bra, gw_brg = _mix_out_bwd(
        dh2, y2, ya, yg, gates, att, gm, ymix, w_bra, w_brg, w_out_full, g2 + ex1[-1], post[1])
    ex2 = exchange_start(2, [_slabs_of_columns(_unpair_heads(gw_bra)), _slabs_of_columns(gw_brg),
                             gw_out.reshape(N_DEV, D // N_DEV, D)])
    dq, dkv, dbias, dsink = _attn_bwd(qkv, bias, sinks8, d_att)
    dzg, d_ws, d_bs, d_lg, d_lb = _gmlp_bwd(zg, d_gm, lg, lb, ws, bst)
    d_rel = _rel_bias_grad(dbias.reshape(N_KV, GROUP * BLK, 2 * BLK), bucket)
    early = jnp.concatenate([
        jnp.concatenate([d_lg.reshape(4, 128), d_lb.reshape(4, 128)], axis=0),
        d_bs, d_rel, dsink, d_ws.reshape(N_HEADS * BLK, BLK)], axis=0)
    sm0 = _slabs_start("gather_all", [early], sq, "small_gather_start")
    dh1, d_sh2, d_sc2, d_pre1 = _mix_dn(dq, dkv, dzg, dzgate, w_in_full, w_q, h1, dh2, sc2 + ex2[-1] + sm0[-1],
                                        pre[1])
    gw_in = jnp.concatenate(
        [_unpair_heads(_tn_matmul(dq, n2, "w_in_q_wgrad")), _tn_matmul(dkv, n2, "w_in_kv_wgrad"),
         _tn_matmul(dzg, n2, "w_in_zg_wgrad"), _tn_matmul(dzgate, n2, "w_in_gate_wgrad")],
        axis=0).reshape(N_DEV, IN_W // N_DEV, D)
    ex3 = exchange_start(3, [gw_in])

    dy1, dgu1, d_g1, d_post0 = _ffn_out_bwd(dh1, y1, fg1, fu1, wf1_out, g1 + ex3[-1], post[0], "ffn1_out_bwd")
    gw_f1_out = _tn_matmul(fa1, dy1, "ffn1_out_wgrad", tm=D_FF // 2).reshape(N_DEV, D_FF // N_DEV, D)
    ex4 = exchange_start(4, [gw_f1_out])
    gw_f1_in = _tn_matmul(dgu1, n1, "ffn1_in_wgrad", tm=D_FF // 2).reshape(N_DEV, FS, D)
    ex5 = exchange_start(5, [gw_f1_in])
    grad_x, d_sh1, d_sc1, d_pre0 = _ffn_dn(dgu1, wf1_in, x0, dh1, sc1 + ex4[-1] + ex5[-1], pre[0], "ffn1_dn")

    landed = {}
    for i, (ex, nms) in enumerate([(ex1, ["w_ffn2_out", "w_ffn2_in"]),
                                   (ex2, ["w_br_attn", "w_br_gmlp", "w_out"]), (ex3, ["w_in"]),
                                   (ex4, ["w_ffn1_out"]), (ex5, ["w_ffn1_in"])]):
        for nm, land in zip(nms, _slabs_wait("exchange", len(nms), ex, grad_x, "exchange_wait_%d" % i)):
            landed[nm] = land
    moments = [(m_w_ffn1_in, v_w_ffn1_in), (m_w_ffn1_out, v_w_ffn1_out), (m_w_in, v_w_in),
               (m_w_br_attn, v_w_br_attn), (m_w_br_gmlp, v_w_br_gmlp), (m_w_out, v_w_out),
               (m_w_ffn2_in, v_w_ffn2_in), (m_w_ffn2_out, v_w_ffn2_out)]
    names = ["w_ffn1_in", "w_ffn1_out", "w_in", "w_br_attn", "w_br_gmlp", "w_out", "w_ffn2_in", "w_ffn2_out"]
    big = {}
    for nm, w_, (m_, v_) in zip(names, shards, moments):
        if nm in transposed:
            res4 = _adamw_sharded(landed[nm], w_, m_[0].T, v_[0].T, "adamw_" + nm)
            big[nm] = [a.T[None] for a in res4]
        else:
            big[nm] = [a[None] for a in _adamw_sharded(landed[nm], w_, m_[0], v_[0], "adamw_" + nm)]

    my_loss = jnp.broadcast_to(sq * (0.5 / D), (1, D))
    my_loss, _ = lax.optimization_barrier((my_loss, landed["w_ffn1_in"]))
    tot, every = _small_allreduce([d_sh1, d_sc1, d_g1, d_sh2, d_sc2, d_g2, d_sh3, d_sc3, d_g3,
                                   d_pre0, d_pre1, d_pre2, d_post0, d_post1, d_post2, my_loss])
    (early_land,) = _slabs_wait("gather_all", 1, sm0, grad_x, "small_gather_wait")
    tot_early = _sum_slabs(early_land)

    loss = tot[15, 0]
    g_b_ada = tot[0:9].reshape(1, 9 * D)
    g_pre = lax.dynamic_slice_in_dim(tot[9:12], 128 * me, 128, axis=1)[None]
    g_post = lax.dynamic_slice_in_dim(tot[12:15], 128 * me, 128, axis=1)[None]
    g_lg = tot_early[0:4].reshape(1, G_W)
    g_lb = tot_early[4:8].reshape(1, G_W)
    g_bs = tot_early[8:16][None]
    g_rel = jnp.transpose(tot_early[16:24, 0:N_BUCKETS])
    g_sinks = tot_early[24:32, 0][None]
    g_ws = tot_early[32:1056].reshape(1, N_HEADS, BLK, BLK)

    d_ada_mine = lax.dynamic_slice_in_dim(every[:, 0:9].reshape(N_DEV, 9 * D), ADA_W * me, ADA_W, axis=1)
    ada_out = [a[None] for a in _w_ada_update(gath8[:, 0:D], d_ada_mine, w_ada[0], m_w_ada[0], v_w_ada[0])]

    small = [("rel_bias", rel_bias, g_rel, m_rel_bias, v_rel_bias), ("b_ada", b_ada, g_b_ada, m_b_ada, v_b_ada),
             ("pre_norm_g", pre_norm_g, g_pre, m_pre_norm_g, v_pre_norm_g),
             ("post_norm_g", post_norm_g, g_post, m_post_norm_g, v_post_norm_g),
             ("sinks", sinks, g_sinks, m_sinks, v_sinks), ("gmlp_ln_g", gmlp_ln_g, g_lg, m_gmlp_ln_g, v_gmlp_ln_g),
             ("gmlp_ln_b", gmlp_ln_b, g_lb, m_gmlp_ln_b, v_gmlp_ln_b),
             ("gmlp_w_s", gmlp_w_s, g_ws, m_gmlp_w_s, v_gmlp_w_s), ("gmlp_b_s", gmlp_b_s, g_bs, m_gmlp_b_s, v_gmlp_b_s)]
    two_d = lambda a: a.reshape(int(math.prod(a.shape[:-1])), a.shape[-1])
    stepped = _adamw_small([tuple(two_d(a) for a in item[1:]) for item in small])
    res = {"w_ada": ada_out}
    for (nm, w_, g_, _, _), new in zip(small, stepped):
        res[nm] = [g_] + [a.reshape(w_.shape) for a in new]
    res.update(big)
    order = ["rel_bias", "w_ada", "b_ada", "pre_norm_g", "post_norm_g", "w_ffn1_in", "w_ffn1_out", "w_in", "sinks",
             "gmlp_ln_g", "gmlp_ln_b", "gmlp_w_s", "gmlp_b_s", "w_br_attn", "w_br_gmlp", "w_out", "w_ffn2_in",
             "w_ffn2_out"]
    outs = [loss, grad_x[None]]
    for k in range(4):
        outs += [res[nm][k] for nm in order]
    return tuple(outs)
```

```python
import math

import jax
import jax.numpy as jnp
import numpy as np
from jax import lax
from jax.experimental import pallas as pl
from jax.experimental.pallas import tpu as pltpu

F32 = jnp.float32
BF = jnp.bfloat16

N_DEV = 8
D = 1024
D_FF = 2816
FS = D_FF // 4
N_HEADS = 8
N_KV = 2
GROUP = 4
HD = 64
BLK = 128
Q_W = 512
KV_W = 128
G_W = 512
QKV_W = Q_W + 2 * KV_W
ZG_OFF = QKV_W
GATE_OFF = ZG_OFF + 2 * G_W
IN_W = GATE_OFF + 2 * D
N_BUCKETS = 32
MAX_DISTANCE = 128
EPS = 1e-6
NEG = -1e30
SCALE = HD ** -0.5
ADA_W = 9 * D // N_DEV

ADAM_LR = 0.001
ADAM_B1 = 0.9
ADAM_B2 = 0.999
ADAM_EPS = 1e-08
ADAM_WD = 0.01
ADAM_STEP = 10

CHUNK = 256
MIB = 1024 * 1024
MESH = pl.DeviceIdType.MESH
HIGH = lax.Precision.HIGHEST


def _cp(n_grid, vmem_mib):
    return pltpu.CompilerParams(dimension_semantics=("arbitrary",) * n_grid,
                                vmem_limit_bytes=vmem_mib * MIB)


def _const(shape):
    return pl.BlockSpec(shape, lambda *_: (0,) * len(shape))


def _resident(shape):
    return pl.BlockSpec(shape, lambda *_: (0,) * len(shape), pipeline_mode=pl.Buffered(1))


def _behind(body, n_in, after):
    k = len(after)
    return (lambda *refs: body(*refs[:n_in], *refs[n_in + k:])), [pl.BlockSpec(memory_space=pl.ANY)] * k


def _in_hbm(*arrays):
    return [pltpu.with_memory_space_constraint(a, pltpu.HBM) for a in arrays]


def _sds(shape, dtype):
    return jax.ShapeDtypeStruct(shape, dtype)


def _dot(a, b):
    return jnp.dot(a, b, preferred_element_type=F32)


def _dot_nt(a, b):
    return lax.dot_general(a, b, (((1,), (1,)), ((), ())), preferred_element_type=F32)


def _dot_tn(a, b):
    return lax.dot_general(a, b, (((0,), (0,)), ((), ())), preferred_element_type=F32)


def _rms_r(x):
    return lax.rsqrt(jnp.mean(x * x, axis=-1, keepdims=True) + EPS)


def _colsum(x):
    return jnp.sum(x, axis=0, keepdims=True)


def _prenorm(x, gp, sc, sh):
    return (x * _rms_r(x) * gp) * (1.0 + sc) + sh


def _prenorm_bwd(dn, x, gp, sc):
    r = _rms_r(x)
    xh = x * r
    t = dn * (1.0 + sc) * gp
    dx = r * (t - xh * jnp.mean(t * xh, axis=-1, keepdims=True))
    return dx, _colsum(dn), _colsum(dn * xh * gp), _colsum(dn * (1.0 + sc) * xh)


def _postnorm_bwd(dh, y, gate, gp, res):
    y = y.astype(F32)
    r = _rms_r(y)
    yh = y * r
    dyn = (res * gate) * dh
    t = dyn * gp
    dy = r * (t - yh * jnp.mean(t * yh, axis=-1, keepdims=True))
    return dy, _colsum(res * dh * yh * gp), _colsum(dyn * yh)


def _gelu(x):
    k = math.sqrt(2.0 / math.pi)
    return 0.5 * x * (1.0 + jnp.tanh(k * (x + 0.044715 * x * x * x)))


def _gelu_grad(x):
    k = math.sqrt(2.0 / math.pi)
    t = jnp.tanh(k * (x + 0.044715 * x * x * x))
    return 0.5 * (1.0 + t) + 0.5 * x * (1.0 - t * t) * (k * (1.0 + 3.0 * 0.044715 * x * x))


def _my_place():
    x, y, c = lax.axis_index("x"), lax.axis_index("y"), lax.axis_index("c")
    return x, y, c, 4 * x + 2 * y + c


def _peer(x, y, c, k):
    px = 1 - x if k & 4 else x
    py = 1 - y if k & 2 else y
    pc = 1 - c if k & 1 else c
    return (px, py, pc), 4 * px + 2 * py + pc


HBM_SPEC = pl.BlockSpec(memory_space=pltpu.HBM)
SEM_SPEC = pl.BlockSpec(memory_space=pltpu.SEMAPHORE)
EFFECT = pltpu.SideEffectType.DATAFLOW_SIDE_EFFECTING


RELATIONS = {"exchange": (1, 2, 3, 4, 5, 6, 7), "gather": (1, 2, 4, 6), "forward": (2, 4, 6),
             "gather_all": (1, 2, 3, 4, 5, 6, 7)}


def _slab_copies(mode, srcs, lands, send, recv, loc):
    x, y, c, me = _my_place()
    rel = RELATIONS[mode]
    remote, local = [], []
    for t in range(len(lands)):
        for i, k in enumerate(rel):
            peer, peer_lin = _peer(x, y, c, k)
            if mode == "exchange":
                src, dst, to = srcs[t].at[peer_lin], lands[t].at[me], peer
            elif mode in ("gather", "gather_all"):
                src, dst, to = srcs[t], lands[t].at[me], peer
            else:
                src, dst, to = lands[t].at[peer_lin], lands[t].at[peer_lin], _peer(x, y, c, 1)[0]
            remote.append(pltpu.make_async_remote_copy(
                src_ref=src, dst_ref=dst, send_sem=send.at[t * len(rel) + i], recv_sem=recv.at[t * len(rel) + i],
                device_id=to, device_id_type=MESH))
        if mode == "exchange":
            local.append(pltpu.make_async_copy(srcs[t].at[me], lands[t].at[me], loc.at[t]))
        elif mode in ("gather", "gather_all"):
            local.append(pltpu.make_async_copy(srcs[t], lands[t].at[me], loc.at[t]))
    return remote, local


def _slabs_start(mode, arrays, after, name):
    n = len(arrays)
    if mode == "forward":
        thru = list(arrays)
    else:
        shapes = [a.shape if mode == "exchange" else (N_DEV,) + a.shape for a in arrays]
        thru = list(arrays) + [lax.empty(s, a.dtype) for s, a in zip(shapes, arrays)]
    m = len(thru)
    n_sem = n * len(RELATIONS[mode])

    def body(*refs):
        srcs, lands = refs[:n], refs[m - n:m]
        send, recv, loc = refs[m + 1:m + 4]
        remote, local = _slab_copies(mode, srcs, lands, send, recv, loc)
        for cp in remote + local:
            cp.start()
        refs[-1][...] = jnp.zeros_like(refs[-1])

    return pl.pallas_call(
        body, name=name,
        out_shape=(pltpu.SemaphoreType.DMA((n_sem,)), pltpu.SemaphoreType.DMA((n_sem,)),
                   pltpu.SemaphoreType.DMA((n,)),
                   *[pltpu.HBM(a.shape, a.dtype) for a in thru],
                   _sds((1, D), F32)),
        in_specs=[HBM_SPEC] * m + [pl.BlockSpec(memory_space=pl.ANY)],
        out_specs=(SEM_SPEC, SEM_SPEC, SEM_SPEC, *[HBM_SPEC] * m, pl.BlockSpec(memory_space=pltpu.VMEM)),
        input_output_aliases={t: 3 + t for t in range(m)},
        compiler_params=pltpu.CompilerParams(has_side_effects=EFFECT),
    )(*[pltpu.with_memory_space_constraint(a, pltpu.HBM) for a in thru], after)


def _slabs_wait(mode, n, started, after, name):
    sems = started[0:3]
    thru = started[3:-1]
    m = len(thru)

    def body(*refs):
        srcs, lands = refs[:n], refs[m - n:m]
        remote, local = _slab_copies(mode, srcs, lands, *refs[m:m + 3])
        for cp in remote:
            cp.wait_send()
            cp.wait_recv()
        for cp in local:
            cp.wait()

    res = pl.pallas_call(
        body, name=name,
        out_shape=tuple(pltpu.HBM(a.shape, a.dtype) for a in thru),
        in_specs=[HBM_SPEC] * m + [SEM_SPEC] * 3 + [pl.BlockSpec(memory_space=pl.ANY)],
        out_specs=tuple([HBM_SPEC] * m),
        input_output_aliases={t: t for t in range(m)},
        compiler_params=pltpu.CompilerParams(has_side_effects=EFFECT),
    )(*thru, *sems, after)
    return list(res[m - n:m])


def _ada_forward(small8, w_ada, b_ada64):
    sw = small8.shape[1]

    def body(sm_ref, w_ref, b_ref, gath_ref, ada_ref, part_ref, send1, recv1, send2, recv2):
        x, y, c, me = _my_place()
        row_me = pl.multiple_of(me * 8, 8)
        gath_ref[pl.ds(row_me, 8), :] = sm_ref[...]
        first = []
        for k in range(1, N_DEV):
            peer, _ = _peer(x, y, c, k)
            cp = pltpu.make_async_remote_copy(
                src_ref=sm_ref, dst_ref=gath_ref.at[pl.ds(row_me, 8), :], send_sem=send1.at[k - 1],
                recv_sem=recv1.at[k - 1], device_id=peer, device_id_type=MESH)
            cp.start()
            first.append(cp)
        for cp in first:
            cp.wait()
        cs = gath_ref[:, 0:D]
        cs = cs * jax.nn.sigmoid(cs)
        part_ref[...] = jnp.dot(cs, w_ref[...], preferred_element_type=F32, precision=HIGH)
        ada_ref[pl.ds(row_me, 8), :] = part_ref[pl.ds(row_me, 8), :]
        second = []
        for k in range(1, N_DEV):
            peer, peer_lin = _peer(x, y, c, k)
            cp = pltpu.make_async_remote_copy(
                src_ref=part_ref.at[pl.ds(pl.multiple_of(peer_lin * 8, 8), 8), :],
                dst_ref=ada_ref.at[pl.ds(row_me, 8), :], send_sem=send2.at[k - 1],
                recv_sem=recv2.at[k - 1], device_id=peer, device_id_type=MESH)
            cp.start()
            second.append(cp)
        for cp in second:
            cp.wait()
        ada_ref[...] = ada_ref[...] + b_ref[...]

    vm = pl.BlockSpec(memory_space=pltpu.VMEM)
    return pl.pallas_call(
        body, name="ada_forward",
        out_shape=[_sds((8 * N_DEV, sw), F32), _sds((8 * N_DEV, ADA_W), F32)],
        in_specs=[vm, vm, vm], out_specs=[vm, vm],
        scratch_shapes=[pltpu.VMEM((8 * N_DEV, ADA_W), F32)] + [pltpu.SemaphoreType.DMA((7,))] * 4,
        compiler_params=pltpu.CompilerParams(vmem_limit_bytes=32 * MIB),
    )(small8, w_ada, b_ada64)


def _sum_slabs(land):
    def body(l_ref, o_ref):
        acc = l_ref[0]
        for j in range(1, N_DEV):
            acc = acc + l_ref[j]
        o_ref[...] = acc

    vm = pl.BlockSpec(memory_space=pltpu.VMEM)
    return pl.pallas_call(body, name="sum_slabs", out_shape=_sds(land.shape[1:], F32), in_specs=[vm], out_specs=vm,
                          compiler_params=pltpu.CompilerParams(vmem_limit_bytes=32 * MIB))(land)


def _small_allreduce(vectors):
    n = len(vectors)

    def body(*refs):
        v_refs, (sum_ref, gath_ref, pack, send, recv) = refs[:n], refs[n:]
        x, y, c, me = _my_place()
        for k in range(n):
            pack[k:k + 1, :] = v_refs[k][...]
        gath_ref[me] = pack[...]
        cps = []
        for k in range(1, N_DEV):
            peer, _ = _peer(x, y, c, k)
            cp = pltpu.make_async_remote_copy(
                src_ref=pack, dst_ref=gath_ref.at[me], send_sem=send.at[k - 1],
                recv_sem=recv.at[k - 1], device_id=peer, device_id_type=MESH)
            cp.start()
            cps.append(cp)
        for cp in cps:
            cp.wait()
        acc = gath_ref[0]
        for j in range(1, N_DEV):
            acc = acc + gath_ref[j]
        sum_ref[...] = acc

    vm = pl.BlockSpec(memory_space=pltpu.VMEM)
    return pl.pallas_call(
        body, name="small_allreduce",
        out_shape=[_sds((n, D), F32), _sds((N_DEV, n, D), F32)],
        in_specs=[vm] * n, out_specs=[vm, vm],
        scratch_shapes=[pltpu.VMEM((n, D), F32), pltpu.SemaphoreType.DMA((7,)), pltpu.SemaphoreType.DMA((7,))],
    )(*vectors)


F_TILES = tuple((f0, min(512, D_FF - f0)) for f0 in range(0, D_FF, 512))
F_TILES_NARROW = tuple((f0, 256) for f0 in range(0, D_FF, 256))


def _swiglu_tile(n, wt_ref, f0, tf):
    g = _dot_nt(n, wt_ref[f0:f0 + tf, :])
    u = _dot_nt(n, wt_ref[D_FF + f0:D_FF + f0 + tf, :])
    sg = jax.nn.sigmoid(g)
    silu = g * sg
    return (u * (sg * (1.0 + g * (1.0 - sg)))).astype(BF), silu.astype(BF), (silu * u).astype(BF)


def _ffn_in(h, sh, sc, gp, wt, name):
    S = h.shape[0]
    R = min(512, S)

    def body(h_ref, sh_ref, sc_ref, gp_ref, w_ref, n_ref, dg_ref, sl_ref, a_ref):
        for r0 in range(0, R, CHUNK):
            rows = slice(r0, r0 + CHUNK)
            n = _prenorm(h_ref[rows, :], gp_ref[...], sc_ref[...], sh_ref[...]).astype(BF)
            n_ref[rows, :] = n
            for f0, tf in F_TILES_NARROW:
                dg_ref[rows, f0:f0 + tf], sl_ref[rows, f0:f0 + tf], a_ref[rows, f0:f0 + tf] = _swiglu_tile(
                    n, w_ref, f0, tf)

    vec = _const((1, D))
    rows_ = lambda w_: pl.BlockSpec((R, w_), lambda i: (i, 0))
    return pl.pallas_call(
        body, name=name, grid=(S // R,),
        out_shape=[_sds((S, D), BF)] + [_sds((S, D_FF), BF)] * 3,
        in_specs=[rows_(D), vec, vec, vec, _resident((2 * D_FF, D))],
        out_specs=[rows_(D), rows_(D_FF), rows_(D_FF), rows_(D_FF)],
        compiler_params=_cp(1, 56),
    )(*_in_hbm(h), sh, sc, gp, *_in_hbm(wt))


def _ffn_out(a, w, h, gate, gp, name, target=None):
    S = h.shape[0]
    R = min(512, S)
    with_loss = target is not None

    def body(a_ref, w_ref, h_ref, gate_ref, gp_ref, *rest):
        if with_loss:
            t_ref, out_ref, y_ref, tot_ref = rest

            @pl.when(pl.program_id(0) == 0)
            def _():
                tot_ref[...] = jnp.zeros_like(tot_ref)
        else:
            out_ref, y_ref = rest
        for r0 in range(0, R, CHUNK):
            rows = slice(r0, r0 + CHUNK)
            y = _dot(a_ref[rows, :], w_ref[...])
            y_ref[rows, :] = y.astype(BF)
            hn = h_ref[rows, :] + (0.5 * gate_ref[...]) * (y * _rms_r(y) * gp_ref[...])
            if with_loss:
                e = hn - t_ref[rows, :]
                out_ref[rows, :] = e * (1.0 / D)
                tot_ref[...] += jnp.sum(jnp.sum(e * e, axis=1, keepdims=True), axis=0, keepdims=True)
            else:
                out_ref[rows, :] = hn

    vec = _const((1, D))
    rows_ = lambda w_: pl.BlockSpec((R, w_), lambda i: (i, 0))
    return pl.pallas_call(
        body, name=name, grid=(S // R,),
        out_shape=[_sds((S, D), F32), _sds((S, D), BF)] + ([_sds((1, 1), F32)] if with_loss else []),
        in_specs=[rows_(D_FF), _resident((D_FF, D)), rows_(D), vec, vec] + ([rows_(D)] if with_loss else []),
        out_specs=[rows_(D), rows_(D)] + ([_const((1, 1))] if with_loss else []),
        compiler_params=_cp(1, 48),
    )(*_in_hbm(a, w, h), gate, gp, *(_in_hbm(target) if with_loss else ()))


def _ffn_out_bwd(dh, y, dsilu_u, silu, w, gate, gp, name, after=()):
    S = dh.shape[0]
    R = min(512, S)

    def body(dh_ref, y_ref, g_ref, u_ref, w_ref, gate_ref, gp_ref, dy_ref, dgu_ref, dgate_ref, dgp_ref):
        @pl.when(pl.program_id(0) == 0)
        def _():
            dgate_ref[...] = jnp.zeros_like(dgate_ref)
            dgp_ref[...] = jnp.zeros_like(dgp_ref)
        for r0 in range(0, R, CHUNK):
            rows = slice(r0, r0 + CHUNK)
            dy, dgate, dgp = _postnorm_bwd(dh_ref[rows, :], y_ref[rows, :], gate_ref[...], gp_ref[...], 0.5)
            dgate_ref[...] += dgate
            dgp_ref[...] += dgp
            dyb = dy.astype(BF)
            dy_ref[rows, :] = dyb
            for f0, tf in F_TILES:
                da = _dot_nt(dyb, w_ref[f0:f0 + tf, :])
                dgu_ref[rows, f0:f0 + tf] = (da * g_ref[rows, f0:f0 + tf].astype(F32)).astype(BF)
                dgu_ref[rows, D_FF + f0:D_FF + f0 + tf] = (da * u_ref[rows, f0:f0 + tf].astype(F32)).astype(BF)

    vec = _const((1, D))
    rows_ = lambda w_: pl.BlockSpec((R, w_), lambda i: (i, 0))
    body, after_specs = _behind(body, 7, after)
    return pl.pallas_call(
        body, name=name, grid=(S // R,),
        out_shape=[_sds((S, D), BF), _sds((S, 2 * D_FF), BF), _sds((1, D), F32), _sds((1, D), F32)],
        in_specs=[rows_(D), rows_(D), rows_(D_FF), rows_(D_FF), _resident((D_FF, D)), vec, vec] + after_specs,
        out_specs=[rows_(D), rows_(2 * D_FF), vec, vec],
        compiler_params=_cp(1, 56),
    )(*_in_hbm(dh, y, dsilu_u, silu, w), gate, gp, *after)


def _ffn_dn(dgu, wt, h, dh, sc, gp, name, after=()):
    S = h.shape[0]
    R = min(512, S)

    def body(dgu_ref, w_ref, h_ref, dh_ref, sc_ref, gp_ref, out_ref, dsh_ref, dsc_ref, dgp_ref):
        @pl.when(pl.program_id(0) == 0)
        def _():
            dsh_ref[...] = jnp.zeros_like(dsh_ref)
            dsc_ref[...] = jnp.zeros_like(dsc_ref)
            dgp_ref[...] = jnp.zeros_like(dgp_ref)

        for r0 in range(0, R, CHUNK):
            rows = slice(r0, r0 + CHUNK)
            dn = _dot(dgu_ref[rows, :], w_ref[...])
            dx, dsh, dsc, dgp = _prenorm_bwd(dn, h_ref[rows, :], gp_ref[...], sc_ref[...])
            out_ref[rows, :] = dh_ref[rows, :] + dx
            dsh_ref[...] += dsh
            dsc_ref[...] += dsc
            dgp_ref[...] += dgp

    vec = _const((1, D))
    rows_ = lambda w_: pl.BlockSpec((R, w_), lambda i: (i, 0))
    body, after_specs = _behind(body, 6, after)
    return pl.pallas_call(
        body, name=name, grid=(S // R,),
        out_shape=[_sds((S, D), F32)] + [_sds((1, D), F32)] * 3,
        in_specs=[rows_(2 * D_FF), _resident((2 * D_FF, D)), rows_(D), rows_(D), vec, vec] + after_specs,
        out_specs=[rows_(D), vec, vec, vec],
        compiler_params=_cp(1, 56),
    )(*_in_hbm(dgu, wt, h, dh), sc, gp, *after)


def _ffn_bwd(dh, y, dsilu_u, silu, w, wt, h, gate, gpost, sc, gpre, name):
    S = dh.shape[0]
    R = min(256, S)

    def body(dh_ref, y_ref, g_ref, u_ref, w_ref, wt_ref, h_ref, gate_ref, gpost_ref, sc_ref, gpre_ref,
             dy_ref, dgu_ref, out_ref, dgate_ref, dgpost_ref, dsh_ref, dsc_ref, dgpre_ref):
        @pl.when(pl.program_id(0) == 0)
        def _():
            for r in (dgate_ref, dgpost_ref, dsh_ref, dsc_ref, dgpre_ref):
                r[...] = jnp.zeros_like(r)
        dhh = dh_ref[...]
        dy, dgate, dgpost = _postnorm_bwd(dhh, y_ref[...], gate_ref[...], gpost_ref[...], 0.5)
        dgate_ref[...] += dgate
        dgpost_ref[...] += dgpost
        dyb = dy.astype(BF)
        dy_ref[...] = dyb
        dn = None
        for f0, tf in F_TILES:
            da = _dot_nt(dyb, w_ref[f0:f0 + tf, :])
            dg = (da * g_ref[:, f0:f0 + tf].astype(F32)).astype(BF)
            du = (da * u_ref[:, f0:f0 + tf].astype(F32)).astype(BF)
            dgu_ref[:, f0:f0 + tf] = dg
            dgu_ref[:, D_FF + f0:D_FF + f0 + tf] = du
            part = _dot(dg, wt_ref[f0:f0 + tf, :]) + _dot(du, wt_ref[D_FF + f0:D_FF + f0 + tf, :])
            dn = part if dn is None else dn + part
        dx, dsh, dsc, dgpre = _prenorm_bwd(dn, h_ref[...], gpre_ref[...], sc_ref[...])
        out_ref[...] = dhh + dx
        dsh_ref[...] += dsh
        dsc_ref[...] += dsc
        dgpre_ref[...] += dgpre

    vec = _const((1, D))
    rows_ = lambda w_: pl.BlockSpec((R, w_), lambda i: (i, 0))
    return pl.pallas_call(
        body, name=name, grid=(S // R,),
        out_shape=[_sds((S, D), BF), _sds((S, 2 * D_FF), BF), _sds((S, D), F32)] + [_sds((1, D), F32)] * 5,
        in_specs=[rows_(D), rows_(D), rows_(D_FF), rows_(D_FF), _resident((D_FF, D)), _resident((2 * D_FF, D)),
                  rows_(D), vec, vec, vec, vec],
        out_specs=[rows_(D), rows_(2 * D_FF), rows_(D)] + [vec] * 5,
        compiler_params=_cp(1, 56),
    )(*_in_hbm(dh, y, dsilu_u, silu, w, wt, h), gate, gpost, sc, gpre)


def _tn_matmul(a, b, name, tm=None):
    S, M_all = a.shape
    N = b.shape[1]
    M = M_all if tm is None else tm
    GA = M_all // M
    ts = min(2048 if M * N <= 2 * D * D else 1024, S)
    nk = S // ts
    chunks = [(m0, min(CHUNK, M - m0)) for m0 in range(0, M, CHUNK)]

    def body(a_ref, b_ref, o_ref, acc):
        k = pl.program_id(1)

        @pl.when(k == 0)
        def _():
            acc[...] = jnp.zeros_like(acc)

        for m0, mc in chunks:
            acc[m0:m0 + mc, :] += _dot_tn(a_ref[:, m0:m0 + mc], b_ref[...])

        @pl.when(k == nk - 1)
        def _():
            for m0, mc in chunks:
                o_ref[m0:m0 + mc, :] = acc[m0:m0 + mc, :].astype(BF)

    return pl.pallas_call(
        body, name=name, grid=(GA, nk),
        out_shape=_sds((M_all, N), BF),
        in_specs=[pl.BlockSpec((ts, M), lambda ga, k: (k, ga)), pl.BlockSpec((ts, N), lambda ga, k: (k, 0))],
        out_specs=pl.BlockSpec((M, N), lambda ga, k: (ga, 0)),
        scratch_shapes=[pltpu.VMEM((M, N), F32)],
        compiler_params=_cp(2, 56),
    )(*_in_hbm(a, b))


def _mix_in(h, sh, sc, gp, w, wq):
    S = h.shape[0]
    R = min(512, S)

    def body(h_ref, sh_ref, sc_ref, gp_ref, w_ref, wq_ref, n_ref, qkv_ref, zg_ref, gates_ref):
        for r0 in range(0, R, CHUNK):
            rows = slice(r0, r0 + CHUNK)
            nb = _prenorm(h_ref[rows, :], gp_ref[...], sc_ref[...], sh_ref[...]).astype(BF)
            n_ref[rows, :] = nb
            qkv_ref[rows, 0:Q_W] = _dot_nt(nb, wq_ref[...]).astype(BF)
            qkv_ref[rows, Q_W:QKV_W] = _dot_nt(nb, w_ref[Q_W:QKV_W, :]).astype(BF)
            zg_ref[rows, :] = _dot_nt(nb, w_ref[ZG_OFF:GATE_OFF, :]).astype(BF)
            gates_ref[rows, :] = jax.nn.sigmoid(_dot_nt(nb, w_ref[GATE_OFF:IN_W, :])).astype(BF)

    vec = _const((1, D))
    rows = lambda w_: pl.BlockSpec((R, w_), lambda i: (i, 0))
    return pl.pallas_call(
        body, name="mix_in", grid=(S // R,),
        out_shape=[_sds((S, D), BF), _sds((S, QKV_W), BF), _sds((S, 2 * G_W), BF), _sds((S, 2 * D), BF)],
        in_specs=[rows(D), vec, vec, vec, _resident((IN_W, D)), _resident((Q_W, D))],
        out_specs=[rows(D), rows(QKV_W), rows(2 * G_W), rows(2 * D)],
        compiler_params=_cp(1, 48),
    )(*_in_hbm(h), sh, sc, gp, *_in_hbm(w, wq))


def _bias_table(rel_bias, bucket):
    def body(rel_ref, bk_ref, out_ref):
        bk = bk_ref[...]
        qi = lax.broadcasted_iota(jnp.int32, (BLK, 2 * BLK), 0)
        kj = lax.broadcasted_iota(jnp.int32, (BLK, 2 * BLK), 1)
        dist = qi + BLK - kj
        window = (dist >= 0) & (dist < BLK)
        for h in range(N_HEADS):
            acc = jnp.zeros((BLK, 2 * BLK), F32)
            for b in range(N_BUCKETS):
                acc = jnp.where(bk == b, rel_ref[b, h], acc)
            out_ref[h // GROUP, pl.ds((h % GROUP) * BLK, BLK), :] = jnp.where(window, acc, NEG)

    return pl.pallas_call(
        body, name="bias_table",
        out_shape=_sds((N_KV, GROUP * BLK, 2 * BLK), F32),
        in_specs=[pl.BlockSpec(memory_space=pltpu.SMEM), pl.BlockSpec(memory_space=pltpu.VMEM)],
        out_specs=pl.BlockSpec(memory_space=pltpu.VMEM),
    )(rel_bias, bucket)


ATT_TB = 4


HEAD_ROWS = N_HEADS * BLK


def _pair_heads(w):
    return jnp.transpose(w.reshape(N_KV, GROUP, HD, w.shape[1]), (1, 0, 2, 3)).reshape(w.shape)


def _unpair_heads(w):
    return jnp.transpose(w.reshape(GROUP, N_KV, HD, w.shape[1]), (1, 0, 2, 3)).reshape(w.shape)


def _halves(x, scale=1.0):
    low = lax.broadcasted_iota(jnp.int32, x.shape, 1) < HD
    xf = x.astype(F32) * scale
    return jnp.where(low, xf, 0.0).astype(BF), jnp.where(low, 0.0, xf).astype(BF)


def _stack_heads(x, scale=1.0):
    halves = [_halves(x[:, g * 128:(g + 1) * 128], scale) for g in range(GROUP)]
    return jnp.concatenate([lo for lo, _ in halves] + [hi for _, hi in halves], axis=0)


def _attn_probs(q, kvc, kvp, bias_ref, sink_ref, has_prev):
    kv2 = jnp.concatenate([kvp, kvc], axis=0)
    kboth, vboth = kv2[:, 0:KV_W], kv2[:, KV_W:2 * KV_W]
    qpad = _stack_heads(q, SCALE)
    s = _dot_nt(qpad, kboth) + bias_ref[...]
    if has_prev is not None:
        col = lax.broadcasted_iota(jnp.int32, (HEAD_ROWS, 2 * BLK), 1)
        s = jnp.where((col >= BLK) | has_prev, s, NEG)
    row_head = lax.broadcasted_iota(jnp.int32, (HEAD_ROWS, 1), 0) // BLK
    sink = jnp.zeros((HEAD_ROWS, 1), F32)
    for h in range(N_HEADS):
        sink = jnp.where(row_head == h, sink_ref[h], sink)
    m = jnp.maximum(jnp.max(s, axis=1, keepdims=True), sink)
    p = jnp.exp(s - m)
    e_sink = jnp.exp(sink - m)
    inv = 1.0 / (jnp.sum(p, axis=1, keepdims=True) + e_sink)
    return qpad, kboth, vboth, p * inv, e_sink * inv


def _attn_fwd(qkv, bias, sinks):
    S = qkv.shape[0]
    tb = min(ATT_TB, S // BLK)
    T = tb * BLK

    def body(sink_ref, q_ref, kv_ref, kvp_ref, bias_ref, o_ref):
        step = pl.program_id(0)
        for j in range(tb):
            rows = slice(j * BLK, (j + 1) * BLK)
            kvp = kvp_ref[...] if j == 0 else kv_ref[(j - 1) * BLK:j * BLK, :]
            has_prev = (step > 0) if j == 0 else None
            _, _, vboth, prob, _ = _attn_probs(q_ref[rows, :], kv_ref[rows, :], kvp, bias_ref, sink_ref, has_prev)
            pb = prob.astype(BF)
            v_low, v_high = _halves(vboth)
            half = HEAD_ROWS // 2
            o = _dot(pb[0:half], v_low) + _dot(pb[half:HEAD_ROWS], v_high)
            for g in range(GROUP):
                o_ref[rows, g * 128:(g + 1) * 128] = o[g * BLK:(g + 1) * BLK].astype(BF)

    return pl.pallas_call(
        body, name="attn_fwd", grid=(S // T,),
        out_shape=_sds((S, Q_W), BF),
        in_specs=[pl.BlockSpec(memory_space=pltpu.SMEM),
                  pl.BlockSpec((T, Q_W), lambda i: (i, 0)),
                  pl.BlockSpec((T, 2 * KV_W), lambda i: (i, 2)),
                  pl.BlockSpec((BLK, 2 * KV_W), lambda i: (jnp.maximum(i * tb - 1, 0), 2)),
                  _const((HEAD_ROWS, 2 * BLK))],
        out_specs=pl.BlockSpec((T, Q_W), lambda i: (i, 0)),
        compiler_params=_cp(1, 32),
    )(sinks, *_in_hbm(qkv, qkv, qkv, bias))


def _attn_bwd(qkv, bias, sinks, do):
    S = qkv.shape[0]
    tb = min(ATT_TB, S // BLK)
    T = tb * BLK
    nt = S // T
    half = HEAD_ROWS // 2

    def body(sink_ref, q_ref, kv_ref, kvp_ref, bias_ref, do_ref, dq_ref, dkv_ref, dbias_ref, dsink_ref, carry):
        i = pl.program_id(0)

        @pl.when(i == 0)
        def _():
            carry[...] = jnp.zeros_like(carry)
            dbias_ref[...] = jnp.zeros_like(dbias_ref)
            dsink_ref[...] = jnp.zeros_like(dsink_ref)

        from_next = carry[...]
        head_row = lax.broadcasted_iota(jnp.int32, (N_HEADS, 128), 0)
        low = lax.broadcasted_iota(jnp.int32, (BLK, 128), 1) < HD
        for j in reversed(range(tb)):
            rows = slice(j * BLK, (j + 1) * BLK)
            kvp = kvp_ref[...] if j == 0 else kv_ref[(j - 1) * BLK:j * BLK, :]
            has_prev = (i < nt - 1) if j == 0 else None
            qpad, kboth, vboth, prob, p_sink = _attn_probs(q_ref[rows, :], kv_ref[rows, :], kvp, bias_ref, sink_ref,
                                                           has_prev)
            pb = prob.astype(BF)
            dopad = _stack_heads(do_ref[rows, :])
            dp = _dot_nt(dopad, vboth)
            delta = jnp.sum(prob * dp, axis=1, keepdims=True)
            ds = prob * (dp - delta)
            dbias_ref[...] += ds
            sink_term = p_sink * delta
            dsink_rows = jnp.zeros((N_HEADS, 128), F32)
            for h in range(N_HEADS):
                val = -jnp.sum(sink_term[h * BLK:(h + 1) * BLK], axis=0, keepdims=True)
                dsink_rows = jnp.where(head_row == h, val, dsink_rows)
            dsink_ref[...] += dsink_rows
            dsb = ds.astype(BF)
            dqpad = _dot(dsb, kboth) * SCALE
            for g in range(GROUP):
                dq_ref[rows, g * 128:(g + 1) * 128] = jnp.where(
                    low, dqpad[g * BLK:(g + 1) * BLK], dqpad[half + g * BLK:half + (g + 1) * BLK]).astype(BF)
            dkv2 = jnp.concatenate([jnp.transpose(_dot_tn(qpad, dsb)),
                                    jnp.transpose(_dot_tn(dopad, pb))], axis=1)
            dkv_ref[rows, :] = (dkv2[BLK:2 * BLK] + from_next).astype(BF)
            from_next = dkv2[0:BLK]
        carry[...] = from_next

    return pl.pallas_call(
        body, name="attn_bwd", grid=(nt,),
        out_shape=[_sds((S, Q_W), BF), _sds((S, 2 * KV_W), BF),
                   _sds((HEAD_ROWS, 2 * BLK), F32), _sds((N_HEADS, 128), F32)],
        in_specs=[pl.BlockSpec(memory_space=pltpu.SMEM),
                  pl.BlockSpec((T, Q_W), lambda i: (nt - 1 - i, 0)),
                  pl.BlockSpec((T, 2 * KV_W), lambda i: (nt - 1 - i, 2)),
                  pl.BlockSpec((BLK, 2 * KV_W), lambda i: (jnp.maximum((nt - 1 - i) * tb - 1, 0), 2)),
                  _const((HEAD_ROWS, 2 * BLK)),
                  pl.BlockSpec((T, Q_W), lambda i: (nt - 1 - i, 0))],
        out_specs=[pl.BlockSpec((T, Q_W), lambda i: (nt - 1 - i, 0)),
                   pl.BlockSpec((T, 2 * KV_W), lambda i: (nt - 1 - i, 0)),
                   _const((HEAD_ROWS, 2 * BLK)), _const((N_HEADS, 128))],
        scratch_shapes=[pltpu.VMEM((BLK, 2 * KV_W), F32)],
        compiler_params=_cp(1, 32),
    )(sinks, *_in_hbm(qkv, qkv, qkv, bias, do))


def _rel_bias_grad(dbias, bucket):
    def body(db_ref, bk_ref, out_ref):
        bk = bk_ref[...]
        lane = lax.broadcasted_iota(jnp.int32, (1, 128), 1)
        for h in range(N_HEADS):
            d = db_ref[h // GROUP, pl.ds((h % GROUP) * BLK, BLK), :]
            row = jnp.zeros((1, 128), F32)
            for b in range(N_BUCKETS):
                tot = jnp.sum(jnp.sum(jnp.where(bk == b, d, 0.0), axis=1, keepdims=True), axis=0, keepdims=True)
                row = jnp.where(lane == b, tot, row)
            out_ref[pl.ds(h, 1), :] = row

    vm = pl.BlockSpec(memory_space=pltpu.VMEM)
    return pl.pallas_call(body, name="rel_bias_grad", out_shape=_sds((N_HEADS, 128), F32),
                          in_specs=[vm, vm], out_specs=vm)(dbias, bucket)


def _gmlp_parts(zg, lg_ref, lb_ref):
    z = zg.astype(F32)
    ge = _gelu(z)
    u, vg = ge[:, 0:G_W], ge[:, G_W:2 * G_W]
    mu = jnp.mean(vg, axis=-1, keepdims=True)
    xc = vg - mu
    rstd = lax.rsqrt(jnp.mean(xc * xc, axis=-1, keepdims=True) + EPS)
    xh = xc * rstd
    return z, u, xh, rstd, xh * lg_ref[...] + lb_ref[...]


def _causal_weights(ws_ref, wc):
    t = lax.broadcasted_iota(jnp.int32, (BLK, BLK), 0)
    s = lax.broadcasted_iota(jnp.int32, (BLK, BLK), 1)
    for g in range(N_HEADS):
        wc[g] = jnp.where(s <= t, ws_ref[g], 0.0).astype(BF)


def _spatial(vb, wc, bst_ref, p, low):
    xp = vb[:, p * 128:(p + 1) * 128]
    s0 = _dot(wc[2 * p], xp) + bst_ref[:, 2 * p:2 * p + 1]
    s1 = _dot(wc[2 * p + 1], xp) + bst_ref[:, 2 * p + 1:2 * p + 2]
    return xp, jnp.where(low, s0, s1)


def _gmlp_fwd(zg, lg, lb, ws, bst):
    S = zg.shape[0]
    tb = min(ATT_TB, S // BLK)
    T = tb * BLK

    def body(zg_ref, lg_ref, lb_ref, ws_ref, bst_ref, o_ref, wc):
        @pl.when(pl.program_id(0) == 0)
        def _():
            _causal_weights(ws_ref, wc)
        low = lax.broadcasted_iota(jnp.int32, (BLK, 128), 1) < HD
        for j in range(tb):
            rows = slice(j * BLK, (j + 1) * BLK)
            _, u, _, _, vln = _gmlp_parts(zg_ref[rows, :], lg_ref, lb_ref)
            vb = vln.astype(BF)
            for p in range(4):
                _, sp = _spatial(vb, wc, bst_ref, p, low)
                o_ref[rows, p * 128:(p + 1) * 128] = (u[:, p * 128:(p + 1) * 128] * sp).astype(BF)

    return pl.pallas_call(
        body, name="gmlp_fwd", grid=(S // T,),
        out_shape=_sds((S, G_W), BF),
        in_specs=[pl.BlockSpec((T, 2 * G_W), lambda i: (i, 0)), _const((1, G_W)), _const((1, G_W)),
                  _const((N_HEADS, BLK, BLK)), _const((BLK, N_HEADS))],
        out_specs=pl.BlockSpec((T, G_W), lambda i: (i, 0)),
        scratch_shapes=[pltpu.VMEM((N_HEADS, BLK, BLK), BF)],
        compiler_params=_cp(1, 32),
    )(*_in_hbm(zg), lg, lb, ws, bst)


def _gmlp_bwd(zg, d_out, lg, lb, ws, bst):
    S = zg.shape[0]
    tb = min(ATT_TB, S // BLK)
    T = tb * BLK
    nb = S // T

    def body(zg_ref, d_ref, lg_ref, lb_ref, ws_ref, bst_ref, dzg_ref, dws_ref, dbs_ref, dlg_ref, dlb_ref, wc, dbacc):
        i = pl.program_id(0)

        @pl.when(i == 0)
        def _():
            _causal_weights(ws_ref, wc)
            dws_ref[...] = jnp.zeros_like(dws_ref)
            dlg_ref[...] = jnp.zeros_like(dlg_ref)
            dlb_ref[...] = jnp.zeros_like(dlb_ref)
            dbacc[...] = jnp.zeros_like(dbacc)

        low = lax.broadcasted_iota(jnp.int32, (BLK, 128), 1) < HD
        for j in range(tb):
            rows = slice(j * BLK, (j + 1) * BLK)
            z, u, xh, rstd, vln = _gmlp_parts(zg_ref[rows, :], lg_ref, lb_ref)
            vb = vln.astype(BF)
            d = d_ref[rows, :].astype(F32)
            du_parts, dvln_parts = [], []
            for p in range(4):
                xp, sp = _spatial(vb, wc, bst_ref, p, low)
                dp = d[:, p * 128:(p + 1) * 128]
                du_parts.append(dp * sp)
                dsp = dp * u[:, p * 128:(p + 1) * 128]
                dbacc[:, p * 128:(p + 1) * 128] += dsp
                d0 = jnp.where(low, dsp, 0.0).astype(BF)
                d1 = jnp.where(low, 0.0, dsp).astype(BF)
                dws_ref[2 * p] += _dot_nt(d0, xp)
                dws_ref[2 * p + 1] += _dot_nt(d1, xp)
                dvln_parts.append(_dot_tn(wc[2 * p], d0) + _dot_tn(wc[2 * p + 1], d1))
            dvln = jnp.concatenate(dvln_parts, axis=1)
            dlg_ref[...] += _colsum(dvln * xh)
            dlb_ref[...] += _colsum(dvln)
            dxh = dvln * lg_ref[...]
            dvg = rstd * (dxh - jnp.mean(dxh, axis=-1, keepdims=True)
                          - xh * jnp.mean(dxh * xh, axis=-1, keepdims=True))
            dge = jnp.concatenate(du_parts + [dvg], axis=1)
            dzg_ref[rows, :] = (dge * _gelu_grad(z)).astype(BF)

        @pl.when(i == nb - 1)
        def _():
            t = lax.broadcasted_iota(jnp.int32, (BLK, BLK), 0)
            s = lax.broadcasted_iota(jnp.int32, (BLK, BLK), 1)
            for g in range(N_HEADS):
                dws_ref[g] = jnp.where(s <= t, dws_ref[g], 0.0)
            grp = lax.broadcasted_iota(jnp.int32, (N_HEADS, G_W), 0)
            lane = lax.broadcasted_iota(jnp.int32, (N_HEADS, G_W), 1) // HD
            pick = jnp.where(grp == lane, 1.0, 0.0).astype(F32)
            dbs_ref[...] = lax.dot_general(pick, dbacc[...], (((1,), (1,)), ((), ())),
                                           preferred_element_type=F32, precision=HIGH)

    return pl.pallas_call(
        body, name="gmlp_bwd", grid=(nb,),
        out_shape=[_sds((S, 2 * G_W), BF), _sds((N_HEADS, BLK, BLK), F32), _sds((N_HEADS, BLK), F32),
                   _sds((1, G_W), F32), _sds((1, G_W), F32)],
        in_specs=[pl.BlockSpec((T, 2 * G_W), lambda i: (i, 0)), pl.BlockSpec((T, G_W), lambda i: (i, 0)),
                  _const((1, G_W)), _const((1, G_W)), _const((N_HEADS, BLK, BLK)), _const((BLK, N_HEADS))],
        out_specs=[pl.BlockSpec((T, 2 * G_W), lambda i: (i, 0)), _const((N_HEADS, BLK, BLK)),
                   _const((N_HEADS, BLK)), _const((1, G_W)), _const((1, G_W))],
        scratch_shapes=[pltpu.VMEM((N_HEADS, BLK, BLK), BF), pltpu.VMEM((BLK, G_W), F32)],
        compiler_params=_cp(1, 32),
    )(*_in_hbm(zg, d_out), lg, lb, ws, bst)


def _mix_out(o, gm, gates, h, wa, wg, wo, gate, gp, after=()):
    S = h.shape[0]
    R = min(512, S)

    def body(o_ref, gm_ref, gates_ref, h_ref, wa_ref, wg_ref, wo_ref, gate_ref, gp_ref,
             ya_ref, yg_ref, ym_ref, y_ref, hn_ref):
        for r0 in range(0, R, CHUNK):
            rows = slice(r0, r0 + CHUNK)
            ya = _dot(o_ref[rows, :], wa_ref[...])
            yg = _dot(gm_ref[rows, :], wg_ref[...])
            ya_ref[rows, :] = ya.astype(BF)
            yg_ref[rows, :] = yg.astype(BF)
            ym = (gates_ref[rows, 0:D].astype(F32) * ya + gates_ref[rows, D:2 * D].astype(F32) * yg).astype(BF)
            ym_ref[rows, :] = ym
            y = _dot(ym, wo_ref[...])
            y_ref[rows, :] = y.astype(BF)
            hn_ref[rows, :] = h_ref[rows, :] + gate_ref[...] * (y * _rms_r(y) * gp_ref[...])

    vec = _const((1, D))
    rows = lambda w_: pl.BlockSpec((R, w_), lambda i: (i, 0))
    body, after_specs = _behind(body, 9, after)
    return pl.pallas_call(
        body, name="mix_out", grid=(S // R,),
        out_shape=[_sds((S, D), BF)] * 4 + [_sds((S, D), F32)],
        in_specs=[rows(Q_W), rows(G_W), rows(2 * D), rows(D), _resident((Q_W, D)), _resident((G_W, D)),
                  _resident((D, D)), vec, vec] + after_specs,
        out_specs=[rows(D)] * 5,
        compiler_params=_cp(1, 48),
    )(*_in_hbm(o, gm, gates, h, wa, wg, wo), gate, gp, *after)


def _mix_out_bwd(dh, y, ya, yg, gates, att, gm, ymix, wa, wg, wo, gate, gp, after=()):
    S = dh.shape[0]
    R = min(512, S)
    nb = S // R

    def body(dh_ref, y_ref, ya_ref, yg_ref, gates_ref, att_ref, gm_ref, ym_ref, wa_ref, wg_ref, wo_ref,
             gate_ref, gp_ref, dz_ref, do_ref, dgm_ref, dgate_ref, dgp_ref, gwo_ref, gwa_ref, gwg_ref,
             acc_o, acc_a, acc_g, dy_scr, dya_scr, dyg_scr):
        i = pl.program_id(0)

        @pl.when(i == 0)
        def _():
            for r in (dgate_ref, dgp_ref, acc_o, acc_a, acc_g):
                r[...] = jnp.zeros_like(r)
        for r0 in range(0, R, 2 * CHUNK):
            rows = slice(r0, min(r0 + 2 * CHUNK, R))
            dy, dgate, dgp = _postnorm_bwd(dh_ref[rows, :], y_ref[rows, :], gate_ref[...], gp_ref[...], 1.0)
            dgate_ref[...] += dgate
            dgp_ref[...] += dgp
            dyb = dy.astype(BF)
            dy_scr[rows, :] = dyb
            dym = _dot_nt(dyb, wo_ref[...])
            ga = gates_ref[rows, 0:D].astype(F32)
            gg = gates_ref[rows, D:2 * D].astype(F32)
            dya = (dym * ga).astype(BF)
            dyg = (dym * gg).astype(BF)
            dya_scr[rows, :] = dya
            dyg_scr[rows, :] = dyg
            dz_ref[rows, 0:D] = (dym * ya_ref[rows, :].astype(F32) * (ga * (1.0 - ga))).astype(BF)
            dz_ref[rows, D:2 * D] = (dym * yg_ref[rows, :].astype(F32) * (gg * (1.0 - gg))).astype(BF)
            do_ref[rows, :] = _dot_nt(dya, wa_ref[...]).astype(BF)
            dgm_ref[rows, :] = _dot_nt(dyg, wg_ref[...]).astype(BF)
        for m0 in range(0, D, CHUNK):
            acc_o[m0:m0 + CHUNK, :] += _dot_tn(ym_ref[:, m0:m0 + CHUNK], dy_scr[...])
        for m0 in range(0, Q_W, CHUNK):
            acc_a[m0:m0 + CHUNK, :] += _dot_tn(att_ref[:, m0:m0 + CHUNK], dya_scr[...])
            acc_g[m0:m0 + CHUNK, :] += _dot_tn(gm_ref[:, m0:m0 + CHUNK], dyg_scr[...])

        @pl.when(i == nb - 1)
        def _():
            for m0 in range(0, D, CHUNK):
                gwo_ref[m0:m0 + CHUNK, :] = acc_o[m0:m0 + CHUNK, :].astype(BF)
            for m0 in range(0, Q_W, CHUNK):
                gwa_ref[m0:m0 + CHUNK, :] = acc_a[m0:m0 + CHUNK, :].astype(BF)
                gwg_ref[m0:m0 + CHUNK, :] = acc_g[m0:m0 + CHUNK, :].astype(BF)

    vec = _const((1, D))
    rows = lambda w_: pl.BlockSpec((R, w_), lambda i: (i, 0))
    body, after_specs = _behind(body, 13, after)
    return pl.pallas_call(
        body, name="mix_out_bwd", grid=(nb,),
        out_shape=[_sds((S, 2 * D), BF), _sds((S, Q_W), BF), _sds((S, G_W), BF), _sds((1, D), F32),
                   _sds((1, D), F32), _sds((D, D), BF), _sds((Q_W, D), BF), _sds((G_W, D), BF)],
        in_specs=[rows(D), rows(D), rows(D), rows(D), rows(2 * D), rows(Q_W), rows(G_W), rows(D),
                  _resident((Q_W, D)), _resident((G_W, D)), _resident((D, D)), vec, vec] + after_specs,
        out_specs=[rows(2 * D), rows(Q_W), rows(G_W), vec, vec, _const((D, D)), _const((Q_W, D)),
                   _const((G_W, D))],
        scratch_shapes=[pltpu.VMEM((D, D), F32), pltpu.VMEM((Q_W, D), F32), pltpu.VMEM((G_W, D), F32)]
        + [pltpu.VMEM((R, D), BF)] * 3,
        compiler_params=_cp(1, 60),
    )(*_in_hbm(dh, y, ya, yg, gates, att, gm, ymix, wa, wg, wo), gate, gp, *after)


def _mix_dn(dq, dkv, dzg, dzgate, w, wq, h, dh, sc, gp, after=()):
    S = h.shape[0]
    R = min(512, S)

    def body(dq_ref, dkv_ref, dzg_ref, dzt_ref, w_ref, wq_ref, h_ref, dh_ref, sc_ref, gp_ref,
             out_ref, dsh_ref, dsc_ref, dgp_ref):
        @pl.when(pl.program_id(0) == 0)
        def _():
            dsh_ref[...] = jnp.zeros_like(dsh_ref)
            dsc_ref[...] = jnp.zeros_like(dsc_ref)
            dgp_ref[...] = jnp.zeros_like(dgp_ref)
        for r0 in range(0, R, CHUNK):
            rows = slice(r0, r0 + CHUNK)
            dn = _dot(dq_ref[rows, :], wq_ref[...])
            dn = dn + _dot(dkv_ref[rows, :], w_ref[Q_W:QKV_W, :])
            dn = dn + _dot(dzg_ref[rows, :], w_ref[ZG_OFF:GATE_OFF, :])
            dn = dn + _dot(dzt_ref[rows, :], w_ref[GATE_OFF:IN_W, :])
            dx, dsh, dsc, dgp = _prenorm_bwd(dn, h_ref[rows, :], gp_ref[...], sc_ref[...])
            out_ref[rows, :] = dh_ref[rows, :] + dx
            dsh_ref[...] += dsh
            dsc_ref[...] += dsc
            dgp_ref[...] += dgp

    vec = _const((1, D))
    rows = lambda w_: pl.BlockSpec((R, w_), lambda i: (i, 0))
    body, after_specs = _behind(body, 10, after)
    return pl.pallas_call(
        body, name="mix_dn", grid=(S // R,),
        out_shape=[_sds((S, D), F32)] + [_sds((1, D), F32)] * 3,
        in_specs=[rows(Q_W), rows(2 * KV_W), rows(2 * G_W), rows(2 * D), _resident((IN_W, D)),
                  _resident((Q_W, D)), rows(D), rows(D), vec, vec] + after_specs,
        out_specs=[rows(D), vec, vec, vec],
        compiler_params=_cp(1, 48),
    )(*_in_hbm(dq, dkv, dzg, dzgate, w, wq, h, dh), sc, gp, *after)


def _adamw_math(w, g, m, v):
    m2 = ADAM_B1 * m + (1.0 - ADAM_B1) * g
    v2 = ADAM_B2 * v + (1.0 - ADAM_B2) * (g * g)
    m_hat = m2 / (1.0 - ADAM_B1 ** ADAM_STEP)
    v_hat = v2 / (1.0 - ADAM_B2 ** ADAM_STEP)
    delta = -ADAM_LR * (m_hat / (jnp.sqrt(v_hat) + ADAM_EPS) + ADAM_WD * w)
    return delta, m2, v2


def _row_tile(rows, cols):
    best = None
    for t in range(16, rows + 1, 16):
        if rows % t == 0 and t * cols <= 256 * 1024:
            best = t
    return best if best is not None else rows


def _adamw_sharded(landing, w, m, v, name):
    r, c = w.shape
    tr = _row_tile(r, c)

    def body(l_ref, w_ref, m_ref, v_ref, g_ref, d_ref, m2_ref, v2_ref):
        g = l_ref[0].astype(F32)
        for j in range(1, N_DEV):
            g = g + l_ref[j].astype(F32)
        delta, m2, v2 = _adamw_math(w_ref[...], g, m_ref[...], v_ref[...])
        g_ref[...] = g
        d_ref[...] = delta
        m2_ref[...] = m2
        v2_ref[...] = v2

    row = pl.BlockSpec((tr, c), lambda i: (i, 0))
    return pl.pallas_call(
        body, name=name, grid=(r // tr,),
        out_shape=[_sds((r, c), F32)] * 4,
        in_specs=[pl.BlockSpec((N_DEV, tr, c), lambda i: (0, i, 0)), row, row, row],
        out_specs=[row] * 4,
        compiler_params=_cp(1, 48),
    )(*_in_hbm(landing, w, m, v))


def _adamw_small(items):
    n = len(items)

    def body(*refs):
        for k in range(n):
            w_ref, g_ref, m_ref, v_ref = refs[4 * k:4 * k + 4]
            outs = refs[4 * n + 3 * k:4 * n + 3 * k + 3]
            for o_ref, val in zip(outs, _adamw_math(w_ref[...], g_ref[...], m_ref[...], v_ref[...])):
                o_ref[...] = val

    vm = pl.BlockSpec(memory_space=pltpu.VMEM)
    flat = pl.pallas_call(
        body, name="adamw_small",
        out_shape=[_sds(it[0].shape, F32) for it in items for _ in range(3)],
        in_specs=[vm] * (4 * n), out_specs=[vm] * (3 * n),
    )(*[a for it in items for a in it])
    return [tuple(flat[3 * k:3 * k + 3]) for k in range(n)]


def _w_ada_update(c8, d_ada, w, m, v):
    tr = 256

    def body(c_ref, d_ref, w_ref, m_ref, v_ref, g_ref, dl_ref, m2_ref, v2_ref):
        cs = c_ref[...]
        cs = cs * jax.nn.sigmoid(cs)
        g = lax.dot_general(cs, d_ref[...], (((0,), (0,)), ((), ())), preferred_element_type=F32, precision=HIGH)
        delta, m2, v2 = _adamw_math(w_ref[...], g, m_ref[...], v_ref[...])
        g_ref[...] = g
        dl_ref[...] = delta
        m2_ref[...] = m2
        v2_ref[...] = v2

    row = pl.BlockSpec((tr, ADA_W), lambda i: (i, 0))
    return pl.pallas_call(
        body, name="w_ada_update", grid=(D // tr,),
        out_shape=[_sds((D, ADA_W), F32)] * 4,
        in_specs=[pl.BlockSpec((N_DEV, tr), lambda i: (0, i)), _const((N_DEV, ADA_W)), row, row, row],
        out_specs=[row] * 4,
        compiler_params=_cp(1, 40),
    )(c8, d_ada, *_in_hbm(w, m, v))


def _t5_bucket():
    qi = np.arange(BLK, dtype=np.int32)[:, None]
    kj = np.arange(2 * BLK, dtype=np.int32)[None, :]
    dist = np.maximum(qi + BLK - kj, 0)
    max_exact = N_BUCKETS // 2
    d_f = np.maximum(dist, max_exact).astype(np.float32)
    large = max_exact + (np.log(d_f / np.float32(max_exact)) / np.float32(math.log(MAX_DISTANCE / max_exact))
                         * np.float32(N_BUCKETS - max_exact)).astype(np.int32)
    large = np.minimum(large, N_BUCKETS - 1)
    return jnp.asarray(np.where(dist < max_exact, dist, large).astype(np.int32))


def _slabs_of_columns(w):
    r, c8 = w.shape
    return jnp.transpose(w.reshape(r, N_DEV, c8 // N_DEV), (1, 0, 2))


def _columns_of_slabs(w8):
    _, r, c = w8.shape
    return jnp.transpose(w8, (1, 0, 2)).reshape(r, N_DEV * c)


def kernel(x, c, rel_bias, w_ada, b_ada, pre_norm_g, post_norm_g, w_ffn1_in, w_ffn1_out, w_in, sinks, gmlp_ln_g, gmlp_ln_b, gmlp_w_s, gmlp_b_s, w_br_attn, w_br_gmlp, w_out, w_ffn2_in, w_ffn2_out, loss_target, m_rel_bias, m_w_ada, m_b_ada, m_pre_norm_g, m_post_norm_g, m_w_ffn1_in, m_w_ffn1_out, m_w_in, m_sinks, m_gmlp_ln_g, m_gmlp_ln_b, m_gmlp_w_s, m_gmlp_b_s, m_w_br_attn, m_w_br_gmlp, m_w_out, m_w_ffn2_in, m_w_ffn2_out, v_rel_bias, v_w_ada, v_b_ada, v_pre_norm_g, v_post_norm_g, v_w_ffn1_in, v_w_ffn1_out, v_w_in, v_sinks, v_gmlp_ln_g, v_gmlp_ln_b, v_gmlp_w_s, v_gmlp_b_s, v_w_br_attn, v_w_br_gmlp, v_w_out, v_w_ffn2_in, v_w_ffn2_out):
    me = 4 * lax.axis_index("x") + 2 * lax.axis_index("y") + lax.axis_index("c")
    x0 = x[0]
    target = loss_target[0]

    transposed = ("w_ffn1_in", "w_in", "w_ffn2_in")
    shards = [w_ffn1_in[0].T, w_ffn1_out[0], w_in[0].T, w_br_attn[0], w_br_gmlp[0], w_out[0],
              w_ffn2_in[0].T, w_ffn2_out[0]]
    shards_bf = [s.astype(BF) for s in shards]
    groups = [shards_bf[0:1], shards_bf[1:6], shards_bf[6:8]]

    def gather_start(i, after):
        return _slabs_start("gather", groups[i], after, "gather_start_%d" % i)

    def forward_start(st, i, after):
        lands = _slabs_wait("gather", len(groups[i]), st, after, "gather_wait_%d" % i)
        return _slabs_start("forward", lands, c, "forward_start_%d" % i)

    def gathered(st, i, after):
        return _slabs_wait("forward", len(groups[i]), st, after, "forward_wait_%d" % i)

    gs0 = gather_start(0, c)

    mine = jnp.concatenate([c[0], pre_norm_g[0].reshape(-1), post_norm_g[0].reshape(-1)])
    small8 = jnp.broadcast_to(mine[None, :], (8, mine.shape[0]))
    b_ada64 = jnp.repeat(b_ada.reshape(N_DEV, ADA_W), 8, axis=0)
    gath, ada64 = _ada_forward(small8, w_ada[0], b_ada64)
    gath8 = gath[::8]
    ada = ada64[::8].reshape(9, D)
    sh1, sc1, g1, sh2, sc2, g2, sh3, sc3, g3 = [ada[k:k + 1] for k in range(9)]
    gains = gath8[:, D:].reshape(N_DEV, 2, 3, 128)
    pre_g = jnp.transpose(gains[:, 0], (1, 0, 2)).reshape(3, D)
    post_g = jnp.transpose(gains[:, 1], (1, 0, 2)).reshape(3, D)
    pre = [pre_g[k:k + 1] for k in range(3)]
    post = [post_g[k:k + 1] for k in range(3)]

    bucket = _t5_bucket()
    bias = _bias_table(rel_bias, bucket).reshape(HEAD_ROWS, 2 * BLK)
    sinks8 = sinks[0]
    lg, lb = gmlp_ln_g, gmlp_ln_b
    ws = gmlp_w_s[0]
    bst = jnp.transpose(gmlp_b_s[0])

    fs0 = forward_start(gs0, 0, sh1)
    gs1 = gather_start(1, fs0[-1])
    wf1_in = gathered(fs0, 0, gs1[-1])[0].reshape(2 * D_FF, D)
    n1, fg1, fu1, fa1 = _ffn_in(x0, sh1, sc1, pre[0], wf1_in, "ffn1_in")
    fs1 = forward_start(gs1, 1, n1)
    gs2 = gather_start(2, fs1[-1])
    mix_w = gathered(fs1, 1, gs2[-1])
    wf1_out = mix_w[0].reshape(D_FF, D)
    w_in_full = mix_w[1].reshape(IN_W, D)
    w_q = _pair_heads(w_in_full[0:Q_W])
    w_bra = _pair_heads(_columns_of_slabs(mix_w[2]))
    w_brg = _columns_of_slabs(mix_w[3])
    w_out_full = mix_w[4].reshape(D, D)
    h1, y1 = _ffn_out(fa1, wf1_out, x0, g1, post[0], "ffn1_out")
    n2, qkv, zg, gates = _mix_in(h1, sh2, sc2, pre[1], w_in_full, w_q)
    att = _attn_fwd(qkv, bias, sinks8)
    gm = _gmlp_fwd(zg, lg, lb, ws, bst)
    fs2 = forward_start(gs2, 2, gm)
    ya, yg, ymix, y2, h2 = _mix_out(att, gm, gates, h1, w_bra, w_brg, w_out_full, g2, post[1], after=(fs2[-1],))
    wf2_in, wf2_out = gathered(fs2, 2, h2)
    wf2_in = wf2_in.reshape(2 * D_FF, D)
    wf2_out = wf2_out.reshape(D_FF, D)
    n3, fg3, fu3, fa3 = _ffn_in(h2, sh3, sc3, pre[2], wf2_in, "ffn2_in")
    dh3, y3, sq = _ffn_out(fa3, wf2_out, h2, g3, post[2], "ffn2_out", target=target)

    def exchange_start(i, arrays):
        return _slabs_start("exchange", arrays, sq, "exchange_start_%d" % i)

    dy3, dgu3, dh2, d_g3, d_post2, d_sh3, d_sc3, d_pre2 = _ffn_bwd(
        dh3, y3, fg3, fu3, wf2_out, wf2_in, h2, g3, post[2], sc3, pre[2], "ffn2_bwd")
    gw_f2_out = _tn_matmul(fa3, dy3, "ffn2_out_wgrad", tm=D_FF // 2).reshape(N_DEV, D_FF // N_DEV, D)
    gw_f2_in = _tn_matmul(dgu3, n3, "ffn2_in_wgrad", tm=D_FF // 2).reshape(N_DEV, FS, D)
    ex1 = exchange_start(1, [gw_f2_out, gw_f2_in])

    dzgate, d_att, d_gm, d_g2, d_post1, gw_out, gw_bra, gw_brg = _mix_out_bwd(
        dh2, y2, ya, yg, gates, att, gm, ymix, w_bra, w_brg, w_out_full, g2, post[1], after=(ex1[-1],))
    ex2 = exchange_start(2, [_slabs_of_columns(_unpair_heads(gw_bra)), _slabs_of_columns(gw_brg),
                             gw_out.reshape(N_DEV, D // N_DEV, D)])
    dq, dkv, dbias, dsink = _attn_bwd(qkv, bias, sinks8, d_att)
    dzg, d_ws, d_bs, d_lg, d_lb = _gmlp_bwd(zg, d_gm, lg, lb, ws, bst)
    d_rel = _rel_bias_grad(dbias.reshape(N_KV, GROUP * BLK, 2 * BLK), bucket)
    early = jnp.concatenate([
        jnp.concatenate([d_lg.reshape(4, 128), d_lb.reshape(4, 128)], axis=0),
        d_bs, d_rel, dsink, d_ws.reshape(N_HEADS * BLK, BLK)], axis=0)
    sm0 = _slabs_start("gather_all", [early], sq, "small_gather_start")
    dh1, d_sh2, d_sc2, d_pre1 = _mix_dn(dq, dkv, dzg, dzgate, w_in_full, w_q, h1, dh2, sc2, pre[1],
                                        after=(ex2[-1], sm0[-1]))
    gw_in = jnp.concatenate(
        [_unpair_heads(_tn_matmul(dq, n2, "w_in_q_wgrad")), _tn_matmul(dkv, n2, "w_in_kv_wgrad"),
         _tn_matmul(dzg, n2, "w_in_zg_wgrad"), _tn_matmul(dzgate, n2, "w_in_gate_wgrad")],
        axis=0).reshape(N_DEV, IN_W // N_DEV, D)
    ex3 = exchange_start(3, [gw_in])

    dy1, dgu1, d_g1, d_post0 = _ffn_out_bwd(dh1, y1, fg1, fu1, wf1_out, g1, post[0], "ffn1_out_bwd",
                                            after=(ex3[-1],))
    gw_f1_out = _tn_matmul(fa1, dy1, "ffn1_out_wgrad", tm=D_FF // 2).reshape(N_DEV, D_FF // N_DEV, D)
    ex4 = exchange_start(4, [gw_f1_out])
    gw_f1_in = _tn_matmul(dgu1, n1, "ffn1_in_wgrad", tm=D_FF // 2).reshape(N_DEV, FS, D)
    ex5 = exchange_start(5, [gw_f1_in])
    grad_x, d_sh1, d_sc1, d_pre0 = _ffn_dn(dgu1, wf1_in, x0, dh1, sc1, pre[0], "ffn1_dn", after=(ex4[-1], ex5[-1]))

    landed = {}
    for i, (ex, nms) in enumerate([(ex1, ["w_ffn2_out", "w_ffn2_in"]),
                                   (ex2, ["w_br_attn", "w_br_gmlp", "w_out"]), (ex3, ["w_in"]),
                                   (ex4, ["w_ffn1_out"]), (ex5, ["w_ffn1_in"])]):
        for nm, land in zip(nms, _slabs_wait("exchange", len(nms), ex, grad_x, "exchange_wait_%d" % i)):
            landed[nm] = land
    moments = [(m_w_ffn1_in, v_w_ffn1_in), (m_w_ffn1_out, v_w_ffn1_out), (m_w_in, v_w_in),
               (m_w_br_attn, v_w_br_attn), (m_w_br_gmlp, v_w_br_gmlp), (m_w_out, v_w_out),
               (m_w_ffn2_in, v_w_ffn2_in), (m_w_ffn2_out, v_w_ffn2_out)]
    names = ["w_ffn1_in", "w_ffn1_out", "w_in", "w_br_attn", "w_br_gmlp", "w_out", "w_ffn2_in", "w_ffn2_out"]
    big = {}
    for nm, w_, (m_, v_) in zip(names, shards, moments):
        if nm in transposed:
            res4 = _adamw_sharded(landed[nm], w_, m_[0].T, v_[0].T, "adamw_" + nm)
            big[nm] = [a.T[None] for a in res4]
        else:
            big[nm] = [a[None] for a in _adamw_sharded(landed[nm], w_, m_[0], v_[0], "adamw_" + nm)]

    my_loss = jnp.broadcast_to(sq * (0.5 / D), (1, D))
    my_loss, _ = lax.optimization_barrier((my_loss, landed["w_ffn1_in"]))
    tot, every = _small_allreduce([d_sh1, d_sc1, d_g1, d_sh2, d_sc2, d_g2, d_sh3, d_sc3, d_g3,
                                   d_pre0, d_pre1, d_pre2, d_post0, d_post1, d_post2, my_loss])
    (early_land,) = _slabs_wait("gather_all", 1, sm0, grad_x, "small_gather_wait")
    tot_early = _sum_slabs(early_land)

    loss = tot[15, 0]
    g_b_ada = tot[0:9].reshape(1, 9 * D)
    g_pre = lax.dynamic_slice_in_dim(tot[9:12], 128 * me, 128, axis=1)[None]
    g_post = lax.dynamic_slice_in_dim(tot[12:15], 128 * me, 128, axis=1)[None]
    g_lg = tot_early[0:4].reshape(1, G_W)
    g_lb = tot_early[4:8].reshape(1, G_W)
    g_bs = tot_early[8:16][None]
    g_rel = jnp.transpose(tot_early[16:24, 0:N_BUCKETS])
    g_sinks = tot_early[24:32, 0][None]
    g_ws = tot_early[32:1056].reshape(1, N_HEADS, BLK, BLK)

    d_ada_mine = lax.dynamic_slice_in_dim(every[:, 0:9].reshape(N_DEV, 9 * D), ADA_W * me, ADA_W, axis=1)
    ada_out = [a[None] for a in _w_ada_update(gath8[:, 0:D], d_ada_mine, w_ada[0], m_w_ada[0], v_w_ada[0])]

    small = [("rel_bias", rel_bias, g_rel, m_rel_bias, v_rel_bias), ("b_ada", b_ada, g_b_ada, m_b_ada, v_b_ada),
             ("pre_norm_g", pre_norm_g, g_pre, m_pre_norm_g, v_pre_norm_g),
             ("post_norm_g", post_norm_g, g_post, m_post_norm_g, v_post_norm_g),
             ("sinks", sinks, g_sinks, m_sinks, v_sinks), ("gmlp_ln_g", gmlp_ln_g, g_lg, m_gmlp_ln_g, v_gmlp_ln_g),
             ("gmlp_ln_b", gmlp_ln_b, g_lb, m_gmlp_ln_b, v_gmlp_ln_b),
             ("gmlp_w_s", gmlp_w_s, g_ws, m_gmlp_w_s, v_gmlp_w_s), ("gmlp_b_s", gmlp_b_s, g_bs, m_gmlp_b_s, v_gmlp_b_s)]
    two_d = lambda a: a.reshape(int(math.prod(a.shape[:-1])), a.shape[-1])
    stepped = _adamw_small([tuple(two_d(a) for a in item[1:]) for item in small])
    res = {"w_ada": ada_out}
    for (nm, w_, g_, _, _), new in zip(small, stepped):
        res[nm] = [g_] + [a.reshape(w_.shape) for a in new]
    res.update(big)
    order = ["rel_bias", "w_ada", "b_ada", "pre_norm_g", "post_norm_g", "w_ffn1_in", "w_ffn1_out", "w_in", "sinks",
             "gmlp_ln_g", "gmlp_ln_b", "gmlp_w_s", "gmlp_b_s", "w_br_attn", "w_br_gmlp", "w_out", "w_ffn2_in",
             "w_ffn2_out"]
    outs = [loss, grad_x[None]]
    for k in range(4):
        outs += [res[nm][k] for nm in order]
    return tuple(outs)
```

```python
import math

import jax
import jax.numpy as jnp
import numpy as np
from jax import lax
from jax.experimental import pallas as pl
from jax.experimental.pallas import tpu as pltpu

F32 = jnp.float32
BF = jnp.bfloat16

N_DEV = 8
D = 1024
D_FF = 2816
FS = D_FF // 4
N_HEADS = 8
N_KV = 2
GROUP = 4
HD = 64
BLK = 128
Q_W = 512
KV_W = 128
G_W = 512
QKV_W = Q_W + 2 * KV_W
ZG_OFF = QKV_W
GATE_OFF = ZG_OFF + 2 * G_W
IN_W = GATE_OFF + 2 * D
N_BUCKETS = 32
MAX_DISTANCE = 128
EPS = 1e-6
NEG = -1e30
SCALE = HD ** -0.5
ADA_W = 9 * D // N_DEV

ADAM_LR = 0.001
ADAM_B1 = 0.9
ADAM_B2 = 0.999
ADAM_EPS = 1e-08
ADAM_WD = 0.01
ADAM_STEP = 10

CHUNK = 256
MIB = 1024 * 1024
MESH = pl.DeviceIdType.MESH
HIGH = lax.Precision.HIGHEST


def _cp(n_grid, vmem_mib):
    return pltpu.CompilerParams(dimension_semantics=("arbitrary",) * n_grid,
                                vmem_limit_bytes=vmem_mib * MIB)


def _const(shape):
    return pl.BlockSpec(shape, lambda *_: (0,) * len(shape))


def _resident(shape):
    return pl.BlockSpec(shape, lambda *_: (0,) * len(shape), pipeline_mode=pl.Buffered(1))


def _behind(body, n_in, after):
    k = len(after)
    return (lambda *refs: body(*refs[:n_in], *refs[n_in + k:])), [pl.BlockSpec(memory_space=pl.ANY)] * k


def _in_hbm(*arrays):
    return [pltpu.with_memory_space_constraint(a, pltpu.HBM) for a in arrays]


def _sds(shape, dtype):
    return jax.ShapeDtypeStruct(shape, dtype)


def _dot(a, b):
    return jnp.dot(a, b, preferred_element_type=F32)


def _dot_nt(a, b):
    return lax.dot_general(a, b, (((1,), (1,)), ((), ())), preferred_element_type=F32)


def _dot_tn(a, b):
    return lax.dot_general(a, b, (((0,), (0,)), ((), ())), preferred_element_type=F32)


def _rms_r(x):
    return lax.rsqrt(jnp.mean(x * x, axis=-1, keepdims=True) + EPS)


def _colsum(x):
    return jnp.sum(x, axis=0, keepdims=True)


def _prenorm(x, gp, sc, sh):
    return (x * _rms_r(x) * gp) * (1.0 + sc) + sh


def _prenorm_bwd(dn, x, gp, sc):
    r = _rms_r(x)
    xh = x * r
    t = dn * (1.0 + sc) * gp
    dx = r * (t - xh * jnp.mean(t * xh, axis=-1, keepdims=True))
    return dx, _colsum(dn), _colsum(dn * xh * gp), _colsum(dn * (1.0 + sc) * xh)


def _postnorm_bwd(dh, y, gate, gp, res):
    y = y.astype(F32)
    r = _rms_r(y)
    yh = y * r
    dyn = (res * gate) * dh
    t = dyn * gp
    dy = r * (t - yh * jnp.mean(t * yh, axis=-1, keepdims=True))
    return dy, _colsum(res * dh * yh * gp), _colsum(dyn * yh)


def _gelu(x):
    k = math.sqrt(2.0 / math.pi)
    return 0.5 * x * (1.0 + jnp.tanh(k * (x + 0.044715 * x * x * x)))


def _gelu_grad(x):
    k = math.sqrt(2.0 / math.pi)
    t = jnp.tanh(k * (x + 0.044715 * x * x * x))
    return 0.5 * (1.0 + t) + 0.5 * x * (1.0 - t * t) * (k * (1.0 + 3.0 * 0.044715 * x * x))


def _my_place():
    x, y, c = lax.axis_index("x"), lax.axis_index("y"), lax.axis_index("c")
    return x, y, c, 4 * x + 2 * y + c


def _peer(x, y, c, k):
    px = 1 - x if k & 4 else x
    py = 1 - y if k & 2 else y
    pc = 1 - c if k & 1 else c
    return (px, py, pc), 4 * px + 2 * py + pc


HBM_SPEC = pl.BlockSpec(memory_space=pltpu.HBM)
SEM_SPEC = pl.BlockSpec(memory_space=pltpu.SEMAPHORE)
EFFECT = pltpu.SideEffectType.DATAFLOW_SIDE_EFFECTING


RELATIONS = {"exchange": (1, 2, 3, 4, 5, 6, 7), "gather": (1, 2, 4, 6), "forward": (2, 4, 6),
             "gather_all": (1, 2, 3, 4, 5, 6, 7)}


def _slab_copies(mode, srcs, lands, send, recv, loc):
    x, y, c, me = _my_place()
    rel = RELATIONS[mode]
    remote, local = [], []
    for t in range(len(lands)):
        for i, k in enumerate(rel):
            peer, peer_lin = _peer(x, y, c, k)
            if mode == "exchange":
                src, dst, to = srcs[t].at[peer_lin], lands[t].at[me], peer
            elif mode in ("gather", "gather_all"):
                src, dst, to = srcs[t], lands[t].at[me], peer
            else:
                src, dst, to = lands[t].at[peer_lin], lands[t].at[peer_lin], _peer(x, y, c, 1)[0]
            remote.append(pltpu.make_async_remote_copy(
                src_ref=src, dst_ref=dst, send_sem=send.at[t * len(rel) + i], recv_sem=recv.at[t * len(rel) + i],
                device_id=to, device_id_type=MESH))
        if mode == "exchange":
            local.append(pltpu.make_async_copy(srcs[t].at[me], lands[t].at[me], loc.at[t]))
        elif mode in ("gather", "gather_all"):
            local.append(pltpu.make_async_copy(srcs[t], lands[t].at[me], loc.at[t]))
    return remote, local


def _slabs_start(mode, arrays, after, name):
    n = len(arrays)
    if mode == "forward":
        thru = list(arrays)
    else:
        shapes = [a.shape if mode == "exchange" else (N_DEV,) + a.shape for a in arrays]
        thru = list(arrays) + [lax.empty(s, a.dtype) for s, a in zip(shapes, arrays)]
    m = len(thru)
    n_sem = n * len(RELATIONS[mode])

    def body(*refs):
        srcs, lands = refs[:n], refs[m - n:m]
        send, recv, loc = refs[m + 1:m + 4]
        remote, local = _slab_copies(mode, srcs, lands, send, recv, loc)
        for cp in remote + local:
            cp.start()
        refs[-1][...] = jnp.zeros_like(refs[-1])

    return pl.pallas_call(
        body, name=name,
        out_shape=(pltpu.SemaphoreType.DMA((n_sem,)), pltpu.SemaphoreType.DMA((n_sem,)),
                   pltpu.SemaphoreType.DMA((n,)),
                   *[pltpu.HBM(a.shape, a.dtype) for a in thru],
                   _sds((1, D), F32)),
        in_specs=[HBM_SPEC] * m + [pl.BlockSpec(memory_space=pl.ANY)],
        out_specs=(SEM_SPEC, SEM_SPEC, SEM_SPEC, *[HBM_SPEC] * m, pl.BlockSpec(memory_space=pltpu.VMEM)),
        input_output_aliases={t: 3 + t for t in range(m)},
        compiler_params=pltpu.CompilerParams(has_side_effects=EFFECT),
    )(*[pltpu.with_memory_space_constraint(a, pltpu.HBM) for a in thru], after)


def _slabs_wait(mode, n, started, after, name):
    sems = started[0:3]
    thru = started[3:-1]
    m = len(thru)

    def body(*refs):
        srcs, lands = refs[:n], refs[m - n:m]
        remote, local = _slab_copies(mode, srcs, lands, *refs[m:m + 3])
        for cp in remote:
            cp.wait_send()
            cp.wait_recv()
        for cp in local:
            cp.wait()

    res = pl.pallas_call(
        body, name=name,
        out_shape=tuple(pltpu.HBM(a.shape, a.dtype) for a in thru),
        in_specs=[HBM_SPEC] * m + [SEM_SPEC] * 3 + [pl.BlockSpec(memory_space=pl.ANY)],
        out_specs=tuple([HBM_SPEC] * m),
        input_output_aliases={t: t for t in range(m)},
        compiler_params=pltpu.CompilerParams(has_side_effects=EFFECT),
    )(*thru, *sems, after)
    return list(res[m - n:m])


def _ada_forward(small8, w_ada, b_ada64):
    sw = small8.shape[1]

    def body(sm_ref, w_ref, b_ref, gath_ref, ada_ref, part_ref, send1, recv1, send2, recv2):
        x, y, c, me = _my_place()
        row_me = pl.multiple_of(me * 8, 8)
        gath_ref[pl.ds(row_me, 8), :] = sm_ref[...]
        first = []
        for k in range(1, N_DEV):
            peer, _ = _peer(x, y, c, k)
            cp = pltpu.make_async_remote_copy(
                src_ref=sm_ref, dst_ref=gath_ref.at[pl.ds(row_me, 8), :], send_sem=send1.at[k - 1],
                recv_sem=recv1.at[k - 1], device_id=peer, device_id_type=MESH)
            cp.start()
            first.append(cp)
        for cp in first:
            cp.wait()
        cs = gath_ref[:, 0:D]
        cs = cs * jax.nn.sigmoid(cs)
        part_ref[...] = jnp.dot(cs, w_ref[...], preferred_element_type=F32, precision=HIGH)
        ada_ref[pl.ds(row_me, 8), :] = part_ref[pl.ds(row_me, 8), :]
        second = []
        for k in range(1, N_DEV):
            peer, peer_lin = _peer(x, y, c, k)
            cp = pltpu.make_async_remote_copy(
                src_ref=part_ref.at[pl.ds(pl.multiple_of(peer_lin * 8, 8), 8), :],
                dst_ref=ada_ref.at[pl.ds(row_me, 8), :], send_sem=send2.at[k - 1],
                recv_sem=recv2.at[k - 1], device_id=peer, device_id_type=MESH)
            cp.start()
            second.append(cp)
        for cp in second:
            cp.wait()
        ada_ref[...] = ada_ref[...] + b_ref[...]

    vm = pl.BlockSpec(memory_space=pltpu.VMEM)
    return pl.pallas_call(
        body, name="ada_forward",
        out_shape=[_sds((8 * N_DEV, sw), F32), _sds((8 * N_DEV, ADA_W), F32)],
        in_specs=[vm, vm, vm], out_specs=[vm, vm],
        scratch_shapes=[pltpu.VMEM((8 * N_DEV, ADA_W), F32)] + [pltpu.SemaphoreType.DMA((7,))] * 4,
        compiler_params=pltpu.CompilerParams(vmem_limit_bytes=32 * MIB),
    )(small8, w_ada, b_ada64)


def _sum_slabs(land):
    def body(l_ref, o_ref):
        acc = l_ref[0]
        for j in range(1, N_DEV):
            acc = acc + l_ref[j]
        o_ref[...] = acc

    vm = pl.BlockSpec(memory_space=pltpu.VMEM)
    return pl.pallas_call(body, name="sum_slabs", out_shape=_sds(land.shape[1:], F32), in_specs=[vm], out_specs=vm,
                          compiler_params=pltpu.CompilerParams(vmem_limit_bytes=32 * MIB))(land)


def _small_allreduce(vectors):
    n = len(vectors)

    def body(*refs):
        v_refs, (sum_ref, gath_ref, pack, send, recv) = refs[:n], refs[n:]
        x, y, c, me = _my_place()
        for k in range(n):
            pack[k:k + 1, :] = v_refs[k][...]
        gath_ref[me] = pack[...]
        cps = []
        for k in range(1, N_DEV):
            peer, _ = _peer(x, y, c, k)
            cp = pltpu.make_async_remote_copy(
                src_ref=pack, dst_ref=gath_ref.at[me], send_sem=send.at[k - 1],
                recv_sem=recv.at[k - 1], device_id=peer, device_id_type=MESH)
            cp.start()
            cps.append(cp)
        for cp in cps:
            cp.wait()
        acc = gath_ref[0]
        for j in range(1, N_DEV):
            acc = acc + gath_ref[j]
        sum_ref[...] = acc

    vm = pl.BlockSpec(memory_space=pltpu.VMEM)
    return pl.pallas_call(
        body, name="small_allreduce",
        out_shape=[_sds((n, D), F32), _sds((N_DEV, n, D), F32)],
        in_specs=[vm] * n, out_specs=[vm, vm],
        scratch_shapes=[pltpu.VMEM((n, D), F32), pltpu.SemaphoreType.DMA((7,)), pltpu.SemaphoreType.DMA((7,))],
    )(*vectors)


F_TILES = tuple((f0, min(512, D_FF - f0)) for f0 in range(0, D_FF, 512))
F_TILES_NARROW = tuple((f0, 256) for f0 in range(0, D_FF, 256))


def _swiglu_tile(n, wt_ref, f0, tf):
    g = _dot_nt(n, wt_ref[f0:f0 + tf, :])
    u = _dot_nt(n, wt_ref[D_FF + f0:D_FF + f0 + tf, :])
    sg = jax.nn.sigmoid(g)
    silu = g * sg
    return (u * (sg * (1.0 + g * (1.0 - sg)))).astype(BF), silu.astype(BF), (silu * u).astype(BF)


def _ffn_in(h, sh, sc, gp, wt, name):
    S = h.shape[0]
    R = min(512, S)

    def body(h_ref, sh_ref, sc_ref, gp_ref, w_ref, n_ref, dg_ref, sl_ref, a_ref):
        for r0 in range(0, R, CHUNK):
            rows = slice(r0, r0 + CHUNK)
            n = _prenorm(h_ref[rows, :], gp_ref[...], sc_ref[...], sh_ref[...]).astype(BF)
            n_ref[rows, :] = n
            for f0, tf in F_TILES_NARROW:
                dg_ref[rows, f0:f0 + tf], sl_ref[rows, f0:f0 + tf], a_ref[rows, f0:f0 + tf] = _swiglu_tile(
                    n, w_ref, f0, tf)

    vec = _const((1, D))
    rows_ = lambda w_: pl.BlockSpec((R, w_), lambda i: (i, 0))
    return pl.pallas_call(
        body, name=name, grid=(S // R,),
        out_shape=[_sds((S, D), BF)] + [_sds((S, D_FF), BF)] * 3,
        in_specs=[rows_(D), vec, vec, vec, _resident((2 * D_FF, D))],
        out_specs=[rows_(D), rows_(D_FF), rows_(D_FF), rows_(D_FF)],
        compiler_params=_cp(1, 56),
    )(*_in_hbm(h), sh, sc, gp, *_in_hbm(wt))


def _ffn_out(a, w, h, gate, gp, name, target=None):
    S = h.shape[0]
    R = min(512, S)
    with_loss = target is not None

    def body(a_ref, w_ref, h_ref, gate_ref, gp_ref, *rest):
        if with_loss:
            t_ref, out_ref, y_ref, tot_ref = rest

            @pl.when(pl.program_id(0) == 0)
            def _():
                tot_ref[...] = jnp.zeros_like(tot_ref)
        else:
            out_ref, y_ref = rest
        for r0 in range(0, R, CHUNK):
            rows = slice(r0, r0 + CHUNK)
            y = _dot(a_ref[rows, :], w_ref[...])
            y_ref[rows, :] = y.astype(BF)
            hn = h_ref[rows, :] + (0.5 * gate_ref[...]) * (y * _rms_r(y) * gp_ref[...])
            if with_loss:
                e = hn - t_ref[rows, :]
                out_ref[rows, :] = e * (1.0 / D)
                tot_ref[...] += jnp.sum(jnp.sum(e * e, axis=1, keepdims=True), axis=0, keepdims=True)
            else:
                out_ref[rows, :] = hn

    vec = _const((1, D))
    rows_ = lambda w_: pl.BlockSpec((R, w_), lambda i: (i, 0))
    return pl.pallas_call(
        body, name=name, grid=(S // R,),
        out_shape=[_sds((S, D), F32), _sds((S, D), BF)] + ([_sds((1, 1), F32)] if with_loss else []),
        in_specs=[rows_(D_FF), _resident((D_FF, D)), rows_(D), vec, vec] + ([rows_(D)] if with_loss else []),
        out_specs=[rows_(D), rows_(D)] + ([_const((1, 1))] if with_loss else []),
        compiler_params=_cp(1, 48),
    )(*_in_hbm(a, w, h), gate, gp, *(_in_hbm(target) if with_loss else ()))


def _ffn_out_bwd(dh, y, dsilu_u, silu, w, gate, gp, name, after=()):
    S = dh.shape[0]
    R = min(512, S)

    def body(dh_ref, y_ref, g_ref, u_ref, w_ref, gate_ref, gp_ref, dy_ref, dgu_ref, dgate_ref, dgp_ref):
        @pl.when(pl.program_id(0) == 0)
        def _():
            dgate_ref[...] = jnp.zeros_like(dgate_ref)
            dgp_ref[...] = jnp.zeros_like(dgp_ref)
        for r0 in range(0, R, CHUNK):
            rows = slice(r0, r0 + CHUNK)
            dy, dgate, dgp = _postnorm_bwd(dh_ref[rows, :], y_ref[rows, :], gate_ref[...], gp_ref[...], 0.5)
            dgate_ref[...] += dgate
            dgp_ref[...] += dgp
            dyb = dy.astype(BF)
            dy_ref[rows, :] = dyb
            for f0, tf in F_TILES:
                da = _dot_nt(dyb, w_ref[f0:f0 + tf, :])
                dgu_ref[rows, f0:f0 + tf] = (da * g_ref[rows, f0:f0 + tf].astype(F32)).astype(BF)
                dgu_ref[rows, D_FF + f0:D_FF + f0 + tf] = (da * u_ref[rows, f0:f0 + tf].astype(F32)).astype(BF)

    vec = _const((1, D))
    rows_ = lambda w_: pl.BlockSpec((R, w_), lambda i: (i, 0))
    body, after_specs = _behind(body, 7, after)
    return pl.pallas_call(
        body, name=name, grid=(S // R,),
        out_shape=[_sds((S, D), BF), _sds((S, 2 * D_FF), BF), _sds((1, D), F32), _sds((1, D), F32)],
        in_specs=[rows_(D), rows_(D), rows_(D_FF), rows_(D_FF), _resident((D_FF, D)), vec, vec] + after_specs,
        out_specs=[rows_(D), rows_(2 * D_FF), vec, vec],
        compiler_params=_cp(1, 56),
    )(*_in_hbm(dh, y, dsilu_u, silu, w), gate, gp, *after)


def _ffn_dn(dgu, wt, h, dh, sc, gp, name, after=()):
    S = h.shape[0]
    R = min(512, S)

    def body(dgu_ref, w_ref, h_ref, dh_ref, sc_ref, gp_ref, out_ref, dsh_ref, dsc_ref, dgp_ref):
        @pl.when(pl.program_id(0) == 0)
        def _():
            dsh_ref[...] = jnp.zeros_like(dsh_ref)
            dsc_ref[...] = jnp.zeros_like(dsc_ref)
            dgp_ref[...] = jnp.zeros_like(dgp_ref)

        for r0 in range(0, R, CHUNK):
            rows = slice(r0, r0 + CHUNK)
            dn = _dot(dgu_ref[rows, :], w_ref[...])
            dx, dsh, dsc, dgp = _prenorm_bwd(dn, h_ref[rows, :], gp_ref[...], sc_ref[...])
            out_ref[rows, :] = dh_ref[rows, :] + dx
            dsh_ref[...] += dsh
            dsc_ref[...] += dsc
            dgp_ref[...] += dgp

    vec = _const((1, D))
    rows_ = lambda w_: pl.BlockSpec((R, w_), lambda i: (i, 0))
    body, after_specs = _behind(body, 6, after)
    return pl.pallas_call(
        body, name=name, grid=(S // R,),
        out_shape=[_sds((S, D), F32)] + [_sds((1, D), F32)] * 3,
        in_specs=[rows_(2 * D_FF), _resident((2 * D_FF, D)), rows_(D), rows_(D), vec, vec] + after_specs,
        out_specs=[rows_(D), vec, vec, vec],
        compiler_params=_cp(1, 56),
    )(*_in_hbm(dgu, wt, h, dh), sc, gp, *after)


def _ffn_bwd(dh, y, dsilu_u, silu, w, wt, h, gate, gpost, sc, gpre, name):
    S = dh.shape[0]
    R = min(256, S)

    def body(dh_ref, y_ref, g_ref, u_ref, w_ref, wt_ref, h_ref, gate_ref, gpost_ref, sc_ref, gpre_ref,
             dy_ref, dgu_ref, out_ref, dgate_ref, dgpost_ref, dsh_ref, dsc_ref, dgpre_ref):
        @pl.when(pl.program_id(0) == 0)
        def _():
            for r in (dgate_ref, dgpost_ref, dsh_ref, dsc_ref, dgpre_ref):
                r[...] = jnp.zeros_like(r)
        dhh = dh_ref[...]
        dy, dgate, dgpost = _postnorm_bwd(dhh, y_ref[...], gate_ref[...], gpost_ref[...], 0.5)
        dgate_ref[...] += dgate
        dgpost_ref[...] += dgpost
        dyb = dy.astype(BF)
        dy_ref[...] = dyb
        for f0, tf in F_TILES:
            da = _dot_nt(dyb, w_ref[f0:f0 + tf, :])
            dgu_ref[:, f0:f0 + tf] = (da * g_ref[:, f0:f0 + tf].astype(F32)).astype(BF)
            dgu_ref[:, D_FF + f0:D_FF + f0 + tf] = (da * u_ref[:, f0:f0 + tf].astype(F32)).astype(BF)
        dn = _dot(dgu_ref[...], wt_ref[...])
        dx, dsh, dsc, dgpre = _prenorm_bwd(dn, h_ref[...], gpre_ref[...], sc_ref[...])
        out_ref[...] = dhh + dx
        dsh_ref[...] += dsh
        dsc_ref[...] += dsc
        dgpre_ref[...] += dgpre

    vec = _const((1, D))
    rows_ = lambda w_: pl.BlockSpec((R, w_), lambda i: (i, 0))
    return pl.pallas_call(
        body, name=name, grid=(S // R,),
        out_shape=[_sds((S, D), BF), _sds((S, 2 * D_FF), BF), _sds((S, D), F32)] + [_sds((1, D), F32)] * 5,
        in_specs=[rows_(D), rows_(D), rows_(D_FF), rows_(D_FF), _resident((D_FF, D)), _resident((2 * D_FF, D)),
                  rows_(D), vec, vec, vec, vec],
        out_specs=[rows_(D), rows_(2 * D_FF), rows_(D)] + [vec] * 5,
        compiler_params=_cp(1, 56),
    )(*_in_hbm(dh, y, dsilu_u, silu, w, wt, h), gate, gpost, sc, gpre)


def _tn_matmul(a, b, name, tm=None):
    S, M_all = a.shape
    N = b.shape[1]
    M = M_all if tm is None else tm
    GA = M_all // M
    ts = min(2048 if M * N <= 2 * D * D else 1024, S)
    nk = S // ts
    chunks = [(m0, min(CHUNK, M - m0)) for m0 in range(0, M, CHUNK)]

    def body(a_ref, b_ref, o_ref, acc):
        k = pl.program_id(1)

        @pl.when(k == 0)
        def _():
            acc[...] = jnp.zeros_like(acc)

        for m0, mc in chunks:
            acc[m0:m0 + mc, :] += _dot_tn(a_ref[:, m0:m0 + mc], b_ref[...])

        @pl.when(k == nk - 1)
        def _():
            for m0, mc in chunks:
                o_ref[m0:m0 + mc, :] = acc[m0:m0 + mc, :].astype(BF)

    return pl.pallas_call(
        body, name=name, grid=(GA, nk),
        out_shape=_sds((M_all, N), BF),
        in_specs=[pl.BlockSpec((ts, M), lambda ga, k: (k, ga)), pl.BlockSpec((ts, N), lambda ga, k: (k, 0))],
        out_specs=pl.BlockSpec((M, N), lambda ga, k: (ga, 0)),
        scratch_shapes=[pltpu.VMEM((M, N), F32)],
        compiler_params=_cp(2, 56),
    )(*_in_hbm(a, b))


def _mix_in(h, sh, sc, gp, w, wq):
    S = h.shape[0]
    R = min(512, S)

    def body(h_ref, sh_ref, sc_ref, gp_ref, w_ref, wq_ref, n_ref, qkv_ref, zg_ref, gates_ref):
        for r0 in range(0, R, CHUNK):
            rows = slice(r0, r0 + CHUNK)
            nb = _prenorm(h_ref[rows, :], gp_ref[...], sc_ref[...], sh_ref[...]).astype(BF)
            n_ref[rows, :] = nb
            qkv_ref[rows, 0:Q_W] = _dot_nt(nb, wq_ref[...]).astype(BF)
            qkv_ref[rows, Q_W:QKV_W] = _dot_nt(nb, w_ref[Q_W:QKV_W, :]).astype(BF)
            zg_ref[rows, :] = _dot_nt(nb, w_ref[ZG_OFF:GATE_OFF, :]).astype(BF)
            gates_ref[rows, :] = jax.nn.sigmoid(_dot_nt(nb, w_ref[GATE_OFF:IN_W, :])).astype(BF)

    vec = _const((1, D))
    rows = lambda w_: pl.BlockSpec((R, w_), lambda i: (i, 0))
    return pl.pallas_call(
        body, name="mix_in", grid=(S // R,),
        out_shape=[_sds((S, D), BF), _sds((S, QKV_W), BF), _sds((S, 2 * G_W), BF), _sds((S, 2 * D), BF)],
        in_specs=[rows(D), vec, vec, vec, _resident((IN_W, D)), _resident((Q_W, D))],
        out_specs=[rows(D), rows(QKV_W), rows(2 * G_W), rows(2 * D)],
        compiler_params=_cp(1, 48),
    )(*_in_hbm(h), sh, sc, gp, *_in_hbm(w, wq))


def _bias_table(rel_bias, bucket):
    def body(rel_ref, bk_ref, out_ref):
        bk = bk_ref[...]
        qi = lax.broadcasted_iota(jnp.int32, (BLK, 2 * BLK), 0)
        kj = lax.broadcasted_iota(jnp.int32, (BLK, 2 * BLK), 1)
        dist = qi + BLK - kj
        window = (dist >= 0) & (dist < BLK)
        for h in range(N_HEADS):
            acc = jnp.zeros((BLK, 2 * BLK), F32)
            for b in range(N_BUCKETS):
                acc = jnp.where(bk == b, rel_ref[b, h], acc)
            out_ref[h // GROUP, pl.ds((h % GROUP) * BLK, BLK), :] = jnp.where(window, acc, NEG)

    return pl.pallas_call(
        body, name="bias_table",
        out_shape=_sds((N_KV, GROUP * BLK, 2 * BLK), F32),
        in_specs=[pl.BlockSpec(memory_space=pltpu.SMEM), pl.BlockSpec(memory_space=pltpu.VMEM)],
        out_specs=pl.BlockSpec(memory_space=pltpu.VMEM),
    )(rel_bias, bucket)


ATT_TB = 4


HEAD_ROWS = N_HEADS * BLK


def _pair_heads(w):
    return jnp.transpose(w.reshape(N_KV, GROUP, HD, w.shape[1]), (1, 0, 2, 3)).reshape(w.shape)


def _unpair_heads(w):
    return jnp.transpose(w.reshape(GROUP, N_KV, HD, w.shape[1]), (1, 0, 2, 3)).reshape(w.shape)


def _halves(x, scale=1.0):
    low = lax.broadcasted_iota(jnp.int32, x.shape, 1) < HD
    xf = x.astype(F32) * scale
    return jnp.where(low, xf, 0.0).astype(BF), jnp.where(low, 0.0, xf).astype(BF)


def _stack_heads(x, scale=1.0):
    halves = [_halves(x[:, g * 128:(g + 1) * 128], scale) for g in range(GROUP)]
    return jnp.concatenate([lo for lo, _ in halves] + [hi for _, hi in halves], axis=0)


def _attn_probs(q, kvc, kvp, bias_ref, sink_ref, has_prev):
    kv2 = jnp.concatenate([kvp, kvc], axis=0)
    kboth, vboth = kv2[:, 0:KV_W], kv2[:, KV_W:2 * KV_W]
    qpad = _stack_heads(q, SCALE)
    s = _dot_nt(qpad, kboth) + bias_ref[...]
    if has_prev is not None:
        col = lax.broadcasted_iota(jnp.int32, (HEAD_ROWS, 2 * BLK), 1)
        s = jnp.where((col >= BLK) | has_prev, s, NEG)
    row_head = lax.broadcasted_iota(jnp.int32, (HEAD_ROWS, 1), 0) // BLK
    sink = jnp.zeros((HEAD_ROWS, 1), F32)
    for h in range(N_HEADS):
        sink = jnp.where(row_head == h, sink_ref[h], sink)
    m = jnp.maximum(jnp.max(s, axis=1, keepdims=True), sink)
    p = jnp.exp(s - m)
    e_sink = jnp.exp(sink - m)
    inv = 1.0 / (jnp.sum(p, axis=1, keepdims=True) + e_sink)
    return qpad, kboth, vboth, p * inv, e_sink * inv


def _attn_fwd(qkv, bias, sinks):
    S = qkv.shape[0]
    tb = min(ATT_TB, S // BLK)
    T = tb * BLK

    def body(sink_ref, q_ref, kv_ref, kvp_ref, bias_ref, o_ref):
        step = pl.program_id(0)
        for j in range(tb):
            rows = slice(j * BLK, (j + 1) * BLK)
            kvp = kvp_ref[...] if j == 0 else kv_ref[(j - 1) * BLK:j * BLK, :]
            has_prev = (step > 0) if j == 0 else None
            _, _, vboth, prob, _ = _attn_probs(q_ref[rows, :], kv_ref[rows, :], kvp, bias_ref, sink_ref, has_prev)
            pb = prob.astype(BF)
            v_low, v_high = _halves(vboth)
            half = HEAD_ROWS // 2
            o = _dot(pb[0:half], v_low) + _dot(pb[half:HEAD_ROWS], v_high)
            for g in range(GROUP):
                o_ref[rows, g * 128:(g + 1) * 128] = o[g * BLK:(g + 1) * BLK].astype(BF)

    return pl.pallas_call(
        body, name="attn_fwd", grid=(S // T,),
        out_shape=_sds((S, Q_W), BF),
        in_specs=[pl.BlockSpec(memory_space=pltpu.SMEM),
                  pl.BlockSpec((T, Q_W), lambda i: (i, 0)),
                  pl.BlockSpec((T, 2 * KV_W), lambda i: (i, 2)),
                  pl.BlockSpec((BLK, 2 * KV_W), lambda i: (jnp.maximum(i * tb - 1, 0), 2)),
                  _const((HEAD_ROWS, 2 * BLK))],
        out_specs=pl.BlockSpec((T, Q_W), lambda i: (i, 0)),
        compiler_params=_cp(1, 32),
    )(sinks, *_in_hbm(qkv, qkv, qkv, bias))


def _attn_bwd(qkv, bias, sinks, do):
    S = qkv.shape[0]
    tb = min(ATT_TB, S // BLK)
    T = tb * BLK
    nt = S // T
    half = HEAD_ROWS // 2

    def body(sink_ref, q_ref, kv_ref, kvp_ref, bias_ref, do_ref, dq_ref, dkv_ref, dbias_ref, dsink_ref, carry):
        i = pl.program_id(0)

        @pl.when(i == 0)
        def _():
            carry[...] = jnp.zeros_like(carry)
            dbias_ref[...] = jnp.zeros_like(dbias_ref)
            dsink_ref[...] = jnp.zeros_like(dsink_ref)

        from_next = carry[...]
        head_row = lax.broadcasted_iota(jnp.int32, (N_HEADS, 128), 0)
        low = lax.broadcasted_iota(jnp.int32, (BLK, 128), 1) < HD
        for j in reversed(range(tb)):
            rows = slice(j * BLK, (j + 1) * BLK)
            kvp = kvp_ref[...] if j == 0 else kv_ref[(j - 1) * BLK:j * BLK, :]
            has_prev = (i < nt - 1) if j == 0 else None
            qpad, kboth, vboth, prob, p_sink = _attn_probs(q_ref[rows, :], kv_ref[rows, :], kvp, bias_ref, sink_ref,
                                                           has_prev)
            pb = prob.astype(BF)
            dopad = _stack_heads(do_ref[rows, :])
            dp = _dot_nt(dopad, vboth)
            delta = jnp.sum(prob * dp, axis=1, keepdims=True)
            ds = prob * (dp - delta)
            dbias_ref[...] += ds
            sink_term = p_sink * delta
            dsink_rows = jnp.zeros((N_HEADS, 128), F32)
            for h in range(N_HEADS):
                val = -jnp.sum(sink_term[h * BLK:(h + 1) * BLK], axis=0, keepdims=True)
                dsink_rows = jnp.where(head_row == h, val, dsink_rows)
            dsink_ref[...] += dsink_rows
            dsb = ds.astype(BF)
            dqpad = _dot(dsb, kboth) * SCALE
            for g in range(GROUP):
                dq_ref[rows, g * 128:(g + 1) * 128] = jnp.where(
                    low, dqpad[g * BLK:(g + 1) * BLK], dqpad[half + g * BLK:half + (g + 1) * BLK]).astype(BF)
            dkv2 = jnp.concatenate([jnp.transpose(_dot_tn(qpad, dsb)),
                                    jnp.transpose(_dot_tn(dopad, pb))], axis=1)
            dkv_ref[rows, :] = (dkv2[BLK:2 * BLK] + from_next).astype(BF)
            from_next = dkv2[0:BLK]
        carry[...] = from_next

    return pl.pallas_call(
        body, name="attn_bwd", grid=(nt,),
        out_shape=[_sds((S, Q_W), BF), _sds((S, 2 * KV_W), BF),
                   _sds((HEAD_ROWS, 2 * BLK), F32), _sds((N_HEADS, 128), F32)],
        in_specs=[pl.BlockSpec(memory_space=pltpu.SMEM),
                  pl.BlockSpec((T, Q_W), lambda i: (nt - 1 - i, 0)),
                  pl.BlockSpec((T, 2 * KV_W), lambda i: (nt - 1 - i, 2)),
                  pl.BlockSpec((BLK, 2 * KV_W), lambda i: (jnp.maximum((nt - 1 - i) * tb - 1, 0), 2)),
                  _const((HEAD_ROWS, 2 * BLK)),
                  pl.BlockSpec((T, Q_W), lambda i: (nt - 1 - i, 0))],
        out_specs=[pl.BlockSpec((T, Q_W), lambda i: (nt - 1 - i, 0)),
                   pl.BlockSpec((T, 2 * KV_W), lambda i: (nt - 1 - i, 0)),
                   _const((HEAD_ROWS, 2 * BLK)), _const((N_HEADS, 128))],
        scratch_shapes=[pltpu.VMEM((BLK, 2 * KV_W), F32)],
        compiler_params=_cp(1, 32),
    )(sinks, *_in_hbm(qkv, qkv, qkv, bias, do))


def _rel_bias_grad(dbias, bucket):
    def body(db_ref, bk_ref, out_ref):
        bk = bk_ref[...]
        lane = lax.broadcasted_iota(jnp.int32, (1, 128), 1)
        for h in range(N_HEADS):
            d = db_ref[h // GROUP, pl.ds((h % GROUP) * BLK, BLK), :]
            row = jnp.zeros((1, 128), F32)
            for b in range(N_BUCKETS):
                tot = jnp.sum(jnp.sum(jnp.where(bk == b, d, 0.0), axis=1, keepdims=True), axis=0, keepdims=True)
                row = jnp.where(lane == b, tot, row)
            out_ref[pl.ds(h, 1), :] = row

    vm = pl.BlockSpec(memory_space=pltpu.VMEM)
    return pl.pallas_call(body, name="rel_bias_grad", out_shape=_sds((N_HEADS, 128), F32),
                          in_specs=[vm, vm], out_specs=vm)(dbias, bucket)


def _gmlp_parts(zg, lg_ref, lb_ref):
    z = zg.astype(F32)
    ge = _gelu(z)
    u, vg = ge[:, 0:G_W], ge[:, G_W:2 * G_W]
    mu = jnp.mean(vg, axis=-1, keepdims=True)
    xc = vg - mu
    rstd = lax.rsqrt(jnp.mean(xc * xc, axis=-1, keepdims=True) + EPS)
    xh = xc * rstd
    return z, u, xh, rstd, xh * lg_ref[...] + lb_ref[...]


def _causal_weights(ws_ref, wc):
    t = lax.broadcasted_iota(jnp.int32, (BLK, BLK), 0)
    s = lax.broadcasted_iota(jnp.int32, (BLK, BLK), 1)
    for g in range(N_HEADS):
        wc[g] = jnp.where(s <= t, ws_ref[g], 0.0).astype(BF)


def _spatial(vb, wc, bst_ref, p, low):
    xp = vb[:, p * 128:(p + 1) * 128]
    s0 = _dot(wc[2 * p], xp) + bst_ref[:, 2 * p:2 * p + 1]
    s1 = _dot(wc[2 * p + 1], xp) + bst_ref[:, 2 * p + 1:2 * p + 2]
    return xp, jnp.where(low, s0, s1)


def _gmlp_fwd(zg, lg, lb, ws, bst):
    S = zg.shape[0]
    tb = min(ATT_TB, S // BLK)
    T = tb * BLK

    def body(zg_ref, lg_ref, lb_ref, ws_ref, bst_ref, o_ref, wc):
        @pl.when(pl.program_id(0) == 0)
        def _():
            _causal_weights(ws_ref, wc)
        low = lax.broadcasted_iota(jnp.int32, (BLK, 128), 1) < HD
        for j in range(tb):
            rows = slice(j * BLK, (j + 1) * BLK)
            _, u, _, _, vln = _gmlp_parts(zg_ref[rows, :], lg_ref, lb_ref)
            vb = vln.astype(BF)
            for p in range(4):
                _, sp = _spatial(vb, wc, bst_ref, p, low)
                o_ref[rows, p * 128:(p + 1) * 128] = (u[:, p * 128:(p + 1) * 128] * sp).astype(BF)

    return pl.pallas_call(
        body, name="gmlp_fwd", grid=(S // T,),
        out_shape=_sds((S, G_W), BF),
        in_specs=[pl.BlockSpec((T, 2 * G_W), lambda i: (i, 0)), _const((1, G_W)), _const((1, G_W)),
                  _const((N_HEADS, BLK, BLK)), _const((BLK, N_HEADS))],
        out_specs=pl.BlockSpec((T, G_W), lambda i: (i, 0)),
        scratch_shapes=[pltpu.VMEM((N_HEADS, BLK, BLK), BF)],
        compiler_params=_cp(1, 32),
    )(*_in_hbm(zg), lg, lb, ws, bst)


def _gmlp_bwd(zg, d_out, lg, lb, ws, bst):
    S = zg.shape[0]
    tb = min(ATT_TB, S // BLK)
    T = tb * BLK
    nb = S // T

    def body(zg_ref, d_ref, lg_ref, lb_ref, ws_ref, bst_ref, dzg_ref, dws_ref, dbs_ref, dlg_ref, dlb_ref, wc, dbacc):
        i = pl.program_id(0)

        @pl.when(i == 0)
        def _():
            _causal_weights(ws_ref, wc)
            dws_ref[...] = jnp.zeros_like(dws_ref)
            dlg_ref[...] = jnp.zeros_like(dlg_ref)
            dlb_ref[...] = jnp.zeros_like(dlb_ref)
            dbacc[...] = jnp.zeros_like(dbacc)

        low = lax.broadcasted_iota(jnp.int32, (BLK, 128), 1) < HD
        for j in range(tb):
            rows = slice(j * BLK, (j + 1) * BLK)
            z, u, xh, rstd, vln = _gmlp_parts(zg_ref[rows, :], lg_ref, lb_ref)
            vb = vln.astype(BF)
            d = d_ref[rows, :].astype(F32)
            du_parts, dvln_parts = [], []
            for p in range(4):
                xp, sp = _spatial(vb, wc, bst_ref, p, low)
                dp = d[:, p * 128:(p + 1) * 128]
                du_parts.append(dp * sp)
                dsp = dp * u[:, p * 128:(p + 1) * 128]
                dbacc[:, p * 128:(p + 1) * 128] += dsp
                d0 = jnp.where(low, dsp, 0.0).astype(BF)
                d1 = jnp.where(low, 0.0, dsp).astype(BF)
                dws_ref[2 * p] += _dot_nt(d0, xp)
                dws_ref[2 * p + 1] += _dot_nt(d1, xp)
                dvln_parts.append(_dot_tn(wc[2 * p], d0) + _dot_tn(wc[2 * p + 1], d1))
            dvln = jnp.concatenate(dvln_parts, axis=1)
            dlg_ref[...] += _colsum(dvln * xh)
            dlb_ref[...] += _colsum(dvln)
            dxh = dvln * lg_ref[...]
            dvg = rstd * (dxh - jnp.mean(dxh, axis=-1, keepdims=True)
                          - xh * jnp.mean(dxh * xh, axis=-1, keepdims=True))
            dge = jnp.concatenate(du_parts + [dvg], axis=1)
            dzg_ref[rows, :] = (dge * _gelu_grad(z)).astype(BF)

        @pl.when(i == nb - 1)
        def _():
            t = lax.broadcasted_iota(jnp.int32, (BLK, BLK), 0)
            s = lax.broadcasted_iota(jnp.int32, (BLK, BLK), 1)
            for g in range(N_HEADS):
                dws_ref[g] = jnp.where(s <= t, dws_ref[g], 0.0)
            grp = lax.broadcasted_iota(jnp.int32, (N_HEADS, G_W), 0)
            lane = lax.broadcasted_iota(jnp.int32, (N_HEADS, G_W), 1) // HD
            pick = jnp.where(grp == lane, 1.0, 0.0).astype(F32)
            dbs_ref[...] = lax.dot_general(pick, dbacc[...], (((1,), (1,)), ((), ())),
                                           preferred_element_type=F32, precision=HIGH)

    return pl.pallas_call(
        body, name="gmlp_bwd", grid=(nb,),
        out_shape=[_sds((S, 2 * G_W), BF), _sds((N_HEADS, BLK, BLK), F32), _sds((N_HEADS, BLK), F32),
                   _sds((1, G_W), F32), _sds((1, G_W), F32)],
        in_specs=[pl.BlockSpec((T, 2 * G_W), lambda i: (i, 0)), pl.BlockSpec((T, G_W), lambda i: (i, 0)),
                  _const((1, G_W)), _const((1, G_W)), _const((N_HEADS, BLK, BLK)), _const((BLK, N_HEADS))],
        out_specs=[pl.BlockSpec((T, 2 * G_W), lambda i: (i, 0)), _const((N_HEADS, BLK, BLK)),
                   _const((N_HEADS, BLK)), _const((1, G_W)), _const((1, G_W))],
        scratch_shapes=[pltpu.VMEM((N_HEADS, BLK, BLK), BF), pltpu.VMEM((BLK, G_W), F32)],
        compiler_params=_cp(1, 32),
    )(*_in_hbm(zg, d_out), lg, lb, ws, bst)


def _mix_out(o, gm, gates, h, wa, wg, wo, gate, gp, after=()):
    S = h.shape[0]
    R = min(512, S)

    def body(o_ref, gm_ref, gates_ref, h_ref, wa_ref, wg_ref, wo_ref, gate_ref, gp_ref,
             ya_ref, yg_ref, ym_ref, y_ref, hn_ref):
        for r0 in range(0, R, CHUNK):
            rows = slice(r0, r0 + CHUNK)
            ya = _dot(o_ref[rows, :], wa_ref[...])
            yg = _dot(gm_ref[rows, :], wg_ref[...])
            ya_ref[rows, :] = ya.astype(BF)
            yg_ref[rows, :] = yg.astype(BF)
            ym = (gates_ref[rows, 0:D].astype(F32) * ya + gates_ref[rows, D:2 * D].astype(F32) * yg).astype(BF)
            ym_ref[rows, :] = ym
            y = _dot(ym, wo_ref[...])
            y_ref[rows, :] = y.astype(BF)
            hn_ref[rows, :] = h_ref[rows, :] + gate_ref[...] * (y * _rms_r(y) * gp_ref[...])

    vec = _const((1, D))
    rows = lambda w_: pl.BlockSpec((R, w_), lambda i: (i, 0))
    body, after_specs = _behind(body, 9, after)
    return pl.pallas_call(
        body, name="mix_out", grid=(S // R,),
        out_shape=[_sds((S, D), BF)] * 4 + [_sds((S, D), F32)],
        in_specs=[rows(Q_W), rows(G_W), rows(2 * D), rows(D), _resident((Q_W, D)), _resident((G_W, D)),
                  _resident((D, D)), vec, vec] + after_specs,
        out_specs=[rows(D)] * 5,
        compiler_params=_cp(1, 48),
    )(*_in_hbm(o, gm, gates, h, wa, wg, wo), gate, gp, *after)


def _mix_out_bwd(dh, y, ya, yg, gates, att, gm, ymix, wa, wg, wo, gate, gp, after=()):
    S = dh.shape[0]
    R = min(512, S)
    nb = S // R

    def body(dh_ref, y_ref, ya_ref, yg_ref, gates_ref, att_ref, gm_ref, ym_ref, wa_ref, wg_ref, wo_ref,
             gate_ref, gp_ref, dz_ref, do_ref, dgm_ref, dgate_ref, dgp_ref, gwo_ref, gwa_ref, gwg_ref,
             acc_o, acc_a, acc_g, dy_scr, dya_scr, dyg_scr):
        i = pl.program_id(0)

        @pl.when(i == 0)
        def _():
            for r in (dgate_ref, dgp_ref, acc_o, acc_a, acc_g):
                r[...] = jnp.zeros_like(r)
        for r0 in range(0, R, 2 * CHUNK):
            rows = slice(r0, min(r0 + 2 * CHUNK, R))
            dy, dgate, dgp = _postnorm_bwd(dh_ref[rows, :], y_ref[rows, :], gate_ref[...], gp_ref[...], 1.0)
            dgate_ref[...] += dgate
            dgp_ref[...] += dgp
            dyb = dy.astype(BF)
            dy_scr[rows, :] = dyb
            dym = _dot_nt(dyb, wo_ref[...])
            ga = gates_ref[rows, 0:D].astype(F32)
            gg = gates_ref[rows, D:2 * D].astype(F32)
            dya = (dym * ga).astype(BF)
            dyg = (dym * gg).astype(BF)
            dya_scr[rows, :] = dya
            dyg_scr[rows, :] = dyg
            dz_ref[rows, 0:D] = (dym * ya_ref[rows, :].astype(F32) * (ga * (1.0 - ga))).astype(BF)
            dz_ref[rows, D:2 * D] = (dym * yg_ref[rows, :].astype(F32) * (gg * (1.0 - gg))).astype(BF)
            do_ref[rows, :] = _dot_nt(dya, wa_ref[...]).astype(BF)
            dgm_ref[rows, :] = _dot_nt(dyg, wg_ref[...]).astype(BF)
        for m0 in range(0, D, CHUNK):
            acc_o[m0:m0 + CHUNK, :] += _dot_tn(ym_ref[:, m0:m0 + CHUNK], dy_scr[...])
        for m0 in range(0, Q_W, CHUNK):
            acc_a[m0:m0 + CHUNK, :] += _dot_tn(att_ref[:, m0:m0 + CHUNK], dya_scr[...])
            acc_g[m0:m0 + CHUNK, :] += _dot_tn(gm_ref[:, m0:m0 + CHUNK], dyg_scr[...])

        @pl.when(i == nb - 1)
        def _():
            for m0 in range(0, D, CHUNK):
                gwo_ref[m0:m0 + CHUNK, :] = acc_o[m0:m0 + CHUNK, :].astype(BF)
            for m0 in range(0, Q_W, CHUNK):
                gwa_ref[m0:m0 + CHUNK, :] = acc_a[m0:m0 + CHUNK, :].astype(BF)
                gwg_ref[m0:m0 + CHUNK, :] = acc_g[m0:m0 + CHUNK, :].astype(BF)

    vec = _const((1, D))
    rows = lambda w_: pl.BlockSpec((R, w_), lambda i: (i, 0))
    body, after_specs = _behind(body, 13, after)
    return pl.pallas_call(
        body, name="mix_out_bwd", grid=(nb,),
        out_shape=[_sds((S, 2 * D), BF), _sds((S, Q_W), BF), _sds((S, G_W), BF), _sds((1, D), F32),
                   _sds((1, D), F32), _sds((D, D), BF), _sds((Q_W, D), BF), _sds((G_W, D), BF)],
        in_specs=[rows(D), rows(D), rows(D), rows(D), rows(2 * D), rows(Q_W), rows(G_W), rows(D),
                  _resident((Q_W, D)), _resident((G_W, D)), _resident((D, D)), vec, vec] + after_specs,
        out_specs=[rows(2 * D), rows(Q_W), rows(G_W), vec, vec, _const((D, D)), _const((Q_W, D)),
                   _const((G_W, D))],
        scratch_shapes=[pltpu.VMEM((D, D), F32), pltpu.VMEM((Q_W, D), F32), pltpu.VMEM((G_W, D), F32)]
        + [pltpu.VMEM((R, D), BF)] * 3,
        compiler_params=_cp(1, 60),
    )(*_in_hbm(dh, y, ya, yg, gates, att, gm, ymix, wa, wg, wo), gate, gp, *after)


def _mix_dn(dq, dkv, dzg, dzgate, w, wq, h, dh, sc, gp, after=()):
    S = h.shape[0]
    R = min(512, S)

    def body(dq_ref, dkv_ref, dzg_ref, dzt_ref, w_ref, wq_ref, h_ref, dh_ref, sc_ref, gp_ref,
             out_ref, dsh_ref, dsc_ref, dgp_ref):
        @pl.when(pl.program_id(0) == 0)
        def _():
            dsh_ref[...] = jnp.zeros_like(dsh_ref)
            dsc_ref[...] = jnp.zeros_like(dsc_ref)
            dgp_ref[...] = jnp.zeros_like(dgp_ref)
        for r0 in range(0, R, CHUNK):
            rows = slice(r0, r0 + CHUNK)
            dn = _dot(dq_ref[rows, :], wq_ref[...])
            dn = dn + _dot(dkv_ref[rows, :], w_ref[Q_W:QKV_W, :])
            dn = dn + _dot(dzg_ref[rows, :], w_ref[ZG_OFF:GATE_OFF, :])
            dn = dn + _dot(dzt_ref[rows, :], w_ref[GATE_OFF:IN_W, :])
            dx, dsh, dsc, dgp = _prenorm_bwd(dn, h_ref[rows, :], gp_ref[...], sc_ref[...])
            out_ref[rows, :] = dh_ref[rows, :] + dx
            dsh_ref[...] += dsh
            dsc_ref[...] += dsc
            dgp_ref[...] += dgp

    vec = _const((1, D))
    rows = lambda w_: pl.BlockSpec((R, w_), lambda i: (i, 0))
    body, after_specs = _behind(body, 10, after)
    return pl.pallas_call(
        body, name="mix_dn", grid=(S // R,),
        out_shape=[_sds((S, D), F32)] + [_sds((1, D), F32)] * 3,
        in_specs=[rows(Q_W), rows(2 * KV_W), rows(2 * G_W), rows(2 * D), _resident((IN_W, D)),
                  _resident((Q_W, D)), rows(D), rows(D), vec, vec] + after_specs,
        out_specs=[rows(D), vec, vec, vec],
        compiler_params=_cp(1, 48),
    )(*_in_hbm(dq, dkv, dzg, dzgate, w, wq, h, dh), sc, gp, *after)


def _adamw_math(w, g, m, v):
    m2 = ADAM_B1 * m + (1.0 - ADAM_B1) * g
    v2 = ADAM_B2 * v + (1.0 - ADAM_B2) * (g * g)
    m_hat = m2 / (1.0 - ADAM_B1 ** ADAM_STEP)
    v_hat = v2 / (1.0 - ADAM_B2 ** ADAM_STEP)
    delta = -ADAM_LR * (m_hat / (jnp.sqrt(v_hat) + ADAM_EPS) + ADAM_WD * w)
    return delta, m2, v2


def _row_tile(rows, cols):
    best = None
    for t in range(16, rows + 1, 16):
        if rows % t == 0 and t * cols <= 256 * 1024:
            best = t
    return best if best is not None else rows


def _adamw_sharded(landing, w, m, v, name):
    r, c = w.shape
    tr = _row_tile(r, c)

    def body(l_ref, w_ref, m_ref, v_ref, g_ref, d_ref, m2_ref, v2_ref):
        g = l_ref[0].astype(F32)
        for j in range(1, N_DEV):
            g = g + l_ref[j].astype(F32)
        delta, m2, v2 = _adamw_math(w_ref[...], g, m_ref[...], v_ref[...])
        g_ref[...] = g
        d_ref[...] = delta
        m2_ref[...] = m2
        v2_ref[...] = v2

    row = pl.BlockSpec((tr, c), lambda i: (i, 0))
    return pl.pallas_call(
        body, name=name, grid=(r // tr,),
        out_shape=[_sds((r, c), F32)] * 4,
        in_specs=[pl.BlockSpec((N_DEV, tr, c), lambda i: (0, i, 0)), row, row, row],
        out_specs=[row] * 4,
        compiler_params=_cp(1, 48),
    )(*_in_hbm(landing, w, m, v))


def _adamw_small(items):
    n = len(items)

    def body(*refs):
        for k in range(n):
            w_ref, g_ref, m_ref, v_ref = refs[4 * k:4 * k + 4]
            outs = refs[4 * n + 3 * k:4 * n + 3 * k + 3]
            for o_ref, val in zip(outs, _adamw_math(w_ref[...], g_ref[...], m_ref[...], v_ref[...])):
                o_ref[...] = val

    vm = pl.BlockSpec(memory_space=pltpu.VMEM)
    flat = pl.pallas_call(
        body, name="adamw_small",
        out_shape=[_sds(it[0].shape, F32) for it in items for _ in range(3)],
        in_specs=[vm] * (4 * n), out_specs=[vm] * (3 * n),
    )(*[a for it in items for a in it])
    return [tuple(flat[3 * k:3 * k + 3]) for k in range(n)]


def _w_ada_update(c8, d_ada, w, m, v):
    tr = 256

    def body(c_ref, d_ref, w_ref, m_ref, v_ref, g_ref, dl_ref, m2_ref, v2_ref):
        cs = c_ref[...]
        cs = cs * jax.nn.sigmoid(cs)
        g = lax.dot_general(cs, d_ref[...], (((0,), (0,)), ((), ())), preferred_element_type=F32, precision=HIGH)
        delta, m2, v2 = _adamw_math(w_ref[...], g, m_ref[...], v_ref[...])
        g_ref[...] = g
        dl_ref[...] = delta
        m2_ref[...] = m2
        v2_ref[...] = v2

    row = pl.BlockSpec((tr, ADA_W), lambda i: (i, 0))
    return pl.pallas_call(
        body, name="w_ada_update", grid=(D // tr,),
        out_shape=[_sds((D, ADA_W), F32)] * 4,
        in_specs=[pl.BlockSpec((N_DEV, tr), lambda i: (0, i)), _const((N_DEV, ADA_W)), row, row, row],
        out_specs=[row] * 4,
        compiler_params=_cp(1, 40),
    )(c8, d_ada, *_in_hbm(w, m, v))


def _t5_bucket():
    qi = np.arange(BLK, dtype=np.int32)[:, None]
    kj = np.arange(2 * BLK, dtype=np.int32)[None, :]
    dist = np.maximum(qi + BLK - kj, 0)
    max_exact = N_BUCKETS // 2
    d_f = np.maximum(dist, max_exact).astype(np.float32)
    large = max_exact + (np.log(d_f / np.float32(max_exact)) / np.float32(math.log(MAX_DISTANCE / max_exact))
                         * np.float32(N_BUCKETS - max_exact)).astype(np.int32)
    large = np.minimum(large, N_BUCKETS - 1)
    return jnp.asarray(np.where(dist < max_exact, dist, large).astype(np.int32))


def _slabs_of_columns(w):
    r, c8 = w.shape
    return jnp.transpose(w.reshape(r, N_DEV, c8 // N_DEV), (1, 0, 2))


def _columns_of_slabs(w8):
    _, r, c = w8.shape
    return jnp.transpose(w8, (1, 0, 2)).reshape(r, N_DEV * c)


def kernel(x, c, rel_bias, w_ada, b_ada, pre_norm_g, post_norm_g, w_ffn1_in, w_ffn1_out, w_in, sinks, gmlp_ln_g, gmlp_ln_b, gmlp_w_s, gmlp_b_s, w_br_attn, w_br_gmlp, w_out, w_ffn2_in, w_ffn2_out, loss_target, m_rel_bias, m_w_ada, m_b_ada, m_pre_norm_g, m_post_norm_g, m_w_ffn1_in, m_w_ffn1_out, m_w_in, m_sinks, m_gmlp_ln_g, m_gmlp_ln_b, m_gmlp_w_s, m_gmlp_b_s, m_w_br_attn, m_w_br_gmlp, m_w_out, m_w_ffn2_in, m_w_ffn2_out, v_rel_bias, v_w_ada, v_b_ada, v_pre_norm_g, v_post_norm_g, v_w_ffn1_in, v_w_ffn1_out, v_w_in, v_sinks, v_gmlp_ln_g, v_gmlp_ln_b, v_gmlp_w_s, v_gmlp_b_s, v_w_br_attn, v_w_br_gmlp, v_w_out, v_w_ffn2_in, v_w_ffn2_out):
    me = 4 * lax.axis_index("x") + 2 * lax.axis_index("y") + lax.axis_index("c")
    x0 = x[0]
    target = loss_target[0]

    transposed = ("w_ffn1_in", "w_in", "w_ffn2_in")
    shards = [w_ffn1_in[0].T, w_ffn1_out[0], w_in[0].T, w_br_attn[0], w_br_gmlp[0], w_out[0],
              w_ffn2_in[0].T, w_ffn2_out[0]]
    shards_bf = [s.astype(BF) for s in shards]
    groups = [shards_bf[0:1], shards_bf[1:6], shards_bf[6:8]]

    def gather_start(i, after):
        return _slabs_start("gather", groups[i], after, "gather_start_%d" % i)

    def forward_start(st, i, after):
        lands = _slabs_wait("gather", len(groups[i]), st, after, "gather_wait_%d" % i)
        return _slabs_start("forward", lands, c, "forward_start_%d" % i)

    def gathered(st, i, after):
        return _slabs_wait("forward", len(groups[i]), st, after, "forward_wait_%d" % i)

    gs0 = gather_start(0, c)

    mine = jnp.concatenate([c[0], pre_norm_g[0].reshape(-1), post_norm_g[0].reshape(-1)])
    small8 = jnp.broadcast_to(mine[None, :], (8, mine.shape[0]))
    b_ada64 = jnp.repeat(b_ada.reshape(N_DEV, ADA_W), 8, axis=0)
    gath, ada64 = _ada_forward(small8, w_ada[0], b_ada64)
    gath8 = gath[::8]
    ada = ada64[::8].reshape(9, D)
    sh1, sc1, g1, sh2, sc2, g2, sh3, sc3, g3 = [ada[k:k + 1] for k in range(9)]
    gains = gath8[:, D:].reshape(N_DEV, 2, 3, 128)
    pre_g = jnp.transpose(gains[:, 0], (1, 0, 2)).reshape(3, D)
    post_g = jnp.transpose(gains[:, 1], (1, 0, 2)).reshape(3, D)
    pre = [pre_g[k:k + 1] for k in range(3)]
    post = [post_g[k:k + 1] for k in range(3)]

    bucket = _t5_bucket()
    bias = _bias_table(rel_bias, bucket).reshape(HEAD_ROWS, 2 * BLK)
    sinks8 = sinks[0]
    lg, lb = gmlp_ln_g, gmlp_ln_b
    ws = gmlp_w_s[0]
    bst = jnp.transpose(gmlp_b_s[0])

    fs0 = forward_start(gs0, 0, sh1)
    gs1 = gather_start(1, fs0[-1])
    wf1_in = gathered(fs0, 0, gs1[-1])[0].reshape(2 * D_FF, D)
    n1, fg1, fu1, fa1 = _ffn_in(x0, sh1, sc1, pre[0], wf1_in, "ffn1_in")
    fs1 = forward_start(gs1, 1, n1)
    gs2 = gather_start(2, fs1[-1])
    mix_w = gathered(fs1, 1, gs2[-1])
    wf1_out = mix_w[0].reshape(D_FF, D)
    w_in_full = mix_w[1].reshape(IN_W, D)
    w_q = _pair_heads(w_in_full[0:Q_W])
    w_bra = _pair_heads(_columns_of_slabs(mix_w[2]))
    w_brg = _columns_of_slabs(mix_w[3])
    w_out_full = mix_w[4].reshape(D, D)
    h1, y1 = _ffn_out(fa1, wf1_out, x0, g1, post[0], "ffn1_out")
    n2, qkv, zg, gates = _mix_in(h1, sh2, sc2, pre[1], w_in_full, w_q)
    att = _attn_fwd(qkv, bias, sinks8)
    gm = _gmlp_fwd(zg, lg, lb, ws, bst)
    fs2 = forward_start(gs2, 2, gm)
    ya, yg, ymix, y2, h2 = _mix_out(att, gm, gates, h1, w_bra, w_brg, w_out_full, g2, post[1], after=(fs2[-1],))
    wf2_in, wf2_out = gathered(fs2, 2, h2)
    wf2_in = wf2_in.reshape(2 * D_FF, D)
    wf2_out = wf2_out.reshape(D_FF, D)
    n3, fg3, fu3, fa3 = _ffn_in(h2, sh3, sc3, pre[2], wf2_in, "ffn2_in")
    dh3, y3, sq = _ffn_out(fa3, wf2_out, h2, g3, post[2], "ffn2_out", target=target)

    def exchange_start(i, arrays):
        return _slabs_start("exchange", arrays, sq, "exchange_start_%d" % i)

    dy3, dgu3, dh2, d_g3, d_post2, d_sh3, d_sc3, d_pre2 = _ffn_bwd(
        dh3, y3, fg3, fu3, wf2_out, wf2_in, h2, g3, post[2], sc3, pre[2], "ffn2_bwd")
    gw_f2_out = _tn_matmul(fa3, dy3, "ffn2_out_wgrad", tm=D_FF // 2).reshape(N_DEV, D_FF // N_DEV, D)
    gw_f2_in = _tn_matmul(dgu3, n3, "ffn2_in_wgrad", tm=D_FF // 2).reshape(N_DEV, FS, D)
    ex1 = exchange_start(1, [gw_f2_out, gw_f2_in])

    dzgate, d_att, d_gm, d_g2, d_post1, gw_out, gw_bra, gw_brg = _mix_out_bwd(
        dh2, y2, ya, yg, gates, att, gm, ymix, w_bra, w_brg, w_out_full, g2, post[1], after=(ex1[-1],))
    ex2 = exchange_start(2, [_slabs_of_columns(_unpair_heads(gw_bra)), _slabs_of_columns(gw_brg),
                             gw_out.reshape(N_DEV, D // N_DEV, D)])
    dq, dkv, dbias, dsink = _attn_bwd(qkv, bias, sinks8, d_att)
    dzg, d_ws, d_bs, d_lg, d_lb = _gmlp_bwd(zg, d_gm, lg, lb, ws, bst)
    d_rel = _rel_bias_grad(dbias.reshape(N_KV, GROUP * BLK, 2 * BLK), bucket)
    early = jnp.concatenate([
        jnp.concatenate([d_lg.reshape(4, 128), d_lb.reshape(4, 128)], axis=0),
        d_bs, d_rel, dsink, d_ws.reshape(N_HEADS * BLK, BLK)], axis=0)
    sm0 = _slabs_start("gather_all", [early], sq, "small_gather_start")
    dh1, d_sh2, d_sc2, d_pre1 = _mix_dn(dq, dkv, dzg, dzgate, w_in_full, w_q, h1, dh2, sc2, pre[1],
                                        after=(ex2[-1], sm0[-1]))
    gw_in = jnp.concatenate(
        [_unpair_heads(_tn_matmul(dq, n2, "w_in_q_wgrad")), _tn_matmul(dkv, n2, "w_in_kv_wgrad"),
         _tn_matmul(dzg, n2, "w_in_zg_wgrad"), _tn_matmul(dzgate, n2, "w_in_gate_wgrad")],
        axis=0).reshape(N_DEV, IN_W // N_DEV, D)
    ex3 = exchange_start(3, [gw_in])

    dy1, dgu1, d_g1, d_post0 = _ffn_out_bwd(dh1, y1, fg1, fu1, wf1_out, g1, post[0], "ffn1_out_bwd",
                                            after=(ex3[-1],))
    gw_f1_out = _tn_matmul(fa1, dy1, "ffn1_out_wgrad", tm=D_FF // 2).reshape(N_DEV, D_FF // N_DEV, D)
    ex4 = exchange_start(4, [gw_f1_out])
    gw_f1_in = _tn_matmul(dgu1, n1, "ffn1_in_wgrad", tm=D_FF // 2).reshape(N_DEV, FS, D)
    ex5 = exchange_start(5, [gw_f1_in])
    grad_x, d_sh1, d_sc1, d_pre0 = _ffn_dn(dgu1, wf1_in, x0, dh1, sc1, pre[0], "ffn1_dn", after=(ex4[-1], ex5[-1]))

    landed = {}
    for i, (ex, nms) in enumerate([(ex1, ["w_ffn2_out", "w_ffn2_in"]),
                                   (ex2, ["w_br_attn", "w_br_gmlp", "w_out"]), (ex3, ["w_in"]),
                                   (ex4, ["w_ffn1_out"]), (ex5, ["w_ffn1_in"])]):
        for nm, land in zip(nms, _slabs_wait("exchange", len(nms), ex, grad_x, "exchange_wait_%d" % i)):
            landed[nm] = land
    moments = [(m_w_ffn1_in, v_w_ffn1_in), (m_w_ffn1_out, v_w_ffn1_out), (m_w_in, v_w_in),
               (m_w_br_attn, v_w_br_attn), (m_w_br_gmlp, v_w_br_gmlp), (m_w_out, v_w_out),
               (m_w_ffn2_in, v_w_ffn2_in), (m_w_ffn2_out, v_w_ffn2_out)]
    names = ["w_ffn1_in", "w_ffn1_out", "w_in", "w_br_attn", "w_br_gmlp", "w_out", "w_ffn2_in", "w_ffn2_out"]
    big = {}
    for nm, w_, (m_, v_) in zip(names, shards, moments):
        if nm in transposed:
            res4 = _adamw_sharded(landed[nm], w_, m_[0].T, v_[0].T, "adamw_" + nm)
            big[nm] = [a.T[None] for a in res4]
        else:
            big[nm] = [a[None] for a in _adamw_sharded(landed[nm], w_, m_[0], v_[0], "adamw_" + nm)]

    my_loss = jnp.broadcast_to(sq * (0.5 / D), (1, D))
    my_loss, _ = lax.optimization_barrier((my_loss, landed["w_ffn1_in"]))
    tot, every = _small_allreduce([d_sh1, d_sc1, d_g1, d_sh2, d_sc2, d_g2, d_sh3, d_sc3, d_g3,
                                   d_pre0, d_pre1, d_pre2, d_post0, d_post1, d_post2, my_loss])
    (early_land,) = _slabs_wait("gather_all", 1, sm0, grad_x, "small_gather_wait")
    tot_early = _sum_slabs(early_land)

    loss = tot[15, 0]
    g_b_ada = tot[0:9].reshape(1, 9 * D)
    g_pre = lax.dynamic_slice_in_dim(tot[9:12], 128 * me, 128, axis=1)[None]
    g_post = lax.dynamic_slice_in_dim(tot[12:15], 128 * me, 128, axis=1)[None]
    g_lg = tot_early[0:4].reshape(1, G_W)
    g_lb = tot_early[4:8].reshape(1, G_W)
    g_bs = tot_early[8:16][None]
    g_rel = jnp.transpose(tot_early[16:24, 0:N_BUCKETS])
    g_sinks = tot_early[24:32, 0][None]
    g_ws = tot_early[32:1056].reshape(1, N_HEADS, BLK, BLK)

    d_ada_mine = lax.dynamic_slice_in_dim(every[:, 0:9].reshape(N_DEV, 9 * D), ADA_W * me, ADA_W, axis=1)
    ada_out = [a[None] for a in _w_ada_update(gath8[:, 0:D], d_ada_mine, w_ada[0], m_w_ada[0], v_w_ada[0])]

    small = [("rel_bias", rel_bias, g_rel, m_rel_bias, v_rel_bias), ("b_ada", b_ada, g_b_ada, m_b_ada, v_b_ada),
             ("pre_norm_g", pre_norm_g, g_pre, m_pre_norm_g, v_pre_norm_g),
             ("post_norm_g", post_norm_g, g_post, m_post_norm_g, v_post_norm_g),
             ("sinks", sinks, g_sinks, m_sinks, v_sinks), ("gmlp_ln_g", gmlp_ln_g, g_lg, m_gmlp_ln_g, v_gmlp_ln_g),
             ("gmlp_ln_b", gmlp_ln_b, g_lb, m_gmlp_ln_b, v_gmlp_ln_b),
             ("gmlp_w_s", gmlp_w_s, g_ws, m_gmlp_w_s, v_gmlp_w_s), ("gmlp_b_s", gmlp_b_s, g_bs, m_gmlp_b_s, v_gmlp_b_s)]
    two_d = lambda a: a.reshape(int(math.prod(a.shape[:-1])), a.shape[-1])
    stepped = _adamw_small([tuple(two_d(a) for a in item[1:]) for item in small])
    res = {"w_ada": ada_out}
    for (nm, w_, g_, _, _), new in zip(small, stepped):
        res[nm] = [g_] + [a.reshape(w_.shape) for a in new]
    res.update(big)
    order = ["rel_bias", "w_ada", "b_ada", "pre_norm_g", "post_norm_g", "w_ffn1_in", "w_ffn1_out", "w_in", "sinks",
             "gmlp_ln_g", "gmlp_ln_b", "gmlp_w_s", "gmlp_b_s", "w_br_attn", "w_br_gmlp", "w_out", "w_ffn2_in",
             "w_ffn2_out"]
    outs = [loss, grad_x[None]]
    for k in range(4):
        outs += [res[nm][k] for nm in order]
    return tuple(outs)
```

```python
import math

import jax
import jax.numpy as jnp
import numpy as np
from jax import lax
from jax.experimental import pallas as pl
from jax.experimental.pallas import tpu as pltpu

F32 = jnp.float32
BF = jnp.bfloat16

N_DEV = 8
D = 1024
D_FF = 2816
FS = D_FF // 4
N_HEADS = 8
N_KV = 2
GROUP = 4
HD = 64
BLK = 128
Q_W = 512
KV_W = 128
G_W = 512
QKV_W = Q_W + 2 * KV_W
ZG_OFF = QKV_W
GATE_OFF = ZG_OFF + 2 * G_W
IN_W = GATE_OFF + 2 * D
N_BUCKETS = 32
MAX_DISTANCE = 128
EPS = 1e-6
NEG = -1e30
SCALE = HD ** -0.5
ADA_W = 9 * D // N_DEV

ADAM_LR = 0.001
ADAM_B1 = 0.9
ADAM_B2 = 0.999
ADAM_EPS = 1e-08
ADAM_WD = 0.01
ADAM_STEP = 10

CHUNK = 256
MIB = 1024 * 1024
MESH = pl.DeviceIdType.MESH
HIGH = lax.Precision.HIGHEST


def _cp(n_grid, vmem_mib):
    return pltpu.CompilerParams(dimension_semantics=("arbitrary",) * n_grid,
                                vmem_limit_bytes=vmem_mib * MIB)


def _const(shape):
    return pl.BlockSpec(shape, lambda *_: (0,) * len(shape))


def _resident(shape):
    return pl.BlockSpec(shape, lambda *_: (0,) * len(shape), pipeline_mode=pl.Buffered(1))


def _behind(body, n_in, after):
    k = len(after)
    return (lambda *refs: body(*refs[:n_in], *refs[n_in + k:])), [pl.BlockSpec(memory_space=pl.ANY)] * k


def _in_hbm(*arrays):
    return [pltpu.with_memory_space_constraint(a, pltpu.HBM) for a in arrays]


def _sds(shape, dtype):
    return jax.ShapeDtypeStruct(shape, dtype)


def _dot(a, b):
    return jnp.dot(a, b, preferred_element_type=F32)


def _dot_nt(a, b):
    return lax.dot_general(a, b, (((1,), (1,)), ((), ())), preferred_element_type=F32)


def _dot_tn(a, b):
    return lax.dot_general(a, b, (((0,), (0,)), ((), ())), preferred_element_type=F32)


def _rms_r(x):
    return lax.rsqrt(jnp.mean(x * x, axis=-1, keepdims=True) + EPS)


def _colsum(x):
    return jnp.sum(x, axis=0, keepdims=True)


def _prenorm(x, gp, sc, sh):
    return (x * _rms_r(x) * gp) * (1.0 + sc) + sh


def _prenorm_bwd(dn, x, gp, sc):
    r = _rms_r(x)
    xh = x * r
    t = dn * (1.0 + sc) * gp
    dx = r * (t - xh * jnp.mean(t * xh, axis=-1, keepdims=True))
    return dx, _colsum(dn), _colsum(dn * xh * gp), _colsum(dn * (1.0 + sc) * xh)


def _postnorm_bwd(dh, y, gate, gp, res):
    y = y.astype(F32)
    r = _rms_r(y)
    yh = y * r
    dyn = (res * gate) * dh
    t = dyn * gp
    dy = r * (t - yh * jnp.mean(t * yh, axis=-1, keepdims=True))
    return dy, _colsum(res * dh * yh * gp), _colsum(dyn * yh)


def _gelu(x):
    k = math.sqrt(2.0 / math.pi)
    return 0.5 * x * (1.0 + jnp.tanh(k * (x + 0.044715 * x * x * x)))


def _gelu_grad(x):
    k = math.sqrt(2.0 / math.pi)
    t = jnp.tanh(k * (x + 0.044715 * x * x * x))
    return 0.5 * (1.0 + t) + 0.5 * x * (1.0 - t * t) * (k * (1.0 + 3.0 * 0.044715 * x * x))


def _my_place():
    x, y, c = lax.axis_index("x"), lax.axis_index("y"), lax.axis_index("c")
    return x, y, c, 4 * x + 2 * y + c


def _peer(x, y, c, k):
    px = 1 - x if k & 4 else x
    py = 1 - y if k & 2 else y
    pc = 1 - c if k & 1 else c
    return (px, py, pc), 4 * px + 2 * py + pc


HBM_SPEC = pl.BlockSpec(memory_space=pltpu.HBM)
SEM_SPEC = pl.BlockSpec(memory_space=pltpu.SEMAPHORE)
EFFECT = pltpu.SideEffectType.DATAFLOW_SIDE_EFFECTING


RELATIONS = {"exchange": (1, 2, 3, 4, 5, 6, 7), "gather": (1, 2, 4, 6), "forward": (2, 4, 6),
             "gather_all": (1, 2, 3, 4, 5, 6, 7)}


def _slab_copies(mode, srcs, lands, send, recv, loc):
    x, y, c, me = _my_place()
    rel = RELATIONS[mode]
    remote, local = [], []
    for t in range(len(lands)):
        for i, k in enumerate(rel):
            peer, peer_lin = _peer(x, y, c, k)
            if mode == "exchange":
                src, dst, to = srcs[t].at[peer_lin], lands[t].at[me], peer
            elif mode in ("gather", "gather_all"):
                src, dst, to = srcs[t], lands[t].at[me], peer
            else:
                src, dst, to = lands[t].at[peer_lin], lands[t].at[peer_lin], _peer(x, y, c, 1)[0]
            remote.append(pltpu.make_async_remote_copy(
                src_ref=src, dst_ref=dst, send_sem=send.at[t * len(rel) + i], recv_sem=recv.at[t * len(rel) + i],
                device_id=to, device_id_type=MESH))
        if mode == "exchange":
            local.append(pltpu.make_async_copy(srcs[t].at[me], lands[t].at[me], loc.at[t]))
        elif mode in ("gather", "gather_all"):
            local.append(pltpu.make_async_copy(srcs[t], lands[t].at[me], loc.at[t]))
    return remote, local


def _slabs_start(mode, arrays, after, name):
    n = len(arrays)
    if mode == "forward":
        thru = list(arrays)
    else:
        shapes = [a.shape if mode == "exchange" else (N_DEV,) + a.shape for a in arrays]
        thru = list(arrays) + [lax.empty(s, a.dtype) for s, a in zip(shapes, arrays)]
    m = len(thru)
    n_sem = n * len(RELATIONS[mode])

    def body(*refs):
        srcs, lands = refs[:n], refs[m - n:m]
        send, recv, loc = refs[m + 1:m + 4]
        remote, local = _slab_copies(mode, srcs, lands, send, recv, loc)
        for cp in remote + local:
            cp.start()
        refs[-1][...] = jnp.zeros_like(refs[-1])

    return pl.pallas_call(
        body, name=name,
        out_shape=(pltpu.SemaphoreType.DMA((n_sem,)), pltpu.SemaphoreType.DMA((n_sem,)),
                   pltpu.SemaphoreType.DMA((n,)),
                   *[pltpu.HBM(a.shape, a.dtype) for a in thru],
                   _sds((1, D), F32)),
        in_specs=[HBM_SPEC] * m + [pl.BlockSpec(memory_space=pl.ANY)],
        out_specs=(SEM_SPEC, SEM_SPEC, SEM_SPEC, *[HBM_SPEC] * m, pl.BlockSpec(memory_space=pltpu.VMEM)),
        input_output_aliases={t: 3 + t for t in range(m)},
        compiler_params=pltpu.CompilerParams(has_side_effects=EFFECT),
    )(*[pltpu.with_memory_space_constraint(a, pltpu.HBM) for a in thru], after)


def _slabs_wait(mode, n, started, after, name):
    sems = started[0:3]
    thru = started[3:-1]
    m = len(thru)

    def body(*refs):
        srcs, lands = refs[:n], refs[m - n:m]
        remote, local = _slab_copies(mode, srcs, lands, *refs[m:m + 3])
        for cp in remote:
            cp.wait_send()
            cp.wait_recv()
        for cp in local:
            cp.wait()

    res = pl.pallas_call(
        body, name=name,
        out_shape=tuple(pltpu.HBM(a.shape, a.dtype) for a in thru),
        in_specs=[HBM_SPEC] * m + [SEM_SPEC] * 3 + [pl.BlockSpec(memory_space=pl.ANY)],
        out_specs=tuple([HBM_SPEC] * m),
        input_output_aliases={t: t for t in range(m)},
        compiler_params=pltpu.CompilerParams(has_side_effects=EFFECT),
    )(*thru, *sems, after)
    return list(res[m - n:m])


def _ada_forward(small8, w_ada, b_ada64):
    sw = small8.shape[1]

    def body(sm_ref, w_ref, b_ref, gath_ref, ada_ref, part_ref, send1, recv1, send2, recv2):
        x, y, c, me = _my_place()
        row_me = pl.multiple_of(me * 8, 8)
        gath_ref[pl.ds(row_me, 8), :] = sm_ref[...]
        first = []
        for k in range(1, N_DEV):
            peer, _ = _peer(x, y, c, k)
            cp = pltpu.make_async_remote_copy(
                src_ref=sm_ref, dst_ref=gath_ref.at[pl.ds(row_me, 8), :], send_sem=send1.at[k - 1],
                recv_sem=recv1.at[k - 1], device_id=peer, device_id_type=MESH)
            cp.start()
            first.append(cp)
        for cp in first:
            cp.wait()
        cs = gath_ref[:, 0:D]
        cs = cs * jax.nn.sigmoid(cs)
        part_ref[...] = jnp.dot(cs, w_ref[...], preferred_element_type=F32, precision=HIGH)
        ada_ref[pl.ds(row_me, 8), :] = part_ref[pl.ds(row_me, 8), :]
        second = []
        for k in range(1, N_DEV):
            peer, peer_lin = _peer(x, y, c, k)
            cp = pltpu.make_async_remote_copy(
                src_ref=part_ref.at[pl.ds(pl.multiple_of(peer_lin * 8, 8), 8), :],
                dst_ref=ada_ref.at[pl.ds(row_me, 8), :], send_sem=send2.at[k - 1],
                recv_sem=recv2.at[k - 1], device_id=peer, device_id_type=MESH)
            cp.start()
            second.append(cp)
        for cp in second:
            cp.wait()
        ada_ref[...] = ada_ref[...] + b_ref[...]

    vm = pl.BlockSpec(memory_space=pltpu.VMEM)
    return pl.pallas_call(
        body, name="ada_forward",
        out_shape=[_sds((8 * N_DEV, sw), F32), _sds((8 * N_DEV, ADA_W), F32)],
        in_specs=[vm, vm, vm], out_specs=[vm, vm],
        scratch_shapes=[pltpu.VMEM((8 * N_DEV, ADA_W), F32)] + [pltpu.SemaphoreType.DMA((7,))] * 4,
        compiler_params=pltpu.CompilerParams(vmem_limit_bytes=32 * MIB),
    )(small8, w_ada, b_ada64)


def _sum_slabs(land):
    def body(l_ref, o_ref):
        acc = l_ref[0]
        for j in range(1, N_DEV):
            acc = acc + l_ref[j]
        o_ref[...] = acc

    vm = pl.BlockSpec(memory_space=pltpu.VMEM)
    return pl.pallas_call(body, name="sum_slabs", out_shape=_sds(land.shape[1:], F32), in_specs=[vm], out_specs=vm,
                          compiler_params=pltpu.CompilerParams(vmem_limit_bytes=32 * MIB))(land)


def _small_allreduce(vectors):
    n = len(vectors)

    def body(*refs):
        v_refs, (sum_ref, gath_ref, pack, send, recv) = refs[:n], refs[n:]
        x, y, c, me = _my_place()
        for k in range(n):
            pack[k:k + 1, :] = v_refs[k][...]
        gath_ref[me] = pack[...]
        cps = []
        for k in range(1, N_DEV):
            peer, _ = _peer(x, y, c, k)
            cp = pltpu.make_async_remote_copy(
                src_ref=pack, dst_ref=gath_ref.at[me], send_sem=send.at[k - 1],
                recv_sem=recv.at[k - 1], device_id=peer, device_id_type=MESH)
            cp.start()
            cps.append(cp)
        for cp in cps:
            cp.wait()
        acc = gath_ref[0]
        for j in range(1, N_DEV):
            acc = acc + gath_ref[j]
        sum_ref[...] = acc

    vm = pl.BlockSpec(memory_space=pltpu.VMEM)
    return pl.pallas_call(
        body, name="small_allreduce",
        out_shape=[_sds((n, D), F32), _sds((N_DEV, n, D), F32)],
        in_specs=[vm] * n, out_specs=[vm, vm],
        scratch_shapes=[pltpu.VMEM((n, D), F32), pltpu.SemaphoreType.DMA((7,)), pltpu.SemaphoreType.DMA((7,))],
    )(*vectors)


F_TILES = tuple((f0, min(512, D_FF - f0)) for f0 in range(0, D_FF, 512))
F_TILES_NARROW = tuple((f0, 256) for f0 in range(0, D_FF, 256))


def _swiglu_tile(n, wt_ref, f0, tf):
    g = _dot_nt(n, wt_ref[f0:f0 + tf, :])
    u = _dot_nt(n, wt_ref[D_FF + f0:D_FF + f0 + tf, :])
    sg = jax.nn.sigmoid(g)
    silu = g * sg
    return (u * (sg * (1.0 + g * (1.0 - sg)))).astype(BF), silu.astype(BF), (silu * u).astype(BF)


def _ffn_in(h, sh, sc, gp, wt, name):
    S = h.shape[0]
    R = min(512, S)

    def body(h_ref, sh_ref, sc_ref, gp_ref, w_ref, n_ref, dg_ref, sl_ref, a_ref):
        for r0 in range(0, R, CHUNK):
            rows = slice(r0, r0 + CHUNK)
            n = _prenorm(h_ref[rows, :], gp_ref[...], sc_ref[...], sh_ref[...]).astype(BF)
            n_ref[rows, :] = n
            for f0, tf in F_TILES_NARROW:
                dg_ref[rows, f0:f0 + tf], sl_ref[rows, f0:f0 + tf], a_ref[rows, f0:f0 + tf] = _swiglu_tile(
                    n, w_ref, f0, tf)

    vec = _const((1, D))
    rows_ = lambda w_: pl.BlockSpec((R, w_), lambda i: (i, 0))
    return pl.pallas_call(
        body, name=name, grid=(S // R,),
        out_shape=[_sds((S, D), BF)] + [_sds((S, D_FF), BF)] * 3,
        in_specs=[rows_(D), vec, vec, vec, _resident((2 * D_FF, D))],
        out_specs=[rows_(D), rows_(D_FF), rows_(D_FF), rows_(D_FF)],
        compiler_params=_cp(1, 56),
    )(*_in_hbm(h), sh, sc, gp, *_in_hbm(wt))


def _ffn_out(a, w, h, gate, gp, name, target=None):
    S = h.shape[0]
    R = min(512, S)
    with_loss = target is not None

    def body(a_ref, w_ref, h_ref, gate_ref, gp_ref, *rest):
        if with_loss:
            t_ref, out_ref, y_ref, tot_ref = rest

            @pl.when(pl.program_id(0) == 0)
            def _():
                tot_ref[...] = jnp.zeros_like(tot_ref)
        else:
            out_ref, y_ref = rest
        for r0 in range(0, R, CHUNK):
            rows = slice(r0, r0 + CHUNK)
            y = _dot(a_ref[rows, :], w_ref[...])
            y_ref[rows, :] = y.astype(BF)
            hn = h_ref[rows, :] + (0.5 * gate_ref[...]) * (y * _rms_r(y) * gp_ref[...])
            if with_loss:
                e = hn - t_ref[rows, :]
                out_ref[rows, :] = e * (1.0 / D)
                tot_ref[...] += jnp.sum(jnp.sum(e * e, axis=1, keepdims=True), axis=0, keepdims=True)
            else:
                out_ref[rows, :] = hn

    vec = _const((1, D))
    rows_ = lambda w_: pl.BlockSpec((R, w_), lambda i: (i, 0))
    return pl.pallas_call(
        body, name=name, grid=(S // R,),
        out_shape=[_sds((S, D), F32), _sds((S, D), BF)] + ([_sds((1, 1), F32)] if with_loss else []),
        in_specs=[rows_(D_FF), _resident((D_FF, D)), rows_(D), vec, vec] + ([rows_(D)] if with_loss else []),
        out_specs=[rows_(D), rows_(D)] + ([_const((1, 1))] if with_loss else []),
        compiler_params=_cp(1, 48),
    )(*_in_hbm(a, w, h), gate, gp, *(_in_hbm(target) if with_loss else ()))


def _ffn_out_bwd(dh, y, dsilu_u, silu, w, gate, gp, name, after=()):
    S = dh.shape[0]
    R = min(512, S)

    def body(dh_ref, y_ref, g_ref, u_ref, w_ref, gate_ref, gp_ref, dy_ref, dgu_ref, dgate_ref, dgp_ref):
        @pl.when(pl.program_id(0) == 0)
        def _():
            dgate_ref[...] = jnp.zeros_like(dgate_ref)
            dgp_ref[...] = jnp.zeros_like(dgp_ref)
        for r0 in range(0, R, CHUNK):
            rows = slice(r0, r0 + CHUNK)
            dy, dgate, dgp = _postnorm_bwd(dh_ref[rows, :], y_ref[rows, :], gate_ref[...], gp_ref[...], 0.5)
            dgate_ref[...] += dgate
            dgp_ref[...] += dgp
            dyb = dy.astype(BF)
            dy_ref[rows, :] = dyb
            for f0, tf in F_TILES:
                da = _dot_nt(dyb, w_ref[f0:f0 + tf, :])
                dgu_ref[rows, f0:f0 + tf] = (da * g_ref[rows, f0:f0 + tf].astype(F32)).astype(BF)
                dgu_ref[rows, D_FF + f0:D_FF + f0 + tf] = (da * u_ref[rows, f0:f0 + tf].astype(F32)).astype(BF)

    vec = _const((1, D))
    rows_ = lambda w_: pl.BlockSpec((R, w_), lambda i: (i, 0))
    body, after_specs = _behind(body, 7, after)
    return pl.pallas_call(
        body, name=name, grid=(S // R,),
        out_shape=[_sds((S, D), BF), _sds((S, 2 * D_FF), BF), _sds((1, D), F32), _sds((1, D), F32)],
        in_specs=[rows_(D), rows_(D), rows_(D_FF), rows_(D_FF), _resident((D_FF, D)), vec, vec] + after_specs,
        out_specs=[rows_(D), rows_(2 * D_FF), vec, vec],
        compiler_params=_cp(1, 56),
    )(*_in_hbm(dh, y, dsilu_u, silu, w), gate, gp, *after)


def _ffn_dn(dgu, wt, h, dh, sc, gp, name, after=()):
    S = h.shape[0]
    R = min(512, S)

    def body(dgu_ref, w_ref, h_ref, dh_ref, sc_ref, gp_ref, out_ref, dsh_ref, dsc_ref, dgp_ref):
        @pl.when(pl.program_id(0) == 0)
        def _():
            dsh_ref[...] = jnp.zeros_like(dsh_ref)
            dsc_ref[...] = jnp.zeros_like(dsc_ref)
            dgp_ref[...] = jnp.zeros_like(dgp_ref)

        for r0 in range(0, R, CHUNK):
            rows = slice(r0, r0 + CHUNK)
            dn = _dot(dgu_ref[rows, :], w_ref[...])
            dx, dsh, dsc, dgp = _prenorm_bwd(dn, h_ref[rows, :], gp_ref[...], sc_ref[...])
            out_ref[rows, :] = dh_ref[rows, :] + dx
            dsh_ref[...] += dsh
            dsc_ref[...] += dsc
            dgp_ref[...] += dgp

    vec = _const((1, D))
    rows_ = lambda w_: pl.BlockSpec((R, w_), lambda i: (i, 0))
    body, after_specs = _behind(body, 6, after)
    return pl.pallas_call(
        body, name=name, grid=(S // R,),
        out_shape=[_sds((S, D), F32)] + [_sds((1, D), F32)] * 3,
        in_specs=[rows_(2 * D_FF), _resident((2 * D_FF, D)), rows_(D), rows_(D), vec, vec] + after_specs,
        out_specs=[rows_(D), vec, vec, vec],
        compiler_params=_cp(1, 56),
    )(*_in_hbm(dgu, wt, h, dh), sc, gp, *after)


def _ffn_bwd(dh, y, dsilu_u, silu, w, wt, h, gate, gpost, sc, gpre, name):
    S = dh.shape[0]
    R = min(256, S)

    def body(dh_ref, y_ref, g_ref, u_ref, w_ref, wt_ref, h_ref, gate_ref, gpost_ref, sc_ref, gpre_ref,
             dy_ref, dgu_ref, out_ref, dgate_ref, dgpost_ref, dsh_ref, dsc_ref, dgpre_ref):
        @pl.when(pl.program_id(0) == 0)
        def _():
            for r in (dgate_ref, dgpost_ref, dsh_ref, dsc_ref, dgpre_ref):
                r[...] = jnp.zeros_like(r)
        dhh = dh_ref[...]
        dy, dgate, dgpost = _postnorm_bwd(dhh, y_ref[...], gate_ref[...], gpost_ref[...], 0.5)
        dgate_ref[...] += dgate
        dgpost_ref[...] += dgpost
        dyb = dy.astype(BF)
        dy_ref[...] = dyb
        for f0, tf in F_TILES:
            da = _dot_nt(dyb, w_ref[f0:f0 + tf, :])
            dgu_ref[:, f0:f0 + tf] = (da * g_ref[:, f0:f0 + tf].astype(F32)).astype(BF)
            dgu_ref[:, D_FF + f0:D_FF + f0 + tf] = (da * u_ref[:, f0:f0 + tf].astype(F32)).astype(BF)
        dn = _dot(dgu_ref[...], wt_ref[...])
        dx, dsh, dsc, dgpre = _prenorm_bwd(dn, h_ref[...], gpre_ref[...], sc_ref[...])
        out_ref[...] = dhh + dx
        dsh_ref[...] += dsh
        dsc_ref[...] += dsc
        dgpre_ref[...] += dgpre

    vec = _const((1, D))
    rows_ = lambda w_: pl.BlockSpec((R, w_), lambda i: (i, 0))
    return pl.pallas_call(
        body, name=name, grid=(S // R,),
        out_shape=[_sds((S, D), BF), _sds((S, 2 * D_FF), BF), _sds((S, D), F32)] + [_sds((1, D), F32)] * 5,
        in_specs=[rows_(D), rows_(D), rows_(D_FF), rows_(D_FF), _resident((D_FF, D)), _resident((2 * D_FF, D)),
                  rows_(D), vec, vec, vec, vec],
        out_specs=[rows_(D), rows_(2 * D_FF), rows_(D)] + [vec] * 5,
        compiler_params=_cp(1, 56),
    )(*_in_hbm(dh, y, dsilu_u, silu, w, wt, h), gate, gpost, sc, gpre)


def _tn_matmul(a, b, name, tm=None):
    S, M_all = a.shape
    N = b.shape[1]
    M = M_all if tm is None else tm
    GA = M_all // M
    ts = min(2048 if M * N <= 2 * D * D else 1024, S)
    nk = S // ts
    chunks = [(m0, min(CHUNK, M - m0)) for m0 in range(0, M, CHUNK)]

    def body(a_ref, b_ref, o_ref, acc):
        k = pl.program_id(1)

        @pl.when(k == 0)
        def _():
            acc[...] = jnp.zeros_like(acc)

        for m0, mc in chunks:
            acc[m0:m0 + mc, :] += _dot_tn(a_ref[:, m0:m0 + mc], b_ref[...])

        @pl.when(k == nk - 1)
        def _():
            for m0, mc in chunks:
                o_ref[m0:m0 + mc, :] = acc[m0:m0 + mc, :].astype(BF)

    return pl.pallas_call(
        body, name=name, grid=(GA, nk),
        out_shape=_sds((M_all, N), BF),
        in_specs=[pl.BlockSpec((ts, M), lambda ga, k: (k, ga)), pl.BlockSpec((ts, N), lambda ga, k: (k, 0))],
        out_specs=pl.BlockSpec((M, N), lambda ga, k: (ga, 0)),
        scratch_shapes=[pltpu.VMEM((M, N), F32)],
        compiler_params=_cp(2, 56),
    )(*_in_hbm(a, b))


def _mix_in(h, sh, sc, gp, w, wq):
    S = h.shape[0]
    R = min(512, S)

    def body(h_ref, sh_ref, sc_ref, gp_ref, w_ref, wq_ref, n_ref, qkv_ref, zg_ref, gates_ref):
        for r0 in range(0, R, CHUNK):
            rows = slice(r0, r0 + CHUNK)
            nb = _prenorm(h_ref[rows, :], gp_ref[...], sc_ref[...], sh_ref[...]).astype(BF)
            n_ref[rows, :] = nb
            qkv_ref[rows, 0:Q_W] = _dot_nt(nb, wq_ref[...]).astype(BF)
            qkv_ref[rows, Q_W:QKV_W] = _dot_nt(nb, w_ref[Q_W:QKV_W, :]).astype(BF)
            zg_ref[rows, :] = _dot_nt(nb, w_ref[ZG_OFF:GATE_OFF, :]).astype(BF)
            gates_ref[rows, :] = jax.nn.sigmoid(_dot_nt(nb, w_ref[GATE_OFF:IN_W, :])).astype(BF)

    vec = _const((1, D))
    rows = lambda w_: pl.BlockSpec((R, w_), lambda i: (i, 0))
    return pl.pallas_call(
        body, name="mix_in", grid=(S // R,),
        out_shape=[_sds((S, D), BF), _sds((S, QKV_W), BF), _sds((S, 2 * G_W), BF), _sds((S, 2 * D), BF)],
        in_specs=[rows(D), vec, vec, vec, _resident((IN_W, D)), _resident((Q_W, D))],
        out_specs=[rows(D), rows(QKV_W), rows(2 * G_W), rows(2 * D)],
        compiler_params=_cp(1, 48),
    )(*_in_hbm(h), sh, sc, gp, *_in_hbm(w, wq))


def _bias_table(rel_bias, bucket):
    def body(rel_ref, bk_ref, out_ref):
        bk = bk_ref[...]
        qi = lax.broadcasted_iota(jnp.int32, (BLK, 2 * BLK), 0)
        kj = lax.broadcasted_iota(jnp.int32, (BLK, 2 * BLK), 1)
        dist = qi + BLK - kj
        window = (dist >= 0) & (dist < BLK)
        for h in range(N_HEADS):
            acc = jnp.zeros((BLK, 2 * BLK), F32)
            for b in range(N_BUCKETS):
                acc = jnp.where(bk == b, rel_ref[b, h], acc)
            out_ref[h // GROUP, pl.ds((h % GROUP) * BLK, BLK), :] = jnp.where(window, acc, NEG)

    return pl.pallas_call(
        body, name="bias_table",
        out_shape=_sds((N_KV, GROUP * BLK, 2 * BLK), F32),
        in_specs=[pl.BlockSpec(memory_space=pltpu.SMEM), pl.BlockSpec(memory_space=pltpu.VMEM)],
        out_specs=pl.BlockSpec(memory_space=pltpu.VMEM),
    )(rel_bias, bucket)


ATT_TB = 8


HEAD_ROWS = N_HEADS * BLK


def _pair_heads(w):
    return jnp.transpose(w.reshape(N_KV, GROUP, HD, w.shape[1]), (1, 0, 2, 3)).reshape(w.shape)


def _unpair_heads(w):
    return jnp.transpose(w.reshape(GROUP, N_KV, HD, w.shape[1]), (1, 0, 2, 3)).reshape(w.shape)


def _halves(x, scale=1.0):
    low = lax.broadcasted_iota(jnp.int32, x.shape, 1) < HD
    xf = x.astype(F32) * scale
    return jnp.where(low, xf, 0.0).astype(BF), jnp.where(low, 0.0, xf).astype(BF)


def _stack_heads(x, scale=1.0):
    halves = [_halves(x[:, g * 128:(g + 1) * 128], scale) for g in range(GROUP)]
    return jnp.concatenate([lo for lo, _ in halves] + [hi for _, hi in halves], axis=0)


def _attn_probs(q, kvc, kvp, bias_ref, sink_ref, has_prev):
    kv2 = jnp.concatenate([kvp, kvc], axis=0)
    kboth, vboth = kv2[:, 0:KV_W], kv2[:, KV_W:2 * KV_W]
    qpad = _stack_heads(q, SCALE)
    s = _dot_nt(qpad, kboth) + bias_ref[...]
    if has_prev is not None:
        col = lax.broadcasted_iota(jnp.int32, (HEAD_ROWS, 2 * BLK), 1)
        s = jnp.where((col >= BLK) | has_prev, s, NEG)
    row_head = lax.broadcasted_iota(jnp.int32, (HEAD_ROWS, 1), 0) // BLK
    sink = jnp.zeros((HEAD_ROWS, 1), F32)
    for h in range(N_HEADS):
        sink = jnp.where(row_head == h, sink_ref[h], sink)
    m = jnp.maximum(jnp.max(s, axis=1, keepdims=True), sink)
    p = jnp.exp(s - m)
    e_sink = jnp.exp(sink - m)
    inv = 1.0 / (jnp.sum(p, axis=1, keepdims=True) + e_sink)
    return qpad, kboth, vboth, p * inv, e_sink * inv


def _attn_fwd(qkv, bias, sinks):
    S = qkv.shape[0]
    tb = min(ATT_TB, S // BLK)
    T = tb * BLK

    def body(sink_ref, q_ref, kv_ref, kvp_ref, bias_ref, o_ref):
        step = pl.program_id(0)
        for j in range(tb):
            rows = slice(j * BLK, (j + 1) * BLK)
            kvp = kvp_ref[...] if j == 0 else kv_ref[(j - 1) * BLK:j * BLK, :]
            has_prev = (step > 0) if j == 0 else None
            _, _, vboth, prob, _ = _attn_probs(q_ref[rows, :], kv_ref[rows, :], kvp, bias_ref, sink_ref, has_prev)
            pb = prob.astype(BF)
            v_low, v_high = _halves(vboth)
            half = HEAD_ROWS // 2
            o = _dot(pb[0:half], v_low) + _dot(pb[half:HEAD_ROWS], v_high)
            for g in range(GROUP):
                o_ref[rows, g * 128:(g + 1) * 128] = o[g * BLK:(g + 1) * BLK].astype(BF)

    return pl.pallas_call(
        body, name="attn_fwd", grid=(S // T,),
        out_shape=_sds((S, Q_W), BF),
        in_specs=[pl.BlockSpec(memory_space=pltpu.SMEM),
                  pl.BlockSpec((T, Q_W), lambda i: (i, 0)),
                  pl.BlockSpec((T, 2 * KV_W), lambda i: (i, 2)),
                  pl.BlockSpec((BLK, 2 * KV_W), lambda i: (jnp.maximum(i * tb - 1, 0), 2)),
                  _const((HEAD_ROWS, 2 * BLK))],
        out_specs=pl.BlockSpec((T, Q_W), lambda i: (i, 0)),
        compiler_params=_cp(1, 32),
    )(sinks, *_in_hbm(qkv, qkv, qkv, bias))


def _attn_bwd(qkv, bias, sinks, do):
    S = qkv.shape[0]
    tb = min(ATT_TB, S // BLK)
    T = tb * BLK
    nt = S // T
    half = HEAD_ROWS // 2

    def body(sink_ref, q_ref, kv_ref, kvp_ref, bias_ref, do_ref, dq_ref, dkv_ref, dbias_ref, dsink_ref, carry):
        i = pl.program_id(0)

        @pl.when(i == 0)
        def _():
            carry[...] = jnp.zeros_like(carry)
            dbias_ref[...] = jnp.zeros_like(dbias_ref)
            dsink_ref[...] = jnp.zeros_like(dsink_ref)

        from_next = carry[...]
        head_row = lax.broadcasted_iota(jnp.int32, (N_HEADS, 128), 0)
        low = lax.broadcasted_iota(jnp.int32, (BLK, 128), 1) < HD
        for j in reversed(range(tb)):
            rows = slice(j * BLK, (j + 1) * BLK)
            kvp = kvp_ref[...] if j == 0 else kv_ref[(j - 1) * BLK:j * BLK, :]
            has_prev = (i < nt - 1) if j == 0 else None
            qpad, kboth, vboth, prob, p_sink = _attn_probs(q_ref[rows, :], kv_ref[rows, :], kvp, bias_ref, sink_ref,
                                                           has_prev)
            pb = prob.astype(BF)
            dopad = _stack_heads(do_ref[rows, :])
            dp = _dot_nt(dopad, vboth)
            delta = jnp.sum(prob * dp, axis=1, keepdims=True)
            ds = prob * (dp - delta)
            dbias_ref[...] += ds
            sink_term = p_sink * delta
            dsink_rows = jnp.zeros((N_HEADS, 128), F32)
            for h in range(N_HEADS):
                val = -jnp.sum(sink_term[h * BLK:(h + 1) * BLK], axis=0, keepdims=True)
                dsink_rows = jnp.where(head_row == h, val, dsink_rows)
            dsink_ref[...] += dsink_rows
            dsb = ds.astype(BF)
            dqpad = _dot(dsb, kboth) * SCALE
            for g in range(GROUP):
                dq_ref[rows, g * 128:(g + 1) * 128] = jnp.where(
                    low, dqpad[g * BLK:(g + 1) * BLK], dqpad[half + g * BLK:half + (g + 1) * BLK]).astype(BF)
            dkv2 = jnp.concatenate([jnp.transpose(_dot_tn(qpad, dsb)),
                                    jnp.transpose(_dot_tn(dopad, pb))], axis=1)
            dkv_ref[rows, :] = (dkv2[BLK:2 * BLK] + from_next).astype(BF)
            from_next = dkv2[0:BLK]
        carry[...] = from_next

    return pl.pallas_call(
        body, name="attn_bwd", grid=(nt,),
        out_shape=[_sds((S, Q_W), BF), _sds((S, 2 * KV_W), BF),
                   _sds((HEAD_ROWS, 2 * BLK), F32), _sds((N_HEADS, 128), F32)],
        in_specs=[pl.BlockSpec(memory_space=pltpu.SMEM),
                  pl.BlockSpec((T, Q_W), lambda i: (nt - 1 - i, 0)),
                  pl.BlockSpec((T, 2 * KV_W), lambda i: (nt - 1 - i, 2)),
                  pl.BlockSpec((BLK, 2 * KV_W), lambda i: (jnp.maximum((nt - 1 - i) * tb - 1, 0), 2)),
                  _const((HEAD_ROWS, 2 * BLK)),
                  pl.BlockSpec((T, Q_W), lambda i: (nt - 1 - i, 0))],
        out_specs=[pl.BlockSpec((T, Q_W), lambda i: (nt - 1 - i, 0)),
                   pl.BlockSpec((T, 2 * KV_W), lambda i: (nt - 1 - i, 0)),
                   _const((HEAD_ROWS, 2 * BLK)), _const((N_HEADS, 128))],
        scratch_shapes=[pltpu.VMEM((BLK, 2 * KV_W), F32)],
        compiler_params=_cp(1, 32),
    )(sinks, *_in_hbm(qkv, qkv, qkv, bias, do))


def _rel_bias_grad(dbias, bucket):
    def body(db_ref, bk_ref, out_ref):
        bk = bk_ref[...]
        lane = lax.broadcasted_iota(jnp.int32, (1, 128), 1)
        for h in range(N_HEADS):
            d = db_ref[h // GROUP, pl.ds((h % GROUP) * BLK, BLK), :]
            row = jnp.zeros((1, 128), F32)
            for b in range(N_BUCKETS):
                tot = jnp.sum(jnp.sum(jnp.where(bk == b, d, 0.0), axis=1, keepdims=True), axis=0, keepdims=True)
                row = jnp.where(lane == b, tot, row)
            out_ref[pl.ds(h, 1), :] = row

    vm = pl.BlockSpec(memory_space=pltpu.VMEM)
    return pl.pallas_call(body, name="rel_bias_grad", out_shape=_sds((N_HEADS, 128), F32),
                          in_specs=[vm, vm], out_specs=vm)(dbias, bucket)


def _gmlp_parts(zg, lg_ref, lb_ref):
    z = zg.astype(F32)
    ge = _gelu(z)
    u, vg = ge[:, 0:G_W], ge[:, G_W:2 * G_W]
    mu = jnp.mean(vg, axis=-1, keepdims=True)
    xc = vg - mu
    rstd = lax.rsqrt(jnp.mean(xc * xc, axis=-1, keepdims=True) + EPS)
    xh = xc * rstd
    return z, u, xh, rstd, xh * lg_ref[...] + lb_ref[...]


def _causal_weights(ws_ref, wc):
    t = lax.broadcasted_iota(jnp.int32, (BLK, BLK), 0)
    s = lax.broadcasted_iota(jnp.int32, (BLK, BLK), 1)
    for g in range(N_HEADS):
        wc[g] = jnp.where(s <= t, ws_ref[g], 0.0).astype(BF)


def _spatial(vb, wc, bst_ref, p, low):
    xp = vb[:, p * 128:(p + 1) * 128]
    s0 = _dot(wc[2 * p], xp) + bst_ref[:, 2 * p:2 * p + 1]
    s1 = _dot(wc[2 * p + 1], xp) + bst_ref[:, 2 * p + 1:2 * p + 2]
    return xp, jnp.where(low, s0, s1)


def _gmlp_fwd(zg, lg, lb, ws, bst):
    S = zg.shape[0]
    tb = min(ATT_TB, S // BLK)
    T = tb * BLK

    def body(zg_ref, lg_ref, lb_ref, ws_ref, bst_ref, o_ref, wc):
        @pl.when(pl.program_id(0) == 0)
        def _():
            _causal_weights(ws_ref, wc)
        low = lax.broadcasted_iota(jnp.int32, (BLK, 128), 1) < HD
        for j in range(tb):
            rows = slice(j * BLK, (j + 1) * BLK)
            _, u, _, _, vln = _gmlp_parts(zg_ref[rows, :], lg_ref, lb_ref)
            vb = vln.astype(BF)
            for p in range(4):
                _, sp = _spatial(vb, wc, bst_ref, p, low)
                o_ref[rows, p * 128:(p + 1) * 128] = (u[:, p * 128:(p + 1) * 128] * sp).astype(BF)

    return pl.pallas_call(
        body, name="gmlp_fwd", grid=(S // T,),
        out_shape=_sds((S, G_W), BF),
        in_specs=[pl.BlockSpec((T, 2 * G_W), lambda i: (i, 0)), _const((1, G_W)), _const((1, G_W)),
                  _const((N_HEADS, BLK, BLK)), _const((BLK, N_HEADS))],
        out_specs=pl.BlockSpec((T, G_W), lambda i: (i, 0)),
        scratch_shapes=[pltpu.VMEM((N_HEADS, BLK, BLK), BF)],
        compiler_params=_cp(1, 32),
    )(*_in_hbm(zg), lg, lb, ws, bst)


def _gmlp_bwd(zg, d_out, lg, lb, ws, bst):
    S = zg.shape[0]
    tb = min(ATT_TB, S // BLK)
    T = tb * BLK
    nb = S // T

    def body(zg_ref, d_ref, lg_ref, lb_ref, ws_ref, bst_ref, dzg_ref, dws_ref, dbs_ref, dlg_ref, dlb_ref, wc, dbacc):
        i = pl.program_id(0)

        @pl.when(i == 0)
        def _():
            _causal_weights(ws_ref, wc)
            dws_ref[...] = jnp.zeros_like(dws_ref)
            dlg_ref[...] = jnp.zeros_like(dlg_ref)
            dlb_ref[...] = jnp.zeros_like(dlb_ref)
            dbacc[...] = jnp.zeros_like(dbacc)

        low = lax.broadcasted_iota(jnp.int32, (BLK, 128), 1) < HD
        for j in range(tb):
            rows = slice(j * BLK, (j + 1) * BLK)
            z, u, xh, rstd, vln = _gmlp_parts(zg_ref[rows, :], lg_ref, lb_ref)
            vb = vln.astype(BF)
            d = d_ref[rows, :].astype(F32)
            du_parts, dvln_parts = [], []
            for p in range(4):
                xp, sp = _spatial(vb, wc, bst_ref, p, low)
                dp = d[:, p * 128:(p + 1) * 128]
                du_parts.append(dp * sp)
                dsp = dp * u[:, p * 128:(p + 1) * 128]
                dbacc[:, p * 128:(p + 1) * 128] += dsp
                d0 = jnp.where(low, dsp, 0.0).astype(BF)
                d1 = jnp.where(low, 0.0, dsp).astype(BF)
                dws_ref[2 * p] += _dot_nt(d0, xp)
                dws_ref[2 * p + 1] += _dot_nt(d1, xp)
                dvln_parts.append(_dot_tn(wc[2 * p], d0) + _dot_tn(wc[2 * p + 1], d1))
            dvln = jnp.concatenate(dvln_parts, axis=1)
            dlg_ref[...] += _colsum(dvln * xh)
            dlb_ref[...] += _colsum(dvln)
            dxh = dvln * lg_ref[...]
            dvg = rstd * (dxh - jnp.mean(dxh, axis=-1, keepdims=True)
                          - xh * jnp.mean(dxh * xh, axis=-1, keepdims=True))
            dge = jnp.concatenate(du_parts + [dvg], axis=1)
            dzg_ref[rows, :] = (dge * _gelu_grad(z)).astype(BF)

        @pl.when(i == nb - 1)
        def _():
            t = lax.broadcasted_iota(jnp.int32, (BLK, BLK), 0)
            s = lax.broadcasted_iota(jnp.int32, (BLK, BLK), 1)
            for g in range(N_HEADS):
                dws_ref[g] = jnp.where(s <= t, dws_ref[g], 0.0)
            grp = lax.broadcasted_iota(jnp.int32, (N_HEADS, G_W), 0)
            lane = lax.broadcasted_iota(jnp.int32, (N_HEADS, G_W), 1) // HD
            pick = jnp.where(grp == lane, 1.0, 0.0).astype(F32)
            dbs_ref[...] = lax.dot_general(pick, dbacc[...], (((1,), (1,)), ((), ())),
                                           preferred_element_type=F32, precision=HIGH)

    return pl.pallas_call(
        body, name="gmlp_bwd", grid=(nb,),
        out_shape=[_sds((S, 2 * G_W), BF), _sds((N_HEADS, BLK, BLK), F32), _sds((N_HEADS, BLK), F32),
                   _sds((1, G_W), F32), _sds((1, G_W), F32)],
        in_specs=[pl.BlockSpec((T, 2 * G_W), lambda i: (i, 0)), pl.BlockSpec((T, G_W), lambda i: (i, 0)),
                  _const((1, G_W)), _const((1, G_W)), _const((N_HEADS, BLK, BLK)), _const((BLK, N_HEADS))],
        out_specs=[pl.BlockSpec((T, 2 * G_W), lambda i: (i, 0)), _const((N_HEADS, BLK, BLK)),
                   _const((N_HEADS, BLK)), _const((1, G_W)), _const((1, G_W))],
        scratch_shapes=[pltpu.VMEM((N_HEADS, BLK, BLK), BF), pltpu.VMEM((BLK, G_W), F32)],
        compiler_params=_cp(1, 32),
    )(*_in_hbm(zg, d_out), lg, lb, ws, bst)


def _mix_out(o, gm, gates, h, wa, wg, wo, gate, gp, after=()):
    S = h.shape[0]
    R = min(512, S)

    def body(o_ref, gm_ref, gates_ref, h_ref, wa_ref, wg_ref, wo_ref, gate_ref, gp_ref,
             ya_ref, yg_ref, ym_ref, y_ref, hn_ref):
        for r0 in range(0, R, CHUNK):
            rows = slice(r0, r0 + CHUNK)
            ya = _dot(o_ref[rows, :], wa_ref[...])
            yg = _dot(gm_ref[rows, :], wg_ref[...])
            ya_ref[rows, :] = ya.astype(BF)
            yg_ref[rows, :] = yg.astype(BF)
            ym = (gates_ref[rows, 0:D].astype(F32) * ya + gates_ref[rows, D:2 * D].astype(F32) * yg).astype(BF)
            ym_ref[rows, :] = ym
            y = _dot(ym, wo_ref[...])
            y_ref[rows, :] = y.astype(BF)
            hn_ref[rows, :] = h_ref[rows, :] + gate_ref[...] * (y * _rms_r(y) * gp_ref[...])

    vec = _const((1, D))
    rows = lambda w_: pl.BlockSpec((R, w_), lambda i: (i, 0))
    body, after_specs = _behind(body, 9, after)
    return pl.pallas_call(
        body, name="mix_out", grid=(S // R,),
        out_shape=[_sds((S, D), BF)] * 4 + [_sds((S, D), F32)],
        in_specs=[rows(Q_W), rows(G_W), rows(2 * D), rows(D), _resident((Q_W, D)), _resident((G_W, D)),
                  _resident((D, D)), vec, vec] + after_specs,
        out_specs=[rows(D)] * 5,
        compiler_params=_cp(1, 48),
    )(*_in_hbm(o, gm, gates, h, wa, wg, wo), gate, gp, *after)


def _mix_out_bwd(dh, y, ya, yg, gates, att, gm, ymix, wa, wg, wo, gate, gp, after=()):
    S = dh.shape[0]
    R = min(512, S)
    nb = S // R

    def body(dh_ref, y_ref, ya_ref, yg_ref, gates_ref, att_ref, gm_ref, ym_ref, wa_ref, wg_ref, wo_ref,
             gate_ref, gp_ref, dz_ref, do_ref, dgm_ref, dgate_ref, dgp_ref, gwo_ref, gwa_ref, gwg_ref,
             acc_o, acc_a, acc_g, dy_scr, dya_scr, dyg_scr):
        i = pl.program_id(0)

        @pl.when(i == 0)
        def _():
            for r in (dgate_ref, dgp_ref, acc_o, acc_a, acc_g):
                r[...] = jnp.zeros_like(r)
        for r0 in range(0, R, 2 * CHUNK):
            rows = slice(r0, min(r0 + 2 * CHUNK, R))
            dy, dgate, dgp = _postnorm_bwd(dh_ref[rows, :], y_ref[rows, :], gate_ref[...], gp_ref[...], 1.0)
            dgate_ref[...] += dgate
            dgp_ref[...] += dgp
            dyb = dy.astype(BF)
            dy_scr[rows, :] = dyb
            dym = _dot_nt(dyb, wo_ref[...])
            ga = gates_ref[rows, 0:D].astype(F32)
            gg = gates_ref[rows, D:2 * D].astype(F32)
            dya = (dym * ga).astype(BF)
            dyg = (dym * gg).astype(BF)
            dya_scr[rows, :] = dya
            dyg_scr[rows, :] = dyg
            dz_ref[rows, 0:D] = (dym * ya_ref[rows, :].astype(F32) * (ga * (1.0 - ga))).astype(BF)
            dz_ref[rows, D:2 * D] = (dym * yg_ref[rows, :].astype(F32) * (gg * (1.0 - gg))).astype(BF)
            do_ref[rows, :] = _dot_nt(dya, wa_ref[...]).astype(BF)
            dgm_ref[rows, :] = _dot_nt(dyg, wg_ref[...]).astype(BF)
        for m0 in range(0, D, CHUNK):
            acc_o[m0:m0 + CHUNK, :] += _dot_tn(ym_ref[:, m0:m0 + CHUNK], dy_scr[...])
        for m0 in range(0, Q_W, CHUNK):
            acc_a[m0:m0 + CHUNK, :] += _dot_tn(att_ref[:, m0:m0 + CHUNK], dya_scr[...])
            acc_g[m0:m0 + CHUNK, :] += _dot_tn(gm_ref[:, m0:m0 + CHUNK], dyg_scr[...])

        @pl.when(i == nb - 1)
        def _():
            for m0 in range(0, D, CHUNK):
                gwo_ref[m0:m0 + CHUNK, :] = acc_o[m0:m0 + CHUNK, :].astype(BF)
            for m0 in range(0, Q_W, CHUNK):
                gwa_ref[m0:m0 + CHUNK, :] = acc_a[m0:m0 + CHUNK, :].astype(BF)
                gwg_ref[m0:m0 + CHUNK, :] = acc_g[m0:m0 + CHUNK, :].astype(BF)

    vec = _const((1, D))
    rows = lambda w_: pl.BlockSpec((R, w_), lambda i: (i, 0))
    body, after_specs = _behind(body, 13, after)
    return pl.pallas_call(
        body, name="mix_out_bwd", grid=(nb,),
        out_shape=[_sds((S, 2 * D), BF), _sds((S, Q_W), BF), _sds((S, G_W), BF), _sds((1, D), F32),
                   _sds((1, D), F32), _sds((D, D), BF), _sds((Q_W, D), BF), _sds((G_W, D), BF)],
        in_specs=[rows(D), rows(D), rows(D), rows(D), rows(2 * D), rows(Q_W), rows(G_W), rows(D),
                  _resident((Q_W, D)), _resident((G_W, D)), _resident((D, D)), vec, vec] + after_specs,
        out_specs=[rows(2 * D), rows(Q_W), rows(G_W), vec, vec, _const((D, D)), _const((Q_W, D)),
                   _const((G_W, D))],
        scratch_shapes=[pltpu.VMEM((D, D), F32), pltpu.VMEM((Q_W, D), F32), pltpu.VMEM((G_W, D), F32)]
        + [pltpu.VMEM((R, D), BF)] * 3,
        compiler_params=_cp(1, 60),
    )(*_in_hbm(dh, y, ya, yg, gates, att, gm, ymix, wa, wg, wo), gate, gp, *after)


def _mix_dn(dq, dkv, dzg, dzgate, w, wq, h, dh, sc, gp, after=()):
    S = h.shape[0]
    R = min(512, S)

    def body(dq_ref, dkv_ref, dzg_ref, dzt_ref, w_ref, wq_ref, h_ref, dh_ref, sc_ref, gp_ref,
             out_ref, dsh_ref, dsc_ref, dgp_ref):
        @pl.when(pl.program_id(0) == 0)
        def _():
            dsh_ref[...] = jnp.zeros_like(dsh_ref)
            dsc_ref[...] = jnp.zeros_like(dsc_ref)
            dgp_ref[...] = jnp.zeros_like(dgp_ref)
        for r0 in range(0, R, CHUNK):
            rows = slice(r0, r0 + CHUNK)
            dn = _dot(dq_ref[rows, :], wq_ref[...])
            dn = dn + _dot(dkv_ref[rows, :], w_ref[Q_W:QKV_W, :])
            dn = dn + _dot(dzg_ref[rows, :], w_ref[ZG_OFF:GATE_OFF, :])
            dn = dn + _dot(dzt_ref[rows, :], w_ref[GATE_OFF:IN_W, :])
            dx, dsh, dsc, dgp = _prenorm_bwd(dn, h_ref[rows, :], gp_ref[...], sc_ref[...])
            out_ref[rows, :] = dh_ref[rows, :] + dx
            dsh_ref[...] += dsh
            dsc_ref[...] += dsc
            dgp_ref[...] += dgp

    vec = _const((1, D))
    rows = lambda w_: pl.BlockSpec((R, w_), lambda i: (i, 0))
    body, after_specs = _behind(body, 10, after)
    return pl.pallas_call(
        body, name="mix_dn", grid=(S // R,),
        out_shape=[_sds((S, D), F32)] + [_sds((1, D), F32)] * 3,
        in_specs=[rows(Q_W), rows(2 * KV_W), rows(2 * G_W), rows(2 * D), _resident((IN_W, D)),
                  _resident((Q_W, D)), rows(D), rows(D), vec, vec] + after_specs,
        out_specs=[rows(D), vec, vec, vec],
        compiler_params=_cp(1, 48),
    )(*_in_hbm(dq, dkv, dzg, dzgate, w, wq, h, dh), sc, gp, *after)


def _adamw_math(w, g, m, v):
    m2 = ADAM_B1 * m + (1.0 - ADAM_B1) * g
    v2 = ADAM_B2 * v + (1.0 - ADAM_B2) * (g * g)
    m_hat = m2 / (1.0 - ADAM_B1 ** ADAM_STEP)
    v_hat = v2 / (1.0 - ADAM_B2 ** ADAM_STEP)
    delta = -ADAM_LR * (m_hat / (jnp.sqrt(v_hat) + ADAM_EPS) + ADAM_WD * w)
    return delta, m2, v2


def _row_tile(rows, cols):
    best = None
    for t in range(16, rows + 1, 16):
        if rows % t == 0 and t * cols <= 256 * 1024:
            best = t
    return best if best is not None else rows


def _adamw_sharded(landing, w, m, v, name):
    r, c = w.shape
    tr = _row_tile(r, c)

    def body(l_ref, w_ref, m_ref, v_ref, g_ref, d_ref, m2_ref, v2_ref):
        g = l_ref[0].astype(F32)
        for j in range(1, N_DEV):
            g = g + l_ref[j].astype(F32)
        delta, m2, v2 = _adamw_math(w_ref[...], g, m_ref[...], v_ref[...])
        g_ref[...] = g
        d_ref[...] = delta
        m2_ref[...] = m2
        v2_ref[...] = v2

    row = pl.BlockSpec((tr, c), lambda i: (i, 0))
    return pl.pallas_call(
        body, name=name, grid=(r // tr,),
        out_shape=[_sds((r, c), F32)] * 4,
        in_specs=[pl.BlockSpec((N_DEV, tr, c), lambda i: (0, i, 0)), row, row, row],
        out_specs=[row] * 4,
        compiler_params=_cp(1, 48),
    )(*_in_hbm(landing, w, m, v))


def _adamw_small(items):
    n = len(items)

    def body(*refs):
        for k in range(n):
            w_ref, g_ref, m_ref, v_ref = refs[4 * k:4 * k + 4]
            outs = refs[4 * n + 3 * k:4 * n + 3 * k + 3]
            for o_ref, val in zip(outs, _adamw_math(w_ref[...], g_ref[...], m_ref[...], v_ref[...])):
                o_ref[...] = val

    vm = pl.BlockSpec(memory_space=pltpu.VMEM)
    flat = pl.pallas_call(
        body, name="adamw_small",
        out_shape=[_sds(it[0].shape, F32) for it in items for _ in range(3)],
        in_specs=[vm] * (4 * n), out_specs=[vm] * (3 * n),
    )(*[a for it in items for a in it])
    return [tuple(flat[3 * k:3 * k + 3]) for k in range(n)]


def _w_ada_update(c8, d_ada, w, m, v):
    tr = 256

    def body(c_ref, d_ref, w_ref, m_ref, v_ref, g_ref, dl_ref, m2_ref, v2_ref):
        cs = c_ref[...]
        cs = cs * jax.nn.sigmoid(cs)
        g = lax.dot_general(cs, d_ref[...], (((0,), (0,)), ((), ())), preferred_element_type=F32, precision=HIGH)
        delta, m2, v2 = _adamw_math(w_ref[...], g, m_ref[...], v_ref[...])
        g_ref[...] = g
        dl_ref[...] = delta
        m2_ref[...] = m2
        v2_ref[...] = v2

    row = pl.BlockSpec((tr, ADA_W), lambda i: (i, 0))
    return pl.pallas_call(
        body, name="w_ada_update", grid=(D // tr,),
        out_shape=[_sds((D, ADA_W), F32)] * 4,
        in_specs=[pl.BlockSpec((N_DEV, tr), lambda i: (0, i)), _const((N_DEV, ADA_W)), row, row, row],
        out_specs=[row] * 4,
        compiler_params=_cp(1, 40),
    )(c8, d_ada, *_in_hbm(w, m, v))


def _t5_bucket():
    qi = np.arange(BLK, dtype=np.int32)[:, None]
    kj = np.arange(2 * BLK, dtype=np.int32)[None, :]
    dist = np.maximum(qi + BLK - kj, 0)
    max_exact = N_BUCKETS // 2
    d_f = np.maximum(dist, max_exact).astype(np.float32)
    large = max_exact + (np.log(d_f / np.float32(max_exact)) / np.float32(math.log(MAX_DISTANCE / max_exact))
                         * np.float32(N_BUCKETS - max_exact)).astype(np.int32)
    large = np.minimum(large, N_BUCKETS - 1)
    return jnp.asarray(np.where(dist < max_exact, dist, large).astype(np.int32))


def _slabs_of_columns(w):
    r, c8 = w.shape
    return jnp.transpose(w.reshape(r, N_DEV, c8 // N_DEV), (1, 0, 2))


def _columns_of_slabs(w8):
    _, r, c = w8.shape
    return jnp.transpose(w8, (1, 0, 2)).reshape(r, N_DEV * c)


def kernel(x, c, rel_bias, w_ada, b_ada, pre_norm_g, post_norm_g, w_ffn1_in, w_ffn1_out, w_in, sinks, gmlp_ln_g, gmlp_ln_b, gmlp_w_s, gmlp_b_s, w_br_attn, w_br_gmlp, w_out, w_ffn2_in, w_ffn2_out, loss_target, m_rel_bias, m_w_ada, m_b_ada, m_pre_norm_g, m_post_norm_g, m_w_ffn1_in, m_w_ffn1_out, m_w_in, m_sinks, m_gmlp_ln_g, m_gmlp_ln_b, m_gmlp_w_s, m_gmlp_b_s, m_w_br_attn, m_w_br_gmlp, m_w_out, m_w_ffn2_in, m_w_ffn2_out, v_rel_bias, v_w_ada, v_b_ada, v_pre_norm_g, v_post_norm_g, v_w_ffn1_in, v_w_ffn1_out, v_w_in, v_sinks, v_gmlp_ln_g, v_gmlp_ln_b, v_gmlp_w_s, v_gmlp_b_s, v_w_br_attn, v_w_br_gmlp, v_w_out, v_w_ffn2_in, v_w_ffn2_out):
    me = 4 * lax.axis_index("x") + 2 * lax.axis_index("y") + lax.axis_index("c")
    x0 = x[0]
    target = loss_target[0]

    transposed = ("w_ffn1_in", "w_in", "w_ffn2_in")
    shards = [w_ffn1_in[0].T, w_ffn1_out[0], w_in[0].T, w_br_attn[0], w_br_gmlp[0], w_out[0],
              w_ffn2_in[0].T, w_ffn2_out[0]]
    shards_bf = [s.astype(BF) for s in shards]
    groups = [shards_bf[0:1], shards_bf[1:6], shards_bf[6:8]]

    def gather_start(i, after):
        return _slabs_start("gather", groups[i], after, "gather_start_%d" % i)

    def forward_start(st, i, after):
        lands = _slabs_wait("gather", len(groups[i]), st, after, "gather_wait_%d" % i)
        return _slabs_start("forward", lands, c, "forward_start_%d" % i)

    def gathered(st, i, after):
        return _slabs_wait("forward", len(groups[i]), st, after, "forward_wait_%d" % i)

    gs0 = gather_start(0, c)

    mine = jnp.concatenate([c[0], pre_norm_g[0].reshape(-1), post_norm_g[0].reshape(-1)])
    small8 = jnp.broadcast_to(mine[None, :], (8, mine.shape[0]))
    b_ada64 = jnp.repeat(b_ada.reshape(N_DEV, ADA_W), 8, axis=0)
    gath, ada64 = _ada_forward(small8, w_ada[0], b_ada64)
    gath8 = gath[::8]
    ada = ada64[::8].reshape(9, D)
    sh1, sc1, g1, sh2, sc2, g2, sh3, sc3, g3 = [ada[k:k + 1] for k in range(9)]
    gains = gath8[:, D:].reshape(N_DEV, 2, 3, 128)
    pre_g = jnp.transpose(gains[:, 0], (1, 0, 2)).reshape(3, D)
    post_g = jnp.transpose(gains[:, 1], (1, 0, 2)).reshape(3, D)
    pre = [pre_g[k:k + 1] for k in range(3)]
    post = [post_g[k:k + 1] for k in range(3)]

    bucket = _t5_bucket()
    bias = _bias_table(rel_bias, bucket).reshape(HEAD_ROWS, 2 * BLK)
    sinks8 = sinks[0]
    lg, lb = gmlp_ln_g, gmlp_ln_b
    ws = gmlp_w_s[0]
    bst = jnp.transpose(gmlp_b_s[0])

    fs0 = forward_start(gs0, 0, sh1)
    gs1 = gather_start(1, fs0[-1])
    wf1_in = gathered(fs0, 0, gs1[-1])[0].reshape(2 * D_FF, D)
    n1, fg1, fu1, fa1 = _ffn_in(x0, sh1, sc1, pre[0], wf1_in, "ffn1_in")
    fs1 = forward_start(gs1, 1, n1)
    gs2 = gather_start(2, fs1[-1])
    mix_w = gathered(fs1, 1, gs2[-1])
    wf1_out = mix_w[0].reshape(D_FF, D)
    w_in_full = mix_w[1].reshape(IN_W, D)
    w_q = _pair_heads(w_in_full[0:Q_W])
    w_bra = _pair_heads(_columns_of_slabs(mix_w[2]))
    w_brg = _columns_of_slabs(mix_w[3])
    w_out_full = mix_w[4].reshape(D, D)
    h1, y1 = _ffn_out(fa1, wf1_out, x0, g1, post[0], "ffn1_out")
    n2, qkv, zg, gates = _mix_in(h1, sh2, sc2, pre[1], w_in_full, w_q)
    att = _attn_fwd(qkv, bias, sinks8)
    gm = _gmlp_fwd(zg, lg, lb, ws, bst)
    fs2 = forward_start(gs2, 2, gm)
    ya, yg, ymix, y2, h2 = _mix_out(att, gm, gates, h1, w_bra, w_brg, w_out_full, g2, post[1], after=(fs2[-1],))
    wf2_in, wf2_out = gathered(fs2, 2, h2)
    wf2_in = wf2_in.reshape(2 * D_FF, D)
    wf2_out = wf2_out.reshape(D_FF, D)
    n3, fg3, fu3, fa3 = _ffn_in(h2, sh3, sc3, pre[2], wf2_in, "ffn2_in")
    dh3, y3, sq = _ffn_out(fa3, wf2_out, h2, g3, post[2], "ffn2_out", target=target)

    def exchange_start(i, arrays):
        return _slabs_start("exchange", arrays, sq, "exchange_start_%d" % i)

    dy3, dgu3, dh2, d_g3, d_post2, d_sh3, d_sc3, d_pre2 = _ffn_bwd(
        dh3, y3, fg3, fu3, wf2_out, wf2_in, h2, g3, post[2], sc3, pre[2], "ffn2_bwd")
    gw_f2_out = _tn_matmul(fa3, dy3, "ffn2_out_wgrad", tm=D_FF // 2).reshape(N_DEV, D_FF // N_DEV, D)
    gw_f2_in = _tn_matmul(dgu3, n3, "ffn2_in_wgrad", tm=D_FF // 2).reshape(N_DEV, FS, D)
    ex1 = exchange_start(1, [gw_f2_out, gw_f2_in])

    dzgate, d_att, d_gm, d_g2, d_post1, gw_out, gw_bra, gw_brg = _mix_out_bwd(
        dh2, y2, ya, yg, gates, att, gm, ymix, w_bra, w_brg, w_out_full, g2, post[1], after=(ex1[-1],))
    ex2 = exchange_start(2, [_slabs_of_columns(_unpair_heads(gw_bra)), _slabs_of_columns(gw_brg),
                             gw_out.reshape(N_DEV, D // N_DEV, D)])
    dq, dkv, dbias, dsink = _attn_bwd(qkv, bias, sinks8, d_att)
    dzg, d_ws, d_bs, d_lg, d_lb = _gmlp_bwd(zg, d_gm, lg, lb, ws, bst)
    d_rel = _rel_bias_grad(dbias.reshape(N_KV, GROUP * BLK, 2 * BLK), bucket)
    early = jnp.concatenate([
        jnp.concatenate([d_lg.reshape(4, 128), d_lb.reshape(4, 128)], axis=0),
        d_bs, d_rel, dsink, d_ws.reshape(N_HEADS * BLK, BLK)], axis=0)
    sm0 = _slabs_start("gather_all", [early], sq, "small_gather_start")
    dh1, d_sh2, d_sc2, d_pre1 = _mix_dn(dq, dkv, dzg, dzgate, w_in_full, w_q, h1, dh2, sc2, pre[1],
                                        after=(ex2[-1], sm0[-1]))
    gw_in = jnp.concatenate(
        [_unpair_heads(_tn_matmul(dq, n2, "w_in_q_wgrad")), _tn_matmul(dkv, n2, "w_in_kv_wgrad"),
         _tn_matmul(dzg, n2, "w_in_zg_wgrad"), _tn_matmul(dzgate, n2, "w_in_gate_wgrad")],
        axis=0).reshape(N_DEV, IN_W // N_DEV, D)
    ex3 = exchange_start(3, [gw_in])

    dy1, dgu1, d_g1, d_post0 = _ffn_out_bwd(dh1, y1, fg1, fu1, wf1_out, g1, post[0], "ffn1_out_bwd",
                                            after=(ex3[-1],))
    gw_f1_out = _tn_matmul(fa1, dy1, "ffn1_out_wgrad", tm=D_FF // 2).reshape(N_DEV, D_FF // N_DEV, D)
    ex4 = exchange_start(4, [gw_f1_out])
    gw_f1_in = _tn_matmul(dgu1, n1, "ffn1_in_wgrad", tm=D_FF // 2).reshape(N_DEV, FS, D)
    ex5 = exchange_start(5, [gw_f1_in])
    grad_x, d_sh1, d_sc1, d_pre0 = _ffn_dn(dgu1, wf1_in, x0, dh1, sc1, pre[0], "ffn1_dn", after=(ex4[-1], ex5[-1]))

    landed = {}
    for i, (ex, nms) in enumerate([(ex1, ["w_ffn2_out", "w_ffn2_in"]),
                                   (ex2, ["w_br_attn", "w_br_gmlp", "w_out"]), (ex3, ["w_in"]),
                                   (ex4, ["w_ffn1_out"]), (ex5, ["w_ffn1_in"])]):
        for nm, land in zip(nms, _slabs_wait("exchange", len(nms), ex, grad_x, "exchange_wait_%d" % i)):
            landed[nm] = land
    moments = [(m_w_ffn1_in, v_w_ffn1_in), (m_w_ffn1_out, v_w_ffn1_out), (m_w_in, v_w_in),
               (m_w_br_attn, v_w_br_attn), (m_w_br_gmlp, v_w_br_gmlp), (m_w_out, v_w_out),
               (m_w_ffn2_in, v_w_ffn2_in), (m_w_ffn2_out, v_w_ffn2_out)]
    names = ["w_ffn1_in", "w_ffn1_out", "w_in", "w_br_attn", "w_br_gmlp", "w_out", "w_ffn2_in", "w_ffn2_out"]
    big = {}
    for nm, w_, (m_, v_) in zip(names, shards, moments):
        if nm in transposed:
            res4 = _adamw_sharded(landed[nm], w_, m_[0].T, v_[0].T, "adamw_" + nm)
            big[nm] = [a.T[None] for a in res4]
        else:
            big[nm] = [a[None] for a in _adamw_sharded(landed[nm], w_, m_[0], v_[0], "adamw_" + nm)]

    my_loss = jnp.broadcast_to(sq * (0.5 / D), (1, D))
    my_loss, _ = lax.optimization_barrier((my_loss, landed["w_ffn1_in"]))
    tot, every = _small_allreduce([d_sh1, d_sc1, d_g1, d_sh2, d_sc2, d_g2, d_sh3, d_sc3, d_g3,
                                   d_pre0, d_pre1, d_pre2, d_post0, d_post1, d_post2, my_loss])
    (early_land,) = _slabs_wait("gather_all", 1, sm0, grad_x, "small_gather_wait")
    tot_early = _sum_slabs(early_land)

    loss = tot[15, 0]
    g_b_ada = tot[0:9].reshape(1, 9 * D)
    g_pre = lax.dynamic_slice_in_dim(tot[9:12], 128 * me, 128, axis=1)[None]
    g_post = lax.dynamic_slice_in_dim(tot[12:15], 128 * me, 128, axis=1)[None]
    g_lg = tot_early[0:4].reshape(1, G_W)
    g_lb = tot_early[4:8].reshape(1, G_W)
    g_bs = tot_early[8:16][None]
    g_rel = jnp.transpose(tot_early[16:24, 0:N_BUCKETS])
    g_sinks = tot_early[24:32, 0][None]
    g_ws = tot_early[32:1056].reshape(1, N_HEADS, BLK, BLK)

    d_ada_mine = lax.dynamic_slice_in_dim(every[:, 0:9].reshape(N_DEV, 9 * D), ADA_W * me, ADA_W, axis=1)
    ada_out = [a[None] for a in _w_ada_update(gath8[:, 0:D], d_ada_mine, w_ada[0], m_w_ada[0], v_w_ada[0])]

    small = [("rel_bias", rel_bias, g_rel, m_rel_bias, v_rel_bias), ("b_ada", b_ada, g_b_ada, m_b_ada, v_b_ada),
             ("pre_norm_g", pre_norm_g, g_pre, m_pre_norm_g, v_pre_norm_g),
             ("post_norm_g", post_norm_g, g_post, m_post_norm_g, v_post_norm_g),
             ("sinks", sinks, g_sinks, m_sinks, v_sinks), ("gmlp_ln_g", gmlp_ln_g, g_lg, m_gmlp_ln_g, v_gmlp_ln_g),
             ("gmlp_ln_b", gmlp_ln_b, g_lb, m_gmlp_ln_b, v_gmlp_ln_b),
             ("gmlp_w_s", gmlp_w_s, g_ws, m_gmlp_w_s, v_gmlp_w_s), ("gmlp_b_s", gmlp_b_s, g_bs, m_gmlp_b_s, v_gmlp_b_s)]
    two_d = lambda a: a.reshape(int(math.prod(a.shape[:-1])), a.shape[-1])
    stepped = _adamw_small([tuple(two_d(a) for a in item[1:]) for item in small])
    res = {"w_ada": ada_out}
    for (nm, w_, g_, _, _), new in zip(small, stepped):
        res[nm] = [g_] + [a.reshape(w_.shape) for a in new]
    res.update(big)
    order = ["rel_bias", "w_ada", "b_ada", "pre_norm_g", "post_norm_g", "w_ffn1_in", "w_ffn1_out", "w_in", "sinks",
             "gmlp_ln_g", "gmlp_ln_b", "gmlp_w_s", "gmlp_b_s", "w_br_attn", "w_br_gmlp", "w_out", "w_ffn2_in",
             "w_ffn2_out"]
    outs = [loss, grad_x[None]]
    for k in range(4):
        outs += [res[nm][k] for nm in order]
    return tuple(outs)
```

```python
import math

import jax
import jax.numpy as jnp
import numpy as np
from jax import lax
from jax.experimental import pallas as pl
from jax.experimental.pallas import tpu as pltpu

F32 = jnp.float32
BF = jnp.bfloat16

N_DEV = 8
D = 1024
D_FF = 2816
FS = D_FF // 4
N_HEADS = 8
N_KV = 2
GROUP = 4
HD = 64
BLK = 128
Q_W = 512
KV_W = 128
G_W = 512
QKV_W = Q_W + 2 * KV_W
ZG_OFF = QKV_W
GATE_OFF = ZG_OFF + 2 * G_W
IN_W = GATE_OFF + 2 * D
N_BUCKETS = 32
MAX_DISTANCE = 128
EPS = 1e-6
NEG = -1e30
SCALE = HD ** -0.5
ADA_W = 9 * D // N_DEV

ADAM_LR = 0.001
ADAM_B1 = 0.9
ADAM_B2 = 0.999
ADAM_EPS = 1e-08
ADAM_WD = 0.01
ADAM_STEP = 10

CHUNK = 256
MIB = 1024 * 1024
MESH = pl.DeviceIdType.MESH
HIGH = lax.Precision.HIGHEST


def _cp(n_grid, vmem_mib):
    return pltpu.CompilerParams(dimension_semantics=("arbitrary",) * n_grid,
                                vmem_limit_bytes=vmem_mib * MIB)


def _const(shape):
    return pl.BlockSpec(shape, lambda *_: (0,) * len(shape))


def _resident(shape):
    return pl.BlockSpec(shape, lambda *_: (0,) * len(shape), pipeline_mode=pl.Buffered(1))


def _behind(body, n_in, after):
    k = len(after)
    return (lambda *refs: body(*refs[:n_in], *refs[n_in + k:])), [pl.BlockSpec(memory_space=pl.ANY)] * k


def _in_hbm(*arrays):
    return [pltpu.with_memory_space_constraint(a, pltpu.HBM) for a in arrays]


def _sds(shape, dtype):
    return jax.ShapeDtypeStruct(shape, dtype)


def _dot(a, b):
    return jnp.dot(a, b, preferred_element_type=F32)


def _dot_nt(a, b):
    return lax.dot_general(a, b, (((1,), (1,)), ((), ())), preferred_element_type=F32)


def _dot_tn(a, b):
    return lax.dot_general(a, b, (((0,), (0,)), ((), ())), preferred_element_type=F32)


def _rms_r(x):
    return lax.rsqrt(jnp.mean(x * x, axis=-1, keepdims=True) + EPS)


def _colsum(x):
    return jnp.sum(x, axis=0, keepdims=True)


def _prenorm(x, gp, sc, sh):
    return (x * _rms_r(x) * gp) * (1.0 + sc) + sh


def _prenorm_bwd(dn, x, gp, sc):
    r = _rms_r(x)
    xh = x * r
    t = dn * (1.0 + sc) * gp
    dx = r * (t - xh * jnp.mean(t * xh, axis=-1, keepdims=True))
    return dx, _colsum(dn), _colsum(dn * xh * gp), _colsum(dn * (1.0 + sc) * xh)


def _postnorm_bwd(dh, y, gate, gp, res):
    y = y.astype(F32)
    r = _rms_r(y)
    yh = y * r
    dyn = (res * gate) * dh
    t = dyn * gp
    dy = r * (t - yh * jnp.mean(t * yh, axis=-1, keepdims=True))
    return dy, _colsum(res * dh * yh * gp), _colsum(dyn * yh)


def _gelu(x):
    k = math.sqrt(2.0 / math.pi)
    return 0.5 * x * (1.0 + jnp.tanh(k * (x + 0.044715 * x * x * x)))


def _gelu_grad(x):
    k = math.sqrt(2.0 / math.pi)
    t = jnp.tanh(k * (x + 0.044715 * x * x * x))
    return 0.5 * (1.0 + t) + 0.5 * x * (1.0 - t * t) * (k * (1.0 + 3.0 * 0.044715 * x * x))


def _my_place():
    x, y, c = lax.axis_index("x"), lax.axis_index("y"), lax.axis_index("c")
    return x, y, c, 4 * x + 2 * y + c


def _peer(x, y, c, k):
    px = 1 - x if k & 4 else x
    py = 1 - y if k & 2 else y
    pc = 1 - c if k & 1 else c
    return (px, py, pc), 4 * px + 2 * py + pc


HBM_SPEC = pl.BlockSpec(memory_space=pltpu.HBM)
SEM_SPEC = pl.BlockSpec(memory_space=pltpu.SEMAPHORE)
EFFECT = pltpu.SideEffectType.DATAFLOW_SIDE_EFFECTING


RELATIONS = {"exchange": (1, 2, 3, 4, 5, 6, 7), "gather": (1, 2, 4, 6), "forward": (2, 4, 6),
             "gather_all": (1, 2, 3, 4, 5, 6, 7)}


def _slab_copies(mode, srcs, lands, send, recv, loc):
    x, y, c, me = _my_place()
    rel = RELATIONS[mode]
    remote, local = [], []
    for t in range(len(lands)):
        for i, k in enumerate(rel):
            peer, peer_lin = _peer(x, y, c, k)
            if mode == "exchange":
                src, dst, to = srcs[t].at[peer_lin], lands[t].at[me], peer
            elif mode in ("gather", "gather_all"):
                src, dst, to = srcs[t], lands[t].at[me], peer
            else:
                src, dst, to = lands[t].at[peer_lin], lands[t].at[peer_lin], _peer(x, y, c, 1)[0]
            remote.append(pltpu.make_async_remote_copy(
                src_ref=src, dst_ref=dst, send_sem=send.at[t * len(rel) + i], recv_sem=recv.at[t * len(rel) + i],
                device_id=to, device_id_type=MESH))
        if mode == "exchange":
            local.append(pltpu.make_async_copy(srcs[t].at[me], lands[t].at[me], loc.at[t]))
        elif mode in ("gather", "gather_all"):
            local.append(pltpu.make_async_copy(srcs[t], lands[t].at[me], loc.at[t]))
    return remote, local


def _slabs_start(mode, arrays, after, name):
    n = len(arrays)
    if mode == "forward":
        thru = list(arrays)
    else:
        shapes = [a.shape if mode == "exchange" else (N_DEV,) + a.shape for a in arrays]
        thru = list(arrays) + [lax.empty(s, a.dtype) for s, a in zip(shapes, arrays)]
    m = len(thru)
    n_sem = n * len(RELATIONS[mode])

    def body(*refs):
        srcs, lands = refs[:n], refs[m - n:m]
        send, recv, loc = refs[m + 1:m + 4]
        remote, local = _slab_copies(mode, srcs, lands, send, recv, loc)
        for cp in remote + local:
            cp.start()
        refs[-1][...] = jnp.zeros_like(refs[-1])

    return pl.pallas_call(
        body, name=name,
        out_shape=(pltpu.SemaphoreType.DMA((n_sem,)), pltpu.SemaphoreType.DMA((n_sem,)),
                   pltpu.SemaphoreType.DMA((n,)),
                   *[pltpu.HBM(a.shape, a.dtype) for a in thru],
                   _sds((1, D), F32)),
        in_specs=[HBM_SPEC] * m + [pl.BlockSpec(memory_space=pl.ANY)],
        out_specs=(SEM_SPEC, SEM_SPEC, SEM_SPEC, *[HBM_SPEC] * m, pl.BlockSpec(memory_space=pltpu.VMEM)),
        input_output_aliases={t: 3 + t for t in range(m)},
        compiler_params=pltpu.CompilerParams(has_side_effects=EFFECT),
    )(*[pltpu.with_memory_space_constraint(a, pltpu.HBM) for a in thru], after)


def _slabs_wait(mode, n, started, after, name):
    sems = started[0:3]
    thru = started[3:-1]
    m = len(thru)

    def body(*refs):
        srcs, lands = refs[:n], refs[m - n:m]
        remote, local = _slab_copies(mode, srcs, lands, *refs[m:m + 3])
        for cp in remote:
            cp.wait_send()
            cp.wait_recv()
        for cp in local:
            cp.wait()

    res = pl.pallas_call(
        body, name=name,
        out_shape=tuple(pltpu.HBM(a.shape, a.dtype) for a in thru),
        in_specs=[HBM_SPEC] * m + [SEM_SPEC] * 3 + [pl.BlockSpec(memory_space=pl.ANY)],
        out_specs=tuple([HBM_SPEC] * m),
        input_output_aliases={t: t for t in range(m)},
        compiler_params=pltpu.CompilerParams(has_side_effects=EFFECT),
    )(*thru, *sems, after)
    return list(res[m - n:m])


def _ada_forward(small8, w_ada, b_ada64):
    sw = small8.shape[1]

    def body(sm_ref, w_ref, b_ref, gath_ref, ada_ref, part_ref, send1, recv1, send2, recv2):
        x, y, c, me = _my_place()
        row_me = pl.multiple_of(me * 8, 8)
        gath_ref[pl.ds(row_me, 8), :] = sm_ref[...]
        first = []
        for k in range(1, N_DEV):
            peer, _ = _peer(x, y, c, k)
            cp = pltpu.make_async_remote_copy(
                src_ref=sm_ref, dst_ref=gath_ref.at[pl.ds(row_me, 8), :], send_sem=send1.at[k - 1],
                recv_sem=recv1.at[k - 1], device_id=peer, device_id_type=MESH)
            cp.start()
            first.append(cp)
        for cp in first:
            cp.wait()
        cs = gath_ref[:, 0:D]
        cs = cs * jax.nn.sigmoid(cs)
        part_ref[...] = jnp.dot(cs, w_ref[...], preferred_element_type=F32, precision=HIGH)
        ada_ref[pl.ds(row_me, 8), :] = part_ref[pl.ds(row_me, 8), :]
        second = []
        for k in range(1, N_DEV):
            peer, peer_lin = _peer(x, y, c, k)
            cp = pltpu.make_async_remote_copy(
                src_ref=part_ref.at[pl.ds(pl.multiple_of(peer_lin * 8, 8), 8), :],
                dst_ref=ada_ref.at[pl.ds(row_me, 8), :], send_sem=send2.at[k - 1],
                recv_sem=recv2.at[k - 1], device_id=peer, device_id_type=MESH)
            cp.start()
            second.append(cp)
        for cp in second:
            cp.wait()
        ada_ref[...] = ada_ref[...] + b_ref[...]

    vm = pl.BlockSpec(memory_space=pltpu.VMEM)
    return pl.pallas_call(
        body, name="ada_forward",
        out_shape=[_sds((8 * N_DEV, sw), F32), _sds((8 * N_DEV, ADA_W), F32)],
        in_specs=[vm, vm, vm], out_specs=[vm, vm],
        scratch_shapes=[pltpu.VMEM((8 * N_DEV, ADA_W), F32)] + [pltpu.SemaphoreType.DMA((7,))] * 4,
        compiler_params=pltpu.CompilerParams(vmem_limit_bytes=32 * MIB),
    )(small8, w_ada, b_ada64)


def _sum_slabs(land):
    def body(l_ref, o_ref):
        acc = l_ref[0]
        for j in range(1, N_DEV):
            acc = acc + l_ref[j]
        o_ref[...] = acc

    vm = pl.BlockSpec(memory_space=pltpu.VMEM)
    return pl.pallas_call(body, name="sum_slabs", out_shape=_sds(land.shape[1:], F32), in_specs=[vm], out_specs=vm,
                          compiler_params=pltpu.CompilerParams(vmem_limit_bytes=32 * MIB))(land)


def _small_allreduce(vectors):
    n = len(vectors)

    def body(*refs):
        v_refs, (sum_ref, gath_ref, pack, send, recv) = refs[:n], refs[n:]
        x, y, c, me = _my_place()
        for k in range(n):
            pack[k:k + 1, :] = v_refs[k][...]
        gath_ref[me] = pack[...]
        cps = []
        for k in range(1, N_DEV):
            peer, _ = _peer(x, y, c, k)
            cp = pltpu.make_async_remote_copy(
                src_ref=pack, dst_ref=gath_ref.at[me], send_sem=send.at[k - 1],
                recv_sem=recv.at[k - 1], device_id=peer, device_id_type=MESH)
            cp.start()
            cps.append(cp)
        for cp in cps:
            cp.wait()
        acc = gath_ref[0]
        for j in range(1, N_DEV):
            acc = acc + gath_ref[j]
        sum_ref[...] = acc

    vm = pl.BlockSpec(memory_space=pltpu.VMEM)
    return pl.pallas_call(
        body, name="small_allreduce",
        out_shape=[_sds((n, D), F32), _sds((N_DEV, n, D), F32)],
        in_specs=[vm] * n, out_specs=[vm, vm],
        scratch_shapes=[pltpu.VMEM((n, D), F32), pltpu.SemaphoreType.DMA((7,)), pltpu.SemaphoreType.DMA((7,))],
    )(*vectors)


F_TILES = tuple((f0, min(512, D_FF - f0)) for f0 in range(0, D_FF, 512))
F_TILES_NARROW = tuple((f0, 256) for f0 in range(0, D_FF, 256))


def _swiglu_tile(n, wt_ref, f0, tf):
    g = _dot_nt(n, wt_ref[f0:f0 + tf, :])
    u = _dot_nt(n, wt_ref[D_FF + f0:D_FF + f0 + tf, :])
    sg = jax.nn.sigmoid(g)
    silu = g * sg
    return (u * (sg * (1.0 + g * (1.0 - sg)))).astype(BF), silu.astype(BF), (silu * u).astype(BF)


def _ffn_in(h, sh, sc, gp, wt, name):
    S = h.shape[0]
    R = min(512, S)

    def body(h_ref, sh_ref, sc_ref, gp_ref, w_ref, n_ref, dg_ref, sl_ref, a_ref):
        for r0 in range(0, R, CHUNK):
            rows = slice(r0, r0 + CHUNK)
            n = _prenorm(h_ref[rows, :], gp_ref[...], sc_ref[...], sh_ref[...]).astype(BF)
            n_ref[rows, :] = n
            for f0, tf in F_TILES_NARROW:
                dg_ref[rows, f0:f0 + tf], sl_ref[rows, f0:f0 + tf], a_ref[rows, f0:f0 + tf] = _swiglu_tile(
                    n, w_ref, f0, tf)

    vec = _const((1, D))
    rows_ = lambda w_: pl.BlockSpec((R, w_), lambda i: (i, 0))
    return pl.pallas_call(
        body, name=name, grid=(S // R,),
        out_shape=[_sds((S, D), BF)] + [_sds((S, D_FF), BF)] * 3,
        in_specs=[rows_(D), vec, vec, vec, _resident((2 * D_FF, D))],
        out_specs=[rows_(D), rows_(D_FF), rows_(D_FF), rows_(D_FF)],
        compiler_params=_cp(1, 56),
    )(*_in_hbm(h), sh, sc, gp, *_in_hbm(wt))


def _ffn_out(a, w, h, gate, gp, name, target=None):
    S = h.shape[0]
    R = min(1024, S)
    with_loss = target is not None

    def body(a_ref, w_ref, h_ref, gate_ref, gp_ref, *rest):
        if with_loss:
            t_ref, out_ref, y_ref, tot_ref = rest

            @pl.when(pl.program_id(0) == 0)
            def _():
                tot_ref[...] = jnp.zeros_like(tot_ref)
        else:
            out_ref, y_ref = rest
        for r0 in range(0, R, CHUNK):
            rows = slice(r0, r0 + CHUNK)
            y = _dot(a_ref[rows, :], w_ref[...])
            y_ref[rows, :] = y.astype(BF)
            hn = h_ref[rows, :] + (0.5 * gate_ref[...]) * (y * _rms_r(y) * gp_ref[...])
            if with_loss:
                e = hn - t_ref[rows, :]
                out_ref[rows, :] = e * (1.0 / D)
                tot_ref[...] += jnp.sum(jnp.sum(e * e, axis=1, keepdims=True), axis=0, keepdims=True)
            else:
                out_ref[rows, :] = hn

    vec = _const((1, D))
    rows_ = lambda w_: pl.BlockSpec((R, w_), lambda i: (i, 0))
    return pl.pallas_call(
        body, name=name, grid=(S // R,),
        out_shape=[_sds((S, D), F32), _sds((S, D), BF)] + ([_sds((1, 1), F32)] if with_loss else []),
        in_specs=[rows_(D_FF), _resident((D_FF, D)), rows_(D), vec, vec] + ([rows_(D)] if with_loss else []),
        out_specs=[rows_(D), rows_(D)] + ([_const((1, 1))] if with_loss else []),
        compiler_params=_cp(1, 56),
    )(*_in_hbm(a, w, h), gate, gp, *(_in_hbm(target) if with_loss else ()))


def _ffn_out_bwd(dh, y, dsilu_u, silu, w, gate, gp, name, after=()):
    S = dh.shape[0]
    R = min(512, S)

    def body(dh_ref, y_ref, g_ref, u_ref, w_ref, gate_ref, gp_ref, dy_ref, dgu_ref, dgate_ref, dgp_ref):
        @pl.when(pl.program_id(0) == 0)
        def _():
            dgate_ref[...] = jnp.zeros_like(dgate_ref)
            dgp_ref[...] = jnp.zeros_like(dgp_ref)
        for r0 in range(0, R, CHUNK):
            rows = slice(r0, r0 + CHUNK)
            dy, dgate, dgp = _postnorm_bwd(dh_ref[rows, :], y_ref[rows, :], gate_ref[...], gp_ref[...], 0.5)
            dgate_ref[...] += dgate
            dgp_ref[...] += dgp
            dyb = dy.astype(BF)
            dy_ref[rows, :] = dyb
            for f0, tf in F_TILES:
                da = _dot_nt(dyb, w_ref[f0:f0 + tf, :])
                dgu_ref[rows, f0:f0 + tf] = (da * g_ref[rows, f0:f0 + tf].astype(F32)).astype(BF)
                dgu_ref[rows, D_FF + f0:D_FF + f0 + tf] = (da * u_ref[rows, f0:f0 + tf].astype(F32)).astype(BF)

    vec = _const((1, D))
    rows_ = lambda w_: pl.BlockSpec((R, w_), lambda i: (i, 0))
    body, after_specs = _behind(body, 7, after)
    return pl.pallas_call(
        body, name=name, grid=(S // R,),
        out_shape=[_sds((S, D), BF), _sds((S, 2 * D_FF), BF), _sds((1, D), F32), _sds((1, D), F32)],
        in_specs=[rows_(D), rows_(D), rows_(D_FF), rows_(D_FF), _resident((D_FF, D)), vec, vec] + after_specs,
        out_specs=[rows_(D), rows_(2 * D_FF), vec, vec],
        compiler_params=_cp(1, 56),
    )(*_in_hbm(dh, y, dsilu_u, silu, w), gate, gp, *after)


def _ffn_dn(dgu, wt, h, dh, sc, gp, name, after=()):
    S = h.shape[0]
    R = min(512, S)

    def body(dgu_ref, w_ref, h_ref, dh_ref, sc_ref, gp_ref, out_ref, dsh_ref, dsc_ref, dgp_ref):
        @pl.when(pl.program_id(0) == 0)
        def _():
            dsh_ref[...] = jnp.zeros_like(dsh_ref)
            dsc_ref[...] = jnp.zeros_like(dsc_ref)
            dgp_ref[...] = jnp.zeros_like(dgp_ref)

        for r0 in range(0, R, CHUNK):
            rows = slice(r0, r0 + CHUNK)
            dn = _dot(dgu_ref[rows, :], w_ref[...])
            dx, dsh, dsc, dgp = _prenorm_bwd(dn, h_ref[rows, :], gp_ref[...], sc_ref[...])
            out_ref[rows, :] = dh_ref[rows, :] + dx
            dsh_ref[...] += dsh
            dsc_ref[...] += dsc
            dgp_ref[...] += dgp

    vec = _const((1, D))
    rows_ = lambda w_: pl.BlockSpec((R, w_), lambda i: (i, 0))
    body, after_specs = _behind(body, 6, after)
    return pl.pallas_call(
        body, name=name, grid=(S // R,),
        out_shape=[_sds((S, D), F32)] + [_sds((1, D), F32)] * 3,
        in_specs=[rows_(2 * D_FF), _resident((2 * D_FF, D)), rows_(D), rows_(D), vec, vec] + after_specs,
        out_specs=[rows_(D), vec, vec, vec],
        compiler_params=_cp(1, 56),
    )(*_in_hbm(dgu, wt, h, dh), sc, gp, *after)


def _ffn_bwd(dh, y, dsilu_u, silu, w, wt, h, gate, gpost, sc, gpre, name):
    S = dh.shape[0]
    R = min(256, S)

    def body(dh_ref, y_ref, g_ref, u_ref, w_ref, wt_ref, h_ref, gate_ref, gpost_ref, sc_ref, gpre_ref,
             dy_ref, dgu_ref, out_ref, dgate_ref, dgpost_ref, dsh_ref, dsc_ref, dgpre_ref):
        @pl.when(pl.program_id(0) == 0)
        def _():
            for r in (dgate_ref, dgpost_ref, dsh_ref, dsc_ref, dgpre_ref):
                r[...] = jnp.zeros_like(r)
        dhh = dh_ref[...]
        dy, dgate, dgpost = _postnorm_bwd(dhh, y_ref[...], gate_ref[...], gpost_ref[...], 0.5)
        dgate_ref[...] += dgate
        dgpost_ref[...] += dgpost
        dyb = dy.astype(BF)
        dy_ref[...] = dyb
        for f0, tf in F_TILES:
            da = _dot_nt(dyb, w_ref[f0:f0 + tf, :])
            dgu_ref[:, f0:f0 + tf] = (da * g_ref[:, f0:f0 + tf].astype(F32)).astype(BF)
            dgu_ref[:, D_FF + f0:D_FF + f0 + tf] = (da * u_ref[:, f0:f0 + tf].astype(F32)).astype(BF)
        dn = _dot(dgu_ref[...], wt_ref[...])
        dx, dsh, dsc, dgpre = _prenorm_bwd(dn, h_ref[...], gpre_ref[...], sc_ref[...])
        out_ref[...] = dhh + dx
        dsh_ref[...] += dsh
        dsc_ref[...] += dsc
        dgpre_ref[...] += dgpre

    vec = _const((1, D))
    rows_ = lambda w_: pl.BlockSpec((R, w_), lambda i: (i, 0))
    return pl.pallas_call(
        body, name=name, grid=(S // R,),
        out_shape=[_sds((S, D), BF), _sds((S, 2 * D_FF), BF), _sds((S, D), F32)] + [_sds((1, D), F32)] * 5,
        in_specs=[rows_(D), rows_(D), rows_(D_FF), rows_(D_FF), _resident((D_FF, D)), _resident((2 * D_FF, D)),
                  rows_(D), vec, vec, vec, vec],
        out_specs=[rows_(D), rows_(2 * D_FF), rows_(D)] + [vec] * 5,
        compiler_params=_cp(1, 56),
    )(*_in_hbm(dh, y, dsilu_u, silu, w, wt, h), gate, gpost, sc, gpre)


def _tn_matmul(a, b, name, tm=None):
    S, M_all = a.shape
    N = b.shape[1]
    M = M_all if tm is None else tm
    GA = M_all // M
    ts = min(2048 if M * N <= 2 * D * D else 1024, S)
    nk = S // ts
    chunks = [(m0, min(CHUNK, M - m0)) for m0 in range(0, M, CHUNK)]

    def body(a_ref, b_ref, o_ref, acc):
        k = pl.program_id(1)

        @pl.when(k == 0)
        def _():
            acc[...] = jnp.zeros_like(acc)

        for m0, mc in chunks:
            acc[m0:m0 + mc, :] += _dot_tn(a_ref[:, m0:m0 + mc], b_ref[...])

        @pl.when(k == nk - 1)
        def _():
            for m0, mc in chunks:
                o_ref[m0:m0 + mc, :] = acc[m0:m0 + mc, :].astype(BF)

    return pl.pallas_call(
        body, name=name, grid=(GA, nk),
        out_shape=_sds((M_all, N), BF),
        in_specs=[pl.BlockSpec((ts, M), lambda ga, k: (k, ga)), pl.BlockSpec((ts, N), lambda ga, k: (k, 0))],
        out_specs=pl.BlockSpec((M, N), lambda ga, k: (ga, 0)),
        scratch_shapes=[pltpu.VMEM((M, N), F32)],
        compiler_params=_cp(2, 56),
    )(*_in_hbm(a, b))


def _mix_in(h, sh, sc, gp, w, wq):
    S = h.shape[0]
    R = min(512, S)

    def body(h_ref, sh_ref, sc_ref, gp_ref, w_ref, wq_ref, n_ref, qkv_ref, zg_ref, gates_ref):
        for r0 in range(0, R, CHUNK):
            rows = slice(r0, r0 + CHUNK)
            nb = _prenorm(h_ref[rows, :], gp_ref[...], sc_ref[...], sh_ref[...]).astype(BF)
            n_ref[rows, :] = nb
            qkv_ref[rows, 0:Q_W] = _dot_nt(nb, wq_ref[...]).astype(BF)
            qkv_ref[rows, Q_W:QKV_W] = _dot_nt(nb, w_ref[Q_W:QKV_W, :]).astype(BF)
            zg_ref[rows, :] = _dot_nt(nb, w_ref[ZG_OFF:GATE_OFF, :]).astype(BF)
            gates_ref[rows, :] = jax.nn.sigmoid(_dot_nt(nb, w_ref[GATE_OFF:IN_W, :])).astype(BF)

    vec = _const((1, D))
    rows = lambda w_: pl.BlockSpec((R, w_), lambda i: (i, 0))
    return pl.pallas_call(
        body, name="mix_in", grid=(S // R,),
        out_shape=[_sds((S, D), BF), _sds((S, QKV_W), BF), _sds((S, 2 * G_W), BF), _sds((S, 2 * D), BF)],
        in_specs=[rows(D), vec, vec, vec, _resident((IN_W, D)), _resident((Q_W, D))],
        out_specs=[rows(D), rows(QKV_W), rows(2 * G_W), rows(2 * D)],
        compiler_params=_cp(1, 48),
    )(*_in_hbm(h), sh, sc, gp, *_in_hbm(w, wq))


def _bias_table(rel_bias, bucket):
    def body(rel_ref, bk_ref, out_ref):
        bk = bk_ref[...]
        qi = lax.broadcasted_iota(jnp.int32, (BLK, 2 * BLK), 0)
        kj = lax.broadcasted_iota(jnp.int32, (BLK, 2 * BLK), 1)
        dist = qi + BLK - kj
        window = (dist >= 0) & (dist < BLK)
        for h in range(N_HEADS):
            acc = jnp.zeros((BLK, 2 * BLK), F32)
            for b in range(N_BUCKETS):
                acc = jnp.where(bk == b, rel_ref[b, h], acc)
            out_ref[h // GROUP, pl.ds((h % GROUP) * BLK, BLK), :] = jnp.where(window, acc, NEG)

    return pl.pallas_call(
        body, name="bias_table",
        out_shape=_sds((N_KV, GROUP * BLK, 2 * BLK), F32),
        in_specs=[pl.BlockSpec(memory_space=pltpu.SMEM), pl.BlockSpec(memory_space=pltpu.VMEM)],
        out_specs=pl.BlockSpec(memory_space=pltpu.VMEM),
    )(rel_bias, bucket)


ATT_TB = 8


HEAD_ROWS = N_HEADS * BLK


def _pair_heads(w):
    return jnp.transpose(w.reshape(N_KV, GROUP, HD, w.shape[1]), (1, 0, 2, 3)).reshape(w.shape)


def _unpair_heads(w):
    return jnp.transpose(w.reshape(GROUP, N_KV, HD, w.shape[1]), (1, 0, 2, 3)).reshape(w.shape)


def _halves(x, scale=1.0):
    low = lax.broadcasted_iota(jnp.int32, x.shape, 1) < HD
    xf = x.astype(F32) * scale
    return jnp.where(low, xf, 0.0).astype(BF), jnp.where(low, 0.0, xf).astype(BF)


def _stack_heads(x, scale=1.0):
    halves = [_halves(x[:, g * 128:(g + 1) * 128], scale) for g in range(GROUP)]
    return jnp.concatenate([lo for lo, _ in halves] + [hi for _, hi in halves], axis=0)


def _attn_probs(q, kvc, kvp, bias_ref, sink_ref, has_prev):
    kv2 = jnp.concatenate([kvp, kvc], axis=0)
    kboth, vboth = kv2[:, 0:KV_W], kv2[:, KV_W:2 * KV_W]
    qpad = _stack_heads(q, SCALE)
    s = _dot_nt(qpad, kboth) + bias_ref[...]
    if has_prev is not None:
        col = lax.broadcasted_iota(jnp.int32, (HEAD_ROWS, 2 * BLK), 1)
        s = jnp.where((col >= BLK) | has_prev, s, NEG)
    row_head = lax.broadcasted_iota(jnp.int32, (HEAD_ROWS, 1), 0) // BLK
    sink = jnp.zeros((HEAD_ROWS, 1), F32)
    for h in range(N_HEADS):
        sink = jnp.where(row_head == h, sink_ref[h], sink)
    m = jnp.maximum(jnp.max(s, axis=1, keepdims=True), sink)
    p = jnp.exp(s - m)
    e_sink = jnp.exp(sink - m)
    inv = 1.0 / (jnp.sum(p, axis=1, keepdims=True) + e_sink)
    return qpad, kboth, vboth, p * inv, e_sink * inv


def _attn_fwd(qkv, bias, sinks):
    S = qkv.shape[0]
    tb = min(ATT_TB, S // BLK)
    T = tb * BLK

    def body(sink_ref, q_ref, kv_ref, kvp_ref, bias_ref, o_ref):
        step = pl.program_id(0)
        for j in range(tb):
            rows = slice(j * BLK, (j + 1) * BLK)
            kvp = kvp_ref[...] if j == 0 else kv_ref[(j - 1) * BLK:j * BLK, :]
            has_prev = (step > 0) if j == 0 else None
            _, _, vboth, prob, _ = _attn_probs(q_ref[rows, :], kv_ref[rows, :], kvp, bias_ref, sink_ref, has_prev)
            pb = prob.astype(BF)
            v_low, v_high = _halves(vboth)
            half = HEAD_ROWS // 2
            o = _dot(pb[0:half], v_low) + _dot(pb[half:HEAD_ROWS], v_high)
            for g in range(GROUP):
                o_ref[rows, g * 128:(g + 1) * 128] = o[g * BLK:(g + 1) * BLK].astype(BF)

    return pl.pallas_call(
        body, name="attn_fwd", grid=(S // T,),
        out_shape=_sds((S, Q_W), BF),
        in_specs=[pl.BlockSpec(memory_space=pltpu.SMEM),
                  pl.BlockSpec((T, Q_W), lambda i: (i, 0)),
                  pl.BlockSpec((T, 2 * KV_W), lambda i: (i, 2)),
                  pl.BlockSpec((BLK, 2 * KV_W), lambda i: (jnp.maximum(i * tb - 1, 0), 2)),
                  _const((HEAD_ROWS, 2 * BLK))],
        out_specs=pl.BlockSpec((T, Q_W), lambda i: (i, 0)),
        compiler_params=_cp(1, 32),
    )(sinks, *_in_hbm(qkv, qkv, qkv, bias))


def _attn_bwd(qkv, bias, sinks, do):
    S = qkv.shape[0]
    tb = min(ATT_TB, S // BLK)
    T = tb * BLK
    nt = S // T
    half = HEAD_ROWS // 2

    def body(sink_ref, q_ref, kv_ref, kvp_ref, bias_ref, do_ref, dq_ref, dkv_ref, dbias_ref, dsink_ref, carry):
        i = pl.program_id(0)

        @pl.when(i == 0)
        def _():
            carry[...] = jnp.zeros_like(carry)
            dbias_ref[...] = jnp.zeros_like(dbias_ref)
            dsink_ref[...] = jnp.zeros_like(dsink_ref)

        from_next = carry[...]
        head_row = lax.broadcasted_iota(jnp.int32, (N_HEADS, 128), 0)
        low = lax.broadcasted_iota(jnp.int32, (BLK, 128), 1) < HD
        for j in reversed(range(tb)):
            rows = slice(j * BLK, (j + 1) * BLK)
            kvp = kvp_ref[...] if j == 0 else kv_ref[(j - 1) * BLK:j * BLK, :]
            has_prev = (i < nt - 1) if j == 0 else None
            qpad, kboth, vboth, prob, p_sink = _attn_probs(q_ref[rows, :], kv_ref[rows, :], kvp, bias_ref, sink_ref,
                                                           has_prev)
            pb = prob.astype(BF)
            dopad = _stack_heads(do_ref[rows, :])
            dp = _dot_nt(dopad, vboth)
            delta = jnp.sum(prob * dp, axis=1, keepdims=True)
            ds = prob * (dp - delta)
            dbias_ref[...] += ds
            sink_term = p_sink * delta
            dsink_rows = jnp.zeros((N_HEADS, 128), F32)
            for h in range(N_HEADS):
                val = -jnp.sum(sink_term[h * BLK:(h + 1) * BLK], axis=0, keepdims=True)
                dsink_rows = jnp.where(head_row == h, val, dsink_rows)
            dsink_ref[...] += dsink_rows
            dsb = ds.astype(BF)
            dqpad = _dot(dsb, kboth) * SCALE
            for g in range(GROUP):
                dq_ref[rows, g * 128:(g + 1) * 128] = jnp.where(
                    low, dqpad[g * BLK:(g + 1) * BLK], dqpad[half + g * BLK:half + (g + 1) * BLK]).astype(BF)
            dkv2 = jnp.concatenate([jnp.transpose(_dot_tn(qpad, dsb)),
                                    jnp.transpose(_dot_tn(dopad, pb))], axis=1)
            dkv_ref[rows, :] = (dkv2[BLK:2 * BLK] + from_next).astype(BF)
            from_next = dkv2[0:BLK]
        carry[...] = from_next

    return pl.pallas_call(
        body, name="attn_bwd", grid=(nt,),
        out_shape=[_sds((S, Q_W), BF), _sds((S, 2 * KV_W), BF),
                   _sds((HEAD_ROWS, 2 * BLK), F32), _sds((N_HEADS, 128), F32)],
        in_specs=[pl.BlockSpec(memory_space=pltpu.SMEM),
                  pl.BlockSpec((T, Q_W), lambda i: (nt - 1 - i, 0)),
                  pl.BlockSpec((T, 2 * KV_W), lambda i: (nt - 1 - i, 2)),
                  pl.BlockSpec((BLK, 2 * KV_W), lambda i: (jnp.maximum((nt - 1 - i) * tb - 1, 0), 2)),
                  _const((HEAD_ROWS, 2 * BLK)),
                  pl.BlockSpec((T, Q_W), lambda i: (nt - 1 - i, 0))],
        out_specs=[pl.BlockSpec((T, Q_W), lambda i: (nt - 1 - i, 0)),
                   pl.BlockSpec((T, 2 * KV_W), lambda i: (nt - 1 - i, 0)),
                   _const((HEAD_ROWS, 2 * BLK)), _const((N_HEADS, 128))],
        scratch_shapes=[pltpu.VMEM((BLK, 2 * KV_W), F32)],
        compiler_params=_cp(1, 32),
    )(sinks, *_in_hbm(qkv, qkv, qkv, bias, do))


def _rel_bias_grad(dbias, bucket):
    def body(db_ref, bk_ref, out_ref):
        bk = bk_ref[...]
        lane = lax.broadcasted_iota(jnp.int32, (1, 128), 1)
        for h in range(N_HEADS):
            d = db_ref[h // GROUP, pl.ds((h % GROUP) * BLK, BLK), :]
            row = jnp.zeros((1, 128), F32)
            for b in range(N_BUCKETS):
                tot = jnp.sum(jnp.sum(jnp.where(bk == b, d, 0.0), axis=1, keepdims=True), axis=0, keepdims=True)
                row = jnp.where(lane == b, tot, row)
            out_ref[pl.ds(h, 1), :] = row

    vm = pl.BlockSpec(memory_space=pltpu.VMEM)
    return pl.pallas_call(body, name="rel_bias_grad", out_shape=_sds((N_HEADS, 128), F32),
                          in_specs=[vm, vm], out_specs=vm)(dbias, bucket)


def _gmlp_parts(zg, lg_ref, lb_ref):
    z = zg.astype(F32)
    ge = _gelu(z)
    u, vg = ge[:, 0:G_W], ge[:, G_W:2 * G_W]
    mu = jnp.mean(vg, axis=-1, keepdims=True)
    xc = vg - mu
    rstd = lax.rsqrt(jnp.mean(xc * xc, axis=-1, keepdims=True) + EPS)
    xh = xc * rstd
    return z, u, xh, rstd, xh * lg_ref[...] + lb_ref[...]


def _causal_weights(ws_ref, wc):
    t = lax.broadcasted_iota(jnp.int32, (BLK, BLK), 0)
    s = lax.broadcasted_iota(jnp.int32, (BLK, BLK), 1)
    for g in range(N_HEADS):
        wc[g] = jnp.where(s <= t, ws_ref[g], 0.0).astype(BF)


def _spatial(vb, wc, bst_ref, p, low):
    xp = vb[:, p * 128:(p + 1) * 128]
    s0 = _dot(wc[2 * p], xp) + bst_ref[:, 2 * p:2 * p + 1]
    s1 = _dot(wc[2 * p + 1], xp) + bst_ref[:, 2 * p + 1:2 * p + 2]
    return xp, jnp.where(low, s0, s1)


def _gmlp_fwd(zg, lg, lb, ws, bst):
    S = zg.shape[0]
    tb = min(ATT_TB, S // BLK)
    T = tb * BLK

    def body(zg_ref, lg_ref, lb_ref, ws_ref, bst_ref, o_ref, wc):
        @pl.when(pl.program_id(0) == 0)
        def _():
            _causal_weights(ws_ref, wc)
        low = lax.broadcasted_iota(jnp.int32, (BLK, 128), 1) < HD
        for j in range(tb):
            rows = slice(j * BLK, (j + 1) * BLK)
            _, u, _, _, vln = _gmlp_parts(zg_ref[rows, :], lg_ref, lb_ref)
            vb = vln.astype(BF)
            for p in range(4):
                _, sp = _spatial(vb, wc, bst_ref, p, low)
                o_ref[rows, p * 128:(p + 1) * 128] = (u[:, p * 128:(p + 1) * 128] * sp).astype(BF)

    return pl.pallas_call(
        body, name="gmlp_fwd", grid=(S // T,),
        out_shape=_sds((S, G_W), BF),
        in_specs=[pl.BlockSpec((T, 2 * G_W), lambda i: (i, 0)), _const((1, G_W)), _const((1, G_W)),
                  _const((N_HEADS, BLK, BLK)), _const((BLK, N_HEADS))],
        out_specs=pl.BlockSpec((T, G_W), lambda i: (i, 0)),
        scratch_shapes=[pltpu.VMEM((N_HEADS, BLK, BLK), BF)],
        compiler_params=_cp(1, 32),
    )(*_in_hbm(zg), lg, lb, ws, bst)


def _gmlp_bwd(zg, d_out, lg, lb, ws, bst):
    S = zg.shape[0]
    tb = min(ATT_TB, S // BLK)
    T = tb * BLK
    nb = S // T

    def body(zg_ref, d_ref, lg_ref, lb_ref, ws_ref, bst_ref, dzg_ref, dws_ref, dbs_ref, dlg_ref, dlb_ref, wc, dbacc):
        i = pl.program_id(0)

        @pl.when(i == 0)
        def _():
            _causal_weights(ws_ref, wc)
            dws_ref[...] = jnp.zeros_like(dws_ref)
            dlg_ref[...] = jnp.zeros_like(dlg_ref)
            dlb_ref[...] = jnp.zeros_like(dlb_ref)
            dbacc[...] = jnp.zeros_like(dbacc)

        low = lax.broadcasted_iota(jnp.int32, (BLK, 128), 1) < HD
        for j in range(tb):
            rows = slice(j * BLK, (j + 1) * BLK)
            z, u, xh, rstd, vln = _gmlp_parts(zg_ref[rows, :], lg_ref, lb_ref)
            vb = vln.astype(BF)
            d = d_ref[rows, :].astype(F32)
            du_parts, dvln_parts = [], []
            for p in range(4):
                xp, sp = _spatial(vb, wc, bst_ref, p, low)
                dp = d[:, p * 128:(p + 1) * 128]
                du_parts.append(dp * sp)
                dsp = dp * u[:, p * 128:(p + 1) * 128]
                dbacc[:, p * 128:(p + 1) * 128] += dsp
                d0 = jnp.where(low, dsp, 0.0).astype(BF)
                d1 = jnp.where(low, 0.0, dsp).astype(BF)
                dws_ref[2 * p] += _dot_nt(d0, xp)
                dws_ref[2 * p + 1] += _dot_nt(d1, xp)
                dvln_parts.append(_dot_tn(wc[2 * p], d0) + _dot_tn(wc[2 * p + 1], d1))
            dvln = jnp.concatenate(dvln_parts, axis=1)
            dlg_ref[...] += _colsum(dvln * xh)
            dlb_ref[...] += _colsum(dvln)
            dxh = dvln * lg_ref[...]
            dvg = rstd * (dxh - jnp.mean(dxh, axis=-1, keepdims=True)
                          - xh * jnp.mean(dxh * xh, axis=-1, keepdims=True))
            dge = jnp.concatenate(du_parts + [dvg], axis=1)
            dzg_ref[rows, :] = (dge * _gelu_grad(z)).astype(BF)

        @pl.when(i == nb - 1)
        def _():
            t = lax.broadcasted_iota(jnp.int32, (BLK, BLK), 0)
            s = lax.broadcasted_iota(jnp.int32, (BLK, BLK), 1)
            for g in range(N_HEADS):
                dws_ref[g] = jnp.where(s <= t, dws_ref[g], 0.0)
            grp = lax.broadcasted_iota(jnp.int32, (N_HEADS, G_W), 0)
            lane = lax.broadcasted_iota(jnp.int32, (N_HEADS, G_W), 1) // HD
            pick = jnp.where(grp == lane, 1.0, 0.0).astype(F32)
            dbs_ref[...] = lax.dot_general(pick, dbacc[...], (((1,), (1,)), ((), ())),
                                           preferred_element_type=F32, precision=HIGH)

    return pl.pallas_call(
        body, name="gmlp_bwd", grid=(nb,),
        out_shape=[_sds((S, 2 * G_W), BF), _sds((N_HEADS, BLK, BLK), F32), _sds((N_HEADS, BLK), F32),
                   _sds((1, G_W), F32), _sds((1, G_W), F32)],
        in_specs=[pl.BlockSpec((T, 2 * G_W), lambda i: (i, 0)), pl.BlockSpec((T, G_W), lambda i: (i, 0)),
                  _const((1, G_W)), _const((1, G_W)), _const((N_HEADS, BLK, BLK)), _const((BLK, N_HEADS))],
        out_specs=[pl.BlockSpec((T, 2 * G_W), lambda i: (i, 0)), _const((N_HEADS, BLK, BLK)),
                   _const((N_HEADS, BLK)), _const((1, G_W)), _const((1, G_W))],
        scratch_shapes=[pltpu.VMEM((N_HEADS, BLK, BLK), BF), pltpu.VMEM((BLK, G_W), F32)],
        compiler_params=_cp(1, 32),
    )(*_in_hbm(zg, d_out), lg, lb, ws, bst)


def _mix_out(o, gm, gates, h, wa, wg, wo, gate, gp, after=()):
    S = h.shape[0]
    R = min(512, S)

    def body(o_ref, gm_ref, gates_ref, h_ref, wa_ref, wg_ref, wo_ref, gate_ref, gp_ref,
             ya_ref, yg_ref, ym_ref, y_ref, hn_ref):
        for r0 in range(0, R, CHUNK):
            rows = slice(r0, r0 + CHUNK)
            ya = _dot(o_ref[rows, :], wa_ref[...])
            yg = _dot(gm_ref[rows, :], wg_ref[...])
            ya_ref[rows, :] = ya.astype(BF)
            yg_ref[rows, :] = yg.astype(BF)
            ym = (gates_ref[rows, 0:D].astype(F32) * ya + gates_ref[rows, D:2 * D].astype(F32) * yg).astype(BF)
            ym_ref[rows, :] = ym
            y = _dot(ym, wo_ref[...])
            y_ref[rows, :] = y.astype(BF)
            hn_ref[rows, :] = h_ref[rows, :] + gate_ref[...] * (y * _rms_r(y) * gp_ref[...])

    vec = _const((1, D))
    rows = lambda w_: pl.BlockSpec((R, w_), lambda i: (i, 0))
    body, after_specs = _behind(body, 9, after)
    return pl.pallas_call(
        body, name="mix_out", grid=(S // R,),
        out_shape=[_sds((S, D), BF)] * 4 + [_sds((S, D), F32)],
        in_specs=[rows(Q_W), rows(G_W), rows(2 * D), rows(D), _resident((Q_W, D)), _resident((G_W, D)),
                  _resident((D, D)), vec, vec] + after_specs,
        out_specs=[rows(D)] * 5,
        compiler_params=_cp(1, 48),
    )(*_in_hbm(o, gm, gates, h, wa, wg, wo), gate, gp, *after)


def _mix_out_bwd(dh, y, ya, yg, gates, att, gm, ymix, wa, wg, wo, gate, gp, after=()):
    S = dh.shape[0]
    R = min(512, S)
    nb = S // R

    def body(dh_ref, y_ref, ya_ref, yg_ref, gates_ref, att_ref, gm_ref, ym_ref, wa_ref, wg_ref, wo_ref,
             gate_ref, gp_ref, dz_ref, do_ref, dgm_ref, dgate_ref, dgp_ref, gwo_ref, gwa_ref, gwg_ref,
             acc_o, acc_a, acc_g, dy_scr, dya_scr, dyg_scr):
        i = pl.program_id(0)

        @pl.when(i == 0)
        def _():
            for r in (dgate_ref, dgp_ref, acc_o, acc_a, acc_g):
                r[...] = jnp.zeros_like(r)
        for r0 in range(0, R, 2 * CHUNK):
            rows = slice(r0, min(r0 + 2 * CHUNK, R))
            dy, dgate, dgp = _postnorm_bwd(dh_ref[rows, :], y_ref[rows, :], gate_ref[...], gp_ref[...], 1.0)
            dgate_ref[...] += dgate
            dgp_ref[...] += dgp
            dyb = dy.astype(BF)
            dy_scr[rows, :] = dyb
            dym = _dot_nt(dyb, wo_ref[...])
            ga = gates_ref[rows, 0:D].astype(F32)
            gg = gates_ref[rows, D:2 * D].astype(F32)
            dya = (dym * ga).astype(BF)
            dyg = (dym * gg).astype(BF)
            dya_scr[rows, :] = dya
            dyg_scr[rows, :] = dyg
            dz_ref[rows, 0:D] = (dym * ya_ref[rows, :].astype(F32) * (ga * (1.0 - ga))).astype(BF)
            dz_ref[rows, D:2 * D] = (dym * yg_ref[rows, :].astype(F32) * (gg * (1.0 - gg))).astype(BF)
            do_ref[rows, :] = _dot_nt(dya, wa_ref[...]).astype(BF)
            dgm_ref[rows, :] = _dot_nt(dyg, wg_ref[...]).astype(BF)
        for m0 in range(0, D, CHUNK):
            acc_o[m0:m0 + CHUNK, :] += _dot_tn(ym_ref[:, m0:m0 + CHUNK], dy_scr[...])
        for m0 in range(0, Q_W, CHUNK):
            acc_a[m0:m0 + CHUNK, :] += _dot_tn(att_ref[:, m0:m0 + CHUNK], dya_scr[...])
            acc_g[m0:m0 + CHUNK, :] += _dot_tn(gm_ref[:, m0:m0 + CHUNK], dyg_scr[...])

        @pl.when(i == nb - 1)
        def _():
            for m0 in range(0, D, CHUNK):
                gwo_ref[m0:m0 + CHUNK, :] = acc_o[m0:m0 + CHUNK, :].astype(BF)
            for m0 in range(0, Q_W, CHUNK):
                gwa_ref[m0:m0 + CHUNK, :] = acc_a[m0:m0 + CHUNK, :].astype(BF)
                gwg_ref[m0:m0 + CHUNK, :] = acc_g[m0:m0 + CHUNK, :].astype(BF)

    vec = _const((1, D))
    rows = lambda w_: pl.BlockSpec((R, w_), lambda i: (i, 0))
    body, after_specs = _behind(body, 13, after)
    return pl.pallas_call(
        body, name="mix_out_bwd", grid=(nb,),
        out_shape=[_sds((S, 2 * D), BF), _sds((S, Q_W), BF), _sds((S, G_W), BF), _sds((1, D), F32),
                   _sds((1, D), F32), _sds((D, D), BF), _sds((Q_W, D), BF), _sds((G_W, D), BF)],
        in_specs=[rows(D), rows(D), rows(D), rows(D), rows(2 * D), rows(Q_W), rows(G_W), rows(D),
                  _resident((Q_W, D)), _resident((G_W, D)), _resident((D, D)), vec, vec] + after_specs,
        out_specs=[rows(2 * D), rows(Q_W), rows(G_W), vec, vec, _const((D, D)), _const((Q_W, D)),
                   _const((G_W, D))],
        scratch_shapes=[pltpu.VMEM((D, D), F32), pltpu.VMEM((Q_W, D), F32), pltpu.VMEM((G_W, D), F32)]
        + [pltpu.VMEM((R, D), BF)] * 3,
        compiler_params=_cp(1, 60),
    )(*_in_hbm(dh, y, ya, yg, gates, att, gm, ymix, wa, wg, wo), gate, gp, *after)


def _mix_dn(dq, dkv, dzg, dzgate, w, wq, h, dh, sc, gp, after=()):
    S = h.shape[0]
    R = min(512, S)

    def body(dq_ref, dkv_ref, dzg_ref, dzt_ref, w_ref, wq_ref, h_ref, dh_ref, sc_ref, gp_ref,
             out_ref, dsh_ref, dsc_ref, dgp_ref):
        @pl.when(pl.program_id(0) == 0)
        def _():
            dsh_ref[...] = jnp.zeros_like(dsh_ref)
            dsc_ref[...] = jnp.zeros_like(dsc_ref)
            dgp_ref[...] = jnp.zeros_like(dgp_ref)
        for r0 in range(0, R, CHUNK):
            rows = slice(r0, r0 + CHUNK)
            dn = _dot(dq_ref[rows, :], wq_ref[...])
            dn = dn + _dot(dkv_ref[rows, :], w_ref[Q_W:QKV_W, :])
            dn = dn + _dot(dzg_ref[rows, :], w_ref[ZG_OFF:GATE_OFF, :])
            dn = dn + _dot(dzt_ref[rows, :], w_ref[GATE_OFF:IN_W, :])
            dx, dsh, dsc, dgp = _prenorm_bwd(dn, h_ref[rows, :], gp_ref[...], sc_ref[...])
            out_ref[rows, :] = dh_ref[rows, :] + dx
            dsh_ref[...] += dsh
            dsc_ref[...] += dsc
            dgp_ref[...] += dgp

    vec = _const((1, D))
    rows = lambda w_: pl.BlockSpec((R, w_), lambda i: (i, 0))
    body, after_specs = _behind(body, 10, after)
    return pl.pallas_call(
        body, name="mix_dn", grid=(S // R,),
        out_shape=[_sds((S, D), F32)] + [_sds((1, D), F32)] * 3,
        in_specs=[rows(Q_W), rows(2 * KV_W), rows(2 * G_W), rows(2 * D), _resident((IN_W, D)),
                  _resident((Q_W, D)), rows(D), rows(D), vec, vec] + after_specs,
        out_specs=[rows(D), vec, vec, vec],
        compiler_params=_cp(1, 48),
    )(*_in_hbm(dq, dkv, dzg, dzgate, w, wq, h, dh), sc, gp, *after)


def _adamw_math(w, g, m, v):
    m2 = ADAM_B1 * m + (1.0 - ADAM_B1) * g
    v2 = ADAM_B2 * v + (1.0 - ADAM_B2) * (g * g)
    m_hat = m2 / (1.0 - ADAM_B1 ** ADAM_STEP)
    v_hat = v2 / (1.0 - ADAM_B2 ** ADAM_STEP)
    delta = -ADAM_LR * (m_hat / (jnp.sqrt(v_hat) + ADAM_EPS) + ADAM_WD * w)
    return delta, m2, v2


def _row_tile(rows, cols):
    best = None
    for t in range(16, rows + 1, 16):
        if rows % t == 0 and t * cols <= 256 * 1024:
            best = t
    return best if best is not None else rows


def _adamw_sharded(landing, w, m, v, name):
    r, c = w.shape
    tr = _row_tile(r, c)

    def body(l_ref, w_ref, m_ref, v_ref, g_ref, d_ref, m2_ref, v2_ref):
        g = l_ref[0].astype(F32)
        for j in range(1, N_DEV):
            g = g + l_ref[j].astype(F32)
        delta, m2, v2 = _adamw_math(w_ref[...], g, m_ref[...], v_ref[...])
        g_ref[...] = g
        d_ref[...] = delta
        m2_ref[...] = m2
        v2_ref[...] = v2

    row = pl.BlockSpec((tr, c), lambda i: (i, 0))
    return pl.pallas_call(
        body, name=name, grid=(r // tr,),
        out_shape=[_sds((r, c), F32)] * 4,
        in_specs=[pl.BlockSpec((N_DEV, tr, c), lambda i: (0, i, 0)), row, row, row],
        out_specs=[row] * 4,
        compiler_params=_cp(1, 48),
    )(*_in_hbm(landing, w, m, v))


def _adamw_small(items):
    n = len(items)

    def body(*refs):
        for k in range(n):
            w_ref, g_ref, m_ref, v_ref = refs[4 * k:4 * k + 4]
            outs = refs[4 * n + 3 * k:4 * n + 3 * k + 3]
            for o_ref, val in zip(outs, _adamw_math(w_ref[...], g_ref[...], m_ref[...], v_ref[...])):
                o_ref[...] = val

    vm = pl.BlockSpec(memory_space=pltpu.VMEM)
    flat = pl.pallas_call(
        body, name="adamw_small",
        out_shape=[_sds(it[0].shape, F32) for it in items for _ in range(3)],
        in_specs=[vm] * (4 * n), out_specs=[vm] * (3 * n),
    )(*[a for it in items for a in it])
    return [tuple(flat[3 * k:3 * k + 3]) for k in range(n)]


def _w_ada_update(c8, d_ada, w, m, v):
    tr = 256

    def body(c_ref, d_ref, w_ref, m_ref, v_ref, g_ref, dl_ref, m2_ref, v2_ref):
        cs = c_ref[...]
        cs = cs * jax.nn.sigmoid(cs)
        g = lax.dot_general(cs, d_ref[...], (((0,), (0,)), ((), ())), preferred_element_type=F32, precision=HIGH)
        delta, m2, v2 = _adamw_math(w_ref[...], g, m_ref[...], v_ref[...])
        g_ref[...] = g
        dl_ref[...] = delta
        m2_ref[...] = m2
        v2_ref[...] = v2

    row = pl.BlockSpec((tr, ADA_W), lambda i: (i, 0))
    return pl.pallas_call(
        body, name="w_ada_update", grid=(D // tr,),
        out_shape=[_sds((D, ADA_W), F32)] * 4,
        in_specs=[pl.BlockSpec((N_DEV, tr), lambda i: (0, i)), _const((N_DEV, ADA_W)), row, row, row],
        out_specs=[row] * 4,
        compiler_params=_cp(1, 40),
    )(c8, d_ada, *_in_hbm(w, m, v))


def _t5_bucket():
    qi = np.arange(BLK, dtype=np.int32)[:, None]
    kj = np.arange(2 * BLK, dtype=np.int32)[None, :]
    dist = np.maximum(qi + BLK - kj, 0)
    max_exact = N_BUCKETS // 2
    d_f = np.maximum(dist, max_exact).astype(np.float32)
    large = max_exact + (np.log(d_f / np.float32(max_exact)) / np.float32(math.log(MAX_DISTANCE / max_exact))
                         * np.float32(N_BUCKETS - max_exact)).astype(np.int32)
    large = np.minimum(large, N_BUCKETS - 1)
    return jnp.asarray(np.where(dist < max_exact, dist, large).astype(np.int32))


def _slabs_of_columns(w):
    r, c8 = w.shape
    return jnp.transpose(w.reshape(r, N_DEV, c8 // N_DEV), (1, 0, 2))


def _columns_of_slabs(w8):
    _, r, c = w8.shape
    return jnp.transpose(w8, (1, 0, 2)).reshape(r, N_DEV * c)


def kernel(x, c, rel_bias, w_ada, b_ada, pre_norm_g, post_norm_g, w_ffn1_in, w_ffn1_out, w_in, sinks, gmlp_ln_g, gmlp_ln_b, gmlp_w_s, gmlp_b_s, w_br_attn, w_br_gmlp, w_out, w_ffn2_in, w_ffn2_out, loss_target, m_rel_bias, m_w_ada, m_b_ada, m_pre_norm_g, m_post_norm_g, m_w_ffn1_in, m_w_ffn1_out, m_w_in, m_sinks, m_gmlp_ln_g, m_gmlp_ln_b, m_gmlp_w_s, m_gmlp_b_s, m_w_br_attn, m_w_br_gmlp, m_w_out, m_w_ffn2_in, m_w_ffn2_out, v_rel_bias, v_w_ada, v_b_ada, v_pre_norm_g, v_post_norm_g, v_w_ffn1_in, v_w_ffn1_out, v_w_in, v_sinks, v_gmlp_ln_g, v_gmlp_ln_b, v_gmlp_w_s, v_gmlp_b_s, v_w_br_attn, v_w_br_gmlp, v_w_out, v_w_ffn2_in, v_w_ffn2_out):
    me = 4 * lax.axis_index("x") + 2 * lax.axis_index("y") + lax.axis_index("c")
    x0 = x[0]
    target = loss_target[0]

    transposed = ("w_ffn1_in", "w_in", "w_ffn2_in")
    shards = [w_ffn1_in[0].T, w_ffn1_out[0], w_in[0].T, w_br_attn[0], w_br_gmlp[0], w_out[0],
              w_ffn2_in[0].T, w_ffn2_out[0]]
    shards_bf = [s.astype(BF) for s in shards]
    groups = [shards_bf[0:1], shards_bf[1:6], shards_bf[6:8]]

    def gather_start(i, after):
        return _slabs_start("gather", groups[i], after, "gather_start_%d" % i)

    def forward_start(st, i, after):
        lands = _slabs_wait("gather", len(groups[i]), st, after, "gather_wait_%d" % i)
        return _slabs_start("forward", lands, c, "forward_start_%d" % i)

    def gathered(st, i, after):
        return _slabs_wait("forward", len(groups[i]), st, after, "forward_wait_%d" % i)

    gs0 = gather_start(0, c)

    mine = jnp.concatenate([c[0], pre_norm_g[0].reshape(-1), post_norm_g[0].reshape(-1)])
    small8 = jnp.broadcast_to(mine[None, :], (8, mine.shape[0]))
    b_ada64 = jnp.repeat(b_ada.reshape(N_DEV, ADA_W), 8, axis=0)
    gath, ada64 = _ada_forward(small8, w_ada[0], b_ada64)
    gath8 = gath[::8]
    ada = ada64[::8].reshape(9, D)
    sh1, sc1, g1, sh2, sc2, g2, sh3, sc3, g3 = [ada[k:k + 1] for k in range(9)]
    gains = gath8[:, D:].reshape(N_DEV, 2, 3, 128)
    pre_g = jnp.transpose(gains[:, 0], (1, 0, 2)).reshape(3, D)
    post_g = jnp.transpose(gains[:, 1], (1, 0, 2)).reshape(3, D)
    pre = [pre_g[k:k + 1] for k in range(3)]
    post = [post_g[k:k + 1] for k in range(3)]

    bucket = _t5_bucket()
    bias = _bias_table(rel_bias, bucket).reshape(HEAD_ROWS, 2 * BLK)
    sinks8 = sinks[0]
    lg, lb = gmlp_ln_g, gmlp_ln_b
    ws = gmlp_w_s[0]
    bst = jnp.transpose(gmlp_b_s[0])

    fs0 = forward_start(gs0, 0, sh1)
    gs1 = gather_start(1, fs0[-1])
    wf1_in = gathered(fs0, 0, gs1[-1])[0].reshape(2 * D_FF, D)
    n1, fg1, fu1, fa1 = _ffn_in(x0, sh1, sc1, pre[0], wf1_in, "ffn1_in")
    fs1 = forward_start(gs1, 1, n1)
    gs2 = gather_start(2, fs1[-1])
    mix_w = gathered(fs1, 1, gs2[-1])
    wf1_out = mix_w[0].reshape(D_FF, D)
    w_in_full = mix_w[1].reshape(IN_W, D)
    w_q = _pair_heads(w_in_full[0:Q_W])
    w_bra = _pair_heads(_columns_of_slabs(mix_w[2]))
    w_brg = _columns_of_slabs(mix_w[3])
    w_out_full = mix_w[4].reshape(D, D)
    h1, y1 = _ffn_out(fa1, wf1_out, x0, g1, post[0], "ffn1_out")
    n2, qkv, zg, gates = _mix_in(h1, sh2, sc2, pre[1], w_in_full, w_q)
    att = _attn_fwd(qkv, bias, sinks8)
    gm = _gmlp_fwd(zg, lg, lb, ws, bst)
    fs2 = forward_start(gs2, 2, gm)
    ya, yg, ymix, y2, h2 = _mix_out(att, gm, gates, h1, w_bra, w_brg, w_out_full, g2, post[1], after=(fs2[-1],))
    wf2_in, wf2_out = gathered(fs2, 2, h2)
    wf2_in = wf2_in.reshape(2 * D_FF, D)
    wf2_out = wf2_out.reshape(D_FF, D)
    n3, fg3, fu3, fa3 = _ffn_in(h2, sh3, sc3, pre[2], wf2_in, "ffn2_in")
    dh3, y3, sq = _ffn_out(fa3, wf2_out, h2, g3, post[2], "ffn2_out", target=target)

    def exchange_start(i, arrays):
        return _slabs_start("exchange", arrays, sq, "exchange_start_%d" % i)

    dy3, dgu3, dh2, d_g3, d_post2, d_sh3, d_sc3, d_pre2 = _ffn_bwd(
        dh3, y3, fg3, fu3, wf2_out, wf2_in, h2, g3, post[2], sc3, pre[2], "ffn2_bwd")
    gw_f2_out = _tn_matmul(fa3, dy3, "ffn2_out_wgrad", tm=D_FF // 2).reshape(N_DEV, D_FF // N_DEV, D)
    gw_f2_in = _tn_matmul(dgu3, n3, "ffn2_in_wgrad", tm=D_FF // 2).reshape(N_DEV, FS, D)
    ex1 = exchange_start(1, [gw_f2_out, gw_f2_in])

    dzgate, d_att, d_gm, d_g2, d_post1, gw_out, gw_bra, gw_brg = _mix_out_bwd(
        dh2, y2, ya, yg, gates, att, gm, ymix, w_bra, w_brg, w_out_full, g2, post[1], after=(ex1[-1],))
    ex2 = exchange_start(2, [_slabs_of_columns(_unpair_heads(gw_bra)), _slabs_of_columns(gw_brg),
                             gw_out.reshape(N_DEV, D // N_DEV, D)])
    dq, dkv, dbias, dsink = _attn_bwd(qkv, bias, sinks8, d_att)
    dzg, d_ws, d_bs, d_lg, d_lb = _gmlp_bwd(zg, d_gm, lg, lb, ws, bst)
    d_rel = _rel_bias_grad(dbias.reshape(N_KV, GROUP * BLK, 2 * BLK), bucket)
    early = jnp.concatenate([
        jnp.concatenate([d_lg.reshape(4, 128), d_lb.reshape(4, 128)], axis=0),
        d_bs, d_rel, dsink, d_ws.reshape(N_HEADS * BLK, BLK)], axis=0)
    sm0 = _slabs_start("gather_all", [early], sq, "small_gather_start")
    dh1, d_sh2, d_sc2, d_pre1 = _mix_dn(dq, dkv, dzg, dzgate, w_in_full, w_q, h1, dh2, sc2, pre[1],
                                        after=(ex2[-1], sm0[-1]))
    gw_in = jnp.concatenate(
        [_unpair_heads(_tn_matmul(dq, n2, "w_in_q_wgrad")), _tn_matmul(dkv, n2, "w_in_kv_wgrad"),
         _tn_matmul(dzg, n2, "w_in_zg_wgrad"), _tn_matmul(dzgate, n2, "w_in_gate_wgrad")],
        axis=0).reshape(N_DEV, IN_W // N_DEV, D)
    ex3 = exchange_start(3, [gw_in])

    dy1, dgu1, d_g1, d_post0 = _ffn_out_bwd(dh1, y1, fg1, fu1, wf1_out, g1, post[0], "ffn1_out_bwd",
                                            after=(ex3[-1],))
    gw_f1_out = _tn_matmul(fa1, dy1, "ffn1_out_wgrad", tm=D_FF // 2).reshape(N_DEV, D_FF // N_DEV, D)
    ex4 = exchange_start(4, [gw_f1_out])
    gw_f1_in = _tn_matmul(dgu1, n1, "ffn1_in_wgrad", tm=D_FF // 2).reshape(N_DEV, FS, D)
    ex5 = exchange_start(5, [gw_f1_in])
    grad_x, d_sh1, d_sc1, d_pre0 = _ffn_dn(dgu1, wf1_in, x0, dh1, sc1, pre[0], "ffn1_dn", after=(ex4[-1], ex5[-1]))

    landed = {}
    for i, (ex, nms) in enumerate([(ex1, ["w_ffn2_out", "w_ffn2_in"]),
                                   (ex2, ["w_br_attn", "w_br_gmlp", "w_out"]), (ex3, ["w_in"]),
                                   (ex4, ["w_ffn1_out"]), (ex5, ["w_ffn1_in"])]):
        for nm, land in zip(nms, _slabs_wait("exchange", len(nms), ex, grad_x, "exchange_wait_%d" % i)):
            landed[nm] = land
    moments = [(m_w_ffn1_in, v_w_ffn1_in), (m_w_ffn1_out, v_w_ffn1_out), (m_w_in, v_w_in),
               (m_w_br_attn, v_w_br_attn), (m_w_br_gmlp, v_w_br_gmlp), (m_w_out, v_w_out),
               (m_w_ffn2_in, v_w_ffn2_in), (m_w_ffn2_out, v_w_ffn2_out)]
    names = ["w_ffn1_in", "w_ffn1_out", "w_in", "w_br_attn", "w_br_gmlp", "w_out", "w_ffn2_in", "w_ffn2_out"]
    big = {}
    for nm, w_, (m_, v_) in zip(names, shards, moments):
        if nm in transposed:
            res4 = _adamw_sharded(landed[nm], w_, m_[0].T, v_[0].T, "adamw_" + nm)
            big[nm] = [a.T[None] for a in res4]
        else:
            big[nm] = [a[None] for a in _adamw_sharded(landed[nm], w_, m_[0], v_[0], "adamw_" + nm)]

    my_loss = jnp.broadcast_to(sq * (0.5 / D), (1, D))
    my_loss, _ = lax.optimization_barrier((my_loss, landed["w_ffn1_in"]))
    tot, every = _small_allreduce([d_sh1, d_sc1, d_g1, d_sh2, d_sc2, d_g2, d_sh3, d_sc3, d_g3,
                                   d_pre0, d_pre1, d_pre2, d_post0, d_post1, d_post2, my_loss])
    (early_land,) = _slabs_wait("gather_all", 1, sm0, grad_x, "small_gather_wait")
    tot_early = _sum_slabs(early_land)

    loss = tot[15, 0]
    g_b_ada = tot[0:9].reshape(1, 9 * D)
    g_pre = lax.dynamic_slice_in_dim(tot[9:12], 128 * me, 128, axis=1)[None]
    g_post = lax.dynamic_slice_in_dim(tot[12:15], 128 * me, 128, axis=1)[None]
    g_lg = tot_early[0:4].reshape(1, G_W)
    g_lb = tot_early[4:8].reshape(1, G_W)
    g_bs = tot_early[8:16][None]
    g_rel = jnp.transpose(tot_early[16:24, 0:N_BUCKETS])
    g_sinks = tot_early[24:32, 0][None]
    g_ws = tot_early[32:1056].reshape(1, N_HEADS, BLK, BLK)

    d_ada_mine = lax.dynamic_slice_in_dim(every[:, 0:9].reshape(N_DEV, 9 * D), ADA_W * me, ADA_W, axis=1)
    ada_out = [a[None] for a in _w_ada_update(gath8[:, 0:D], d_ada_mine, w_ada[0], m_w_ada[0], v_w_ada[0])]

    small = [("rel_bias", rel_bias, g_rel, m_rel_bias, v_rel_bias), ("b_ada", b_ada, g_b_ada, m_b_ada, v_b_ada),
             ("pre_norm_g", pre_norm_g, g_pre, m_pre_norm_g, v_pre_norm_g),
             ("post_norm_g", post_norm_g, g_post, m_post_norm_g, v_post_norm_g),
             ("sinks", sinks, g_sinks, m_sinks, v_sinks), ("gmlp_ln_g", gmlp_ln_g, g_lg, m_gmlp_ln_g, v_gmlp_ln_g),
             ("gmlp_ln_b", gmlp_ln_b, g_lb, m_gmlp_ln_b, v_gmlp_ln_b),
             ("gmlp_w_s", gmlp_w_s, g_ws, m_gmlp_w_s, v_gmlp_w_s), ("gmlp_b_s", gmlp_b_s, g_bs, m_gmlp_b_s, v_gmlp_b_s)]
    two_d = lambda a: a.reshape(int(math.prod(a.shape[:-1])), a.shape[-1])
    stepped = _adamw_small([tuple(two_d(a) for a in item[1:]) for item in small])
    res = {"w_ada": ada_out}
    for (nm, w_, g_, _, _), new in zip(small, stepped):
        res[nm] = [g_] + [a.reshape(w_.shape) for a in new]
    res.update(big)
    order = ["rel_bias", "w_ada", "b_ada", "pre_norm_g", "post_norm_g", "w_ffn1_in", "w_ffn1_out", "w_in", "sinks",
             "gmlp_ln_g", "gmlp_ln_b", "gmlp_w_s", "gmlp_b_s", "w_br_attn", "w_br_gmlp", "w_out", "w_ffn2_in",
             "w_ffn2_out"]
    outs = [loss, grad_x[None]]
    for k in range(4):
        outs += [res[nm][k] for nm in order]
    return tuple(outs)
```

```python
import math

import jax
import jax.numpy as jnp
import numpy as np
from jax import lax
from jax.experimental import pallas as pl
from jax.experimental.pallas import tpu as pltpu

F32 = jnp.float32
BF = jnp.bfloat16

N_DEV = 8
D = 1024
D_FF = 2816
FS = D_FF // 4
N_HEADS = 8
N_KV = 2
GROUP = 4
HD = 64
BLK = 128
Q_W = 512
KV_W = 128
G_W = 512
QKV_W = Q_W + 2 * KV_W
ZG_OFF = QKV_W
GATE_OFF = ZG_OFF + 2 * G_W
IN_W = GATE_OFF + 2 * D
N_BUCKETS = 32
MAX_DISTANCE = 128
EPS = 1e-6
NEG = -1e30
SCALE = HD ** -0.5
ADA_W = 9 * D // N_DEV

ADAM_LR = 0.001
ADAM_B1 = 0.9
ADAM_B2 = 0.999
ADAM_EPS = 1e-08
ADAM_WD = 0.01
ADAM_STEP = 10

CHUNK = 256
MIB = 1024 * 1024
MESH = pl.DeviceIdType.MESH
HIGH = lax.Precision.HIGHEST


def _cp(n_grid, vmem_mib):
    return pltpu.CompilerParams(dimension_semantics=("arbitrary",) * n_grid,
                                vmem_limit_bytes=vmem_mib * MIB)


def _const(shape):
    return pl.BlockSpec(shape, lambda *_: (0,) * len(shape))


def _resident(shape):
    return pl.BlockSpec(shape, lambda *_: (0,) * len(shape), pipeline_mode=pl.Buffered(1))


def _behind(body, n_in, after):
    k = len(after)
    return (lambda *refs: body(*refs[:n_in], *refs[n_in + k:])), [pl.BlockSpec(memory_space=pl.ANY)] * k


def _in_hbm(*arrays):
    return [pltpu.with_memory_space_constraint(a, pltpu.HBM) for a in arrays]


def _sds(shape, dtype):
    return jax.ShapeDtypeStruct(shape, dtype)


def _dot(a, b):
    return jnp.dot(a, b, preferred_element_type=F32)


def _dot_nt(a, b):
    return lax.dot_general(a, b, (((1,), (1,)), ((), ())), preferred_element_type=F32)


def _dot_tn(a, b):
    return lax.dot_general(a, b, (((0,), (0,)), ((), ())), preferred_element_type=F32)


def _rms_r(x):
    return lax.rsqrt(jnp.mean(x * x, axis=-1, keepdims=True) + EPS)


def _colsum(x):
    return jnp.sum(x, axis=0, keepdims=True)


def _prenorm(x, gp, sc, sh):
    return (x * _rms_r(x) * gp) * (1.0 + sc) + sh


def _prenorm_bwd(dn, x, gp, sc):
    r = _rms_r(x)
    xh = x * r
    t = dn * (1.0 + sc) * gp
    dx = r * (t - xh * jnp.mean(t * xh, axis=-1, keepdims=True))
    return dx, _colsum(dn), _colsum(dn * xh * gp), _colsum(dn * (1.0 + sc) * xh)


def _postnorm_bwd(dh, y, gate, gp, res):
    y = y.astype(F32)
    r = _rms_r(y)
    yh = y * r
    dyn = (res * gate) * dh
    t = dyn * gp
    dy = r * (t - yh * jnp.mean(t * yh, axis=-1, keepdims=True))
    return dy, _colsum(res * dh * yh * gp), _colsum(dyn * yh)


def _gelu(x):
    k = math.sqrt(2.0 / math.pi)
    return 0.5 * x * (1.0 + jnp.tanh(k * (x + 0.044715 * x * x * x)))


def _gelu_grad(x):
    k = math.sqrt(2.0 / math.pi)
    t = jnp.tanh(k * (x + 0.044715 * x * x * x))
    return 0.5 * (1.0 + t) + 0.5 * x * (1.0 - t * t) * (k * (1.0 + 3.0 * 0.044715 * x * x))


def _my_place():
    x, y, c = lax.axis_index("x"), lax.axis_index("y"), lax.axis_index("c")
    return x, y, c, 4 * x + 2 * y + c


def _peer(x, y, c, k):
    px = 1 - x if k & 4 else x
    py = 1 - y if k & 2 else y
    pc = 1 - c if k & 1 else c
    return (px, py, pc), 4 * px + 2 * py + pc


HBM_SPEC = pl.BlockSpec(memory_space=pltpu.HBM)
SEM_SPEC = pl.BlockSpec(memory_space=pltpu.SEMAPHORE)
EFFECT = pltpu.SideEffectType.DATAFLOW_SIDE_EFFECTING


RELATIONS = {"exchange": (1, 2, 3, 4, 5, 6, 7), "gather": (1, 2, 4, 6), "forward": (2, 4, 6),
             "gather_all": (1, 2, 3, 4, 5, 6, 7)}


def _slab_copies(mode, srcs, lands, send, recv, loc):
    x, y, c, me = _my_place()
    rel = RELATIONS[mode]
    remote, local = [], []
    for t in range(len(lands)):
        for i, k in enumerate(rel):
            peer, peer_lin = _peer(x, y, c, k)
            if mode == "exchange":
                src, dst, to = srcs[t].at[peer_lin], lands[t].at[me], peer
            elif mode in ("gather", "gather_all"):
                src, dst, to = srcs[t], lands[t].at[me], peer
            else:
                src, dst, to = lands[t].at[peer_lin], lands[t].at[peer_lin], _peer(x, y, c, 1)[0]
            remote.append(pltpu.make_async_remote_copy(
                src_ref=src, dst_ref=dst, send_sem=send.at[t * len(rel) + i], recv_sem=recv.at[t * len(rel) + i],
                device_id=to, device_id_type=MESH))
        if mode == "exchange":
            local.append(pltpu.make_async_copy(srcs[t].at[me], lands[t].at[me], loc.at[t]))
        elif mode in ("gather", "gather_all"):
            local.append(pltpu.make_async_copy(srcs[t], lands[t].at[me], loc.at[t]))
    return remote, local


def _slabs_start_many(parts, after, name):
    layout, thru = [], []
    for mode, arrays in parts:
        mine = list(arrays)
        if mode != "forward":
            shapes = [a.shape if mode == "exchange" else (N_DEV,) + a.shape for a in arrays]
            mine += [lax.empty(s, a.dtype) for s, a in zip(shapes, arrays)]
        layout.append((mode, len(arrays), len(thru), len(mine)))
        thru += mine
    m, n_parts = len(thru), len(parts)

    def body(*refs):
        sems = refs[m + 1:m + 1 + 3 * n_parts]
        for p, (mode, n, first, count) in enumerate(layout):
            srcs, lands = refs[first:first + n], refs[first + count - n:first + count]
            remote, local = _slab_copies(mode, srcs, lands, *sems[3 * p:3 * p + 3])
            for cp in remote + local:
                cp.start()
        refs[-1][...] = jnp.zeros_like(refs[-1])

    sem_shapes = []
    for mode, n, _, _ in layout:
        k = n * len(RELATIONS[mode])
        sem_shapes += [pltpu.SemaphoreType.DMA((k,)), pltpu.SemaphoreType.DMA((k,)), pltpu.SemaphoreType.DMA((n,))]
    res = pl.pallas_call(
        body, name=name,
        out_shape=(*sem_shapes, *[pltpu.HBM(a.shape, a.dtype) for a in thru], _sds((1, D), F32)),
        in_specs=[HBM_SPEC] * m + [pl.BlockSpec(memory_space=pl.ANY)],
        out_specs=(*[SEM_SPEC] * (3 * n_parts), *[HBM_SPEC] * m, pl.BlockSpec(memory_space=pltpu.VMEM)),
        input_output_aliases={t: 3 * n_parts + t for t in range(m)},
        compiler_params=pltpu.CompilerParams(has_side_effects=EFFECT),
    )(*[pltpu.with_memory_space_constraint(a, pltpu.HBM) for a in thru], after)
    return [(*res[3 * p:3 * p + 3], *res[3 * n_parts + first:3 * n_parts + first + count], res[-1])
            for p, (_, _, first, count) in enumerate(layout)]


def _slabs_start(mode, arrays, after, name):
    return _slabs_start_many([(mode, arrays)], after, name)[0]


def _slabs_wait(mode, n, started, after, name):
    sems = started[0:3]
    thru = started[3:-1]
    m = len(thru)

    def body(*refs):
        srcs, lands = refs[:n], refs[m - n:m]
        remote, local = _slab_copies(mode, srcs, lands, *refs[m:m + 3])
        for cp in remote:
            cp.wait_send()
            cp.wait_recv()
        for cp in local:
            cp.wait()

    res = pl.pallas_call(
        body, name=name,
        out_shape=tuple(pltpu.HBM(a.shape, a.dtype) for a in thru),
        in_specs=[HBM_SPEC] * m + [SEM_SPEC] * 3 + [pl.BlockSpec(memory_space=pl.ANY)],
        out_specs=tuple([HBM_SPEC] * m),
        input_output_aliases={t: t for t in range(m)},
        compiler_params=pltpu.CompilerParams(has_side_effects=EFFECT),
    )(*thru, *sems, after)
    return list(res[m - n:m])


def _ada_forward(small8, w_ada, b_ada64):
    sw = small8.shape[1]

    def body(sm_ref, w_ref, b_ref, gath_ref, ada_ref, part_ref, send1, recv1, send2, recv2):
        x, y, c, me = _my_place()
        row_me = pl.multiple_of(me * 8, 8)
        gath_ref[pl.ds(row_me, 8), :] = sm_ref[...]
        first = []
        for k in range(1, N_DEV):
            peer, _ = _peer(x, y, c, k)
            cp = pltpu.make_async_remote_copy(
                src_ref=sm_ref, dst_ref=gath_ref.at[pl.ds(row_me, 8), :], send_sem=send1.at[k - 1],
                recv_sem=recv1.at[k - 1], device_id=peer, device_id_type=MESH)
            cp.start()
            first.append(cp)
        for cp in first:
            cp.wait()
        cs = gath_ref[:, 0:D]
        cs = cs * jax.nn.sigmoid(cs)
        part_ref[...] = jnp.dot(cs, w_ref[...], preferred_element_type=F32, precision=HIGH)
        ada_ref[pl.ds(row_me, 8), :] = part_ref[pl.ds(row_me, 8), :]
        second = []
        for k in range(1, N_DEV):
            peer, peer_lin = _peer(x, y, c, k)
            cp = pltpu.make_async_remote_copy(
                src_ref=part_ref.at[pl.ds(pl.multiple_of(peer_lin * 8, 8), 8), :],
                dst_ref=ada_ref.at[pl.ds(row_me, 8), :], send_sem=send2.at[k - 1],
                recv_sem=recv2.at[k - 1], device_id=peer, device_id_type=MESH)
            cp.start()
            second.append(cp)
        for cp in second:
            cp.wait()
        ada_ref[...] = ada_ref[...] + b_ref[...]

    vm = pl.BlockSpec(memory_space=pltpu.VMEM)
    return pl.pallas_call(
        body, name="ada_forward",
        out_shape=[_sds((8 * N_DEV, sw), F32), _sds((8 * N_DEV, ADA_W), F32)],
        in_specs=[vm, vm, vm], out_specs=[vm, vm],
        scratch_shapes=[pltpu.VMEM((8 * N_DEV, ADA_W), F32)] + [pltpu.SemaphoreType.DMA((7,))] * 4,
        compiler_params=pltpu.CompilerParams(vmem_limit_bytes=32 * MIB),
    )(small8, w_ada, b_ada64)


def _sum_slabs(land):
    def body(l_ref, o_ref):
        acc = l_ref[0]
        for j in range(1, N_DEV):
            acc = acc + l_ref[j]
        o_ref[...] = acc

    vm = pl.BlockSpec(memory_space=pltpu.VMEM)
    return pl.pallas_call(body, name="sum_slabs", out_shape=_sds(land.shape[1:], F32), in_specs=[vm], out_specs=vm,
                          compiler_params=pltpu.CompilerParams(vmem_limit_bytes=32 * MIB))(land)


def _small_allreduce(vectors):
    n = len(vectors)

    def body(*refs):
        v_refs, (sum_ref, gath_ref, pack, send, recv) = refs[:n], refs[n:]
        x, y, c, me = _my_place()
        for k in range(n):
            pack[k:k + 1, :] = v_refs[k][...]
        gath_ref[me] = pack[...]
        cps = []
        for k in range(1, N_DEV):
            peer, _ = _peer(x, y, c, k)
            cp = pltpu.make_async_remote_copy(
                src_ref=pack, dst_ref=gath_ref.at[me], send_sem=send.at[k - 1],
                recv_sem=recv.at[k - 1], device_id=peer, device_id_type=MESH)
            cp.start()
            cps.append(cp)
        for cp in cps:
            cp.wait()
        acc = gath_ref[0]
        for j in range(1, N_DEV):
            acc = acc + gath_ref[j]
        sum_ref[...] = acc

    vm = pl.BlockSpec(memory_space=pltpu.VMEM)
    return pl.pallas_call(
        body, name="small_allreduce",
        out_shape=[_sds((n, D), F32), _sds((N_DEV, n, D), F32)],
        in_specs=[vm] * n, out_specs=[vm, vm],
        scratch_shapes=[pltpu.VMEM((n, D), F32), pltpu.SemaphoreType.DMA((7,)), pltpu.SemaphoreType.DMA((7,))],
    )(*vectors)


F_TILES = tuple((f0, min(512, D_FF - f0)) for f0 in range(0, D_FF, 512))
F_TILES_NARROW = tuple((f0, 256) for f0 in range(0, D_FF, 256))


def _swiglu_tile(n, wt_ref, f0, tf):
    g = _dot_nt(n, wt_ref[f0:f0 + tf, :])
    u = _dot_nt(n, wt_ref[D_FF + f0:D_FF + f0 + tf, :])
    sg = jax.nn.sigmoid(g)
    silu = g * sg
    return (u * (sg * (1.0 + g * (1.0 - sg)))).astype(BF), silu.astype(BF), (silu * u).astype(BF)


def _ffn_in(h, sh, sc, gp, wt, name):
    S = h.shape[0]
    R = min(512, S)

    def body(h_ref, sh_ref, sc_ref, gp_ref, w_ref, n_ref, dg_ref, sl_ref, a_ref):
        for r0 in range(0, R, CHUNK):
            rows = slice(r0, r0 + CHUNK)
            n = _prenorm(h_ref[rows, :], gp_ref[...], sc_ref[...], sh_ref[...]).astype(BF)
            n_ref[rows, :] = n
            for f0, tf in F_TILES_NARROW:
                dg_ref[rows, f0:f0 + tf], sl_ref[rows, f0:f0 + tf], a_ref[rows, f0:f0 + tf] = _swiglu_tile(
                    n, w_ref, f0, tf)

    vec = _const((1, D))
    rows_ = lambda w_: pl.BlockSpec((R, w_), lambda i: (i, 0))
    return pl.pallas_call(
        body, name=name, grid=(S // R,),
        out_shape=[_sds((S, D), BF)] + [_sds((S, D_FF), BF)] * 3,
        in_specs=[rows_(D), vec, vec, vec, _resident((2 * D_FF, D))],
        out_specs=[rows_(D), rows_(D_FF), rows_(D_FF), rows_(D_FF)],
        compiler_params=_cp(1, 56),
    )(*_in_hbm(h), sh, sc, gp, *_in_hbm(wt))


def _ffn_out(a, w, h, gate, gp, name, target=None):
    S = h.shape[0]
    R = min(512, S)
    with_loss = target is not None

    def body(a_ref, w_ref, h_ref, gate_ref, gp_ref, *rest):
        if with_loss:
            t_ref, out_ref, y_ref, tot_ref = rest

            @pl.when(pl.program_id(0) == 0)
            def _():
                tot_ref[...] = jnp.zeros_like(tot_ref)
        else:
            out_ref, y_ref = rest
        for r0 in range(0, R, CHUNK):
            rows = slice(r0, r0 + CHUNK)
            y = _dot(a_ref[rows, :], w_ref[...])
            y_ref[rows, :] = y.astype(BF)
            hn = h_ref[rows, :] + (0.5 * gate_ref[...]) * (y * _rms_r(y) * gp_ref[...])
            if with_loss:
                e = hn - t_ref[rows, :]
                out_ref[rows, :] = e * (1.0 / D)
                tot_ref[...] += jnp.sum(jnp.sum(e * e, axis=1, keepdims=True), axis=0, keepdims=True)
            else:
                out_ref[rows, :] = hn

    vec = _const((1, D))
    rows_ = lambda w_: pl.BlockSpec((R, w_), lambda i: (i, 0))
    return pl.pallas_call(
        body, name=name, grid=(S // R,),
        out_shape=[_sds((S, D), F32), _sds((S, D), BF)] + ([_sds((1, 1), F32)] if with_loss else []),
        in_specs=[rows_(D_FF), _resident((D_FF, D)), rows_(D), vec, vec] + ([rows_(D)] if with_loss else []),
        out_specs=[rows_(D), rows_(D)] + ([_const((1, 1))] if with_loss else []),
        compiler_params=_cp(1, 48),
    )(*_in_hbm(a, w, h), gate, gp, *(_in_hbm(target) if with_loss else ()))


def _ffn_out_bwd(dh, y, dsilu_u, silu, w, gate, gp, name, after=()):
    S = dh.shape[0]
    R = min(512, S)

    def body(dh_ref, y_ref, g_ref, u_ref, w_ref, gate_ref, gp_ref, dy_ref, dgu_ref, dgate_ref, dgp_ref):
        @pl.when(pl.program_id(0) == 0)
        def _():
            dgate_ref[...] = jnp.zeros_like(dgate_ref)
            dgp_ref[...] = jnp.zeros_like(dgp_ref)
        for r0 in range(0, R, CHUNK):
            rows = slice(r0, r0 + CHUNK)
            dy, dgate, dgp = _postnorm_bwd(dh_ref[rows, :], y_ref[rows, :], gate_ref[...], gp_ref[...], 0.5)
            dgate_ref[...] += dgate
            dgp_ref[...] += dgp
            dyb = dy.astype(BF)
            dy_ref[rows, :] = dyb
            for f0, tf in F_TILES:
                da = _dot_nt(dyb, w_ref[f0:f0 + tf, :])
                dgu_ref[rows, f0:f0 + tf] = (da * g_ref[rows, f0:f0 + tf].astype(F32)).astype(BF)
                dgu_ref[rows, D_FF + f0:D_FF + f0 + tf] = (da * u_ref[rows, f0:f0 + tf].astype(F32)).astype(BF)

    vec = _const((1, D))
    rows_ = lambda w_: pl.BlockSpec((R, w_), lambda i: (i, 0))
    body, after_specs = _behind(body, 7, after)
    return pl.pallas_call(
        body, name=name, grid=(S // R,),
        out_shape=[_sds((S, D), BF), _sds((S, 2 * D_FF), BF), _sds((1, D), F32), _sds((1, D), F32)],
        in_specs=[rows_(D), rows_(D), rows_(D_FF), rows_(D_FF), _resident((D_FF, D)), vec, vec] + after_specs,
        out_specs=[rows_(D), rows_(2 * D_FF), vec, vec],
        compiler_params=_cp(1, 56),
    )(*_in_hbm(dh, y, dsilu_u, silu, w), gate, gp, *after)


def _ffn_dn(dgu, wt, h, dh, sc, gp, name, after=()):
    S = h.shape[0]
    R = min(512, S)

    def body(dgu_ref, w_ref, h_ref, dh_ref, sc_ref, gp_ref, out_ref, dsh_ref, dsc_ref, dgp_ref):
        @pl.when(pl.program_id(0) == 0)
        def _():
            dsh_ref[...] = jnp.zeros_like(dsh_ref)
            dsc_ref[...] = jnp.zeros_like(dsc_ref)
            dgp_ref[...] = jnp.zeros_like(dgp_ref)

        for r0 in range(0, R, CHUNK):
            rows = slice(r0, r0 + CHUNK)
            dn = _dot(dgu_ref[rows, :], w_ref[...])
            dx, dsh, dsc, dgp = _prenorm_bwd(dn, h_ref[rows, :], gp_ref[...], sc_ref[...])
            out_ref[rows, :] = dh_ref[rows, :] + dx
            dsh_ref[...] += dsh
            dsc_ref[...] += dsc
            dgp_ref[...] += dgp

    vec = _const((1, D))
    rows_ = lambda w_: pl.BlockSpec((R, w_), lambda i: (i, 0))
    body, after_specs = _behind(body, 6, after)
    return pl.pallas_call(
        body, name=name, grid=(S // R,),
        out_shape=[_sds((S, D), F32)] + [_sds((1, D), F32)] * 3,
        in_specs=[rows_(2 * D_FF), _resident((2 * D_FF, D)), rows_(D), rows_(D), vec, vec] + after_specs,
        out_specs=[rows_(D), vec, vec, vec],
        compiler_params=_cp(1, 56),
    )(*_in_hbm(dgu, wt, h, dh), sc, gp, *after)


def _ffn_bwd(dh, y, dsilu_u, silu, w, wt, h, gate, gpost, sc, gpre, name):
    S = dh.shape[0]
    R = min(256, S)

    def body(dh_ref, y_ref, g_ref, u_ref, w_ref, wt_ref, h_ref, gate_ref, gpost_ref, sc_ref, gpre_ref,
             dy_ref, dgu_ref, out_ref, dgate_ref, dgpost_ref, dsh_ref, dsc_ref, dgpre_ref):
        @pl.when(pl.program_id(0) == 0)
        def _():
            for r in (dgate_ref, dgpost_ref, dsh_ref, dsc_ref, dgpre_ref):
                r[...] = jnp.zeros_like(r)
        dhh = dh_ref[...]
        dy, dgate, dgpost = _postnorm_bwd(dhh, y_ref[...], gate_ref[...], gpost_ref[...], 0.5)
        dgate_ref[...] += dgate
        dgpost_ref[...] += dgpost
        dyb = dy.astype(BF)
        dy_ref[...] = dyb
        for f0, tf in F_TILES:
            da = _dot_nt(dyb, w_ref[f0:f0 + tf, :])
            dgu_ref[:, f0:f0 + tf] = (da * g_ref[:, f0:f0 + tf].astype(F32)).astype(BF)
            dgu_ref[:, D_FF + f0:D_FF + f0 + tf] = (da * u_ref[:, f0:f0 + tf].astype(F32)).astype(BF)
        dn = _dot(dgu_ref[...], wt_ref[...])
        dx, dsh, dsc, dgpre = _prenorm_bwd(dn, h_ref[...], gpre_ref[...], sc_ref[...])
        out_ref[...] = dhh + dx
        dsh_ref[...] += dsh
        dsc_ref[...] += dsc
        dgpre_ref[...] += dgpre

    vec = _const((1, D))
    rows_ = lambda w_: pl.BlockSpec((R, w_), lambda i: (i, 0))
    return pl.pallas_call(
        body, name=name, grid=(S // R,),
        out_shape=[_sds((S, D), BF), _sds((S, 2 * D_FF), BF), _sds((S, D), F32)] + [_sds((1, D), F32)] * 5,
        in_specs=[rows_(D), rows_(D), rows_(D_FF), rows_(D_FF), _resident((D_FF, D)), _resident((2 * D_FF, D)),
                  rows_(D), vec, vec, vec, vec],
        out_specs=[rows_(D), rows_(2 * D_FF), rows_(D)] + [vec] * 5,
        compiler_params=_cp(1, 56),
    )(*_in_hbm(dh, y, dsilu_u, silu, w, wt, h), gate, gpost, sc, gpre)


def _tn_matmul(a, b, name, tm=None):
    S, M_all = a.shape
    N = b.shape[1]
    M = M_all if tm is None else tm
    GA = M_all // M
    ts = min(2048 if M * N <= 2 * D * D else 1024, S)
    nk = S // ts
    chunks = [(m0, min(CHUNK, M - m0)) for m0 in range(0, M, CHUNK)]

    def body(a_ref, b_ref, o_ref, acc):
        k = pl.program_id(1)

        @pl.when(k == 0)
        def _():
            acc[...] = jnp.zeros_like(acc)

        for m0, mc in chunks:
            acc[m0:m0 + mc, :] += _dot_tn(a_ref[:, m0:m0 + mc], b_ref[...])

        @pl.when(k == nk - 1)
        def _():
            for m0, mc in chunks:
                o_ref[m0:m0 + mc, :] = acc[m0:m0 + mc, :].astype(BF)

    return pl.pallas_call(
        body, name=name, grid=(GA, nk),
        out_shape=_sds((M_all, N), BF),
        in_specs=[pl.BlockSpec((ts, M), lambda ga, k: (k, ga)), pl.BlockSpec((ts, N), lambda ga, k: (k, 0))],
        out_specs=pl.BlockSpec((M, N), lambda ga, k: (ga, 0)),
        scratch_shapes=[pltpu.VMEM((M, N), F32)],
        compiler_params=_cp(2, 56),
    )(*_in_hbm(a, b))


def _mix_in(h, sh, sc, gp, w, wq):
    S = h.shape[0]
    R = min(512, S)

    def body(h_ref, sh_ref, sc_ref, gp_ref, w_ref, wq_ref, n_ref, qkv_ref, zg_ref, gates_ref):
        for r0 in range(0, R, CHUNK):
            rows = slice(r0, r0 + CHUNK)
            nb = _prenorm(h_ref[rows, :], gp_ref[...], sc_ref[...], sh_ref[...]).astype(BF)
            n_ref[rows, :] = nb
            qkv_ref[rows, 0:Q_W] = _dot_nt(nb, wq_ref[...]).astype(BF)
            qkv_ref[rows, Q_W:QKV_W] = _dot_nt(nb, w_ref[Q_W:QKV_W, :]).astype(BF)
            zg_ref[rows, :] = _dot_nt(nb, w_ref[ZG_OFF:GATE_OFF, :]).astype(BF)
            gates_ref[rows, :] = jax.nn.sigmoid(_dot_nt(nb, w_ref[GATE_OFF:IN_W, :])).astype(BF)

    vec = _const((1, D))
    rows = lambda w_: pl.BlockSpec((R, w_), lambda i: (i, 0))
    return pl.pallas_call(
        body, name="mix_in", grid=(S // R,),
        out_shape=[_sds((S, D), BF), _sds((S, QKV_W), BF), _sds((S, 2 * G_W), BF), _sds((S, 2 * D), BF)],
        in_specs=[rows(D), vec, vec, vec, _resident((IN_W, D)), _resident((Q_W, D))],
        out_specs=[rows(D), rows(QKV_W), rows(2 * G_W), rows(2 * D)],
        compiler_params=_cp(1, 48),
    )(*_in_hbm(h), sh, sc, gp, *_in_hbm(w, wq))


def _bias_table(rel_bias, bucket):
    def body(rel_ref, bk_ref, out_ref):
        bk = bk_ref[...]
        qi = lax.broadcasted_iota(jnp.int32, (BLK, 2 * BLK), 0)
        kj = lax.broadcasted_iota(jnp.int32, (BLK, 2 * BLK), 1)
        dist = qi + BLK - kj
        window = (dist >= 0) & (dist < BLK)
        for h in range(N_HEADS):
            acc = jnp.zeros((BLK, 2 * BLK), F32)
            for b in range(N_BUCKETS):
                acc = jnp.where(bk == b, rel_ref[b, h], acc)
            out_ref[h // GROUP, pl.ds((h % GROUP) * BLK, BLK), :] = jnp.where(window, acc, NEG)

    return pl.pallas_call(
        body, name="bias_table",
        out_shape=_sds((N_KV, GROUP * BLK, 2 * BLK), F32),
        in_specs=[pl.BlockSpec(memory_space=pltpu.SMEM), pl.BlockSpec(memory_space=pltpu.VMEM)],
        out_specs=pl.BlockSpec(memory_space=pltpu.VMEM),
    )(rel_bias, bucket)


ATT_TB = 8


HEAD_ROWS = N_HEADS * BLK


def _pair_heads(w):
    return jnp.transpose(w.reshape(N_KV, GROUP, HD, w.shape[1]), (1, 0, 2, 3)).reshape(w.shape)


def _unpair_heads(w):
    return jnp.transpose(w.reshape(GROUP, N_KV, HD, w.shape[1]), (1, 0, 2, 3)).reshape(w.shape)


def _halves(x, scale=1.0):
    low = lax.broadcasted_iota(jnp.int32, x.shape, 1) < HD
    xf = x.astype(F32) * scale
    return jnp.where(low, xf, 0.0).astype(BF), jnp.where(low, 0.0, xf).astype(BF)


def _stack_heads(x, scale=1.0):
    halves = [_halves(x[:, g * 128:(g + 1) * 128], scale) for g in range(GROUP)]
    return jnp.concatenate([lo for lo, _ in halves] + [hi for _, hi in halves], axis=0)


def _attn_probs(q, kvc, kvp, bias_ref, sink_ref, has_prev):
    kv2 = jnp.concatenate([kvp, kvc], axis=0)
    kboth, vboth = kv2[:, 0:KV_W], kv2[:, KV_W:2 * KV_W]
    qpad = _stack_heads(q, SCALE)
    s = _dot_nt(qpad, kboth) + bias_ref[...]
    if has_prev is not None:
        col = lax.broadcasted_iota(jnp.int32, (HEAD_ROWS, 2 * BLK), 1)
        s = jnp.where((col >= BLK) | has_prev, s, NEG)
    row_head = lax.broadcasted_iota(jnp.int32, (HEAD_ROWS, 1), 0) // BLK
    sink = jnp.zeros((HEAD_ROWS, 1), F32)
    for h in range(N_HEADS):
        sink = jnp.where(row_head == h, sink_ref[h], sink)
    m = jnp.maximum(jnp.max(s, axis=1, keepdims=True), sink)
    p = jnp.exp(s - m)
    e_sink = jnp.exp(sink - m)
    inv = 1.0 / (jnp.sum(p, axis=1, keepdims=True) + e_sink)
    return qpad, kboth, vboth, p * inv, e_sink * inv


def _attn_fwd(qkv, bias, sinks):
    S = qkv.shape[0]
    tb = min(ATT_TB, S // BLK)
    T = tb * BLK

    def body(sink_ref, q_ref, kv_ref, kvp_ref, bias_ref, o_ref):
        step = pl.program_id(0)
        for j in range(tb):
            rows = slice(j * BLK, (j + 1) * BLK)
            kvp = kvp_ref[...] if j == 0 else kv_ref[(j - 1) * BLK:j * BLK, :]
            has_prev = (step > 0) if j == 0 else None
            _, _, vboth, prob, _ = _attn_probs(q_ref[rows, :], kv_ref[rows, :], kvp, bias_ref, sink_ref, has_prev)
            pb = prob.astype(BF)
            v_low, v_high = _halves(vboth)
            half = HEAD_ROWS // 2
            o = _dot(pb[0:half], v_low) + _dot(pb[half:HEAD_ROWS], v_high)
            for g in range(GROUP):
                o_ref[rows, g * 128:(g + 1) * 128] = o[g * BLK:(g + 1) * BLK].astype(BF)

    return pl.pallas_call(
        body, name="attn_fwd", grid=(S // T,),
        out_shape=_sds((S, Q_W), BF),
        in_specs=[pl.BlockSpec(memory_space=pltpu.SMEM),
                  pl.BlockSpec((T, Q_W), lambda i: (i, 0)),
                  pl.BlockSpec((T, 2 * KV_W), lambda i: (i, 2)),
                  pl.BlockSpec((BLK, 2 * KV_W), lambda i: (jnp.maximum(i * tb - 1, 0), 2)),
                  _const((HEAD_ROWS, 2 * BLK))],
        out_specs=pl.BlockSpec((T, Q_W), lambda i: (i, 0)),
        compiler_params=_cp(1, 32),
    )(sinks, *_in_hbm(qkv, qkv, qkv, bias))


def _attn_bwd(qkv, bias, sinks, do):
    S = qkv.shape[0]
    tb = min(ATT_TB, S // BLK)
    T = tb * BLK
    nt = S // T
    half = HEAD_ROWS // 2

    def body(sink_ref, q_ref, kv_ref, kvp_ref, bias_ref, do_ref, dq_ref, dkv_ref, dbias_ref, dsink_ref, carry):
        i = pl.program_id(0)

        @pl.when(i == 0)
        def _():
            carry[...] = jnp.zeros_like(carry)
            dbias_ref[...] = jnp.zeros_like(dbias_ref)
            dsink_ref[...] = jnp.zeros_like(dsink_ref)

        from_next = carry[...]
        head_row = lax.broadcasted_iota(jnp.int32, (N_HEADS, 128), 0)
        low = lax.broadcasted_iota(jnp.int32, (BLK, 128), 1) < HD
        for j in reversed(range(tb)):
            rows = slice(j * BLK, (j + 1) * BLK)
            kvp = kvp_ref[...] if j == 0 else kv_ref[(j - 1) * BLK:j * BLK, :]
            has_prev = (i < nt - 1) if j == 0 else None
            qpad, kboth, vboth, prob, p_sink = _attn_probs(q_ref[rows, :], kv_ref[rows, :], kvp, bias_ref, sink_ref,
                                                           has_prev)
            pb = prob.astype(BF)
            dopad = _stack_heads(do_ref[rows, :])
            dp = _dot_nt(dopad, vboth)
            delta = jnp.sum(prob * dp, axis=1, keepdims=True)
            ds = prob * (dp - delta)
            dbias_ref[...] += ds
            sink_term = p_sink * delta
            dsink_rows = jnp.zeros((N_HEADS, 128), F32)
            for h in range(N_HEADS):
                val = -jnp.sum(sink_term[h * BLK:(h + 1) * BLK], axis=0, keepdims=True)
                dsink_rows = jnp.where(head_row == h, val, dsink_rows)
            dsink_ref[...] += dsink_rows
            dsb = ds.astype(BF)
            dqpad = _dot(dsb, kboth) * SCALE
            for g in range(GROUP):
                dq_ref[rows, g * 128:(g + 1) * 128] = jnp.where(
                    low, dqpad[g * BLK:(g + 1) * BLK], dqpad[half + g * BLK:half + (g + 1) * BLK]).astype(BF)
            dkv2 = jnp.concatenate([jnp.transpose(_dot_tn(qpad, dsb)),
                                    jnp.transpose(_dot_tn(dopad, pb))], axis=1)
            dkv_ref[rows, :] = (dkv2[BLK:2 * BLK] + from_next).astype(BF)
            from_next = dkv2[0:BLK]
        carry[...] = from_next

    return pl.pallas_call(
        body, name="attn_bwd", grid=(nt,),
        out_shape=[_sds((S, Q_W), BF), _sds((S, 2 * KV_W), BF),
                   _sds((HEAD_ROWS, 2 * BLK), F32), _sds((N_HEADS, 128), F32)],
        in_specs=[pl.BlockSpec(memory_space=pltpu.SMEM),
                  pl.BlockSpec((T, Q_W), lambda i: (nt - 1 - i, 0)),
                  pl.BlockSpec((T, 2 * KV_W), lambda i: (nt - 1 - i, 2)),
                  pl.BlockSpec((BLK, 2 * KV_W), lambda i: (jnp.maximum((nt - 1 - i) * tb - 1, 0), 2)),
                  _const((HEAD_ROWS, 2 * BLK)),
                  pl.BlockSpec((T, Q_W), lambda i: (nt - 1 - i, 0))],
        out_specs=[pl.BlockSpec((T, Q_W), lambda i: (nt - 1 - i, 0)),
                   pl.BlockSpec((T, 2 * KV_W), lambda i: (nt - 1 - i, 0)),
                   _const((HEAD_ROWS, 2 * BLK)), _const((N_HEADS, 128))],
        scratch_shapes=[pltpu.VMEM((BLK, 2 * KV_W), F32)],
        compiler_params=_cp(1, 32),
    )(sinks, *_in_hbm(qkv, qkv, qkv, bias, do))


def _rel_bias_grad(dbias, bucket):
    def body(db_ref, bk_ref, out_ref):
        bk = bk_ref[...]
        lane = lax.broadcasted_iota(jnp.int32, (1, 128), 1)
        for h in range(N_HEADS):
            d = db_ref[h // GROUP, pl.ds((h % GROUP) * BLK, BLK), :]
            row = jnp.zeros((1, 128), F32)
            for b in range(N_BUCKETS):
                tot = jnp.sum(jnp.sum(jnp.where(bk == b, d, 0.0), axis=1, keepdims=True), axis=0, keepdims=True)
                row = jnp.where(lane == b, tot, row)
            out_ref[pl.ds(h, 1), :] = row

    vm = pl.BlockSpec(memory_space=pltpu.VMEM)
    return pl.pallas_call(body, name="rel_bias_grad", out_shape=_sds((N_HEADS, 128), F32),
                          in_specs=[vm, vm], out_specs=vm)(dbias, bucket)


def _gmlp_parts(zg, lg_ref, lb_ref):
    z = zg.astype(F32)
    ge = _gelu(z)
    u, vg = ge[:, 0:G_W], ge[:, G_W:2 * G_W]
    mu = jnp.mean(vg, axis=-1, keepdims=True)
    xc = vg - mu
    rstd = lax.rsqrt(jnp.mean(xc * xc, axis=-1, keepdims=True) + EPS)
    xh = xc * rstd
    return z, u, xh, rstd, xh * lg_ref[...] + lb_ref[...]


def _causal_weights(ws_ref, wc):
    t = lax.broadcasted_iota(jnp.int32, (BLK, BLK), 0)
    s = lax.broadcasted_iota(jnp.int32, (BLK, BLK), 1)
    for g in range(N_HEADS):
        wc[g] = jnp.where(s <= t, ws_ref[g], 0.0).astype(BF)


def _spatial(vb, wc, bst_ref, p, low):
    xp = vb[:, p * 128:(p + 1) * 128]
    s0 = _dot(wc[2 * p], xp) + bst_ref[:, 2 * p:2 * p + 1]
    s1 = _dot(wc[2 * p + 1], xp) + bst_ref[:, 2 * p + 1:2 * p + 2]
    return xp, jnp.where(low, s0, s1)


def _gmlp_fwd(zg, lg, lb, ws, bst):
    S = zg.shape[0]
    tb = min(ATT_TB, S // BLK)
    T = tb * BLK

    def body(zg_ref, lg_ref, lb_ref, ws_ref, bst_ref, o_ref, wc):
        @pl.when(pl.program_id(0) == 0)
        def _():
            _causal_weights(ws_ref, wc)
        low = lax.broadcasted_iota(jnp.int32, (BLK, 128), 1) < HD
        for j in range(tb):
            rows = slice(j * BLK, (j + 1) * BLK)
            _, u, _, _, vln = _gmlp_parts(zg_ref[rows, :], lg_ref, lb_ref)
            vb = vln.astype(BF)
            for p in range(4):
                _, sp = _spatial(vb, wc, bst_ref, p, low)
                o_ref[rows, p * 128:(p + 1) * 128] = (u[:, p * 128:(p + 1) * 128] * sp).astype(BF)

    return pl.pallas_call(
        body, name="gmlp_fwd", grid=(S // T,),
        out_shape=_sds((S, G_W), BF),
        in_specs=[pl.BlockSpec((T, 2 * G_W), lambda i: (i, 0)), _const((1, G_W)), _const((1, G_W)),
                  _const((N_HEADS, BLK, BLK)), _const((BLK, N_HEADS))],
        out_specs=pl.BlockSpec((T, G_W), lambda i: (i, 0)),
        scratch_shapes=[pltpu.VMEM((N_HEADS, BLK, BLK), BF)],
        compiler_params=_cp(1, 32),
    )(*_in_hbm(zg), lg, lb, ws, bst)


def _gmlp_bwd(zg, d_out, lg, lb, ws, bst):
    S = zg.shape[0]
    tb = min(ATT_TB, S // BLK)
    T = tb * BLK
    nb = S // T

    def body(zg_ref, d_ref, lg_ref, lb_ref, ws_ref, bst_ref, dzg_ref, dws_ref, dbs_ref, dlg_ref, dlb_ref, wc, dbacc):
        i = pl.program_id(0)

        @pl.when(i == 0)
        def _():
            _causal_weights(ws_ref, wc)
            dws_ref[...] = jnp.zeros_like(dws_ref)
            dlg_ref[...] = jnp.zeros_like(dlg_ref)
            dlb_ref[...] = jnp.zeros_like(dlb_ref)
            dbacc[...] = jnp.zeros_like(dbacc)

        low = lax.broadcasted_iota(jnp.int32, (BLK, 128), 1) < HD
        for j in range(tb):
            rows = slice(j * BLK, (j + 1) * BLK)
            z, u, xh, rstd, vln = _gmlp_parts(zg_ref[rows, :], lg_ref, lb_ref)
            vb = vln.astype(BF)
            d = d_ref[rows, :].astype(F32)
            du_parts, dvln_parts = [], []
            for p in range(4):
                xp, sp = _spatial(vb, wc, bst_ref, p, low)
                dp = d[:, p * 128:(p + 1) * 128]
                du_parts.append(dp * sp)
                dsp = dp * u[:, p * 128:(p + 1) * 128]
                dbacc[:, p * 128:(p + 1) * 128] += dsp
                d0 = jnp.where(low, dsp, 0.0).astype(BF)
                d1 = jnp.where(low, 0.0, dsp).astype(BF)
                dws_ref[2 * p] += _dot_nt(d0, xp)
                dws_ref[2 * p + 1] += _dot_nt(d1, xp)
                dvln_parts.append(_dot_tn(wc[2 * p], d0) + _dot_tn(wc[2 * p + 1], d1))
            dvln = jnp.concatenate(dvln_parts, axis=1)
            dlg_ref[...] += _colsum(dvln * xh)
            dlb_ref[...] += _colsum(dvln)
            dxh = dvln * lg_ref[...]
            dvg = rstd * (dxh - jnp.mean(dxh, axis=-1, keepdims=True)
                          - xh * jnp.mean(dxh * xh, axis=-1, keepdims=True))
            dge = jnp.concatenate(du_parts + [dvg], axis=1)
            dzg_ref[rows, :] = (dge * _gelu_grad(z)).astype(BF)

        @pl.when(i == nb - 1)
        def _():
            t = lax.broadcasted_iota(jnp.int32, (BLK, BLK), 0)
            s = lax.broadcasted_iota(jnp.int32, (BLK, BLK), 1)
            for g in range(N_HEADS):
                dws_ref[g] = jnp.where(s <= t, dws_ref[g], 0.0)
            grp = lax.broadcasted_iota(jnp.int32, (N_HEADS, G_W), 0)
            lane = lax.broadcasted_iota(jnp.int32, (N_HEADS, G_W), 1) // HD
            pick = jnp.where(grp == lane, 1.0, 0.0).astype(F32)
            dbs_ref[...] = lax.dot_general(pick, dbacc[...], (((1,), (1,)), ((), ())),
                                           preferred_element_type=F32, precision=HIGH)

    return pl.pallas_call(
        body, name="gmlp_bwd", grid=(nb,),
        out_shape=[_sds((S, 2 * G_W), BF), _sds((N_HEADS, BLK, BLK), F32), _sds((N_HEADS, BLK), F32),
                   _sds((1, G_W), F32), _sds((1, G_W), F32)],
        in_specs=[pl.BlockSpec((T, 2 * G_W), lambda i: (i, 0)), pl.BlockSpec((T, G_W), lambda i: (i, 0)),
                  _const((1, G_W)), _const((1, G_W)), _const((N_HEADS, BLK, BLK)), _const((BLK, N_HEADS))],
        out_specs=[pl.BlockSpec((T, 2 * G_W), lambda i: (i, 0)), _const((N_HEADS, BLK, BLK)),
                   _const((N_HEADS, BLK)), _const((1, G_W)), _const((1, G_W))],
        scratch_shapes=[pltpu.VMEM((N_HEADS, BLK, BLK), BF), pltpu.VMEM((BLK, G_W), F32)],
        compiler_params=_cp(1, 32),
    )(*_in_hbm(zg, d_out), lg, lb, ws, bst)


def _mix_out(o, gm, gates, h, wa, wg, wo, gate, gp, after=()):
    S = h.shape[0]
    R = min(512, S)

    def body(o_ref, gm_ref, gates_ref, h_ref, wa_ref, wg_ref, wo_ref, gate_ref, gp_ref,
             ya_ref, yg_ref, ym_ref, y_ref, hn_ref):
        for r0 in range(0, R, CHUNK):
            rows = slice(r0, r0 + CHUNK)
            ya = _dot(o_ref[rows, :], wa_ref[...])
            yg = _dot(gm_ref[rows, :], wg_ref[...])
            ya_ref[rows, :] = ya.astype(BF)
            yg_ref[rows, :] = yg.astype(BF)
            ym = (gates_ref[rows, 0:D].astype(F32) * ya + gates_ref[rows, D:2 * D].astype(F32) * yg).astype(BF)
            ym_ref[rows, :] = ym
            y = _dot(ym, wo_ref[...])
            y_ref[rows, :] = y.astype(BF)
            hn_ref[rows, :] = h_ref[rows, :] + gate_ref[...] * (y * _rms_r(y) * gp_ref[...])

    vec = _const((1, D))
    rows = lambda w_: pl.BlockSpec((R, w_), lambda i: (i, 0))
    body, after_specs = _behind(body, 9, after)
    return pl.pallas_call(
        body, name="mix_out", grid=(S // R,),
        out_shape=[_sds((S, D), BF)] * 4 + [_sds((S, D), F32)],
        in_specs=[rows(Q_W), rows(G_W), rows(2 * D), rows(D), _resident((Q_W, D)), _resident((G_W, D)),
                  _resident((D, D)), vec, vec] + after_specs,
        out_specs=[rows(D)] * 5,
        compiler_params=_cp(1, 48),
    )(*_in_hbm(o, gm, gates, h, wa, wg, wo), gate, gp, *after)


def _mix_out_bwd(dh, y, ya, yg, gates, att, gm, ymix, wa, wg, wo, gate, gp, after=()):
    S = dh.shape[0]
    R = min(512, S)
    nb = S // R

    def body(dh_ref, y_ref, ya_ref, yg_ref, gates_ref, att_ref, gm_ref, ym_ref, wa_ref, wg_ref, wo_ref,
             gate_ref, gp_ref, dz_ref, do_ref, dgm_ref, dgate_ref, dgp_ref, gwo_ref, gwa_ref, gwg_ref,
             acc_o, acc_a, acc_g, dy_scr, dya_scr, dyg_scr):
        i = pl.program_id(0)

        @pl.when(i == 0)
        def _():
            for r in (dgate_ref, dgp_ref, acc_o, acc_a, acc_g):
                r[...] = jnp.zeros_like(r)
        for r0 in range(0, R, 2 * CHUNK):
            rows = slice(r0, min(r0 + 2 * CHUNK, R))
            dy, dgate, dgp = _postnorm_bwd(dh_ref[rows, :], y_ref[rows, :], gate_ref[...], gp_ref[...], 1.0)
            dgate_ref[...] += dgate
            dgp_ref[...] += dgp
            dyb = dy.astype(BF)
            dy_scr[rows, :] = dyb
            dym = _dot_nt(dyb, wo_ref[...])
            ga = gates_ref[rows, 0:D].astype(F32)
            gg = gates_ref[rows, D:2 * D].astype(F32)
            dya = (dym * ga).astype(BF)
            dyg = (dym * gg).astype(BF)
            dya_scr[rows, :] = dya
            dyg_scr[rows, :] = dyg
            dz_ref[rows, 0:D] = (dym * ya_ref[rows, :].astype(F32) * (ga * (1.0 - ga))).astype(BF)
            dz_ref[rows, D:2 * D] = (dym * yg_ref[rows, :].astype(F32) * (gg * (1.0 - gg))).astype(BF)
            do_ref[rows, :] = _dot_nt(dya, wa_ref[...]).astype(BF)
            dgm_ref[rows, :] = _dot_nt(dyg, wg_ref[...]).astype(BF)
        for m0 in range(0, D, CHUNK):
            acc_o[m0:m0 + CHUNK, :] += _dot_tn(ym_ref[:, m0:m0 + CHUNK], dy_scr[...])
        for m0 in range(0, Q_W, CHUNK):
            acc_a[m0:m0 + CHUNK, :] += _dot_tn(att_ref[:, m0:m0 + CHUNK], dya_scr[...])
            acc_g[m0:m0 + CHUNK, :] += _dot_tn(gm_ref[:, m0:m0 + CHUNK], dyg_scr[...])

        @pl.when(i == nb - 1)
        def _():
            for m0 in range(0, D, CHUNK):
                gwo_ref[m0:m0 + CHUNK, :] = acc_o[m0:m0 + CHUNK, :].astype(BF)
            for m0 in range(0, Q_W, CHUNK):
                gwa_ref[m0:m0 + CHUNK, :] = acc_a[m0:m0 + CHUNK, :].astype(BF)
                gwg_ref[m0:m0 + CHUNK, :] = acc_g[m0:m0 + CHUNK, :].astype(BF)

    vec = _const((1, D))
    rows = lambda w_: pl.BlockSpec((R, w_), lambda i: (i, 0))
    body, after_specs = _behind(body, 13, after)
    return pl.pallas_call(
        body, name="mix_out_bwd", grid=(nb,),
        out_shape=[_sds((S, 2 * D), BF), _sds((S, Q_W), BF), _sds((S, G_W), BF), _sds((1, D), F32),
                   _sds((1, D), F32), _sds((D, D), BF), _sds((Q_W, D), BF), _sds((G_W, D), BF)],
        in_specs=[rows(D), rows(D), rows(D), rows(D), rows(2 * D), rows(Q_W), rows(G_W), rows(D),
                  _resident((Q_W, D)), _resident((G_W, D)), _resident((D, D)), vec, vec] + after_specs,
        out_specs=[rows(2 * D), rows(Q_W), rows(G_W), vec, vec, _const((D, D)), _const((Q_W, D)),
                   _const((G_W, D))],
        scratch_shapes=[pltpu.VMEM((D, D), F32), pltpu.VMEM((Q_W, D), F32), pltpu.VMEM((G_W, D), F32)]
        + [pltpu.VMEM((R, D), BF)] * 3,
        compiler_params=_cp(1, 60),
    )(*_in_hbm(dh, y, ya, yg, gates, att, gm, ymix, wa, wg, wo), gate, gp, *after)


def _mix_dn(dq, dkv, dzg, dzgate, w, wq, h, dh, sc, gp, after=()):
    S = h.shape[0]
    R = min(512, S)

    def body(dq_ref, dkv_ref, dzg_ref, dzt_ref, w_ref, wq_ref, h_ref, dh_ref, sc_ref, gp_ref,
             out_ref, dsh_ref, dsc_ref, dgp_ref):
        @pl.when(pl.program_id(0) == 0)
        def _():
            dsh_ref[...] = jnp.zeros_like(dsh_ref)
            dsc_ref[...] = jnp.zeros_like(dsc_ref)
            dgp_ref[...] = jnp.zeros_like(dgp_ref)
        for r0 in range(0, R, CHUNK):
            rows = slice(r0, r0 + CHUNK)
            dn = _dot(dq_ref[rows, :], wq_ref[...])
            dn = dn + _dot(dkv_ref[rows, :], w_ref[Q_W:QKV_W, :])
            dn = dn + _dot(dzg_ref[rows, :], w_ref[ZG_OFF:GATE_OFF, :])
            dn = dn + _dot(dzt_ref[rows, :], w_ref[GATE_OFF:IN_W, :])
            dx, dsh, dsc, dgp = _prenorm_bwd(dn, h_ref[rows, :], gp_ref[...], sc_ref[...])
            out_ref[rows, :] = dh_ref[rows, :] + dx
            dsh_ref[...] += dsh
            dsc_ref[...] += dsc
            dgp_ref[...] += dgp

    vec = _const((1, D))
    rows = lambda w_: pl.BlockSpec((R, w_), lambda i: (i, 0))
    body, after_specs = _behind(body, 10, after)
    return pl.pallas_call(
        body, name="mix_dn", grid=(S // R,),
        out_shape=[_sds((S, D), F32)] + [_sds((1, D), F32)] * 3,
        in_specs=[rows(Q_W), rows(2 * KV_W), rows(2 * G_W), rows(2 * D), _resident((IN_W, D)),
                  _resident((Q_W, D)), rows(D), rows(D), vec, vec] + after_specs,
        out_specs=[rows(D), vec, vec, vec],
        compiler_params=_cp(1, 48),
    )(*_in_hbm(dq, dkv, dzg, dzgate, w, wq, h, dh), sc, gp, *after)


def _adamw_math(w, g, m, v):
    m2 = ADAM_B1 * m + (1.0 - ADAM_B1) * g
    v2 = ADAM_B2 * v + (1.0 - ADAM_B2) * (g * g)
    m_hat = m2 / (1.0 - ADAM_B1 ** ADAM_STEP)
    v_hat = v2 / (1.0 - ADAM_B2 ** ADAM_STEP)
    delta = -ADAM_LR * (m_hat / (jnp.sqrt(v_hat) + ADAM_EPS) + ADAM_WD * w)
    return delta, m2, v2


def _row_tile(rows, cols):
    best = None
    for t in range(16, rows + 1, 16):
        if rows % t == 0 and t * cols <= 256 * 1024:
            best = t
    return best if best is not None else rows


def _adamw_sharded(landing, w, m, v, name):
    r, c = w.shape
    tr = _row_tile(r, c)

    def body(l_ref, w_ref, m_ref, v_ref, g_ref, d_ref, m2_ref, v2_ref):
        g = l_ref[0].astype(F32)
        for j in range(1, N_DEV):
            g = g + l_ref[j].astype(F32)
        delta, m2, v2 = _adamw_math(w_ref[...], g, m_ref[...], v_ref[...])
        g_ref[...] = g
        d_ref[...] = delta
        m2_ref[...] = m2
        v2_ref[...] = v2

    row = pl.BlockSpec((tr, c), lambda i: (i, 0))
    return pl.pallas_call(
        body, name=name, grid=(r // tr,),
        out_shape=[_sds((r, c), F32)] * 4,
        in_specs=[pl.BlockSpec((N_DEV, tr, c), lambda i: (0, i, 0)), row, row, row],
        out_specs=[row] * 4,
        compiler_params=_cp(1, 48),
    )(*_in_hbm(landing, w, m, v))


def _adamw_small(items):
    n = len(items)

    def body(*refs):
        for k in range(n):
            w_ref, g_ref, m_ref, v_ref = refs[4 * k:4 * k + 4]
            outs = refs[4 * n + 3 * k:4 * n + 3 * k + 3]
            for o_ref, val in zip(outs, _adamw_math(w_ref[...], g_ref[...], m_ref[...], v_ref[...])):
                o_ref[...] = val

    vm = pl.BlockSpec(memory_space=pltpu.VMEM)
    flat = pl.pallas_call(
        body, name="adamw_small",
        out_shape=[_sds(it[0].shape, F32) for it in items for _ in range(3)],
        in_specs=[vm] * (4 * n), out_specs=[vm] * (3 * n),
    )(*[a for it in items for a in it])
    return [tuple(flat[3 * k:3 * k + 3]) for k in range(n)]


def _w_ada_update(c8, d_ada, w, m, v):
    tr = 256

    def body(c_ref, d_ref, w_ref, m_ref, v_ref, g_ref, dl_ref, m2_ref, v2_ref):
        cs = c_ref[...]
        cs = cs * jax.nn.sigmoid(cs)
        g = lax.dot_general(cs, d_ref[...], (((0,), (0,)), ((), ())), preferred_element_type=F32, precision=HIGH)
        delta, m2, v2 = _adamw_math(w_ref[...], g, m_ref[...], v_ref[...])
        g_ref[...] = g
        dl_ref[...] = delta
        m2_ref[...] = m2
        v2_ref[...] = v2

    row = pl.BlockSpec((tr, ADA_W), lambda i: (i, 0))
    return pl.pallas_call(
        body, name="w_ada_update", grid=(D // tr,),
        out_shape=[_sds((D, ADA_W), F32)] * 4,
        in_specs=[pl.BlockSpec((N_DEV, tr), lambda i: (0, i)), _const((N_DEV, ADA_W)), row, row, row],
        out_specs=[row] * 4,
        compiler_params=_cp(1, 40),
    )(c8, d_ada, *_in_hbm(w, m, v))


def _t5_bucket():
    qi = np.arange(BLK, dtype=np.int32)[:, None]
    kj = np.arange(2 * BLK, dtype=np.int32)[None, :]
    dist = np.maximum(qi + BLK - kj, 0)
    max_exact = N_BUCKETS // 2
    d_f = np.maximum(dist, max_exact).astype(np.float32)
    large = max_exact + (np.log(d_f / np.float32(max_exact)) / np.float32(math.log(MAX_DISTANCE / max_exact))
                         * np.float32(N_BUCKETS - max_exact)).astype(np.int32)
    large = np.minimum(large, N_BUCKETS - 1)
    return jnp.asarray(np.where(dist < max_exact, dist, large).astype(np.int32))


def _slabs_of_columns(w):
    r, c8 = w.shape
    return jnp.transpose(w.reshape(r, N_DEV, c8 // N_DEV), (1, 0, 2))


def _columns_of_slabs(w8):
    _, r, c = w8.shape
    return jnp.transpose(w8, (1, 0, 2)).reshape(r, N_DEV * c)


def kernel(x, c, rel_bias, w_ada, b_ada, pre_norm_g, post_norm_g, w_ffn1_in, w_ffn1_out, w_in, sinks, gmlp_ln_g, gmlp_ln_b, gmlp_w_s, gmlp_b_s, w_br_attn, w_br_gmlp, w_out, w_ffn2_in, w_ffn2_out, loss_target, m_rel_bias, m_w_ada, m_b_ada, m_pre_norm_g, m_post_norm_g, m_w_ffn1_in, m_w_ffn1_out, m_w_in, m_sinks, m_gmlp_ln_g, m_gmlp_ln_b, m_gmlp_w_s, m_gmlp_b_s, m_w_br_attn, m_w_br_gmlp, m_w_out, m_w_ffn2_in, m_w_ffn2_out, v_rel_bias, v_w_ada, v_b_ada, v_pre_norm_g, v_post_norm_g, v_w_ffn1_in, v_w_ffn1_out, v_w_in, v_sinks, v_gmlp_ln_g, v_gmlp_ln_b, v_gmlp_w_s, v_gmlp_b_s, v_w_br_attn, v_w_br_gmlp, v_w_out, v_w_ffn2_in, v_w_ffn2_out):
    me = 4 * lax.axis_index("x") + 2 * lax.axis_index("y") + lax.axis_index("c")
    x0 = x[0]
    target = loss_target[0]

    transposed = ("w_ffn1_in", "w_in", "w_ffn2_in")
    shards = [w_ffn1_in[0].T, w_ffn1_out[0], w_in[0].T, w_br_attn[0], w_br_gmlp[0], w_out[0],
              w_ffn2_in[0].T, w_ffn2_out[0]]
    shards_bf = [s.astype(BF) for s in shards]
    groups = [shards_bf[0:1], shards_bf[1:6], shards_bf[6:8]]

    def forward_start(st, i, after, then_gather=None):
        lands = _slabs_wait("gather", len(groups[i]), st, after, "gather_wait_%d" % i)
        parts = [("forward", lands)] + ([("gather", groups[then_gather])] if then_gather is not None else [])
        return _slabs_start_many(parts, c, "forward_start_%d" % i)

    def gathered(st, i, after):
        return _slabs_wait("forward", len(groups[i]), st, after, "forward_wait_%d" % i)

    gs0 = _slabs_start("gather", groups[0], c, "gather_start_0")

    mine = jnp.concatenate([c[0], pre_norm_g[0].reshape(-1), post_norm_g[0].reshape(-1)])
    small8 = jnp.broadcast_to(mine[None, :], (8, mine.shape[0]))
    b_ada64 = jnp.repeat(b_ada.reshape(N_DEV, ADA_W), 8, axis=0)
    gath, ada64 = _ada_forward(small8, w_ada[0], b_ada64)
    gath8 = gath[::8]
    ada = ada64[::8].reshape(9, D)
    sh1, sc1, g1, sh2, sc2, g2, sh3, sc3, g3 = [ada[k:k + 1] for k in range(9)]
    gains = gath8[:, D:].reshape(N_DEV, 2, 3, 128)
    pre_g = jnp.transpose(gains[:, 0], (1, 0, 2)).reshape(3, D)
    post_g = jnp.transpose(gains[:, 1], (1, 0, 2)).reshape(3, D)
    pre = [pre_g[k:k + 1] for k in range(3)]
    post = [post_g[k:k + 1] for k in range(3)]

    bucket = _t5_bucket()
    bias = _bias_table(rel_bias, bucket).reshape(HEAD_ROWS, 2 * BLK)
    sinks8 = sinks[0]
    lg, lb = gmlp_ln_g, gmlp_ln_b
    ws = gmlp_w_s[0]
    bst = jnp.transpose(gmlp_b_s[0])

    fs0, gs1 = forward_start(gs0, 0, sh1, then_gather=1)
    wf1_in = gathered(fs0, 0, gs1[-1])[0].reshape(2 * D_FF, D)
    n1, fg1, fu1, fa1 = _ffn_in(x0, sh1, sc1, pre[0], wf1_in, "ffn1_in")
    fs1, gs2 = forward_start(gs1, 1, n1, then_gather=2)
    mix_w = gathered(fs1, 1, gs2[-1])
    wf1_out = mix_w[0].reshape(D_FF, D)
    w_in_full = mix_w[1].reshape(IN_W, D)
    w_q = _pair_heads(w_in_full[0:Q_W])
    w_bra = _pair_heads(_columns_of_slabs(mix_w[2]))
    w_brg = _columns_of_slabs(mix_w[3])
    w_out_full = mix_w[4].reshape(D, D)
    h1, y1 = _ffn_out(fa1, wf1_out, x0, g1, post[0], "ffn1_out")
    n2, qkv, zg, gates = _mix_in(h1, sh2, sc2, pre[1], w_in_full, w_q)
    att = _attn_fwd(qkv, bias, sinks8)
    gm = _gmlp_fwd(zg, lg, lb, ws, bst)
    (fs2,) = forward_start(gs2, 2, gm)
    ya, yg, ymix, y2, h2 = _mix_out(att, gm, gates, h1, w_bra, w_brg, w_out_full, g2, post[1], after=(fs2[-1],))
    wf2_in, wf2_out = gathered(fs2, 2, h2)
    wf2_in = wf2_in.reshape(2 * D_FF, D)
    wf2_out = wf2_out.reshape(D_FF, D)
    n3, fg3, fu3, fa3 = _ffn_in(h2, sh3, sc3, pre[2], wf2_in, "ffn2_in")
    dh3, y3, sq = _ffn_out(fa3, wf2_out, h2, g3, post[2], "ffn2_out", target=target)

    def exchange_start(i, arrays):
        return _slabs_start("exchange", arrays, sq, "exchange_start_%d" % i)

    dy3, dgu3, dh2, d_g3, d_post2, d_sh3, d_sc3, d_pre2 = _ffn_bwd(
        dh3, y3, fg3, fu3, wf2_out, wf2_in, h2, g3, post[2], sc3, pre[2], "ffn2_bwd")
    gw_f2_out = _tn_matmul(fa3, dy3, "ffn2_out_wgrad", tm=D_FF // 2).reshape(N_DEV, D_FF // N_DEV, D)
    gw_f2_in = _tn_matmul(dgu3, n3, "ffn2_in_wgrad", tm=D_FF // 2).reshape(N_DEV, FS, D)
    ex1 = exchange_start(1, [gw_f2_out, gw_f2_in])

    dzgate, d_att, d_gm, d_g2, d_post1, gw_out, gw_bra, gw_brg = _mix_out_bwd(
        dh2, y2, ya, yg, gates, att, gm, ymix, w_bra, w_brg, w_out_full, g2, post[1], after=(ex1[-1],))
    ex2 = exchange_start(2, [_slabs_of_columns(_unpair_heads(gw_bra)), _slabs_of_columns(gw_brg),
                             gw_out.reshape(N_DEV, D // N_DEV, D)])
    dq, dkv, dbias, dsink = _attn_bwd(qkv, bias, sinks8, d_att)
    dzg, d_ws, d_bs, d_lg, d_lb = _gmlp_bwd(zg, d_gm, lg, lb, ws, bst)
    d_rel = _rel_bias_grad(dbias.reshape(N_KV, GROUP * BLK, 2 * BLK), bucket)
    early = jnp.concatenate([
        jnp.concatenate([d_lg.reshape(4, 128), d_lb.reshape(4, 128)], axis=0),
        d_bs, d_rel, dsink, d_ws.reshape(N_HEADS * BLK, BLK)], axis=0)
    sm0 = _slabs_start("gather_all", [early], sq, "small_gather_start")
    dh1, d_sh2, d_sc2, d_pre1 = _mix_dn(dq, dkv, dzg, dzgate, w_in_full, w_q, h1, dh2, sc2, pre[1],
                                        after=(ex2[-1], sm0[-1]))
    gw_in = jnp.concatenate(
        [_unpair_heads(_tn_matmul(dq, n2, "w_in_q_wgrad")), _tn_matmul(dkv, n2, "w_in_kv_wgrad"),
         _tn_matmul(dzg, n2, "w_in_zg_wgrad"), _tn_matmul(dzgate, n2, "w_in_gate_wgrad")],
        axis=0).reshape(N_DEV, IN_W // N_DEV, D)
    ex3 = exchange_start(3, [gw_in])

    dy1, dgu1, d_g1, d_post0 = _ffn_out_bwd(dh1, y1, fg1, fu1, wf1_out, g1, post[0], "ffn1_out_bwd",
                                            after=(ex3[-1],))
    gw_f1_out = _tn_matmul(fa1, dy1, "ffn1_out_wgrad", tm=D_FF // 2).reshape(N_DEV, D_FF // N_DEV, D)
    ex4 = exchange_start(4, [gw_f1_out])
    gw_f1_in = _tn_matmul(dgu1, n1, "ffn1_in_wgrad", tm=D_FF // 2).reshape(N_DEV, FS, D)
    ex5 = exchange_start(5, [gw_f1_in])
    grad_x, d_sh1, d_sc1, d_pre0 = _ffn_dn(dgu1, wf1_in, x0, dh1, sc1, pre[0], "ffn1_dn", after=(ex4[-1], ex5[-1]))

    landed = {}
    for i, (ex, nms) in enumerate([(ex1, ["w_ffn2_out", "w_ffn2_in"]),
                                   (ex2, ["w_br_attn", "w_br_gmlp", "w_out"]), (ex3, ["w_in"]),
                                   (ex4, ["w_ffn1_out"]), (ex5, ["w_ffn1_in"])]):
        for nm, land in zip(nms, _slabs_wait("exchange", len(nms), ex, grad_x, "exchange_wait_%d" % i)):
            landed[nm] = land
    moments = [(m_w_ffn1_in, v_w_ffn1_in), (m_w_ffn1_out, v_w_ffn1_out), (m_w_in, v_w_in),
               (m_w_br_attn, v_w_br_attn), (m_w_br_gmlp, v_w_br_gmlp), (m_w_out, v_w_out),
               (m_w_ffn2_in, v_w_ffn2_in), (m_w_ffn2_out, v_w_ffn2_out)]
    names = ["w_ffn1_in", "w_ffn1_out", "w_in", "w_br_attn", "w_br_gmlp", "w_out", "w_ffn2_in", "w_ffn2_out"]
    big = {}
    for nm, w_, (m_, v_) in zip(names, shards, moments):
        if nm in transposed:
            res4 = _adamw_sharded(landed[nm], w_, m_[0].T, v_[0].T, "adamw_" + nm)
            big[nm] = [a.T[None] for a in res4]
        else:
            big[nm] = [a[None] for a in _adamw_sharded(landed[nm], w_, m_[0], v_[0], "adamw_" + nm)]

    my_loss = jnp.broadcast_to(sq * (0.5 / D), (1, D))
    my_loss, _ = lax.optimization_barrier((my_loss, landed["w_ffn1_in"]))
    tot, every = _small_allreduce([d_sh1, d_sc1, d_g1, d_sh2, d_sc2, d_g2, d_sh3, d_sc3, d_g3,
                                   d_pre0, d_pre1, d_pre2, d_post0, d_post1, d_post2, my_loss])
    (early_land,) = _slabs_wait("gather_all", 1, sm0, grad_x, "small_gather_wait")
    tot_early = _sum_slabs(early_land)

    loss = tot[15, 0]
    g_b_ada = tot[0:9].reshape(1, 9 * D)
    g_pre = lax.dynamic_slice_in_dim(tot[9:12], 128 * me, 128, axis=1)[None]
    g_post = lax.dynamic_slice_in_dim(tot[12:15], 128 * me, 128, axis=1)[None]
    g_lg = tot_early[0:4].reshape(1, G_W)
    g_lb = tot_early[4:8].reshape(1, G_W)
    g_bs = tot_early[8:16][None]
    g_rel = jnp.transpose(tot_early[16:24, 0:N_BUCKETS])
    g_sinks = tot_early[24:32, 0][None]
    g_ws = tot_early[32:1056].reshape(1, N_HEADS, BLK, BLK)

    d_ada_mine = lax.dynamic_slice_in_dim(every[:, 0:9].reshape(N_DEV, 9 * D), ADA_W * me, ADA_W, axis=1)
    ada_out = [a[None] for a in _w_ada_update(gath8[:, 0:D], d_ada_mine, w_ada[0], m_w_ada[0], v_w_ada[0])]

    small = [("rel_bias", rel_bias, g_rel, m_rel_bias, v_rel_bias), ("b_ada", b_ada, g_b_ada, m_b_ada, v_b_ada),
             ("pre_norm_g", pre_norm_g, g_pre, m_pre_norm_g, v_pre_norm_g),
             ("post_norm_g", post_norm_g, g_post, m_post_norm_g, v_post_norm_g),
             ("sinks", sinks, g_sinks, m_sinks, v_sinks), ("gmlp_ln_g", gmlp_ln_g, g_lg, m_gmlp_ln_g, v_gmlp_ln_g),
             ("gmlp_ln_b", gmlp_ln_b, g_lb, m_gmlp_ln_b, v_gmlp_ln_b),
             ("gmlp_w_s", gmlp_w_s, g_ws, m_gmlp_w_s, v_gmlp_w_s), ("gmlp_b_s", gmlp_b_s, g_bs, m_gmlp_b_s, v_gmlp_b_s)]
    two_d = lambda a: a.reshape(int(math.prod(a.shape[:-1])), a.shape[-1])
    stepped = _adamw_small([tuple(two_d(a) for a in item[1:]) for item in small])
    res = {"w_ada": ada_out}
    for (nm, w_, g_, _, _), new in zip(small, stepped):
        res[nm] = [g_] + [a.reshape(w_.shape) for a in new]
    res.update(big)
    order = ["rel_bias", "w_ada", "b_ada", "pre_norm_g", "post_norm_g", "w_ffn1_in", "w_ffn1_out", "w_in", "sinks",
             "gmlp_ln_g", "gmlp_ln_b", "gmlp_w_s", "gmlp_b_s", "w_br_attn", "w_br_gmlp", "w_out", "w_ffn2_in",
             "w_ffn2_out"]
    outs = [loss, grad_x[None]]
    for k in range(4):
        outs += [res[nm][k] for nm in order]
    return tuple(outs)
```

```python
import math

import jax
import jax.numpy as jnp
import numpy as np
from jax import lax
from jax.experimental import pallas as pl
from jax.experimental.pallas import tpu as pltpu

F32 = jnp.float32
BF = jnp.bfloat16

N_DEV = 8
D = 1024
D_FF = 2816
FS = D_FF // 4
N_HEADS = 8
N_KV = 2
GROUP = 4
HD = 64
BLK = 128
Q_W = 512
KV_W = 128
G_W = 512
QKV_W = Q_W + 2 * KV_W
ZG_OFF = QKV_W
GATE_OFF = ZG_OFF + 2 * G_W
IN_W = GATE_OFF + 2 * D
N_BUCKETS = 32
MAX_DISTANCE = 128
EPS = 1e-6
NEG = -1e30
SCALE = HD ** -0.5
ADA_W = 9 * D // N_DEV

ADAM_LR = 0.001
ADAM_B1 = 0.9
ADAM_B2 = 0.999
ADAM_EPS = 1e-08
ADAM_WD = 0.01
ADAM_STEP = 10

CHUNK = 256
MIB = 1024 * 1024
MESH = pl.DeviceIdType.MESH
HIGH = lax.Precision.HIGHEST


def _cp(n_grid, vmem_mib):
    return pltpu.CompilerParams(dimension_semantics=("arbitrary",) * n_grid,
                                vmem_limit_bytes=vmem_mib * MIB)


def _const(shape):
    return pl.BlockSpec(shape, lambda *_: (0,) * len(shape))


def _resident(shape):
    return pl.BlockSpec(shape, lambda *_: (0,) * len(shape), pipeline_mode=pl.Buffered(1))


def _behind(body, n_in, after):
    k = len(after)
    return (lambda *refs: body(*refs[:n_in], *refs[n_in + k:])), [pl.BlockSpec(memory_space=pl.ANY)] * k


def _in_hbm(*arrays):
    return [pltpu.with_memory_space_constraint(a, pltpu.HBM) for a in arrays]


def _sds(shape, dtype):
    return jax.ShapeDtypeStruct(shape, dtype)


def _dot(a, b):
    return jnp.dot(a, b, preferred_element_type=F32)


def _dot_nt(a, b):
    return lax.dot_general(a, b, (((1,), (1,)), ((), ())), preferred_element_type=F32)


def _dot_tn(a, b):
    return lax.dot_general(a, b, (((0,), (0,)), ((), ())), preferred_element_type=F32)


def _rms_r(x):
    return lax.rsqrt(jnp.mean(x * x, axis=-1, keepdims=True) + EPS)


def _colsum(x):
    return jnp.sum(x, axis=0, keepdims=True)


def _prenorm(x, gp, sc, sh):
    return (x * _rms_r(x) * gp) * (1.0 + sc) + sh


def _prenorm_bwd(dn, x, gp, sc):
    r = _rms_r(x)
    xh = x * r
    t = dn * (1.0 + sc) * gp
    dx = r * (t - xh * jnp.mean(t * xh, axis=-1, keepdims=True))
    return dx, _colsum(dn), _colsum(dn * xh * gp), _colsum(dn * (1.0 + sc) * xh)


def _postnorm_bwd(dh, y, gate, gp, res):
    y = y.astype(F32)
    r = _rms_r(y)
    yh = y * r
    dyn = (res * gate) * dh
    t = dyn * gp
    dy = r * (t - yh * jnp.mean(t * yh, axis=-1, keepdims=True))
    return dy, _colsum(res * dh * yh * gp), _colsum(dyn * yh)


def _gelu(x):
    k = math.sqrt(2.0 / math.pi)
    return 0.5 * x * (1.0 + jnp.tanh(k * (x + 0.044715 * x * x * x)))


def _gelu_grad(x):
    k = math.sqrt(2.0 / math.pi)
    t = jnp.tanh(k * (x + 0.044715 * x * x * x))
    return 0.5 * (1.0 + t) + 0.5 * x * (1.0 - t * t) * (k * (1.0 + 3.0 * 0.044715 * x * x))


def _my_place():
    x, y, c = lax.axis_index("x"), lax.axis_index("y"), lax.axis_index("c")
    return x, y, c, 4 * x + 2 * y + c


def _peer(x, y, c, k):
    px = 1 - x if k & 4 else x
    py = 1 - y if k & 2 else y
    pc = 1 - c if k & 1 else c
    return (px, py, pc), 4 * px + 2 * py + pc


HBM_SPEC = pl.BlockSpec(memory_space=pltpu.HBM)
SEM_SPEC = pl.BlockSpec(memory_space=pltpu.SEMAPHORE)
EFFECT = pltpu.SideEffectType.DATAFLOW_SIDE_EFFECTING


RELATIONS = {"exchange": (1, 2, 3, 4, 5, 6, 7), "gather": (1, 2, 4, 6), "forward": (2, 4, 6),
             "gather_all": (1, 2, 3, 4, 5, 6, 7)}


def _slab_copies(mode, srcs, lands, send, recv, loc):
    x, y, c, me = _my_place()
    rel = RELATIONS[mode]
    remote, local = [], []
    for t in range(len(lands)):
        for i, k in enumerate(rel):
            peer, peer_lin = _peer(x, y, c, k)
            if mode == "exchange":
                src, dst, to = srcs[t].at[peer_lin], lands[t].at[me], peer
            elif mode in ("gather", "gather_all"):
                src, dst, to = srcs[t], lands[t].at[me], peer
            else:
                src, dst, to = lands[t].at[peer_lin], lands[t].at[peer_lin], _peer(x, y, c, 1)[0]
            remote.append(pltpu.make_async_remote_copy(
                src_ref=src, dst_ref=dst, send_sem=send.at[t * len(rel) + i], recv_sem=recv.at[t * len(rel) + i],
                device_id=to, device_id_type=MESH))
        if mode == "exchange":
            local.append(pltpu.make_async_copy(srcs[t].at[me], lands[t].at[me], loc.at[t]))
        elif mode in ("gather", "gather_all"):
            local.append(pltpu.make_async_copy(srcs[t], lands[t].at[me], loc.at[t]))
    return remote, local


def _slabs_start(mode, arrays, after, name):
    n = len(arrays)
    if mode == "forward":
        thru = list(arrays)
    else:
        shapes = [a.shape if mode == "exchange" else (N_DEV,) + a.shape for a in arrays]
        thru = list(arrays) + [lax.empty(s, a.dtype) for s, a in zip(shapes, arrays)]
    m = len(thru)
    n_sem = n * len(RELATIONS[mode])

    def body(*refs):
        srcs, lands = refs[:n], refs[m - n:m]
        send, recv, loc = refs[m + 1:m + 4]
        remote, local = _slab_copies(mode, srcs, lands, send, recv, loc)
        for cp in remote + local:
            cp.start()
        refs[-1][...] = jnp.zeros_like(refs[-1])

    return pl.pallas_call(
        body, name=name,
        out_shape=(pltpu.SemaphoreType.DMA((n_sem,)), pltpu.SemaphoreType.DMA((n_sem,)),
                   pltpu.SemaphoreType.DMA((n,)),
                   *[pltpu.HBM(a.shape, a.dtype) for a in thru],
                   _sds((1, D), F32)),
        in_specs=[HBM_SPEC] * m + [pl.BlockSpec(memory_space=pl.ANY)],
        out_specs=(SEM_SPEC, SEM_SPEC, SEM_SPEC, *[HBM_SPEC] * m, pl.BlockSpec(memory_space=pltpu.VMEM)),
        input_output_aliases={t: 3 + t for t in range(m)},
        compiler_params=pltpu.CompilerParams(has_side_effects=EFFECT),
    )(*[pltpu.with_memory_space_constraint(a, pltpu.HBM) for a in thru], after)


def _slabs_wait(mode, n, started, after, name):
    sems = started[0:3]
    thru = started[3:-1]
    m = len(thru)

    def body(*refs):
        srcs, lands = refs[:n], refs[m - n:m]
        remote, local = _slab_copies(mode, srcs, lands, *refs[m:m + 3])
        for cp in remote:
            cp.wait_send()
            cp.wait_recv()
        for cp in local:
            cp.wait()

    res = pl.pallas_call(
        body, name=name,
        out_shape=tuple(pltpu.HBM(a.shape, a.dtype) for a in thru),
        in_specs=[HBM_SPEC] * m + [SEM_SPEC] * 3 + [pl.BlockSpec(memory_space=pl.ANY)],
        out_specs=tuple([HBM_SPEC] * m),
        input_output_aliases={t: t for t in range(m)},
        compiler_params=pltpu.CompilerParams(has_side_effects=EFFECT),
    )(*thru, *sems, after)
    return list(res[m - n:m])


def _ada_forward(small8, w_ada, b_ada64):
    sw = small8.shape[1]

    def body(sm_ref, w_ref, b_ref, gath_ref, ada_ref, part_ref, send1, recv1, send2, recv2):
        x, y, c, me = _my_place()
        row_me = pl.multiple_of(me * 8, 8)
        gath_ref[pl.ds(row_me, 8), :] = sm_ref[...]
        first = []
        for k in range(1, N_DEV):
            peer, _ = _peer(x, y, c, k)
            cp = pltpu.make_async_remote_copy(
                src_ref=sm_ref, dst_ref=gath_ref.at[pl.ds(row_me, 8), :], send_sem=send1.at[k - 1],
                recv_sem=recv1.at[k - 1], device_id=peer, device_id_type=MESH)
            cp.start()
            first.append(cp)
        for cp in first:
            cp.wait()
        cs = gath_ref[:, 0:D]
        cs = cs * jax.nn.sigmoid(cs)
        part_ref[...] = jnp.dot(cs, w_ref[...], preferred_element_type=F32, precision=HIGH)
        ada_ref[pl.ds(row_me, 8), :] = part_ref[pl.ds(row_me, 8), :]
        second = []
        for k in range(1, N_DEV):
            peer, peer_lin = _peer(x, y, c, k)
            cp = pltpu.make_async_remote_copy(
                src_ref=part_ref.at[pl.ds(pl.multiple_of(peer_lin * 8, 8), 8), :],
                dst_ref=ada_ref.at[pl.ds(row_me, 8), :], send_sem=send2.at[k - 1],
                recv_sem=recv2.at[k - 1], device_id=peer, device_id_type=MESH)
            cp.start()
            second.append(cp)
        for cp in second:
            cp.wait()
        ada_ref[...] = ada_ref[...] + b_ref[...]

    vm = pl.BlockSpec(memory_space=pltpu.VMEM)
    return pl.pallas_call(
        body, name="ada_forward",
        out_shape=[_sds((8 * N_DEV, sw), F32), _sds((8 * N_DEV, ADA_W), F32)],
        in_specs=[vm, vm, vm], out_specs=[vm, vm],
        scratch_shapes=[pltpu.VMEM((8 * N_DEV, ADA_W), F32)] + [pltpu.SemaphoreType.DMA((7,))] * 4,
        compiler_params=pltpu.CompilerParams(vmem_limit_bytes=32 * MIB),
    )(small8, w_ada, b_ada64)


def _sum_slabs(land):
    def body(l_ref, o_ref):
        acc = l_ref[0]
        for j in range(1, N_DEV):
            acc = acc + l_ref[j]
        o_ref[...] = acc

    vm = pl.BlockSpec(memory_space=pltpu.VMEM)
    return pl.pallas_call(body, name="sum_slabs", out_shape=_sds(land.shape[1:], F32), in_specs=[vm], out_specs=vm,
                          compiler_params=pltpu.CompilerParams(vmem_limit_bytes=32 * MIB))(land)


def _small_allreduce(vectors):
    n = len(vectors)

    def body(*refs):
        v_refs, (sum_ref, gath_ref, pack, send, recv) = refs[:n], refs[n:]
        x, y, c, me = _my_place()
        for k in range(n):
            pack[k:k + 1, :] = v_refs[k][...]
        gath_ref[me] = pack[...]
        cps = []
        for k in range(1, N_DEV):
            peer, _ = _peer(x, y, c, k)
            cp = pltpu.make_async_remote_copy(
                src_ref=pack, dst_ref=gath_ref.at[me], send_sem=send.at[k - 1],
                recv_sem=recv.at[k - 1], device_id=peer, device_id_type=MESH)
            cp.start()
            cps.append(cp)
        for cp in cps:
            cp.wait()
        acc = gath_ref[0]
        for j in range(1, N_DEV):
            acc = acc + gath_ref[j]
        sum_ref[...] = acc

    vm = pl.BlockSpec(memory_space=pltpu.VMEM)
    return pl.pallas_call(
        body, name="small_allreduce",
        out_shape=[_sds((n, D), F32), _sds((N_DEV, n, D), F32)],
        in_specs=[vm] * n, out_specs=[vm, vm],
        scratch_shapes=[pltpu.VMEM((n, D), F32), pltpu.SemaphoreType.DMA((7,)), pltpu.SemaphoreType.DMA((7,))],
    )(*vectors)


F_TILES = tuple((f0, min(512, D_FF - f0)) for f0 in range(0, D_FF, 512))
F_TILES_NARROW = tuple((f0, 256) for f0 in range(0, D_FF, 256))


def _swiglu_tile(n, wt_ref, f0, tf):
    g = _dot_nt(n, wt_ref[f0:f0 + tf, :])
    u = _dot_nt(n, wt_ref[D_FF + f0:D_FF + f0 + tf, :])
    sg = jax.nn.sigmoid(g)
    silu = g * sg
    return (u * (sg * (1.0 + g * (1.0 - sg)))).astype(BF), silu.astype(BF), (silu * u).astype(BF)


def _ffn_in(h, sh, sc, gp, wt, name):
    S = h.shape[0]
    R = min(512, S)

    def body(h_ref, sh_ref, sc_ref, gp_ref, w_ref, n_ref, dg_ref, sl_ref, a_ref):
        for r0 in range(0, R, CHUNK):
            rows = slice(r0, r0 + CHUNK)
            n = _prenorm(h_ref[rows, :], gp_ref[...], sc_ref[...], sh_ref[...]).astype(BF)
            n_ref[rows, :] = n
            for f0, tf in F_TILES_NARROW:
                dg_ref[rows, f0:f0 + tf], sl_ref[rows, f0:f0 + tf], a_ref[rows, f0:f0 + tf] = _swiglu_tile(
                    n, w_ref, f0, tf)

    vec = _const((1, D))
    rows_ = lambda w_: pl.BlockSpec((R, w_), lambda i: (i, 0))
    return pl.pallas_call(
        body, name=name, grid=(S // R,),
        out_shape=[_sds((S, D), BF)] + [_sds((S, D_FF), BF)] * 3,
        in_specs=[rows_(D), vec, vec, vec, _resident((2 * D_FF, D))],
        out_specs=[rows_(D), rows_(D_FF), rows_(D_FF), rows_(D_FF)],
        compiler_params=_cp(1, 56),
    )(*_in_hbm(h), sh, sc, gp, *_in_hbm(wt))


def _ffn_out(a, w, h, gate, gp, name, target=None):
    S = h.shape[0]
    R = min(512, S)
    with_loss = target is not None

    def body(a_ref, w_ref, h_ref, gate_ref, gp_ref, *rest):
        if with_loss:
            t_ref, out_ref, y_ref, tot_ref = rest

            @pl.when(pl.program_id(0) == 0)
            def _():
                tot_ref[...] = jnp.zeros_like(tot_ref)
        else:
            out_ref, y_ref = rest
        for r0 in range(0, R, CHUNK):
            rows = slice(r0, r0 + CHUNK)
            y = _dot(a_ref[rows, :], w_ref[...])
            y_ref[rows, :] = y.astype(BF)
            hn = h_ref[rows, :] + (0.5 * gate_ref[...]) * (y * _rms_r(y) * gp_ref[...])
            if with_loss:
                e = hn - t_ref[rows, :]
                out_ref[rows, :] = e * (1.0 / D)
                tot_ref[...] += jnp.sum(jnp.sum(e * e, axis=1, keepdims=True), axis=0, keepdims=True)
            else:
                out_ref[rows, :] = hn

    vec = _const((1, D))
    rows_ = lambda w_: pl.BlockSpec((R, w_), lambda i: (i, 0))
    return pl.pallas_call(
        body, name=name, grid=(S // R,),
        out_shape=[_sds((S, D), F32), _sds((S, D), BF)] + ([_sds((1, 1), F32)] if with_loss else []),
        in_specs=[rows_(D_FF), _resident((D_FF, D)), rows_(D), vec, vec] + ([rows_(D)] if with_loss else []),
        out_specs=[rows_(D), rows_(D)] + ([_const((1, 1))] if with_loss else []),
        compiler_params=_cp(1, 48),
    )(*_in_hbm(a, w, h), gate, gp, *(_in_hbm(target) if with_loss else ()))


def _ffn_out_bwd(dh, y, dsilu_u, silu, w, gate, gp, name, after=()):
    S = dh.shape[0]
    R = min(512, S)

    def body(dh_ref, y_ref, g_ref, u_ref, w_ref, gate_ref, gp_ref, dy_ref, dgu_ref, dgate_ref, dgp_ref):
        @pl.when(pl.program_id(0) == 0)
        def _():
            dgate_ref[...] = jnp.zeros_like(dgate_ref)
            dgp_ref[...] = jnp.zeros_like(dgp_ref)
        for r0 in range(0, R, CHUNK):
            rows = slice(r0, r0 + CHUNK)
            dy, dgate, dgp = _postnorm_bwd(dh_ref[rows, :], y_ref[rows, :], gate_ref[...], gp_ref[...], 0.5)
            dgate_ref[...] += dgate
            dgp_ref[...] += dgp
            dyb = dy.astype(BF)
            dy_ref[rows, :] = dyb
            for f0, tf in F_TILES:
                da = _dot_nt(dyb, w_ref[f0:f0 + tf, :])
                dgu_ref[rows, f0:f0 + tf] = (da * g_ref[rows, f0:f0 + tf].astype(F32)).astype(BF)
                dgu_ref[rows, D_FF + f0:D_FF + f0 + tf] = (da * u_ref[rows, f0:f0 + tf].astype(F32)).astype(BF)

    vec = _const((1, D))
    rows_ = lambda w_: pl.BlockSpec((R, w_), lambda i: (i, 0))
    body, after_specs = _behind(body, 7, after)
    return pl.pallas_call(
        body, name=name, grid=(S // R,),
        out_shape=[_sds((S, D), BF), _sds((S, 2 * D_FF), BF), _sds((1, D), F32), _sds((1, D), F32)],
        in_specs=[rows_(D), rows_(D), rows_(D_FF), rows_(D_FF), _resident((D_FF, D)), vec, vec] + after_specs,
        out_specs=[rows_(D), rows_(2 * D_FF), vec, vec],
        compiler_params=_cp(1, 56),
    )(*_in_hbm(dh, y, dsilu_u, silu, w), gate, gp, *after)


def _ffn_dn(dgu, wt, h, dh, sc, gp, name, after=()):
    S = h.shape[0]
    R = min(512, S)

    def body(dgu_ref, w_ref, h_ref, dh_ref, sc_ref, gp_ref, out_ref, dsh_ref, dsc_ref, dgp_ref):
        @pl.when(pl.program_id(0) == 0)
        def _():
            dsh_ref[...] = jnp.zeros_like(dsh_ref)
            dsc_ref[...] = jnp.zeros_like(dsc_ref)
            dgp_ref[...] = jnp.zeros_like(dgp_ref)

        for r0 in range(0, R, CHUNK):
            rows = slice(r0, r0 + CHUNK)
            dn = _dot(dgu_ref[rows, :], w_ref[...])
            dx, dsh, dsc, dgp = _prenorm_bwd(dn, h_ref[rows, :], gp_ref[...], sc_ref[...])
            out_ref[rows, :] = dh_ref[rows, :] + dx
            dsh_ref[...] += dsh
            dsc_ref[...] += dsc
            dgp_ref[...] += dgp

    vec = _const((1, D))
    rows_ = lambda w_: pl.BlockSpec((R, w_), lambda i: (i, 0))
    body, after_specs = _behind(body, 6, after)
    return pl.pallas_call(
        body, name=name, grid=(S // R,),
        out_shape=[_sds((S, D), F32)] + [_sds((1, D), F32)] * 3,
        in_specs=[rows_(2 * D_FF), _resident((2 * D_FF, D)), rows_(D), rows_(D), vec, vec] + after_specs,
        out_specs=[rows_(D), vec, vec, vec],
        compiler_params=_cp(1, 56),
    )(*_in_hbm(dgu, wt, h, dh), sc, gp, *after)


def _ffn_bwd(dh, y, dsilu_u, silu, w, wt, h, gate, gpost, sc, gpre, name):
    S = dh.shape[0]
    R = min(256, S)

    def body(dh_ref, y_ref, g_ref, u_ref, w_ref, wt_ref, h_ref, gate_ref, gpost_ref, sc_ref, gpre_ref,
             dy_ref, dgu_ref, out_ref, dgate_ref, dgpost_ref, dsh_ref, dsc_ref, dgpre_ref):
        @pl.when(pl.program_id(0) == 0)
        def _():
            for r in (dgate_ref, dgpost_ref, dsh_ref, dsc_ref, dgpre_ref):
                r[...] = jnp.zeros_like(r)
        dhh = dh_ref[...]
        dy, dgate, dgpost = _postnorm_bwd(dhh, y_ref[...], gate_ref[...], gpost_ref[...], 0.5)
        dgate_ref[...] += dgate
        dgpost_ref[...] += dgpost
        dyb = dy.astype(BF)
        dy_ref[...] = dyb
        for f0, tf in F_TILES:
            da = _dot_nt(dyb, w_ref[f0:f0 + tf, :])
            dgu_ref[:, f0:f0 + tf] = (da * g_ref[:, f0:f0 + tf].astype(F32)).astype(BF)
            dgu_ref[:, D_FF + f0:D_FF + f0 + tf] = (da * u_ref[:, f0:f0 + tf].astype(F32)).astype(BF)
        dn = _dot(dgu_ref[...], wt_ref[...])
        dx, dsh, dsc, dgpre = _prenorm_bwd(dn, h_ref[...], gpre_ref[...], sc_ref[...])
        out_ref[...] = dhh + dx
        dsh_ref[...] += dsh
        dsc_ref[...] += dsc
        dgpre_ref[...] += dgpre

    vec = _const((1, D))
    rows_ = lambda w_: pl.BlockSpec((R, w_), lambda i: (i, 0))
    return pl.pallas_call(
        body, name=name, grid=(S // R,),
        out_shape=[_sds((S, D), BF), _sds((S, 2 * D_FF), BF), _sds((S, D), F32)] + [_sds((1, D), F32)] * 5,
        in_specs=[rows_(D), rows_(D), rows_(D_FF), rows_(D_FF), _resident((D_FF, D)), _resident((2 * D_FF, D)),
                  rows_(D), vec, vec, vec, vec],
        out_specs=[rows_(D), rows_(2 * D_FF), rows_(D)] + [vec] * 5,
        compiler_params=_cp(1, 56),
    )(*_in_hbm(dh, y, dsilu_u, silu, w, wt, h), gate, gpost, sc, gpre)


def _tn_matmul(a, b, name, tm=None):
    S, M_all = a.shape
    N = b.shape[1]
    M = M_all if tm is None else tm
    GA = M_all // M
    ts = min(2048 if M * N <= 2 * D * D else 1024, S // 2)
    nk = S // ts
    assert nk >= 2 and nk * ts == S
    chunks = [(m0, min(CHUNK, M - m0)) for m0 in range(0, M, CHUNK)]

    def body(a_ref, b_ref, o_ref, acc):
        k = pl.program_id(1)

        @pl.when(k == 0)
        def _():
            for m0, mc in chunks:
                acc[m0:m0 + mc, :] = _dot_tn(a_ref[:, m0:m0 + mc], b_ref[...])

        @pl.when(jnp.logical_and(k > 0, k < nk - 1))
        def _():
            for m0, mc in chunks:
                acc[m0:m0 + mc, :] += _dot_tn(a_ref[:, m0:m0 + mc], b_ref[...])

        @pl.when(k == nk - 1)
        def _():
            for m0, mc in chunks:
                o_ref[m0:m0 + mc, :] = (acc[m0:m0 + mc, :] + _dot_tn(a_ref[:, m0:m0 + mc], b_ref[...])).astype(BF)

    return pl.pallas_call(
        body, name=name, grid=(GA, nk),
        out_shape=_sds((M_all, N), BF),
        in_specs=[pl.BlockSpec((ts, M), lambda ga, k: (k, ga)), pl.BlockSpec((ts, N), lambda ga, k: (k, 0))],
        out_specs=pl.BlockSpec((M, N), lambda ga, k: (ga, 0)),
        scratch_shapes=[pltpu.VMEM((M, N), F32)],
        compiler_params=_cp(2, 56),
    )(*_in_hbm(a, b))


def _mix_in(h, sh, sc, gp, w, wq):
    S = h.shape[0]
    R = min(512, S)

    def body(h_ref, sh_ref, sc_ref, gp_ref, w_ref, wq_ref, n_ref, qkv_ref, zg_ref, gates_ref):
        for r0 in range(0, R, CHUNK):
            rows = slice(r0, r0 + CHUNK)
            nb = _prenorm(h_ref[rows, :], gp_ref[...], sc_ref[...], sh_ref[...]).astype(BF)
            n_ref[rows, :] = nb
            qkv_ref[rows, 0:Q_W] = _dot_nt(nb, wq_ref[...]).astype(BF)
            qkv_ref[rows, Q_W:QKV_W] = _dot_nt(nb, w_ref[Q_W:QKV_W, :]).astype(BF)
            zg_ref[rows, :] = _dot_nt(nb, w_ref[ZG_OFF:GATE_OFF, :]).astype(BF)
            gates_ref[rows, :] = jax.nn.sigmoid(_dot_nt(nb, w_ref[GATE_OFF:IN_W, :])).astype(BF)

    vec = _const((1, D))
    rows = lambda w_: pl.BlockSpec((R, w_), lambda i: (i, 0))
    return pl.pallas_call(
        body, name="mix_in", grid=(S // R,),
        out_shape=[_sds((S, D), BF), _sds((S, QKV_W), BF), _sds((S, 2 * G_W), BF), _sds((S, 2 * D), BF)],
        in_specs=[rows(D), vec, vec, vec, _resident((IN_W, D)), _resident((Q_W, D))],
        out_specs=[rows(D), rows(QKV_W), rows(2 * G_W), rows(2 * D)],
        compiler_params=_cp(1, 48),
    )(*_in_hbm(h), sh, sc, gp, *_in_hbm(w, wq))


def _bias_table(rel_bias, bucket):
    def body(rel_ref, bk_ref, out_ref):
        bk = bk_ref[...]
        qi = lax.broadcasted_iota(jnp.int32, (BLK, 2 * BLK), 0)
        kj = lax.broadcasted_iota(jnp.int32, (BLK, 2 * BLK), 1)
        dist = qi + BLK - kj
        window = (dist >= 0) & (dist < BLK)
        for h in range(N_HEADS):
            acc = jnp.zeros((BLK, 2 * BLK), F32)
            for b in range(N_BUCKETS):
                acc = jnp.where(bk == b, rel_ref[b, h], acc)
            out_ref[h // GROUP, pl.ds((h % GROUP) * BLK, BLK), :] = jnp.where(window, acc, NEG)

    return pl.pallas_call(
        body, name="bias_table",
        out_shape=_sds((N_KV, GROUP * BLK, 2 * BLK), F32),
        in_specs=[pl.BlockSpec(memory_space=pltpu.SMEM), pl.BlockSpec(memory_space=pltpu.VMEM)],
        out_specs=pl.BlockSpec(memory_space=pltpu.VMEM),
    )(rel_bias, bucket)


ATT_TB = 8


HEAD_ROWS = N_HEADS * BLK


def _pair_heads(w):
    return jnp.transpose(w.reshape(N_KV, GROUP, HD, w.shape[1]), (1, 0, 2, 3)).reshape(w.shape)


def _unpair_heads(w):
    return jnp.transpose(w.reshape(GROUP, N_KV, HD, w.shape[1]), (1, 0, 2, 3)).reshape(w.shape)


def _halves(x, scale=1.0):
    low = lax.broadcasted_iota(jnp.int32, x.shape, 1) < HD
    xf = x.astype(F32) * scale
    return jnp.where(low, xf, 0.0).astype(BF), jnp.where(low, 0.0, xf).astype(BF)


def _stack_heads(x, scale=1.0):
    halves = [_halves(x[:, g * 128:(g + 1) * 128], scale) for g in range(GROUP)]
    return jnp.concatenate([lo for lo, _ in halves] + [hi for _, hi in halves], axis=0)


def _attn_probs(q, kvc, kvp, bias_ref, sink_ref, has_prev):
    kv2 = jnp.concatenate([kvp, kvc], axis=0)
    kboth, vboth = kv2[:, 0:KV_W], kv2[:, KV_W:2 * KV_W]
    qpad = _stack_heads(q, SCALE)
    s = _dot_nt(qpad, kboth) + bias_ref[...]
    if has_prev is not None:
        col = lax.broadcasted_iota(jnp.int32, (HEAD_ROWS, 2 * BLK), 1)
        s = jnp.where((col >= BLK) | has_prev, s, NEG)
    row_head = lax.broadcasted_iota(jnp.int32, (HEAD_ROWS, 1), 0) // BLK
    sink = jnp.zeros((HEAD_ROWS, 1), F32)
    for h in range(N_HEADS):
        sink = jnp.where(row_head == h, sink_ref[h], sink)
    m = jnp.maximum(jnp.max(s, axis=1, keepdims=True), sink)
    p = jnp.exp(s - m)
    e_sink = jnp.exp(sink - m)
    inv = 1.0 / (jnp.sum(p, axis=1, keepdims=True) + e_sink)
    return qpad, kboth, vboth, p * inv, e_sink * inv


def _attn_fwd(qkv, bias, sinks):
    S = qkv.shape[0]
    tb = min(ATT_TB, S // BLK)
    T = tb * BLK

    def body(sink_ref, q_ref, kv_ref, kvp_ref, bias_ref, o_ref):
        step = pl.program_id(0)
        for j in range(tb):
            rows = slice(j * BLK, (j + 1) * BLK)
            kvp = kvp_ref[...] if j == 0 else kv_ref[(j - 1) * BLK:j * BLK, :]
            has_prev = (step > 0) if j == 0 else None
            _, _, vboth, prob, _ = _attn_probs(q_ref[rows, :], kv_ref[rows, :], kvp, bias_ref, sink_ref, has_prev)
            pb = prob.astype(BF)
            v_low, v_high = _halves(vboth)
            half = HEAD_ROWS // 2
            o = _dot(pb[0:half], v_low) + _dot(pb[half:HEAD_ROWS], v_high)
            for g in range(GROUP):
                o_ref[rows, g * 128:(g + 1) * 128] = o[g * BLK:(g + 1) * BLK].astype(BF)

    return pl.pallas_call(
        body, name="attn_fwd", grid=(S // T,),
        out_shape=_sds((S, Q_W), BF),
        in_specs=[pl.BlockSpec(memory_space=pltpu.SMEM),
                  pl.BlockSpec((T, Q_W), lambda i: (i, 0)),
                  pl.BlockSpec((T, 2 * KV_W), lambda i: (i, 2)),
                  pl.BlockSpec((BLK, 2 * KV_W), lambda i: (jnp.maximum(i * tb - 1, 0), 2)),
                  _const((HEAD_ROWS, 2 * BLK))],
        out_specs=pl.BlockSpec((T, Q_W), lambda i: (i, 0)),
        compiler_params=_cp(1, 32),
    )(sinks, *_in_hbm(qkv, qkv, qkv, bias))


def _attn_bwd(qkv, bias, sinks, do):
    S = qkv.shape[0]
    tb = min(ATT_TB, S // BLK)
    T = tb * BLK
    nt = S // T
    half = HEAD_ROWS // 2

    def body(sink_ref, q_ref, kv_ref, kvp_ref, bias_ref, do_ref, dq_ref, dkv_ref, dbias_ref, dsink_ref, carry):
        i = pl.program_id(0)

        @pl.when(i == 0)
        def _():
            carry[...] = jnp.zeros_like(carry)
            dbias_ref[...] = jnp.zeros_like(dbias_ref)
            dsink_ref[...] = jnp.zeros_like(dsink_ref)

        from_next = carry[...]
        head_row = lax.broadcasted_iota(jnp.int32, (N_HEADS, 128), 0)
        low = lax.broadcasted_iota(jnp.int32, (BLK, 128), 1) < HD
        for j in reversed(range(tb)):
            rows = slice(j * BLK, (j + 1) * BLK)
            kvp = kvp_ref[...] if j == 0 else kv_ref[(j - 1) * BLK:j * BLK, :]
            has_prev = (i < nt - 1) if j == 0 else None
            qpad, kboth, vboth, prob, p_sink = _attn_probs(q_ref[rows, :], kv_ref[rows, :], kvp, bias_ref, sink_ref,
                                                           has_prev)
            pb = prob.astype(BF)
            dopad = _stack_heads(do_ref[rows, :])
            dp = _dot_nt(dopad, vboth)
            delta = jnp.sum(prob * dp, axis=1, keepdims=True)
            ds = prob * (dp - delta)
            dbias_ref[...] += ds
            sink_term = p_sink * delta
            dsink_rows = jnp.zeros((N_HEADS, 128), F32)
            for h in range(N_HEADS):
                val = -jnp.sum(sink_term[h * BLK:(h + 1) * BLK], axis=0, keepdims=True)
                dsink_rows = jnp.where(head_row == h, val, dsink_rows)
            dsink_ref[...] += dsink_rows
            dsb = ds.astype(BF)
            dqpad = _dot(dsb, kboth) * SCALE
            for g in range(GROUP):
                dq_ref[rows, g * 128:(g + 1) * 128] = jnp.where(
                    low, dqpad[g * BLK:(g + 1) * BLK], dqpad[half + g * BLK:half + (g + 1) * BLK]).astype(BF)
            dkv2 = jnp.concatenate([jnp.transpose(_dot_tn(qpad, dsb)),
                                    jnp.transpose(_dot_tn(dopad, pb))], axis=1)
            dkv_ref[rows, :] = (dkv2[BLK:2 * BLK] + from_next).astype(BF)
            from_next = dkv2[0:BLK]
        carry[...] = from_next

    return pl.pallas_call(
        body, name="attn_bwd", grid=(nt,),
        out_shape=[_sds((S, Q_W), BF), _sds((S, 2 * KV_W), BF),
                   _sds((HEAD_ROWS, 2 * BLK), F32), _sds((N_HEADS, 128), F32)],
        in_specs=[pl.BlockSpec(memory_space=pltpu.SMEM),
                  pl.BlockSpec((T, Q_W), lambda i: (nt - 1 - i, 0)),
                  pl.BlockSpec((T, 2 * KV_W), lambda i: (nt - 1 - i, 2)),
                  pl.BlockSpec((BLK, 2 * KV_W), lambda i: (jnp.maximum((nt - 1 - i) * tb - 1, 0), 2)),
                  _const((HEAD_ROWS, 2 * BLK)),
                  pl.BlockSpec((T, Q_W), lambda i: (nt - 1 - i, 0))],
        out_specs=[pl.BlockSpec((T, Q_W), lambda i: (nt - 1 - i, 0)),
                   pl.BlockSpec((T, 2 * KV_W), lambda i: (nt - 1 - i, 0)),
                   _const((HEAD_ROWS, 2 * BLK)), _const((N_HEADS, 128))],
        scratch_shapes=[pltpu.VMEM((BLK, 2 * KV_W), F32)],
        compiler_params=_cp(1, 32),
    )(sinks, *_in_hbm(qkv, qkv, qkv, bias, do))


def _rel_bias_grad(dbias, bucket):
    def body(db_ref, bk_ref, out_ref):
        bk = bk_ref[...]
        lane = lax.broadcasted_iota(jnp.int32, (1, 128), 1)
        for h in range(N_HEADS):
            d = db_ref[h // GROUP, pl.ds((h % GROUP) * BLK, BLK), :]
            row = jnp.zeros((1, 128), F32)
            for b in range(N_BUCKETS):
                tot = jnp.sum(jnp.sum(jnp.where(bk == b, d, 0.0), axis=1, keepdims=True), axis=0, keepdims=True)
                row = jnp.where(lane == b, tot, row)
            out_ref[pl.ds(h, 1), :] = row

    vm = pl.BlockSpec(memory_space=pltpu.VMEM)
    return pl.pallas_call(body, name="rel_bias_grad", out_shape=_sds((N_HEADS, 128), F32),
                          in_specs=[vm, vm], out_specs=vm)(dbias, bucket)


def _gmlp_parts(zg, lg_ref, lb_ref):
    z = zg.astype(F32)
    ge = _gelu(z)
    u, vg = ge[:, 0:G_W], ge[:, G_W:2 * G_W]
    mu = jnp.mean(vg, axis=-1, keepdims=True)
    xc = vg - mu
    rstd = lax.rsqrt(jnp.mean(xc * xc, axis=-1, keepdims=True) + EPS)
    xh = xc * rstd
    return z, u, xh, rstd, xh * lg_ref[...] + lb_ref[...]


def _causal_weights(ws_ref, wc):
    t = lax.broadcasted_iota(jnp.int32, (BLK, BLK), 0)
    s = lax.broadcasted_iota(jnp.int32, (BLK, BLK), 1)
    for g in range(N_HEADS):
        wc[g] = jnp.where(s <= t, ws_ref[g], 0.0).astype(BF)


def _spatial(vb, wc, bst_ref, p, low):
    xp = vb[:, p * 128:(p + 1) * 128]
    s0 = _dot(wc[2 * p], xp) + bst_ref[:, 2 * p:2 * p + 1]
    s1 = _dot(wc[2 * p + 1], xp) + bst_ref[:, 2 * p + 1:2 * p + 2]
    return xp, jnp.where(low, s0, s1)


def _gmlp_fwd(zg, lg, lb, ws, bst):
    S = zg.shape[0]
    tb = min(ATT_TB, S // BLK)
    T = tb * BLK

    def body(zg_ref, lg_ref, lb_ref, ws_ref, bst_ref, o_ref, wc):
        @pl.when(pl.program_id(0) == 0)
        def _():
            _causal_weights(ws_ref, wc)
        low = lax.broadcasted_iota(jnp.int32, (BLK, 128), 1) < HD
        for j in range(tb):
            rows = slice(j * BLK, (j + 1) * BLK)
            _, u, _, _, vln = _gmlp_parts(zg_ref[rows, :], lg_ref, lb_ref)
            vb = vln.astype(BF)
            for p in range(4):
                _, sp = _spatial(vb, wc, bst_ref, p, low)
                o_ref[rows, p * 128:(p + 1) * 128] = (u[:, p * 128:(p + 1) * 128] * sp).astype(BF)

    return pl.pallas_call(
        body, name="gmlp_fwd", grid=(S // T,),
        out_shape=_sds((S, G_W), BF),
        in_specs=[pl.BlockSpec((T, 2 * G_W), lambda i: (i, 0)), _const((1, G_W)), _const((1, G_W)),
                  _const((N_HEADS, BLK, BLK)), _const((BLK, N_HEADS))],
        out_specs=pl.BlockSpec((T, G_W), lambda i: (i, 0)),
        scratch_shapes=[pltpu.VMEM((N_HEADS, BLK, BLK), BF)],
        compiler_params=_cp(1, 32),
    )(*_in_hbm(zg), lg, lb, ws, bst)


def _gmlp_bwd(zg, d_out, lg, lb, ws, bst):
    S = zg.shape[0]
    tb = min(ATT_TB, S // BLK)
    T = tb * BLK
    nb = S // T

    def body(zg_ref, d_ref, lg_ref, lb_ref, ws_ref, bst_ref, dzg_ref, dws_ref, dbs_ref, dlg_ref, dlb_ref, wc, dbacc):
        i = pl.program_id(0)

        @pl.when(i == 0)
        def _():
            _causal_weights(ws_ref, wc)
            dws_ref[...] = jnp.zeros_like(dws_ref)
            dlg_ref[...] = jnp.zeros_like(dlg_ref)
            dlb_ref[...] = jnp.zeros_like(dlb_ref)
            dbacc[...] = jnp.zeros_like(dbacc)

        low = lax.broadcasted_iota(jnp.int32, (BLK, 128), 1) < HD
        for j in range(tb):
            rows = slice(j * BLK, (j + 1) * BLK)
            z, u, xh, rstd, vln = _gmlp_parts(zg_ref[rows, :], lg_ref, lb_ref)
            vb = vln.astype(BF)
            d = d_ref[rows, :].astype(F32)
            du_parts, dvln_parts = [], []
            for p in range(4):
                xp, sp = _spatial(vb, wc, bst_ref, p, low)
                dp = d[:, p * 128:(p + 1) * 128]
                du_parts.append(dp * sp)
                dsp = dp * u[:, p * 128:(p + 1) * 128]
                dbacc[:, p * 128:(p + 1) * 128] += dsp
                d0 = jnp.where(low, dsp, 0.0).astype(BF)
                d1 = jnp.where(low, 0.0, dsp).astype(BF)
                dws_ref[2 * p] += _dot_nt(d0, xp)
                dws_ref[2 * p + 1] += _dot_nt(d1, xp)
                dvln_parts.append(_dot_tn(wc[2 * p], d0) + _dot_tn(wc[2 * p + 1], d1))
            dvln = jnp.concatenate(dvln_parts, axis=1)
            dlg_ref[...] += _colsum(dvln * xh)
            dlb_ref[...] += _colsum(dvln)
            dxh = dvln * lg_ref[...]
            dvg = rstd * (dxh - jnp.mean(dxh, axis=-1, keepdims=True)
                          - xh * jnp.mean(dxh * xh, axis=-1, keepdims=True))
            dge = jnp.concatenate(du_parts + [dvg], axis=1)
            dzg_ref[rows, :] = (dge * _gelu_grad(z)).astype(BF)

        @pl.when(i == nb - 1)
        def _():
            t = lax.broadcasted_iota(jnp.int32, (BLK, BLK), 0)
            s = lax.broadcasted_iota(jnp.int32, (BLK, BLK), 1)
            for g in range(N_HEADS):
                dws_ref[g] = jnp.where(s <= t, dws_ref[g], 0.0)
            grp = lax.broadcasted_iota(jnp.int32, (N_HEADS, G_W), 0)
            lane = lax.broadcasted_iota(jnp.int32, (N_HEADS, G_W), 1) // HD
            pick = jnp.where(grp == lane, 1.0, 0.0).astype(F32)
            dbs_ref[...] = lax.dot_general(pick, dbacc[...], (((1,), (1,)), ((), ())),
                                           preferred_element_type=F32, precision=HIGH)

    return pl.pallas_call(
        body, name="gmlp_bwd", grid=(nb,),
        out_shape=[_sds((S, 2 * G_W), BF), _sds((N_HEADS, BLK, BLK), F32), _sds((N_HEADS, BLK), F32),
                   _sds((1, G_W), F32), _sds((1, G_W), F32)],
        in_specs=[pl.BlockSpec((T, 2 * G_W), lambda i: (i, 0)), pl.BlockSpec((T, G_W), lambda i: (i, 0)),
                  _const((1, G_W)), _const((1, G_W)), _const((N_HEADS, BLK, BLK)), _const((BLK, N_HEADS))],
        out_specs=[pl.BlockSpec((T, 2 * G_W), lambda i: (i, 0)), _const((N_HEADS, BLK, BLK)),
                   _const((N_HEADS, BLK)), _const((1, G_W)), _const((1, G_W))],
        scratch_shapes=[pltpu.VMEM((N_HEADS, BLK, BLK), BF), pltpu.VMEM((BLK, G_W), F32)],
        compiler_params=_cp(1, 32),
    )(*_in_hbm(zg, d_out), lg, lb, ws, bst)


def _mix_out(o, gm, gates, h, wa, wg, wo, gate, gp, after=()):
    S = h.shape[0]
    R = min(512, S)

    def body(o_ref, gm_ref, gates_ref, h_ref, wa_ref, wg_ref, wo_ref, gate_ref, gp_ref,
             ya_ref, yg_ref, ym_ref, y_ref, hn_ref):
        for r0 in range(0, R, CHUNK):
            rows = slice(r0, r0 + CHUNK)
            ya = _dot(o_ref[rows, :], wa_ref[...])
            yg = _dot(gm_ref[rows, :], wg_ref[...])
            ya_ref[rows, :] = ya.astype(BF)
            yg_ref[rows, :] = yg.astype(BF)
            ym = (gates_ref[rows, 0:D].astype(F32) * ya + gates_ref[rows, D:2 * D].astype(F32) * yg).astype(BF)
            ym_ref[rows, :] = ym
            y = _dot(ym, wo_ref[...])
            y_ref[rows, :] = y.astype(BF)
            hn_ref[rows, :] = h_ref[rows, :] + gate_ref[...] * (y * _rms_r(y) * gp_ref[...])

    vec = _const((1, D))
    rows = lambda w_: pl.BlockSpec((R, w_), lambda i: (i, 0))
    body, after_specs = _behind(body, 9, after)
    return pl.pallas_call(
        body, name="mix_out", grid=(S // R,),
        out_shape=[_sds((S, D), BF)] * 4 + [_sds((S, D), F32)],
        in_specs=[rows(Q_W), rows(G_W), rows(2 * D), rows(D), _resident((Q_W, D)), _resident((G_W, D)),
                  _resident((D, D)), vec, vec] + after_specs,
        out_specs=[rows(D)] * 5,
        compiler_params=_cp(1, 48),
    )(*_in_hbm(o, gm, gates, h, wa, wg, wo), gate, gp, *after)


def _mix_out_bwd(dh, y, ya, yg, gates, att, gm, ymix, wa, wg, wo, gate, gp, after=()):
    S = dh.shape[0]
    R = min(512, S)
    nb = S // R

    def body(dh_ref, y_ref, ya_ref, yg_ref, gates_ref, att_ref, gm_ref, ym_ref, wa_ref, wg_ref, wo_ref,
             gate_ref, gp_ref, dz_ref, do_ref, dgm_ref, dgate_ref, dgp_ref, gwo_ref, gwa_ref, gwg_ref,
             acc_o, acc_a, acc_g, dy_scr, dya_scr, dyg_scr):
        i = pl.program_id(0)

        @pl.when(i == 0)
        def _():
            for r in (dgate_ref, dgp_ref, acc_o, acc_a, acc_g):
                r[...] = jnp.zeros_like(r)
        for r0 in range(0, R, 2 * CHUNK):
            rows = slice(r0, min(r0 + 2 * CHUNK, R))
            dy, dgate, dgp = _postnorm_bwd(dh_ref[rows, :], y_ref[rows, :], gate_ref[...], gp_ref[...], 1.0)
            dgate_ref[...] += dgate
            dgp_ref[...] += dgp
            dyb = dy.astype(BF)
            dy_scr[rows, :] = dyb
            dym = _dot_nt(dyb, wo_ref[...])
            ga = gates_ref[rows, 0:D].astype(F32)
            gg = gates_ref[rows, D:2 * D].astype(F32)
            dya = (dym * ga).astype(BF)
            dyg = (dym * gg).astype(BF)
            dya_scr[rows, :] = dya
            dyg_scr[rows, :] = dyg
            dz_ref[rows, 0:D] = (dym * ya_ref[rows, :].astype(F32) * (ga * (1.0 - ga))).astype(BF)
            dz_ref[rows, D:2 * D] = (dym * yg_ref[rows, :].astype(F32) * (gg * (1.0 - gg))).astype(BF)
            do_ref[rows, :] = _dot_nt(dya, wa_ref[...]).astype(BF)
            dgm_ref[rows, :] = _dot_nt(dyg, wg_ref[...]).astype(BF)
        for m0 in range(0, D, CHUNK):
            acc_o[m0:m0 + CHUNK, :] += _dot_tn(ym_ref[:, m0:m0 + CHUNK], dy_scr[...])
        for m0 in range(0, Q_W, CHUNK):
            acc_a[m0:m0 + CHUNK, :] += _dot_tn(att_ref[:, m0:m0 + CHUNK], dya_scr[...])
            acc_g[m0:m0 + CHUNK, :] += _dot_tn(gm_ref[:, m0:m0 + CHUNK], dyg_scr[...])

        @pl.when(i == nb - 1)
        def _():
            for m0 in range(0, D, CHUNK):
                gwo_ref[m0:m0 + CHUNK, :] = acc_o[m0:m0 + CHUNK, :].astype(BF)
            for m0 in range(0, Q_W, CHUNK):
                gwa_ref[m0:m0 + CHUNK, :] = acc_a[m0:m0 + CHUNK, :].astype(BF)
                gwg_ref[m0:m0 + CHUNK, :] = acc_g[m0:m0 + CHUNK, :].astype(BF)

    vec = _const((1, D))
    rows = lambda w_: pl.BlockSpec((R, w_), lambda i: (i, 0))
    body, after_specs = _behind(body, 13, after)
    return pl.pallas_call(
        body, name="mix_out_bwd", grid=(nb,),
        out_shape=[_sds((S, 2 * D), BF), _sds((S, Q_W), BF), _sds((S, G_W), BF), _sds((1, D), F32),
                   _sds((1, D), F32), _sds((D, D), BF), _sds((Q_W, D), BF), _sds((G_W, D), BF)],
        in_specs=[rows(D), rows(D), rows(D), rows(D), rows(2 * D), rows(Q_W), rows(G_W), rows(D),
                  _resident((Q_W, D)), _resident((G_W, D)), _resident((D, D)), vec, vec] + after_specs,
        out_specs=[rows(2 * D), rows(Q_W), rows(G_W), vec, vec, _const((D, D)), _const((Q_W, D)),
                   _const((G_W, D))],
        scratch_shapes=[pltpu.VMEM((D, D), F32), pltpu.VMEM((Q_W, D), F32), pltpu.VMEM((G_W, D), F32)]
        + [pltpu.VMEM((R, D), BF)] * 3,
        compiler_params=_cp(1, 60),
    )(*_in_hbm(dh, y, ya, yg, gates, att, gm, ymix, wa, wg, wo), gate, gp, *after)


def _mix_dn(dq, dkv, dzg, dzgate, w, wq, h, dh, sc, gp, after=()):
    S = h.shape[0]
    R = min(512, S)

    def body(dq_ref, dkv_ref, dzg_ref, dzt_ref, w_ref, wq_ref, h_ref, dh_ref, sc_ref, gp_ref,
             out_ref, dsh_ref, dsc_ref, dgp_ref):
        @pl.when(pl.program_id(0) == 0)
        def _():
            dsh_ref[...] = jnp.zeros_like(dsh_ref)
            dsc_ref[...] = jnp.zeros_like(dsc_ref)
            dgp_ref[...] = jnp.zeros_like(dgp_ref)
        for r0 in range(0, R, CHUNK):
            rows = slice(r0, r0 + CHUNK)
            dn = _dot(dq_ref[rows, :], wq_ref[...])
            dn = dn + _dot(dkv_ref[rows, :], w_ref[Q_W:QKV_W, :])
            dn = dn + _dot(dzg_ref[rows, :], w_ref[ZG_OFF:GATE_OFF, :])
            dn = dn + _dot(dzt_ref[rows, :], w_ref[GATE_OFF:IN_W, :])
            dx, dsh, dsc, dgp = _prenorm_bwd(dn, h_ref[rows, :], gp_ref[...], sc_ref[...])
            out_ref[rows, :] = dh_ref[rows, :] + dx
            dsh_ref[...] += dsh
            dsc_ref[...] += dsc
            dgp_ref[...] += dgp

    vec = _const((1, D))
    rows = lambda w_: pl.BlockSpec((R, w_), lambda i: (i, 0))
    body, after_specs = _behind(body, 10, after)
    return pl.pallas_call(
        body, name="mix_dn", grid=(S // R,),
        out_shape=[_sds((S, D), F32)] + [_sds((1, D), F32)] * 3,
        in_specs=[rows(Q_W), rows(2 * KV_W), rows(2 * G_W), rows(2 * D), _resident((IN_W, D)),
                  _resident((Q_W, D)), rows(D), rows(D), vec, vec] + after_specs,
        out_specs=[rows(D), vec, vec, vec],
        compiler_params=_cp(1, 48),
    )(*_in_hbm(dq, dkv, dzg, dzgate, w, wq, h, dh), sc, gp, *after)


def _adamw_math(w, g, m, v):
    m2 = ADAM_B1 * m + (1.0 - ADAM_B1) * g
    v2 = ADAM_B2 * v + (1.0 - ADAM_B2) * (g * g)
    m_hat = m2 / (1.0 - ADAM_B1 ** ADAM_STEP)
    v_hat = v2 / (1.0 - ADAM_B2 ** ADAM_STEP)
    delta = -ADAM_LR * (m_hat / (jnp.sqrt(v_hat) + ADAM_EPS) + ADAM_WD * w)
    return delta, m2, v2


def _row_tile(rows, cols):
    best = None
    for t in range(16, rows + 1, 16):
        if rows % t == 0 and t * cols <= 256 * 1024:
            best = t
    return best if best is not None else rows


def _adamw_sharded(landing, w, m, v, name):
    r, c = w.shape
    tr = _row_tile(r, c)

    def body(l_ref, w_ref, m_ref, v_ref, g_ref, d_ref, m2_ref, v2_ref):
        g = l_ref[0].astype(F32)
        for j in range(1, N_DEV):
            g = g + l_ref[j].astype(F32)
        delta, m2, v2 = _adamw_math(w_ref[...], g, m_ref[...], v_ref[...])
        g_ref[...] = g
        d_ref[...] = delta
        m2_ref[...] = m2
        v2_ref[...] = v2

    row = pl.BlockSpec((tr, c), lambda i: (i, 0))
    return pl.pallas_call(
        body, name=name, grid=(r // tr,),
        out_shape=[_sds((r, c), F32)] * 4,
        in_specs=[pl.BlockSpec((N_DEV, tr, c), lambda i: (0, i, 0)), row, row, row],
        out_specs=[row] * 4,
        compiler_params=_cp(1, 48),
    )(*_in_hbm(landing, w, m, v))


def _adamw_small(items):
    n = len(items)

    def body(*refs):
        for k in range(n):
            w_ref, g_ref, m_ref, v_ref = refs[4 * k:4 * k + 4]
            outs = refs[4 * n + 3 * k:4 * n + 3 * k + 3]
            for o_ref, val in zip(outs, _adamw_math(w_ref[...], g_ref[...], m_ref[...], v_ref[...])):
                o_ref[...] = val

    vm = pl.BlockSpec(memory_space=pltpu.VMEM)
    flat = pl.pallas_call(
        body, name="adamw_small",
        out_shape=[_sds(it[0].shape, F32) for it in items for _ in range(3)],
        in_specs=[vm] * (4 * n), out_specs=[vm] * (3 * n),
    )(*[a for it in items for a in it])
    return [tuple(flat[3 * k:3 * k + 3]) for k in range(n)]


def _w_ada_update(c8, d_ada, w, m, v):
    tr = 256

    def body(c_ref, d_ref, w_ref, m_ref, v_ref, g_ref, dl_ref, m2_ref, v2_ref):
        cs = c_ref[...]
        cs = cs * jax.nn.sigmoid(cs)
        g = lax.dot_general(cs, d_ref[...], (((0,), (0,)), ((), ())), preferred_element_type=F32, precision=HIGH)
        delta, m2, v2 = _adamw_math(w_ref[...], g, m_ref[...], v_ref[...])
        g_ref[...] = g
        dl_ref[...] = delta
        m2_ref[...] = m2
        v2_ref[...] = v2

    row = pl.BlockSpec((tr, ADA_W), lambda i: (i, 0))
    return pl.pallas_call(
        body, name="w_ada_update", grid=(D // tr,),
        out_shape=[_sds((D, ADA_W), F32)] * 4,
        in_specs=[pl.BlockSpec((N_DEV, tr), lambda i: (0, i)), _const((N_DEV, ADA_W)), row, row, row],
        out_specs=[row] * 4,
        compiler_params=_cp(1, 40),
    )(c8, d_ada, *_in_hbm(w, m, v))


def _t5_bucket():
    qi = np.arange(BLK, dtype=np.int32)[:, None]
    kj = np.arange(2 * BLK, dtype=np.int32)[None, :]
    dist = np.maximum(qi + BLK - kj, 0)
    max_exact = N_BUCKETS // 2
    d_f = np.maximum(dist, max_exact).astype(np.float32)
    large = max_exact + (np.log(d_f / np.float32(max_exact)) / np.float32(math.log(MAX_DISTANCE / max_exact))
                         * np.float32(N_BUCKETS - max_exact)).astype(np.int32)
    large = np.minimum(large, N_BUCKETS - 1)
    return jnp.asarray(np.where(dist < max_exact, dist, large).astype(np.int32))


def _slabs_of_columns(w):
    r, c8 = w.shape
    return jnp.transpose(w.reshape(r, N_DEV, c8 // N_DEV), (1, 0, 2))


def _columns_of_slabs(w8):
    _, r, c = w8.shape
    return jnp.transpose(w8, (1, 0, 2)).reshape(r, N_DEV * c)


def kernel(x, c, rel_bias, w_ada, b_ada, pre_norm_g, post_norm_g, w_ffn1_in, w_ffn1_out, w_in, sinks, gmlp_ln_g, gmlp_ln_b, gmlp_w_s, gmlp_b_s, w_br_attn, w_br_gmlp, w_out, w_ffn2_in, w_ffn2_out, loss_target, m_rel_bias, m_w_ada, m_b_ada, m_pre_norm_g, m_post_norm_g, m_w_ffn1_in, m_w_ffn1_out, m_w_in, m_sinks, m_gmlp_ln_g, m_gmlp_ln_b, m_gmlp_w_s, m_gmlp_b_s, m_w_br_attn, m_w_br_gmlp, m_w_out, m_w_ffn2_in, m_w_ffn2_out, v_rel_bias, v_w_ada, v_b_ada, v_pre_norm_g, v_post_norm_g, v_w_ffn1_in, v_w_ffn1_out, v_w_in, v_sinks, v_gmlp_ln_g, v_gmlp_ln_b, v_gmlp_w_s, v_gmlp_b_s, v_w_br_attn, v_w_br_gmlp, v_w_out, v_w_ffn2_in, v_w_ffn2_out):
    me = 4 * lax.axis_index("x") + 2 * lax.axis_index("y") + lax.axis_index("c")
    x0 = x[0]
    target = loss_target[0]

    transposed = ("w_ffn1_in", "w_in", "w_ffn2_in")
    shards = [w_ffn1_in[0].T, w_ffn1_out[0], w_in[0].T, w_br_attn[0], w_br_gmlp[0], w_out[0],
              w_ffn2_in[0].T, w_ffn2_out[0]]
    shards_bf = [s.astype(BF) for s in shards]
    groups = [shards_bf[0:1], shards_bf[1:6], shards_bf[6:8]]

    def gather_start(i, after):
        return _slabs_start("gather", groups[i], after, "gather_start_%d" % i)

    def forward_start(st, i, after):
        lands = _slabs_wait("gather", len(groups[i]), st, after, "gather_wait_%d" % i)
        return _slabs_start("forward", lands, c, "forward_start_%d" % i)

    def gathered(st, i, after):
        return _slabs_wait("forward", len(groups[i]), st, after, "forward_wait_%d" % i)

    gs0 = gather_start(0, c)

    mine = jnp.concatenate([c[0], pre_norm_g[0].reshape(-1), post_norm_g[0].reshape(-1)])
    small8 = jnp.broadcast_to(mine[None, :], (8, mine.shape[0]))
    b_ada64 = jnp.repeat(b_ada.reshape(N_DEV, ADA_W), 8, axis=0)
    gath, ada64 = _ada_forward(small8, w_ada[0], b_ada64)
    gath8 = gath[::8]
    ada = ada64[::8].reshape(9, D)
    sh1, sc1, g1, sh2, sc2, g2, sh3, sc3, g3 = [ada[k:k + 1] for k in range(9)]
    gains = gath8[:, D:].reshape(N_DEV, 2, 3, 128)
    pre_g = jnp.transpose(gains[:, 0], (1, 0, 2)).reshape(3, D)
    post_g = jnp.transpose(gains[:, 1], (1, 0, 2)).reshape(3, D)
    pre = [pre_g[k:k + 1] for k in range(3)]
    post = [post_g[k:k + 1] for k in range(3)]

    bucket = _t5_bucket()
    bias = _bias_table(rel_bias, bucket).reshape(HEAD_ROWS, 2 * BLK)
    sinks8 = sinks[0]
    lg, lb = gmlp_ln_g, gmlp_ln_b
    ws = gmlp_w_s[0]
    bst = jnp.transpose(gmlp_b_s[0])

    fs0 = forward_start(gs0, 0, sh1)
    gs1 = gather_start(1, fs0[-1])
    wf1_in = gathered(fs0, 0, gs1[-1])[0].reshape(2 * D_FF, D)
    n1, fg1, fu1, fa1 = _ffn_in(x0, sh1, sc1, pre[0], wf1_in, "ffn1_in")
    fs1 = forward_start(gs1, 1, n1)
    gs2 = gather_start(2, fs1[-1])
    mix_w = gathered(fs1, 1, gs2[-1])
    wf1_out = mix_w[0].reshape(D_FF, D)
    w_in_full = mix_w[1].reshape(IN_W, D)
    w_q = _pair_heads(w_in_full[0:Q_W])
    w_bra = _pair_heads(_columns_of_slabs(mix_w[2]))
    w_brg = _columns_of_slabs(mix_w[3])
    w_out_full = mix_w[4].reshape(D, D)
    h1, y1 = _ffn_out(fa1, wf1_out, x0, g1, post[0], "ffn1_out")
    n2, qkv, zg, gates = _mix_in(h1, sh2, sc2, pre[1], w_in_full, w_q)
    att = _attn_fwd(qkv, bias, sinks8)
    gm = _gmlp_fwd(zg, lg, lb, ws, bst)
    fs2 = forward_start(gs2, 2, gm)
    ya, yg, ymix, y2, h2 = _mix_out(att, gm, gates, h1, w_bra, w_brg, w_out_full, g2, post[1], after=(fs2[-1],))
    wf2_in, wf2_out = gathered(fs2, 2, h2)
    wf2_in = wf2_in.reshape(2 * D_FF, D)
    wf2_out = wf2_out.reshape(D_FF, D)
    n3, fg3, fu3, fa3 = _ffn_in(h2, sh3, sc3, pre[2], wf2_in, "ffn2_in")
    dh3, y3, sq = _ffn_out(fa3, wf2_out, h2, g3, post[2], "ffn2_out", target=target)

    def exchange_start(i, arrays):
        return _slabs_start("exchange", arrays, sq, "exchange_start_%d" % i)

    dy3, dgu3, dh2, d_g3, d_post2, d_sh3, d_sc3, d_pre2 = _ffn_bwd(
        dh3, y3, fg3, fu3, wf2_out, wf2_in, h2, g3, post[2], sc3, pre[2], "ffn2_bwd")
    gw_f2_out = _tn_matmul(fa3, dy3, "ffn2_out_wgrad", tm=D_FF // 2).reshape(N_DEV, D_FF // N_DEV, D)
    gw_f2_in = _tn_matmul(dgu3, n3, "ffn2_in_wgrad", tm=D_FF // 2).reshape(N_DEV, FS, D)
    ex1 = exchange_start(1, [gw_f2_out, gw_f2_in])

    dzgate, d_att, d_gm, d_g2, d_post1, gw_out, gw_bra, gw_brg = _mix_out_bwd(
        dh2, y2, ya, yg, gates, att, gm, ymix, w_bra, w_brg, w_out_full, g2, post[1], after=(ex1[-1],))
    ex2 = exchange_start(2, [_slabs_of_columns(_unpair_heads(gw_bra)), _slabs_of_columns(gw_brg),
                             gw_out.reshape(N_DEV, D // N_DEV, D)])
    dq, dkv, dbias, dsink = _attn_bwd(qkv, bias, sinks8, d_att)
    dzg, d_ws, d_bs, d_lg, d_lb = _gmlp_bwd(zg, d_gm, lg, lb, ws, bst)
    d_rel = _rel_bias_grad(dbias.reshape(N_KV, GROUP * BLK, 2 * BLK), bucket)
    early = jnp.concatenate([
        jnp.concatenate([d_lg.reshape(4, 128), d_lb.reshape(4, 128)], axis=0),
        d_bs, d_rel, dsink, d_ws.reshape(N_HEADS * BLK, BLK)], axis=0)
    sm0 = _slabs_start("gather_all", [early], sq, "small_gather_start")
    dh1, d_sh2, d_sc2, d_pre1 = _mix_dn(dq, dkv, dzg, dzgate, w_in_full, w_q, h1, dh2, sc2, pre[1],
                                        after=(ex2[-1], sm0[-1]))
    gw_in = jnp.concatenate(
        [_unpair_heads(_tn_matmul(dq, n2, "w_in_q_wgrad")), _tn_matmul(dkv, n2, "w_in_kv_wgrad"),
         _tn_matmul(dzg, n2, "w_in_zg_wgrad"), _tn_matmul(dzgate, n2, "w_in_gate_wgrad")],
        axis=0).reshape(N_DEV, IN_W // N_DEV, D)
    ex3 = exchange_start(3, [gw_in])

    dy1, dgu1, d_g1, d_post0 = _ffn_out_bwd(dh1, y1, fg1, fu1, wf1_out, g1, post[0], "ffn1_out_bwd",
                                            after=(ex3[-1],))
    gw_f1_out = _tn_matmul(fa1, dy1, "ffn1_out_wgrad", tm=D_FF // 2).reshape(N_DEV, D_FF // N_DEV, D)
    ex4 = exchange_start(4, [gw_f1_out])
    gw_f1_in = _tn_matmul(dgu1, n1, "ffn1_in_wgrad", tm=D_FF // 2).reshape(N_DEV, FS, D)
    ex5 = exchange_start(5, [gw_f1_in])
    grad_x, d_sh1, d_sc1, d_pre0 = _ffn_dn(dgu1, wf1_in, x0, dh1, sc1, pre[0], "ffn1_dn", after=(ex4[-1], ex5[-1]))

    landed = {}
    for i, (ex, nms) in enumerate([(ex1, ["w_ffn2_out", "w_ffn2_in"]),
                                   (ex2, ["w_br_attn", "w_br_gmlp", "w_out"]), (ex3, ["w_in"]),
                                   (ex4, ["w_ffn1_out"]), (ex5, ["w_ffn1_in"])]):
        for nm, land in zip(nms, _slabs_wait("exchange", len(nms), ex, grad_x, "exchange_wait_%d" % i)):
            landed[nm] = land
    moments = [(m_w_ffn1_in, v_w_ffn1_in), (m_w_ffn1_out, v_w_ffn1_out), (m_w_in, v_w_in),
               (m_w_br_attn, v_w_br_attn), (m_w_br_gmlp, v_w_br_gmlp), (m_w_out, v_w_out),
               (m_w_ffn2_in, v_w_ffn2_in), (m_w_ffn2_out, v_w_ffn2_out)]
    names = ["w_ffn1_in", "w_ffn1_out", "w_in", "w_br_attn", "w_br_gmlp", "w_out", "w_ffn2_in", "w_ffn2_out"]
    big = {}
    for nm, w_, (m_, v_) in zip(names, shards, moments):
        if nm in transposed:
            res4 = _adamw_sharded(landed[nm], w_, m_[0].T, v_[0].T, "adamw_" + nm)
            big[nm] = [a.T[None] for a in res4]
        else:
            big[nm] = [a[None] for a in _adamw_sharded(landed[nm], w_, m_[0], v_[0], "adamw_" + nm)]

    my_loss = jnp.broadcast_to(sq * (0.5 / D), (1, D))
    my_loss, _ = lax.optimization_barrier((my_loss, landed["w_ffn1_in"]))
    tot, every = _small_allreduce([d_sh1, d_sc1, d_g1, d_sh2, d_sc2, d_g2, d_sh3, d_sc3, d_g3,
                                   d_pre0, d_pre1, d_pre2, d_post0, d_post1, d_post2, my_loss])
    (early_land,) = _slabs_wait("gather_all", 1, sm0, grad_x, "small_gather_wait")
    tot_early = _sum_slabs(early_land)

    loss = tot[15, 0]
    g_b_ada = tot[0:9].reshape(1, 9 * D)
    g_pre = lax.dynamic_slice_in_dim(tot[9:12], 128 * me, 128, axis=1)[None]
    g_post = lax.dynamic_slice_in_dim(tot[12:15], 128 * me, 128, axis=1)[None]
    g_lg = tot_early[0:4].reshape(1, G_W)
    g_lb = tot_early[4:8].reshape(1, G_W)
    g_bs = tot_early[8:16][None]
    g_rel = jnp.transpose(tot_early[16:24, 0:N_BUCKETS])
    g_sinks = tot_early[24:32, 0][None]
    g_ws = tot_early[32:1056].reshape(1, N_HEADS, BLK, BLK)

    d_ada_mine = lax.dynamic_slice_in_dim(every[:, 0:9].reshape(N_DEV, 9 * D), ADA_W * me, ADA_W, axis=1)
    ada_out = [a[None] for a in _w_ada_update(gath8[:, 0:D], d_ada_mine, w_ada[0], m_w_ada[0], v_w_ada[0])]

    small = [("rel_bias", rel_bias, g_rel, m_rel_bias, v_rel_bias), ("b_ada", b_ada, g_b_ada, m_b_ada, v_b_ada),
             ("pre_norm_g", pre_norm_g, g_pre, m_pre_norm_g, v_pre_norm_g),
             ("post_norm_g", post_norm_g, g_post, m_post_norm_g, v_post_norm_g),
             ("sinks", sinks, g_sinks, m_sinks, v_sinks), ("gmlp_ln_g", gmlp_ln_g, g_lg, m_gmlp_ln_g, v_gmlp_ln_g),
             ("gmlp_ln_b", gmlp_ln_b, g_lb, m_gmlp_ln_b, v_gmlp_ln_b),
             ("gmlp_w_s", gmlp_w_s, g_ws, m_gmlp_w_s, v_gmlp_w_s), ("gmlp_b_s", gmlp_b_s, g_bs, m_gmlp_b_s, v_gmlp_b_s)]
    two_d = lambda a: a.reshape(int(math.prod(a.shape[:-1])), a.shape[-1])
    stepped = _adamw_small([tuple(two_d(a) for a in item[1:]) for item in small])
    res = {"w_ada": ada_out}
    for (nm, w_, g_, _, _), new in zip(small, stepped):
        res[nm] = [g_] + [a.reshape(w_.shape) for a in new]
    res.update(big)
    order = ["rel_bias", "w_ada", "b_ada", "pre_norm_g", "post_norm_g", "w_ffn1_in", "w_ffn1_out", "w_in", "sinks",
             "gmlp_ln_g", "gmlp_ln_b", "gmlp_w_s", "gmlp_b_s", "w_br_attn", "w_br_gmlp", "w_out", "w_ffn2_in",
             "w_ffn2_out"]
    outs = [loss, grad_x[None]]
    for k in range(4):
        outs += [res[nm][k] for nm in order]
    return tuple(outs)
```

```python
import math

import jax
import jax.numpy as jnp
import numpy as np
from jax import lax
from jax.experimental import pallas as pl
from jax.experimental.pallas import tpu as pltpu

F32 = jnp.float32
BF = jnp.bfloat16

N_DEV = 8
D = 1024
D_FF = 2816
FS = D_FF // 4
N_HEADS = 8
N_KV = 2
GROUP = 4
HD = 64
BLK = 128
Q_W = 512
KV_W = 128
G_W = 512
QKV_W = Q_W + 2 * KV_W
ZG_OFF = QKV_W
GATE_OFF = ZG_OFF + 2 * G_W
IN_W = GATE_OFF + 2 * D
N_BUCKETS = 32
MAX_DISTANCE = 128
EPS = 1e-6
NEG = -1e30
SCALE = HD ** -0.5
ADA_W = 9 * D // N_DEV

ADAM_LR = 0.001
ADAM_B1 = 0.9
ADAM_B2 = 0.999
ADAM_EPS = 1e-08
ADAM_WD = 0.01
ADAM_STEP = 10

CHUNK = 256
MIB = 1024 * 1024
MESH = pl.DeviceIdType.MESH
HIGH = lax.Precision.HIGHEST


def _cp(n_grid, vmem_mib):
    return pltpu.CompilerParams(dimension_semantics=("arbitrary",) * n_grid,
                                vmem_limit_bytes=vmem_mib * MIB)


def _const(shape):
    return pl.BlockSpec(shape, lambda *_: (0,) * len(shape))


def _resident(shape):
    return pl.BlockSpec(shape, lambda *_: (0,) * len(shape), pipeline_mode=pl.Buffered(1))


def _behind(body, n_in, after):
    k = len(after)
    return (lambda *refs: body(*refs[:n_in], *refs[n_in + k:])), [pl.BlockSpec(memory_space=pl.ANY)] * k


def _in_hbm(*arrays):
    return [pltpu.with_memory_space_constraint(a, pltpu.HBM) for a in arrays]


def _sds(shape, dtype):
    return jax.ShapeDtypeStruct(shape, dtype)


def _dot(a, b):
    return jnp.dot(a, b, preferred_element_type=F32)


def _dot_nt(a, b):
    return lax.dot_general(a, b, (((1,), (1,)), ((), ())), preferred_element_type=F32)


def _dot_tn(a, b):
    return lax.dot_general(a, b, (((0,), (0,)), ((), ())), preferred_element_type=F32)


def _rms_r(x):
    return lax.rsqrt(jnp.mean(x * x, axis=-1, keepdims=True) + EPS)


def _colsum(x):
    return jnp.sum(x, axis=0, keepdims=True)


def _prenorm(x, gp, sc, sh):
    return (x * _rms_r(x) * gp) * (1.0 + sc) + sh


def _prenorm_bwd(dn, x, gp, sc):
    r = _rms_r(x)
    xh = x * r
    t = dn * (1.0 + sc) * gp
    dx = r * (t - xh * jnp.mean(t * xh, axis=-1, keepdims=True))
    return dx, _colsum(dn), _colsum(dn * xh * gp), _colsum(dn * (1.0 + sc) * xh)


def _postnorm_bwd(dh, y, gate, gp, res):
    y = y.astype(F32)
    r = _rms_r(y)
    yh = y * r
    dyn = (res * gate) * dh
    t = dyn * gp
    dy = r * (t - yh * jnp.mean(t * yh, axis=-1, keepdims=True))
    return dy, _colsum(res * dh * yh * gp), _colsum(dyn * yh)


def _gelu(x):
    k = math.sqrt(2.0 / math.pi)
    return 0.5 * x * (1.0 + jnp.tanh(k * (x + 0.044715 * x * x * x)))


def _gelu_grad(x):
    k = math.sqrt(2.0 / math.pi)
    t = jnp.tanh(k * (x + 0.044715 * x * x * x))
    return 0.5 * (1.0 + t) + 0.5 * x * (1.0 - t * t) * (k * (1.0 + 3.0 * 0.044715 * x * x))


def _my_place():
    x, y, c = lax.axis_index("x"), lax.axis_index("y"), lax.axis_index("c")
    return x, y, c, 4 * x + 2 * y + c


def _peer(x, y, c, k):
    px = 1 - x if k & 4 else x
    py = 1 - y if k & 2 else y
    pc = 1 - c if k & 1 else c
    return (px, py, pc), 4 * px + 2 * py + pc


HBM_SPEC = pl.BlockSpec(memory_space=pltpu.HBM)
SEM_SPEC = pl.BlockSpec(memory_space=pltpu.SEMAPHORE)
EFFECT = pltpu.SideEffectType.DATAFLOW_SIDE_EFFECTING


RELATIONS = {"exchange": (1, 2, 3, 4, 5, 6, 7), "gather": (1, 2, 4, 6), "forward": (2, 4, 6),
             "gather_all": (1, 2, 3, 4, 5, 6, 7)}


def _slab_copies(mode, srcs, lands, send, recv, loc):
    x, y, c, me = _my_place()
    rel = RELATIONS[mode]
    remote, local = [], []
    for t in range(len(lands)):
        for i, k in enumerate(rel):
            peer, peer_lin = _peer(x, y, c, k)
            if mode == "exchange":
                src, dst, to = srcs[t].at[peer_lin], lands[t].at[me], peer
            elif mode in ("gather", "gather_all"):
                src, dst, to = srcs[t], lands[t].at[me], peer
            else:
                src, dst, to = lands[t].at[peer_lin], lands[t].at[peer_lin], _peer(x, y, c, 1)[0]
            remote.append(pltpu.make_async_remote_copy(
                src_ref=src, dst_ref=dst, send_sem=send.at[t * len(rel) + i], recv_sem=recv.at[t * len(rel) + i],
                device_id=to, device_id_type=MESH))
        if mode == "exchange":
            local.append(pltpu.make_async_copy(srcs[t].at[me], lands[t].at[me], loc.at[t]))
        elif mode in ("gather", "gather_all"):
            local.append(pltpu.make_async_copy(srcs[t], lands[t].at[me], loc.at[t]))
    return remote, local


def _slabs_start(mode, arrays, after, name):
    n = len(arrays)
    if mode == "forward":
        thru = list(arrays)
    else:
        shapes = [a.shape if mode == "exchange" else (N_DEV,) + a.shape for a in arrays]
        thru = list(arrays) + [lax.empty(s, a.dtype) for s, a in zip(shapes, arrays)]
    m = len(thru)
    n_sem = n * len(RELATIONS[mode])

    def body(*refs):
        srcs, lands = refs[:n], refs[m - n:m]
        send, recv, loc = refs[m + 1:m + 4]
        remote, local = _slab_copies(mode, srcs, lands, send, recv, loc)
        for cp in remote + local:
            cp.start()
        refs[-1][...] = jnp.zeros_like(refs[-1])

    return pl.pallas_call(
        body, name=name,
        out_shape=(pltpu.SemaphoreType.DMA((n_sem,)), pltpu.SemaphoreType.DMA((n_sem,)),
                   pltpu.SemaphoreType.DMA((n,)),
                   *[pltpu.HBM(a.shape, a.dtype) for a in thru],
                   _sds((1, D), F32)),
        in_specs=[HBM_SPEC] * m + [pl.BlockSpec(memory_space=pl.ANY)],
        out_specs=(SEM_SPEC, SEM_SPEC, SEM_SPEC, *[HBM_SPEC] * m, pl.BlockSpec(memory_space=pltpu.VMEM)),
        input_output_aliases={t: 3 + t for t in range(m)},
        compiler_params=pltpu.CompilerParams(has_side_effects=EFFECT),
    )(*[pltpu.with_memory_space_constraint(a, pltpu.HBM) for a in thru], after)


def _slabs_wait(mode, n, started, after, name):
    sems = started[0:3]
    thru = started[3:-1]
    m = len(thru)

    def body(*refs):
        srcs, lands = refs[:n], refs[m - n:m]
        remote, local = _slab_copies(mode, srcs, lands, *refs[m:m + 3])
        for cp in remote:
            cp.wait_send()
            cp.wait_recv()
        for cp in local:
            cp.wait()

    res = pl.pallas_call(
        body, name=name,
        out_shape=tuple(pltpu.HBM(a.shape, a.dtype) for a in thru),
        in_specs=[HBM_SPEC] * m + [SEM_SPEC] * 3 + [pl.BlockSpec(memory_space=pl.ANY)],
        out_specs=tuple([HBM_SPEC] * m),
        input_output_aliases={t: t for t in range(m)},
        compiler_params=pltpu.CompilerParams(has_side_effects=EFFECT),
    )(*thru, *sems, after)
    return list(res[m - n:m])


def _ada_forward(small8, w_ada, b_ada64):
    sw = small8.shape[1]

    def body(sm_ref, w_ref, b_ref, gath_ref, ada_ref, part_ref, send1, recv1, send2, recv2):
        x, y, c, me = _my_place()
        row_me = pl.multiple_of(me * 8, 8)
        gath_ref[pl.ds(row_me, 8), :] = sm_ref[...]
        first = []
        for k in range(1, N_DEV):
            peer, _ = _peer(x, y, c, k)
            cp = pltpu.make_async_remote_copy(
                src_ref=sm_ref, dst_ref=gath_ref.at[pl.ds(row_me, 8), :], send_sem=send1.at[k - 1],
                recv_sem=recv1.at[k - 1], device_id=peer, device_id_type=MESH)
            cp.start()
            first.append(cp)
        for cp in first:
            cp.wait()
        cs = gath_ref[:, 0:D]
        cs = cs * jax.nn.sigmoid(cs)
        part_ref[...] = jnp.dot(cs, w_ref[...], preferred_element_type=F32, precision=HIGH)
        ada_ref[pl.ds(row_me, 8), :] = part_ref[pl.ds(row_me, 8), :]
        second = []
        for k in range(1, N_DEV):
            peer, peer_lin = _peer(x, y, c, k)
            cp = pltpu.make_async_remote_copy(
                src_ref=part_ref.at[pl.ds(pl.multiple_of(peer_lin * 8, 8), 8), :],
                dst_ref=ada_ref.at[pl.ds(row_me, 8), :], send_sem=send2.at[k - 1],
                recv_sem=recv2.at[k - 1], device_id=peer, device_id_type=MESH)
            cp.start()
            second.append(cp)
        for cp in second:
            cp.wait()
        ada_ref[...] = ada_ref[...] + b_ref[...]

    vm = pl.BlockSpec(memory_space=pltpu.VMEM)
    return pl.pallas_call(
        body, name="ada_forward",
        out_shape=[_sds((8 * N_DEV, sw), F32), _sds((8 * N_DEV, ADA_W), F32)],
        in_specs=[vm, vm, vm], out_specs=[vm, vm],
        scratch_shapes=[pltpu.VMEM((8 * N_DEV, ADA_W), F32)] + [pltpu.SemaphoreType.DMA((7,))] * 4,
        compiler_params=pltpu.CompilerParams(vmem_limit_bytes=32 * MIB),
    )(small8, w_ada, b_ada64)


def _sum_slabs(land):
    def body(l_ref, o_ref):
        acc = l_ref[0]
        for j in range(1, N_DEV):
            acc = acc + l_ref[j]
        o_ref[...] = acc

    vm = pl.BlockSpec(memory_space=pltpu.VMEM)
    return pl.pallas_call(body, name="sum_slabs", out_shape=_sds(land.shape[1:], F32), in_specs=[vm], out_specs=vm,
                          compiler_params=pltpu.CompilerParams(vmem_limit_bytes=32 * MIB))(land)


def _small_allreduce(vectors):
    n = len(vectors)

    def body(*refs):
        v_refs, (sum_ref, gath_ref, pack, send, recv) = refs[:n], refs[n:]
        x, y, c, me = _my_place()
        for k in range(n):
            pack[k:k + 1, :] = v_refs[k][...]
        gath_ref[me] = pack[...]
        cps = []
        for k in range(1, N_DEV):
            peer, _ = _peer(x, y, c, k)
            cp = pltpu.make_async_remote_copy(
                src_ref=pack, dst_ref=gath_ref.at[me], send_sem=send.at[k - 1],
                recv_sem=recv.at[k - 1], device_id=peer, device_id_type=MESH)
            cp.start()
            cps.append(cp)
        for cp in cps:
            cp.wait()
        acc = gath_ref[0]
        for j in range(1, N_DEV):
            acc = acc + gath_ref[j]
        sum_ref[...] = acc

    vm = pl.BlockSpec(memory_space=pltpu.VMEM)
    return pl.pallas_call(
        body, name="small_allreduce",
        out_shape=[_sds((n, D), F32), _sds((N_DEV, n, D), F32)],
        in_specs=[vm] * n, out_specs=[vm, vm],
        scratch_shapes=[pltpu.VMEM((n, D), F32), pltpu.SemaphoreType.DMA((7,)), pltpu.SemaphoreType.DMA((7,))],
    )(*vectors)


F_TILES = tuple((f0, min(512, D_FF - f0)) for f0 in range(0, D_FF, 512))
F_TILES_NARROW = tuple((f0, 256) for f0 in range(0, D_FF, 256))


def _swiglu_tile(n, wt_ref, f0, tf):
    g = _dot_nt(n, wt_ref[f0:f0 + tf, :])
    u = _dot_nt(n, wt_ref[D_FF + f0:D_FF + f0 + tf, :])
    sg = jax.nn.sigmoid(g)
    silu = g * sg
    return (u * (sg * (1.0 + g * (1.0 - sg)))).astype(BF), silu.astype(BF), (silu * u).astype(BF)


def _ffn_in(h, sh, sc, gp, wt, name):
    S = h.shape[0]
    R = min(512, S)

    def body(h_ref, sh_ref, sc_ref, gp_ref, w_ref, n_ref, dg_ref, sl_ref, a_ref):
        for r0 in range(0, R, CHUNK):
            rows = slice(r0, r0 + CHUNK)
            n = _prenorm(h_ref[rows, :], gp_ref[...], sc_ref[...], sh_ref[...]).astype(BF)
            n_ref[rows, :] = n
            for f0, tf in F_TILES_NARROW:
                dg_ref[rows, f0:f0 + tf], sl_ref[rows, f0:f0 + tf], a_ref[rows, f0:f0 + tf] = _swiglu_tile(
                    n, w_ref, f0, tf)

    vec = _const((1, D))
    rows_ = lambda w_: pl.BlockSpec((R, w_), lambda i: (i, 0))
    return pl.pallas_call(
        body, name=name, grid=(S // R,),
        out_shape=[_sds((S, D), BF)] + [_sds((S, D_FF), BF)] * 3,
        in_specs=[rows_(D), vec, vec, vec, _resident((2 * D_FF, D))],
        out_specs=[rows_(D), rows_(D_FF), rows_(D_FF), rows_(D_FF)],
        compiler_params=_cp(1, 56),
    )(*_in_hbm(h), sh, sc, gp, *_in_hbm(wt))


def _ffn_out(a, w, h, gate, gp, name, target=None):
    S = h.shape[0]
    R = min(512, S)
    with_loss = target is not None

    def body(a_ref, w_ref, h_ref, gate_ref, gp_ref, *rest):
        if with_loss:
            t_ref, out_ref, y_ref, tot_ref = rest

            @pl.when(pl.program_id(0) == 0)
            def _():
                tot_ref[...] = jnp.zeros_like(tot_ref)
        else:
            out_ref, y_ref = rest
        for r0 in range(0, R, CHUNK):
            rows = slice(r0, r0 + CHUNK)
            y = _dot(a_ref[rows, :], w_ref[...])
            y_ref[rows, :] = y.astype(BF)
            hn = h_ref[rows, :] + (0.5 * gate_ref[...]) * (y * _rms_r(y) * gp_ref[...])
            if with_loss:
                e = hn - t_ref[rows, :]
                out_ref[rows, :] = e * (1.0 / D)
                tot_ref[...] += jnp.sum(jnp.sum(e * e, axis=1, keepdims=True), axis=0, keepdims=True)
            else:
                out_ref[rows, :] = hn

    vec = _const((1, D))
    rows_ = lambda w_: pl.BlockSpec((R, w_), lambda i: (i, 0))
    return pl.pallas_call(
        body, name=name, grid=(S // R,),
        out_shape=[_sds((S, D), F32), _sds((S, D), BF)] + ([_sds((1, 1), F32)] if with_loss else []),
        in_specs=[rows_(D_FF), _resident((D_FF, D)), rows_(D), vec, vec] + ([rows_(D)] if with_loss else []),
        out_specs=[rows_(D), rows_(D)] + ([_const((1, 1))] if with_loss else []),
        compiler_params=_cp(1, 48),
    )(*_in_hbm(a, w, h), gate, gp, *(_in_hbm(target) if with_loss else ()))


def _ffn_out_bwd(dh, y, dsilu_u, silu, w, gate, gp, name, after=()):
    S = dh.shape[0]
    R = min(512, S)

    def body(dh_ref, y_ref, g_ref, u_ref, w_ref, gate_ref, gp_ref, dy_ref, dgu_ref, dgate_ref, dgp_ref):
        @pl.when(pl.program_id(0) == 0)
        def _():
            dgate_ref[...] = jnp.zeros_like(dgate_ref)
            dgp_ref[...] = jnp.zeros_like(dgp_ref)
        for r0 in range(0, R, CHUNK):
            rows = slice(r0, r0 + CHUNK)
            dy, dgate, dgp = _postnorm_bwd(dh_ref[rows, :], y_ref[rows, :], gate_ref[...], gp_ref[...], 0.5)
            dgate_ref[...] += dgate
            dgp_ref[...] += dgp
            dyb = dy.astype(BF)
            dy_ref[rows, :] = dyb
            for f0, tf in F_TILES:
                da = _dot_nt(dyb, w_ref[f0:f0 + tf, :])
                dgu_ref[rows, f0:f0 + tf] = (da * g_ref[rows, f0:f0 + tf].astype(F32)).astype(BF)
                dgu_ref[rows, D_FF + f0:D_FF + f0 + tf] = (da * u_ref[rows, f0:f0 + tf].astype(F32)).astype(BF)

    vec = _const((1, D))
    rows_ = lambda w_: pl.BlockSpec((R, w_), lambda i: (i, 0))
    body, after_specs = _behind(body, 7, after)
    return pl.pallas_call(
        body, name=name, grid=(S // R,),
        out_shape=[_sds((S, D), BF), _sds((S, 2 * D_FF), BF), _sds((1, D), F32), _sds((1, D), F32)],
        in_specs=[rows_(D), rows_(D), rows_(D_FF), rows_(D_FF), _resident((D_FF, D)), vec, vec] + after_specs,
        out_specs=[rows_(D), rows_(2 * D_FF), vec, vec],
        compiler_params=_cp(1, 56),
    )(*_in_hbm(dh, y, dsilu_u, silu, w), gate, gp, *after)


def _ffn_dn(dgu, wt, h, dh, sc, gp, name, after=()):
    S = h.shape[0]
    R = min(512, S)

    def body(dgu_ref, w_ref, h_ref, dh_ref, sc_ref, gp_ref, out_ref, dsh_ref, dsc_ref, dgp_ref):
        @pl.when(pl.program_id(0) == 0)
        def _():
            dsh_ref[...] = jnp.zeros_like(dsh_ref)
            dsc_ref[...] = jnp.zeros_like(dsc_ref)
            dgp_ref[...] = jnp.zeros_like(dgp_ref)

        for r0 in range(0, R, CHUNK):
            rows = slice(r0, r0 + CHUNK)
            dn = _dot(dgu_ref[rows, :], w_ref[...])
            dx, dsh, dsc, dgp = _prenorm_bwd(dn, h_ref[rows, :], gp_ref[...], sc_ref[...])
            out_ref[rows, :] = dh_ref[rows, :] + dx
            dsh_ref[...] += dsh
            dsc_ref[...] += dsc
            dgp_ref[...] += dgp

    vec = _const((1, D))
    rows_ = lambda w_: pl.BlockSpec((R, w_), lambda i: (i, 0))
    body, after_specs = _behind(body, 6, after)
    return pl.pallas_call(
        body, name=name, grid=(S // R,),
        out_shape=[_sds((S, D), F32)] + [_sds((1, D), F32)] * 3,
        in_specs=[rows_(2 * D_FF), _resident((2 * D_FF, D)), rows_(D), rows_(D), vec, vec] + after_specs,
        out_specs=[rows_(D), vec, vec, vec],
        compiler_params=_cp(1, 56),
    )(*_in_hbm(dgu, wt, h, dh), sc, gp, *after)


def _ffn_bwd(dh, y, dsilu_u, silu, w, wt, h, gate, gpost, sc, gpre, name):
    S = dh.shape[0]
    R = min(256, S)

    def body(dh_ref, y_ref, g_ref, u_ref, w_ref, wt_ref, h_ref, gate_ref, gpost_ref, sc_ref, gpre_ref,
             dy_ref, dgu_ref, out_ref, dgate_ref, dgpost_ref, dsh_ref, dsc_ref, dgpre_ref):
        @pl.when(pl.program_id(0) == 0)
        def _():
            for r in (dgate_ref, dgpost_ref, dsh_ref, dsc_ref, dgpre_ref):
                r[...] = jnp.zeros_like(r)
        dhh = dh_ref[...]
        dy, dgate, dgpost = _postnorm_bwd(dhh, y_ref[...], gate_ref[...], gpost_ref[...], 0.5)
        dgate_ref[...] += dgate
        dgpost_ref[...] += dgpost
        dyb = dy.astype(BF)
        dy_ref[...] = dyb
        for f0, tf in F_TILES:
            da = _dot_nt(dyb, w_ref[f0:f0 + tf, :])
            dgu_ref[:, f0:f0 + tf] = (da * g_ref[:, f0:f0 + tf].astype(F32)).astype(BF)
            dgu_ref[:, D_FF + f0:D_FF + f0 + tf] = (da * u_ref[:, f0:f0 + tf].astype(F32)).astype(BF)
        dn = _dot(dgu_ref[...], wt_ref[...])
        dx, dsh, dsc, dgpre = _prenorm_bwd(dn, h_ref[...], gpre_ref[...], sc_ref[...])
        out_ref[...] = dhh + dx
        dsh_ref[...] += dsh
        dsc_ref[...] += dsc
        dgpre_ref[...] += dgpre

    vec = _const((1, D))
    rows_ = lambda w_: pl.BlockSpec((R, w_), lambda i: (i, 0))
    return pl.pallas_call(
        body, name=name, grid=(S // R,),
        out_shape=[_sds((S, D), BF), _sds((S, 2 * D_FF), BF), _sds((S, D), F32)] + [_sds((1, D), F32)] * 5,
        in_specs=[rows_(D), rows_(D), rows_(D_FF), rows_(D_FF), _resident((D_FF, D)), _resident((2 * D_FF, D)),
                  rows_(D), vec, vec, vec, vec],
        out_specs=[rows_(D), rows_(2 * D_FF), rows_(D)] + [vec] * 5,
        compiler_params=_cp(1, 56),
    )(*_in_hbm(dh, y, dsilu_u, silu, w, wt, h), gate, gpost, sc, gpre)


def _tn_matmul(a, b, name, tm=None):
    S, M_all = a.shape
    N = b.shape[1]
    M = M_all if tm is None else tm
    GA = M_all // M
    ts = min(2048 if M * N <= 2 * D * D else 1024, S // 2)
    nk = S // ts
    assert nk >= 2 and nk * ts == S
    chunks = [(m0, min(CHUNK, M - m0)) for m0 in range(0, M, CHUNK)]

    def body(a_ref, b_ref, o_ref, acc):
        k = pl.program_id(1)

        @pl.when(k == 0)
        def _():
            for m0, mc in chunks:
                acc[m0:m0 + mc, :] = _dot_tn(a_ref[:, m0:m0 + mc], b_ref[...])

        @pl.when(jnp.logical_and(k > 0, k < nk - 1))
        def _():
            for m0, mc in chunks:
                acc[m0:m0 + mc, :] += _dot_tn(a_ref[:, m0:m0 + mc], b_ref[...])

        @pl.when(k == nk - 1)
        def _():
            for m0, mc in chunks:
                o_ref[m0:m0 + mc, :] = (acc[m0:m0 + mc, :] + _dot_tn(a_ref[:, m0:m0 + mc], b_ref[...])).astype(BF)

    return pl.pallas_call(
        body, name=name, grid=(GA, nk),
        out_shape=_sds((M_all, N), BF),
        in_specs=[pl.BlockSpec((ts, M), lambda ga, k: (k, ga)), pl.BlockSpec((ts, N), lambda ga, k: (k, 0))],
        out_specs=pl.BlockSpec((M, N), lambda ga, k: (ga, 0)),
        scratch_shapes=[pltpu.VMEM((M, N), F32)],
        compiler_params=_cp(2, 56),
    )(*_in_hbm(a, b))


def _mix_in(h, sh, sc, gp, w, wq):
    S = h.shape[0]
    R = min(512, S)

    def body(h_ref, sh_ref, sc_ref, gp_ref, w_ref, wq_ref, n_ref, qkv_ref, zg_ref, gates_ref):
        for r0 in range(0, R, CHUNK):
            rows = slice(r0, r0 + CHUNK)
            nb = _prenorm(h_ref[rows, :], gp_ref[...], sc_ref[...], sh_ref[...]).astype(BF)
            n_ref[rows, :] = nb
            qkv_ref[rows, 0:Q_W] = _dot_nt(nb, wq_ref[...]).astype(BF)
            qkv_ref[rows, Q_W:QKV_W] = _dot_nt(nb, w_ref[Q_W:QKV_W, :]).astype(BF)
            zg_ref[rows, :] = _dot_nt(nb, w_ref[ZG_OFF:GATE_OFF, :]).astype(BF)
            gates_ref[rows, :] = jax.nn.sigmoid(_dot_nt(nb, w_ref[GATE_OFF:IN_W, :])).astype(BF)

    vec = _const((1, D))
    rows = lambda w_: pl.BlockSpec((R, w_), lambda i: (i, 0))
    return pl.pallas_call(
        body, name="mix_in", grid=(S // R,),
        out_shape=[_sds((S, D), BF), _sds((S, QKV_W), BF), _sds((S, 2 * G_W), BF), _sds((S, 2 * D), BF)],
        in_specs=[rows(D), vec, vec, vec, _resident((IN_W, D)), _resident((Q_W, D))],
        out_specs=[rows(D), rows(QKV_W), rows(2 * G_W), rows(2 * D)],
        compiler_params=_cp(1, 48),
    )(*_in_hbm(h), sh, sc, gp, *_in_hbm(w, wq))


def _bias_table(rel_bias, bucket):
    def body(rel_ref, bk_ref, out_ref):
        bk = bk_ref[...]
        qi = lax.broadcasted_iota(jnp.int32, (BLK, 2 * BLK), 0)
        kj = lax.broadcasted_iota(jnp.int32, (BLK, 2 * BLK), 1)
        dist = qi + BLK - kj
        window = (dist >= 0) & (dist < BLK)
        for h in range(N_HEADS):
            acc = jnp.zeros((BLK, 2 * BLK), F32)
            for b in range(N_BUCKETS):
                acc = jnp.where(bk == b, rel_ref[b, h], acc)
            out_ref[h // GROUP, pl.ds((h % GROUP) * BLK, BLK), :] = jnp.where(window, acc, NEG)

    return pl.pallas_call(
        body, name="bias_table",
        out_shape=_sds((N_KV, GROUP * BLK, 2 * BLK), F32),
        in_specs=[pl.BlockSpec(memory_space=pltpu.SMEM), pl.BlockSpec(memory_space=pltpu.VMEM)],
        out_specs=pl.BlockSpec(memory_space=pltpu.VMEM),
    )(rel_bias, bucket)


ATT_TB = 8


HEAD_ROWS = N_HEADS * BLK


def _pair_heads(w):
    return jnp.transpose(w.reshape(N_KV, GROUP, HD, w.shape[1]), (1, 0, 2, 3)).reshape(w.shape)


def _unpair_heads(w):
    return jnp.transpose(w.reshape(GROUP, N_KV, HD, w.shape[1]), (1, 0, 2, 3)).reshape(w.shape)


def _halves(x, scale=1.0):
    low = lax.broadcasted_iota(jnp.int32, x.shape, 1) < HD
    xf = x.astype(F32) * scale
    return jnp.where(low, xf, 0.0).astype(BF), jnp.where(low, 0.0, xf).astype(BF)


def _stack_heads(x, scale=1.0):
    halves = [_halves(x[:, g * 128:(g + 1) * 128], scale) for g in range(GROUP)]
    return jnp.concatenate([lo for lo, _ in halves] + [hi for _, hi in halves], axis=0)


def _attn_probs(q, kvc, kvp, bias_ref, sink_ref, has_prev):
    kv2 = jnp.concatenate([kvp, kvc], axis=0)
    kboth, vboth = kv2[:, 0:KV_W], kv2[:, KV_W:2 * KV_W]
    qpad = _stack_heads(q, SCALE)
    s = _dot_nt(qpad, kboth) + bias_ref[...]
    if has_prev is not None:
        col = lax.broadcasted_iota(jnp.int32, (HEAD_ROWS, 2 * BLK), 1)
        s = jnp.where((col >= BLK) | has_prev, s, NEG)
    row_head = lax.broadcasted_iota(jnp.int32, (HEAD_ROWS, 1), 0) // BLK
    sink = jnp.zeros((HEAD_ROWS, 1), F32)
    for h in range(N_HEADS):
        sink = jnp.where(row_head == h, sink_ref[h], sink)
    m = jnp.maximum(jnp.max(s, axis=1, keepdims=True), sink)
    p = jnp.exp(s - m)
    e_sink = jnp.exp(sink - m)
    inv = 1.0 / (jnp.sum(p, axis=1, keepdims=True) + e_sink)
    return qpad, kboth, vboth, p * inv, e_sink * inv


def _attn_fwd(qkv, bias, sinks):
    S = qkv.shape[0]
    tb = min(ATT_TB, S // BLK)
    T = tb * BLK

    def body(sink_ref, q_ref, kv_ref, kvp_ref, bias_ref, o_ref):
        step = pl.program_id(0)
        for j in range(tb):
            rows = slice(j * BLK, (j + 1) * BLK)
            kvp = kvp_ref[...] if j == 0 else kv_ref[(j - 1) * BLK:j * BLK, :]
            has_prev = (step > 0) if j == 0 else None
            _, _, vboth, prob, _ = _attn_probs(q_ref[rows, :], kv_ref[rows, :], kvp, bias_ref, sink_ref, has_prev)
            pb = prob.astype(BF)
            v_low, v_high = _halves(vboth)
            half = HEAD_ROWS // 2
            o = _dot(pb[0:half], v_low) + _dot(pb[half:HEAD_ROWS], v_high)
            for g in range(GROUP):
                o_ref[rows, g * 128:(g + 1) * 128] = o[g * BLK:(g + 1) * BLK].astype(BF)

    return pl.pallas_call(
        body, name="attn_fwd", grid=(S // T,),
        out_shape=_sds((S, Q_W), BF),
        in_specs=[pl.BlockSpec(memory_space=pltpu.SMEM),
                  pl.BlockSpec((T, Q_W), lambda i: (i, 0)),
                  pl.BlockSpec((T, 2 * KV_W), lambda i: (i, 2)),
                  pl.BlockSpec((BLK, 2 * KV_W), lambda i: (jnp.maximum(i * tb - 1, 0), 2)),
                  _const((HEAD_ROWS, 2 * BLK))],
        out_specs=pl.BlockSpec((T, Q_W), lambda i: (i, 0)),
        compiler_params=_cp(1, 32),
    )(sinks, *_in_hbm(qkv, qkv, qkv, bias))


def _attn_bwd(qkv, bias, sinks, do):
    S = qkv.shape[0]
    tb = min(ATT_TB, S // BLK)
    T = tb * BLK
    nt = S // T
    half = HEAD_ROWS // 2

    def body(sink_ref, q_ref, kv_ref, kvp_ref, bias_ref, do_ref, dq_ref, dkv_ref, dbias_ref, dsink_ref, carry):
        i = pl.program_id(0)

        @pl.when(i == 0)
        def _():
            carry[...] = jnp.zeros_like(carry)
            dbias_ref[...] = jnp.zeros_like(dbias_ref)
            dsink_ref[...] = jnp.zeros_like(dsink_ref)

        from_next = carry[...]
        head_row = lax.broadcasted_iota(jnp.int32, (N_HEADS, 128), 0)
        low = lax.broadcasted_iota(jnp.int32, (BLK, 128), 1) < HD
        for j in reversed(range(tb)):
            rows = slice(j * BLK, (j + 1) * BLK)
            kvp = kvp_ref[...] if j == 0 else kv_ref[(j - 1) * BLK:j * BLK, :]
            has_prev = (i < nt - 1) if j == 0 else None
            qpad, kboth, vboth, prob, p_sink = _attn_probs(q_ref[rows, :], kv_ref[rows, :], kvp, bias_ref, sink_ref,
                                                           has_prev)
            pb = prob.astype(BF)
            dopad = _stack_heads(do_ref[rows, :])
            dp = _dot_nt(dopad, vboth)
            delta = jnp.sum(prob * dp, axis=1, keepdims=True)
            ds = prob * (dp - delta)
            dbias_ref[...] += ds
            sink_term = p_sink * delta
            dsink_rows = jnp.zeros((N_HEADS, 128), F32)
            for h in range(N_HEADS):
                val = -jnp.sum(sink_term[h * BLK:(h + 1) * BLK], axis=0, keepdims=True)
                dsink_rows = jnp.where(head_row == h, val, dsink_rows)
            dsink_ref[...] += dsink_rows
            dsb = ds.astype(BF)
            dqpad = _dot(dsb, kboth) * SCALE
            for g in range(GROUP):
                dq_ref[rows, g * 128:(g + 1) * 128] = jnp.where(
                    low, dqpad[g * BLK:(g + 1) * BLK], dqpad[half + g * BLK:half + (g + 1) * BLK]).astype(BF)
            dkv2 = jnp.concatenate([jnp.transpose(_dot_tn(qpad, dsb)),
                                    jnp.transpose(_dot_tn(dopad, pb))], axis=1)
            dkv_ref[rows, :] = (dkv2[BLK:2 * BLK] + from_next).astype(BF)
            from_next = dkv2[0:BLK]
        carry[...] = from_next

    return pl.pallas_call(
        body, name="attn_bwd", grid=(nt,),
        out_shape=[_sds((S, Q_W), BF), _sds((S, 2 * KV_W), BF),
                   _sds((HEAD_ROWS, 2 * BLK), F32), _sds((N_HEADS, 128), F32)],
        in_specs=[pl.BlockSpec(memory_space=pltpu.SMEM),
                  pl.BlockSpec((T, Q_W), lambda i: (nt - 1 - i, 0)),
                  pl.BlockSpec((T, 2 * KV_W), lambda i: (nt - 1 - i, 2)),
                  pl.BlockSpec((BLK, 2 * KV_W), lambda i: (jnp.maximum((nt - 1 - i) * tb - 1, 0), 2)),
                  _const((HEAD_ROWS, 2 * BLK)),
                  pl.BlockSpec((T, Q_W), lambda i: (nt - 1 - i, 0))],
        out_specs=[pl.BlockSpec((T, Q_W), lambda i: (nt - 1 - i, 0)),
                   pl.BlockSpec((T, 2 * KV_W), lambda i: (nt - 1 - i, 0)),
                   _const((HEAD_ROWS, 2 * BLK)), _const((N_HEADS, 128))],
        scratch_shapes=[pltpu.VMEM((BLK, 2 * KV_W), F32)],
        compiler_params=_cp(1, 32),
    )(sinks, *_in_hbm(qkv, qkv, qkv, bias, do))


def _rel_bias_grad(dbias, bucket):
    def body(db_ref, bk_ref, out_ref):
        bk = bk_ref[...]
        lane = lax.broadcasted_iota(jnp.int32, (1, 128), 1)
        for h in range(N_HEADS):
            d = db_ref[h // GROUP, pl.ds((h % GROUP) * BLK, BLK), :]
            row = jnp.zeros((1, 128), F32)
            for b in range(N_BUCKETS):
                tot = jnp.sum(jnp.sum(jnp.where(bk == b, d, 0.0), axis=1, keepdims=True), axis=0, keepdims=True)
                row = jnp.where(lane == b, tot, row)
            out_ref[pl.ds(h, 1), :] = row

    vm = pl.BlockSpec(memory_space=pltpu.VMEM)
    return pl.pallas_call(body, name="rel_bias_grad", out_shape=_sds((N_HEADS, 128), F32),
                          in_specs=[vm, vm], out_specs=vm)(dbias, bucket)


def _gmlp_parts(zg, lg_ref, lb_ref):
    z = zg.astype(F32)
    ge = _gelu(z)
    u, vg = ge[:, 0:G_W], ge[:, G_W:2 * G_W]
    mu = jnp.mean(vg, axis=-1, keepdims=True)
    xc = vg - mu
    rstd = lax.rsqrt(jnp.mean(xc * xc, axis=-1, keepdims=True) + EPS)
    xh = xc * rstd
    return z, u, xh, rstd, xh * lg_ref[...] + lb_ref[...]


def _causal_weights(ws_ref, wc):
    t = lax.broadcasted_iota(jnp.int32, (BLK, BLK), 0)
    s = lax.broadcasted_iota(jnp.int32, (BLK, BLK), 1)
    for g in range(N_HEADS):
        wc[g] = jnp.where(s <= t, ws_ref[g], 0.0).astype(BF)


def _spatial(vb, wc, bst_ref, p, low):
    xp = vb[:, p * 128:(p + 1) * 128]
    s0 = _dot(wc[2 * p], xp) + bst_ref[:, 2 * p:2 * p + 1]
    s1 = _dot(wc[2 * p + 1], xp) + bst_ref[:, 2 * p + 1:2 * p + 2]
    return xp, jnp.where(low, s0, s1)


def _gmlp_fwd(zg, lg, lb, ws, bst):
    S = zg.shape[0]
    tb = min(ATT_TB, S // BLK)
    T = tb * BLK

    def body(zg_ref, lg_ref, lb_ref, ws_ref, bst_ref, o_ref, wc):
        @pl.when(pl.program_id(0) == 0)
        def _():
            _causal_weights(ws_ref, wc)
        low = lax.broadcasted_iota(jnp.int32, (BLK, 128), 1) < HD
        for j in range(tb):
            rows = slice(j * BLK, (j + 1) * BLK)
            _, u, _, _, vln = _gmlp_parts(zg_ref[rows, :], lg_ref, lb_ref)
            vb = vln.astype(BF)
            for p in range(4):
                _, sp = _spatial(vb, wc, bst_ref, p, low)
                o_ref[rows, p * 128:(p + 1) * 128] = (u[:, p * 128:(p + 1) * 128] * sp).astype(BF)

    return pl.pallas_call(
        body, name="gmlp_fwd", grid=(S // T,),
        out_shape=_sds((S, G_W), BF),
        in_specs=[pl.BlockSpec((T, 2 * G_W), lambda i: (i, 0)), _const((1, G_W)), _const((1, G_W)),
                  _const((N_HEADS, BLK, BLK)), _const((BLK, N_HEADS))],
        out_specs=pl.BlockSpec((T, G_W), lambda i: (i, 0)),
        scratch_shapes=[pltpu.VMEM((N_HEADS, BLK, BLK), BF)],
        compiler_params=_cp(1, 32),
    )(*_in_hbm(zg), lg, lb, ws, bst)


def _gmlp_bwd(zg, d_out, lg, lb, ws, bst):
    S = zg.shape[0]
    tb = min(ATT_TB, S // BLK)
    T = tb * BLK
    nb = S // T

    def body(zg_ref, d_ref, lg_ref, lb_ref, ws_ref, bst_ref, dzg_ref, dws_ref, dbs_ref, dlg_ref, dlb_ref, wc, dbacc):
        i = pl.program_id(0)

        @pl.when(i == 0)
        def _():
            _causal_weights(ws_ref, wc)
            dws_ref[...] = jnp.zeros_like(dws_ref)
            dlg_ref[...] = jnp.zeros_like(dlg_ref)
            dlb_ref[...] = jnp.zeros_like(dlb_ref)
            dbacc[...] = jnp.zeros_like(dbacc)

        low = lax.broadcasted_iota(jnp.int32, (BLK, 128), 1) < HD
        for j in range(tb):
            rows = slice(j * BLK, (j + 1) * BLK)
            z, u, xh, rstd, vln = _gmlp_parts(zg_ref[rows, :], lg_ref, lb_ref)
            vb = vln.astype(BF)
            d = d_ref[rows, :].astype(F32)
            du_parts, dvln_parts = [], []
            for p in range(4):
                xp, sp = _spatial(vb, wc, bst_ref, p, low)
                dp = d[:, p * 128:(p + 1) * 128]
                du_parts.append(dp * sp)
                dsp = dp * u[:, p * 128:(p + 1) * 128]
                dbacc[:, p * 128:(p + 1) * 128] += dsp
                d0 = jnp.where(low, dsp, 0.0).astype(BF)
                d1 = jnp.where(low, 0.0, dsp).astype(BF)
                dws_ref[2 * p] += _dot_nt(d0, xp)
                dws_ref[2 * p + 1] += _dot_nt(d1, xp)
                dvln_parts.append(_dot_tn(wc[2 * p], d0) + _dot_tn(wc[2 * p + 1], d1))
            dvln = jnp.concatenate(dvln_parts, axis=1)
            dlg_ref[...] += _colsum(dvln * xh)
            dlb_ref[...] += _colsum(dvln)
            dxh = dvln * lg_ref[...]
            dvg = rstd * (dxh - jnp.mean(dxh, axis=-1, keepdims=True)
                          - xh * jnp.mean(dxh * xh, axis=-1, keepdims=True))
            dge = jnp.concatenate(du_parts + [dvg], axis=1)
            dzg_ref[rows, :] = (dge * _gelu_grad(z)).astype(BF)

        @pl.when(i == nb - 1)
        def _():
            t = lax.broadcasted_iota(jnp.int32, (BLK, BLK), 0)
            s = lax.broadcasted_iota(jnp.int32, (BLK, BLK), 1)
            for g in range(N_HEADS):
                dws_ref[g] = jnp.where(s <= t, dws_ref[g], 0.0)
            grp = lax.broadcasted_iota(jnp.int32, (N_HEADS, G_W), 0)
            lane = lax.broadcasted_iota(jnp.int32, (N_HEADS, G_W), 1) // HD
            pick = jnp.where(grp == lane, 1.0, 0.0).astype(F32)
            dbs_ref[...] = lax.dot_general(pick, dbacc[...], (((1,), (1,)), ((), ())),
                                           preferred_element_type=F32, precision=HIGH)

    return pl.pallas_call(
        body, name="gmlp_bwd", grid=(nb,),
        out_shape=[_sds((S, 2 * G_W), BF), _sds((N_HEADS, BLK, BLK), F32), _sds((N_HEADS, BLK), F32),
                   _sds((1, G_W), F32), _sds((1, G_W), F32)],
        in_specs=[pl.BlockSpec((T, 2 * G_W), lambda i: (i, 0)), pl.BlockSpec((T, G_W), lambda i: (i, 0)),
                  _const((1, G_W)), _const((1, G_W)), _const((N_HEADS, BLK, BLK)), _const((BLK, N_HEADS))],
        out_specs=[pl.BlockSpec((T, 2 * G_W), lambda i: (i, 0)), _const((N_HEADS, BLK, BLK)),
                   _const((N_HEADS, BLK)), _const((1, G_W)), _const((1, G_W))],
        scratch_shapes=[pltpu.VMEM((N_HEADS, BLK, BLK), BF), pltpu.VMEM((BLK, G_W), F32)],
        compiler_params=_cp(1, 32),
    )(*_in_hbm(zg, d_out), lg, lb, ws, bst)


def _mix_out(o, gm, gates, h, wa, wg, wo, gate, gp, after=()):
    S = h.shape[0]
    R = min(512, S)

    def body(o_ref, gm_ref, gates_ref, h_ref, wa_ref, wg_ref, wo_ref, gate_ref, gp_ref,
             ya_ref, yg_ref, ym_ref, y_ref, hn_ref):
        for r0 in range(0, R, CHUNK):
            rows = slice(r0, r0 + CHUNK)
            ya = _dot(o_ref[rows, :], wa_ref[...])
            yg = _dot(gm_ref[rows, :], wg_ref[...])
            ya_ref[rows, :] = ya.astype(BF)
            yg_ref[rows, :] = yg.astype(BF)
            ym = (gates_ref[rows, 0:D].astype(F32) * ya + gates_ref[rows, D:2 * D].astype(F32) * yg).astype(BF)
            ym_ref[rows, :] = ym
            y = _dot(ym, wo_ref[...])
            y_ref[rows, :] = y.astype(BF)
            hn_ref[rows, :] = h_ref[rows, :] + gate_ref[...] * (y * _rms_r(y) * gp_ref[...])

    vec = _const((1, D))
    rows = lambda w_: pl.BlockSpec((R, w_), lambda i: (i, 0))
    body, after_specs = _behind(body, 9, after)
    return pl.pallas_call(
        body, name="mix_out", grid=(S // R,),
        out_shape=[_sds((S, D), BF)] * 4 + [_sds((S, D), F32)],
        in_specs=[rows(Q_W), rows(G_W), rows(2 * D), rows(D), _resident((Q_W, D)), _resident((G_W, D)),
                  _resident((D, D)), vec, vec] + after_specs,
        out_specs=[rows(D)] * 5,
        compiler_params=_cp(1, 48),
    )(*_in_hbm(o, gm, gates, h, wa, wg, wo), gate, gp, *after)


def _mix_out_bwd(dh, y, ya, yg, gates, att, gm, ymix, wa, wg, wo, gate, gp, after=()):
    S = dh.shape[0]
    R = min(512, S)
    nb = S // R

    def body(dh_ref, y_ref, ya_ref, yg_ref, gates_ref, att_ref, gm_ref, ym_ref, wa_ref, wg_ref, wo_ref,
             gate_ref, gp_ref, dz_ref, do_ref, dgm_ref, dgate_ref, dgp_ref, gwo_ref, gwa_ref, gwg_ref,
             acc_o, acc_a, acc_g, dy_scr, dya_scr, dyg_scr):
        i = pl.program_id(0)

        @pl.when(i == 0)
        def _():
            for r in (dgate_ref, dgp_ref, acc_o, acc_a, acc_g):
                r[...] = jnp.zeros_like(r)
        for r0 in range(0, R, 2 * CHUNK):
            rows = slice(r0, min(r0 + 2 * CHUNK, R))
            dy, dgate, dgp = _postnorm_bwd(dh_ref[rows, :], y_ref[rows, :], gate_ref[...], gp_ref[...], 1.0)
            dgate_ref[...] += dgate
            dgp_ref[...] += dgp
            dyb = dy.astype(BF)
            dy_scr[rows, :] = dyb
            dym = _dot_nt(dyb, wo_ref[...])
            ga = gates_ref[rows, 0:D].astype(F32)
            gg = gates_ref[rows, D:2 * D].astype(F32)
            dya = (dym * ga).astype(BF)
            dyg = (dym * gg).astype(BF)
            dya_scr[rows, :] = dya
            dyg_scr[rows, :] = dyg
            dz_ref[rows, 0:D] = (dym * ya_ref[rows, :].astype(F32) * (ga * (1.0 - ga))).astype(BF)
            dz_ref[rows, D:2 * D] = (dym * yg_ref[rows, :].astype(F32) * (gg * (1.0 - gg))).astype(BF)
            do_ref[rows, :] = _dot_nt(dya, wa_ref[...]).astype(BF)
            dgm_ref[rows, :] = _dot_nt(dyg, wg_ref[...]).astype(BF)
        for m0 in range(0, D, CHUNK):
            acc_o[m0:m0 + CHUNK, :] += _dot_tn(ym_ref[:, m0:m0 + CHUNK], dy_scr[...])
        for m0 in range(0, Q_W, CHUNK):
            acc_a[m0:m0 + CHUNK, :] += _dot_tn(att_ref[:, m0:m0 + CHUNK], dya_scr[...])
            acc_g[m0:m0 + CHUNK, :] += _dot_tn(gm_ref[:, m0:m0 + CHUNK], dyg_scr[...])

        @pl.when(i == nb - 1)
        def _():
            for m0 in range(0, D, CHUNK):
                gwo_ref[m0:m0 + CHUNK, :] = acc_o[m0:m0 + CHUNK, :].astype(BF)
            for m0 in range(0, Q_W, CHUNK):
                gwa_ref[m0:m0 + CHUNK, :] = acc_a[m0:m0 + CHUNK, :].astype(BF)
                gwg_ref[m0:m0 + CHUNK, :] = acc_g[m0:m0 + CHUNK, :].astype(BF)

    vec = _const((1, D))
    rows = lambda w_: pl.BlockSpec((R, w_), lambda i: (i, 0))
    body, after_specs = _behind(body, 13, after)
    return pl.pallas_call(
        body, name="mix_out_bwd", grid=(nb,),
        out_shape=[_sds((S, 2 * D), BF), _sds((S, Q_W), BF), _sds((S, G_W), BF), _sds((1, D), F32),
                   _sds((1, D), F32), _sds((D, D), BF), _sds((Q_W, D), BF), _sds((G_W, D), BF)],
        in_specs=[rows(D), rows(D), rows(D), rows(D), rows(2 * D), rows(Q_W), rows(G_W), rows(D),
                  _resident((Q_W, D)), _resident((G_W, D)), _resident((D, D)), vec, vec] + after_specs,
        out_specs=[rows(2 * D), rows(Q_W), rows(G_W), vec, vec, _const((D, D)), _const((Q_W, D)),
                   _const((G_W, D))],
        scratch_shapes=[pltpu.VMEM((D, D), F32), pltpu.VMEM((Q_W, D), F32), pltpu.VMEM((G_W, D), F32)]
        + [pltpu.VMEM((R, D), BF)] * 3,
        compiler_params=_cp(1, 60),
    )(*_in_hbm(dh, y, ya, yg, gates, att, gm, ymix, wa, wg, wo), gate, gp, *after)


def _mix_dn(dq, dkv, dzg, dzgate, w, wq, h, dh, sc, gp, after=()):
    S = h.shape[0]
    R = min(512, S)

    def body(dq_ref, dkv_ref, dzg_ref, dzt_ref, w_ref, wq_ref, h_ref, dh_ref, sc_ref, gp_ref,
             out_ref, dsh_ref, dsc_ref, dgp_ref):
        @pl.when(pl.program_id(0) == 0)
        def _():
            dsh_ref[...] = jnp.zeros_like(dsh_ref)
            dsc_ref[...] = jnp.zeros_like(dsc_ref)
            dgp_ref[...] = jnp.zeros_like(dgp_ref)
        for r0 in range(0, R, CHUNK):
            rows = slice(r0, r0 + CHUNK)
            dn = _dot(dq_ref[rows, :], wq_ref[...])
            dn = dn + _dot(dkv_ref[rows, :], w_ref[Q_W:QKV_W, :])
            dn = dn + _dot(dzg_ref[rows, :], w_ref[ZG_OFF:GATE_OFF, :])
            dn = dn + _dot(dzt_ref[rows, :], w_ref[GATE_OFF:IN_W, :])
            dx, dsh, dsc, dgp = _prenorm_bwd(dn, h_ref[rows, :], gp_ref[...], sc_ref[...])
            out_ref[rows, :] = dh_ref[rows, :] + dx
            dsh_ref[...] += dsh
            dsc_ref[...] += dsc
            dgp_ref[...] += dgp

    vec = _const((1, D))
    rows = lambda w_: pl.BlockSpec((R, w_), lambda i: (i, 0))
    body, after_specs = _behind(body, 10, after)
    return pl.pallas_call(
        body, name="mix_dn", grid=(S // R,),
        out_shape=[_sds((S, D), F32)] + [_sds((1, D), F32)] * 3,
        in_specs=[rows(Q_W), rows(2 * KV_W), rows(2 * G_W), rows(2 * D), _resident((IN_W, D)),
                  _resident((Q_W, D)), rows(D), rows(D), vec, vec] + after_specs,
        out_specs=[rows(D), vec, vec, vec],
        compiler_params=_cp(1, 48),
    )(*_in_hbm(dq, dkv, dzg, dzgate, w, wq, h, dh), sc, gp, *after)


def _adamw_math(w, g, m, v):
    m2 = ADAM_B1 * m + (1.0 - ADAM_B1) * g
    v2 = ADAM_B2 * v + (1.0 - ADAM_B2) * (g * g)
    m_hat = m2 / (1.0 - ADAM_B1 ** ADAM_STEP)
    v_hat = v2 / (1.0 - ADAM_B2 ** ADAM_STEP)
    delta = -ADAM_LR * (m_hat / (jnp.sqrt(v_hat) + ADAM_EPS) + ADAM_WD * w)
    return delta, m2, v2


def _row_tile(rows, cols):
    best = None
    for t in range(16, rows + 1, 16):
        if rows % t == 0 and t * cols <= 64 * 1024:
            best = t
    return best if best is not None else rows


def _adamw_sharded(landing, w, m, v, name):
    r, c = w.shape
    tr = _row_tile(r, c)

    def body(l_ref, w_ref, m_ref, v_ref, g_ref, d_ref, m2_ref, v2_ref):
        g = l_ref[0].astype(F32)
        for j in range(1, N_DEV):
            g = g + l_ref[j].astype(F32)
        delta, m2, v2 = _adamw_math(w_ref[...], g, m_ref[...], v_ref[...])
        g_ref[...] = g
        d_ref[...] = delta
        m2_ref[...] = m2
        v2_ref[...] = v2

    row = pl.BlockSpec((tr, c), lambda i: (i, 0))
    return pl.pallas_call(
        body, name=name, grid=(r // tr,),
        out_shape=[_sds((r, c), F32)] * 4,
        in_specs=[pl.BlockSpec((N_DEV, tr, c), lambda i: (0, i, 0)), row, row, row],
        out_specs=[row] * 4,
        compiler_params=_cp(1, 48),
    )(*_in_hbm(landing, w, m, v))


def _adamw_small(items):
    n = len(items)

    def body(*refs):
        for k in range(n):
            w_ref, g_ref, m_ref, v_ref = refs[4 * k:4 * k + 4]
            outs = refs[4 * n + 3 * k:4 * n + 3 * k + 3]
            for o_ref, val in zip(outs, _adamw_math(w_ref[...], g_ref[...], m_ref[...], v_ref[...])):
                o_ref[...] = val

    vm = pl.BlockSpec(memory_space=pltpu.VMEM)
    flat = pl.pallas_call(
        body, name="adamw_small",
        out_shape=[_sds(it[0].shape, F32) for it in items for _ in range(3)],
        in_specs=[vm] * (4 * n), out_specs=[vm] * (3 * n),
    )(*[a for it in items for a in it])
    return [tuple(flat[3 * k:3 * k + 3]) for k in range(n)]


def _w_ada_update(c8, d_ada, w, m, v):
    tr = 256

    def body(c_ref, d_ref, w_ref, m_ref, v_ref, g_ref, dl_ref, m2_ref, v2_ref):
        cs = c_ref[...]
        cs = cs * jax.nn.sigmoid(cs)
        g = lax.dot_general(cs, d_ref[...], (((0,), (0,)), ((), ())), preferred_element_type=F32, precision=HIGH)
        delta, m2, v2 = _adamw_math(w_ref[...], g, m_ref[...], v_ref[...])
        g_ref[...] = g
        dl_ref[...] = delta
        m2_ref[...] = m2
        v2_ref[...] = v2

    row = pl.BlockSpec((tr, ADA_W), lambda i: (i, 0))
    return pl.pallas_call(
        body, name="w_ada_update", grid=(D // tr,),
        out_shape=[_sds((D, ADA_W), F32)] * 4,
        in_specs=[pl.BlockSpec((N_DEV, tr), lambda i: (0, i)), _const((N_DEV, ADA_W)), row, row, row],
        out_specs=[row] * 4,
        compiler_params=_cp(1, 40),
    )(c8, d_ada, *_in_hbm(w, m, v))


def _t5_bucket():
    qi = np.arange(BLK, dtype=np.int32)[:, None]
    kj = np.arange(2 * BLK, dtype=np.int32)[None, :]
    dist = np.maximum(qi + BLK - kj, 0)
    max_exact = N_BUCKETS // 2
    d_f = np.maximum(dist, max_exact).astype(np.float32)
    large = max_exact + (np.log(d_f / np.float32(max_exact)) / np.float32(math.log(MAX_DISTANCE / max_exact))
                         * np.float32(N_BUCKETS - max_exact)).astype(np.int32)
    large = np.minimum(large, N_BUCKETS - 1)
    return jnp.asarray(np.where(dist < max_exact, dist, large).astype(np.int32))


def _slabs_of_columns(w):
    r, c8 = w.shape
    return jnp.transpose(w.reshape(r, N_DEV, c8 // N_DEV), (1, 0, 2))


def _columns_of_slabs(w8):
    _, r, c = w8.shape
    return jnp.transpose(w8, (1, 0, 2)).reshape(r, N_DEV * c)


def kernel(x, c, rel_bias, w_ada, b_ada, pre_norm_g, post_norm_g, w_ffn1_in, w_ffn1_out, w_in, sinks, gmlp_ln_g, gmlp_ln_b, gmlp_w_s, gmlp_b_s, w_br_attn, w_br_gmlp, w_out, w_ffn2_in, w_ffn2_out, loss_target, m_rel_bias, m_w_ada, m_b_ada, m_pre_norm_g, m_post_norm_g, m_w_ffn1_in, m_w_ffn1_out, m_w_in, m_sinks, m_gmlp_ln_g, m_gmlp_ln_b, m_gmlp_w_s, m_gmlp_b_s, m_w_br_attn, m_w_br_gmlp, m_w_out, m_w_ffn2_in, m_w_ffn2_out, v_rel_bias, v_w_ada, v_b_ada, v_pre_norm_g, v_post_norm_g, v_w_ffn1_in, v_w_ffn1_out, v_w_in, v_sinks, v_gmlp_ln_g, v_gmlp_ln_b, v_gmlp_w_s, v_gmlp_b_s, v_w_br_attn, v_w_br_gmlp, v_w_out, v_w_ffn2_in, v_w_ffn2_out):
    me = 4 * lax.axis_index("x") + 2 * lax.axis_index("y") + lax.axis_index("c")
    x0 = x[0]
    target = loss_target[0]

    transposed = ("w_ffn1_in", "w_in", "w_ffn2_in")
    shards = [w_ffn1_in[0].T, w_ffn1_out[0], w_in[0].T, w_br_attn[0], w_br_gmlp[0], w_out[0],
              w_ffn2_in[0].T, w_ffn2_out[0]]
    shards_bf = [s.astype(BF) for s in shards]
    groups = [shards_bf[0:1], shards_bf[1:6], shards_bf[6:8]]

    def gather_start(i, after):
        return _slabs_start("gather", groups[i], after, "gather_start_%d" % i)

    def forward_start(st, i, after):
        lands = _slabs_wait("gather", len(groups[i]), st, after, "gather_wait_%d" % i)
        return _slabs_start("forward", lands, c, "forward_start_%d" % i)

    def gathered(st, i, after):
        return _slabs_wait("forward", len(groups[i]), st, after, "forward_wait_%d" % i)

    gs0 = gather_start(0, c)

    mine = jnp.concatenate([c[0], pre_norm_g[0].reshape(-1), post_norm_g[0].reshape(-1)])
    small8 = jnp.broadcast_to(mine[None, :], (8, mine.shape[0]))
    b_ada64 = jnp.repeat(b_ada.reshape(N_DEV, ADA_W), 8, axis=0)
    gath, ada64 = _ada_forward(small8, w_ada[0], b_ada64)
    gath8 = gath[::8]
    ada = ada64[::8].reshape(9, D)
    sh1, sc1, g1, sh2, sc2, g2, sh3, sc3, g3 = [ada[k:k + 1] for k in range(9)]
    gains = gath8[:, D:].reshape(N_DEV, 2, 3, 128)
    pre_g = jnp.transpose(gains[:, 0], (1, 0, 2)).reshape(3, D)
    post_g = jnp.transpose(gains[:, 1], (1, 0, 2)).reshape(3, D)
    pre = [pre_g[k:k + 1] for k in range(3)]
    post = [post_g[k:k + 1] for k in range(3)]

    bucket = _t5_bucket()
    bias = _bias_table(rel_bias, bucket).reshape(HEAD_ROWS, 2 * BLK)
    sinks8 = sinks[0]
    lg, lb = gmlp_ln_g, gmlp_ln_b
    ws = gmlp_w_s[0]
    bst = jnp.transpose(gmlp_b_s[0])

    fs0 = forward_start(gs0, 0, sh1)
    gs1 = gather_start(1, fs0[-1])
    wf1_in = gathered(fs0, 0, gs1[-1])[0].reshape(2 * D_FF, D)
    n1, fg1, fu1, fa1 = _ffn_in(x0, sh1, sc1, pre[0], wf1_in, "ffn1_in")
    fs1 = forward_start(gs1, 1, n1)
    gs2 = gather_start(2, fs1[-1])
    mix_w = gathered(fs1, 1, gs2[-1])
    wf1_out = mix_w[0].reshape(D_FF, D)
    w_in_full = mix_w[1].reshape(IN_W, D)
    w_q = _pair_heads(w_in_full[0:Q_W])
    w_bra = _pair_heads(_columns_of_slabs(mix_w[2]))
    w_brg = _columns_of_slabs(mix_w[3])
    w_out_full = mix_w[4].reshape(D, D)
    h1, y1 = _ffn_out(fa1, wf1_out, x0, g1, post[0], "ffn1_out")
    n2, qkv, zg, gates = _mix_in(h1, sh2, sc2, pre[1], w_in_full, w_q)
    att = _attn_fwd(qkv, bias, sinks8)
    gm = _gmlp_fwd(zg, lg, lb, ws, bst)
    fs2 = forward_start(gs2, 2, gm)
    ya, yg, ymix, y2, h2 = _mix_out(att, gm, gates, h1, w_bra, w_brg, w_out_full, g2, post[1], after=(fs2[-1],))
    wf2_in, wf2_out = gathered(fs2, 2, h2)
    wf2_in = wf2_in.reshape(2 * D_FF, D)
    wf2_out = wf2_out.reshape(D_FF, D)
    n3, fg3, fu3, fa3 = _ffn_in(h2, sh3, sc3, pre[2], wf2_in, "ffn2_in")
    dh3, y3, sq = _ffn_out(fa3, wf2_out, h2, g3, post[2], "ffn2_out", target=target)

    def exchange_start(i, arrays):
        return _slabs_start("exchange", arrays, sq, "exchange_start_%d" % i)

    dy3, dgu3, dh2, d_g3, d_post2, d_sh3, d_sc3, d_pre2 = _ffn_bwd(
        dh3, y3, fg3, fu3, wf2_out, wf2_in, h2, g3, post[2], sc3, pre[2], "ffn2_bwd")
    gw_f2_out = _tn_matmul(fa3, dy3, "ffn2_out_wgrad", tm=D_FF // 2).reshape(N_DEV, D_FF // N_DEV, D)
    gw_f2_in = _tn_matmul(dgu3, n3, "ffn2_in_wgrad", tm=D_FF // 2).reshape(N_DEV, FS, D)
    ex1 = exchange_start(1, [gw_f2_out, gw_f2_in])

    dzgate, d_att, d_gm, d_g2, d_post1, gw_out, gw_bra, gw_brg = _mix_out_bwd(
        dh2, y2, ya, yg, gates, att, gm, ymix, w_bra, w_brg, w_out_full, g2, post[1], after=(ex1[-1],))
    ex2 = exchange_start(2, [_slabs_of_columns(_unpair_heads(gw_bra)), _slabs_of_columns(gw_brg),
                             gw_out.reshape(N_DEV, D // N_DEV, D)])
    dq, dkv, dbias, dsink = _attn_bwd(qkv, bias, sinks8, d_att)
    dzg, d_ws, d_bs, d_lg, d_lb = _gmlp_bwd(zg, d_gm, lg, lb, ws, bst)
    d_rel = _rel_bias_grad(dbias.reshape(N_KV, GROUP * BLK, 2 * BLK), bucket)
    early = jnp.concatenate([
        jnp.concatenate([d_lg.reshape(4, 128), d_lb.reshape(4, 128)], axis=0),
        d_bs, d_rel, dsink, d_ws.reshape(N_HEADS * BLK, BLK)], axis=0)
    sm0 = _slabs_start("gather_all", [early], sq, "small_gather_start")
    dh1, d_sh2, d_sc2, d_pre1 = _mix_dn(dq, dkv, dzg, dzgate, w_in_full, w_q, h1, dh2, sc2, pre[1],
                                        after=(ex2[-1], sm0[-1]))
    gw_in = jnp.concatenate(
        [_unpair_heads(_tn_matmul(dq, n2, "w_in_q_wgrad")), _tn_matmul(dkv, n2, "w_in_kv_wgrad"),
         _tn_matmul(dzg, n2, "w_in_zg_wgrad"), _tn_matmul(dzgate, n2, "w_in_gate_wgrad")],
        axis=0).reshape(N_DEV, IN_W // N_DEV, D)
    ex3 = exchange_start(3, [gw_in])

    dy1, dgu1, d_g1, d_post0 = _ffn_out_bwd(dh1, y1, fg1, fu1, wf1_out, g1, post[0], "ffn1_out_bwd",
                                            after=(ex3[-1],))
    gw_f1_out = _tn_matmul(fa1, dy1, "ffn1_out_wgrad", tm=D_FF // 2).reshape(N_DEV, D_FF // N_DEV, D)
    ex4 = exchange_start(4, [gw_f1_out])
    gw_f1_in = _tn_matmul(dgu1, n1, "ffn1_in_wgrad", tm=D_FF // 2).reshape(N_DEV, FS, D)
    ex5 = exchange_start(5, [gw_f1_in])
    grad_x, d_sh1, d_sc1, d_pre0 = _ffn_dn(dgu1, wf1_in, x0, dh1, sc1, pre[0], "ffn1_dn", after=(ex4[-1], ex5[-1]))

    landed = {}
    for i, (ex, nms) in enumerate([(ex1, ["w_ffn2_out", "w_ffn2_in"]),
                                   (ex2, ["w_br_attn", "w_br_gmlp", "w_out"]), (ex3, ["w_in"]),
                                   (ex4, ["w_ffn1_out"]), (ex5, ["w_ffn1_in"])]):
        for nm, land in zip(nms, _slabs_wait("exchange", len(nms), ex, grad_x, "exchange_wait_%d" % i)):
            landed[nm] = land
    moments = [(m_w_ffn1_in, v_w_ffn1_in), (m_w_ffn1_out, v_w_ffn1_out), (m_w_in, v_w_in),
               (m_w_br_attn, v_w_br_attn), (m_w_br_gmlp, v_w_br_gmlp), (m_w_out, v_w_out),
               (m_w_ffn2_in, v_w_ffn2_in), (m_w_ffn2_out, v_w_ffn2_out)]
    names = ["w_ffn1_in", "w_ffn1_out", "w_in", "w_br_attn", "w_br_gmlp", "w_out", "w_ffn2_in", "w_ffn2_out"]
    big = {}
    for nm, w_, (m_, v_) in zip(names, shards, moments):
        if nm in transposed:
            res4 = _adamw_sharded(landed[nm], w_, m_[0].T, v_[0].T, "adamw_" + nm)
            big[nm] = [a.T[None] for a in res4]
        else:
            big[nm] = [a[None] for a in _adamw_sharded(landed[nm], w_, m_[0], v_[0], "adamw_" + nm)]

    my_loss = jnp.broadcast_to(sq * (0.5 / D), (1, D))
    my_loss, _ = lax.optimization_barrier((my_loss, landed["w_ffn1_in"]))
    tot, every = _small_allreduce([d_sh1, d_sc1, d_g1, d_sh2, d_sc2, d_g2, d_sh3, d_sc3, d_g3,
                                   d_pre0, d_pre1, d_pre2, d_post0, d_post1, d_post2, my_loss])
    (early_land,) = _slabs_wait("gather_all", 1, sm0, grad_x, "small_gather_wait")
    tot_early = _sum_slabs(early_land)

    loss = tot[15, 0]
    g_b_ada = tot[0:9].reshape(1, 9 * D)
    g_pre = lax.dynamic_slice_in_dim(tot[9:12], 128 * me, 128, axis=1)[None]
    g_post = lax.dynamic_slice_in_dim(tot[12:15], 128 * me, 128, axis=1)[None]
    g_lg = tot_early[0:4].reshape(1, G_W)
    g_lb = tot_early[4:8].reshape(1, G_W)
    g_bs = tot_early[8:16][None]
    g_rel = jnp.transpose(tot_early[16:24, 0:N_BUCKETS])
    g_sinks = tot_early[24:32, 0][None]
    g_ws = tot_early[32:1056].reshape(1, N_HEADS, BLK, BLK)

    d_ada_mine = lax.dynamic_slice_in_dim(every[:, 0:9].reshape(N_DEV, 9 * D), ADA_W * me, ADA_W, axis=1)
    ada_out = [a[None] for a in _w_ada_update(gath8[:, 0:D], d_ada_mine, w_ada[0], m_w_ada[0], v_w_ada[0])]

    small = [("rel_bias", rel_bias, g_rel, m_rel_bias, v_rel_bias), ("b_ada", b_ada, g_b_ada, m_b_ada, v_b_ada),
             ("pre_norm_g", pre_norm_g, g_pre, m_pre_norm_g, v_pre_norm_g),
             ("post_norm_g", post_norm_g, g_post, m_post_norm_g, v_post_norm_g),
             ("sinks", sinks, g_sinks, m_sinks, v_sinks), ("gmlp_ln_g", gmlp_ln_g, g_lg, m_gmlp_ln_g, v_gmlp_ln_g),
             ("gmlp_ln_b", gmlp_ln_b, g_lb, m_gmlp_ln_b, v_gmlp_ln_b),
             ("gmlp_w_s", gmlp_w_s, g_ws, m_gmlp_w_s, v_gmlp_w_s), ("gmlp_b_s", gmlp_b_s, g_bs, m_gmlp_b_s, v_gmlp_b_s)]
    two_d = lambda a: a.reshape(int(math.prod(a.shape[:-1])), a.shape[-1])
    stepped = _adamw_small([tuple(two_d(a) for a in item[1:]) for item in small])
    res = {"w_ada": ada_out}
    for (nm, w_, g_, _, _), new in zip(small, stepped):
        res[nm] = [g_] + [a.reshape(w_.shape) for a in new]
    res.update(big)
    order = ["rel_bias", "w_ada", "b_ada", "pre_norm_g", "post_norm_g", "w_ffn1_in", "w_ffn1_out", "w_in", "sinks",
             "gmlp_ln_g", "gmlp_ln_b", "gmlp_w_s", "gmlp_b_s", "w_br_attn", "w_br_gmlp", "w_out", "w_ffn2_in",
             "w_ffn2_out"]
    outs = [loss, grad_x[None]]
    for k in range(4):
        outs += [res[nm][k] for nm in order]
    return tuple(outs)
```

```python
import math

import jax
import jax.numpy as jnp
import numpy as np
from jax import lax
from jax.experimental import pallas as pl
from jax.experimental.pallas import tpu as pltpu

F32 = jnp.float32
BF = jnp.bfloat16

N_DEV = 8
D = 1024
D_FF = 2816
FS = D_FF // 4
N_HEADS = 8
N_KV = 2
GROUP = 4
HD = 64
BLK = 128
Q_W = 512
KV_W = 128
G_W = 512
QKV_W = Q_W + 2 * KV_W
ZG_OFF = QKV_W
GATE_OFF = ZG_OFF + 2 * G_W
IN_W = GATE_OFF + 2 * D
N_BUCKETS = 32
MAX_DISTANCE = 128
EPS = 1e-6
NEG = -1e30
SCALE = HD ** -0.5
ADA_W = 9 * D // N_DEV

ADAM_LR = 0.001
ADAM_B1 = 0.9
ADAM_B2 = 0.999
ADAM_EPS = 1e-08
ADAM_WD = 0.01
ADAM_STEP = 10

CHUNK = 256
MIB = 1024 * 1024
MESH = pl.DeviceIdType.MESH
HIGH = lax.Precision.HIGHEST


def _cp(n_grid, vmem_mib):
    return pltpu.CompilerParams(dimension_semantics=("arbitrary",) * n_grid,
                                vmem_limit_bytes=vmem_mib * MIB)


def _const(shape):
    return pl.BlockSpec(shape, lambda *_: (0,) * len(shape))


def _resident(shape):
    return pl.BlockSpec(shape, lambda *_: (0,) * len(shape), pipeline_mode=pl.Buffered(1))


def _behind(body, n_in, after):
    k = len(after)
    return (lambda *refs: body(*refs[:n_in], *refs[n_in + k:])), [pl.BlockSpec(memory_space=pl.ANY)] * k


def _in_hbm(*arrays):
    return [pltpu.with_memory_space_constraint(a, pltpu.HBM) for a in arrays]


def _sds(shape, dtype):
    return jax.ShapeDtypeStruct(shape, dtype)


def _dot(a, b):
    return jnp.dot(a, b, preferred_element_type=F32)


def _dot_nt(a, b):
    return lax.dot_general(a, b, (((1,), (1,)), ((), ())), preferred_element_type=F32)


def _dot_tn(a, b):
    return lax.dot_general(a, b, (((0,), (0,)), ((), ())), preferred_element_type=F32)


def _rms_r(x):
    return lax.rsqrt(jnp.mean(x * x, axis=-1, keepdims=True) + EPS)


def _colsum(x):
    return jnp.sum(x, axis=0, keepdims=True)


def _prenorm(x, gp, sc, sh):
    return (x * _rms_r(x) * gp) * (1.0 + sc) + sh


def _prenorm_bwd(dn, x, gp, sc):
    r = _rms_r(x)
    xh = x * r
    t = dn * (1.0 + sc) * gp
    dx = r * (t - xh * jnp.mean(t * xh, axis=-1, keepdims=True))
    return dx, _colsum(dn), _colsum(dn * xh * gp), _colsum(dn * (1.0 + sc) * xh)


def _postnorm_bwd(dh, y, gate, gp, res):
    y = y.astype(F32)
    r = _rms_r(y)
    yh = y * r
    dyn = (res * gate) * dh
    t = dyn * gp
    dy = r * (t - yh * jnp.mean(t * yh, axis=-1, keepdims=True))
    return dy, _colsum(res * dh * yh * gp), _colsum(dyn * yh)


def _gelu(x):
    k = math.sqrt(2.0 / math.pi)
    return 0.5 * x * (1.0 + jnp.tanh(k * (x + 0.044715 * x * x * x)))


def _gelu_grad(x):
    k = math.sqrt(2.0 / math.pi)
    t = jnp.tanh(k * (x + 0.044715 * x * x * x))
    return 0.5 * (1.0 + t) + 0.5 * x * (1.0 - t * t) * (k * (1.0 + 3.0 * 0.044715 * x * x))


def _my_place():
    x, y, c = lax.axis_index("x"), lax.axis_index("y"), lax.axis_index("c")
    return x, y, c, 4 * x + 2 * y + c


def _peer(x, y, c, k):
    px = 1 - x if k & 4 else x
    py = 1 - y if k & 2 else y
    pc = 1 - c if k & 1 else c
    return (px, py, pc), 4 * px + 2 * py + pc


HBM_SPEC = pl.BlockSpec(memory_space=pltpu.HBM)
SEM_SPEC = pl.BlockSpec(memory_space=pltpu.SEMAPHORE)
EFFECT = pltpu.SideEffectType.DATAFLOW_SIDE_EFFECTING


RELATIONS = {"exchange": (1, 2, 3, 4, 5, 6, 7), "gather": (1, 2, 4, 6), "forward": (2, 4, 6),
             "gather_all": (1, 2, 3, 4, 5, 6, 7)}


def _slab_copies(mode, srcs, lands, send, recv, loc):
    x, y, c, me = _my_place()
    rel = RELATIONS[mode]
    remote, local = [], []
    for t in range(len(lands)):
        for i, k in enumerate(rel):
            peer, peer_lin = _peer(x, y, c, k)
            if mode == "exchange":
                src, dst, to = srcs[t].at[peer_lin], lands[t].at[me], peer
            elif mode in ("gather", "gather_all"):
                src, dst, to = srcs[t], lands[t].at[me], peer
            else:
                src, dst, to = lands[t].at[peer_lin], lands[t].at[peer_lin], _peer(x, y, c, 1)[0]
            remote.append(pltpu.make_async_remote_copy(
                src_ref=src, dst_ref=dst, send_sem=send.at[t * len(rel) + i], recv_sem=recv.at[t * len(rel) + i],
                device_id=to, device_id_type=MESH))
        if mode == "exchange":
            local.append(pltpu.make_async_copy(srcs[t].at[me], lands[t].at[me], loc.at[t]))
        elif mode in ("gather", "gather_all"):
            local.append(pltpu.make_async_copy(srcs[t], lands[t].at[me], loc.at[t]))
    return remote, local


def _slabs_start(mode, arrays, after, name):
    n = len(arrays)
    if mode == "forward":
        thru = list(arrays)
    else:
        shapes = [a.shape if mode == "exchange" else (N_DEV,) + a.shape for a in arrays]
        thru = list(arrays) + [lax.empty(s, a.dtype) for s, a in zip(shapes, arrays)]
    m = len(thru)
    n_sem = n * len(RELATIONS[mode])

    def body(*refs):
        srcs, lands = refs[:n], refs[m - n:m]
        send, recv, loc = refs[m + 1:m + 4]
        remote, local = _slab_copies(mode, srcs, lands, send, recv, loc)
        for cp in remote + local:
            cp.start()
        refs[-1][...] = jnp.zeros_like(refs[-1])

    return pl.pallas_call(
        body, name=name,
        out_shape=(pltpu.SemaphoreType.DMA((n_sem,)), pltpu.SemaphoreType.DMA((n_sem,)),
                   pltpu.SemaphoreType.DMA((n,)),
                   *[pltpu.HBM(a.shape, a.dtype) for a in thru],
                   _sds((1, D), F32)),
        in_specs=[HBM_SPEC] * m + [pl.BlockSpec(memory_space=pl.ANY)],
        out_specs=(SEM_SPEC, SEM_SPEC, SEM_SPEC, *[HBM_SPEC] * m, pl.BlockSpec(memory_space=pltpu.VMEM)),
        input_output_aliases={t: 3 + t for t in range(m)},
        compiler_params=pltpu.CompilerParams(has_side_effects=EFFECT),
    )(*[pltpu.with_memory_space_constraint(a, pltpu.HBM) for a in thru], after)


def _slabs_wait(mode, n, started, after, name):
    sems = started[0:3]
    thru = started[3:-1]
    m = len(thru)

    def body(*refs):
        srcs, lands = refs[:n], refs[m - n:m]
        remote, local = _slab_copies(mode, srcs, lands, *refs[m:m + 3])
        for cp in remote:
            cp.wait_send()
            cp.wait_recv()
        for cp in local:
            cp.wait()

    res = pl.pallas_call(
        body, name=name,
        out_shape=tuple(pltpu.HBM(a.shape, a.dtype) for a in thru),
        in_specs=[HBM_SPEC] * m + [SEM_SPEC] * 3 + [pl.BlockSpec(memory_space=pl.ANY)],
        out_specs=tuple([HBM_SPEC] * m),
        input_output_aliases={t: t for t in range(m)},
        compiler_params=pltpu.CompilerParams(has_side_effects=EFFECT),
    )(*thru, *sems, after)
    return list(res[m - n:m])


def _ada_forward(small8, w_ada, b_ada64):
    sw = small8.shape[1]

    def body(sm_ref, w_ref, b_ref, gath_ref, ada_ref, part_ref, send1, recv1, send2, recv2):
        x, y, c, me = _my_place()
        row_me = pl.multiple_of(me * 8, 8)
        gath_ref[pl.ds(row_me, 8), :] = sm_ref[...]
        first = []
        for k in range(1, N_DEV):
            peer, _ = _peer(x, y, c, k)
            cp = pltpu.make_async_remote_copy(
                src_ref=sm_ref, dst_ref=gath_ref.at[pl.ds(row_me, 8), :], send_sem=send1.at[k - 1],
                recv_sem=recv1.at[k - 1], device_id=peer, device_id_type=MESH)
            cp.start()
            first.append(cp)
        for cp in first:
            cp.wait()
        cs = gath_ref[:, 0:D]
        cs = cs * jax.nn.sigmoid(cs)
        part_ref[...] = jnp.dot(cs, w_ref[...], preferred_element_type=F32, precision=HIGH)
        ada_ref[pl.ds(row_me, 8), :] = part_ref[pl.ds(row_me, 8), :]
        second = []
        for k in range(1, N_DEV):
            peer, peer_lin = _peer(x, y, c, k)
            cp = pltpu.make_async_remote_copy(
                src_ref=part_ref.at[pl.ds(pl.multiple_of(peer_lin * 8, 8), 8), :],
                dst_ref=ada_ref.at[pl.ds(row_me, 8), :], send_sem=send2.at[k - 1],
                recv_sem=recv2.at[k - 1], device_id=peer, device_id_type=MESH)
            cp.start()
            second.append(cp)
        for cp in second:
            cp.wait()
        ada_ref[...] = ada_ref[...] + b_ref[...]

    vm = pl.BlockSpec(memory_space=pltpu.VMEM)
    return pl.pallas_call(
        body, name="ada_forward",
        out_shape=[_sds((8 * N_DEV, sw), F32), _sds((8 * N_DEV, ADA_W), F32)],
        in_specs=[vm, vm, vm], out_specs=[vm, vm],
        scratch_shapes=[pltpu.VMEM((8 * N_DEV, ADA_W), F32)] + [pltpu.SemaphoreType.DMA((7,))] * 4,
        compiler_params=pltpu.CompilerParams(vmem_limit_bytes=32 * MIB),
    )(small8, w_ada, b_ada64)


def _sum_slabs(land):
    def body(l_ref, o_ref):
        acc = l_ref[0]
        for j in range(1, N_DEV):
            acc = acc + l_ref[j]
        o_ref[...] = acc

    vm = pl.BlockSpec(memory_space=pltpu.VMEM)
    return pl.pallas_call(body, name="sum_slabs", out_shape=_sds(land.shape[1:], F32), in_specs=[vm], out_specs=vm,
                          compiler_params=pltpu.CompilerParams(vmem_limit_bytes=32 * MIB))(land)


def _small_allreduce(vectors):
    n = len(vectors)

    def body(*refs):
        v_refs, (sum_ref, gath_ref, pack, send, recv) = refs[:n], refs[n:]
        x, y, c, me = _my_place()
        for k in range(n):
            pack[k:k + 1, :] = v_refs[k][...]
        gath_ref[me] = pack[...]
        cps = []
        for k in range(1, N_DEV):
            peer, _ = _peer(x, y, c, k)
            cp = pltpu.make_async_remote_copy(
                src_ref=pack, dst_ref=gath_ref.at[me], send_sem=send.at[k - 1],
                recv_sem=recv.at[k - 1], device_id=peer, device_id_type=MESH)
            cp.start()
            cps.append(cp)
        for cp in cps:
            cp.wait()
        acc = gath_ref[0]
        for j in range(1, N_DEV):
            acc = acc + gath_ref[j]
        sum_ref[...] = acc

    vm = pl.BlockSpec(memory_space=pltpu.VMEM)
    return pl.pallas_call(
        body, name="small_allreduce",
        out_shape=[_sds((n, D), F32), _sds((N_DEV, n, D), F32)],
        in_specs=[vm] * n, out_specs=[vm, vm],
        scratch_shapes=[pltpu.VMEM((n, D), F32), pltpu.SemaphoreType.DMA((7,)), pltpu.SemaphoreType.DMA((7,))],
    )(*vectors)


F_TILES = tuple((f0, min(512, D_FF - f0)) for f0 in range(0, D_FF, 512))
F_TILES_NARROW = tuple((f0, 256) for f0 in range(0, D_FF, 256))


def _swiglu_tile(n, wt_ref, f0, tf):
    g = _dot_nt(n, wt_ref[f0:f0 + tf, :])
    u = _dot_nt(n, wt_ref[D_FF + f0:D_FF + f0 + tf, :])
    sg = jax.nn.sigmoid(g)
    silu = g * sg
    return (u * (sg * (1.0 + g * (1.0 - sg)))).astype(BF), silu.astype(BF), (silu * u).astype(BF)


def _ffn_in(h, sh, sc, gp, wt, name):
    S = h.shape[0]
    R = min(512, S)

    def body(h_ref, sh_ref, sc_ref, gp_ref, w_ref, n_ref, dg_ref, sl_ref, a_ref):
        for r0 in range(0, R, CHUNK):
            rows = slice(r0, r0 + CHUNK)
            n = _prenorm(h_ref[rows, :], gp_ref[...], sc_ref[...], sh_ref[...]).astype(BF)
            n_ref[rows, :] = n
            for f0, tf in F_TILES_NARROW:
                dg_ref[rows, f0:f0 + tf], sl_ref[rows, f0:f0 + tf], a_ref[rows, f0:f0 + tf] = _swiglu_tile(
                    n, w_ref, f0, tf)

    vec = _const((1, D))
    rows_ = lambda w_: pl.BlockSpec((R, w_), lambda i: (i, 0))
    return pl.pallas_call(
        body, name=name, grid=(S // R,),
        out_shape=[_sds((S, D), BF)] + [_sds((S, D_FF), BF)] * 3,
        in_specs=[rows_(D), vec, vec, vec, _resident((2 * D_FF, D))],
        out_specs=[rows_(D), rows_(D_FF), rows_(D_FF), rows_(D_FF)],
        compiler_params=_cp(1, 56),
    )(*_in_hbm(h), sh, sc, gp, *_in_hbm(wt))


def _ffn_out(a, w, h, gate, gp, name, target=None):
    S = h.shape[0]
    R = min(512, S)
    with_loss = target is not None

    def body(a_ref, w_ref, h_ref, gate_ref, gp_ref, *rest):
        if with_loss:
            t_ref, out_ref, y_ref, tot_ref = rest

            @pl.when(pl.program_id(0) == 0)
            def _():
                tot_ref[...] = jnp.zeros_like(tot_ref)
        else:
            out_ref, y_ref = rest
        for r0 in range(0, R, CHUNK):
            rows = slice(r0, r0 + CHUNK)
            y = _dot(a_ref[rows, :], w_ref[...])
            y_ref[rows, :] = y.astype(BF)
            hn = h_ref[rows, :] + (0.5 * gate_ref[...]) * (y * _rms_r(y) * gp_ref[...])
            if with_loss:
                e = hn - t_ref[rows, :]
                out_ref[rows, :] = e * (1.0 / D)
                tot_ref[...] += jnp.sum(jnp.sum(e * e, axis=1, keepdims=True), axis=0, keepdims=True)
            else:
                out_ref[rows, :] = hn

    vec = _const((1, D))
    rows_ = lambda w_: pl.BlockSpec((R, w_), lambda i: (i, 0))
    return pl.pallas_call(
        body, name=name, grid=(S // R,),
        out_shape=[_sds((S, D), F32), _sds((S, D), BF)] + ([_sds((1, 1), F32)] if with_loss else []),
        in_specs=[rows_(D_FF), _resident((D_FF, D)), rows_(D), vec, vec] + ([rows_(D)] if with_loss else []),
        out_specs=[rows_(D), rows_(D)] + ([_const((1, 1))] if with_loss else []),
        compiler_params=_cp(1, 48),
    )(*_in_hbm(a, w, h), gate, gp, *(_in_hbm(target) if with_loss else ()))


def _ffn_out_bwd(dh, y, dsilu_u, silu, w, gate, gp, name, after=()):
    S = dh.shape[0]
    R = min(512, S)

    def body(dh_ref, y_ref, g_ref, u_ref, w_ref, gate_ref, gp_ref, dy_ref, dgu_ref, dgate_ref, dgp_ref):
        @pl.when(pl.program_id(0) == 0)
        def _():
            dgate_ref[...] = jnp.zeros_like(dgate_ref)
            dgp_ref[...] = jnp.zeros_like(dgp_ref)
        for r0 in range(0, R, CHUNK):
            rows = slice(r0, r0 + CHUNK)
            dy, dgate, dgp = _postnorm_bwd(dh_ref[rows, :], y_ref[rows, :], gate_ref[...], gp_ref[...], 0.5)
            dgate_ref[...] += dgate
            dgp_ref[...] += dgp
            dyb = dy.astype(BF)
            dy_ref[rows, :] = dyb
            for f0, tf in F_TILES:
                da = _dot_nt(dyb, w_ref[f0:f0 + tf, :])
                dgu_ref[rows, f0:f0 + tf] = (da * g_ref[rows, f0:f0 + tf].astype(F32)).astype(BF)
                dgu_ref[rows, D_FF + f0:D_FF + f0 + tf] = (da * u_ref[rows, f0:f0 + tf].astype(F32)).astype(BF)

    vec = _const((1, D))
    rows_ = lambda w_: pl.BlockSpec((R, w_), lambda i: (i, 0))
    body, after_specs = _behind(body, 7, after)
    return pl.pallas_call(
        body, name=name, grid=(S // R,),
        out_shape=[_sds((S, D), BF), _sds((S, 2 * D_FF), BF), _sds((1, D), F32), _sds((1, D), F32)],
        in_specs=[rows_(D), rows_(D), rows_(D_FF), rows_(D_FF), _resident((D_FF, D)), vec, vec] + after_specs,
        out_specs=[rows_(D), rows_(2 * D_FF), vec, vec],
        compiler_params=_cp(1, 56),
    )(*_in_hbm(dh, y, dsilu_u, silu, w), gate, gp, *after)


def _ffn_dn(dgu, wt, h, dh, sc, gp, name, after=()):
    S = h.shape[0]
    R = min(512, S)

    def body(dgu_ref, w_ref, h_ref, dh_ref, sc_ref, gp_ref, out_ref, dsh_ref, dsc_ref, dgp_ref):
        @pl.when(pl.program_id(0) == 0)
        def _():
            dsh_ref[...] = jnp.zeros_like(dsh_ref)
            dsc_ref[...] = jnp.zeros_like(dsc_ref)
            dgp_ref[...] = jnp.zeros_like(dgp_ref)

        for r0 in range(0, R, CHUNK):
            rows = slice(r0, r0 + CHUNK)
            dn = _dot(dgu_ref[rows, :], w_ref[...])
            dx, dsh, dsc, dgp = _prenorm_bwd(dn, h_ref[rows, :], gp_ref[...], sc_ref[...])
            out_ref[rows, :] = dh_ref[rows, :] + dx
            dsh_ref[...] += dsh
            dsc_ref[...] += dsc
            dgp_ref[...] += dgp

    vec = _const((1, D))
    rows_ = lambda w_: pl.BlockSpec((R, w_), lambda i: (i, 0))
    body, after_specs = _behind(body, 6, after)
    return pl.pallas_call(
        body, name=name, grid=(S // R,),
        out_shape=[_sds((S, D), F32)] + [_sds((1, D), F32)] * 3,
        in_specs=[rows_(2 * D_FF), _resident((2 * D_FF, D)), rows_(D), rows_(D), vec, vec] + after_specs,
        out_specs=[rows_(D), vec, vec, vec],
        compiler_params=_cp(1, 56),
    )(*_in_hbm(dgu, wt, h, dh), sc, gp, *after)


def _ffn_bwd(dh, y, dsilu_u, silu, w, wt, h, gate, gpost, sc, gpre, name):
    S = dh.shape[0]
    R = min(256, S)

    def body(dh_ref, y_ref, g_ref, u_ref, w_ref, wt_ref, h_ref, gate_ref, gpost_ref, sc_ref, gpre_ref,
             dy_ref, dgu_ref, out_ref, dgate_ref, dgpost_ref, dsh_ref, dsc_ref, dgpre_ref):
        @pl.when(pl.program_id(0) == 0)
        def _():
            for r in (dgate_ref, dgpost_ref, dsh_ref, dsc_ref, dgpre_ref):
                r[...] = jnp.zeros_like(r)
        dhh = dh_ref[...]
        dy, dgate, dgpost = _postnorm_bwd(dhh, y_ref[...], gate_ref[...], gpost_ref[...], 0.5)
        dgate_ref[...] += dgate
        dgpost_ref[...] += dgpost
        dyb = dy.astype(BF)
        dy_ref[...] = dyb
        for f0, tf in F_TILES:
            da = _dot_nt(dyb, w_ref[f0:f0 + tf, :])
            dgu_ref[:, f0:f0 + tf] = (da * g_ref[:, f0:f0 + tf].astype(F32)).astype(BF)
            dgu_ref[:, D_FF + f0:D_FF + f0 + tf] = (da * u_ref[:, f0:f0 + tf].astype(F32)).astype(BF)
        dn = _dot(dgu_ref[...], wt_ref[...])
        dx, dsh, dsc, dgpre = _prenorm_bwd(dn, h_ref[...], gpre_ref[...], sc_ref[...])
        out_ref[...] = dhh + dx
        dsh_ref[...] += dsh
        dsc_ref[...] += dsc
        dgpre_ref[...] += dgpre

    vec = _const((1, D))
    rows_ = lambda w_: pl.BlockSpec((R, w_), lambda i: (i, 0))
    return pl.pallas_call(
        body, name=name, grid=(S // R,),
        out_shape=[_sds((S, D), BF), _sds((S, 2 * D_FF), BF), _sds((S, D), F32)] + [_sds((1, D), F32)] * 5,
        in_specs=[rows_(D), rows_(D), rows_(D_FF), rows_(D_FF), _resident((D_FF, D)), _resident((2 * D_FF, D)),
                  rows_(D), vec, vec, vec, vec],
        out_specs=[rows_(D), rows_(2 * D_FF), rows_(D)] + [vec] * 5,
        compiler_params=_cp(1, 56),
    )(*_in_hbm(dh, y, dsilu_u, silu, w, wt, h), gate, gpost, sc, gpre)


def _tn_matmul(a, b, name, tm=None):
    many = isinstance(a, (list, tuple))
    arrays = list(a) if many else [a]
    assert tm is None or not many
    n = len(arrays)
    S, N = b.shape
    widths = [x.shape[1] if tm is None else tm for x in arrays]
    GA = arrays[0].shape[1] // widths[0]
    ts = min(2048 if sum(widths) * N <= 2 * D * D else 1024, S // 2)
    nk = S // ts
    assert nk >= 2 and nk * ts == S

    def body(*refs):
        a_refs, b_ref, o_refs, accs = refs[:n], refs[n], refs[n + 1:2 * n + 1], refs[2 * n + 1:]
        k = pl.program_id(1)
        parts = [(a_ref, o_ref, acc, m0, min(CHUNK, w - m0))
                 for a_ref, o_ref, acc, w in zip(a_refs, o_refs, accs, widths) for m0 in range(0, w, CHUNK)]

        @pl.when(k == 0)
        def _():
            for a_ref, _, acc, m0, mc in parts:
                acc[m0:m0 + mc, :] = _dot_tn(a_ref[:, m0:m0 + mc], b_ref[...])

        @pl.when(jnp.logical_and(k > 0, k < nk - 1))
        def _():
            for a_ref, _, acc, m0, mc in parts:
                acc[m0:m0 + mc, :] += _dot_tn(a_ref[:, m0:m0 + mc], b_ref[...])

        @pl.when(k == nk - 1)
        def _():
            for a_ref, o_ref, acc, m0, mc in parts:
                o_ref[m0:m0 + mc, :] = (acc[m0:m0 + mc, :] + _dot_tn(a_ref[:, m0:m0 + mc], b_ref[...])).astype(BF)

    res = pl.pallas_call(
        body, name=name, grid=(GA, nk),
        out_shape=[_sds((x.shape[1], N), BF) for x in arrays],
        in_specs=[pl.BlockSpec((ts, w), lambda ga, k: (k, ga)) for w in widths]
                 + [pl.BlockSpec((ts, N), lambda ga, k: (k, 0))],
        out_specs=[pl.BlockSpec((w, N), lambda ga, k: (ga, 0)) for w in widths],
        scratch_shapes=[pltpu.VMEM((w, N), F32) for w in widths],
        compiler_params=_cp(2, 56),
    )(*_in_hbm(*arrays, b))
    return list(res) if many else res[0]


def _mix_in(h, sh, sc, gp, w, wq):
    S = h.shape[0]
    R = min(512, S)

    def body(h_ref, sh_ref, sc_ref, gp_ref, w_ref, wq_ref, n_ref, qkv_ref, zg_ref, gates_ref):
        for r0 in range(0, R, CHUNK):
            rows = slice(r0, r0 + CHUNK)
            nb = _prenorm(h_ref[rows, :], gp_ref[...], sc_ref[...], sh_ref[...]).astype(BF)
            n_ref[rows, :] = nb
            qkv_ref[rows, 0:Q_W] = _dot_nt(nb, wq_ref[...]).astype(BF)
            qkv_ref[rows, Q_W:QKV_W] = _dot_nt(nb, w_ref[Q_W:QKV_W, :]).astype(BF)
            zg_ref[rows, :] = _dot_nt(nb, w_ref[ZG_OFF:GATE_OFF, :]).astype(BF)
            gates_ref[rows, :] = jax.nn.sigmoid(_dot_nt(nb, w_ref[GATE_OFF:IN_W, :])).astype(BF)

    vec = _const((1, D))
    rows = lambda w_: pl.BlockSpec((R, w_), lambda i: (i, 0))
    return pl.pallas_call(
        body, name="mix_in", grid=(S // R,),
        out_shape=[_sds((S, D), BF), _sds((S, QKV_W), BF), _sds((S, 2 * G_W), BF), _sds((S, 2 * D), BF)],
        in_specs=[rows(D), vec, vec, vec, _resident((IN_W, D)), _resident((Q_W, D))],
        out_specs=[rows(D), rows(QKV_W), rows(2 * G_W), rows(2 * D)],
        compiler_params=_cp(1, 48),
    )(*_in_hbm(h), sh, sc, gp, *_in_hbm(w, wq))


def _bias_table(rel_bias, bucket):
    def body(rel_ref, bk_ref, out_ref):
        bk = bk_ref[...]
        qi = lax.broadcasted_iota(jnp.int32, (BLK, 2 * BLK), 0)
        kj = lax.broadcasted_iota(jnp.int32, (BLK, 2 * BLK), 1)
        dist = qi + BLK - kj
        window = (dist >= 0) & (dist < BLK)
        for h in range(N_HEADS):
            acc = jnp.zeros((BLK, 2 * BLK), F32)
            for b in range(N_BUCKETS):
                acc = jnp.where(bk == b, rel_ref[b, h], acc)
            out_ref[h // GROUP, pl.ds((h % GROUP) * BLK, BLK), :] = jnp.where(window, acc, NEG)

    return pl.pallas_call(
        body, name="bias_table",
        out_shape=_sds((N_KV, GROUP * BLK, 2 * BLK), F32),
        in_specs=[pl.BlockSpec(memory_space=pltpu.SMEM), pl.BlockSpec(memory_space=pltpu.VMEM)],
        out_specs=pl.BlockSpec(memory_space=pltpu.VMEM),
    )(rel_bias, bucket)


ATT_TB = 8


HEAD_ROWS = N_HEADS * BLK


def _pair_heads(w):
    return jnp.transpose(w.reshape(N_KV, GROUP, HD, w.shape[1]), (1, 0, 2, 3)).reshape(w.shape)


def _unpair_heads(w):
    return jnp.transpose(w.reshape(GROUP, N_KV, HD, w.shape[1]), (1, 0, 2, 3)).reshape(w.shape)


def _halves(x, scale=1.0):
    low = lax.broadcasted_iota(jnp.int32, x.shape, 1) < HD
    xf = x.astype(F32) * scale
    return jnp.where(low, xf, 0.0).astype(BF), jnp.where(low, 0.0, xf).astype(BF)


def _stack_heads(x, scale=1.0):
    halves = [_halves(x[:, g * 128:(g + 1) * 128], scale) for g in range(GROUP)]
    return jnp.concatenate([lo for lo, _ in halves] + [hi for _, hi in halves], axis=0)


def _attn_probs(q, kvc, kvp, bias_ref, sink_ref, has_prev):
    kv2 = jnp.concatenate([kvp, kvc], axis=0)
    kboth, vboth = kv2[:, 0:KV_W], kv2[:, KV_W:2 * KV_W]
    qpad = _stack_heads(q, SCALE)
    s = _dot_nt(qpad, kboth) + bias_ref[...]
    if has_prev is not None:
        col = lax.broadcasted_iota(jnp.int32, (HEAD_ROWS, 2 * BLK), 1)
        s = jnp.where((col >= BLK) | has_prev, s, NEG)
    row_head = lax.broadcasted_iota(jnp.int32, (HEAD_ROWS, 1), 0) // BLK
    sink = jnp.zeros((HEAD_ROWS, 1), F32)
    for h in range(N_HEADS):
        sink = jnp.where(row_head == h, sink_ref[h], sink)
    m = jnp.maximum(jnp.max(s, axis=1, keepdims=True), sink)
    p = jnp.exp(s - m)
    e_sink = jnp.exp(sink - m)
    inv = 1.0 / (jnp.sum(p, axis=1, keepdims=True) + e_sink)
    return qpad, kboth, vboth, p * inv, e_sink * inv


def _attn_fwd(qkv, bias, sinks):
    S = qkv.shape[0]
    tb = min(ATT_TB, S // BLK)
    T = tb * BLK

    def body(sink_ref, q_ref, kv_ref, kvp_ref, bias_ref, o_ref):
        step = pl.program_id(0)
        for j in range(tb):
            rows = slice(j * BLK, (j + 1) * BLK)
            kvp = kvp_ref[...] if j == 0 else kv_ref[(j - 1) * BLK:j * BLK, :]
            has_prev = (step > 0) if j == 0 else None
            _, _, vboth, prob, _ = _attn_probs(q_ref[rows, :], kv_ref[rows, :], kvp, bias_ref, sink_ref, has_prev)
            pb = prob.astype(BF)
            v_low, v_high = _halves(vboth)
            half = HEAD_ROWS // 2
            o = _dot(pb[0:half], v_low) + _dot(pb[half:HEAD_ROWS], v_high)
            for g in range(GROUP):
                o_ref[rows, g * 128:(g + 1) * 128] = o[g * BLK:(g + 1) * BLK].astype(BF)

    return pl.pallas_call(
        body, name="attn_fwd", grid=(S // T,),
        out_shape=_sds((S, Q_W), BF),
        in_specs=[pl.BlockSpec(memory_space=pltpu.SMEM),
                  pl.BlockSpec((T, Q_W), lambda i: (i, 0)),
                  pl.BlockSpec((T, 2 * KV_W), lambda i: (i, 2)),
                  pl.BlockSpec((BLK, 2 * KV_W), lambda i: (jnp.maximum(i * tb - 1, 0), 2)),
                  _const((HEAD_ROWS, 2 * BLK))],
        out_specs=pl.BlockSpec((T, Q_W), lambda i: (i, 0)),
        compiler_params=_cp(1, 32),
    )(sinks, *_in_hbm(qkv, qkv, qkv, bias))


def _attn_bwd(qkv, bias, sinks, do):
    S = qkv.shape[0]
    tb = min(ATT_TB, S // BLK)
    T = tb * BLK
    nt = S // T
    half = HEAD_ROWS // 2

    def body(sink_ref, q_ref, kv_ref, kvp_ref, bias_ref, do_ref, dq_ref, dkv_ref, dbias_ref, dsink_ref, carry):
        i = pl.program_id(0)

        @pl.when(i == 0)
        def _():
            carry[...] = jnp.zeros_like(carry)
            dbias_ref[...] = jnp.zeros_like(dbias_ref)
            dsink_ref[...] = jnp.zeros_like(dsink_ref)

        from_next = carry[...]
        head_row = lax.broadcasted_iota(jnp.int32, (N_HEADS, 128), 0)
        low = lax.broadcasted_iota(jnp.int32, (BLK, 128), 1) < HD
        for j in reversed(range(tb)):
            rows = slice(j * BLK, (j + 1) * BLK)
            kvp = kvp_ref[...] if j == 0 else kv_ref[(j - 1) * BLK:j * BLK, :]
            has_prev = (i < nt - 1) if j == 0 else None
            qpad, kboth, vboth, prob, p_sink = _attn_probs(q_ref[rows, :], kv_ref[rows, :], kvp, bias_ref, sink_ref,
                                                           has_prev)
            pb = prob.astype(BF)
            dopad = _stack_heads(do_ref[rows, :])
            dp = _dot_nt(dopad, vboth)
            delta = jnp.sum(prob * dp, axis=1, keepdims=True)
            ds = prob * (dp - delta)
            dbias_ref[...] += ds
            sink_term = p_sink * delta
            dsink_rows = jnp.zeros((N_HEADS, 128), F32)
            for h in range(N_HEADS):
                val = -jnp.sum(sink_term[h * BLK:(h + 1) * BLK], axis=0, keepdims=True)
                dsink_rows = jnp.where(head_row == h, val, dsink_rows)
            dsink_ref[...] += dsink_rows
            dsb = ds.astype(BF)
            dqpad = _dot(dsb, kboth) * SCALE
            for g in range(GROUP):
                dq_ref[rows, g * 128:(g + 1) * 128] = jnp.where(
                    low, dqpad[g * BLK:(g + 1) * BLK], dqpad[half + g * BLK:half + (g + 1) * BLK]).astype(BF)
            dkv2 = jnp.concatenate([jnp.transpose(_dot_tn(qpad, dsb)),
                                    jnp.transpose(_dot_tn(dopad, pb))], axis=1)
            dkv_ref[rows, :] = (dkv2[BLK:2 * BLK] + from_next).astype(BF)
            from_next = dkv2[0:BLK]
        carry[...] = from_next

    return pl.pallas_call(
        body, name="attn_bwd", grid=(nt,),
        out_shape=[_sds((S, Q_W), BF), _sds((S, 2 * KV_W), BF),
                   _sds((HEAD_ROWS, 2 * BLK), F32), _sds((N_HEADS, 128), F32)],
        in_specs=[pl.BlockSpec(memory_space=pltpu.SMEM),
                  pl.BlockSpec((T, Q_W), lambda i: (nt - 1 - i, 0)),
                  pl.BlockSpec((T, 2 * KV_W), lambda i: (nt - 1 - i, 2)),
                  pl.BlockSpec((BLK, 2 * KV_W), lambda i: (jnp.maximum((nt - 1 - i) * tb - 1, 0), 2)),
                  _const((HEAD_ROWS, 2 * BLK)),
                  pl.BlockSpec((T, Q_W), lambda i: (nt - 1 - i, 0))],
        out_specs=[pl.BlockSpec((T, Q_W), lambda i: (nt - 1 - i, 0)),
                   pl.BlockSpec((T, 2 * KV_W), lambda i: (nt - 1 - i, 0)),
                   _const((HEAD_ROWS, 2 * BLK)), _const((N_HEADS, 128))],
        scratch_shapes=[pltpu.VMEM((BLK, 2 * KV_W), F32)],
        compiler_params=_cp(1, 32),
    )(sinks, *_in_hbm(qkv, qkv, qkv, bias, do))


def _rel_bias_grad(dbias, bucket):
    def body(db_ref, bk_ref, out_ref):
        bk = bk_ref[...]
        lane = lax.broadcasted_iota(jnp.int32, (1, 128), 1)
        for h in range(N_HEADS):
            d = db_ref[h // GROUP, pl.ds((h % GROUP) * BLK, BLK), :]
            row = jnp.zeros((1, 128), F32)
            for b in range(N_BUCKETS):
                tot = jnp.sum(jnp.sum(jnp.where(bk == b, d, 0.0), axis=1, keepdims=True), axis=0, keepdims=True)
                row = jnp.where(lane == b, tot, row)
            out_ref[pl.ds(h, 1), :] = row

    vm = pl.BlockSpec(memory_space=pltpu.VMEM)
    return pl.pallas_call(body, name="rel_bias_grad", out_shape=_sds((N_HEADS, 128), F32),
                          in_specs=[vm, vm], out_specs=vm)(dbias, bucket)


def _gmlp_parts(zg, lg_ref, lb_ref):
    z = zg.astype(F32)
    ge = _gelu(z)
    u, vg = ge[:, 0:G_W], ge[:, G_W:2 * G_W]
    mu = jnp.mean(vg, axis=-1, keepdims=True)
    xc = vg - mu
    rstd = lax.rsqrt(jnp.mean(xc * xc, axis=-1, keepdims=True) + EPS)
    xh = xc * rstd
    return z, u, xh, rstd, xh * lg_ref[...] + lb_ref[...]


def _causal_weights(ws_ref, wc):
    t = lax.broadcasted_iota(jnp.int32, (BLK, BLK), 0)
    s = lax.broadcasted_iota(jnp.int32, (BLK, BLK), 1)
    for g in range(N_HEADS):
        wc[g] = jnp.where(s <= t, ws_ref[g], 0.0).astype(BF)


def _spatial(vb, wc, bst_ref, p, low):
    xp = vb[:, p * 128:(p + 1) * 128]
    s0 = _dot(wc[2 * p], xp) + bst_ref[:, 2 * p:2 * p + 1]
    s1 = _dot(wc[2 * p + 1], xp) + bst_ref[:, 2 * p + 1:2 * p + 2]
    return xp, jnp.where(low, s0, s1)


def _gmlp_fwd(zg, lg, lb, ws, bst):
    S = zg.shape[0]
    tb = min(ATT_TB, S // BLK)
    T = tb * BLK

    def body(zg_ref, lg_ref, lb_ref, ws_ref, bst_ref, o_ref, wc):
        @pl.when(pl.program_id(0) == 0)
        def _():
            _causal_weights(ws_ref, wc)
        low = lax.broadcasted_iota(jnp.int32, (BLK, 128), 1) < HD
        for j in range(tb):
            rows = slice(j * BLK, (j + 1) * BLK)
            _, u, _, _, vln = _gmlp_parts(zg_ref[rows, :], lg_ref, lb_ref)
            vb = vln.astype(BF)
            for p in range(4):
                _, sp = _spatial(vb, wc, bst_ref, p, low)
                o_ref[rows, p * 128:(p + 1) * 128] = (u[:, p * 128:(p + 1) * 128] * sp).astype(BF)

    return pl.pallas_call(
        body, name="gmlp_fwd", grid=(S // T,),
        out_shape=_sds((S, G_W), BF),
        in_specs=[pl.BlockSpec((T, 2 * G_W), lambda i: (i, 0)), _const((1, G_W)), _const((1, G_W)),
                  _const((N_HEADS, BLK, BLK)), _const((BLK, N_HEADS))],
        out_specs=pl.BlockSpec((T, G_W), lambda i: (i, 0)),
        scratch_shapes=[pltpu.VMEM((N_HEADS, BLK, BLK), BF)],
        compiler_params=_cp(1, 32),
    )(*_in_hbm(zg), lg, lb, ws, bst)


def _gmlp_bwd(zg, d_out, lg, lb, ws, bst):
    S = zg.shape[0]
    tb = min(ATT_TB, S // BLK)
    T = tb * BLK
    nb = S // T

    def body(zg_ref, d_ref, lg_ref, lb_ref, ws_ref, bst_ref, dzg_ref, dws_ref, dbs_ref, dlg_ref, dlb_ref, wc, dbacc):
        i = pl.program_id(0)

        @pl.when(i == 0)
        def _():
            _causal_weights(ws_ref, wc)
            dws_ref[...] = jnp.zeros_like(dws_ref)
            dlg_ref[...] = jnp.zeros_like(dlg_ref)
            dlb_ref[...] = jnp.zeros_like(dlb_ref)
            dbacc[...] = jnp.zeros_like(dbacc)

        low = lax.broadcasted_iota(jnp.int32, (BLK, 128), 1) < HD
        for j in range(tb):
            rows = slice(j * BLK, (j + 1) * BLK)
            z, u, xh, rstd, vln = _gmlp_parts(zg_ref[rows, :], lg_ref, lb_ref)
            vb = vln.astype(BF)
            d = d_ref[rows, :].astype(F32)
            du_parts, dvln_parts = [], []
            for p in range(4):
                xp, sp = _spatial(vb, wc, bst_ref, p, low)
                dp = d[:, p * 128:(p + 1) * 128]
                du_parts.append(dp * sp)
                dsp = dp * u[:, p * 128:(p + 1) * 128]
                dbacc[:, p * 128:(p + 1) * 128] += dsp
                d0 = jnp.where(low, dsp, 0.0).astype(BF)
                d1 = jnp.where(low, 0.0, dsp).astype(BF)
                dws_ref[2 * p] += _dot_nt(d0, xp)
                dws_ref[2 * p + 1] += _dot_nt(d1, xp)
                dvln_parts.append(_dot_tn(wc[2 * p], d0) + _dot_tn(wc[2 * p + 1], d1))
            dvln = jnp.concatenate(dvln_parts, axis=1)
            dlg_ref[...] += _colsum(dvln * xh)
            dlb_ref[...] += _colsum(dvln)
            dxh = dvln * lg_ref[...]
            dvg = rstd * (dxh - jnp.mean(dxh, axis=-1, keepdims=True)
                          - xh * jnp.mean(dxh * xh, axis=-1, keepdims=True))
            dge = jnp.concatenate(du_parts + [dvg], axis=1)
            dzg_ref[rows, :] = (dge * _gelu_grad(z)).astype(BF)

        @pl.when(i == nb - 1)
        def _():
            t = lax.broadcasted_iota(jnp.int32, (BLK, BLK), 0)
            s = lax.broadcasted_iota(jnp.int32, (BLK, BLK), 1)
            for g in range(N_HEADS):
                dws_ref[g] = jnp.where(s <= t, dws_ref[g], 0.0)
            grp = lax.broadcasted_iota(jnp.int32, (N_HEADS, G_W), 0)
            lane = lax.broadcasted_iota(jnp.int32, (N_HEADS, G_W), 1) // HD
            pick = jnp.where(grp == lane, 1.0, 0.0).astype(F32)
            dbs_ref[...] = lax.dot_general(pick, dbacc[...], (((1,), (1,)), ((), ())),
                                           preferred_element_type=F32, precision=HIGH)

    return pl.pallas_call(
        body, name="gmlp_bwd", grid=(nb,),
        out_shape=[_sds((S, 2 * G_W), BF), _sds((N_HEADS, BLK, BLK), F32), _sds((N_HEADS, BLK), F32),
                   _sds((1, G_W), F32), _sds((1, G_W), F32)],
        in_specs=[pl.BlockSpec((T, 2 * G_W), lambda i: (i, 0)), pl.BlockSpec((T, G_W), lambda i: (i, 0)),
                  _const((1, G_W)), _const((1, G_W)), _const((N_HEADS, BLK, BLK)), _const((BLK, N_HEADS))],
        out_specs=[pl.BlockSpec((T, 2 * G_W), lambda i: (i, 0)), _const((N_HEADS, BLK, BLK)),
                   _const((N_HEADS, BLK)), _const((1, G_W)), _const((1, G_W))],
        scratch_shapes=[pltpu.VMEM((N_HEADS, BLK, BLK), BF), pltpu.VMEM((BLK, G_W), F32)],
        compiler_params=_cp(1, 32),
    )(*_in_hbm(zg, d_out), lg, lb, ws, bst)


def _mix_out(o, gm, gates, h, wa, wg, wo, gate, gp, after=()):
    S = h.shape[0]
    R = min(512, S)

    def body(o_ref, gm_ref, gates_ref, h_ref, wa_ref, wg_ref, wo_ref, gate_ref, gp_ref,
             ya_ref, yg_ref, ym_ref, y_ref, hn_ref):
        for r0 in range(0, R, CHUNK):
            rows = slice(r0, r0 + CHUNK)
            ya = _dot(o_ref[rows, :], wa_ref[...])
            yg = _dot(gm_ref[rows, :], wg_ref[...])
            ya_ref[rows, :] = ya.astype(BF)
            yg_ref[rows, :] = yg.astype(BF)
            ym = (gates_ref[rows, 0:D].astype(F32) * ya + gates_ref[rows, D:2 * D].astype(F32) * yg).astype(BF)
            ym_ref[rows, :] = ym
            y = _dot(ym, wo_ref[...])
            y_ref[rows, :] = y.astype(BF)
            hn_ref[rows, :] = h_ref[rows, :] + gate_ref[...] * (y * _rms_r(y) * gp_ref[...])

    vec = _const((1, D))
    rows = lambda w_: pl.BlockSpec((R, w_), lambda i: (i, 0))
    body, after_specs = _behind(body, 9, after)
    return pl.pallas_call(
        body, name="mix_out", grid=(S // R,),
        out_shape=[_sds((S, D), BF)] * 4 + [_sds((S, D), F32)],
        in_specs=[rows(Q_W), rows(G_W), rows(2 * D), rows(D), _resident((Q_W, D)), _resident((G_W, D)),
                  _resident((D, D)), vec, vec] + after_specs,
        out_specs=[rows(D)] * 5,
        compiler_params=_cp(1, 48),
    )(*_in_hbm(o, gm, gates, h, wa, wg, wo), gate, gp, *after)


def _mix_out_bwd(dh, y, ya, yg, gates, att, gm, ymix, wa, wg, wo, gate, gp, after=()):
    S = dh.shape[0]
    R = min(512, S)
    nb = S // R

    def body(dh_ref, y_ref, ya_ref, yg_ref, gates_ref, att_ref, gm_ref, ym_ref, wa_ref, wg_ref, wo_ref,
             gate_ref, gp_ref, dz_ref, do_ref, dgm_ref, dgate_ref, dgp_ref, gwo_ref, gwa_ref, gwg_ref,
             acc_o, acc_a, acc_g, dy_scr, dya_scr, dyg_scr):
        i = pl.program_id(0)

        @pl.when(i == 0)
        def _():
            for r in (dgate_ref, dgp_ref, acc_o, acc_a, acc_g):
                r[...] = jnp.zeros_like(r)
        for r0 in range(0, R, 2 * CHUNK):
            rows = slice(r0, min(r0 + 2 * CHUNK, R))
            dy, dgate, dgp = _postnorm_bwd(dh_ref[rows, :], y_ref[rows, :], gate_ref[...], gp_ref[...], 1.0)
            dgate_ref[...] += dgate
            dgp_ref[...] += dgp
            dyb = dy.astype(BF)
            dy_scr[rows, :] = dyb
            dym = _dot_nt(dyb, wo_ref[...])
            ga = gates_ref[rows, 0:D].astype(F32)
            gg = gates_ref[rows, D:2 * D].astype(F32)
            dya = (dym * ga).astype(BF)
            dyg = (dym * gg).astype(BF)
            dya_scr[rows, :] = dya
            dyg_scr[rows, :] = dyg
            dz_ref[rows, 0:D] = (dym * ya_ref[rows, :].astype(F32) * (ga * (1.0 - ga))).astype(BF)
            dz_ref[rows, D:2 * D] = (dym * yg_ref[rows, :].astype(F32) * (gg * (1.0 - gg))).astype(BF)
            do_ref[rows, :] = _dot_nt(dya, wa_ref[...]).astype(BF)
            dgm_ref[rows, :] = _dot_nt(dyg, wg_ref[...]).astype(BF)
        for m0 in range(0, D, CHUNK):
            acc_o[m0:m0 + CHUNK, :] += _dot_tn(ym_ref[:, m0:m0 + CHUNK], dy_scr[...])
        for m0 in range(0, Q_W, CHUNK):
            acc_a[m0:m0 + CHUNK, :] += _dot_tn(att_ref[:, m0:m0 + CHUNK], dya_scr[...])
            acc_g[m0:m0 + CHUNK, :] += _dot_tn(gm_ref[:, m0:m0 + CHUNK], dyg_scr[...])

        @pl.when(i == nb - 1)
        def _():
            for m0 in range(0, D, CHUNK):
                gwo_ref[m0:m0 + CHUNK, :] = acc_o[m0:m0 + CHUNK, :].astype(BF)
            for m0 in range(0, Q_W, CHUNK):
                gwa_ref[m0:m0 + CHUNK, :] = acc_a[m0:m0 + CHUNK, :].astype(BF)
                gwg_ref[m0:m0 + CHUNK, :] = acc_g[m0:m0 + CHUNK, :].astype(BF)

    vec = _const((1, D))
    rows = lambda w_: pl.BlockSpec((R, w_), lambda i: (i, 0))
    body, after_specs = _behind(body, 13, after)
    return pl.pallas_call(
        body, name="mix_out_bwd", grid=(nb,),
        out_shape=[_sds((S, 2 * D), BF), _sds((S, Q_W), BF), _sds((S, G_W), BF), _sds((1, D), F32),
                   _sds((1, D), F32), _sds((D, D), BF), _sds((Q_W, D), BF), _sds((G_W, D), BF)],
        in_specs=[rows(D), rows(D), rows(D), rows(D), rows(2 * D), rows(Q_W), rows(G_W), rows(D),
                  _resident((Q_W, D)), _resident((G_W, D)), _resident((D, D)), vec, vec] + after_specs,
        out_specs=[rows(2 * D), rows(Q_W), rows(G_W), vec, vec, _const((D, D)), _const((Q_W, D)),
                   _const((G_W, D))],
        scratch_shapes=[pltpu.VMEM((D, D), F32), pltpu.VMEM((Q_W, D), F32), pltpu.VMEM((G_W, D), F32)]
        + [pltpu.VMEM((R, D), BF)] * 3,
        compiler_params=_cp(1, 60),
    )(*_in_hbm(dh, y, ya, yg, gates, att, gm, ymix, wa, wg, wo), gate, gp, *after)


def _mix_dn(dq, dkv, dzg, dzgate, w, wq, h, dh, sc, gp, after=()):
    S = h.shape[0]
    R = min(512, S)

    def body(dq_ref, dkv_ref, dzg_ref, dzt_ref, w_ref, wq_ref, h_ref, dh_ref, sc_ref, gp_ref,
             out_ref, dsh_ref, dsc_ref, dgp_ref):
        @pl.when(pl.program_id(0) == 0)
        def _():
            dsh_ref[...] = jnp.zeros_like(dsh_ref)
            dsc_ref[...] = jnp.zeros_like(dsc_ref)
            dgp_ref[...] = jnp.zeros_like(dgp_ref)
        for r0 in range(0, R, CHUNK):
            rows = slice(r0, r0 + CHUNK)
            dn = _dot(dq_ref[rows, :], wq_ref[...])
            dn = dn + _dot(dkv_ref[rows, :], w_ref[Q_W:QKV_W, :])
            dn = dn + _dot(dzg_ref[rows, :], w_ref[ZG_OFF:GATE_OFF, :])
            dn = dn + _dot(dzt_ref[rows, :], w_ref[GATE_OFF:IN_W, :])
            dx, dsh, dsc, dgp = _prenorm_bwd(dn, h_ref[rows, :], gp_ref[...], sc_ref[...])
            out_ref[rows, :] = dh_ref[rows, :] + dx
            dsh_ref[...] += dsh
            dsc_ref[...] += dsc
            dgp_ref[...] += dgp

    vec = _const((1, D))
    rows = lambda w_: pl.BlockSpec((R, w_), lambda i: (i, 0))
    body, after_specs = _behind(body, 10, after)
    return pl.pallas_call(
        body, name="mix_dn", grid=(S // R,),
        out_shape=[_sds((S, D), F32)] + [_sds((1, D), F32)] * 3,
        in_specs=[rows(Q_W), rows(2 * KV_W), rows(2 * G_W), rows(2 * D), _resident((IN_W, D)),
                  _resident((Q_W, D)), rows(D), rows(D), vec, vec] + after_specs,
        out_specs=[rows(D), vec, vec, vec],
        compiler_params=_cp(1, 48),
    )(*_in_hbm(dq, dkv, dzg, dzgate, w, wq, h, dh), sc, gp, *after)


def _adamw_math(w, g, m, v):
    m2 = ADAM_B1 * m + (1.0 - ADAM_B1) * g
    v2 = ADAM_B2 * v + (1.0 - ADAM_B2) * (g * g)
    m_hat = m2 / (1.0 - ADAM_B1 ** ADAM_STEP)
    v_hat = v2 / (1.0 - ADAM_B2 ** ADAM_STEP)
    delta = -ADAM_LR * (m_hat / (jnp.sqrt(v_hat) + ADAM_EPS) + ADAM_WD * w)
    return delta, m2, v2


def _row_tile(rows, cols):
    best = None
    for t in range(16, rows + 1, 16):
        if rows % t == 0 and t * cols <= 256 * 1024:
            best = t
    return best if best is not None else rows


def _adamw_sharded(landing, w, m, v, name):
    r, c = w.shape
    tr = _row_tile(r, c)

    def body(l_ref, w_ref, m_ref, v_ref, g_ref, d_ref, m2_ref, v2_ref):
        g = l_ref[0].astype(F32)
        for j in range(1, N_DEV):
            g = g + l_ref[j].astype(F32)
        delta, m2, v2 = _adamw_math(w_ref[...], g, m_ref[...], v_ref[...])
        g_ref[...] = g
        d_ref[...] = delta
        m2_ref[...] = m2
        v2_ref[...] = v2

    row = pl.BlockSpec((tr, c), lambda i: (i, 0))
    return pl.pallas_call(
        body, name=name, grid=(r // tr,),
        out_shape=[_sds((r, c), F32)] * 4,
        in_specs=[pl.BlockSpec((N_DEV, tr, c), lambda i: (0, i, 0)), row, row, row],
        out_specs=[row] * 4,
        compiler_params=_cp(1, 48),
    )(*_in_hbm(landing, w, m, v))


def _adamw_small(items):
    n = len(items)

    def body(*refs):
        for k in range(n):
            w_ref, g_ref, m_ref, v_ref = refs[4 * k:4 * k + 4]
            outs = refs[4 * n + 3 * k:4 * n + 3 * k + 3]
            for o_ref, val in zip(outs, _adamw_math(w_ref[...], g_ref[...], m_ref[...], v_ref[...])):
                o_ref[...] = val

    vm = pl.BlockSpec(memory_space=pltpu.VMEM)
    flat = pl.pallas_call(
        body, name="adamw_small",
        out_shape=[_sds(it[0].shape, F32) for it in items for _ in range(3)],
        in_specs=[vm] * (4 * n), out_specs=[vm] * (3 * n),
    )(*[a for it in items for a in it])
    return [tuple(flat[3 * k:3 * k + 3]) for k in range(n)]


def _w_ada_update(c8, d_ada, w, m, v):
    tr = 256

    def body(c_ref, d_ref, w_ref, m_ref, v_ref, g_ref, dl_ref, m2_ref, v2_ref):
        cs = c_ref[...]
        cs = cs * jax.nn.sigmoid(cs)
        g = lax.dot_general(cs, d_ref[...], (((0,), (0,)), ((), ())), preferred_element_type=F32, precision=HIGH)
        delta, m2, v2 = _adamw_math(w_ref[...], g, m_ref[...], v_ref[...])
        g_ref[...] = g
        dl_ref[...] = delta
        m2_ref[...] = m2
        v2_ref[...] = v2

    row = pl.BlockSpec((tr, ADA_W), lambda i: (i, 0))
    return pl.pallas_call(
        body, name="w_ada_update", grid=(D // tr,),
        out_shape=[_sds((D, ADA_W), F32)] * 4,
        in_specs=[pl.BlockSpec((N_DEV, tr), lambda i: (0, i)), _const((N_DEV, ADA_W)), row, row, row],
        out_specs=[row] * 4,
        compiler_params=_cp(1, 40),
    )(c8, d_ada, *_in_hbm(w, m, v))


def _t5_bucket():
    qi = np.arange(BLK, dtype=np.int32)[:, None]
    kj = np.arange(2 * BLK, dtype=np.int32)[None, :]
    dist = np.maximum(qi + BLK - kj, 0)
    max_exact = N_BUCKETS // 2
    d_f = np.maximum(dist, max_exact).astype(np.float32)
    large = max_exact + (np.log(d_f / np.float32(max_exact)) / np.float32(math.log(MAX_DISTANCE / max_exact))
                         * np.float32(N_BUCKETS - max_exact)).astype(np.int32)
    large = np.minimum(large, N_BUCKETS - 1)
    return jnp.asarray(np.where(dist < max_exact, dist, large).astype(np.int32))


def _slabs_of_columns(w):
    r, c8 = w.shape
    return jnp.transpose(w.reshape(r, N_DEV, c8 // N_DEV), (1, 0, 2))


def _columns_of_slabs(w8):
    _, r, c = w8.shape
    return jnp.transpose(w8, (1, 0, 2)).reshape(r, N_DEV * c)


def kernel(x, c, rel_bias, w_ada, b_ada, pre_norm_g, post_norm_g, w_ffn1_in, w_ffn1_out, w_in, sinks, gmlp_ln_g, gmlp_ln_b, gmlp_w_s, gmlp_b_s, w_br_attn, w_br_gmlp, w_out, w_ffn2_in, w_ffn2_out, loss_target, m_rel_bias, m_w_ada, m_b_ada, m_pre_norm_g, m_post_norm_g, m_w_ffn1_in, m_w_ffn1_out, m_w_in, m_sinks, m_gmlp_ln_g, m_gmlp_ln_b, m_gmlp_w_s, m_gmlp_b_s, m_w_br_attn, m_w_br_gmlp, m_w_out, m_w_ffn2_in, m_w_ffn2_out, v_rel_bias, v_w_ada, v_b_ada, v_pre_norm_g, v_post_norm_g, v_w_ffn1_in, v_w_ffn1_out, v_w_in, v_sinks, v_gmlp_ln_g, v_gmlp_ln_b, v_gmlp_w_s, v_gmlp_b_s, v_w_br_attn, v_w_br_gmlp, v_w_out, v_w_ffn2_in, v_w_ffn2_out):
    me = 4 * lax.axis_index("x") + 2 * lax.axis_index("y") + lax.axis_index("c")
    x0 = x[0]
    target = loss_target[0]

    transposed = ("w_ffn1_in", "w_in", "w_ffn2_in")
    shards = [w_ffn1_in[0].T, w_ffn1_out[0], w_in[0].T, w_br_attn[0], w_br_gmlp[0], w_out[0],
              w_ffn2_in[0].T, w_ffn2_out[0]]
    shards_bf = [s.astype(BF) for s in shards]
    groups = [shards_bf[0:1], shards_bf[1:6], shards_bf[6:8]]

    def gather_start(i, after):
        return _slabs_start("gather", groups[i], after, "gather_start_%d" % i)

    def forward_start(st, i, after):
        lands = _slabs_wait("gather", len(groups[i]), st, after, "gather_wait_%d" % i)
        return _slabs_start("forward", lands, c, "forward_start_%d" % i)

    def gathered(st, i, after):
        return _slabs_wait("forward", len(groups[i]), st, after, "forward_wait_%d" % i)

    gs0 = gather_start(0, c)

    mine = jnp.concatenate([c[0], pre_norm_g[0].reshape(-1), post_norm_g[0].reshape(-1)])
    small8 = jnp.broadcast_to(mine[None, :], (8, mine.shape[0]))
    b_ada64 = jnp.repeat(b_ada.reshape(N_DEV, ADA_W), 8, axis=0)
    gath, ada64 = _ada_forward(small8, w_ada[0], b_ada64)
    gath8 = gath[::8]
    ada = ada64[::8].reshape(9, D)
    sh1, sc1, g1, sh2, sc2, g2, sh3, sc3, g3 = [ada[k:k + 1] for k in range(9)]
    gains = gath8[:, D:].reshape(N_DEV, 2, 3, 128)
    pre_g = jnp.transpose(gains[:, 0], (1, 0, 2)).reshape(3, D)
    post_g = jnp.transpose(gains[:, 1], (1, 0, 2)).reshape(3, D)
    pre = [pre_g[k:k + 1] for k in range(3)]
    post = [post_g[k:k + 1] for k in range(3)]

    bucket = _t5_bucket()
    bias = _bias_table(rel_bias, bucket).reshape(HEAD_ROWS, 2 * BLK)
    sinks8 = sinks[0]
    lg, lb = gmlp_ln_g, gmlp_ln_b
    ws = gmlp_w_s[0]
    bst = jnp.transpose(gmlp_b_s[0])

    fs0 = forward_start(gs0, 0, sh1)
    gs1 = gather_start(1, fs0[-1])
    wf1_in = gathered(fs0, 0, gs1[-1])[0].reshape(2 * D_FF, D)
    n1, fg1, fu1, fa1 = _ffn_in(x0, sh1, sc1, pre[0], wf1_in, "ffn1_in")
    fs1 = forward_start(gs1, 1, n1)
    gs2 = gather_start(2, fs1[-1])
    mix_w = gathered(fs1, 1, gs2[-1])
    wf1_out = mix_w[0].reshape(D_FF, D)
    w_in_full = mix_w[1].reshape(IN_W, D)
    w_q = _pair_heads(w_in_full[0:Q_W])
    w_bra = _pair_heads(_columns_of_slabs(mix_w[2]))
    w_brg = _columns_of_slabs(mix_w[3])
    w_out_full = mix_w[4].reshape(D, D)
    h1, y1 = _ffn_out(fa1, wf1_out, x0, g1, post[0], "ffn1_out")
    n2, qkv, zg, gates = _mix_in(h1, sh2, sc2, pre[1], w_in_full, w_q)
    att = _attn_fwd(qkv, bias, sinks8)
    gm = _gmlp_fwd(zg, lg, lb, ws, bst)
    fs2 = forward_start(gs2, 2, gm)
    ya, yg, ymix, y2, h2 = _mix_out(att, gm, gates, h1, w_bra, w_brg, w_out_full, g2, post[1], after=(fs2[-1],))
    wf2_in, wf2_out = gathered(fs2, 2, h2)
    wf2_in = wf2_in.reshape(2 * D_FF, D)
    wf2_out = wf2_out.reshape(D_FF, D)
    n3, fg3, fu3, fa3 = _ffn_in(h2, sh3, sc3, pre[2], wf2_in, "ffn2_in")
    dh3, y3, sq = _ffn_out(fa3, wf2_out, h2, g3, post[2], "ffn2_out", target=target)

    def exchange_start(i, arrays):
        return _slabs_start("exchange", arrays, sq, "exchange_start_%d" % i)

    dy3, dgu3, dh2, d_g3, d_post2, d_sh3, d_sc3, d_pre2 = _ffn_bwd(
        dh3, y3, fg3, fu3, wf2_out, wf2_in, h2, g3, post[2], sc3, pre[2], "ffn2_bwd")
    gw_f2_out = _tn_matmul(fa3, dy3, "ffn2_out_wgrad", tm=D_FF // 2).reshape(N_DEV, D_FF // N_DEV, D)
    gw_f2_in = _tn_matmul(dgu3, n3, "ffn2_in_wgrad", tm=D_FF // 2).reshape(N_DEV, FS, D)
    ex1 = exchange_start(1, [gw_f2_out, gw_f2_in])

    dzgate, d_att, d_gm, d_g2, d_post1, gw_out, gw_bra, gw_brg = _mix_out_bwd(
        dh2, y2, ya, yg, gates, att, gm, ymix, w_bra, w_brg, w_out_full, g2, post[1], after=(ex1[-1],))
    ex2 = exchange_start(2, [_slabs_of_columns(_unpair_heads(gw_bra)), _slabs_of_columns(gw_brg),
                             gw_out.reshape(N_DEV, D // N_DEV, D)])
    dq, dkv, dbias, dsink = _attn_bwd(qkv, bias, sinks8, d_att)
    dzg, d_ws, d_bs, d_lg, d_lb = _gmlp_bwd(zg, d_gm, lg, lb, ws, bst)
    d_rel = _rel_bias_grad(dbias.reshape(N_KV, GROUP * BLK, 2 * BLK), bucket)
    early = jnp.concatenate([
        jnp.concatenate([d_lg.reshape(4, 128), d_lb.reshape(4, 128)], axis=0),
        d_bs, d_rel, dsink, d_ws.reshape(N_HEADS * BLK, BLK)], axis=0)
    sm0 = _slabs_start("gather_all", [early], sq, "small_gather_start")
    dh1, d_sh2, d_sc2, d_pre1 = _mix_dn(dq, dkv, dzg, dzgate, w_in_full, w_q, h1, dh2, sc2, pre[1],
                                        after=(ex2[-1], sm0[-1]))
    gw_q, gw_kv, gw_zg, gw_zgate = _tn_matmul([dq, dkv, dzg, dzgate], n2, "w_in_wgrad")
    gw_in = jnp.concatenate([_unpair_heads(gw_q), gw_kv, gw_zg, gw_zgate],
                            axis=0).reshape(N_DEV, IN_W // N_DEV, D)
    ex3 = exchange_start(3, [gw_in])

    dy1, dgu1, d_g1, d_post0 = _ffn_out_bwd(dh1, y1, fg1, fu1, wf1_out, g1, post[0], "ffn1_out_bwd",
                                            after=(ex3[-1],))
    gw_f1_out = _tn_matmul(fa1, dy1, "ffn1_out_wgrad", tm=D_FF // 2).reshape(N_DEV, D_FF // N_DEV, D)
    ex4 = exchange_start(4, [gw_f1_out])
    gw_f1_in = _tn_matmul(dgu1, n1, "ffn1_in_wgrad", tm=D_FF // 2).reshape(N_DEV, FS, D)
    ex5 = exchange_start(5, [gw_f1_in])
    grad_x, d_sh1, d_sc1, d_pre0 = _ffn_dn(dgu1, wf1_in, x0, dh1, sc1, pre[0], "ffn1_dn", after=(ex4[-1], ex5[-1]))

    landed = {}
    for i, (ex, nms) in enumerate([(ex1, ["w_ffn2_out", "w_ffn2_in"]),
                                   (ex2, ["w_br_attn", "w_br_gmlp", "w_out"]), (ex3, ["w_in"]),
                                   (ex4, ["w_ffn1_out"]), (ex5, ["w_ffn1_in"])]):
        for nm, land in zip(nms, _slabs_wait("exchange", len(nms), ex, grad_x, "exchange_wait_%d" % i)):
            landed[nm] = land
    moments = [(m_w_ffn1_in, v_w_ffn1_in), (m_w_ffn1_out, v_w_ffn1_out), (m_w_in, v_w_in),
               (m_w_br_attn, v_w_br_attn), (m_w_br_gmlp, v_w_br_gmlp), (m_w_out, v_w_out),
               (m_w_ffn2_in, v_w_ffn2_in), (m_w_ffn2_out, v_w_ffn2_out)]
    names = ["w_ffn1_in", "w_ffn1_out", "w_in", "w_br_attn", "w_br_gmlp", "w_out", "w_ffn2_in", "w_ffn2_out"]
    big = {}
    for nm, w_, (m_, v_) in zip(names, shards, moments):
        if nm in transposed:
            res4 = _adamw_sharded(landed[nm], w_, m_[0].T, v_[0].T, "adamw_" + nm)
            big[nm] = [a.T[None] for a in res4]
        else:
            big[nm] = [a[None] for a in _adamw_sharded(landed[nm], w_, m_[0], v_[0], "adamw_" + nm)]

    my_loss = jnp.broadcast_to(sq * (0.5 / D), (1, D))
    my_loss, _ = lax.optimization_barrier((my_loss, landed["w_ffn1_in"]))
    tot, every = _small_allreduce([d_sh1, d_sc1, d_g1, d_sh2, d_sc2, d_g2, d_sh3, d_sc3, d_g3,
                                   d_pre0, d_pre1, d_pre2, d_post0, d_post1, d_post2, my_loss])
    (early_land,) = _slabs_wait("gather_all", 1, sm0, grad_x, "small_gather_wait")
    tot_early = _sum_slabs(early_land)

    loss = tot[15, 0]
    g_b_ada = tot[0:9].reshape(1, 9 * D)
    g_pre = lax.dynamic_slice_in_dim(tot[9:12], 128 * me, 128, axis=1)[None]
    g_post = lax.dynamic_slice_in_dim(tot[12:15], 128 * me, 128, axis=1)[None]
    g_lg = tot_early[0:4].reshape(1, G_W)
    g_lb = tot_early[4:8].reshape(1, G_W)
    g_bs = tot_early[8:16][None]
    g_rel = jnp.transpose(tot_early[16:24, 0:N_BUCKETS])
    g_sinks = tot_early[24:32, 0][None]
    g_ws = tot_early[32:1056].reshape(1, N_HEADS, BLK, BLK)

    d_ada_mine = lax.dynamic_slice_in_dim(every[:, 0:9].reshape(N_DEV, 9 * D), ADA_W * me, ADA_W, axis=1)
    ada_out = [a[None] for a in _w_ada_update(gath8[:, 0:D], d_ada_mine, w_ada[0], m_w_ada[0], v_w_ada[0])]

    small = [("rel_bias", rel_bias, g_rel, m_rel_bias, v_rel_bias), ("b_ada", b_ada, g_b_ada, m_b_ada, v_b_ada),
             ("pre_norm_g", pre_norm_g, g_pre, m_pre_norm_g, v_pre_norm_g),
             ("post_norm_g", post_norm_g, g_post, m_post_norm_g, v_post_norm_g),
             ("sinks", sinks, g_sinks, m_sinks, v_sinks), ("gmlp_ln_g", gmlp_ln_g, g_lg, m_gmlp_ln_g, v_gmlp_ln_g),
             ("gmlp_ln_b", gmlp_ln_b, g_lb, m_gmlp_ln_b, v_gmlp_ln_b),
             ("gmlp_w_s", gmlp_w_s, g_ws, m_gmlp_w_s, v_gmlp_w_s), ("gmlp_b_s", gmlp_b_s, g_bs, m_gmlp_b_s, v_gmlp_b_s)]
    two_d = lambda a: a.reshape(int(math.prod(a.shape[:-1])), a.shape[-1])
    stepped = _adamw_small([tuple(two_d(a) for a in item[1:]) for item in small])
    res = {"w_ada": ada_out}
    for (nm, w_, g_, _, _), new in zip(small, stepped):
        res[nm] = [g_] + [a.reshape(w_.shape) for a in new]
    res.update(big)
    order = ["rel_bias", "w_ada", "b_ada", "pre_norm_g", "post_norm_g", "w_ffn1_in", "w_ffn1_out", "w_in", "sinks",
             "gmlp_ln_g", "gmlp_ln_b", "gmlp_w_s", "gmlp_b_s", "w_br_attn", "w_br_gmlp", "w_out", "w_ffn2_in",
             "w_ffn2_out"]
    outs = [loss, grad_x[None]]
    for k in range(4):
        outs += [res[nm][k] for nm in order]
    return tuple(outs)
```

```python
import math

import jax
import jax.numpy as jnp
import numpy as np
from jax import lax
from jax.experimental import pallas as pl
from jax.experimental.pallas import tpu as pltpu

F32 = jnp.float32
BF = jnp.bfloat16

N_DEV = 8
D = 1024
D_FF = 2816
FS = D_FF // 4
N_HEADS = 8
N_KV = 2
GROUP = 4
HD = 64
BLK = 128
Q_W = 512
KV_W = 128
G_W = 512
QKV_W = Q_W + 2 * KV_W
ZG_OFF = QKV_W
GATE_OFF = ZG_OFF + 2 * G_W
IN_W = GATE_OFF + 2 * D
N_BUCKETS = 32
MAX_DISTANCE = 128
EPS = 1e-6
NEG = -1e30
SCALE = HD ** -0.5
ADA_W = 9 * D // N_DEV

ADAM_LR = 0.001
ADAM_B1 = 0.9
ADAM_B2 = 0.999
ADAM_EPS = 1e-08
ADAM_WD = 0.01
ADAM_STEP = 10

CHUNK = 256
MIB = 1024 * 1024
MESH = pl.DeviceIdType.MESH
HIGH = lax.Precision.HIGHEST


def _cp(n_grid, vmem_mib):
    return pltpu.CompilerParams(dimension_semantics=("arbitrary",) * n_grid,
                                vmem_limit_bytes=vmem_mib * MIB)


def _const(shape):
    return pl.BlockSpec(shape, lambda *_: (0,) * len(shape))


def _resident(shape):
    return pl.BlockSpec(shape, lambda *_: (0,) * len(shape), pipeline_mode=pl.Buffered(1))


def _behind(body, n_in, after):
    k = len(after)
    return (lambda *refs: body(*refs[:n_in], *refs[n_in + k:])), [pl.BlockSpec(memory_space=pl.ANY)] * k


def _in_hbm(*arrays):
    return [pltpu.with_memory_space_constraint(a, pltpu.HBM) for a in arrays]


def _sds(shape, dtype):
    return jax.ShapeDtypeStruct(shape, dtype)


def _dot(a, b):
    return jnp.dot(a, b, preferred_element_type=F32)


def _dot_nt(a, b):
    return lax.dot_general(a, b, (((1,), (1,)), ((), ())), preferred_element_type=F32)


def _dot_tn(a, b):
    return lax.dot_general(a, b, (((0,), (0,)), ((), ())), preferred_element_type=F32)


def _rms_r(x):
    return lax.rsqrt(jnp.mean(x * x, axis=-1, keepdims=True) + EPS)


def _colsum(x):
    return jnp.sum(x, axis=0, keepdims=True)


def _prenorm(x, gp, sc, sh):
    return (x * _rms_r(x) * gp) * (1.0 + sc) + sh


def _prenorm_bwd(dn, x, gp, sc):
    r = _rms_r(x)
    xh = x * r
    t = dn * (1.0 + sc) * gp
    dx = r * (t - xh * jnp.mean(t * xh, axis=-1, keepdims=True))
    return dx, _colsum(dn), _colsum(dn * xh * gp), _colsum(dn * (1.0 + sc) * xh)


def _postnorm_bwd(dh, y, gate, gp, res):
    y = y.astype(F32)
    r = _rms_r(y)
    yh = y * r
    dyn = (res * gate) * dh
    t = dyn * gp
    dy = r * (t - yh * jnp.mean(t * yh, axis=-1, keepdims=True))
    return dy, _colsum(res * dh * yh * gp), _colsum(dyn * yh)


def _gelu(x):
    k = math.sqrt(2.0 / math.pi)
    return 0.5 * x * (1.0 + jnp.tanh(k * (x + 0.044715 * x * x * x)))


def _gelu_grad(x):
    k = math.sqrt(2.0 / math.pi)
    t = jnp.tanh(k * (x + 0.044715 * x * x * x))
    return 0.5 * (1.0 + t) + 0.5 * x * (1.0 - t * t) * (k * (1.0 + 3.0 * 0.044715 * x * x))


def _my_place():
    x, y, c = lax.axis_index("x"), lax.axis_index("y"), lax.axis_index("c")
    return x, y, c, 4 * x + 2 * y + c


def _peer(x, y, c, k):
    px = 1 - x if k & 4 else x
    py = 1 - y if k & 2 else y
    pc = 1 - c if k & 1 else c
    return (px, py, pc), 4 * px + 2 * py + pc


HBM_SPEC = pl.BlockSpec(memory_space=pltpu.HBM)
SEM_SPEC = pl.BlockSpec(memory_space=pltpu.SEMAPHORE)
EFFECT = pltpu.SideEffectType.DATAFLOW_SIDE_EFFECTING


RELATIONS = {"exchange": (1, 2, 3, 4, 5, 6, 7), "gather": (1, 2, 4, 6), "forward": (2, 4, 6),
             "gather_all": (1, 2, 3, 4, 5, 6, 7)}


def _slab_copies(mode, srcs, lands, send, recv, loc):
    x, y, c, me = _my_place()
    rel = RELATIONS[mode]
    remote, local = [], []
    for t in range(len(lands)):
        for i, k in enumerate(rel):
            peer, peer_lin = _peer(x, y, c, k)
            if mode == "exchange":
                src, dst, to = srcs[t].at[peer_lin], lands[t].at[me], peer
            elif mode in ("gather", "gather_all"):
                src, dst, to = srcs[t], lands[t].at[me], peer
            else:
                src, dst, to = lands[t].at[peer_lin], lands[t].at[peer_lin], _peer(x, y, c, 1)[0]
            remote.append(pltpu.make_async_remote_copy(
                src_ref=src, dst_ref=dst, send_sem=send.at[t * len(rel) + i], recv_sem=recv.at[t * len(rel) + i],
                device_id=to, device_id_type=MESH))
        if mode == "exchange":
            local.append(pltpu.make_async_copy(srcs[t].at[me], lands[t].at[me], loc.at[t]))
        elif mode in ("gather", "gather_all"):
            local.append(pltpu.make_async_copy(srcs[t], lands[t].at[me], loc.at[t]))
    return remote, local


def _slabs_start(mode, arrays, after, name):
    n = len(arrays)
    if mode == "forward":
        thru = list(arrays)
    else:
        shapes = [a.shape if mode == "exchange" else (N_DEV,) + a.shape for a in arrays]
        thru = list(arrays) + [lax.empty(s, a.dtype) for s, a in zip(shapes, arrays)]
    m = len(thru)
    n_sem = n * len(RELATIONS[mode])

    def body(*refs):
        srcs, lands = refs[:n], refs[m - n:m]
        send, recv, loc = refs[m + 1:m + 4]
        remote, local = _slab_copies(mode, srcs, lands, send, recv, loc)
        for cp in remote + local:
            cp.start()
        refs[-1][...] = jnp.zeros_like(refs[-1])

    return pl.pallas_call(
        body, name=name,
        out_shape=(pltpu.SemaphoreType.DMA((n_sem,)), pltpu.SemaphoreType.DMA((n_sem,)),
                   pltpu.SemaphoreType.DMA((n,)),
                   *[pltpu.HBM(a.shape, a.dtype) for a in thru],
                   _sds((1, D), F32)),
        in_specs=[HBM_SPEC] * m + [pl.BlockSpec(memory_space=pl.ANY)],
        out_specs=(SEM_SPEC, SEM_SPEC, SEM_SPEC, *[HBM_SPEC] * m, pl.BlockSpec(memory_space=pltpu.VMEM)),
        input_output_aliases={t: 3 + t for t in range(m)},
        compiler_params=pltpu.CompilerParams(has_side_effects=EFFECT),
    )(*[pltpu.with_memory_space_constraint(a, pltpu.HBM) for a in thru], after)


def _slabs_wait(mode, n, started, after, name):
    sems = started[0:3]
    thru = started[3:-1]
    m = len(thru)

    def body(*refs):
        srcs, lands = refs[:n], refs[m - n:m]
        remote, local = _slab_copies(mode, srcs, lands, *refs[m:m + 3])
        for cp in remote:
            cp.wait_send()
            cp.wait_recv()
        for cp in local:
            cp.wait()

    res = pl.pallas_call(
        body, name=name,
        out_shape=tuple(pltpu.HBM(a.shape, a.dtype) for a in thru),
        in_specs=[HBM_SPEC] * m + [SEM_SPEC] * 3 + [pl.BlockSpec(memory_space=pl.ANY)],
        out_specs=tuple([HBM_SPEC] * m),
        input_output_aliases={t: t for t in range(m)},
        compiler_params=pltpu.CompilerParams(has_side_effects=EFFECT),
    )(*thru, *sems, after)
    return list(res[m - n:m])


def _ada_forward(small8, w_ada, b_ada64):
    sw = small8.shape[1]

    def body(sm_ref, w_ref, b_ref, gath_ref, ada_ref, part_ref, send1, recv1, send2, recv2):
        x, y, c, me = _my_place()
        row_me = pl.multiple_of(me * 8, 8)
        gath_ref[pl.ds(row_me, 8), :] = sm_ref[...]
        first = []
        for k in range(1, N_DEV):
            peer, _ = _peer(x, y, c, k)
            cp = pltpu.make_async_remote_copy(
                src_ref=sm_ref, dst_ref=gath_ref.at[pl.ds(row_me, 8), :], send_sem=send1.at[k - 1],
                recv_sem=recv1.at[k - 1], device_id=peer, device_id_type=MESH)
            cp.start()
            first.append(cp)
        for cp in first:
            cp.wait()
        cs = gath_ref[:, 0:D]
        cs = cs * jax.nn.sigmoid(cs)
        part_ref[...] = jnp.dot(cs, w_ref[...], preferred_element_type=F32, precision=HIGH)
        ada_ref[pl.ds(row_me, 8), :] = part_ref[pl.ds(row_me, 8), :]
        second = []
        for k in range(1, N_DEV):
            peer, peer_lin = _peer(x, y, c, k)
            cp = pltpu.make_async_remote_copy(
                src_ref=part_ref.at[pl.ds(pl.multiple_of(peer_lin * 8, 8), 8), :],
                dst_ref=ada_ref.at[pl.ds(row_me, 8), :], send_sem=send2.at[k - 1],
                recv_sem=recv2.at[k - 1], device_id=peer, device_id_type=MESH)
            cp.start()
            second.append(cp)
        for cp in second:
            cp.wait()
        ada_ref[...] = ada_ref[...] + b_ref[...]

    vm = pl.BlockSpec(memory_space=pltpu.VMEM)
    return pl.pallas_call(
        body, name="ada_forward",
        out_shape=[_sds((8 * N_DEV, sw), F32), _sds((8 * N_DEV, ADA_W), F32)],
        in_specs=[vm, vm, vm], out_specs=[vm, vm],
        scratch_shapes=[pltpu.VMEM((8 * N_DEV, ADA_W), F32)] + [pltpu.SemaphoreType.DMA((7,))] * 4,
        compiler_params=pltpu.CompilerParams(vmem_limit_bytes=32 * MIB),
    )(small8, w_ada, b_ada64)


def _sum_slabs(land):
    def body(l_ref, o_ref):
        acc = l_ref[0]
        for j in range(1, N_DEV):
            acc = acc + l_ref[j]
        o_ref[...] = acc

    vm = pl.BlockSpec(memory_space=pltpu.VMEM)
    return pl.pallas_call(body, name="sum_slabs", out_shape=_sds(land.shape[1:], F32), in_specs=[vm], out_specs=vm,
                          compiler_params=pltpu.CompilerParams(vmem_limit_bytes=32 * MIB))(land)


def _small_allreduce(vectors):
    n = len(vectors)

    def body(*refs):
        v_refs, (sum_ref, gath_ref, pack, send, recv) = refs[:n], refs[n:]
        x, y, c, me = _my_place()
        for k in range(n):
            pack[k:k + 1, :] = v_refs[k][...]
        gath_ref[me] = pack[...]
        cps = []
        for k in range(1, N_DEV):
            peer, _ = _peer(x, y, c, k)
            cp = pltpu.make_async_remote_copy(
                src_ref=pack, dst_ref=gath_ref.at[me], send_sem=send.at[k - 1],
                recv_sem=recv.at[k - 1], device_id=peer, device_id_type=MESH)
            cp.start()
            cps.append(cp)
        for cp in cps:
            cp.wait()
        acc = gath_ref[0]
        for j in range(1, N_DEV):
            acc = acc + gath_ref[j]
        sum_ref[...] = acc

    vm = pl.BlockSpec(memory_space=pltpu.VMEM)
    return pl.pallas_call(
        body, name="small_allreduce",
        out_shape=[_sds((n, D), F32), _sds((N_DEV, n, D), F32)],
        in_specs=[vm] * n, out_specs=[vm, vm],
        scratch_shapes=[pltpu.VMEM((n, D), F32), pltpu.SemaphoreType.DMA((7,)), pltpu.SemaphoreType.DMA((7,))],
    )(*vectors)


F_TILES = tuple((f0, min(512, D_FF - f0)) for f0 in range(0, D_FF, 512))
F_TILES_NARROW = tuple((f0, 256) for f0 in range(0, D_FF, 256))


def _swiglu_tile(n, wt_ref, f0, tf):
    g = _dot_nt(n, wt_ref[f0:f0 + tf, :])
    u = _dot_nt(n, wt_ref[D_FF + f0:D_FF + f0 + tf, :])
    sg = jax.nn.sigmoid(g)
    silu = g * sg
    return (u * (sg * (1.0 + g * (1.0 - sg)))).astype(BF), silu.astype(BF), (silu * u).astype(BF)


def _ffn_in(h, sh, sc, gp, wt, name):
    S = h.shape[0]
    R = min(512, S)

    def body(h_ref, sh_ref, sc_ref, gp_ref, w_ref, n_ref, dg_ref, sl_ref, a_ref):
        for r0 in range(0, R, CHUNK):
            rows = slice(r0, r0 + CHUNK)
            n = _prenorm(h_ref[rows, :], gp_ref[...], sc_ref[...], sh_ref[...]).astype(BF)
            n_ref[rows, :] = n
            for f0, tf in F_TILES_NARROW:
                dg_ref[rows, f0:f0 + tf], sl_ref[rows, f0:f0 + tf], a_ref[rows, f0:f0 + tf] = _swiglu_tile(
                    n, w_ref, f0, tf)

    vec = _const((1, D))
    rows_ = lambda w_: pl.BlockSpec((R, w_), lambda i: (i, 0))
    return pl.pallas_call(
        body, name=name, grid=(S // R,),
        out_shape=[_sds((S, D), BF)] + [_sds((S, D_FF), BF)] * 3,
        in_specs=[rows_(D), vec, vec, vec, _resident((2 * D_FF, D))],
        out_specs=[rows_(D), rows_(D_FF), rows_(D_FF), rows_(D_FF)],
        compiler_params=_cp(1, 56),
    )(*_in_hbm(h), sh, sc, gp, *_in_hbm(wt))


def _ffn_out(a, w, h, gate, gp, name, target=None):
    S = h.shape[0]
    R = min(512, S)
    with_loss = target is not None

    def body(a_ref, w_ref, h_ref, gate_ref, gp_ref, *rest):
        if with_loss:
            t_ref, out_ref, y_ref, tot_ref = rest

            @pl.when(pl.program_id(0) == 0)
            def _():
                tot_ref[...] = jnp.zeros_like(tot_ref)
        else:
            out_ref, y_ref = rest
        for r0 in range(0, R, CHUNK):
            rows = slice(r0, r0 + CHUNK)
            y = _dot(a_ref[rows, :], w_ref[...])
            y_ref[rows, :] = y.astype(BF)
            hn = h_ref[rows, :] + (0.5 * gate_ref[...]) * (y * _rms_r(y) * gp_ref[...])
            if with_loss:
                e = hn - t_ref[rows, :]
                out_ref[rows, :] = e * (1.0 / D)
                tot_ref[...] += jnp.sum(jnp.sum(e * e, axis=1, keepdims=True), axis=0, keepdims=True)
            else:
                out_ref[rows, :] = hn

    vec = _const((1, D))
    rows_ = lambda w_: pl.BlockSpec((R, w_), lambda i: (i, 0))
    return pl.pallas_call(
        body, name=name, grid=(S // R,),
        out_shape=[_sds((S, D), F32), _sds((S, D), BF)] + ([_sds((1, 1), F32)] if with_loss else []),
        in_specs=[rows_(D_FF), _resident((D_FF, D)), rows_(D), vec, vec] + ([rows_(D)] if with_loss else []),
        out_specs=[rows_(D), rows_(D)] + ([_const((1, 1))] if with_loss else []),
        compiler_params=_cp(1, 48),
    )(*_in_hbm(a, w, h), gate, gp, *(_in_hbm(target) if with_loss else ()))


def _ffn_out_bwd(dh, y, dsilu_u, silu, w, gate, gp, name, after=()):
    S = dh.shape[0]
    R = min(512, S)

    def body(dh_ref, y_ref, g_ref, u_ref, w_ref, gate_ref, gp_ref, dy_ref, dgu_ref, dgate_ref, dgp_ref):
        @pl.when(pl.program_id(0) == 0)
        def _():
            dgate_ref[...] = jnp.zeros_like(dgate_ref)
            dgp_ref[...] = jnp.zeros_like(dgp_ref)
        for r0 in range(0, R, CHUNK):
            rows = slice(r0, r0 + CHUNK)
            dy, dgate, dgp = _postnorm_bwd(dh_ref[rows, :], y_ref[rows, :], gate_ref[...], gp_ref[...], 0.5)
            dgate_ref[...] += dgate
            dgp_ref[...] += dgp
            dyb = dy.astype(BF)
            dy_ref[rows, :] = dyb
            for f0, tf in F_TILES:
                da = _dot_nt(dyb, w_ref[f0:f0 + tf, :])
                dgu_ref[rows, f0:f0 + tf] = (da * g_ref[rows, f0:f0 + tf].astype(F32)).astype(BF)
                dgu_ref[rows, D_FF + f0:D_FF + f0 + tf] = (da * u_ref[rows, f0:f0 + tf].astype(F32)).astype(BF)

    vec = _const((1, D))
    rows_ = lambda w_: pl.BlockSpec((R, w_), lambda i: (i, 0))
    body, after_specs = _behind(body, 7, after)
    return pl.pallas_call(
        body, name=name, grid=(S // R,),
        out_shape=[_sds((S, D), BF), _sds((S, 2 * D_FF), BF), _sds((1, D), F32), _sds((1, D), F32)],
        in_specs=[rows_(D), rows_(D), rows_(D_FF), rows_(D_FF), _resident((D_FF, D)), vec, vec] + after_specs,
        out_specs=[rows_(D), rows_(2 * D_FF), vec, vec],
        compiler_params=_cp(1, 56),
    )(*_in_hbm(dh, y, dsilu_u, silu, w), gate, gp, *after)


def _ffn_dn(dgu, wt, h, dh, sc, gp, name, after=()):
    S = h.shape[0]
    R = min(512, S)

    def body(dgu_ref, w_ref, h_ref, dh_ref, sc_ref, gp_ref, out_ref, dsh_ref, dsc_ref, dgp_ref):
        @pl.when(pl.program_id(0) == 0)
        def _():
            dsh_ref[...] = jnp.zeros_like(dsh_ref)
            dsc_ref[...] = jnp.zeros_like(dsc_ref)
            dgp_ref[...] = jnp.zeros_like(dgp_ref)

        for r0 in range(0, R, CHUNK):
            rows = slice(r0, r0 + CHUNK)
            dn = _dot(dgu_ref[rows, :], w_ref[...])
            dx, dsh, dsc, dgp = _prenorm_bwd(dn, h_ref[rows, :], gp_ref[...], sc_ref[...])
            out_ref[rows, :] = dh_ref[rows, :] + dx
            dsh_ref[...] += dsh
            dsc_ref[...] += dsc
            dgp_ref[...] += dgp

    vec = _const((1, D))
    rows_ = lambda w_: pl.BlockSpec((R, w_), lambda i: (i, 0))
    body, after_specs = _behind(body, 6, after)
    return pl.pallas_call(
        body, name=name, grid=(S // R,),
        out_shape=[_sds((S, D), F32)] + [_sds((1, D), F32)] * 3,
        in_specs=[rows_(2 * D_FF), _resident((2 * D_FF, D)), rows_(D), rows_(D), vec, vec] + after_specs,
        out_specs=[rows_(D), vec, vec, vec],
        compiler_params=_cp(1, 56),
    )(*_in_hbm(dgu, wt, h, dh), sc, gp, *after)


def _ffn_bwd(dh, y, dsilu_u, silu, w, wt, h, gate, gpost, sc, gpre, name):
    S = dh.shape[0]
    R = min(256, S)

    def body(dh_ref, y_ref, g_ref, u_ref, w_ref, wt_ref, h_ref, gate_ref, gpost_ref, sc_ref, gpre_ref,
             dy_ref, dgu_ref, out_ref, dgate_ref, dgpost_ref, dsh_ref, dsc_ref, dgpre_ref):
        @pl.when(pl.program_id(0) == 0)
        def _():
            for r in (dgate_ref, dgpost_ref, dsh_ref, dsc_ref, dgpre_ref):
                r[...] = jnp.zeros_like(r)
        dhh = dh_ref[...]
        dy, dgate, dgpost = _postnorm_bwd(dhh, y_ref[...], gate_ref[...], gpost_ref[...], 0.5)
        dgate_ref[...] += dgate
        dgpost_ref[...] += dgpost
        dyb = dy.astype(BF)
        dy_ref[...] = dyb
        for f0, tf in F_TILES:
            da = _dot_nt(dyb, w_ref[f0:f0 + tf, :])
            dgu_ref[:, f0:f0 + tf] = (da * g_ref[:, f0:f0 + tf].astype(F32)).astype(BF)
            dgu_ref[:, D_FF + f0:D_FF + f0 + tf] = (da * u_ref[:, f0:f0 + tf].astype(F32)).astype(BF)
        dn = _dot(dgu_ref[...], wt_ref[...])
        dx, dsh, dsc, dgpre = _prenorm_bwd(dn, h_ref[...], gpre_ref[...], sc_ref[...])
        out_ref[...] = dhh + dx
        dsh_ref[...] += dsh
        dsc_ref[...] += dsc
        dgpre_ref[...] += dgpre

    vec = _const((1, D))
    rows_ = lambda w_: pl.BlockSpec((R, w_), lambda i: (i, 0))
    return pl.pallas_call(
        body, name=name, grid=(S // R,),
        out_shape=[_sds((S, D), BF), _sds((S, 2 * D_FF), BF), _sds((S, D), F32)] + [_sds((1, D), F32)] * 5,
        in_specs=[rows_(D), rows_(D), rows_(D_FF), rows_(D_FF), _resident((D_FF, D)), _resident((2 * D_FF, D)),
                  rows_(D), vec, vec, vec, vec],
        out_specs=[rows_(D), rows_(2 * D_FF), rows_(D)] + [vec] * 5,
        compiler_params=_cp(1, 56),
    )(*_in_hbm(dh, y, dsilu_u, silu, w, wt, h), gate, gpost, sc, gpre)


def _tn_matmul(a, b, name, tm=None):
    many = isinstance(a, (list, tuple))
    arrays = list(a) if many else [a]
    assert tm is None or not many
    n = len(arrays)
    S, N = b.shape
    widths = [x.shape[1] if tm is None else tm for x in arrays]
    GA = arrays[0].shape[1] // widths[0]
    ts = min(2048 if sum(widths) * N <= 2 * D * D else 1024, S // 2)
    nk = S // ts
    assert nk >= 2 and nk * ts == S

    def body(*refs):
        a_refs, b_ref, o_refs, accs = refs[:n], refs[n], refs[n + 1:2 * n + 1], refs[2 * n + 1:]
        k = pl.program_id(1)
        parts = [(a_ref, o_ref, acc, m0, min(CHUNK, w - m0))
                 for a_ref, o_ref, acc, w in zip(a_refs, o_refs, accs, widths) for m0 in range(0, w, CHUNK)]

        @pl.when(k == 0)
        def _():
            for a_ref, _, acc, m0, mc in parts:
                acc[m0:m0 + mc, :] = _dot_tn(a_ref[:, m0:m0 + mc], b_ref[...])

        @pl.when(jnp.logical_and(k > 0, k < nk - 1))
        def _():
            for a_ref, _, acc, m0, mc in parts:
                acc[m0:m0 + mc, :] += _dot_tn(a_ref[:, m0:m0 + mc], b_ref[...])

        @pl.when(k == nk - 1)
        def _():
            for a_ref, o_ref, acc, m0, mc in parts:
                o_ref[m0:m0 + mc, :] = (acc[m0:m0 + mc, :] + _dot_tn(a_ref[:, m0:m0 + mc], b_ref[...])).astype(BF)

    res = pl.pallas_call(
        body, name=name, grid=(GA, nk),
        out_shape=[_sds((x.shape[1], N), BF) for x in arrays],
        in_specs=[pl.BlockSpec((ts, w), lambda ga, k: (k, ga)) for w in widths]
                 + [pl.BlockSpec((ts, N), lambda ga, k: (k, 0))],
        out_specs=[pl.BlockSpec((w, N), lambda ga, k: (ga, 0)) for w in widths],
        scratch_shapes=[pltpu.VMEM((w, N), F32) for w in widths],
        compiler_params=_cp(2, 56),
    )(*_in_hbm(*arrays, b))
    return list(res) if many else res[0]


def _mix_in(h, sh, sc, gp, w, wq):
    S = h.shape[0]
    R = min(512, S)

    def body(h_ref, sh_ref, sc_ref, gp_ref, w_ref, wq_ref, n_ref, qkv_ref, zg_ref, gates_ref):
        for r0 in range(0, R, CHUNK):
            rows = slice(r0, r0 + CHUNK)
            nb = _prenorm(h_ref[rows, :], gp_ref[...], sc_ref[...], sh_ref[...]).astype(BF)
            n_ref[rows, :] = nb
            qkv_ref[rows, 0:Q_W] = _dot_nt(nb, wq_ref[...]).astype(BF)
            qkv_ref[rows, Q_W:QKV_W] = _dot_nt(nb, w_ref[Q_W:QKV_W, :]).astype(BF)
            zg_ref[rows, :] = _dot_nt(nb, w_ref[ZG_OFF:GATE_OFF, :]).astype(BF)
            gates_ref[rows, :] = jax.nn.sigmoid(_dot_nt(nb, w_ref[GATE_OFF:IN_W, :])).astype(BF)

    vec = _const((1, D))
    rows = lambda w_: pl.BlockSpec((R, w_), lambda i: (i, 0))
    return pl.pallas_call(
        body, name="mix_in", grid=(S // R,),
        out_shape=[_sds((S, D), BF), _sds((S, QKV_W), BF), _sds((S, 2 * G_W), BF), _sds((S, 2 * D), BF)],
        in_specs=[rows(D), vec, vec, vec, _resident((IN_W, D)), _resident((Q_W, D))],
        out_specs=[rows(D), rows(QKV_W), rows(2 * G_W), rows(2 * D)],
        compiler_params=_cp(1, 48),
    )(*_in_hbm(h), sh, sc, gp, *_in_hbm(w, wq))


def _bias_table(rel_bias, bucket):
    def body(rel_ref, bk_ref, out_ref):
        bk = bk_ref[...]
        qi = lax.broadcasted_iota(jnp.int32, (BLK, 2 * BLK), 0)
        kj = lax.broadcasted_iota(jnp.int32, (BLK, 2 * BLK), 1)
        dist = qi + BLK - kj
        window = (dist >= 0) & (dist < BLK)
        for h in range(N_HEADS):
            acc = jnp.zeros((BLK, 2 * BLK), F32)
            for b in range(N_BUCKETS):
                acc = jnp.where(bk == b, rel_ref[b, h], acc)
            out_ref[h // GROUP, pl.ds((h % GROUP) * BLK, BLK), :] = jnp.where(window, acc, NEG)

    return pl.pallas_call(
        body, name="bias_table",
        out_shape=_sds((N_KV, GROUP * BLK, 2 * BLK), F32),
        in_specs=[pl.BlockSpec(memory_space=pltpu.SMEM), pl.BlockSpec(memory_space=pltpu.VMEM)],
        out_specs=pl.BlockSpec(memory_space=pltpu.VMEM),
    )(rel_bias, bucket)


ATT_TB = 8


HEAD_ROWS = N_HEADS * BLK


def _pair_heads(w):
    return jnp.transpose(w.reshape(N_KV, GROUP, HD, w.shape[1]), (1, 0, 2, 3)).reshape(w.shape)


def _unpair_heads(w):
    return jnp.transpose(w.reshape(GROUP, N_KV, HD, w.shape[1]), (1, 0, 2, 3)).reshape(w.shape)


def _halves(x, scale=1.0):
    low = lax.broadcasted_iota(jnp.int32, x.shape, 1) < HD
    xf = x.astype(F32) * scale
    return jnp.where(low, xf, 0.0).astype(BF), jnp.where(low, 0.0, xf).astype(BF)


def _stack_heads(x, scale=1.0):
    halves = [_halves(x[:, g * 128:(g + 1) * 128], scale) for g in range(GROUP)]
    return jnp.concatenate([lo for lo, _ in halves] + [hi for _, hi in halves], axis=0)


def _attn_probs(q, kvc, kvp, bias_ref, sink_ref, has_prev):
    kv2 = jnp.concatenate([kvp, kvc], axis=0)
    kboth, vboth = kv2[:, 0:KV_W], kv2[:, KV_W:2 * KV_W]
    qpad = _stack_heads(q, SCALE)
    s = _dot_nt(qpad, kboth) + bias_ref[...]
    if has_prev is not None:
        col = lax.broadcasted_iota(jnp.int32, (HEAD_ROWS, 2 * BLK), 1)
        s = jnp.where((col >= BLK) | has_prev, s, NEG)
    row_head = lax.broadcasted_iota(jnp.int32, (HEAD_ROWS, 1), 0) // BLK
    sink = jnp.zeros((HEAD_ROWS, 1), F32)
    for h in range(N_HEADS):
        sink = jnp.where(row_head == h, sink_ref[h], sink)
    m = jnp.maximum(jnp.max(s, axis=1, keepdims=True), sink)
    p = jnp.exp(s - m)
    e_sink = jnp.exp(sink - m)
    inv = 1.0 / (jnp.sum(p, axis=1, keepdims=True) + e_sink)
    return qpad, kboth, vboth, p * inv, e_sink * inv


def _attn_fwd(qkv, bias, sinks):
    S = qkv.shape[0]
    tb = min(ATT_TB, S // BLK)
    T = tb * BLK

    def body(sink_ref, q_ref, kv_ref, kvp_ref, bias_ref, o_ref):
        step = pl.program_id(0)
        for j in range(tb):
            rows = slice(j * BLK, (j + 1) * BLK)
            kvp = kvp_ref[...] if j == 0 else kv_ref[(j - 1) * BLK:j * BLK, :]
            has_prev = (step > 0) if j == 0 else None
            _, _, vboth, prob, _ = _attn_probs(q_ref[rows, :], kv_ref[rows, :], kvp, bias_ref, sink_ref, has_prev)
            pb = prob.astype(BF)
            v_low, v_high = _halves(vboth)
            half = HEAD_ROWS // 2
            o = _dot(pb[0:half], v_low) + _dot(pb[half:HEAD_ROWS], v_high)
            for g in range(GROUP):
                o_ref[rows, g * 128:(g + 1) * 128] = o[g * BLK:(g + 1) * BLK].astype(BF)

    return pl.pallas_call(
        body, name="attn_fwd", grid=(S // T,),
        out_shape=_sds((S, Q_W), BF),
        in_specs=[pl.BlockSpec(memory_space=pltpu.SMEM),
                  pl.BlockSpec((T, Q_W), lambda i: (i, 0)),
                  pl.BlockSpec((T, 2 * KV_W), lambda i: (i, 2)),
                  pl.BlockSpec((BLK, 2 * KV_W), lambda i: (jnp.maximum(i * tb - 1, 0), 2)),
                  _const((HEAD_ROWS, 2 * BLK))],
        out_specs=pl.BlockSpec((T, Q_W), lambda i: (i, 0)),
        compiler_params=_cp(1, 32),
    )(sinks, *_in_hbm(qkv, qkv, qkv, bias))


def _attn_bwd(qkv, bias, sinks, do):
    S = qkv.shape[0]
    tb = min(ATT_TB, S // BLK)
    T = tb * BLK
    nt = S // T
    half = HEAD_ROWS // 2

    def body(sink_ref, q_ref, kv_ref, kvp_ref, bias_ref, do_ref, dq_ref, dkv_ref, dbias_ref, dsink_ref, carry):
        i = pl.program_id(0)

        @pl.when(i == 0)
        def _():
            carry[...] = jnp.zeros_like(carry)
            dbias_ref[...] = jnp.zeros_like(dbias_ref)
            dsink_ref[...] = jnp.zeros_like(dsink_ref)

        from_next = carry[...]
        head_row = lax.broadcasted_iota(jnp.int32, (N_HEADS, 128), 0)
        low = lax.broadcasted_iota(jnp.int32, (BLK, 128), 1) < HD
        for j in reversed(range(tb)):
            rows = slice(j * BLK, (j + 1) * BLK)
            kvp = kvp_ref[...] if j == 0 else kv_ref[(j - 1) * BLK:j * BLK, :]
            has_prev = (i < nt - 1) if j == 0 else None
            qpad, kboth, vboth, prob, p_sink = _attn_probs(q_ref[rows, :], kv_ref[rows, :], kvp, bias_ref, sink_ref,
                                                           has_prev)
            pb = prob.astype(BF)
            dopad = _stack_heads(do_ref[rows, :])
            dp = _dot_nt(dopad, vboth)
            delta = jnp.sum(prob * dp, axis=1, keepdims=True)
            ds = prob * (dp - delta)
            dbias_ref[...] += ds
            sink_term = p_sink * delta
            dsink_rows = jnp.zeros((N_HEADS, 128), F32)
            for h in range(N_HEADS):
                val = -jnp.sum(sink_term[h * BLK:(h + 1) * BLK], axis=0, keepdims=True)
                dsink_rows = jnp.where(head_row == h, val, dsink_rows)
            dsink_ref[...] += dsink_rows
            dsb = ds.astype(BF)
            dqpad = _dot(dsb, kboth) * SCALE
            for g in range(GROUP):
                dq_ref[rows, g * 128:(g + 1) * 128] = jnp.where(
                    low, dqpad[g * BLK:(g + 1) * BLK], dqpad[half + g * BLK:half + (g + 1) * BLK]).astype(BF)
            dkv2 = jnp.concatenate([jnp.transpose(_dot_tn(qpad, dsb)),
                                    jnp.transpose(_dot_tn(dopad, pb))], axis=1)
            dkv_ref[rows, :] = (dkv2[BLK:2 * BLK] + from_next).astype(BF)
            from_next = dkv2[0:BLK]
        carry[...] = from_next

    return pl.pallas_call(
        body, name="attn_bwd", grid=(nt,),
        out_shape=[_sds((S, Q_W), BF), _sds((S, 2 * KV_W), BF),
                   _sds((HEAD_ROWS, 2 * BLK), F32), _sds((N_HEADS, 128), F32)],
        in_specs=[pl.BlockSpec(memory_space=pltpu.SMEM),
                  pl.BlockSpec((T, Q_W), lambda i: (nt - 1 - i, 0)),
                  pl.BlockSpec((T, 2 * KV_W), lambda i: (nt - 1 - i, 2)),
                  pl.BlockSpec((BLK, 2 * KV_W), lambda i: (jnp.maximum((nt - 1 - i) * tb - 1, 0), 2)),
                  _const((HEAD_ROWS, 2 * BLK)),
                  pl.BlockSpec((T, Q_W), lambda i: (nt - 1 - i, 0))],
        out_specs=[pl.BlockSpec((T, Q_W), lambda i: (nt - 1 - i, 0)),
                   pl.BlockSpec((T, 2 * KV_W), lambda i: (nt - 1 - i, 0)),
                   _const((HEAD_ROWS, 2 * BLK)), _const((N_HEADS, 128))],
        scratch_shapes=[pltpu.VMEM((BLK, 2 * KV_W), F32)],
        compiler_params=_cp(1, 32),
    )(sinks, *_in_hbm(qkv, qkv, qkv, bias, do))


def _rel_bias_grad(dbias, bucket):
    def body(db_ref, bk_ref, out_ref):
        bk = bk_ref[...]
        lane = lax.broadcasted_iota(jnp.int32, (1, 128), 1)
        for h in range(N_HEADS):
            d = db_ref[h // GROUP, pl.ds((h % GROUP) * BLK, BLK), :]
            row = jnp.zeros((1, 128), F32)
            for b in range(N_BUCKETS):
                tot = jnp.sum(jnp.sum(jnp.where(bk == b, d, 0.0), axis=1, keepdims=True), axis=0, keepdims=True)
                row = jnp.where(lane == b, tot, row)
            out_ref[pl.ds(h, 1), :] = row

    vm = pl.BlockSpec(memory_space=pltpu.VMEM)
    return pl.pallas_call(body, name="rel_bias_grad", out_shape=_sds((N_HEADS, 128), F32),
                          in_specs=[vm, vm], out_specs=vm)(dbias, bucket)


def _gmlp_parts(zg, lg_ref, lb_ref):
    z = zg.astype(F32)
    ge = _gelu(z)
    u, vg = ge[:, 0:G_W], ge[:, G_W:2 * G_W]
    mu = jnp.mean(vg, axis=-1, keepdims=True)
    xc = vg - mu
    rstd = lax.rsqrt(jnp.mean(xc * xc, axis=-1, keepdims=True) + EPS)
    xh = xc * rstd
    return z, u, xh, rstd, xh * lg_ref[...] + lb_ref[...]


def _causal_weights(ws_ref, wc):
    t = lax.broadcasted_iota(jnp.int32, (BLK, BLK), 0)
    s = lax.broadcasted_iota(jnp.int32, (BLK, BLK), 1)
    for g in range(N_HEADS):
        wc[g] = jnp.where(s <= t, ws_ref[g], 0.0).astype(BF)


def _spatial(vb, wc, bst_ref, p, low):
    xp = vb[:, p * 128:(p + 1) * 128]
    s0 = _dot(wc[2 * p], xp) + bst_ref[:, 2 * p:2 * p + 1]
    s1 = _dot(wc[2 * p + 1], xp) + bst_ref[:, 2 * p + 1:2 * p + 2]
    return xp, jnp.where(low, s0, s1)


def _gmlp_fwd(zg, lg, lb, ws, bst):
    S = zg.shape[0]
    tb = min(ATT_TB, S // BLK)
    T = tb * BLK

    def body(zg_ref, lg_ref, lb_ref, ws_ref, bst_ref, o_ref, wc):
        @pl.when(pl.program_id(0) == 0)
        def _():
            _causal_weights(ws_ref, wc)
        low = lax.broadcasted_iota(jnp.int32, (BLK, 128), 1) < HD
        for j in range(tb):
            rows = slice(j * BLK, (j + 1) * BLK)
            _, u, _, _, vln = _gmlp_parts(zg_ref[rows, :], lg_ref, lb_ref)
            vb = vln.astype(BF)
            for p in range(4):
                _, sp = _spatial(vb, wc, bst_ref, p, low)
                o_ref[rows, p * 128:(p + 1) * 128] = (u[:, p * 128:(p + 1) * 128] * sp).astype(BF)

    return pl.pallas_call(
        body, name="gmlp_fwd", grid=(S // T,),
        out_shape=_sds((S, G_W), BF),
        in_specs=[pl.BlockSpec((T, 2 * G_W), lambda i: (i, 0)), _const((1, G_W)), _const((1, G_W)),
                  _const((N_HEADS, BLK, BLK)), _const((BLK, N_HEADS))],
        out_specs=pl.BlockSpec((T, G_W), lambda i: (i, 0)),
        scratch_shapes=[pltpu.VMEM((N_HEADS, BLK, BLK), BF)],
        compiler_params=_cp(1, 32),
    )(*_in_hbm(zg), lg, lb, ws, bst)


def _gmlp_bwd(zg, d_out, lg, lb, ws, bst):
    S = zg.shape[0]
    tb = min(ATT_TB, S // BLK)
    T = tb * BLK
    nb = S // T

    def body(zg_ref, d_ref, lg_ref, lb_ref, ws_ref, bst_ref, dzg_ref, dws_ref, dbs_ref, dlg_ref, dlb_ref, wc, dbacc):
        i = pl.program_id(0)

        @pl.when(i == 0)
        def _():
            _causal_weights(ws_ref, wc)
            dws_ref[...] = jnp.zeros_like(dws_ref)
            dlg_ref[...] = jnp.zeros_like(dlg_ref)
            dlb_ref[...] = jnp.zeros_like(dlb_ref)
            dbacc[...] = jnp.zeros_like(dbacc)

        low = lax.broadcasted_iota(jnp.int32, (BLK, 128), 1) < HD
        for j in range(tb):
            rows = slice(j * BLK, (j + 1) * BLK)
            z, u, xh, rstd, vln = _gmlp_parts(zg_ref[rows, :], lg_ref, lb_ref)
            vb = vln.astype(BF)
            d = d_ref[rows, :].astype(F32)
            du_parts, dvln_parts = [], []
            for p in range(4):
                xp, sp = _spatial(vb, wc, bst_ref, p, low)
                dp = d[:, p * 128:(p + 1) * 128]
                du_parts.append(dp * sp)
                dsp = dp * u[:, p * 128:(p + 1) * 128]
                dbacc[:, p * 128:(p + 1) * 128] += dsp
                d0 = jnp.where(low, dsp, 0.0).astype(BF)
                d1 = jnp.where(low, 0.0, dsp).astype(BF)
                dws_ref[2 * p] += _dot_nt(d0, xp)
                dws_ref[2 * p + 1] += _dot_nt(d1, xp)
                dvln_parts.append(_dot_tn(wc[2 * p], d0) + _dot_tn(wc[2 * p + 1], d1))
            dvln = jnp.concatenate(dvln_parts, axis=1)
            dlg_ref[...] += _colsum(dvln * xh)
            dlb_ref[...] += _colsum(dvln)
            dxh = dvln * lg_ref[...]
            dvg = rstd * (dxh - jnp.mean(dxh, axis=-1, keepdims=True)
                          - xh * jnp.mean(dxh * xh, axis=-1, keepdims=True))
            dge = jnp.concatenate(du_parts + [dvg], axis=1)
            dzg_ref[rows, :] = (dge * _gelu_grad(z)).astype(BF)

        @pl.when(i == nb - 1)
        def _():
            t = lax.broadcasted_iota(jnp.int32, (BLK, BLK), 0)
            s = lax.broadcasted_iota(jnp.int32, (BLK, BLK), 1)
            for g in range(N_HEADS):
                dws_ref[g] = jnp.where(s <= t, dws_ref[g], 0.0)
            grp = lax.broadcasted_iota(jnp.int32, (N_HEADS, G_W), 0)
            lane = lax.broadcasted_iota(jnp.int32, (N_HEADS, G_W), 1) // HD
            pick = jnp.where(grp == lane, 1.0, 0.0).astype(F32)
            dbs_ref[...] = lax.dot_general(pick, dbacc[...], (((1,), (1,)), ((), ())),
                                           preferred_element_type=F32, precision=HIGH)

    return pl.pallas_call(
        body, name="gmlp_bwd", grid=(nb,),
        out_shape=[_sds((S, 2 * G_W), BF), _sds((N_HEADS, BLK, BLK), F32), _sds((N_HEADS, BLK), F32),
                   _sds((1, G_W), F32), _sds((1, G_W), F32)],
        in_specs=[pl.BlockSpec((T, 2 * G_W), lambda i: (i, 0)), pl.BlockSpec((T, G_W), lambda i: (i, 0)),
                  _const((1, G_W)), _const((1, G_W)), _const((N_HEADS, BLK, BLK)), _const((BLK, N_HEADS))],
        out_specs=[pl.BlockSpec((T, 2 * G_W), lambda i: (i, 0)), _const((N_HEADS, BLK, BLK)),
                   _const((N_HEADS, BLK)), _const((1, G_W)), _const((1, G_W))],
        scratch_shapes=[pltpu.VMEM((N_HEADS, BLK, BLK), BF), pltpu.VMEM((BLK, G_W), F32)],
        compiler_params=_cp(1, 32),
    )(*_in_hbm(zg, d_out), lg, lb, ws, bst)


def _mix_out(o, gm, gates, h, wa, wg, wo, gate, gp, after=()):
    S = h.shape[0]
    R = min(512, S)

    def body(o_ref, gm_ref, gates_ref, h_ref, wa_ref, wg_ref, wo_ref, gate_ref, gp_ref,
             ya_ref, yg_ref, ym_ref, y_ref, hn_ref):
        for r0 in range(0, R, CHUNK):
            rows = slice(r0, r0 + CHUNK)
            ya = _dot(o_ref[rows, :], wa_ref[...])
            yg = _dot(gm_ref[rows, :], wg_ref[...])
            ya_ref[rows, :] = ya.astype(BF)
            yg_ref[rows, :] = yg.astype(BF)
            ym = (gates_ref[rows, 0:D].astype(F32) * ya + gates_ref[rows, D:2 * D].astype(F32) * yg).astype(BF)
            ym_ref[rows, :] = ym
            y = _dot(ym, wo_ref[...])
            y_ref[rows, :] = y.astype(BF)
            hn_ref[rows, :] = h_ref[rows, :] + gate_ref[...] * (y * _rms_r(y) * gp_ref[...])

    vec = _const((1, D))
    rows = lambda w_: pl.BlockSpec((R, w_), lambda i: (i, 0))
    body, after_specs = _behind(body, 9, after)
    return pl.pallas_call(
        body, name="mix_out", grid=(S // R,),
        out_shape=[_sds((S, D), BF)] * 4 + [_sds((S, D), F32)],
        in_specs=[rows(Q_W), rows(G_W), rows(2 * D), rows(D), _resident((Q_W, D)), _resident((G_W, D)),
                  _resident((D, D)), vec, vec] + after_specs,
        out_specs=[rows(D)] * 5,
        compiler_params=_cp(1, 48),
    )(*_in_hbm(o, gm, gates, h, wa, wg, wo), gate, gp, *after)


def _mix_out_bwd(dh, y, ya, yg, gates, att, gm, ymix, wa, wg, wo, gate, gp, after=()):
    S = dh.shape[0]
    R = min(512, S)
    nb = S // R

    def body(dh_ref, y_ref, ya_ref, yg_ref, gates_ref, att_ref, gm_ref, ym_ref, wa_ref, wg_ref, wo_ref,
             gate_ref, gp_ref, dz_ref, do_ref, dgm_ref, dgate_ref, dgp_ref, gwo_ref, gwa_ref, gwg_ref,
             acc_o, acc_a, acc_g, dy_scr, dya_scr, dyg_scr):
        i = pl.program_id(0)

        @pl.when(i == 0)
        def _():
            for r in (dgate_ref, dgp_ref, acc_o, acc_a, acc_g):
                r[...] = jnp.zeros_like(r)
        for r0 in range(0, R, 2 * CHUNK):
            rows = slice(r0, min(r0 + 2 * CHUNK, R))
            dy, dgate, dgp = _postnorm_bwd(dh_ref[rows, :], y_ref[rows, :], gate_ref[...], gp_ref[...], 1.0)
            dgate_ref[...] += dgate
            dgp_ref[...] += dgp
            dyb = dy.astype(BF)
            dy_scr[rows, :] = dyb
            dym = _dot_nt(dyb, wo_ref[...])
            ga = gates_ref[rows, 0:D].astype(F32)
            gg = gates_ref[rows, D:2 * D].astype(F32)
            dya = (dym * ga).astype(BF)
            dyg = (dym * gg).astype(BF)
            dya_scr[rows, :] = dya
            dyg_scr[rows, :] = dyg
            dz_ref[rows, 0:D] = (dym * ya_ref[rows, :].astype(F32) * (ga * (1.0 - ga))).astype(BF)
            dz_ref[rows, D:2 * D] = (dym * yg_ref[rows, :].astype(F32) * (gg * (1.0 - gg))).astype(BF)
            do_ref[rows, :] = _dot_nt(dya, wa_ref[...]).astype(BF)
            dgm_ref[rows, :] = _dot_nt(dyg, wg_ref[...]).astype(BF)
        for m0 in range(0, D, CHUNK):
            acc_o[m0:m0 + CHUNK, :] += _dot_tn(ym_ref[:, m0:m0 + CHUNK], dy_scr[...])
        for m0 in range(0, Q_W, CHUNK):
            acc_a[m0:m0 + CHUNK, :] += _dot_tn(att_ref[:, m0:m0 + CHUNK], dya_scr[...])
            acc_g[m0:m0 + CHUNK, :] += _dot_tn(gm_ref[:, m0:m0 + CHUNK], dyg_scr[...])

        @pl.when(i == nb - 1)
        def _():
            for m0 in range(0, D, CHUNK):
                gwo_ref[m0:m0 + CHUNK, :] = acc_o[m0:m0 + CHUNK, :].astype(BF)
            for m0 in range(0, Q_W, CHUNK):
                gwa_ref[m0:m0 + CHUNK, :] = acc_a[m0:m0 + CHUNK, :].astype(BF)
                gwg_ref[m0:m0 + CHUNK, :] = acc_g[m0:m0 + CHUNK, :].astype(BF)

    vec = _const((1, D))
    rows = lambda w_: pl.BlockSpec((R, w_), lambda i: (i, 0))
    body, after_specs = _behind(body, 13, after)
    return pl.pallas_call(
        body, name="mix_out_bwd", grid=(nb,),
        out_shape=[_sds((S, 2 * D), BF), _sds((S, Q_W), BF), _sds((S, G_W), BF), _sds((1, D), F32),
                   _sds((1, D), F32), _sds((D, D), BF), _sds((Q_W, D), BF), _sds((G_W, D), BF)],
        in_specs=[rows(D), rows(D), rows(D), rows(D), rows(2 * D), rows(Q_W), rows(G_W), rows(D),
                  _resident((Q_W, D)), _resident((G_W, D)), _resident((D, D)), vec, vec] + after_specs,
        out_specs=[rows(2 * D), rows(Q_W), rows(G_W), vec, vec, _const((D, D)), _const((Q_W, D)),
                   _const((G_W, D))],
        scratch_shapes=[pltpu.VMEM((D, D), F32), pltpu.VMEM((Q_W, D), F32), pltpu.VMEM((G_W, D), F32)]
        + [pltpu.VMEM((R, D), BF)] * 3,
        compiler_params=_cp(1, 60),
    )(*_in_hbm(dh, y, ya, yg, gates, att, gm, ymix, wa, wg, wo), gate, gp, *after)


def _mix_dn(dq, dkv, dzg, dzgate, w, wq, h, dh, sc, gp, after=()):
    S = h.shape[0]
    R = min(512, S)

    def body(dq_ref, dkv_ref, dzg_ref, dzt_ref, w_ref, wq_ref, h_ref, dh_ref, sc_ref, gp_ref,
             out_ref, dsh_ref, dsc_ref, dgp_ref):
        @pl.when(pl.program_id(0) == 0)
        def _():
            dsh_ref[...] = jnp.zeros_like(dsh_ref)
            dsc_ref[...] = jnp.zeros_like(dsc_ref)
            dgp_ref[...] = jnp.zeros_like(dgp_ref)
        for r0 in range(0, R, CHUNK):
            rows = slice(r0, r0 + CHUNK)
            dn = _dot(dq_ref[rows, :], wq_ref[...])
            dn = dn + _dot(dkv_ref[rows, :], w_ref[Q_W:QKV_W, :])
            dn = dn + _dot(dzg_ref[rows, :], w_ref[ZG_OFF:GATE_OFF, :])
            dn = dn + _dot(dzt_ref[rows, :], w_ref[GATE_OFF:IN_W, :])
            dx, dsh, dsc, dgp = _prenorm_bwd(dn, h_ref[rows, :], gp_ref[...], sc_ref[...])
            out_ref[rows, :] = dh_ref[rows, :] + dx
            dsh_ref[...] += dsh
            dsc_ref[...] += dsc
            dgp_ref[...] += dgp

    vec = _const((1, D))
    rows = lambda w_: pl.BlockSpec((R, w_), lambda i: (i, 0))
    body, after_specs = _behind(body, 10, after)
    return pl.pallas_call(
        body, name="mix_dn", grid=(S // R,),
        out_shape=[_sds((S, D), F32)] + [_sds((1, D), F32)] * 3,
        in_specs=[rows(Q_W), rows(2 * KV_W), rows(2 * G_W), rows(2 * D), _resident((IN_W, D)),
                  _resident((Q_W, D)), rows(D), rows(D), vec, vec] + after_specs,
        out_specs=[rows(D), vec, vec, vec],
        compiler_params=_cp(1, 48),
    )(*_in_hbm(dq, dkv, dzg, dzgate, w, wq, h, dh), sc, gp, *after)


def _adamw_math(w, g, m, v):
    m2 = ADAM_B1 * m + (1.0 - ADAM_B1) * g
    v2 = ADAM_B2 * v + (1.0 - ADAM_B2) * (g * g)
    m_hat = m2 / (1.0 - ADAM_B1 ** ADAM_STEP)
    v_hat = v2 / (1.0 - ADAM_B2 ** ADAM_STEP)
    delta = -ADAM_LR * (m_hat / (jnp.sqrt(v_hat) + ADAM_EPS) + ADAM_WD * w)
    return delta, m2, v2


def _row_tile(rows, cols):
    best = None
    for t in range(16, rows + 1, 16):
        if rows % t == 0 and t * cols <= 256 * 1024:
            best = t
    return best if best is not None else rows


def _adamw_sharded(landing, w, m, v, name):
    r, c = w.shape
    tr = _row_tile(r, c)
    n = r // tr
    depth = min(3, n)

    def body(l_hbm, w_ref, m_ref, v_ref, g_ref, d_ref, m2_ref, v2_ref, buf, sem):
        i = pl.program_id(0)

        def fetch(t, slot):
            return pltpu.make_async_copy(l_hbm.at[:, pl.ds(pl.multiple_of(t * tr, tr), tr), :], buf.at[slot],
                                         sem.at[slot])

        @pl.when(i == 0)
        def _():
            for t in range(depth - 1):
                fetch(t, t).start()

        ahead = i + (depth - 1)

        @pl.when(ahead < n)
        def _():
            fetch(ahead, ahead % depth).start()

        slot = i % depth
        fetch(i, slot).wait()
        g = buf[slot, 0].astype(F32)
        for j in range(1, N_DEV):
            g = g + buf[slot, j].astype(F32)
        delta, m2, v2 = _adamw_math(w_ref[...], g, m_ref[...], v_ref[...])
        g_ref[...] = g
        d_ref[...] = delta
        m2_ref[...] = m2
        v2_ref[...] = v2

    row = pl.BlockSpec((tr, c), lambda i: (i, 0))
    return pl.pallas_call(
        body, name=name, grid=(n,),
        out_shape=[_sds((r, c), F32)] * 4,
        in_specs=[HBM_SPEC, row, row, row],
        out_specs=[row] * 4,
        scratch_shapes=[pltpu.VMEM((depth, N_DEV, tr, c), BF), pltpu.SemaphoreType.DMA((depth,))],
        compiler_params=_cp(1, 48),
    )(*_in_hbm(landing, w, m, v))


def _adamw_small(items):
    n = len(items)

    def body(*refs):
        for k in range(n):
            w_ref, g_ref, m_ref, v_ref = refs[4 * k:4 * k + 4]
            outs = refs[4 * n + 3 * k:4 * n + 3 * k + 3]
            for o_ref, val in zip(outs, _adamw_math(w_ref[...], g_ref[...], m_ref[...], v_ref[...])):
                o_ref[...] = val

    vm = pl.BlockSpec(memory_space=pltpu.VMEM)
    flat = pl.pallas_call(
        body, name="adamw_small",
        out_shape=[_sds(it[0].shape, F32) for it in items for _ in range(3)],
        in_specs=[vm] * (4 * n), out_specs=[vm] * (3 * n),
    )(*[a for it in items for a in it])
    return [tuple(flat[3 * k:3 * k + 3]) for k in range(n)]


def _w_ada_update(c8, d_ada, w, m, v):
    tr = 256

    def body(c_ref, d_ref, w_ref, m_ref, v_ref, g_ref, dl_ref, m2_ref, v2_ref):
        cs = c_ref[...]
        cs = cs * jax.nn.sigmoid(cs)
        g = lax.dot_general(cs, d_ref[...], (((0,), (0,)), ((), ())), preferred_element_type=F32, precision=HIGH)
        delta, m2, v2 = _adamw_math(w_ref[...], g, m_ref[...], v_ref[...])
        g_ref[...] = g
        dl_ref[...] = delta
        m2_ref[...] = m2
        v2_ref[...] = v2

    row = pl.BlockSpec((tr, ADA_W), lambda i: (i, 0))
    return pl.pallas_call(
        body, name="w_ada_update", grid=(D // tr,),
        out_shape=[_sds((D, ADA_W), F32)] * 4,
        in_specs=[pl.BlockSpec((N_DEV, tr), lambda i: (0, i)), _const((N_DEV, ADA_W)), row, row, row],
        out_specs=[row] * 4,
        compiler_params=_cp(1, 40),
    )(c8, d_ada, *_in_hbm(w, m, v))


def _t5_bucket():
    qi = np.arange(BLK, dtype=np.int32)[:, None]
    kj = np.arange(2 * BLK, dtype=np.int32)[None, :]
    dist = np.maximum(qi + BLK - kj, 0)
    max_exact = N_BUCKETS // 2
    d_f = np.maximum(dist, max_exact).astype(np.float32)
    large = max_exact + (np.log(d_f / np.float32(max_exact)) / np.float32(math.log(MAX_DISTANCE / max_exact))
                         * np.float32(N_BUCKETS - max_exact)).astype(np.int32)
    large = np.minimum(large, N_BUCKETS - 1)
    return jnp.asarray(np.where(dist < max_exact, dist, large).astype(np.int32))


def _slabs_of_columns(w):
    r, c8 = w.shape
    return jnp.transpose(w.reshape(r, N_DEV, c8 // N_DEV), (1, 0, 2))


def _columns_of_slabs(w8):
    _, r, c = w8.shape
    return jnp.transpose(w8, (1, 0, 2)).reshape(r, N_DEV * c)


def kernel(x, c, rel_bias, w_ada, b_ada, pre_norm_g, post_norm_g, w_ffn1_in, w_ffn1_out, w_in, sinks, gmlp_ln_g, gmlp_ln_b, gmlp_w_s, gmlp_b_s, w_br_attn, w_br_gmlp, w_out, w_ffn2_in, w_ffn2_out, loss_target, m_rel_bias, m_w_ada, m_b_ada, m_pre_norm_g, m_post_norm_g, m_w_ffn1_in, m_w_ffn1_out, m_w_in, m_sinks, m_gmlp_ln_g, m_gmlp_ln_b, m_gmlp_w_s, m_gmlp_b_s, m_w_br_attn, m_w_br_gmlp, m_w_out, m_w_ffn2_in, m_w_ffn2_out, v_rel_bias, v_w_ada, v_b_ada, v_pre_norm_g, v_post_norm_g, v_w_ffn1_in, v_w_ffn1_out, v_w_in, v_sinks, v_gmlp_ln_g, v_gmlp_ln_b, v_gmlp_w_s, v_gmlp_b_s, v_w_br_attn, v_w_br_gmlp, v_w_out, v_w_ffn2_in, v_w_ffn2_out):
    me = 4 * lax.axis_index("x") + 2 * lax.axis_index("y") + lax.axis_index("c")
    x0 = x[0]
    target = loss_target[0]

    transposed = ("w_ffn1_in", "w_in", "w_ffn2_in")
    shards = [w_ffn1_in[0].T, w_ffn1_out[0], w_in[0].T, w_br_attn[0], w_br_gmlp[0], w_out[0],
              w_ffn2_in[0].T, w_ffn2_out[0]]
    shards_bf = [s.astype(BF) for s in shards]
    groups = [shards_bf[0:1], shards_bf[1:6], shards_bf[6:8]]

    def gather_start(i, after):
        return _slabs_start("gather", groups[i], after, "gather_start_%d" % i)

    def forward_start(st, i, after):
        lands = _slabs_wait("gather", len(groups[i]), st, after, "gather_wait_%d" % i)
        return _slabs_start("forward", lands, c, "forward_start_%d" % i)

    def gathered(st, i, after):
        return _slabs_wait("forward", len(groups[i]), st, after, "forward_wait_%d" % i)

    gs0 = gather_start(0, c)

    mine = jnp.concatenate([c[0], pre_norm_g[0].reshape(-1), post_norm_g[0].reshape(-1)])
    small8 = jnp.broadcast_to(mine[None, :], (8, mine.shape[0]))
    b_ada64 = jnp.repeat(b_ada.reshape(N_DEV, ADA_W), 8, axis=0)
    gath, ada64 = _ada_forward(small8, w_ada[0], b_ada64)
    gath8 = gath[::8]
    ada = ada64[::8].reshape(9, D)
    sh1, sc1, g1, sh2, sc2, g2, sh3, sc3, g3 = [ada[k:k + 1] for k in range(9)]
    gains = gath8[:, D:].reshape(N_DEV, 2, 3, 128)
    pre_g = jnp.transpose(gains[:, 0], (1, 0, 2)).reshape(3, D)
    post_g = jnp.transpose(gains[:, 1], (1, 0, 2)).reshape(3, D)
    pre = [pre_g[k:k + 1] for k in range(3)]
    post = [post_g[k:k + 1] for k in range(3)]

    bucket = _t5_bucket()
    bias = _bias_table(rel_bias, bucket).reshape(HEAD_ROWS, 2 * BLK)
    sinks8 = sinks[0]
    lg, lb = gmlp_ln_g, gmlp_ln_b
    ws = gmlp_w_s[0]
    bst = jnp.transpose(gmlp_b_s[0])

    fs0 = forward_start(gs0, 0, sh1)
    gs1 = gather_start(1, fs0[-1])
    wf1_in = gathered(fs0, 0, gs1[-1])[0].reshape(2 * D_FF, D)
    n1, fg1, fu1, fa1 = _ffn_in(x0, sh1, sc1, pre[0], wf1_in, "ffn1_in")
    fs1 = forward_start(gs1, 1, n1)
    gs2 = gather_start(2, fs1[-1])
    mix_w = gathered(fs1, 1, gs2[-1])
    wf1_out = mix_w[0].reshape(D_FF, D)
    w_in_full = mix_w[1].reshape(IN_W, D)
    w_q = _pair_heads(w_in_full[0:Q_W])
    w_bra = _pair_heads(_columns_of_slabs(mix_w[2]))
    w_brg = _columns_of_slabs(mix_w[3])
    w_out_full = mix_w[4].reshape(D, D)
    h1, y1 = _ffn_out(fa1, wf1_out, x0, g1, post[0], "ffn1_out")
    n2, qkv, zg, gates = _mix_in(h1, sh2, sc2, pre[1], w_in_full, w_q)
    att = _attn_fwd(qkv, bias, sinks8)
    gm = _gmlp_fwd(zg, lg, lb, ws, bst)
    fs2 = forward_start(gs2, 2, gm)
    ya, yg, ymix, y2, h2 = _mix_out(att, gm, gates, h1, w_bra, w_brg, w_out_full, g2, post[1], after=(fs2[-1],))
    wf2_in, wf2_out = gathered(fs2, 2, h2)
    wf2_in = wf2_in.reshape(2 * D_FF, D)
    wf2_out = wf2_out.reshape(D_FF, D)
    n3, fg3, fu3, fa3 = _ffn_in(h2, sh3, sc3, pre[2], wf2_in, "ffn2_in")
    dh3, y3, sq = _ffn_out(fa3, wf2_out, h2, g3, post[2], "ffn2_out", target=target)

    def exchange_start(i, arrays):
        return _slabs_start("exchange", arrays, sq, "exchange_start_%d" % i)

    dy3, dgu3, dh2, d_g3, d_post2, d_sh3, d_sc3, d_pre2 = _ffn_bwd(
        dh3, y3, fg3, fu3, wf2_out, wf2_in, h2, g3, post[2], sc3, pre[2], "ffn2_bwd")
    gw_f2_out = _tn_matmul(fa3, dy3, "ffn2_out_wgrad", tm=D_FF // 2).reshape(N_DEV, D_FF // N_DEV, D)
    gw_f2_in = _tn_matmul(dgu3, n3, "ffn2_in_wgrad", tm=D_FF // 2).reshape(N_DEV, FS, D)
    ex1 = exchange_start(1, [gw_f2_out, gw_f2_in])

    dzgate, d_att, d_gm, d_g2, d_post1, gw_out, gw_bra, gw_brg = _mix_out_bwd(
        dh2, y2, ya, yg, gates, att, gm, ymix, w_bra, w_brg, w_out_full, g2, post[1], after=(ex1[-1],))
    ex2 = exchange_start(2, [_slabs_of_columns(_unpair_heads(gw_bra)), _slabs_of_columns(gw_brg),
                             gw_out.reshape(N_DEV, D // N_DEV, D)])
    dq, dkv, dbias, dsink = _attn_bwd(qkv, bias, sinks8, d_att)
    dzg, d_ws, d_bs, d_lg, d_lb = _gmlp_bwd(zg, d_gm, lg, lb, ws, bst)
    d_rel = _rel_bias_grad(dbias.reshape(N_KV, GROUP * BLK, 2 * BLK), bucket)
    early = jnp.concatenate([
        jnp.concatenate([d_lg.reshape(4, 128), d_lb.reshape(4, 128)], axis=0),
        d_bs, d_rel, dsink, d_ws.reshape(N_HEADS * BLK, BLK)], axis=0)
    sm0 = _slabs_start("gather_all", [early], sq, "small_gather_start")
    dh1, d_sh2, d_sc2, d_pre1 = _mix_dn(dq, dkv, dzg, dzgate, w_in_full, w_q, h1, dh2, sc2, pre[1],
                                        after=(ex2[-1], sm0[-1]))
    gw_q, gw_kv, gw_zg, gw_zgate = _tn_matmul([dq, dkv, dzg, dzgate], n2, "w_in_wgrad")
    gw_in = jnp.concatenate([_unpair_heads(gw_q), gw_kv, gw_zg, gw_zgate],
                            axis=0).reshape(N_DEV, IN_W // N_DEV, D)
    ex3 = exchange_start(3, [gw_in])

    dy1, dgu1, d_g1, d_post0 = _ffn_out_bwd(dh1, y1, fg1, fu1, wf1_out, g1, post[0], "ffn1_out_bwd",
                                            after=(ex3[-1],))
    gw_f1_out = _tn_matmul(fa1, dy1, "ffn1_out_wgrad", tm=D_FF // 2).reshape(N_DEV, D_FF // N_DEV, D)
    ex4 = exchange_start(4, [gw_f1_out])
    gw_f1_in = _tn_matmul(dgu1, n1, "ffn1_in_wgrad", tm=D_FF // 2).reshape(N_DEV, FS, D)
    ex5 = exchange_start(5, [gw_f1_in])
    grad_x, d_sh1, d_sc1, d_pre0 = _ffn_dn(dgu1, wf1_in, x0, dh1, sc1, pre[0], "ffn1_dn", after=(ex4[-1], ex5[-1]))

    landed = {}
    for i, (ex, nms) in enumerate([(ex1, ["w_ffn2_out", "w_ffn2_in"]),
                                   (ex2, ["w_br_attn", "w_br_gmlp", "w_out"]), (ex3, ["w_in"]),
                                   (ex4, ["w_ffn1_out"]), (ex5, ["w_ffn1_in"])]):
        for nm, land in zip(nms, _slabs_wait("exchange", len(nms), ex, grad_x, "exchange_wait_%d" % i)):
            landed[nm] = land
    moments = [(m_w_ffn1_in, v_w_ffn1_in), (m_w_ffn1_out, v_w_ffn1_out), (m_w_in, v_w_in),
               (m_w_br_attn, v_w_br_attn), (m_w_br_gmlp, v_w_br_gmlp), (m_w_out, v_w_out),
               (m_w_ffn2_in, v_w_ffn2_in), (m_w_ffn2_out, v_w_ffn2_out)]
    names = ["w_ffn1_in", "w_ffn1_out", "w_in", "w_br_attn", "w_br_gmlp", "w_out", "w_ffn2_in", "w_ffn2_out"]
    big = {}
    for nm, w_, (m_, v_) in zip(names, shards, moments):
        if nm in transposed:
            res4 = _adamw_sharded(landed[nm], w_, m_[0].T, v_[0].T, "adamw_" + nm)
            big[nm] = [a.T[None] for a in res4]
        else:
            big[nm] = [a[None] for a in _adamw_sharded(landed[nm], w_, m_[0], v_[0], "adamw_" + nm)]

    my_loss = jnp.broadcast_to(sq * (0.5 / D), (1, D))
    my_loss, _ = lax.optimization_barrier((my_loss, landed["w_ffn1_in"]))
    tot, every = _small_allreduce([d_sh1, d_sc1, d_g1, d_sh2, d_sc2, d_g2, d_sh3, d_sc3, d_g3,
                                   d_pre0, d_pre1, d_pre2, d_post0, d_post1, d_post2, my_loss])
    (early_land,) = _slabs_wait("gather_all", 1, sm0, grad_x, "small_gather_wait")
    tot_early = _sum_slabs(early_land)

    loss = tot[15, 0]
    g_b_ada = tot[0:9].reshape(1, 9 * D)
    g_pre = lax.dynamic_slice_in_dim(tot[9:12], 128 * me, 128, axis=1)[None]
    g_post = lax.dynamic_slice_in_dim(tot[12:15], 128 * me, 128, axis=1)[None]
    g_lg = tot_early[0:4].reshape(1, G_W)
    g_lb = tot_early[4:8].reshape(1, G_W)
    g_bs = tot_early[8:16][None]
    g_rel = jnp.transpose(tot_early[16:24, 0:N_BUCKETS])
    g_sinks = tot_early[24:32, 0][None]
    g_ws = tot_early[32:1056].reshape(1, N_HEADS, BLK, BLK)

    d_ada_mine = lax.dynamic_slice_in_dim(every[:, 0:9].reshape(N_DEV, 9 * D), ADA_W * me, ADA_W, axis=1)
    ada_out = [a[None] for a in _w_ada_update(gath8[:, 0:D], d_ada_mine, w_ada[0], m_w_ada[0], v_w_ada[0])]

    small = [("rel_bias", rel_bias, g_rel, m_rel_bias, v_rel_bias), ("b_ada", b_ada, g_b_ada, m_b_ada, v_b_ada),
             ("pre_norm_g", pre_norm_g, g_pre, m_pre_norm_g, v_pre_norm_g),
             ("post_norm_g", post_norm_g, g_post, m_post_norm_g, v_post_norm_g),
             ("sinks", sinks, g_sinks, m_sinks, v_sinks), ("gmlp_ln_g", gmlp_ln_g, g_lg, m_gmlp_ln_g, v_gmlp_ln_g),
             ("gmlp_ln_b", gmlp_ln_b, g_lb, m_gmlp_ln_b, v_gmlp_ln_b),
             ("gmlp_w_s", gmlp_w_s, g_ws, m_gmlp_w_s, v_gmlp_w_s), ("gmlp_b_s", gmlp_b_s, g_bs, m_gmlp_b_s, v_gmlp_b_s)]
    two_d = lambda a: a.reshape(int(math.prod(a.shape[:-1])), a.shape[-1])
    stepped = _adamw_small([tuple(two_d(a) for a in item[1:]) for item in small])
    res = {"w_ada": ada_out}
    for (nm, w_, g_, _, _), new in zip(small, stepped):
        res[nm] = [g_] + [a.reshape(w_.shape) for a in new]
    res.update(big)
    order = ["rel_bias", "w_ada", "b_ada", "pre_norm_g", "post_norm_g", "w_ffn1_in", "w_ffn1_out", "w_in", "sinks",
             "gmlp_ln_g", "gmlp_ln_b", "gmlp_w_s", "gmlp_b_s", "w_br_attn", "w_br_gmlp", "w_out", "w_ffn2_in",
             "w_ffn2_out"]
    outs = [loss, grad_x[None]]
    for k in range(4):
        outs += [res[nm][k] for nm in order]
    return tuple(outs)
```
